```python
import jax, jax.numpy as jnp
from jax import lax
import numpy as np

D_MODEL = 1024
BATCH = 8
SEQ = 8192
DEPTH = 1

GRID_W = 64
CTX_LEN = 256
CHUNK = 128
MIX_WIDTH = D_MODEL
A_WIDTH = MIX_WIDTH // 2
A_GROUPS = 4
A_GROUP_DIM = A_WIDTH // A_GROUPS
R_WIDTH = MIX_WIDTH - A_WIDTH
R_HEADS = 4
R_HEAD_DIM = R_WIDTH // R_HEADS
IN_COLS = 2 * A_WIDTH + 5 * R_WIDTH
D_FF = -(-8 * D_MODEL // (3 * 256)) * 256
RMS_EPS = 1e-6
ROPE_BASE = 10000.0

kernel_name = "hymba_gmlp_retention_prefix_dit_block"


def rmsnorm(x, g):
    x32 = x.astype(jnp.float32)
    y = x32 * lax.rsqrt(jnp.mean(x32 * x32, axis=-1, keepdims=True) + RMS_EPS)
    return (y * g.astype(jnp.float32)).astype(x.dtype)


def head_norm(o):
    return o * lax.rsqrt(jnp.mean(o * o, axis=-1, keepdims=True) + RMS_EPS)


def modulate(h, shift, scale):
    return h * (1.0 + scale[:, None, :]) + shift[:, None, :]


def rope_2d(t, rows, cols):
    half = t.shape[-1] // 2
    n_freq = half // 2
    inv = ROPE_BASE ** (-jnp.arange(n_freq, dtype=jnp.float32) / n_freq)

    def rot(u, pos):
        ang = pos[:, None] * inv[None, :]
        cos = jnp.cos(ang)[None, :, None, :]
        sin = jnp.sin(ang)[None, :, None, :]
        u1, u2 = u[..., :n_freq], u[..., n_freq:]
        return jnp.concatenate([u1 * cos - u2 * sin, u1 * sin + u2 * cos], axis=-1)

    return jnp.concatenate([rot(t[..., :half], rows), rot(t[..., half:], cols)], axis=-1)


def chunk_states(k, v, log_gamma, s0):
    B, L, H, dk = k.shape
    n = L // CHUNK
    kc = k.reshape(B, n, CHUNK, H, dk)
    vc = v.reshape(B, n, CHUNK, H, v.shape[-1])
    pos = jnp.arange(CHUNK, dtype=jnp.float32)
    zeta = jnp.exp(log_gamma[:, None] * (CHUNK - 1 - pos)[None, :])
    upd = jnp.einsum('bnjhd,hj,bnjhe->nbhde', kc, zeta, vc)
    decay = jnp.exp(log_gamma * CHUNK)[None, :, None, None]

    def step(s, u):
        return decay * s + u, s

    s_final, s_prev = lax.scan(step, s0, upd)
    return s_prev, s_final


def retention_out(q, k, v, log_gamma, s_prev):
    B, L, H, dk = q.shape
    n = L // CHUNK
    qc = q.reshape(B, n, CHUNK, H, dk)
    kc = k.reshape(B, n, CHUNK, H, dk)
    vc = v.reshape(B, n, CHUNK, H, v.shape[-1])
    pos = jnp.arange(CHUNK, dtype=jnp.float32)
    diff = pos[:, None] - pos[None, :]
    dmask = jnp.where(diff[None] >= 0,
                      jnp.exp(log_gamma[:, None, None] * jnp.maximum(diff, 0.0)[None]), 0.0)
    scores = jnp.einsum('bnihd,bnjhd->bnhij', qc, kc) * dmask[None, None]
    inner = jnp.einsum('bnhij,bnjhe->bnihe', scores, vc)
    xi = jnp.exp(log_gamma[:, None] * (pos + 1.0)[None, :])
    cross = jnp.einsum('bnihd,hi,nbhde->bnihe', qc, xi, s_prev)
    return (inner + cross).reshape(B, L, H, v.shape[-1])


def spatial_gating(zu, zv, sg_gain, sg_w, sg_b):
    B, L, _ = zu.shape
    n = L // CHUNK
    u = jax.nn.gelu(zu)
    v = jax.nn.gelu(zv).reshape(B, n, CHUNK, A_GROUPS, A_GROUP_DIM)
    v32 = v.astype(jnp.float32)
    v = (v32 * lax.rsqrt(jnp.mean(v32 * v32, axis=-1, keepdims=True) + RMS_EPS)
         * sg_gain.reshape(A_GROUPS, A_GROUP_DIM)).astype(zu.dtype)
    mixed = jnp.einsum('gpq,bnqgc->bnpgc', sg_w, v) + jnp.transpose(sg_b)[None, None, :, :, None]
    return u * mixed.reshape(B, L, A_WIDTH)


def split_proj(z):
    cuts = [A_WIDTH, 2 * A_WIDTH] + [2 * A_WIDTH + i * R_WIDTH for i in range(1, 5)]
    return jnp.split(z, cuts, axis=-1)


def heads(t):
    B, L, _ = t.shape
    return t.astype(jnp.float32).reshape(B, L, R_HEADS, R_HEAD_DIM)


def ffn(h, w_gate, w_up, w_down):
    return (jax.nn.silu(h @ w_gate) * (h @ w_up)) @ w_down


def _fwd_setup_inputs(seed: int = 0) -> dict:
    key = jax.random.key(seed)
    ks = jax.random.split(key, 24)
    f32 = jnp.float32
    nrm = lambda k, s: jax.random.normal(k, s, dtype=f32)
    gamma0 = 1.0 - 2.0 ** (-5.0 - np.arange(R_HEADS, dtype=np.float32))
    logit0 = jnp.asarray(np.log(gamma0) - np.log1p(-gamma0), dtype=f32)
    return {
        "x": nrm(ks[0], (BATCH, SEQ, D_MODEL)),
        "c": nrm(ks[1], (BATCH, D_MODEL)),
        "ctx": nrm(ks[2], (BATCH, CTX_LEN, D_MODEL)),
        "c_ctx": nrm(ks[3], (D_MODEL,)),
        "w_mod": nrm(ks[4], (DEPTH, D_MODEL, 6 * D_MODEL)) * (0.5 * D_MODEL ** -0.5),
        "b_mod": nrm(ks[5], (DEPTH, 6 * D_MODEL)) * 0.02,
        "norm1": 1.0 + 0.02 * nrm(ks[6], (DEPTH, D_MODEL)),
        "w_in": nrm(ks[7], (DEPTH, D_MODEL, IN_COLS)) * D_MODEL ** -0.5,
        "sg_gain": 1.0 + 0.02 * nrm(ks[8], (DEPTH, A_WIDTH)),
        "sg_w": nrm(ks[9], (DEPTH, A_GROUPS, CHUNK, CHUNK)) * CHUNK ** -0.5,
        "sg_b": 1.0 + 0.02 * nrm(ks[10], (DEPTH, A_GROUPS, CHUNK)),
        "ret_logit_f": logit0[None, :] + 0.05 * nrm(ks[11], (DEPTH, R_HEADS)),
        "ret_logit_b": logit0[None, :] + 0.05 * nrm(ks[12], (DEPTH, R_HEADS)),
        "w_out": nrm(ks[13], (DEPTH, MIX_WIDTH, D_MODEL)) * MIX_WIDTH ** -0.5,
        "norm2": 1.0 + 0.02 * nrm(ks[14], (DEPTH, D_MODEL)),
        "w_gate": nrm(ks[15], (DEPTH, D_MODEL, D_FF)) * D_MODEL ** -0.5,
        "w_up": nrm(ks[16], (DEPTH, D_MODEL, D_FF)) * D_MODEL ** -0.5,
        "w_down": nrm(ks[17], (DEPTH, D_FF, D_MODEL)) * D_FF ** -0.5,
        "norm_f": 1.0 + 0.02 * nrm(ks[18], (D_MODEL,)),
    }


def _fwd_reference(x, c, ctx, c_ctx, w_mod, b_mod, norm1, w_in, sg_gain, sg_w, sg_b,
              ret_logit_f, ret_logit_b, w_out, norm2, w_gate, w_up, w_down, norm_f):
    B, L, _ = x.shape
    rows_n = L // GRID_W
    rows = jnp.repeat(jnp.arange(rows_n, dtype=jnp.float32), GRID_W)
    cols = jnp.tile(jnp.arange(GRID_W, dtype=jnp.float32), rows_n)
    s_zero = jnp.zeros((B, R_HEADS, R_HEAD_DIM, R_HEAD_DIM), jnp.float32)
    k_scale = R_HEAD_DIM ** -0.5
    kv_lo = 2 * A_WIDTH + R_WIDTH
    kv_hi = 2 * A_WIDTH + 3 * R_WIDTH

    for l in range(DEPTH):
        last = l == DEPTH - 1
        sh1, sc1, g1, sh2, sc2, g2 = jnp.split(jax.nn.silu(c) @ w_mod[l] + b_mod[l], 6, axis=-1)
        csh1, csc1, cg1, csh2, csc2, cg2 = jnp.split(
            (jax.nn.silu(c_ctx) @ w_mod[l] + b_mod[l])[None, :], 6, axis=-1)
        lg_f = jax.nn.log_sigmoid(ret_logit_f[l].astype(jnp.float32))
        lg_b = jax.nn.log_sigmoid(ret_logit_b[l].astype(jnp.float32))

        hc = modulate(rmsnorm(ctx, norm1[l]), csh1, csc1)
        if last:
            kc_r, vc_r = jnp.split(hc @ w_in[l][:, kv_lo:kv_hi], 2, axis=-1)
        else:
            cu, cv, cq_r, kc_r, vc_r, cgf, cgb = split_proj(hc @ w_in[l])
        kc = heads(kc_r) * k_scale
        vc = heads(vc_r)
        kc_rev, vc_rev = jnp.flip(kc, 1), jnp.flip(vc, 1)
        sp_cf, s_cf = chunk_states(kc, vc, lg_f, s_zero)
        sp_cb, s_cb = chunk_states(kc_rev, vc_rev, lg_b, s_zero)
        if not last:
            qc = heads(cq_r)
            o_cf = retention_out(qc, kc, vc, lg_f, sp_cf)
            o_cb = jnp.flip(retention_out(jnp.flip(qc, 1), kc_rev, vc_rev, lg_b, sp_cb), 1)
            Lc = ctx.shape[1]
            yc_r = (jax.nn.silu(cgf.astype(jnp.float32)) * head_norm(o_cf).reshape(B, Lc, R_WIDTH)
                    + jax.nn.silu(cgb.astype(jnp.float32)) * head_norm(o_cb).reshape(B, Lc, R_WIDTH))
            yc_a = spatial_gating(cu, cv, sg_gain[l], sg_w[l], sg_b[l])
            yc = jnp.concatenate([yc_a, yc_r.astype(ctx.dtype)], axis=-1) @ w_out[l]
            ctx_new = ctx + cg1[:, None, :] * yc
            hc2 = modulate(rmsnorm(ctx_new, norm2[l]), csh2, csc2)
            ctx_next = ctx_new + cg2[:, None, :] * ffn(hc2, w_gate[l], w_up[l], w_down[l])

        hx = modulate(rmsnorm(x, norm1[l]), sh1, sc1)
        u, v, q_r, k_r, v_r, gf, gb = split_proj(hx @ w_in[l])
        y_a = spatial_gating(u, v, sg_gain[l], sg_w[l], sg_b[l])
        q = rope_2d(heads(q_r), rows, cols)
        k = rope_2d(heads(k_r), rows, cols) * k_scale
        vv = heads(v_r)
        sp_f, _ = chunk_states(k, vv, lg_f, s_cf)
        o_f = retention_out(q, k, vv, lg_f, sp_f)
        q_rev, k_rev, v_rev = jnp.flip(q, 1), jnp.flip(k, 1), jnp.flip(vv, 1)
        sp_b, _ = chunk_states(k_rev, v_rev, lg_b, s_cb)
        o_b = jnp.flip(retention_out(q_rev, k_rev, v_rev, lg_b, sp_b), 1)
        y_r = (jax.nn.silu(gf.astype(jnp.float32)) * head_norm(o_f).reshape(B, L, R_WIDTH)
               + jax.nn.silu(gb.astype(jnp.float32)) * head_norm(o_b).reshape(B, L, R_WIDTH))
        y = jnp.concatenate([y_a, y_r.astype(x.dtype)], axis=-1) @ w_out[l]
        x = x + g1[:, None, :] * y
        h2 = modulate(rmsnorm(x, norm2[l]), sh2, sc2)
        x = x + g2[:, None, :] * ffn(h2, w_gate[l], w_up[l], w_down[l])
        if not last:
            ctx = ctx_next

    return rmsnorm(x, norm_f)


import jax as _jax
import jax.numpy as _jnp

TWIN_FORMAT = 'train_step'
FWD_PARAMS = ['x', 'c', 'ctx', 'c_ctx', 'w_mod', 'b_mod', 'norm1', 'w_in', 'sg_gain', 'sg_w', 'sg_b', 'ret_logit_f', 'ret_logit_b', 'w_out', 'norm2', 'w_gate', 'w_up', 'w_down', 'norm_f']
TWIN_WEIGHTS = ['c_ctx', 'w_mod', 'b_mod', 'norm1', 'w_in', 'sg_gain', 'sg_w', 'sg_b', 'ret_logit_f', 'ret_logit_b', 'w_out', 'norm2', 'w_gate', 'w_up', 'w_down', 'norm_f']
TWIN_DIFF_INPUT = 'x'
TWIN_INPUTS = ['x', 'c', 'ctx', 'c_ctx', 'w_mod', 'b_mod', 'norm1', 'w_in', 'sg_gain', 'sg_w', 'sg_b', 'ret_logit_f', 'ret_logit_b', 'w_out', 'norm2', 'w_gate', 'w_up', 'w_down', 'norm_f', 'loss_target', 'm_c_ctx', 'm_w_mod', 'm_b_mod', 'm_norm1', 'm_w_in', 'm_sg_gain', 'm_sg_w', 'm_sg_b', 'm_ret_logit_f', 'm_ret_logit_b', 'm_w_out', 'm_norm2', 'm_w_gate', 'm_w_up', 'm_w_down', 'm_norm_f', 'v_c_ctx', 'v_w_mod', 'v_b_mod', 'v_norm1', 'v_w_in', 'v_sg_gain', 'v_sg_w', 'v_sg_b', 'v_ret_logit_f', 'v_ret_logit_b', 'v_w_out', 'v_norm2', 'v_w_gate', 'v_w_up', 'v_w_down', 'v_norm_f']
TWIN_OUTPUTS = ['loss', 'grad_x', 'grad_c_ctx', 'grad_w_mod', 'grad_b_mod', 'grad_norm1', 'grad_w_in', 'grad_sg_gain', 'grad_sg_w', 'grad_sg_b', 'grad_ret_logit_f', 'grad_ret_logit_b', 'grad_w_out', 'grad_norm2', 'grad_w_gate', 'grad_w_up', 'grad_w_down', 'grad_norm_f', 'delta_c_ctx', 'delta_w_mod', 'delta_b_mod', 'delta_norm1', 'delta_w_in', 'delta_sg_gain', 'delta_sg_w', 'delta_sg_b', 'delta_ret_logit_f', 'delta_ret_logit_b', 'delta_w_out', 'delta_norm2', 'delta_w_gate', 'delta_w_up', 'delta_w_down', 'delta_norm_f', 'new_m_c_ctx', 'new_m_w_mod', 'new_m_b_mod', 'new_m_norm1', 'new_m_w_in', 'new_m_sg_gain', 'new_m_sg_w', 'new_m_sg_b', 'new_m_ret_logit_f', 'new_m_ret_logit_b', 'new_m_w_out', 'new_m_norm2', 'new_m_w_gate', 'new_m_w_up', 'new_m_w_down', 'new_m_norm_f', 'new_v_c_ctx', 'new_v_w_mod', 'new_v_b_mod', 'new_v_norm1', 'new_v_w_in', 'new_v_sg_gain', 'new_v_sg_w', 'new_v_sg_b', 'new_v_ret_logit_f', 'new_v_ret_logit_b', 'new_v_w_out', 'new_v_norm2', 'new_v_w_gate', 'new_v_w_up', 'new_v_w_down', 'new_v_norm_f']
TWIN_LEAF_KINDS = {'loss': 'loss', 'grad_x': 'grad_x', 'grad_c_ctx': 'grad_w', 'grad_w_mod': 'grad_w', 'grad_b_mod': 'grad_w', 'grad_norm1': 'grad_w', 'grad_w_in': 'grad_w', 'grad_sg_gain': 'grad_w', 'grad_sg_w': 'grad_w', 'grad_sg_b': 'grad_w', 'grad_ret_logit_f': 'grad_w', 'grad_ret_logit_b': 'grad_w', 'grad_w_out': 'grad_w', 'grad_norm2': 'grad_w', 'grad_w_gate': 'grad_w', 'grad_w_up': 'grad_w', 'grad_w_down': 'grad_w', 'grad_norm_f': 'grad_w', 'delta_c_ctx': 'delta_w', 'delta_w_mod': 'delta_w', 'delta_b_mod': 'delta_w', 'delta_norm1': 'delta_w', 'delta_w_in': 'delta_w', 'delta_sg_gain': 'delta_w', 'delta_sg_w': 'delta_w', 'delta_sg_b': 'delta_w', 'delta_ret_logit_f': 'delta_w', 'delta_ret_logit_b': 'delta_w', 'delta_w_out': 'delta_w', 'delta_norm2': 'delta_w', 'delta_w_gate': 'delta_w', 'delta_w_up': 'delta_w', 'delta_w_down': 'delta_w', 'delta_norm_f': 'delta_w', 'new_m_c_ctx': 'new_m', 'new_m_w_mod': 'new_m', 'new_m_b_mod': 'new_m', 'new_m_norm1': 'new_m', 'new_m_w_in': 'new_m', 'new_m_sg_gain': 'new_m', 'new_m_sg_w': 'new_m', 'new_m_sg_b': 'new_m', 'new_m_ret_logit_f': 'new_m', 'new_m_ret_logit_b': 'new_m', 'new_m_w_out': 'new_m', 'new_m_norm2': 'new_m', 'new_m_w_gate': 'new_m', 'new_m_w_up': 'new_m', 'new_m_w_down': 'new_m', 'new_m_norm_f': 'new_m', 'new_v_c_ctx': 'new_v', 'new_v_w_mod': 'new_v', 'new_v_b_mod': 'new_v', 'new_v_norm1': 'new_v', 'new_v_w_in': 'new_v', 'new_v_sg_gain': 'new_v', 'new_v_sg_w': 'new_v', 'new_v_sg_b': 'new_v', 'new_v_ret_logit_f': 'new_v', 'new_v_ret_logit_b': 'new_v', 'new_v_w_out': 'new_v', 'new_v_norm2': 'new_v', 'new_v_w_gate': 'new_v', 'new_v_w_up': 'new_v', 'new_v_w_down': 'new_v', 'new_v_norm_f': 'new_v'}


def _forward(args):
    return _fwd_reference(*[args[k] for k in FWD_PARAMS])


def _output_shape():
    def fwd():
        inp = _fwd_setup_inputs(0)
        return _fwd_reference(*[inp[k] for k in FWD_PARAMS])
    out = _jax.eval_shape(fwd)
    return out.shape, out.dtype

N_MICROBATCH = 1
ADAM_LR = 0.001
ADAM_B1 = 0.9
ADAM_B2 = 0.999
ADAM_EPS = 1e-08
ADAM_WD = 0.01
ADAM_STEP = 10
PER_EXAMPLE_BATCH_AXIS = {'x': 0, 'c': 0, 'ctx': 0, 'loss_target': 0}
SHARED_INPUTS = []
_WEIGHT_DTYPES = {'c_ctx': _jnp.float32, 'w_mod': _jnp.float32, 'b_mod': _jnp.float32, 'norm1': _jnp.float32, 'w_in': _jnp.float32, 'sg_gain': _jnp.float32, 'sg_w': _jnp.float32, 'sg_b': _jnp.float32, 'ret_logit_f': _jnp.float32, 'ret_logit_b': _jnp.float32, 'w_out': _jnp.float32, 'norm2': _jnp.float32, 'w_gate': _jnp.float32, 'w_up': _jnp.float32, 'w_down': _jnp.float32, 'norm_f': _jnp.float32}
MOMENT_SCALE = {'c_ctx': 2.544605e-02, 'w_mod': 9.984257e-02, 'b_mod': 1.752043e-01, 'norm1': 1.186046e-01, 'w_in': 6.718931e-02, 'sg_gain': 5.287224e-02, 'sg_w': 5.450536e-02, 'sg_b': 6.092477e-02, 'ret_logit_f': 2.289982e-01, 'ret_logit_b': 2.394716e-01, 'w_out': 7.294074e-02, 'norm2': 7.325629e-02, 'w_gate': 3.191100e-02, 'w_up': 3.086945e-02, 'w_down': 5.114469e-02, 'norm_f': 6.408963e+01}


def _to_microbatches(a, axis):
    t = _jnp.moveaxis(a, axis, 0)
    t = t.reshape((N_MICROBATCH, t.shape[0] // N_MICROBATCH) + t.shape[1:])
    return _jnp.moveaxis(t, 1, axis + 1)


def setup_inputs(seed: int = 0) -> dict:
    inp = _fwd_setup_inputs(seed)
    key = _jax.random.fold_in(_jax.random.key(seed), 7919)
    shape, _ = _output_shape()
    out = dict(inp)
    out["loss_target"] = _jax.random.normal(_jax.random.fold_in(key, 0), shape, _jnp.float32)
    for i, name in enumerate(TWIN_WEIGHTS):
        w = inp[name].astype(_jnp.float32)
        if MOMENT_SCALE is None:
            s = _jnp.sqrt(_jnp.mean(_jnp.square(w)) + 1e-30)
        else:
            s = MOMENT_SCALE[name]
        km, kv = _jax.random.split(_jax.random.fold_in(key, i + 1))
        out[name] = w
        out["m_" + name] = s * _jax.random.normal(km, w.shape, _jnp.float32)
        out["v_" + name] = (s * s) * _jax.random.uniform(kv, w.shape, _jnp.float32, 0.5, 1.5)
    if N_MICROBATCH > 1:
        for name, axis in PER_EXAMPLE_BATCH_AXIS.items():
            out[name] = _to_microbatches(out[name], axis)
    return {'x': out['x'], 'c': out['c'], 'ctx': out['ctx'], 'c_ctx': out['c_ctx'], 'w_mod': out['w_mod'], 'b_mod': out['b_mod'], 'norm1': out['norm1'], 'w_in': out['w_in'], 'sg_gain': out['sg_gain'], 'sg_w': out['sg_w'], 'sg_b': out['sg_b'], 'ret_logit_f': out['ret_logit_f'], 'ret_logit_b': out['ret_logit_b'], 'w_out': out['w_out'], 'norm2': out['norm2'], 'w_gate': out['w_gate'], 'w_up': out['w_up'], 'w_down': out['w_down'], 'norm_f': out['norm_f'], 'loss_target': out['loss_target'], 'm_c_ctx': out['m_c_ctx'], 'm_w_mod': out['m_w_mod'], 'm_b_mod': out['m_b_mod'], 'm_norm1': out['m_norm1'], 'm_w_in': out['m_w_in'], 'm_sg_gain': out['m_sg_gain'], 'm_sg_w': out['m_sg_w'], 'm_sg_b': out['m_sg_b'], 'm_ret_logit_f': out['m_ret_logit_f'], 'm_ret_logit_b': out['m_ret_logit_b'], 'm_w_out': out['m_w_out'], 'm_norm2': out['m_norm2'], 'm_w_gate': out['m_w_gate'], 'm_w_up': out['m_w_up'], 'm_w_down': out['m_w_down'], 'm_norm_f': out['m_norm_f'], 'v_c_ctx': out['v_c_ctx'], 'v_w_mod': out['v_w_mod'], 'v_b_mod': out['v_b_mod'], 'v_norm1': out['v_norm1'], 'v_w_in': out['v_w_in'], 'v_sg_gain': out['v_sg_gain'], 'v_sg_w': out['v_sg_w'], 'v_sg_b': out['v_sg_b'], 'v_ret_logit_f': out['v_ret_logit_f'], 'v_ret_logit_b': out['v_ret_logit_b'], 'v_w_out': out['v_w_out'], 'v_norm2': out['v_norm2'], 'v_w_gate': out['v_w_gate'], 'v_w_up': out['v_w_up'], 'v_w_down': out['v_w_down'], 'v_norm_f': out['v_norm_f']}


def _loss(weights, diff, rest, loss_target):
    with _jax.named_scope("forward"):
        args = {**rest, TWIN_DIFF_INPUT: diff, **{k: w.astype(_WEIGHT_DTYPES[k]) for k, w in weights.items()}}
        y = _forward(args)
    with _jax.named_scope("loss_head"):
        err = _jnp.square(y.astype(_jnp.float32) - loss_target)
        return 0.5 * _jnp.sum(_jnp.mean(err, axis=-1)) if err.ndim else 0.5 * err


def _adamw(w, g, m, v):
    m = ADAM_B1 * m + (1.0 - ADAM_B1) * g
    v = ADAM_B2 * v + (1.0 - ADAM_B2) * _jnp.square(g)
    m_hat = m / (1.0 - ADAM_B1 ** ADAM_STEP)
    v_hat = v / (1.0 - ADAM_B2 ** ADAM_STEP)
    delta = -ADAM_LR * (m_hat / (_jnp.sqrt(v_hat) + ADAM_EPS) + ADAM_WD * w)
    return delta, m, v


def reference(x, c, ctx, c_ctx, w_mod, b_mod, norm1, w_in, sg_gain, sg_w, sg_b, ret_logit_f, ret_logit_b, w_out, norm2, w_gate, w_up, w_down, norm_f, loss_target, m_c_ctx, m_w_mod, m_b_mod, m_norm1, m_w_in, m_sg_gain, m_sg_w, m_sg_b, m_ret_logit_f, m_ret_logit_b, m_w_out, m_norm2, m_w_gate, m_w_up, m_w_down, m_norm_f, v_c_ctx, v_w_mod, v_b_mod, v_norm1, v_w_in, v_sg_gain, v_sg_w, v_sg_b, v_ret_logit_f, v_ret_logit_b, v_w_out, v_norm2, v_w_gate, v_w_up, v_w_down, v_norm_f):
    given = dict(x=x, c=c, ctx=ctx, c_ctx=c_ctx, w_mod=w_mod, b_mod=b_mod, norm1=norm1, w_in=w_in, sg_gain=sg_gain, sg_w=sg_w, sg_b=sg_b, ret_logit_f=ret_logit_f, ret_logit_b=ret_logit_b, w_out=w_out, norm2=norm2, w_gate=w_gate, w_up=w_up, w_down=w_down, norm_f=norm_f, loss_target=loss_target, m_c_ctx=m_c_ctx, m_w_mod=m_w_mod, m_b_mod=m_b_mod, m_norm1=m_norm1, m_w_in=m_w_in, m_sg_gain=m_sg_gain, m_sg_w=m_sg_w, m_sg_b=m_sg_b, m_ret_logit_f=m_ret_logit_f, m_ret_logit_b=m_ret_logit_b, m_w_out=m_w_out, m_norm2=m_norm2, m_w_gate=m_w_gate, m_w_up=m_w_up, m_w_down=m_w_down, m_norm_f=m_norm_f, v_c_ctx=v_c_ctx, v_w_mod=v_w_mod, v_b_mod=v_b_mod, v_norm1=v_norm1, v_w_in=v_w_in, v_sg_gain=v_sg_gain, v_sg_w=v_sg_w, v_sg_b=v_sg_b, v_ret_logit_f=v_ret_logit_f, v_ret_logit_b=v_ret_logit_b, v_w_out=v_w_out, v_norm2=v_norm2, v_w_gate=v_w_gate, v_w_up=v_w_up, v_w_down=v_w_down, v_norm_f=v_norm_f)
    weights = {n: given[n] for n in TWIN_WEIGHTS}
    shared = {n: given[n] for n in SHARED_INPUTS}
    per_example = {n: given[n] for n in ['x', 'c', 'ctx']}
    grad_fn = _jax.value_and_grad(_loss, argnums=(0, 1))

    def one_microbatch(ex, loss_target):
        ex = dict(ex)
        diff = ex.pop(TWIN_DIFF_INPUT)
        return grad_fn(weights, diff, {**shared, **ex}, loss_target)

    if N_MICROBATCH == 1:
        loss, (grad_w, grad_x) = one_microbatch(per_example, given["loss_target"])
    else:
        def body(carry, xs):
            loss_sum, grad_sum = carry
            l_k, (gw_k, gx_k) = one_microbatch(xs[0], xs[1])
            with _jax.named_scope("update"):
                return (loss_sum + l_k, _jax.tree.map(_jnp.add, grad_sum, gw_k)), gx_k

        init = (_jnp.zeros((), _jnp.float32), _jax.tree.map(_jnp.zeros_like, weights))
        (loss, grad_w), grad_x = _jax.lax.scan(body, init, (per_example, given["loss_target"]))
    with _jax.named_scope("update"):
        delta_w, new_m, new_v = {}, {}, {}
        for n in TWIN_WEIGHTS:
            delta_w[n], new_m[n], new_v[n] = _adamw(weights[n], grad_w[n], given["m_" + n], given["v_" + n])
    return (loss, grad_x, *[grad_w[n] for n in TWIN_WEIGHTS], *[delta_w[n] for n in TWIN_WEIGHTS],
            *[new_m[n] for n in TWIN_WEIGHTS], *[new_v[n] for n in TWIN_WEIGHTS])
```

```python
import functools
import math

import jax
import jax.numpy as jnp
from jax import lax
from jax.experimental import pallas as pl
from jax.experimental.pallas import tpu as pltpu

F32 = jnp.float32
BF16 = jnp.bfloat16

D = 1024
AW = 512
RW = 512
H = 4
DH = 128
G = 4
C = 128
DFF = 2816
INC = 3584
KV_LO, KV_HI = 1536, 2560
NDEV = 8
EPS = 1e-6
KSCALE = DH ** -0.5
ROPE_BASE = 10000.0
GRID_W = 64

ADAM_LR = 0.001
ADAM_B1 = 0.9
ADAM_B2 = 0.999
ADAM_EPS = 1e-08
ADAM_WD = 0.01
ADAM_STEP = 10

VMEM_LIMIT = 56 * 1024 * 1024
MESH = pl.DeviceIdType.MESH
PACK_ROWS = 88


def _cparams(n_grid=0, **kw):
    sem = ("arbitrary",) * n_grid if n_grid else None
    return pltpu.CompilerParams(dimension_semantics=sem, vmem_limit_bytes=VMEM_LIMIT, **kw)


def _dot(a, b):
    return jnp.dot(a, b, preferred_element_type=F32)


def _dot_nt(a, b):
    return lax.dot_general(a, b, (((1,), (1,)), ((), ())), preferred_element_type=F32)


def _dot_tn(a, b):
    return lax.dot_general(a, b, (((0,), (0,)), ((), ())), preferred_element_type=F32)


def _whole(shape):
    nd = len(shape)
    return pl.BlockSpec(shape, lambda *_: (0,) * nd)


def _rows(tm, n):
    return pl.BlockSpec((tm, n), lambda i: (i, 0))


def _rows_rev(tm, n, nt):
    return pl.BlockSpec((tm, n), lambda i: (nt - 1 - i, 0))


def _sigmoid(x):
    return 1.0 / (1.0 + jnp.exp(-x))


def _log_sigmoid(x):
    return jnp.minimum(x, 0.0) - jnp.log(1.0 + jnp.exp(-jnp.abs(x)))


_GK0 = math.sqrt(2.0 / math.pi)
_GK1 = 0.044715


def _gelu(x):
    x2 = x * x
    t = jnp.tanh(_GK0 * x * (1.0 + _GK1 * x2))
    g = 0.5 * x * (1.0 + t)
    dg = 0.5 * (1.0 + t) + 0.5 * x * (1.0 - t * t) * (_GK0 * (1.0 + 3.0 * _GK1 * x2))
    return g, dg


def _silu_parts(a):
    s = _sigmoid(a)
    return a * s, s * (1.0 + a * (1.0 - s))


def _rope(t, cos, sins):
    n = t.shape[1]
    lane = lax.broadcasted_iota(jnp.int32, t.shape, 1)
    first = (lane & 63) < 32
    partner = jnp.where(first, pltpu.roll(t, n - 32, 1), pltpu.roll(t, 32, 1))
    return t * cos + partner * sins


def _headnorm(o):
    outs, rs = [], []
    for h in range(H):
        oh = o[:, h * DH:(h + 1) * DH]
        r = lax.rsqrt(jnp.mean(oh * oh, axis=-1, keepdims=True) + EPS)
        outs.append(oh * r)
        rs.append(r)
    return jnp.concatenate(outs, axis=1), rs


def _decay_consts(lg, forward):
    ii = lax.broadcasted_iota(jnp.int32, (C, C), 0).astype(F32)
    jj = lax.broadcasted_iota(jnp.int32, (C, C), 1).astype(F32)
    ic = lax.broadcasted_iota(jnp.int32, (C, 1), 0).astype(F32)
    if forward:
        diff = ii - jj
        xi = jnp.exp(lg * (ic + 1.0))
        z = jnp.exp(lg * (C - 1.0 - ic))
    else:
        diff = jj - ii
        xi = jnp.exp(lg * (C - ic))
        z = jnp.exp(lg * ic)
    dm = jnp.where(diff >= 0.0, jnp.exp(lg * jnp.maximum(diff, 0.0)), 0.0)
    dec = jnp.exp(lg * float(C))
    return dm, xi, z, dec, ic


def _ctx_fwd(ctx, cmod6, norm1, win, rlf, rlb):
    lc = ctx.shape[0]

    def body(ctx_ref, cmod_ref, n1_ref, win_ref, rlf_ref, rlb_ref, hc_ref, kvc_ref, scf_ref, scb_ref):
        x = ctx_ref[...]
        r = lax.rsqrt(jnp.mean(x * x, axis=-1, keepdims=True) + EPS)
        hc = (x * r) * (n1_ref[...] * (1.0 + cmod_ref[1:2, :])) + cmod_ref[0:1, :]
        hcb = hc.astype(BF16)
        hc_ref[...] = hcb
        kv = _dot(hcb, win_ref[:, KV_LO:KV_HI])
        k = kv[:, :RW] * KSCALE
        v = kv[:, RW:]
        kvc_ref[:, :RW] = k
        kvc_ref[:, RW:] = v
        t = lax.broadcasted_iota(jnp.int32, (lc, 1), 0).astype(F32)
        lgf = _log_sigmoid(rlf_ref[...])
        lgb = _log_sigmoid(rlb_ref[...])
        for h in range(H):
            hs = slice(h * DH, (h + 1) * DH)
            wf = jnp.exp(lgf[h:h + 1, 0:1] * (lc - 1.0 - t))
            wb = jnp.exp(lgb[h:h + 1, 0:1] * t)
            vh = v[:, hs].astype(BF16)
            scf_ref[h] = _dot_tn((k[:, hs] * wf).astype(BF16), vh)
            scb_ref[h] = _dot_tn((k[:, hs] * wb).astype(BF16), vh)

    return pl.pallas_call(
        body, name="ctx_fwd",
        out_shape=(jax.ShapeDtypeStruct((lc, D), BF16), jax.ShapeDtypeStruct((lc, 2 * RW), F32),
                   jax.ShapeDtypeStruct((H, DH, DH), F32), jax.ShapeDtypeStruct((H, DH, DH), F32)),
        compiler_params=_cparams(),
    )(ctx, cmod6, norm1, win, rlf, rlb)


def _sg_forward(u, v, sgw_ref, sgbt_ref, sgg_ref, nc):
    ua, dgu = _gelu(u)
    va, dgv = _gelu(v)
    gain = sgg_ref[...]
    mixed, vn_b, vah_l, rv_l = [], [], [], []
    for g in range(G):
        gs = slice(g * DH, (g + 1) * DH)
        vag = va[:, gs]
        rv = lax.rsqrt(jnp.mean(vag * vag, axis=-1, keepdims=True) + EPS)
        vah = vag * rv
        vn = (vah * gain[:, gs]).astype(BF16)
        wg = sgw_ref[g].astype(BF16)
        bcol = sgbt_ref[g]
        rows = [_dot(wg, vn[c * C:(c + 1) * C, :]) + bcol for c in range(nc)]
        mixed.append(jnp.concatenate(rows, axis=0) if nc > 1 else rows[0])
        vn_b.append(vn)
        vah_l.append(vah)
        rv_l.append(rv)
    mixed = jnp.concatenate(mixed, axis=1)
    return ua * mixed, (ua, dgu, dgv, mixed, vn_b, vah_l, rv_l)


def _fwd_a(x, mod6, norm1, win, sgw, sgbt, sggain, cos, sin, rlf, scf, *, tm):
    t_len = x.shape[0]
    nt, nc = t_len // tm, tm // C

    def body(x_ref, mod_ref, n1_ref, win_ref, sgw_ref, sgbt_ref, sgg_ref, cos_ref, sin_ref, rlf_ref, scf_ref,
             hx_ref, zuv_ref, q_ref, k_ref, v_ref, gfb_ref, of_ref, ya_ref, sst_ref, s_scr):
        i = pl.program_id(0)

        @pl.when(i == 0)
        def _():
            s_scr[...] = scf_ref[...]

        x = x_ref[...]
        r = lax.rsqrt(jnp.mean(x * x, axis=-1, keepdims=True) + EPS)
        hx = (x * r) * (n1_ref[...] * (1.0 + mod_ref[1:2, :])) + mod_ref[0:1, :]
        hxb = hx.astype(BF16)
        hx_ref[...] = hxb
        z = _dot(hxb, win_ref[...])
        zuv_ref[...] = z[:, :2 * AW]
        gfb_ref[...] = z[:, 2560:3584]
        cos = jnp.tile(cos_ref[...], (1, H))
        sins = jnp.tile(sin_ref[...], (1, H))
        q_ref[...] = _rope(z[:, 1024:1536], cos, sins).astype(BF16)
        k_ref[...] = (_rope(z[:, 1536:2048], cos, sins) * KSCALE).astype(BF16)
        v_ref[...] = z[:, 2048:2560].astype(BF16)
        ya, _ = _sg_forward(z[:, :AW], z[:, AW:2 * AW], sgw_ref, sgbt_ref, sgg_ref, nc)
        ya_ref[...] = ya

        lg = _log_sigmoid(rlf_ref[...])
        for h in range(H):
            dm, xi, zc, dec, _ = _decay_consts(lg[h:h + 1, 0:1], True)
            hs = slice(h * DH, (h + 1) * DH)
            for c in range(nc):
                rs = slice(c * C, (c + 1) * C)
                qc, kc, vc = q_ref[rs, hs], k_ref[rs, hs], v_ref[rs, hs]
                s = s_scr[h]
                sst_ref[c, h] = s
                a = (_dot_nt(qc, kc) * dm).astype(BF16)
                of_ref[rs, hs] = _dot(a, vc) + xi * _dot(qc, s.astype(BF16))
                kz = (kc.astype(F32) * zc).astype(BF16)
                s_scr[h] = dec * s + _dot_tn(kz, vc)

    n_chunks = t_len // C
    return pl.pallas_call(
        body, name="fwd_a", grid=(nt,),
        in_specs=[_rows(tm, D), _whole((6, D)), _whole((1, D)), _whole((D, INC)), _whole((G, C, C)),
                  _whole((G, C, 128)), _whole((1, AW)), _rows(tm, DH), _rows(tm, DH), _whole((H, 128)),
                  _whole((H, DH, DH))],
        out_specs=[_rows(tm, D), _rows(tm, 2 * AW), _rows(tm, RW), _rows(tm, RW), _rows(tm, RW),
                   _rows(tm, 2 * RW), _rows(tm, RW), _rows(tm, AW),
                   pl.BlockSpec((nc, H, DH, DH), lambda i: (i, 0, 0, 0))],
        out_shape=(jax.ShapeDtypeStruct((t_len, D), BF16), jax.ShapeDtypeStruct((t_len, 2 * AW), F32),
                   jax.ShapeDtypeStruct((t_len, RW), BF16), jax.ShapeDtypeStruct((t_len, RW), BF16),
                   jax.ShapeDtypeStruct((t_len, RW), BF16), jax.ShapeDtypeStruct((t_len, 2 * RW), F32),
                   jax.ShapeDtypeStruct((t_len, RW), F32), jax.ShapeDtypeStruct((t_len, AW), F32),
                   jax.ShapeDtypeStruct((n_chunks, H, DH, DH), F32)),
        scratch_shapes=[pltpu.VMEM((H, DH, DH), F32)],
        compiler_params=_cparams(1),
    )(x, mod6, norm1, win, sgw, sgbt, sggain, cos, sin, rlf, scf)


def _fwd_b(q, k, v, of, gfb, ya, x, mod6, wout, rlb, scb, *, tm):
    t_len = x.shape[0]
    nt, nc = t_len // tm, tm // C

    def body(q_ref, k_ref, v_ref, of_ref, gfb_ref, ya_ref, x_ref, mod_ref, wout_ref, rlb_ref, scb_ref,
             ob_ref, ycat_ref, y_ref, x1_ref, rst_ref, r_scr):
        i = pl.program_id(0)

        @pl.when(i == 0)
        def _():
            r_scr[...] = scb_ref[...]

        lg = _log_sigmoid(rlb_ref[...])
        for h in range(H):
            dm, xi, zc, dec, _ = _decay_consts(lg[h:h + 1, 0:1], False)
            hs = slice(h * DH, (h + 1) * DH)
            for c in reversed(range(nc)):
                rs = slice(c * C, (c + 1) * C)
                qc, kc, vc = q_ref[rs, hs], k_ref[rs, hs], v_ref[rs, hs]
                s = r_scr[h]
                rst_ref[c, h] = s
                a = (_dot_nt(qc, kc) * dm).astype(BF16)
                ob_ref[rs, hs] = _dot(a, vc) + xi * _dot(qc, s.astype(BF16))
                kz = (kc.astype(F32) * zc).astype(BF16)
                r_scr[h] = dec * s + _dot_tn(kz, vc)

        gfb = gfb_ref[...]
        sf, _ = _silu_parts(gfb[:, :RW])
        sb, _ = _silu_parts(gfb[:, RW:])
        hnf, _ = _headnorm(of_ref[...])
        hnb, _ = _headnorm(ob_ref[...])
        yr = sf * hnf + sb * hnb
        ycat = jnp.concatenate([ya_ref[...], yr], axis=1).astype(BF16)
        ycat_ref[...] = ycat
        y = _dot(ycat, wout_ref[...])
        y_ref[...] = y.astype(BF16)
        x1_ref[...] = x_ref[...] + mod_ref[2:3, :] * y

    n_chunks = t_len // C
    rr = functools.partial(_rows_rev, nt=nt)
    return pl.pallas_call(
        body, name="fwd_b", grid=(nt,),
        in_specs=[rr(tm, RW), rr(tm, RW), rr(tm, RW), rr(tm, RW), rr(tm, 2 * RW), rr(tm, AW), rr(tm, D),
                  _whole((6, D)), _whole((D, D)), _whole((H, 128)), _whole((H, DH, DH))],
        out_specs=[rr(tm, RW), rr(tm, D), rr(tm, D), rr(tm, D),
                   pl.BlockSpec((nc, H, DH, DH), lambda i: (nt - 1 - i, 0, 0, 0))],
        out_shape=(jax.ShapeDtypeStruct((t_len, RW), F32), jax.ShapeDtypeStruct((t_len, D), BF16),
                   jax.ShapeDtypeStruct((t_len, D), BF16), jax.ShapeDtypeStruct((t_len, D), F32),
                   jax.ShapeDtypeStruct((n_chunks, H, DH, DH), F32)),
        scratch_shapes=[pltpu.VMEM((H, DH, DH), F32)],
        compiler_params=_cparams(1),
    )(q, k, v, of, gfb, ya, x, mod6, wout, rlb, scb)


def _ffn(x1, tgt, mod6, norm2, normf, wg, wu, wd, *, tm):
    t_len = x1.shape[0]
    nt = t_len // tm

    def body(x1_ref, tgt_ref, mod_ref, n2_ref, nf_ref, wg_ref, wu_ref, wd_ref,
             dx1_ref, h2_ref, da_ref, db_ref, hm_ref, df_ref, small_ref, loss_ref):
        i = pl.program_id(0)

        @pl.when(i == 0)
        def _():
            small_ref[...] = jnp.zeros_like(small_ref)

        sh2, sc2, g2 = mod_ref[3:4, :], mod_ref[4:5, :], mod_ref[5:6, :]
        n2, nf = n2_ref[...], nf_ref[...]
        x1 = x1_ref[...]
        r2 = lax.rsqrt(jnp.mean(x1 * x1, axis=-1, keepdims=True) + EPS)
        x1h = x1 * r2
        w2 = n2 * (1.0 + sc2)
        h2b = (x1h * w2 + sh2).astype(BF16)
        h2_ref[...] = h2b
        a = _dot(h2b, wg_ref[...])
        b = _dot(h2b, wu_ref[...])
        sa, dsa = _silu_parts(a)
        hmb = (sa * b).astype(BF16)
        hm_ref[...] = hmb
        f = _dot(hmb, wd_ref[...])
        x2 = x1 + g2 * f
        r3 = lax.rsqrt(jnp.mean(x2 * x2, axis=-1, keepdims=True) + EPS)
        x2h = x2 * r3
        diff = x2h * nf - tgt_ref[...]
        small_ref[5:6, :] += jnp.sum(diff * diff, axis=0, keepdims=True)
        dout = diff * (1.0 / D)
        small_ref[0:1, :] += jnp.sum(dout * x2h, axis=0, keepdims=True)
        dyg = dout * nf
        dx2 = r3 * (dyg - x2h * jnp.mean(dyg * x2h, axis=-1, keepdims=True))
        small_ref[1:2, :] += jnp.sum(dx2 * f, axis=0, keepdims=True)
        dfb = (dx2 * g2).astype(BF16)
        df_ref[...] = dfb
        dhm = _dot_nt(dfb, wd_ref[...])
        dbb = (dhm * sa).astype(BF16)
        dab = (dhm * b * dsa).astype(BF16)
        da_ref[...] = dab
        db_ref[...] = dbb
        dh2 = _dot_nt(dab, wg_ref[...]) + _dot_nt(dbb, wu_ref[...])
        small_ref[2:3, :] += jnp.sum(dh2, axis=0, keepdims=True)
        dhx = dh2 * x1h
        small_ref[3:4, :] += jnp.sum(dhx, axis=0, keepdims=True) * n2
        small_ref[4:5, :] += jnp.sum(dhx, axis=0, keepdims=True) * (1.0 + sc2)
        dyg2 = dh2 * w2
        dx1_ref[...] = dx2 + r2 * (dyg2 - x1h * jnp.mean(dyg2 * x1h, axis=-1, keepdims=True))

        @pl.when(i == nt - 1)
        def _():
            loss_ref[...] = jnp.full(loss_ref.shape, (0.5 / D) * jnp.sum(small_ref[5:6, :]), F32)

    return pl.pallas_call(
        body, name="ffn", grid=(nt,),
        in_specs=[_rows(tm, D), _rows(tm, D), _whole((6, D)), _whole((1, D)), _whole((1, D)),
                  _whole((D, DFF)), _whole((D, DFF)), _whole((DFF, D))],
        out_specs=[_rows(tm, D), _rows(tm, D), _rows(tm, DFF), _rows(tm, DFF), _rows(tm, DFF), _rows(tm, D),
                   _whole((8, D)), _whole((8, 128))],
        out_shape=(jax.ShapeDtypeStruct((t_len, D), F32), jax.ShapeDtypeStruct((t_len, D), BF16),
                   jax.ShapeDtypeStruct((t_len, DFF), BF16), jax.ShapeDtypeStruct((t_len, DFF), BF16),
                   jax.ShapeDtypeStruct((t_len, DFF), BF16), jax.ShapeDtypeStruct((t_len, D), BF16),
                   jax.ShapeDtypeStruct((8, D), F32), jax.ShapeDtypeStruct((8, 128), F32)),
        compiler_params=_cparams(1),
    )(x1, tgt, mod6, norm2, normf, wg, wu, wd)


def _mid_bwd(dx1, y, of, ob, gfb, mod6, wout, *, tm):
    t_len = dx1.shape[0]
    nt = t_len // tm

    def body(dx1_ref, y_ref, of_ref, ob_ref, gfb_ref, mod_ref, wout_ref,
             dy_ref, dya_ref, dgfb_ref, dof_ref, dob_ref, dg1_ref):
        i = pl.program_id(0)

        @pl.when(i == 0)
        def _():
            dg1_ref[...] = jnp.zeros_like(dg1_ref)

        dx1 = dx1_ref[...]
        dg1_ref[0:1, :] += jnp.sum(dx1 * y_ref[...].astype(F32), axis=0, keepdims=True)
        dyb = (dx1 * mod_ref[2:3, :]).astype(BF16)
        dy_ref[...] = dyb
        dycat = _dot_nt(dyb, wout_ref[...])
        dya_ref[...] = dycat[:, :AW]
        dyr = dycat[:, AW:]
        gfb = gfb_ref[...]
        for o_ref, do_ref, lo in ((of_ref, dof_ref, 0), (ob_ref, dob_ref, RW)):
            sg, dsg = _silu_parts(gfb[:, lo:lo + RW])
            o = o_ref[...]
            hn, rs = _headnorm(o)
            dgfb_ref[:, lo:lo + RW] = (dyr * hn * dsg).astype(BF16)
            dhn = dyr * sg
            for h in range(H):
                hs = slice(h * DH, (h + 1) * DH)
                oh, dh = hn[:, hs], dhn[:, hs]
                do_ref[:, hs] = (rs[h] * (dh - oh * jnp.mean(dh * oh, axis=-1, keepdims=True))).astype(BF16)

    return pl.pallas_call(
        body, name="mid_bwd", grid=(nt,),
        in_specs=[_rows(tm, D), _rows(tm, D), _rows(tm, RW), _rows(tm, RW), _rows(tm, 2 * RW),
                  _whole((6, D)), _whole((D, D))],
        out_specs=[_rows(tm, D), _rows(tm, AW), _rows(tm, 2 * RW), _rows(tm, RW), _rows(tm, RW), _whole((8, D))],
        out_shape=(jax.ShapeDtypeStruct((t_len, D), BF16), jax.ShapeDtypeStruct((t_len, AW), F32),
                   jax.ShapeDtypeStruct((t_len, 2 * RW), BF16), jax.ShapeDtypeStruct((t_len, RW), BF16),
                   jax.ShapeDtypeStruct((t_len, RW), BF16), jax.ShapeDtypeStruct((8, D), F32)),
        compiler_params=_cparams(1),
    )(dx1, y, of, ob, gfb, mod6, wout)


def _ret_bwd_dir(q_ref, k_ref, v_ref, do_ref, st_ref, dq_ref, dk_ref, dv_ref, ds_scr, dlg_scr, lg, forward, nc, row0):
    order = reversed(range(nc)) if forward else range(nc)
    order = list(order)
    for h in range(H):
        dm, xi, zc, dec, ic = _decay_consts(lg[h:h + 1, 0:1], forward)
        hs = slice(h * DH, (h + 1) * DH)
        if forward:
            w_qi, w_qc, w_ki, w_ks = ic, ic + 1.0, -ic, (C - 1.0) - ic
        else:
            w_qi, w_qc, w_ki, w_ks = -ic, C - ic, ic, ic
        acc = jnp.zeros((1, DH), F32)
        for c in order:
            rs = slice(c * C, (c + 1) * C)
            qc, kc, vc, doc = q_ref[rs, hs], k_ref[rs, hs], v_ref[rs, hs], do_ref[rs, hs]
            s = st_ref[c, h]
            dsn = ds_scr[h]
            sb, dsnb = s.astype(BF16), dsn.astype(BF16)
            qf, kf = qc.astype(F32), kc.astype(F32)
            a = (_dot_nt(qc, kc) * dm).astype(BF16)
            da = (_dot_nt(doc, vc) * dm).astype(BF16)
            xdo = (xi * doc.astype(F32)).astype(BF16)
            kz = (kf * zc).astype(BF16)
            vz = (vc.astype(F32) * zc).astype(BF16)
            dq_i = _dot(da, kc)
            dq_c = _dot_nt(xdo, sb)
            dk_i = _dot_tn(da, qc)
            dk_s = _dot_nt(vz, dsnb)
            dq_ref[rs, hs] = dq_i + dq_c
            dk_ref[rs, hs] = dk_i + dk_s
            dv_ref[rs, hs] = _dot_tn(a, doc) + _dot(kz, dsnb)
            rowterm = qf * (w_qi * dq_i + w_qc * dq_c) + kf * (w_ki * dk_i + w_ks * dk_s)
            acc = acc + jnp.sum(rowterm, axis=0, keepdims=True)
            acc = acc + (float(C) * dec) * jnp.sum(s * dsn, axis=0, keepdims=True)
            ds_scr[h] = _dot_tn((xi * qf).astype(BF16), doc) + dec * dsn
        dlg_scr[row0 + h:row0 + h + 1, :] += acc


def _ret_bwd(q, k, v, dof, dob, sst, rst, rlf, rlb, *, tm):
    t_len = q.shape[0]
    nt, nc = t_len // tm, tm // C

    def body(qf_ref, kf_ref, vf_ref, dof_ref, sst_ref, qb_ref, kb_ref, vb_ref, dob_ref, rst_ref, rlf_ref, rlb_ref,
             dqf_ref, dkf_ref, dvf_ref, dqb_ref, dkb_ref, dvb_ref, dscf_ref, dscb_ref, dlg_ref,
             dsf_scr, dsb_scr, dlg_scr):
        i = pl.program_id(0)

        @pl.when(i == 0)
        def _():
            dsf_scr[...] = jnp.zeros_like(dsf_scr)
            dsb_scr[...] = jnp.zeros_like(dsb_scr)
            dlg_scr[...] = jnp.zeros_like(dlg_scr)

        lgf = _log_sigmoid(rlf_ref[...])
        lgb = _log_sigmoid(rlb_ref[...])
        _ret_bwd_dir(qf_ref, kf_ref, vf_ref, dof_ref, sst_ref, dqf_ref, dkf_ref, dvf_ref, dsf_scr, dlg_scr, lgf, True, nc, 0)
        _ret_bwd_dir(qb_ref, kb_ref, vb_ref, dob_ref, rst_ref, dqb_ref, dkb_ref, dvb_ref, dsb_scr, dlg_scr, lgb, False, nc, H)

        @pl.when(i == nt - 1)
        def _():
            dscf_ref[...] = dsf_scr[...]
            dscb_ref[...] = dsb_scr[...]
            dlg_ref[...] = jnp.broadcast_to(jnp.sum(dlg_scr[...], axis=1, keepdims=True), dlg_ref.shape)

    rr = functools.partial(_rows_rev, nt=nt)
    st_f = pl.BlockSpec((nc, H, DH, DH), lambda i: (nt - 1 - i, 0, 0, 0))
    st_b = pl.BlockSpec((nc, H, DH, DH), lambda i: (i, 0, 0, 0))
    tok = jax.ShapeDtypeStruct((t_len, RW), F32)
    st = jax.ShapeDtypeStruct((H, DH, DH), F32)
    return pl.pallas_call(
        body, name="ret_bwd", grid=(nt,),
        in_specs=[rr(tm, RW), rr(tm, RW), rr(tm, RW), rr(tm, RW), st_f,
                  _rows(tm, RW), _rows(tm, RW), _rows(tm, RW), _rows(tm, RW), st_b,
                  _whole((H, 128)), _whole((H, 128))],
        out_specs=[rr(tm, RW), rr(tm, RW), rr(tm, RW), _rows(tm, RW), _rows(tm, RW), _rows(tm, RW),
                   _whole((H, DH, DH)), _whole((H, DH, DH)), _whole((8, 128))],
        out_shape=(tok, tok, tok, tok, tok, tok, st, st, jax.ShapeDtypeStruct((8, 128), F32)),
        scratch_shapes=[pltpu.VMEM((H, DH, DH), F32), pltpu.VMEM((H, DH, DH), F32), pltpu.VMEM((8, 128), F32)],
        compiler_params=_cparams(1),
    )(q, k, v, dof, sst, q, k, v, dob, rst, rlf, rlb)


def _in_bwd(x, zuv, dya, dqf, dkf, dvf, dqb, dkb, dvb, dgfb, dx1, cos, sin, mod6, norm1, win, sgw, sgbt, sggain, *, tm):
    t_len = x.shape[0]
    nt, nc = t_len // tm, tm // C

    def body(x_ref, zuv_ref, dya_ref, dqf_ref, dkf_ref, dvf_ref, dqb_ref, dkb_ref, dvb_ref, dgfb_ref, dx1_ref,
             cos_ref, sin_ref, mod_ref, n1_ref, win_ref, sgw_ref, sgbt_ref, sgg_ref,
             gx_ref, dz_ref, small_ref, dsgw_ref, dsgbt_ref):
        i = pl.program_id(0)

        @pl.when(i == 0)
        def _():
            small_ref[...] = jnp.zeros_like(small_ref)
            dsgw_ref[...] = jnp.zeros_like(dsgw_ref)
            dsgbt_ref[...] = jnp.zeros_like(dsgbt_ref)

        zuv = zuv_ref[...]
        _, (ua, dgu, dgv, mixed, vn_b, vah_l, rv_l) = _sg_forward(zuv[:, :AW], zuv[:, AW:], sgw_ref, sgbt_ref, sgg_ref, nc)
        dya = dya_ref[...]
        dz_ref[:, 0:AW] = (dya * mixed * dgu).astype(BF16)
        dmixed = dya * ua
        gain = sgg_ref[...]
        for g in range(G):
            gs = slice(g * DH, (g + 1) * DH)
            dmg = dmixed[:, gs]
            dmb = dmg.astype(BF16)
            wgt = sgw_ref[g].astype(BF16)
            dsw = jnp.zeros((C, C), F32)
            dsb = jnp.zeros((C, DH), F32)
            rows = []
            for c in range(nc):
                rs = slice(c * C, (c + 1) * C)
                dsw = dsw + _dot_nt(dmb[rs, :], vn_b[g][rs, :])
                dsb = dsb + dmg[rs, :]
                rows.append(_dot_tn(wgt, dmb[rs, :]))
            dsgw_ref[g] += dsw
            dsgbt_ref[g] += dsb
            dvn = jnp.concatenate(rows, axis=0) if nc > 1 else rows[0]
            vah, rv = vah_l[g], rv_l[g]
            small_ref[3:4, gs] += jnp.sum(dvn * vah, axis=0, keepdims=True)
            dyg = dvn * gain[:, gs]
            dva = rv * (dyg - vah * jnp.mean(dyg * vah, axis=-1, keepdims=True))
            dz_ref[:, AW + g * DH:AW + (g + 1) * DH] = (dva * dgv[:, gs]).astype(BF16)

        cos = jnp.tile(cos_ref[...], (1, H))
        nsins = -jnp.tile(sin_ref[...], (1, H))
        dz_ref[:, 1024:1536] = _rope(dqf_ref[...] + dqb_ref[...], cos, nsins).astype(BF16)
        dz_ref[:, 1536:2048] = (_rope(dkf_ref[...] + dkb_ref[...], cos, nsins) * KSCALE).astype(BF16)
        dz_ref[:, 2048:2560] = (dvf_ref[...] + dvb_ref[...]).astype(BF16)
        dz_ref[:, 2560:3584] = dgfb_ref[...]

        dhx = _dot_nt(dz_ref[...], win_ref[...])
        x = x_ref[...]
        r = lax.rsqrt(jnp.mean(x * x, axis=-1, keepdims=True) + EPS)
        xh = x * r
        n1, sc1 = n1_ref[...], mod_ref[1:2, :]
        small_ref[0:1, :] += jnp.sum(dhx, axis=0, keepdims=True)
        dhxx = jnp.sum(dhx * xh, axis=0, keepdims=True)
        small_ref[1:2, :] += dhxx * n1
        small_ref[2:3, :] += dhxx * (1.0 + sc1)
        dyg = dhx * (n1 * (1.0 + sc1))
        gx_ref[...] = dx1_ref[...] + r * (dyg - xh * jnp.mean(dyg * xh, axis=-1, keepdims=True))

        @pl.when(i == nt - 1)
        def _():
            for g in range(G):
                dsgbt_ref[g] = jnp.broadcast_to(jnp.sum(dsgbt_ref[g], axis=1, keepdims=True), (C, DH))

    tok = functools.partial(_rows, tm)
    return pl.pallas_call(
        body, name="in_bwd", grid=(nt,),
        in_specs=[tok(D), tok(2 * AW), tok(AW), tok(RW), tok(RW), tok(RW), tok(RW), tok(RW), tok(RW),
                  tok(2 * RW), tok(D), tok(DH), tok(DH), _whole((6, D)), _whole((1, D)), _whole((D, INC)),
                  _whole((G, C, C)), _whole((G, C, 128)), _whole((1, AW))],
        out_specs=[tok(D), tok(INC), _whole((8, D)), _whole((G, C, C)), _whole((G, C, 128))],
        out_shape=(jax.ShapeDtypeStruct((t_len, D), F32), jax.ShapeDtypeStruct((t_len, INC), BF16),
                   jax.ShapeDtypeStruct((8, D), F32), jax.ShapeDtypeStruct((G, C, C), F32),
                   jax.ShapeDtypeStruct((G, C, 128), F32)),
        compiler_params=_cparams(1),
    )(x, zuv, dya, dqf, dkf, dvf, dqb, dkb, dvb, dgfb, dx1, cos, sin, mod6, norm1, win, sgw, sgbt, sggain)


def _tn_matmul(a, b, *, bn, bt, name, init=None, init_blocks=()):
    t_len, m = a.shape
    n = b.shape[1]
    nj, ntt = n // bn, t_len // bt
    has_init = init is not None

    def body(*refs):
        if has_init:
            a_ref, b_ref, init_ref, o_ref = refs
        else:
            a_ref, b_ref, o_ref = refs
        j, t = pl.program_id(0), pl.program_id(1)

        @pl.when(t == 0)
        def _():
            o_ref[...] = jnp.zeros_like(o_ref)

        if has_init:
            hit = functools.reduce(jnp.logical_or, [j == jb for jb in init_blocks])

            @pl.when(jnp.logical_and(t == 0, hit))
            def _():
                o_ref[...] = init_ref[...]

        o_ref[...] += _dot_tn(a_ref[...], b_ref[...])

    in_specs = [pl.BlockSpec((bt, m), lambda j, t: (t, 0)), pl.BlockSpec((bt, bn), lambda j, t: (t, j))]
    args = [a, b]
    if has_init:
        lo, cnt = init_blocks[0], len(init_blocks)
        in_specs.append(pl.BlockSpec((m, bn), lambda j, t: (0, jnp.clip(j - lo, 0, cnt - 1))))
        args.append(init)
    return pl.pallas_call(
        body, name=name, grid=(nj, ntt),
        in_specs=in_specs,
        out_specs=pl.BlockSpec((m, bn), lambda j, t: (0, j)),
        out_shape=jax.ShapeDtypeStruct((m, n), F32),
        compiler_params=_cparams(2),
    )(*args)


def _ctx_bwd(ctx, hc, kvc, cmod6, norm1, win, rlf, rlb, dscf, dscb):
    lc = ctx.shape[0]

    def body(ctx_ref, hc_ref, kvc_ref, cmod_ref, n1_ref, win_ref, rlf_ref, rlb_ref, dscf_ref, dscb_ref,
             dwin_ref, small_ref, dlg_ref):
        k = kvc_ref[:, :RW]
        v = kvc_ref[:, RW:]
        t = lax.broadcasted_iota(jnp.int32, (lc, 1), 0).astype(F32)
        lgf = _log_sigmoid(rlf_ref[...])
        lgb = _log_sigmoid(rlb_ref[...])
        dks, dvs = [], []
        dlg_ref[...] = jnp.zeros_like(dlg_ref)
        for h in range(H):
            hs = slice(h * DH, (h + 1) * DH)
            wf = jnp.exp(lgf[h:h + 1, 0:1] * (lc - 1.0 - t))
            wb = jnp.exp(lgb[h:h + 1, 0:1] * t)
            kh, vhb = k[:, hs], v[:, hs].astype(BF16)
            dsf, dsb = dscf_ref[h].astype(BF16), dscb_ref[h].astype(BF16)
            dkf = wf * _dot_nt(vhb, dsf)
            dkb = wb * _dot_nt(vhb, dsb)
            dvs.append(_dot((kh * wf).astype(BF16), dsf) + _dot((kh * wb).astype(BF16), dsb))
            dks.append((dkf + dkb) * KSCALE)
            lf = jnp.sum((lc - 1.0 - t) * kh * dkf, axis=0, keepdims=True)
            lb = jnp.sum(t * kh * dkb, axis=0, keepdims=True)
            dlg_ref[h:h + 1, :] = jnp.broadcast_to(jnp.sum(lf, axis=1, keepdims=True), (1, 128))
            dlg_ref[H + h:H + h + 1, :] = jnp.broadcast_to(jnp.sum(lb, axis=1, keepdims=True), (1, 128))
        dkv = jnp.concatenate(dks + dvs, axis=1).astype(BF16)
        dhc = _dot_nt(dkv, win_ref[:, KV_LO:KV_HI])
        dwin_ref[...] = _dot_tn(hc_ref[...], dkv)
        x = ctx_ref[...]
        r = lax.rsqrt(jnp.mean(x * x, axis=-1, keepdims=True) + EPS)
        xh = x * r
        small_ref[...] = jnp.zeros_like(small_ref)
        small_ref[0:1, :] = jnp.sum(dhc, axis=0, keepdims=True)
        dhxx = jnp.sum(dhc * xh, axis=0, keepdims=True)
        small_ref[1:2, :] = dhxx * n1_ref[...]
        small_ref[2:3, :] = dhxx * (1.0 + cmod_ref[1:2, :])

    return pl.pallas_call(
        body, name="ctx_bwd",
        out_shape=(jax.ShapeDtypeStruct((D, 2 * RW), F32), jax.ShapeDtypeStruct((8, D), F32),
                   jax.ShapeDtypeStruct((8, 128), F32)),
        compiler_params=_cparams(),
    )(ctx, hc, kvc, cmod6, norm1, win, rlf, rlb, dscf, dscb)


def _adam_math(w, g, m, v):
    m = ADAM_B1 * m + (1.0 - ADAM_B1) * g
    v = ADAM_B2 * v + (1.0 - ADAM_B2) * (g * g)
    m_hat = m / (1.0 - ADAM_B1 ** ADAM_STEP)
    v_hat = v / (1.0 - ADAM_B2 ** ADAM_STEP)
    delta = -ADAM_LR * (m_hat / (jnp.sqrt(v_hat) + ADAM_EPS) + ADAM_WD * w)
    return delta, m, v


def _adam(w, g, m, v, *, name, br=None):
    r, c = w.shape
    br = r if br is None else br

    def body(w_ref, g_ref, m_ref, v_ref, d_ref, mo_ref, vo_ref):
        d_ref[...], mo_ref[...], vo_ref[...] = _adam_math(w_ref[...], g_ref[...], m_ref[...], v_ref[...])

    spec = pl.BlockSpec((br, c), lambda i: (i, 0))
    sh = jax.ShapeDtypeStruct((r, c), F32)
    return pl.pallas_call(
        body, name=name, grid=(r // br,), in_specs=[spec] * 4, out_specs=[spec] * 3, out_shape=(sh, sh, sh),
        compiler_params=_cparams(1),
    )(w, g, m, v)


def _place():
    x, y, c = lax.axis_index("x"), lax.axis_index("y"), lax.axis_index("c")
    return x, y, c, 4 * x + 2 * y + c


def _flip(x, y, c, k):
    px = 1 - x if (k >> 2) & 1 else x
    py = 1 - y if (k >> 1) & 1 else y
    pc = 1 - c if k & 1 else c
    return (px, py, pc), 4 * px + 2 * py + pc


def _gather_direct(buf_ref, send_sems, recv_sems, sem0=0):
    x, y, c, me = _place()
    sends = []
    for k in range(1, NDEV):
        dev, _ = _flip(x, y, c, k)
        cp = pltpu.make_async_remote_copy(
            src_ref=buf_ref.at[me], dst_ref=buf_ref.at[me],
            send_sem=send_sems.at[sem0 + k - 1], recv_sem=recv_sems.at[sem0 + k - 1],
            device_id=dev, device_id_type=MESH)
        cp.start()
        sends.append(cp)
    for k in range(1, NDEV):
        dev, idx = _flip(x, y, c, k)
        pltpu.make_async_remote_copy(
            src_ref=buf_ref.at[idx], dst_ref=buf_ref.at[idx],
            send_sem=send_sems.at[sem0 + k - 1], recv_sem=recv_sems.at[sem0 + k - 1],
            device_id=dev, device_id_type=MESH).wait_recv()
    for cp in sends:
        cp.wait_send()


def _mod_gather(c_row, cctx_row, wmod, bmod):
    ncol = wmod.shape[1]

    def body(c_ref, cctx_ref, wmod_ref, bmod_ref, mod_ref, cs_ref, cbuf, pbuf, send_sems, recv_sems):
        _, _, _, me = _place()
        cbuf[me] = jnp.broadcast_to(c_ref[...], (8, D))
        _gather_direct(cbuf, send_sems, recv_sems, 0)
        row = lax.broadcasted_iota(jnp.int32, (16, D), 0)
        call = jnp.where(row == 8, jnp.broadcast_to(cctx_ref[...], (16, D)), 0.0)
        for d in range(NDEV):
            call = jnp.where(row == d, jnp.concatenate([cbuf[d], cbuf[d]], axis=0), call)
        cs = call * _sigmoid(call)
        cs_ref[...] = cs
        pbuf[me] = _dot(cs.astype(BF16), wmod_ref[...].astype(BF16))
        _gather_direct(pbuf, send_sems, recv_sems, NDEV - 1)
        for d in range(NDEV):
            mod_ref[:, d * ncol:(d + 1) * ncol] = pbuf[d] + bmod_ref[:, d * ncol:(d + 1) * ncol]

    return pl.pallas_call(
        body, name="mod_gather",
        out_shape=(jax.ShapeDtypeStruct((16, 6 * D), F32), jax.ShapeDtypeStruct((16, D), F32)),
        scratch_shapes=[pltpu.VMEM((NDEV, 8, D), F32), pltpu.VMEM((NDEV, 16, ncol), F32),
                        pltpu.SemaphoreType.DMA((2 * (NDEV - 1),)), pltpu.SemaphoreType.DMA((2 * (NDEV - 1),))],
        compiler_params=_cparams(),
    )(c_row, cctx_row, wmod, bmod)


def _weight_gather(shards):
    n = len(shards)

    def body(*refs):
        in_refs, out_refs = refs[:n], refs[n:2 * n]
        cast_refs = refs[2 * n:3 * n]
        send_sems, recv_sems, local_sems = refs[3 * n:]
        x, y, c, me = _place()
        sib = (x, y, 1 - c)
        chips = [(1 - x, y), (x, 1 - y), (1 - x, 1 - y)]

        def blk(px, py, pc):
            return 4 * px + 2 * py + pc

        def copy(w, k, block, to, src=None):
            dst = out_refs[w].at[block]
            return pltpu.make_async_remote_copy(
                src_ref=dst if src is None else src, dst_ref=dst,
                send_sem=send_sems.at[w, k], recv_sem=recv_sems.at[w, k], device_id=to, device_id_type=MESH)

        first, passed, mine = [], [], []
        for w in range(n):
            cast_refs[w][...] = in_refs[w][...].astype(BF16)
            m = pltpu.make_async_copy(cast_refs[w], out_refs[w].at[me], local_sems.at[w])
            m.start()
            mine.append(m)
            cps = [copy(w, 0, me, sib, src=cast_refs[w])]
            cps += [copy(w, 1 + j, me, (*chip, c), src=cast_refs[w]) for j, chip in enumerate(chips)]
            for cp in cps:
                cp.start()
            first += cps
        for w in range(n):
            for j, chip in enumerate(chips):
                b = blk(*chip, c)
                copy(w, 1 + j, b, (x, y, c)).wait_recv()
                fw = copy(w, 4 + j, b, sib)
                fw.start()
                passed.append(fw)
        for w in range(n):
            copy(w, 0, blk(x, y, 1 - c), (x, y, c)).wait_recv()
            for j, chip in enumerate(chips):
                copy(w, 4 + j, blk(*chip, 1 - c), (x, y, c)).wait_recv()
        for cp in first + passed:
            cp.wait_send()
        for m in mine:
            m.wait()

    vm = pl.BlockSpec(memory_space=pltpu.VMEM)
    hbm = pl.BlockSpec(memory_space=pltpu.HBM)
    return pl.pallas_call(
        body, name="weight_gather",
        in_specs=[vm] * n, out_specs=[hbm] * n,
        out_shape=tuple(jax.ShapeDtypeStruct((NDEV,) + s.shape, BF16) for s in shards),
        scratch_shapes=[pltpu.VMEM(s.shape, BF16) for s in shards]
        + [pltpu.SemaphoreType.DMA((n, 7)), pltpu.SemaphoreType.DMA((n, 7)), pltpu.SemaphoreType.DMA((n,))],
        compiler_params=_cparams(),
    )(*shards)


def _reduce_scatter(g8, *, name):
    _, r, cdim = g8.shape

    def body(g_ref, out_ref, s1, r1, s2, r2, ss1, rs1, ss2, rs2):
        x, y, c, me = _place()
        sib = (x, y, 1 - c)
        sends = []
        for j in range(4):
            px, py = j >> 1, j & 1
            s1[j] = g_ref[4 * px + 2 * py + (1 - c)].astype(BF16)
            cp = pltpu.make_async_remote_copy(src_ref=s1.at[j], dst_ref=r1.at[j], send_sem=ss1.at[j],
                                              recv_sem=rs1.at[j], device_id=sib, device_id_type=MESH)
            cp.start()
            sends.append(cp)
        for j in range(4):
            pltpu.make_async_remote_copy(src_ref=s1.at[j], dst_ref=r1.at[j], send_sem=ss1.at[j],
                                         recv_sem=rs1.at[j], device_id=sib, device_id_type=MESH).wait_recv()
        flips = [(1, 0), (0, 1), (1, 1)]
        for k, (dx, dy) in enumerate(flips):
            px = 1 - x if dx else x
            py = 1 - y if dy else y
            s2[k] = (g_ref[4 * px + 2 * py + c] + r1[2 * px + py].astype(F32)).astype(BF16)
            cp = pltpu.make_async_remote_copy(src_ref=s2.at[k], dst_ref=r2.at[k], send_sem=ss2.at[k],
                                              recv_sem=rs2.at[k], device_id=(px, py, c), device_id_type=MESH)
            cp.start()
            sends.append(cp)
        acc = g_ref[me] + r1[2 * x + y].astype(F32)
        for k, (dx, dy) in enumerate(flips):
            px = 1 - x if dx else x
            py = 1 - y if dy else y
            pltpu.make_async_remote_copy(src_ref=s2.at[k], dst_ref=r2.at[k], send_sem=ss2.at[k],
                                         recv_sem=rs2.at[k], device_id=(px, py, c), device_id_type=MESH).wait_recv()
            acc = acc + r2[k].astype(F32)
        out_ref[...] = acc
        for cp in sends:
            cp.wait_send()

    return pl.pallas_call(
        body, name=name,
        out_shape=jax.ShapeDtypeStruct((r, cdim), F32),
        scratch_shapes=[pltpu.VMEM((4, r, cdim), BF16), pltpu.VMEM((4, r, cdim), BF16),
                        pltpu.VMEM((3, r, cdim), BF16), pltpu.VMEM((3, r, cdim), BF16),
                        pltpu.SemaphoreType.DMA((4,)), pltpu.SemaphoreType.DMA((4,)),
                        pltpu.SemaphoreType.DMA((3,)), pltpu.SemaphoreType.DMA((3,))],
        compiler_params=_cparams(),
    )(g8)


def _small_reduce(pack, rl_row):
    def body(pack_ref, rl_ref, sum_ref, all_ref, send_sems, recv_sems):
        _, _, _, me = _place()
        all_ref[me] = pack_ref[...]
        _gather_direct(all_ref, send_sems, recv_sems, 0)
        acc = all_ref[0]
        for d in range(1, NDEV):
            acc = acc + all_ref[d]
        sum_ref[...] = acc
        sum_ref[69:70, :] = acc[69:70, :] * _sigmoid(-rl_ref[...])

    return pl.pallas_call(
        body, name="small_reduce",
        out_shape=(jax.ShapeDtypeStruct((PACK_ROWS, D), F32), jax.ShapeDtypeStruct((NDEV, PACK_ROWS, D), F32)),
        scratch_shapes=[pltpu.SemaphoreType.DMA((NDEV - 1,)), pltpu.SemaphoreType.DMA((NDEV - 1,))],
        compiler_params=_cparams(),
    )(pack, rl_row)


def _wmod_grad(cs_all, dmod_cols, wmod, m, v):
    ncol = wmod.shape[1]

    def body(cs_ref, dm_ref, w_ref, m_ref, v_ref, g_ref, d_ref, mo_ref, vo_ref, part_ref):
        dmb = dm_ref[...].astype(BF16)
        g = _dot_tn(cs_ref[...].astype(BF16), dmb)
        g_ref[...] = g
        d_ref[...], mo_ref[...], vo_ref[...] = _adam_math(w_ref[...], g, m_ref[...], v_ref[...])
        part_ref[...] = _dot_nt(dmb, w_ref[...].astype(BF16))

    sh = jax.ShapeDtypeStruct((D, ncol), F32)
    return pl.pallas_call(
        body, name="wmod_grad", out_shape=(sh, sh, sh, sh, jax.ShapeDtypeStruct((16, D), F32)),
        compiler_params=_cparams(),
    )(cs_all, dmod_cols, wmod, m, v)


def _cctx_reduce(part, cctx_row, m_row, v_row):
    def body(part_ref, c_ref, m_ref, v_ref, g_ref, d_ref, mo_ref, vo_ref, buf, send_sems, recv_sems):
        _, _, _, me = _place()
        buf[me] = part_ref[8:16, :]
        _gather_direct(buf, send_sems, recv_sems, 0)
        acc = buf[0]
        for d in range(1, NDEV):
            acc = acc + buf[d]
        cc = c_ref[...]
        _, dsilu = _silu_parts(cc)
        g = acc[0:1, :] * dsilu
        g_ref[...] = g
        d_ref[...], mo_ref[...], vo_ref[...] = _adam_math(cc, g, m_ref[...], v_ref[...])

    sh = jax.ShapeDtypeStruct((1, D), F32)
    return pl.pallas_call(
        body, name="cctx_reduce", out_shape=(sh, sh, sh, sh),
        scratch_shapes=[pltpu.VMEM((NDEV, 8, D), F32),
                        pltpu.SemaphoreType.DMA((NDEV - 1,)), pltpu.SemaphoreType.DMA((NDEV - 1,))],
        compiler_params=_cparams(),
    )(part, cctx_row, m_row, v_row)


def _rope_tables(t_len):
    n_freq = DH // 4
    inv = ROPE_BASE ** (-jnp.arange(n_freq, dtype=F32) / n_freq)
    tok = jnp.arange(t_len, dtype=jnp.int32)
    rows = (tok // GRID_W).astype(F32)
    cols = (tok % GRID_W).astype(F32)
    ang_r = rows[:, None] * inv[None, :]
    ang_c = cols[:, None] * inv[None, :]
    cos = jnp.concatenate([jnp.cos(ang_r)] * 2 + [jnp.cos(ang_c)] * 2, axis=1)
    sin = jnp.concatenate([-jnp.sin(ang_r), jnp.sin(ang_r), -jnp.sin(ang_c), jnp.sin(ang_c)], axis=1)
    return cos, sin


def _cols_from_blocks(g8):
    n, r, c = g8.shape
    return jnp.transpose(g8, (1, 0, 2)).reshape(r, n * c)


def _blocks_from_cols(w, n=NDEV):
    r, nc = w.shape
    return jnp.transpose(w.reshape(r, n, nc // n), (1, 0, 2))


def _pad_row(v, width=D):
    v = v.reshape(1, -1)
    return jnp.pad(v, ((0, 0), (0, width - v.shape[1])))


def _local_step(x, tgt, ctx, mod6, cmod6, norm1, norm2, normf, win, wout, wg, wu, wd, sgw, sgb, sggain, rlf, rlb,
                *, tm_a, tm_b, tm_f, tm_m, tm_r, tm_i, bt_w):
    t_len = x.shape[0]
    cos, sin = _rope_tables(t_len)
    sgbt = jnp.broadcast_to(sgb[:, :, None], (G, C, 128))
    rlf = jnp.broadcast_to(rlf.reshape(H, 1), (H, 128))
    rlb = jnp.broadcast_to(rlb.reshape(H, 1), (H, 128))
    hc, kvc, scf, scb = _ctx_fwd(ctx, cmod6, norm1, win, rlf, rlb)
    hx, zuv, q, k, v, gfb, of, ya, sst = _fwd_a(x, mod6, norm1, win, sgw, sgbt, sggain, cos, sin, rlf, scf, tm=tm_a)
    ob, ycat, y, x1, rst = _fwd_b(q, k, v, of, gfb, ya, x, mod6, wout, rlb, scb, tm=tm_b)
    dx1, h2, da, db, hm, df, small_f, loss = _ffn(x1, tgt, mod6, norm2, normf, wg, wu, wd, tm=tm_f)
    dy, dya, dgfb, dof, dob, dg1 = _mid_bwd(dx1, y, of, ob, gfb, mod6, wout, tm=tm_m)
    dqf, dkf, dvf, dqb, dkb, dvb, dscf, dscb, dlg = _ret_bwd(q, k, v, dof, dob, sst, rst, rlf, rlb, tm=tm_r)
    gx, dz, small_i, dsgw, dsgbt = _in_bwd(x, zuv, dya, dqf, dkf, dvf, dqb, dkb, dvb, dgfb, dx1, cos, sin,
                                           mod6, norm1, win, sgw, sgbt, sggain, tm=tm_i)
    dwin_c, small_c, dlg_c = _ctx_bwd(ctx, hc, kvc, cmod6, norm1, win, rlf, rlb, dscf, dscb)
    d_wd = _tn_matmul(hm, df, bn=D, bt=bt_w, name="dw_down")
    d_wg = _tn_matmul(h2, da, bn=DFF // 2, bt=bt_w, name="dw_gate")
    d_wu = _tn_matmul(h2, db, bn=DFF // 2, bt=bt_w, name="dw_up")
    d_wo = _tn_matmul(ycat, dy, bn=D, bt=bt_w, name="dw_out")
    d_wi = _tn_matmul(hx, dz, bn=512, bt=bt_w, name="dw_in", init=dwin_c, init_blocks=(3, 4))
    dlg8 = (dlg[:, 0] + dlg_c[:, 0]).reshape(1, 8)
    zero_row = jnp.zeros((1, D), F32)
    pack = jnp.concatenate([
        dsgw.reshape(64, D),
        small_i[2:3] + small_c[2:3],
        small_f[4:5],
        small_f[0:1],
        small_i[3:4],
        _pad_row(dsgbt[:, :, 0]),
        _pad_row(dlg8),
        small_i[0:1], small_i[1:2], dg1[0:1], small_f[2:3], small_f[3:4], small_f[1:2],
        small_c[0:1], small_c[1:2], zero_row, zero_row, zero_row, zero_row,
        jnp.zeros((PACK_ROWS - 82, D), F32)], axis=0)
    return loss[0, 0], gx, d_wi, d_wo, d_wg, d_wu, d_wd, pack


def kernel(x, c, ctx, c_ctx, w_mod, b_mod, norm1, w_in, sg_gain, sg_w, sg_b, ret_logit_f, ret_logit_b, w_out, norm2, w_gate, w_up, w_down, norm_f, loss_target, m_c_ctx, m_w_mod, m_b_mod, m_norm1, m_w_in, m_sg_gain, m_sg_w, m_sg_b, m_ret_logit_f, m_ret_logit_b, m_w_out, m_norm2, m_w_gate, m_w_up, m_w_down, m_norm_f, v_c_ctx, v_w_mod, v_b_mod, v_norm1, v_w_in, v_sg_gain, v_sg_w, v_sg_b, v_ret_logit_f, v_ret_logit_b, v_w_out, v_norm2, v_w_gate, v_w_up, v_w_down, v_norm_f):
    t_len = x.shape[1]
    tm = min(256, t_len)
    me = 4 * lax.axis_index("x") + 2 * lax.axis_index("y") + lax.axis_index("c")

    cctx_row = c_ctx.reshape(1, D)
    mod_all, cs_all = _mod_gather(c, cctx_row, w_mod[0], b_mod)
    mod6 = lax.dynamic_slice(mod_all, (me, 0), (1, 6 * D)).reshape(6, D)
    cmod6 = mod_all[8].reshape(6, D)

    g_in, g_out, g_gate, g_up, g_down = _weight_gather([w_in[0], w_out[0], w_gate[0], w_up[0], w_down[0]])
    win = _cols_from_blocks(g_in)
    wg = _cols_from_blocks(g_gate)
    wu = _cols_from_blocks(g_up)
    wout = g_out.reshape(D, D)
    wd = g_down.reshape(DFF, D)

    loss_dev, gx, d_wi, d_wo, d_wg, d_wu, d_wd, pack = _local_step(
        x[0], loss_target[0], ctx[0], mod6, cmod6, norm1, norm2[0:1], norm_f.reshape(1, D), win, wout, wg, wu, wd,
        sg_w[0], sg_b[0], sg_gain, ret_logit_f, ret_logit_b,
        tm_a=tm, tm_b=tm, tm_f=tm, tm_m=tm, tm_r=tm, tm_i=tm, bt_w=min(512, t_len))
    loss = lax.psum(loss_dev, ("x", "y", "c"))

    outs = {}

    def finish(name, g8, w, m, v, br=None):
        g = _reduce_scatter(g8, name="rs_" + name)
        d, mo, vo = _adam(w[0], g, m[0], v[0], name="adam_" + name, br=br)
        outs[name] = (g[None], d[None], mo[None], vo[None])

    finish("w_in", _blocks_from_cols(d_wi), w_in, m_w_in, v_w_in)
    finish("w_out", d_wo.reshape(NDEV, D // NDEV, D), w_out, m_w_out, v_w_out)
    finish("w_gate", _blocks_from_cols(d_wg), w_gate, m_w_gate, v_w_gate)
    finish("w_up", _blocks_from_cols(d_wu), w_up, m_w_up, v_w_up)
    finish("w_down", d_wd.reshape(NDEV, DFF // NDEV, D), w_down, m_w_down, v_w_down)

    rl_row = _pad_row(jnp.concatenate([ret_logit_f, ret_logit_b], axis=1))
    psum_rows, pall = _small_reduce(pack, rl_row)
    dmod_all = pall[:, 70:76, :].reshape(NDEV, 6 * D)
    dcmod = psum_rows[76:82].reshape(1, 6 * D)
    dmod_mat = jnp.concatenate([dmod_all, dcmod, jnp.zeros((7, 6 * D), F32)], axis=0)
    ncol = 6 * D // NDEV
    dmod_cols = lax.dynamic_slice(dmod_mat, (0, me * ncol), (16, ncol))
    g_wm, d_wm, m_wm, v_wm, part = _wmod_grad(cs_all, dmod_cols, w_mod[0], m_w_mod[0], v_w_mod[0])
    outs["w_mod"] = (g_wm[None], d_wm[None], m_wm[None], v_wm[None])
    g_cc, d_cc, m_cc, v_cc = _cctx_reduce(part, cctx_row, m_c_ctx.reshape(1, D), v_c_ctx.reshape(1, D))
    outs["c_ctx"] = tuple(a.reshape(D) for a in (g_cc, d_cc, m_cc, v_cc))

    g_small = jnp.concatenate([
        psum_rows[0:70],
        psum_rows[70:76] + psum_rows[76:82],
        jnp.zeros((PACK_ROWS - 76, D), F32)], axis=0)

    def pack_small(sgw_, n1_, n2_, nf_, gain_, sgb_, rlf_, rlb_, bm_):
        return jnp.concatenate([
            sgw_.reshape(64, D), n1_.reshape(1, D), n2_.reshape(1, D), nf_.reshape(1, D), _pad_row(gain_),
            _pad_row(sgb_), _pad_row(jnp.concatenate([rlf_.reshape(1, H), rlb_.reshape(1, H)], axis=1)),
            bm_.reshape(6, D), jnp.zeros((PACK_ROWS - 76, D), F32)], axis=0)

    w_s = pack_small(sg_w, norm1, norm2, norm_f, sg_gain, sg_b, ret_logit_f, ret_logit_b, b_mod)
    m_s = pack_small(m_sg_w, m_norm1, m_norm2, m_norm_f, m_sg_gain, m_sg_b, m_ret_logit_f, m_ret_logit_b, m_b_mod)
    v_s = pack_small(v_sg_w, v_norm1, v_norm2, v_norm_f, v_sg_gain, v_sg_b, v_ret_logit_f, v_ret_logit_b, v_b_mod)
    d_s, mo_s, vo_s = _adam(w_s, g_small, m_s, v_s, name="adam_small")

    def unpack_small(p):
        return {
            "sg_w": p[0:64].reshape(1, G, C, C), "norm1": p[64:65], "norm2": p[65:66], "norm_f": p[66],
            "sg_gain": p[67:68, :AW], "sg_b": p[68, :AW].reshape(1, G, C),
            "ret_logit_f": p[69:70, 0:H], "ret_logit_b": p[69:70, H:2 * H], "b_mod": p[70:76].reshape(1, 6 * D)}

    small = [unpack_small(p) for p in (g_small, d_s, mo_s, vo_s)]
    for name in small[0]:
        outs[name] = tuple(s[name] for s in small)

    order = ["c_ctx", "w_mod", "b_mod", "norm1", "w_in", "sg_gain", "sg_w", "sg_b", "ret_logit_f", "ret_logit_b",
             "w_out", "norm2", "w_gate", "w_up", "w_down", "norm_f"]
    res = [loss, gx[None]]
    for j in range(4):
        res += [outs[name][j] for name in order]
    return tuple(res)
```

```python
import functools
import math

import numpy as np
import jax
import jax.numpy as jnp
from jax import lax
from jax.experimental import pallas as pl
from jax.experimental.pallas import tpu as pltpu

F32 = jnp.float32
BF16 = jnp.bfloat16

D = 1024
AW = 512
RW = 512
H = 4
DH = 128
G = 4
C = 128
DFF = 2816
INC = 3584
KV_LO, KV_HI = 1536, 2560
NDEV = 8
EPS = 1e-6
KSCALE = DH ** -0.5
ROPE_BASE = 10000.0
GRID_W = 64

ADAM_LR = 0.001
ADAM_B1 = 0.9
ADAM_B2 = 0.999
ADAM_EPS = 1e-08
ADAM_WD = 0.01
ADAM_STEP = 10

VMEM_LIMIT = 56 * 1024 * 1024
MESH = pl.DeviceIdType.MESH
PACK_ROWS = 88


def _cparams(n_grid=0, **kw):
    sem = ("arbitrary",) * n_grid if n_grid else None
    return pltpu.CompilerParams(dimension_semantics=sem, vmem_limit_bytes=VMEM_LIMIT, **kw)


def _dot(a, b):
    return jnp.dot(a, b, preferred_element_type=F32)


def _dot_nt(a, b):
    return lax.dot_general(a, b, (((1,), (1,)), ((), ())), preferred_element_type=F32)


def _dot_tn(a, b):
    return lax.dot_general(a, b, (((0,), (0,)), ((), ())), preferred_element_type=F32)


def _whole(shape):
    nd = len(shape)
    return pl.BlockSpec(shape, lambda *_: (0,) * nd, pipeline_mode=pl.Buffered(1))


def _acc(shape):
    nd = len(shape)
    return pl.BlockSpec(shape, lambda *_: (0,) * nd)


def _rows(tm, n):
    return pl.BlockSpec((tm, n), lambda i: (i, 0))


def _rows_rev(tm, n, nt):
    return pl.BlockSpec((tm, n), lambda i: (nt - 1 - i, 0))


def _sigmoid(x):
    return 1.0 / (1.0 + jnp.exp(-x))


def _log_sigmoid(x):
    return jnp.minimum(x, 0.0) - jnp.log(1.0 + jnp.exp(-jnp.abs(x)))


_GK0 = math.sqrt(2.0 / math.pi)
_GK1 = 0.044715


def _gelu(x):
    x2 = x * x
    t = jnp.tanh(_GK0 * x * (1.0 + _GK1 * x2))
    g = 0.5 * x * (1.0 + t)
    dg = 0.5 * (1.0 + t) + 0.5 * x * (1.0 - t * t) * (_GK0 * (1.0 + 3.0 * _GK1 * x2))
    return g, dg


def _silu_parts(a):
    s = _sigmoid(a)
    return a * s, s * (1.0 + a * (1.0 - s))


def _rope(t, cos, sins):
    n = t.shape[1]
    lane = lax.broadcasted_iota(jnp.int32, t.shape, 1)
    first = (lane & 63) < 32
    partner = jnp.where(first, pltpu.roll(t, n - 32, 1), pltpu.roll(t, 32, 1))
    return t * cos + partner * sins


def _headnorm(o):
    outs, rs = [], []
    for h in range(H):
        oh = o[:, h * DH:(h + 1) * DH]
        r = lax.rsqrt(jnp.mean(oh * oh, axis=-1, keepdims=True) + EPS)
        outs.append(oh * r)
        rs.append(r)
    return jnp.concatenate(outs, axis=1), rs


def _decay_consts(lg, forward):
    ii = lax.broadcasted_iota(jnp.int32, (C, C), 0).astype(F32)
    jj = lax.broadcasted_iota(jnp.int32, (C, C), 1).astype(F32)
    ic = lax.broadcasted_iota(jnp.int32, (C, 1), 0).astype(F32)
    if forward:
        diff = ii - jj
        xi = jnp.exp(lg * (ic + 1.0))
        z = jnp.exp(lg * (C - 1.0 - ic))
    else:
        diff = jj - ii
        xi = jnp.exp(lg * (C - ic))
        z = jnp.exp(lg * ic)
    dm = jnp.where(diff >= 0.0, jnp.exp(lg * jnp.maximum(diff, 0.0)), 0.0)
    dec = jnp.exp(lg * float(C))
    return dm, xi, z, dec, ic


def _ctx_fwd(ctx, cmod6, norm1, win, rlf, rlb):
    lc = ctx.shape[0]

    def body(ctx_ref, cmod_ref, n1_ref, win_ref, rlf_ref, rlb_ref, hc_ref, kvc_ref, scf_ref, scb_ref):
        x = ctx_ref[...]
        r = lax.rsqrt(jnp.mean(x * x, axis=-1, keepdims=True) + EPS)
        hc = (x * r) * (n1_ref[...] * (1.0 + cmod_ref[1:2, :])) + cmod_ref[0:1, :]
        hcb = hc.astype(BF16)
        hc_ref[...] = hcb
        kv = _dot_nt(hcb, win_ref[KV_LO:KV_HI, :])
        k = kv[:, :RW] * KSCALE
        v = kv[:, RW:]
        kvc_ref[:, :RW] = k
        kvc_ref[:, RW:] = v
        t = lax.broadcasted_iota(jnp.int32, (lc, 1), 0).astype(F32)
        lgf = _log_sigmoid(rlf_ref[...])
        lgb = _log_sigmoid(rlb_ref[...])
        for h in range(H):
            hs = slice(h * DH, (h + 1) * DH)
            wf = jnp.exp(lgf[h:h + 1, 0:1] * (lc - 1.0 - t))
            wb = jnp.exp(lgb[h:h + 1, 0:1] * t)
            vh = v[:, hs].astype(BF16)
            scf_ref[h] = _dot_tn((k[:, hs] * wf).astype(BF16), vh)
            scb_ref[h] = _dot_tn((k[:, hs] * wb).astype(BF16), vh)

    return pl.pallas_call(
        body, name="ctx_fwd",
        out_shape=(jax.ShapeDtypeStruct((lc, D), BF16), jax.ShapeDtypeStruct((lc, 2 * RW), F32),
                   jax.ShapeDtypeStruct((H, DH, DH), F32), jax.ShapeDtypeStruct((H, DH, DH), F32)),
        compiler_params=_cparams(),
    )(ctx, cmod6, norm1, win, rlf, rlb)


def _sg_forward(u, v, sgw_ref, sgbt_ref, sgg_ref, nc):
    ua, dgu = _gelu(u)
    va, dgv = _gelu(v)
    gain = sgg_ref[...]
    mixed, vn_b, vah_l, rv_l = [], [], [], []
    for g in range(G):
        gs = slice(g * DH, (g + 1) * DH)
        vag = va[:, gs]
        rv = lax.rsqrt(jnp.mean(vag * vag, axis=-1, keepdims=True) + EPS)
        vah = vag * rv
        vn = (vah * gain[:, gs]).astype(BF16)
        wg = sgw_ref[g].astype(BF16)
        bcol = sgbt_ref[g]
        rows = [_dot(wg, vn[c * C:(c + 1) * C, :]) + bcol for c in range(nc)]
        mixed.append(jnp.concatenate(rows, axis=0) if nc > 1 else rows[0])
        vn_b.append(vn)
        vah_l.append(vah)
        rv_l.append(rv)
    mixed = jnp.concatenate(mixed, axis=1)
    return ua * mixed, (ua, dgu, dgv, mixed, vn_b, vah_l, rv_l)


def _fwd_a(x, mod6, norm1, win, sgw, sgbt, sggain, cos, sin, rlf, scf, *, tm):
    t_len = x.shape[0]
    nt, nc = t_len // tm, tm // C

    def body(x_ref, mod_ref, n1_ref, win_ref, sgw_ref, sgbt_ref, sgg_ref, cos_ref, sin_ref, rlf_ref, scf_ref,
             hx_ref, zuv_ref, q_ref, k_ref, v_ref, gfb_ref, of_ref, ya_ref, sst_ref, s_scr):
        i = pl.program_id(0)

        @pl.when(i == 0)
        def _():
            s_scr[...] = scf_ref[...]

        x = x_ref[...]
        r = lax.rsqrt(jnp.mean(x * x, axis=-1, keepdims=True) + EPS)
        hx = (x * r) * (n1_ref[...] * (1.0 + mod_ref[1:2, :])) + mod_ref[0:1, :]
        hxb = hx.astype(BF16)
        hx_ref[...] = hxb
        z = _dot_nt(hxb, win_ref[...])
        zuv_ref[...] = z[:, :2 * AW]
        gfb_ref[...] = z[:, 2560:3584]
        cos = jnp.tile(cos_ref[...], (1, H))
        sins = jnp.tile(sin_ref[...], (1, H))
        q_ref[...] = _rope(z[:, 1024:1536], cos, sins).astype(BF16)
        k_ref[...] = (_rope(z[:, 1536:2048], cos, sins) * KSCALE).astype(BF16)
        v_ref[...] = z[:, 2048:2560].astype(BF16)
        ya, _ = _sg_forward(z[:, :AW], z[:, AW:2 * AW], sgw_ref, sgbt_ref, sgg_ref, nc)
        ya_ref[...] = ya

        lg = _log_sigmoid(rlf_ref[...])
        for h in range(H):
            dm, xi, zc, dec, _ = _decay_consts(lg[h:h + 1, 0:1], True)
            hs = slice(h * DH, (h + 1) * DH)
            for c in range(nc):
                rs = slice(c * C, (c + 1) * C)
                qc, kc, vc = q_ref[rs, hs], k_ref[rs, hs], v_ref[rs, hs]
                s = s_scr[h]
                sst_ref[c, h] = s
                a = (_dot_nt(qc, kc) * dm).astype(BF16)
                of_ref[rs, hs] = _dot(a, vc) + xi * _dot(qc, s.astype(BF16))
                kz = (kc.astype(F32) * zc).astype(BF16)
                s_scr[h] = dec * s + _dot_tn(kz, vc)

    n_chunks = t_len // C
    return pl.pallas_call(
        body, name="fwd_a", grid=(nt,),
        in_specs=[_rows(tm, D), _whole((6, D)), _whole((1, D)), _whole((INC, D)), _whole((G, C, C)),
                  _whole((G, C, 128)), _whole((1, AW)), _rows(tm, DH), _rows(tm, DH), _whole((H, 128)),
                  _whole((H, DH, DH))],
        out_specs=[_rows(tm, D), _rows(tm, 2 * AW), _rows(tm, RW), _rows(tm, RW), _rows(tm, RW),
                   _rows(tm, 2 * RW), _rows(tm, RW), _rows(tm, AW),
                   pl.BlockSpec((nc, H, DH, DH), lambda i: (i, 0, 0, 0))],
        out_shape=(jax.ShapeDtypeStruct((t_len, D), BF16), jax.ShapeDtypeStruct((t_len, 2 * AW), F32),
                   jax.ShapeDtypeStruct((t_len, RW), BF16), jax.ShapeDtypeStruct((t_len, RW), BF16),
                   jax.ShapeDtypeStruct((t_len, RW), BF16), jax.ShapeDtypeStruct((t_len, 2 * RW), F32),
                   jax.ShapeDtypeStruct((t_len, RW), F32), jax.ShapeDtypeStruct((t_len, AW), F32),
                   jax.ShapeDtypeStruct((n_chunks, H, DH, DH), F32)),
        scratch_shapes=[pltpu.VMEM((H, DH, DH), F32)],
        compiler_params=_cparams(1),
    )(x, mod6, norm1, win, sgw, sgbt, sggain, cos, sin, rlf, scf)


def _fwd_b(q, k, v, of, gfb, ya, x, mod6, wout, rlb, scb, *, tm):
    t_len = x.shape[0]
    nt, nc = t_len // tm, tm // C

    def body(q_ref, k_ref, v_ref, of_ref, gfb_ref, ya_ref, x_ref, mod_ref, wout_ref, rlb_ref, scb_ref,
             ob_ref, ycat_ref, y_ref, x1_ref, rst_ref, r_scr):
        i = pl.program_id(0)

        @pl.when(i == 0)
        def _():
            r_scr[...] = scb_ref[...]

        lg = _log_sigmoid(rlb_ref[...])
        for h in range(H):
            dm, xi, zc, dec, _ = _decay_consts(lg[h:h + 1, 0:1], False)
            hs = slice(h * DH, (h + 1) * DH)
            for c in reversed(range(nc)):
                rs = slice(c * C, (c + 1) * C)
                qc, kc, vc = q_ref[rs, hs], k_ref[rs, hs], v_ref[rs, hs]
                s = r_scr[h]
                rst_ref[c, h] = s
                a = (_dot_nt(qc, kc) * dm).astype(BF16)
                ob_ref[rs, hs] = _dot(a, vc) + xi * _dot(qc, s.astype(BF16))
                kz = (kc.astype(F32) * zc).astype(BF16)
                r_scr[h] = dec * s + _dot_tn(kz, vc)

        gfb = gfb_ref[...]
        sf, _ = _silu_parts(gfb[:, :RW])
        sb, _ = _silu_parts(gfb[:, RW:])
        hnf, _ = _headnorm(of_ref[...])
        hnb, _ = _headnorm(ob_ref[...])
        yr = sf * hnf + sb * hnb
        ycat = jnp.concatenate([ya_ref[...], yr], axis=1).astype(BF16)
        ycat_ref[...] = ycat
        y = _dot(ycat, wout_ref[...])
        y_ref[...] = y.astype(BF16)
        x1_ref[...] = x_ref[...] + mod_ref[2:3, :] * y

    n_chunks = t_len // C
    rr = functools.partial(_rows_rev, nt=nt)
    return pl.pallas_call(
        body, name="fwd_b", grid=(nt,),
        in_specs=[rr(tm, RW), rr(tm, RW), rr(tm, RW), rr(tm, RW), rr(tm, 2 * RW), rr(tm, AW), rr(tm, D),
                  _whole((6, D)), _whole((D, D)), _whole((H, 128)), _whole((H, DH, DH))],
        out_specs=[rr(tm, RW), rr(tm, D), rr(tm, D), rr(tm, D),
                   pl.BlockSpec((nc, H, DH, DH), lambda i: (nt - 1 - i, 0, 0, 0))],
        out_shape=(jax.ShapeDtypeStruct((t_len, RW), F32), jax.ShapeDtypeStruct((t_len, D), BF16),
                   jax.ShapeDtypeStruct((t_len, D), BF16), jax.ShapeDtypeStruct((t_len, D), F32),
                   jax.ShapeDtypeStruct((n_chunks, H, DH, DH), F32)),
        scratch_shapes=[pltpu.VMEM((H, DH, DH), F32)],
        compiler_params=_cparams(1),
    )(q, k, v, of, gfb, ya, x, mod6, wout, rlb, scb)


def _ffn(x1, tgt, mod6, norm2, normf, wg, wu, wd, *, tm):
    t_len = x1.shape[0]
    nt = t_len // tm

    def body(x1_ref, tgt_ref, mod_ref, n2_ref, nf_ref, wg_ref, wu_ref, wd_ref,
             dx1_ref, h2_ref, da_ref, db_ref, hm_ref, df_ref, small_ref, loss_ref):
        i = pl.program_id(0)

        @pl.when(i == 0)
        def _():
            small_ref[...] = jnp.zeros_like(small_ref)

        sh2, sc2, g2 = mod_ref[3:4, :], mod_ref[4:5, :], mod_ref[5:6, :]
        n2, nf = n2_ref[...], nf_ref[...]
        x1 = x1_ref[...]
        r2 = lax.rsqrt(jnp.mean(x1 * x1, axis=-1, keepdims=True) + EPS)
        x1h = x1 * r2
        w2 = n2 * (1.0 + sc2)
        h2b = (x1h * w2 + sh2).astype(BF16)
        h2_ref[...] = h2b
        a = _dot_nt(h2b, wg_ref[...])
        b = _dot_nt(h2b, wu_ref[...])
        sa, dsa = _silu_parts(a)
        hmb = (sa * b).astype(BF16)
        hm_ref[...] = hmb
        f = _dot(hmb, wd_ref[...])
        x2 = x1 + g2 * f
        r3 = lax.rsqrt(jnp.mean(x2 * x2, axis=-1, keepdims=True) + EPS)
        x2h = x2 * r3
        diff = x2h * nf - tgt_ref[...]
        small_ref[5:6, :] += jnp.sum(diff * diff, axis=0, keepdims=True)
        dout = diff * (1.0 / D)
        small_ref[0:1, :] += jnp.sum(dout * x2h, axis=0, keepdims=True)
        dyg = dout * nf
        dx2 = r3 * (dyg - x2h * jnp.mean(dyg * x2h, axis=-1, keepdims=True))
        small_ref[1:2, :] += jnp.sum(dx2 * f, axis=0, keepdims=True)
        dfb = (dx2 * g2).astype(BF16)
        df_ref[...] = dfb
        dhm = _dot_nt(dfb, wd_ref[...])
        dbb = (dhm * sa).astype(BF16)
        dab = (dhm * b * dsa).astype(BF16)
        da_ref[...] = dab
        db_ref[...] = dbb
        dh2 = _dot(dab, wg_ref[...]) + _dot(dbb, wu_ref[...])
        small_ref[2:3, :] += jnp.sum(dh2, axis=0, keepdims=True)
        dhx = dh2 * x1h
        small_ref[3:4, :] += jnp.sum(dhx, axis=0, keepdims=True) * n2
        small_ref[4:5, :] += jnp.sum(dhx, axis=0, keepdims=True) * (1.0 + sc2)
        dyg2 = dh2 * w2
        dx1_ref[...] = dx2 + r2 * (dyg2 - x1h * jnp.mean(dyg2 * x1h, axis=-1, keepdims=True))

        @pl.when(i == nt - 1)
        def _():
            loss_ref[...] = jnp.full(loss_ref.shape, (0.5 / D) * jnp.sum(small_ref[5:6, :]), F32)

    return pl.pallas_call(
        body, name="ffn", grid=(nt,),
        in_specs=[_rows(tm, D), _rows(tm, D), _whole((6, D)), _whole((1, D)), _whole((1, D)),
                  _whole((DFF, D)), _whole((DFF, D)), _whole((DFF, D))],
        out_specs=[_rows(tm, D), _rows(tm, D), _rows(tm, DFF), _rows(tm, DFF), _rows(tm, DFF), _rows(tm, D),
                   _acc((8, D)), _acc((8, 128))],
        out_shape=(jax.ShapeDtypeStruct((t_len, D), F32), jax.ShapeDtypeStruct((t_len, D), BF16),
                   jax.ShapeDtypeStruct((t_len, DFF), BF16), jax.ShapeDtypeStruct((t_len, DFF), BF16),
                   jax.ShapeDtypeStruct((t_len, DFF), BF16), jax.ShapeDtypeStruct((t_len, D), BF16),
                   jax.ShapeDtypeStruct((8, D), F32), jax.ShapeDtypeStruct((8, 128), F32)),
        compiler_params=_cparams(1),
    )(x1, tgt, mod6, norm2, normf, wg, wu, wd)


def _mid_bwd(dx1, y, of, ob, gfb, mod6, wout, *, tm):
    t_len = dx1.shape[0]
    nt = t_len // tm

    def body(dx1_ref, y_ref, of_ref, ob_ref, gfb_ref, mod_ref, wout_ref,
             dy_ref, dya_ref, dgfb_ref, dof_ref, dob_ref, dg1_ref):
        i = pl.program_id(0)

        @pl.when(i == 0)
        def _():
            dg1_ref[...] = jnp.zeros_like(dg1_ref)

        dx1 = dx1_ref[...]
        dg1_ref[0:1, :] += jnp.sum(dx1 * y_ref[...].astype(F32), axis=0, keepdims=True)
        dyb = (dx1 * mod_ref[2:3, :]).astype(BF16)
        dy_ref[...] = dyb
        dycat = _dot_nt(dyb, wout_ref[...])
        dya_ref[...] = dycat[:, :AW]
        dyr = dycat[:, AW:]
        gfb = gfb_ref[...]
        for o_ref, do_ref, lo in ((of_ref, dof_ref, 0), (ob_ref, dob_ref, RW)):
            sg, dsg = _silu_parts(gfb[:, lo:lo + RW])
            o = o_ref[...]
            hn, rs = _headnorm(o)
            dgfb_ref[:, lo:lo + RW] = (dyr * hn * dsg).astype(BF16)
            dhn = dyr * sg
            for h in range(H):
                hs = slice(h * DH, (h + 1) * DH)
                oh, dh = hn[:, hs], dhn[:, hs]
                do_ref[:, hs] = (rs[h] * (dh - oh * jnp.mean(dh * oh, axis=-1, keepdims=True))).astype(BF16)

    return pl.pallas_call(
        body, name="mid_bwd", grid=(nt,),
        in_specs=[_rows(tm, D), _rows(tm, D), _rows(tm, RW), _rows(tm, RW), _rows(tm, 2 * RW),
                  _whole((6, D)), _whole((D, D))],
        out_specs=[_rows(tm, D), _rows(tm, AW), _rows(tm, 2 * RW), _rows(tm, RW), _rows(tm, RW), _acc((8, D))],
        out_shape=(jax.ShapeDtypeStruct((t_len, D), BF16), jax.ShapeDtypeStruct((t_len, AW), F32),
                   jax.ShapeDtypeStruct((t_len, 2 * RW), BF16), jax.ShapeDtypeStruct((t_len, RW), BF16),
                   jax.ShapeDtypeStruct((t_len, RW), BF16), jax.ShapeDtypeStruct((8, D), F32)),
        compiler_params=_cparams(1),
    )(dx1, y, of, ob, gfb, mod6, wout)


def _ret_bwd_dir(q_ref, k_ref, v_ref, do_ref, st_ref, dq_ref, dk_ref, dv_ref, ds_scr, dlg_scr, lg, forward, nc, row0):
    order = reversed(range(nc)) if forward else range(nc)
    order = list(order)
    for h in range(H):
        dm, xi, zc, dec, ic = _decay_consts(lg[h:h + 1, 0:1], forward)
        hs = slice(h * DH, (h + 1) * DH)
        if forward:
            w_qi, w_qc, w_ki, w_ks = ic, ic + 1.0, -ic, (C - 1.0) - ic
        else:
            w_qi, w_qc, w_ki, w_ks = -ic, C - ic, ic, ic
        acc = jnp.zeros((1, DH), F32)
        for c in order:
            rs = slice(c * C, (c + 1) * C)
            qc, kc, vc, doc = q_ref[rs, hs], k_ref[rs, hs], v_ref[rs, hs], do_ref[rs, hs]
            s = st_ref[c, h]
            dsn = ds_scr[h]
            sb, dsnb = s.astype(BF16), dsn.astype(BF16)
            qf, kf = qc.astype(F32), kc.astype(F32)
            a = (_dot_nt(qc, kc) * dm).astype(BF16)
            da = (_dot_nt(doc, vc) * dm).astype(BF16)
            xdo = (xi * doc.astype(F32)).astype(BF16)
            kz = (kf * zc).astype(BF16)
            vz = (vc.astype(F32) * zc).astype(BF16)
            dq_i = _dot(da, kc)
            dq_c = _dot_nt(xdo, sb)
            dk_i = _dot_tn(da, qc)
            dk_s = _dot_nt(vz, dsnb)
            dq_ref[rs, hs] = dq_i + dq_c
            dk_ref[rs, hs] = dk_i + dk_s
            dv_ref[rs, hs] = _dot_tn(a, doc) + _dot(kz, dsnb)
            rowterm = qf * (w_qi * dq_i + w_qc * dq_c) + kf * (w_ki * dk_i + w_ks * dk_s)
            acc = acc + jnp.sum(rowterm, axis=0, keepdims=True)
            acc = acc + (float(C) * dec) * jnp.sum(s * dsn, axis=0, keepdims=True)
            ds_scr[h] = _dot_tn((xi * qf).astype(BF16), doc) + dec * dsn
        dlg_scr[row0 + h:row0 + h + 1, :] += acc


def _ret_bwd(q, k, v, dof, dob, sst, rst, rlf, rlb, *, tm):
    t_len = q.shape[0]
    nt, nc = t_len // tm, tm // C

    def body(qf_ref, kf_ref, vf_ref, dof_ref, sst_ref, qb_ref, kb_ref, vb_ref, dob_ref, rst_ref, rlf_ref, rlb_ref,
             dqf_ref, dkf_ref, dvf_ref, dqb_ref, dkb_ref, dvb_ref, dscf_ref, dscb_ref, dlg_ref,
             dsf_scr, dsb_scr, dlg_scr):
        i = pl.program_id(0)

        @pl.when(i == 0)
        def _():
            dsf_scr[...] = jnp.zeros_like(dsf_scr)
            dsb_scr[...] = jnp.zeros_like(dsb_scr)
            dlg_scr[...] = jnp.zeros_like(dlg_scr)

        lgf = _log_sigmoid(rlf_ref[...])
        lgb = _log_sigmoid(rlb_ref[...])
        _ret_bwd_dir(qf_ref, kf_ref, vf_ref, dof_ref, sst_ref, dqf_ref, dkf_ref, dvf_ref, dsf_scr, dlg_scr, lgf, True, nc, 0)
        _ret_bwd_dir(qb_ref, kb_ref, vb_ref, dob_ref, rst_ref, dqb_ref, dkb_ref, dvb_ref, dsb_scr, dlg_scr, lgb, False, nc, H)

        @pl.when(i == nt - 1)
        def _():
            dscf_ref[...] = dsf_scr[...]
            dscb_ref[...] = dsb_scr[...]
            dlg_ref[...] = jnp.broadcast_to(jnp.sum(dlg_scr[...], axis=1, keepdims=True), dlg_ref.shape)

    rr = functools.partial(_rows_rev, nt=nt)
    st_f = pl.BlockSpec((nc, H, DH, DH), lambda i: (nt - 1 - i, 0, 0, 0))
    st_b = pl.BlockSpec((nc, H, DH, DH), lambda i: (i, 0, 0, 0))
    tok = jax.ShapeDtypeStruct((t_len, RW), F32)
    st = jax.ShapeDtypeStruct((H, DH, DH), F32)
    return pl.pallas_call(
        body, name="ret_bwd", grid=(nt,),
        in_specs=[rr(tm, RW), rr(tm, RW), rr(tm, RW), rr(tm, RW), st_f,
                  _rows(tm, RW), _rows(tm, RW), _rows(tm, RW), _rows(tm, RW), st_b,
                  _whole((H, 128)), _whole((H, 128))],
        out_specs=[rr(tm, RW), rr(tm, RW), rr(tm, RW), _rows(tm, RW), _rows(tm, RW), _rows(tm, RW),
                   _acc((H, DH, DH)), _acc((H, DH, DH)), _acc((8, 128))],
        out_shape=(tok, tok, tok, tok, tok, tok, st, st, jax.ShapeDtypeStruct((8, 128), F32)),
        scratch_shapes=[pltpu.VMEM((H, DH, DH), F32), pltpu.VMEM((H, DH, DH), F32), pltpu.VMEM((8, 128), F32)],
        compiler_params=_cparams(1),
    )(q, k, v, dof, sst, q, k, v, dob, rst, rlf, rlb)


def _in_bwd(x, zuv, dya, dqf, dkf, dvf, dqb, dkb, dvb, dgfb, dx1, cos, sin, mod6, norm1, win, sgw, sgbt, sggain, *, tm):
    t_len = x.shape[0]
    nt, nc = t_len // tm, tm // C

    def body(x_ref, zuv_ref, dya_ref, dqf_ref, dkf_ref, dvf_ref, dqb_ref, dkb_ref, dvb_ref, dgfb_ref, dx1_ref,
             cos_ref, sin_ref, mod_ref, n1_ref, win_ref, sgw_ref, sgbt_ref, sgg_ref,
             gx_ref, dz_ref, small_ref, dsgw_ref, dsgbt_ref):
        i = pl.program_id(0)

        @pl.when(i == 0)
        def _():
            small_ref[...] = jnp.zeros_like(small_ref)
            dsgw_ref[...] = jnp.zeros_like(dsgw_ref)
            dsgbt_ref[...] = jnp.zeros_like(dsgbt_ref)

        zuv = zuv_ref[...]
        _, (ua, dgu, dgv, mixed, vn_b, vah_l, rv_l) = _sg_forward(zuv[:, :AW], zuv[:, AW:], sgw_ref, sgbt_ref, sgg_ref, nc)
        dya = dya_ref[...]
        dz_ref[:, 0:AW] = (dya * mixed * dgu).astype(BF16)
        dmixed = dya * ua
        gain = sgg_ref[...]
        for g in range(G):
            gs = slice(g * DH, (g + 1) * DH)
            dmg = dmixed[:, gs]
            dmb = dmg.astype(BF16)
            wgt = sgw_ref[g].astype(BF16)
            dsw = jnp.zeros((C, C), F32)
            dsb = jnp.zeros((C, DH), F32)
            rows = []
            for c in range(nc):
                rs = slice(c * C, (c + 1) * C)
                dsw = dsw + _dot_nt(dmb[rs, :], vn_b[g][rs, :])
                dsb = dsb + dmg[rs, :]
                rows.append(_dot_tn(wgt, dmb[rs, :]))
            dsgw_ref[g] += dsw
            dsgbt_ref[g] += dsb
            dvn = jnp.concatenate(rows, axis=0) if nc > 1 else rows[0]
            vah, rv = vah_l[g], rv_l[g]
            small_ref[3:4, gs] += jnp.sum(dvn * vah, axis=0, keepdims=True)
            dyg = dvn * gain[:, gs]
            dva = rv * (dyg - vah * jnp.mean(dyg * vah, axis=-1, keepdims=True))
            dz_ref[:, AW + g * DH:AW + (g + 1) * DH] = (dva * dgv[:, gs]).astype(BF16)

        cos = jnp.tile(cos_ref[...], (1, H))
        nsins = -jnp.tile(sin_ref[...], (1, H))
        dz_ref[:, 1024:1536] = _rope(dqf_ref[...] + dqb_ref[...], cos, nsins).astype(BF16)
        dz_ref[:, 1536:2048] = (_rope(dkf_ref[...] + dkb_ref[...], cos, nsins) * KSCALE).astype(BF16)
        dz_ref[:, 2048:2560] = (dvf_ref[...] + dvb_ref[...]).astype(BF16)
        dz_ref[:, 2560:3584] = dgfb_ref[...]

        dhx = _dot(dz_ref[...], win_ref[...])
        x = x_ref[...]
        r = lax.rsqrt(jnp.mean(x * x, axis=-1, keepdims=True) + EPS)
        xh = x * r
        n1, sc1 = n1_ref[...], mod_ref[1:2, :]
        small_ref[0:1, :] += jnp.sum(dhx, axis=0, keepdims=True)
        dhxx = jnp.sum(dhx * xh, axis=0, keepdims=True)
        small_ref[1:2, :] += dhxx * n1
        small_ref[2:3, :] += dhxx * (1.0 + sc1)
        dyg = dhx * (n1 * (1.0 + sc1))
        gx_ref[...] = dx1_ref[...] + r * (dyg - xh * jnp.mean(dyg * xh, axis=-1, keepdims=True))

        @pl.when(i == nt - 1)
        def _():
            for g in range(G):
                dsgbt_ref[g] = jnp.broadcast_to(jnp.sum(dsgbt_ref[g], axis=1, keepdims=True), (C, DH))

    tok = functools.partial(_rows, tm)
    return pl.pallas_call(
        body, name="in_bwd", grid=(nt,),
        in_specs=[tok(D), tok(2 * AW), tok(AW), tok(RW), tok(RW), tok(RW), tok(RW), tok(RW), tok(RW),
                  tok(2 * RW), tok(D), tok(DH), tok(DH), _whole((6, D)), _whole((1, D)), _whole((INC, D)),
                  _whole((G, C, C)), _whole((G, C, 128)), _whole((1, AW))],
        out_specs=[tok(D), tok(INC), _acc((8, D)), _acc((G, C, C)), _acc((G, C, 128))],
        out_shape=(jax.ShapeDtypeStruct((t_len, D), F32), jax.ShapeDtypeStruct((t_len, INC), BF16),
                   jax.ShapeDtypeStruct((8, D), F32), jax.ShapeDtypeStruct((G, C, C), F32),
                   jax.ShapeDtypeStruct((G, C, 128), F32)),
        compiler_params=_cparams(1),
    )(x, zuv, dya, dqf, dkf, dvf, dqb, dkb, dvb, dgfb, dx1, cos, sin, mod6, norm1, win, sgw, sgbt, sggain)


def _tn_matmul(a, b, *, bm, bt, name, init=None, init_rows=None):
    t_len, m = a.shape
    n = b.shape[1]
    ni, ntt = m // bm, t_len // bt
    has_init = init is not None
    assert not has_init or bm == m

    def body(*refs):
        if has_init:
            a_ref, b_ref, init_ref, o_ref = refs
        else:
            a_ref, b_ref, o_ref = refs
        t = pl.program_id(1)

        @pl.when(t == 0)
        def _():
            o_ref[...] = jnp.zeros_like(o_ref)
            if has_init:
                o_ref[init_rows[0]:init_rows[1], :] = init_ref[...]

        o_ref[...] += _dot_tn(a_ref[...], b_ref[...])

    in_specs = [pl.BlockSpec((bt, bm), lambda i, t: (t, i)), pl.BlockSpec((bt, n), lambda i, t: (t, 0))]
    args = [a, b]
    if has_init:
        in_specs.append(pl.BlockSpec(init.shape, lambda i, t: (0, 0), pipeline_mode=pl.Buffered(1)))
        args.append(init)
    return pl.pallas_call(
        body, name=name, grid=(ni, ntt),
        in_specs=in_specs,
        out_specs=pl.BlockSpec((bm, n), lambda i, t: (i, 0)),
        out_shape=jax.ShapeDtypeStruct((m, n), F32),
        compiler_params=_cparams(2),
    )(*args)


def _ctx_bwd(ctx, hc, kvc, cmod6, norm1, win, rlf, rlb, dscf, dscb):
    lc = ctx.shape[0]

    def body(ctx_ref, hc_ref, kvc_ref, cmod_ref, n1_ref, win_ref, rlf_ref, rlb_ref, dscf_ref, dscb_ref,
             dwin_ref, small_ref, dlg_ref):
        k = kvc_ref[:, :RW]
        v = kvc_ref[:, RW:]
        t = lax.broadcasted_iota(jnp.int32, (lc, 1), 0).astype(F32)
        lgf = _log_sigmoid(rlf_ref[...])
        lgb = _log_sigmoid(rlb_ref[...])
        dks, dvs = [], []
        dlg_ref[...] = jnp.zeros_like(dlg_ref)
        for h in range(H):
            hs = slice(h * DH, (h + 1) * DH)
            wf = jnp.exp(lgf[h:h + 1, 0:1] * (lc - 1.0 - t))
            wb = jnp.exp(lgb[h:h + 1, 0:1] * t)
            kh, vhb = k[:, hs], v[:, hs].astype(BF16)
            dsf, dsb = dscf_ref[h].astype(BF16), dscb_ref[h].astype(BF16)
            dkf = wf * _dot_nt(vhb, dsf)
            dkb = wb * _dot_nt(vhb, dsb)
            dvs.append(_dot((kh * wf).astype(BF16), dsf) + _dot((kh * wb).astype(BF16), dsb))
            dks.append((dkf + dkb) * KSCALE)
            lf = jnp.sum((lc - 1.0 - t) * kh * dkf, axis=0, keepdims=True)
            lb = jnp.sum(t * kh * dkb, axis=0, keepdims=True)
            dlg_ref[h:h + 1, :] = jnp.broadcast_to(jnp.sum(lf, axis=1, keepdims=True), (1, 128))
            dlg_ref[H + h:H + h + 1, :] = jnp.broadcast_to(jnp.sum(lb, axis=1, keepdims=True), (1, 128))
        dkv = jnp.concatenate(dks + dvs, axis=1).astype(BF16)
        dhc = _dot(dkv, win_ref[KV_LO:KV_HI, :])
        dwin_ref[...] = _dot_tn(dkv, hc_ref[...])
        x = ctx_ref[...]
        r = lax.rsqrt(jnp.mean(x * x, axis=-1, keepdims=True) + EPS)
        xh = x * r
        small_ref[...] = jnp.zeros_like(small_ref)
        small_ref[0:1, :] = jnp.sum(dhc, axis=0, keepdims=True)
        dhxx = jnp.sum(dhc * xh, axis=0, keepdims=True)
        small_ref[1:2, :] = dhxx * n1_ref[...]
        small_ref[2:3, :] = dhxx * (1.0 + cmod_ref[1:2, :])

    return pl.pallas_call(
        body, name="ctx_bwd",
        out_shape=(jax.ShapeDtypeStruct((D, 2 * RW), F32), jax.ShapeDtypeStruct((8, D), F32),
                   jax.ShapeDtypeStruct((8, 128), F32)),
        compiler_params=_cparams(),
    )(ctx, hc, kvc, cmod6, norm1, win, rlf, rlb, dscf, dscb)


def _adam_math(w, g, m, v):
    m = ADAM_B1 * m + (1.0 - ADAM_B1) * g
    v = ADAM_B2 * v + (1.0 - ADAM_B2) * (g * g)
    m_hat = m / (1.0 - ADAM_B1 ** ADAM_STEP)
    v_hat = v / (1.0 - ADAM_B2 ** ADAM_STEP)
    delta = -ADAM_LR * (m_hat / (jnp.sqrt(v_hat) + ADAM_EPS) + ADAM_WD * w)
    return delta, m, v


def _adam(w, g, m, v, *, name, br=None):
    r, c = w.shape
    br = r if br is None else br

    def body(w_ref, g_ref, m_ref, v_ref, d_ref, mo_ref, vo_ref):
        d_ref[...], mo_ref[...], vo_ref[...] = _adam_math(w_ref[...], g_ref[...], m_ref[...], v_ref[...])

    spec = pl.BlockSpec((br, c), lambda i: (i, 0))
    sh = jax.ShapeDtypeStruct((r, c), F32)
    return pl.pallas_call(
        body, name=name, grid=(r // br,), in_specs=[spec] * 4, out_specs=[spec] * 3, out_shape=(sh, sh, sh),
        compiler_params=_cparams(1),
    )(w, g, m, v)


def _place():
    x, y, c = lax.axis_index("x"), lax.axis_index("y"), lax.axis_index("c")
    return x, y, c, 4 * x + 2 * y + c


def _flip(x, y, c, k):
    px = 1 - x if (k >> 2) & 1 else x
    py = 1 - y if (k >> 1) & 1 else y
    pc = 1 - c if k & 1 else c
    return (px, py, pc), 4 * px + 2 * py + pc


def _gather_direct(buf_ref, send_sems, recv_sems, sem0=0):
    x, y, c, me = _place()
    sends = []
    for k in range(1, NDEV):
        dev, _ = _flip(x, y, c, k)
        cp = pltpu.make_async_remote_copy(
            src_ref=buf_ref.at[me], dst_ref=buf_ref.at[me],
            send_sem=send_sems.at[sem0 + k - 1], recv_sem=recv_sems.at[sem0 + k - 1],
            device_id=dev, device_id_type=MESH)
        cp.start()
        sends.append(cp)
    for k in range(1, NDEV):
        dev, idx = _flip(x, y, c, k)
        pltpu.make_async_remote_copy(
            src_ref=buf_ref.at[idx], dst_ref=buf_ref.at[idx],
            send_sem=send_sems.at[sem0 + k - 1], recv_sem=recv_sems.at[sem0 + k - 1],
            device_id=dev, device_id_type=MESH).wait_recv()
    for cp in sends:
        cp.wait_send()


def _mod_gather(c_row, cctx_row, wmod, bmod):
    ncol = wmod.shape[1]

    def body(c_ref, cctx_ref, wmod_ref, bmod_ref, mod_ref, cs_ref, cbuf, pbuf, send_sems, recv_sems):
        _, _, _, me = _place()
        cbuf[me] = jnp.broadcast_to(c_ref[...], (8, D))
        _gather_direct(cbuf, send_sems, recv_sems, 0)
        row = lax.broadcasted_iota(jnp.int32, (16, D), 0)
        call = jnp.where(row == 8, jnp.broadcast_to(cctx_ref[...], (16, D)), 0.0)
        for d in range(NDEV):
            call = jnp.where(row == d, jnp.concatenate([cbuf[d], cbuf[d]], axis=0), call)
        cs = call * _sigmoid(call)
        cs_ref[...] = cs
        pbuf[me] = _dot(cs.astype(BF16), wmod_ref[...].astype(BF16))
        _gather_direct(pbuf, send_sems, recv_sems, NDEV - 1)
        for d in range(NDEV):
            mod_ref[:, d * ncol:(d + 1) * ncol] = pbuf[d] + bmod_ref[:, d * ncol:(d + 1) * ncol]

    return pl.pallas_call(
        body, name="mod_gather",
        out_shape=(jax.ShapeDtypeStruct((16, 6 * D), F32), jax.ShapeDtypeStruct((16, D), F32)),
        scratch_shapes=[pltpu.VMEM((NDEV, 8, D), F32), pltpu.VMEM((NDEV, 16, ncol), F32),
                        pltpu.SemaphoreType.DMA((2 * (NDEV - 1),)), pltpu.SemaphoreType.DMA((2 * (NDEV - 1),))],
        compiler_params=_cparams(),
    )(c_row, cctx_row, wmod, bmod)


def _weight_gather(shards):
    n = len(shards)

    def body(*refs):
        in_refs, out_refs = refs[:n], refs[n:2 * n]
        cast_refs = refs[2 * n:3 * n]
        send_sems, recv_sems, local_sems = refs[3 * n:]
        x, y, c, me = _place()
        sib = (x, y, 1 - c)
        chips = [(1 - x, y), (x, 1 - y), (1 - x, 1 - y)]

        def blk(px, py, pc):
            return 4 * px + 2 * py + pc

        def copy(w, k, block, to, src=None):
            dst = out_refs[w].at[block]
            return pltpu.make_async_remote_copy(
                src_ref=dst if src is None else src, dst_ref=dst,
                send_sem=send_sems.at[w, k], recv_sem=recv_sems.at[w, k], device_id=to, device_id_type=MESH)

        first, passed, mine = [], [], []
        for w in range(n):
            cast_refs[w][...] = in_refs[w][...].astype(BF16)
            m = pltpu.make_async_copy(cast_refs[w], out_refs[w].at[me], local_sems.at[w])
            m.start()
            mine.append(m)
            cps = [copy(w, 0, me, sib, src=cast_refs[w])]
            cps += [copy(w, 1 + j, me, (*chip, c), src=cast_refs[w]) for j, chip in enumerate(chips)]
            for cp in cps:
                cp.start()
            first += cps
        for w in range(n):
            for j, chip in enumerate(chips):
                b = blk(*chip, c)
                copy(w, 1 + j, b, (x, y, c)).wait_recv()
                fw = copy(w, 4 + j, b, sib)
                fw.start()
                passed.append(fw)
        for w in range(n):
            copy(w, 0, blk(x, y, 1 - c), (x, y, c)).wait_recv()
            for j, chip in enumerate(chips):
                copy(w, 4 + j, blk(*chip, 1 - c), (x, y, c)).wait_recv()
        for cp in first + passed:
            cp.wait_send()
        for m in mine:
            m.wait()

    vm = pl.BlockSpec(memory_space=pltpu.VMEM)
    hbm = pl.BlockSpec(memory_space=pltpu.HBM)
    return pl.pallas_call(
        body, name="weight_gather",
        in_specs=[vm] * n, out_specs=[hbm] * n,
        out_shape=tuple(jax.ShapeDtypeStruct((NDEV,) + s.shape, BF16) for s in shards),
        scratch_shapes=[pltpu.VMEM(s.shape, BF16) for s in shards]
        + [pltpu.SemaphoreType.DMA((n, 7)), pltpu.SemaphoreType.DMA((n, 7)), pltpu.SemaphoreType.DMA((n,))],
        compiler_params=_cparams(),
    )(*shards)


def _reduce_scatter(g8, *, name):
    _, r, cdim = g8.shape

    def body(g_ref, out_ref, s1, r1, s2, r2, ss1, rs1, ss2, rs2):
        x, y, c, me = _place()
        sib = (x, y, 1 - c)
        sends = []
        for j in range(4):
            px, py = j >> 1, j & 1
            s1[j] = g_ref[4 * px + 2 * py + (1 - c)].astype(BF16)
            cp = pltpu.make_async_remote_copy(src_ref=s1.at[j], dst_ref=r1.at[j], send_sem=ss1.at[j],
                                              recv_sem=rs1.at[j], device_id=sib, device_id_type=MESH)
            cp.start()
            sends.append(cp)
        for j in range(4):
            pltpu.make_async_remote_copy(src_ref=s1.at[j], dst_ref=r1.at[j], send_sem=ss1.at[j],
                                         recv_sem=rs1.at[j], device_id=sib, device_id_type=MESH).wait_recv()
        flips = [(1, 0), (0, 1), (1, 1)]
        for k, (dx, dy) in enumerate(flips):
            px = 1 - x if dx else x
            py = 1 - y if dy else y
            s2[k] = (g_ref[4 * px + 2 * py + c] + r1[2 * px + py].astype(F32)).astype(BF16)
            cp = pltpu.make_async_remote_copy(src_ref=s2.at[k], dst_ref=r2.at[k], send_sem=ss2.at[k],
                                              recv_sem=rs2.at[k], device_id=(px, py, c), device_id_type=MESH)
            cp.start()
            sends.append(cp)
        acc = g_ref[me] + r1[2 * x + y].astype(F32)
        for k, (dx, dy) in enumerate(flips):
            px = 1 - x if dx else x
            py = 1 - y if dy else y
            pltpu.make_async_remote_copy(src_ref=s2.at[k], dst_ref=r2.at[k], send_sem=ss2.at[k],
                                         recv_sem=rs2.at[k], device_id=(px, py, c), device_id_type=MESH).wait_recv()
            acc = acc + r2[k].astype(F32)
        out_ref[...] = acc
        for cp in sends:
            cp.wait_send()

    return pl.pallas_call(
        body, name=name,
        out_shape=jax.ShapeDtypeStruct((r, cdim), F32),
        scratch_shapes=[pltpu.VMEM((4, r, cdim), BF16), pltpu.VMEM((4, r, cdim), BF16),
                        pltpu.VMEM((3, r, cdim), BF16), pltpu.VMEM((3, r, cdim), BF16),
                        pltpu.SemaphoreType.DMA((4,)), pltpu.SemaphoreType.DMA((4,)),
                        pltpu.SemaphoreType.DMA((3,)), pltpu.SemaphoreType.DMA((3,))],
        compiler_params=_cparams(),
    )(g8)


def _small_reduce(pack, rl_row):
    def body(pack_ref, rl_ref, sum_ref, all_ref, send_sems, recv_sems):
        _, _, _, me = _place()
        all_ref[me] = pack_ref[...]
        _gather_direct(all_ref, send_sems, recv_sems, 0)
        acc = all_ref[0]
        for d in range(1, NDEV):
            acc = acc + all_ref[d]
        sum_ref[...] = acc
        sum_ref[69:70, :] = acc[69:70, :] * _sigmoid(-rl_ref[...])

    return pl.pallas_call(
        body, name="small_reduce",
        out_shape=(jax.ShapeDtypeStruct((PACK_ROWS, D), F32), jax.ShapeDtypeStruct((NDEV, PACK_ROWS, D), F32)),
        scratch_shapes=[pltpu.SemaphoreType.DMA((NDEV - 1,)), pltpu.SemaphoreType.DMA((NDEV - 1,))],
        compiler_params=_cparams(),
    )(pack, rl_row)


def _wmod_grad(cs_all, dmod_cols, wmod, m, v):
    ncol = wmod.shape[1]

    def body(cs_ref, dm_ref, w_ref, m_ref, v_ref, g_ref, d_ref, mo_ref, vo_ref, part_ref):
        dmb = dm_ref[...].astype(BF16)
        g = _dot_tn(cs_ref[...].astype(BF16), dmb)
        g_ref[...] = g
        d_ref[...], mo_ref[...], vo_ref[...] = _adam_math(w_ref[...], g, m_ref[...], v_ref[...])
        part_ref[...] = _dot_nt(dmb, w_ref[...].astype(BF16))

    sh = jax.ShapeDtypeStruct((D, ncol), F32)
    return pl.pallas_call(
        body, name="wmod_grad", out_shape=(sh, sh, sh, sh, jax.ShapeDtypeStruct((16, D), F32)),
        compiler_params=_cparams(),
    )(cs_all, dmod_cols, wmod, m, v)


def _cctx_reduce(part, cctx_row, m_row, v_row):
    def body(part_ref, c_ref, m_ref, v_ref, g_ref, d_ref, mo_ref, vo_ref, buf, send_sems, recv_sems):
        _, _, _, me = _place()
        buf[me] = part_ref[8:16, :]
        _gather_direct(buf, send_sems, recv_sems, 0)
        acc = buf[0]
        for d in range(1, NDEV):
            acc = acc + buf[d]
        cc = c_ref[...]
        _, dsilu = _silu_parts(cc)
        g = acc[0:1, :] * dsilu
        g_ref[...] = g
        d_ref[...], mo_ref[...], vo_ref[...] = _adam_math(cc, g, m_ref[...], v_ref[...])

    sh = jax.ShapeDtypeStruct((1, D), F32)
    return pl.pallas_call(
        body, name="cctx_reduce", out_shape=(sh, sh, sh, sh),
        scratch_shapes=[pltpu.VMEM((NDEV, 8, D), F32),
                        pltpu.SemaphoreType.DMA((NDEV - 1,)), pltpu.SemaphoreType.DMA((NDEV - 1,))],
        compiler_params=_cparams(),
    )(part, cctx_row, m_row, v_row)


def _rope_tables(t_len):
    n_freq = DH // 4
    inv = (ROPE_BASE ** (-np.arange(n_freq, dtype=np.float32) / n_freq)).astype(np.float32)
    tok = np.arange(t_len)
    rows = (tok // GRID_W).astype(np.float32)
    cols = (tok % GRID_W).astype(np.float32)
    ang_r = rows[:, None] * inv[None, :]
    ang_c = cols[:, None] * inv[None, :]
    cos = np.concatenate([np.cos(ang_r)] * 2 + [np.cos(ang_c)] * 2, axis=1).astype(np.float32)
    sin = np.concatenate([-np.sin(ang_r), np.sin(ang_r), -np.sin(ang_c), np.sin(ang_c)], axis=1).astype(np.float32)
    return jnp.asarray(cos), jnp.asarray(sin)


def _pad_row(v, width=D):
    v = v.reshape(1, -1)
    return jnp.pad(v, ((0, 0), (0, width - v.shape[1])))


def _local_step(x, tgt, ctx, mod6, cmod6, norm1, norm2, normf, win, wout, wg, wu, wd, sgw, sgb, sggain, rlf, rlb,
                *, tm_a, tm_b, tm_f, tm_m, tm_r, tm_i, bt_w):
    t_len = x.shape[0]
    cos, sin = _rope_tables(t_len)
    sgbt = jnp.broadcast_to(sgb[:, :, None], (G, C, 128))
    rlf = jnp.broadcast_to(rlf.reshape(H, 1), (H, 128))
    rlb = jnp.broadcast_to(rlb.reshape(H, 1), (H, 128))
    hc, kvc, scf, scb = _ctx_fwd(ctx, cmod6, norm1, win, rlf, rlb)
    hx, zuv, q, k, v, gfb, of, ya, sst = _fwd_a(x, mod6, norm1, win, sgw, sgbt, sggain, cos, sin, rlf, scf, tm=tm_a)
    ob, ycat, y, x1, rst = _fwd_b(q, k, v, of, gfb, ya, x, mod6, wout, rlb, scb, tm=tm_b)
    dx1, h2, da, db, hm, df, small_f, loss = _ffn(x1, tgt, mod6, norm2, normf, wg, wu, wd, tm=tm_f)
    dy, dya, dgfb, dof, dob, dg1 = _mid_bwd(dx1, y, of, ob, gfb, mod6, wout, tm=tm_m)
    dqf, dkf, dvf, dqb, dkb, dvb, dscf, dscb, dlg = _ret_bwd(q, k, v, dof, dob, sst, rst, rlf, rlb, tm=tm_r)
    gx, dz, small_i, dsgw, dsgbt = _in_bwd(x, zuv, dya, dqf, dkf, dvf, dqb, dkb, dvb, dgfb, dx1, cos, sin,
                                           mod6, norm1, win, sgw, sgbt, sggain, tm=tm_i)
    dwin_c, small_c, dlg_c = _ctx_bwd(ctx, hc, kvc, cmod6, norm1, win, rlf, rlb, dscf, dscb)
    d_wd = _tn_matmul(hm, df, bm=DFF, bt=bt_w, name="dw_down")
    d_wg = _tn_matmul(da, h2, bm=DFF, bt=bt_w, name="dw_gate")
    d_wu = _tn_matmul(db, h2, bm=DFF, bt=bt_w, name="dw_up")
    d_wo = _tn_matmul(ycat, dy, bm=D, bt=bt_w, name="dw_out")
    d_wi = _tn_matmul(dz, hx, bm=INC, bt=min(512, bt_w), name="dw_in", init=dwin_c, init_rows=(KV_LO, KV_HI))
    dlg8 = (dlg[:, 0] + dlg_c[:, 0]).reshape(1, 8)
    zero_row = jnp.zeros((1, D), F32)
    pack = jnp.concatenate([
        dsgw.reshape(64, D),
        small_i[2:3] + small_c[2:3],
        small_f[4:5],
        small_f[0:1],
        small_i[3:4],
        _pad_row(dsgbt[:, :, 0]),
        _pad_row(dlg8),
        small_i[0:1], small_i[1:2], dg1[0:1], small_f[2:3], small_f[3:4], small_f[1:2],
        small_c[0:1], small_c[1:2], zero_row, zero_row, zero_row, zero_row,
        _pad_row(loss[0:1, :]),
        jnp.zeros((PACK_ROWS - 83, D), F32)], axis=0)
    return gx, d_wi, d_wo, d_wg, d_wu, d_wd, pack


def kernel(x, c, ctx, c_ctx, w_mod, b_mod, norm1, w_in, sg_gain, sg_w, sg_b, ret_logit_f, ret_logit_b, w_out, norm2, w_gate, w_up, w_down, norm_f, loss_target, m_c_ctx, m_w_mod, m_b_mod, m_norm1, m_w_in, m_sg_gain, m_sg_w, m_sg_b, m_ret_logit_f, m_ret_logit_b, m_w_out, m_norm2, m_w_gate, m_w_up, m_w_down, m_norm_f, v_c_ctx, v_w_mod, v_b_mod, v_norm1, v_w_in, v_sg_gain, v_sg_w, v_sg_b, v_ret_logit_f, v_ret_logit_b, v_w_out, v_norm2, v_w_gate, v_w_up, v_w_down, v_norm_f):
    t_len = x.shape[1]
    tm = min(512, t_len)
    me = 4 * lax.axis_index("x") + 2 * lax.axis_index("y") + lax.axis_index("c")
    tr = lambda a: jnp.transpose(a[0])

    cctx_row = c_ctx.reshape(1, D)
    mod_all, cs_all = _mod_gather(c, cctx_row, w_mod[0], b_mod)
    mod6 = lax.dynamic_slice(mod_all, (me, 0), (1, 6 * D)).reshape(6, D)
    cmod6 = mod_all[8].reshape(6, D)

    g_in, g_out, g_gate, g_up, g_down = _weight_gather([tr(w_in), w_out[0], tr(w_gate), tr(w_up), w_down[0]])
    win_t = g_in.reshape(INC, D)
    wg_t = g_gate.reshape(DFF, D)
    wu_t = g_up.reshape(DFF, D)
    wout = g_out.reshape(D, D)
    wd = g_down.reshape(DFF, D)

    gx, d_wi, d_wo, d_wg, d_wu, d_wd, pack = _local_step(
        x[0], loss_target[0], ctx[0], mod6, cmod6, norm1, norm2[0:1], norm_f.reshape(1, D), win_t, wout, wg_t, wu_t, wd,
        sg_w[0], sg_b[0], sg_gain, ret_logit_f, ret_logit_b,
        tm_a=tm, tm_b=tm, tm_f=min(256, t_len), tm_m=tm, tm_r=tm, tm_i=tm, bt_w=min(1024, t_len))

    outs = {}

    def finish(name, g_full, w, m, v, transposed):
        rows = g_full.shape[0] // NDEV
        g = _reduce_scatter(g_full.reshape(NDEV, rows, D), name="rs_" + name)
        take = tr if transposed else (lambda a: a[0])
        d, mo, vo = _adam(take(w), g, take(m), take(v), name="adam_" + name)
        put = (lambda a: jnp.transpose(a)[None]) if transposed else (lambda a: a[None])
        outs[name] = tuple(put(a) for a in (g, d, mo, vo))

    finish("w_in", d_wi, w_in, m_w_in, v_w_in, True)
    finish("w_out", d_wo, w_out, m_w_out, v_w_out, False)
    finish("w_gate", d_wg, w_gate, m_w_gate, v_w_gate, True)
    finish("w_up", d_wu, w_up, m_w_up, v_w_up, True)
    finish("w_down", d_wd, w_down, m_w_down, v_w_down, False)

    rl_row = _pad_row(jnp.concatenate([ret_logit_f, ret_logit_b], axis=1))
    psum_rows, pall = _small_reduce(pack, rl_row)
    dmod_all = pall[:, 70:76, :].reshape(NDEV, 6 * D)
    dcmod = psum_rows[76:82].reshape(1, 6 * D)
    dmod_mat = jnp.concatenate([dmod_all, dcmod, jnp.zeros((7, 6 * D), F32)], axis=0)
    ncol = 6 * D // NDEV
    dmod_cols = lax.dynamic_slice(dmod_mat, (0, me * ncol), (16, ncol))
    g_wm, d_wm, m_wm, v_wm, part = _wmod_grad(cs_all, dmod_cols, w_mod[0], m_w_mod[0], v_w_mod[0])
    outs["w_mod"] = (g_wm[None], d_wm[None], m_wm[None], v_wm[None])
    g_cc, d_cc, m_cc, v_cc = _cctx_reduce(part, cctx_row, m_c_ctx.reshape(1, D), v_c_ctx.reshape(1, D))
    outs["c_ctx"] = tuple(a.reshape(D) for a in (g_cc, d_cc, m_cc, v_cc))

    g_small = jnp.concatenate([
        psum_rows[0:70],
        psum_rows[70:76] + psum_rows[76:82],
        jnp.zeros((PACK_ROWS - 76, D), F32)], axis=0)

    def pack_small(sgw_, n1_, n2_, nf_, gain_, sgb_, rlf_, rlb_, bm_):
        return jnp.concatenate([
            sgw_.reshape(64, D), n1_.reshape(1, D), n2_.reshape(1, D), nf_.reshape(1, D), _pad_row(gain_),
            _pad_row(sgb_), _pad_row(jnp.concatenate([rlf_.reshape(1, H), rlb_.reshape(1, H)], axis=1)),
            bm_.reshape(6, D), jnp.zeros((PACK_ROWS - 76, D), F32)], axis=0)

    w_s = pack_small(sg_w, norm1, norm2, norm_f, sg_gain, sg_b, ret_logit_f, ret_logit_b, b_mod)
    m_s = pack_small(m_sg_w, m_norm1, m_norm2, m_norm_f, m_sg_gain, m_sg_b, m_ret_logit_f, m_ret_logit_b, m_b_mod)
    v_s = pack_small(v_sg_w, v_norm1, v_norm2, v_norm_f, v_sg_gain, v_sg_b, v_ret_logit_f, v_ret_logit_b, v_b_mod)
    d_s, mo_s, vo_s = _adam(w_s, g_small, m_s, v_s, name="adam_small")

    def unpack_small(p):
        return {
            "sg_w": p[0:64].reshape(1, G, C, C), "norm1": p[64:65], "norm2": p[65:66], "norm_f": p[66],
            "sg_gain": p[67:68, :AW], "sg_b": p[68, :AW].reshape(1, G, C),
            "ret_logit_f": p[69:70, 0:H], "ret_logit_b": p[69:70, H:2 * H], "b_mod": p[70:76].reshape(1, 6 * D)}

    small = [unpack_small(p) for p in (g_small, d_s, mo_s, vo_s)]
    for name in small[0]:
        outs[name] = tuple(s[name] for s in small)

    order = ["c_ctx", "w_mod", "b_mod", "norm1", "w_in", "sg_gain", "sg_w", "sg_b", "ret_logit_f", "ret_logit_b",
             "w_out", "norm2", "w_gate", "w_up", "w_down", "norm_f"]
    res = [psum_rows[82, 0], gx[None]]
    for j in range(4):
        res += [outs[name][j] for name in order]
    return tuple(res)
```

```python
import functools
import math

import numpy as np
import jax
import jax.numpy as jnp
from jax import lax
from jax.experimental import pallas as pl
from jax.experimental.pallas import tpu as pltpu

F32 = jnp.float32
BF16 = jnp.bfloat16

D = 1024
AW = 512
RW = 512
H = 4
DH = 128
G = 4
C = 128
DFF = 2816
INC = 3584
KV_LO, KV_HI = 1536, 2560
NDEV = 8
EPS = 1e-6
KSCALE = DH ** -0.5
ROPE_BASE = 10000.0
GRID_W = 64

ADAM_LR = 0.001
ADAM_B1 = 0.9
ADAM_B2 = 0.999
ADAM_EPS = 1e-08
ADAM_WD = 0.01
ADAM_STEP = 10

VMEM_LIMIT = 56 * 1024 * 1024
MESH = pl.DeviceIdType.MESH


def _cparams(n_grid=0, **kw):
    sem = ("arbitrary",) * n_grid if n_grid else None
    return pltpu.CompilerParams(dimension_semantics=sem, vmem_limit_bytes=VMEM_LIMIT, **kw)


def _dot(a, b):
    return jnp.dot(a, b, preferred_element_type=F32)


def _dot_nt(a, b):
    return lax.dot_general(a, b, (((1,), (1,)), ((), ())), preferred_element_type=F32)


def _dot_tn(a, b):
    return lax.dot_general(a, b, (((0,), (0,)), ((), ())), preferred_element_type=F32)


def _whole(shape):
    nd = len(shape)
    return pl.BlockSpec(shape, lambda *_: (0,) * nd, pipeline_mode=pl.Buffered(1))


def _acc(shape):
    nd = len(shape)
    return pl.BlockSpec(shape, lambda *_: (0,) * nd)


def _rows(tm, n):
    return pl.BlockSpec((tm, n), lambda i: (i, 0))


def _rows_rev(tm, n, nt):
    return pl.BlockSpec((tm, n), lambda i: (nt - 1 - i, 0))


def _sigmoid(x):
    return 1.0 / (1.0 + jnp.exp(-x))


def _log_sigmoid(x):
    return jnp.minimum(x, 0.0) - jnp.log(1.0 + jnp.exp(-jnp.abs(x)))


_GK0 = math.sqrt(2.0 / math.pi)
_GK1 = 0.044715


def _gelu(x):
    x2 = x * x
    t = jnp.tanh(_GK0 * x * (1.0 + _GK1 * x2))
    g = 0.5 * x * (1.0 + t)
    dg = 0.5 * (1.0 + t) + 0.5 * x * (1.0 - t * t) * (_GK0 * (1.0 + 3.0 * _GK1 * x2))
    return g, dg


def _silu_parts(a):
    s = _sigmoid(a)
    return a * s, s * (1.0 + a * (1.0 - s))


def _rope(t, cos, sins):
    n = t.shape[1]
    lane = lax.broadcasted_iota(jnp.int32, t.shape, 1)
    first = (lane & 63) < 32
    partner = jnp.where(first, pltpu.roll(t, n - 32, 1), pltpu.roll(t, 32, 1))
    return t * cos + partner * sins


def _headnorm(o):
    outs, rs = [], []
    for h in range(H):
        oh = o[:, h * DH:(h + 1) * DH]
        r = lax.rsqrt(jnp.mean(oh * oh, axis=-1, keepdims=True) + EPS)
        outs.append(oh * r)
        rs.append(r)
    return jnp.concatenate(outs, axis=1), rs


def _decay_consts(lg, forward):
    ii = lax.broadcasted_iota(jnp.int32, (C, C), 0).astype(F32)
    jj = lax.broadcasted_iota(jnp.int32, (C, C), 1).astype(F32)
    ic = lax.broadcasted_iota(jnp.int32, (C, 1), 0).astype(F32)
    if forward:
        diff = ii - jj
        xi = jnp.exp(lg * (ic + 1.0))
        z = jnp.exp(lg * (C - 1.0 - ic))
    else:
        diff = jj - ii
        xi = jnp.exp(lg * (C - ic))
        z = jnp.exp(lg * ic)
    dm = jnp.where(diff >= 0.0, jnp.exp(lg * jnp.maximum(diff, 0.0)), 0.0)
    dec = jnp.exp(lg * float(C))
    return dm, xi, z, dec, ic


def _ctx_fwd(ctx, cmod6, norm1, win, rlf, rlb):
    lc = ctx.shape[0]

    def body(ctx_ref, cmod_ref, n1_ref, win_ref, rlf_ref, rlb_ref, hc_ref, kvc_ref, scf_ref, scb_ref):
        x = ctx_ref[...]
        r = lax.rsqrt(jnp.mean(x * x, axis=-1, keepdims=True) + EPS)
        hc = (x * r) * (n1_ref[...] * (1.0 + cmod_ref[1:2, :])) + cmod_ref[0:1, :]
        hcb = hc.astype(BF16)
        hc_ref[...] = hcb
        kv = _dot_nt(hcb, win_ref[KV_LO:KV_HI, :])
        k = kv[:, :RW] * KSCALE
        v = kv[:, RW:]
        kvc_ref[:, :RW] = k
        kvc_ref[:, RW:] = v
        t = lax.broadcasted_iota(jnp.int32, (lc, 1), 0).astype(F32)
        lgf = _log_sigmoid(rlf_ref[...])
        lgb = _log_sigmoid(rlb_ref[...])
        for h in range(H):
            hs = slice(h * DH, (h + 1) * DH)
            wf = jnp.exp(lgf[h:h + 1, 0:1] * (lc - 1.0 - t))
            wb = jnp.exp(lgb[h:h + 1, 0:1] * t)
            vh = v[:, hs].astype(BF16)
            scf_ref[h] = _dot_tn((k[:, hs] * wf).astype(BF16), vh)
            scb_ref[h] = _dot_tn((k[:, hs] * wb).astype(BF16), vh)

    return pl.pallas_call(
        body, name="ctx_fwd",
        out_shape=(jax.ShapeDtypeStruct((lc, D), BF16), jax.ShapeDtypeStruct((lc, 2 * RW), F32),
                   jax.ShapeDtypeStruct((H, DH, DH), F32), jax.ShapeDtypeStruct((H, DH, DH), F32)),
        compiler_params=_cparams(),
    )(ctx, cmod6, norm1, win, rlf, rlb)


def _sg_forward(u, v, sgw_ref, sgbt_ref, sgg_ref, nc):
    ua, dgu = _gelu(u)
    va, dgv = _gelu(v)
    gain = sgg_ref[...]
    mixed, vn_b, vah_l, rv_l = [], [], [], []
    for g in range(G):
        gs = slice(g * DH, (g + 1) * DH)
        vag = va[:, gs]
        rv = lax.rsqrt(jnp.mean(vag * vag, axis=-1, keepdims=True) + EPS)
        vah = vag * rv
        vn = (vah * gain[:, gs]).astype(BF16)
        wg = sgw_ref[g].astype(BF16)
        bcol = sgbt_ref[g]
        rows = [_dot(wg, vn[c * C:(c + 1) * C, :]) + bcol for c in range(nc)]
        mixed.append(jnp.concatenate(rows, axis=0) if nc > 1 else rows[0])
        vn_b.append(vn)
        vah_l.append(vah)
        rv_l.append(rv)
    mixed = jnp.concatenate(mixed, axis=1)
    return ua * mixed, (ua, dgu, dgv, mixed, vn_b, vah_l, rv_l)


def _fwd_a(x, mod6, norm1, win, sgw, sgbt, sggain, cos, sin, rlf, scf, *, tm):
    t_len = x.shape[0]
    nt, nc = t_len // tm, tm // C

    def body(x_ref, mod_ref, n1_ref, win_ref, sgw_ref, sgbt_ref, sgg_ref, cos_ref, sin_ref, rlf_ref, scf_ref,
             hx_ref, zuv_ref, q_ref, k_ref, v_ref, gfb_ref, of_ref, ya_ref, sst_ref, s_scr):
        i = pl.program_id(0)

        @pl.when(i == 0)
        def _():
            s_scr[...] = scf_ref[...]

        x = x_ref[...]
        r = lax.rsqrt(jnp.mean(x * x, axis=-1, keepdims=True) + EPS)
        hx = (x * r) * (n1_ref[...] * (1.0 + mod_ref[1:2, :])) + mod_ref[0:1, :]
        hxb = hx.astype(BF16)
        hx_ref[...] = hxb
        z = _dot_nt(hxb, win_ref[...])
        zuv_ref[...] = z[:, :2 * AW]
        gfb_ref[...] = z[:, 2560:3584]
        cos = jnp.tile(cos_ref[...], (1, H))
        sins = jnp.tile(sin_ref[...], (1, H))
        q_ref[...] = _rope(z[:, 1024:1536], cos, sins).astype(BF16)
        k_ref[...] = (_rope(z[:, 1536:2048], cos, sins) * KSCALE).astype(BF16)
        v_ref[...] = z[:, 2048:2560].astype(BF16)
        ya, _ = _sg_forward(z[:, :AW], z[:, AW:2 * AW], sgw_ref, sgbt_ref, sgg_ref, nc)
        ya_ref[...] = ya

        lg = _log_sigmoid(rlf_ref[...])
        for h in range(H):
            dm, xi, zc, dec, _ = _decay_consts(lg[h:h + 1, 0:1], True)
            hs = slice(h * DH, (h + 1) * DH)
            for c in range(nc):
                rs = slice(c * C, (c + 1) * C)
                qc, kc, vc = q_ref[rs, hs], k_ref[rs, hs], v_ref[rs, hs]
                s = s_scr[h]
                sst_ref[c, h] = s
                a = (_dot_nt(qc, kc) * dm).astype(BF16)
                of_ref[rs, hs] = _dot(a, vc) + xi * _dot(qc, s.astype(BF16))
                kz = (kc.astype(F32) * zc).astype(BF16)
                s_scr[h] = dec * s + _dot_tn(kz, vc)

    n_chunks = t_len // C
    return pl.pallas_call(
        body, name="fwd_a", grid=(nt,),
        in_specs=[_rows(tm, D), _whole((6, D)), _whole((1, D)), _whole((INC, D)), _whole((G, C, C)),
                  _whole((G, C, 128)), _whole((1, AW)), _rows(tm, DH), _rows(tm, DH), _whole((H, 128)),
                  _whole((H, DH, DH))],
        out_specs=[_rows(tm, D), _rows(tm, 2 * AW), _rows(tm, RW), _rows(tm, RW), _rows(tm, RW),
                   _rows(tm, 2 * RW), _rows(tm, RW), _rows(tm, AW),
                   pl.BlockSpec((nc, H, DH, DH), lambda i: (i, 0, 0, 0))],
        out_shape=(jax.ShapeDtypeStruct((t_len, D), BF16), jax.ShapeDtypeStruct((t_len, 2 * AW), F32),
                   jax.ShapeDtypeStruct((t_len, RW), BF16), jax.ShapeDtypeStruct((t_len, RW), BF16),
                   jax.ShapeDtypeStruct((t_len, RW), BF16), jax.ShapeDtypeStruct((t_len, 2 * RW), F32),
                   jax.ShapeDtypeStruct((t_len, RW), F32), jax.ShapeDtypeStruct((t_len, AW), F32),
                   jax.ShapeDtypeStruct((n_chunks, H, DH, DH), F32)),
        scratch_shapes=[pltpu.VMEM((H, DH, DH), F32)],
        compiler_params=_cparams(1),
    )(x, mod6, norm1, win, sgw, sgbt, sggain, cos, sin, rlf, scf)


def _fwd_b(q, k, v, of, gfb, ya, x, mod6, wout, rlb, scb, *, tm):
    t_len = x.shape[0]
    nt, nc = t_len // tm, tm // C

    def body(q_ref, k_ref, v_ref, of_ref, gfb_ref, ya_ref, x_ref, mod_ref, wout_ref, rlb_ref, scb_ref,
             ob_ref, ycat_ref, y_ref, x1_ref, rst_ref, r_scr):
        i = pl.program_id(0)

        @pl.when(i == 0)
        def _():
            r_scr[...] = scb_ref[...]

        lg = _log_sigmoid(rlb_ref[...])
        for h in range(H):
            dm, xi, zc, dec, _ = _decay_consts(lg[h:h + 1, 0:1], False)
            hs = slice(h * DH, (h + 1) * DH)
            for c in reversed(range(nc)):
                rs = slice(c * C, (c + 1) * C)
                qc, kc, vc = q_ref[rs, hs], k_ref[rs, hs], v_ref[rs, hs]
                s = r_scr[h]
                rst_ref[c, h] = s
                a = (_dot_nt(qc, kc) * dm).astype(BF16)
                ob_ref[rs, hs] = _dot(a, vc) + xi * _dot(qc, s.astype(BF16))
                kz = (kc.astype(F32) * zc).astype(BF16)
                r_scr[h] = dec * s + _dot_tn(kz, vc)

        gfb = gfb_ref[...]
        sf, _ = _silu_parts(gfb[:, :RW])
        sb, _ = _silu_parts(gfb[:, RW:])
        hnf, _ = _headnorm(of_ref[...])
        hnb, _ = _headnorm(ob_ref[...])
        yr = sf * hnf + sb * hnb
        ycat = jnp.concatenate([ya_ref[...], yr], axis=1).astype(BF16)
        ycat_ref[...] = ycat
        y = _dot(ycat, wout_ref[...])
        y_ref[...] = y.astype(BF16)
        x1_ref[...] = x_ref[...] + mod_ref[2:3, :] * y

    n_chunks = t_len // C
    rr = functools.partial(_rows_rev, nt=nt)
    return pl.pallas_call(
        body, name="fwd_b", grid=(nt,),
        in_specs=[rr(tm, RW), rr(tm, RW), rr(tm, RW), rr(tm, RW), rr(tm, 2 * RW), rr(tm, AW), rr(tm, D),
                  _whole((6, D)), _whole((D, D)), _whole((H, 128)), _whole((H, DH, DH))],
        out_specs=[rr(tm, RW), rr(tm, D), rr(tm, D), rr(tm, D),
                   pl.BlockSpec((nc, H, DH, DH), lambda i: (nt - 1 - i, 0, 0, 0))],
        out_shape=(jax.ShapeDtypeStruct((t_len, RW), F32), jax.ShapeDtypeStruct((t_len, D), BF16),
                   jax.ShapeDtypeStruct((t_len, D), BF16), jax.ShapeDtypeStruct((t_len, D), F32),
                   jax.ShapeDtypeStruct((n_chunks, H, DH, DH), F32)),
        scratch_shapes=[pltpu.VMEM((H, DH, DH), F32)],
        compiler_params=_cparams(1),
    )(q, k, v, of, gfb, ya, x, mod6, wout, rlb, scb)


def _ffn(x1, tgt, mod6, norm2, normf, wg, wu, wd, *, tm):
    t_len = x1.shape[0]
    nt = t_len // tm

    def body(x1_ref, tgt_ref, mod_ref, n2_ref, nf_ref, wg_ref, wu_ref, wd_ref,
             dx1_ref, h2_ref, da_ref, db_ref, hm_ref, df_ref, small_ref):
        i = pl.program_id(0)

        @pl.when(i == 0)
        def _():
            small_ref[...] = jnp.zeros_like(small_ref)

        sh2, sc2, g2 = mod_ref[3:4, :], mod_ref[4:5, :], mod_ref[5:6, :]
        n2, nf = n2_ref[...], nf_ref[...]
        x1 = x1_ref[...]
        r2 = lax.rsqrt(jnp.mean(x1 * x1, axis=-1, keepdims=True) + EPS)
        x1h = x1 * r2
        w2 = n2 * (1.0 + sc2)
        h2b = (x1h * w2 + sh2).astype(BF16)
        h2_ref[...] = h2b
        a = _dot_nt(h2b, wg_ref[...])
        b = _dot_nt(h2b, wu_ref[...])
        sa, dsa = _silu_parts(a)
        hmb = (sa * b).astype(BF16)
        hm_ref[...] = hmb
        f = _dot(hmb, wd_ref[...])
        x2 = x1 + g2 * f
        r3 = lax.rsqrt(jnp.mean(x2 * x2, axis=-1, keepdims=True) + EPS)
        x2h = x2 * r3
        diff = x2h * nf - tgt_ref[...]
        small_ref[5:6, :] += jnp.sum(diff * diff, axis=0, keepdims=True)
        dout = diff * (1.0 / D)
        small_ref[0:1, :] += jnp.sum(dout * x2h, axis=0, keepdims=True)
        dyg = dout * nf
        dx2 = r3 * (dyg - x2h * jnp.mean(dyg * x2h, axis=-1, keepdims=True))
        small_ref[1:2, :] += jnp.sum(dx2 * f, axis=0, keepdims=True)
        dfb = (dx2 * g2).astype(BF16)
        df_ref[...] = dfb
        dhm = _dot_nt(dfb, wd_ref[...])
        dbb = (dhm * sa).astype(BF16)
        dab = (dhm * b * dsa).astype(BF16)
        da_ref[...] = dab
        db_ref[...] = dbb
        dh2 = _dot(dab, wg_ref[...]) + _dot(dbb, wu_ref[...])
        small_ref[2:3, :] += jnp.sum(dh2, axis=0, keepdims=True)
        dhx = dh2 * x1h
        small_ref[3:4, :] += jnp.sum(dhx, axis=0, keepdims=True) * n2
        small_ref[4:5, :] += jnp.sum(dhx, axis=0, keepdims=True) * (1.0 + sc2)
        dyg2 = dh2 * w2
        dx1_ref[...] = dx2 + r2 * (dyg2 - x1h * jnp.mean(dyg2 * x1h, axis=-1, keepdims=True))

    return pl.pallas_call(
        body, name="ffn", grid=(nt,),
        in_specs=[_rows(tm, D), _rows(tm, D), _whole((6, D)), _whole((1, D)), _whole((1, D)),
                  _whole((DFF, D)), _whole((DFF, D)), _whole((DFF, D))],
        out_specs=[_rows(tm, D), _rows(tm, D), _rows(tm, DFF), _rows(tm, DFF), _rows(tm, DFF), _rows(tm, D),
                   _acc((8, D))],
        out_shape=(jax.ShapeDtypeStruct((t_len, D), F32), jax.ShapeDtypeStruct((t_len, D), BF16),
                   jax.ShapeDtypeStruct((t_len, DFF), BF16), jax.ShapeDtypeStruct((t_len, DFF), BF16),
                   jax.ShapeDtypeStruct((t_len, DFF), BF16), jax.ShapeDtypeStruct((t_len, D), BF16),
                   jax.ShapeDtypeStruct((8, D), F32)),
        compiler_params=_cparams(1),
    )(x1, tgt, mod6, norm2, normf, wg, wu, wd)


def _mid_bwd(dx1, y, of, ob, gfb, mod6, wout, *, tm):
    t_len = dx1.shape[0]
    nt = t_len // tm

    def body(dx1_ref, y_ref, of_ref, ob_ref, gfb_ref, mod_ref, wout_ref,
             dy_ref, dya_ref, dgfb_ref, dof_ref, dob_ref, dg1_ref):
        i = pl.program_id(0)

        @pl.when(i == 0)
        def _():
            dg1_ref[...] = jnp.zeros_like(dg1_ref)

        dx1 = dx1_ref[...]
        dg1_ref[0:1, :] += jnp.sum(dx1 * y_ref[...].astype(F32), axis=0, keepdims=True)
        dyb = (dx1 * mod_ref[2:3, :]).astype(BF16)
        dy_ref[...] = dyb
        dycat = _dot_nt(dyb, wout_ref[...])
        dya_ref[...] = dycat[:, :AW]
        dyr = dycat[:, AW:]
        gfb = gfb_ref[...]
        for o_ref, do_ref, lo in ((of_ref, dof_ref, 0), (ob_ref, dob_ref, RW)):
            sg, dsg = _silu_parts(gfb[:, lo:lo + RW])
            o = o_ref[...]
            hn, rs = _headnorm(o)
            dgfb_ref[:, lo:lo + RW] = (dyr * hn * dsg).astype(BF16)
            dhn = dyr * sg
            for h in range(H):
                hs = slice(h * DH, (h + 1) * DH)
                oh, dh = hn[:, hs], dhn[:, hs]
                do_ref[:, hs] = (rs[h] * (dh - oh * jnp.mean(dh * oh, axis=-1, keepdims=True))).astype(BF16)

    return pl.pallas_call(
        body, name="mid_bwd", grid=(nt,),
        in_specs=[_rows(tm, D), _rows(tm, D), _rows(tm, RW), _rows(tm, RW), _rows(tm, 2 * RW),
                  _whole((6, D)), _whole((D, D))],
        out_specs=[_rows(tm, D), _rows(tm, AW), _rows(tm, 2 * RW), _rows(tm, RW), _rows(tm, RW), _acc((8, D))],
        out_shape=(jax.ShapeDtypeStruct((t_len, D), BF16), jax.ShapeDtypeStruct((t_len, AW), F32),
                   jax.ShapeDtypeStruct((t_len, 2 * RW), BF16), jax.ShapeDtypeStruct((t_len, RW), BF16),
                   jax.ShapeDtypeStruct((t_len, RW), BF16), jax.ShapeDtypeStruct((8, D), F32)),
        compiler_params=_cparams(1),
    )(dx1, y, of, ob, gfb, mod6, wout)


def _ret_bwd_dir(q_ref, k_ref, v_ref, do_ref, st_ref, dq_ref, dk_ref, dv_ref, ds_scr, dlg_scr, lg, forward, nc, row0):
    order = reversed(range(nc)) if forward else range(nc)
    order = list(order)
    for h in range(H):
        dm, xi, zc, dec, ic = _decay_consts(lg[h:h + 1, 0:1], forward)
        hs = slice(h * DH, (h + 1) * DH)
        if forward:
            w_qi, w_qc, w_ki, w_ks = ic, ic + 1.0, -ic, (C - 1.0) - ic
        else:
            w_qi, w_qc, w_ki, w_ks = -ic, C - ic, ic, ic
        acc = jnp.zeros((1, DH), F32)
        for c in order:
            rs = slice(c * C, (c + 1) * C)
            qc, kc, vc, doc = q_ref[rs, hs], k_ref[rs, hs], v_ref[rs, hs], do_ref[rs, hs]
            s = st_ref[c, h]
            dsn = ds_scr[h]
            sb, dsnb = s.astype(BF16), dsn.astype(BF16)
            qf, kf = qc.astype(F32), kc.astype(F32)
            a = (_dot_nt(qc, kc) * dm).astype(BF16)
            da = (_dot_nt(doc, vc) * dm).astype(BF16)
            xdo = (xi * doc.astype(F32)).astype(BF16)
            kz = (kf * zc).astype(BF16)
            vz = (vc.astype(F32) * zc).astype(BF16)
            dq_i = _dot(da, kc)
            dq_c = _dot_nt(xdo, sb)
            dk_i = _dot_tn(da, qc)
            dk_s = _dot_nt(vz, dsnb)
            dq_ref[rs, hs] = dq_i + dq_c
            dk_ref[rs, hs] = dk_i + dk_s
            dv_ref[rs, hs] = _dot_tn(a, doc) + _dot(kz, dsnb)
            rowterm = qf * (w_qi * dq_i + w_qc * dq_c) + kf * (w_ki * dk_i + w_ks * dk_s)
            acc = acc + jnp.sum(rowterm, axis=0, keepdims=True)
            acc = acc + (float(C) * dec) * jnp.sum(s * dsn, axis=0, keepdims=True)
            ds_scr[h] = _dot_tn((xi * qf).astype(BF16), doc) + dec * dsn
        dlg_scr[row0 + h:row0 + h + 1, :] += acc


def _ret_bwd(q, k, v, dof, dob, sst, rst, rlf, rlb, *, tm):
    t_len = q.shape[0]
    nt, nc = t_len // tm, tm // C

    def body(qf_ref, kf_ref, vf_ref, dof_ref, sst_ref, qb_ref, kb_ref, vb_ref, dob_ref, rst_ref, rlf_ref, rlb_ref,
             dqf_ref, dkf_ref, dvf_ref, dqb_ref, dkb_ref, dvb_ref, dscf_ref, dscb_ref, dlg_ref,
             dsf_scr, dsb_scr, dlg_scr):
        i = pl.program_id(0)

        @pl.when(i == 0)
        def _():
            dsf_scr[...] = jnp.zeros_like(dsf_scr)
            dsb_scr[...] = jnp.zeros_like(dsb_scr)
            dlg_scr[...] = jnp.zeros_like(dlg_scr)

        lgf = _log_sigmoid(rlf_ref[...])
        lgb = _log_sigmoid(rlb_ref[...])
        _ret_bwd_dir(qf_ref, kf_ref, vf_ref, dof_ref, sst_ref, dqf_ref, dkf_ref, dvf_ref, dsf_scr, dlg_scr, lgf, True, nc, 0)
        _ret_bwd_dir(qb_ref, kb_ref, vb_ref, dob_ref, rst_ref, dqb_ref, dkb_ref, dvb_ref, dsb_scr, dlg_scr, lgb, False, nc, H)

        @pl.when(i == nt - 1)
        def _():
            dscf_ref[...] = dsf_scr[...]
            dscb_ref[...] = dsb_scr[...]
            tot = jnp.broadcast_to(jnp.sum(dlg_scr[...], axis=1, keepdims=True), (8, 128))
            ri = lax.broadcasted_iota(jnp.int32, (8, 128), 0)
            li = lax.broadcasted_iota(jnp.int32, (8, 128), 1)
            dlg_ref[...] = jnp.zeros_like(dlg_ref)
            dlg_ref[0:1, 0:128] = jnp.sum(jnp.where(ri == li, tot, 0.0), axis=0, keepdims=True)
            dlg_ref[1:2, 0:128] = jnp.sum(jnp.where(ri - H == li, tot, 0.0), axis=0, keepdims=True)

    rr = functools.partial(_rows_rev, nt=nt)
    st_f = pl.BlockSpec((nc, H, DH, DH), lambda i: (nt - 1 - i, 0, 0, 0))
    st_b = pl.BlockSpec((nc, H, DH, DH), lambda i: (i, 0, 0, 0))
    tok = jax.ShapeDtypeStruct((t_len, RW), F32)
    st = jax.ShapeDtypeStruct((H, DH, DH), F32)
    return pl.pallas_call(
        body, name="ret_bwd", grid=(nt,),
        in_specs=[rr(tm, RW), rr(tm, RW), rr(tm, RW), rr(tm, RW), st_f,
                  _rows(tm, RW), _rows(tm, RW), _rows(tm, RW), _rows(tm, RW), st_b,
                  _whole((H, 128)), _whole((H, 128))],
        out_specs=[rr(tm, RW), rr(tm, RW), rr(tm, RW), _rows(tm, RW), _rows(tm, RW), _rows(tm, RW),
                   _acc((H, DH, DH)), _acc((H, DH, DH)), _acc((8, D))],
        out_shape=(tok, tok, tok, tok, tok, tok, st, st, jax.ShapeDtypeStruct((8, D), F32)),
        scratch_shapes=[pltpu.VMEM((H, DH, DH), F32), pltpu.VMEM((H, DH, DH), F32), pltpu.VMEM((8, 128), F32)],
        compiler_params=_cparams(1),
    )(q, k, v, dof, sst, q, k, v, dob, rst, rlf, rlb)


def _in_bwd(x, zuv, dya, dqf, dkf, dvf, dqb, dkb, dvb, dgfb, dx1, cos, sin, mod6, norm1, win, sgw, sgbt, sggain, *, tm):
    t_len = x.shape[0]
    nt, nc = t_len // tm, tm // C

    def body(x_ref, zuv_ref, dya_ref, dqf_ref, dkf_ref, dvf_ref, dqb_ref, dkb_ref, dvb_ref, dgfb_ref, dx1_ref,
             cos_ref, sin_ref, mod_ref, n1_ref, win_ref, sgw_ref, sgbt_ref, sgg_ref,
             gx_ref, dz_ref, small_ref, dsgw_ref, dsgbt_ref):
        i = pl.program_id(0)

        @pl.when(i == 0)
        def _():
            small_ref[...] = jnp.zeros_like(small_ref)
            dsgw_ref[...] = jnp.zeros_like(dsgw_ref)
            dsgbt_ref[...] = jnp.zeros_like(dsgbt_ref)

        zuv = zuv_ref[...]
        _, (ua, dgu, dgv, mixed, vn_b, vah_l, rv_l) = _sg_forward(zuv[:, :AW], zuv[:, AW:], sgw_ref, sgbt_ref, sgg_ref, nc)
        dya = dya_ref[...]
        dz_ref[:, 0:AW] = (dya * mixed * dgu).astype(BF16)
        dmixed = dya * ua
        gain = sgg_ref[...]
        for g in range(G):
            gs = slice(g * DH, (g + 1) * DH)
            dmg = dmixed[:, gs]
            dmb = dmg.astype(BF16)
            wgt = sgw_ref[g].astype(BF16)
            dsw = jnp.zeros((C, C), F32)
            dsb = jnp.zeros((C, DH), F32)
            rows = []
            for c in range(nc):
                rs = slice(c * C, (c + 1) * C)
                dsw = dsw + _dot_nt(dmb[rs, :], vn_b[g][rs, :])
                dsb = dsb + dmg[rs, :]
                rows.append(_dot_tn(wgt, dmb[rs, :]))
            dsgw_ref[g] += dsw
            dsgbt_ref[g] += dsb
            dvn = jnp.concatenate(rows, axis=0) if nc > 1 else rows[0]
            vah, rv = vah_l[g], rv_l[g]
            small_ref[3:4, gs] += jnp.sum(dvn * vah, axis=0, keepdims=True)
            dyg = dvn * gain[:, gs]
            dva = rv * (dyg - vah * jnp.mean(dyg * vah, axis=-1, keepdims=True))
            dz_ref[:, AW + g * DH:AW + (g + 1) * DH] = (dva * dgv[:, gs]).astype(BF16)

        cos = jnp.tile(cos_ref[...], (1, H))
        nsins = -jnp.tile(sin_ref[...], (1, H))
        dz_ref[:, 1024:1536] = _rope(dqf_ref[...] + dqb_ref[...], cos, nsins).astype(BF16)
        dz_ref[:, 1536:2048] = (_rope(dkf_ref[...] + dkb_ref[...], cos, nsins) * KSCALE).astype(BF16)
        dz_ref[:, 2048:2560] = (dvf_ref[...] + dvb_ref[...]).astype(BF16)
        dz_ref[:, 2560:3584] = dgfb_ref[...]

        dhx = _dot(dz_ref[...], win_ref[...])
        x = x_ref[...]
        r = lax.rsqrt(jnp.mean(x * x, axis=-1, keepdims=True) + EPS)
        xh = x * r
        n1, sc1 = n1_ref[...], mod_ref[1:2, :]
        small_ref[0:1, :] += jnp.sum(dhx, axis=0, keepdims=True)
        dhxx = jnp.sum(dhx * xh, axis=0, keepdims=True)
        small_ref[1:2, :] += dhxx * n1
        small_ref[2:3, :] += dhxx * (1.0 + sc1)
        dyg = dhx * (n1 * (1.0 + sc1))
        gx_ref[...] = dx1_ref[...] + r * (dyg - xh * jnp.mean(dyg * xh, axis=-1, keepdims=True))

        @pl.when(i == nt - 1)
        def _():
            ri = lax.broadcasted_iota(jnp.int32, (C, DH), 0)
            li = lax.broadcasted_iota(jnp.int32, (C, DH), 1)
            for g in range(G):
                tot = jnp.broadcast_to(jnp.sum(dsgbt_ref[g], axis=1, keepdims=True), (C, DH))
                small_ref[4 + g:5 + g, 0:DH] = jnp.sum(jnp.where(ri == li, tot, 0.0), axis=0, keepdims=True)

    tok = functools.partial(_rows, tm)
    return pl.pallas_call(
        body, name="in_bwd", grid=(nt,),
        in_specs=[tok(D), tok(2 * AW), tok(AW), tok(RW), tok(RW), tok(RW), tok(RW), tok(RW), tok(RW),
                  tok(2 * RW), tok(D), tok(DH), tok(DH), _whole((6, D)), _whole((1, D)), _whole((INC, D)),
                  _whole((G, C, C)), _whole((G, C, 128)), _whole((1, AW))],
        out_specs=[tok(D), tok(INC), _acc((8, D)), _acc((G, C, C))],
        out_shape=(jax.ShapeDtypeStruct((t_len, D), F32), jax.ShapeDtypeStruct((t_len, INC), BF16),
                   jax.ShapeDtypeStruct((8, D), F32), jax.ShapeDtypeStruct((G, C, C), F32)),
        scratch_shapes=[pltpu.VMEM((G, C, 128), F32)],
        compiler_params=_cparams(1),
    )(x, zuv, dya, dqf, dkf, dvf, dqb, dkb, dvb, dgfb, dx1, cos, sin, mod6, norm1, win, sgw, sgbt, sggain)


def _tn_matmul(a, b, *, bm, bt, name, init=None, init_rows=None):
    t_len, m = a.shape
    n = b.shape[1]
    ni, ntt = m // bm, t_len // bt
    has_init = init is not None
    assert not has_init or bm == m

    def body(*refs):
        if has_init:
            a_ref, b_ref, init_ref, o_ref, ob_ref = refs
        else:
            a_ref, b_ref, o_ref, ob_ref = refs
        t = pl.program_id(1)

        @pl.when(t == 0)
        def _():
            o_ref[...] = jnp.zeros_like(o_ref)
            if has_init:
                o_ref[init_rows[0]:init_rows[1], :] = init_ref[...]

        o_ref[...] += _dot_tn(a_ref[...], b_ref[...])

        @pl.when(t == ntt - 1)
        def _():
            ob_ref[...] = o_ref[...].astype(BF16)

    in_specs = [pl.BlockSpec((bt, bm), lambda i, t: (t, i)), pl.BlockSpec((bt, n), lambda i, t: (t, 0))]
    args = [a, b]
    if has_init:
        in_specs.append(pl.BlockSpec(init.shape, lambda i, t: (0, 0), pipeline_mode=pl.Buffered(1)))
        args.append(init)
    out_spec = pl.BlockSpec((bm, n), lambda i, t: (i, 0))
    return pl.pallas_call(
        body, name=name, grid=(ni, ntt),
        in_specs=in_specs,
        out_specs=[out_spec, out_spec],
        out_shape=(jax.ShapeDtypeStruct((m, n), F32), jax.ShapeDtypeStruct((m, n), BF16)),
        compiler_params=_cparams(2),
    )(*args)


def _ctx_bwd(ctx, hc, kvc, cmod6, norm1, win, rlf, rlb, dscf, dscb, dlg_in):
    lc = ctx.shape[0]

    def body(ctx_ref, hc_ref, kvc_ref, cmod_ref, n1_ref, win_ref, rlf_ref, rlb_ref, dscf_ref, dscb_ref, dlgin_ref,
             dwin_ref, small_ref, dlg_ref):
        k = kvc_ref[:, :RW]
        v = kvc_ref[:, RW:]
        t = lax.broadcasted_iota(jnp.int32, (lc, 1), 0).astype(F32)
        lgf = _log_sigmoid(rlf_ref[...])
        lgb = _log_sigmoid(rlb_ref[...])
        dks, dvs = [], []
        lane = lax.broadcasted_iota(jnp.int32, (1, 128), 1)
        row_f = jnp.zeros((1, 128), F32)
        row_b = jnp.zeros((1, 128), F32)
        for h in range(H):
            hs = slice(h * DH, (h + 1) * DH)
            wf = jnp.exp(lgf[h:h + 1, 0:1] * (lc - 1.0 - t))
            wb = jnp.exp(lgb[h:h + 1, 0:1] * t)
            kh, vhb = k[:, hs], v[:, hs].astype(BF16)
            dsf, dsb = dscf_ref[h].astype(BF16), dscb_ref[h].astype(BF16)
            dkf = wf * _dot_nt(vhb, dsf)
            dkb = wb * _dot_nt(vhb, dsb)
            dvs.append(_dot((kh * wf).astype(BF16), dsf) + _dot((kh * wb).astype(BF16), dsb))
            dks.append((dkf + dkb) * KSCALE)
            lf = jnp.sum((lc - 1.0 - t) * kh * dkf, axis=0, keepdims=True)
            lb = jnp.sum(t * kh * dkb, axis=0, keepdims=True)
            row_f = row_f + jnp.where(lane == h, jnp.sum(lf, axis=1, keepdims=True), 0.0)
            row_b = row_b + jnp.where(lane == h, jnp.sum(lb, axis=1, keepdims=True), 0.0)
        dlg_ref[...] = dlgin_ref[...]
        dlg_ref[0:1, 0:128] += row_f
        dlg_ref[1:2, 0:128] += row_b
        dkv = jnp.concatenate(dks + dvs, axis=1).astype(BF16)
        dhc = _dot(dkv, win_ref[KV_LO:KV_HI, :])
        dwin_ref[...] = _dot_tn(dkv, hc_ref[...])
        x = ctx_ref[...]
        r = lax.rsqrt(jnp.mean(x * x, axis=-1, keepdims=True) + EPS)
        xh = x * r
        small_ref[...] = jnp.zeros_like(small_ref)
        small_ref[0:1, :] = jnp.sum(dhc, axis=0, keepdims=True)
        dhxx = jnp.sum(dhc * xh, axis=0, keepdims=True)
        small_ref[1:2, :] = dhxx * n1_ref[...]
        small_ref[2:3, :] = dhxx * (1.0 + cmod_ref[1:2, :])

    return pl.pallas_call(
        body, name="ctx_bwd",
        out_shape=(jax.ShapeDtypeStruct((2 * RW, D), F32), jax.ShapeDtypeStruct((8, D), F32),
                   jax.ShapeDtypeStruct((8, D), F32)),
        compiler_params=_cparams(),
    )(ctx, hc, kvc, cmod6, norm1, win, rlf, rlb, dscf, dscb, dlg_in)


def _adam_math(w, g, m, v):
    m = ADAM_B1 * m + (1.0 - ADAM_B1) * g
    v = ADAM_B2 * v + (1.0 - ADAM_B2) * (g * g)
    m_hat = m / (1.0 - ADAM_B1 ** ADAM_STEP)
    v_hat = v / (1.0 - ADAM_B2 ** ADAM_STEP)
    delta = -ADAM_LR * (m_hat / (jnp.sqrt(v_hat) + ADAM_EPS) + ADAM_WD * w)
    return delta, m, v


def _adam(w, g, m, v, *, name, br=None):
    r, c = w.shape
    br = r if br is None else br

    def body(w_ref, g_ref, m_ref, v_ref, d_ref, mo_ref, vo_ref):
        d_ref[...], mo_ref[...], vo_ref[...] = _adam_math(w_ref[...], g_ref[...], m_ref[...], v_ref[...])

    spec = pl.BlockSpec((br, c), lambda i: (i, 0))
    sh = jax.ShapeDtypeStruct((r, c), F32)
    return pl.pallas_call(
        body, name=name, grid=(r // br,), in_specs=[spec] * 4, out_specs=[spec] * 3, out_shape=(sh, sh, sh),
        compiler_params=_cparams(1),
    )(w, g, m, v)


def _place():
    x, y, c = lax.axis_index("x"), lax.axis_index("y"), lax.axis_index("c")
    return x, y, c, 4 * x + 2 * y + c


def _flip(x, y, c, k):
    px = 1 - x if (k >> 2) & 1 else x
    py = 1 - y if (k >> 1) & 1 else y
    pc = 1 - c if k & 1 else c
    return (px, py, pc), 4 * px + 2 * py + pc


def _gather_direct(buf_ref, send_sems, recv_sems, sem0=0):
    x, y, c, me = _place()
    sends = []
    for k in range(1, NDEV):
        dev, _ = _flip(x, y, c, k)
        cp = pltpu.make_async_remote_copy(
            src_ref=buf_ref.at[me], dst_ref=buf_ref.at[me],
            send_sem=send_sems.at[sem0 + k - 1], recv_sem=recv_sems.at[sem0 + k - 1],
            device_id=dev, device_id_type=MESH)
        cp.start()
        sends.append(cp)
    for k in range(1, NDEV):
        dev, idx = _flip(x, y, c, k)
        pltpu.make_async_remote_copy(
            src_ref=buf_ref.at[idx], dst_ref=buf_ref.at[idx],
            send_sem=send_sems.at[sem0 + k - 1], recv_sem=recv_sems.at[sem0 + k - 1],
            device_id=dev, device_id_type=MESH).wait_recv()
    for cp in sends:
        cp.wait_send()


def _mod_gather(c_row, cctx_row, wmod, bmod):
    ncol = wmod.shape[1]

    def body(c_ref, cctx_ref, wmod_ref, bmod_ref, mod_ref, cs_ref, cbuf, pbuf, send_sems, recv_sems):
        _, _, _, me = _place()
        cbuf[me] = jnp.broadcast_to(c_ref[...], (8, D))
        _gather_direct(cbuf, send_sems, recv_sems, 0)
        row = lax.broadcasted_iota(jnp.int32, (16, D), 0)
        call = jnp.where(row == 8, jnp.broadcast_to(cctx_ref[...], (16, D)), 0.0)
        for d in range(NDEV):
            call = jnp.where(row == d, jnp.concatenate([cbuf[d], cbuf[d]], axis=0), call)
        cs = call * _sigmoid(call)
        cs_ref[...] = cs
        pbuf[me] = _dot(cs.astype(BF16), wmod_ref[...].astype(BF16))
        _gather_direct(pbuf, send_sems, recv_sems, NDEV - 1)
        for d in range(NDEV):
            mod_ref[:, d * ncol:(d + 1) * ncol] = pbuf[d] + bmod_ref[:, d * ncol:(d + 1) * ncol]

    return pl.pallas_call(
        body, name="mod_gather",
        out_shape=(jax.ShapeDtypeStruct((16, 6 * D), F32), jax.ShapeDtypeStruct((16, D), F32)),
        scratch_shapes=[pltpu.VMEM((NDEV, 8, D), F32), pltpu.VMEM((NDEV, 16, ncol), F32),
                        pltpu.SemaphoreType.DMA((2 * (NDEV - 1),)), pltpu.SemaphoreType.DMA((2 * (NDEV - 1),))],
        compiler_params=_cparams(),
    )(c_row, cctx_row, wmod, bmod)


def _weight_gather(shards):
    n = len(shards)

    def body(*refs):
        in_refs, out_refs = refs[:n], refs[n:2 * n]
        cast_refs = refs[2 * n:3 * n]
        send_sems, recv_sems, local_sems = refs[3 * n:]
        x, y, c, me = _place()
        sib = (x, y, 1 - c)
        chips = [(1 - x, y), (x, 1 - y), (1 - x, 1 - y)]

        def blk(px, py, pc):
            return 4 * px + 2 * py + pc

        def copy(w, k, block, to, src=None):
            dst = out_refs[w].at[block]
            return pltpu.make_async_remote_copy(
                src_ref=dst if src is None else src, dst_ref=dst,
                send_sem=send_sems.at[w, k], recv_sem=recv_sems.at[w, k], device_id=to, device_id_type=MESH)

        first, passed, mine = [], [], []
        for w in range(n):
            cast_refs[w][...] = in_refs[w][...].astype(BF16)
            m = pltpu.make_async_copy(cast_refs[w], out_refs[w].at[me], local_sems.at[w])
            m.start()
            mine.append(m)
            cps = [copy(w, 0, me, sib, src=cast_refs[w])]
            cps += [copy(w, 1 + j, me, (*chip, c), src=cast_refs[w]) for j, chip in enumerate(chips)]
            for cp in cps:
                cp.start()
            first += cps
        for w in range(n):
            for j, chip in enumerate(chips):
                b = blk(*chip, c)
                copy(w, 1 + j, b, (x, y, c)).wait_recv()
                fw = copy(w, 4 + j, b, sib)
                fw.start()
                passed.append(fw)
        for w in range(n):
            copy(w, 0, blk(x, y, 1 - c), (x, y, c)).wait_recv()
            for j, chip in enumerate(chips):
                copy(w, 4 + j, blk(*chip, 1 - c), (x, y, c)).wait_recv()
        for cp in first + passed:
            cp.wait_send()
        for m in mine:
            m.wait()

    vm = pl.BlockSpec(memory_space=pltpu.VMEM)
    hbm = pl.BlockSpec(memory_space=pltpu.HBM)
    return pl.pallas_call(
        body, name="weight_gather",
        in_specs=[vm] * n, out_specs=[hbm] * n,
        out_shape=tuple(jax.ShapeDtypeStruct((NDEV,) + s.shape, BF16) for s in shards),
        scratch_shapes=[pltpu.VMEM(s.shape, BF16) for s in shards]
        + [pltpu.SemaphoreType.DMA((n, 7)), pltpu.SemaphoreType.DMA((n, 7)), pltpu.SemaphoreType.DMA((n,))],
        compiler_params=_cparams(),
    )(*shards)


_HBM = pl.BlockSpec(memory_space=pltpu.HBM)
_SEMS = pl.BlockSpec(memory_space=pltpu.SEMAPHORE)
_EFFECT = pltpu.SideEffectType.DATAFLOW_SIDE_EFFECTING


def _exchange_copy(src_refs, land_refs, send_sems, recv_sems, w, k, scatter, landing_slot_is_mine):
    x, y, c, me = _place()
    dev, pidx = _flip(x, y, c, k)
    src = src_refs[w].at[pidx] if scatter else src_refs[w]
    slot = me if landing_slot_is_mine else pidx
    return pltpu.make_async_remote_copy(
        src_ref=src, dst_ref=land_refs[w].at[slot],
        send_sem=send_sems.at[w * (NDEV - 1) + k - 1], recv_sem=recv_sems.at[w * (NDEV - 1) + k - 1],
        device_id=dev, device_id_type=MESH)


def _exchange_start(srcs, *, scatter, name):
    n = len(srcs)
    lands = [lax.empty((NDEV,) + tuple(s.shape[-2:]), s.dtype) for s in srcs]

    def body(*refs):
        src_refs, land_refs = refs[:n], refs[n:2 * n]
        send_sems, recv_sems = refs[2 * n], refs[2 * n + 1]
        token = refs[-1]
        for w in range(n):
            for k in range(1, NDEV):
                _exchange_copy(src_refs, land_refs, send_sems, recv_sems, w, k, scatter, True).start()
        token[...] = jnp.zeros_like(token)

    arrs = list(srcs) + lands
    out_shape = ([pltpu.SemaphoreType.DMA((n * (NDEV - 1),)), pltpu.SemaphoreType.DMA((n * (NDEV - 1),))]
                 + [pltpu.HBM(a.shape, a.dtype) for a in arrs] + [jax.ShapeDtypeStruct((8, 128), F32)])
    res = pl.pallas_call(
        body, name=name, out_shape=tuple(out_shape),
        in_specs=[_HBM] * (2 * n),
        out_specs=tuple([_SEMS, _SEMS] + [_HBM] * (2 * n) + [pl.BlockSpec(memory_space=pltpu.VMEM)]),
        input_output_aliases={i: 2 + i for i in range(2 * n)},
        compiler_params=pltpu.CompilerParams(has_side_effects=_EFFECT),
    )(*[pltpu.with_memory_space_constraint(a, pltpu.HBM) for a in arrs])
    return res[:-1], res[-1]


def _exchange_wait(handles, after, *, scatter, name):
    n = (len(handles) - 2) // 2
    na = len(after)

    def body(*refs):
        src_refs, land_refs = refs[:n], refs[n:2 * n]
        send_sems, recv_sems = refs[2 * n], refs[2 * n + 1]
        for w in range(n):
            for k in range(1, NDEV):
                cp = _exchange_copy(src_refs, land_refs, send_sems, recv_sems, w, k, scatter, False)
                cp.wait_send()
                cp.wait_recv()

    arrs = handles[2:]
    res = pl.pallas_call(
        body, name=name, out_shape=tuple(pltpu.HBM(a.shape, a.dtype) for a in arrs),
        in_specs=[_HBM] * (2 * n) + [_SEMS, _SEMS] + [_HBM] * na,
        out_specs=tuple([_HBM] * (2 * n)),
        input_output_aliases={i: i for i in range(2 * n)},
        compiler_params=pltpu.CompilerParams(has_side_effects=_EFFECT),
    )(*arrs, handles[0], handles[1], *[pltpu.with_memory_space_constraint(a, pltpu.HBM) for a in after])
    return res[:n], res[n:]


def _cast_bf16(arrs, *, name):
    def body(*refs):
        for i_ref, o_ref in zip(refs[:len(arrs)], refs[len(arrs):]):
            o_ref[...] = i_ref[...].astype(BF16)

    return pl.pallas_call(body, name=name, out_shape=tuple(jax.ShapeDtypeStruct(a.shape, BF16) for a in arrs),
                          compiler_params=_cparams())(*arrs)


def _rs_adam(me_arr, own, land, w, m, v, *, name):
    def body(me_ref, own_ref, land_ref, w_ref, m_ref, v_ref, g_ref, d_ref, mo_ref, vo_ref):
        me = me_ref[0]
        acc = jnp.zeros(own_ref.shape, F32)
        for p in range(NDEV):
            acc = acc + jnp.where(p == me, own_ref[...], land_ref[p].astype(F32))
        g_ref[...] = acc
        d_ref[...], mo_ref[...], vo_ref[...] = _adam_math(w_ref[...], acc, m_ref[...], v_ref[...])

    sh = jax.ShapeDtypeStruct(own.shape, F32)
    vm = pl.BlockSpec(memory_space=pltpu.VMEM)
    return pl.pallas_call(
        body, name=name, out_shape=(sh, sh, sh, sh),
        in_specs=[pl.BlockSpec(memory_space=pltpu.SMEM)] + [vm] * 5,
        compiler_params=_cparams(),
    )(me_arr, own, land, w, m, v)


ROW_F, ROW_I, ROW_C, ROW_G1, ROW_LG = 0, 8, 16, 24, 32
PACK_ROWS = 40


def _small_reduce(pack, dsgw):
    def body(pack_ref, sgw_ref, sum_ref, all_ref, sgw_sum_ref, sgw_all, send_sems, recv_sems):
        _, _, _, me = _place()
        all_ref[me] = pack_ref[...]
        sgw_all[me] = sgw_ref[...]
        _gather_direct(all_ref, send_sems, recv_sems, 0)
        _gather_direct(sgw_all, send_sems, recv_sems, NDEV - 1)
        acc = all_ref[0]
        acc_w = sgw_all[0]
        for d in range(1, NDEV):
            acc = acc + all_ref[d]
            acc_w = acc_w + sgw_all[d]
        sum_ref[...] = acc
        sgw_sum_ref[...] = acc_w

    return pl.pallas_call(
        body, name="small_reduce",
        out_shape=(jax.ShapeDtypeStruct((PACK_ROWS, D), F32), jax.ShapeDtypeStruct((NDEV, PACK_ROWS, D), F32),
                   jax.ShapeDtypeStruct((G, C, C), F32)),
        scratch_shapes=[pltpu.VMEM((NDEV, G, C, C), F32),
                        pltpu.SemaphoreType.DMA((2 * (NDEV - 1),)), pltpu.SemaphoreType.DMA((2 * (NDEV - 1),))],
        compiler_params=_cparams(),
    )(pack, dsgw)


def _adam_small(s, sgw_g, params):
    names = ["norm1", "norm2", "norm_f", "sg_gain", "sg_b", "ret_logit_f", "ret_logit_b", "b_mod", "sg_w"]
    flat = [a for n in names for a in params[n]]

    def body(*refs):
        s_ref, sgw_ref = refs[0], refs[1]
        p_refs = refs[2:2 + 3 * len(names)]
        o_refs = refs[2 + 3 * len(names):]
        loss_ref = o_refs[-1]

        def row(r):
            return s_ref[r:r + 1, :]

        grads = {
            "norm1": row(ROW_I + 2) + row(ROW_C + 2),
            "norm2": row(ROW_F + 4),
            "norm_f": row(ROW_F + 0),
            "sg_gain": s_ref[ROW_I + 3:ROW_I + 4, 0:AW],
            "sg_b": s_ref[ROW_I + 4:ROW_I + 8, 0:DH],
            "sg_w": sgw_ref[...],
        }
        for j, n in enumerate(names):
            w_ref, m_ref, v_ref = p_refs[3 * j:3 * j + 3]
            g_ref, d_ref, mo_ref, vo_ref = o_refs[4 * j:4 * j + 4]
            w = w_ref[...]
            if n in ("ret_logit_f", "ret_logit_b"):
                r = ROW_LG + (0 if n == "ret_logit_f" else 1)
                g = s_ref[r:r + 1, 0:H] * _sigmoid(-w)
            elif n == "b_mod":
                rows = [row(ROW_I + 0) + row(ROW_C + 0), row(ROW_I + 1) + row(ROW_C + 1), row(ROW_G1),
                        row(ROW_F + 2), row(ROW_F + 3), row(ROW_F + 1)]
                g = jnp.concatenate(rows, axis=1)
            else:
                g = grads[n]
            g_ref[...] = g
            d_ref[...], mo_ref[...], vo_ref[...] = _adam_math(w, g, m_ref[...], v_ref[...])
        loss_ref[...] = jnp.full((1, 1), 0.5 / D, F32) * jnp.sum(row(ROW_F + 5), axis=1, keepdims=True)

    out_shape = []
    for n in names:
        w = params[n][0]
        out_shape += [jax.ShapeDtypeStruct(w.shape, F32)] * 4
    out_shape.append(jax.ShapeDtypeStruct((1, 1), F32))
    outs = pl.pallas_call(body, name="adam_small", out_shape=tuple(out_shape), compiler_params=_cparams())(
        s, sgw_g, *flat)
    return {n: tuple(outs[4 * j:4 * j + 4]) for j, n in enumerate(names)}, outs[-1]


def _wmod_grad(cs_all, dmod_cols, wmod, m, v):
    ncol = wmod.shape[1]

    def body(cs_ref, dm_ref, w_ref, m_ref, v_ref, g_ref, d_ref, mo_ref, vo_ref, part_ref):
        dmb = dm_ref[...].astype(BF16)
        g = _dot_tn(cs_ref[...].astype(BF16), dmb)
        g_ref[...] = g
        d_ref[...], mo_ref[...], vo_ref[...] = _adam_math(w_ref[...], g, m_ref[...], v_ref[...])
        part_ref[...] = _dot_nt(dmb, w_ref[...].astype(BF16))

    sh = jax.ShapeDtypeStruct((D, ncol), F32)
    return pl.pallas_call(
        body, name="wmod_grad", out_shape=(sh, sh, sh, sh, jax.ShapeDtypeStruct((16, D), F32)),
        compiler_params=_cparams(),
    )(cs_all, dmod_cols, wmod, m, v)


def _cctx_reduce(part, cctx_row, m_row, v_row):
    def body(part_ref, c_ref, m_ref, v_ref, g_ref, d_ref, mo_ref, vo_ref, buf, send_sems, recv_sems):
        _, _, _, me = _place()
        buf[me] = part_ref[8:16, :]
        _gather_direct(buf, send_sems, recv_sems, 0)
        acc = buf[0]
        for d in range(1, NDEV):
            acc = acc + buf[d]
        cc = c_ref[...]
        _, dsilu = _silu_parts(cc)
        g = acc[0:1, :] * dsilu
        g_ref[...] = g
        d_ref[...], mo_ref[...], vo_ref[...] = _adam_math(cc, g, m_ref[...], v_ref[...])

    sh = jax.ShapeDtypeStruct((1, D), F32)
    return pl.pallas_call(
        body, name="cctx_reduce", out_shape=(sh, sh, sh, sh),
        scratch_shapes=[pltpu.VMEM((NDEV, 8, D), F32),
                        pltpu.SemaphoreType.DMA((NDEV - 1,)), pltpu.SemaphoreType.DMA((NDEV - 1,))],
        compiler_params=_cparams(),
    )(part, cctx_row, m_row, v_row)


def _rope_tables(t_len):
    n_freq = DH // 4
    inv = (ROPE_BASE ** (-np.arange(n_freq, dtype=np.float32) / n_freq)).astype(np.float32)
    tok = np.arange(t_len)
    rows = (tok // GRID_W).astype(np.float32)
    cols = (tok % GRID_W).astype(np.float32)
    ang_r = rows[:, None] * inv[None, :]
    ang_c = cols[:, None] * inv[None, :]
    cos = np.concatenate([np.cos(ang_r)] * 2 + [np.cos(ang_c)] * 2, axis=1).astype(np.float32)
    sin = np.concatenate([-np.sin(ang_r), np.sin(ang_r), -np.sin(ang_c), np.sin(ang_c)], axis=1).astype(np.float32)
    return jnp.asarray(cos), jnp.asarray(sin)


def _pad_row(v, width=D):
    v = v.reshape(1, -1)
    return jnp.pad(v, ((0, 0), (0, width - v.shape[1])))


def _local_step(x, tgt, ctx, mod6, cmod6, norm1, norm2, normf, win, wout, ffn_weights, sgw, sgb, sggain, rlf, rlb,
                *, tm_a, tm_b, tm_f, tm_m, tm_r, tm_i, bt_w, after_ffn_grads=None):
    t_len = x.shape[0]
    cos, sin = _rope_tables(t_len)
    sgbt = jnp.broadcast_to(sgb[:, :, None], (G, C, 128))
    rlf = jnp.broadcast_to(rlf.reshape(H, 1), (H, 128))
    rlb = jnp.broadcast_to(rlb.reshape(H, 1), (H, 128))
    hc, kvc, scf, scb = _ctx_fwd(ctx, cmod6, norm1, win, rlf, rlb)
    hx, zuv, q, k, v, gfb, of, ya, sst = _fwd_a(x, mod6, norm1, win, sgw, sgbt, sggain, cos, sin, rlf, scf, tm=tm_a)
    ob, ycat, y, x1, rst = _fwd_b(q, k, v, of, gfb, ya, x, mod6, wout, rlb, scb, tm=tm_b)
    wg, wu, wd = ffn_weights(x1) if callable(ffn_weights) else ffn_weights
    dx1, h2, da, db, hm, df, small_f = _ffn(x1, tgt, mod6, norm2, normf, wg, wu, wd, tm=tm_f)
    d_wd, d_wd_b = _tn_matmul(hm, df, bm=DFF, bt=bt_w, name="dw_down")
    d_wg, d_wg_b = _tn_matmul(da, h2, bm=DFF, bt=bt_w, name="dw_gate")
    d_wu, d_wu_b = _tn_matmul(db, h2, bm=DFF, bt=bt_w, name="dw_up")
    if after_ffn_grads is not None:
        mod6 = mod6 + after_ffn_grads((d_wd_b, d_wg_b, d_wu_b))
    dy, dya, dgfb, dof, dob, dg1 = _mid_bwd(dx1, y, of, ob, gfb, mod6, wout, tm=tm_m)
    dqf, dkf, dvf, dqb, dkb, dvb, dscf, dscb, dlg = _ret_bwd(q, k, v, dof, dob, sst, rst, rlf, rlb, tm=tm_r)
    gx, dz, small_i, dsgw = _in_bwd(x, zuv, dya, dqf, dkf, dvf, dqb, dkb, dvb, dgfb, dx1, cos, sin,
                                    mod6, norm1, win, sgw, sgbt, sggain, tm=tm_i)
    dwin_c, small_c, dlg = _ctx_bwd(ctx, hc, kvc, cmod6, norm1, win, rlf, rlb, dscf, dscb, dlg)
    d_wo, d_wo_b = _tn_matmul(ycat, dy, bm=D, bt=bt_w, name="dw_out")
    d_wi, d_wi_b = _tn_matmul(dz, hx, bm=INC, bt=min(512, bt_w), name="dw_in", init=dwin_c,
                              init_rows=(KV_LO, KV_HI))
    pack = jnp.concatenate([small_f, small_i, small_c, dg1, dlg], axis=0)
    grads = {"w_in": (d_wi, d_wi_b), "w_out": (d_wo, d_wo_b), "w_gate": (d_wg, d_wg_b), "w_up": (d_wu, d_wu_b),
             "w_down": (d_wd, d_wd_b)}
    return gx, grads, pack, dsgw


def kernel(x, c, ctx, c_ctx, w_mod, b_mod, norm1, w_in, sg_gain, sg_w, sg_b, ret_logit_f, ret_logit_b, w_out, norm2, w_gate, w_up, w_down, norm_f, loss_target, m_c_ctx, m_w_mod, m_b_mod, m_norm1, m_w_in, m_sg_gain, m_sg_w, m_sg_b, m_ret_logit_f, m_ret_logit_b, m_w_out, m_norm2, m_w_gate, m_w_up, m_w_down, m_norm_f, v_c_ctx, v_w_mod, v_b_mod, v_norm1, v_w_in, v_sg_gain, v_sg_w, v_sg_b, v_ret_logit_f, v_ret_logit_b, v_w_out, v_norm2, v_w_gate, v_w_up, v_w_down, v_norm_f):
    t_len = x.shape[1]
    tm = min(512, t_len)
    me = 4 * lax.axis_index("x") + 2 * lax.axis_index("y") + lax.axis_index("c")
    tr = lambda a: jnp.transpose(a[0])

    cctx_row = c_ctx.reshape(1, D)
    mod_all, cs_all = _mod_gather(c, cctx_row, w_mod[0], b_mod)
    mod6 = lax.dynamic_slice(mod_all, (me, 0), (1, 6 * D)).reshape(6, D)
    cmod6 = mod_all[8].reshape(6, D)

    g_in, g_out = _weight_gather([tr(w_in), w_out[0]])
    win_t = g_in.reshape(INC, D)
    wout = g_out.reshape(D, D)

    ffn_shards = _cast_bf16([tr(w_gate), tr(w_up), w_down[0]], name="cast_ffn")
    h_ffn, tok = _exchange_start(ffn_shards, scatter=False, name="wffn_start")
    mod6 = mod6 + tok[0, 0]

    def ffn_weights(after):
        own, lands = _exchange_wait(h_ffn, [after], scatter=False, name="wffn_wait")
        return [lax.dynamic_update_slice(l, o[None], (me, 0, 0)).reshape(DFF, D) for o, l in zip(own, lands)]

    rs = {}

    def after_ffn_grads(bf):
        rs["ffn"], tok_ffn = _exchange_start([b.reshape(NDEV, DFF // NDEV, D) for b in bf], scatter=True,
                                             name="rs_ffn_start")
        return tok_ffn[0, 0]

    gx, grads, pack, dsgw = _local_step(
        x[0], loss_target[0], ctx[0], mod6, cmod6, norm1, norm2[0:1], norm_f.reshape(1, D), win_t, wout, ffn_weights,
        sg_w[0], sg_b[0], sg_gain, ret_logit_f, ret_logit_b,
        tm_a=tm, tm_b=tm, tm_f=min(256, t_len), tm_m=tm, tm_r=tm, tm_i=tm, bt_w=min(1024, t_len),
        after_ffn_grads=after_ffn_grads)

    h_io, tok_io = _exchange_start([grads["w_out"][1].reshape(NDEV, D // NDEV, D),
                                    grads["w_in"][1].reshape(NDEV, INC // NDEV, D)], scatter=True, name="rs_io_start")
    _, lands_ffn = _exchange_wait(rs["ffn"], [tok_io], scatter=True, name="rs_ffn_wait")

    outs = {}
    me_arr = me.reshape(1).astype(jnp.int32)

    def finish(name, land, w, m, v, transposed):
        rows = land.shape[1]
        own = lax.dynamic_slice(grads[name][0], (me * rows, 0), (rows, D))
        take = tr if transposed else (lambda a: a[0])
        res = _rs_adam(me_arr, own, land, take(w), take(m), take(v), name="adam_" + name)
        put = (lambda a: jnp.transpose(a)[None]) if transposed else (lambda a: a[None])
        outs[name] = tuple(put(a) for a in res)
        return res[0]

    g_down = finish("w_down", lands_ffn[0], w_down, m_w_down, v_w_down, False)
    g_gate = finish("w_gate", lands_ffn[1], w_gate, m_w_gate, v_w_gate, True)
    g_up = finish("w_up", lands_ffn[2], w_up, m_w_up, v_w_up, True)

    psum_rows, pall, sgw_sum = _small_reduce(pack + tok_io[0, 0], dsgw)
    dmod_rows = (ROW_I + 0, ROW_I + 1, ROW_G1, ROW_F + 2, ROW_F + 3, ROW_F + 1)
    dmod_all = jnp.concatenate([pall[:, r, :] for r in dmod_rows], axis=1)
    dcmod = jnp.concatenate([psum_rows[ROW_C + 0:ROW_C + 1], psum_rows[ROW_C + 1:ROW_C + 2],
                             jnp.zeros((1, 4 * D), F32)], axis=1)
    dmod_mat = jnp.concatenate([dmod_all, jnp.pad(dcmod, ((0, 7), (0, 0)))], axis=0)
    ncol = 6 * D // NDEV
    dmod_cols = lax.dynamic_slice(dmod_mat, (0, me * ncol), (16, ncol))
    g_wm, d_wm, m_wm, v_wm, part = _wmod_grad(cs_all, dmod_cols, w_mod[0], m_w_mod[0], v_w_mod[0])
    outs["w_mod"] = (g_wm[None], d_wm[None], m_wm[None], v_wm[None])
    g_cc, d_cc, m_cc, v_cc = _cctx_reduce(part, cctx_row, m_c_ctx.reshape(1, D), v_c_ctx.reshape(1, D))
    outs["c_ctx"] = tuple(a.reshape(D) for a in (g_cc, d_cc, m_cc, v_cc))

    small_params = {
        "norm1": (norm1, m_norm1, v_norm1), "norm2": (norm2, m_norm2, v_norm2),
        "norm_f": tuple(a.reshape(1, D) for a in (norm_f, m_norm_f, v_norm_f)),
        "sg_gain": (sg_gain, m_sg_gain, v_sg_gain), "sg_b": (sg_b[0], m_sg_b[0], v_sg_b[0]),
        "ret_logit_f": (ret_logit_f, m_ret_logit_f, v_ret_logit_f),
        "ret_logit_b": (ret_logit_b, m_ret_logit_b, v_ret_logit_b),
        "b_mod": (b_mod, m_b_mod, v_b_mod), "sg_w": (sg_w[0], m_sg_w[0], v_sg_w[0])}
    small, loss = _adam_small(psum_rows, sgw_sum, small_params)
    back = {"norm_f": lambda a: a.reshape(D), "sg_b": lambda a: a[None], "sg_w": lambda a: a[None]}
    for name, vals in small.items():
        outs[name] = tuple(back.get(name, lambda a: a)(a) for a in vals)

    _, lands_io = _exchange_wait(h_io, [g_wm, g_down, g_gate, g_up], scatter=True, name="rs_io_wait")
    finish("w_out", lands_io[0], w_out, m_w_out, v_w_out, False)
    finish("w_in", lands_io[1], w_in, m_w_in, v_w_in, True)

    order = ["c_ctx", "w_mod", "b_mod", "norm1", "w_in", "sg_gain", "sg_w", "sg_b", "ret_logit_f", "ret_logit_b",
             "w_out", "norm2", "w_gate", "w_up", "w_down", "norm_f"]
    res = [loss.reshape(()), gx[None]]
    for j in range(4):
        res += [outs[name][j] for name in order]
    return tuple(res)
```

```python
import functools
import math

import numpy as np
import jax
import jax.numpy as jnp
from jax import lax
from jax.experimental import pallas as pl
from jax.experimental.pallas import tpu as pltpu

F32 = jnp.float32
BF16 = jnp.bfloat16

D = 1024
AW = 512
RW = 512
H = 4
DH = 128
G = 4
C = 128
DFF = 2816
INC = 3584
KV_LO, KV_HI = 1536, 2560
NDEV = 8
EPS = 1e-6
KSCALE = DH ** -0.5
ROPE_BASE = 10000.0
GRID_W = 64

ADAM_LR = 0.001
ADAM_B1 = 0.9
ADAM_B2 = 0.999
ADAM_EPS = 1e-08
ADAM_WD = 0.01
ADAM_STEP = 10

VMEM_LIMIT = 56 * 1024 * 1024
MESH = pl.DeviceIdType.MESH


def _cparams(n_grid=0, **kw):
    sem = ("arbitrary",) * n_grid if n_grid else None
    return pltpu.CompilerParams(dimension_semantics=sem, vmem_limit_bytes=VMEM_LIMIT, **kw)


def _dot(a, b):
    return jnp.dot(a, b, preferred_element_type=F32)


def _dot_nt(a, b):
    return lax.dot_general(a, b, (((1,), (1,)), ((), ())), preferred_element_type=F32)


def _dot_tn(a, b):
    return lax.dot_general(a, b, (((0,), (0,)), ((), ())), preferred_element_type=F32)


def _whole(shape):
    nd = len(shape)
    return pl.BlockSpec(shape, lambda *_: (0,) * nd, pipeline_mode=pl.Buffered(1))


def _acc(shape):
    nd = len(shape)
    return pl.BlockSpec(shape, lambda *_: (0,) * nd)


def _rows(tm, n):
    return pl.BlockSpec((tm, n), lambda i: (i, 0))


def _rows_rev(tm, n, nt):
    return pl.BlockSpec((tm, n), lambda i: (nt - 1 - i, 0))


def _sigmoid(x):
    return 1.0 / (1.0 + jnp.exp(-x))


def _log_sigmoid(x):
    return jnp.minimum(x, 0.0) - jnp.log(1.0 + jnp.exp(-jnp.abs(x)))


_GK0 = math.sqrt(2.0 / math.pi)
_GK1 = 0.044715


def _gelu(x):
    x2 = x * x
    t = jnp.tanh(_GK0 * x * (1.0 + _GK1 * x2))
    g = 0.5 * x * (1.0 + t)
    dg = 0.5 * (1.0 + t) + 0.5 * x * (1.0 - t * t) * (_GK0 * (1.0 + 3.0 * _GK1 * x2))
    return g, dg


def _silu_parts(a):
    s = _sigmoid(a)
    return a * s, s * (1.0 + a * (1.0 - s))


def _rope(t, cos, sins):
    n = t.shape[1]
    lane = lax.broadcasted_iota(jnp.int32, t.shape, 1)
    first = (lane & 63) < 32
    partner = jnp.where(first, pltpu.roll(t, n - 32, 1), pltpu.roll(t, 32, 1))
    return t * cos + partner * sins


def _headnorm(o):
    outs, rs = [], []
    for h in range(H):
        oh = o[:, h * DH:(h + 1) * DH]
        r = lax.rsqrt(jnp.mean(oh * oh, axis=-1, keepdims=True) + EPS)
        outs.append(oh * r)
        rs.append(r)
    return jnp.concatenate(outs, axis=1), rs


def _decay_consts(lg, forward):
    ii = lax.broadcasted_iota(jnp.int32, (C, C), 0).astype(F32)
    jj = lax.broadcasted_iota(jnp.int32, (C, C), 1).astype(F32)
    ic = lax.broadcasted_iota(jnp.int32, (C, 1), 0).astype(F32)
    if forward:
        diff = ii - jj
        xi = jnp.exp(lg * (ic + 1.0))
        z = jnp.exp(lg * (C - 1.0 - ic))
    else:
        diff = jj - ii
        xi = jnp.exp(lg * (C - ic))
        z = jnp.exp(lg * ic)
    dm = jnp.where(diff >= 0.0, jnp.exp(lg * jnp.maximum(diff, 0.0)), 0.0)
    dec = jnp.exp(lg * float(C))
    return dm, xi, z, dec, ic


def _ctx_fwd(ctx, cmod6, norm1, win, rlf, rlb):
    lc = ctx.shape[0]

    def body(ctx_ref, cmod_ref, n1_ref, win_ref, rlf_ref, rlb_ref, hc_ref, kvc_ref, scf_ref, scb_ref):
        x = ctx_ref[...]
        r = lax.rsqrt(jnp.mean(x * x, axis=-1, keepdims=True) + EPS)
        hc = (x * r) * (n1_ref[...] * (1.0 + cmod_ref[1:2, :])) + cmod_ref[0:1, :]
        hcb = hc.astype(BF16)
        hc_ref[...] = hcb
        kv = _dot_nt(hcb, win_ref[KV_LO:KV_HI, :])
        k = kv[:, :RW] * KSCALE
        v = kv[:, RW:]
        kvc_ref[:, :RW] = k
        kvc_ref[:, RW:] = v
        t = lax.broadcasted_iota(jnp.int32, (lc, 1), 0).astype(F32)
        lgf = _log_sigmoid(rlf_ref[...])
        lgb = _log_sigmoid(rlb_ref[...])
        for h in range(H):
            hs = slice(h * DH, (h + 1) * DH)
            wf = jnp.exp(lgf[h:h + 1, 0:1] * (lc - 1.0 - t))
            wb = jnp.exp(lgb[h:h + 1, 0:1] * t)
            vh = v[:, hs].astype(BF16)
            scf_ref[h] = _dot_tn((k[:, hs] * wf).astype(BF16), vh)
            scb_ref[h] = _dot_tn((k[:, hs] * wb).astype(BF16), vh)

    return pl.pallas_call(
        body, name="ctx_fwd",
        out_shape=(jax.ShapeDtypeStruct((lc, D), BF16), jax.ShapeDtypeStruct((lc, 2 * RW), F32),
                   jax.ShapeDtypeStruct((H, DH, DH), F32), jax.ShapeDtypeStruct((H, DH, DH), F32)),
        compiler_params=_cparams(),
    )(ctx, cmod6, norm1, win, rlf, rlb)


def _sg_forward(u, v, sgw_ref, sgbt_ref, sgg_ref, nc):
    ua, dgu = _gelu(u)
    va, dgv = _gelu(v)
    gain = sgg_ref[...]
    mixed, vn_b, vah_l, rv_l = [], [], [], []
    for g in range(G):
        gs = slice(g * DH, (g + 1) * DH)
        vag = va[:, gs]
        rv = lax.rsqrt(jnp.mean(vag * vag, axis=-1, keepdims=True) + EPS)
        vah = vag * rv
        vn = (vah * gain[:, gs]).astype(BF16)
        wg = sgw_ref[g].astype(BF16)
        bcol = sgbt_ref[g]
        rows = [_dot(wg, vn[c * C:(c + 1) * C, :]) + bcol for c in range(nc)]
        mixed.append(jnp.concatenate(rows, axis=0) if nc > 1 else rows[0])
        vn_b.append(vn)
        vah_l.append(vah)
        rv_l.append(rv)
    mixed = jnp.concatenate(mixed, axis=1)
    return ua * mixed, (ua, dgu, dgv, mixed, vn_b, vah_l, rv_l)


def _fwd_a(x, mod6, norm1, win, sgw, sgbt, sggain, cos, sin, rlf, scf, *, tm):
    t_len = x.shape[0]
    nt, nc = t_len // tm, tm // C

    def body(x_ref, mod_ref, n1_ref, win_ref, sgw_ref, sgbt_ref, sgg_ref, cos_ref, sin_ref, rlf_ref, scf_ref,
             hx_ref, zuv_ref, q_ref, k_ref, v_ref, gfb_ref, of_ref, ya_ref, sst_ref, s_scr):
        i = pl.program_id(0)

        @pl.when(i == 0)
        def _():
            s_scr[...] = scf_ref[...]

        x = x_ref[...]
        r = lax.rsqrt(jnp.mean(x * x, axis=-1, keepdims=True) + EPS)
        hx = (x * r) * (n1_ref[...] * (1.0 + mod_ref[1:2, :])) + mod_ref[0:1, :]
        hxb = hx.astype(BF16)
        hx_ref[...] = hxb
        z = _dot_nt(hxb, win_ref[...])
        zuv_ref[...] = z[:, :2 * AW]
        gfb_ref[...] = z[:, 2560:3584]
        cos = jnp.tile(cos_ref[...], (1, H))
        sins = jnp.tile(sin_ref[...], (1, H))
        q_ref[...] = _rope(z[:, 1024:1536], cos, sins).astype(BF16)
        k_ref[...] = (_rope(z[:, 1536:2048], cos, sins) * KSCALE).astype(BF16)
        v_ref[...] = z[:, 2048:2560].astype(BF16)
        ya, _ = _sg_forward(z[:, :AW], z[:, AW:2 * AW], sgw_ref, sgbt_ref, sgg_ref, nc)
        ya_ref[...] = ya

        lg = _log_sigmoid(rlf_ref[...])
        for h in range(H):
            dm, xi, zc, dec, _ = _decay_consts(lg[h:h + 1, 0:1], True)
            hs = slice(h * DH, (h + 1) * DH)
            for c in range(nc):
                rs = slice(c * C, (c + 1) * C)
                qc, kc, vc = q_ref[rs, hs], k_ref[rs, hs], v_ref[rs, hs]
                s = s_scr[h]
                sst_ref[c, h] = s
                a = (_dot_nt(qc, kc) * dm).astype(BF16)
                of_ref[rs, hs] = _dot(a, vc) + xi * _dot(qc, s.astype(BF16))
                kz = (kc.astype(F32) * zc).astype(BF16)
                s_scr[h] = dec * s + _dot_tn(kz, vc)

    n_chunks = t_len // C
    return pl.pallas_call(
        body, name="fwd_a", grid=(nt,),
        in_specs=[_rows(tm, D), _whole((6, D)), _whole((1, D)), _whole((INC, D)), _whole((G, C, C)),
                  _whole((G, C, 128)), _whole((1, AW)), _rows(tm, DH), _rows(tm, DH), _whole((H, 128)),
                  _whole((H, DH, DH))],
        out_specs=[_rows(tm, D), _rows(tm, 2 * AW), _rows(tm, RW), _rows(tm, RW), _rows(tm, RW),
                   _rows(tm, 2 * RW), _rows(tm, RW), _rows(tm, AW),
                   pl.BlockSpec((nc, H, DH, DH), lambda i: (i, 0, 0, 0))],
        out_shape=(jax.ShapeDtypeStruct((t_len, D), BF16), jax.ShapeDtypeStruct((t_len, 2 * AW), F32),
                   jax.ShapeDtypeStruct((t_len, RW), BF16), jax.ShapeDtypeStruct((t_len, RW), BF16),
                   jax.ShapeDtypeStruct((t_len, RW), BF16), jax.ShapeDtypeStruct((t_len, 2 * RW), F32),
                   jax.ShapeDtypeStruct((t_len, RW), F32), jax.ShapeDtypeStruct((t_len, AW), F32),
                   jax.ShapeDtypeStruct((n_chunks, H, DH, DH), F32)),
        scratch_shapes=[pltpu.VMEM((H, DH, DH), F32)],
        compiler_params=_cparams(1),
    )(x, mod6, norm1, win, sgw, sgbt, sggain, cos, sin, rlf, scf)


def _fwd_b(q, k, v, of, gfb, ya, x, mod6, wout, rlb, scb, *, tm):
    t_len = x.shape[0]
    nt, nc = t_len // tm, tm // C

    def body(q_ref, k_ref, v_ref, of_ref, gfb_ref, ya_ref, x_ref, mod_ref, wout_ref, rlb_ref, scb_ref,
             ob_ref, ycat_ref, y_ref, x1_ref, rst_ref, r_scr):
        i = pl.program_id(0)

        @pl.when(i == 0)
        def _():
            r_scr[...] = scb_ref[...]

        lg = _log_sigmoid(rlb_ref[...])
        for h in range(H):
            dm, xi, zc, dec, _ = _decay_consts(lg[h:h + 1, 0:1], False)
            hs = slice(h * DH, (h + 1) * DH)
            for c in reversed(range(nc)):
                rs = slice(c * C, (c + 1) * C)
                qc, kc, vc = q_ref[rs, hs], k_ref[rs, hs], v_ref[rs, hs]
                s = r_scr[h]
                rst_ref[c, h] = s
                a = (_dot_nt(qc, kc) * dm).astype(BF16)
                ob_ref[rs, hs] = _dot(a, vc) + xi * _dot(qc, s.astype(BF16))
                kz = (kc.astype(F32) * zc).astype(BF16)
                r_scr[h] = dec * s + _dot_tn(kz, vc)

        gfb = gfb_ref[...]
        sf, _ = _silu_parts(gfb[:, :RW])
        sb, _ = _silu_parts(gfb[:, RW:])
        hnf, _ = _headnorm(of_ref[...])
        hnb, _ = _headnorm(ob_ref[...])
        yr = sf * hnf + sb * hnb
        ycat = jnp.concatenate([ya_ref[...], yr], axis=1).astype(BF16)
        ycat_ref[...] = ycat
        y = _dot(ycat, wout_ref[...])
        y_ref[...] = y.astype(BF16)
        x1_ref[...] = x_ref[...] + mod_ref[2:3, :] * y

    n_chunks = t_len // C
    rr = functools.partial(_rows_rev, nt=nt)
    return pl.pallas_call(
        body, name="fwd_b", grid=(nt,),
        in_specs=[rr(tm, RW), rr(tm, RW), rr(tm, RW), rr(tm, RW), rr(tm, 2 * RW), rr(tm, AW), rr(tm, D),
                  _whole((6, D)), _whole((D, D)), _whole((H, 128)), _whole((H, DH, DH))],
        out_specs=[rr(tm, RW), rr(tm, D), rr(tm, D), rr(tm, D),
                   pl.BlockSpec((nc, H, DH, DH), lambda i: (nt - 1 - i, 0, 0, 0))],
        out_shape=(jax.ShapeDtypeStruct((t_len, RW), F32), jax.ShapeDtypeStruct((t_len, D), BF16),
                   jax.ShapeDtypeStruct((t_len, D), BF16), jax.ShapeDtypeStruct((t_len, D), F32),
                   jax.ShapeDtypeStruct((n_chunks, H, DH, DH), F32)),
        scratch_shapes=[pltpu.VMEM((H, DH, DH), F32)],
        compiler_params=_cparams(1),
    )(q, k, v, of, gfb, ya, x, mod6, wout, rlb, scb)


def _ffn(x1, tgt, mod6, norm2, normf, wg, wu, wd, *, tm):
    t_len = x1.shape[0]
    nt = t_len // tm

    def body(x1_ref, tgt_ref, mod_ref, n2_ref, nf_ref, wg_ref, wu_ref, wd_ref,
             dx1_ref, h2_ref, da_ref, db_ref, hm_ref, df_ref, small_ref):
        i = pl.program_id(0)

        @pl.when(i == 0)
        def _():
            small_ref[...] = jnp.zeros_like(small_ref)

        sh2, sc2, g2 = mod_ref[3:4, :], mod_ref[4:5, :], mod_ref[5:6, :]
        n2, nf = n2_ref[...], nf_ref[...]
        x1 = x1_ref[...]
        r2 = lax.rsqrt(jnp.mean(x1 * x1, axis=-1, keepdims=True) + EPS)
        x1h = x1 * r2
        w2 = n2 * (1.0 + sc2)
        h2b = (x1h * w2 + sh2).astype(BF16)
        h2_ref[...] = h2b
        a = _dot_nt(h2b, wg_ref[...])
        b = _dot_nt(h2b, wu_ref[...])
        sa, dsa = _silu_parts(a)
        hmb = (sa * b).astype(BF16)
        hm_ref[...] = hmb
        f = _dot(hmb, wd_ref[...])
        x2 = x1 + g2 * f
        r3 = lax.rsqrt(jnp.mean(x2 * x2, axis=-1, keepdims=True) + EPS)
        x2h = x2 * r3
        diff = x2h * nf - tgt_ref[...]
        small_ref[5:6, :] += jnp.sum(diff * diff, axis=0, keepdims=True)
        dout = diff * (1.0 / D)
        small_ref[0:1, :] += jnp.sum(dout * x2h, axis=0, keepdims=True)
        dyg = dout * nf
        dx2 = r3 * (dyg - x2h * jnp.mean(dyg * x2h, axis=-1, keepdims=True))
        small_ref[1:2, :] += jnp.sum(dx2 * f, axis=0, keepdims=True)
        dfb = (dx2 * g2).astype(BF16)
        df_ref[...] = dfb
        dhm = _dot_nt(dfb, wd_ref[...])
        dbb = (dhm * sa).astype(BF16)
        dab = (dhm * b * dsa).astype(BF16)
        da_ref[...] = dab
        db_ref[...] = dbb
        dh2 = _dot(dab, wg_ref[...]) + _dot(dbb, wu_ref[...])
        small_ref[2:3, :] += jnp.sum(dh2, axis=0, keepdims=True)
        dhx = dh2 * x1h
        small_ref[3:4, :] += jnp.sum(dhx, axis=0, keepdims=True) * n2
        small_ref[4:5, :] += jnp.sum(dhx, axis=0, keepdims=True) * (1.0 + sc2)
        dyg2 = dh2 * w2
        dx1_ref[...] = dx2 + r2 * (dyg2 - x1h * jnp.mean(dyg2 * x1h, axis=-1, keepdims=True))

    return pl.pallas_call(
        body, name="ffn", grid=(nt,),
        in_specs=[_rows(tm, D), _rows(tm, D), _whole((6, D)), _whole((1, D)), _whole((1, D)),
                  _whole((DFF, D)), _whole((DFF, D)), _whole((DFF, D))],
        out_specs=[_rows(tm, D), _rows(tm, D), _rows(tm, DFF), _rows(tm, DFF), _rows(tm, DFF), _rows(tm, D),
                   _acc((8, D))],
        out_shape=(jax.ShapeDtypeStruct((t_len, D), F32), jax.ShapeDtypeStruct((t_len, D), BF16),
                   jax.ShapeDtypeStruct((t_len, DFF), BF16), jax.ShapeDtypeStruct((t_len, DFF), BF16),
                   jax.ShapeDtypeStruct((t_len, DFF), BF16), jax.ShapeDtypeStruct((t_len, D), BF16),
                   jax.ShapeDtypeStruct((8, D), F32)),
        compiler_params=_cparams(1),
    )(x1, tgt, mod6, norm2, normf, wg, wu, wd)


def _mid_bwd(dx1, y, of, ob, gfb, mod6, wout, *, tm):
    t_len = dx1.shape[0]
    nt = t_len // tm

    def body(dx1_ref, y_ref, of_ref, ob_ref, gfb_ref, mod_ref, wout_ref,
             dy_ref, dya_ref, dgfb_ref, dof_ref, dob_ref, dg1_ref):
        i = pl.program_id(0)

        @pl.when(i == 0)
        def _():
            dg1_ref[...] = jnp.zeros_like(dg1_ref)

        dx1 = dx1_ref[...]
        dg1_ref[0:1, :] += jnp.sum(dx1 * y_ref[...].astype(F32), axis=0, keepdims=True)
        dyb = (dx1 * mod_ref[2:3, :]).astype(BF16)
        dy_ref[...] = dyb
        dycat = _dot_nt(dyb, wout_ref[...])
        dya_ref[...] = dycat[:, :AW]
        dyr = dycat[:, AW:]
        gfb = gfb_ref[...]
        for o_ref, do_ref, lo in ((of_ref, dof_ref, 0), (ob_ref, dob_ref, RW)):
            sg, dsg = _silu_parts(gfb[:, lo:lo + RW])
            o = o_ref[...]
            hn, rs = _headnorm(o)
            dgfb_ref[:, lo:lo + RW] = (dyr * hn * dsg).astype(BF16)
            dhn = dyr * sg
            for h in range(H):
                hs = slice(h * DH, (h + 1) * DH)
                oh, dh = hn[:, hs], dhn[:, hs]
                do_ref[:, hs] = (rs[h] * (dh - oh * jnp.mean(dh * oh, axis=-1, keepdims=True))).astype(BF16)

    return pl.pallas_call(
        body, name="mid_bwd", grid=(nt,),
        in_specs=[_rows(tm, D), _rows(tm, D), _rows(tm, RW), _rows(tm, RW), _rows(tm, 2 * RW),
                  _whole((6, D)), _whole((D, D))],
        out_specs=[_rows(tm, D), _rows(tm, AW), _rows(tm, 2 * RW), _rows(tm, RW), _rows(tm, RW), _acc((8, D))],
        out_shape=(jax.ShapeDtypeStruct((t_len, D), BF16), jax.ShapeDtypeStruct((t_len, AW), F32),
                   jax.ShapeDtypeStruct((t_len, 2 * RW), BF16), jax.ShapeDtypeStruct((t_len, RW), BF16),
                   jax.ShapeDtypeStruct((t_len, RW), BF16), jax.ShapeDtypeStruct((8, D), F32)),
        compiler_params=_cparams(1),
    )(dx1, y, of, ob, gfb, mod6, wout)


def _ret_bwd_dir(q_ref, k_ref, v_ref, do_ref, st_ref, dq_ref, dk_ref, dv_ref, ds_scr, dlg_scr, lg, forward, nc, row0):
    order = reversed(range(nc)) if forward else range(nc)
    order = list(order)
    for h in range(H):
        dm, xi, zc, dec, ic = _decay_consts(lg[h:h + 1, 0:1], forward)
        hs = slice(h * DH, (h + 1) * DH)
        if forward:
            w_qi, w_qc, w_ki, w_ks = ic, ic + 1.0, -ic, (C - 1.0) - ic
        else:
            w_qi, w_qc, w_ki, w_ks = -ic, C - ic, ic, ic
        acc = jnp.zeros((1, DH), F32)
        for c in order:
            rs = slice(c * C, (c + 1) * C)
            qc, kc, vc, doc = q_ref[rs, hs], k_ref[rs, hs], v_ref[rs, hs], do_ref[rs, hs]
            s = st_ref[c, h]
            dsn = ds_scr[h]
            sb, dsnb = s.astype(BF16), dsn.astype(BF16)
            qf, kf = qc.astype(F32), kc.astype(F32)
            a = (_dot_nt(qc, kc) * dm).astype(BF16)
            da = (_dot_nt(doc, vc) * dm).astype(BF16)
            xdo = (xi * doc.astype(F32)).astype(BF16)
            kz = (kf * zc).astype(BF16)
            vz = (vc.astype(F32) * zc).astype(BF16)
            dq_i = _dot(da, kc)
            dq_c = _dot_nt(xdo, sb)
            dk_i = _dot_tn(da, qc)
            dk_s = _dot_nt(vz, dsnb)
            dq_ref[rs, hs] = dq_i + dq_c
            dk_ref[rs, hs] = dk_i + dk_s
            dv_ref[rs, hs] = _dot_tn(a, doc) + _dot(kz, dsnb)
            rowterm = qf * (w_qi * dq_i + w_qc * dq_c) + kf * (w_ki * dk_i + w_ks * dk_s)
            acc = acc + jnp.sum(rowterm, axis=0, keepdims=True)
            acc = acc + (float(C) * dec) * jnp.sum(s * dsn, axis=0, keepdims=True)
            ds_scr[h] = _dot_tn((xi * qf).astype(BF16), doc) + dec * dsn
        dlg_scr[row0 + h:row0 + h + 1, :] += acc


def _ret_bwd(q, k, v, dof, dob, sst, rst, rlf, rlb, *, tm):
    t_len = q.shape[0]
    nt, nc = t_len // tm, tm // C

    def body(qf_ref, kf_ref, vf_ref, dof_ref, sst_ref, qb_ref, kb_ref, vb_ref, dob_ref, rst_ref, rlf_ref, rlb_ref,
             dqf_ref, dkf_ref, dvf_ref, dqb_ref, dkb_ref, dvb_ref, dscf_ref, dscb_ref, dlg_ref,
             dsf_scr, dsb_scr, dlg_scr):
        i = pl.program_id(0)

        @pl.when(i == 0)
        def _():
            dsf_scr[...] = jnp.zeros_like(dsf_scr)
            dsb_scr[...] = jnp.zeros_like(dsb_scr)
            dlg_scr[...] = jnp.zeros_like(dlg_scr)

        lgf = _log_sigmoid(rlf_ref[...])
        lgb = _log_sigmoid(rlb_ref[...])
        _ret_bwd_dir(qf_ref, kf_ref, vf_ref, dof_ref, sst_ref, dqf_ref, dkf_ref, dvf_ref, dsf_scr, dlg_scr, lgf, True, nc, 0)
        _ret_bwd_dir(qb_ref, kb_ref, vb_ref, dob_ref, rst_ref, dqb_ref, dkb_ref, dvb_ref, dsb_scr, dlg_scr, lgb, False, nc, H)

        @pl.when(i == nt - 1)
        def _():
            dscf_ref[...] = dsf_scr[...]
            dscb_ref[...] = dsb_scr[...]
            tot = jnp.broadcast_to(jnp.sum(dlg_scr[...], axis=1, keepdims=True), (8, 128))
            ri = lax.broadcasted_iota(jnp.int32, (8, 128), 0)
            li = lax.broadcasted_iota(jnp.int32, (8, 128), 1)
            dlg_ref[...] = jnp.zeros_like(dlg_ref)
            dlg_ref[0:1, 0:128] = jnp.sum(jnp.where(ri == li, tot, 0.0), axis=0, keepdims=True)
            dlg_ref[1:2, 0:128] = jnp.sum(jnp.where(ri - H == li, tot, 0.0), axis=0, keepdims=True)

    rr = functools.partial(_rows_rev, nt=nt)
    st_f = pl.BlockSpec((nc, H, DH, DH), lambda i: (nt - 1 - i, 0, 0, 0))
    st_b = pl.BlockSpec((nc, H, DH, DH), lambda i: (i, 0, 0, 0))
    tok = jax.ShapeDtypeStruct((t_len, RW), F32)
    st = jax.ShapeDtypeStruct((H, DH, DH), F32)
    return pl.pallas_call(
        body, name="ret_bwd", grid=(nt,),
        in_specs=[rr(tm, RW), rr(tm, RW), rr(tm, RW), rr(tm, RW), st_f,
                  _rows(tm, RW), _rows(tm, RW), _rows(tm, RW), _rows(tm, RW), st_b,
                  _whole((H, 128)), _whole((H, 128))],
        out_specs=[rr(tm, RW), rr(tm, RW), rr(tm, RW), _rows(tm, RW), _rows(tm, RW), _rows(tm, RW),
                   _acc((H, DH, DH)), _acc((H, DH, DH)), _acc((8, D))],
        out_shape=(tok, tok, tok, tok, tok, tok, st, st, jax.ShapeDtypeStruct((8, D), F32)),
        scratch_shapes=[pltpu.VMEM((H, DH, DH), F32), pltpu.VMEM((H, DH, DH), F32), pltpu.VMEM((8, 128), F32)],
        compiler_params=_cparams(1),
    )(q, k, v, dof, sst, q, k, v, dob, rst, rlf, rlb)


def _in_bwd(x, zuv, dya, dqf, dkf, dvf, dqb, dkb, dvb, dgfb, dx1, cos, sin, mod6, norm1, win, sgw, sgbt, sggain, *, tm):
    t_len = x.shape[0]
    nt, nc = t_len // tm, tm // C

    def body(x_ref, zuv_ref, dya_ref, dqf_ref, dkf_ref, dvf_ref, dqb_ref, dkb_ref, dvb_ref, dgfb_ref, dx1_ref,
             cos_ref, sin_ref, mod_ref, n1_ref, win_ref, sgw_ref, sgbt_ref, sgg_ref,
             gx_ref, dz_ref, small_ref, dsgw_ref, dsgbt_ref):
        i = pl.program_id(0)

        @pl.when(i == 0)
        def _():
            small_ref[...] = jnp.zeros_like(small_ref)
            dsgw_ref[...] = jnp.zeros_like(dsgw_ref)
            dsgbt_ref[...] = jnp.zeros_like(dsgbt_ref)

        zuv = zuv_ref[...]
        _, (ua, dgu, dgv, mixed, vn_b, vah_l, rv_l) = _sg_forward(zuv[:, :AW], zuv[:, AW:], sgw_ref, sgbt_ref, sgg_ref, nc)
        dya = dya_ref[...]
        dz_ref[:, 0:AW] = (dya * mixed * dgu).astype(BF16)
        dmixed = dya * ua
        gain = sgg_ref[...]
        for g in range(G):
            gs = slice(g * DH, (g + 1) * DH)
            dmg = dmixed[:, gs]
            dmb = dmg.astype(BF16)
            wgt = sgw_ref[g].astype(BF16)
            dsw = jnp.zeros((C, C), F32)
            dsb = jnp.zeros((C, DH), F32)
            rows = []
            for c in range(nc):
                rs = slice(c * C, (c + 1) * C)
                dsw = dsw + _dot_nt(dmb[rs, :], vn_b[g][rs, :])
                dsb = dsb + dmg[rs, :]
                rows.append(_dot_tn(wgt, dmb[rs, :]))
            dsgw_ref[g] += dsw
            dsgbt_ref[g] += dsb
            dvn = jnp.concatenate(rows, axis=0) if nc > 1 else rows[0]
            vah, rv = vah_l[g], rv_l[g]
            small_ref[3:4, gs] += jnp.sum(dvn * vah, axis=0, keepdims=True)
            dyg = dvn * gain[:, gs]
            dva = rv * (dyg - vah * jnp.mean(dyg * vah, axis=-1, keepdims=True))
            dz_ref[:, AW + g * DH:AW + (g + 1) * DH] = (dva * dgv[:, gs]).astype(BF16)

        cos = jnp.tile(cos_ref[...], (1, H))
        nsins = -jnp.tile(sin_ref[...], (1, H))
        dz_ref[:, 1024:1536] = _rope(dqf_ref[...] + dqb_ref[...], cos, nsins).astype(BF16)
        dz_ref[:, 1536:2048] = (_rope(dkf_ref[...] + dkb_ref[...], cos, nsins) * KSCALE).astype(BF16)
        dz_ref[:, 2048:2560] = (dvf_ref[...] + dvb_ref[...]).astype(BF16)
        dz_ref[:, 2560:3584] = dgfb_ref[...]

        dhx = _dot(dz_ref[...], win_ref[...])
        x = x_ref[...]
        r = lax.rsqrt(jnp.mean(x * x, axis=-1, keepdims=True) + EPS)
        xh = x * r
        n1, sc1 = n1_ref[...], mod_ref[1:2, :]
        small_ref[0:1, :] += jnp.sum(dhx, axis=0, keepdims=True)
        dhxx = jnp.sum(dhx * xh, axis=0, keepdims=True)
        small_ref[1:2, :] += dhxx * n1
        small_ref[2:3, :] += dhxx * (1.0 + sc1)
        dyg = dhx * (n1 * (1.0 + sc1))
        gx_ref[...] = dx1_ref[...] + r * (dyg - xh * jnp.mean(dyg * xh, axis=-1, keepdims=True))

        @pl.when(i == nt - 1)
        def _():
            ri = lax.broadcasted_iota(jnp.int32, (C, DH), 0)
            li = lax.broadcasted_iota(jnp.int32, (C, DH), 1)
            for g in range(G):
                tot = jnp.broadcast_to(jnp.sum(dsgbt_ref[g], axis=1, keepdims=True), (C, DH))
                small_ref[4 + g:5 + g, 0:DH] = jnp.sum(jnp.where(ri == li, tot, 0.0), axis=0, keepdims=True)

    tok = functools.partial(_rows, tm)
    return pl.pallas_call(
        body, name="in_bwd", grid=(nt,),
        in_specs=[tok(D), tok(2 * AW), tok(AW), tok(RW), tok(RW), tok(RW), tok(RW), tok(RW), tok(RW),
                  tok(2 * RW), tok(D), tok(DH), tok(DH), _whole((6, D)), _whole((1, D)), _whole((INC, D)),
                  _whole((G, C, C)), _whole((G, C, 128)), _whole((1, AW))],
        out_specs=[tok(D), tok(INC), _acc((8, D)), _acc((G, C, C))],
        out_shape=(jax.ShapeDtypeStruct((t_len, D), F32), jax.ShapeDtypeStruct((t_len, INC), BF16),
                   jax.ShapeDtypeStruct((8, D), F32), jax.ShapeDtypeStruct((G, C, C), F32)),
        scratch_shapes=[pltpu.VMEM((G, C, 128), F32)],
        compiler_params=_cparams(1),
    )(x, zuv, dya, dqf, dkf, dvf, dqb, dkb, dvb, dgfb, dx1, cos, sin, mod6, norm1, win, sgw, sgbt, sggain)


def _tn_matmul(a, b, *, bm, bt, name, init=None, init_rows=None, after=()):
    t_len, m = a.shape
    n = b.shape[1]
    ni, ntt = m // bm, t_len // bt
    has_init = init is not None
    assert not has_init or bm == m

    def body(*refs):
        o_ref, ob_ref = refs[-2:]
        a_ref, b_ref = refs[:2]
        init_ref = refs[2] if has_init else None
        t = pl.program_id(1)

        @pl.when(t == 0)
        def _():
            o_ref[...] = jnp.zeros_like(o_ref)
            if has_init:
                o_ref[init_rows[0]:init_rows[1], :] = init_ref[...]

        o_ref[...] += _dot_tn(a_ref[...], b_ref[...])

        @pl.when(t == ntt - 1)
        def _():
            ob_ref[...] = o_ref[...].astype(BF16)

    in_specs = [pl.BlockSpec((bt, bm), lambda i, t: (t, i)), pl.BlockSpec((bt, n), lambda i, t: (t, 0))]
    args = [a, b]
    if has_init:
        in_specs.append(pl.BlockSpec(init.shape, lambda i, t: (0, 0), pipeline_mode=pl.Buffered(1)))
        args.append(init)
    for arr in after:
        in_specs.append(pl.BlockSpec(memory_space=pl.ANY))
        args.append(arr)
    out_spec = pl.BlockSpec((bm, n), lambda i, t: (i, 0))
    return pl.pallas_call(
        body, name=name, grid=(ni, ntt),
        in_specs=in_specs,
        out_specs=[out_spec, out_spec],
        out_shape=(jax.ShapeDtypeStruct((m, n), F32), jax.ShapeDtypeStruct((m, n), BF16)),
        compiler_params=_cparams(2),
    )(*args)


def _ctx_bwd(ctx, hc, kvc, cmod6, norm1, win, rlf, rlb, dscf, dscb, dlg_in):
    lc = ctx.shape[0]

    def body(ctx_ref, hc_ref, kvc_ref, cmod_ref, n1_ref, win_ref, rlf_ref, rlb_ref, dscf_ref, dscb_ref, dlgin_ref,
             dwin_ref, small_ref, dlg_ref):
        k = kvc_ref[:, :RW]
        v = kvc_ref[:, RW:]
        t = lax.broadcasted_iota(jnp.int32, (lc, 1), 0).astype(F32)
        lgf = _log_sigmoid(rlf_ref[...])
        lgb = _log_sigmoid(rlb_ref[...])
        dks, dvs = [], []
        lane = lax.broadcasted_iota(jnp.int32, (1, 128), 1)
        row_f = jnp.zeros((1, 128), F32)
        row_b = jnp.zeros((1, 128), F32)
        for h in range(H):
            hs = slice(h * DH, (h + 1) * DH)
            wf = jnp.exp(lgf[h:h + 1, 0:1] * (lc - 1.0 - t))
            wb = jnp.exp(lgb[h:h + 1, 0:1] * t)
            kh, vhb = k[:, hs], v[:, hs].astype(BF16)
            dsf, dsb = dscf_ref[h].astype(BF16), dscb_ref[h].astype(BF16)
            dkf = wf * _dot_nt(vhb, dsf)
            dkb = wb * _dot_nt(vhb, dsb)
            dvs.append(_dot((kh * wf).astype(BF16), dsf) + _dot((kh * wb).astype(BF16), dsb))
            dks.append((dkf + dkb) * KSCALE)
            lf = jnp.sum((lc - 1.0 - t) * kh * dkf, axis=0, keepdims=True)
            lb = jnp.sum(t * kh * dkb, axis=0, keepdims=True)
            row_f = row_f + jnp.where(lane == h, jnp.sum(lf, axis=1, keepdims=True), 0.0)
            row_b = row_b + jnp.where(lane == h, jnp.sum(lb, axis=1, keepdims=True), 0.0)
        dlg_ref[...] = dlgin_ref[...]
        dlg_ref[0:1, 0:128] += row_f
        dlg_ref[1:2, 0:128] += row_b
        dkv = jnp.concatenate(dks + dvs, axis=1).astype(BF16)
        dhc = _dot(dkv, win_ref[KV_LO:KV_HI, :])
        dwin_ref[...] = _dot_tn(dkv, hc_ref[...])
        x = ctx_ref[...]
        r = lax.rsqrt(jnp.mean(x * x, axis=-1, keepdims=True) + EPS)
        xh = x * r
        small_ref[...] = jnp.zeros_like(small_ref)
        small_ref[0:1, :] = jnp.sum(dhc, axis=0, keepdims=True)
        dhxx = jnp.sum(dhc * xh, axis=0, keepdims=True)
        small_ref[1:2, :] = dhxx * n1_ref[...]
        small_ref[2:3, :] = dhxx * (1.0 + cmod_ref[1:2, :])

    return pl.pallas_call(
        body, name="ctx_bwd",
        out_shape=(jax.ShapeDtypeStruct((2 * RW, D), F32), jax.ShapeDtypeStruct((8, D), F32),
                   jax.ShapeDtypeStruct((8, D), F32)),
        compiler_params=_cparams(),
    )(ctx, hc, kvc, cmod6, norm1, win, rlf, rlb, dscf, dscb, dlg_in)


def _adam_math(w, g, m, v):
    m = ADAM_B1 * m + (1.0 - ADAM_B1) * g
    v = ADAM_B2 * v + (1.0 - ADAM_B2) * (g * g)
    m_hat = m / (1.0 - ADAM_B1 ** ADAM_STEP)
    v_hat = v / (1.0 - ADAM_B2 ** ADAM_STEP)
    delta = -ADAM_LR * (m_hat / (jnp.sqrt(v_hat) + ADAM_EPS) + ADAM_WD * w)
    return delta, m, v


def _adam(w, g, m, v, *, name, br=None):
    r, c = w.shape
    br = r if br is None else br

    def body(w_ref, g_ref, m_ref, v_ref, d_ref, mo_ref, vo_ref):
        d_ref[...], mo_ref[...], vo_ref[...] = _adam_math(w_ref[...], g_ref[...], m_ref[...], v_ref[...])

    spec = pl.BlockSpec((br, c), lambda i: (i, 0))
    sh = jax.ShapeDtypeStruct((r, c), F32)
    return pl.pallas_call(
        body, name=name, grid=(r // br,), in_specs=[spec] * 4, out_specs=[spec] * 3, out_shape=(sh, sh, sh),
        compiler_params=_cparams(1),
    )(w, g, m, v)


def _place():
    x, y, c = lax.axis_index("x"), lax.axis_index("y"), lax.axis_index("c")
    return x, y, c, 4 * x + 2 * y + c


def _flip(x, y, c, k):
    px = 1 - x if (k >> 2) & 1 else x
    py = 1 - y if (k >> 1) & 1 else y
    pc = 1 - c if k & 1 else c
    return (px, py, pc), 4 * px + 2 * py + pc


def _gather_direct(buf_ref, send_sems, recv_sems, sem0=0):
    x, y, c, me = _place()
    sends = []
    for k in range(1, NDEV):
        dev, _ = _flip(x, y, c, k)
        cp = pltpu.make_async_remote_copy(
            src_ref=buf_ref.at[me], dst_ref=buf_ref.at[me],
            send_sem=send_sems.at[sem0 + k - 1], recv_sem=recv_sems.at[sem0 + k - 1],
            device_id=dev, device_id_type=MESH)
        cp.start()
        sends.append(cp)
    for k in range(1, NDEV):
        dev, idx = _flip(x, y, c, k)
        pltpu.make_async_remote_copy(
            src_ref=buf_ref.at[idx], dst_ref=buf_ref.at[idx],
            send_sem=send_sems.at[sem0 + k - 1], recv_sem=recv_sems.at[sem0 + k - 1],
            device_id=dev, device_id_type=MESH).wait_recv()
    for cp in sends:
        cp.wait_send()


def _mod_gather(c_row, cctx_row, wmod, bmod):
    ncol = wmod.shape[1]

    def body(c_ref, cctx_ref, wmod_ref, bmod_ref, mod_ref, cs_ref, cbuf, pbuf, send_sems, recv_sems):
        _, _, _, me = _place()
        cbuf[me] = jnp.broadcast_to(c_ref[...], (8, D))
        _gather_direct(cbuf, send_sems, recv_sems, 0)
        row = lax.broadcasted_iota(jnp.int32, (16, D), 0)
        call = jnp.where(row == 8, jnp.broadcast_to(cctx_ref[...], (16, D)), 0.0)
        for d in range(NDEV):
            call = jnp.where(row == d, jnp.concatenate([cbuf[d], cbuf[d]], axis=0), call)
        cs = call * _sigmoid(call)
        cs_ref[...] = cs
        pbuf[me] = _dot(cs.astype(BF16), wmod_ref[...].astype(BF16))
        _gather_direct(pbuf, send_sems, recv_sems, NDEV - 1)
        for d in range(NDEV):
            mod_ref[:, d * ncol:(d + 1) * ncol] = pbuf[d] + bmod_ref[:, d * ncol:(d + 1) * ncol]

    return pl.pallas_call(
        body, name="mod_gather",
        out_shape=(jax.ShapeDtypeStruct((16, 6 * D), F32), jax.ShapeDtypeStruct((16, D), F32)),
        scratch_shapes=[pltpu.VMEM((NDEV, 8, D), F32), pltpu.VMEM((NDEV, 16, ncol), F32),
                        pltpu.SemaphoreType.DMA((2 * (NDEV - 1),)), pltpu.SemaphoreType.DMA((2 * (NDEV - 1),))],
        compiler_params=_cparams(),
    )(c_row, cctx_row, wmod, bmod)


def _weight_gather(shards):
    n = len(shards)

    def body(*refs):
        in_refs, out_refs = refs[:n], refs[n:2 * n]
        cast_refs = refs[2 * n:3 * n]
        send_sems, recv_sems, local_sems = refs[3 * n:]
        x, y, c, me = _place()
        sib = (x, y, 1 - c)
        chips = [(1 - x, y), (x, 1 - y), (1 - x, 1 - y)]

        def blk(px, py, pc):
            return 4 * px + 2 * py + pc

        def copy(w, k, block, to, src=None):
            dst = out_refs[w].at[block]
            return pltpu.make_async_remote_copy(
                src_ref=dst if src is None else src, dst_ref=dst,
                send_sem=send_sems.at[w, k], recv_sem=recv_sems.at[w, k], device_id=to, device_id_type=MESH)

        first, passed, mine = [], [], []
        for w in range(n):
            cast_refs[w][...] = in_refs[w][...].astype(BF16)
            m = pltpu.make_async_copy(cast_refs[w], out_refs[w].at[me], local_sems.at[w])
            m.start()
            mine.append(m)
            cps = [copy(w, 0, me, sib, src=cast_refs[w])]
            cps += [copy(w, 1 + j, me, (*chip, c), src=cast_refs[w]) for j, chip in enumerate(chips)]
            for cp in cps:
                cp.start()
            first += cps
        for w in range(n):
            for j, chip in enumerate(chips):
                b = blk(*chip, c)
                copy(w, 1 + j, b, (x, y, c)).wait_recv()
                fw = copy(w, 4 + j, b, sib)
                fw.start()
                passed.append(fw)
        for w in range(n):
            copy(w, 0, blk(x, y, 1 - c), (x, y, c)).wait_recv()
            for j, chip in enumerate(chips):
                copy(w, 4 + j, blk(*chip, 1 - c), (x, y, c)).wait_recv()
        for cp in first + passed:
            cp.wait_send()
        for m in mine:
            m.wait()

    vm = pl.BlockSpec(memory_space=pltpu.VMEM)
    hbm = pl.BlockSpec(memory_space=pltpu.HBM)
    return pl.pallas_call(
        body, name="weight_gather",
        in_specs=[vm] * n, out_specs=[hbm] * n,
        out_shape=tuple(jax.ShapeDtypeStruct((NDEV,) + s.shape, BF16) for s in shards),
        scratch_shapes=[pltpu.VMEM(s.shape, BF16) for s in shards]
        + [pltpu.SemaphoreType.DMA((n, 7)), pltpu.SemaphoreType.DMA((n, 7)), pltpu.SemaphoreType.DMA((n,))],
        compiler_params=_cparams(),
    )(*shards)


_HBM = pl.BlockSpec(memory_space=pltpu.HBM)
_SEMS = pl.BlockSpec(memory_space=pltpu.SEMAPHORE)
_EFFECT = pltpu.SideEffectType.DATAFLOW_SIDE_EFFECTING


def _exchange_copy(src_refs, land_refs, send_sems, recv_sems, w, k, scatter, landing_slot_is_mine):
    x, y, c, me = _place()
    dev, pidx = _flip(x, y, c, k)
    src = src_refs[w].at[pidx] if scatter else src_refs[w]
    slot = me if landing_slot_is_mine else pidx
    return pltpu.make_async_remote_copy(
        src_ref=src, dst_ref=land_refs[w].at[slot],
        send_sem=send_sems.at[w * (NDEV - 1) + k - 1], recv_sem=recv_sems.at[w * (NDEV - 1) + k - 1],
        device_id=dev, device_id_type=MESH)


def _exchange_start(srcs, *, scatter, name):
    n = len(srcs)
    lands = [lax.empty((NDEV,) + tuple(s.shape[-2:]), s.dtype) for s in srcs]

    def body(*refs):
        src_refs, land_refs = refs[:n], refs[n:2 * n]
        send_sems, recv_sems = refs[2 * n], refs[2 * n + 1]
        token = refs[-1]
        for w in range(n):
            for k in range(1, NDEV):
                _exchange_copy(src_refs, land_refs, send_sems, recv_sems, w, k, scatter, True).start()
        token[...] = jnp.zeros_like(token)

    arrs = list(srcs) + lands
    out_shape = ([pltpu.SemaphoreType.DMA((n * (NDEV - 1),)), pltpu.SemaphoreType.DMA((n * (NDEV - 1),))]
                 + [pltpu.HBM(a.shape, a.dtype) for a in arrs] + [jax.ShapeDtypeStruct((8, 128), F32)])
    res = pl.pallas_call(
        body, name=name, out_shape=tuple(out_shape),
        in_specs=[_HBM] * (2 * n),
        out_specs=tuple([_SEMS, _SEMS] + [_HBM] * (2 * n) + [pl.BlockSpec(memory_space=pltpu.VMEM)]),
        input_output_aliases={i: 2 + i for i in range(2 * n)},
        compiler_params=pltpu.CompilerParams(has_side_effects=_EFFECT),
    )(*[pltpu.with_memory_space_constraint(a, pltpu.HBM) for a in arrs])
    return res[:-1], res[-1]


def _exchange_wait(handles, after, *, scatter, name):
    n = (len(handles) - 2) // 2
    na = len(after)

    def body(*refs):
        src_refs, land_refs = refs[:n], refs[n:2 * n]
        send_sems, recv_sems = refs[2 * n], refs[2 * n + 1]
        for w in range(n):
            for k in range(1, NDEV):
                cp = _exchange_copy(src_refs, land_refs, send_sems, recv_sems, w, k, scatter, False)
                cp.wait_send()
                cp.wait_recv()

    arrs = handles[2:]
    res = pl.pallas_call(
        body, name=name, out_shape=tuple(pltpu.HBM(a.shape, a.dtype) for a in arrs),
        in_specs=[_HBM] * (2 * n) + [_SEMS, _SEMS] + [_HBM] * na,
        out_specs=tuple([_HBM] * (2 * n)),
        input_output_aliases={i: i for i in range(2 * n)},
        compiler_params=pltpu.CompilerParams(has_side_effects=_EFFECT),
    )(*arrs, handles[0], handles[1], *[pltpu.with_memory_space_constraint(a, pltpu.HBM) for a in after])
    return res[:n], res[n:]


def _cast_bf16(arrs, *, name, after=()):
    n = len(arrs)

    def body(*refs):
        for i_ref, o_ref in zip(refs[:n], refs[n + len(after):]):
            o_ref[...] = i_ref[...].astype(BF16)

    return pl.pallas_call(
        body, name=name, out_shape=tuple(jax.ShapeDtypeStruct(a.shape, BF16) for a in arrs),
        in_specs=[pl.BlockSpec(memory_space=pltpu.VMEM)] * n + [pl.BlockSpec(memory_space=pl.ANY)] * len(after),
        compiler_params=_cparams())(*arrs, *after)


def _rs_adam(me_arr, own, land, w, m, v, *, name):
    def body(me_ref, own_ref, land_ref, w_ref, m_ref, v_ref, g_ref, d_ref, mo_ref, vo_ref):
        me = me_ref[0]
        acc = jnp.zeros(own_ref.shape, F32)
        for p in range(NDEV):
            acc = acc + jnp.where(p == me, own_ref[...], land_ref[p].astype(F32))
        g_ref[...] = acc
        d_ref[...], mo_ref[...], vo_ref[...] = _adam_math(w_ref[...], acc, m_ref[...], v_ref[...])

    sh = jax.ShapeDtypeStruct(own.shape, F32)
    vm = pl.BlockSpec(memory_space=pltpu.VMEM)
    return pl.pallas_call(
        body, name=name, out_shape=(sh, sh, sh, sh),
        in_specs=[pl.BlockSpec(memory_space=pltpu.SMEM)] + [vm] * 5,
        compiler_params=_cparams(),
    )(me_arr, own, land, w, m, v)


ROW_F, ROW_I, ROW_C, ROW_G1, ROW_LG = 0, 8, 16, 24, 32
PACK_ROWS = 40


def _small_reduce(pack, dsgw, wmod, cctx_row, m_row, v_row):
    ncol = wmod.shape[1]

    def body(pack_ref, sgw_ref, wmod_ref, c_ref, m_ref, v_ref,
             sum_ref, all_ref, sgw_sum_ref, g_ref, d_ref, mo_ref, vo_ref, sgw_all, cbuf, send_sems, recv_sems):
        _, _, _, me = _place()
        all_ref[me] = pack_ref[...]
        sgw_all[me] = sgw_ref[...]
        _gather_direct(all_ref, send_sems, recv_sems, 0)
        _gather_direct(sgw_all, send_sems, recv_sems, NDEV - 1)
        acc = all_ref[0]
        acc_w = sgw_all[0]
        for d in range(1, NDEV):
            acc = acc + all_ref[d]
            acc_w = acc_w + sgw_all[d]
        sum_ref[...] = acc
        sgw_sum_ref[...] = acc_w

        zero = jnp.zeros((1, D), F32)
        dcmod = jnp.concatenate([acc[ROW_C:ROW_C + 1], acc[ROW_C + 1:ROW_C + 2], zero, zero, zero, zero], axis=1)
        mine = jnp.zeros((1, ncol), F32)
        for d in range(NDEV):
            mine = mine + jnp.where(d == me, dcmod[:, d * ncol:(d + 1) * ncol], 0.0)
        cbuf[me] = _dot_nt(jnp.broadcast_to(mine, (8, ncol)).astype(BF16), wmod_ref[...].astype(BF16))
        _gather_direct(cbuf, send_sems, recv_sems, 2 * (NDEV - 1))
        tot = cbuf[0]
        for d in range(1, NDEV):
            tot = tot + cbuf[d]
        cc = c_ref[...]
        _, dsilu = _silu_parts(cc)
        g = tot[0:1, :] * dsilu
        g_ref[...] = g
        d_ref[...], mo_ref[...], vo_ref[...] = _adam_math(cc, g, m_ref[...], v_ref[...])

    row = jax.ShapeDtypeStruct((1, D), F32)
    return pl.pallas_call(
        body, name="small_reduce",
        out_shape=(jax.ShapeDtypeStruct((PACK_ROWS, D), F32), jax.ShapeDtypeStruct((NDEV, PACK_ROWS, D), F32),
                   jax.ShapeDtypeStruct((G, C, C), F32), row, row, row, row),
        scratch_shapes=[pltpu.VMEM((NDEV, G, C, C), F32), pltpu.VMEM((NDEV, 8, D), F32),
                        pltpu.SemaphoreType.DMA((3 * (NDEV - 1),)), pltpu.SemaphoreType.DMA((3 * (NDEV - 1),))],
        compiler_params=_cparams(),
    )(pack, dsgw, wmod, cctx_row, m_row, v_row)


def _adam_small(s, sgw_g, params):
    names = ["norm1", "norm2", "norm_f", "sg_gain", "sg_b", "ret_logit_f", "ret_logit_b", "b_mod", "sg_w"]
    flat = [a for n in names for a in params[n]]

    def body(*refs):
        s_ref, sgw_ref = refs[0], refs[1]
        p_refs = refs[2:2 + 3 * len(names)]
        o_refs = refs[2 + 3 * len(names):]
        loss_ref = o_refs[-1]

        def row(r):
            return s_ref[r:r + 1, :]

        grads = {
            "norm1": row(ROW_I + 2) + row(ROW_C + 2),
            "norm2": row(ROW_F + 4),
            "norm_f": row(ROW_F + 0),
            "sg_gain": s_ref[ROW_I + 3:ROW_I + 4, 0:AW],
            "sg_b": s_ref[ROW_I + 4:ROW_I + 8, 0:DH],
            "sg_w": sgw_ref[...],
        }
        for j, n in enumerate(names):
            w_ref, m_ref, v_ref = p_refs[3 * j:3 * j + 3]
            g_ref, d_ref, mo_ref, vo_ref = o_refs[4 * j:4 * j + 4]
            w = w_ref[...]
            if n in ("ret_logit_f", "ret_logit_b"):
                r = ROW_LG + (0 if n == "ret_logit_f" else 1)
                g = s_ref[r:r + 1, 0:H] * _sigmoid(-w)
            elif n == "b_mod":
                rows = [row(ROW_I + 0) + row(ROW_C + 0), row(ROW_I + 1) + row(ROW_C + 1), row(ROW_G1),
                        row(ROW_F + 2), row(ROW_F + 3), row(ROW_F + 1)]
                g = jnp.concatenate(rows, axis=1)
            else:
                g = grads[n]
            g_ref[...] = g
            d_ref[...], mo_ref[...], vo_ref[...] = _adam_math(w, g, m_ref[...], v_ref[...])
        loss_ref[...] = jnp.full((1, 1), 0.5 / D, F32) * jnp.sum(row(ROW_F + 5), axis=1, keepdims=True)

    out_shape = []
    for n in names:
        w = params[n][0]
        out_shape += [jax.ShapeDtypeStruct(w.shape, F32)] * 4
    out_shape.append(jax.ShapeDtypeStruct((1, 1), F32))
    outs = pl.pallas_call(body, name="adam_small", out_shape=tuple(out_shape), compiler_params=_cparams())(
        s, sgw_g, *flat)
    return {n: tuple(outs[4 * j:4 * j + 4]) for j, n in enumerate(names)}, outs[-1]


def _wmod_grad(cs_all, dmod_cols, wmod, m, v):
    ncol = wmod.shape[1]

    def body(cs_ref, dm_ref, w_ref, m_ref, v_ref, g_ref, d_ref, mo_ref, vo_ref):
        g = _dot_tn(cs_ref[...].astype(BF16), dm_ref[...].astype(BF16))
        g_ref[...] = g
        d_ref[...], mo_ref[...], vo_ref[...] = _adam_math(w_ref[...], g, m_ref[...], v_ref[...])

    sh = jax.ShapeDtypeStruct((D, ncol), F32)
    return pl.pallas_call(
        body, name="wmod_grad", out_shape=(sh, sh, sh, sh),
        compiler_params=_cparams(),
    )(cs_all, dmod_cols, wmod, m, v)


def _rope_tables(t_len):
    n_freq = DH // 4
    inv = (ROPE_BASE ** (-np.arange(n_freq, dtype=np.float32) / n_freq)).astype(np.float32)
    tok = np.arange(t_len)
    rows = (tok // GRID_W).astype(np.float32)
    cols = (tok % GRID_W).astype(np.float32)
    ang_r = rows[:, None] * inv[None, :]
    ang_c = cols[:, None] * inv[None, :]
    cos = np.concatenate([np.cos(ang_r)] * 2 + [np.cos(ang_c)] * 2, axis=1).astype(np.float32)
    sin = np.concatenate([-np.sin(ang_r), np.sin(ang_r), -np.sin(ang_c), np.sin(ang_c)], axis=1).astype(np.float32)
    return jnp.asarray(cos), jnp.asarray(sin)


def _pad_row(v, width=D):
    v = v.reshape(1, -1)
    return jnp.pad(v, ((0, 0), (0, width - v.shape[1])))


def _local_step(x, tgt, ctx, mod6, cmod6, norm1, norm2, normf, win, wout, ffn_weights, sgw, sgb, sggain, rlf, rlb,
                *, tm_a, tm_b, tm_f, tm_m, tm_r, tm_i, bt_w, after_first_grads=None, small_path=None):
    t_len = x.shape[0]
    cos, sin = _rope_tables(t_len)
    sgbt = jnp.broadcast_to(sgb[:, :, None], (G, C, 128))
    rlf = jnp.broadcast_to(rlf.reshape(H, 1), (H, 128))
    rlb = jnp.broadcast_to(rlb.reshape(H, 1), (H, 128))
    hc, kvc, scf, scb = _ctx_fwd(ctx, cmod6, norm1, win, rlf, rlb)
    hx, zuv, q, k, v, gfb, of, ya, sst = _fwd_a(x, mod6, norm1, win, sgw, sgbt, sggain, cos, sin, rlf, scf, tm=tm_a)
    ob, ycat, y, x1, rst = _fwd_b(q, k, v, of, gfb, ya, x, mod6, wout, rlb, scb, tm=tm_b)
    wg, wu, wd = ffn_weights(x1) if callable(ffn_weights) else ffn_weights
    dx1, h2, da, db, hm, df, small_f = _ffn(x1, tgt, mod6, norm2, normf, wg, wu, wd, tm=tm_f)
    d_wd, d_wd_b = _tn_matmul(hm, df, bm=DFF, bt=bt_w, name="dw_down")
    d_wg, d_wg_b = _tn_matmul(da, h2, bm=DFF, bt=bt_w, name="dw_gate")
    d_wu, d_wu_b = _tn_matmul(db, h2, bm=DFF, bt=bt_w, name="dw_up")
    dy, dya, dgfb, dof, dob, dg1 = _mid_bwd(dx1, y, of, ob, gfb, mod6, wout, tm=tm_m)
    d_wo, d_wo_b = _tn_matmul(ycat, dy, bm=D, bt=bt_w, name="dw_out")
    if after_first_grads is not None:
        rlf = rlf + after_first_grads((d_wd_b, d_wg_b, d_wu_b, d_wo_b))
    dqf, dkf, dvf, dqb, dkb, dvb, dscf, dscb, dlg = _ret_bwd(q, k, v, dof, dob, sst, rst, rlf, rlb, tm=tm_r)
    gx, dz, small_i, dsgw = _in_bwd(x, zuv, dya, dqf, dkf, dvf, dqb, dkb, dvb, dgfb, dx1, cos, sin,
                                    mod6, norm1, win, sgw, sgbt, sggain, tm=tm_i)
    dwin_c, small_c, dlg = _ctx_bwd(ctx, hc, kvc, cmod6, norm1, win, rlf, rlb, dscf, dscb, dlg)
    pack = jnp.concatenate([small_f, small_i, small_c, dg1, dlg], axis=0)
    after = small_path(pack, dsgw) if small_path is not None else ()
    d_wi, d_wi_b = _tn_matmul(dz, hx, bm=INC, bt=min(512, bt_w), name="dw_in", init=dwin_c,
                              init_rows=(KV_LO, KV_HI), after=after)
    grads = {"w_in": (d_wi, d_wi_b), "w_out": (d_wo, d_wo_b), "w_gate": (d_wg, d_wg_b), "w_up": (d_wu, d_wu_b),
             "w_down": (d_wd, d_wd_b)}
    return gx, grads, pack, dsgw


def kernel(x, c, ctx, c_ctx, w_mod, b_mod, norm1, w_in, sg_gain, sg_w, sg_b, ret_logit_f, ret_logit_b, w_out, norm2, w_gate, w_up, w_down, norm_f, loss_target, m_c_ctx, m_w_mod, m_b_mod, m_norm1, m_w_in, m_sg_gain, m_sg_w, m_sg_b, m_ret_logit_f, m_ret_logit_b, m_w_out, m_norm2, m_w_gate, m_w_up, m_w_down, m_norm_f, v_c_ctx, v_w_mod, v_b_mod, v_norm1, v_w_in, v_sg_gain, v_sg_w, v_sg_b, v_ret_logit_f, v_ret_logit_b, v_w_out, v_norm2, v_w_gate, v_w_up, v_w_down, v_norm_f):
    t_len = x.shape[1]
    tm = min(512, t_len)
    me = 4 * lax.axis_index("x") + 2 * lax.axis_index("y") + lax.axis_index("c")
    tr = lambda a: jnp.transpose(a[0])

    cctx_row = c_ctx.reshape(1, D)
    mod_all, cs_all = _mod_gather(c, cctx_row, w_mod[0], b_mod)
    mod6 = lax.dynamic_slice(mod_all, (me, 0), (1, 6 * D)).reshape(6, D)
    cmod6 = mod_all[8].reshape(6, D)

    g_in, g_out = _weight_gather([tr(w_in), w_out[0]])
    win_t = g_in.reshape(INC, D)
    wout = g_out.reshape(D, D)

    ffn_shards = _cast_bf16([tr(w_gate), tr(w_up), w_down[0]], name="cast_ffn", after=[g_in])
    h_ffn, tok = _exchange_start(ffn_shards, scatter=False, name="wffn_start")
    mod6 = mod6 + tok[0, 0]

    def ffn_weights(after):
        own, lands = _exchange_wait(h_ffn, [after], scatter=False, name="wffn_wait")
        return [lax.dynamic_update_slice(l, o[None], (me, 0, 0)).reshape(DFF, D) for o, l in zip(own, lands)]

    rs, outs = {}, {}

    def after_first_grads(bf):
        rs["first"], tok_first = _exchange_start([b.reshape(NDEV, b.shape[0] // NDEV, D) for b in bf], scatter=True,
                                                 name="rs_first_start")
        return tok_first[0, 0]

    def small_path(pack, dsgw):
        psum_rows, pall, sgw_sum, g_cc, d_cc, m_cc, v_cc = _small_reduce(
            pack, dsgw, w_mod[0], cctx_row, m_c_ctx.reshape(1, D), v_c_ctx.reshape(1, D))
        outs["c_ctx"] = tuple(a.reshape(D) for a in (g_cc, d_cc, m_cc, v_cc))
        dmod_rows = (ROW_I + 0, ROW_I + 1, ROW_G1, ROW_F + 2, ROW_F + 3, ROW_F + 1)
        dmod_all = jnp.concatenate([pall[:, r, :] for r in dmod_rows], axis=1)
        dcmod = jnp.concatenate([psum_rows[ROW_C + 0:ROW_C + 1], psum_rows[ROW_C + 1:ROW_C + 2],
                                 jnp.zeros((1, 4 * D), F32)], axis=1)
        dmod_mat = jnp.concatenate([dmod_all, jnp.pad(dcmod, ((0, 7), (0, 0)))], axis=0)
        ncol = 6 * D // NDEV
        dmod_cols = lax.dynamic_slice(dmod_mat, (0, me * ncol), (16, ncol))
        g_wm, d_wm, m_wm, v_wm = _wmod_grad(cs_all, dmod_cols, w_mod[0], m_w_mod[0], v_w_mod[0])
        outs["w_mod"] = (g_wm[None], d_wm[None], m_wm[None], v_wm[None])
        small_params = {
            "norm1": (norm1, m_norm1, v_norm1), "norm2": (norm2, m_norm2, v_norm2),
            "norm_f": tuple(a.reshape(1, D) for a in (norm_f, m_norm_f, v_norm_f)),
            "sg_gain": (sg_gain, m_sg_gain, v_sg_gain), "sg_b": (sg_b[0], m_sg_b[0], v_sg_b[0]),
            "ret_logit_f": (ret_logit_f, m_ret_logit_f, v_ret_logit_f),
            "ret_logit_b": (ret_logit_b, m_ret_logit_b, v_ret_logit_b),
            "b_mod": (b_mod, m_b_mod, v_b_mod), "sg_w": (sg_w[0], m_sg_w[0], v_sg_w[0])}
        small, loss = _adam_small(psum_rows, sgw_sum, small_params)
        back = {"norm_f": lambda a: a.reshape(D), "sg_b": lambda a: a[None], "sg_w": lambda a: a[None]}
        for name, vals in small.items():
            outs[name] = tuple(back.get(name, lambda a: a)(a) for a in vals)
        outs["loss"] = loss
        return [pall]

    gx, grads, pack, dsgw = _local_step(
        x[0], loss_target[0], ctx[0], mod6, cmod6, norm1, norm2[0:1], norm_f.reshape(1, D), win_t, wout, ffn_weights,
        sg_w[0], sg_b[0], sg_gain, ret_logit_f, ret_logit_b,
        tm_a=tm, tm_b=tm, tm_f=min(256, t_len), tm_m=tm, tm_r=tm, tm_i=tm, bt_w=min(1024, t_len),
        after_first_grads=after_first_grads, small_path=small_path)

    h_in, tok_in = _exchange_start([grads["w_in"][1].reshape(NDEV, INC // NDEV, D)], scatter=True, name="rs_in_start")
    _, lands_first = _exchange_wait(rs["first"], [tok_in], scatter=True, name="rs_first_wait")
    me_arr = me.reshape(1).astype(jnp.int32)

    def finish(name, land, w, m, v, transposed):
        rows = land.shape[1]
        own = lax.dynamic_slice(grads[name][0], (me * rows, 0), (rows, D))
        take = tr if transposed else (lambda a: a[0])
        res = _rs_adam(me_arr, own, land, take(w), take(m), take(v), name="adam_" + name)
        put = (lambda a: jnp.transpose(a)[None]) if transposed else (lambda a: a[None])
        outs[name] = tuple(put(a) for a in res)
        return res[0]

    g_down = finish("w_down", lands_first[0], w_down, m_w_down, v_w_down, False)
    g_gate = finish("w_gate", lands_first[1], w_gate, m_w_gate, v_w_gate, True)
    g_up = finish("w_up", lands_first[2], w_up, m_w_up, v_w_up, True)
    g_out = finish("w_out", lands_first[3], w_out, m_w_out, v_w_out, False)
    _, lands_in = _exchange_wait(h_in, [g_down, g_gate, g_up, g_out], scatter=True, name="rs_in_wait")
    finish("w_in", lands_in[0], w_in, m_w_in, v_w_in, True)
    loss = outs.pop("loss")

    order = ["c_ctx", "w_mod", "b_mod", "norm1", "w_in", "sg_gain", "sg_w", "sg_b", "ret_logit_f", "ret_logit_b",
             "w_out", "norm2", "w_gate", "w_up", "w_down", "norm_f"]
    res = [loss.reshape(()), gx[None]]
    for j in range(4):
        res += [outs[name][j] for name in order]
    return tuple(res)
```

```python
import functools
import math

import numpy as np
import jax
import jax.numpy as jnp
from jax import lax
from jax.experimental import pallas as pl
from jax.experimental.pallas import tpu as pltpu

F32 = jnp.float32
BF16 = jnp.bfloat16

D = 1024
AW = 512
RW = 512
H = 4
DH = 128
G = 4
C = 128
CR = 256
DFF = 2816
INC = 3584
KV_LO, KV_HI = 1536, 2560
NDEV = 8
EPS = 1e-6
KSCALE = DH ** -0.5
ROPE_BASE = 10000.0
GRID_W = 64

ADAM_LR = 0.001
ADAM_B1 = 0.9
ADAM_B2 = 0.999
ADAM_EPS = 1e-08
ADAM_WD = 0.01
ADAM_STEP = 10

VMEM_LIMIT = 56 * 1024 * 1024
MESH = pl.DeviceIdType.MESH


def _cparams(n_grid=0, **kw):
    sem = ("arbitrary",) * n_grid if n_grid else None
    return pltpu.CompilerParams(dimension_semantics=sem, vmem_limit_bytes=VMEM_LIMIT, **kw)


def _dot(a, b):
    return jnp.dot(a, b, preferred_element_type=F32)


def _dot_nt(a, b):
    return lax.dot_general(a, b, (((1,), (1,)), ((), ())), preferred_element_type=F32)


def _dot_tn(a, b):
    return lax.dot_general(a, b, (((0,), (0,)), ((), ())), preferred_element_type=F32)


def _whole(shape):
    nd = len(shape)
    return pl.BlockSpec(shape, lambda *_: (0,) * nd, pipeline_mode=pl.Buffered(1))


def _acc(shape):
    nd = len(shape)
    return pl.BlockSpec(shape, lambda *_: (0,) * nd)


def _rows(tm, n):
    return pl.BlockSpec((tm, n), lambda i: (i, 0))


def _rows_rev(tm, n, nt):
    return pl.BlockSpec((tm, n), lambda i: (nt - 1 - i, 0))


def _sigmoid(x):
    return 1.0 / (1.0 + jnp.exp(-x))


def _log_sigmoid(x):
    return jnp.minimum(x, 0.0) - jnp.log(1.0 + jnp.exp(-jnp.abs(x)))


_GK0 = math.sqrt(2.0 / math.pi)
_GK1 = 0.044715


def _gelu(x):
    x2 = x * x
    t = jnp.tanh(_GK0 * x * (1.0 + _GK1 * x2))
    g = 0.5 * x * (1.0 + t)
    dg = 0.5 * (1.0 + t) + 0.5 * x * (1.0 - t * t) * (_GK0 * (1.0 + 3.0 * _GK1 * x2))
    return g, dg


def _silu_parts(a):
    s = _sigmoid(a)
    return a * s, s * (1.0 + a * (1.0 - s))


def _rope(t, cos, sins):
    n = t.shape[1]
    lane = lax.broadcasted_iota(jnp.int32, t.shape, 1)
    first = (lane & 63) < 32
    partner = jnp.where(first, pltpu.roll(t, n - 32, 1), pltpu.roll(t, 32, 1))
    return t * cos + partner * sins


def _headnorm(o):
    outs, rs = [], []
    for h in range(H):
        oh = o[:, h * DH:(h + 1) * DH]
        r = lax.rsqrt(jnp.mean(oh * oh, axis=-1, keepdims=True) + EPS)
        outs.append(oh * r)
        rs.append(r)
    return jnp.concatenate(outs, axis=1), rs


def _decay_consts(lg, forward):
    ii = lax.broadcasted_iota(jnp.int32, (CR, CR), 0).astype(F32)
    jj = lax.broadcasted_iota(jnp.int32, (CR, CR), 1).astype(F32)
    ic = lax.broadcasted_iota(jnp.int32, (CR, 1), 0).astype(F32)
    if forward:
        diff = ii - jj
        xi = jnp.exp(lg * (ic + 1.0))
        z = jnp.exp(lg * (CR - 1.0 - ic))
    else:
        diff = jj - ii
        xi = jnp.exp(lg * (CR - ic))
        z = jnp.exp(lg * ic)
    dm = jnp.where(diff >= 0.0, jnp.exp(lg * jnp.maximum(diff, 0.0)), 0.0)
    dec = jnp.exp(lg * float(CR))
    return dm, xi, z, dec, ic


def _ctx_fwd(ctx, cmod6, norm1, win, rlf, rlb):
    lc = ctx.shape[0]

    def body(ctx_ref, cmod_ref, n1_ref, win_ref, rlf_ref, rlb_ref, hc_ref, kvc_ref, scf_ref, scb_ref):
        x = ctx_ref[...]
        r = lax.rsqrt(jnp.mean(x * x, axis=-1, keepdims=True) + EPS)
        hc = (x * r) * (n1_ref[...] * (1.0 + cmod_ref[1:2, :])) + cmod_ref[0:1, :]
        hcb = hc.astype(BF16)
        hc_ref[...] = hcb
        kv = _dot_nt(hcb, win_ref[KV_LO:KV_HI, :])
        k = kv[:, :RW] * KSCALE
        v = kv[:, RW:]
        kvc_ref[:, :RW] = k
        kvc_ref[:, RW:] = v
        t = lax.broadcasted_iota(jnp.int32, (lc, 1), 0).astype(F32)
        lgf = _log_sigmoid(rlf_ref[...])
        lgb = _log_sigmoid(rlb_ref[...])
        for h in range(H):
            hs = slice(h * DH, (h + 1) * DH)
            wf = jnp.exp(lgf[h:h + 1, 0:1] * (lc - 1.0 - t))
            wb = jnp.exp(lgb[h:h + 1, 0:1] * t)
            vh = v[:, hs].astype(BF16)
            scf_ref[h] = _dot_tn((k[:, hs] * wf).astype(BF16), vh)
            scb_ref[h] = _dot_tn((k[:, hs] * wb).astype(BF16), vh)

    return pl.pallas_call(
        body, name="ctx_fwd",
        out_shape=(jax.ShapeDtypeStruct((lc, D), BF16), jax.ShapeDtypeStruct((lc, 2 * RW), F32),
                   jax.ShapeDtypeStruct((H, DH, DH), F32), jax.ShapeDtypeStruct((H, DH, DH), F32)),
        compiler_params=_cparams(),
    )(ctx, cmod6, norm1, win, rlf, rlb)


def _sg_forward(u, v, sgw_ref, sgbt_ref, sgg_ref, nc):
    ua, dgu = _gelu(u)
    va, dgv = _gelu(v)
    gain = sgg_ref[...]
    mixed, vn_b, vah_l, rv_l = [], [], [], []
    for g in range(G):
        gs = slice(g * DH, (g + 1) * DH)
        vag = va[:, gs]
        rv = lax.rsqrt(jnp.mean(vag * vag, axis=-1, keepdims=True) + EPS)
        vah = vag * rv
        vn = (vah * gain[:, gs]).astype(BF16)
        wg = sgw_ref[g].astype(BF16)
        bcol = sgbt_ref[g]
        rows = [_dot(wg, vn[c * C:(c + 1) * C, :]) + bcol for c in range(nc)]
        mixed.append(jnp.concatenate(rows, axis=0) if nc > 1 else rows[0])
        vn_b.append(vn)
        vah_l.append(vah)
        rv_l.append(rv)
    mixed = jnp.concatenate(mixed, axis=1)
    return ua * mixed, (ua, dgu, dgv, mixed, vn_b, vah_l, rv_l)


def _fwd_a(x, mod6, norm1, win, sgw, sgbt, sggain, cos, sin, rlf, scf, *, tm):
    t_len = x.shape[0]
    nt, nc, ncr = t_len // tm, tm // C, tm // CR

    def body(x_ref, mod_ref, n1_ref, win_ref, sgw_ref, sgbt_ref, sgg_ref, cos_ref, sin_ref, rlf_ref, scf_ref,
             hx_ref, zuv_ref, q_ref, k_ref, v_ref, gfb_ref, of_ref, ya_ref, sst_ref, s_scr):
        i = pl.program_id(0)

        @pl.when(i == 0)
        def _():
            s_scr[...] = scf_ref[...]

        x = x_ref[...]
        r = lax.rsqrt(jnp.mean(x * x, axis=-1, keepdims=True) + EPS)
        hx = (x * r) * (n1_ref[...] * (1.0 + mod_ref[1:2, :])) + mod_ref[0:1, :]
        hxb = hx.astype(BF16)
        hx_ref[...] = hxb
        z = _dot_nt(hxb, win_ref[...])
        zuv_ref[...] = z[:, :2 * AW]
        gfb_ref[...] = z[:, 2560:3584]
        cos = jnp.tile(cos_ref[...], (1, H))
        sins = jnp.tile(sin_ref[...], (1, H))
        q_ref[...] = _rope(z[:, 1024:1536], cos, sins).astype(BF16)
        k_ref[...] = (_rope(z[:, 1536:2048], cos, sins) * KSCALE).astype(BF16)
        v_ref[...] = z[:, 2048:2560].astype(BF16)
        ya, _ = _sg_forward(z[:, :AW], z[:, AW:2 * AW], sgw_ref, sgbt_ref, sgg_ref, nc)
        ya_ref[...] = ya

        lg = _log_sigmoid(rlf_ref[...])
        for h in range(H):
            dm, xi, zc, dec, _ = _decay_consts(lg[h:h + 1, 0:1], True)
            hs = slice(h * DH, (h + 1) * DH)
            for c in range(ncr):
                rs = slice(c * CR, (c + 1) * CR)
                qc, kc, vc = q_ref[rs, hs], k_ref[rs, hs], v_ref[rs, hs]
                s = s_scr[h]
                sst_ref[c, h] = s
                a = (_dot_nt(qc, kc) * dm).astype(BF16)
                of_ref[rs, hs] = _dot(a, vc) + xi * _dot(qc, s.astype(BF16))
                kz = (kc.astype(F32) * zc).astype(BF16)
                s_scr[h] = dec * s + _dot_tn(kz, vc)

    n_chunks = t_len // CR
    return pl.pallas_call(
        body, name="fwd_a", grid=(nt,),
        in_specs=[_rows(tm, D), _whole((6, D)), _whole((1, D)), _whole((INC, D)), _whole((G, C, C)),
                  _whole((G, C, 128)), _whole((1, AW)), _rows(tm, DH), _rows(tm, DH), _whole((H, 128)),
                  _whole((H, DH, DH))],
        out_specs=[_rows(tm, D), _rows(tm, 2 * AW), _rows(tm, RW), _rows(tm, RW), _rows(tm, RW),
                   _rows(tm, 2 * RW), _rows(tm, RW), _rows(tm, AW),
                   pl.BlockSpec((ncr, H, DH, DH), lambda i: (i, 0, 0, 0))],
        out_shape=(jax.ShapeDtypeStruct((t_len, D), BF16), jax.ShapeDtypeStruct((t_len, 2 * AW), F32),
                   jax.ShapeDtypeStruct((t_len, RW), BF16), jax.ShapeDtypeStruct((t_len, RW), BF16),
                   jax.ShapeDtypeStruct((t_len, RW), BF16), jax.ShapeDtypeStruct((t_len, 2 * RW), F32),
                   jax.ShapeDtypeStruct((t_len, RW), F32), jax.ShapeDtypeStruct((t_len, AW), F32),
                   jax.ShapeDtypeStruct((n_chunks, H, DH, DH), F32)),
        scratch_shapes=[pltpu.VMEM((H, DH, DH), F32)],
        compiler_params=_cparams(1),
    )(x, mod6, norm1, win, sgw, sgbt, sggain, cos, sin, rlf, scf)


def _fwd_b(q, k, v, of, gfb, ya, x, mod6, wout, rlb, scb, *, tm):
    t_len = x.shape[0]
    nt, nc = t_len // tm, tm // CR

    def body(q_ref, k_ref, v_ref, of_ref, gfb_ref, ya_ref, x_ref, mod_ref, wout_ref, rlb_ref, scb_ref,
             ob_ref, ycat_ref, y_ref, x1_ref, rst_ref, r_scr):
        i = pl.program_id(0)

        @pl.when(i == 0)
        def _():
            r_scr[...] = scb_ref[...]

        lg = _log_sigmoid(rlb_ref[...])
        for h in range(H):
            dm, xi, zc, dec, _ = _decay_consts(lg[h:h + 1, 0:1], False)
            hs = slice(h * DH, (h + 1) * DH)
            for c in reversed(range(nc)):
                rs = slice(c * CR, (c + 1) * CR)
                qc, kc, vc = q_ref[rs, hs], k_ref[rs, hs], v_ref[rs, hs]
                s = r_scr[h]
                rst_ref[c, h] = s
                a = (_dot_nt(qc, kc) * dm).astype(BF16)
                ob_ref[rs, hs] = _dot(a, vc) + xi * _dot(qc, s.astype(BF16))
                kz = (kc.astype(F32) * zc).astype(BF16)
                r_scr[h] = dec * s + _dot_tn(kz, vc)

        gfb = gfb_ref[...]
        sf, _ = _silu_parts(gfb[:, :RW])
        sb, _ = _silu_parts(gfb[:, RW:])
        hnf, _ = _headnorm(of_ref[...])
        hnb, _ = _headnorm(ob_ref[...])
        yr = sf * hnf + sb * hnb
        ycat = jnp.concatenate([ya_ref[...], yr], axis=1).astype(BF16)
        ycat_ref[...] = ycat
        y = _dot(ycat, wout_ref[...])
        y_ref[...] = y.astype(BF16)
        x1_ref[...] = x_ref[...] + mod_ref[2:3, :] * y

    n_chunks = t_len // CR
    rr = functools.partial(_rows_rev, nt=nt)
    return pl.pallas_call(
        body, name="fwd_b", grid=(nt,),
        in_specs=[rr(tm, RW), rr(tm, RW), rr(tm, RW), rr(tm, RW), rr(tm, 2 * RW), rr(tm, AW), rr(tm, D),
                  _whole((6, D)), _whole((D, D)), _whole((H, 128)), _whole((H, DH, DH))],
        out_specs=[rr(tm, RW), rr(tm, D), rr(tm, D), rr(tm, D),
                   pl.BlockSpec((nc, H, DH, DH), lambda i: (nt - 1 - i, 0, 0, 0))],
        out_shape=(jax.ShapeDtypeStruct((t_len, RW), F32), jax.ShapeDtypeStruct((t_len, D), BF16),
                   jax.ShapeDtypeStruct((t_len, D), BF16), jax.ShapeDtypeStruct((t_len, D), F32),
                   jax.ShapeDtypeStruct((n_chunks, H, DH, DH), F32)),
        scratch_shapes=[pltpu.VMEM((H, DH, DH), F32)],
        compiler_params=_cparams(1),
    )(q, k, v, of, gfb, ya, x, mod6, wout, rlb, scb)


def _ffn(x1, tgt, mod6, norm2, normf, wg, wu, wd, *, tm):
    t_len = x1.shape[0]
    nt = t_len // tm

    def body(x1_ref, tgt_ref, mod_ref, n2_ref, nf_ref, wg_ref, wu_ref, wd_ref,
             dx1_ref, h2_ref, da_ref, db_ref, hm_ref, df_ref, small_ref):
        i = pl.program_id(0)

        @pl.when(i == 0)
        def _():
            small_ref[...] = jnp.zeros_like(small_ref)

        sh2, sc2, g2 = mod_ref[3:4, :], mod_ref[4:5, :], mod_ref[5:6, :]
        n2, nf = n2_ref[...], nf_ref[...]
        x1 = x1_ref[...]
        r2 = lax.rsqrt(jnp.mean(x1 * x1, axis=-1, keepdims=True) + EPS)
        x1h = x1 * r2
        w2 = n2 * (1.0 + sc2)
        h2b = (x1h * w2 + sh2).astype(BF16)
        h2_ref[...] = h2b
        a = _dot_nt(h2b, wg_ref[...])
        b = _dot_nt(h2b, wu_ref[...])
        sa, dsa = _silu_parts(a)
        hmb = (sa * b).astype(BF16)
        hm_ref[...] = hmb
        f = _dot(hmb, wd_ref[...])
        x2 = x1 + g2 * f
        r3 = lax.rsqrt(jnp.mean(x2 * x2, axis=-1, keepdims=True) + EPS)
        x2h = x2 * r3
        diff = x2h * nf - tgt_ref[...]
        small_ref[5:6, :] += jnp.sum(diff * diff, axis=0, keepdims=True)
        dout = diff * (1.0 / D)
        small_ref[0:1, :] += jnp.sum(dout * x2h, axis=0, keepdims=True)
        dyg = dout * nf
        dx2 = r3 * (dyg - x2h * jnp.mean(dyg * x2h, axis=-1, keepdims=True))
        small_ref[1:2, :] += jnp.sum(dx2 * f, axis=0, keepdims=True)
        dfb = (dx2 * g2).astype(BF16)
        df_ref[...] = dfb
        dhm = _dot_nt(dfb, wd_ref[...])
        dbb = (dhm * sa).astype(BF16)
        dab = (dhm * b * dsa).astype(BF16)
        da_ref[...] = dab
        db_ref[...] = dbb
        dh2 = _dot(dab, wg_ref[...]) + _dot(dbb, wu_ref[...])
        small_ref[2:3, :] += jnp.sum(dh2, axis=0, keepdims=True)
        dhx = dh2 * x1h
        small_ref[3:4, :] += jnp.sum(dhx, axis=0, keepdims=True) * n2
        small_ref[4:5, :] += jnp.sum(dhx, axis=0, keepdims=True) * (1.0 + sc2)
        dyg2 = dh2 * w2
        dx1_ref[...] = dx2 + r2 * (dyg2 - x1h * jnp.mean(dyg2 * x1h, axis=-1, keepdims=True))

    return pl.pallas_call(
        body, name="ffn", grid=(nt,),
        in_specs=[_rows(tm, D), _rows(tm, D), _whole((6, D)), _whole((1, D)), _whole((1, D)),
                  _whole((DFF, D)), _whole((DFF, D)), _whole((DFF, D))],
        out_specs=[_rows(tm, D), _rows(tm, D), _rows(tm, DFF), _rows(tm, DFF), _rows(tm, DFF), _rows(tm, D),
                   _acc((8, D))],
        out_shape=(jax.ShapeDtypeStruct((t_len, D), F32), jax.ShapeDtypeStruct((t_len, D), BF16),
                   jax.ShapeDtypeStruct((t_len, DFF), BF16), jax.ShapeDtypeStruct((t_len, DFF), BF16),
                   jax.ShapeDtypeStruct((t_len, DFF), BF16), jax.ShapeDtypeStruct((t_len, D), BF16),
                   jax.ShapeDtypeStruct((8, D), F32)),
        compiler_params=_cparams(1),
    )(x1, tgt, mod6, norm2, normf, wg, wu, wd)


def _mid_bwd(dx1, y, of, ob, gfb, mod6, wout, *, tm):
    t_len = dx1.shape[0]
    nt = t_len // tm

    def body(dx1_ref, y_ref, of_ref, ob_ref, gfb_ref, mod_ref, wout_ref,
             dy_ref, dya_ref, dgfb_ref, dof_ref, dob_ref, dg1_ref):
        i = pl.program_id(0)

        @pl.when(i == 0)
        def _():
            dg1_ref[...] = jnp.zeros_like(dg1_ref)

        dx1 = dx1_ref[...]
        dg1_ref[0:1, :] += jnp.sum(dx1 * y_ref[...].astype(F32), axis=0, keepdims=True)
        dyb = (dx1 * mod_ref[2:3, :]).astype(BF16)
        dy_ref[...] = dyb
        dycat = _dot_nt(dyb, wout_ref[...])
        dya_ref[...] = dycat[:, :AW]
        dyr = dycat[:, AW:]
        gfb = gfb_ref[...]
        for o_ref, do_ref, lo in ((of_ref, dof_ref, 0), (ob_ref, dob_ref, RW)):
            sg, dsg = _silu_parts(gfb[:, lo:lo + RW])
            o = o_ref[...]
            hn, rs = _headnorm(o)
            dgfb_ref[:, lo:lo + RW] = (dyr * hn * dsg).astype(BF16)
            dhn = dyr * sg
            for h in range(H):
                hs = slice(h * DH, (h + 1) * DH)
                oh, dh = hn[:, hs], dhn[:, hs]
                do_ref[:, hs] = (rs[h] * (dh - oh * jnp.mean(dh * oh, axis=-1, keepdims=True))).astype(BF16)

    return pl.pallas_call(
        body, name="mid_bwd", grid=(nt,),
        in_specs=[_rows(tm, D), _rows(tm, D), _rows(tm, RW), _rows(tm, RW), _rows(tm, 2 * RW),
                  _whole((6, D)), _whole((D, D))],
        out_specs=[_rows(tm, D), _rows(tm, AW), _rows(tm, 2 * RW), _rows(tm, RW), _rows(tm, RW), _acc((8, D))],
        out_shape=(jax.ShapeDtypeStruct((t_len, D), BF16), jax.ShapeDtypeStruct((t_len, AW), F32),
                   jax.ShapeDtypeStruct((t_len, 2 * RW), BF16), jax.ShapeDtypeStruct((t_len, RW), BF16),
                   jax.ShapeDtypeStruct((t_len, RW), BF16), jax.ShapeDtypeStruct((8, D), F32)),
        compiler_params=_cparams(1),
    )(dx1, y, of, ob, gfb, mod6, wout)


def _ret_bwd_dir(q_ref, k_ref, v_ref, do_ref, st_ref, dq_ref, dk_ref, dv_ref, ds_scr, dlg_scr, lg, forward, nc, row0):
    order = reversed(range(nc)) if forward else range(nc)
    order = list(order)
    for h in range(H):
        dm, xi, zc, dec, ic = _decay_consts(lg[h:h + 1, 0:1], forward)
        hs = slice(h * DH, (h + 1) * DH)
        if forward:
            w_qi, w_qc, w_ki, w_ks = ic, ic + 1.0, -ic, (CR - 1.0) - ic
        else:
            w_qi, w_qc, w_ki, w_ks = -ic, CR - ic, ic, ic
        acc = jnp.zeros((1, DH), F32)
        for c in order:
            rs = slice(c * CR, (c + 1) * CR)
            qc, kc, vc, doc = q_ref[rs, hs], k_ref[rs, hs], v_ref[rs, hs], do_ref[rs, hs]
            s = st_ref[c, h]
            dsn = ds_scr[h]
            sb, dsnb = s.astype(BF16), dsn.astype(BF16)
            qf, kf = qc.astype(F32), kc.astype(F32)
            a = (_dot_nt(qc, kc) * dm).astype(BF16)
            da = (_dot_nt(doc, vc) * dm).astype(BF16)
            xdo = (xi * doc.astype(F32)).astype(BF16)
            kz = (kf * zc).astype(BF16)
            vz = (vc.astype(F32) * zc).astype(BF16)
            dq_i = _dot(da, kc)
            dq_c = _dot_nt(xdo, sb)
            dk_i = _dot_tn(da, qc)
            dk_s = _dot_nt(vz, dsnb)
            dq_ref[rs, hs] = dq_i + dq_c
            dk_ref[rs, hs] = dk_i + dk_s
            dv_ref[rs, hs] = _dot_tn(a, doc) + _dot(kz, dsnb)
            rowterm = qf * (w_qi * dq_i + w_qc * dq_c) + kf * (w_ki * dk_i + w_ks * dk_s)
            acc = acc + jnp.sum(rowterm, axis=0, keepdims=True)
            acc = acc + (float(CR) * dec) * jnp.sum(s * dsn, axis=0, keepdims=True)
            ds_scr[h] = _dot_tn((xi * qf).astype(BF16), doc) + dec * dsn
        dlg_scr[row0 + h:row0 + h + 1, :] += acc


def _ret_bwd(q, k, v, dof, dob, sst, rst, rlf, rlb, *, tm):
    t_len = q.shape[0]
    nt, nc = t_len // tm, tm // CR

    def body(qf_ref, kf_ref, vf_ref, dof_ref, sst_ref, qb_ref, kb_ref, vb_ref, dob_ref, rst_ref, rlf_ref, rlb_ref,
             dqf_ref, dkf_ref, dvf_ref, dqb_ref, dkb_ref, dvb_ref, dscf_ref, dscb_ref, dlg_ref,
             dsf_scr, dsb_scr, dlg_scr):
        i = pl.program_id(0)

        @pl.when(i == 0)
        def _():
            dsf_scr[...] = jnp.zeros_like(dsf_scr)
            dsb_scr[...] = jnp.zeros_like(dsb_scr)
            dlg_scr[...] = jnp.zeros_like(dlg_scr)

        lgf = _log_sigmoid(rlf_ref[...])
        lgb = _log_sigmoid(rlb_ref[...])
        _ret_bwd_dir(qf_ref, kf_ref, vf_ref, dof_ref, sst_ref, dqf_ref, dkf_ref, dvf_ref, dsf_scr, dlg_scr, lgf, True, nc, 0)
        _ret_bwd_dir(qb_ref, kb_ref, vb_ref, dob_ref, rst_ref, dqb_ref, dkb_ref, dvb_ref, dsb_scr, dlg_scr, lgb, False, nc, H)

        @pl.when(i == nt - 1)
        def _():
            dscf_ref[...] = dsf_scr[...]
            dscb_ref[...] = dsb_scr[...]
            tot = jnp.broadcast_to(jnp.sum(dlg_scr[...], axis=1, keepdims=True), (8, 128))
            ri = lax.broadcasted_iota(jnp.int32, (8, 128), 0)
            li = lax.broadcasted_iota(jnp.int32, (8, 128), 1)
            dlg_ref[...] = jnp.zeros_like(dlg_ref)
            dlg_ref[0:1, 0:128] = jnp.sum(jnp.where(ri == li, tot, 0.0), axis=0, keepdims=True)
            dlg_ref[1:2, 0:128] = jnp.sum(jnp.where(ri - H == li, tot, 0.0), axis=0, keepdims=True)

    rr = functools.partial(_rows_rev, nt=nt)
    st_f = pl.BlockSpec((nc, H, DH, DH), lambda i: (nt - 1 - i, 0, 0, 0))
    st_b = pl.BlockSpec((nc, H, DH, DH), lambda i: (i, 0, 0, 0))
    tok = jax.ShapeDtypeStruct((t_len, RW), F32)
    st = jax.ShapeDtypeStruct((H, DH, DH), F32)
    return pl.pallas_call(
        body, name="ret_bwd", grid=(nt,),
        in_specs=[rr(tm, RW), rr(tm, RW), rr(tm, RW), rr(tm, RW), st_f,
                  _rows(tm, RW), _rows(tm, RW), _rows(tm, RW), _rows(tm, RW), st_b,
                  _whole((H, 128)), _whole((H, 128))],
        out_specs=[rr(tm, RW), rr(tm, RW), rr(tm, RW), _rows(tm, RW), _rows(tm, RW), _rows(tm, RW),
                   _acc((H, DH, DH)), _acc((H, DH, DH)), _acc((8, D))],
        out_shape=(tok, tok, tok, tok, tok, tok, st, st, jax.ShapeDtypeStruct((8, D), F32)),
        scratch_shapes=[pltpu.VMEM((H, DH, DH), F32), pltpu.VMEM((H, DH, DH), F32), pltpu.VMEM((8, 128), F32)],
        compiler_params=_cparams(1),
    )(q, k, v, dof, sst, q, k, v, dob, rst, rlf, rlb)


def _in_bwd(x, zuv, dya, dqf, dkf, dvf, dqb, dkb, dvb, dgfb, dx1, cos, sin, mod6, norm1, win, sgw, sgbt, sggain, *, tm):
    t_len = x.shape[0]
    nt, nc = t_len // tm, tm // C

    def body(x_ref, zuv_ref, dya_ref, dqf_ref, dkf_ref, dvf_ref, dqb_ref, dkb_ref, dvb_ref, dgfb_ref, dx1_ref,
             cos_ref, sin_ref, mod_ref, n1_ref, win_ref, sgw_ref, sgbt_ref, sgg_ref,
             gx_ref, dz_ref, small_ref, dsgw_ref, dsgbt_ref):
        i = pl.program_id(0)

        @pl.when(i == 0)
        def _():
            small_ref[...] = jnp.zeros_like(small_ref)
            dsgw_ref[...] = jnp.zeros_like(dsgw_ref)
            dsgbt_ref[...] = jnp.zeros_like(dsgbt_ref)

        zuv = zuv_ref[...]
        _, (ua, dgu, dgv, mixed, vn_b, vah_l, rv_l) = _sg_forward(zuv[:, :AW], zuv[:, AW:], sgw_ref, sgbt_ref, sgg_ref, nc)
        dya = dya_ref[...]
        dz_ref[:, 0:AW] = (dya * mixed * dgu).astype(BF16)
        dmixed = dya * ua
        gain = sgg_ref[...]
        for g in range(G):
            gs = slice(g * DH, (g + 1) * DH)
            dmg = dmixed[:, gs]
            dmb = dmg.astype(BF16)
            wgt = sgw_ref[g].astype(BF16)
            dsw = jnp.zeros((C, C), F32)
            dsb = jnp.zeros((C, DH), F32)
            rows = []
            for c in range(nc):
                rs = slice(c * C, (c + 1) * C)
                dsw = dsw + _dot_nt(dmb[rs, :], vn_b[g][rs, :])
                dsb = dsb + dmg[rs, :]
                rows.append(_dot_tn(wgt, dmb[rs, :]))
            dsgw_ref[g] += dsw
            dsgbt_ref[g] += dsb
            dvn = jnp.concatenate(rows, axis=0) if nc > 1 else rows[0]
            vah, rv = vah_l[g], rv_l[g]
            small_ref[3:4, gs] += jnp.sum(dvn * vah, axis=0, keepdims=True)
            dyg = dvn * gain[:, gs]
            dva = rv * (dyg - vah * jnp.mean(dyg * vah, axis=-1, keepdims=True))
            dz_ref[:, AW + g * DH:AW + (g + 1) * DH] = (dva * dgv[:, gs]).astype(BF16)

        cos = jnp.tile(cos_ref[...], (1, H))
        nsins = -jnp.tile(sin_ref[...], (1, H))
        dz_ref[:, 1024:1536] = _rope(dqf_ref[...] + dqb_ref[...], cos, nsins).astype(BF16)
        dz_ref[:, 1536:2048] = (_rope(dkf_ref[...] + dkb_ref[...], cos, nsins) * KSCALE).astype(BF16)
        dz_ref[:, 2048:2560] = (dvf_ref[...] + dvb_ref[...]).astype(BF16)
        dz_ref[:, 2560:3584] = dgfb_ref[...]

        dhx = _dot(dz_ref[...], win_ref[...])
        x = x_ref[...]
        r = lax.rsqrt(jnp.mean(x * x, axis=-1, keepdims=True) + EPS)
        xh = x * r
        n1, sc1 = n1_ref[...], mod_ref[1:2, :]
        small_ref[0:1, :] += jnp.sum(dhx, axis=0, keepdims=True)
        dhxx = jnp.sum(dhx * xh, axis=0, keepdims=True)
        small_ref[1:2, :] += dhxx * n1
        small_ref[2:3, :] += dhxx * (1.0 + sc1)
        dyg = dhx * (n1 * (1.0 + sc1))
        gx_ref[...] = dx1_ref[...] + r * (dyg - xh * jnp.mean(dyg * xh, axis=-1, keepdims=True))

        @pl.when(i == nt - 1)
        def _():
            ri = lax.broadcasted_iota(jnp.int32, (C, DH), 0)
            li = lax.broadcasted_iota(jnp.int32, (C, DH), 1)
            for g in range(G):
                tot = jnp.broadcast_to(jnp.sum(dsgbt_ref[g], axis=1, keepdims=True), (C, DH))
                small_ref[4 + g:5 + g, 0:DH] = jnp.sum(jnp.where(ri == li, tot, 0.0), axis=0, keepdims=True)

    tok = functools.partial(_rows, tm)
    return pl.pallas_call(
        body, name="in_bwd", grid=(nt,),
        in_specs=[tok(D), tok(2 * AW), tok(AW), tok(RW), tok(RW), tok(RW), tok(RW), tok(RW), tok(RW),
                  tok(2 * RW), tok(D), tok(DH), tok(DH), _whole((6, D)), _whole((1, D)), _whole((INC, D)),
                  _whole((G, C, C)), _whole((G, C, 128)), _whole((1, AW))],
        out_specs=[tok(D), tok(INC), _acc((8, D)), _acc((G, C, C))],
        out_shape=(jax.ShapeDtypeStruct((t_len, D), F32), jax.ShapeDtypeStruct((t_len, INC), BF16),
                   jax.ShapeDtypeStruct((8, D), F32), jax.ShapeDtypeStruct((G, C, C), F32)),
        scratch_shapes=[pltpu.VMEM((G, C, 128), F32)],
        compiler_params=_cparams(1),
    )(x, zuv, dya, dqf, dkf, dvf, dqb, dkb, dvb, dgfb, dx1, cos, sin, mod6, norm1, win, sgw, sgbt, sggain)


def _tn_matmul(a, b, *, bm, bt, name, init=None, init_rows=None, after=()):
    t_len, m = a.shape
    n = b.shape[1]
    ni, ntt = m // bm, t_len // bt
    has_init = init is not None
    assert not has_init or bm == m

    def body(*refs):
        o_ref, ob_ref = refs[-2:]
        a_ref, b_ref = refs[:2]
        init_ref = refs[2] if has_init else None
        t = pl.program_id(1)

        @pl.when(t == 0)
        def _():
            o_ref[...] = jnp.zeros_like(o_ref)
            if has_init:
                o_ref[init_rows[0]:init_rows[1], :] = init_ref[...]

        o_ref[...] += _dot_tn(a_ref[...], b_ref[...])

        @pl.when(t == ntt - 1)
        def _():
            ob_ref[...] = o_ref[...].astype(BF16)

    in_specs = [pl.BlockSpec((bt, bm), lambda i, t: (t, i)), pl.BlockSpec((bt, n), lambda i, t: (t, 0))]
    args = [a, b]
    if has_init:
        in_specs.append(pl.BlockSpec(init.shape, lambda i, t: (0, 0), pipeline_mode=pl.Buffered(1)))
        args.append(init)
    for arr in after:
        in_specs.append(pl.BlockSpec(memory_space=pl.ANY))
        args.append(arr)
    out_spec = pl.BlockSpec((bm, n), lambda i, t: (i, 0))
    return pl.pallas_call(
        body, name=name, grid=(ni, ntt),
        in_specs=in_specs,
        out_specs=[out_spec, out_spec],
        out_shape=(jax.ShapeDtypeStruct((m, n), F32), jax.ShapeDtypeStruct((m, n), BF16)),
        compiler_params=_cparams(2),
    )(*args)


def _ctx_bwd(ctx, hc, kvc, cmod6, norm1, win, rlf, rlb, dscf, dscb, dlg_in):
    lc = ctx.shape[0]

    def body(ctx_ref, hc_ref, kvc_ref, cmod_ref, n1_ref, win_ref, rlf_ref, rlb_ref, dscf_ref, dscb_ref, dlgin_ref,
             dwin_ref, small_ref, dlg_ref):
        k = kvc_ref[:, :RW]
        v = kvc_ref[:, RW:]
        t = lax.broadcasted_iota(jnp.int32, (lc, 1), 0).astype(F32)
        lgf = _log_sigmoid(rlf_ref[...])
        lgb = _log_sigmoid(rlb_ref[...])
        dks, dvs = [], []
        lane = lax.broadcasted_iota(jnp.int32, (1, 128), 1)
        row_f = jnp.zeros((1, 128), F32)
        row_b = jnp.zeros((1, 128), F32)
        for h in range(H):
            hs = slice(h * DH, (h + 1) * DH)
            wf = jnp.exp(lgf[h:h + 1, 0:1] * (lc - 1.0 - t))
            wb = jnp.exp(lgb[h:h + 1, 0:1] * t)
            kh, vhb = k[:, hs], v[:, hs].astype(BF16)
            dsf, dsb = dscf_ref[h].astype(BF16), dscb_ref[h].astype(BF16)
            dkf = wf * _dot_nt(vhb, dsf)
            dkb = wb * _dot_nt(vhb, dsb)
            dvs.append(_dot((kh * wf).astype(BF16), dsf) + _dot((kh * wb).astype(BF16), dsb))
            dks.append((dkf + dkb) * KSCALE)
            lf = jnp.sum((lc - 1.0 - t) * kh * dkf, axis=0, keepdims=True)
            lb = jnp.sum(t * kh * dkb, axis=0, keepdims=True)
            row_f = row_f + jnp.where(lane == h, jnp.sum(lf, axis=1, keepdims=True), 0.0)
            row_b = row_b + jnp.where(lane == h, jnp.sum(lb, axis=1, keepdims=True), 0.0)
        dlg_ref[...] = dlgin_ref[...]
        dlg_ref[0:1, 0:128] += row_f
        dlg_ref[1:2, 0:128] += row_b
        dkv = jnp.concatenate(dks + dvs, axis=1).astype(BF16)
        dhc = _dot(dkv, win_ref[KV_LO:KV_HI, :])
        dwin_ref[...] = _dot_tn(dkv, hc_ref[...])
        x = ctx_ref[...]
        r = lax.rsqrt(jnp.mean(x * x, axis=-1, keepdims=True) + EPS)
        xh = x * r
        small_ref[...] = jnp.zeros_like(small_ref)
        small_ref[0:1, :] = jnp.sum(dhc, axis=0, keepdims=True)
        dhxx = jnp.sum(dhc * xh, axis=0, keepdims=True)
        small_ref[1:2, :] = dhxx * n1_ref[...]
        small_ref[2:3, :] = dhxx * (1.0 + cmod_ref[1:2, :])

    return pl.pallas_call(
        body, name="ctx_bwd",
        out_shape=(jax.ShapeDtypeStruct((2 * RW, D), F32), jax.ShapeDtypeStruct((8, D), F32),
                   jax.ShapeDtypeStruct((8, D), F32)),
        compiler_params=_cparams(),
    )(ctx, hc, kvc, cmod6, norm1, win, rlf, rlb, dscf, dscb, dlg_in)


def _adam_math(w, g, m, v):
    m = ADAM_B1 * m + (1.0 - ADAM_B1) * g
    v = ADAM_B2 * v + (1.0 - ADAM_B2) * (g * g)
    m_hat = m / (1.0 - ADAM_B1 ** ADAM_STEP)
    v_hat = v / (1.0 - ADAM_B2 ** ADAM_STEP)
    delta = -ADAM_LR * (m_hat / (jnp.sqrt(v_hat) + ADAM_EPS) + ADAM_WD * w)
    return delta, m, v


def _adam(w, g, m, v, *, name, br=None):
    r, c = w.shape
    br = r if br is None else br

    def body(w_ref, g_ref, m_ref, v_ref, d_ref, mo_ref, vo_ref):
        d_ref[...], mo_ref[...], vo_ref[...] = _adam_math(w_ref[...], g_ref[...], m_ref[...], v_ref[...])

    spec = pl.BlockSpec((br, c), lambda i: (i, 0))
    sh = jax.ShapeDtypeStruct((r, c), F32)
    return pl.pallas_call(
        body, name=name, grid=(r // br,), in_specs=[spec] * 4, out_specs=[spec] * 3, out_shape=(sh, sh, sh),
        compiler_params=_cparams(1),
    )(w, g, m, v)


def _place():
    x, y, c = lax.axis_index("x"), lax.axis_index("y"), lax.axis_index("c")
    return x, y, c, 4 * x + 2 * y + c


def _flip(x, y, c, k):
    px = 1 - x if (k >> 2) & 1 else x
    py = 1 - y if (k >> 1) & 1 else y
    pc = 1 - c if k & 1 else c
    return (px, py, pc), 4 * px + 2 * py + pc


def _gather_direct(buf_ref, send_sems, recv_sems, sem0=0):
    x, y, c, me = _place()
    sends = []
    for k in range(1, NDEV):
        dev, _ = _flip(x, y, c, k)
        cp = pltpu.make_async_remote_copy(
            src_ref=buf_ref.at[me], dst_ref=buf_ref.at[me],
            send_sem=send_sems.at[sem0 + k - 1], recv_sem=recv_sems.at[sem0 + k - 1],
            device_id=dev, device_id_type=MESH)
        cp.start()
        sends.append(cp)
    for k in range(1, NDEV):
        dev, idx = _flip(x, y, c, k)
        pltpu.make_async_remote_copy(
            src_ref=buf_ref.at[idx], dst_ref=buf_ref.at[idx],
            send_sem=send_sems.at[sem0 + k - 1], recv_sem=recv_sems.at[sem0 + k - 1],
            device_id=dev, device_id_type=MESH).wait_recv()
    for cp in sends:
        cp.wait_send()


def _mod_gather(c_row, cctx_row, wmod, bmod):
    ncol = wmod.shape[1]

    def body(c_ref, cctx_ref, wmod_ref, bmod_ref, mod_ref, cs_ref, cbuf, pbuf, send_sems, recv_sems):
        _, _, _, me = _place()
        cbuf[me] = jnp.broadcast_to(c_ref[...], (8, D))
        _gather_direct(cbuf, send_sems, recv_sems, 0)
        row = lax.broadcasted_iota(jnp.int32, (16, D), 0)
        call = jnp.where(row == 8, jnp.broadcast_to(cctx_ref[...], (16, D)), 0.0)
        for d in range(NDEV):
            call = jnp.where(row == d, jnp.concatenate([cbuf[d], cbuf[d]], axis=0), call)
        cs = call * _sigmoid(call)
        cs_ref[...] = cs
        pbuf[me] = _dot(cs.astype(BF16), wmod_ref[...].astype(BF16))
        _gather_direct(pbuf, send_sems, recv_sems, NDEV - 1)
        for d in range(NDEV):
            mod_ref[:, d * ncol:(d + 1) * ncol] = pbuf[d] + bmod_ref[:, d * ncol:(d + 1) * ncol]

    return pl.pallas_call(
        body, name="mod_gather",
        out_shape=(jax.ShapeDtypeStruct((16, 6 * D), F32), jax.ShapeDtypeStruct((16, D), F32)),
        scratch_shapes=[pltpu.VMEM((NDEV, 8, D), F32), pltpu.VMEM((NDEV, 16, ncol), F32),
                        pltpu.SemaphoreType.DMA((2 * (NDEV - 1),)), pltpu.SemaphoreType.DMA((2 * (NDEV - 1),))],
        compiler_params=_cparams(),
    )(c_row, cctx_row, wmod, bmod)


def _weight_gather(shards, after=()):
    n = len(shards)
    na = len(after)

    def body(*refs):
        in_refs, out_refs = refs[:n], refs[n + na:2 * n + na]
        cast_refs = refs[2 * n + na:3 * n + na]
        send_sems, recv_sems, local_sems = refs[3 * n + na:]
        x, y, c, me = _place()
        sib = (x, y, 1 - c)
        chips = [(1 - x, y), (x, 1 - y), (1 - x, 1 - y)]

        def blk(px, py, pc):
            return 4 * px + 2 * py + pc

        def copy(w, k, block, to, src=None):
            dst = out_refs[w].at[block]
            return pltpu.make_async_remote_copy(
                src_ref=dst if src is None else src, dst_ref=dst,
                send_sem=send_sems.at[w, k], recv_sem=recv_sems.at[w, k], device_id=to, device_id_type=MESH)

        first, passed, mine = [], [], []
        for w in range(n):
            cast_refs[w][...] = in_refs[w][...].astype(BF16)
            m = pltpu.make_async_copy(cast_refs[w], out_refs[w].at[me], local_sems.at[w])
            m.start()
            mine.append(m)
            cps = [copy(w, 0, me, sib, src=cast_refs[w])]
            cps += [copy(w, 1 + j, me, (*chip, c), src=cast_refs[w]) for j, chip in enumerate(chips)]
            for cp in cps:
                cp.start()
            first += cps
        for w in range(n):
            for j, chip in enumerate(chips):
                b = blk(*chip, c)
                copy(w, 1 + j, b, (x, y, c)).wait_recv()
                fw = copy(w, 4 + j, b, sib)
                fw.start()
                passed.append(fw)
        for w in range(n):
            copy(w, 0, blk(x, y, 1 - c), (x, y, c)).wait_recv()
            for j, chip in enumerate(chips):
                copy(w, 4 + j, blk(*chip, 1 - c), (x, y, c)).wait_recv()
        for cp in first + passed:
            cp.wait_send()
        for m in mine:
            m.wait()

    vm = pl.BlockSpec(memory_space=pltpu.VMEM)
    hbm = pl.BlockSpec(memory_space=pltpu.HBM)
    return pl.pallas_call(
        body, name="weight_gather",
        in_specs=[vm] * (n + na), out_specs=[hbm] * n,
        out_shape=tuple(jax.ShapeDtypeStruct((NDEV,) + s.shape, BF16) for s in shards),
        scratch_shapes=[pltpu.VMEM(s.shape, BF16) for s in shards]
        + [pltpu.SemaphoreType.DMA((n, 7)), pltpu.SemaphoreType.DMA((n, 7)), pltpu.SemaphoreType.DMA((n,))],
        compiler_params=_cparams(),
    )(*shards, *after)


_HBM = pl.BlockSpec(memory_space=pltpu.HBM)
_SEMS = pl.BlockSpec(memory_space=pltpu.SEMAPHORE)
_EFFECT = pltpu.SideEffectType.DATAFLOW_SIDE_EFFECTING


def _exchange_copy(src_refs, land_refs, send_sems, recv_sems, w, k, scatter, landing_slot_is_mine):
    x, y, c, me = _place()
    dev, pidx = _flip(x, y, c, k)
    src = src_refs[w].at[pidx] if scatter else src_refs[w]
    slot = me if landing_slot_is_mine else pidx
    return pltpu.make_async_remote_copy(
        src_ref=src, dst_ref=land_refs[w].at[slot],
        send_sem=send_sems.at[w * (NDEV - 1) + k - 1], recv_sem=recv_sems.at[w * (NDEV - 1) + k - 1],
        device_id=dev, device_id_type=MESH)


def _exchange_start(srcs, *, scatter, name):
    n = len(srcs)
    lands = [lax.empty((NDEV,) + tuple(s.shape[-2:]), s.dtype) for s in srcs]

    def body(*refs):
        src_refs, land_refs = refs[:n], refs[n:2 * n]
        send_sems, recv_sems = refs[2 * n], refs[2 * n + 1]
        token = refs[-1]
        for w in range(n):
            for k in range(1, NDEV):
                _exchange_copy(src_refs, land_refs, send_sems, recv_sems, w, k, scatter, True).start()
        token[...] = jnp.zeros_like(token)

    arrs = list(srcs) + lands
    out_shape = ([pltpu.SemaphoreType.DMA((n * (NDEV - 1),)), pltpu.SemaphoreType.DMA((n * (NDEV - 1),))]
                 + [pltpu.HBM(a.shape, a.dtype) for a in arrs] + [jax.ShapeDtypeStruct((8, 128), F32)])
    res = pl.pallas_call(
        body, name=name, out_shape=tuple(out_shape),
        in_specs=[_HBM] * (2 * n),
        out_specs=tuple([_SEMS, _SEMS] + [_HBM] * (2 * n) + [pl.BlockSpec(memory_space=pltpu.VMEM)]),
        input_output_aliases={i: 2 + i for i in range(2 * n)},
        compiler_params=pltpu.CompilerParams(has_side_effects=_EFFECT),
    )(*[pltpu.with_memory_space_constraint(a, pltpu.HBM) for a in arrs])
    return res[:-1], res[-1]


def _exchange_wait(handles, after, *, scatter, name):
    n = (len(handles) - 2) // 2
    na = len(after)

    def body(*refs):
        src_refs, land_refs = refs[:n], refs[n:2 * n]
        send_sems, recv_sems = refs[2 * n], refs[2 * n + 1]
        for w in range(n):
            for k in range(1, NDEV):
                cp = _exchange_copy(src_refs, land_refs, send_sems, recv_sems, w, k, scatter, False)
                cp.wait_send()
                cp.wait_recv()

    arrs = handles[2:]
    res = pl.pallas_call(
        body, name=name, out_shape=tuple(pltpu.HBM(a.shape, a.dtype) for a in arrs),
        in_specs=[_HBM] * (2 * n) + [_SEMS, _SEMS] + [_HBM] * na,
        out_specs=tuple([_HBM] * (2 * n)),
        input_output_aliases={i: i for i in range(2 * n)},
        compiler_params=pltpu.CompilerParams(has_side_effects=_EFFECT),
    )(*arrs, handles[0], handles[1], *[pltpu.with_memory_space_constraint(a, pltpu.HBM) for a in after])
    return res[:n], res[n:]


def _cast_bf16(arrs, *, name, after=()):
    n = len(arrs)

    def body(*refs):
        for i_ref, o_ref in zip(refs[:n], refs[n + len(after):]):
            o_ref[...] = i_ref[...].astype(BF16)

    return pl.pallas_call(
        body, name=name, out_shape=tuple(jax.ShapeDtypeStruct(a.shape, BF16) for a in arrs),
        in_specs=[pl.BlockSpec(memory_space=pltpu.VMEM)] * n + [pl.BlockSpec(memory_space=pl.ANY)] * len(after),
        compiler_params=_cparams())(*arrs, *after)


def _rs_adam(me_arr, own, land, w, m, v, *, name):
    def body(me_ref, own_ref, land_ref, w_ref, m_ref, v_ref, g_ref, d_ref, mo_ref, vo_ref):
        me = me_ref[0]
        acc = jnp.zeros(own_ref.shape, F32)
        for p in range(NDEV):
            acc = acc + jnp.where(p == me, own_ref[...], land_ref[p].astype(F32))
        g_ref[...] = acc
        d_ref[...], mo_ref[...], vo_ref[...] = _adam_math(w_ref[...], acc, m_ref[...], v_ref[...])

    sh = jax.ShapeDtypeStruct(own.shape, F32)
    vm = pl.BlockSpec(memory_space=pltpu.VMEM)
    return pl.pallas_call(
        body, name=name, out_shape=(sh, sh, sh, sh),
        in_specs=[pl.BlockSpec(memory_space=pltpu.SMEM)] + [vm] * 5,
        compiler_params=_cparams(),
    )(me_arr, own, land, w, m, v)


ROW_F, ROW_I, ROW_C, ROW_G1, ROW_LG = 0, 8, 16, 24, 32
PACK_ROWS = 40


def _small_reduce(pack, dsgw, wmod, cctx_row, m_row, v_row):
    ncol = wmod.shape[1]

    def body(pack_ref, sgw_ref, wmod_ref, c_ref, m_ref, v_ref,
             sum_ref, all_ref, sgw_sum_ref, g_ref, d_ref, mo_ref, vo_ref, sgw_all, cbuf, send_sems, recv_sems):
        _, _, _, me = _place()
        all_ref[me] = pack_ref[...]
        sgw_all[me] = sgw_ref[...]
        _gather_direct(all_ref, send_sems, recv_sems, 0)
        _gather_direct(sgw_all, send_sems, recv_sems, NDEV - 1)
        acc = all_ref[0]
        acc_w = sgw_all[0]
        for d in range(1, NDEV):
            acc = acc + all_ref[d]
            acc_w = acc_w + sgw_all[d]
        sum_ref[...] = acc
        sgw_sum_ref[...] = acc_w

        zero = jnp.zeros((1, D), F32)
        dcmod = jnp.concatenate([acc[ROW_C:ROW_C + 1], acc[ROW_C + 1:ROW_C + 2], zero, zero, zero, zero], axis=1)
        mine = jnp.zeros((1, ncol), F32)
        for d in range(NDEV):
            mine = mine + jnp.where(d == me, dcmod[:, d * ncol:(d + 1) * ncol], 0.0)
        cbuf[me] = _dot_nt(jnp.broadcast_to(mine, (8, ncol)).astype(BF16), wmod_ref[...].astype(BF16))
        _gather_direct(cbuf, send_sems, recv_sems, 2 * (NDEV - 1))
        tot = cbuf[0]
        for d in range(1, NDEV):
            tot = tot + cbuf[d]
        cc = c_ref[...]
        _, dsilu = _silu_parts(cc)
        g = tot[0:1, :] * dsilu
        g_ref[...] = g
        d_ref[...], mo_ref[...], vo_ref[...] = _adam_math(cc, g, m_ref[...], v_ref[...])

    row = jax.ShapeDtypeStruct((1, D), F32)
    return pl.pallas_call(
        body, name="small_reduce",
        out_shape=(jax.ShapeDtypeStruct((PACK_ROWS, D), F32), jax.ShapeDtypeStruct((NDEV, PACK_ROWS, D), F32),
                   jax.ShapeDtypeStruct((G, C, C), F32), row, row, row, row),
        scratch_shapes=[pltpu.VMEM((NDEV, G, C, C), F32), pltpu.VMEM((NDEV, 8, D), F32),
                        pltpu.SemaphoreType.DMA((3 * (NDEV - 1),)), pltpu.SemaphoreType.DMA((3 * (NDEV - 1),))],
        compiler_params=_cparams(),
    )(pack, dsgw, wmod, cctx_row, m_row, v_row)


def _adam_small(s, sgw_g, params):
    names = ["norm1", "norm2", "norm_f", "sg_gain", "sg_b", "ret_logit_f", "ret_logit_b", "b_mod", "sg_w"]
    flat = [a for n in names for a in params[n]]

    def body(*refs):
        s_ref, sgw_ref = refs[0], refs[1]
        p_refs = refs[2:2 + 3 * len(names)]
        o_refs = refs[2 + 3 * len(names):]
        loss_ref = o_refs[-1]

        def row(r):
            return s_ref[r:r + 1, :]

        grads = {
            "norm1": row(ROW_I + 2) + row(ROW_C + 2),
            "norm2": row(ROW_F + 4),
            "norm_f": row(ROW_F + 0),
            "sg_gain": s_ref[ROW_I + 3:ROW_I + 4, 0:AW],
            "sg_b": s_ref[ROW_I + 4:ROW_I + 8, 0:DH],
            "sg_w": sgw_ref[...],
        }
        for j, n in enumerate(names):
            w_ref, m_ref, v_ref = p_refs[3 * j:3 * j + 3]
            g_ref, d_ref, mo_ref, vo_ref = o_refs[4 * j:4 * j + 4]
            w = w_ref[...]
            if n in ("ret_logit_f", "ret_logit_b"):
                r = ROW_LG + (0 if n == "ret_logit_f" else 1)
                g = s_ref[r:r + 1, 0:H] * _sigmoid(-w)
            elif n == "b_mod":
                rows = [row(ROW_I + 0) + row(ROW_C + 0), row(ROW_I + 1) + row(ROW_C + 1), row(ROW_G1),
                        row(ROW_F + 2), row(ROW_F + 3), row(ROW_F + 1)]
                g = jnp.concatenate(rows, axis=1)
            else:
                g = grads[n]
            g_ref[...] = g
            d_ref[...], mo_ref[...], vo_ref[...] = _adam_math(w, g, m_ref[...], v_ref[...])
        loss_ref[...] = jnp.full((1, 1), 0.5 / D, F32) * jnp.sum(row(ROW_F + 5), axis=1, keepdims=True)

    out_shape = []
    for n in names:
        w = params[n][0]
        out_shape += [jax.ShapeDtypeStruct(w.shape, F32)] * 4
    out_shape.append(jax.ShapeDtypeStruct((1, 1), F32))
    outs = pl.pallas_call(body, name="adam_small", out_shape=tuple(out_shape), compiler_params=_cparams())(
        s, sgw_g, *flat)
    return {n: tuple(outs[4 * j:4 * j + 4]) for j, n in enumerate(names)}, outs[-1]


def _wmod_grad(cs_all, dmod_cols, wmod, m, v):
    ncol = wmod.shape[1]

    def body(cs_ref, dm_ref, w_ref, m_ref, v_ref, g_ref, d_ref, mo_ref, vo_ref):
        g = _dot_tn(cs_ref[...].astype(BF16), dm_ref[...].astype(BF16))
        g_ref[...] = g
        d_ref[...], mo_ref[...], vo_ref[...] = _adam_math(w_ref[...], g, m_ref[...], v_ref[...])

    sh = jax.ShapeDtypeStruct((D, ncol), F32)
    return pl.pallas_call(
        body, name="wmod_grad", out_shape=(sh, sh, sh, sh),
        compiler_params=_cparams(),
    )(cs_all, dmod_cols, wmod, m, v)


def _rope_tables(t_len):
    n_freq = DH // 4
    inv = (ROPE_BASE ** (-np.arange(n_freq, dtype=np.float32) / n_freq)).astype(np.float32)
    tok = np.arange(t_len)
    rows = (tok // GRID_W).astype(np.float32)
    cols = (tok % GRID_W).astype(np.float32)
    ang_r = rows[:, None] * inv[None, :]
    ang_c = cols[:, None] * inv[None, :]
    cos = np.concatenate([np.cos(ang_r)] * 2 + [np.cos(ang_c)] * 2, axis=1).astype(np.float32)
    sin = np.concatenate([-np.sin(ang_r), np.sin(ang_r), -np.sin(ang_c), np.sin(ang_c)], axis=1).astype(np.float32)
    return jnp.asarray(cos), jnp.asarray(sin)


def _pad_row(v, width=D):
    v = v.reshape(1, -1)
    return jnp.pad(v, ((0, 0), (0, width - v.shape[1])))


def _local_step(x, tgt, ctx, mod6, cmod6, norm1, norm2, normf, win, wout, ffn_weights, sgw, sgb, sggain, rlf, rlb,
                *, tm_a, tm_b, tm_f, tm_m, tm_r, tm_i, bt_w, after_first_grads=None, small_path=None):
    t_len = x.shape[0]
    cos, sin = _rope_tables(t_len)
    sgbt = jnp.broadcast_to(sgb[:, :, None], (G, C, 128))
    rlf = jnp.broadcast_to(rlf.reshape(H, 1), (H, 128))
    rlb = jnp.broadcast_to(rlb.reshape(H, 1), (H, 128))
    hc, kvc, scf, scb = _ctx_fwd(ctx, cmod6, norm1, win, rlf, rlb)
    hx, zuv, q, k, v, gfb, of, ya, sst = _fwd_a(x, mod6, norm1, win, sgw, sgbt, sggain, cos, sin, rlf, scf, tm=tm_a)
    ob, ycat, y, x1, rst = _fwd_b(q, k, v, of, gfb, ya, x, mod6, wout, rlb, scb, tm=tm_b)
    wg, wu, wd = ffn_weights(x1) if callable(ffn_weights) else ffn_weights
    dx1, h2, da, db, hm, df, small_f = _ffn(x1, tgt, mod6, norm2, normf, wg, wu, wd, tm=tm_f)
    d_wd, d_wd_b = _tn_matmul(hm, df, bm=DFF, bt=bt_w, name="dw_down")
    d_wg, d_wg_b = _tn_matmul(da, h2, bm=DFF, bt=bt_w, name="dw_gate")
    d_wu, d_wu_b = _tn_matmul(db, h2, bm=DFF, bt=bt_w, name="dw_up")
    dy, dya, dgfb, dof, dob, dg1 = _mid_bwd(dx1, y, of, ob, gfb, mod6, wout, tm=tm_m)
    d_wo, d_wo_b = _tn_matmul(ycat, dy, bm=D, bt=bt_w, name="dw_out")
    if after_first_grads is not None:
        rlf = rlf + after_first_grads((d_wd_b, d_wg_b, d_wu_b, d_wo_b))
    dqf, dkf, dvf, dqb, dkb, dvb, dscf, dscb, dlg = _ret_bwd(q, k, v, dof, dob, sst, rst, rlf, rlb, tm=tm_r)
    gx, dz, small_i, dsgw = _in_bwd(x, zuv, dya, dqf, dkf, dvf, dqb, dkb, dvb, dgfb, dx1, cos, sin,
                                    mod6, norm1, win, sgw, sgbt, sggain, tm=tm_i)
    dwin_c, small_c, dlg = _ctx_bwd(ctx, hc, kvc, cmod6, norm1, win, rlf, rlb, dscf, dscb, dlg)
    pack = jnp.concatenate([small_f, small_i, small_c, dg1, dlg], axis=0)
    after = small_path(pack, dsgw) if small_path is not None else ()
    d_wi, d_wi_b = _tn_matmul(dz, hx, bm=INC, bt=min(512, bt_w), name="dw_in", init=dwin_c,
                              init_rows=(KV_LO, KV_HI), after=after)
    grads = {"w_in": (d_wi, d_wi_b), "w_out": (d_wo, d_wo_b), "w_gate": (d_wg, d_wg_b), "w_up": (d_wu, d_wu_b),
             "w_down": (d_wd, d_wd_b)}
    return gx, grads, pack, dsgw


def kernel(x, c, ctx, c_ctx, w_mod, b_mod, norm1, w_in, sg_gain, sg_w, sg_b, ret_logit_f, ret_logit_b, w_out, norm2, w_gate, w_up, w_down, norm_f, loss_target, m_c_ctx, m_w_mod, m_b_mod, m_norm1, m_w_in, m_sg_gain, m_sg_w, m_sg_b, m_ret_logit_f, m_ret_logit_b, m_w_out, m_norm2, m_w_gate, m_w_up, m_w_down, m_norm_f, v_c_ctx, v_w_mod, v_b_mod, v_norm1, v_w_in, v_sg_gain, v_sg_w, v_sg_b, v_ret_logit_f, v_ret_logit_b, v_w_out, v_norm2, v_w_gate, v_w_up, v_w_down, v_norm_f):
    t_len = x.shape[1]
    tm = min(512, t_len)
    me = 4 * lax.axis_index("x") + 2 * lax.axis_index("y") + lax.axis_index("c")
    tr = lambda a: jnp.transpose(a[0])

    cctx_row = c_ctx.reshape(1, D)
    mod_all, cs_all = _mod_gather(c, cctx_row, w_mod[0], b_mod)
    mod6 = lax.dynamic_slice(mod_all, (me, 0), (1, 6 * D)).reshape(6, D)
    cmod6 = mod_all[8].reshape(6, D)

    g_in, g_out = _weight_gather([tr(w_in), w_out[0]], after=[cs_all])
    win_t = g_in.reshape(INC, D)
    wout = g_out.reshape(D, D)

    ffn_shards = _cast_bf16([tr(w_gate), tr(w_up), w_down[0]], name="cast_ffn", after=[g_in])
    h_ffn, tok = _exchange_start(ffn_shards, scatter=False, name="wffn_start")
    mod6 = mod6 + tok[0, 0]

    def ffn_weights(after):
        own, lands = _exchange_wait(h_ffn, [after], scatter=False, name="wffn_wait")
        return [lax.dynamic_update_slice(l, o[None], (me, 0, 0)).reshape(DFF, D) for o, l in zip(own, lands)]

    rs, outs = {}, {}

    def after_first_grads(bf):
        rs["first"], tok_first = _exchange_start([b.reshape(NDEV, b.shape[0] // NDEV, D) for b in bf], scatter=True,
                                                 name="rs_first_start")
        return tok_first[0, 0]

    def small_path(pack, dsgw):
        psum_rows, pall, sgw_sum, g_cc, d_cc, m_cc, v_cc = _small_reduce(
            pack, dsgw, w_mod[0], cctx_row, m_c_ctx.reshape(1, D), v_c_ctx.reshape(1, D))
        outs["c_ctx"] = tuple(a.reshape(D) for a in (g_cc, d_cc, m_cc, v_cc))
        dmod_rows = (ROW_I + 0, ROW_I + 1, ROW_G1, ROW_F + 2, ROW_F + 3, ROW_F + 1)
        dmod_all = jnp.concatenate([pall[:, r, :] for r in dmod_rows], axis=1)
        dcmod = jnp.concatenate([psum_rows[ROW_C + 0:ROW_C + 1], psum_rows[ROW_C + 1:ROW_C + 2],
                                 jnp.zeros((1, 4 * D), F32)], axis=1)
        dmod_mat = jnp.concatenate([dmod_all, jnp.pad(dcmod, ((0, 7), (0, 0)))], axis=0)
        ncol = 6 * D // NDEV
        dmod_cols = lax.dynamic_slice(dmod_mat, (0, me * ncol), (16, ncol))
        g_wm, d_wm, m_wm, v_wm = _wmod_grad(cs_all, dmod_cols, w_mod[0], m_w_mod[0], v_w_mod[0])
        outs["w_mod"] = (g_wm[None], d_wm[None], m_wm[None], v_wm[None])
        small_params = {
            "norm1": (norm1, m_norm1, v_norm1), "norm2": (norm2, m_norm2, v_norm2),
            "norm_f": tuple(a.reshape(1, D) for a in (norm_f, m_norm_f, v_norm_f)),
            "sg_gain": (sg_gain, m_sg_gain, v_sg_gain), "sg_b": (sg_b[0], m_sg_b[0], v_sg_b[0]),
            "ret_logit_f": (ret_logit_f, m_ret_logit_f, v_ret_logit_f),
            "ret_logit_b": (ret_logit_b, m_ret_logit_b, v_ret_logit_b),
            "b_mod": (b_mod, m_b_mod, v_b_mod), "sg_w": (sg_w[0], m_sg_w[0], v_sg_w[0])}
        small, loss = _adam_small(psum_rows, sgw_sum, small_params)
        back = {"norm_f": lambda a: a.reshape(D), "sg_b": lambda a: a[None], "sg_w": lambda a: a[None]}
        for name, vals in small.items():
            outs[name] = tuple(back.get(name, lambda a: a)(a) for a in vals)
        outs["loss"] = loss
        return [pall]

    gx, grads, pack, dsgw = _local_step(
        x[0], loss_target[0], ctx[0], mod6, cmod6, norm1, norm2[0:1], norm_f.reshape(1, D), win_t, wout, ffn_weights,
        sg_w[0], sg_b[0], sg_gain, ret_logit_f, ret_logit_b,
        tm_a=tm, tm_b=tm, tm_f=min(256, t_len), tm_m=tm, tm_r=tm, tm_i=tm, bt_w=min(1024, t_len),
        after_first_grads=after_first_grads, small_path=small_path)

    h_in, tok_in = _exchange_start([grads["w_in"][1].reshape(NDEV, INC // NDEV, D)], scatter=True, name="rs_in_start")
    _, lands_first = _exchange_wait(rs["first"], [tok_in], scatter=True, name="rs_first_wait")
    me_arr = me.reshape(1).astype(jnp.int32)

    def finish(name, land, w, m, v, transposed):
        rows = land.shape[1]
        own = lax.dynamic_slice(grads[name][0], (me * rows, 0), (rows, D))
        take = tr if transposed else (lambda a: a[0])
        res = _rs_adam(me_arr, own, land, take(w), take(m), take(v), name="adam_" + name)
        put = (lambda a: jnp.transpose(a)[None]) if transposed else (lambda a: a[None])
        outs[name] = tuple(put(a) for a in res)
        return res[0]

    g_down = finish("w_down", lands_first[0], w_down, m_w_down, v_w_down, False)
    g_gate = finish("w_gate", lands_first[1], w_gate, m_w_gate, v_w_gate, True)
    g_up = finish("w_up", lands_first[2], w_up, m_w_up, v_w_up, True)
    g_out = finish("w_out", lands_first[3], w_out, m_w_out, v_w_out, False)
    _, lands_in = _exchange_wait(h_in, [g_down, g_gate, g_up, g_out], scatter=True, name="rs_in_wait")
    finish("w_in", lands_in[0], w_in, m_w_in, v_w_in, True)
    loss = outs.pop("loss")

    order = ["c_ctx", "w_mod", "b_mod", "norm1", "w_in", "sg_gain", "sg_w", "sg_b", "ret_logit_f", "ret_logit_b",
             "w_out", "norm2", "w_gate", "w_up", "w_down", "norm_f"]
    res = [loss.reshape(()), gx[None]]
    for j in range(4):
        res += [outs[name][j] for name in order]
    return tuple(res)
```

```python
import functools
import math

import numpy as np
import jax
import jax.numpy as jnp
from jax import lax
from jax.experimental import pallas as pl
from jax.experimental.pallas import tpu as pltpu

F32 = jnp.float32
BF16 = jnp.bfloat16

D = 1024
AW = 512
RW = 512
H = 4
DH = 128
G = 4
C = 128
CR = 256
DFF = 2816
INC = 3584
KV_LO, KV_HI = 1536, 2560
NDEV = 8
EPS = 1e-6
KSCALE = DH ** -0.5
ROPE_BASE = 10000.0
GRID_W = 64

ADAM_LR = 0.001
ADAM_B1 = 0.9
ADAM_B2 = 0.999
ADAM_EPS = 1e-08
ADAM_WD = 0.01
ADAM_STEP = 10

VMEM_LIMIT = 56 * 1024 * 1024
MESH = pl.DeviceIdType.MESH


def _cparams(n_grid=0, **kw):
    sem = ("arbitrary",) * n_grid if n_grid else None
    return pltpu.CompilerParams(dimension_semantics=sem, vmem_limit_bytes=VMEM_LIMIT, **kw)


def _dot(a, b):
    return jnp.dot(a, b, preferred_element_type=F32)


def _dot_nt(a, b):
    return lax.dot_general(a, b, (((1,), (1,)), ((), ())), preferred_element_type=F32)


def _dot_tn(a, b):
    return lax.dot_general(a, b, (((0,), (0,)), ((), ())), preferred_element_type=F32)


def _whole(shape):
    nd = len(shape)
    return pl.BlockSpec(shape, lambda *_: (0,) * nd, pipeline_mode=pl.Buffered(1))


def _acc(shape):
    nd = len(shape)
    return pl.BlockSpec(shape, lambda *_: (0,) * nd)


def _rows(tm, n):
    return pl.BlockSpec((tm, n), lambda i: (i, 0))


def _rows_rev(tm, n, nt):
    return pl.BlockSpec((tm, n), lambda i: (nt - 1 - i, 0))


def _cols(n, tm):
    return pl.BlockSpec((n, tm), lambda i: (0, i))


def _sigmoid(x):
    return 1.0 / (1.0 + jnp.exp(-x))


def _log_sigmoid(x):
    return jnp.minimum(x, 0.0) - jnp.log(1.0 + jnp.exp(-jnp.abs(x)))


_GK0 = math.sqrt(2.0 / math.pi)
_GK1 = 0.044715


def _gelu(x):
    x2 = x * x
    t = jnp.tanh(_GK0 * x * (1.0 + _GK1 * x2))
    g = 0.5 * x * (1.0 + t)
    dg = 0.5 * (1.0 + t) + 0.5 * x * (1.0 - t * t) * (_GK0 * (1.0 + 3.0 * _GK1 * x2))
    return g, dg


def _silu_parts(a):
    s = _sigmoid(a)
    return a * s, s * (1.0 + a * (1.0 - s))


def _rope(t, cos, sins):
    n = t.shape[1]
    lane = lax.broadcasted_iota(jnp.int32, t.shape, 1)
    first = (lane & 63) < 32
    partner = jnp.where(first, pltpu.roll(t, n - 32, 1), pltpu.roll(t, 32, 1))
    return t * cos + partner * sins


def _headnorm(o):
    outs, rs = [], []
    for h in range(H):
        oh = o[:, h * DH:(h + 1) * DH]
        r = lax.rsqrt(jnp.mean(oh * oh, axis=-1, keepdims=True) + EPS)
        outs.append(oh * r)
        rs.append(r)
    return jnp.concatenate(outs, axis=1), rs


def _decay_consts(lg, forward):
    ii = lax.broadcasted_iota(jnp.int32, (CR, CR), 0).astype(F32)
    jj = lax.broadcasted_iota(jnp.int32, (CR, CR), 1).astype(F32)
    ic = lax.broadcasted_iota(jnp.int32, (CR, 1), 0).astype(F32)
    if forward:
        diff = ii - jj
        xi = jnp.exp(lg * (ic + 1.0))
        z = jnp.exp(lg * (CR - 1.0 - ic))
    else:
        diff = jj - ii
        xi = jnp.exp(lg * (CR - ic))
        z = jnp.exp(lg * ic)
    dm = jnp.where(diff >= 0.0, jnp.exp(lg * jnp.maximum(diff, 0.0)), 0.0)
    dec = jnp.exp(lg * float(CR))
    return dm, xi, z, dec, ic


def _ctx_fwd(ctx, cmod6, norm1, win, rlf, rlb):
    lc = ctx.shape[0]

    def body(ctx_ref, cmod_ref, n1_ref, win_ref, rlf_ref, rlb_ref, hc_ref, kvc_ref, scf_ref, scb_ref):
        x = ctx_ref[...]
        r = lax.rsqrt(jnp.mean(x * x, axis=-1, keepdims=True) + EPS)
        hc = (x * r) * (n1_ref[...] * (1.0 + cmod_ref[1:2, :])) + cmod_ref[0:1, :]
        hcb = hc.astype(BF16)
        hc_ref[...] = hcb
        kv = _dot_nt(hcb, win_ref[KV_LO:KV_HI, :])
        k = kv[:, :RW] * KSCALE
        v = kv[:, RW:]
        kvc_ref[:, :RW] = k
        kvc_ref[:, RW:] = v
        t = lax.broadcasted_iota(jnp.int32, (lc, 1), 0).astype(F32)
        lgf = _log_sigmoid(rlf_ref[...])
        lgb = _log_sigmoid(rlb_ref[...])
        for h in range(H):
            hs = slice(h * DH, (h + 1) * DH)
            wf = jnp.exp(lgf[h:h + 1, 0:1] * (lc - 1.0 - t))
            wb = jnp.exp(lgb[h:h + 1, 0:1] * t)
            vh = v[:, hs].astype(BF16)
            scf_ref[h] = _dot_tn((k[:, hs] * wf).astype(BF16), vh)
            scb_ref[h] = _dot_tn((k[:, hs] * wb).astype(BF16), vh)

    return pl.pallas_call(
        body, name="ctx_fwd",
        out_shape=(jax.ShapeDtypeStruct((lc, D), BF16), jax.ShapeDtypeStruct((lc, 2 * RW), F32),
                   jax.ShapeDtypeStruct((H, DH, DH), F32), jax.ShapeDtypeStruct((H, DH, DH), F32)),
        compiler_params=_cparams(),
    )(ctx, cmod6, norm1, win, rlf, rlb)


def _sg_forward(u, v, sgw_ref, sgbt_ref, sgg_ref, nc):
    ua, dgu = _gelu(u)
    va, dgv = _gelu(v)
    gain = sgg_ref[...]
    mixed, vn_b, vah_l, rv_l = [], [], [], []
    for g in range(G):
        gs = slice(g * DH, (g + 1) * DH)
        vag = va[:, gs]
        rv = lax.rsqrt(jnp.mean(vag * vag, axis=-1, keepdims=True) + EPS)
        vah = vag * rv
        vn = (vah * gain[:, gs]).astype(BF16)
        wg = sgw_ref[g].astype(BF16)
        bcol = sgbt_ref[g]
        rows = [_dot(wg, vn[c * C:(c + 1) * C, :]) + bcol for c in range(nc)]
        mixed.append(jnp.concatenate(rows, axis=0) if nc > 1 else rows[0])
        vn_b.append(vn)
        vah_l.append(vah)
        rv_l.append(rv)
    mixed = jnp.concatenate(mixed, axis=1)
    return ua * mixed, (ua, dgu, dgv, mixed, vn_b, vah_l, rv_l)


def _fwd_a(x, mod6, norm1, win, sgw, sgbt, sggain, cos, sin, rlf, scf, *, tm):
    t_len = x.shape[0]
    nt, nc, ncr = t_len // tm, tm // C, tm // CR

    def body(x_ref, mod_ref, n1_ref, win_ref, sgw_ref, sgbt_ref, sgg_ref, cos_ref, sin_ref, rlf_ref, scf_ref,
             hx_ref, zuv_ref, q_ref, k_ref, v_ref, gfb_ref, of_ref, ya_ref, sst_ref, s_scr):
        i = pl.program_id(0)

        @pl.when(i == 0)
        def _():
            s_scr[...] = scf_ref[...]

        x = x_ref[...]
        r = lax.rsqrt(jnp.mean(x * x, axis=-1, keepdims=True) + EPS)
        hx = (x * r) * (n1_ref[...] * (1.0 + mod_ref[1:2, :])) + mod_ref[0:1, :]
        hxb = hx.astype(BF16)
        hx_ref[...] = hxb
        z = _dot_nt(hxb, win_ref[...])
        zuv_ref[...] = z[:, :2 * AW]
        gfb_ref[...] = z[:, 2560:3584].astype(BF16)
        cos = jnp.tile(cos_ref[...], (1, H))
        sins = jnp.tile(sin_ref[...], (1, H))
        q_ref[...] = _rope(z[:, 1024:1536], cos, sins).astype(BF16)
        k_ref[...] = (_rope(z[:, 1536:2048], cos, sins) * KSCALE).astype(BF16)
        v_ref[...] = z[:, 2048:2560].astype(BF16)
        ya, _ = _sg_forward(z[:, :AW], z[:, AW:2 * AW], sgw_ref, sgbt_ref, sgg_ref, nc)
        ya_ref[...] = ya.astype(BF16)

        lg = _log_sigmoid(rlf_ref[...])
        for h in range(H):
            dm, xi, zc, dec, _ = _decay_consts(lg[h:h + 1, 0:1], True)
            hs = slice(h * DH, (h + 1) * DH)
            for c in range(ncr):
                rs = slice(c * CR, (c + 1) * CR)
                qc, kc, vc = q_ref[rs, hs], k_ref[rs, hs], v_ref[rs, hs]
                s = s_scr[h]
                sst_ref[c, h] = s
                a = (_dot_nt(qc, kc) * dm).astype(BF16)
                of_ref[rs, hs] = (_dot(a, vc) + xi * _dot(qc, s.astype(BF16))).astype(BF16)
                kz = (kc.astype(F32) * zc).astype(BF16)
                s_scr[h] = dec * s + _dot_tn(kz, vc)

    n_chunks = t_len // CR
    return pl.pallas_call(
        body, name="fwd_a", grid=(nt,),
        in_specs=[_rows(tm, D), _whole((6, D)), _whole((1, D)), _whole((INC, D)), _whole((G, C, C)),
                  _whole((G, C, 128)), _whole((1, AW)), _rows(tm, DH), _rows(tm, DH), _whole((H, 128)),
                  _whole((H, DH, DH))],
        out_specs=[_rows(tm, D), _rows(tm, 2 * AW), _rows(tm, RW), _rows(tm, RW), _rows(tm, RW),
                   _rows(tm, 2 * RW), _rows(tm, RW), _rows(tm, AW),
                   pl.BlockSpec((ncr, H, DH, DH), lambda i: (i, 0, 0, 0))],
        out_shape=(jax.ShapeDtypeStruct((t_len, D), BF16), jax.ShapeDtypeStruct((t_len, 2 * AW), F32),
                   jax.ShapeDtypeStruct((t_len, RW), BF16), jax.ShapeDtypeStruct((t_len, RW), BF16),
                   jax.ShapeDtypeStruct((t_len, RW), BF16), jax.ShapeDtypeStruct((t_len, 2 * RW), BF16),
                   jax.ShapeDtypeStruct((t_len, RW), BF16), jax.ShapeDtypeStruct((t_len, AW), BF16),
                   jax.ShapeDtypeStruct((n_chunks, H, DH, DH), F32)),
        scratch_shapes=[pltpu.VMEM((H, DH, DH), F32)],
        compiler_params=_cparams(1),
    )(x, mod6, norm1, win, sgw, sgbt, sggain, cos, sin, rlf, scf)


def _fwd_b(q, k, v, of, gfb, ya, x, mod6, wout, rlb, scb, *, tm):
    t_len = x.shape[0]
    nt, nc = t_len // tm, tm // CR

    def body(q_ref, k_ref, v_ref, of_ref, gfb_ref, ya_ref, x_ref, mod_ref, wout_ref, rlb_ref, scb_ref,
             ob_ref, ycat_ref, y_ref, x1_ref, rst_ref, r_scr):
        i = pl.program_id(0)

        @pl.when(i == 0)
        def _():
            r_scr[...] = scb_ref[...]

        lg = _log_sigmoid(rlb_ref[...])
        for h in range(H):
            dm, xi, zc, dec, _ = _decay_consts(lg[h:h + 1, 0:1], False)
            hs = slice(h * DH, (h + 1) * DH)
            for c in reversed(range(nc)):
                rs = slice(c * CR, (c + 1) * CR)
                qc, kc, vc = q_ref[rs, hs], k_ref[rs, hs], v_ref[rs, hs]
                s = r_scr[h]
                rst_ref[c, h] = s
                a = (_dot_nt(qc, kc) * dm).astype(BF16)
                ob_ref[rs, hs] = (_dot(a, vc) + xi * _dot(qc, s.astype(BF16))).astype(BF16)
                kz = (kc.astype(F32) * zc).astype(BF16)
                r_scr[h] = dec * s + _dot_tn(kz, vc)

        gfb = gfb_ref[...].astype(F32)
        sf, _ = _silu_parts(gfb[:, :RW])
        sb, _ = _silu_parts(gfb[:, RW:])
        hnf, _ = _headnorm(of_ref[...].astype(F32))
        hnb, _ = _headnorm(ob_ref[...].astype(F32))
        yr = sf * hnf + sb * hnb
        ycat = jnp.concatenate([ya_ref[...], yr.astype(BF16)], axis=1)
        ycat_ref[...] = ycat.T
        y = _dot(ycat, wout_ref[...])
        y_ref[...] = y.astype(BF16)
        x1_ref[...] = x_ref[...] + mod_ref[2:3, :] * y

    n_chunks = t_len // CR
    rr = functools.partial(_rows_rev, nt=nt)
    return pl.pallas_call(
        body, name="fwd_b", grid=(nt,),
        in_specs=[rr(tm, RW), rr(tm, RW), rr(tm, RW), rr(tm, RW), rr(tm, 2 * RW), rr(tm, AW), rr(tm, D),
                  _whole((6, D)), _whole((D, D)), _whole((H, 128)), _whole((H, DH, DH))],
        out_specs=[rr(tm, RW), pl.BlockSpec((D, tm), lambda i: (0, nt - 1 - i)), rr(tm, D), rr(tm, D),
                   pl.BlockSpec((nc, H, DH, DH), lambda i: (nt - 1 - i, 0, 0, 0))],
        out_shape=(jax.ShapeDtypeStruct((t_len, RW), BF16), jax.ShapeDtypeStruct((D, t_len), BF16),
                   jax.ShapeDtypeStruct((t_len, D), BF16), jax.ShapeDtypeStruct((t_len, D), F32),
                   jax.ShapeDtypeStruct((n_chunks, H, DH, DH), F32)),
        scratch_shapes=[pltpu.VMEM((H, DH, DH), F32)],
        compiler_params=_cparams(1),
    )(q, k, v, of, gfb, ya, x, mod6, wout, rlb, scb)


def _ffn(x1, tgt, mod6, norm2, normf, wg, wu, wd, *, tm):
    t_len = x1.shape[0]
    nt = t_len // tm

    def body(x1_ref, tgt_ref, mod_ref, n2_ref, nf_ref, wg_ref, wu_ref, wd_ref,
             dx1_ref, h2_ref, da_ref, db_ref, hm_ref, df_ref, small_ref):
        i = pl.program_id(0)

        @pl.when(i == 0)
        def _():
            small_ref[...] = jnp.zeros_like(small_ref)

        sh2, sc2, g2 = mod_ref[3:4, :], mod_ref[4:5, :], mod_ref[5:6, :]
        n2, nf = n2_ref[...], nf_ref[...]
        x1 = x1_ref[...]
        r2 = lax.rsqrt(jnp.mean(x1 * x1, axis=-1, keepdims=True) + EPS)
        x1h = x1 * r2
        w2 = n2 * (1.0 + sc2)
        h2b = (x1h * w2 + sh2).astype(BF16)
        h2_ref[...] = h2b
        a = _dot_nt(h2b, wg_ref[...])
        b = _dot_nt(h2b, wu_ref[...])
        sa, dsa = _silu_parts(a)
        hmb = (sa * b).astype(BF16)
        hm_ref[...] = hmb.T
        f = _dot(hmb, wd_ref[...])
        x2 = x1 + g2 * f
        r3 = lax.rsqrt(jnp.mean(x2 * x2, axis=-1, keepdims=True) + EPS)
        x2h = x2 * r3
        diff = x2h * nf - tgt_ref[...]
        small_ref[5:6, :] += jnp.sum(diff * diff, axis=0, keepdims=True)
        dout = diff * (1.0 / D)
        small_ref[0:1, :] += jnp.sum(dout * x2h, axis=0, keepdims=True)
        dyg = dout * nf
        dx2 = r3 * (dyg - x2h * jnp.mean(dyg * x2h, axis=-1, keepdims=True))
        small_ref[1:2, :] += jnp.sum(dx2 * f, axis=0, keepdims=True)
        dfb = (dx2 * g2).astype(BF16)
        df_ref[...] = dfb
        dhm = _dot_nt(dfb, wd_ref[...])
        dbb = (dhm * sa).astype(BF16)
        dab = (dhm * b * dsa).astype(BF16)
        da_ref[...] = dab.T
        db_ref[...] = dbb.T
        dh2 = _dot(dab, wg_ref[...]) + _dot(dbb, wu_ref[...])
        small_ref[2:3, :] += jnp.sum(dh2, axis=0, keepdims=True)
        dhx = dh2 * x1h
        small_ref[3:4, :] += jnp.sum(dhx, axis=0, keepdims=True) * n2
        small_ref[4:5, :] += jnp.sum(dhx, axis=0, keepdims=True) * (1.0 + sc2)
        dyg2 = dh2 * w2
        dx1_ref[...] = dx2 + r2 * (dyg2 - x1h * jnp.mean(dyg2 * x1h, axis=-1, keepdims=True))

    return pl.pallas_call(
        body, name="ffn", grid=(nt,),
        in_specs=[_rows(tm, D), _rows(tm, D), _whole((6, D)), _whole((1, D)), _whole((1, D)),
                  _whole((DFF, D)), _whole((DFF, D)), _whole((DFF, D))],
        out_specs=[_rows(tm, D), _rows(tm, D), _cols(DFF, tm), _cols(DFF, tm), _cols(DFF, tm), _rows(tm, D),
                   _acc((8, D))],
        out_shape=(jax.ShapeDtypeStruct((t_len, D), F32), jax.ShapeDtypeStruct((t_len, D), BF16),
                   jax.ShapeDtypeStruct((DFF, t_len), BF16), jax.ShapeDtypeStruct((DFF, t_len), BF16),
                   jax.ShapeDtypeStruct((DFF, t_len), BF16), jax.ShapeDtypeStruct((t_len, D), BF16),
                   jax.ShapeDtypeStruct((8, D), F32)),
        compiler_params=_cparams(1),
    )(x1, tgt, mod6, norm2, normf, wg, wu, wd)


def _mid_bwd(dx1, y, of, ob, gfb, mod6, wout, *, tm):
    t_len = dx1.shape[0]
    nt = t_len // tm

    def body(dx1_ref, y_ref, of_ref, ob_ref, gfb_ref, mod_ref, wout_ref,
             dy_ref, dya_ref, dgfb_ref, dof_ref, dob_ref, dg1_ref):
        i = pl.program_id(0)

        @pl.when(i == 0)
        def _():
            dg1_ref[...] = jnp.zeros_like(dg1_ref)

        dx1 = dx1_ref[...]
        dg1_ref[0:1, :] += jnp.sum(dx1 * y_ref[...].astype(F32), axis=0, keepdims=True)
        dyb = (dx1 * mod_ref[2:3, :]).astype(BF16)
        dy_ref[...] = dyb
        dycat = _dot_nt(dyb, wout_ref[...])
        dya_ref[...] = dycat[:, :AW]
        dyr = dycat[:, AW:]
        gfb = gfb_ref[...].astype(F32)
        for o_ref, do_ref, lo in ((of_ref, dof_ref, 0), (ob_ref, dob_ref, RW)):
            sg, dsg = _silu_parts(gfb[:, lo:lo + RW])
            o = o_ref[...].astype(F32)
            hn, rs = _headnorm(o)
            dgfb_ref[:, lo:lo + RW] = (dyr * hn * dsg).astype(BF16)
            dhn = dyr * sg
            for h in range(H):
                hs = slice(h * DH, (h + 1) * DH)
                oh, dh = hn[:, hs], dhn[:, hs]
                do_ref[:, hs] = (rs[h] * (dh - oh * jnp.mean(dh * oh, axis=-1, keepdims=True))).astype(BF16)

    return pl.pallas_call(
        body, name="mid_bwd", grid=(nt,),
        in_specs=[_rows(tm, D), _rows(tm, D), _rows(tm, RW), _rows(tm, RW), _rows(tm, 2 * RW),
                  _whole((6, D)), _whole((D, D))],
        out_specs=[_rows(tm, D), _rows(tm, AW), _rows(tm, 2 * RW), _rows(tm, RW), _rows(tm, RW), _acc((8, D))],
        out_shape=(jax.ShapeDtypeStruct((t_len, D), BF16), jax.ShapeDtypeStruct((t_len, AW), F32),
                   jax.ShapeDtypeStruct((t_len, 2 * RW), BF16), jax.ShapeDtypeStruct((t_len, RW), BF16),
                   jax.ShapeDtypeStruct((t_len, RW), BF16), jax.ShapeDtypeStruct((8, D), F32)),
        compiler_params=_cparams(1),
    )(dx1, y, of, ob, gfb, mod6, wout)


def _ret_bwd_dir(q_ref, k_ref, v_ref, do_ref, st_ref, dq_ref, dk_ref, dv_ref, ds_scr, dlg_scr, lg, forward, nc, row0):
    order = reversed(range(nc)) if forward else range(nc)
    order = list(order)
    for h in range(H):
        dm, xi, zc, dec, ic = _decay_consts(lg[h:h + 1, 0:1], forward)
        hs = slice(h * DH, (h + 1) * DH)
        if forward:
            w_qi, w_qc, w_ki, w_ks = ic, ic + 1.0, -ic, (CR - 1.0) - ic
        else:
            w_qi, w_qc, w_ki, w_ks = -ic, CR - ic, ic, ic
        acc = jnp.zeros((1, DH), F32)
        for c in order:
            rs = slice(c * CR, (c + 1) * CR)
            qc, kc, vc, doc = q_ref[rs, hs], k_ref[rs, hs], v_ref[rs, hs], do_ref[rs, hs]
            s = st_ref[c, h]
            dsn = ds_scr[h]
            sb, dsnb = s.astype(BF16), dsn.astype(BF16)
            qf, kf = qc.astype(F32), kc.astype(F32)
            a = (_dot_nt(qc, kc) * dm).astype(BF16)
            da = (_dot_nt(doc, vc) * dm).astype(BF16)
            xdo = (xi * doc.astype(F32)).astype(BF16)
            kz = (kf * zc).astype(BF16)
            vz = (vc.astype(F32) * zc).astype(BF16)
            dq_i = _dot(da, kc)
            dq_c = _dot_nt(xdo, sb)
            dk_i = _dot_tn(da, qc)
            dk_s = _dot_nt(vz, dsnb)
            dq_ref[rs, hs] = dq_i + dq_c
            dk_ref[rs, hs] = dk_i + dk_s
            dv_ref[rs, hs] = _dot_tn(a, doc) + _dot(kz, dsnb)
            rowterm = qf * (w_qi * dq_i + w_qc * dq_c) + kf * (w_ki * dk_i + w_ks * dk_s)
            acc = acc + jnp.sum(rowterm, axis=0, keepdims=True)
            acc = acc + (float(CR) * dec) * jnp.sum(s * dsn, axis=0, keepdims=True)
            ds_scr[h] = _dot_tn((xi * qf).astype(BF16), doc) + dec * dsn
        dlg_scr[row0 + h:row0 + h + 1, :] += acc


def _ret_bwd(q, k, v, dof, dob, sst, rst, rlf, rlb, *, tm):
    t_len = q.shape[0]
    nt, nc = t_len // tm, tm // CR

    def body(qf_ref, kf_ref, vf_ref, dof_ref, sst_ref, qb_ref, kb_ref, vb_ref, dob_ref, rst_ref, rlf_ref, rlb_ref,
             dqf_ref, dkf_ref, dvf_ref, dqb_ref, dkb_ref, dvb_ref, dscf_ref, dscb_ref, dlg_ref,
             dsf_scr, dsb_scr, dlg_scr):
        i = pl.program_id(0)

        @pl.when(i == 0)
        def _():
            dsf_scr[...] = jnp.zeros_like(dsf_scr)
            dsb_scr[...] = jnp.zeros_like(dsb_scr)
            dlg_scr[...] = jnp.zeros_like(dlg_scr)

        lgf = _log_sigmoid(rlf_ref[...])
        lgb = _log_sigmoid(rlb_ref[...])
        _ret_bwd_dir(qf_ref, kf_ref, vf_ref, dof_ref, sst_ref, dqf_ref, dkf_ref, dvf_ref, dsf_scr, dlg_scr, lgf, True, nc, 0)
        _ret_bwd_dir(qb_ref, kb_ref, vb_ref, dob_ref, rst_ref, dqb_ref, dkb_ref, dvb_ref, dsb_scr, dlg_scr, lgb, False, nc, H)

        @pl.when(i == nt - 1)
        def _():
            dscf_ref[...] = dsf_scr[...]
            dscb_ref[...] = dsb_scr[...]
            tot = jnp.broadcast_to(jnp.sum(dlg_scr[...], axis=1, keepdims=True), (8, 128))
            ri = lax.broadcasted_iota(jnp.int32, (8, 128), 0)
            li = lax.broadcasted_iota(jnp.int32, (8, 128), 1)
            dlg_ref[...] = jnp.zeros_like(dlg_ref)
            dlg_ref[0:1, 0:128] = jnp.sum(jnp.where(ri == li, tot, 0.0), axis=0, keepdims=True)
            dlg_ref[1:2, 0:128] = jnp.sum(jnp.where(ri - H == li, tot, 0.0), axis=0, keepdims=True)

    rr = functools.partial(_rows_rev, nt=nt)
    st_f = pl.BlockSpec((nc, H, DH, DH), lambda i: (nt - 1 - i, 0, 0, 0))
    st_b = pl.BlockSpec((nc, H, DH, DH), lambda i: (i, 0, 0, 0))
    tok = jax.ShapeDtypeStruct((t_len, RW), F32)
    st = jax.ShapeDtypeStruct((H, DH, DH), F32)
    return pl.pallas_call(
        body, name="ret_bwd", grid=(nt,),
        in_specs=[rr(tm, RW), rr(tm, RW), rr(tm, RW), rr(tm, RW), st_f,
                  _rows(tm, RW), _rows(tm, RW), _rows(tm, RW), _rows(tm, RW), st_b,
                  _whole((H, 128)), _whole((H, 128))],
        out_specs=[rr(tm, RW), rr(tm, RW), rr(tm, RW), _rows(tm, RW), _rows(tm, RW), _rows(tm, RW),
                   _acc((H, DH, DH)), _acc((H, DH, DH)), _acc((8, D))],
        out_shape=(tok, tok, tok, tok, tok, tok, st, st, jax.ShapeDtypeStruct((8, D), F32)),
        scratch_shapes=[pltpu.VMEM((H, DH, DH), F32), pltpu.VMEM((H, DH, DH), F32), pltpu.VMEM((8, 128), F32)],
        compiler_params=_cparams(1),
    )(q, k, v, dof, sst, q, k, v, dob, rst, rlf, rlb)


def _in_bwd(x, zuv, dya, dqf, dkf, dvf, dqb, dkb, dvb, dgfb, dx1, cos, sin, mod6, norm1, win, sgw, sgbt, sggain, *, tm):
    t_len = x.shape[0]
    nt, nc = t_len // tm, tm // C

    def body(x_ref, zuv_ref, dya_ref, dqf_ref, dkf_ref, dvf_ref, dqb_ref, dkb_ref, dvb_ref, dgfb_ref, dx1_ref,
             cos_ref, sin_ref, mod_ref, n1_ref, win_ref, sgw_ref, sgbt_ref, sgg_ref,
             gx_ref, dzt_ref, small_ref, dsgw_ref, dsgbt_ref, dz_ref):
        i = pl.program_id(0)

        @pl.when(i == 0)
        def _():
            small_ref[...] = jnp.zeros_like(small_ref)
            dsgw_ref[...] = jnp.zeros_like(dsgw_ref)
            dsgbt_ref[...] = jnp.zeros_like(dsgbt_ref)

        zuv = zuv_ref[...]
        _, (ua, dgu, dgv, mixed, vn_b, vah_l, rv_l) = _sg_forward(zuv[:, :AW], zuv[:, AW:], sgw_ref, sgbt_ref, sgg_ref, nc)
        dya = dya_ref[...]
        dz_ref[:, 0:AW] = (dya * mixed * dgu).astype(BF16)
        dmixed = dya * ua
        gain = sgg_ref[...]
        for g in range(G):
            gs = slice(g * DH, (g + 1) * DH)
            dmg = dmixed[:, gs]
            dmb = dmg.astype(BF16)
            wgt = sgw_ref[g].astype(BF16)
            dsw = jnp.zeros((C, C), F32)
            dsb = jnp.zeros((C, DH), F32)
            rows = []
            for c in range(nc):
                rs = slice(c * C, (c + 1) * C)
                dsw = dsw + _dot_nt(dmb[rs, :], vn_b[g][rs, :])
                dsb = dsb + dmg[rs, :]
                rows.append(_dot_tn(wgt, dmb[rs, :]))
            dsgw_ref[g] += dsw
            dsgbt_ref[g] += dsb
            dvn = jnp.concatenate(rows, axis=0) if nc > 1 else rows[0]
            vah, rv = vah_l[g], rv_l[g]
            small_ref[3:4, gs] += jnp.sum(dvn * vah, axis=0, keepdims=True)
            dyg = dvn * gain[:, gs]
            dva = rv * (dyg - vah * jnp.mean(dyg * vah, axis=-1, keepdims=True))
            dz_ref[:, AW + g * DH:AW + (g + 1) * DH] = (dva * dgv[:, gs]).astype(BF16)

        cos = jnp.tile(cos_ref[...], (1, H))
        nsins = -jnp.tile(sin_ref[...], (1, H))
        dz_ref[:, 1024:1536] = _rope(dqf_ref[...] + dqb_ref[...], cos, nsins).astype(BF16)
        dz_ref[:, 1536:2048] = (_rope(dkf_ref[...] + dkb_ref[...], cos, nsins) * KSCALE).astype(BF16)
        dz_ref[:, 2048:2560] = (dvf_ref[...] + dvb_ref[...]).astype(BF16)
        dz_ref[:, 2560:3584] = dgfb_ref[...]

        dz = dz_ref[...]
        dzt_ref[...] = dz.T
        dhx = _dot(dz, win_ref[...])
        x = x_ref[...]
        r = lax.rsqrt(jnp.mean(x * x, axis=-1, keepdims=True) + EPS)
        xh = x * r
        n1, sc1 = n1_ref[...], mod_ref[1:2, :]
        small_ref[0:1, :] += jnp.sum(dhx, axis=0, keepdims=True)
        dhxx = jnp.sum(dhx * xh, axis=0, keepdims=True)
        small_ref[1:2, :] += dhxx * n1
        small_ref[2:3, :] += dhxx * (1.0 + sc1)
        dyg = dhx * (n1 * (1.0 + sc1))
        gx_ref[...] = dx1_ref[...] + r * (dyg - xh * jnp.mean(dyg * xh, axis=-1, keepdims=True))

        @pl.when(i == nt - 1)
        def _():
            ri = lax.broadcasted_iota(jnp.int32, (C, DH), 0)
            li = lax.broadcasted_iota(jnp.int32, (C, DH), 1)
            for g in range(G):
                tot = jnp.broadcast_to(jnp.sum(dsgbt_ref[g], axis=1, keepdims=True), (C, DH))
                small_ref[4 + g:5 + g, 0:DH] = jnp.sum(jnp.where(ri == li, tot, 0.0), axis=0, keepdims=True)

    tok = functools.partial(_rows, tm)
    return pl.pallas_call(
        body, name="in_bwd", grid=(nt,),
        in_specs=[tok(D), tok(2 * AW), tok(AW), tok(RW), tok(RW), tok(RW), tok(RW), tok(RW), tok(RW),
                  tok(2 * RW), tok(D), tok(DH), tok(DH), _whole((6, D)), _whole((1, D)), _whole((INC, D)),
                  _whole((G, C, C)), _whole((G, C, 128)), _whole((1, AW))],
        out_specs=[tok(D), _cols(INC, tm), _acc((8, D)), _acc((G, C, C))],
        out_shape=(jax.ShapeDtypeStruct((t_len, D), F32), jax.ShapeDtypeStruct((INC, t_len), BF16),
                   jax.ShapeDtypeStruct((8, D), F32), jax.ShapeDtypeStruct((G, C, C), F32)),
        scratch_shapes=[pltpu.VMEM((G, C, 128), F32), pltpu.VMEM((tm, INC), BF16)],
        compiler_params=_cparams(1),
    )(x, zuv, dya, dqf, dkf, dvf, dqb, dkb, dvb, dgfb, dx1, cos, sin, mod6, norm1, win, sgw, sgbt, sggain)


def _tn_matmul(at, b, *, bm, bt, name, init=None, init_rows=None, after=()):
    m, t_len = at.shape
    n = b.shape[1]
    ni, ntt = m // bm, t_len // bt
    has_init = init is not None

    def body(*refs):
        o_ref, ob_ref = refs[-2:]
        a_ref, b_ref = refs[:2]
        init_ref = refs[2] if has_init else None
        i, t = pl.program_id(0), pl.program_id(1)

        @pl.when(t == 0)
        def _():
            o_ref[...] = jnp.zeros_like(o_ref)

        if has_init:
            for ib in range(ni):
                lo, hi = max(init_rows[0], ib * bm), min(init_rows[1], (ib + 1) * bm)
                if lo < hi:
                    @pl.when(jnp.logical_and(t == 0, i == ib))
                    def _(lo=lo, hi=hi, ib=ib):
                        o_ref[lo - ib * bm:hi - ib * bm, :] = init_ref[lo - init_rows[0]:hi - init_rows[0], :]

        o_ref[...] += _dot(a_ref[...], b_ref[...])

        @pl.when(t == ntt - 1)
        def _():
            ob_ref[...] = o_ref[...].astype(BF16)

    in_specs = [pl.BlockSpec((bm, bt), lambda i, t: (i, t)), pl.BlockSpec((bt, n), lambda i, t: (t, 0))]
    args = [at, b]
    if has_init:
        in_specs.append(pl.BlockSpec(init.shape, lambda i, t: (0, 0), pipeline_mode=pl.Buffered(1)))
        args.append(init)
    for arr in after:
        in_specs.append(pl.BlockSpec(memory_space=pl.ANY))
        args.append(arr)
    out_spec = pl.BlockSpec((bm, n), lambda i, t: (i, 0))
    return pl.pallas_call(
        body, name=name, grid=(ni, ntt),
        in_specs=in_specs,
        out_specs=[out_spec, out_spec],
        out_shape=(jax.ShapeDtypeStruct((m, n), F32), jax.ShapeDtypeStruct((m, n), BF16)),
        compiler_params=_cparams(2),
    )(*args)


def _ctx_bwd(ctx, hc, kvc, cmod6, norm1, win, rlf, rlb, dscf, dscb, dlg_in):
    lc = ctx.shape[0]

    def body(ctx_ref, hc_ref, kvc_ref, cmod_ref, n1_ref, win_ref, rlf_ref, rlb_ref, dscf_ref, dscb_ref, dlgin_ref,
             dwin_ref, small_ref, dlg_ref):
        k = kvc_ref[:, :RW]
        v = kvc_ref[:, RW:]
        t = lax.broadcasted_iota(jnp.int32, (lc, 1), 0).astype(F32)
        lgf = _log_sigmoid(rlf_ref[...])
        lgb = _log_sigmoid(rlb_ref[...])
        dks, dvs = [], []
        lane = lax.broadcasted_iota(jnp.int32, (1, 128), 1)
        row_f = jnp.zeros((1, 128), F32)
        row_b = jnp.zeros((1, 128), F32)
        for h in range(H):
            hs = slice(h * DH, (h + 1) * DH)
            wf = jnp.exp(lgf[h:h + 1, 0:1] * (lc - 1.0 - t))
            wb = jnp.exp(lgb[h:h + 1, 0:1] * t)
            kh, vhb = k[:, hs], v[:, hs].astype(BF16)
            dsf, dsb = dscf_ref[h].astype(BF16), dscb_ref[h].astype(BF16)
            dkf = wf * _dot_nt(vhb, dsf)
            dkb = wb * _dot_nt(vhb, dsb)
            dvs.append(_dot((kh * wf).astype(BF16), dsf) + _dot((kh * wb).astype(BF16), dsb))
            dks.append((dkf + dkb) * KSCALE)
            lf = jnp.sum((lc - 1.0 - t) * kh * dkf, axis=0, keepdims=True)
            lb = jnp.sum(t * kh * dkb, axis=0, keepdims=True)
            row_f = row_f + jnp.where(lane == h, jnp.sum(lf, axis=1, keepdims=True), 0.0)
            row_b = row_b + jnp.where(lane == h, jnp.sum(lb, axis=1, keepdims=True), 0.0)
        dlg_ref[...] = dlgin_ref[...]
        dlg_ref[0:1, 0:128] += row_f
        dlg_ref[1:2, 0:128] += row_b
        dkv = jnp.concatenate(dks + dvs, axis=1).astype(BF16)
        dhc = _dot(dkv, win_ref[KV_LO:KV_HI, :])
        dwin_ref[...] = _dot_tn(dkv, hc_ref[...])
        x = ctx_ref[...]
        r = lax.rsqrt(jnp.mean(x * x, axis=-1, keepdims=True) + EPS)
        xh = x * r
        small_ref[...] = jnp.zeros_like(small_ref)
        small_ref[0:1, :] = jnp.sum(dhc, axis=0, keepdims=True)
        dhxx = jnp.sum(dhc * xh, axis=0, keepdims=True)
        small_ref[1:2, :] = dhxx * n1_ref[...]
        small_ref[2:3, :] = dhxx * (1.0 + cmod_ref[1:2, :])

    return pl.pallas_call(
        body, name="ctx_bwd",
        out_shape=(jax.ShapeDtypeStruct((2 * RW, D), F32), jax.ShapeDtypeStruct((8, D), F32),
                   jax.ShapeDtypeStruct((8, D), F32)),
        compiler_params=_cparams(),
    )(ctx, hc, kvc, cmod6, norm1, win, rlf, rlb, dscf, dscb, dlg_in)


def _adam_math(w, g, m, v):
    m = ADAM_B1 * m + (1.0 - ADAM_B1) * g
    v = ADAM_B2 * v + (1.0 - ADAM_B2) * (g * g)
    m_hat = m / (1.0 - ADAM_B1 ** ADAM_STEP)
    v_hat = v / (1.0 - ADAM_B2 ** ADAM_STEP)
    delta = -ADAM_LR * (m_hat / (jnp.sqrt(v_hat) + ADAM_EPS) + ADAM_WD * w)
    return delta, m, v


def _place():
    x, y, c = lax.axis_index("x"), lax.axis_index("y"), lax.axis_index("c")
    return x, y, c, 4 * x + 2 * y + c


def _flip(x, y, c, k):
    px = 1 - x if (k >> 2) & 1 else x
    py = 1 - y if (k >> 1) & 1 else y
    pc = 1 - c if k & 1 else c
    return (px, py, pc), 4 * px + 2 * py + pc


def _gather_direct(buf_ref, send_sems, recv_sems, sem0=0):
    x, y, c, me = _place()
    sends = []
    for k in range(1, NDEV):
        dev, _ = _flip(x, y, c, k)
        cp = pltpu.make_async_remote_copy(
            src_ref=buf_ref.at[me], dst_ref=buf_ref.at[me],
            send_sem=send_sems.at[sem0 + k - 1], recv_sem=recv_sems.at[sem0 + k - 1],
            device_id=dev, device_id_type=MESH)
        cp.start()
        sends.append(cp)
    for k in range(1, NDEV):
        dev, idx = _flip(x, y, c, k)
        pltpu.make_async_remote_copy(
            src_ref=buf_ref.at[idx], dst_ref=buf_ref.at[idx],
            send_sem=send_sems.at[sem0 + k - 1], recv_sem=recv_sems.at[sem0 + k - 1],
            device_id=dev, device_id_type=MESH).wait_recv()
    for cp in sends:
        cp.wait_send()


def _mod_gather(c_row, cctx_row, wmod, bmod):
    ncol = wmod.shape[1]

    def body(c_ref, cctx_ref, wmod_ref, bmod_ref, mod_ref, cs_ref, cbuf, pbuf, send_sems, recv_sems):
        _, _, _, me = _place()
        cbuf[me] = jnp.broadcast_to(c_ref[...], (8, D))
        _gather_direct(cbuf, send_sems, recv_sems, 0)
        row = lax.broadcasted_iota(jnp.int32, (16, D), 0)
        call = jnp.where(row == 8, jnp.broadcast_to(cctx_ref[...], (16, D)), 0.0)
        for d in range(NDEV):
            call = jnp.where(row == d, jnp.concatenate([cbuf[d], cbuf[d]], axis=0), call)
        cs = call * _sigmoid(call)
        cs_ref[...] = cs
        pbuf[me] = _dot(cs.astype(BF16), wmod_ref[...].astype(BF16))
        _gather_direct(pbuf, send_sems, recv_sems, NDEV - 1)
        for d in range(NDEV):
            mod_ref[:, d * ncol:(d + 1) * ncol] = pbuf[d] + bmod_ref[:, d * ncol:(d + 1) * ncol]

    return pl.pallas_call(
        body, name="mod_gather",
        out_shape=(jax.ShapeDtypeStruct((16, 6 * D), F32), jax.ShapeDtypeStruct((16, D), F32)),
        scratch_shapes=[pltpu.VMEM((NDEV, 8, D), F32), pltpu.VMEM((NDEV, 16, ncol), F32),
                        pltpu.SemaphoreType.DMA((2 * (NDEV - 1),)), pltpu.SemaphoreType.DMA((2 * (NDEV - 1),))],
        compiler_params=_cparams(),
    )(c_row, cctx_row, wmod, bmod)


def _weight_gather(shards, after=()):
    n = len(shards)
    na = len(after)

    def body(*refs):
        in_refs, out_refs = refs[:n], refs[n + na:2 * n + na]
        cast_refs = refs[2 * n + na:3 * n + na]
        send_sems, recv_sems, local_sems = refs[3 * n + na:]
        x, y, c, me = _place()
        sib = (x, y, 1 - c)
        chips = [(1 - x, y), (x, 1 - y), (1 - x, 1 - y)]

        def blk(px, py, pc):
            return 4 * px + 2 * py + pc

        def copy(w, k, block, to, src=None):
            dst = out_refs[w].at[block]
            return pltpu.make_async_remote_copy(
                src_ref=dst if src is None else src, dst_ref=dst,
                send_sem=send_sems.at[w, k], recv_sem=recv_sems.at[w, k], device_id=to, device_id_type=MESH)

        first, passed, mine = [], [], []
        for w in range(n):
            cast_refs[w][...] = in_refs[w][...].astype(BF16)
            m = pltpu.make_async_copy(cast_refs[w], out_refs[w].at[me], local_sems.at[w])
            m.start()
            mine.append(m)
            cps = [copy(w, 0, me, sib, src=cast_refs[w])]
            cps += [copy(w, 1 + j, me, (*chip, c), src=cast_refs[w]) for j, chip in enumerate(chips)]
            for cp in cps:
                cp.start()
            first += cps
        for w in range(n):
            for j, chip in enumerate(chips):
                b = blk(*chip, c)
                copy(w, 1 + j, b, (x, y, c)).wait_recv()
                fw = copy(w, 4 + j, b, sib)
                fw.start()
                passed.append(fw)
        for w in range(n):
            copy(w, 0, blk(x, y, 1 - c), (x, y, c)).wait_recv()
            for j, chip in enumerate(chips):
                copy(w, 4 + j, blk(*chip, 1 - c), (x, y, c)).wait_recv()
        for cp in first + passed:
            cp.wait_send()
        for m in mine:
            m.wait()

    vm = pl.BlockSpec(memory_space=pltpu.VMEM)
    hbm = pl.BlockSpec(memory_space=pltpu.HBM)
    return pl.pallas_call(
        body, name="weight_gather",
        in_specs=[vm] * (n + na), out_specs=[hbm] * n,
        out_shape=tuple(jax.ShapeDtypeStruct((NDEV,) + s.shape, BF16) for s in shards),
        scratch_shapes=[pltpu.VMEM(s.shape, BF16) for s in shards]
        + [pltpu.SemaphoreType.DMA((n, 7)), pltpu.SemaphoreType.DMA((n, 7)), pltpu.SemaphoreType.DMA((n,))],
        compiler_params=_cparams(),
    )(*shards, *after)


_HBM = pl.BlockSpec(memory_space=pltpu.HBM)
_SEMS = pl.BlockSpec(memory_space=pltpu.SEMAPHORE)
_EFFECT = pltpu.SideEffectType.DATAFLOW_SIDE_EFFECTING


def _exchange_copy(src_refs, land_refs, send_sems, recv_sems, w, k, scatter, landing_slot_is_mine):
    x, y, c, me = _place()
    dev, pidx = _flip(x, y, c, k)
    src = src_refs[w].at[pidx] if scatter else src_refs[w]
    slot = me if landing_slot_is_mine else pidx
    return pltpu.make_async_remote_copy(
        src_ref=src, dst_ref=land_refs[w].at[slot],
        send_sem=send_sems.at[w * (NDEV - 1) + k - 1], recv_sem=recv_sems.at[w * (NDEV - 1) + k - 1],
        device_id=dev, device_id_type=MESH)


def _exchange_start(srcs, *, scatter, name):
    n = len(srcs)
    lands = [lax.empty((NDEV,) + tuple(s.shape[-2:]), s.dtype) for s in srcs]

    def body(*refs):
        src_refs, land_refs = refs[:n], refs[n:2 * n]
        send_sems, recv_sems = refs[2 * n], refs[2 * n + 1]
        token = refs[-1]
        for w in range(n):
            for k in range(1, NDEV):
                _exchange_copy(src_refs, land_refs, send_sems, recv_sems, w, k, scatter, True).start()
        token[...] = jnp.zeros_like(token)

    arrs = list(srcs) + lands
    out_shape = ([pltpu.SemaphoreType.DMA((n * (NDEV - 1),)), pltpu.SemaphoreType.DMA((n * (NDEV - 1),))]
                 + [pltpu.HBM(a.shape, a.dtype) for a in arrs] + [jax.ShapeDtypeStruct((8, 128), F32)])
    res = pl.pallas_call(
        body, name=name, out_shape=tuple(out_shape),
        in_specs=[_HBM] * (2 * n),
        out_specs=tuple([_SEMS, _SEMS] + [_HBM] * (2 * n) + [pl.BlockSpec(memory_space=pltpu.VMEM)]),
        input_output_aliases={i: 2 + i for i in range(2 * n)},
        compiler_params=pltpu.CompilerParams(has_side_effects=_EFFECT),
    )(*[pltpu.with_memory_space_constraint(a, pltpu.HBM) for a in arrs])
    return res[:-1], res[-1]


def _exchange_wait(handles, after, *, scatter, name):
    n = (len(handles) - 2) // 2
    na = len(after)

    def body(*refs):
        src_refs, land_refs = refs[:n], refs[n:2 * n]
        send_sems, recv_sems = refs[2 * n], refs[2 * n + 1]
        for w in range(n):
            for k in range(1, NDEV):
                cp = _exchange_copy(src_refs, land_refs, send_sems, recv_sems, w, k, scatter, False)
                cp.wait_send()
                cp.wait_recv()

    arrs = handles[2:]
    res = pl.pallas_call(
        body, name=name, out_shape=tuple(pltpu.HBM(a.shape, a.dtype) for a in arrs),
        in_specs=[_HBM] * (2 * n) + [_SEMS, _SEMS] + [_HBM] * na,
        out_specs=tuple([_HBM] * (2 * n)),
        input_output_aliases={i: i for i in range(2 * n)},
        compiler_params=pltpu.CompilerParams(has_side_effects=_EFFECT),
    )(*arrs, handles[0], handles[1], *[pltpu.with_memory_space_constraint(a, pltpu.HBM) for a in after])
    return res[:n], res[n:]


def _cast_bf16(arrs, *, name, after=()):
    n = len(arrs)

    def body(*refs):
        for i_ref, o_ref in zip(refs[:n], refs[n + len(after):]):
            o_ref[...] = i_ref[...].astype(BF16)

    return pl.pallas_call(
        body, name=name, out_shape=tuple(jax.ShapeDtypeStruct(a.shape, BF16) for a in arrs),
        in_specs=[pl.BlockSpec(memory_space=pltpu.VMEM)] * n + [pl.BlockSpec(memory_space=pl.ANY)] * len(after),
        compiler_params=_cparams())(*arrs, *after)


def _rs_adam(me_arr, own, land, w, m, v, *, name):
    def body(me_ref, own_ref, land_ref, w_ref, m_ref, v_ref, g_ref, d_ref, mo_ref, vo_ref):
        me = me_ref[0]
        acc = jnp.zeros(own_ref.shape, F32)
        for p in range(NDEV):
            acc = acc + jnp.where(p == me, own_ref[...], land_ref[p].astype(F32))
        g_ref[...] = acc
        d_ref[...], mo_ref[...], vo_ref[...] = _adam_math(w_ref[...], acc, m_ref[...], v_ref[...])

    sh = jax.ShapeDtypeStruct(own.shape, F32)
    vm = pl.BlockSpec(memory_space=pltpu.VMEM)
    return pl.pallas_call(
        body, name=name, out_shape=(sh, sh, sh, sh),
        in_specs=[pl.BlockSpec(memory_space=pltpu.SMEM)] + [vm] * 5,
        compiler_params=_cparams(),
    )(me_arr, own, land, w, m, v)


ROW_F, ROW_I, ROW_C, ROW_G1, ROW_LG = 0, 8, 16, 24, 32
PACK_ROWS = 40


def _small_reduce(pack, dsgw, wmod, cctx_row, m_row, v_row):
    ncol = wmod.shape[1]

    def body(pack_ref, sgw_ref, wmod_ref, c_ref, m_ref, v_ref,
             sum_ref, all_ref, sgw_sum_ref, g_ref, d_ref, mo_ref, vo_ref, sgw_all, cbuf, send_sems, recv_sems):
        _, _, _, me = _place()
        all_ref[me] = pack_ref[...]
        sgw_all[me] = sgw_ref[...]
        _gather_direct(all_ref, send_sems, recv_sems, 0)
        _gather_direct(sgw_all, send_sems, recv_sems, NDEV - 1)
        acc = all_ref[0]
        acc_w = sgw_all[0]
        for d in range(1, NDEV):
            acc = acc + all_ref[d]
            acc_w = acc_w + sgw_all[d]
        sum_ref[...] = acc
        sgw_sum_ref[...] = acc_w

        zero = jnp.zeros((1, D), F32)
        dcmod = jnp.concatenate([acc[ROW_C:ROW_C + 1], acc[ROW_C + 1:ROW_C + 2], zero, zero, zero, zero], axis=1)
        mine = jnp.zeros((1, ncol), F32)
        for d in range(NDEV):
            mine = mine + jnp.where(d == me, dcmod[:, d * ncol:(d + 1) * ncol], 0.0)
        cbuf[me] = _dot_nt(jnp.broadcast_to(mine, (8, ncol)).astype(BF16), wmod_ref[...].astype(BF16))
        _gather_direct(cbuf, send_sems, recv_sems, 2 * (NDEV - 1))
        tot = cbuf[0]
        for d in range(1, NDEV):
            tot = tot + cbuf[d]
        cc = c_ref[...]
        _, dsilu = _silu_parts(cc)
        g = tot[0:1, :] * dsilu
        g_ref[...] = g
        d_ref[...], mo_ref[...], vo_ref[...] = _adam_math(cc, g, m_ref[...], v_ref[...])

    row = jax.ShapeDtypeStruct((1, D), F32)
    return pl.pallas_call(
        body, name="small_reduce",
        out_shape=(jax.ShapeDtypeStruct((PACK_ROWS, D), F32), jax.ShapeDtypeStruct((NDEV, PACK_ROWS, D), F32),
                   jax.ShapeDtypeStruct((G, C, C), F32), row, row, row, row),
        scratch_shapes=[pltpu.VMEM((NDEV, G, C, C), F32), pltpu.VMEM((NDEV, 8, D), F32),
                        pltpu.SemaphoreType.DMA((3 * (NDEV - 1),)), pltpu.SemaphoreType.DMA((3 * (NDEV - 1),))],
        compiler_params=_cparams(),
    )(pack, dsgw, wmod, cctx_row, m_row, v_row)


def _adam_small(s, sgw_g, params):
    names = ["norm1", "norm2", "norm_f", "sg_gain", "sg_b", "ret_logit_f", "ret_logit_b", "b_mod", "sg_w"]
    flat = [a for n in names for a in params[n]]

    def body(*refs):
        s_ref, sgw_ref = refs[0], refs[1]
        p_refs = refs[2:2 + 3 * len(names)]
        o_refs = refs[2 + 3 * len(names):]
        loss_ref = o_refs[-1]

        def row(r):
            return s_ref[r:r + 1, :]

        grads = {
            "norm1": row(ROW_I + 2) + row(ROW_C + 2),
            "norm2": row(ROW_F + 4),
            "norm_f": row(ROW_F + 0),
            "sg_gain": s_ref[ROW_I + 3:ROW_I + 4, 0:AW],
            "sg_b": s_ref[ROW_I + 4:ROW_I + 8, 0:DH],
            "sg_w": sgw_ref[...],
        }
        for j, n in enumerate(names):
            w_ref, m_ref, v_ref = p_refs[3 * j:3 * j + 3]
            g_ref, d_ref, mo_ref, vo_ref = o_refs[4 * j:4 * j + 4]
            w = w_ref[...]
            if n in ("ret_logit_f", "ret_logit_b"):
                r = ROW_LG + (0 if n == "ret_logit_f" else 1)
                g = s_ref[r:r + 1, 0:H] * _sigmoid(-w)
            elif n == "b_mod":
                rows = [row(ROW_I + 0) + row(ROW_C + 0), row(ROW_I + 1) + row(ROW_C + 1), row(ROW_G1),
                        row(ROW_F + 2), row(ROW_F + 3), row(ROW_F + 1)]
                g = jnp.concatenate(rows, axis=1)
            else:
                g = grads[n]
            g_ref[...] = g
            d_ref[...], mo_ref[...], vo_ref[...] = _adam_math(w, g, m_ref[...], v_ref[...])
        loss_ref[...] = jnp.full((1, 1), 0.5 / D, F32) * jnp.sum(row(ROW_F + 5), axis=1, keepdims=True)

    out_shape = []
    for n in names:
        w = params[n][0]
        out_shape += [jax.ShapeDtypeStruct(w.shape, F32)] * 4
    out_shape.append(jax.ShapeDtypeStruct((1, 1), F32))
    outs = pl.pallas_call(body, name="adam_small", out_shape=tuple(out_shape), compiler_params=_cparams())(
        s, sgw_g, *flat)
    return {n: tuple(outs[4 * j:4 * j + 4]) for j, n in enumerate(names)}, outs[-1]


def _wmod_grad(cs_all, dmod_cols, wmod, m, v):
    ncol = wmod.shape[1]

    def body(cs_ref, dm_ref, w_ref, m_ref, v_ref, g_ref, d_ref, mo_ref, vo_ref):
        g = _dot_tn(cs_ref[...].astype(BF16), dm_ref[...].astype(BF16))
        g_ref[...] = g
        d_ref[...], mo_ref[...], vo_ref[...] = _adam_math(w_ref[...], g, m_ref[...], v_ref[...])

    sh = jax.ShapeDtypeStruct((D, ncol), F32)
    return pl.pallas_call(
        body, name="wmod_grad", out_shape=(sh, sh, sh, sh),
        compiler_params=_cparams(),
    )(cs_all, dmod_cols, wmod, m, v)


def _rope_tables(t_len):
    n_freq = DH // 4
    inv = (ROPE_BASE ** (-np.arange(n_freq, dtype=np.float32) / n_freq)).astype(np.float32)
    tok = np.arange(t_len)
    rows = (tok // GRID_W).astype(np.float32)
    cols = (tok % GRID_W).astype(np.float32)
    ang_r = rows[:, None] * inv[None, :]
    ang_c = cols[:, None] * inv[None, :]
    cos = np.concatenate([np.cos(ang_r)] * 2 + [np.cos(ang_c)] * 2, axis=1).astype(np.float32)
    sin = np.concatenate([-np.sin(ang_r), np.sin(ang_r), -np.sin(ang_c), np.sin(ang_c)], axis=1).astype(np.float32)
    return jnp.asarray(cos), jnp.asarray(sin)


def _local_step(x, tgt, ctx, mod6, cmod6, norm1, norm2, normf, win, wout, ffn_weights, sgw, sgb, sggain, rlf, rlb,
                *, tm_a, tm_b, tm_f, tm_m, tm_r, tm_i, bt_w, after_first_grads=None, small_path=None):
    t_len = x.shape[0]
    cos, sin = _rope_tables(t_len)
    sgbt = jnp.broadcast_to(sgb[:, :, None], (G, C, 128))
    rlf = jnp.broadcast_to(rlf.reshape(H, 1), (H, 128))
    rlb = jnp.broadcast_to(rlb.reshape(H, 1), (H, 128))
    hc, kvc, scf, scb = _ctx_fwd(ctx, cmod6, norm1, win, rlf, rlb)
    hx, zuv, q, k, v, gfb, of, ya, sst = _fwd_a(x, mod6, norm1, win, sgw, sgbt, sggain, cos, sin, rlf, scf, tm=tm_a)
    ob, ycat, y, x1, rst = _fwd_b(q, k, v, of, gfb, ya, x, mod6, wout, rlb, scb, tm=tm_b)
    wg, wu, wd = ffn_weights(x1) if callable(ffn_weights) else ffn_weights
    dx1, h2, da, db, hm, df, small_f = _ffn(x1, tgt, mod6, norm2, normf, wg, wu, wd, tm=tm_f)
    d_wd, d_wd_b = _tn_matmul(hm, df, bm=DFF // 2, bt=bt_w, name="dw_down")
    d_wg, d_wg_b = _tn_matmul(da, h2, bm=DFF // 2, bt=bt_w, name="dw_gate")
    d_wu, d_wu_b = _tn_matmul(db, h2, bm=DFF // 2, bt=bt_w, name="dw_up")
    dy, dya, dgfb, dof, dob, dg1 = _mid_bwd(dx1, y, of, ob, gfb, mod6, wout, tm=tm_m)
    d_wo, d_wo_b = _tn_matmul(ycat, dy, bm=D, bt=bt_w, name="dw_out")
    if after_first_grads is not None:
        rlf = rlf + after_first_grads((d_wd_b, d_wg_b, d_wu_b, d_wo_b))
    dqf, dkf, dvf, dqb, dkb, dvb, dscf, dscb, dlg = _ret_bwd(q, k, v, dof, dob, sst, rst, rlf, rlb, tm=tm_r)
    gx, dz, small_i, dsgw = _in_bwd(x, zuv, dya, dqf, dkf, dvf, dqb, dkb, dvb, dgfb, dx1, cos, sin,
                                    mod6, norm1, win, sgw, sgbt, sggain, tm=tm_i)
    dwin_c, small_c, dlg = _ctx_bwd(ctx, hc, kvc, cmod6, norm1, win, rlf, rlb, dscf, dscb, dlg)
    pack = jnp.concatenate([small_f, small_i, small_c, dg1, dlg], axis=0)
    after = small_path(pack, dsgw) if small_path is not None else ()
    d_wi, d_wi_b = _tn_matmul(dz, hx, bm=INC // 2, bt=bt_w, name="dw_in", init=dwin_c,
                              init_rows=(KV_LO, KV_HI), after=after)
    grads = {"w_in": (d_wi, d_wi_b), "w_out": (d_wo, d_wo_b), "w_gate": (d_wg, d_wg_b), "w_up": (d_wu, d_wu_b),
             "w_down": (d_wd, d_wd_b)}
    return gx, grads, pack, dsgw


def kernel(x, c, ctx, c_ctx, w_mod, b_mod, norm1, w_in, sg_gain, sg_w, sg_b, ret_logit_f, ret_logit_b, w_out, norm2, w_gate, w_up, w_down, norm_f, loss_target, m_c_ctx, m_w_mod, m_b_mod, m_norm1, m_w_in, m_sg_gain, m_sg_w, m_sg_b, m_ret_logit_f, m_ret_logit_b, m_w_out, m_norm2, m_w_gate, m_w_up, m_w_down, m_norm_f, v_c_ctx, v_w_mod, v_b_mod, v_norm1, v_w_in, v_sg_gain, v_sg_w, v_sg_b, v_ret_logit_f, v_ret_logit_b, v_w_out, v_norm2, v_w_gate, v_w_up, v_w_down, v_norm_f):
    t_len = x.shape[1]
    tm = min(512, t_len)
    me = 4 * lax.axis_index("x") + 2 * lax.axis_index("y") + lax.axis_index("c")
    tr = lambda a: jnp.transpose(a[0])

    cctx_row = c_ctx.reshape(1, D)
    mod_all, cs_all = _mod_gather(c, cctx_row, w_mod[0], b_mod)
    mod6 = lax.dynamic_slice(mod_all, (me, 0), (1, 6 * D)).reshape(6, D)
    cmod6 = mod_all[8].reshape(6, D)

    g_in, g_out = _weight_gather([tr(w_in), w_out[0]], after=[cs_all])
    win_t = g_in.reshape(INC, D)
    wout = g_out.reshape(D, D)

    ffn_shards = _cast_bf16([tr(w_gate), tr(w_up), w_down[0]], name="cast_ffn", after=[g_in])
    h_ffn, tok = _exchange_start(ffn_shards, scatter=False, name="wffn_start")
    mod6 = mod6 + tok[0, 0]

    def ffn_weights(after):
        own, lands = _exchange_wait(h_ffn, [after], scatter=False, name="wffn_wait")
        return [lax.dynamic_update_slice(l, o[None], (me, 0, 0)).reshape(DFF, D) for o, l in zip(own, lands)]

    rs, outs = {}, {}

    def after_first_grads(bf):
        rs["first"], tok_first = _exchange_start([b.reshape(NDEV, b.shape[0] // NDEV, D) for b in bf], scatter=True,
                                                 name="rs_first_start")
        return tok_first[0, 0]

    def small_path(pack, dsgw):
        psum_rows, pall, sgw_sum, g_cc, d_cc, m_cc, v_cc = _small_reduce(
            pack, dsgw, w_mod[0], cctx_row, m_c_ctx.reshape(1, D), v_c_ctx.reshape(1, D))
        outs["c_ctx"] = tuple(a.reshape(D) for a in (g_cc, d_cc, m_cc, v_cc))
        dmod_rows = (ROW_I + 0, ROW_I + 1, ROW_G1, ROW_F + 2, ROW_F + 3, ROW_F + 1)
        dmod_all = jnp.concatenate([pall[:, r, :] for r in dmod_rows], axis=1)
        dcmod = jnp.concatenate([psum_rows[ROW_C + 0:ROW_C + 1], psum_rows[ROW_C + 1:ROW_C + 2],
                                 jnp.zeros((1, 4 * D), F32)], axis=1)
        dmod_mat = jnp.concatenate([dmod_all, jnp.pad(dcmod, ((0, 7), (0, 0)))], axis=0)
        ncol = 6 * D // NDEV
        dmod_cols = lax.dynamic_slice(dmod_mat, (0, me * ncol), (16, ncol))
        g_wm, d_wm, m_wm, v_wm = _wmod_grad(cs_all, dmod_cols, w_mod[0], m_w_mod[0], v_w_mod[0])
        outs["w_mod"] = (g_wm[None], d_wm[None], m_wm[None], v_wm[None])
        small_params = {
            "norm1": (norm1, m_norm1, v_norm1), "norm2": (norm2, m_norm2, v_norm2),
            "norm_f": tuple(a.reshape(1, D) for a in (norm_f, m_norm_f, v_norm_f)),
            "sg_gain": (sg_gain, m_sg_gain, v_sg_gain), "sg_b": (sg_b[0], m_sg_b[0], v_sg_b[0]),
            "ret_logit_f": (ret_logit_f, m_ret_logit_f, v_ret_logit_f),
            "ret_logit_b": (ret_logit_b, m_ret_logit_b, v_ret_logit_b),
            "b_mod": (b_mod, m_b_mod, v_b_mod), "sg_w": (sg_w[0], m_sg_w[0], v_sg_w[0])}
        small, loss = _adam_small(psum_rows, sgw_sum, small_params)
        back = {"norm_f": lambda a: a.reshape(D), "sg_b": lambda a: a[None], "sg_w": lambda a: a[None]}
        for name, vals in small.items():
            outs[name] = tuple(back.get(name, lambda a: a)(a) for a in vals)
        outs["loss"] = loss
        return [pall]

    gx, grads, pack, dsgw = _local_step(
        x[0], loss_target[0], ctx[0], mod6, cmod6, norm1, norm2[0:1], norm_f.reshape(1, D), win_t, wout, ffn_weights,
        sg_w[0], sg_b[0], sg_gain, ret_logit_f, ret_logit_b,
        tm_a=tm, tm_b=tm, tm_f=min(256, t_len), tm_m=tm, tm_r=tm, tm_i=min(256, t_len), bt_w=min(1024, t_len),
        after_first_grads=after_first_grads, small_path=small_path)

    h_in, tok_in = _exchange_start([grads["w_in"][1].reshape(NDEV, INC // NDEV, D)], scatter=True, name="rs_in_start")
    _, lands_first = _exchange_wait(rs["first"], [tok_in], scatter=True, name="rs_first_wait")
    me_arr = me.reshape(1).astype(jnp.int32)

    def finish(name, land, w, m, v, transposed):
        rows = land.shape[1]
        own = lax.dynamic_slice(grads[name][0], (me * rows, 0), (rows, D))
        take = tr if transposed else (lambda a: a[0])
        res = _rs_adam(me_arr, own, land, take(w), take(m), take(v), name="adam_" + name)
        put = (lambda a: jnp.transpose(a)[None]) if transposed else (lambda a: a[None])
        outs[name] = tuple(put(a) for a in res)
        return res[0]

    g_down = finish("w_down", lands_first[0], w_down, m_w_down, v_w_down, False)
    g_gate = finish("w_gate", lands_first[1], w_gate, m_w_gate, v_w_gate, True)
    g_up = finish("w_up", lands_first[2], w_up, m_w_up, v_w_up, True)
    g_out = finish("w_out", lands_first[3], w_out, m_w_out, v_w_out, False)
    _, lands_in = _exchange_wait(h_in, [g_down, g_gate, g_up, g_out], scatter=True, name="rs_in_wait")
    finish("w_in", lands_in[0], w_in, m_w_in, v_w_in, True)
    loss = outs.pop("loss")

    order = ["c_ctx", "w_mod", "b_mod", "norm1", "w_in", "sg_gain", "sg_w", "sg_b", "ret_logit_f", "ret_logit_b",
             "w_out", "norm2", "w_gate", "w_up", "w_down", "norm_f"]
    res = [loss.reshape(()), gx[None]]
    for j in range(4):
        res += [outs[name][j] for name in order]
    return tuple(res)
```

```python
import functools
import math

import numpy as np
import jax
import jax.numpy as jnp
from jax import lax
from jax.experimental import pallas as pl
from jax.experimental.pallas import tpu as pltpu

F32 = jnp.float32
BF16 = jnp.bfloat16

D = 1024
AW = 512
RW = 512
H = 4
DH = 128
G = 4
C = 128
CR = 256
DFF = 2816
INC = 3584
KV_LO, KV_HI = 1536, 2560
NDEV = 8
EPS = 1e-6
KSCALE = DH ** -0.5
ROPE_BASE = 10000.0
GRID_W = 64

ADAM_LR = 0.001
ADAM_B1 = 0.9
ADAM_B2 = 0.999
ADAM_EPS = 1e-08
ADAM_WD = 0.01
ADAM_STEP = 10

VMEM_LIMIT = 56 * 1024 * 1024
MESH = pl.DeviceIdType.MESH


def _cparams(n_grid=0, **kw):
    sem = ("arbitrary",) * n_grid if n_grid else None
    return pltpu.CompilerParams(dimension_semantics=sem, vmem_limit_bytes=VMEM_LIMIT, **kw)


def _dot(a, b):
    return jnp.dot(a, b, preferred_element_type=F32)


def _dot_nt(a, b):
    return lax.dot_general(a, b, (((1,), (1,)), ((), ())), preferred_element_type=F32)


def _dot_tn(a, b):
    return lax.dot_general(a, b, (((0,), (0,)), ((), ())), preferred_element_type=F32)


def _whole(shape):
    nd = len(shape)
    return pl.BlockSpec(shape, lambda *_: (0,) * nd, pipeline_mode=pl.Buffered(1))


def _acc(shape):
    nd = len(shape)
    return pl.BlockSpec(shape, lambda *_: (0,) * nd)


def _rows(tm, n):
    return pl.BlockSpec((tm, n), lambda i: (i, 0))


def _rows_rev(tm, n, nt):
    return pl.BlockSpec((tm, n), lambda i: (nt - 1 - i, 0))


def _cols(n, tm):
    return pl.BlockSpec((n, tm), lambda i: (0, i))


def _sigmoid(x):
    return 1.0 / (1.0 + jnp.exp(-x))


def _log_sigmoid(x):
    return jnp.minimum(x, 0.0) - jnp.log(1.0 + jnp.exp(-jnp.abs(x)))


_GK0 = math.sqrt(2.0 / math.pi)
_GK1 = 0.044715


def _gelu(x):
    x2 = x * x
    t = jnp.tanh(_GK0 * x * (1.0 + _GK1 * x2))
    g = 0.5 * x * (1.0 + t)
    dg = 0.5 * (1.0 + t) + 0.5 * x * (1.0 - t * t) * (_GK0 * (1.0 + 3.0 * _GK1 * x2))
    return g, dg


def _silu_parts(a):
    s = _sigmoid(a)
    return a * s, s * (1.0 + a * (1.0 - s))


def _rope(t, cos, sins):
    n = t.shape[1]
    lane = lax.broadcasted_iota(jnp.int32, t.shape, 1)
    first = (lane & 63) < 32
    partner = jnp.where(first, pltpu.roll(t, n - 32, 1), pltpu.roll(t, 32, 1))
    return t * cos + partner * sins


def _headnorm(o):
    outs, rs = [], []
    for h in range(H):
        oh = o[:, h * DH:(h + 1) * DH]
        r = lax.rsqrt(jnp.mean(oh * oh, axis=-1, keepdims=True) + EPS)
        outs.append(oh * r)
        rs.append(r)
    return jnp.concatenate(outs, axis=1), rs


def _decay_consts(lg, forward):
    ii = lax.broadcasted_iota(jnp.int32, (CR, CR), 0).astype(F32)
    jj = lax.broadcasted_iota(jnp.int32, (CR, CR), 1).astype(F32)
    ic = lax.broadcasted_iota(jnp.int32, (CR, 1), 0).astype(F32)
    if forward:
        diff = ii - jj
        xi = jnp.exp(lg * (ic + 1.0))
        z = jnp.exp(lg * (CR - 1.0 - ic))
    else:
        diff = jj - ii
        xi = jnp.exp(lg * (CR - ic))
        z = jnp.exp(lg * ic)
    dm = jnp.where(diff >= 0.0, jnp.exp(lg * jnp.maximum(diff, 0.0)), 0.0)
    dec = jnp.exp(lg * float(CR))
    return dm, xi, z, dec, ic


def _ctx_fwd(ctx, cmod6, norm1, win, rlf, rlb):
    lc = ctx.shape[0]

    def body(ctx_ref, cmod_ref, n1_ref, win_ref, rlf_ref, rlb_ref, hc_ref, kvc_ref, scf_ref, scb_ref):
        x = ctx_ref[...]
        r = lax.rsqrt(jnp.mean(x * x, axis=-1, keepdims=True) + EPS)
        hc = (x * r) * (n1_ref[...] * (1.0 + cmod_ref[1:2, :])) + cmod_ref[0:1, :]
        hcb = hc.astype(BF16)
        hc_ref[...] = hcb
        kv = _dot_nt(hcb, win_ref[KV_LO:KV_HI, :])
        k = kv[:, :RW] * KSCALE
        v = kv[:, RW:]
        kvc_ref[:, :RW] = k
        kvc_ref[:, RW:] = v
        t = lax.broadcasted_iota(jnp.int32, (lc, 1), 0).astype(F32)
        lgf = _log_sigmoid(rlf_ref[...])
        lgb = _log_sigmoid(rlb_ref[...])
        for h in range(H):
            hs = slice(h * DH, (h + 1) * DH)
            wf = jnp.exp(lgf[h:h + 1, 0:1] * (lc - 1.0 - t))
            wb = jnp.exp(lgb[h:h + 1, 0:1] * t)
            vh = v[:, hs].astype(BF16)
            scf_ref[h] = _dot_tn((k[:, hs] * wf).astype(BF16), vh)
            scb_ref[h] = _dot_tn((k[:, hs] * wb).astype(BF16), vh)

    return pl.pallas_call(
        body, name="ctx_fwd",
        out_shape=(jax.ShapeDtypeStruct((lc, D), BF16), jax.ShapeDtypeStruct((lc, 2 * RW), F32),
                   jax.ShapeDtypeStruct((H, DH, DH), F32), jax.ShapeDtypeStruct((H, DH, DH), F32)),
        compiler_params=_cparams(),
    )(ctx, cmod6, norm1, win, rlf, rlb)


def _sg_forward(u, v, sgw_ref, sgbt_ref, sgg_ref, nc):
    ua, dgu = _gelu(u)
    va, dgv = _gelu(v)
    gain = sgg_ref[...]
    mixed, vn_b, vah_l, rv_l = [], [], [], []
    for g in range(G):
        gs = slice(g * DH, (g + 1) * DH)
        vag = va[:, gs]
        rv = lax.rsqrt(jnp.mean(vag * vag, axis=-1, keepdims=True) + EPS)
        vah = vag * rv
        vn = (vah * gain[:, gs]).astype(BF16)
        wg = sgw_ref[g].astype(BF16)
        bcol = sgbt_ref[g]
        rows = [_dot(wg, vn[c * C:(c + 1) * C, :]) + bcol for c in range(nc)]
        mixed.append(jnp.concatenate(rows, axis=0) if nc > 1 else rows[0])
        vn_b.append(vn)
        vah_l.append(vah)
        rv_l.append(rv)
    mixed = jnp.concatenate(mixed, axis=1)
    return ua * mixed, (ua, dgu, dgv, mixed, vn_b, vah_l, rv_l)


def _fwd_a(x, mod6, norm1, win, sgw, sgbt, sggain, cos, sin, rlf, scf, *, tm):
    t_len = x.shape[0]
    nt, nc, ncr = t_len // tm, tm // C, tm // CR

    def body(x_ref, mod_ref, n1_ref, win_ref, sgw_ref, sgbt_ref, sgg_ref, cos_ref, sin_ref, rlf_ref, scf_ref,
             hx_ref, zuv_ref, q_ref, k_ref, v_ref, gfb_ref, of_ref, ya_ref, sst_ref, s_scr):
        i = pl.program_id(0)

        @pl.when(i == 0)
        def _():
            s_scr[...] = scf_ref[...]

        x = x_ref[...]
        r = lax.rsqrt(jnp.mean(x * x, axis=-1, keepdims=True) + EPS)
        hx = (x * r) * (n1_ref[...] * (1.0 + mod_ref[1:2, :])) + mod_ref[0:1, :]
        hxb = hx.astype(BF16)
        hx_ref[...] = hxb
        z = _dot_nt(hxb, win_ref[...])
        zuv_ref[...] = z[:, :2 * AW]
        gfb_ref[...] = z[:, 2560:3584].astype(BF16)
        cos = jnp.tile(cos_ref[...], (1, H))
        sins = jnp.tile(sin_ref[...], (1, H))
        q_ref[...] = _rope(z[:, 1024:1536], cos, sins).astype(BF16)
        k_ref[...] = (_rope(z[:, 1536:2048], cos, sins) * KSCALE).astype(BF16)
        v_ref[...] = z[:, 2048:2560].astype(BF16)
        ya, _ = _sg_forward(z[:, :AW], z[:, AW:2 * AW], sgw_ref, sgbt_ref, sgg_ref, nc)
        ya_ref[...] = ya.astype(BF16)

        lg = _log_sigmoid(rlf_ref[...])
        for h in range(H):
            dm, xi, zc, dec, _ = _decay_consts(lg[h:h + 1, 0:1], True)
            hs = slice(h * DH, (h + 1) * DH)
            for c in range(ncr):
                rs = slice(c * CR, (c + 1) * CR)
                qc, kc, vc = q_ref[rs, hs], k_ref[rs, hs], v_ref[rs, hs]
                s = s_scr[h]
                sst_ref[c, h] = s
                a = (_dot_nt(qc, kc) * dm).astype(BF16)
                of_ref[rs, hs] = (_dot(a, vc) + xi * _dot(qc, s.astype(BF16))).astype(BF16)
                kz = (kc.astype(F32) * zc).astype(BF16)
                s_scr[h] = dec * s + _dot_tn(kz, vc)

    n_chunks = t_len // CR
    return pl.pallas_call(
        body, name="fwd_a", grid=(nt,),
        in_specs=[_rows(tm, D), _whole((6, D)), _whole((1, D)), _whole((INC, D)), _whole((G, C, C)),
                  _whole((G, C, 128)), _whole((1, AW)), _rows(tm, DH), _rows(tm, DH), _whole((H, 128)),
                  _whole((H, DH, DH))],
        out_specs=[_rows(tm, D), _rows(tm, 2 * AW), _rows(tm, RW), _rows(tm, RW), _rows(tm, RW),
                   _rows(tm, 2 * RW), _rows(tm, RW), _rows(tm, AW),
                   pl.BlockSpec((ncr, H, DH, DH), lambda i: (i, 0, 0, 0))],
        out_shape=(jax.ShapeDtypeStruct((t_len, D), BF16), jax.ShapeDtypeStruct((t_len, 2 * AW), F32),
                   jax.ShapeDtypeStruct((t_len, RW), BF16), jax.ShapeDtypeStruct((t_len, RW), BF16),
                   jax.ShapeDtypeStruct((t_len, RW), BF16), jax.ShapeDtypeStruct((t_len, 2 * RW), BF16),
                   jax.ShapeDtypeStruct((t_len, RW), BF16), jax.ShapeDtypeStruct((t_len, AW), BF16),
                   jax.ShapeDtypeStruct((n_chunks, H, DH, DH), F32)),
        scratch_shapes=[pltpu.VMEM((H, DH, DH), F32)],
        compiler_params=_cparams(1),
    )(x, mod6, norm1, win, sgw, sgbt, sggain, cos, sin, rlf, scf)


def _fwd_b(q, k, v, of, gfb, ya, x, mod6, wout, rlb, scb, *, tm):
    t_len = x.shape[0]
    nt, nc = t_len // tm, tm // CR

    def body(q_ref, k_ref, v_ref, of_ref, gfb_ref, ya_ref, x_ref, mod_ref, wout_ref, rlb_ref, scb_ref,
             ob_ref, ycat_ref, y_ref, x1_ref, rst_ref, r_scr):
        i = pl.program_id(0)

        @pl.when(i == 0)
        def _():
            r_scr[...] = scb_ref[...]

        lg = _log_sigmoid(rlb_ref[...])
        for h in range(H):
            dm, xi, zc, dec, _ = _decay_consts(lg[h:h + 1, 0:1], False)
            hs = slice(h * DH, (h + 1) * DH)
            for c in reversed(range(nc)):
                rs = slice(c * CR, (c + 1) * CR)
                qc, kc, vc = q_ref[rs, hs], k_ref[rs, hs], v_ref[rs, hs]
                s = r_scr[h]
                rst_ref[c, h] = s
                a = (_dot_nt(qc, kc) * dm).astype(BF16)
                ob_ref[rs, hs] = (_dot(a, vc) + xi * _dot(qc, s.astype(BF16))).astype(BF16)
                kz = (kc.astype(F32) * zc).astype(BF16)
                r_scr[h] = dec * s + _dot_tn(kz, vc)

        gfb = gfb_ref[...].astype(F32)
        sf, _ = _silu_parts(gfb[:, :RW])
        sb, _ = _silu_parts(gfb[:, RW:])
        hnf, _ = _headnorm(of_ref[...].astype(F32))
        hnb, _ = _headnorm(ob_ref[...].astype(F32))
        yr = sf * hnf + sb * hnb
        ycat = jnp.concatenate([ya_ref[...], yr.astype(BF16)], axis=1)
        ycat_ref[...] = ycat.T
        y = _dot(ycat, wout_ref[...])
        y_ref[...] = y.astype(BF16)
        x1_ref[...] = x_ref[...] + mod_ref[2:3, :] * y

    n_chunks = t_len // CR
    rr = functools.partial(_rows_rev, nt=nt)
    return pl.pallas_call(
        body, name="fwd_b", grid=(nt,),
        in_specs=[rr(tm, RW), rr(tm, RW), rr(tm, RW), rr(tm, RW), rr(tm, 2 * RW), rr(tm, AW), rr(tm, D),
                  _whole((6, D)), _whole((D, D)), _whole((H, 128)), _whole((H, DH, DH))],
        out_specs=[rr(tm, RW), pl.BlockSpec((D, tm), lambda i: (0, nt - 1 - i)), rr(tm, D), rr(tm, D),
                   pl.BlockSpec((nc, H, DH, DH), lambda i: (nt - 1 - i, 0, 0, 0))],
        out_shape=(jax.ShapeDtypeStruct((t_len, RW), BF16), jax.ShapeDtypeStruct((D, t_len), BF16),
                   jax.ShapeDtypeStruct((t_len, D), BF16), jax.ShapeDtypeStruct((t_len, D), F32),
                   jax.ShapeDtypeStruct((n_chunks, H, DH, DH), F32)),
        scratch_shapes=[pltpu.VMEM((H, DH, DH), F32)],
        compiler_params=_cparams(1),
    )(q, k, v, of, gfb, ya, x, mod6, wout, rlb, scb)


def _ffn(x1, tgt, mod6, norm2, normf, wg, wu, wd, *, tm):
    t_len = x1.shape[0]
    nt = t_len // tm

    def body(x1_ref, tgt_ref, mod_ref, n2_ref, nf_ref, wg_ref, wu_ref, wd_ref,
             dx1_ref, h2_ref, da_ref, db_ref, hm_ref, df_ref, small_ref):
        i = pl.program_id(0)

        @pl.when(i == 0)
        def _():
            small_ref[...] = jnp.zeros_like(small_ref)

        sh2, sc2, g2 = mod_ref[3:4, :], mod_ref[4:5, :], mod_ref[5:6, :]
        n2, nf = n2_ref[...], nf_ref[...]
        x1 = x1_ref[...]
        r2 = lax.rsqrt(jnp.mean(x1 * x1, axis=-1, keepdims=True) + EPS)
        x1h = x1 * r2
        w2 = n2 * (1.0 + sc2)
        h2b = (x1h * w2 + sh2).astype(BF16)
        h2_ref[...] = h2b
        a = _dot_nt(h2b, wg_ref[...])
        b = _dot_nt(h2b, wu_ref[...])
        sa, dsa = _silu_parts(a)
        hmb = (sa * b).astype(BF16)
        hm_ref[...] = hmb.T
        f = _dot(hmb, wd_ref[...])
        x2 = x1 + g2 * f
        r3 = lax.rsqrt(jnp.mean(x2 * x2, axis=-1, keepdims=True) + EPS)
        x2h = x2 * r3
        diff = x2h * nf - tgt_ref[...]
        small_ref[5:6, :] += jnp.sum(diff * diff, axis=0, keepdims=True)
        dout = diff * (1.0 / D)
        small_ref[0:1, :] += jnp.sum(dout * x2h, axis=0, keepdims=True)
        dyg = dout * nf
        dx2 = r3 * (dyg - x2h * jnp.mean(dyg * x2h, axis=-1, keepdims=True))
        small_ref[1:2, :] += jnp.sum(dx2 * f, axis=0, keepdims=True)
        dfb = (dx2 * g2).astype(BF16)
        df_ref[...] = dfb
        dhm = _dot_nt(dfb, wd_ref[...])
        dbb = (dhm * sa).astype(BF16)
        dab = (dhm * b * dsa).astype(BF16)
        da_ref[...] = dab.T
        db_ref[...] = dbb.T
        dh2 = _dot(dab, wg_ref[...]) + _dot(dbb, wu_ref[...])
        small_ref[2:3, :] += jnp.sum(dh2, axis=0, keepdims=True)
        dhx = dh2 * x1h
        small_ref[3:4, :] += jnp.sum(dhx, axis=0, keepdims=True) * n2
        small_ref[4:5, :] += jnp.sum(dhx, axis=0, keepdims=True) * (1.0 + sc2)
        dyg2 = dh2 * w2
        dx1_ref[...] = dx2 + r2 * (dyg2 - x1h * jnp.mean(dyg2 * x1h, axis=-1, keepdims=True))

    return pl.pallas_call(
        body, name="ffn", grid=(nt,),
        in_specs=[_rows(tm, D), _rows(tm, D), _whole((6, D)), _whole((1, D)), _whole((1, D)),
                  _whole((DFF, D)), _whole((DFF, D)), _whole((DFF, D))],
        out_specs=[_rows(tm, D), _rows(tm, D), _cols(DFF, tm), _cols(DFF, tm), _cols(DFF, tm), _rows(tm, D),
                   _acc((8, D))],
        out_shape=(jax.ShapeDtypeStruct((t_len, D), F32), jax.ShapeDtypeStruct((t_len, D), BF16),
                   jax.ShapeDtypeStruct((DFF, t_len), BF16), jax.ShapeDtypeStruct((DFF, t_len), BF16),
                   jax.ShapeDtypeStruct((DFF, t_len), BF16), jax.ShapeDtypeStruct((t_len, D), BF16),
                   jax.ShapeDtypeStruct((8, D), F32)),
        compiler_params=_cparams(1),
    )(x1, tgt, mod6, norm2, normf, wg, wu, wd)


def _mid_bwd(dx1, y, of, ob, gfb, mod6, wout, *, tm):
    t_len = dx1.shape[0]
    nt = t_len // tm

    def body(dx1_ref, y_ref, of_ref, ob_ref, gfb_ref, mod_ref, wout_ref,
             dy_ref, dya_ref, dgfb_ref, dof_ref, dob_ref, dg1_ref):
        i = pl.program_id(0)

        @pl.when(i == 0)
        def _():
            dg1_ref[...] = jnp.zeros_like(dg1_ref)

        dx1 = dx1_ref[...]
        dg1_ref[0:1, :] += jnp.sum(dx1 * y_ref[...].astype(F32), axis=0, keepdims=True)
        dyb = (dx1 * mod_ref[2:3, :]).astype(BF16)
        dy_ref[...] = dyb
        dycat = _dot_nt(dyb, wout_ref[...])
        dya_ref[...] = dycat[:, :AW]
        dyr = dycat[:, AW:]
        gfb = gfb_ref[...].astype(F32)
        for o_ref, do_ref, lo in ((of_ref, dof_ref, 0), (ob_ref, dob_ref, RW)):
            sg, dsg = _silu_parts(gfb[:, lo:lo + RW])
            o = o_ref[...].astype(F32)
            hn, rs = _headnorm(o)
            dgfb_ref[:, lo:lo + RW] = (dyr * hn * dsg).astype(BF16)
            dhn = dyr * sg
            for h in range(H):
                hs = slice(h * DH, (h + 1) * DH)
                oh, dh = hn[:, hs], dhn[:, hs]
                do_ref[:, hs] = (rs[h] * (dh - oh * jnp.mean(dh * oh, axis=-1, keepdims=True))).astype(BF16)

    return pl.pallas_call(
        body, name="mid_bwd", grid=(nt,),
        in_specs=[_rows(tm, D), _rows(tm, D), _rows(tm, RW), _rows(tm, RW), _rows(tm, 2 * RW),
                  _whole((6, D)), _whole((D, D))],
        out_specs=[_rows(tm, D), _rows(tm, AW), _rows(tm, 2 * RW), _rows(tm, RW), _rows(tm, RW), _acc((8, D))],
        out_shape=(jax.ShapeDtypeStruct((t_len, D), BF16), jax.ShapeDtypeStruct((t_len, AW), F32),
                   jax.ShapeDtypeStruct((t_len, 2 * RW), BF16), jax.ShapeDtypeStruct((t_len, RW), BF16),
                   jax.ShapeDtypeStruct((t_len, RW), BF16), jax.ShapeDtypeStruct((8, D), F32)),
        compiler_params=_cparams(1),
    )(dx1, y, of, ob, gfb, mod6, wout)


def _ret_bwd_dir(q_ref, k_ref, v_ref, do_ref, st_ref, dq_ref, dk_ref, dv_ref, ds_scr, dlg_scr, lg, forward, nc, row0):
    order = reversed(range(nc)) if forward else range(nc)
    order = list(order)
    for h in range(H):
        dm, xi, zc, dec, ic = _decay_consts(lg[h:h + 1, 0:1], forward)
        hs = slice(h * DH, (h + 1) * DH)
        if forward:
            w_qi, w_qc, w_ki, w_ks = ic, ic + 1.0, -ic, (CR - 1.0) - ic
        else:
            w_qi, w_qc, w_ki, w_ks = -ic, CR - ic, ic, ic
        acc = jnp.zeros((1, DH), F32)
        for c in order:
            rs = slice(c * CR, (c + 1) * CR)
            qc, kc, vc, doc = q_ref[rs, hs], k_ref[rs, hs], v_ref[rs, hs], do_ref[rs, hs]
            s = st_ref[c, h]
            dsn = ds_scr[h]
            sb, dsnb = s.astype(BF16), dsn.astype(BF16)
            qf, kf = qc.astype(F32), kc.astype(F32)
            a = (_dot_nt(qc, kc) * dm).astype(BF16)
            da = (_dot_nt(doc, vc) * dm).astype(BF16)
            xdo = (xi * doc.astype(F32)).astype(BF16)
            kz = (kf * zc).astype(BF16)
            vz = (vc.astype(F32) * zc).astype(BF16)
            dq_i = _dot(da, kc)
            dq_c = _dot_nt(xdo, sb)
            dk_i = _dot_tn(da, qc)
            dk_s = _dot_nt(vz, dsnb)
            dq_ref[rs, hs] = dq_i + dq_c
            dk_ref[rs, hs] = dk_i + dk_s
            dv_ref[rs, hs] = _dot_tn(a, doc) + _dot(kz, dsnb)
            rowterm = qf * (w_qi * dq_i + w_qc * dq_c) + kf * (w_ki * dk_i + w_ks * dk_s)
            acc = acc + jnp.sum(rowterm, axis=0, keepdims=True)
            acc = acc + (float(CR) * dec) * jnp.sum(s * dsn, axis=0, keepdims=True)
            ds_scr[h] = _dot_tn((xi * qf).astype(BF16), doc) + dec * dsn
        dlg_scr[row0 + h:row0 + h + 1, :] += acc


def _ret_bwd(q, k, v, dof, dob, sst, rst, rlf, rlb, *, tm):
    t_len = q.shape[0]
    nt, nc = t_len // tm, tm // CR

    def body(qf_ref, kf_ref, vf_ref, dof_ref, sst_ref, qb_ref, kb_ref, vb_ref, dob_ref, rst_ref, rlf_ref, rlb_ref,
             dqf_ref, dkf_ref, dvf_ref, dqb_ref, dkb_ref, dvb_ref, dscf_ref, dscb_ref, dlg_ref,
             dsf_scr, dsb_scr, dlg_scr):
        i = pl.program_id(0)

        @pl.when(i == 0)
        def _():
            dsf_scr[...] = jnp.zeros_like(dsf_scr)
            dsb_scr[...] = jnp.zeros_like(dsb_scr)
            dlg_scr[...] = jnp.zeros_like(dlg_scr)

        lgf = _log_sigmoid(rlf_ref[...])
        lgb = _log_sigmoid(rlb_ref[...])
        _ret_bwd_dir(qf_ref, kf_ref, vf_ref, dof_ref, sst_ref, dqf_ref, dkf_ref, dvf_ref, dsf_scr, dlg_scr, lgf, True, nc, 0)
        _ret_bwd_dir(qb_ref, kb_ref, vb_ref, dob_ref, rst_ref, dqb_ref, dkb_ref, dvb_ref, dsb_scr, dlg_scr, lgb, False, nc, H)

        @pl.when(i == nt - 1)
        def _():
            dscf_ref[...] = dsf_scr[...]
            dscb_ref[...] = dsb_scr[...]
            tot = jnp.broadcast_to(jnp.sum(dlg_scr[...], axis=1, keepdims=True), (8, 128))
            ri = lax.broadcasted_iota(jnp.int32, (8, 128), 0)
            li = lax.broadcasted_iota(jnp.int32, (8, 128), 1)
            dlg_ref[...] = jnp.zeros_like(dlg_ref)
            dlg_ref[0:1, 0:128] = jnp.sum(jnp.where(ri == li, tot, 0.0), axis=0, keepdims=True)
            dlg_ref[1:2, 0:128] = jnp.sum(jnp.where(ri - H == li, tot, 0.0), axis=0, keepdims=True)

    rr = functools.partial(_rows_rev, nt=nt)
    st_f = pl.BlockSpec((nc, H, DH, DH), lambda i: (nt - 1 - i, 0, 0, 0))
    st_b = pl.BlockSpec((nc, H, DH, DH), lambda i: (i, 0, 0, 0))
    tok = jax.ShapeDtypeStruct((t_len, RW), F32)
    st = jax.ShapeDtypeStruct((H, DH, DH), F32)
    return pl.pallas_call(
        body, name="ret_bwd", grid=(nt,),
        in_specs=[rr(tm, RW), rr(tm, RW), rr(tm, RW), rr(tm, RW), st_f,
                  _rows(tm, RW), _rows(tm, RW), _rows(tm, RW), _rows(tm, RW), st_b,
                  _whole((H, 128)), _whole((H, 128))],
        out_specs=[rr(tm, RW), rr(tm, RW), rr(tm, RW), _rows(tm, RW), _rows(tm, RW), _rows(tm, RW),
                   _acc((H, DH, DH)), _acc((H, DH, DH)), _acc((8, D))],
        out_shape=(tok, tok, tok, tok, tok, tok, st, st, jax.ShapeDtypeStruct((8, D), F32)),
        scratch_shapes=[pltpu.VMEM((H, DH, DH), F32), pltpu.VMEM((H, DH, DH), F32), pltpu.VMEM((8, 128), F32)],
        compiler_params=_cparams(1),
    )(q, k, v, dof, sst, q, k, v, dob, rst, rlf, rlb)


def _in_bwd(x, zuv, dya, dqf, dkf, dvf, dqb, dkb, dvb, dgfb, dx1, cos, sin, mod6, norm1, win, sgw, sgbt, sggain, *, tm):
    t_len = x.shape[0]
    nt, nc = t_len // tm, tm // C

    def body(x_ref, zuv_ref, dya_ref, dqf_ref, dkf_ref, dvf_ref, dqb_ref, dkb_ref, dvb_ref, dgfb_ref, dx1_ref,
             cos_ref, sin_ref, mod_ref, n1_ref, win_ref, sgw_ref, sgbt_ref, sgg_ref,
             gx_ref, dzt_ref, small_ref, dsgw_ref, dsgbt_ref, dz_ref):
        i = pl.program_id(0)

        @pl.when(i == 0)
        def _():
            small_ref[...] = jnp.zeros_like(small_ref)
            dsgw_ref[...] = jnp.zeros_like(dsgw_ref)
            dsgbt_ref[...] = jnp.zeros_like(dsgbt_ref)

        zuv = zuv_ref[...]
        _, (ua, dgu, dgv, mixed, vn_b, vah_l, rv_l) = _sg_forward(zuv[:, :AW], zuv[:, AW:], sgw_ref, sgbt_ref, sgg_ref, nc)
        dya = dya_ref[...]
        dz_ref[:, 0:AW] = (dya * mixed * dgu).astype(BF16)
        dmixed = dya * ua
        gain = sgg_ref[...]
        for g in range(G):
            gs = slice(g * DH, (g + 1) * DH)
            dmg = dmixed[:, gs]
            dmb = dmg.astype(BF16)
            wgt = sgw_ref[g].astype(BF16)
            dsw = jnp.zeros((C, C), F32)
            dsb = jnp.zeros((C, DH), F32)
            rows = []
            for c in range(nc):
                rs = slice(c * C, (c + 1) * C)
                dsw = dsw + _dot_nt(dmb[rs, :], vn_b[g][rs, :])
                dsb = dsb + dmg[rs, :]
                rows.append(_dot_tn(wgt, dmb[rs, :]))
            dsgw_ref[g] += dsw
            dsgbt_ref[g] += dsb
            dvn = jnp.concatenate(rows, axis=0) if nc > 1 else rows[0]
            vah, rv = vah_l[g], rv_l[g]
            small_ref[3:4, gs] += jnp.sum(dvn * vah, axis=0, keepdims=True)
            dyg = dvn * gain[:, gs]
            dva = rv * (dyg - vah * jnp.mean(dyg * vah, axis=-1, keepdims=True))
            dz_ref[:, AW + g * DH:AW + (g + 1) * DH] = (dva * dgv[:, gs]).astype(BF16)

        cos = jnp.tile(cos_ref[...], (1, H))
        nsins = -jnp.tile(sin_ref[...], (1, H))
        dz_ref[:, 1024:1536] = _rope(dqf_ref[...] + dqb_ref[...], cos, nsins).astype(BF16)
        dz_ref[:, 1536:2048] = (_rope(dkf_ref[...] + dkb_ref[...], cos, nsins) * KSCALE).astype(BF16)
        dz_ref[:, 2048:2560] = (dvf_ref[...] + dvb_ref[...]).astype(BF16)
        dz_ref[:, 2560:3584] = dgfb_ref[...]

        dz = dz_ref[...]
        dzt_ref[...] = dz.T
        dhx = _dot(dz, win_ref[...])
        x = x_ref[...]
        r = lax.rsqrt(jnp.mean(x * x, axis=-1, keepdims=True) + EPS)
        xh = x * r
        n1, sc1 = n1_ref[...], mod_ref[1:2, :]
        small_ref[0:1, :] += jnp.sum(dhx, axis=0, keepdims=True)
        dhxx = jnp.sum(dhx * xh, axis=0, keepdims=True)
        small_ref[1:2, :] += dhxx * n1
        small_ref[2:3, :] += dhxx * (1.0 + sc1)
        dyg = dhx * (n1 * (1.0 + sc1))
        gx_ref[...] = dx1_ref[...] + r * (dyg - xh * jnp.mean(dyg * xh, axis=-1, keepdims=True))

        @pl.when(i == nt - 1)
        def _():
            ri = lax.broadcasted_iota(jnp.int32, (C, DH), 0)
            li = lax.broadcasted_iota(jnp.int32, (C, DH), 1)
            for g in range(G):
                tot = jnp.broadcast_to(jnp.sum(dsgbt_ref[g], axis=1, keepdims=True), (C, DH))
                small_ref[4 + g:5 + g, 0:DH] = jnp.sum(jnp.where(ri == li, tot, 0.0), axis=0, keepdims=True)

    tok = functools.partial(_rows, tm)
    return pl.pallas_call(
        body, name="in_bwd", grid=(nt,),
        in_specs=[tok(D), tok(2 * AW), tok(AW), tok(RW), tok(RW), tok(RW), tok(RW), tok(RW), tok(RW),
                  tok(2 * RW), tok(D), tok(DH), tok(DH), _whole((6, D)), _whole((1, D)), _whole((INC, D)),
                  _whole((G, C, C)), _whole((G, C, 128)), _whole((1, AW))],
        out_specs=[tok(D), _cols(INC, tm), _acc((8, D)), _acc((G, C, C))],
        out_shape=(jax.ShapeDtypeStruct((t_len, D), F32), jax.ShapeDtypeStruct((INC, t_len), BF16),
                   jax.ShapeDtypeStruct((8, D), F32), jax.ShapeDtypeStruct((G, C, C), F32)),
        scratch_shapes=[pltpu.VMEM((G, C, 128), F32), pltpu.VMEM((tm, INC), BF16)],
        compiler_params=_cparams(1),
    )(x, zuv, dya, dqf, dkf, dvf, dqb, dkb, dvb, dgfb, dx1, cos, sin, mod6, norm1, win, sgw, sgbt, sggain)


def _tn_matmul(at, b, *, bm, bt, name, init=None, init_rows=None, after=()):
    m, t_len = at.shape
    n = b.shape[1]
    ni, ntt = m // bm, t_len // bt
    has_init = init is not None

    def body(*refs):
        o_ref, ob_ref = refs[-2:]
        a_ref, b_ref = refs[:2]
        init_ref = refs[2] if has_init else None
        i, t = pl.program_id(0), pl.program_id(1)

        @pl.when(t == 0)
        def _():
            o_ref[...] = jnp.zeros_like(o_ref)

        if has_init:
            for ib in range(ni):
                lo, hi = max(init_rows[0], ib * bm), min(init_rows[1], (ib + 1) * bm)
                if lo < hi:
                    @pl.when(jnp.logical_and(t == 0, i == ib))
                    def _(lo=lo, hi=hi, ib=ib):
                        o_ref[lo - ib * bm:hi - ib * bm, :] = init_ref[lo - init_rows[0]:hi - init_rows[0], :]

        o_ref[...] += _dot(a_ref[...], b_ref[...])

        @pl.when(t == ntt - 1)
        def _():
            ob_ref[...] = o_ref[...].astype(BF16)

    in_specs = [pl.BlockSpec((bm, bt), lambda i, t: (i, t)), pl.BlockSpec((bt, n), lambda i, t: (t, 0))]
    args = [at, b]
    if has_init:
        in_specs.append(pl.BlockSpec(init.shape, lambda i, t: (0, 0), pipeline_mode=pl.Buffered(1)))
        args.append(init)
    for arr in after:
        in_specs.append(pl.BlockSpec(memory_space=pl.ANY))
        args.append(arr)
    out_spec = pl.BlockSpec((bm, n), lambda i, t: (i, 0))
    return pl.pallas_call(
        body, name=name, grid=(ni, ntt),
        in_specs=in_specs,
        out_specs=[out_spec, out_spec],
        out_shape=(jax.ShapeDtypeStruct((m, n), F32), jax.ShapeDtypeStruct((m, n), BF16)),
        compiler_params=_cparams(2),
    )(*args)


def _ctx_bwd(ctx, hc, kvc, cmod6, norm1, win, rlf, rlb, dscf, dscb, dlg_in):
    lc = ctx.shape[0]

    def body(ctx_ref, hc_ref, kvc_ref, cmod_ref, n1_ref, win_ref, rlf_ref, rlb_ref, dscf_ref, dscb_ref, dlgin_ref,
             dwin_ref, small_ref, dlg_ref):
        k = kvc_ref[:, :RW]
        v = kvc_ref[:, RW:]
        t = lax.broadcasted_iota(jnp.int32, (lc, 1), 0).astype(F32)
        lgf = _log_sigmoid(rlf_ref[...])
        lgb = _log_sigmoid(rlb_ref[...])
        dks, dvs = [], []
        lane = lax.broadcasted_iota(jnp.int32, (1, 128), 1)
        row_f = jnp.zeros((1, 128), F32)
        row_b = jnp.zeros((1, 128), F32)
        for h in range(H):
            hs = slice(h * DH, (h + 1) * DH)
            wf = jnp.exp(lgf[h:h + 1, 0:1] * (lc - 1.0 - t))
            wb = jnp.exp(lgb[h:h + 1, 0:1] * t)
            kh, vhb = k[:, hs], v[:, hs].astype(BF16)
            dsf, dsb = dscf_ref[h].astype(BF16), dscb_ref[h].astype(BF16)
            dkf = wf * _dot_nt(vhb, dsf)
            dkb = wb * _dot_nt(vhb, dsb)
            dvs.append(_dot((kh * wf).astype(BF16), dsf) + _dot((kh * wb).astype(BF16), dsb))
            dks.append((dkf + dkb) * KSCALE)
            lf = jnp.sum((lc - 1.0 - t) * kh * dkf, axis=0, keepdims=True)
            lb = jnp.sum(t * kh * dkb, axis=0, keepdims=True)
            row_f = row_f + jnp.where(lane == h, jnp.sum(lf, axis=1, keepdims=True), 0.0)
            row_b = row_b + jnp.where(lane == h, jnp.sum(lb, axis=1, keepdims=True), 0.0)
        dlg_ref[...] = dlgin_ref[...]
        dlg_ref[0:1, 0:128] += row_f
        dlg_ref[1:2, 0:128] += row_b
        dkv = jnp.concatenate(dks + dvs, axis=1).astype(BF16)
        dhc = _dot(dkv, win_ref[KV_LO:KV_HI, :])
        dwin_ref[...] = _dot_tn(dkv, hc_ref[...])
        x = ctx_ref[...]
        r = lax.rsqrt(jnp.mean(x * x, axis=-1, keepdims=True) + EPS)
        xh = x * r
        small_ref[...] = jnp.zeros_like(small_ref)
        small_ref[0:1, :] = jnp.sum(dhc, axis=0, keepdims=True)
        dhxx = jnp.sum(dhc * xh, axis=0, keepdims=True)
        small_ref[1:2, :] = dhxx * n1_ref[...]
        small_ref[2:3, :] = dhxx * (1.0 + cmod_ref[1:2, :])

    return pl.pallas_call(
        body, name="ctx_bwd",
        out_shape=(jax.ShapeDtypeStruct((2 * RW, D), F32), jax.ShapeDtypeStruct((8, D), F32),
                   jax.ShapeDtypeStruct((8, D), F32)),
        compiler_params=_cparams(),
    )(ctx, hc, kvc, cmod6, norm1, win, rlf, rlb, dscf, dscb, dlg_in)


def _adam_math(w, g, m, v):
    m = ADAM_B1 * m + (1.0 - ADAM_B1) * g
    v = ADAM_B2 * v + (1.0 - ADAM_B2) * (g * g)
    m_hat = m / (1.0 - ADAM_B1 ** ADAM_STEP)
    v_hat = v / (1.0 - ADAM_B2 ** ADAM_STEP)
    delta = -ADAM_LR * (m_hat / (jnp.sqrt(v_hat) + ADAM_EPS) + ADAM_WD * w)
    return delta, m, v


def _place():
    x, y, c = lax.axis_index("x"), lax.axis_index("y"), lax.axis_index("c")
    return x, y, c, 4 * x + 2 * y + c


def _flip(x, y, c, k):
    px = 1 - x if (k >> 2) & 1 else x
    py = 1 - y if (k >> 1) & 1 else y
    pc = 1 - c if k & 1 else c
    return (px, py, pc), 4 * px + 2 * py + pc


def _gather_direct(buf_ref, send_sems, recv_sems, sem0=0):
    x, y, c, me = _place()
    sends = []
    for k in range(1, NDEV):
        dev, _ = _flip(x, y, c, k)
        cp = pltpu.make_async_remote_copy(
            src_ref=buf_ref.at[me], dst_ref=buf_ref.at[me],
            send_sem=send_sems.at[sem0 + k - 1], recv_sem=recv_sems.at[sem0 + k - 1],
            device_id=dev, device_id_type=MESH)
        cp.start()
        sends.append(cp)
    for k in range(1, NDEV):
        dev, idx = _flip(x, y, c, k)
        pltpu.make_async_remote_copy(
            src_ref=buf_ref.at[idx], dst_ref=buf_ref.at[idx],
            send_sem=send_sems.at[sem0 + k - 1], recv_sem=recv_sems.at[sem0 + k - 1],
            device_id=dev, device_id_type=MESH).wait_recv()
    for cp in sends:
        cp.wait_send()


def _mod_gather(c_row, cctx_row, wmod, bmod):
    ncol = wmod.shape[1]

    def body(c_ref, cctx_ref, wmod_ref, bmod_ref, mod_ref, cs_ref, cbuf, pbuf, send_sems, recv_sems):
        _, _, _, me = _place()
        cbuf[me] = jnp.broadcast_to(c_ref[...], (8, D))
        _gather_direct(cbuf, send_sems, recv_sems, 0)
        row = lax.broadcasted_iota(jnp.int32, (16, D), 0)
        call = jnp.where(row == 8, jnp.broadcast_to(cctx_ref[...], (16, D)), 0.0)
        for d in range(NDEV):
            call = jnp.where(row == d, jnp.concatenate([cbuf[d], cbuf[d]], axis=0), call)
        cs = call * _sigmoid(call)
        cs_ref[...] = cs
        pbuf[me] = _dot(cs.astype(BF16), wmod_ref[...].astype(BF16))
        _gather_direct(pbuf, send_sems, recv_sems, NDEV - 1)
        for d in range(NDEV):
            mod_ref[:, d * ncol:(d + 1) * ncol] = pbuf[d] + bmod_ref[:, d * ncol:(d + 1) * ncol]

    return pl.pallas_call(
        body, name="mod_gather",
        out_shape=(jax.ShapeDtypeStruct((16, 6 * D), F32), jax.ShapeDtypeStruct((16, D), F32)),
        scratch_shapes=[pltpu.VMEM((NDEV, 8, D), F32), pltpu.VMEM((NDEV, 16, ncol), F32),
                        pltpu.SemaphoreType.DMA((2 * (NDEV - 1),)), pltpu.SemaphoreType.DMA((2 * (NDEV - 1),))],
        compiler_params=_cparams(),
    )(c_row, cctx_row, wmod, bmod)


def _weight_gather(shards, after=()):
    n = len(shards)
    na = len(after)

    def body(*refs):
        in_refs, out_refs = refs[:n], refs[n + na:2 * n + na]
        cast_refs = refs[2 * n + na:3 * n + na]
        send_sems, recv_sems, local_sems = refs[3 * n + na:]
        x, y, c, me = _place()
        sib = (x, y, 1 - c)
        chips = [(1 - x, y), (x, 1 - y), (1 - x, 1 - y)]

        def blk(px, py, pc):
            return 4 * px + 2 * py + pc

        def copy(w, k, block, to, src=None):
            dst = out_refs[w].at[block]
            return pltpu.make_async_remote_copy(
                src_ref=dst if src is None else src, dst_ref=dst,
                send_sem=send_sems.at[w, k], recv_sem=recv_sems.at[w, k], device_id=to, device_id_type=MESH)

        first, passed, mine = [], [], []
        for w in range(n):
            cast_refs[w][...] = in_refs[w][...].astype(BF16)
            m = pltpu.make_async_copy(cast_refs[w], out_refs[w].at[me], local_sems.at[w])
            m.start()
            mine.append(m)
            cps = [copy(w, 0, me, sib, src=cast_refs[w])]
            cps += [copy(w, 1 + j, me, (*chip, c), src=cast_refs[w]) for j, chip in enumerate(chips)]
            for cp in cps:
                cp.start()
            first += cps
        for w in range(n):
            for j, chip in enumerate(chips):
                b = blk(*chip, c)
                copy(w, 1 + j, b, (x, y, c)).wait_recv()
                fw = copy(w, 4 + j, b, sib)
                fw.start()
                passed.append(fw)
        for w in range(n):
            copy(w, 0, blk(x, y, 1 - c), (x, y, c)).wait_recv()
            for j, chip in enumerate(chips):
                copy(w, 4 + j, blk(*chip, 1 - c), (x, y, c)).wait_recv()
        for cp in first + passed:
            cp.wait_send()
        for m in mine:
            m.wait()

    vm = pl.BlockSpec(memory_space=pltpu.VMEM)
    hbm = pl.BlockSpec(memory_space=pltpu.HBM)
    return pl.pallas_call(
        body, name="weight_gather",
        in_specs=[vm] * (n + na), out_specs=[hbm] * n,
        out_shape=tuple(jax.ShapeDtypeStruct((NDEV,) + s.shape, BF16) for s in shards),
        scratch_shapes=[pltpu.VMEM(s.shape, BF16) for s in shards]
        + [pltpu.SemaphoreType.DMA((n, 7)), pltpu.SemaphoreType.DMA((n, 7)), pltpu.SemaphoreType.DMA((n,))],
        compiler_params=_cparams(),
    )(*shards, *after)


_HBM = pl.BlockSpec(memory_space=pltpu.HBM)
_SEMS = pl.BlockSpec(memory_space=pltpu.SEMAPHORE)
_EFFECT = pltpu.SideEffectType.DATAFLOW_SIDE_EFFECTING


def _exchange_copy(src_refs, land_refs, send_sems, recv_sems, w, k, scatter, landing_slot_is_mine):
    x, y, c, me = _place()
    dev, pidx = _flip(x, y, c, k)
    src = src_refs[w].at[pidx] if scatter else src_refs[w]
    slot = me if landing_slot_is_mine else pidx
    return pltpu.make_async_remote_copy(
        src_ref=src, dst_ref=land_refs[w].at[slot],
        send_sem=send_sems.at[w * (NDEV - 1) + k - 1], recv_sem=recv_sems.at[w * (NDEV - 1) + k - 1],
        device_id=dev, device_id_type=MESH)


def _exchange_start(srcs, *, scatter, name):
    n = len(srcs)
    lands = [lax.empty((NDEV,) + tuple(s.shape[-2:]), s.dtype) for s in srcs]

    def body(*refs):
        src_refs, land_refs = refs[:n], refs[n:2 * n]
        send_sems, recv_sems = refs[2 * n], refs[2 * n + 1]
        token = refs[-1]
        for w in range(n):
            for k in range(1, NDEV):
                _exchange_copy(src_refs, land_refs, send_sems, recv_sems, w, k, scatter, True).start()
        token[...] = jnp.zeros_like(token)

    arrs = list(srcs) + lands
    out_shape = ([pltpu.SemaphoreType.DMA((n * (NDEV - 1),)), pltpu.SemaphoreType.DMA((n * (NDEV - 1),))]
                 + [pltpu.HBM(a.shape, a.dtype) for a in arrs] + [jax.ShapeDtypeStruct((8, 128), F32)])
    res = pl.pallas_call(
        body, name=name, out_shape=tuple(out_shape),
        in_specs=[_HBM] * (2 * n),
        out_specs=tuple([_SEMS, _SEMS] + [_HBM] * (2 * n) + [pl.BlockSpec(memory_space=pltpu.VMEM)]),
        input_output_aliases={i: 2 + i for i in range(2 * n)},
        compiler_params=pltpu.CompilerParams(has_side_effects=_EFFECT),
    )(*[pltpu.with_memory_space_constraint(a, pltpu.HBM) for a in arrs])
    return res[:-1], res[-1]


def _exchange_wait(handles, after, *, scatter, name):
    n = (len(handles) - 2) // 2
    na = len(after)

    def body(*refs):
        src_refs, land_refs = refs[:n], refs[n:2 * n]
        send_sems, recv_sems = refs[2 * n], refs[2 * n + 1]
        for w in range(n):
            for k in range(1, NDEV):
                cp = _exchange_copy(src_refs, land_refs, send_sems, recv_sems, w, k, scatter, False)
                cp.wait_send()
                cp.wait_recv()

    arrs = handles[2:]
    res = pl.pallas_call(
        body, name=name, out_shape=tuple(pltpu.HBM(a.shape, a.dtype) for a in arrs),
        in_specs=[_HBM] * (2 * n) + [_SEMS, _SEMS] + [_HBM] * na,
        out_specs=tuple([_HBM] * (2 * n)),
        input_output_aliases={i: i for i in range(2 * n)},
        compiler_params=pltpu.CompilerParams(has_side_effects=_EFFECT),
    )(*arrs, handles[0], handles[1], *[pltpu.with_memory_space_constraint(a, pltpu.HBM) for a in after])
    return res[:n], res[n:]


def _cast_bf16(arrs, *, name, after=()):
    n = len(arrs)

    def body(*refs):
        for i_ref, o_ref in zip(refs[:n], refs[n + len(after):]):
            o_ref[...] = i_ref[...].astype(BF16)

    return pl.pallas_call(
        body, name=name, out_shape=tuple(jax.ShapeDtypeStruct(a.shape, BF16) for a in arrs),
        in_specs=[pl.BlockSpec(memory_space=pltpu.VMEM)] * n + [pl.BlockSpec(memory_space=pl.ANY)] * len(after),
        compiler_params=_cparams())(*arrs, *after)


def _rs_adam(me_arr, own, land, w, m, v, *, name):
    def body(me_ref, own_ref, land_ref, w_ref, m_ref, v_ref, g_ref, d_ref, mo_ref, vo_ref):
        me = me_ref[0]
        acc = jnp.zeros(own_ref.shape, F32)
        for p in range(NDEV):
            acc = acc + jnp.where(p == me, own_ref[...], land_ref[p].astype(F32))
        g_ref[...] = acc
        d_ref[...], mo_ref[...], vo_ref[...] = _adam_math(w_ref[...], acc, m_ref[...], v_ref[...])

    sh = jax.ShapeDtypeStruct(own.shape, F32)
    vm = pl.BlockSpec(memory_space=pltpu.VMEM)
    return pl.pallas_call(
        body, name=name, out_shape=(sh, sh, sh, sh),
        in_specs=[pl.BlockSpec(memory_space=pltpu.SMEM)] + [vm] * 5,
        compiler_params=_cparams(),
    )(me_arr, own, land, w, m, v)


ROW_F, ROW_I, ROW_C, ROW_G1, ROW_LG = 0, 8, 16, 24, 32
PACK_ROWS = 40


def _small_sum(me_arr, pack, pack_land, sgw, sgw_land, wmod):
    ncol = wmod.shape[1]

    def body(me_ref, pack_ref, pland_ref, sgw_ref, sland_ref, wmod_ref, sum_ref, all_ref, sgw_sum_ref, part_ref):
        me = me_ref[0]
        acc = jnp.zeros((PACK_ROWS, D), F32)
        acc_w = jnp.zeros(sgw_ref.shape, F32)
        for p in range(NDEV):
            rows = jnp.where(p == me, pack_ref[...], pland_ref[p])
            all_ref[p] = rows
            acc = acc + rows
            acc_w = acc_w + jnp.where(p == me, sgw_ref[...], sland_ref[p])
        sum_ref[...] = acc
        sgw_sum_ref[...] = acc_w
        zero = jnp.zeros((1, D), F32)
        dcmod = jnp.concatenate([acc[ROW_C:ROW_C + 1], acc[ROW_C + 1:ROW_C + 2], zero, zero, zero, zero], axis=1)
        mine = jnp.zeros((1, ncol), F32)
        for d in range(NDEV):
            mine = mine + jnp.where(d == me, dcmod[:, d * ncol:(d + 1) * ncol], 0.0)
        part_ref[...] = _dot_nt(jnp.broadcast_to(mine, (8, ncol)).astype(BF16), wmod_ref[...].astype(BF16))

    vm = pl.BlockSpec(memory_space=pltpu.VMEM)
    return pl.pallas_call(
        body, name="small_sum",
        out_shape=(jax.ShapeDtypeStruct((PACK_ROWS, D), F32), jax.ShapeDtypeStruct((NDEV, PACK_ROWS, D), F32),
                   jax.ShapeDtypeStruct(sgw.shape, F32), jax.ShapeDtypeStruct((8, D), F32)),
        in_specs=[pl.BlockSpec(memory_space=pltpu.SMEM)] + [vm] * 5,
        compiler_params=_cparams(),
    )(me_arr, pack, pack_land, sgw, sgw_land, wmod)


def _cctx_final(me_arr, part, part_land, cctx_row, m_row, v_row):
    def body(me_ref, part_ref, land_ref, c_ref, m_ref, v_ref, g_ref, d_ref, mo_ref, vo_ref):
        me = me_ref[0]
        tot = jnp.zeros((8, D), F32)
        for p in range(NDEV):
            tot = tot + jnp.where(p == me, part_ref[...], land_ref[p])
        cc = c_ref[...]
        _, dsilu = _silu_parts(cc)
        g = tot[0:1, :] * dsilu
        g_ref[...] = g
        d_ref[...], mo_ref[...], vo_ref[...] = _adam_math(cc, g, m_ref[...], v_ref[...])

    row = jax.ShapeDtypeStruct((1, D), F32)
    vm = pl.BlockSpec(memory_space=pltpu.VMEM)
    return pl.pallas_call(
        body, name="cctx_final", out_shape=(row, row, row, row),
        in_specs=[pl.BlockSpec(memory_space=pltpu.SMEM)] + [vm] * 5,
        compiler_params=_cparams(),
    )(me_arr, part, part_land, cctx_row, m_row, v_row)


def _adam_small(s, sgw_g, params):
    names = ["norm1", "norm2", "norm_f", "sg_gain", "sg_b", "ret_logit_f", "ret_logit_b", "b_mod", "sg_w"]
    flat = [a for n in names for a in params[n]]

    def body(*refs):
        s_ref, sgw_ref = refs[0], refs[1]
        p_refs = refs[2:2 + 3 * len(names)]
        o_refs = refs[2 + 3 * len(names):]
        loss_ref = o_refs[-1]

        def row(r):
            return s_ref[r:r + 1, :]

        grads = {
            "norm1": row(ROW_I + 2) + row(ROW_C + 2),
            "norm2": row(ROW_F + 4),
            "norm_f": row(ROW_F + 0),
            "sg_gain": s_ref[ROW_I + 3:ROW_I + 4, 0:AW],
            "sg_b": s_ref[ROW_I + 4:ROW_I + 8, 0:DH],
            "sg_w": sgw_ref[...],
        }
        for j, n in enumerate(names):
            w_ref, m_ref, v_ref = p_refs[3 * j:3 * j + 3]
            g_ref, d_ref, mo_ref, vo_ref = o_refs[4 * j:4 * j + 4]
            w = w_ref[...]
            if n in ("ret_logit_f", "ret_logit_b"):
                r = ROW_LG + (0 if n == "ret_logit_f" else 1)
                g = s_ref[r:r + 1, 0:H] * _sigmoid(-w)
            elif n == "b_mod":
                rows = [row(ROW_I + 0) + row(ROW_C + 0), row(ROW_I + 1) + row(ROW_C + 1), row(ROW_G1),
                        row(ROW_F + 2), row(ROW_F + 3), row(ROW_F + 1)]
                g = jnp.concatenate(rows, axis=1)
            else:
                g = grads[n]
            g_ref[...] = g
            d_ref[...], mo_ref[...], vo_ref[...] = _adam_math(w, g, m_ref[...], v_ref[...])
        loss_ref[...] = jnp.full((1, 1), 0.5 / D, F32) * jnp.sum(row(ROW_F + 5), axis=1, keepdims=True)

    out_shape = []
    for n in names:
        w = params[n][0]
        out_shape += [jax.ShapeDtypeStruct(w.shape, F32)] * 4
    out_shape.append(jax.ShapeDtypeStruct((1, 1), F32))
    outs = pl.pallas_call(body, name="adam_small", out_shape=tuple(out_shape), compiler_params=_cparams())(
        s, sgw_g, *flat)
    return {n: tuple(outs[4 * j:4 * j + 4]) for j, n in enumerate(names)}, outs[-1]


def _wmod_grad(cs_all, dmod_cols, wmod, m, v):
    ncol = wmod.shape[1]

    def body(cs_ref, dm_ref, w_ref, m_ref, v_ref, g_ref, d_ref, mo_ref, vo_ref):
        g = _dot_tn(cs_ref[...].astype(BF16), dm_ref[...].astype(BF16))
        g_ref[...] = g
        d_ref[...], mo_ref[...], vo_ref[...] = _adam_math(w_ref[...], g, m_ref[...], v_ref[...])

    sh = jax.ShapeDtypeStruct((D, ncol), F32)
    return pl.pallas_call(
        body, name="wmod_grad", out_shape=(sh, sh, sh, sh),
        compiler_params=_cparams(),
    )(cs_all, dmod_cols, wmod, m, v)


def _rope_tables(t_len):
    n_freq = DH // 4
    inv = (ROPE_BASE ** (-np.arange(n_freq, dtype=np.float32) / n_freq)).astype(np.float32)
    tok = np.arange(t_len)
    rows = (tok // GRID_W).astype(np.float32)
    cols = (tok % GRID_W).astype(np.float32)
    ang_r = rows[:, None] * inv[None, :]
    ang_c = cols[:, None] * inv[None, :]
    cos = np.concatenate([np.cos(ang_r)] * 2 + [np.cos(ang_c)] * 2, axis=1).astype(np.float32)
    sin = np.concatenate([-np.sin(ang_r), np.sin(ang_r), -np.sin(ang_c), np.sin(ang_c)], axis=1).astype(np.float32)
    return jnp.asarray(cos), jnp.asarray(sin)


def _local_step(x, tgt, ctx, mod6, cmod6, norm1, norm2, normf, win, wout, ffn_weights, sgw, sgb, sggain, rlf, rlb,
                *, tm_a, tm_b, tm_f, tm_m, tm_r, tm_i, bt_w, after_first_grads=None, small_path=None):
    t_len = x.shape[0]
    cos, sin = _rope_tables(t_len)
    sgbt = jnp.broadcast_to(sgb[:, :, None], (G, C, 128))
    rlf = jnp.broadcast_to(rlf.reshape(H, 1), (H, 128))
    rlb = jnp.broadcast_to(rlb.reshape(H, 1), (H, 128))
    hc, kvc, scf, scb = _ctx_fwd(ctx, cmod6, norm1, win, rlf, rlb)
    hx, zuv, q, k, v, gfb, of, ya, sst = _fwd_a(x, mod6, norm1, win, sgw, sgbt, sggain, cos, sin, rlf, scf, tm=tm_a)
    ob, ycat, y, x1, rst = _fwd_b(q, k, v, of, gfb, ya, x, mod6, wout, rlb, scb, tm=tm_b)
    wg, wu, wd = ffn_weights(x1) if callable(ffn_weights) else ffn_weights
    dx1, h2, da, db, hm, df, small_f = _ffn(x1, tgt, mod6, norm2, normf, wg, wu, wd, tm=tm_f)
    bt2 = min(2 * bt_w, t_len)
    d_wd, d_wd_b = _tn_matmul(hm, df, bm=DFF // 2, bt=bt2, name="dw_down")
    d_wg, d_wg_b = _tn_matmul(da, h2, bm=DFF // 2, bt=bt2, name="dw_gate")
    d_wu, d_wu_b = _tn_matmul(db, h2, bm=DFF // 2, bt=bt2, name="dw_up")
    dy, dya, dgfb, dof, dob, dg1 = _mid_bwd(dx1, y, of, ob, gfb, mod6, wout, tm=tm_m)
    d_wo, d_wo_b = _tn_matmul(ycat, dy, bm=D, bt=bt_w, name="dw_out")
    if after_first_grads is not None:
        rlf = rlf + after_first_grads((d_wd_b, d_wg_b, d_wu_b, d_wo_b))
    dqf, dkf, dvf, dqb, dkb, dvb, dscf, dscb, dlg = _ret_bwd(q, k, v, dof, dob, sst, rst, rlf, rlb, tm=tm_r)
    gx, dz, small_i, dsgw = _in_bwd(x, zuv, dya, dqf, dkf, dvf, dqb, dkb, dvb, dgfb, dx1, cos, sin,
                                    mod6, norm1, win, sgw, sgbt, sggain, tm=tm_i)
    dwin_c, small_c, dlg = _ctx_bwd(ctx, hc, kvc, cmod6, norm1, win, rlf, rlb, dscf, dscb, dlg)
    pack = jnp.concatenate([small_f, small_i, small_c, dg1, dlg], axis=0)
    after = small_path(pack, dsgw) if small_path is not None else ()
    d_wi, d_wi_b = _tn_matmul(dz, hx, bm=INC // 2, bt=bt_w, name="dw_in", init=dwin_c,
                              init_rows=(KV_LO, KV_HI), after=after)
    grads = {"w_in": (d_wi, d_wi_b), "w_out": (d_wo, d_wo_b), "w_gate": (d_wg, d_wg_b), "w_up": (d_wu, d_wu_b),
             "w_down": (d_wd, d_wd_b)}
    return gx, grads, pack, dsgw


def kernel(x, c, ctx, c_ctx, w_mod, b_mod, norm1, w_in, sg_gain, sg_w, sg_b, ret_logit_f, ret_logit_b, w_out, norm2, w_gate, w_up, w_down, norm_f, loss_target, m_c_ctx, m_w_mod, m_b_mod, m_norm1, m_w_in, m_sg_gain, m_sg_w, m_sg_b, m_ret_logit_f, m_ret_logit_b, m_w_out, m_norm2, m_w_gate, m_w_up, m_w_down, m_norm_f, v_c_ctx, v_w_mod, v_b_mod, v_norm1, v_w_in, v_sg_gain, v_sg_w, v_sg_b, v_ret_logit_f, v_ret_logit_b, v_w_out, v_norm2, v_w_gate, v_w_up, v_w_down, v_norm_f):
    t_len = x.shape[1]
    tm = min(512, t_len)
    me = 4 * lax.axis_index("x") + 2 * lax.axis_index("y") + lax.axis_index("c")
    tr = lambda a: jnp.transpose(a[0])

    cctx_row = c_ctx.reshape(1, D)
    mod_all, cs_all = _mod_gather(c, cctx_row, w_mod[0], b_mod)
    mod6 = lax.dynamic_slice(mod_all, (me, 0), (1, 6 * D)).reshape(6, D)
    cmod6 = mod_all[8].reshape(6, D)

    g_in, g_out = _weight_gather([tr(w_in), w_out[0]], after=[cs_all])
    win_t = g_in.reshape(INC, D)
    wout = g_out.reshape(D, D)

    ffn_shards = _cast_bf16([tr(w_gate), tr(w_up), w_down[0]], name="cast_ffn", after=[g_in])
    h_ffn, tok = _exchange_start(ffn_shards, scatter=False, name="wffn_start")
    mod6 = mod6 + tok[0, 0]

    def ffn_weights(after):
        own, lands = _exchange_wait(h_ffn, [after], scatter=False, name="wffn_wait")
        return [lax.dynamic_update_slice(l, o[None], (me, 0, 0)).reshape(DFF, D) for o, l in zip(own, lands)]

    rs, outs = {}, {}

    def after_first_grads(bf):
        rs["first"], tok_first = _exchange_start([b.reshape(NDEV, b.shape[0] // NDEV, D) for b in bf], scatter=True,
                                                 name="rs_first_start")
        return tok_first[0, 0]

    def small_path(pack, dsgw):
        rs["small"], _ = _exchange_start([pack, dsgw.reshape(G * C, C)], scatter=False, name="small_start")
        return [rs["small"][2], rs["small"][3]]

    gx, grads, pack, dsgw = _local_step(
        x[0], loss_target[0], ctx[0], mod6, cmod6, norm1, norm2[0:1], norm_f.reshape(1, D), win_t, wout, ffn_weights,
        sg_w[0], sg_b[0], sg_gain, ret_logit_f, ret_logit_b,
        tm_a=tm, tm_b=tm, tm_f=min(256, t_len), tm_m=tm, tm_r=tm, tm_i=min(256, t_len), bt_w=min(1024, t_len),
        after_first_grads=after_first_grads, small_path=small_path)

    me_arr = me.reshape(1).astype(jnp.int32)
    h_in, tok_in = _exchange_start([grads["w_in"][1].reshape(NDEV, INC // NDEV, D)], scatter=True, name="rs_in_start")
    (pack_own, sgw_own), (pack_land, sgw_land) = _exchange_wait(rs["small"], [h_in[2]], scatter=False,
                                                               name="small_wait")
    psum_rows, pall, sgw_sum, part = _small_sum(me_arr, pack_own, pack_land, sgw_own, sgw_land, w_mod[0])
    h_cc, _ = _exchange_start([part], scatter=False, name="cctx_start")

    dmod_rows = (ROW_I + 0, ROW_I + 1, ROW_G1, ROW_F + 2, ROW_F + 3, ROW_F + 1)
    dmod_all = jnp.concatenate([pall[:, r, :] for r in dmod_rows], axis=1)
    dcmod = jnp.concatenate([psum_rows[ROW_C + 0:ROW_C + 1], psum_rows[ROW_C + 1:ROW_C + 2],
                             jnp.zeros((1, 4 * D), F32)], axis=1)
    dmod_mat = jnp.concatenate([dmod_all, jnp.pad(dcmod, ((0, 7), (0, 0)))], axis=0)
    ncol = 6 * D // NDEV
    dmod_cols = lax.dynamic_slice(dmod_mat, (0, me * ncol), (16, ncol))
    g_wm, d_wm, m_wm, v_wm = _wmod_grad(cs_all, dmod_cols, w_mod[0], m_w_mod[0], v_w_mod[0])
    outs["w_mod"] = (g_wm[None], d_wm[None], m_wm[None], v_wm[None])
    small_params = {
        "norm1": (norm1, m_norm1, v_norm1), "norm2": (norm2, m_norm2, v_norm2),
        "norm_f": tuple(a.reshape(1, D) for a in (norm_f, m_norm_f, v_norm_f)),
        "sg_gain": (sg_gain, m_sg_gain, v_sg_gain), "sg_b": (sg_b[0], m_sg_b[0], v_sg_b[0]),
        "ret_logit_f": (ret_logit_f, m_ret_logit_f, v_ret_logit_f),
        "ret_logit_b": (ret_logit_b, m_ret_logit_b, v_ret_logit_b),
        "b_mod": (b_mod, m_b_mod, v_b_mod), "sg_w": (sg_w[0], m_sg_w[0], v_sg_w[0])}
    small, loss = _adam_small(psum_rows, sgw_sum.reshape(G, C, C), small_params)
    back = {"norm_f": lambda a: a.reshape(D), "sg_b": lambda a: a[None], "sg_w": lambda a: a[None]}
    for name, vals in small.items():
        outs[name] = tuple(back.get(name, lambda a: a)(a) for a in vals)

    _, lands_first = _exchange_wait(rs["first"], [g_wm], scatter=True, name="rs_first_wait")

    def finish(name, land, w, m, v, transposed):
        rows = land.shape[1]
        own = lax.dynamic_slice(grads[name][0], (me * rows, 0), (rows, D))
        take = tr if transposed else (lambda a: a[0])
        res = _rs_adam(me_arr, own, land, take(w), take(m), take(v), name="adam_" + name)
        put = (lambda a: jnp.transpose(a)[None]) if transposed else (lambda a: a[None])
        outs[name] = tuple(put(a) for a in res)
        return res[0]

    g_down = finish("w_down", lands_first[0], w_down, m_w_down, v_w_down, False)
    g_gate = finish("w_gate", lands_first[1], w_gate, m_w_gate, v_w_gate, True)
    g_up = finish("w_up", lands_first[2], w_up, m_w_up, v_w_up, True)
    g_out = finish("w_out", lands_first[3], w_out, m_w_out, v_w_out, False)
    (part_own,), (part_land,) = _exchange_wait(h_cc, [g_down, g_gate, g_up, g_out], scatter=False, name="cctx_wait")
    res_cc = _cctx_final(me_arr, part_own, part_land, cctx_row, m_c_ctx.reshape(1, D), v_c_ctx.reshape(1, D))
    outs["c_ctx"] = tuple(a.reshape(D) for a in res_cc)
    _, lands_in = _exchange_wait(h_in, [g_down, g_gate, g_up, g_out], scatter=True, name="rs_in_wait")
    finish("w_in", lands_in[0], w_in, m_w_in, v_w_in, True)

    order = ["c_ctx", "w_mod", "b_mod", "norm1", "w_in", "sg_gain", "sg_w", "sg_b", "ret_logit_f", "ret_logit_b",
             "w_out", "norm2", "w_gate", "w_up", "w_down", "norm_f"]
    res = [loss.reshape(()), gx[None]]
    for j in range(4):
        res += [outs[name][j] for name in order]
    return tuple(res)
```

```python
import functools
import math

import numpy as np
import jax
import jax.numpy as jnp
from jax import lax
from jax.experimental import pallas as pl
from jax.experimental.pallas import tpu as pltpu

F32 = jnp.float32
BF16 = jnp.bfloat16

D = 1024
AW = 512
RW = 512
H = 4
DH = 128
G = 4
C = 128
CR = 256
DFF = 2816
INC = 3584
KV_LO, KV_HI = 1536, 2560
NDEV = 8
EPS = 1e-6
KSCALE = DH ** -0.5
ROPE_BASE = 10000.0
GRID_W = 64

ADAM_LR = 0.001
ADAM_B1 = 0.9
ADAM_B2 = 0.999
ADAM_EPS = 1e-08
ADAM_WD = 0.01
ADAM_STEP = 10

VMEM_LIMIT = 56 * 1024 * 1024
MESH = pl.DeviceIdType.MESH


def _cparams(n_grid=0, **kw):
    sem = ("arbitrary",) * n_grid if n_grid else None
    return pltpu.CompilerParams(dimension_semantics=sem, vmem_limit_bytes=VMEM_LIMIT, **kw)


def _dot(a, b):
    return jnp.dot(a, b, preferred_element_type=F32)


def _dot_nt(a, b):
    return lax.dot_general(a, b, (((1,), (1,)), ((), ())), preferred_element_type=F32)


def _dot_tn(a, b):
    return lax.dot_general(a, b, (((0,), (0,)), ((), ())), preferred_element_type=F32)


def _whole(shape):
    nd = len(shape)
    return pl.BlockSpec(shape, lambda *_: (0,) * nd, pipeline_mode=pl.Buffered(1))


def _acc(shape):
    nd = len(shape)
    return pl.BlockSpec(shape, lambda *_: (0,) * nd)


def _rows(tm, n):
    return pl.BlockSpec((tm, n), lambda i: (i, 0))


def _rows_rev(tm, n, nt):
    return pl.BlockSpec((tm, n), lambda i: (nt - 1 - i, 0))


def _cols(n, tm):
    return pl.BlockSpec((n, tm), lambda i: (0, i))


def _sigmoid(x):
    return 1.0 / (1.0 + jnp.exp(-x))


def _log_sigmoid(x):
    return jnp.minimum(x, 0.0) - jnp.log(1.0 + jnp.exp(-jnp.abs(x)))


_GK0 = math.sqrt(2.0 / math.pi)
_GK1 = 0.044715


def _gelu(x):
    x2 = x * x
    t = jnp.tanh(_GK0 * x * (1.0 + _GK1 * x2))
    g = 0.5 * x * (1.0 + t)
    dg = 0.5 * (1.0 + t) + 0.5 * x * (1.0 - t * t) * (_GK0 * (1.0 + 3.0 * _GK1 * x2))
    return g, dg


def _silu_parts(a):
    s = _sigmoid(a)
    return a * s, s * (1.0 + a * (1.0 - s))


def _rope(t, cos, sins):
    n = t.shape[1]
    lane = lax.broadcasted_iota(jnp.int32, t.shape, 1)
    first = (lane & 63) < 32
    partner = jnp.where(first, pltpu.roll(t, n - 32, 1), pltpu.roll(t, 32, 1))
    return t * cos + partner * sins


def _headnorm(o):
    outs, rs = [], []
    for h in range(H):
        oh = o[:, h * DH:(h + 1) * DH]
        r = lax.rsqrt(jnp.mean(oh * oh, axis=-1, keepdims=True) + EPS)
        outs.append(oh * r)
        rs.append(r)
    return jnp.concatenate(outs, axis=1), rs


def _decay_consts(lg, forward):
    ii = lax.broadcasted_iota(jnp.int32, (CR, CR), 0).astype(F32)
    jj = lax.broadcasted_iota(jnp.int32, (CR, CR), 1).astype(F32)
    ic = lax.broadcasted_iota(jnp.int32, (CR, 1), 0).astype(F32)
    if forward:
        diff = ii - jj
        xi = jnp.exp(lg * (ic + 1.0))
        z = jnp.exp(lg * (CR - 1.0 - ic))
    else:
        diff = jj - ii
        xi = jnp.exp(lg * (CR - ic))
        z = jnp.exp(lg * ic)
    dm = jnp.where(diff >= 0.0, jnp.exp(lg * jnp.maximum(diff, 0.0)), 0.0)
    dec = jnp.exp(lg * float(CR))
    return dm, xi, z, dec, ic


def _ctx_fwd(ctx, cmod6, norm1, win, rlf, rlb):
    lc = ctx.shape[0]

    def body(ctx_ref, cmod_ref, n1_ref, win_ref, rlf_ref, rlb_ref, hc_ref, kvc_ref, scf_ref, scb_ref):
        x = ctx_ref[...]
        r = lax.rsqrt(jnp.mean(x * x, axis=-1, keepdims=True) + EPS)
        hc = (x * r) * (n1_ref[...] * (1.0 + cmod_ref[1:2, :])) + cmod_ref[0:1, :]
        hcb = hc.astype(BF16)
        hc_ref[...] = hcb
        kv = _dot_nt(hcb, win_ref[KV_LO:KV_HI, :])
        k = kv[:, :RW] * KSCALE
        v = kv[:, RW:]
        kvc_ref[:, :RW] = k
        kvc_ref[:, RW:] = v
        t = lax.broadcasted_iota(jnp.int32, (lc, 1), 0).astype(F32)
        lgf = _log_sigmoid(rlf_ref[...])
        lgb = _log_sigmoid(rlb_ref[...])
        for h in range(H):
            hs = slice(h * DH, (h + 1) * DH)
            wf = jnp.exp(lgf[h:h + 1, 0:1] * (lc - 1.0 - t))
            wb = jnp.exp(lgb[h:h + 1, 0:1] * t)
            vh = v[:, hs].astype(BF16)
            scf_ref[h] = _dot_tn((k[:, hs] * wf).astype(BF16), vh)
            scb_ref[h] = _dot_tn((k[:, hs] * wb).astype(BF16), vh)

    return pl.pallas_call(
        body, name="ctx_fwd",
        out_shape=(jax.ShapeDtypeStruct((lc, D), BF16), jax.ShapeDtypeStruct((lc, 2 * RW), F32),
                   jax.ShapeDtypeStruct((H, DH, DH), F32), jax.ShapeDtypeStruct((H, DH, DH), F32)),
        compiler_params=_cparams(),
    )(ctx, cmod6, norm1, win, rlf, rlb)


def _sg_forward(u, v, sgw_ref, sgbt_ref, sgg_ref, nc):
    ua, dgu = _gelu(u)
    va, dgv = _gelu(v)
    gain = sgg_ref[...]
    mixed, vn_b, vah_l, rv_l = [], [], [], []
    for g in range(G):
        gs = slice(g * DH, (g + 1) * DH)
        vag = va[:, gs]
        rv = lax.rsqrt(jnp.mean(vag * vag, axis=-1, keepdims=True) + EPS)
        vah = vag * rv
        vn = (vah * gain[:, gs]).astype(BF16)
        wg = sgw_ref[g].astype(BF16)
        bcol = sgbt_ref[g]
        rows = [_dot(wg, vn[c * C:(c + 1) * C, :]) + bcol for c in range(nc)]
        mixed.append(jnp.concatenate(rows, axis=0) if nc > 1 else rows[0])
        vn_b.append(vn)
        vah_l.append(vah)
        rv_l.append(rv)
    mixed = jnp.concatenate(mixed, axis=1)
    return ua * mixed, (ua, dgu, dgv, mixed, vn_b, vah_l, rv_l)


def _fwd_a(x, mod6, norm1, win, sgw, sgbt, sggain, cos, sin, rlf, scf, *, tm):
    t_len = x.shape[0]
    nt, nc, ncr = t_len // tm, tm // C, tm // CR

    def body(x_ref, mod_ref, n1_ref, win_ref, sgw_ref, sgbt_ref, sgg_ref, cos_ref, sin_ref, rlf_ref, scf_ref,
             hx_ref, zuv_ref, q_ref, k_ref, v_ref, gfb_ref, of_ref, ya_ref, sst_ref, s_scr):
        i = pl.program_id(0)

        @pl.when(i == 0)
        def _():
            s_scr[...] = scf_ref[...]

        x = x_ref[...]
        r = lax.rsqrt(jnp.mean(x * x, axis=-1, keepdims=True) + EPS)
        hx = (x * r) * (n1_ref[...] * (1.0 + mod_ref[1:2, :])) + mod_ref[0:1, :]
        hxb = hx.astype(BF16)
        hx_ref[...] = hxb
        z = _dot_nt(hxb, win_ref[...])
        zuv_ref[...] = z[:, :2 * AW]
        gfb_ref[...] = z[:, 2560:3584].astype(BF16)
        cos = jnp.tile(cos_ref[...], (1, H))
        sins = jnp.tile(sin_ref[...], (1, H))
        q_ref[...] = _rope(z[:, 1024:1536], cos, sins).astype(BF16)
        k_ref[...] = (_rope(z[:, 1536:2048], cos, sins) * KSCALE).astype(BF16)
        v_ref[...] = z[:, 2048:2560].astype(BF16)
        ya, _ = _sg_forward(z[:, :AW], z[:, AW:2 * AW], sgw_ref, sgbt_ref, sgg_ref, nc)
        ya_ref[...] = ya.astype(BF16)

        lg = _log_sigmoid(rlf_ref[...])
        for h in range(H):
            dm, xi, zc, dec, _ = _decay_consts(lg[h:h + 1, 0:1], True)
            hs = slice(h * DH, (h + 1) * DH)
            for c in range(ncr):
                rs = slice(c * CR, (c + 1) * CR)
                qc, kc, vc = q_ref[rs, hs], k_ref[rs, hs], v_ref[rs, hs]
                s = s_scr[h]
                sst_ref[c, h] = s
                a = (_dot_nt(qc, kc) * dm).astype(BF16)
                of_ref[rs, hs] = (_dot(a, vc) + xi * _dot(qc, s.astype(BF16))).astype(BF16)
                kz = (kc.astype(F32) * zc).astype(BF16)
                s_scr[h] = dec * s + _dot_tn(kz, vc)

    n_chunks = t_len // CR
    return pl.pallas_call(
        body, name="fwd_a", grid=(nt,),
        in_specs=[_rows(tm, D), _whole((6, D)), _whole((1, D)), _whole((INC, D)), _whole((G, C, C)),
                  _whole((G, C, 128)), _whole((1, AW)), _rows(tm, DH), _rows(tm, DH), _whole((H, 128)),
                  _whole((H, DH, DH))],
        out_specs=[_rows(tm, D), _rows(tm, 2 * AW), _rows(tm, RW), _rows(tm, RW), _rows(tm, RW),
                   _rows(tm, 2 * RW), _rows(tm, RW), _rows(tm, AW),
                   pl.BlockSpec((ncr, H, DH, DH), lambda i: (i, 0, 0, 0))],
        out_shape=(jax.ShapeDtypeStruct((t_len, D), BF16), jax.ShapeDtypeStruct((t_len, 2 * AW), F32),
                   jax.ShapeDtypeStruct((t_len, RW), BF16), jax.ShapeDtypeStruct((t_len, RW), BF16),
                   jax.ShapeDtypeStruct((t_len, RW), BF16), jax.ShapeDtypeStruct((t_len, 2 * RW), BF16),
                   jax.ShapeDtypeStruct((t_len, RW), BF16), jax.ShapeDtypeStruct((t_len, AW), BF16),
                   jax.ShapeDtypeStruct((n_chunks, H, DH, DH), F32)),
        scratch_shapes=[pltpu.VMEM((H, DH, DH), F32)],
        compiler_params=_cparams(1),
    )(x, mod6, norm1, win, sgw, sgbt, sggain, cos, sin, rlf, scf)


def _fwd_b(q, k, v, of, gfb, ya, x, mod6, wout, rlb, scb, *, tm):
    t_len = x.shape[0]
    nt, nc = t_len // tm, tm // CR

    def body(q_ref, k_ref, v_ref, of_ref, gfb_ref, ya_ref, x_ref, mod_ref, wout_ref, rlb_ref, scb_ref,
             ob_ref, ycat_ref, y_ref, x1_ref, rst_ref, r_scr):
        i = pl.program_id(0)

        @pl.when(i == 0)
        def _():
            r_scr[...] = scb_ref[...]

        lg = _log_sigmoid(rlb_ref[...])
        for h in range(H):
            dm, xi, zc, dec, _ = _decay_consts(lg[h:h + 1, 0:1], False)
            hs = slice(h * DH, (h + 1) * DH)
            for c in reversed(range(nc)):
                rs = slice(c * CR, (c + 1) * CR)
                qc, kc, vc = q_ref[rs, hs], k_ref[rs, hs], v_ref[rs, hs]
                s = r_scr[h]
                rst_ref[c, h] = s
                a = (_dot_nt(qc, kc) * dm).astype(BF16)
                ob_ref[rs, hs] = (_dot(a, vc) + xi * _dot(qc, s.astype(BF16))).astype(BF16)
                kz = (kc.astype(F32) * zc).astype(BF16)
                r_scr[h] = dec * s + _dot_tn(kz, vc)

        gfb = gfb_ref[...].astype(F32)
        sf, _ = _silu_parts(gfb[:, :RW])
        sb, _ = _silu_parts(gfb[:, RW:])
        hnf, _ = _headnorm(of_ref[...].astype(F32))
        hnb, _ = _headnorm(ob_ref[...].astype(F32))
        yr = sf * hnf + sb * hnb
        ycat = jnp.concatenate([ya_ref[...], yr.astype(BF16)], axis=1)
        ycat_ref[...] = ycat.T
        y = _dot(ycat, wout_ref[...])
        y_ref[...] = y.astype(BF16)
        x1_ref[...] = x_ref[...] + mod_ref[2:3, :] * y

    n_chunks = t_len // CR
    rr = functools.partial(_rows_rev, nt=nt)
    return pl.pallas_call(
        body, name="fwd_b", grid=(nt,),
        in_specs=[rr(tm, RW), rr(tm, RW), rr(tm, RW), rr(tm, RW), rr(tm, 2 * RW), rr(tm, AW), rr(tm, D),
                  _whole((6, D)), _whole((D, D)), _whole((H, 128)), _whole((H, DH, DH))],
        out_specs=[rr(tm, RW), pl.BlockSpec((D, tm), lambda i: (0, nt - 1 - i)), rr(tm, D), rr(tm, D),
                   pl.BlockSpec((nc, H, DH, DH), lambda i: (nt - 1 - i, 0, 0, 0))],
        out_shape=(jax.ShapeDtypeStruct((t_len, RW), BF16), jax.ShapeDtypeStruct((D, t_len), BF16),
                   jax.ShapeDtypeStruct((t_len, D), BF16), jax.ShapeDtypeStruct((t_len, D), F32),
                   jax.ShapeDtypeStruct((n_chunks, H, DH, DH), F32)),
        scratch_shapes=[pltpu.VMEM((H, DH, DH), F32)],
        compiler_params=_cparams(1),
    )(q, k, v, of, gfb, ya, x, mod6, wout, rlb, scb)


def _ffn(x1, tgt, mod6, norm2, normf, wg, wu, wd, *, tm):
    t_len = x1.shape[0]
    nt = t_len // tm

    def body(x1_ref, tgt_ref, mod_ref, n2_ref, nf_ref, wg_ref, wu_ref, wd_ref,
             dx1_ref, h2_ref, da_ref, db_ref, hm_ref, df_ref, small_ref):
        i = pl.program_id(0)

        @pl.when(i == 0)
        def _():
            small_ref[...] = jnp.zeros_like(small_ref)

        sh2, sc2, g2 = mod_ref[3:4, :], mod_ref[4:5, :], mod_ref[5:6, :]
        n2, nf = n2_ref[...], nf_ref[...]
        x1 = x1_ref[...]
        r2 = lax.rsqrt(jnp.mean(x1 * x1, axis=-1, keepdims=True) + EPS)
        x1h = x1 * r2
        w2 = n2 * (1.0 + sc2)
        h2b = (x1h * w2 + sh2).astype(BF16)
        h2_ref[...] = h2b
        a = _dot_nt(h2b, wg_ref[...])
        b = _dot_nt(h2b, wu_ref[...])
        sa, dsa = _silu_parts(a)
        hmb = (sa * b).astype(BF16)
        hm_ref[...] = hmb.T
        f = _dot(hmb, wd_ref[...])
        x2 = x1 + g2 * f
        r3 = lax.rsqrt(jnp.mean(x2 * x2, axis=-1, keepdims=True) + EPS)
        x2h = x2 * r3
        diff = x2h * nf - tgt_ref[...]
        small_ref[5:6, :] += jnp.sum(diff * diff, axis=0, keepdims=True)
        dout = diff * (1.0 / D)
        small_ref[0:1, :] += jnp.sum(dout * x2h, axis=0, keepdims=True)
        dyg = dout * nf
        dx2 = r3 * (dyg - x2h * jnp.mean(dyg * x2h, axis=-1, keepdims=True))
        small_ref[1:2, :] += jnp.sum(dx2 * f, axis=0, keepdims=True)
        dfb = (dx2 * g2).astype(BF16)
        df_ref[...] = dfb
        dhm = _dot_nt(dfb, wd_ref[...])
        dbb = (dhm * sa).astype(BF16)
        dab = (dhm * b * dsa).astype(BF16)
        da_ref[...] = dab.T
        db_ref[...] = dbb.T
        dh2 = _dot(dab, wg_ref[...]) + _dot(dbb, wu_ref[...])
        small_ref[2:3, :] += jnp.sum(dh2, axis=0, keepdims=True)
        dhx = dh2 * x1h
        small_ref[3:4, :] += jnp.sum(dhx, axis=0, keepdims=True) * n2
        small_ref[4:5, :] += jnp.sum(dhx, axis=0, keepdims=True) * (1.0 + sc2)
        dyg2 = dh2 * w2
        dx1_ref[...] = dx2 + r2 * (dyg2 - x1h * jnp.mean(dyg2 * x1h, axis=-1, keepdims=True))

    return pl.pallas_call(
        body, name="ffn", grid=(nt,),
        in_specs=[_rows(tm, D), _rows(tm, D), _whole((6, D)), _whole((1, D)), _whole((1, D)),
                  _whole((DFF, D)), _whole((DFF, D)), _whole((DFF, D))],
        out_specs=[_rows(tm, D), _rows(tm, D), _cols(DFF, tm), _cols(DFF, tm), _cols(DFF, tm), _rows(tm, D),
                   _acc((8, D))],
        out_shape=(jax.ShapeDtypeStruct((t_len, D), F32), jax.ShapeDtypeStruct((t_len, D), BF16),
                   jax.ShapeDtypeStruct((DFF, t_len), BF16), jax.ShapeDtypeStruct((DFF, t_len), BF16),
                   jax.ShapeDtypeStruct((DFF, t_len), BF16), jax.ShapeDtypeStruct((t_len, D), BF16),
                   jax.ShapeDtypeStruct((8, D), F32)),
        compiler_params=_cparams(1),
    )(x1, tgt, mod6, norm2, normf, wg, wu, wd)


def _mid_bwd(dx1, y, of, ob, gfb, mod6, wout, *, tm):
    t_len = dx1.shape[0]
    nt = t_len // tm

    def body(dx1_ref, y_ref, of_ref, ob_ref, gfb_ref, mod_ref, wout_ref,
             dy_ref, dya_ref, dgfb_ref, dof_ref, dob_ref, dg1_ref):
        i = pl.program_id(0)

        @pl.when(i == 0)
        def _():
            dg1_ref[...] = jnp.zeros_like(dg1_ref)

        dx1 = dx1_ref[...]
        dg1_ref[0:1, :] += jnp.sum(dx1 * y_ref[...].astype(F32), axis=0, keepdims=True)
        dyb = (dx1 * mod_ref[2:3, :]).astype(BF16)
        dy_ref[...] = dyb
        dycat = _dot_nt(dyb, wout_ref[...])
        dya_ref[...] = dycat[:, :AW]
        dyr = dycat[:, AW:]
        gfb = gfb_ref[...].astype(F32)
        for o_ref, do_ref, lo in ((of_ref, dof_ref, 0), (ob_ref, dob_ref, RW)):
            sg, dsg = _silu_parts(gfb[:, lo:lo + RW])
            o = o_ref[...].astype(F32)
            hn, rs = _headnorm(o)
            dgfb_ref[:, lo:lo + RW] = (dyr * hn * dsg).astype(BF16)
            dhn = dyr * sg
            for h in range(H):
                hs = slice(h * DH, (h + 1) * DH)
                oh, dh = hn[:, hs], dhn[:, hs]
                do_ref[:, hs] = (rs[h] * (dh - oh * jnp.mean(dh * oh, axis=-1, keepdims=True))).astype(BF16)

    return pl.pallas_call(
        body, name="mid_bwd", grid=(nt,),
        in_specs=[_rows(tm, D), _rows(tm, D), _rows(tm, RW), _rows(tm, RW), _rows(tm, 2 * RW),
                  _whole((6, D)), _whole((D, D))],
        out_specs=[_rows(tm, D), _rows(tm, AW), _rows(tm, 2 * RW), _rows(tm, RW), _rows(tm, RW), _acc((8, D))],
        out_shape=(jax.ShapeDtypeStruct((t_len, D), BF16), jax.ShapeDtypeStruct((t_len, AW), F32),
                   jax.ShapeDtypeStruct((t_len, 2 * RW), BF16), jax.ShapeDtypeStruct((t_len, RW), BF16),
                   jax.ShapeDtypeStruct((t_len, RW), BF16), jax.ShapeDtypeStruct((8, D), F32)),
        compiler_params=_cparams(1),
    )(dx1, y, of, ob, gfb, mod6, wout)


def _ret_bwd_dir(q_ref, k_ref, v_ref, do_ref, st_ref, dq_ref, dk_ref, dv_ref, ds_scr, dlg_scr, lg, forward, nc, row0):
    order = reversed(range(nc)) if forward else range(nc)
    order = list(order)
    for h in range(H):
        dm, xi, zc, dec, ic = _decay_consts(lg[h:h + 1, 0:1], forward)
        hs = slice(h * DH, (h + 1) * DH)
        if forward:
            w_qi, w_qc, w_ki, w_ks = ic, ic + 1.0, -ic, (CR - 1.0) - ic
        else:
            w_qi, w_qc, w_ki, w_ks = -ic, CR - ic, ic, ic
        acc = jnp.zeros((1, DH), F32)
        for c in order:
            rs = slice(c * CR, (c + 1) * CR)
            qc, kc, vc, doc = q_ref[rs, hs], k_ref[rs, hs], v_ref[rs, hs], do_ref[rs, hs]
            s = st_ref[c, h]
            dsn = ds_scr[h]
            sb, dsnb = s.astype(BF16), dsn.astype(BF16)
            qf, kf = qc.astype(F32), kc.astype(F32)
            a = (_dot_nt(qc, kc) * dm).astype(BF16)
            da = (_dot_nt(doc, vc) * dm).astype(BF16)
            xdo = (xi * doc.astype(F32)).astype(BF16)
            kz = (kf * zc).astype(BF16)
            vz = (vc.astype(F32) * zc).astype(BF16)
            dq_i = _dot(da, kc)
            dq_c = _dot_nt(xdo, sb)
            dk_i = _dot_tn(da, qc)
            dk_s = _dot_nt(vz, dsnb)
            dq_ref[rs, hs] = dq_i + dq_c
            dk_ref[rs, hs] = dk_i + dk_s
            dv_ref[rs, hs] = _dot_tn(a, doc) + _dot(kz, dsnb)
            rowterm = qf * (w_qi * dq_i + w_qc * dq_c) + kf * (w_ki * dk_i + w_ks * dk_s)
            acc = acc + jnp.sum(rowterm, axis=0, keepdims=True)
            acc = acc + (float(CR) * dec) * jnp.sum(s * dsn, axis=0, keepdims=True)
            ds_scr[h] = _dot_tn((xi * qf).astype(BF16), doc) + dec * dsn
        dlg_scr[row0 + h:row0 + h + 1, :] += acc


def _ret_bwd(q, k, v, dof, dob, sst, rst, rlf, rlb, *, tm):
    t_len = q.shape[0]
    nt, nc = t_len // tm, tm // CR

    def body(qf_ref, kf_ref, vf_ref, dof_ref, sst_ref, qb_ref, kb_ref, vb_ref, dob_ref, rst_ref, rlf_ref, rlb_ref,
             dqf_ref, dkf_ref, dvf_ref, dqb_ref, dkb_ref, dvb_ref, dscf_ref, dscb_ref, dlg_ref,
             dsf_scr, dsb_scr, dlg_scr):
        i = pl.program_id(0)

        @pl.when(i == 0)
        def _():
            dsf_scr[...] = jnp.zeros_like(dsf_scr)
            dsb_scr[...] = jnp.zeros_like(dsb_scr)
            dlg_scr[...] = jnp.zeros_like(dlg_scr)

        lgf = _log_sigmoid(rlf_ref[...])
        lgb = _log_sigmoid(rlb_ref[...])
        _ret_bwd_dir(qf_ref, kf_ref, vf_ref, dof_ref, sst_ref, dqf_ref, dkf_ref, dvf_ref, dsf_scr, dlg_scr, lgf, True, nc, 0)
        _ret_bwd_dir(qb_ref, kb_ref, vb_ref, dob_ref, rst_ref, dqb_ref, dkb_ref, dvb_ref, dsb_scr, dlg_scr, lgb, False, nc, H)

        @pl.when(i == nt - 1)
        def _():
            dscf_ref[...] = dsf_scr[...]
            dscb_ref[...] = dsb_scr[...]
            tot = jnp.broadcast_to(jnp.sum(dlg_scr[...], axis=1, keepdims=True), (8, 128))
            ri = lax.broadcasted_iota(jnp.int32, (8, 128), 0)
            li = lax.broadcasted_iota(jnp.int32, (8, 128), 1)
            dlg_ref[...] = jnp.zeros_like(dlg_ref)
            dlg_ref[0:1, 0:128] = jnp.sum(jnp.where(ri == li, tot, 0.0), axis=0, keepdims=True)
            dlg_ref[1:2, 0:128] = jnp.sum(jnp.where(ri - H == li, tot, 0.0), axis=0, keepdims=True)

    rr = functools.partial(_rows_rev, nt=nt)
    st_f = pl.BlockSpec((nc, H, DH, DH), lambda i: (nt - 1 - i, 0, 0, 0))
    st_b = pl.BlockSpec((nc, H, DH, DH), lambda i: (i, 0, 0, 0))
    tok = jax.ShapeDtypeStruct((t_len, RW), F32)
    st = jax.ShapeDtypeStruct((H, DH, DH), F32)
    return pl.pallas_call(
        body, name="ret_bwd", grid=(nt,),
        in_specs=[rr(tm, RW), rr(tm, RW), rr(tm, RW), rr(tm, RW), st_f,
                  _rows(tm, RW), _rows(tm, RW), _rows(tm, RW), _rows(tm, RW), st_b,
                  _whole((H, 128)), _whole((H, 128))],
        out_specs=[rr(tm, RW), rr(tm, RW), rr(tm, RW), _rows(tm, RW), _rows(tm, RW), _rows(tm, RW),
                   _acc((H, DH, DH)), _acc((H, DH, DH)), _acc((8, D))],
        out_shape=(tok, tok, tok, tok, tok, tok, st, st, jax.ShapeDtypeStruct((8, D), F32)),
        scratch_shapes=[pltpu.VMEM((H, DH, DH), F32), pltpu.VMEM((H, DH, DH), F32), pltpu.VMEM((8, 128), F32)],
        compiler_params=_cparams(1),
    )(q, k, v, dof, sst, q, k, v, dob, rst, rlf, rlb)


def _in_bwd(x, zuv, dya, dqf, dkf, dvf, dqb, dkb, dvb, dgfb, dx1, cos, sin, mod6, norm1, win, sgw, sgbt, sggain, *, tm):
    t_len = x.shape[0]
    nt, nc = t_len // tm, tm // C

    def body(x_ref, zuv_ref, dya_ref, dqf_ref, dkf_ref, dvf_ref, dqb_ref, dkb_ref, dvb_ref, dgfb_ref, dx1_ref,
             cos_ref, sin_ref, mod_ref, n1_ref, win_ref, sgw_ref, sgbt_ref, sgg_ref,
             gx_ref, dzt_ref, small_ref, dsgw_ref, dsgbt_ref, dz_ref):
        i = pl.program_id(0)

        @pl.when(i == 0)
        def _():
            small_ref[...] = jnp.zeros_like(small_ref)
            dsgw_ref[...] = jnp.zeros_like(dsgw_ref)
            dsgbt_ref[...] = jnp.zeros_like(dsgbt_ref)

        zuv = zuv_ref[...]
        _, (ua, dgu, dgv, mixed, vn_b, vah_l, rv_l) = _sg_forward(zuv[:, :AW], zuv[:, AW:], sgw_ref, sgbt_ref, sgg_ref, nc)
        dya = dya_ref[...]
        dz_ref[:, 0:AW] = (dya * mixed * dgu).astype(BF16)
        dmixed = dya * ua
        gain = sgg_ref[...]
        for g in range(G):
            gs = slice(g * DH, (g + 1) * DH)
            dmg = dmixed[:, gs]
            dmb = dmg.astype(BF16)
            wgt = sgw_ref[g].astype(BF16)
            dsw = jnp.zeros((C, C), F32)
            dsb = jnp.zeros((C, DH), F32)
            rows = []
            for c in range(nc):
                rs = slice(c * C, (c + 1) * C)
                dsw = dsw + _dot_nt(dmb[rs, :], vn_b[g][rs, :])
                dsb = dsb + dmg[rs, :]
                rows.append(_dot_tn(wgt, dmb[rs, :]))
            dsgw_ref[g] += dsw
            dsgbt_ref[g] += dsb
            dvn = jnp.concatenate(rows, axis=0) if nc > 1 else rows[0]
            vah, rv = vah_l[g], rv_l[g]
            small_ref[3:4, gs] += jnp.sum(dvn * vah, axis=0, keepdims=True)
            dyg = dvn * gain[:, gs]
            dva = rv * (dyg - vah * jnp.mean(dyg * vah, axis=-1, keepdims=True))
            dz_ref[:, AW + g * DH:AW + (g + 1) * DH] = (dva * dgv[:, gs]).astype(BF16)

        cos = jnp.tile(cos_ref[...], (1, H))
        nsins = -jnp.tile(sin_ref[...], (1, H))
        dz_ref[:, 1024:1536] = _rope(dqf_ref[...] + dqb_ref[...], cos, nsins).astype(BF16)
        dz_ref[:, 1536:2048] = (_rope(dkf_ref[...] + dkb_ref[...], cos, nsins) * KSCALE).astype(BF16)
        dz_ref[:, 2048:2560] = (dvf_ref[...] + dvb_ref[...]).astype(BF16)
        dz_ref[:, 2560:3584] = dgfb_ref[...]

        dz = dz_ref[...]
        dzt_ref[...] = dz.T
        dhx = _dot(dz, win_ref[...])
        x = x_ref[...]
        r = lax.rsqrt(jnp.mean(x * x, axis=-1, keepdims=True) + EPS)
        xh = x * r
        n1, sc1 = n1_ref[...], mod_ref[1:2, :]
        small_ref[0:1, :] += jnp.sum(dhx, axis=0, keepdims=True)
        dhxx = jnp.sum(dhx * xh, axis=0, keepdims=True)
        small_ref[1:2, :] += dhxx * n1
        small_ref[2:3, :] += dhxx * (1.0 + sc1)
        dyg = dhx * (n1 * (1.0 + sc1))
        gx_ref[...] = dx1_ref[...] + r * (dyg - xh * jnp.mean(dyg * xh, axis=-1, keepdims=True))

        @pl.when(i == nt - 1)
        def _():
            ri = lax.broadcasted_iota(jnp.int32, (C, DH), 0)
            li = lax.broadcasted_iota(jnp.int32, (C, DH), 1)
            for g in range(G):
                tot = jnp.broadcast_to(jnp.sum(dsgbt_ref[g], axis=1, keepdims=True), (C, DH))
                small_ref[4 + g:5 + g, 0:DH] = jnp.sum(jnp.where(ri == li, tot, 0.0), axis=0, keepdims=True)

    tok = functools.partial(_rows, tm)
    return pl.pallas_call(
        body, name="in_bwd", grid=(nt,),
        in_specs=[tok(D), tok(2 * AW), tok(AW), tok(RW), tok(RW), tok(RW), tok(RW), tok(RW), tok(RW),
                  tok(2 * RW), tok(D), tok(DH), tok(DH), _whole((6, D)), _whole((1, D)), _whole((INC, D)),
                  _whole((G, C, C)), _whole((G, C, 128)), _whole((1, AW))],
        out_specs=[tok(D), _cols(INC, tm), _acc((8, D)), _acc((G, C, C))],
        out_shape=(jax.ShapeDtypeStruct((t_len, D), F32), jax.ShapeDtypeStruct((INC, t_len), BF16),
                   jax.ShapeDtypeStruct((8, D), F32), jax.ShapeDtypeStruct((G, C, C), F32)),
        scratch_shapes=[pltpu.VMEM((G, C, 128), F32), pltpu.VMEM((tm, INC), BF16)],
        compiler_params=_cparams(1),
    )(x, zuv, dya, dqf, dkf, dvf, dqb, dkb, dvb, dgfb, dx1, cos, sin, mod6, norm1, win, sgw, sgbt, sggain)


def _tn_matmul(at, b, *, bm, bt, name, init=None, init_rows=None, after=()):
    m, t_len = at.shape
    n = b.shape[1]
    ni, ntt = m // bm, t_len // bt
    has_init = init is not None

    def body(*refs):
        o_ref, ob_ref = refs[-2:]
        a_ref, b_ref = refs[:2]
        init_ref = refs[2] if has_init else None
        i, t = pl.program_id(0), pl.program_id(1)

        @pl.when(t == 0)
        def _():
            o_ref[...] = jnp.zeros_like(o_ref)

        if has_init:
            for ib in range(ni):
                lo, hi = max(init_rows[0], ib * bm), min(init_rows[1], (ib + 1) * bm)
                if lo < hi:
                    @pl.when(jnp.logical_and(t == 0, i == ib))
                    def _(lo=lo, hi=hi, ib=ib):
                        o_ref[lo - ib * bm:hi - ib * bm, :] = init_ref[lo - init_rows[0]:hi - init_rows[0], :]

        o_ref[...] += _dot(a_ref[...], b_ref[...])

        @pl.when(t == ntt - 1)
        def _():
            ob_ref[...] = o_ref[...].astype(BF16)

    in_specs = [pl.BlockSpec((bm, bt), lambda i, t: (i, t)), pl.BlockSpec((bt, n), lambda i, t: (t, 0))]
    args = [at, b]
    if has_init:
        in_specs.append(pl.BlockSpec(init.shape, lambda i, t: (0, 0), pipeline_mode=pl.Buffered(1)))
        args.append(init)
    for arr in after:
        in_specs.append(pl.BlockSpec(memory_space=pl.ANY))
        args.append(arr)
    out_spec = pl.BlockSpec((bm, n), lambda i, t: (i, 0))
    return pl.pallas_call(
        body, name=name, grid=(ni, ntt),
        in_specs=in_specs,
        out_specs=[out_spec, out_spec],
        out_shape=(jax.ShapeDtypeStruct((m, n), F32), jax.ShapeDtypeStruct((m, n), BF16)),
        compiler_params=_cparams(2),
    )(*args)


def _ctx_bwd(ctx, hc, kvc, cmod6, norm1, win, rlf, rlb, dscf, dscb, dlg_in):
    lc = ctx.shape[0]

    def body(ctx_ref, hc_ref, kvc_ref, cmod_ref, n1_ref, win_ref, rlf_ref, rlb_ref, dscf_ref, dscb_ref, dlgin_ref,
             dwin_ref, small_ref, dlg_ref):
        k = kvc_ref[:, :RW]
        v = kvc_ref[:, RW:]
        t = lax.broadcasted_iota(jnp.int32, (lc, 1), 0).astype(F32)
        lgf = _log_sigmoid(rlf_ref[...])
        lgb = _log_sigmoid(rlb_ref[...])
        dks, dvs = [], []
        lane = lax.broadcasted_iota(jnp.int32, (1, 128), 1)
        row_f = jnp.zeros((1, 128), F32)
        row_b = jnp.zeros((1, 128), F32)
        for h in range(H):
            hs = slice(h * DH, (h + 1) * DH)
            wf = jnp.exp(lgf[h:h + 1, 0:1] * (lc - 1.0 - t))
            wb = jnp.exp(lgb[h:h + 1, 0:1] * t)
            kh, vhb = k[:, hs], v[:, hs].astype(BF16)
            dsf, dsb = dscf_ref[h].astype(BF16), dscb_ref[h].astype(BF16)
            dkf = wf * _dot_nt(vhb, dsf)
            dkb = wb * _dot_nt(vhb, dsb)
            dvs.append(_dot((kh * wf).astype(BF16), dsf) + _dot((kh * wb).astype(BF16), dsb))
            dks.append((dkf + dkb) * KSCALE)
            lf = jnp.sum((lc - 1.0 - t) * kh * dkf, axis=0, keepdims=True)
            lb = jnp.sum(t * kh * dkb, axis=0, keepdims=True)
            row_f = row_f + jnp.where(lane == h, jnp.sum(lf, axis=1, keepdims=True), 0.0)
            row_b = row_b + jnp.where(lane == h, jnp.sum(lb, axis=1, keepdims=True), 0.0)
        dlg_ref[...] = dlgin_ref[...]
        dlg_ref[0:1, 0:128] += row_f
        dlg_ref[1:2, 0:128] += row_b
        dkv = jnp.concatenate(dks + dvs, axis=1).astype(BF16)
        dhc = _dot(dkv, win_ref[KV_LO:KV_HI, :])
        dwin_ref[...] = _dot_tn(dkv, hc_ref[...])
        x = ctx_ref[...]
        r = lax.rsqrt(jnp.mean(x * x, axis=-1, keepdims=True) + EPS)
        xh = x * r
        small_ref[...] = jnp.zeros_like(small_ref)
        small_ref[0:1, :] = jnp.sum(dhc, axis=0, keepdims=True)
        dhxx = jnp.sum(dhc * xh, axis=0, keepdims=True)
        small_ref[1:2, :] = dhxx * n1_ref[...]
        small_ref[2:3, :] = dhxx * (1.0 + cmod_ref[1:2, :])

    return pl.pallas_call(
        body, name="ctx_bwd",
        out_shape=(jax.ShapeDtypeStruct((2 * RW, D), F32), jax.ShapeDtypeStruct((8, D), F32),
                   jax.ShapeDtypeStruct((8, D), F32)),
        compiler_params=_cparams(),
    )(ctx, hc, kvc, cmod6, norm1, win, rlf, rlb, dscf, dscb, dlg_in)


def _adam_math(w, g, m, v):
    m = ADAM_B1 * m + (1.0 - ADAM_B1) * g
    v = ADAM_B2 * v + (1.0 - ADAM_B2) * (g * g)
    m_hat = m / (1.0 - ADAM_B1 ** ADAM_STEP)
    v_hat = v / (1.0 - ADAM_B2 ** ADAM_STEP)
    delta = -ADAM_LR * (m_hat / (jnp.sqrt(v_hat) + ADAM_EPS) + ADAM_WD * w)
    return delta, m, v


def _place():
    x, y, c = lax.axis_index("x"), lax.axis_index("y"), lax.axis_index("c")
    return x, y, c, 4 * x + 2 * y + c


def _flip(x, y, c, k):
    px = 1 - x if (k >> 2) & 1 else x
    py = 1 - y if (k >> 1) & 1 else y
    pc = 1 - c if k & 1 else c
    return (px, py, pc), 4 * px + 2 * py + pc


def _gather_copy(buf_ref, send_sems, recv_sems, sem0, k, mine):
    x, y, c, me = _place()
    dev, idx = _flip(x, y, c, k)
    blk = me if mine else idx
    return pltpu.make_async_remote_copy(
        src_ref=buf_ref.at[blk], dst_ref=buf_ref.at[blk],
        send_sem=send_sems.at[sem0 + k - 1], recv_sem=recv_sems.at[sem0 + k - 1],
        device_id=dev, device_id_type=MESH)


def _entry_gather(c_row, cctx_row, wmod, bmod, shards):
    n = len(shards)
    ncol = wmod.shape[1]

    def body(*refs):
        c_ref, cctx_ref, wmod_ref, bmod_ref = refs[:4]
        in_refs = refs[4:4 + n]
        mod_ref, cs_ref = refs[4 + n:6 + n]
        out_refs = refs[6 + n:6 + 2 * n]
        cbuf, pbuf = refs[6 + 2 * n:8 + 2 * n]
        cast_refs = refs[8 + 2 * n:8 + 3 * n]
        small_send, small_recv, send_sems, recv_sems, local_sems = refs[8 + 3 * n:]
        x, y, c, me = _place()
        sib = (x, y, 1 - c)
        chips = [(1 - x, y), (x, 1 - y), (1 - x, 1 - y)]

        cbuf[me] = jnp.broadcast_to(c_ref[...], (8, D))
        small = [_gather_copy(cbuf, small_send, small_recv, 0, k, True) for k in range(1, NDEV)]
        for cp in small:
            cp.start()

        def blk(px, py, pc):
            return 4 * px + 2 * py + pc

        def copy(w, k, block, to, src=None):
            dst = out_refs[w].at[block]
            return pltpu.make_async_remote_copy(
                src_ref=dst if src is None else src, dst_ref=dst,
                send_sem=send_sems.at[w, k], recv_sem=recv_sems.at[w, k], device_id=to, device_id_type=MESH)

        first, passed, mine = [], [], []
        for w in range(n):
            cast_refs[w][...] = in_refs[w][...].astype(BF16)
            m = pltpu.make_async_copy(cast_refs[w], out_refs[w].at[me], local_sems.at[w])
            m.start()
            mine.append(m)
            cps = [copy(w, 0, me, sib, src=cast_refs[w])]
            cps += [copy(w, 1 + j, me, (*chip, c), src=cast_refs[w]) for j, chip in enumerate(chips)]
            for cp in cps:
                cp.start()
            first += cps

        for k in range(1, NDEV):
            _gather_copy(cbuf, small_send, small_recv, 0, k, False).wait_recv()
        row = lax.broadcasted_iota(jnp.int32, (16, D), 0)
        call = jnp.where(row == 8, jnp.broadcast_to(cctx_ref[...], (16, D)), 0.0)
        for d in range(NDEV):
            call = jnp.where(row == d, jnp.concatenate([cbuf[d], cbuf[d]], axis=0), call)
        cs = call * _sigmoid(call)
        cs_ref[...] = cs
        pbuf[me] = _dot(cs.astype(BF16), wmod_ref[...].astype(BF16))
        small2 = [_gather_copy(pbuf, small_send, small_recv, NDEV - 1, k, True) for k in range(1, NDEV)]
        for cp in small2:
            cp.start()

        for w in range(n):
            for j, chip in enumerate(chips):
                b = blk(*chip, c)
                copy(w, 1 + j, b, (x, y, c)).wait_recv()
                fw = copy(w, 4 + j, b, sib)
                fw.start()
                passed.append(fw)
        for w in range(n):
            copy(w, 0, blk(x, y, 1 - c), (x, y, c)).wait_recv()
            for j, chip in enumerate(chips):
                copy(w, 4 + j, blk(*chip, 1 - c), (x, y, c)).wait_recv()

        for k in range(1, NDEV):
            _gather_copy(pbuf, small_send, small_recv, NDEV - 1, k, False).wait_recv()
        for d in range(NDEV):
            mod_ref[:, d * ncol:(d + 1) * ncol] = pbuf[d] + bmod_ref[:, d * ncol:(d + 1) * ncol]
        for cp in small + small2 + first + passed:
            cp.wait_send()
        for m in mine:
            m.wait()

    vm = pl.BlockSpec(memory_space=pltpu.VMEM)
    hbm = pl.BlockSpec(memory_space=pltpu.HBM)
    return pl.pallas_call(
        body, name="entry_gather",
        in_specs=[vm] * (4 + n), out_specs=[vm, vm] + [hbm] * n,
        out_shape=(jax.ShapeDtypeStruct((16, 6 * D), F32), jax.ShapeDtypeStruct((16, D), F32))
        + tuple(jax.ShapeDtypeStruct((NDEV,) + s.shape, BF16) for s in shards),
        scratch_shapes=[pltpu.VMEM((NDEV, 8, D), F32), pltpu.VMEM((NDEV, 16, ncol), F32)]
        + [pltpu.VMEM(s.shape, BF16) for s in shards]
        + [pltpu.SemaphoreType.DMA((2 * (NDEV - 1),)), pltpu.SemaphoreType.DMA((2 * (NDEV - 1),)),
           pltpu.SemaphoreType.DMA((n, 7)), pltpu.SemaphoreType.DMA((n, 7)), pltpu.SemaphoreType.DMA((n,))],
        compiler_params=_cparams(),
    )(c_row, cctx_row, wmod, bmod, *shards)


_HBM = pl.BlockSpec(memory_space=pltpu.HBM)
_SEMS = pl.BlockSpec(memory_space=pltpu.SEMAPHORE)
_EFFECT = pltpu.SideEffectType.DATAFLOW_SIDE_EFFECTING


def _exchange_copy(src_refs, land_refs, send_sems, recv_sems, w, k, scatter, landing_slot_is_mine):
    x, y, c, me = _place()
    dev, pidx = _flip(x, y, c, k)
    src = src_refs[w].at[pidx] if scatter else src_refs[w]
    slot = me if landing_slot_is_mine else pidx
    return pltpu.make_async_remote_copy(
        src_ref=src, dst_ref=land_refs[w].at[slot],
        send_sem=send_sems.at[w * (NDEV - 1) + k - 1], recv_sem=recv_sems.at[w * (NDEV - 1) + k - 1],
        device_id=dev, device_id_type=MESH)


def _exchange_start(srcs, *, scatter, name):
    n = len(srcs)
    lands = [lax.empty((NDEV,) + tuple(s.shape[-2:]), s.dtype) for s in srcs]

    def body(*refs):
        src_refs, land_refs = refs[:n], refs[n:2 * n]
        send_sems, recv_sems = refs[2 * n], refs[2 * n + 1]
        token = refs[-1]
        for w in range(n):
            for k in range(1, NDEV):
                _exchange_copy(src_refs, land_refs, send_sems, recv_sems, w, k, scatter, True).start()
        token[...] = jnp.zeros_like(token)

    arrs = list(srcs) + lands
    out_shape = ([pltpu.SemaphoreType.DMA((n * (NDEV - 1),)), pltpu.SemaphoreType.DMA((n * (NDEV - 1),))]
                 + [pltpu.HBM(a.shape, a.dtype) for a in arrs] + [jax.ShapeDtypeStruct((8, 128), F32)])
    res = pl.pallas_call(
        body, name=name, out_shape=tuple(out_shape),
        in_specs=[_HBM] * (2 * n),
        out_specs=tuple([_SEMS, _SEMS] + [_HBM] * (2 * n) + [pl.BlockSpec(memory_space=pltpu.VMEM)]),
        input_output_aliases={i: 2 + i for i in range(2 * n)},
        compiler_params=pltpu.CompilerParams(has_side_effects=_EFFECT),
    )(*[pltpu.with_memory_space_constraint(a, pltpu.HBM) for a in arrs])
    return res[:-1], res[-1]


def _exchange_wait(handles, after, *, scatter, name):
    n = (len(handles) - 2) // 2
    na = len(after)

    def body(*refs):
        src_refs, land_refs = refs[:n], refs[n:2 * n]
        send_sems, recv_sems = refs[2 * n], refs[2 * n + 1]
        for w in range(n):
            for k in range(1, NDEV):
                cp = _exchange_copy(src_refs, land_refs, send_sems, recv_sems, w, k, scatter, False)
                cp.wait_send()
                cp.wait_recv()

    arrs = handles[2:]
    res = pl.pallas_call(
        body, name=name, out_shape=tuple(pltpu.HBM(a.shape, a.dtype) for a in arrs),
        in_specs=[_HBM] * (2 * n) + [_SEMS, _SEMS] + [_HBM] * na,
        out_specs=tuple([_HBM] * (2 * n)),
        input_output_aliases={i: i for i in range(2 * n)},
        compiler_params=pltpu.CompilerParams(has_side_effects=_EFFECT),
    )(*arrs, handles[0], handles[1], *[pltpu.with_memory_space_constraint(a, pltpu.HBM) for a in after])
    return res[:n], res[n:]


def _cast_bf16(arrs, *, name, after=()):
    n = len(arrs)

    def body(*refs):
        for i_ref, o_ref in zip(refs[:n], refs[n + len(after):]):
            o_ref[...] = i_ref[...].astype(BF16)

    return pl.pallas_call(
        body, name=name, out_shape=tuple(jax.ShapeDtypeStruct(a.shape, BF16) for a in arrs),
        in_specs=[pl.BlockSpec(memory_space=pltpu.VMEM)] * n + [pl.BlockSpec(memory_space=pl.ANY)] * len(after),
        compiler_params=_cparams())(*arrs, *after)


def _rs_adam(me_arr, full, land, w, m, v, *, name):
    rows = land.shape[1]

    def body(me_ref, own_ref, land_ref, w_ref, m_ref, v_ref, g_ref, d_ref, mo_ref, vo_ref):
        me = me_ref[0]
        acc = jnp.zeros(own_ref.shape, F32)
        for p in range(NDEV):
            acc = acc + jnp.where(p == me, own_ref[...], land_ref[p].astype(F32))
        g_ref[...] = acc
        d_ref[...], mo_ref[...], vo_ref[...] = _adam_math(w_ref[...], acc, m_ref[...], v_ref[...])

    sh = jax.ShapeDtypeStruct((rows, D), F32)
    whole2 = pl.BlockSpec((rows, D), lambda i, me: (0, 0))
    grid_spec = pltpu.PrefetchScalarGridSpec(
        num_scalar_prefetch=1, grid=(1,),
        in_specs=[pl.BlockSpec((rows, D), lambda i, me: (me[0], 0)),
                  pl.BlockSpec((NDEV, rows, D), lambda i, me: (0, 0, 0)), whole2, whole2, whole2],
        out_specs=[whole2] * 4)
    return pl.pallas_call(
        body, name=name, out_shape=(sh, sh, sh, sh), grid_spec=grid_spec, compiler_params=_cparams(1),
    )(me_arr, full, land, w, m, v)


ROW_F, ROW_I, ROW_C, ROW_G1, ROW_LG = 0, 8, 16, 24, 32
PACK_ROWS = 40


def _small_sum(me_arr, pack, pack_land, sgw, sgw_land, wmod):
    ncol = wmod.shape[1]

    def body(me_ref, pack_ref, pland_ref, sgw_ref, sland_ref, wmod_ref, sum_ref, all_ref, sgw_sum_ref, part_ref):
        me = me_ref[0]
        acc = jnp.zeros((PACK_ROWS, D), F32)
        acc_w = jnp.zeros(sgw_ref.shape, F32)
        for p in range(NDEV):
            rows = jnp.where(p == me, pack_ref[...], pland_ref[p])
            all_ref[p] = rows
            acc = acc + rows
            acc_w = acc_w + jnp.where(p == me, sgw_ref[...], sland_ref[p])
        sum_ref[...] = acc
        sgw_sum_ref[...] = acc_w
        zero = jnp.zeros((1, D), F32)
        dcmod = jnp.concatenate([acc[ROW_C:ROW_C + 1], acc[ROW_C + 1:ROW_C + 2], zero, zero, zero, zero], axis=1)
        mine = jnp.zeros((1, ncol), F32)
        for d in range(NDEV):
            mine = mine + jnp.where(d == me, dcmod[:, d * ncol:(d + 1) * ncol], 0.0)
        part_ref[...] = _dot_nt(jnp.broadcast_to(mine, (8, ncol)).astype(BF16), wmod_ref[...].astype(BF16))

    vm = pl.BlockSpec(memory_space=pltpu.VMEM)
    return pl.pallas_call(
        body, name="small_sum",
        out_shape=(jax.ShapeDtypeStruct((PACK_ROWS, D), F32), jax.ShapeDtypeStruct((NDEV, PACK_ROWS, D), F32),
                   jax.ShapeDtypeStruct(sgw.shape, F32), jax.ShapeDtypeStruct((8, D), F32)),
        in_specs=[pl.BlockSpec(memory_space=pltpu.SMEM)] + [vm] * 5,
        compiler_params=_cparams(),
    )(me_arr, pack, pack_land, sgw, sgw_land, wmod)


def _cctx_final(me_arr, part, part_land, cctx_row, m_row, v_row):
    def body(me_ref, part_ref, land_ref, c_ref, m_ref, v_ref, g_ref, d_ref, mo_ref, vo_ref):
        me = me_ref[0]
        tot = jnp.zeros((8, D), F32)
        for p in range(NDEV):
            tot = tot + jnp.where(p == me, part_ref[...], land_ref[p])
        cc = c_ref[...]
        _, dsilu = _silu_parts(cc)
        g = tot[0:1, :] * dsilu
        g_ref[...] = g
        d_ref[...], mo_ref[...], vo_ref[...] = _adam_math(cc, g, m_ref[...], v_ref[...])

    row = jax.ShapeDtypeStruct((1, D), F32)
    vm = pl.BlockSpec(memory_space=pltpu.VMEM)
    return pl.pallas_call(
        body, name="cctx_final", out_shape=(row, row, row, row),
        in_specs=[pl.BlockSpec(memory_space=pltpu.SMEM)] + [vm] * 5,
        compiler_params=_cparams(),
    )(me_arr, part, part_land, cctx_row, m_row, v_row)


def _adam_small(s, sgw_g, params):
    names = ["norm1", "norm2", "norm_f", "sg_gain", "sg_b", "ret_logit_f", "ret_logit_b", "b_mod", "sg_w"]
    flat = [a for n in names for a in params[n]]

    def body(*refs):
        s_ref, sgw_ref = refs[0], refs[1]
        p_refs = refs[2:2 + 3 * len(names)]
        o_refs = refs[2 + 3 * len(names):]
        loss_ref = o_refs[-1]

        def row(r):
            return s_ref[r:r + 1, :]

        grads = {
            "norm1": row(ROW_I + 2) + row(ROW_C + 2),
            "norm2": row(ROW_F + 4),
            "norm_f": row(ROW_F + 0),
            "sg_gain": s_ref[ROW_I + 3:ROW_I + 4, 0:AW],
            "sg_b": s_ref[ROW_I + 4:ROW_I + 8, 0:DH],
            "sg_w": sgw_ref[...],
        }
        for j, n in enumerate(names):
            w_ref, m_ref, v_ref = p_refs[3 * j:3 * j + 3]
            g_ref, d_ref, mo_ref, vo_ref = o_refs[4 * j:4 * j + 4]
            w = w_ref[...]
            if n in ("ret_logit_f", "ret_logit_b"):
                r = ROW_LG + (0 if n == "ret_logit_f" else 1)
                g = s_ref[r:r + 1, 0:H] * _sigmoid(-w)
            elif n == "b_mod":
                rows = [row(ROW_I + 0) + row(ROW_C + 0), row(ROW_I + 1) + row(ROW_C + 1), row(ROW_G1),
                        row(ROW_F + 2), row(ROW_F + 3), row(ROW_F + 1)]
                g = jnp.concatenate(rows, axis=1)
            else:
                g = grads[n]
            g_ref[...] = g
            d_ref[...], mo_ref[...], vo_ref[...] = _adam_math(w, g, m_ref[...], v_ref[...])
        loss_ref[...] = jnp.full((1, 1), 0.5 / D, F32) * jnp.sum(row(ROW_F + 5), axis=1, keepdims=True)

    out_shape = []
    for n in names:
        w = params[n][0]
        out_shape += [jax.ShapeDtypeStruct(w.shape, F32)] * 4
    out_shape.append(jax.ShapeDtypeStruct((1, 1), F32))
    outs = pl.pallas_call(body, name="adam_small", out_shape=tuple(out_shape), compiler_params=_cparams())(
        s, sgw_g, *flat)
    return {n: tuple(outs[4 * j:4 * j + 4]) for j, n in enumerate(names)}, outs[-1]


def _wmod_grad(cs_all, dmod_cols, wmod, m, v):
    ncol = wmod.shape[1]

    def body(cs_ref, dm_ref, w_ref, m_ref, v_ref, g_ref, d_ref, mo_ref, vo_ref):
        g = _dot_tn(cs_ref[...].astype(BF16), dm_ref[...].astype(BF16))
        g_ref[...] = g
        d_ref[...], mo_ref[...], vo_ref[...] = _adam_math(w_ref[...], g, m_ref[...], v_ref[...])

    sh = jax.ShapeDtypeStruct((D, ncol), F32)
    return pl.pallas_call(
        body, name="wmod_grad", out_shape=(sh, sh, sh, sh),
        compiler_params=_cparams(),
    )(cs_all, dmod_cols, wmod, m, v)


def _rope_tables(t_len):
    n_freq = DH // 4
    inv = (ROPE_BASE ** (-np.arange(n_freq, dtype=np.float32) / n_freq)).astype(np.float32)
    tok = np.arange(t_len)
    rows = (tok // GRID_W).astype(np.float32)
    cols = (tok % GRID_W).astype(np.float32)
    ang_r = rows[:, None] * inv[None, :]
    ang_c = cols[:, None] * inv[None, :]
    cos = np.concatenate([np.cos(ang_r)] * 2 + [np.cos(ang_c)] * 2, axis=1).astype(np.float32)
    sin = np.concatenate([-np.sin(ang_r), np.sin(ang_r), -np.sin(ang_c), np.sin(ang_c)], axis=1).astype(np.float32)
    return jnp.asarray(cos), jnp.asarray(sin)


def _local_step(x, tgt, ctx, mod6, cmod6, norm1, norm2, normf, win, wout, ffn_weights, sgw, sgb, sggain, rlf, rlb,
                *, tm_a, tm_b, tm_f, tm_m, tm_r, tm_i, bt_w, after_first_grads=None, small_path=None):
    t_len = x.shape[0]
    cos, sin = _rope_tables(t_len)
    sgbt = jnp.broadcast_to(sgb[:, :, None], (G, C, 128))
    rlf = jnp.broadcast_to(rlf.reshape(H, 1), (H, 128))
    rlb = jnp.broadcast_to(rlb.reshape(H, 1), (H, 128))
    hc, kvc, scf, scb = _ctx_fwd(ctx, cmod6, norm1, win, rlf, rlb)
    hx, zuv, q, k, v, gfb, of, ya, sst = _fwd_a(x, mod6, norm1, win, sgw, sgbt, sggain, cos, sin, rlf, scf, tm=tm_a)
    ob, ycat, y, x1, rst = _fwd_b(q, k, v, of, gfb, ya, x, mod6, wout, rlb, scb, tm=tm_b)
    wg, wu, wd = ffn_weights(x1) if callable(ffn_weights) else ffn_weights
    dx1, h2, da, db, hm, df, small_f = _ffn(x1, tgt, mod6, norm2, normf, wg, wu, wd, tm=tm_f)
    bt2 = min(2 * bt_w, t_len)
    d_wd, d_wd_b = _tn_matmul(hm, df, bm=DFF // 2, bt=bt2, name="dw_down")
    d_wg, d_wg_b = _tn_matmul(da, h2, bm=DFF // 2, bt=bt2, name="dw_gate")
    d_wu, d_wu_b = _tn_matmul(db, h2, bm=DFF // 2, bt=bt2, name="dw_up")
    dy, dya, dgfb, dof, dob, dg1 = _mid_bwd(dx1, y, of, ob, gfb, mod6, wout, tm=tm_m)
    d_wo, d_wo_b = _tn_matmul(ycat, dy, bm=D, bt=bt_w, name="dw_out")
    if after_first_grads is not None:
        rlf = rlf + after_first_grads((d_wd_b, d_wg_b, d_wu_b, d_wo_b))
    dqf, dkf, dvf, dqb, dkb, dvb, dscf, dscb, dlg = _ret_bwd(q, k, v, dof, dob, sst, rst, rlf, rlb, tm=tm_r)
    gx, dz, small_i, dsgw = _in_bwd(x, zuv, dya, dqf, dkf, dvf, dqb, dkb, dvb, dgfb, dx1, cos, sin,
                                    mod6, norm1, win, sgw, sgbt, sggain, tm=tm_i)
    dwin_c, small_c, dlg = _ctx_bwd(ctx, hc, kvc, cmod6, norm1, win, rlf, rlb, dscf, dscb, dlg)
    pack = jnp.concatenate([small_f, small_i, small_c, dg1, dlg], axis=0)
    after = small_path(pack, dsgw) if small_path is not None else ()
    d_wi, d_wi_b = _tn_matmul(dz, hx, bm=INC // 2, bt=bt_w, name="dw_in", init=dwin_c,
                              init_rows=(KV_LO, KV_HI), after=after)
    grads = {"w_in": (d_wi, d_wi_b), "w_out": (d_wo, d_wo_b), "w_gate": (d_wg, d_wg_b), "w_up": (d_wu, d_wu_b),
             "w_down": (d_wd, d_wd_b)}
    return gx, grads, pack, dsgw


def kernel(x, c, ctx, c_ctx, w_mod, b_mod, norm1, w_in, sg_gain, sg_w, sg_b, ret_logit_f, ret_logit_b, w_out, norm2, w_gate, w_up, w_down, norm_f, loss_target, m_c_ctx, m_w_mod, m_b_mod, m_norm1, m_w_in, m_sg_gain, m_sg_w, m_sg_b, m_ret_logit_f, m_ret_logit_b, m_w_out, m_norm2, m_w_gate, m_w_up, m_w_down, m_norm_f, v_c_ctx, v_w_mod, v_b_mod, v_norm1, v_w_in, v_sg_gain, v_sg_w, v_sg_b, v_ret_logit_f, v_ret_logit_b, v_w_out, v_norm2, v_w_gate, v_w_up, v_w_down, v_norm_f):
    t_len = x.shape[1]
    tm = min(512, t_len)
    me = 4 * lax.axis_index("x") + 2 * lax.axis_index("y") + lax.axis_index("c")
    tr = lambda a: jnp.transpose(a[0])

    cctx_row = c_ctx.reshape(1, D)
    mod_all, cs_all, g_in, g_out = _entry_gather(c, cctx_row, w_mod[0], b_mod, [tr(w_in), w_out[0]])
    mod6 = lax.dynamic_slice(mod_all, (me, 0), (1, 6 * D)).reshape(6, D)
    cmod6 = mod_all[8].reshape(6, D)
    win_t = g_in.reshape(INC, D)
    wout = g_out.reshape(D, D)

    ffn_shards = _cast_bf16([tr(w_gate), tr(w_up), w_down[0]], name="cast_ffn", after=[g_in])
    h_ffn, tok = _exchange_start(ffn_shards, scatter=False, name="wffn_start")
    mod6 = mod6 + tok[0, 0]

    def ffn_weights(after):
        own, lands = _exchange_wait(h_ffn, [after], scatter=False, name="wffn_wait")
        return [lax.dynamic_update_slice(l, o[None], (me, 0, 0)).reshape(DFF, D) for o, l in zip(own, lands)]

    rs, outs = {}, {}

    def after_first_grads(bf):
        rs["first"], tok_first = _exchange_start([b.reshape(NDEV, b.shape[0] // NDEV, D) for b in bf], scatter=True,
                                                 name="rs_first_start")
        return tok_first[0, 0]

    def small_path(pack, dsgw):
        rs["small"], _ = _exchange_start([pack, dsgw.reshape(G * C, C)], scatter=False, name="small_start")
        return [rs["small"][2], rs["small"][3]]

    gx, grads, pack, dsgw = _local_step(
        x[0], loss_target[0], ctx[0], mod6, cmod6, norm1, norm2[0:1], norm_f.reshape(1, D), win_t, wout, ffn_weights,
        sg_w[0], sg_b[0], sg_gain, ret_logit_f, ret_logit_b,
        tm_a=tm, tm_b=tm, tm_f=min(256, t_len), tm_m=tm, tm_r=tm, tm_i=min(256, t_len), bt_w=min(1024, t_len),
        after_first_grads=after_first_grads, small_path=small_path)

    me_arr = me.reshape(1).astype(jnp.int32)
    h_in, tok_in = _exchange_start([grads["w_in"][1].reshape(NDEV, INC // NDEV, D)], scatter=True, name="rs_in_start")
    (pack_own, sgw_own), (pack_land, sgw_land) = _exchange_wait(rs["small"], [h_in[2]], scatter=False,
                                                               name="small_wait")
    psum_rows, pall, sgw_sum, part = _small_sum(me_arr, pack_own, pack_land, sgw_own, sgw_land, w_mod[0])
    h_cc, _ = _exchange_start([part], scatter=False, name="cctx_start")

    dmod_rows = (ROW_I + 0, ROW_I + 1, ROW_G1, ROW_F + 2, ROW_F + 3, ROW_F + 1)
    dmod_all = jnp.concatenate([pall[:, r, :] for r in dmod_rows], axis=1)
    dcmod = jnp.concatenate([psum_rows[ROW_C + 0:ROW_C + 1], psum_rows[ROW_C + 1:ROW_C + 2],
                             jnp.zeros((1, 4 * D), F32)], axis=1)
    dmod_mat = jnp.concatenate([dmod_all, jnp.pad(dcmod, ((0, 7), (0, 0)))], axis=0)
    ncol = 6 * D // NDEV
    dmod_cols = lax.dynamic_slice(dmod_mat, (0, me * ncol), (16, ncol))
    g_wm, d_wm, m_wm, v_wm = _wmod_grad(cs_all, dmod_cols, w_mod[0], m_w_mod[0], v_w_mod[0])
    outs["w_mod"] = (g_wm[None], d_wm[None], m_wm[None], v_wm[None])
    small_params = {
        "norm1": (norm1, m_norm1, v_norm1), "norm2": (norm2, m_norm2, v_norm2),
        "norm_f": tuple(a.reshape(1, D) for a in (norm_f, m_norm_f, v_norm_f)),
        "sg_gain": (sg_gain, m_sg_gain, v_sg_gain), "sg_b": (sg_b[0], m_sg_b[0], v_sg_b[0]),
        "ret_logit_f": (ret_logit_f, m_ret_logit_f, v_ret_logit_f),
        "ret_logit_b": (ret_logit_b, m_ret_logit_b, v_ret_logit_b),
        "b_mod": (b_mod, m_b_mod, v_b_mod), "sg_w": (sg_w[0], m_sg_w[0], v_sg_w[0])}
    small, loss = _adam_small(psum_rows, sgw_sum.reshape(G, C, C), small_params)
    back = {"norm_f": lambda a: a.reshape(D), "sg_b": lambda a: a[None], "sg_w": lambda a: a[None]}
    for name, vals in small.items():
        outs[name] = tuple(back.get(name, lambda a: a)(a) for a in vals)

    _, lands_first = _exchange_wait(rs["first"], [g_wm], scatter=True, name="rs_first_wait")

    def finish(name, land, w, m, v, transposed):
        take = tr if transposed else (lambda a: a[0])
        res = _rs_adam(me_arr, grads[name][0], land, take(w), take(m), take(v), name="adam_" + name)
        put = (lambda a: jnp.transpose(a)[None]) if transposed else (lambda a: a[None])
        outs[name] = tuple(put(a) for a in res)
        return res[0]

    g_down = finish("w_down", lands_first[0], w_down, m_w_down, v_w_down, False)
    g_gate = finish("w_gate", lands_first[1], w_gate, m_w_gate, v_w_gate, True)
    g_up = finish("w_up", lands_first[2], w_up, m_w_up, v_w_up, True)
    g_out = finish("w_out", lands_first[3], w_out, m_w_out, v_w_out, False)
    (part_own,), (part_land,) = _exchange_wait(h_cc, [g_down, g_gate, g_up, g_out], scatter=False, name="cctx_wait")
    res_cc = _cctx_final(me_arr, part_own, part_land, cctx_row, m_c_ctx.reshape(1, D), v_c_ctx.reshape(1, D))
    outs["c_ctx"] = tuple(a.reshape(D) for a in res_cc)
    _, lands_in = _exchange_wait(h_in, [g_down, g_gate, g_up, g_out], scatter=True, name="rs_in_wait")
    finish("w_in", lands_in[0], w_in, m_w_in, v_w_in, True)

    order = ["c_ctx", "w_mod", "b_mod", "norm1", "w_in", "sg_gain", "sg_w", "sg_b", "ret_logit_f", "ret_logit_b",
             "w_out", "norm2", "w_gate", "w_up", "w_down", "norm_f"]
    res = [loss.reshape(()), gx[None]]
    for j in range(4):
        res += [outs[name][j] for name in order]
    return tuple(res)
```

```python
import functools
import math

import numpy as np
import jax
import jax.numpy as jnp
from jax import lax
from jax.experimental import pallas as pl
from jax.experimental.pallas import tpu as pltpu

F32 = jnp.float32
BF16 = jnp.bfloat16

D = 1024
AW = 512
RW = 512
H = 4
DH = 128
G = 4
C = 128
CR = 256
DFF = 2816
INC = 3584
KV_LO, KV_HI = 1536, 2560
NDEV = 8
EPS = 1e-6
KSCALE = DH ** -0.5
ROPE_BASE = 10000.0
GRID_W = 64

ADAM_LR = 0.001
ADAM_B1 = 0.9
ADAM_B2 = 0.999
ADAM_EPS = 1e-08
ADAM_WD = 0.01
ADAM_STEP = 10

VMEM_LIMIT = 56 * 1024 * 1024
MESH = pl.DeviceIdType.MESH


def _cparams(n_grid=0, **kw):
    sem = ("arbitrary",) * n_grid if n_grid else None
    return pltpu.CompilerParams(dimension_semantics=sem, vmem_limit_bytes=VMEM_LIMIT, **kw)


def _dot(a, b):
    return jnp.dot(a, b, preferred_element_type=F32)


def _dot_nt(a, b):
    return lax.dot_general(a, b, (((1,), (1,)), ((), ())), preferred_element_type=F32)


def _dot_tn(a, b):
    return lax.dot_general(a, b, (((0,), (0,)), ((), ())), preferred_element_type=F32)


def _whole(shape):
    nd = len(shape)
    return pl.BlockSpec(shape, lambda *_: (0,) * nd, pipeline_mode=pl.Buffered(1))


def _acc(shape):
    nd = len(shape)
    return pl.BlockSpec(shape, lambda *_: (0,) * nd)


def _rows(tm, n):
    return pl.BlockSpec((tm, n), lambda i: (i, 0))


def _rows_rev(tm, n, nt):
    return pl.BlockSpec((tm, n), lambda i: (nt - 1 - i, 0))


def _cols(n, tm):
    return pl.BlockSpec((n, tm), lambda i: (0, i))


def _sigmoid(x):
    return 1.0 / (1.0 + jnp.exp(-x))


def _log_sigmoid(x):
    return jnp.minimum(x, 0.0) - jnp.log(1.0 + jnp.exp(-jnp.abs(x)))


_GK0 = math.sqrt(2.0 / math.pi)
_GK1 = 0.044715


def _gelu(x):
    x2 = x * x
    t = jnp.tanh(_GK0 * x * (1.0 + _GK1 * x2))
    g = 0.5 * x * (1.0 + t)
    dg = 0.5 * (1.0 + t) + 0.5 * x * (1.0 - t * t) * (_GK0 * (1.0 + 3.0 * _GK1 * x2))
    return g, dg


def _silu_parts(a):
    s = _sigmoid(a)
    return a * s, s * (1.0 + a * (1.0 - s))


def _rope(t, cos, sins):
    n = t.shape[1]
    lane = lax.broadcasted_iota(jnp.int32, t.shape, 1)
    first = (lane & 63) < 32
    partner = jnp.where(first, pltpu.roll(t, n - 32, 1), pltpu.roll(t, 32, 1))
    return t * cos + partner * sins


def _headnorm(o):
    outs, rs = [], []
    for h in range(H):
        oh = o[:, h * DH:(h + 1) * DH]
        r = lax.rsqrt(jnp.mean(oh * oh, axis=-1, keepdims=True) + EPS)
        outs.append(oh * r)
        rs.append(r)
    return jnp.concatenate(outs, axis=1), rs


def _decay_consts(lg, forward):
    ii = lax.broadcasted_iota(jnp.int32, (CR, CR), 0).astype(F32)
    jj = lax.broadcasted_iota(jnp.int32, (CR, CR), 1).astype(F32)
    ic = lax.broadcasted_iota(jnp.int32, (CR, 1), 0).astype(F32)
    if forward:
        diff = ii - jj
        xi = jnp.exp(lg * (ic + 1.0))
        z = jnp.exp(lg * (CR - 1.0 - ic))
    else:
        diff = jj - ii
        xi = jnp.exp(lg * (CR - ic))
        z = jnp.exp(lg * ic)
    dm = jnp.where(diff >= 0.0, jnp.exp(lg * jnp.maximum(diff, 0.0)), 0.0)
    dec = jnp.exp(lg * float(CR))
    return dm, xi, z, dec, ic


def _ctx_fwd(ctx, cmod6, norm1, win, rlf, rlb):
    lc = ctx.shape[0]

    def body(ctx_ref, cmod_ref, n1_ref, win_ref, rlf_ref, rlb_ref, hc_ref, kvc_ref, scf_ref, scb_ref):
        x = ctx_ref[...]
        r = lax.rsqrt(jnp.mean(x * x, axis=-1, keepdims=True) + EPS)
        hc = (x * r) * (n1_ref[...] * (1.0 + cmod_ref[1:2, :])) + cmod_ref[0:1, :]
        hcb = hc.astype(BF16)
        hc_ref[...] = hcb
        kv = _dot_nt(hcb, win_ref[KV_LO:KV_HI, :])
        k = kv[:, :RW] * KSCALE
        v = kv[:, RW:]
        kvc_ref[:, :RW] = k
        kvc_ref[:, RW:] = v
        t = lax.broadcasted_iota(jnp.int32, (lc, 1), 0).astype(F32)
        lgf = _log_sigmoid(rlf_ref[...])
        lgb = _log_sigmoid(rlb_ref[...])
        for h in range(H):
            hs = slice(h * DH, (h + 1) * DH)
            wf = jnp.exp(lgf[h:h + 1, 0:1] * (lc - 1.0 - t))
            wb = jnp.exp(lgb[h:h + 1, 0:1] * t)
            vh = v[:, hs].astype(BF16)
            scf_ref[h] = _dot_tn((k[:, hs] * wf).astype(BF16), vh)
            scb_ref[h] = _dot_tn((k[:, hs] * wb).astype(BF16), vh)

    return pl.pallas_call(
        body, name="ctx_fwd",
        out_shape=(jax.ShapeDtypeStruct((lc, D), BF16), jax.ShapeDtypeStruct((lc, 2 * RW), F32),
                   jax.ShapeDtypeStruct((H, DH, DH), F32), jax.ShapeDtypeStruct((H, DH, DH), F32)),
        compiler_params=_cparams(),
    )(ctx, cmod6, norm1, win, rlf, rlb)


def _sg_forward(u, v, sgw_ref, sgbt_ref, sgg_ref, nc):
    ua, dgu = _gelu(u)
    va, dgv = _gelu(v)
    gain = sgg_ref[...]
    mixed, vn_b, vah_l, rv_l = [], [], [], []
    for g in range(G):
        gs = slice(g * DH, (g + 1) * DH)
        vag = va[:, gs]
        rv = lax.rsqrt(jnp.mean(vag * vag, axis=-1, keepdims=True) + EPS)
        vah = vag * rv
        vn = (vah * gain[:, gs]).astype(BF16)
        wg = sgw_ref[g].astype(BF16)
        bcol = sgbt_ref[g]
        rows = [_dot(wg, vn[c * C:(c + 1) * C, :]) + bcol for c in range(nc)]
        mixed.append(jnp.concatenate(rows, axis=0) if nc > 1 else rows[0])
        vn_b.append(vn)
        vah_l.append(vah)
        rv_l.append(rv)
    mixed = jnp.concatenate(mixed, axis=1)
    return ua * mixed, (ua, dgu, dgv, mixed, vn_b, vah_l, rv_l)


def _fwd_a(x, mod6, norm1, win, sgw, sgbt, sggain, cos, sin, rlf, scf, *, tm):
    t_len = x.shape[0]
    nt, nc, ncr = t_len // tm, tm // C, tm // CR

    def body(x_ref, mod_ref, n1_ref, win_ref, sgw_ref, sgbt_ref, sgg_ref, cos_ref, sin_ref, rlf_ref, scf_ref,
             hx_ref, zuv_ref, q_ref, k_ref, v_ref, gfb_ref, of_ref, ya_ref, sst_ref, s_scr):
        i = pl.program_id(0)

        @pl.when(i == 0)
        def _():
            s_scr[...] = scf_ref[...]

        x = x_ref[...]
        r = lax.rsqrt(jnp.mean(x * x, axis=-1, keepdims=True) + EPS)
        hx = (x * r) * (n1_ref[...] * (1.0 + mod_ref[1:2, :])) + mod_ref[0:1, :]
        hxb = hx.astype(BF16)
        hx_ref[...] = hxb
        z = _dot_nt(hxb, win_ref[...])
        zuv_ref[...] = z[:, :2 * AW]
        gfb_ref[...] = z[:, 2560:3584].astype(BF16)
        cos = jnp.tile(cos_ref[...], (1, H))
        sins = jnp.tile(sin_ref[...], (1, H))
        q_ref[...] = _rope(z[:, 1024:1536], cos, sins).astype(BF16)
        k_ref[...] = (_rope(z[:, 1536:2048], cos, sins) * KSCALE).astype(BF16)
        v_ref[...] = z[:, 2048:2560].astype(BF16)
        ya, _ = _sg_forward(z[:, :AW], z[:, AW:2 * AW], sgw_ref, sgbt_ref, sgg_ref, nc)
        ya_ref[...] = ya.astype(BF16)

        lg = _log_sigmoid(rlf_ref[...])
        for h in range(H):
            dm, xi, zc, dec, _ = _decay_consts(lg[h:h + 1, 0:1], True)
            hs = slice(h * DH, (h + 1) * DH)
            for c in range(ncr):
                rs = slice(c * CR, (c + 1) * CR)
                qc, kc, vc = q_ref[rs, hs], k_ref[rs, hs], v_ref[rs, hs]
                s = s_scr[h]
                sst_ref[c, h] = s
                a = (_dot_nt(qc, kc) * dm).astype(BF16)
                of_ref[rs, hs] = (_dot(a, vc) + xi * _dot(qc, s.astype(BF16))).astype(BF16)
                kz = (kc.astype(F32) * zc).astype(BF16)
                s_scr[h] = dec * s + _dot_tn(kz, vc)

    n_chunks = t_len // CR
    return pl.pallas_call(
        body, name="fwd_a", grid=(nt,),
        in_specs=[_rows(tm, D), _whole((6, D)), _whole((1, D)), _whole((INC, D)), _whole((G, C, C)),
                  _whole((G, C, 128)), _whole((1, AW)), _rows(tm, DH), _rows(tm, DH), _whole((H, 128)),
                  _whole((H, DH, DH))],
        out_specs=[_rows(tm, D), _rows(tm, 2 * AW), _rows(tm, RW), _rows(tm, RW), _rows(tm, RW),
                   _rows(tm, 2 * RW), _rows(tm, RW), _rows(tm, AW),
                   pl.BlockSpec((ncr, H, DH, DH), lambda i: (i, 0, 0, 0))],
        out_shape=(jax.ShapeDtypeStruct((t_len, D), BF16), jax.ShapeDtypeStruct((t_len, 2 * AW), F32),
                   jax.ShapeDtypeStruct((t_len, RW), BF16), jax.ShapeDtypeStruct((t_len, RW), BF16),
                   jax.ShapeDtypeStruct((t_len, RW), BF16), jax.ShapeDtypeStruct((t_len, 2 * RW), BF16),
                   jax.ShapeDtypeStruct((t_len, RW), BF16), jax.ShapeDtypeStruct((t_len, AW), BF16),
                   jax.ShapeDtypeStruct((n_chunks, H, DH, DH), F32)),
        scratch_shapes=[pltpu.VMEM((H, DH, DH), F32)],
        compiler_params=_cparams(1),
    )(x, mod6, norm1, win, sgw, sgbt, sggain, cos, sin, rlf, scf)


def _fwd_b(q, k, v, of, gfb, ya, x, mod6, wout, rlb, scb, *, tm):
    t_len = x.shape[0]
    nt, nc = t_len // tm, tm // CR

    def body(q_ref, k_ref, v_ref, of_ref, gfb_ref, ya_ref, x_ref, mod_ref, wout_ref, rlb_ref, scb_ref,
             ob_ref, ycat_ref, y_ref, x1_ref, rst_ref, r_scr):
        i = pl.program_id(0)

        @pl.when(i == 0)
        def _():
            r_scr[...] = scb_ref[...]

        lg = _log_sigmoid(rlb_ref[...])
        for h in range(H):
            dm, xi, zc, dec, _ = _decay_consts(lg[h:h + 1, 0:1], False)
            hs = slice(h * DH, (h + 1) * DH)
            for c in reversed(range(nc)):
                rs = slice(c * CR, (c + 1) * CR)
                qc, kc, vc = q_ref[rs, hs], k_ref[rs, hs], v_ref[rs, hs]
                s = r_scr[h]
                rst_ref[c, h] = s
                a = (_dot_nt(qc, kc) * dm).astype(BF16)
                ob_ref[rs, hs] = (_dot(a, vc) + xi * _dot(qc, s.astype(BF16))).astype(BF16)
                kz = (kc.astype(F32) * zc).astype(BF16)
                r_scr[h] = dec * s + _dot_tn(kz, vc)

        gfb = gfb_ref[...].astype(F32)
        sf, _ = _silu_parts(gfb[:, :RW])
        sb, _ = _silu_parts(gfb[:, RW:])
        hnf, _ = _headnorm(of_ref[...].astype(F32))
        hnb, _ = _headnorm(ob_ref[...].astype(F32))
        yr = sf * hnf + sb * hnb
        ycat = jnp.concatenate([ya_ref[...], yr.astype(BF16)], axis=1)
        ycat_ref[...] = ycat.T
        y = _dot(ycat, wout_ref[...])
        y_ref[...] = y.astype(BF16)
        x1_ref[...] = x_ref[...] + mod_ref[2:3, :] * y

    n_chunks = t_len // CR
    rr = functools.partial(_rows_rev, nt=nt)
    return pl.pallas_call(
        body, name="fwd_b", grid=(nt,),
        in_specs=[rr(tm, RW), rr(tm, RW), rr(tm, RW), rr(tm, RW), rr(tm, 2 * RW), rr(tm, AW), rr(tm, D),
                  _whole((6, D)), _whole((D, D)), _whole((H, 128)), _whole((H, DH, DH))],
        out_specs=[rr(tm, RW), pl.BlockSpec((D, tm), lambda i: (0, nt - 1 - i)), rr(tm, D), rr(tm, D),
                   pl.BlockSpec((nc, H, DH, DH), lambda i: (nt - 1 - i, 0, 0, 0))],
        out_shape=(jax.ShapeDtypeStruct((t_len, RW), BF16), jax.ShapeDtypeStruct((D, t_len), BF16),
                   jax.ShapeDtypeStruct((t_len, D), BF16), jax.ShapeDtypeStruct((t_len, D), F32),
                   jax.ShapeDtypeStruct((n_chunks, H, DH, DH), F32)),
        scratch_shapes=[pltpu.VMEM((H, DH, DH), F32)],
        compiler_params=_cparams(1),
    )(q, k, v, of, gfb, ya, x, mod6, wout, rlb, scb)


def _ffn(x1, tgt, mod6, norm2, normf, wg, wu, wd, *, tm):
    t_len = x1.shape[0]
    nt = t_len // tm

    def body(x1_ref, tgt_ref, mod_ref, n2_ref, nf_ref, wg_ref, wu_ref, wd_ref,
             dx1_ref, h2_ref, da_ref, db_ref, hm_ref, df_ref, small_ref):
        i = pl.program_id(0)

        @pl.when(i == 0)
        def _():
            small_ref[...] = jnp.zeros_like(small_ref)

        sh2, sc2, g2 = mod_ref[3:4, :], mod_ref[4:5, :], mod_ref[5:6, :]
        n2, nf = n2_ref[...], nf_ref[...]
        x1 = x1_ref[...]
        r2 = lax.rsqrt(jnp.mean(x1 * x1, axis=-1, keepdims=True) + EPS)
        x1h = x1 * r2
        w2 = n2 * (1.0 + sc2)
        h2b = (x1h * w2 + sh2).astype(BF16)
        h2_ref[...] = h2b
        a = _dot_nt(h2b, wg_ref[...])
        b = _dot_nt(h2b, wu_ref[...])
        sa, dsa = _silu_parts(a)
        hmb = (sa * b).astype(BF16)
        hm_ref[...] = hmb.T
        f = _dot(hmb, wd_ref[...])
        x2 = x1 + g2 * f
        r3 = lax.rsqrt(jnp.mean(x2 * x2, axis=-1, keepdims=True) + EPS)
        x2h = x2 * r3
        diff = x2h * nf - tgt_ref[...]
        small_ref[5:6, :] += jnp.sum(diff * diff, axis=0, keepdims=True)
        dout = diff * (1.0 / D)
        small_ref[0:1, :] += jnp.sum(dout * x2h, axis=0, keepdims=True)
        dyg = dout * nf
        dx2 = r3 * (dyg - x2h * jnp.mean(dyg * x2h, axis=-1, keepdims=True))
        small_ref[1:2, :] += jnp.sum(dx2 * f, axis=0, keepdims=True)
        dfb = (dx2 * g2).astype(BF16)
        df_ref[...] = dfb
        dhm = _dot_nt(dfb, wd_ref[...])
        dbb = (dhm * sa).astype(BF16)
        dab = (dhm * b * dsa).astype(BF16)
        da_ref[...] = dab.T
        db_ref[...] = dbb.T
        dh2 = _dot(dab, wg_ref[...]) + _dot(dbb, wu_ref[...])
        small_ref[2:3, :] += jnp.sum(dh2, axis=0, keepdims=True)
        dhx = dh2 * x1h
        small_ref[3:4, :] += jnp.sum(dhx, axis=0, keepdims=True) * n2
        small_ref[4:5, :] += jnp.sum(dhx, axis=0, keepdims=True) * (1.0 + sc2)
        dyg2 = dh2 * w2
        dx1_ref[...] = dx2 + r2 * (dyg2 - x1h * jnp.mean(dyg2 * x1h, axis=-1, keepdims=True))

    return pl.pallas_call(
        body, name="ffn", grid=(nt,),
        in_specs=[_rows(tm, D), _rows(tm, D), _whole((6, D)), _whole((1, D)), _whole((1, D)),
                  _whole((DFF, D)), _whole((DFF, D)), _whole((DFF, D))],
        out_specs=[_rows(tm, D), _rows(tm, D), _cols(DFF, tm), _cols(DFF, tm), _cols(DFF, tm), _rows(tm, D),
                   _acc((8, D))],
        out_shape=(jax.ShapeDtypeStruct((t_len, D), F32), jax.ShapeDtypeStruct((t_len, D), BF16),
                   jax.ShapeDtypeStruct((DFF, t_len), BF16), jax.ShapeDtypeStruct((DFF, t_len), BF16),
                   jax.ShapeDtypeStruct((DFF, t_len), BF16), jax.ShapeDtypeStruct((t_len, D), BF16),
                   jax.ShapeDtypeStruct((8, D), F32)),
        compiler_params=_cparams(1),
    )(x1, tgt, mod6, norm2, normf, wg, wu, wd)


def _mid_bwd(dx1, y, of, ob, gfb, mod6, wout, *, tm):
    t_len = dx1.shape[0]
    nt = t_len // tm

    def body(dx1_ref, y_ref, of_ref, ob_ref, gfb_ref, mod_ref, wout_ref,
             dy_ref, dya_ref, dgfb_ref, dof_ref, dob_ref, dg1_ref):
        i = pl.program_id(0)

        @pl.when(i == 0)
        def _():
            dg1_ref[...] = jnp.zeros_like(dg1_ref)

        dx1 = dx1_ref[...]
        dg1_ref[0:1, :] += jnp.sum(dx1 * y_ref[...].astype(F32), axis=0, keepdims=True)
        dyb = (dx1 * mod_ref[2:3, :]).astype(BF16)
        dy_ref[...] = dyb
        dycat = _dot_nt(dyb, wout_ref[...])
        dya_ref[...] = dycat[:, :AW]
        dyr = dycat[:, AW:]
        gfb = gfb_ref[...].astype(F32)
        for o_ref, do_ref, lo in ((of_ref, dof_ref, 0), (ob_ref, dob_ref, RW)):
            sg, dsg = _silu_parts(gfb[:, lo:lo + RW])
            o = o_ref[...].astype(F32)
            hn, rs = _headnorm(o)
            dgfb_ref[:, lo:lo + RW] = (dyr * hn * dsg).astype(BF16)
            dhn = dyr * sg
            for h in range(H):
                hs = slice(h * DH, (h + 1) * DH)
                oh, dh = hn[:, hs], dhn[:, hs]
                do_ref[:, hs] = (rs[h] * (dh - oh * jnp.mean(dh * oh, axis=-1, keepdims=True))).astype(BF16)

    return pl.pallas_call(
        body, name="mid_bwd", grid=(nt,),
        in_specs=[_rows(tm, D), _rows(tm, D), _rows(tm, RW), _rows(tm, RW), _rows(tm, 2 * RW),
                  _whole((6, D)), _whole((D, D))],
        out_specs=[_rows(tm, D), _rows(tm, AW), _rows(tm, 2 * RW), _rows(tm, RW), _rows(tm, RW), _acc((8, D))],
        out_shape=(jax.ShapeDtypeStruct((t_len, D), BF16), jax.ShapeDtypeStruct((t_len, AW), F32),
                   jax.ShapeDtypeStruct((t_len, 2 * RW), BF16), jax.ShapeDtypeStruct((t_len, RW), BF16),
                   jax.ShapeDtypeStruct((t_len, RW), BF16), jax.ShapeDtypeStruct((8, D), F32)),
        compiler_params=_cparams(1),
    )(dx1, y, of, ob, gfb, mod6, wout)


def _ret_bwd_dir(q_ref, k_ref, v_ref, do_ref, st_ref, dq_ref, dk_ref, dv_ref, ds_scr, dlg_scr, lg, forward, nc, row0):
    order = reversed(range(nc)) if forward else range(nc)
    order = list(order)
    for h in range(H):
        dm, xi, zc, dec, ic = _decay_consts(lg[h:h + 1, 0:1], forward)
        hs = slice(h * DH, (h + 1) * DH)
        if forward:
            w_qi, w_qc, w_ki, w_ks = ic, ic + 1.0, -ic, (CR - 1.0) - ic
        else:
            w_qi, w_qc, w_ki, w_ks = -ic, CR - ic, ic, ic
        acc = jnp.zeros((1, DH), F32)
        for c in order:
            rs = slice(c * CR, (c + 1) * CR)
            qc, kc, vc, doc = q_ref[rs, hs], k_ref[rs, hs], v_ref[rs, hs], do_ref[rs, hs]
            s = st_ref[c, h]
            dsn = ds_scr[h]
            sb, dsnb = s.astype(BF16), dsn.astype(BF16)
            qf, kf = qc.astype(F32), kc.astype(F32)
            a = (_dot_nt(qc, kc) * dm).astype(BF16)
            da = (_dot_nt(doc, vc) * dm).astype(BF16)
            xdo = (xi * doc.astype(F32)).astype(BF16)
            kz = (kf * zc).astype(BF16)
            vz = (vc.astype(F32) * zc).astype(BF16)
            dq_i = _dot(da, kc)
            dq_c = _dot_nt(xdo, sb)
            dk_i = _dot_tn(da, qc)
            dk_s = _dot_nt(vz, dsnb)
            dq_ref[rs, hs] = dq_i + dq_c
            dk_ref[rs, hs] = dk_i + dk_s
            dv_ref[rs, hs] = _dot_tn(a, doc) + _dot(kz, dsnb)
            rowterm = qf * (w_qi * dq_i + w_qc * dq_c) + kf * (w_ki * dk_i + w_ks * dk_s)
            acc = acc + jnp.sum(rowterm, axis=0, keepdims=True)
            acc = acc + (float(CR) * dec) * jnp.sum(s * dsn, axis=0, keepdims=True)
            ds_scr[h] = _dot_tn((xi * qf).astype(BF16), doc) + dec * dsn
        dlg_scr[row0 + h:row0 + h + 1, :] += acc


def _ret_bwd(q, k, v, dof, dob, sst, rst, rlf, rlb, *, tm):
    t_len = q.shape[0]
    nt, nc = t_len // tm, tm // CR

    def body(qf_ref, kf_ref, vf_ref, dof_ref, sst_ref, qb_ref, kb_ref, vb_ref, dob_ref, rst_ref, rlf_ref, rlb_ref,
             dqf_ref, dkf_ref, dvf_ref, dqb_ref, dkb_ref, dvb_ref, dscf_ref, dscb_ref, dlg_ref,
             dsf_scr, dsb_scr, dlg_scr):
        i = pl.program_id(0)

        @pl.when(i == 0)
        def _():
            dsf_scr[...] = jnp.zeros_like(dsf_scr)
            dsb_scr[...] = jnp.zeros_like(dsb_scr)
            dlg_scr[...] = jnp.zeros_like(dlg_scr)

        lgf = _log_sigmoid(rlf_ref[...])
        lgb = _log_sigmoid(rlb_ref[...])
        _ret_bwd_dir(qf_ref, kf_ref, vf_ref, dof_ref, sst_ref, dqf_ref, dkf_ref, dvf_ref, dsf_scr, dlg_scr, lgf, True, nc, 0)
        _ret_bwd_dir(qb_ref, kb_ref, vb_ref, dob_ref, rst_ref, dqb_ref, dkb_ref, dvb_ref, dsb_scr, dlg_scr, lgb, False, nc, H)

        @pl.when(i == nt - 1)
        def _():
            dscf_ref[...] = dsf_scr[...]
            dscb_ref[...] = dsb_scr[...]
            tot = jnp.broadcast_to(jnp.sum(dlg_scr[...], axis=1, keepdims=True), (8, 128))
            ri = lax.broadcasted_iota(jnp.int32, (8, 128), 0)
            li = lax.broadcasted_iota(jnp.int32, (8, 128), 1)
            dlg_ref[...] = jnp.zeros_like(dlg_ref)
            dlg_ref[0:1, 0:128] = jnp.sum(jnp.where(ri == li, tot, 0.0), axis=0, keepdims=True)
            dlg_ref[1:2, 0:128] = jnp.sum(jnp.where(ri - H == li, tot, 0.0), axis=0, keepdims=True)

    rr = functools.partial(_rows_rev, nt=nt)
    st_f = pl.BlockSpec((nc, H, DH, DH), lambda i: (nt - 1 - i, 0, 0, 0))
    st_b = pl.BlockSpec((nc, H, DH, DH), lambda i: (i, 0, 0, 0))
    tok = jax.ShapeDtypeStruct((t_len, RW), F32)
    st = jax.ShapeDtypeStruct((H, DH, DH), F32)
    return pl.pallas_call(
        body, name="ret_bwd", grid=(nt,),
        in_specs=[rr(tm, RW), rr(tm, RW), rr(tm, RW), rr(tm, RW), st_f,
                  _rows(tm, RW), _rows(tm, RW), _rows(tm, RW), _rows(tm, RW), st_b,
                  _whole((H, 128)), _whole((H, 128))],
        out_specs=[rr(tm, RW), rr(tm, RW), rr(tm, RW), _rows(tm, RW), _rows(tm, RW), _rows(tm, RW),
                   _acc((H, DH, DH)), _acc((H, DH, DH)), _acc((8, D))],
        out_shape=(tok, tok, tok, tok, tok, tok, st, st, jax.ShapeDtypeStruct((8, D), F32)),
        scratch_shapes=[pltpu.VMEM((H, DH, DH), F32), pltpu.VMEM((H, DH, DH), F32), pltpu.VMEM((8, 128), F32)],
        compiler_params=_cparams(1),
    )(q, k, v, dof, sst, q, k, v, dob, rst, rlf, rlb)


def _in_bwd(x, zuv, dya, dqf, dkf, dvf, dqb, dkb, dvb, dgfb, dx1, cos, sin, mod6, norm1, win, sgw, sgbt, sggain, *, tm):
    t_len = x.shape[0]
    nt, nc = t_len // tm, tm // C

    def body(x_ref, zuv_ref, dya_ref, dqf_ref, dkf_ref, dvf_ref, dqb_ref, dkb_ref, dvb_ref, dgfb_ref, dx1_ref,
             cos_ref, sin_ref, mod_ref, n1_ref, win_ref, sgw_ref, sgbt_ref, sgg_ref,
             gx_ref, dzt_ref, small_ref, dsgw_ref, dsgbt_ref, dz_ref):
        i = pl.program_id(0)

        @pl.when(i == 0)
        def _():
            small_ref[...] = jnp.zeros_like(small_ref)
            dsgw_ref[...] = jnp.zeros_like(dsgw_ref)
            dsgbt_ref[...] = jnp.zeros_like(dsgbt_ref)

        zuv = zuv_ref[...]
        _, (ua, dgu, dgv, mixed, vn_b, vah_l, rv_l) = _sg_forward(zuv[:, :AW], zuv[:, AW:], sgw_ref, sgbt_ref, sgg_ref, nc)
        dya = dya_ref[...]
        dz_ref[:, 0:AW] = (dya * mixed * dgu).astype(BF16)
        dmixed = dya * ua
        gain = sgg_ref[...]
        for g in range(G):
            gs = slice(g * DH, (g + 1) * DH)
            dmg = dmixed[:, gs]
            dmb = dmg.astype(BF16)
            wgt = sgw_ref[g].astype(BF16)
            dsw = jnp.zeros((C, C), F32)
            dsb = jnp.zeros((C, DH), F32)
            rows = []
            for c in range(nc):
                rs = slice(c * C, (c + 1) * C)
                dsw = dsw + _dot_nt(dmb[rs, :], vn_b[g][rs, :])
                dsb = dsb + dmg[rs, :]
                rows.append(_dot_tn(wgt, dmb[rs, :]))
            dsgw_ref[g] += dsw
            dsgbt_ref[g] += dsb
            dvn = jnp.concatenate(rows, axis=0) if nc > 1 else rows[0]
            vah, rv = vah_l[g], rv_l[g]
            small_ref[3:4, gs] += jnp.sum(dvn * vah, axis=0, keepdims=True)
            dyg = dvn * gain[:, gs]
            dva = rv * (dyg - vah * jnp.mean(dyg * vah, axis=-1, keepdims=True))
            dz_ref[:, AW + g * DH:AW + (g + 1) * DH] = (dva * dgv[:, gs]).astype(BF16)

        cos = jnp.tile(cos_ref[...], (1, H))
        nsins = -jnp.tile(sin_ref[...], (1, H))
        dz_ref[:, 1024:1536] = _rope(dqf_ref[...] + dqb_ref[...], cos, nsins).astype(BF16)
        dz_ref[:, 1536:2048] = (_rope(dkf_ref[...] + dkb_ref[...], cos, nsins) * KSCALE).astype(BF16)
        dz_ref[:, 2048:2560] = (dvf_ref[...] + dvb_ref[...]).astype(BF16)
        dz_ref[:, 2560:3584] = dgfb_ref[...]

        dz = dz_ref[...]
        dzt_ref[...] = dz.T
        dhx = _dot(dz, win_ref[...])
        x = x_ref[...]
        r = lax.rsqrt(jnp.mean(x * x, axis=-1, keepdims=True) + EPS)
        xh = x * r
        n1, sc1 = n1_ref[...], mod_ref[1:2, :]
        small_ref[0:1, :] += jnp.sum(dhx, axis=0, keepdims=True)
        dhxx = jnp.sum(dhx * xh, axis=0, keepdims=True)
        small_ref[1:2, :] += dhxx * n1
        small_ref[2:3, :] += dhxx * (1.0 + sc1)
        dyg = dhx * (n1 * (1.0 + sc1))
        gx_ref[...] = dx1_ref[...] + r * (dyg - xh * jnp.mean(dyg * xh, axis=-1, keepdims=True))

        @pl.when(i == nt - 1)
        def _():
            ri = lax.broadcasted_iota(jnp.int32, (C, DH), 0)
            li = lax.broadcasted_iota(jnp.int32, (C, DH), 1)
            for g in range(G):
                tot = jnp.broadcast_to(jnp.sum(dsgbt_ref[g], axis=1, keepdims=True), (C, DH))
                small_ref[4 + g:5 + g, 0:DH] = jnp.sum(jnp.where(ri == li, tot, 0.0), axis=0, keepdims=True)

    tok = functools.partial(_rows, tm)
    return pl.pallas_call(
        body, name="in_bwd", grid=(nt,),
        in_specs=[tok(D), tok(2 * AW), tok(AW), tok(RW), tok(RW), tok(RW), tok(RW), tok(RW), tok(RW),
                  tok(2 * RW), tok(D), tok(DH), tok(DH), _whole((6, D)), _whole((1, D)), _whole((INC, D)),
                  _whole((G, C, C)), _whole((G, C, 128)), _whole((1, AW))],
        out_specs=[tok(D), _cols(INC, tm), _acc((8, D)), _acc((G, C, C))],
        out_shape=(jax.ShapeDtypeStruct((t_len, D), F32), jax.ShapeDtypeStruct((INC, t_len), BF16),
                   jax.ShapeDtypeStruct((8, D), F32), jax.ShapeDtypeStruct((G, C, C), F32)),
        scratch_shapes=[pltpu.VMEM((G, C, 128), F32), pltpu.VMEM((tm, INC), BF16)],
        compiler_params=_cparams(1),
    )(x, zuv, dya, dqf, dkf, dvf, dqb, dkb, dvb, dgfb, dx1, cos, sin, mod6, norm1, win, sgw, sgbt, sggain)


def _tn_matmul(at, b, *, bm, bt, name, init=None, init_rows=None, after=(), cols=None):
    m, t_len = at.shape
    cj, n = (0, b.shape[1]) if cols is None else cols
    ni, ntt = m // bm, t_len // bt
    has_init = init is not None

    def body(*refs):
        o_ref, ob_ref = refs[-2:]
        a_ref, b_ref = refs[:2]
        init_ref = refs[2] if has_init else None
        i, t = pl.program_id(0), pl.program_id(1)

        @pl.when(t == 0)
        def _():
            o_ref[...] = jnp.zeros_like(o_ref)

        if has_init:
            for ib in range(ni):
                lo, hi = max(init_rows[0], ib * bm), min(init_rows[1], (ib + 1) * bm)
                if lo < hi:
                    @pl.when(jnp.logical_and(t == 0, i == ib))
                    def _(lo=lo, hi=hi, ib=ib):
                        o_ref[lo - ib * bm:hi - ib * bm, :] = init_ref[lo - init_rows[0]:hi - init_rows[0], :]

        o_ref[...] += _dot(a_ref[...], b_ref[...])

        @pl.when(t == ntt - 1)
        def _():
            ob_ref[...] = o_ref[...].astype(BF16)

    in_specs = [pl.BlockSpec((bm, bt), lambda i, t: (i, t)), pl.BlockSpec((bt, n), lambda i, t: (t, cj))]
    args = [at, b]
    if has_init:
        in_specs.append(pl.BlockSpec((init.shape[0], n), lambda i, t: (0, cj), pipeline_mode=pl.Buffered(1)))
        args.append(init)
    for arr in after:
        in_specs.append(pl.BlockSpec(memory_space=pl.ANY))
        args.append(arr)
    out_spec = pl.BlockSpec((bm, n), lambda i, t: (i, 0))
    return pl.pallas_call(
        body, name=name, grid=(ni, ntt),
        in_specs=in_specs,
        out_specs=[out_spec, out_spec],
        out_shape=(jax.ShapeDtypeStruct((m, n), F32), jax.ShapeDtypeStruct((m, n), BF16)),
        compiler_params=_cparams(2),
    )(*args)


def _ctx_bwd(ctx, hc, kvc, cmod6, norm1, win, rlf, rlb, dscf, dscb, dlg_in):
    lc = ctx.shape[0]

    def body(ctx_ref, hc_ref, kvc_ref, cmod_ref, n1_ref, win_ref, rlf_ref, rlb_ref, dscf_ref, dscb_ref, dlgin_ref,
             dwin_ref, small_ref, dlg_ref):
        k = kvc_ref[:, :RW]
        v = kvc_ref[:, RW:]
        t = lax.broadcasted_iota(jnp.int32, (lc, 1), 0).astype(F32)
        lgf = _log_sigmoid(rlf_ref[...])
        lgb = _log_sigmoid(rlb_ref[...])
        dks, dvs = [], []
        lane = lax.broadcasted_iota(jnp.int32, (1, 128), 1)
        row_f = jnp.zeros((1, 128), F32)
        row_b = jnp.zeros((1, 128), F32)
        for h in range(H):
            hs = slice(h * DH, (h + 1) * DH)
            wf = jnp.exp(lgf[h:h + 1, 0:1] * (lc - 1.0 - t))
            wb = jnp.exp(lgb[h:h + 1, 0:1] * t)
            kh, vhb = k[:, hs], v[:, hs].astype(BF16)
            dsf, dsb = dscf_ref[h].astype(BF16), dscb_ref[h].astype(BF16)
            dkf = wf * _dot_nt(vhb, dsf)
            dkb = wb * _dot_nt(vhb, dsb)
            dvs.append(_dot((kh * wf).astype(BF16), dsf) + _dot((kh * wb).astype(BF16), dsb))
            dks.append((dkf + dkb) * KSCALE)
            lf = jnp.sum((lc - 1.0 - t) * kh * dkf, axis=0, keepdims=True)
            lb = jnp.sum(t * kh * dkb, axis=0, keepdims=True)
            row_f = row_f + jnp.where(lane == h, jnp.sum(lf, axis=1, keepdims=True), 0.0)
            row_b = row_b + jnp.where(lane == h, jnp.sum(lb, axis=1, keepdims=True), 0.0)
        dlg_ref[...] = dlgin_ref[...]
        dlg_ref[0:1, 0:128] += row_f
        dlg_ref[1:2, 0:128] += row_b
        dkv = jnp.concatenate(dks + dvs, axis=1).astype(BF16)
        dhc = _dot(dkv, win_ref[KV_LO:KV_HI, :])
        dwin_ref[...] = _dot_tn(dkv, hc_ref[...])
        x = ctx_ref[...]
        r = lax.rsqrt(jnp.mean(x * x, axis=-1, keepdims=True) + EPS)
        xh = x * r
        small_ref[...] = jnp.zeros_like(small_ref)
        small_ref[0:1, :] = jnp.sum(dhc, axis=0, keepdims=True)
        dhxx = jnp.sum(dhc * xh, axis=0, keepdims=True)
        small_ref[1:2, :] = dhxx * n1_ref[...]
        small_ref[2:3, :] = dhxx * (1.0 + cmod_ref[1:2, :])

    return pl.pallas_call(
        body, name="ctx_bwd",
        out_shape=(jax.ShapeDtypeStruct((2 * RW, D), F32), jax.ShapeDtypeStruct((8, D), F32),
                   jax.ShapeDtypeStruct((8, D), F32)),
        compiler_params=_cparams(),
    )(ctx, hc, kvc, cmod6, norm1, win, rlf, rlb, dscf, dscb, dlg_in)


def _adam_math(w, g, m, v):
    m = ADAM_B1 * m + (1.0 - ADAM_B1) * g
    v = ADAM_B2 * v + (1.0 - ADAM_B2) * (g * g)
    m_hat = m / (1.0 - ADAM_B1 ** ADAM_STEP)
    v_hat = v / (1.0 - ADAM_B2 ** ADAM_STEP)
    delta = -ADAM_LR * (m_hat / (jnp.sqrt(v_hat) + ADAM_EPS) + ADAM_WD * w)
    return delta, m, v


def _place():
    x, y, c = lax.axis_index("x"), lax.axis_index("y"), lax.axis_index("c")
    return x, y, c, 4 * x + 2 * y + c


def _flip(x, y, c, k):
    px = 1 - x if (k >> 2) & 1 else x
    py = 1 - y if (k >> 1) & 1 else y
    pc = 1 - c if k & 1 else c
    return (px, py, pc), 4 * px + 2 * py + pc


def _gather_copy(buf_ref, send_sems, recv_sems, sem0, k, mine):
    x, y, c, me = _place()
    dev, idx = _flip(x, y, c, k)
    blk = me if mine else idx
    return pltpu.make_async_remote_copy(
        src_ref=buf_ref.at[blk], dst_ref=buf_ref.at[blk],
        send_sem=send_sems.at[sem0 + k - 1], recv_sem=recv_sems.at[sem0 + k - 1],
        device_id=dev, device_id_type=MESH)


def _entry_gather(c_row, cctx_row, wmod, bmod, shards):
    n = len(shards)
    ncol = wmod.shape[1]

    def body(*refs):
        c_ref, cctx_ref, wmod_ref, bmod_ref = refs[:4]
        in_refs = refs[4:4 + n]
        mod_ref, cs_ref = refs[4 + n:6 + n]
        out_refs = refs[6 + n:6 + 2 * n]
        cbuf, pbuf = refs[6 + 2 * n:8 + 2 * n]
        cast_refs = refs[8 + 2 * n:8 + 3 * n]
        small_send, small_recv, send_sems, recv_sems, local_sems = refs[8 + 3 * n:]
        x, y, c, me = _place()
        sib = (x, y, 1 - c)
        chips = [(1 - x, y), (x, 1 - y), (1 - x, 1 - y)]

        cbuf[me] = jnp.broadcast_to(c_ref[...], (8, D))
        small = [_gather_copy(cbuf, small_send, small_recv, 0, k, True) for k in range(1, NDEV)]
        for cp in small:
            cp.start()

        def blk(px, py, pc):
            return 4 * px + 2 * py + pc

        def copy(w, k, block, to, src=None):
            dst = out_refs[w].at[block]
            return pltpu.make_async_remote_copy(
                src_ref=dst if src is None else src, dst_ref=dst,
                send_sem=send_sems.at[w, k], recv_sem=recv_sems.at[w, k], device_id=to, device_id_type=MESH)

        first, passed, mine = [], [], []
        for w in range(n):
            cast_refs[w][...] = in_refs[w][...].astype(BF16)
            m = pltpu.make_async_copy(cast_refs[w], out_refs[w].at[me], local_sems.at[w])
            m.start()
            mine.append(m)
            cps = [copy(w, 0, me, sib, src=cast_refs[w])]
            cps += [copy(w, 1 + j, me, (*chip, c), src=cast_refs[w]) for j, chip in enumerate(chips)]
            for cp in cps:
                cp.start()
            first += cps

        for k in range(1, NDEV):
            _gather_copy(cbuf, small_send, small_recv, 0, k, False).wait_recv()
        row = lax.broadcasted_iota(jnp.int32, (16, D), 0)
        call = jnp.where(row == 8, jnp.broadcast_to(cctx_ref[...], (16, D)), 0.0)
        for d in range(NDEV):
            call = jnp.where(row == d, jnp.concatenate([cbuf[d], cbuf[d]], axis=0), call)
        cs = call * _sigmoid(call)
        cs_ref[...] = cs
        pbuf[me] = _dot(cs.astype(BF16), wmod_ref[...].astype(BF16))
        small2 = [_gather_copy(pbuf, small_send, small_recv, NDEV - 1, k, True) for k in range(1, NDEV)]
        for cp in small2:
            cp.start()

        for w in range(n):
            for j, chip in enumerate(chips):
                b = blk(*chip, c)
                copy(w, 1 + j, b, (x, y, c)).wait_recv()
                fw = copy(w, 4 + j, b, sib)
                fw.start()
                passed.append(fw)
        for w in range(n):
            copy(w, 0, blk(x, y, 1 - c), (x, y, c)).wait_recv()
            for j, chip in enumerate(chips):
                copy(w, 4 + j, blk(*chip, 1 - c), (x, y, c)).wait_recv()

        for k in range(1, NDEV):
            _gather_copy(pbuf, small_send, small_recv, NDEV - 1, k, False).wait_recv()
        for d in range(NDEV):
            mod_ref[:, d * ncol:(d + 1) * ncol] = pbuf[d] + bmod_ref[:, d * ncol:(d + 1) * ncol]
        for cp in small + small2 + first + passed:
            cp.wait_send()
        for m in mine:
            m.wait()

    vm = pl.BlockSpec(memory_space=pltpu.VMEM)
    hbm = pl.BlockSpec(memory_space=pltpu.HBM)
    return pl.pallas_call(
        body, name="entry_gather",
        in_specs=[vm] * (4 + n), out_specs=[vm, vm] + [hbm] * n,
        out_shape=(jax.ShapeDtypeStruct((16, 6 * D), F32), jax.ShapeDtypeStruct((16, D), F32))
        + tuple(jax.ShapeDtypeStruct((NDEV,) + s.shape, BF16) for s in shards),
        scratch_shapes=[pltpu.VMEM((NDEV, 8, D), F32), pltpu.VMEM((NDEV, 16, ncol), F32)]
        + [pltpu.VMEM(s.shape, BF16) for s in shards]
        + [pltpu.SemaphoreType.DMA((2 * (NDEV - 1),)), pltpu.SemaphoreType.DMA((2 * (NDEV - 1),)),
           pltpu.SemaphoreType.DMA((n, 7)), pltpu.SemaphoreType.DMA((n, 7)), pltpu.SemaphoreType.DMA((n,))],
        compiler_params=_cparams(),
    )(c_row, cctx_row, wmod, bmod, *shards)


_HBM = pl.BlockSpec(memory_space=pltpu.HBM)
_SEMS = pl.BlockSpec(memory_space=pltpu.SEMAPHORE)
_EFFECT = pltpu.SideEffectType.DATAFLOW_SIDE_EFFECTING


def _exchange_copy(src_refs, land_refs, send_sems, recv_sems, w, k, scatter, landing_slot_is_mine):
    x, y, c, me = _place()
    dev, pidx = _flip(x, y, c, k)
    src = src_refs[w].at[pidx] if scatter else src_refs[w]
    slot = me if landing_slot_is_mine else pidx
    return pltpu.make_async_remote_copy(
        src_ref=src, dst_ref=land_refs[w].at[slot],
        send_sem=send_sems.at[w * (NDEV - 1) + k - 1], recv_sem=recv_sems.at[w * (NDEV - 1) + k - 1],
        device_id=dev, device_id_type=MESH)


def _exchange_start(srcs, *, scatter, name):
    n = len(srcs)
    lands = [lax.empty((NDEV,) + tuple(s.shape[-2:]), s.dtype) for s in srcs]

    def body(*refs):
        src_refs, land_refs = refs[:n], refs[n:2 * n]
        send_sems, recv_sems = refs[2 * n], refs[2 * n + 1]
        token = refs[-1]
        for w in range(n):
            for k in range(1, NDEV):
                _exchange_copy(src_refs, land_refs, send_sems, recv_sems, w, k, scatter, True).start()
        token[...] = jnp.zeros_like(token)

    arrs = list(srcs) + lands
    out_shape = ([pltpu.SemaphoreType.DMA((n * (NDEV - 1),)), pltpu.SemaphoreType.DMA((n * (NDEV - 1),))]
                 + [pltpu.HBM(a.shape, a.dtype) for a in arrs] + [jax.ShapeDtypeStruct((8, 128), F32)])
    res = pl.pallas_call(
        body, name=name, out_shape=tuple(out_shape),
        in_specs=[_HBM] * (2 * n),
        out_specs=tuple([_SEMS, _SEMS] + [_HBM] * (2 * n) + [pl.BlockSpec(memory_space=pltpu.VMEM)]),
        input_output_aliases={i: 2 + i for i in range(2 * n)},
        compiler_params=pltpu.CompilerParams(has_side_effects=_EFFECT),
    )(*[pltpu.with_memory_space_constraint(a, pltpu.HBM) for a in arrs])
    return res[:-1], res[-1]


def _exchange_wait(handles, after, *, scatter, name):
    n = (len(handles) - 2) // 2
    na = len(after)

    def body(*refs):
        src_refs, land_refs = refs[:n], refs[n:2 * n]
        send_sems, recv_sems = refs[2 * n], refs[2 * n + 1]
        for w in range(n):
            for k in range(1, NDEV):
                cp = _exchange_copy(src_refs, land_refs, send_sems, recv_sems, w, k, scatter, False)
                cp.wait_send()
                cp.wait_recv()

    arrs = handles[2:]
    res = pl.pallas_call(
        body, name=name, out_shape=tuple(pltpu.HBM(a.shape, a.dtype) for a in arrs),
        in_specs=[_HBM] * (2 * n) + [_SEMS, _SEMS] + [_HBM] * na,
        out_specs=tuple([_HBM] * (2 * n)),
        input_output_aliases={i: i for i in range(2 * n)},
        compiler_params=pltpu.CompilerParams(has_side_effects=_EFFECT),
    )(*arrs, handles[0], handles[1], *[pltpu.with_memory_space_constraint(a, pltpu.HBM) for a in after])
    return res[:n], res[n:]


def _cast_bf16(arrs, *, name, after=()):
    n = len(arrs)

    def body(*refs):
        for i_ref, o_ref in zip(refs[:n], refs[n + len(after):]):
            o_ref[...] = i_ref[...].astype(BF16)

    return pl.pallas_call(
        body, name=name, out_shape=tuple(jax.ShapeDtypeStruct(a.shape, BF16) for a in arrs),
        in_specs=[pl.BlockSpec(memory_space=pltpu.VMEM)] * n + [pl.BlockSpec(memory_space=pl.ANY)] * len(after),
        compiler_params=_cparams())(*arrs, *after)


def _rs_adam(me_arr, fulls, lands, w, m, v, *, name):
    rows = lands[0].shape[1]
    k = len(fulls)

    def body(me_ref, *refs):
        own_refs, land_refs = refs[:k], refs[k:2 * k]
        w_ref, m_ref, v_ref, g_ref, d_ref, mo_ref, vo_ref = refs[2 * k:]
        me = me_ref[0]
        parts = []
        for own_ref, land_ref in zip(own_refs, land_refs):
            acc = jnp.zeros(own_ref.shape, F32)
            for p in range(NDEV):
                acc = acc + jnp.where(p == me, own_ref[...], land_ref[p].astype(F32))
            parts.append(acc)
        acc = parts[0] if k == 1 else jnp.concatenate(parts, axis=1)
        g_ref[...] = acc
        d_ref[...], mo_ref[...], vo_ref[...] = _adam_math(w_ref[...], acc, m_ref[...], v_ref[...])

    sh = jax.ShapeDtypeStruct((rows, D), F32)
    whole2 = pl.BlockSpec((rows, D), lambda i, me: (0, 0))
    grid_spec = pltpu.PrefetchScalarGridSpec(
        num_scalar_prefetch=1, grid=(1,),
        in_specs=[pl.BlockSpec((rows, f.shape[1]), lambda i, me: (me[0], 0)) for f in fulls]
        + [pl.BlockSpec((NDEV, rows, l.shape[2]), lambda i, me: (0, 0, 0)) for l in lands]
        + [whole2, whole2, whole2],
        out_specs=[whole2] * 4)
    return pl.pallas_call(
        body, name=name, out_shape=(sh, sh, sh, sh), grid_spec=grid_spec, compiler_params=_cparams(1),
    )(me_arr, *fulls, *lands, w, m, v)


ROW_F, ROW_I, ROW_C, ROW_G1, ROW_LG = 0, 8, 16, 24, 32
PACK_ROWS = 40


def _small_sum(me_arr, pack, pack_land, sgw, sgw_land, wmod):
    ncol = wmod.shape[1]

    def body(me_ref, pack_ref, pland_ref, sgw_ref, sland_ref, wmod_ref, sum_ref, all_ref, sgw_sum_ref, part_ref):
        me = me_ref[0]
        acc = jnp.zeros((PACK_ROWS, D), F32)
        acc_w = jnp.zeros(sgw_ref.shape, F32)
        for p in range(NDEV):
            rows = jnp.where(p == me, pack_ref[...], pland_ref[p])
            all_ref[p] = rows
            acc = acc + rows
            acc_w = acc_w + jnp.where(p == me, sgw_ref[...], sland_ref[p])
        sum_ref[...] = acc
        sgw_sum_ref[...] = acc_w
        zero = jnp.zeros((1, D), F32)
        dcmod = jnp.concatenate([acc[ROW_C:ROW_C + 1], acc[ROW_C + 1:ROW_C + 2], zero, zero, zero, zero], axis=1)
        mine = jnp.zeros((1, ncol), F32)
        for d in range(NDEV):
            mine = mine + jnp.where(d == me, dcmod[:, d * ncol:(d + 1) * ncol], 0.0)
        part_ref[...] = _dot_nt(jnp.broadcast_to(mine, (8, ncol)).astype(BF16), wmod_ref[...].astype(BF16))

    vm = pl.BlockSpec(memory_space=pltpu.VMEM)
    return pl.pallas_call(
        body, name="small_sum",
        out_shape=(jax.ShapeDtypeStruct((PACK_ROWS, D), F32), jax.ShapeDtypeStruct((NDEV, PACK_ROWS, D), F32),
                   jax.ShapeDtypeStruct(sgw.shape, F32), jax.ShapeDtypeStruct((8, D), F32)),
        in_specs=[pl.BlockSpec(memory_space=pltpu.SMEM)] + [vm] * 5,
        compiler_params=_cparams(),
    )(me_arr, pack, pack_land, sgw, sgw_land, wmod)


def _cctx_final(me_arr, part, part_land, cctx_row, m_row, v_row):
    def body(me_ref, part_ref, land_ref, c_ref, m_ref, v_ref, g_ref, d_ref, mo_ref, vo_ref):
        me = me_ref[0]
        tot = jnp.zeros((8, D), F32)
        for p in range(NDEV):
            tot = tot + jnp.where(p == me, part_ref[...], land_ref[p])
        cc = c_ref[...]
        _, dsilu = _silu_parts(cc)
        g = tot[0:1, :] * dsilu
        g_ref[...] = g
        d_ref[...], mo_ref[...], vo_ref[...] = _adam_math(cc, g, m_ref[...], v_ref[...])

    row = jax.ShapeDtypeStruct((1, D), F32)
    vm = pl.BlockSpec(memory_space=pltpu.VMEM)
    return pl.pallas_call(
        body, name="cctx_final", out_shape=(row, row, row, row),
        in_specs=[pl.BlockSpec(memory_space=pltpu.SMEM)] + [vm] * 5,
        compiler_params=_cparams(),
    )(me_arr, part, part_land, cctx_row, m_row, v_row)


def _adam_small(s, sgw_g, params):
    names = ["norm1", "norm2", "norm_f", "sg_gain", "sg_b", "ret_logit_f", "ret_logit_b", "b_mod", "sg_w"]
    flat = [a for n in names for a in params[n]]

    def body(*refs):
        s_ref, sgw_ref = refs[0], refs[1]
        p_refs = refs[2:2 + 3 * len(names)]
        o_refs = refs[2 + 3 * len(names):]
        loss_ref = o_refs[-1]

        def row(r):
            return s_ref[r:r + 1, :]

        grads = {
            "norm1": row(ROW_I + 2) + row(ROW_C + 2),
            "norm2": row(ROW_F + 4),
            "norm_f": row(ROW_F + 0),
            "sg_gain": s_ref[ROW_I + 3:ROW_I + 4, 0:AW],
            "sg_b": s_ref[ROW_I + 4:ROW_I + 8, 0:DH],
            "sg_w": sgw_ref[...],
        }
        for j, n in enumerate(names):
            w_ref, m_ref, v_ref = p_refs[3 * j:3 * j + 3]
            g_ref, d_ref, mo_ref, vo_ref = o_refs[4 * j:4 * j + 4]
            w = w_ref[...]
            if n in ("ret_logit_f", "ret_logit_b"):
                r = ROW_LG + (0 if n == "ret_logit_f" else 1)
                g = s_ref[r:r + 1, 0:H] * _sigmoid(-w)
            elif n == "b_mod":
                rows = [row(ROW_I + 0) + row(ROW_C + 0), row(ROW_I + 1) + row(ROW_C + 1), row(ROW_G1),
                        row(ROW_F + 2), row(ROW_F + 3), row(ROW_F + 1)]
                g = jnp.concatenate(rows, axis=1)
            else:
                g = grads[n]
            g_ref[...] = g
            d_ref[...], mo_ref[...], vo_ref[...] = _adam_math(w, g, m_ref[...], v_ref[...])
        loss_ref[...] = jnp.full((1, 1), 0.5 / D, F32) * jnp.sum(row(ROW_F + 5), axis=1, keepdims=True)

    out_shape = []
    for n in names:
        w = params[n][0]
        out_shape += [jax.ShapeDtypeStruct(w.shape, F32)] * 4
    out_shape.append(jax.ShapeDtypeStruct((1, 1), F32))
    outs = pl.pallas_call(body, name="adam_small", out_shape=tuple(out_shape), compiler_params=_cparams())(
        s, sgw_g, *flat)
    return {n: tuple(outs[4 * j:4 * j + 4]) for j, n in enumerate(names)}, outs[-1]


def _wmod_grad(cs_all, dmod_cols, wmod, m, v):
    ncol = wmod.shape[1]

    def body(cs_ref, dm_ref, w_ref, m_ref, v_ref, g_ref, d_ref, mo_ref, vo_ref):
        g = _dot_tn(cs_ref[...].astype(BF16), dm_ref[...].astype(BF16))
        g_ref[...] = g
        d_ref[...], mo_ref[...], vo_ref[...] = _adam_math(w_ref[...], g, m_ref[...], v_ref[...])

    sh = jax.ShapeDtypeStruct((D, ncol), F32)
    return pl.pallas_call(
        body, name="wmod_grad", out_shape=(sh, sh, sh, sh),
        compiler_params=_cparams(),
    )(cs_all, dmod_cols, wmod, m, v)


def _rope_tables(t_len):
    n_freq = DH // 4
    inv = (ROPE_BASE ** (-np.arange(n_freq, dtype=np.float32) / n_freq)).astype(np.float32)
    tok = np.arange(t_len)
    rows = (tok // GRID_W).astype(np.float32)
    cols = (tok % GRID_W).astype(np.float32)
    ang_r = rows[:, None] * inv[None, :]
    ang_c = cols[:, None] * inv[None, :]
    cos = np.concatenate([np.cos(ang_r)] * 2 + [np.cos(ang_c)] * 2, axis=1).astype(np.float32)
    sin = np.concatenate([-np.sin(ang_r), np.sin(ang_r), -np.sin(ang_c), np.sin(ang_c)], axis=1).astype(np.float32)
    return jnp.asarray(cos), jnp.asarray(sin)


def _local_step(x, tgt, ctx, mod6, cmod6, norm1, norm2, normf, win, wout, ffn_weights, sgw, sgb, sggain, rlf, rlb,
                *, tm_a, tm_b, tm_f, tm_m, tm_r, tm_i, bt_w, after_first_grads=None, small_path=None,
                after_in_half=None):
    t_len = x.shape[0]
    cos, sin = _rope_tables(t_len)
    sgbt = jnp.broadcast_to(sgb[:, :, None], (G, C, 128))
    rlf = jnp.broadcast_to(rlf.reshape(H, 1), (H, 128))
    rlb = jnp.broadcast_to(rlb.reshape(H, 1), (H, 128))
    hc, kvc, scf, scb = _ctx_fwd(ctx, cmod6, norm1, win, rlf, rlb)
    hx, zuv, q, k, v, gfb, of, ya, sst = _fwd_a(x, mod6, norm1, win, sgw, sgbt, sggain, cos, sin, rlf, scf, tm=tm_a)
    wout = wout(hx) if callable(wout) else wout
    ob, ycat, y, x1, rst = _fwd_b(q, k, v, of, gfb, ya, x, mod6, wout, rlb, scb, tm=tm_b)
    wg, wu, wd = ffn_weights(x1) if callable(ffn_weights) else ffn_weights
    dx1, h2, da, db, hm, df, small_f = _ffn(x1, tgt, mod6, norm2, normf, wg, wu, wd, tm=tm_f)
    bt2 = min(2 * bt_w, t_len)
    d_wd, d_wd_b = _tn_matmul(hm, df, bm=DFF // 2, bt=bt2, name="dw_down")
    d_wg, d_wg_b = _tn_matmul(da, h2, bm=DFF // 2, bt=bt2, name="dw_gate")
    d_wu, d_wu_b = _tn_matmul(db, h2, bm=DFF // 2, bt=bt2, name="dw_up")
    dy, dya, dgfb, dof, dob, dg1 = _mid_bwd(dx1, y, of, ob, gfb, mod6, wout, tm=tm_m)
    d_wo, d_wo_b = _tn_matmul(ycat, dy, bm=D, bt=bt_w, name="dw_out")
    if after_first_grads is not None:
        rlf = rlf + after_first_grads((d_wd_b, d_wg_b, d_wu_b, d_wo_b))
    dqf, dkf, dvf, dqb, dkb, dvb, dscf, dscb, dlg = _ret_bwd(q, k, v, dof, dob, sst, rst, rlf, rlb, tm=tm_r)
    gx, dz, small_i, dsgw = _in_bwd(x, zuv, dya, dqf, dkf, dvf, dqb, dkb, dvb, dgfb, dx1, cos, sin,
                                    mod6, norm1, win, sgw, sgbt, sggain, tm=tm_i)
    dwin_c, small_c, dlg = _ctx_bwd(ctx, hc, kvc, cmod6, norm1, win, rlf, rlb, dscf, dscb, dlg)
    pack = jnp.concatenate([small_f, small_i, small_c, dg1, dlg], axis=0)
    after = small_path(pack, dsgw) if small_path is not None else ()
    halves = []
    for j in range(2):
        halves.append(_tn_matmul(dz, hx, bm=INC // 2, bt=bt_w, name="dw_in_%d" % j, init=dwin_c,
                                 init_rows=(KV_LO, KV_HI), after=after, cols=(j, D // 2)))
        if after_in_half is not None:
            after = after_in_half(j, halves[-1][1])
    grads = {"w_in": halves, "w_out": (d_wo, d_wo_b), "w_gate": (d_wg, d_wg_b), "w_up": (d_wu, d_wu_b),
             "w_down": (d_wd, d_wd_b)}
    return gx, grads, pack, dsgw


def kernel(x, c, ctx, c_ctx, w_mod, b_mod, norm1, w_in, sg_gain, sg_w, sg_b, ret_logit_f, ret_logit_b, w_out, norm2, w_gate, w_up, w_down, norm_f, loss_target, m_c_ctx, m_w_mod, m_b_mod, m_norm1, m_w_in, m_sg_gain, m_sg_w, m_sg_b, m_ret_logit_f, m_ret_logit_b, m_w_out, m_norm2, m_w_gate, m_w_up, m_w_down, m_norm_f, v_c_ctx, v_w_mod, v_b_mod, v_norm1, v_w_in, v_sg_gain, v_sg_w, v_sg_b, v_ret_logit_f, v_ret_logit_b, v_w_out, v_norm2, v_w_gate, v_w_up, v_w_down, v_norm_f):
    t_len = x.shape[1]
    tm = min(512, t_len)
    me = 4 * lax.axis_index("x") + 2 * lax.axis_index("y") + lax.axis_index("c")
    tr = lambda a: jnp.transpose(a[0])

    cctx_row = c_ctx.reshape(1, D)
    mod_all, cs_all, g_in = _entry_gather(c, cctx_row, w_mod[0], b_mod, [tr(w_in)])
    mod6 = lax.dynamic_slice(mod_all, (me, 0), (1, 6 * D)).reshape(6, D)
    cmod6 = mod_all[8].reshape(6, D)
    win_t = g_in.reshape(INC, D)

    (out_shard,) = _cast_bf16([w_out[0]], name="cast_out", after=[g_in])
    h_out, tok_out = _exchange_start([out_shard], scatter=False, name="wout_start")
    ffn_shards = _cast_bf16([tr(w_gate), tr(w_up), w_down[0]], name="cast_ffn", after=[h_out[2]])
    h_ffn, tok = _exchange_start(ffn_shards, scatter=False, name="wffn_start")
    mod6 = mod6 + (tok_out[0, 0] + tok[0, 0])

    def gathered(handles, after, name, rows):
        own, lands = _exchange_wait(handles, [after], scatter=False, name=name)
        return [lax.dynamic_update_slice(l, o[None], (me, 0, 0)).reshape(rows, D) for o, l in zip(own, lands)]

    def wout(after):
        return gathered(h_out, after, "wout_wait", D)[0]

    def ffn_weights(after):
        return gathered(h_ffn, after, "wffn_wait", DFF)

    rs, outs = {}, {}

    def after_first_grads(bf):
        rs["first"], tok_first = _exchange_start([b.reshape(NDEV, b.shape[0] // NDEV, D) for b in bf], scatter=True,
                                                 name="rs_first_start")
        return tok_first[0, 0]

    def small_path(pack, dsgw):
        rs["small"], _ = _exchange_start([pack, dsgw.reshape(G * C, C)], scatter=False, name="small_start")
        return [rs["small"][2], rs["small"][3]]

    def after_in_half(j, half_bf16):
        rs["in%d" % j], _ = _exchange_start([half_bf16.reshape(NDEV, INC // NDEV, D // 2)], scatter=True,
                                            name="rs_in%d_start" % j)
        return [rs["in%d" % j][2]]

    gx, grads, pack, dsgw = _local_step(
        x[0], loss_target[0], ctx[0], mod6, cmod6, norm1, norm2[0:1], norm_f.reshape(1, D), win_t, wout, ffn_weights,
        sg_w[0], sg_b[0], sg_gain, ret_logit_f, ret_logit_b,
        tm_a=tm, tm_b=tm, tm_f=min(256, t_len), tm_m=tm, tm_r=tm, tm_i=min(256, t_len), bt_w=min(1024, t_len),
        after_first_grads=after_first_grads, small_path=small_path, after_in_half=after_in_half)

    me_arr = me.reshape(1).astype(jnp.int32)
    (pack_own, sgw_own), (pack_land, sgw_land) = _exchange_wait(rs["small"], [rs["in1"][2]], scatter=False,
                                                               name="small_wait")
    psum_rows, pall, sgw_sum, part = _small_sum(me_arr, pack_own, pack_land, sgw_own, sgw_land, w_mod[0])
    h_cc, _ = _exchange_start([part], scatter=False, name="cctx_start")

    dmod_rows = (ROW_I + 0, ROW_I + 1, ROW_G1, ROW_F + 2, ROW_F + 3, ROW_F + 1)
    dmod_all = jnp.concatenate([pall[:, r, :] for r in dmod_rows], axis=1)
    dcmod = jnp.concatenate([psum_rows[ROW_C + 0:ROW_C + 1], psum_rows[ROW_C + 1:ROW_C + 2],
                             jnp.zeros((1, 4 * D), F32)], axis=1)
    dmod_mat = jnp.concatenate([dmod_all, jnp.pad(dcmod, ((0, 7), (0, 0)))], axis=0)
    ncol = 6 * D // NDEV
    dmod_cols = lax.dynamic_slice(dmod_mat, (0, me * ncol), (16, ncol))
    g_wm, d_wm, m_wm, v_wm = _wmod_grad(cs_all, dmod_cols, w_mod[0], m_w_mod[0], v_w_mod[0])
    outs["w_mod"] = (g_wm[None], d_wm[None], m_wm[None], v_wm[None])
    small_params = {
        "norm1": (norm1, m_norm1, v_norm1), "norm2": (norm2, m_norm2, v_norm2),
        "norm_f": tuple(a.reshape(1, D) for a in (norm_f, m_norm_f, v_norm_f)),
        "sg_gain": (sg_gain, m_sg_gain, v_sg_gain), "sg_b": (sg_b[0], m_sg_b[0], v_sg_b[0]),
        "ret_logit_f": (ret_logit_f, m_ret_logit_f, v_ret_logit_f),
        "ret_logit_b": (ret_logit_b, m_ret_logit_b, v_ret_logit_b),
        "b_mod": (b_mod, m_b_mod, v_b_mod), "sg_w": (sg_w[0], m_sg_w[0], v_sg_w[0])}
    small, loss = _adam_small(psum_rows, sgw_sum.reshape(G, C, C), small_params)
    back = {"norm_f": lambda a: a.reshape(D), "sg_b": lambda a: a[None], "sg_w": lambda a: a[None]}
    for name, vals in small.items():
        outs[name] = tuple(back.get(name, lambda a: a)(a) for a in vals)

    _, lands_first = _exchange_wait(rs["first"], [g_wm], scatter=True, name="rs_first_wait")

    def finish(name, fulls, lands, w, m, v, transposed):
        take = tr if transposed else (lambda a: a[0])
        res = _rs_adam(me_arr, fulls, lands, take(w), take(m), take(v), name="adam_" + name)
        put = (lambda a: jnp.transpose(a)[None]) if transposed else (lambda a: a[None])
        outs[name] = tuple(put(a) for a in res)
        return res[0]

    g_down = finish("w_down", [grads["w_down"][0]], [lands_first[0]], w_down, m_w_down, v_w_down, False)
    g_gate = finish("w_gate", [grads["w_gate"][0]], [lands_first[1]], w_gate, m_w_gate, v_w_gate, True)
    g_up = finish("w_up", [grads["w_up"][0]], [lands_first[2]], w_up, m_w_up, v_w_up, True)
    g_out = finish("w_out", [grads["w_out"][0]], [lands_first[3]], w_out, m_w_out, v_w_out, False)
    done = [g_down, g_gate, g_up, g_out]
    (part_own,), (part_land,) = _exchange_wait(h_cc, done, scatter=False, name="cctx_wait")
    res_cc = _cctx_final(me_arr, part_own, part_land, cctx_row, m_c_ctx.reshape(1, D), v_c_ctx.reshape(1, D))
    outs["c_ctx"] = tuple(a.reshape(D) for a in res_cc)
    lands_in = [_exchange_wait(rs["in%d" % j], done, scatter=True, name="rs_in%d_wait" % j)[1][0] for j in range(2)]
    finish("w_in", [h[0] for h in grads["w_in"]], lands_in, w_in, m_w_in, v_w_in, True)

    order = ["c_ctx", "w_mod", "b_mod", "norm1", "w_in", "sg_gain", "sg_w", "sg_b", "ret_logit_f", "ret_logit_b",
             "w_out", "norm2", "w_gate", "w_up", "w_down", "norm_f"]
    res = [loss.reshape(()), gx[None]]
    for j in range(4):
        res += [outs[name][j] for name in order]
    return tuple(res)
```

```python
import functools
import math

import numpy as np
import jax
import jax.numpy as jnp
from jax import lax
from jax.experimental import pallas as pl
from jax.experimental.pallas import tpu as pltpu

F32 = jnp.float32
BF16 = jnp.bfloat16

D = 1024
AW = 512
RW = 512
H = 4
DH = 128
G = 4
C = 128
CR = 256
DFF = 2816
INC = 3584
KV_LO, KV_HI = 1536, 2560
NDEV = 8
EPS = 1e-6
KSCALE = DH ** -0.5
ROPE_BASE = 10000.0
GRID_W = 64

ADAM_LR = 0.001
ADAM_B1 = 0.9
ADAM_B2 = 0.999
ADAM_EPS = 1e-08
ADAM_WD = 0.01
ADAM_STEP = 10

VMEM_LIMIT = 56 * 1024 * 1024
MESH = pl.DeviceIdType.MESH


def _cparams(n_grid=0, **kw):
    sem = ("arbitrary",) * n_grid if n_grid else None
    return pltpu.CompilerParams(dimension_semantics=sem, vmem_limit_bytes=VMEM_LIMIT, **kw)


def _dot(a, b):
    return jnp.dot(a, b, preferred_element_type=F32)


def _dot_nt(a, b):
    return lax.dot_general(a, b, (((1,), (1,)), ((), ())), preferred_element_type=F32)


def _dot_tn(a, b):
    return lax.dot_general(a, b, (((0,), (0,)), ((), ())), preferred_element_type=F32)


def _whole(shape):
    nd = len(shape)
    return pl.BlockSpec(shape, lambda *_: (0,) * nd, pipeline_mode=pl.Buffered(1))


def _acc(shape):
    nd = len(shape)
    return pl.BlockSpec(shape, lambda *_: (0,) * nd)


def _rows(tm, n):
    return pl.BlockSpec((tm, n), lambda i: (i, 0))


def _rows_rev(tm, n, nt):
    return pl.BlockSpec((tm, n), lambda i: (nt - 1 - i, 0))


def _cols(n, tm):
    return pl.BlockSpec((n, tm), lambda i: (0, i))


def _sigmoid(x):
    return 1.0 / (1.0 + jnp.exp(-x))


def _log_sigmoid(x):
    return jnp.minimum(x, 0.0) - jnp.log(1.0 + jnp.exp(-jnp.abs(x)))


_GK0 = math.sqrt(2.0 / math.pi)
_GK1 = 0.044715


def _gelu(x):
    x2 = x * x
    t = jnp.tanh(x * (_GK0 + (_GK0 * _GK1) * x2))
    h = 0.5 + 0.5 * t
    g = x * h
    dg = h + g * (1.0 - h) * ((2.0 * _GK0) + (6.0 * _GK0 * _GK1) * x2)
    return g, dg


def _silu_parts(a):
    s = _sigmoid(a)
    sa = a * s
    return sa, s + sa * (1.0 - s)


def _rope(t, cos, sins):
    n = t.shape[1]
    lane = lax.broadcasted_iota(jnp.int32, t.shape, 1)
    first = (lane & 63) < 32
    partner = jnp.where(first, pltpu.roll(t, n - 32, 1), pltpu.roll(t, 32, 1))
    return t * cos + partner * sins


def _headnorm(o):
    outs, rs = [], []
    for h in range(H):
        oh = o[:, h * DH:(h + 1) * DH]
        r = lax.rsqrt(jnp.mean(oh * oh, axis=-1, keepdims=True) + EPS)
        outs.append(oh * r)
        rs.append(r)
    return jnp.concatenate(outs, axis=1), rs


def _decay_consts(lg, forward):
    ii = lax.broadcasted_iota(jnp.int32, (CR, CR), 0).astype(F32)
    jj = lax.broadcasted_iota(jnp.int32, (CR, CR), 1).astype(F32)
    ic = lax.broadcasted_iota(jnp.int32, (CR, 1), 0).astype(F32)
    if forward:
        diff = ii - jj
        xi = jnp.exp(lg * (ic + 1.0))
        z = jnp.exp(lg * (CR - 1.0 - ic))
    else:
        diff = jj - ii
        xi = jnp.exp(lg * (CR - ic))
        z = jnp.exp(lg * ic)
    dm = jnp.where(diff >= 0.0, jnp.exp(lg * jnp.maximum(diff, 0.0)), 0.0)
    dec = jnp.exp(lg * float(CR))
    return dm, xi, z, dec, ic


def _ctx_fwd(ctx, cmod6, norm1, win, rlf, rlb):
    lc = ctx.shape[0]

    def body(ctx_ref, cmod_ref, n1_ref, win_ref, rlf_ref, rlb_ref, hc_ref, kvc_ref, scf_ref, scb_ref):
        x = ctx_ref[...]
        r = lax.rsqrt(jnp.mean(x * x, axis=-1, keepdims=True) + EPS)
        hc = (x * r) * (n1_ref[...] * (1.0 + cmod_ref[1:2, :])) + cmod_ref[0:1, :]
        hcb = hc.astype(BF16)
        hc_ref[...] = hcb
        kv = _dot_nt(hcb, win_ref[KV_LO:KV_HI, :])
        k = kv[:, :RW] * KSCALE
        v = kv[:, RW:]
        kvc_ref[:, :RW] = k
        kvc_ref[:, RW:] = v
        t = lax.broadcasted_iota(jnp.int32, (lc, 1), 0).astype(F32)
        lgf = _log_sigmoid(rlf_ref[...])
        lgb = _log_sigmoid(rlb_ref[...])
        for h in range(H):
            hs = slice(h * DH, (h + 1) * DH)
            wf = jnp.exp(lgf[h:h + 1, 0:1] * (lc - 1.0 - t))
            wb = jnp.exp(lgb[h:h + 1, 0:1] * t)
            vh = v[:, hs].astype(BF16)
            scf_ref[h] = _dot_tn((k[:, hs] * wf).astype(BF16), vh)
            scb_ref[h] = _dot_tn((k[:, hs] * wb).astype(BF16), vh)

    return pl.pallas_call(
        body, name="ctx_fwd",
        out_shape=(jax.ShapeDtypeStruct((lc, D), BF16), jax.ShapeDtypeStruct((lc, 2 * RW), F32),
                   jax.ShapeDtypeStruct((H, DH, DH), F32), jax.ShapeDtypeStruct((H, DH, DH), F32)),
        compiler_params=_cparams(),
    )(ctx, cmod6, norm1, win, rlf, rlb)


def _sg_forward(u, v, sgw_ref, sgbt_ref, sgg_ref, nc):
    ua, dgu = _gelu(u)
    va, dgv = _gelu(v)
    gain = sgg_ref[...]
    mixed, vn_b, vah_l, rv_l = [], [], [], []
    for g in range(G):
        gs = slice(g * DH, (g + 1) * DH)
        vag = va[:, gs]
        rv = lax.rsqrt(jnp.mean(vag * vag, axis=-1, keepdims=True) + EPS)
        vah = vag * rv
        vn = (vah * gain[:, gs]).astype(BF16)
        wg = sgw_ref[g].astype(BF16)
        bcol = sgbt_ref[g]
        rows = [_dot(wg, vn[c * C:(c + 1) * C, :]) + bcol for c in range(nc)]
        mixed.append(jnp.concatenate(rows, axis=0) if nc > 1 else rows[0])
        vn_b.append(vn)
        vah_l.append(vah)
        rv_l.append(rv)
    mixed = jnp.concatenate(mixed, axis=1)
    return ua * mixed, (ua, dgu, dgv, mixed, vn_b, vah_l, rv_l)


def _fwd_a(x, mod6, norm1, win, sgw, sgbt, sggain, cos, sin, rlf, scf, *, tm):
    t_len = x.shape[0]
    nt, nc, ncr = t_len // tm, tm // C, tm // CR

    def body(x_ref, mod_ref, n1_ref, win_ref, sgw_ref, sgbt_ref, sgg_ref, cos_ref, sin_ref, rlf_ref, scf_ref,
             hx_ref, zuv_ref, q_ref, k_ref, v_ref, gfb_ref, of_ref, ya_ref, sst_ref, s_scr):
        i = pl.program_id(0)

        @pl.when(i == 0)
        def _():
            s_scr[...] = scf_ref[...]

        x = x_ref[...]
        r = lax.rsqrt(jnp.mean(x * x, axis=-1, keepdims=True) + EPS)
        hx = (x * r) * (n1_ref[...] * (1.0 + mod_ref[1:2, :])) + mod_ref[0:1, :]
        hxb = hx.astype(BF16)
        hx_ref[...] = hxb
        z = _dot_nt(hxb, win_ref[...])
        zuv_ref[...] = z[:, :2 * AW]
        gfb_ref[...] = z[:, 2560:3584].astype(BF16)
        cos = jnp.tile(cos_ref[...], (1, H))
        sins = jnp.tile(sin_ref[...], (1, H))
        q_ref[...] = _rope(z[:, 1024:1536], cos, sins).astype(BF16)
        k_ref[...] = (_rope(z[:, 1536:2048], cos, sins) * KSCALE).astype(BF16)
        v_ref[...] = z[:, 2048:2560].astype(BF16)
        ya, _ = _sg_forward(z[:, :AW], z[:, AW:2 * AW], sgw_ref, sgbt_ref, sgg_ref, nc)
        ya_ref[...] = ya.astype(BF16)

        lg = _log_sigmoid(rlf_ref[...])
        for h in range(H):
            dm, xi, zc, dec, _ = _decay_consts(lg[h:h + 1, 0:1], True)
            hs = slice(h * DH, (h + 1) * DH)
            for c in range(ncr):
                rs = slice(c * CR, (c + 1) * CR)
                qc, kc, vc = q_ref[rs, hs], k_ref[rs, hs], v_ref[rs, hs]
                s = s_scr[h]
                sst_ref[c, h] = s
                a = (_dot_nt(qc, kc) * dm).astype(BF16)
                of_ref[rs, hs] = (_dot(a, vc) + xi * _dot(qc, s.astype(BF16))).astype(BF16)
                kz = (kc.astype(F32) * zc).astype(BF16)
                s_scr[h] = dec * s + _dot_tn(kz, vc)

    n_chunks = t_len // CR
    return pl.pallas_call(
        body, name="fwd_a", grid=(nt,),
        in_specs=[_rows(tm, D), _whole((6, D)), _whole((1, D)), _whole((INC, D)), _whole((G, C, C)),
                  _whole((G, C, 128)), _whole((1, AW)), _rows(tm, DH), _rows(tm, DH), _whole((H, 128)),
                  _whole((H, DH, DH))],
        out_specs=[_rows(tm, D), _rows(tm, 2 * AW), _rows(tm, RW), _rows(tm, RW), _rows(tm, RW),
                   _rows(tm, 2 * RW), _rows(tm, RW), _rows(tm, AW),
                   pl.BlockSpec((ncr, H, DH, DH), lambda i: (i, 0, 0, 0))],
        out_shape=(jax.ShapeDtypeStruct((t_len, D), BF16), jax.ShapeDtypeStruct((t_len, 2 * AW), F32),
                   jax.ShapeDtypeStruct((t_len, RW), BF16), jax.ShapeDtypeStruct((t_len, RW), BF16),
                   jax.ShapeDtypeStruct((t_len, RW), BF16), jax.ShapeDtypeStruct((t_len, 2 * RW), BF16),
                   jax.ShapeDtypeStruct((t_len, RW), BF16), jax.ShapeDtypeStruct((t_len, AW), BF16),
                   jax.ShapeDtypeStruct((n_chunks, H, DH, DH), F32)),
        scratch_shapes=[pltpu.VMEM((H, DH, DH), F32)],
        compiler_params=_cparams(1),
    )(x, mod6, norm1, win, sgw, sgbt, sggain, cos, sin, rlf, scf)


def _fwd_b(q, k, v, of, gfb, ya, x, mod6, wout, rlb, scb, *, tm):
    t_len = x.shape[0]
    nt, nc = t_len // tm, tm // CR

    def body(q_ref, k_ref, v_ref, of_ref, gfb_ref, ya_ref, x_ref, mod_ref, wout_ref, rlb_ref, scb_ref,
             ob_ref, ycat_ref, y_ref, x1_ref, rst_ref, r_scr):
        i = pl.program_id(0)

        @pl.when(i == 0)
        def _():
            r_scr[...] = scb_ref[...]

        lg = _log_sigmoid(rlb_ref[...])
        for h in range(H):
            dm, xi, zc, dec, _ = _decay_consts(lg[h:h + 1, 0:1], False)
            hs = slice(h * DH, (h + 1) * DH)
            for c in reversed(range(nc)):
                rs = slice(c * CR, (c + 1) * CR)
                qc, kc, vc = q_ref[rs, hs], k_ref[rs, hs], v_ref[rs, hs]
                s = r_scr[h]
                rst_ref[c, h] = s
                a = (_dot_nt(qc, kc) * dm).astype(BF16)
                ob_ref[rs, hs] = (_dot(a, vc) + xi * _dot(qc, s.astype(BF16))).astype(BF16)
                kz = (kc.astype(F32) * zc).astype(BF16)
                r_scr[h] = dec * s + _dot_tn(kz, vc)

        gfb = gfb_ref[...].astype(F32)
        sf, _ = _silu_parts(gfb[:, :RW])
        sb, _ = _silu_parts(gfb[:, RW:])
        hnf, _ = _headnorm(of_ref[...].astype(F32))
        hnb, _ = _headnorm(ob_ref[...].astype(F32))
        yr = sf * hnf + sb * hnb
        ycat = jnp.concatenate([ya_ref[...], yr.astype(BF16)], axis=1)
        ycat_ref[...] = ycat.T
        y = _dot(ycat, wout_ref[...])
        y_ref[...] = y.astype(BF16)
        x1_ref[...] = x_ref[...] + mod_ref[2:3, :] * y

    n_chunks = t_len // CR
    rr = functools.partial(_rows_rev, nt=nt)
    return pl.pallas_call(
        body, name="fwd_b", grid=(nt,),
        in_specs=[rr(tm, RW), rr(tm, RW), rr(tm, RW), rr(tm, RW), rr(tm, 2 * RW), rr(tm, AW), rr(tm, D),
                  _whole((6, D)), _whole((D, D)), _whole((H, 128)), _whole((H, DH, DH))],
        out_specs=[rr(tm, RW), pl.BlockSpec((D, tm), lambda i: (0, nt - 1 - i)), rr(tm, D), rr(tm, D),
                   pl.BlockSpec((nc, H, DH, DH), lambda i: (nt - 1 - i, 0, 0, 0))],
        out_shape=(jax.ShapeDtypeStruct((t_len, RW), BF16), jax.ShapeDtypeStruct((D, t_len), BF16),
                   jax.ShapeDtypeStruct((t_len, D), BF16), jax.ShapeDtypeStruct((t_len, D), F32),
                   jax.ShapeDtypeStruct((n_chunks, H, DH, DH), F32)),
        scratch_shapes=[pltpu.VMEM((H, DH, DH), F32)],
        compiler_params=_cparams(1),
    )(q, k, v, of, gfb, ya, x, mod6, wout, rlb, scb)


def _ffn(x1, tgt, mod6, norm2, normf, wg, wu, wd, *, tm):
    t_len = x1.shape[0]
    nt = t_len // tm

    def body(x1_ref, tgt_ref, mod_ref, n2_ref, nf_ref, wg_ref, wu_ref, wd_ref,
             dx1_ref, h2_ref, da_ref, db_ref, hm_ref, df_ref, small_ref):
        i = pl.program_id(0)

        @pl.when(i == 0)
        def _():
            small_ref[...] = jnp.zeros_like(small_ref)

        sh2, sc2, g2 = mod_ref[3:4, :], mod_ref[4:5, :], mod_ref[5:6, :]
        n2, nf = n2_ref[...], nf_ref[...]
        x1 = x1_ref[...]
        r2 = lax.rsqrt(jnp.mean(x1 * x1, axis=-1, keepdims=True) + EPS)
        x1h = x1 * r2
        w2 = n2 * (1.0 + sc2)
        h2b = (x1h * w2 + sh2).astype(BF16)
        h2_ref[...] = h2b
        a = _dot_nt(h2b, wg_ref[...])
        b = _dot_nt(h2b, wu_ref[...])
        sa, dsa = _silu_parts(a)
        hmb = (sa * b).astype(BF16)
        hm_ref[...] = hmb.T
        f = _dot(hmb, wd_ref[...])
        x2 = x1 + g2 * f
        r3 = lax.rsqrt(jnp.mean(x2 * x2, axis=-1, keepdims=True) + EPS)
        x2h = x2 * r3
        diff = x2h * nf - tgt_ref[...]
        small_ref[5:6, :] += jnp.sum(diff * diff, axis=0, keepdims=True)
        dout = diff * (1.0 / D)
        small_ref[0:1, :] += jnp.sum(dout * x2h, axis=0, keepdims=True)
        dyg = dout * nf
        dx2 = r3 * (dyg - x2h * jnp.mean(dyg * x2h, axis=-1, keepdims=True))
        small_ref[1:2, :] += jnp.sum(dx2 * f, axis=0, keepdims=True)
        dfb = (dx2 * g2).astype(BF16)
        df_ref[...] = dfb
        dhm = _dot_nt(dfb, wd_ref[...])
        dbb = (dhm * sa).astype(BF16)
        dab = (dhm * b * dsa).astype(BF16)
        da_ref[...] = dab.T
        db_ref[...] = dbb.T
        dh2 = _dot(dab, wg_ref[...]) + _dot(dbb, wu_ref[...])
        small_ref[2:3, :] += jnp.sum(dh2, axis=0, keepdims=True)
        dhx = dh2 * x1h
        small_ref[3:4, :] += jnp.sum(dhx, axis=0, keepdims=True) * n2
        small_ref[4:5, :] += jnp.sum(dhx, axis=0, keepdims=True) * (1.0 + sc2)
        dyg2 = dh2 * w2
        dx1_ref[...] = dx2 + r2 * (dyg2 - x1h * jnp.mean(dyg2 * x1h, axis=-1, keepdims=True))

    return pl.pallas_call(
        body, name="ffn", grid=(nt,),
        in_specs=[_rows(tm, D), _rows(tm, D), _whole((6, D)), _whole((1, D)), _whole((1, D)),
                  _whole((DFF, D)), _whole((DFF, D)), _whole((DFF, D))],
        out_specs=[_rows(tm, D), _rows(tm, D), _cols(DFF, tm), _cols(DFF, tm), _cols(DFF, tm), _rows(tm, D),
                   _acc((8, D))],
        out_shape=(jax.ShapeDtypeStruct((t_len, D), F32), jax.ShapeDtypeStruct((t_len, D), BF16),
                   jax.ShapeDtypeStruct((DFF, t_len), BF16), jax.ShapeDtypeStruct((DFF, t_len), BF16),
                   jax.ShapeDtypeStruct((DFF, t_len), BF16), jax.ShapeDtypeStruct((t_len, D), BF16),
                   jax.ShapeDtypeStruct((8, D), F32)),
        compiler_params=_cparams(1),
    )(x1, tgt, mod6, norm2, normf, wg, wu, wd)


def _mid_bwd(dx1, y, of, ob, gfb, mod6, wout, *, tm):
    t_len = dx1.shape[0]
    nt = t_len // tm

    def body(dx1_ref, y_ref, of_ref, ob_ref, gfb_ref, mod_ref, wout_ref,
             dy_ref, dya_ref, dgfb_ref, dof_ref, dob_ref, dg1_ref):
        i = pl.program_id(0)

        @pl.when(i == 0)
        def _():
            dg1_ref[...] = jnp.zeros_like(dg1_ref)

        dx1 = dx1_ref[...]
        dg1_ref[0:1, :] += jnp.sum(dx1 * y_ref[...].astype(F32), axis=0, keepdims=True)
        dyb = (dx1 * mod_ref[2:3, :]).astype(BF16)
        dy_ref[...] = dyb
        dycat = _dot_nt(dyb, wout_ref[...])
        dya_ref[...] = dycat[:, :AW]
        dyr = dycat[:, AW:]
        gfb = gfb_ref[...].astype(F32)
        for o_ref, do_ref, lo in ((of_ref, dof_ref, 0), (ob_ref, dob_ref, RW)):
            sg, dsg = _silu_parts(gfb[:, lo:lo + RW])
            o = o_ref[...].astype(F32)
            hn, rs = _headnorm(o)
            dgfb_ref[:, lo:lo + RW] = (dyr * hn * dsg).astype(BF16)
            dhn = dyr * sg
            for h in range(H):
                hs = slice(h * DH, (h + 1) * DH)
                oh, dh = hn[:, hs], dhn[:, hs]
                do_ref[:, hs] = (rs[h] * (dh - oh * jnp.mean(dh * oh, axis=-1, keepdims=True))).astype(BF16)

    return pl.pallas_call(
        body, name="mid_bwd", grid=(nt,),
        in_specs=[_rows(tm, D), _rows(tm, D), _rows(tm, RW), _rows(tm, RW), _rows(tm, 2 * RW),
                  _whole((6, D)), _whole((D, D))],
        out_specs=[_rows(tm, D), _rows(tm, AW), _rows(tm, 2 * RW), _rows(tm, RW), _rows(tm, RW), _acc((8, D))],
        out_shape=(jax.ShapeDtypeStruct((t_len, D), BF16), jax.ShapeDtypeStruct((t_len, AW), F32),
                   jax.ShapeDtypeStruct((t_len, 2 * RW), BF16), jax.ShapeDtypeStruct((t_len, RW), BF16),
                   jax.ShapeDtypeStruct((t_len, RW), BF16), jax.ShapeDtypeStruct((8, D), F32)),
        compiler_params=_cparams(1),
    )(dx1, y, of, ob, gfb, mod6, wout)


def _ret_bwd_dir(q_ref, k_ref, v_ref, do_ref, st_ref, dq_ref, dk_ref, dv_ref, ds_scr, dlg_scr, lg, forward, nc, row0):
    order = reversed(range(nc)) if forward else range(nc)
    order = list(order)
    for h in range(H):
        dm, xi, zc, dec, ic = _decay_consts(lg[h:h + 1, 0:1], forward)
        hs = slice(h * DH, (h + 1) * DH)
        if forward:
            w_qi, w_qc, w_ki, w_ks = ic, ic + 1.0, -ic, (CR - 1.0) - ic
        else:
            w_qi, w_qc, w_ki, w_ks = -ic, CR - ic, ic, ic
        acc = jnp.zeros((1, DH), F32)
        for c in order:
            rs = slice(c * CR, (c + 1) * CR)
            qc, kc, vc, doc = q_ref[rs, hs], k_ref[rs, hs], v_ref[rs, hs], do_ref[rs, hs]
            s = st_ref[c, h]
            dsn = ds_scr[h]
            sb, dsnb = s.astype(BF16), dsn.astype(BF16)
            qf, kf = qc.astype(F32), kc.astype(F32)
            a = (_dot_nt(qc, kc) * dm).astype(BF16)
            da = (_dot_nt(doc, vc) * dm).astype(BF16)
            xdo = (xi * doc.astype(F32)).astype(BF16)
            kz = (kf * zc).astype(BF16)
            vz = (vc.astype(F32) * zc).astype(BF16)
            dq_i = _dot(da, kc)
            dq_c = _dot_nt(xdo, sb)
            dk_i = _dot_tn(da, qc)
            dk_s = _dot_nt(vz, dsnb)
            dq_ref[rs, hs] = dq_i + dq_c
            dk_ref[rs, hs] = dk_i + dk_s
            dv_ref[rs, hs] = _dot_tn(a, doc) + _dot(kz, dsnb)
            rowterm = qf * (w_qi * dq_i + w_qc * dq_c) + kf * (w_ki * dk_i + w_ks * dk_s)
            acc = acc + jnp.sum(rowterm, axis=0, keepdims=True)
            acc = acc + (float(CR) * dec) * jnp.sum(s * dsn, axis=0, keepdims=True)
            ds_scr[h] = _dot_tn((xi * qf).astype(BF16), doc) + dec * dsn
        dlg_scr[row0 + h:row0 + h + 1, :] += acc


def _ret_bwd(q, k, v, dof, dob, sst, rst, rlf, rlb, *, tm):
    t_len = q.shape[0]
    nt, nc = t_len // tm, tm // CR

    def body(qf_ref, kf_ref, vf_ref, dof_ref, sst_ref, qb_ref, kb_ref, vb_ref, dob_ref, rst_ref, rlf_ref, rlb_ref,
             dqf_ref, dkf_ref, dvf_ref, dqb_ref, dkb_ref, dvb_ref, dscf_ref, dscb_ref, dlg_ref,
             dsf_scr, dsb_scr, dlg_scr):
        i = pl.program_id(0)

        @pl.when(i == 0)
        def _():
            dsf_scr[...] = jnp.zeros_like(dsf_scr)
            dsb_scr[...] = jnp.zeros_like(dsb_scr)
            dlg_scr[...] = jnp.zeros_like(dlg_scr)

        lgf = _log_sigmoid(rlf_ref[...])
        lgb = _log_sigmoid(rlb_ref[...])
        _ret_bwd_dir(qf_ref, kf_ref, vf_ref, dof_ref, sst_ref, dqf_ref, dkf_ref, dvf_ref, dsf_scr, dlg_scr, lgf, True, nc, 0)
        _ret_bwd_dir(qb_ref, kb_ref, vb_ref, dob_ref, rst_ref, dqb_ref, dkb_ref, dvb_ref, dsb_scr, dlg_scr, lgb, False, nc, H)

        @pl.when(i == nt - 1)
        def _():
            dscf_ref[...] = dsf_scr[...]
            dscb_ref[...] = dsb_scr[...]
            tot = jnp.broadcast_to(jnp.sum(dlg_scr[...], axis=1, keepdims=True), (8, 128))
            ri = lax.broadcasted_iota(jnp.int32, (8, 128), 0)
            li = lax.broadcasted_iota(jnp.int32, (8, 128), 1)
            dlg_ref[...] = jnp.zeros_like(dlg_ref)
            dlg_ref[0:1, 0:128] = jnp.sum(jnp.where(ri == li, tot, 0.0), axis=0, keepdims=True)
            dlg_ref[1:2, 0:128] = jnp.sum(jnp.where(ri - H == li, tot, 0.0), axis=0, keepdims=True)

    rr = functools.partial(_rows_rev, nt=nt)
    st_f = pl.BlockSpec((nc, H, DH, DH), lambda i: (nt - 1 - i, 0, 0, 0))
    st_b = pl.BlockSpec((nc, H, DH, DH), lambda i: (i, 0, 0, 0))
    tok = jax.ShapeDtypeStruct((t_len, RW), F32)
    st = jax.ShapeDtypeStruct((H, DH, DH), F32)
    return pl.pallas_call(
        body, name="ret_bwd", grid=(nt,),
        in_specs=[rr(tm, RW), rr(tm, RW), rr(tm, RW), rr(tm, RW), st_f,
                  _rows(tm, RW), _rows(tm, RW), _rows(tm, RW), _rows(tm, RW), st_b,
                  _whole((H, 128)), _whole((H, 128))],
        out_specs=[rr(tm, RW), rr(tm, RW), rr(tm, RW), _rows(tm, RW), _rows(tm, RW), _rows(tm, RW),
                   _acc((H, DH, DH)), _acc((H, DH, DH)), _acc((8, D))],
        out_shape=(tok, tok, tok, tok, tok, tok, st, st, jax.ShapeDtypeStruct((8, D), F32)),
        scratch_shapes=[pltpu.VMEM((H, DH, DH), F32), pltpu.VMEM((H, DH, DH), F32), pltpu.VMEM((8, 128), F32)],
        compiler_params=_cparams(1),
    )(q, k, v, dof, sst, q, k, v, dob, rst, rlf, rlb)


def _in_bwd(x, zuv, dya, dqf, dkf, dvf, dqb, dkb, dvb, dgfb, dx1, cos, sin, mod6, norm1, win, sgw, sgbt, sggain, *, tm):
    t_len = x.shape[0]
    nt, nc = t_len // tm, tm // C

    def body(x_ref, zuv_ref, dya_ref, dqf_ref, dkf_ref, dvf_ref, dqb_ref, dkb_ref, dvb_ref, dgfb_ref, dx1_ref,
             cos_ref, sin_ref, mod_ref, n1_ref, win_ref, sgw_ref, sgbt_ref, sgg_ref,
             gx_ref, dzt_ref, small_ref, dsgw_ref, dsgbt_ref, dz_ref):
        i = pl.program_id(0)

        @pl.when(i == 0)
        def _():
            small_ref[...] = jnp.zeros_like(small_ref)
            dsgw_ref[...] = jnp.zeros_like(dsgw_ref)
            dsgbt_ref[...] = jnp.zeros_like(dsgbt_ref)

        zuv = zuv_ref[...]
        _, (ua, dgu, dgv, mixed, vn_b, vah_l, rv_l) = _sg_forward(zuv[:, :AW], zuv[:, AW:], sgw_ref, sgbt_ref, sgg_ref, nc)
        dya = dya_ref[...]
        dz_ref[:, 0:AW] = (dya * mixed * dgu).astype(BF16)
        dmixed = dya * ua
        gain = sgg_ref[...]
        for g in range(G):
            gs = slice(g * DH, (g + 1) * DH)
            dmg = dmixed[:, gs]
            dmb = dmg.astype(BF16)
            wgt = sgw_ref[g].astype(BF16)
            dsw = jnp.zeros((C, C), F32)
            dsb = jnp.zeros((C, DH), F32)
            rows = []
            for c in range(nc):
                rs = slice(c * C, (c + 1) * C)
                dsw = dsw + _dot_nt(dmb[rs, :], vn_b[g][rs, :])
                dsb = dsb + dmg[rs, :]
                rows.append(_dot_tn(wgt, dmb[rs, :]))
            dsgw_ref[g] += dsw
            dsgbt_ref[g] += dsb
            dvn = jnp.concatenate(rows, axis=0) if nc > 1 else rows[0]
            vah, rv = vah_l[g], rv_l[g]
            small_ref[3:4, gs] += jnp.sum(dvn * vah, axis=0, keepdims=True)
            dyg = dvn * gain[:, gs]
            dva = rv * (dyg - vah * jnp.mean(dyg * vah, axis=-1, keepdims=True))
            dz_ref[:, AW + g * DH:AW + (g + 1) * DH] = (dva * dgv[:, gs]).astype(BF16)

        cos = jnp.tile(cos_ref[...], (1, H))
        nsins = -jnp.tile(sin_ref[...], (1, H))
        dz_ref[:, 1024:1536] = _rope(dqf_ref[...] + dqb_ref[...], cos, nsins).astype(BF16)
        dz_ref[:, 1536:2048] = (_rope(dkf_ref[...] + dkb_ref[...], cos, nsins) * KSCALE).astype(BF16)
        dz_ref[:, 2048:2560] = (dvf_ref[...] + dvb_ref[...]).astype(BF16)
        dz_ref[:, 2560:3584] = dgfb_ref[...]

        dz = dz_ref[...]
        dzt_ref[...] = dz.T
        dhx = _dot(dz, win_ref[...])
        x = x_ref[...]
        r = lax.rsqrt(jnp.mean(x * x, axis=-1, keepdims=True) + EPS)
        xh = x * r
        n1, sc1 = n1_ref[...], mod_ref[1:2, :]
        small_ref[0:1, :] += jnp.sum(dhx, axis=0, keepdims=True)
        dhxx = jnp.sum(dhx * xh, axis=0, keepdims=True)
        small_ref[1:2, :] += dhxx * n1
        small_ref[2:3, :] += dhxx * (1.0 + sc1)
        dyg = dhx * (n1 * (1.0 + sc1))
        gx_ref[...] = dx1_ref[...] + r * (dyg - xh * jnp.mean(dyg * xh, axis=-1, keepdims=True))

        @pl.when(i == nt - 1)
        def _():
            ri = lax.broadcasted_iota(jnp.int32, (C, DH), 0)
            li = lax.broadcasted_iota(jnp.int32, (C, DH), 1)
            for g in range(G):
                tot = jnp.broadcast_to(jnp.sum(dsgbt_ref[g], axis=1, keepdims=True), (C, DH))
                small_ref[4 + g:5 + g, 0:DH] = jnp.sum(jnp.where(ri == li, tot, 0.0), axis=0, keepdims=True)

    tok = functools.partial(_rows, tm)
    return pl.pallas_call(
        body, name="in_bwd", grid=(nt,),
        in_specs=[tok(D), tok(2 * AW), tok(AW), tok(RW), tok(RW), tok(RW), tok(RW), tok(RW), tok(RW),
                  tok(2 * RW), tok(D), tok(DH), tok(DH), _whole((6, D)), _whole((1, D)), _whole((INC, D)),
                  _whole((G, C, C)), _whole((G, C, 128)), _whole((1, AW))],
        out_specs=[tok(D), _cols(INC, tm), _acc((8, D)), _acc((G, C, C))],
        out_shape=(jax.ShapeDtypeStruct((t_len, D), F32), jax.ShapeDtypeStruct((INC, t_len), BF16),
                   jax.ShapeDtypeStruct((8, D), F32), jax.ShapeDtypeStruct((G, C, C), F32)),
        scratch_shapes=[pltpu.VMEM((G, C, 128), F32), pltpu.VMEM((tm, INC), BF16)],
        compiler_params=_cparams(1),
    )(x, zuv, dya, dqf, dkf, dvf, dqb, dkb, dvb, dgfb, dx1, cos, sin, mod6, norm1, win, sgw, sgbt, sggain)


def _tn_matmul(at, b, *, bm, bt, name, init=None, init_rows=None, after=(), cols=None):
    m, t_len = at.shape
    cj, n = (0, b.shape[1]) if cols is None else cols
    ni, ntt = m // bm, t_len // bt
    has_init = init is not None

    def body(*refs):
        o_ref, ob_ref = refs[-2:]
        a_ref, b_ref = refs[:2]
        init_ref = refs[2] if has_init else None
        i, t = pl.program_id(0), pl.program_id(1)

        @pl.when(t == 0)
        def _():
            o_ref[...] = jnp.zeros_like(o_ref)

        if has_init:
            for ib in range(ni):
                lo, hi = max(init_rows[0], ib * bm), min(init_rows[1], (ib + 1) * bm)
                if lo < hi:
                    @pl.when(jnp.logical_and(t == 0, i == ib))
                    def _(lo=lo, hi=hi, ib=ib):
                        o_ref[lo - ib * bm:hi - ib * bm, :] = init_ref[lo - init_rows[0]:hi - init_rows[0], :]

        o_ref[...] += _dot(a_ref[...], b_ref[...])

        @pl.when(t == ntt - 1)
        def _():
            ob_ref[...] = o_ref[...].astype(BF16)

    in_specs = [pl.BlockSpec((bm, bt), lambda i, t: (i, t)), pl.BlockSpec((bt, n), lambda i, t: (t, cj))]
    args = [at, b]
    if has_init:
        in_specs.append(pl.BlockSpec((init.shape[0], n), lambda i, t: (0, cj), pipeline_mode=pl.Buffered(1)))
        args.append(init)
    for arr in after:
        in_specs.append(pl.BlockSpec(memory_space=pl.ANY))
        args.append(arr)
    out_spec = pl.BlockSpec((bm, n), lambda i, t: (i, 0))
    return pl.pallas_call(
        body, name=name, grid=(ni, ntt),
        in_specs=in_specs,
        out_specs=[out_spec, out_spec],
        out_shape=(jax.ShapeDtypeStruct((m, n), F32), jax.ShapeDtypeStruct((m, n), BF16)),
        compiler_params=_cparams(2),
    )(*args)


def _ctx_bwd(ctx, hc, kvc, cmod6, norm1, win, rlf, rlb, dscf, dscb, dlg_in):
    lc = ctx.shape[0]

    def body(ctx_ref, hc_ref, kvc_ref, cmod_ref, n1_ref, win_ref, rlf_ref, rlb_ref, dscf_ref, dscb_ref, dlgin_ref,
             dwin_ref, small_ref, dlg_ref):
        k = kvc_ref[:, :RW]
        v = kvc_ref[:, RW:]
        t = lax.broadcasted_iota(jnp.int32, (lc, 1), 0).astype(F32)
        lgf = _log_sigmoid(rlf_ref[...])
        lgb = _log_sigmoid(rlb_ref[...])
        dks, dvs = [], []
        lane = lax.broadcasted_iota(jnp.int32, (1, 128), 1)
        row_f = jnp.zeros((1, 128), F32)
        row_b = jnp.zeros((1, 128), F32)
        for h in range(H):
            hs = slice(h * DH, (h + 1) * DH)
            wf = jnp.exp(lgf[h:h + 1, 0:1] * (lc - 1.0 - t))
            wb = jnp.exp(lgb[h:h + 1, 0:1] * t)
            kh, vhb = k[:, hs], v[:, hs].astype(BF16)
            dsf, dsb = dscf_ref[h].astype(BF16), dscb_ref[h].astype(BF16)
            dkf = wf * _dot_nt(vhb, dsf)
            dkb = wb * _dot_nt(vhb, dsb)
            dvs.append(_dot((kh * wf).astype(BF16), dsf) + _dot((kh * wb).astype(BF16), dsb))
            dks.append((dkf + dkb) * KSCALE)
            lf = jnp.sum((lc - 1.0 - t) * kh * dkf, axis=0, keepdims=True)
            lb = jnp.sum(t * kh * dkb, axis=0, keepdims=True)
            row_f = row_f + jnp.where(lane == h, jnp.sum(lf, axis=1, keepdims=True), 0.0)
            row_b = row_b + jnp.where(lane == h, jnp.sum(lb, axis=1, keepdims=True), 0.0)
        dlg_ref[...] = dlgin_ref[...]
        dlg_ref[0:1, 0:128] += row_f
        dlg_ref[1:2, 0:128] += row_b
        dkv = jnp.concatenate(dks + dvs, axis=1).astype(BF16)
        dhc = _dot(dkv, win_ref[KV_LO:KV_HI, :])
        dwin_ref[...] = _dot_tn(dkv, hc_ref[...])
        x = ctx_ref[...]
        r = lax.rsqrt(jnp.mean(x * x, axis=-1, keepdims=True) + EPS)
        xh = x * r
        small_ref[...] = jnp.zeros_like(small_ref)
        small_ref[0:1, :] = jnp.sum(dhc, axis=0, keepdims=True)
        dhxx = jnp.sum(dhc * xh, axis=0, keepdims=True)
        small_ref[1:2, :] = dhxx * n1_ref[...]
        small_ref[2:3, :] = dhxx * (1.0 + cmod_ref[1:2, :])

    return pl.pallas_call(
        body, name="ctx_bwd",
        out_shape=(jax.ShapeDtypeStruct((2 * RW, D), F32), jax.ShapeDtypeStruct((8, D), F32),
                   jax.ShapeDtypeStruct((8, D), F32)),
        compiler_params=_cparams(),
    )(ctx, hc, kvc, cmod6, norm1, win, rlf, rlb, dscf, dscb, dlg_in)


def _adam_math(w, g, m, v):
    m = ADAM_B1 * m + (1.0 - ADAM_B1) * g
    v = ADAM_B2 * v + (1.0 - ADAM_B2) * (g * g)
    m_hat = m / (1.0 - ADAM_B1 ** ADAM_STEP)
    v_hat = v / (1.0 - ADAM_B2 ** ADAM_STEP)
    delta = -ADAM_LR * (m_hat / (jnp.sqrt(v_hat) + ADAM_EPS) + ADAM_WD * w)
    return delta, m, v


def _place():
    x, y, c = lax.axis_index("x"), lax.axis_index("y"), lax.axis_index("c")
    return x, y, c, 4 * x + 2 * y + c


def _flip(x, y, c, k):
    px = 1 - x if (k >> 2) & 1 else x
    py = 1 - y if (k >> 1) & 1 else y
    pc = 1 - c if k & 1 else c
    return (px, py, pc), 4 * px + 2 * py + pc


def _gather_copy(buf_ref, send_sems, recv_sems, sem0, k, mine):
    x, y, c, me = _place()
    dev, idx = _flip(x, y, c, k)
    blk = me if mine else idx
    return pltpu.make_async_remote_copy(
        src_ref=buf_ref.at[blk], dst_ref=buf_ref.at[blk],
        send_sem=send_sems.at[sem0 + k - 1], recv_sem=recv_sems.at[sem0 + k - 1],
        device_id=dev, device_id_type=MESH)


def _entry_gather(c_row, cctx_row, wmod, bmod, shards):
    n = len(shards)
    ncol = wmod.shape[1]

    def body(*refs):
        c_ref, cctx_ref, wmod_ref, bmod_ref = refs[:4]
        in_refs = refs[4:4 + n]
        mod_ref, cs_ref = refs[4 + n:6 + n]
        out_refs = refs[6 + n:6 + 2 * n]
        cbuf, pbuf = refs[6 + 2 * n:8 + 2 * n]
        cast_refs = refs[8 + 2 * n:8 + 3 * n]
        small_send, small_recv, send_sems, recv_sems, local_sems = refs[8 + 3 * n:]
        x, y, c, me = _place()
        sib = (x, y, 1 - c)
        chips = [(1 - x, y), (x, 1 - y), (1 - x, 1 - y)]

        cbuf[me] = jnp.broadcast_to(c_ref[...], (8, D))
        small = [_gather_copy(cbuf, small_send, small_recv, 0, k, True) for k in range(1, NDEV)]
        for cp in small:
            cp.start()

        def blk(px, py, pc):
            return 4 * px + 2 * py + pc

        def copy(w, k, block, to, src=None):
            dst = out_refs[w].at[block]
            return pltpu.make_async_remote_copy(
                src_ref=dst if src is None else src, dst_ref=dst,
                send_sem=send_sems.at[w, k], recv_sem=recv_sems.at[w, k], device_id=to, device_id_type=MESH)

        first, passed, mine = [], [], []
        for w in range(n):
            cast_refs[w][...] = in_refs[w][...].astype(BF16)
            m = pltpu.make_async_copy(cast_refs[w], out_refs[w].at[me], local_sems.at[w])
            m.start()
            mine.append(m)
            cps = [copy(w, 0, me, sib, src=cast_refs[w])]
            cps += [copy(w, 1 + j, me, (*chip, c), src=cast_refs[w]) for j, chip in enumerate(chips)]
            for cp in cps:
                cp.start()
            first += cps

        for k in range(1, NDEV):
            _gather_copy(cbuf, small_send, small_recv, 0, k, False).wait_recv()
        row = lax.broadcasted_iota(jnp.int32, (16, D), 0)
        call = jnp.where(row == 8, jnp.broadcast_to(cctx_ref[...], (16, D)), 0.0)
        for d in range(NDEV):
            call = jnp.where(row == d, jnp.concatenate([cbuf[d], cbuf[d]], axis=0), call)
        cs = call * _sigmoid(call)
        cs_ref[...] = cs
        pbuf[me] = _dot(cs.astype(BF16), wmod_ref[...].astype(BF16))
        small2 = [_gather_copy(pbuf, small_send, small_recv, NDEV - 1, k, True) for k in range(1, NDEV)]
        for cp in small2:
            cp.start()

        for w in range(n):
            for j, chip in enumerate(chips):
                b = blk(*chip, c)
                copy(w, 1 + j, b, (x, y, c)).wait_recv()
                fw = copy(w, 4 + j, b, sib)
                fw.start()
                passed.append(fw)
        for w in range(n):
            copy(w, 0, blk(x, y, 1 - c), (x, y, c)).wait_recv()
            for j, chip in enumerate(chips):
                copy(w, 4 + j, blk(*chip, 1 - c), (x, y, c)).wait_recv()

        for k in range(1, NDEV):
            _gather_copy(pbuf, small_send, small_recv, NDEV - 1, k, False).wait_recv()
        for d in range(NDEV):
            mod_ref[:, d * ncol:(d + 1) * ncol] = pbuf[d] + bmod_ref[:, d * ncol:(d + 1) * ncol]
        for cp in small + small2 + first + passed:
            cp.wait_send()
        for m in mine:
            m.wait()

    vm = pl.BlockSpec(memory_space=pltpu.VMEM)
    hbm = pl.BlockSpec(memory_space=pltpu.HBM)
    return pl.pallas_call(
        body, name="entry_gather",
        in_specs=[vm] * (4 + n), out_specs=[vm, vm] + [hbm] * n,
        out_shape=(jax.ShapeDtypeStruct((16, 6 * D), F32), jax.ShapeDtypeStruct((16, D), F32))
        + tuple(jax.ShapeDtypeStruct((NDEV,) + s.shape, BF16) for s in shards),
        scratch_shapes=[pltpu.VMEM((NDEV, 8, D), F32), pltpu.VMEM((NDEV, 16, ncol), F32)]
        + [pltpu.VMEM(s.shape, BF16) for s in shards]
        + [pltpu.SemaphoreType.DMA((2 * (NDEV - 1),)), pltpu.SemaphoreType.DMA((2 * (NDEV - 1),)),
           pltpu.SemaphoreType.DMA((n, 7)), pltpu.SemaphoreType.DMA((n, 7)), pltpu.SemaphoreType.DMA((n,))],
        compiler_params=_cparams(),
    )(c_row, cctx_row, wmod, bmod, *shards)


_HBM = pl.BlockSpec(memory_space=pltpu.HBM)
_SEMS = pl.BlockSpec(memory_space=pltpu.SEMAPHORE)
_EFFECT = pltpu.SideEffectType.DATAFLOW_SIDE_EFFECTING


def _exchange_copy(src_refs, land_refs, send_sems, recv_sems, w, k, scatter, landing_slot_is_mine):
    x, y, c, me = _place()
    dev, pidx = _flip(x, y, c, k)
    src = src_refs[w].at[pidx] if scatter else src_refs[w]
    slot = me if landing_slot_is_mine else pidx
    return pltpu.make_async_remote_copy(
        src_ref=src, dst_ref=land_refs[w].at[slot],
        send_sem=send_sems.at[w * (NDEV - 1) + k - 1], recv_sem=recv_sems.at[w * (NDEV - 1) + k - 1],
        device_id=dev, device_id_type=MESH)


def _exchange_start(srcs, *, scatter, name):
    n = len(srcs)
    lands = [lax.empty((NDEV,) + tuple(s.shape[-2:]), s.dtype) for s in srcs]

    def body(*refs):
        src_refs, land_refs = refs[:n], refs[n:2 * n]
        send_sems, recv_sems = refs[2 * n], refs[2 * n + 1]
        token = refs[-1]
        for w in range(n):
            for k in range(1, NDEV):
                _exchange_copy(src_refs, land_refs, send_sems, recv_sems, w, k, scatter, True).start()
        token[...] = jnp.zeros_like(token)

    arrs = list(srcs) + lands
    out_shape = ([pltpu.SemaphoreType.DMA((n * (NDEV - 1),)), pltpu.SemaphoreType.DMA((n * (NDEV - 1),))]
                 + [pltpu.HBM(a.shape, a.dtype) for a in arrs] + [jax.ShapeDtypeStruct((8, 128), F32)])
    res = pl.pallas_call(
        body, name=name, out_shape=tuple(out_shape),
        in_specs=[_HBM] * (2 * n),
        out_specs=tuple([_SEMS, _SEMS] + [_HBM] * (2 * n) + [pl.BlockSpec(memory_space=pltpu.VMEM)]),
        input_output_aliases={i: 2 + i for i in range(2 * n)},
        compiler_params=pltpu.CompilerParams(has_side_effects=_EFFECT),
    )(*[pltpu.with_memory_space_constraint(a, pltpu.HBM) for a in arrs])
    return res[:-1], res[-1]


def _exchange_wait(handles, after, *, scatter, name):
    n = (len(handles) - 2) // 2
    na = len(after)

    def body(*refs):
        src_refs, land_refs = refs[:n], refs[n:2 * n]
        send_sems, recv_sems = refs[2 * n], refs[2 * n + 1]
        for w in range(n):
            for k in range(1, NDEV):
                cp = _exchange_copy(src_refs, land_refs, send_sems, recv_sems, w, k, scatter, False)
                cp.wait_send()
                cp.wait_recv()

    arrs = handles[2:]
    res = pl.pallas_call(
        body, name=name, out_shape=tuple(pltpu.HBM(a.shape, a.dtype) for a in arrs),
        in_specs=[_HBM] * (2 * n) + [_SEMS, _SEMS] + [_HBM] * na,
        out_specs=tuple([_HBM] * (2 * n)),
        input_output_aliases={i: i for i in range(2 * n)},
        compiler_params=pltpu.CompilerParams(has_side_effects=_EFFECT),
    )(*arrs, handles[0], handles[1], *[pltpu.with_memory_space_constraint(a, pltpu.HBM) for a in after])
    return res[:n], res[n:]


_SAME_CORE = (2, 4, 6)


def _gather2_copy(src_ref, land_ref, send_sems, recv_sems, sem, k, block):
    x, y, c, _ = _place()
    dev, _ = _flip(x, y, c, k)
    dst = land_ref.at[block]
    return pltpu.make_async_remote_copy(
        src_ref=dst if src_ref is None else src_ref, dst_ref=dst,
        send_sem=send_sems.at[sem], recv_sem=recv_sems.at[sem], device_id=dev, device_id_type=MESH)


def _split_call(body, name, arrs, n_sems, sems=None, after=(), token=False):
    na = len(arrs)
    out_shape, out_specs = [], []
    if n_sems is not None:
        out_shape += [pltpu.SemaphoreType.DMA((n_sems,)), pltpu.SemaphoreType.DMA((n_sems,))]
        out_specs += [_SEMS, _SEMS]
    first = len(out_shape)
    out_shape += [pltpu.HBM(a.shape, a.dtype) for a in arrs]
    out_specs += [_HBM] * na
    if token:
        out_shape.append(jax.ShapeDtypeStruct((8, 128), F32))
        out_specs.append(pl.BlockSpec(memory_space=pltpu.VMEM))
    in_specs = [_HBM] * na + ([_SEMS, _SEMS] if sems is not None else []) + [_HBM] * len(after)
    args = [pltpu.with_memory_space_constraint(a, pltpu.HBM) for a in arrs] + list(sems or ()) \
        + [pltpu.with_memory_space_constraint(a, pltpu.HBM) for a in after]
    return pl.pallas_call(
        body, name=name, out_shape=tuple(out_shape), in_specs=in_specs, out_specs=tuple(out_specs),
        input_output_aliases={i: first + i for i in range(na)},
        compiler_params=pltpu.CompilerParams(has_side_effects=_EFFECT))(*args)


def _gather2_start(srcs, *, name):
    n = len(srcs)
    lands = [lax.empty((NDEV,) + tuple(s.shape), s.dtype) for s in srcs]

    def body(*refs):
        src_refs, land_refs = refs[:n], refs[n:2 * n]
        send_sems, recv_sems = refs[2 * n], refs[2 * n + 1]
        _, _, _, me = _place()
        for w in range(n):
            for slot, k in enumerate((1,) + _SAME_CORE):
                _gather2_copy(src_refs[w], land_refs[w], send_sems, recv_sems, 4 * w + slot, k, me).start()
        refs[-1][...] = jnp.zeros_like(refs[-1])

    res = _split_call(body, name, list(srcs) + lands, 4 * n, token=True)
    return res[:2], res[2:-1], res[-1]


def _gather2_arrived(sems, arrs, after, *, name):
    n = len(arrs) // 2

    def body(*refs):
        src_refs, land_refs = refs[:n], refs[n:2 * n]
        send_sems, recv_sems = refs[2 * n], refs[2 * n + 1]
        x, y, c, me = _place()
        for w in range(n):
            for slot, k in enumerate((1,) + _SAME_CORE):
                _, pidx = _flip(x, y, c, k)
                _gather2_copy(src_refs[w], land_refs[w], send_sems, recv_sems, 4 * w + slot, k, me).wait_send()
                _gather2_copy(None, land_refs[w], send_sems, recv_sems, 4 * w + slot, k, pidx).wait_recv()

    return _split_call(body, name, arrs, None, sems=sems, after=after)


def _gather2_forward(lands, *, name):
    n = len(lands)

    def body(*refs):
        land_refs = refs[:n]
        send_sems, recv_sems = refs[n], refs[n + 1]
        x, y, c, _ = _place()
        for w in range(n):
            for j, k in enumerate(_SAME_CORE):
                _, pidx = _flip(x, y, c, k)
                _gather2_copy(None, land_refs[w], send_sems, recv_sems, 3 * w + j, 1, pidx).start()
        refs[-1][...] = jnp.zeros_like(refs[-1])

    res = _split_call(body, name, list(lands), 3 * n, token=True)
    return res[:2], res[2:-1], res[-1]


def _gather2_wait(sems, lands, after, *, name):
    n = len(lands)

    def body(*refs):
        land_refs = refs[:n]
        send_sems, recv_sems = refs[n], refs[n + 1]
        x, y, c, _ = _place()
        for w in range(n):
            for j, k in enumerate(_SAME_CORE):
                _, mine = _flip(x, y, c, k)
                _, theirs = _flip(x, y, c, k ^ 1)
                _gather2_copy(None, land_refs[w], send_sems, recv_sems, 3 * w + j, 1, mine).wait_send()
                _gather2_copy(None, land_refs[w], send_sems, recv_sems, 3 * w + j, 1, theirs).wait_recv()

    return _split_call(body, name, list(lands), None, sems=sems, after=after)


def _cast_bf16(arrs, *, name, after=()):
    n = len(arrs)

    def body(*refs):
        for i_ref, o_ref in zip(refs[:n], refs[n + len(after):]):
            o_ref[...] = i_ref[...].astype(BF16)

    return pl.pallas_call(
        body, name=name, out_shape=tuple(jax.ShapeDtypeStruct(a.shape, BF16) for a in arrs),
        in_specs=[pl.BlockSpec(memory_space=pltpu.VMEM)] * n + [pl.BlockSpec(memory_space=pl.ANY)] * len(after),
        compiler_params=_cparams())(*arrs, *after)


def _rs_adam(me_arr, fulls, lands, w, m, v, *, name):
    rows = lands[0].shape[1]
    k = len(fulls)

    def body(me_ref, *refs):
        own_refs, land_refs = refs[:k], refs[k:2 * k]
        w_ref, m_ref, v_ref, g_ref, d_ref, mo_ref, vo_ref = refs[2 * k:]
        me = me_ref[0]
        parts = []
        for own_ref, land_ref in zip(own_refs, land_refs):
            acc = jnp.zeros(own_ref.shape, F32)
            for p in range(NDEV):
                acc = acc + jnp.where(p == me, own_ref[...], land_ref[p].astype(F32))
            parts.append(acc)
        acc = parts[0] if k == 1 else jnp.concatenate(parts, axis=1)
        g_ref[...] = acc
        d_ref[...], mo_ref[...], vo_ref[...] = _adam_math(w_ref[...], acc, m_ref[...], v_ref[...])

    sh = jax.ShapeDtypeStruct((rows, D), F32)
    whole2 = pl.BlockSpec((rows, D), lambda i, me: (0, 0))
    grid_spec = pltpu.PrefetchScalarGridSpec(
        num_scalar_prefetch=1, grid=(1,),
        in_specs=[pl.BlockSpec((rows, f.shape[1]), lambda i, me: (me[0], 0)) for f in fulls]
        + [pl.BlockSpec((NDEV, rows, l.shape[2]), lambda i, me: (0, 0, 0)) for l in lands]
        + [whole2, whole2, whole2],
        out_specs=[whole2] * 4)
    return pl.pallas_call(
        body, name=name, out_shape=(sh, sh, sh, sh), grid_spec=grid_spec, compiler_params=_cparams(1),
    )(me_arr, *fulls, *lands, w, m, v)


ROW_F, ROW_I, ROW_C, ROW_G1, ROW_LG = 0, 8, 16, 24, 32
PACK_ROWS = 40


def _small_sum(me_arr, pack, pack_land, sgw, sgw_land, wmod):
    ncol = wmod.shape[1]

    def body(me_ref, pack_ref, pland_ref, sgw_ref, sland_ref, wmod_ref, sum_ref, all_ref, sgw_sum_ref, part_ref):
        me = me_ref[0]
        acc = jnp.zeros((PACK_ROWS, D), F32)
        acc_w = jnp.zeros(sgw_ref.shape, F32)
        for p in range(NDEV):
            rows = jnp.where(p == me, pack_ref[...], pland_ref[p])
            all_ref[p] = rows
            acc = acc + rows
            acc_w = acc_w + jnp.where(p == me, sgw_ref[...], sland_ref[p])
        sum_ref[...] = acc
        sgw_sum_ref[...] = acc_w
        zero = jnp.zeros((1, D), F32)
        dcmod = jnp.concatenate([acc[ROW_C:ROW_C + 1], acc[ROW_C + 1:ROW_C + 2], zero, zero, zero, zero], axis=1)
        mine = jnp.zeros((1, ncol), F32)
        for d in range(NDEV):
            mine = mine + jnp.where(d == me, dcmod[:, d * ncol:(d + 1) * ncol], 0.0)
        part_ref[...] = _dot_nt(jnp.broadcast_to(mine, (8, ncol)).astype(BF16), wmod_ref[...].astype(BF16))

    vm = pl.BlockSpec(memory_space=pltpu.VMEM)
    return pl.pallas_call(
        body, name="small_sum",
        out_shape=(jax.ShapeDtypeStruct((PACK_ROWS, D), F32), jax.ShapeDtypeStruct((NDEV, PACK_ROWS, D), F32),
                   jax.ShapeDtypeStruct(sgw.shape, F32), jax.ShapeDtypeStruct((8, D), F32)),
        in_specs=[pl.BlockSpec(memory_space=pltpu.SMEM)] + [vm] * 5,
        compiler_params=_cparams(),
    )(me_arr, pack, pack_land, sgw, sgw_land, wmod)


def _cctx_final(me_arr, part, part_land, cctx_row, m_row, v_row):
    def body(me_ref, part_ref, land_ref, c_ref, m_ref, v_ref, g_ref, d_ref, mo_ref, vo_ref):
        me = me_ref[0]
        tot = jnp.zeros((8, D), F32)
        for p in range(NDEV):
            tot = tot + jnp.where(p == me, part_ref[...], land_ref[p])
        cc = c_ref[...]
        _, dsilu = _silu_parts(cc)
        g = tot[0:1, :] * dsilu
        g_ref[...] = g
        d_ref[...], mo_ref[...], vo_ref[...] = _adam_math(cc, g, m_ref[...], v_ref[...])

    row = jax.ShapeDtypeStruct((1, D), F32)
    vm = pl.BlockSpec(memory_space=pltpu.VMEM)
    return pl.pallas_call(
        body, name="cctx_final", out_shape=(row, row, row, row),
        in_specs=[pl.BlockSpec(memory_space=pltpu.SMEM)] + [vm] * 5,
        compiler_params=_cparams(),
    )(me_arr, part, part_land, cctx_row, m_row, v_row)


def _adam_small(s, sgw_g, params):
    names = ["norm1", "norm2", "norm_f", "sg_gain", "sg_b", "ret_logit_f", "ret_logit_b", "b_mod", "sg_w"]
    flat = [a for n in names for a in params[n]]

    def body(*refs):
        s_ref, sgw_ref = refs[0], refs[1]
        p_refs = refs[2:2 + 3 * len(names)]
        o_refs = refs[2 + 3 * len(names):]
        loss_ref = o_refs[-1]

        def row(r):
            return s_ref[r:r + 1, :]

        grads = {
            "norm1": row(ROW_I + 2) + row(ROW_C + 2),
            "norm2": row(ROW_F + 4),
            "norm_f": row(ROW_F + 0),
            "sg_gain": s_ref[ROW_I + 3:ROW_I + 4, 0:AW],
            "sg_b": s_ref[ROW_I + 4:ROW_I + 8, 0:DH],
            "sg_w": sgw_ref[...],
        }
        for j, n in enumerate(names):
            w_ref, m_ref, v_ref = p_refs[3 * j:3 * j + 3]
            g_ref, d_ref, mo_ref, vo_ref = o_refs[4 * j:4 * j + 4]
            w = w_ref[...]
            if n in ("ret_logit_f", "ret_logit_b"):
                r = ROW_LG + (0 if n == "ret_logit_f" else 1)
                g = s_ref[r:r + 1, 0:H] * _sigmoid(-w)
            elif n == "b_mod":
                rows = [row(ROW_I + 0) + row(ROW_C + 0), row(ROW_I + 1) + row(ROW_C + 1), row(ROW_G1),
                        row(ROW_F + 2), row(ROW_F + 3), row(ROW_F + 1)]
                g = jnp.concatenate(rows, axis=1)
            else:
                g = grads[n]
            g_ref[...] = g
            d_ref[...], mo_ref[...], vo_ref[...] = _adam_math(w, g, m_ref[...], v_ref[...])
        loss_ref[...] = jnp.full((1, 1), 0.5 / D, F32) * jnp.sum(row(ROW_F + 5), axis=1, keepdims=True)

    out_shape = []
    for n in names:
        w = params[n][0]
        out_shape += [jax.ShapeDtypeStruct(w.shape, F32)] * 4
    out_shape.append(jax.ShapeDtypeStruct((1, 1), F32))
    outs = pl.pallas_call(body, name="adam_small", out_shape=tuple(out_shape), compiler_params=_cparams())(
        s, sgw_g, *flat)
    return {n: tuple(outs[4 * j:4 * j + 4]) for j, n in enumerate(names)}, outs[-1]


def _wmod_grad(cs_all, dmod_cols, wmod, m, v):
    ncol = wmod.shape[1]

    def body(cs_ref, dm_ref, w_ref, m_ref, v_ref, g_ref, d_ref, mo_ref, vo_ref):
        g = _dot_tn(cs_ref[...].astype(BF16), dm_ref[...].astype(BF16))
        g_ref[...] = g
        d_ref[...], mo_ref[...], vo_ref[...] = _adam_math(w_ref[...], g, m_ref[...], v_ref[...])

    sh = jax.ShapeDtypeStruct((D, ncol), F32)
    return pl.pallas_call(
        body, name="wmod_grad", out_shape=(sh, sh, sh, sh),
        compiler_params=_cparams(),
    )(cs_all, dmod_cols, wmod, m, v)


def _rope_tables(t_len):
    n_freq = DH // 4
    inv = (ROPE_BASE ** (-np.arange(n_freq, dtype=np.float32) / n_freq)).astype(np.float32)
    tok = np.arange(t_len)
    rows = (tok // GRID_W).astype(np.float32)
    cols = (tok % GRID_W).astype(np.float32)
    ang_r = rows[:, None] * inv[None, :]
    ang_c = cols[:, None] * inv[None, :]
    cos = np.concatenate([np.cos(ang_r)] * 2 + [np.cos(ang_c)] * 2, axis=1).astype(np.float32)
    sin = np.concatenate([-np.sin(ang_r), np.sin(ang_r), -np.sin(ang_c), np.sin(ang_c)], axis=1).astype(np.float32)
    return jnp.asarray(cos), jnp.asarray(sin)


def _local_step(x, tgt, ctx, mod6, cmod6, norm1, norm2, normf, win, wout, ffn_weights, sgw, sgb, sggain, rlf, rlb,
                *, tm_a, tm_b, tm_f, tm_m, tm_r, tm_i, bt_w, after_first_grads=None, small_path=None,
                after_in_half=None):
    t_len = x.shape[0]
    cos, sin = _rope_tables(t_len)
    sgbt = jnp.broadcast_to(sgb[:, :, None], (G, C, 128))
    rlf = jnp.broadcast_to(rlf.reshape(H, 1), (H, 128))
    rlb = jnp.broadcast_to(rlb.reshape(H, 1), (H, 128))
    hc, kvc, scf, scb = _ctx_fwd(ctx, cmod6, norm1, win, rlf, rlb)
    hx, zuv, q, k, v, gfb, of, ya, sst = _fwd_a(x, mod6, norm1, win, sgw, sgbt, sggain, cos, sin, rlf, scf, tm=tm_a)
    wout = wout(hx) if callable(wout) else wout
    ob, ycat, y, x1, rst = _fwd_b(q, k, v, of, gfb, ya, x, mod6, wout, rlb, scb, tm=tm_b)
    wg, wu, wd = ffn_weights(x1) if callable(ffn_weights) else ffn_weights
    dx1, h2, da, db, hm, df, small_f = _ffn(x1, tgt, mod6, norm2, normf, wg, wu, wd, tm=tm_f)
    bt2 = min(2 * bt_w, t_len)
    d_wd, d_wd_b = _tn_matmul(hm, df, bm=DFF // 2, bt=bt2, name="dw_down")
    d_wg, d_wg_b = _tn_matmul(da, h2, bm=DFF // 2, bt=bt2, name="dw_gate")
    d_wu, d_wu_b = _tn_matmul(db, h2, bm=DFF // 2, bt=bt2, name="dw_up")
    dy, dya, dgfb, dof, dob, dg1 = _mid_bwd(dx1, y, of, ob, gfb, mod6, wout, tm=tm_m)
    d_wo, d_wo_b = _tn_matmul(ycat, dy, bm=D, bt=bt2, name="dw_out")
    if after_first_grads is not None:
        rlf = rlf + after_first_grads((d_wd_b, d_wg_b, d_wu_b, d_wo_b))
    dqf, dkf, dvf, dqb, dkb, dvb, dscf, dscb, dlg = _ret_bwd(q, k, v, dof, dob, sst, rst, rlf, rlb, tm=tm_r)
    gx, dz, small_i, dsgw = _in_bwd(x, zuv, dya, dqf, dkf, dvf, dqb, dkb, dvb, dgfb, dx1, cos, sin,
                                    mod6, norm1, win, sgw, sgbt, sggain, tm=tm_i)
    dwin_c, small_c, dlg = _ctx_bwd(ctx, hc, kvc, cmod6, norm1, win, rlf, rlb, dscf, dscb, dlg)
    pack = jnp.concatenate([small_f, small_i, small_c, dg1, dlg], axis=0)
    after = small_path(pack, dsgw) if small_path is not None else ()
    halves = []
    for j in range(2):
        halves.append(_tn_matmul(dz, hx, bm=INC // 2, bt=bt2, name="dw_in_%d" % j, init=dwin_c,
                                 init_rows=(KV_LO, KV_HI), after=after, cols=(j, D // 2)))
        if after_in_half is not None:
            after = after_in_half(j, halves[-1][1])
    grads = {"w_in": halves, "w_out": (d_wo, d_wo_b), "w_gate": (d_wg, d_wg_b), "w_up": (d_wu, d_wu_b),
             "w_down": (d_wd, d_wd_b)}
    return gx, grads, pack, dsgw


def kernel(x, c, ctx, c_ctx, w_mod, b_mod, norm1, w_in, sg_gain, sg_w, sg_b, ret_logit_f, ret_logit_b, w_out, norm2, w_gate, w_up, w_down, norm_f, loss_target, m_c_ctx, m_w_mod, m_b_mod, m_norm1, m_w_in, m_sg_gain, m_sg_w, m_sg_b, m_ret_logit_f, m_ret_logit_b, m_w_out, m_norm2, m_w_gate, m_w_up, m_w_down, m_norm_f, v_c_ctx, v_w_mod, v_b_mod, v_norm1, v_w_in, v_sg_gain, v_sg_w, v_sg_b, v_ret_logit_f, v_ret_logit_b, v_w_out, v_norm2, v_w_gate, v_w_up, v_w_down, v_norm_f):
    t_len = x.shape[1]
    tm = min(512, t_len)
    me = 4 * lax.axis_index("x") + 2 * lax.axis_index("y") + lax.axis_index("c")
    tr = lambda a: jnp.transpose(a[0])

    cctx_row = c_ctx.reshape(1, D)
    mod_all, cs_all, g_in = _entry_gather(c, cctx_row, w_mod[0], b_mod, [tr(w_in)])
    mod6 = lax.dynamic_slice(mod_all, (me, 0), (1, 6 * D)).reshape(6, D)
    cmod6 = mod_all[8].reshape(6, D)
    win_t = g_in.reshape(INC, D)

    (out_shard,) = _cast_bf16([w_out[0]], name="cast_out", after=[g_in])
    h_out, tok_out = _exchange_start([out_shard], scatter=False, name="wout_start")
    ffn_shards = _cast_bf16([tr(w_gate), tr(w_up), w_down[0]], name="cast_ffn", after=[h_out[2]])
    sems_ffn, arrs_ffn, tok = _gather2_start(ffn_shards, name="wffn_start")
    mod6 = mod6 + (tok_out[0, 0] + tok[0, 0])
    ffn = {}

    def place_own(own, lands, rows):
        return [lax.dynamic_update_slice(l, o[None], (me, 0, 0)).reshape(rows, D) for o, l in zip(own, lands)]

    def wout(after):
        own, lands = _exchange_wait(h_out, [after], scatter=False, name="wout_wait")
        arrs = _gather2_arrived(sems_ffn, arrs_ffn, [after], name="wffn_arrived")
        ffn["own"] = arrs[:3]
        ffn["sems"], ffn["lands"], tok_fwd = _gather2_forward(arrs[3:], name="wffn_forward")
        return place_own(own, lands, D)[0] + tok_fwd[0, 0].astype(BF16)

    def ffn_weights(after):
        lands = _gather2_wait(ffn["sems"], ffn["lands"], [after], name="wffn_wait")
        return place_own(ffn["own"], lands, DFF)

    rs, outs = {}, {}

    def after_first_grads(bf):
        rs["first"], tok_first = _exchange_start([b.reshape(NDEV, b.shape[0] // NDEV, D) for b in bf], scatter=True,
                                                 name="rs_first_start")
        return tok_first[0, 0]

    def small_path(pack, dsgw):
        rs["small"], _ = _exchange_start([pack, dsgw.reshape(G * C, C)], scatter=False, name="small_start")
        return [rs["small"][2], rs["small"][3]]

    def after_in_half(j, half_bf16):
        rs["in%d" % j], _ = _exchange_start([half_bf16.reshape(NDEV, INC // NDEV, D // 2)], scatter=True,
                                            name="rs_in%d_start" % j)
        return [rs["in%d" % j][2]]

    gx, grads, pack, dsgw = _local_step(
        x[0], loss_target[0], ctx[0], mod6, cmod6, norm1, norm2[0:1], norm_f.reshape(1, D), win_t, wout, ffn_weights,
        sg_w[0], sg_b[0], sg_gain, ret_logit_f, ret_logit_b,
        tm_a=tm, tm_b=tm, tm_f=min(256, t_len), tm_m=tm, tm_r=tm, tm_i=min(256, t_len), bt_w=min(1024, t_len),
        after_first_grads=after_first_grads, small_path=small_path, after_in_half=after_in_half)

    me_arr = me.reshape(1).astype(jnp.int32)
    (pack_own, sgw_own), (pack_land, sgw_land) = _exchange_wait(rs["small"], [rs["in1"][2]], scatter=False,
                                                               name="small_wait")
    psum_rows, pall, sgw_sum, part = _small_sum(me_arr, pack_own, pack_land, sgw_own, sgw_land, w_mod[0])
    h_cc, _ = _exchange_start([part], scatter=False, name="cctx_start")

    dmod_rows = (ROW_I + 0, ROW_I + 1, ROW_G1, ROW_F + 2, ROW_F + 3, ROW_F + 1)
    dmod_all = jnp.concatenate([pall[:, r, :] for r in dmod_rows], axis=1)
    dcmod = jnp.concatenate([psum_rows[ROW_C + 0:ROW_C + 1], psum_rows[ROW_C + 1:ROW_C + 2],
                             jnp.zeros((1, 4 * D), F32)], axis=1)
    dmod_mat = jnp.concatenate([dmod_all, jnp.pad(dcmod, ((0, 7), (0, 0)))], axis=0)
    ncol = 6 * D // NDEV
    dmod_cols = lax.dynamic_slice(dmod_mat, (0, me * ncol), (16, ncol))
    g_wm, d_wm, m_wm, v_wm = _wmod_grad(cs_all, dmod_cols, w_mod[0], m_w_mod[0], v_w_mod[0])
    outs["w_mod"] = (g_wm[None], d_wm[None], m_wm[None], v_wm[None])
    small_params = {
        "norm1": (norm1, m_norm1, v_norm1), "norm2": (norm2, m_norm2, v_norm2),
        "norm_f": tuple(a.reshape(1, D) for a in (norm_f, m_norm_f, v_norm_f)),
        "sg_gain": (sg_gain, m_sg_gain, v_sg_gain), "sg_b": (sg_b[0], m_sg_b[0], v_sg_b[0]),
        "ret_logit_f": (ret_logit_f, m_ret_logit_f, v_ret_logit_f),
        "ret_logit_b": (ret_logit_b, m_ret_logit_b, v_ret_logit_b),
        "b_mod": (b_mod, m_b_mod, v_b_mod), "sg_w": (sg_w[0], m_sg_w[0], v_sg_w[0])}
    small, loss = _adam_small(psum_rows, sgw_sum.reshape(G, C, C), small_params)
    back = {"norm_f": lambda a: a.reshape(D), "sg_b": lambda a: a[None], "sg_w": lambda a: a[None]}
    for name, vals in small.items():
        outs[name] = tuple(back.get(name, lambda a: a)(a) for a in vals)

    _, lands_first = _exchange_wait(rs["first"], [g_wm], scatter=True, name="rs_first_wait")

    def finish(name, fulls, lands, w, m, v, transposed):
        take = tr if transposed else (lambda a: a[0])
        res = _rs_adam(me_arr, fulls, lands, take(w), take(m), take(v), name="adam_" + name)
        put = (lambda a: jnp.transpose(a)[None]) if transposed else (lambda a: a[None])
        outs[name] = tuple(put(a) for a in res)
        return res[0]

    g_down = finish("w_down", [grads["w_down"][0]], [lands_first[0]], w_down, m_w_down, v_w_down, False)
    g_gate = finish("w_gate", [grads["w_gate"][0]], [lands_first[1]], w_gate, m_w_gate, v_w_gate, True)
    g_up = finish("w_up", [grads["w_up"][0]], [lands_first[2]], w_up, m_w_up, v_w_up, True)
    g_out = finish("w_out", [grads["w_out"][0]], [lands_first[3]], w_out, m_w_out, v_w_out, False)
    done = [g_down, g_gate, g_up, g_out]
    (part_own,), (part_land,) = _exchange_wait(h_cc, done, scatter=False, name="cctx_wait")
    res_cc = _cctx_final(me_arr, part_own, part_land, cctx_row, m_c_ctx.reshape(1, D), v_c_ctx.reshape(1, D))
    outs["c_ctx"] = tuple(a.reshape(D) for a in res_cc)
    lands_in = [_exchange_wait(rs["in%d" % j], done, scatter=True, name="rs_in%d_wait" % j)[1][0] for j in range(2)]
    finish("w_in", [h[0] for h in grads["w_in"]], lands_in, w_in, m_w_in, v_w_in, True)

    order = ["c_ctx", "w_mod", "b_mod", "norm1", "w_in", "sg_gain", "sg_w", "sg_b", "ret_logit_f", "ret_logit_b",
             "w_out", "norm2", "w_gate", "w_up", "w_down", "norm_f"]
    res = [loss.reshape(()), gx[None]]
    for j in range(4):
        res += [outs[name][j] for name in order]
    return tuple(res)
```

```python
import functools
import math

import numpy as np
import jax
import jax.numpy as jnp
from jax import lax
from jax.experimental import pallas as pl
from jax.experimental.pallas import tpu as pltpu

F32 = jnp.float32
BF16 = jnp.bfloat16

D = 1024
AW = 512
RW = 512
H = 4
DH = 128
G = 4
C = 128
CR = 256
DFF = 2816
INC = 3584
KV_LO, KV_HI = 1536, 2560
NDEV = 8
EPS = 1e-6
KSCALE = DH ** -0.5
ROPE_BASE = 10000.0
GRID_W = 64

ADAM_LR = 0.001
ADAM_B1 = 0.9
ADAM_B2 = 0.999
ADAM_EPS = 1e-08
ADAM_WD = 0.01
ADAM_STEP = 10

VMEM_LIMIT = 56 * 1024 * 1024
MESH = pl.DeviceIdType.MESH


def _cparams(n_grid=0, **kw):
    sem = ("arbitrary",) * n_grid if n_grid else None
    return pltpu.CompilerParams(dimension_semantics=sem, vmem_limit_bytes=VMEM_LIMIT, **kw)


def _dot(a, b):
    return jnp.dot(a, b, preferred_element_type=F32)


def _dot_nt(a, b):
    return lax.dot_general(a, b, (((1,), (1,)), ((), ())), preferred_element_type=F32)


def _dot_tn(a, b):
    return lax.dot_general(a, b, (((0,), (0,)), ((), ())), preferred_element_type=F32)


def _whole(shape):
    nd = len(shape)
    return pl.BlockSpec(shape, lambda *_: (0,) * nd, pipeline_mode=pl.Buffered(1))


def _acc(shape):
    nd = len(shape)
    return pl.BlockSpec(shape, lambda *_: (0,) * nd)


def _rows(tm, n):
    return pl.BlockSpec((tm, n), lambda i: (i, 0))


def _rows_rev(tm, n, nt):
    return pl.BlockSpec((tm, n), lambda i: (nt - 1 - i, 0))


def _cols(n, tm):
    return pl.BlockSpec((n, tm), lambda i: (0, i))


def _sigmoid(x):
    return 1.0 / (1.0 + jnp.exp(-x))


def _log_sigmoid(x):
    return jnp.minimum(x, 0.0) - jnp.log(1.0 + jnp.exp(-jnp.abs(x)))


_GK0 = math.sqrt(2.0 / math.pi)
_GK1 = 0.044715


def _gelu(x):
    x2 = x * x
    t = jnp.tanh(x * (_GK0 + (_GK0 * _GK1) * x2))
    h = 0.5 + 0.5 * t
    g = x * h
    dg = h + g * (1.0 - h) * ((2.0 * _GK0) + (6.0 * _GK0 * _GK1) * x2)
    return g, dg


def _silu_parts(a):
    s = _sigmoid(a)
    sa = a * s
    return sa, s + sa * (1.0 - s)


def _rope(t, cos, sins):
    n = t.shape[1]
    lane = lax.broadcasted_iota(jnp.int32, t.shape, 1)
    first = (lane & 63) < 32
    partner = jnp.where(first, pltpu.roll(t, n - 32, 1), pltpu.roll(t, 32, 1))
    return t * cos + partner * sins


def _headnorm(o):
    outs, rs = [], []
    for h in range(H):
        oh = o[:, h * DH:(h + 1) * DH]
        r = lax.rsqrt(jnp.mean(oh * oh, axis=-1, keepdims=True) + EPS)
        outs.append(oh * r)
        rs.append(r)
    return jnp.concatenate(outs, axis=1), rs


def _decay_consts(lg, forward):
    ii = lax.broadcasted_iota(jnp.int32, (CR, CR), 0).astype(F32)
    jj = lax.broadcasted_iota(jnp.int32, (CR, CR), 1).astype(F32)
    ic = lax.broadcasted_iota(jnp.int32, (CR, 1), 0).astype(F32)
    if forward:
        diff = ii - jj
        xi = jnp.exp(lg * (ic + 1.0))
        z = jnp.exp(lg * (CR - 1.0 - ic))
    else:
        diff = jj - ii
        xi = jnp.exp(lg * (CR - ic))
        z = jnp.exp(lg * ic)
    dm = jnp.where(diff >= 0.0, jnp.exp(lg * jnp.maximum(diff, 0.0)), 0.0)
    dec = jnp.exp(lg * float(CR))
    return dm, xi, z, dec, ic


def _ctx_fwd(ctx, cmod6, norm1, win, rlf, rlb):
    lc = ctx.shape[0]

    def body(ctx_ref, cmod_ref, n1_ref, win_ref, rlf_ref, rlb_ref, hc_ref, kvc_ref, scf_ref, scb_ref):
        x = ctx_ref[...]
        r = lax.rsqrt(jnp.mean(x * x, axis=-1, keepdims=True) + EPS)
        hc = (x * r) * (n1_ref[...] * (1.0 + cmod_ref[1:2, :])) + cmod_ref[0:1, :]
        hcb = hc.astype(BF16)
        hc_ref[...] = hcb
        kv = _dot_nt(hcb, win_ref[KV_LO:KV_HI, :])
        k = kv[:, :RW] * KSCALE
        v = kv[:, RW:]
        kvc_ref[:, :RW] = k
        kvc_ref[:, RW:] = v
        t = lax.broadcasted_iota(jnp.int32, (lc, 1), 0).astype(F32)
        lgf = _log_sigmoid(rlf_ref[...])
        lgb = _log_sigmoid(rlb_ref[...])
        for h in range(H):
            hs = slice(h * DH, (h + 1) * DH)
            wf = jnp.exp(lgf[h:h + 1, 0:1] * (lc - 1.0 - t))
            wb = jnp.exp(lgb[h:h + 1, 0:1] * t)
            vh = v[:, hs].astype(BF16)
            scf_ref[h] = _dot_tn((k[:, hs] * wf).astype(BF16), vh)
            scb_ref[h] = _dot_tn((k[:, hs] * wb).astype(BF16), vh)

    return pl.pallas_call(
        body, name="ctx_fwd",
        out_shape=(jax.ShapeDtypeStruct((lc, D), BF16), jax.ShapeDtypeStruct((lc, 2 * RW), F32),
                   jax.ShapeDtypeStruct((H, DH, DH), F32), jax.ShapeDtypeStruct((H, DH, DH), F32)),
        compiler_params=_cparams(),
    )(ctx, cmod6, norm1, win, rlf, rlb)


def _sg_forward(u, v, sgw_ref, sgbt_ref, sgg_ref, nc):
    ua, dgu = _gelu(u)
    va, dgv = _gelu(v)
    gain = sgg_ref[...]
    mixed, vn_b, vah_l, rv_l = [], [], [], []
    for g in range(G):
        gs = slice(g * DH, (g + 1) * DH)
        vag = va[:, gs]
        rv = lax.rsqrt(jnp.mean(vag * vag, axis=-1, keepdims=True) + EPS)
        vah = vag * rv
        vn = (vah * gain[:, gs]).astype(BF16)
        wg = sgw_ref[g].astype(BF16)
        bcol = sgbt_ref[g]
        rows = [_dot(wg, vn[c * C:(c + 1) * C, :]) + bcol for c in range(nc)]
        mixed.append(jnp.concatenate(rows, axis=0) if nc > 1 else rows[0])
        vn_b.append(vn)
        vah_l.append(vah)
        rv_l.append(rv)
    mixed = jnp.concatenate(mixed, axis=1)
    return ua * mixed, (ua, dgu, dgv, mixed, vn_b, vah_l, rv_l)


def _fwd_a(x, mod6, norm1, win, sgw, sgbt, sggain, cos, sin, rlf, scf, *, tm):
    t_len = x.shape[0]
    nt, nc, ncr = t_len // tm, tm // C, tm // CR

    def body(x_ref, mod_ref, n1_ref, win_ref, sgw_ref, sgbt_ref, sgg_ref, cos_ref, sin_ref, rlf_ref, scf_ref,
             hx_ref, zuv_ref, q_ref, k_ref, v_ref, gfb_ref, of_ref, ya_ref, sst_ref, s_scr):
        i = pl.program_id(0)

        @pl.when(i == 0)
        def _():
            s_scr[...] = scf_ref[...]

        x = x_ref[...]
        r = lax.rsqrt(jnp.mean(x * x, axis=-1, keepdims=True) + EPS)
        hx = (x * r) * (n1_ref[...] * (1.0 + mod_ref[1:2, :])) + mod_ref[0:1, :]
        hxb = hx.astype(BF16)
        hx_ref[...] = hxb
        z = _dot_nt(hxb, win_ref[...])
        zuv_ref[...] = z[:, :2 * AW]
        gfb_ref[...] = z[:, 2560:3584].astype(BF16)
        cos = jnp.tile(cos_ref[...], (1, H))
        sins = jnp.tile(sin_ref[...], (1, H))
        q_ref[...] = _rope(z[:, 1024:1536], cos, sins).astype(BF16)
        k_ref[...] = (_rope(z[:, 1536:2048], cos, sins) * KSCALE).astype(BF16)
        v_ref[...] = z[:, 2048:2560].astype(BF16)
        ya, _ = _sg_forward(z[:, :AW], z[:, AW:2 * AW], sgw_ref, sgbt_ref, sgg_ref, nc)
        ya_ref[...] = ya.astype(BF16)

        lg = _log_sigmoid(rlf_ref[...])
        for h in range(H):
            dm, xi, zc, dec, _ = _decay_consts(lg[h:h + 1, 0:1], True)
            hs = slice(h * DH, (h + 1) * DH)
            for c in range(ncr):
                rs = slice(c * CR, (c + 1) * CR)
                qc, kc, vc = q_ref[rs, hs], k_ref[rs, hs], v_ref[rs, hs]
                s = s_scr[h]
                sst_ref[c, h] = s
                a = (_dot_nt(qc, kc) * dm).astype(BF16)
                of_ref[rs, hs] = (_dot(a, vc) + xi * _dot(qc, s.astype(BF16))).astype(BF16)
                kz = (kc.astype(F32) * zc).astype(BF16)
                s_scr[h] = dec * s + _dot_tn(kz, vc)

    n_chunks = t_len // CR
    return pl.pallas_call(
        body, name="fwd_a", grid=(nt,),
        in_specs=[_rows(tm, D), _whole((6, D)), _whole((1, D)), _whole((INC, D)), _whole((G, C, C)),
                  _whole((G, C, 128)), _whole((1, AW)), _rows(tm, DH), _rows(tm, DH), _whole((H, 128)),
                  _whole((H, DH, DH))],
        out_specs=[_rows(tm, D), _rows(tm, 2 * AW), _rows(tm, RW), _rows(tm, RW), _rows(tm, RW),
                   _rows(tm, 2 * RW), _rows(tm, RW), _rows(tm, AW),
                   pl.BlockSpec((ncr, H, DH, DH), lambda i: (i, 0, 0, 0))],
        out_shape=(jax.ShapeDtypeStruct((t_len, D), BF16), jax.ShapeDtypeStruct((t_len, 2 * AW), F32),
                   jax.ShapeDtypeStruct((t_len, RW), BF16), jax.ShapeDtypeStruct((t_len, RW), BF16),
                   jax.ShapeDtypeStruct((t_len, RW), BF16), jax.ShapeDtypeStruct((t_len, 2 * RW), BF16),
                   jax.ShapeDtypeStruct((t_len, RW), BF16), jax.ShapeDtypeStruct((t_len, AW), BF16),
                   jax.ShapeDtypeStruct((n_chunks, H, DH, DH), F32)),
        scratch_shapes=[pltpu.VMEM((H, DH, DH), F32)],
        compiler_params=_cparams(1),
    )(x, mod6, norm1, win, sgw, sgbt, sggain, cos, sin, rlf, scf)


def _fwd_b(q, k, v, of, gfb, ya, x, mod6, wout, rlb, scb, *, tm):
    t_len = x.shape[0]
    nt, nc = t_len // tm, tm // CR

    def body(q_ref, k_ref, v_ref, of_ref, gfb_ref, ya_ref, x_ref, mod_ref, wout_ref, rlb_ref, scb_ref,
             ob_ref, ycat_ref, y_ref, x1_ref, rst_ref, r_scr):
        i = pl.program_id(0)

        @pl.when(i == 0)
        def _():
            r_scr[...] = scb_ref[...]

        lg = _log_sigmoid(rlb_ref[...])
        for h in range(H):
            dm, xi, zc, dec, _ = _decay_consts(lg[h:h + 1, 0:1], False)
            hs = slice(h * DH, (h + 1) * DH)
            for c in reversed(range(nc)):
                rs = slice(c * CR, (c + 1) * CR)
                qc, kc, vc = q_ref[rs, hs], k_ref[rs, hs], v_ref[rs, hs]
                s = r_scr[h]
                rst_ref[c, h] = s
                a = (_dot_nt(qc, kc) * dm).astype(BF16)
                ob_ref[rs, hs] = (_dot(a, vc) + xi * _dot(qc, s.astype(BF16))).astype(BF16)
                kz = (kc.astype(F32) * zc).astype(BF16)
                r_scr[h] = dec * s + _dot_tn(kz, vc)

        gfb = gfb_ref[...].astype(F32)
        sf, _ = _silu_parts(gfb[:, :RW])
        sb, _ = _silu_parts(gfb[:, RW:])
        hnf, _ = _headnorm(of_ref[...].astype(F32))
        hnb, _ = _headnorm(ob_ref[...].astype(F32))
        yr = sf * hnf + sb * hnb
        ycat = jnp.concatenate([ya_ref[...], yr.astype(BF16)], axis=1)
        ycat_ref[...] = ycat.T
        y = _dot(ycat, wout_ref[...])
        y_ref[...] = y.astype(BF16)
        x1_ref[...] = x_ref[...] + mod_ref[2:3, :] * y

    n_chunks = t_len // CR
    rr = functools.partial(_rows_rev, nt=nt)
    return pl.pallas_call(
        body, name="fwd_b", grid=(nt,),
        in_specs=[rr(tm, RW), rr(tm, RW), rr(tm, RW), rr(tm, RW), rr(tm, 2 * RW), rr(tm, AW), rr(tm, D),
                  _whole((6, D)), _whole((D, D)), _whole((H, 128)), _whole((H, DH, DH))],
        out_specs=[rr(tm, RW), pl.BlockSpec((D, tm), lambda i: (0, nt - 1 - i)), rr(tm, D), rr(tm, D),
                   pl.BlockSpec((nc, H, DH, DH), lambda i: (nt - 1 - i, 0, 0, 0))],
        out_shape=(jax.ShapeDtypeStruct((t_len, RW), BF16), jax.ShapeDtypeStruct((D, t_len), BF16),
                   jax.ShapeDtypeStruct((t_len, D), BF16), jax.ShapeDtypeStruct((t_len, D), F32),
                   jax.ShapeDtypeStruct((n_chunks, H, DH, DH), F32)),
        scratch_shapes=[pltpu.VMEM((H, DH, DH), F32)],
        compiler_params=_cparams(1),
    )(q, k, v, of, gfb, ya, x, mod6, wout, rlb, scb)


def _ffn(x1, tgt, mod6, norm2, normf, wg, wu, wd, *, tm):
    t_len = x1.shape[0]
    nt = t_len // tm

    def body(x1_ref, tgt_ref, mod_ref, n2_ref, nf_ref, wg_ref, wu_ref, wd_ref,
             dx1_ref, h2_ref, da_ref, db_ref, hm_ref, df_ref, small_ref):
        i = pl.program_id(0)

        @pl.when(i == 0)
        def _():
            small_ref[...] = jnp.zeros_like(small_ref)

        sh2, sc2, g2 = mod_ref[3:4, :], mod_ref[4:5, :], mod_ref[5:6, :]
        n2, nf = n2_ref[...], nf_ref[...]
        x1 = x1_ref[...]
        r2 = lax.rsqrt(jnp.mean(x1 * x1, axis=-1, keepdims=True) + EPS)
        x1h = x1 * r2
        w2 = n2 * (1.0 + sc2)
        h2b = (x1h * w2 + sh2).astype(BF16)
        h2_ref[...] = h2b
        a = _dot_nt(h2b, wg_ref[...])
        b = _dot_nt(h2b, wu_ref[...])
        sa, dsa = _silu_parts(a)
        hmb = (sa * b).astype(BF16)
        hm_ref[...] = hmb.T
        f = _dot(hmb, wd_ref[...])
        x2 = x1 + g2 * f
        r3 = lax.rsqrt(jnp.mean(x2 * x2, axis=-1, keepdims=True) + EPS)
        x2h = x2 * r3
        diff = x2h * nf - tgt_ref[...]
        small_ref[5:6, :] += jnp.sum(diff * diff, axis=0, keepdims=True)
        dout = diff * (1.0 / D)
        small_ref[0:1, :] += jnp.sum(dout * x2h, axis=0, keepdims=True)
        dyg = dout * nf
        dx2 = r3 * (dyg - x2h * jnp.mean(dyg * x2h, axis=-1, keepdims=True))
        small_ref[1:2, :] += jnp.sum(dx2 * f, axis=0, keepdims=True)
        dfb = (dx2 * g2).astype(BF16)
        df_ref[...] = dfb
        dhm = _dot_nt(dfb, wd_ref[...])
        dbb = (dhm * sa).astype(BF16)
        dab = (dhm * b * dsa).astype(BF16)
        da_ref[...] = dab.T
        db_ref[...] = dbb.T
        dh2 = _dot(dab, wg_ref[...]) + _dot(dbb, wu_ref[...])
        small_ref[2:3, :] += jnp.sum(dh2, axis=0, keepdims=True)
        dhx = dh2 * x1h
        small_ref[3:4, :] += jnp.sum(dhx, axis=0, keepdims=True) * n2
        small_ref[4:5, :] += jnp.sum(dhx, axis=0, keepdims=True) * (1.0 + sc2)
        dyg2 = dh2 * w2
        dx1_ref[...] = dx2 + r2 * (dyg2 - x1h * jnp.mean(dyg2 * x1h, axis=-1, keepdims=True))

    return pl.pallas_call(
        body, name="ffn", grid=(nt,),
        in_specs=[_rows(tm, D), _rows(tm, D), _whole((6, D)), _whole((1, D)), _whole((1, D)),
                  _whole((DFF, D)), _whole((DFF, D)), _whole((DFF, D))],
        out_specs=[_rows(tm, D), _rows(tm, D), _cols(DFF, tm), _cols(DFF, tm), _cols(DFF, tm), _rows(tm, D),
                   _acc((8, D))],
        out_shape=(jax.ShapeDtypeStruct((t_len, D), F32), jax.ShapeDtypeStruct((t_len, D), BF16),
                   jax.ShapeDtypeStruct((DFF, t_len), BF16), jax.ShapeDtypeStruct((DFF, t_len), BF16),
                   jax.ShapeDtypeStruct((DFF, t_len), BF16), jax.ShapeDtypeStruct((t_len, D), BF16),
                   jax.ShapeDtypeStruct((8, D), F32)),
        compiler_params=_cparams(1),
    )(x1, tgt, mod6, norm2, normf, wg, wu, wd)


def _mid_bwd(dx1, y, of, ob, gfb, mod6, wout, *, tm):
    t_len = dx1.shape[0]
    nt = t_len // tm

    def body(dx1_ref, y_ref, of_ref, ob_ref, gfb_ref, mod_ref, wout_ref,
             dy_ref, dya_ref, dgfb_ref, dof_ref, dob_ref, dg1_ref):
        i = pl.program_id(0)

        @pl.when(i == 0)
        def _():
            dg1_ref[...] = jnp.zeros_like(dg1_ref)

        dx1 = dx1_ref[...]
        dg1_ref[0:1, :] += jnp.sum(dx1 * y_ref[...].astype(F32), axis=0, keepdims=True)
        dyb = (dx1 * mod_ref[2:3, :]).astype(BF16)
        dy_ref[...] = dyb
        dycat = _dot_nt(dyb, wout_ref[...])
        dya_ref[...] = dycat[:, :AW]
        dyr = dycat[:, AW:]
        gfb = gfb_ref[...].astype(F32)
        for o_ref, do_ref, lo in ((of_ref, dof_ref, 0), (ob_ref, dob_ref, RW)):
            sg, dsg = _silu_parts(gfb[:, lo:lo + RW])
            o = o_ref[...].astype(F32)
            hn, rs = _headnorm(o)
            dgfb_ref[:, lo:lo + RW] = (dyr * hn * dsg).astype(BF16)
            dhn = dyr * sg
            for h in range(H):
                hs = slice(h * DH, (h + 1) * DH)
                oh, dh = hn[:, hs], dhn[:, hs]
                do_ref[:, hs] = (rs[h] * (dh - oh * jnp.mean(dh * oh, axis=-1, keepdims=True))).astype(BF16)

    return pl.pallas_call(
        body, name="mid_bwd", grid=(nt,),
        in_specs=[_rows(tm, D), _rows(tm, D), _rows(tm, RW), _rows(tm, RW), _rows(tm, 2 * RW),
                  _whole((6, D)), _whole((D, D))],
        out_specs=[_rows(tm, D), _rows(tm, AW), _rows(tm, 2 * RW), _rows(tm, RW), _rows(tm, RW), _acc((8, D))],
        out_shape=(jax.ShapeDtypeStruct((t_len, D), BF16), jax.ShapeDtypeStruct((t_len, AW), F32),
                   jax.ShapeDtypeStruct((t_len, 2 * RW), BF16), jax.ShapeDtypeStruct((t_len, RW), BF16),
                   jax.ShapeDtypeStruct((t_len, RW), BF16), jax.ShapeDtypeStruct((8, D), F32)),
        compiler_params=_cparams(1),
    )(dx1, y, of, ob, gfb, mod6, wout)


def _ret_bwd_dir(q_ref, k_ref, v_ref, do_ref, st_ref, dq_ref, dk_ref, dv_ref, ds_scr, dlg_scr, lg, forward, nc, row0):
    order = reversed(range(nc)) if forward else range(nc)
    order = list(order)
    for h in range(H):
        dm, xi, zc, dec, ic = _decay_consts(lg[h:h + 1, 0:1], forward)
        hs = slice(h * DH, (h + 1) * DH)
        if forward:
            w_q, c_q, w_k, c_k = ic, 1.0, -ic, CR - 1.0
        else:
            w_q, c_q, w_k, c_k = -ic, float(CR), ic, 0.0
        acc = jnp.zeros((1, DH), F32)
        for c in order:
            rs = slice(c * CR, (c + 1) * CR)
            qc, kc, vc, doc = q_ref[rs, hs], k_ref[rs, hs], v_ref[rs, hs], do_ref[rs, hs]
            s = st_ref[c, h]
            dsn = ds_scr[h]
            sb, dsnb = s.astype(BF16), dsn.astype(BF16)
            qf, kf = qc.astype(F32), kc.astype(F32)
            a = (_dot_nt(qc, kc) * dm).astype(BF16)
            da = (_dot_nt(doc, vc) * dm).astype(BF16)
            dq_c = xi * _dot_nt(doc, sb)
            dk_s = zc * _dot_nt(vc, dsnb)
            dq = _dot(da, kc) + dq_c
            dk = _dot_tn(da, qc) + dk_s
            dq_ref[rs, hs] = dq
            dk_ref[rs, hs] = dk
            dv_ref[rs, hs] = _dot_tn(a, doc) + zc * _dot(kc, dsnb)
            rowterm = qf * (w_q * dq + c_q * dq_c) + kf * (w_k * dk + c_k * dk_s)
            acc = acc + jnp.sum(rowterm, axis=0, keepdims=True)
            acc = acc + (float(CR) * dec) * jnp.sum(s * dsn, axis=0, keepdims=True)
            ds_scr[h] = _dot_tn((xi * qf).astype(BF16), doc) + dec * dsn
        dlg_scr[row0 + h:row0 + h + 1, :] += acc


def _ret_bwd(q, k, v, dof, dob, sst, rst, rlf, rlb, *, tm):
    t_len = q.shape[0]
    nt, nc = t_len // tm, tm // CR

    def body(qf_ref, kf_ref, vf_ref, dof_ref, sst_ref, qb_ref, kb_ref, vb_ref, dob_ref, rst_ref, rlf_ref, rlb_ref,
             dqf_ref, dkf_ref, dvf_ref, dqb_ref, dkb_ref, dvb_ref, dscf_ref, dscb_ref, dlg_ref,
             dsf_scr, dsb_scr, dlg_scr):
        i = pl.program_id(0)

        @pl.when(i == 0)
        def _():
            dsf_scr[...] = jnp.zeros_like(dsf_scr)
            dsb_scr[...] = jnp.zeros_like(dsb_scr)
            dlg_scr[...] = jnp.zeros_like(dlg_scr)

        lgf = _log_sigmoid(rlf_ref[...])
        lgb = _log_sigmoid(rlb_ref[...])
        _ret_bwd_dir(qf_ref, kf_ref, vf_ref, dof_ref, sst_ref, dqf_ref, dkf_ref, dvf_ref, dsf_scr, dlg_scr, lgf, True, nc, 0)
        _ret_bwd_dir(qb_ref, kb_ref, vb_ref, dob_ref, rst_ref, dqb_ref, dkb_ref, dvb_ref, dsb_scr, dlg_scr, lgb, False, nc, H)

        @pl.when(i == nt - 1)
        def _():
            dscf_ref[...] = dsf_scr[...]
            dscb_ref[...] = dsb_scr[...]
            tot = jnp.broadcast_to(jnp.sum(dlg_scr[...], axis=1, keepdims=True), (8, 128))
            ri = lax.broadcasted_iota(jnp.int32, (8, 128), 0)
            li = lax.broadcasted_iota(jnp.int32, (8, 128), 1)
            dlg_ref[...] = jnp.zeros_like(dlg_ref)
            dlg_ref[0:1, 0:128] = jnp.sum(jnp.where(ri == li, tot, 0.0), axis=0, keepdims=True)
            dlg_ref[1:2, 0:128] = jnp.sum(jnp.where(ri - H == li, tot, 0.0), axis=0, keepdims=True)

    rr = functools.partial(_rows_rev, nt=nt)
    st_f = pl.BlockSpec((nc, H, DH, DH), lambda i: (nt - 1 - i, 0, 0, 0))
    st_b = pl.BlockSpec((nc, H, DH, DH), lambda i: (i, 0, 0, 0))
    tok = jax.ShapeDtypeStruct((t_len, RW), F32)
    st = jax.ShapeDtypeStruct((H, DH, DH), F32)
    return pl.pallas_call(
        body, name="ret_bwd", grid=(nt,),
        in_specs=[rr(tm, RW), rr(tm, RW), rr(tm, RW), rr(tm, RW), st_f,
                  _rows(tm, RW), _rows(tm, RW), _rows(tm, RW), _rows(tm, RW), st_b,
                  _whole((H, 128)), _whole((H, 128))],
        out_specs=[rr(tm, RW), rr(tm, RW), rr(tm, RW), _rows(tm, RW), _rows(tm, RW), _rows(tm, RW),
                   _acc((H, DH, DH)), _acc((H, DH, DH)), _acc((8, D))],
        out_shape=(tok, tok, tok, tok, tok, tok, st, st, jax.ShapeDtypeStruct((8, D), F32)),
        scratch_shapes=[pltpu.VMEM((H, DH, DH), F32), pltpu.VMEM((H, DH, DH), F32), pltpu.VMEM((8, 128), F32)],
        compiler_params=_cparams(1),
    )(q, k, v, dof, sst, q, k, v, dob, rst, rlf, rlb)


def _in_bwd(x, zuv, dya, dqf, dkf, dvf, dqb, dkb, dvb, dgfb, dx1, cos, sin, mod6, norm1, win, sgw, sgbt, sggain, *, tm):
    t_len = x.shape[0]
    nt, nc = t_len // tm, tm // C

    def body(x_ref, zuv_ref, dya_ref, dqf_ref, dkf_ref, dvf_ref, dqb_ref, dkb_ref, dvb_ref, dgfb_ref, dx1_ref,
             cos_ref, sin_ref, mod_ref, n1_ref, win_ref, sgw_ref, sgbt_ref, sgg_ref,
             gx_ref, dzt_ref, small_ref, dsgw_ref, dsgbt_ref, dz_ref):
        i = pl.program_id(0)

        @pl.when(i == 0)
        def _():
            small_ref[...] = jnp.zeros_like(small_ref)
            dsgw_ref[...] = jnp.zeros_like(dsgw_ref)
            dsgbt_ref[...] = jnp.zeros_like(dsgbt_ref)

        zuv = zuv_ref[...]
        _, (ua, dgu, dgv, mixed, vn_b, vah_l, rv_l) = _sg_forward(zuv[:, :AW], zuv[:, AW:], sgw_ref, sgbt_ref, sgg_ref, nc)
        dya = dya_ref[...]
        dz_ref[:, 0:AW] = (dya * mixed * dgu).astype(BF16)
        dmixed = dya * ua
        gain = sgg_ref[...]
        for g in range(G):
            gs = slice(g * DH, (g + 1) * DH)
            dmg = dmixed[:, gs]
            dmb = dmg.astype(BF16)
            wgt = sgw_ref[g].astype(BF16)
            dsw = jnp.zeros((C, C), F32)
            dsb = jnp.zeros((C, DH), F32)
            rows = []
            for c in range(nc):
                rs = slice(c * C, (c + 1) * C)
                dsw = dsw + _dot_nt(dmb[rs, :], vn_b[g][rs, :])
                dsb = dsb + dmg[rs, :]
                rows.append(_dot_tn(wgt, dmb[rs, :]))
            dsgw_ref[g] += dsw
            dsgbt_ref[g] += dsb
            dvn = jnp.concatenate(rows, axis=0) if nc > 1 else rows[0]
            vah, rv = vah_l[g], rv_l[g]
            small_ref[3:4, gs] += jnp.sum(dvn * vah, axis=0, keepdims=True)
            dyg = dvn * gain[:, gs]
            dva = rv * (dyg - vah * jnp.mean(dyg * vah, axis=-1, keepdims=True))
            dz_ref[:, AW + g * DH:AW + (g + 1) * DH] = (dva * dgv[:, gs]).astype(BF16)

        cos = jnp.tile(cos_ref[...], (1, H))
        nsins = -jnp.tile(sin_ref[...], (1, H))
        dz_ref[:, 1024:1536] = _rope(dqf_ref[...] + dqb_ref[...], cos, nsins).astype(BF16)
        dz_ref[:, 1536:2048] = (_rope(dkf_ref[...] + dkb_ref[...], cos, nsins) * KSCALE).astype(BF16)
        dz_ref[:, 2048:2560] = (dvf_ref[...] + dvb_ref[...]).astype(BF16)
        dz_ref[:, 2560:3584] = dgfb_ref[...]

        dz = dz_ref[...]
        dzt_ref[...] = dz.T
        dhx = _dot(dz, win_ref[...])
        x = x_ref[...]
        r = lax.rsqrt(jnp.mean(x * x, axis=-1, keepdims=True) + EPS)
        xh = x * r
        n1, sc1 = n1_ref[...], mod_ref[1:2, :]
        small_ref[0:1, :] += jnp.sum(dhx, axis=0, keepdims=True)
        dhxx = jnp.sum(dhx * xh, axis=0, keepdims=True)
        small_ref[1:2, :] += dhxx * n1
        small_ref[2:3, :] += dhxx * (1.0 + sc1)
        dyg = dhx * (n1 * (1.0 + sc1))
        gx_ref[...] = dx1_ref[...] + r * (dyg - xh * jnp.mean(dyg * xh, axis=-1, keepdims=True))

        @pl.when(i == nt - 1)
        def _():
            ri = lax.broadcasted_iota(jnp.int32, (C, DH), 0)
            li = lax.broadcasted_iota(jnp.int32, (C, DH), 1)
            for g in range(G):
                tot = jnp.broadcast_to(jnp.sum(dsgbt_ref[g], axis=1, keepdims=True), (C, DH))
                small_ref[4 + g:5 + g, 0:DH] = jnp.sum(jnp.where(ri == li, tot, 0.0), axis=0, keepdims=True)

    tok = functools.partial(_rows, tm)
    return pl.pallas_call(
        body, name="in_bwd", grid=(nt,),
        in_specs=[tok(D), tok(2 * AW), tok(AW), tok(RW), tok(RW), tok(RW), tok(RW), tok(RW), tok(RW),
                  tok(2 * RW), tok(D), tok(DH), tok(DH), _whole((6, D)), _whole((1, D)), _whole((INC, D)),
                  _whole((G, C, C)), _whole((G, C, 128)), _whole((1, AW))],
        out_specs=[tok(D), _cols(INC, tm), _acc((8, D)), _acc((G, C, C))],
        out_shape=(jax.ShapeDtypeStruct((t_len, D), F32), jax.ShapeDtypeStruct((INC, t_len), BF16),
                   jax.ShapeDtypeStruct((8, D), F32), jax.ShapeDtypeStruct((G, C, C), F32)),
        scratch_shapes=[pltpu.VMEM((G, C, 128), F32), pltpu.VMEM((tm, INC), BF16)],
        compiler_params=_cparams(1),
    )(x, zuv, dya, dqf, dkf, dvf, dqb, dkb, dvb, dgfb, dx1, cos, sin, mod6, norm1, win, sgw, sgbt, sggain)


def _tn_matmul(at, b, *, bm, bt, name, init=None, init_rows=None, after=(), cols=None):
    m, t_len = at.shape
    cj, n = (0, b.shape[1]) if cols is None else cols
    ni, ntt = m // bm, t_len // bt
    has_init = init is not None

    def body(*refs):
        o_ref, ob_ref = refs[-2:]
        a_ref, b_ref = refs[:2]
        init_ref = refs[2] if has_init else None
        i, t = pl.program_id(0), pl.program_id(1)

        @pl.when(t == 0)
        def _():
            o_ref[...] = jnp.zeros_like(o_ref)

        if has_init:
            for ib in range(ni):
                lo, hi = max(init_rows[0], ib * bm), min(init_rows[1], (ib + 1) * bm)
                if lo < hi:
                    @pl.when(jnp.logical_and(t == 0, i == ib))
                    def _(lo=lo, hi=hi, ib=ib):
                        o_ref[lo - ib * bm:hi - ib * bm, :] = init_ref[lo - init_rows[0]:hi - init_rows[0], :]

        o_ref[...] += _dot(a_ref[...], b_ref[...])

        @pl.when(t == ntt - 1)
        def _():
            ob_ref[...] = o_ref[...].astype(BF16)

    in_specs = [pl.BlockSpec((bm, bt), lambda i, t: (i, t)), pl.BlockSpec((bt, n), lambda i, t: (t, cj))]
    args = [at, b]
    if has_init:
        in_specs.append(pl.BlockSpec((init.shape[0], n), lambda i, t: (0, cj), pipeline_mode=pl.Buffered(1)))
        args.append(init)
    for arr in after:
        in_specs.append(pl.BlockSpec(memory_space=pl.ANY))
        args.append(arr)
    out_spec = pl.BlockSpec((bm, n), lambda i, t: (i, 0))
    return pl.pallas_call(
        body, name=name, grid=(ni, ntt),
        in_specs=in_specs,
        out_specs=[out_spec, out_spec],
        out_shape=(jax.ShapeDtypeStruct((m, n), F32), jax.ShapeDtypeStruct((m, n), BF16)),
        compiler_params=_cparams(2),
    )(*args)


def _ctx_bwd(ctx, hc, kvc, cmod6, norm1, win, rlf, rlb, dscf, dscb, dlg_in):
    lc = ctx.shape[0]

    def body(ctx_ref, hc_ref, kvc_ref, cmod_ref, n1_ref, win_ref, rlf_ref, rlb_ref, dscf_ref, dscb_ref, dlgin_ref,
             dwin_ref, small_ref, dlg_ref):
        k = kvc_ref[:, :RW]
        v = kvc_ref[:, RW:]
        t = lax.broadcasted_iota(jnp.int32, (lc, 1), 0).astype(F32)
        lgf = _log_sigmoid(rlf_ref[...])
        lgb = _log_sigmoid(rlb_ref[...])
        dks, dvs = [], []
        lane = lax.broadcasted_iota(jnp.int32, (1, 128), 1)
        row_f = jnp.zeros((1, 128), F32)
        row_b = jnp.zeros((1, 128), F32)
        for h in range(H):
            hs = slice(h * DH, (h + 1) * DH)
            wf = jnp.exp(lgf[h:h + 1, 0:1] * (lc - 1.0 - t))
            wb = jnp.exp(lgb[h:h + 1, 0:1] * t)
            kh, vhb = k[:, hs], v[:, hs].astype(BF16)
            dsf, dsb = dscf_ref[h].astype(BF16), dscb_ref[h].astype(BF16)
            dkf = wf * _dot_nt(vhb, dsf)
            dkb = wb * _dot_nt(vhb, dsb)
            dvs.append(_dot((kh * wf).astype(BF16), dsf) + _dot((kh * wb).astype(BF16), dsb))
            dks.append((dkf + dkb) * KSCALE)
            lf = jnp.sum((lc - 1.0 - t) * kh * dkf, axis=0, keepdims=True)
            lb = jnp.sum(t * kh * dkb, axis=0, keepdims=True)
            row_f = row_f + jnp.where(lane == h, jnp.sum(lf, axis=1, keepdims=True), 0.0)
            row_b = row_b + jnp.where(lane == h, jnp.sum(lb, axis=1, keepdims=True), 0.0)
        dlg_ref[...] = dlgin_ref[...]
        dlg_ref[0:1, 0:128] += row_f
        dlg_ref[1:2, 0:128] += row_b
        dkv = jnp.concatenate(dks + dvs, axis=1).astype(BF16)
        dhc = _dot(dkv, win_ref[KV_LO:KV_HI, :])
        dwin_ref[...] = _dot_tn(dkv, hc_ref[...])
        x = ctx_ref[...]
        r = lax.rsqrt(jnp.mean(x * x, axis=-1, keepdims=True) + EPS)
        xh = x * r
        small_ref[...] = jnp.zeros_like(small_ref)
        small_ref[0:1, :] = jnp.sum(dhc, axis=0, keepdims=True)
        dhxx = jnp.sum(dhc * xh, axis=0, keepdims=True)
        small_ref[1:2, :] = dhxx * n1_ref[...]
        small_ref[2:3, :] = dhxx * (1.0 + cmod_ref[1:2, :])

    return pl.pallas_call(
        body, name="ctx_bwd",
        out_shape=(jax.ShapeDtypeStruct((2 * RW, D), F32), jax.ShapeDtypeStruct((8, D), F32),
                   jax.ShapeDtypeStruct((8, D), F32)),
        compiler_params=_cparams(),
    )(ctx, hc, kvc, cmod6, norm1, win, rlf, rlb, dscf, dscb, dlg_in)


def _adam_math(w, g, m, v):
    m = ADAM_B1 * m + (1.0 - ADAM_B1) * g
    v = ADAM_B2 * v + (1.0 - ADAM_B2) * (g * g)
    m_hat = m / (1.0 - ADAM_B1 ** ADAM_STEP)
    v_hat = v / (1.0 - ADAM_B2 ** ADAM_STEP)
    delta = -ADAM_LR * (m_hat / (jnp.sqrt(v_hat) + ADAM_EPS) + ADAM_WD * w)
    return delta, m, v


def _place():
    x, y, c = lax.axis_index("x"), lax.axis_index("y"), lax.axis_index("c")
    return x, y, c, 4 * x + 2 * y + c


def _flip(x, y, c, k):
    px = 1 - x if (k >> 2) & 1 else x
    py = 1 - y if (k >> 1) & 1 else y
    pc = 1 - c if k & 1 else c
    return (px, py, pc), 4 * px + 2 * py + pc


def _gather_copy(buf_ref, send_sems, recv_sems, sem0, k, mine):
    x, y, c, me = _place()
    dev, idx = _flip(x, y, c, k)
    blk = me if mine else idx
    return pltpu.make_async_remote_copy(
        src_ref=buf_ref.at[blk], dst_ref=buf_ref.at[blk],
        send_sem=send_sems.at[sem0 + k - 1], recv_sem=recv_sems.at[sem0 + k - 1],
        device_id=dev, device_id_type=MESH)


def _entry_gather(c_row, cctx_row, wmod, bmod, shards):
    n = len(shards)
    ncol = wmod.shape[1]

    def body(*refs):
        c_ref, cctx_ref, wmod_ref, bmod_ref = refs[:4]
        in_refs = refs[4:4 + n]
        mod_ref, cs_ref = refs[4 + n:6 + n]
        out_refs = refs[6 + n:6 + 2 * n]
        cbuf, pbuf = refs[6 + 2 * n:8 + 2 * n]
        cast_refs = refs[8 + 2 * n:8 + 3 * n]
        small_send, small_recv, send_sems, recv_sems, local_sems = refs[8 + 3 * n:]
        x, y, c, me = _place()
        sib = (x, y, 1 - c)
        chips = [(1 - x, y), (x, 1 - y), (1 - x, 1 - y)]

        cbuf[me] = jnp.broadcast_to(c_ref[...], (8, D))
        small = [_gather_copy(cbuf, small_send, small_recv, 0, k, True) for k in range(1, NDEV)]
        for cp in small:
            cp.start()

        def blk(px, py, pc):
            return 4 * px + 2 * py + pc

        def copy(w, k, block, to, src=None):
            dst = out_refs[w].at[block]
            return pltpu.make_async_remote_copy(
                src_ref=dst if src is None else src, dst_ref=dst,
                send_sem=send_sems.at[w, k], recv_sem=recv_sems.at[w, k], device_id=to, device_id_type=MESH)

        first, passed, mine = [], [], []
        for w in range(n):
            cast_refs[w][...] = in_refs[w][...].astype(BF16)
            m = pltpu.make_async_copy(cast_refs[w], out_refs[w].at[me], local_sems.at[w])
            m.start()
            mine.append(m)
            cps = [copy(w, 0, me, sib, src=cast_refs[w])]
            cps += [copy(w, 1 + j, me, (*chip, c), src=cast_refs[w]) for j, chip in enumerate(chips)]
            for cp in cps:
                cp.start()
            first += cps

        for k in range(1, NDEV):
            _gather_copy(cbuf, small_send, small_recv, 0, k, False).wait_recv()
        row = lax.broadcasted_iota(jnp.int32, (16, D), 0)
        call = jnp.where(row == 8, jnp.broadcast_to(cctx_ref[...], (16, D)), 0.0)
        for d in range(NDEV):
            call = jnp.where(row == d, jnp.concatenate([cbuf[d], cbuf[d]], axis=0), call)
        cs = call * _sigmoid(call)
        cs_ref[...] = cs
        pbuf[me] = _dot(cs.astype(BF16), wmod_ref[...].astype(BF16))
        small2 = [_gather_copy(pbuf, small_send, small_recv, NDEV - 1, k, True) for k in range(1, NDEV)]
        for cp in small2:
            cp.start()

        for w in range(n):
            for j, chip in enumerate(chips):
                b = blk(*chip, c)
                copy(w, 1 + j, b, (x, y, c)).wait_recv()
                fw = copy(w, 4 + j, b, sib)
                fw.start()
                passed.append(fw)
        for w in range(n):
            copy(w, 0, blk(x, y, 1 - c), (x, y, c)).wait_recv()
            for j, chip in enumerate(chips):
                copy(w, 4 + j, blk(*chip, 1 - c), (x, y, c)).wait_recv()

        for k in range(1, NDEV):
            _gather_copy(pbuf, small_send, small_recv, NDEV - 1, k, False).wait_recv()
        for d in range(NDEV):
            mod_ref[:, d * ncol:(d + 1) * ncol] = pbuf[d] + bmod_ref[:, d * ncol:(d + 1) * ncol]
        for cp in small + small2 + first + passed:
            cp.wait_send()
        for m in mine:
            m.wait()

    vm = pl.BlockSpec(memory_space=pltpu.VMEM)
    hbm = pl.BlockSpec(memory_space=pltpu.HBM)
    return pl.pallas_call(
        body, name="entry_gather",
        in_specs=[vm] * (4 + n), out_specs=[vm, vm] + [hbm] * n,
        out_shape=(jax.ShapeDtypeStruct((16, 6 * D), F32), jax.ShapeDtypeStruct((16, D), F32))
        + tuple(jax.ShapeDtypeStruct((NDEV,) + s.shape, BF16) for s in shards),
        scratch_shapes=[pltpu.VMEM((NDEV, 8, D), F32), pltpu.VMEM((NDEV, 16, ncol), F32)]
        + [pltpu.VMEM(s.shape, BF16) for s in shards]
        + [pltpu.SemaphoreType.DMA((2 * (NDEV - 1),)), pltpu.SemaphoreType.DMA((2 * (NDEV - 1),)),
           pltpu.SemaphoreType.DMA((n, 7)), pltpu.SemaphoreType.DMA((n, 7)), pltpu.SemaphoreType.DMA((n,))],
        compiler_params=_cparams(),
    )(c_row, cctx_row, wmod, bmod, *shards)


_HBM = pl.BlockSpec(memory_space=pltpu.HBM)
_SEMS = pl.BlockSpec(memory_space=pltpu.SEMAPHORE)
_EFFECT = pltpu.SideEffectType.DATAFLOW_SIDE_EFFECTING


def _exchange_copy(src_refs, land_refs, send_sems, recv_sems, w, k, scatter, landing_slot_is_mine):
    x, y, c, me = _place()
    dev, pidx = _flip(x, y, c, k)
    src = src_refs[w].at[pidx] if scatter else src_refs[w]
    slot = me if landing_slot_is_mine else pidx
    return pltpu.make_async_remote_copy(
        src_ref=src, dst_ref=land_refs[w].at[slot],
        send_sem=send_sems.at[w * (NDEV - 1) + k - 1], recv_sem=recv_sems.at[w * (NDEV - 1) + k - 1],
        device_id=dev, device_id_type=MESH)


def _exchange_start(srcs, *, scatter, name):
    n = len(srcs)
    lands = [lax.empty((NDEV,) + tuple(s.shape[-2:]), s.dtype) for s in srcs]

    def body(*refs):
        src_refs, land_refs = refs[:n], refs[n:2 * n]
        send_sems, recv_sems = refs[2 * n], refs[2 * n + 1]
        token = refs[-1]
        for w in range(n):
            for k in range(1, NDEV):
                _exchange_copy(src_refs, land_refs, send_sems, recv_sems, w, k, scatter, True).start()
        token[...] = jnp.zeros_like(token)

    arrs = list(srcs) + lands
    out_shape = ([pltpu.SemaphoreType.DMA((n * (NDEV - 1),)), pltpu.SemaphoreType.DMA((n * (NDEV - 1),))]
                 + [pltpu.HBM(a.shape, a.dtype) for a in arrs] + [jax.ShapeDtypeStruct((8, 128), F32)])
    res = pl.pallas_call(
        body, name=name, out_shape=tuple(out_shape),
        in_specs=[_HBM] * (2 * n),
        out_specs=tuple([_SEMS, _SEMS] + [_HBM] * (2 * n) + [pl.BlockSpec(memory_space=pltpu.VMEM)]),
        input_output_aliases={i: 2 + i for i in range(2 * n)},
        compiler_params=pltpu.CompilerParams(has_side_effects=_EFFECT),
    )(*[pltpu.with_memory_space_constraint(a, pltpu.HBM) for a in arrs])
    return res[:-1], res[-1]


def _exchange_wait(handles, after, *, scatter, name):
    n = (len(handles) - 2) // 2
    na = len(after)

    def body(*refs):
        src_refs, land_refs = refs[:n], refs[n:2 * n]
        send_sems, recv_sems = refs[2 * n], refs[2 * n + 1]
        for w in range(n):
            for k in range(1, NDEV):
                cp = _exchange_copy(src_refs, land_refs, send_sems, recv_sems, w, k, scatter, False)
                cp.wait_send()
                cp.wait_recv()

    arrs = handles[2:]
    res = pl.pallas_call(
        body, name=name, out_shape=tuple(pltpu.HBM(a.shape, a.dtype) for a in arrs),
        in_specs=[_HBM] * (2 * n) + [_SEMS, _SEMS] + [_HBM] * na,
        out_specs=tuple([_HBM] * (2 * n)),
        input_output_aliases={i: i for i in range(2 * n)},
        compiler_params=pltpu.CompilerParams(has_side_effects=_EFFECT),
    )(*arrs, handles[0], handles[1], *[pltpu.with_memory_space_constraint(a, pltpu.HBM) for a in after])
    return res[:n], res[n:]


_SAME_CORE = (2, 4, 6)


def _gather2_copy(src_ref, land_ref, send_sems, recv_sems, sem, k, block):
    x, y, c, _ = _place()
    dev, _ = _flip(x, y, c, k)
    dst = land_ref.at[block]
    return pltpu.make_async_remote_copy(
        src_ref=dst if src_ref is None else src_ref, dst_ref=dst,
        send_sem=send_sems.at[sem], recv_sem=recv_sems.at[sem], device_id=dev, device_id_type=MESH)


def _split_call(body, name, arrs, n_sems, sems=None, after=(), token=False):
    na = len(arrs)
    out_shape, out_specs = [], []
    if n_sems is not None:
        out_shape += [pltpu.SemaphoreType.DMA((n_sems,)), pltpu.SemaphoreType.DMA((n_sems,))]
        out_specs += [_SEMS, _SEMS]
    first = len(out_shape)
    out_shape += [pltpu.HBM(a.shape, a.dtype) for a in arrs]
    out_specs += [_HBM] * na
    if token:
        out_shape.append(jax.ShapeDtypeStruct((8, 128), F32))
        out_specs.append(pl.BlockSpec(memory_space=pltpu.VMEM))
    in_specs = [_HBM] * na + ([_SEMS, _SEMS] if sems is not None else []) + [_HBM] * len(after)
    args = [pltpu.with_memory_space_constraint(a, pltpu.HBM) for a in arrs] + list(sems or ()) \
        + [pltpu.with_memory_space_constraint(a, pltpu.HBM) for a in after]
    return pl.pallas_call(
        body, name=name, out_shape=tuple(out_shape), in_specs=in_specs, out_specs=tuple(out_specs),
        input_output_aliases={i: first + i for i in range(na)},
        compiler_params=pltpu.CompilerParams(has_side_effects=_EFFECT))(*args)


def _gather2_start(srcs, *, name):
    n = len(srcs)
    lands = [lax.empty((NDEV,) + tuple(s.shape), s.dtype) for s in srcs]

    def body(*refs):
        src_refs, land_refs = refs[:n], refs[n:2 * n]
        send_sems, recv_sems = refs[2 * n], refs[2 * n + 1]
        _, _, _, me = _place()
        for w in range(n):
            for slot, k in enumerate((1,) + _SAME_CORE):
                _gather2_copy(src_refs[w], land_refs[w], send_sems, recv_sems, 4 * w + slot, k, me).start()
        refs[-1][...] = jnp.zeros_like(refs[-1])

    res = _split_call(body, name, list(srcs) + lands, 4 * n, token=True)
    return res[:2], res[2:-1], res[-1]


def _gather2_arrived(sems, arrs, after, *, name):
    n = len(arrs) // 2

    def body(*refs):
        src_refs, land_refs = refs[:n], refs[n:2 * n]
        send_sems, recv_sems = refs[2 * n], refs[2 * n + 1]
        x, y, c, me = _place()
        for w in range(n):
            for slot, k in enumerate((1,) + _SAME_CORE):
                _, pidx = _flip(x, y, c, k)
                _gather2_copy(src_refs[w], land_refs[w], send_sems, recv_sems, 4 * w + slot, k, me).wait_send()
                _gather2_copy(None, land_refs[w], send_sems, recv_sems, 4 * w + slot, k, pidx).wait_recv()

    return _split_call(body, name, arrs, None, sems=sems, after=after)


def _gather2_forward(lands, *, name):
    n = len(lands)

    def body(*refs):
        land_refs = refs[:n]
        send_sems, recv_sems = refs[n], refs[n + 1]
        x, y, c, _ = _place()
        for w in range(n):
            for j, k in enumerate(_SAME_CORE):
                _, pidx = _flip(x, y, c, k)
                _gather2_copy(None, land_refs[w], send_sems, recv_sems, 3 * w + j, 1, pidx).start()
        refs[-1][...] = jnp.zeros_like(refs[-1])

    res = _split_call(body, name, list(lands), 3 * n, token=True)
    return res[:2], res[2:-1], res[-1]


def _gather2_wait(sems, lands, after, *, name):
    n = len(lands)

    def body(*refs):
        land_refs = refs[:n]
        send_sems, recv_sems = refs[n], refs[n + 1]
        x, y, c, _ = _place()
        for w in range(n):
            for j, k in enumerate(_SAME_CORE):
                _, mine = _flip(x, y, c, k)
                _, theirs = _flip(x, y, c, k ^ 1)
                _gather2_copy(None, land_refs[w], send_sems, recv_sems, 3 * w + j, 1, mine).wait_send()
                _gather2_copy(None, land_refs[w], send_sems, recv_sems, 3 * w + j, 1, theirs).wait_recv()

    return _split_call(body, name, list(lands), None, sems=sems, after=after)


def _cast_bf16(arrs, *, name, after=()):
    n = len(arrs)

    def body(*refs):
        for i_ref, o_ref in zip(refs[:n], refs[n + len(after):]):
            o_ref[...] = i_ref[...].astype(BF16)

    return pl.pallas_call(
        body, name=name, out_shape=tuple(jax.ShapeDtypeStruct(a.shape, BF16) for a in arrs),
        in_specs=[pl.BlockSpec(memory_space=pltpu.VMEM)] * n + [pl.BlockSpec(memory_space=pl.ANY)] * len(after),
        compiler_params=_cparams())(*arrs, *after)


def _rs_adam(me_arr, fulls, lands, w, m, v, *, name):
    rows = lands[0].shape[1]
    k = len(fulls)

    def body(me_ref, *refs):
        own_refs, land_refs = refs[:k], refs[k:2 * k]
        w_ref, m_ref, v_ref, g_ref, d_ref, mo_ref, vo_ref = refs[2 * k:]
        me = me_ref[0]
        parts = []
        for own_ref, land_ref in zip(own_refs, land_refs):
            acc = jnp.zeros(own_ref.shape, F32)
            for p in range(NDEV):
                acc = acc + jnp.where(p == me, own_ref[...], land_ref[p].astype(F32))
            parts.append(acc)
        acc = parts[0] if k == 1 else jnp.concatenate(parts, axis=1)
        g_ref[...] = acc
        d_ref[...], mo_ref[...], vo_ref[...] = _adam_math(w_ref[...], acc, m_ref[...], v_ref[...])

    sh = jax.ShapeDtypeStruct((rows, D), F32)
    whole2 = pl.BlockSpec((rows, D), lambda i, me: (0, 0))
    grid_spec = pltpu.PrefetchScalarGridSpec(
        num_scalar_prefetch=1, grid=(1,),
        in_specs=[pl.BlockSpec((rows, f.shape[1]), lambda i, me: (me[0], 0)) for f in fulls]
        + [pl.BlockSpec((NDEV, rows, l.shape[2]), lambda i, me: (0, 0, 0)) for l in lands]
        + [whole2, whole2, whole2],
        out_specs=[whole2] * 4)
    return pl.pallas_call(
        body, name=name, out_shape=(sh, sh, sh, sh), grid_spec=grid_spec, compiler_params=_cparams(1),
    )(me_arr, *fulls, *lands, w, m, v)


ROW_F, ROW_I, ROW_C, ROW_G1, ROW_LG = 0, 8, 16, 24, 32
PACK_ROWS = 40


def _small_sum(me_arr, pack, pack_land, sgw, sgw_land, wmod):
    ncol = wmod.shape[1]

    def body(me_ref, pack_ref, pland_ref, sgw_ref, sland_ref, wmod_ref, sum_ref, all_ref, sgw_sum_ref, part_ref):
        me = me_ref[0]
        acc = jnp.zeros((PACK_ROWS, D), F32)
        acc_w = jnp.zeros(sgw_ref.shape, F32)
        for p in range(NDEV):
            rows = jnp.where(p == me, pack_ref[...], pland_ref[p])
            all_ref[p] = rows
            acc = acc + rows
            acc_w = acc_w + jnp.where(p == me, sgw_ref[...], sland_ref[p])
        sum_ref[...] = acc
        sgw_sum_ref[...] = acc_w
        zero = jnp.zeros((1, D), F32)
        dcmod = jnp.concatenate([acc[ROW_C:ROW_C + 1], acc[ROW_C + 1:ROW_C + 2], zero, zero, zero, zero], axis=1)
        mine = jnp.zeros((1, ncol), F32)
        for d in range(NDEV):
            mine = mine + jnp.where(d == me, dcmod[:, d * ncol:(d + 1) * ncol], 0.0)
        part_ref[...] = _dot_nt(jnp.broadcast_to(mine, (8, ncol)).astype(BF16), wmod_ref[...].astype(BF16))

    vm = pl.BlockSpec(memory_space=pltpu.VMEM)
    return pl.pallas_call(
        body, name="small_sum",
        out_shape=(jax.ShapeDtypeStruct((PACK_ROWS, D), F32), jax.ShapeDtypeStruct((NDEV, PACK_ROWS, D), F32),
                   jax.ShapeDtypeStruct(sgw.shape, F32), jax.ShapeDtypeStruct((8, D), F32)),
        in_specs=[pl.BlockSpec(memory_space=pltpu.SMEM)] + [vm] * 5,
        compiler_params=_cparams(),
    )(me_arr, pack, pack_land, sgw, sgw_land, wmod)


def _cctx_final(me_arr, part, part_land, cctx_row, m_row, v_row):
    def body(me_ref, part_ref, land_ref, c_ref, m_ref, v_ref, g_ref, d_ref, mo_ref, vo_ref):
        me = me_ref[0]
        tot = jnp.zeros((8, D), F32)
        for p in range(NDEV):
            tot = tot + jnp.where(p == me, part_ref[...], land_ref[p])
        cc = c_ref[...]
        _, dsilu = _silu_parts(cc)
        g = tot[0:1, :] * dsilu
        g_ref[...] = g
        d_ref[...], mo_ref[...], vo_ref[...] = _adam_math(cc, g, m_ref[...], v_ref[...])

    row = jax.ShapeDtypeStruct((1, D), F32)
    vm = pl.BlockSpec(memory_space=pltpu.VMEM)
    return pl.pallas_call(
        body, name="cctx_final", out_shape=(row, row, row, row),
        in_specs=[pl.BlockSpec(memory_space=pltpu.SMEM)] + [vm] * 5,
        compiler_params=_cparams(),
    )(me_arr, part, part_land, cctx_row, m_row, v_row)


def _adam_small(s, sgw_g, params):
    names = ["norm1", "norm2", "norm_f", "sg_gain", "sg_b", "ret_logit_f", "ret_logit_b", "b_mod", "sg_w"]
    flat = [a for n in names for a in params[n]]

    def body(*refs):
        s_ref, sgw_ref = refs[0], refs[1]
        p_refs = refs[2:2 + 3 * len(names)]
        o_refs = refs[2 + 3 * len(names):]
        loss_ref = o_refs[-1]

        def row(r):
            return s_ref[r:r + 1, :]

        grads = {
            "norm1": row(ROW_I + 2) + row(ROW_C + 2),
            "norm2": row(ROW_F + 4),
            "norm_f": row(ROW_F + 0),
            "sg_gain": s_ref[ROW_I + 3:ROW_I + 4, 0:AW],
            "sg_b": s_ref[ROW_I + 4:ROW_I + 8, 0:DH],
            "sg_w": sgw_ref[...],
        }
        for j, n in enumerate(names):
            w_ref, m_ref, v_ref = p_refs[3 * j:3 * j + 3]
            g_ref, d_ref, mo_ref, vo_ref = o_refs[4 * j:4 * j + 4]
            w = w_ref[...]
            if n in ("ret_logit_f", "ret_logit_b"):
                r = ROW_LG + (0 if n == "ret_logit_f" else 1)
                g = s_ref[r:r + 1, 0:H] * _sigmoid(-w)
            elif n == "b_mod":
                rows = [row(ROW_I + 0) + row(ROW_C + 0), row(ROW_I + 1) + row(ROW_C + 1), row(ROW_G1),
                        row(ROW_F + 2), row(ROW_F + 3), row(ROW_F + 1)]
                g = jnp.concatenate(rows, axis=1)
            else:
                g = grads[n]
            g_ref[...] = g
            d_ref[...], mo_ref[...], vo_ref[...] = _adam_math(w, g, m_ref[...], v_ref[...])
        loss_ref[...] = jnp.full((1, 1), 0.5 / D, F32) * jnp.sum(row(ROW_F + 5), axis=1, keepdims=True)

    out_shape = []
    for n in names:
        w = params[n][0]
        out_shape += [jax.ShapeDtypeStruct(w.shape, F32)] * 4
    out_shape.append(jax.ShapeDtypeStruct((1, 1), F32))
    outs = pl.pallas_call(body, name="adam_small", out_shape=tuple(out_shape), compiler_params=_cparams())(
        s, sgw_g, *flat)
    return {n: tuple(outs[4 * j:4 * j + 4]) for j, n in enumerate(names)}, outs[-1]


def _wmod_grad(cs_all, dmod_cols, wmod, m, v):
    ncol = wmod.shape[1]

    def body(cs_ref, dm_ref, w_ref, m_ref, v_ref, g_ref, d_ref, mo_ref, vo_ref):
        g = _dot_tn(cs_ref[...].astype(BF16), dm_ref[...].astype(BF16))
        g_ref[...] = g
        d_ref[...], mo_ref[...], vo_ref[...] = _adam_math(w_ref[...], g, m_ref[...], v_ref[...])

    sh = jax.ShapeDtypeStruct((D, ncol), F32)
    return pl.pallas_call(
        body, name="wmod_grad", out_shape=(sh, sh, sh, sh),
        compiler_params=_cparams(),
    )(cs_all, dmod_cols, wmod, m, v)


def _rope_tables(t_len):
    n_freq = DH // 4
    inv = (ROPE_BASE ** (-np.arange(n_freq, dtype=np.float32) / n_freq)).astype(np.float32)
    tok = np.arange(t_len)
    rows = (tok // GRID_W).astype(np.float32)
    cols = (tok % GRID_W).astype(np.float32)
    ang_r = rows[:, None] * inv[None, :]
    ang_c = cols[:, None] * inv[None, :]
    cos = np.concatenate([np.cos(ang_r)] * 2 + [np.cos(ang_c)] * 2, axis=1).astype(np.float32)
    sin = np.concatenate([-np.sin(ang_r), np.sin(ang_r), -np.sin(ang_c), np.sin(ang_c)], axis=1).astype(np.float32)
    return jnp.asarray(cos), jnp.asarray(sin)


def _local_step(x, tgt, ctx, mod6, cmod6, norm1, norm2, normf, win, wout, ffn_weights, sgw, sgb, sggain, rlf, rlb,
                *, tm_a, tm_b, tm_f, tm_m, tm_r, tm_i, bt_w, after_first_grads=None, small_path=None,
                after_in_half=None):
    t_len = x.shape[0]
    cos, sin = _rope_tables(t_len)
    sgbt = jnp.broadcast_to(sgb[:, :, None], (G, C, 128))
    rlf = jnp.broadcast_to(rlf.reshape(H, 1), (H, 128))
    rlb = jnp.broadcast_to(rlb.reshape(H, 1), (H, 128))
    hc, kvc, scf, scb = _ctx_fwd(ctx, cmod6, norm1, win, rlf, rlb)
    hx, zuv, q, k, v, gfb, of, ya, sst = _fwd_a(x, mod6, norm1, win, sgw, sgbt, sggain, cos, sin, rlf, scf, tm=tm_a)
    wout = wout(hx) if callable(wout) else wout
    ob, ycat, y, x1, rst = _fwd_b(q, k, v, of, gfb, ya, x, mod6, wout, rlb, scb, tm=tm_b)
    wg, wu, wd = ffn_weights(x1) if callable(ffn_weights) else ffn_weights
    dx1, h2, da, db, hm, df, small_f = _ffn(x1, tgt, mod6, norm2, normf, wg, wu, wd, tm=tm_f)
    bt2 = min(2 * bt_w, t_len)
    d_wd, d_wd_b = _tn_matmul(hm, df, bm=DFF // 2, bt=bt2, name="dw_down")
    d_wg, d_wg_b = _tn_matmul(da, h2, bm=DFF // 2, bt=bt2, name="dw_gate")
    d_wu, d_wu_b = _tn_matmul(db, h2, bm=DFF // 2, bt=bt2, name="dw_up")
    dy, dya, dgfb, dof, dob, dg1 = _mid_bwd(dx1, y, of, ob, gfb, mod6, wout, tm=tm_m)
    d_wo, d_wo_b = _tn_matmul(ycat, dy, bm=D, bt=bt2, name="dw_out")
    if after_first_grads is not None:
        rlf = rlf + after_first_grads((d_wd_b, d_wg_b, d_wu_b, d_wo_b))
    dqf, dkf, dvf, dqb, dkb, dvb, dscf, dscb, dlg = _ret_bwd(q, k, v, dof, dob, sst, rst, rlf, rlb, tm=tm_r)
    gx, dz, small_i, dsgw = _in_bwd(x, zuv, dya, dqf, dkf, dvf, dqb, dkb, dvb, dgfb, dx1, cos, sin,
                                    mod6, norm1, win, sgw, sgbt, sggain, tm=tm_i)
    dwin_c, small_c, dlg = _ctx_bwd(ctx, hc, kvc, cmod6, norm1, win, rlf, rlb, dscf, dscb, dlg)
    pack = jnp.concatenate([small_f, small_i, small_c, dg1, dlg], axis=0)
    after = small_path(pack, dsgw) if small_path is not None else ()
    halves = []
    for j in range(2):
        halves.append(_tn_matmul(dz, hx, bm=INC // 2, bt=bt2, name="dw_in_%d" % j, init=dwin_c,
                                 init_rows=(KV_LO, KV_HI), after=after, cols=(j, D // 2)))
        if after_in_half is not None:
            after = after_in_half(j, halves[-1][1])
    grads = {"w_in": halves, "w_out": (d_wo, d_wo_b), "w_gate": (d_wg, d_wg_b), "w_up": (d_wu, d_wu_b),
             "w_down": (d_wd, d_wd_b)}
    return gx, grads, pack, dsgw


def kernel(x, c, ctx, c_ctx, w_mod, b_mod, norm1, w_in, sg_gain, sg_w, sg_b, ret_logit_f, ret_logit_b, w_out, norm2, w_gate, w_up, w_down, norm_f, loss_target, m_c_ctx, m_w_mod, m_b_mod, m_norm1, m_w_in, m_sg_gain, m_sg_w, m_sg_b, m_ret_logit_f, m_ret_logit_b, m_w_out, m_norm2, m_w_gate, m_w_up, m_w_down, m_norm_f, v_c_ctx, v_w_mod, v_b_mod, v_norm1, v_w_in, v_sg_gain, v_sg_w, v_sg_b, v_ret_logit_f, v_ret_logit_b, v_w_out, v_norm2, v_w_gate, v_w_up, v_w_down, v_norm_f):
    t_len = x.shape[1]
    tm = min(512, t_len)
    me = 4 * lax.axis_index("x") + 2 * lax.axis_index("y") + lax.axis_index("c")
    tr = lambda a: jnp.transpose(a[0])

    cctx_row = c_ctx.reshape(1, D)
    mod_all, cs_all, g_in = _entry_gather(c, cctx_row, w_mod[0], b_mod, [tr(w_in)])
    mod6 = lax.dynamic_slice(mod_all, (me, 0), (1, 6 * D)).reshape(6, D)
    cmod6 = mod_all[8].reshape(6, D)
    win_t = g_in.reshape(INC, D)

    (out_shard,) = _cast_bf16([w_out[0]], name="cast_out", after=[g_in])
    h_out, tok_out = _exchange_start([out_shard], scatter=False, name="wout_start")
    ffn_shards = _cast_bf16([tr(w_gate), tr(w_up), w_down[0]], name="cast_ffn", after=[h_out[2]])
    sems_ffn, arrs_ffn, tok = _gather2_start(ffn_shards, name="wffn_start")
    mod6 = mod6 + (tok_out[0, 0] + tok[0, 0])
    ffn = {}

    def place_own(own, lands, rows):
        return [lax.dynamic_update_slice(l, o[None], (me, 0, 0)).reshape(rows, D) for o, l in zip(own, lands)]

    def wout(after):
        own, lands = _exchange_wait(h_out, [after], scatter=False, name="wout_wait")
        arrs = _gather2_arrived(sems_ffn, arrs_ffn, [after], name="wffn_arrived")
        ffn["own"] = arrs[:3]
        ffn["sems"], ffn["lands"], tok_fwd = _gather2_forward(arrs[3:], name="wffn_forward")
        return place_own(own, lands, D)[0] + tok_fwd[0, 0].astype(BF16)

    def ffn_weights(after):
        lands = _gather2_wait(ffn["sems"], ffn["lands"], [after], name="wffn_wait")
        return place_own(ffn["own"], lands, DFF)

    rs, outs = {}, {}

    def after_first_grads(bf):
        rs["first"], tok_first = _exchange_start([b.reshape(NDEV, b.shape[0] // NDEV, D) for b in bf], scatter=True,
                                                 name="rs_first_start")
        return tok_first[0, 0]

    def small_path(pack, dsgw):
        rs["small"], _ = _exchange_start([pack, dsgw.reshape(G * C, C)], scatter=False, name="small_start")
        return [rs["small"][2], rs["small"][3]]

    def after_in_half(j, half_bf16):
        rs["in%d" % j], _ = _exchange_start([half_bf16.reshape(NDEV, INC // NDEV, D // 2)], scatter=True,
                                            name="rs_in%d_start" % j)
        return [rs["in%d" % j][2]]

    gx, grads, pack, dsgw = _local_step(
        x[0], loss_target[0], ctx[0], mod6, cmod6, norm1, norm2[0:1], norm_f.reshape(1, D), win_t, wout, ffn_weights,
        sg_w[0], sg_b[0], sg_gain, ret_logit_f, ret_logit_b,
        tm_a=tm, tm_b=tm, tm_f=min(256, t_len), tm_m=tm, tm_r=tm, tm_i=min(256, t_len), bt_w=min(1024, t_len),
        after_first_grads=after_first_grads, small_path=small_path, after_in_half=after_in_half)

    me_arr = me.reshape(1).astype(jnp.int32)
    (pack_own, sgw_own), (pack_land, sgw_land) = _exchange_wait(rs["small"], [rs["in1"][2]], scatter=False,
                                                               name="small_wait")
    psum_rows, pall, sgw_sum, part = _small_sum(me_arr, pack_own, pack_land, sgw_own, sgw_land, w_mod[0])
    h_cc, _ = _exchange_start([part], scatter=False, name="cctx_start")

    dmod_rows = (ROW_I + 0, ROW_I + 1, ROW_G1, ROW_F + 2, ROW_F + 3, ROW_F + 1)
    dmod_all = jnp.concatenate([pall[:, r, :] for r in dmod_rows], axis=1)
    dcmod = jnp.concatenate([psum_rows[ROW_C + 0:ROW_C + 1], psum_rows[ROW_C + 1:ROW_C + 2],
                             jnp.zeros((1, 4 * D), F32)], axis=1)
    dmod_mat = jnp.concatenate([dmod_all, jnp.pad(dcmod, ((0, 7), (0, 0)))], axis=0)
    ncol = 6 * D // NDEV
    dmod_cols = lax.dynamic_slice(dmod_mat, (0, me * ncol), (16, ncol))
    g_wm, d_wm, m_wm, v_wm = _wmod_grad(cs_all, dmod_cols, w_mod[0], m_w_mod[0], v_w_mod[0])
    outs["w_mod"] = (g_wm[None], d_wm[None], m_wm[None], v_wm[None])
    small_params = {
        "norm1": (norm1, m_norm1, v_norm1), "norm2": (norm2, m_norm2, v_norm2),
        "norm_f": tuple(a.reshape(1, D) for a in (norm_f, m_norm_f, v_norm_f)),
        "sg_gain": (sg_gain, m_sg_gain, v_sg_gain), "sg_b": (sg_b[0], m_sg_b[0], v_sg_b[0]),
        "ret_logit_f": (ret_logit_f, m_ret_logit_f, v_ret_logit_f),
        "ret_logit_b": (ret_logit_b, m_ret_logit_b, v_ret_logit_b),
        "b_mod": (b_mod, m_b_mod, v_b_mod), "sg_w": (sg_w[0], m_sg_w[0], v_sg_w[0])}
    small, loss = _adam_small(psum_rows, sgw_sum.reshape(G, C, C), small_params)
    back = {"norm_f": lambda a: a.reshape(D), "sg_b": lambda a: a[None], "sg_w": lambda a: a[None]}
    for name, vals in small.items():
        outs[name] = tuple(back.get(name, lambda a: a)(a) for a in vals)

    _, lands_first = _exchange_wait(rs["first"], [g_wm], scatter=True, name="rs_first_wait")

    def finish(name, fulls, lands, w, m, v, transposed):
        take = tr if transposed else (lambda a: a[0])
        res = _rs_adam(me_arr, fulls, lands, take(w), take(m), take(v), name="adam_" + name)
        put = (lambda a: jnp.transpose(a)[None]) if transposed else (lambda a: a[None])
        outs[name] = tuple(put(a) for a in res)
        return res[0]

    g_down = finish("w_down", [grads["w_down"][0]], [lands_first[0]], w_down, m_w_down, v_w_down, False)
    g_gate = finish("w_gate", [grads["w_gate"][0]], [lands_first[1]], w_gate, m_w_gate, v_w_gate, True)
    g_up = finish("w_up", [grads["w_up"][0]], [lands_first[2]], w_up, m_w_up, v_w_up, True)
    g_out = finish("w_out", [grads["w_out"][0]], [lands_first[3]], w_out, m_w_out, v_w_out, False)
    done = [g_down, g_gate, g_up, g_out]
    (part_own,), (part_land,) = _exchange_wait(h_cc, done, scatter=False, name="cctx_wait")
    res_cc = _cctx_final(me_arr, part_own, part_land, cctx_row, m_c_ctx.reshape(1, D), v_c_ctx.reshape(1, D))
    outs["c_ctx"] = tuple(a.reshape(D) for a in res_cc)
    lands_in = [_exchange_wait(rs["in%d" % j], done, scatter=True, name="rs_in%d_wait" % j)[1][0] for j in range(2)]
    finish("w_in", [h[0] for h in grads["w_in"]], lands_in, w_in, m_w_in, v_w_in, True)

    order = ["c_ctx", "w_mod", "b_mod", "norm1", "w_in", "sg_gain", "sg_w", "sg_b", "ret_logit_f", "ret_logit_b",
             "w_out", "norm2", "w_gate", "w_up", "w_down", "norm_f"]
    res = [loss.reshape(()), gx[None]]
    for j in range(4):
        res += [outs[name][j] for name in order]
    return tuple(res)
```

```python
import functools
import math

import numpy as np
import jax
import jax.numpy as jnp
from jax import lax
from jax.experimental import pallas as pl
from jax.experimental.pallas import tpu as pltpu

F32 = jnp.float32
BF16 = jnp.bfloat16

D = 1024
AW = 512
RW = 512
H = 4
DH = 128
G = 4
C = 128
CR = 256
DFF = 2816
INC = 3584
Q_LO = 2 * AW
KV_LO, VR_LO, G_LO = Q_LO + RW, Q_LO + 2 * RW, Q_LO + 3 * RW
KV_HI = G_LO
NDEV = 8
EPS = 1e-6
KSCALE = DH ** -0.5
ROPE_BASE = 10000.0
GRID_W = 64

ADAM_LR = 0.001
ADAM_B1 = 0.9
ADAM_B2 = 0.999
ADAM_EPS = 1e-08
ADAM_WD = 0.01
ADAM_STEP = 10

VMEM_LIMIT = 56 * 1024 * 1024
MESH = pl.DeviceIdType.MESH


def _cparams(n_grid=0, **kw):
    sem = ("arbitrary",) * n_grid if n_grid else None
    return pltpu.CompilerParams(dimension_semantics=sem, vmem_limit_bytes=VMEM_LIMIT, **kw)


def _dot(a, b):
    return jnp.dot(a, b, preferred_element_type=F32)


def _dot_nt(a, b):
    return lax.dot_general(a, b, (((1,), (1,)), ((), ())), preferred_element_type=F32)


def _dot_tn(a, b):
    return lax.dot_general(a, b, (((0,), (0,)), ((), ())), preferred_element_type=F32)


def _whole(shape):
    nd = len(shape)
    return pl.BlockSpec(shape, lambda *_: (0,) * nd, pipeline_mode=pl.Buffered(1))


def _acc(shape):
    nd = len(shape)
    return pl.BlockSpec(shape, lambda *_: (0,) * nd)


def _rows(tm, n):
    return pl.BlockSpec((tm, n), lambda i: (i, 0))


def _rows_rev(tm, n, nt):
    return pl.BlockSpec((tm, n), lambda i: (nt - 1 - i, 0))


def _cols(n, tm):
    return pl.BlockSpec((n, tm), lambda i: (0, i))


def _sigmoid(x):
    return 1.0 / (1.0 + jnp.exp(-x))


def _log_sigmoid(x):
    return jnp.minimum(x, 0.0) - jnp.log(1.0 + jnp.exp(-jnp.abs(x)))


_GK0 = math.sqrt(2.0 / math.pi)
_GK1 = 0.044715


def _gelu(x):
    x2 = x * x
    t = jnp.tanh(x * (_GK0 + (_GK0 * _GK1) * x2))
    h = 0.5 + 0.5 * t
    g = x * h
    dg = h + g * (1.0 - h) * ((2.0 * _GK0) + (6.0 * _GK0 * _GK1) * x2)
    return g, dg


def _silu_parts(a):
    s = _sigmoid(a)
    sa = a * s
    return sa, s + sa * (1.0 - s)


def _rope(t, cos, sins):
    n = t.shape[1]
    lane = lax.broadcasted_iota(jnp.int32, t.shape, 1)
    first = (lane & 63) < 32
    partner = jnp.where(first, pltpu.roll(t, n - 32, 1), pltpu.roll(t, 32, 1))
    return t * cos + partner * sins


def _headnorm(o):
    outs, rs = [], []
    for h in range(H):
        oh = o[:, h * DH:(h + 1) * DH]
        r = lax.rsqrt(jnp.mean(oh * oh, axis=-1, keepdims=True) + EPS)
        outs.append(oh * r)
        rs.append(r)
    return jnp.concatenate(outs, axis=1), rs


def _decay_consts(lg, forward):
    ii = lax.broadcasted_iota(jnp.int32, (CR, CR), 0).astype(F32)
    jj = lax.broadcasted_iota(jnp.int32, (CR, CR), 1).astype(F32)
    ic = lax.broadcasted_iota(jnp.int32, (CR, 1), 0).astype(F32)
    if forward:
        diff = ii - jj
        xi = jnp.exp(lg * (ic + 1.0))
        z = jnp.exp(lg * (CR - 1.0 - ic))
    else:
        diff = jj - ii
        xi = jnp.exp(lg * (CR - ic))
        z = jnp.exp(lg * ic)
    dm = jnp.where(diff >= 0.0, jnp.exp(lg * jnp.maximum(diff, 0.0)), 0.0)
    dec = jnp.exp(lg * float(CR))
    return dm, xi, z, dec, ic


def _ctx_fwd(ctx, cmod6, norm1, win, rlf, rlb):
    lc = ctx.shape[0]

    def body(ctx_ref, cmod_ref, n1_ref, win_ref, rlf_ref, rlb_ref, hc_ref, kvc_ref, scf_ref, scb_ref):
        x = ctx_ref[...]
        r = lax.rsqrt(jnp.mean(x * x, axis=-1, keepdims=True) + EPS)
        hc = (x * r) * (n1_ref[...] * (1.0 + cmod_ref[1:2, :])) + cmod_ref[0:1, :]
        hcb = hc.astype(BF16)
        hc_ref[...] = hcb
        kv = _dot_nt(hcb, win_ref[KV_LO:KV_HI, :])
        k = kv[:, :RW] * KSCALE
        v = kv[:, RW:]
        kvc_ref[:, :RW] = k
        kvc_ref[:, RW:] = v
        t = lax.broadcasted_iota(jnp.int32, (lc, 1), 0).astype(F32)
        lgf = _log_sigmoid(rlf_ref[...])
        lgb = _log_sigmoid(rlb_ref[...])
        for h in range(H):
            hs = slice(h * DH, (h + 1) * DH)
            wf = jnp.exp(lgf[h:h + 1, 0:1] * (lc - 1.0 - t))
            wb = jnp.exp(lgb[h:h + 1, 0:1] * t)
            vh = v[:, hs].astype(BF16)
            scf_ref[h] = _dot_tn((k[:, hs] * wf).astype(BF16), vh)
            scb_ref[h] = _dot_tn((k[:, hs] * wb).astype(BF16), vh)

    return pl.pallas_call(
        body, name="ctx_fwd",
        out_shape=(jax.ShapeDtypeStruct((lc, D), BF16), jax.ShapeDtypeStruct((lc, 2 * RW), F32),
                   jax.ShapeDtypeStruct((H, DH, DH), F32), jax.ShapeDtypeStruct((H, DH, DH), F32)),
        compiler_params=_cparams(),
    )(ctx, cmod6, norm1, win, rlf, rlb)


def _sg_forward(u, v, sgw_ref, sgbt_ref, sgg_ref, nc):
    ua, dgu = _gelu(u)
    va, dgv = _gelu(v)
    gain = sgg_ref[...]
    mixed, vn_b, vah_l, rv_l = [], [], [], []
    for g in range(G):
        gs = slice(g * DH, (g + 1) * DH)
        vag = va[:, gs]
        rv = lax.rsqrt(jnp.mean(vag * vag, axis=-1, keepdims=True) + EPS)
        vah = vag * rv
        vn = (vah * gain[:, gs]).astype(BF16)
        wg = sgw_ref[g].astype(BF16)
        bcol = sgbt_ref[g]
        rows = [_dot(wg, vn[c * C:(c + 1) * C, :]) + bcol for c in range(nc)]
        mixed.append(jnp.concatenate(rows, axis=0) if nc > 1 else rows[0])
        vn_b.append(vn)
        vah_l.append(vah)
        rv_l.append(rv)
    mixed = jnp.concatenate(mixed, axis=1)
    return ua * mixed, (ua, dgu, dgv, mixed, vn_b, vah_l, rv_l)


def _fwd_a(x, mod6, norm1, win, sgw, sgbt, sggain, cos, sin, rlf, scf, *, tm):
    t_len = x.shape[0]
    nt, nc, ncr = t_len // tm, tm // C, tm // CR

    def body(x_ref, mod_ref, n1_ref, win_ref, sgw_ref, sgbt_ref, sgg_ref, cos_ref, sin_ref, rlf_ref, scf_ref,
             hx_ref, zuv_ref, q_ref, k_ref, v_ref, gfb_ref, of_ref, ya_ref, sst_ref, s_scr):
        i = pl.program_id(0)

        @pl.when(i == 0)
        def _():
            s_scr[...] = scf_ref[...]

        x = x_ref[...]
        r = lax.rsqrt(jnp.mean(x * x, axis=-1, keepdims=True) + EPS)
        hx = (x * r) * (n1_ref[...] * (1.0 + mod_ref[1:2, :])) + mod_ref[0:1, :]
        hxb = hx.astype(BF16)
        hx_ref[...] = hxb
        z = _dot_nt(hxb, win_ref[...])
        zuv_ref[...] = z[:, :2 * AW]
        gfb_ref[...] = z[:, G_LO:INC].astype(BF16)
        cos = jnp.tile(cos_ref[...], (1, H))
        sins = jnp.tile(sin_ref[...], (1, H))
        q_ref[...] = _rope(z[:, Q_LO:KV_LO], cos, sins).astype(BF16)
        k_ref[...] = (_rope(z[:, KV_LO:VR_LO], cos, sins) * KSCALE).astype(BF16)
        v_ref[...] = z[:, VR_LO:G_LO].astype(BF16)
        ya, _ = _sg_forward(z[:, :AW], z[:, AW:2 * AW], sgw_ref, sgbt_ref, sgg_ref, nc)
        ya_ref[...] = ya.astype(BF16)

        lg = _log_sigmoid(rlf_ref[...])
        for h in range(H):
            dm, xi, zc, dec, _ = _decay_consts(lg[h:h + 1, 0:1], True)
            hs = slice(h * DH, (h + 1) * DH)
            for c in range(ncr):
                rs = slice(c * CR, (c + 1) * CR)
                qc, kc, vc = q_ref[rs, hs], k_ref[rs, hs], v_ref[rs, hs]
                s = s_scr[h]
                sst_ref[c, h] = s
                a = (_dot_nt(qc, kc) * dm).astype(BF16)
                of_ref[rs, hs] = (_dot(a, vc) + xi * _dot(qc, s.astype(BF16))).astype(BF16)
                kz = (kc.astype(F32) * zc).astype(BF16)
                s_scr[h] = dec * s + _dot_tn(kz, vc)

    n_chunks = t_len // CR
    return pl.pallas_call(
        body, name="fwd_a", grid=(nt,),
        in_specs=[_rows(tm, D), _whole((6, D)), _whole((1, D)), _whole((INC, D)), _whole((G, C, C)),
                  _whole((G, C, 128)), _whole((1, AW)), _rows(tm, DH), _rows(tm, DH), _whole((H, 128)),
                  _whole((H, DH, DH))],
        out_specs=[_rows(tm, D), _rows(tm, 2 * AW), _rows(tm, RW), _rows(tm, RW), _rows(tm, RW),
                   _rows(tm, 2 * RW), _rows(tm, RW), _rows(tm, AW),
                   pl.BlockSpec((ncr, H, DH, DH), lambda i: (i, 0, 0, 0))],
        out_shape=(jax.ShapeDtypeStruct((t_len, D), BF16), jax.ShapeDtypeStruct((t_len, 2 * AW), F32),
                   jax.ShapeDtypeStruct((t_len, RW), BF16), jax.ShapeDtypeStruct((t_len, RW), BF16),
                   jax.ShapeDtypeStruct((t_len, RW), BF16), jax.ShapeDtypeStruct((t_len, 2 * RW), BF16),
                   jax.ShapeDtypeStruct((t_len, RW), BF16), jax.ShapeDtypeStruct((t_len, AW), BF16),
                   jax.ShapeDtypeStruct((n_chunks, H, DH, DH), F32)),
        scratch_shapes=[pltpu.VMEM((H, DH, DH), F32)],
        compiler_params=_cparams(1),
    )(x, mod6, norm1, win, sgw, sgbt, sggain, cos, sin, rlf, scf)


def _fwd_b(q, k, v, of, gfb, ya, x, mod6, wout, rlb, scb, *, tm):
    t_len = x.shape[0]
    nt, nc = t_len // tm, tm // CR

    def body(q_ref, k_ref, v_ref, of_ref, gfb_ref, ya_ref, x_ref, mod_ref, wout_ref, rlb_ref, scb_ref,
             ob_ref, ycat_ref, y_ref, x1_ref, rst_ref, r_scr):
        i = pl.program_id(0)

        @pl.when(i == 0)
        def _():
            r_scr[...] = scb_ref[...]

        lg = _log_sigmoid(rlb_ref[...])
        for h in range(H):
            dm, xi, zc, dec, _ = _decay_consts(lg[h:h + 1, 0:1], False)
            hs = slice(h * DH, (h + 1) * DH)
            for c in reversed(range(nc)):
                rs = slice(c * CR, (c + 1) * CR)
                qc, kc, vc = q_ref[rs, hs], k_ref[rs, hs], v_ref[rs, hs]
                s = r_scr[h]
                rst_ref[c, h] = s
                a = (_dot_nt(qc, kc) * dm).astype(BF16)
                ob_ref[rs, hs] = (_dot(a, vc) + xi * _dot(qc, s.astype(BF16))).astype(BF16)
                kz = (kc.astype(F32) * zc).astype(BF16)
                r_scr[h] = dec * s + _dot_tn(kz, vc)

        gfb = gfb_ref[...].astype(F32)
        sf, _ = _silu_parts(gfb[:, :RW])
        sb, _ = _silu_parts(gfb[:, RW:])
        hnf, _ = _headnorm(of_ref[...].astype(F32))
        hnb, _ = _headnorm(ob_ref[...].astype(F32))
        yr = sf * hnf + sb * hnb
        ycat = jnp.concatenate([ya_ref[...], yr.astype(BF16)], axis=1)
        ycat_ref[...] = ycat
        y = _dot(ycat, wout_ref[...])
        y_ref[...] = y.astype(BF16)
        x1_ref[...] = x_ref[...] + mod_ref[2:3, :] * y

    n_chunks = t_len // CR
    rr = functools.partial(_rows_rev, nt=nt)
    return pl.pallas_call(
        body, name="fwd_b", grid=(nt,),
        in_specs=[rr(tm, RW), rr(tm, RW), rr(tm, RW), rr(tm, RW), rr(tm, 2 * RW), rr(tm, AW), rr(tm, D),
                  _whole((6, D)), _whole((D, D)), _whole((H, 128)), _whole((H, DH, DH))],
        out_specs=[rr(tm, RW), rr(tm, D), rr(tm, D), rr(tm, D),
                   pl.BlockSpec((nc, H, DH, DH), lambda i: (nt - 1 - i, 0, 0, 0))],
        out_shape=(jax.ShapeDtypeStruct((t_len, RW), BF16), jax.ShapeDtypeStruct((t_len, D), BF16),
                   jax.ShapeDtypeStruct((t_len, D), BF16), jax.ShapeDtypeStruct((t_len, D), F32),
                   jax.ShapeDtypeStruct((n_chunks, H, DH, DH), F32)),
        scratch_shapes=[pltpu.VMEM((H, DH, DH), F32)],
        compiler_params=_cparams(1),
    )(q, k, v, of, gfb, ya, x, mod6, wout, rlb, scb)


def _ffn(x1, tgt, mod6, norm2, normf, wg, wu, wd, *, tm):
    t_len = x1.shape[0]
    nt = t_len // tm

    def body(x1_ref, tgt_ref, mod_ref, n2_ref, nf_ref, wg_ref, wu_ref, wd_ref,
             dx1_ref, h2_ref, da_ref, db_ref, hm_ref, df_ref, small_ref):
        i = pl.program_id(0)

        @pl.when(i == 0)
        def _():
            small_ref[...] = jnp.zeros_like(small_ref)

        sh2, sc2, g2 = mod_ref[3:4, :], mod_ref[4:5, :], mod_ref[5:6, :]
        n2, nf = n2_ref[...], nf_ref[...]
        x1 = x1_ref[...]
        r2 = lax.rsqrt(jnp.mean(x1 * x1, axis=-1, keepdims=True) + EPS)
        x1h = x1 * r2
        w2 = n2 * (1.0 + sc2)
        h2b = (x1h * w2 + sh2).astype(BF16)
        h2_ref[...] = h2b
        a = _dot_nt(h2b, wg_ref[...])
        b = _dot_nt(h2b, wu_ref[...])
        sa, dsa = _silu_parts(a)
        hmb = (sa * b).astype(BF16)
        hm_ref[...] = hmb.T
        f = _dot(hmb, wd_ref[...])
        x2 = x1 + g2 * f
        r3 = lax.rsqrt(jnp.mean(x2 * x2, axis=-1, keepdims=True) + EPS)
        x2h = x2 * r3
        diff = x2h * nf - tgt_ref[...]
        small_ref[5:6, :] += jnp.sum(diff * diff, axis=0, keepdims=True)
        dout = diff * (1.0 / D)
        small_ref[0:1, :] += jnp.sum(dout * x2h, axis=0, keepdims=True)
        dyg = dout * nf
        dx2 = r3 * (dyg - x2h * jnp.mean(dyg * x2h, axis=-1, keepdims=True))
        small_ref[1:2, :] += jnp.sum(dx2 * f, axis=0, keepdims=True)
        dfb = (dx2 * g2).astype(BF16)
        df_ref[...] = dfb
        dhm = _dot_nt(dfb, wd_ref[...])
        dbb = (dhm * sa).astype(BF16)
        dab = (dhm * b * dsa).astype(BF16)
        da_ref[...] = dab.T
        db_ref[...] = dbb.T
        dh2 = _dot(dab, wg_ref[...]) + _dot(dbb, wu_ref[...])
        small_ref[2:3, :] += jnp.sum(dh2, axis=0, keepdims=True)
        dhx = dh2 * x1h
        small_ref[3:4, :] += jnp.sum(dhx, axis=0, keepdims=True) * n2
        small_ref[4:5, :] += jnp.sum(dhx, axis=0, keepdims=True) * (1.0 + sc2)
        dyg2 = dh2 * w2
        dx1_ref[...] = dx2 + r2 * (dyg2 - x1h * jnp.mean(dyg2 * x1h, axis=-1, keepdims=True))

    return pl.pallas_call(
        body, name="ffn", grid=(nt,),
        in_specs=[_rows(tm, D), _rows(tm, D), _whole((6, D)), _whole((1, D)), _whole((1, D)),
                  _whole((DFF, D)), _whole((DFF, D)), _whole((DFF, D))],
        out_specs=[_rows(tm, D), _rows(tm, D), _cols(DFF, tm), _cols(DFF, tm), _cols(DFF, tm), _rows(tm, D),
                   _acc((8, D))],
        out_shape=(jax.ShapeDtypeStruct((t_len, D), F32), jax.ShapeDtypeStruct((t_len, D), BF16),
                   jax.ShapeDtypeStruct((DFF, t_len), BF16), jax.ShapeDtypeStruct((DFF, t_len), BF16),
                   jax.ShapeDtypeStruct((DFF, t_len), BF16), jax.ShapeDtypeStruct((t_len, D), BF16),
                   jax.ShapeDtypeStruct((8, D), F32)),
        compiler_params=_cparams(1),
    )(x1, tgt, mod6, norm2, normf, wg, wu, wd)


def _mid_bwd(dx1, y, of, ob, gfb, mod6, wout, *, tm):
    t_len = dx1.shape[0]
    nt = t_len // tm

    def body(dx1_ref, y_ref, of_ref, ob_ref, gfb_ref, mod_ref, wout_ref,
             dy_ref, dya_ref, dgfb_ref, dof_ref, dob_ref, dg1_ref):
        i = pl.program_id(0)

        @pl.when(i == 0)
        def _():
            dg1_ref[...] = jnp.zeros_like(dg1_ref)

        dx1 = dx1_ref[...]
        dg1_ref[0:1, :] += jnp.sum(dx1 * y_ref[...].astype(F32), axis=0, keepdims=True)
        dyb = (dx1 * mod_ref[2:3, :]).astype(BF16)
        dy_ref[...] = dyb
        dycat = _dot_nt(dyb, wout_ref[...])
        dya_ref[...] = dycat[:, :AW]
        dyr = dycat[:, AW:]
        gfb = gfb_ref[...].astype(F32)
        for o_ref, do_ref, lo in ((of_ref, dof_ref, 0), (ob_ref, dob_ref, RW)):
            sg, dsg = _silu_parts(gfb[:, lo:lo + RW])
            o = o_ref[...].astype(F32)
            hn, rs = _headnorm(o)
            dgfb_ref[:, lo:lo + RW] = (dyr * hn * dsg).astype(BF16)
            dhn = dyr * sg
            for h in range(H):
                hs = slice(h * DH, (h + 1) * DH)
                oh, dh = hn[:, hs], dhn[:, hs]
                do_ref[:, hs] = (rs[h] * (dh - oh * jnp.mean(dh * oh, axis=-1, keepdims=True))).astype(BF16)

    return pl.pallas_call(
        body, name="mid_bwd", grid=(nt,),
        in_specs=[_rows(tm, D), _rows(tm, D), _rows(tm, RW), _rows(tm, RW), _rows(tm, 2 * RW),
                  _whole((6, D)), _whole((D, D))],
        out_specs=[_rows(tm, D), _rows(tm, AW), _rows(tm, 2 * RW), _rows(tm, RW), _rows(tm, RW), _acc((8, D))],
        out_shape=(jax.ShapeDtypeStruct((t_len, D), BF16), jax.ShapeDtypeStruct((t_len, AW), F32),
                   jax.ShapeDtypeStruct((t_len, 2 * RW), BF16), jax.ShapeDtypeStruct((t_len, RW), BF16),
                   jax.ShapeDtypeStruct((t_len, RW), BF16), jax.ShapeDtypeStruct((8, D), F32)),
        compiler_params=_cparams(1),
    )(dx1, y, of, ob, gfb, mod6, wout)


def _ret_bwd_dir(q_ref, k_ref, v_ref, do_ref, st_ref, dq_ref, dk_ref, dv_ref, ds_scr, dlg_scr, lg, forward, nc, row0):
    order = reversed(range(nc)) if forward else range(nc)
    order = list(order)
    for h in range(H):
        dm, xi, zc, dec, ic = _decay_consts(lg[h:h + 1, 0:1], forward)
        hs = slice(h * DH, (h + 1) * DH)
        if forward:
            w_qi, w_qc, w_ki, w_ks = ic, ic + 1.0, -ic, (CR - 1.0) - ic
        else:
            w_qi, w_qc, w_ki, w_ks = -ic, CR - ic, ic, ic
        acc = jnp.zeros((1, DH), F32)
        for c in order:
            rs = slice(c * CR, (c + 1) * CR)
            qc, kc, vc, doc = q_ref[rs, hs], k_ref[rs, hs], v_ref[rs, hs], do_ref[rs, hs]
            s = st_ref[c, h]
            dsn = ds_scr[h]
            sb, dsnb = s.astype(BF16), dsn.astype(BF16)
            qf, kf = qc.astype(F32), kc.astype(F32)
            a = (_dot_nt(qc, kc) * dm).astype(BF16)
            da = (_dot_nt(doc, vc) * dm).astype(BF16)
            xdo = (xi * doc.astype(F32)).astype(BF16)
            kz = (kf * zc).astype(BF16)
            vz = (vc.astype(F32) * zc).astype(BF16)
            dq_i = _dot(da, kc)
            dq_c = _dot_nt(xdo, sb)
            dk_i = _dot_tn(da, qc)
            dk_s = _dot_nt(vz, dsnb)
            dq_ref[rs, hs] = dq_i + dq_c
            dk_ref[rs, hs] = dk_i + dk_s
            dv_ref[rs, hs] = _dot_tn(a, doc) + _dot(kz, dsnb)
            rowterm = qf * (w_qi * dq_i + w_qc * dq_c) + kf * (w_ki * dk_i + w_ks * dk_s)
            acc = acc + jnp.sum(rowterm, axis=0, keepdims=True)
            acc = acc + (float(CR) * dec) * jnp.sum(s * dsn, axis=0, keepdims=True)
            ds_scr[h] = _dot_tn((xi * qf).astype(BF16), doc) + dec * dsn
        dlg_scr[row0 + h:row0 + h + 1, :] += acc


def _ret_bwd(q, k, v, dof, dob, sst, rst, rlf, rlb, *, tm):
    t_len = q.shape[0]
    nt, nc = t_len // tm, tm // CR

    def body(qf_ref, kf_ref, vf_ref, dof_ref, sst_ref, qb_ref, kb_ref, vb_ref, dob_ref, rst_ref, rlf_ref, rlb_ref,
             dqf_ref, dkf_ref, dvf_ref, dqb_ref, dkb_ref, dvb_ref, dscf_ref, dscb_ref, dlg_ref,
             dsf_scr, dsb_scr, dlg_scr):
        i = pl.program_id(0)

        @pl.when(i == 0)
        def _():
            dsf_scr[...] = jnp.zeros_like(dsf_scr)
            dsb_scr[...] = jnp.zeros_like(dsb_scr)
            dlg_scr[...] = jnp.zeros_like(dlg_scr)

        lgf = _log_sigmoid(rlf_ref[...])
        lgb = _log_sigmoid(rlb_ref[...])
        _ret_bwd_dir(qf_ref, kf_ref, vf_ref, dof_ref, sst_ref, dqf_ref, dkf_ref, dvf_ref, dsf_scr, dlg_scr, lgf, True, nc, 0)
        _ret_bwd_dir(qb_ref, kb_ref, vb_ref, dob_ref, rst_ref, dqb_ref, dkb_ref, dvb_ref, dsb_scr, dlg_scr, lgb, False, nc, H)

        @pl.when(i == nt - 1)
        def _():
            dscf_ref[...] = dsf_scr[...]
            dscb_ref[...] = dsb_scr[...]
            tot = jnp.broadcast_to(jnp.sum(dlg_scr[...], axis=1, keepdims=True), (8, 128))
            ri = lax.broadcasted_iota(jnp.int32, (8, 128), 0)
            li = lax.broadcasted_iota(jnp.int32, (8, 128), 1)
            dlg_ref[...] = jnp.zeros_like(dlg_ref)
            dlg_ref[0:1, 0:128] = jnp.sum(jnp.where(ri == li, tot, 0.0), axis=0, keepdims=True)
            dlg_ref[1:2, 0:128] = jnp.sum(jnp.where(ri - H == li, tot, 0.0), axis=0, keepdims=True)

    rr = functools.partial(_rows_rev, nt=nt)
    st_f = pl.BlockSpec((nc, H, DH, DH), lambda i: (nt - 1 - i, 0, 0, 0))
    st_b = pl.BlockSpec((nc, H, DH, DH), lambda i: (i, 0, 0, 0))
    tok = jax.ShapeDtypeStruct((t_len, RW), F32)
    st = jax.ShapeDtypeStruct((H, DH, DH), F32)
    return pl.pallas_call(
        body, name="ret_bwd", grid=(nt,),
        in_specs=[rr(tm, RW), rr(tm, RW), rr(tm, RW), rr(tm, RW), st_f,
                  _rows(tm, RW), _rows(tm, RW), _rows(tm, RW), _rows(tm, RW), st_b,
                  _whole((H, 128)), _whole((H, 128))],
        out_specs=[rr(tm, RW), rr(tm, RW), rr(tm, RW), _rows(tm, RW), _rows(tm, RW), _rows(tm, RW),
                   _acc((H, DH, DH)), _acc((H, DH, DH)), _acc((8, D))],
        out_shape=(tok, tok, tok, tok, tok, tok, st, st, jax.ShapeDtypeStruct((8, D), F32)),
        scratch_shapes=[pltpu.VMEM((H, DH, DH), F32), pltpu.VMEM((H, DH, DH), F32), pltpu.VMEM((8, 128), F32)],
        compiler_params=_cparams(1),
    )(q, k, v, dof, sst, q, k, v, dob, rst, rlf, rlb)


def _in_bwd(x, zuv, dya, dqf, dkf, dvf, dqb, dkb, dvb, dgfb, dx1, cos, sin, mod6, norm1, win, sgw, sgbt, sggain, *, tm):
    t_len = x.shape[0]
    nt, nc = t_len // tm, tm // C

    def body(x_ref, zuv_ref, dya_ref, dqf_ref, dkf_ref, dvf_ref, dqb_ref, dkb_ref, dvb_ref, dgfb_ref, dx1_ref,
             cos_ref, sin_ref, mod_ref, n1_ref, win_ref, sgw_ref, sgbt_ref, sgg_ref,
             gx_ref, dz_ref, small_ref, dsgw_ref, dsgbt_ref):
        i = pl.program_id(0)

        @pl.when(i == 0)
        def _():
            small_ref[...] = jnp.zeros_like(small_ref)
            dsgw_ref[...] = jnp.zeros_like(dsgw_ref)
            dsgbt_ref[...] = jnp.zeros_like(dsgbt_ref)

        zuv = zuv_ref[...]
        _, (ua, dgu, dgv, mixed, vn_b, vah_l, rv_l) = _sg_forward(zuv[:, :AW], zuv[:, AW:], sgw_ref, sgbt_ref, sgg_ref, nc)
        dya = dya_ref[...]
        dz_ref[:, 0:AW] = (dya * mixed * dgu).astype(BF16)
        dmixed = dya * ua
        gain = sgg_ref[...]
        for g in range(G):
            gs = slice(g * DH, (g + 1) * DH)
            dmg = dmixed[:, gs]
            dmb = dmg.astype(BF16)
            wgt = sgw_ref[g].astype(BF16)
            dsw = jnp.zeros((C, C), F32)
            dsb = jnp.zeros((C, DH), F32)
            rows = []
            for c in range(nc):
                rs = slice(c * C, (c + 1) * C)
                dsw = dsw + _dot_nt(dmb[rs, :], vn_b[g][rs, :])
                dsb = dsb + dmg[rs, :]
                rows.append(_dot_tn(wgt, dmb[rs, :]))
            dsgw_ref[g] += dsw
            dsgbt_ref[g] += dsb
            dvn = jnp.concatenate(rows, axis=0) if nc > 1 else rows[0]
            vah, rv = vah_l[g], rv_l[g]
            small_ref[3:4, gs] += jnp.sum(dvn * vah, axis=0, keepdims=True)
            dyg = dvn * gain[:, gs]
            dva = rv * (dyg - vah * jnp.mean(dyg * vah, axis=-1, keepdims=True))
            dz_ref[:, AW + g * DH:AW + (g + 1) * DH] = (dva * dgv[:, gs]).astype(BF16)

        cos = jnp.tile(cos_ref[...], (1, H))
        nsins = -jnp.tile(sin_ref[...], (1, H))
        dz_ref[:, Q_LO:KV_LO] = _rope(dqf_ref[...] + dqb_ref[...], cos, nsins).astype(BF16)
        dz_ref[:, KV_LO:VR_LO] = (_rope(dkf_ref[...] + dkb_ref[...], cos, nsins) * KSCALE).astype(BF16)
        dz_ref[:, VR_LO:G_LO] = (dvf_ref[...] + dvb_ref[...]).astype(BF16)
        dz_ref[:, G_LO:INC] = dgfb_ref[...]

        dhx = _dot(dz_ref[...], win_ref[...])
        x = x_ref[...]
        r = lax.rsqrt(jnp.mean(x * x, axis=-1, keepdims=True) + EPS)
        xh = x * r
        n1, sc1 = n1_ref[...], mod_ref[1:2, :]
        small_ref[0:1, :] += jnp.sum(dhx, axis=0, keepdims=True)
        dhxx = jnp.sum(dhx * xh, axis=0, keepdims=True)
        small_ref[1:2, :] += dhxx * n1
        small_ref[2:3, :] += dhxx * (1.0 + sc1)
        dyg = dhx * (n1 * (1.0 + sc1))
        gx_ref[...] = dx1_ref[...] + r * (dyg - xh * jnp.mean(dyg * xh, axis=-1, keepdims=True))

        @pl.when(i == nt - 1)
        def _():
            ri = lax.broadcasted_iota(jnp.int32, (C, DH), 0)
            li = lax.broadcasted_iota(jnp.int32, (C, DH), 1)
            for g in range(G):
                tot = jnp.broadcast_to(jnp.sum(dsgbt_ref[g], axis=1, keepdims=True), (C, DH))
                small_ref[4 + g:5 + g, 0:DH] = jnp.sum(jnp.where(ri == li, tot, 0.0), axis=0, keepdims=True)

    tok = functools.partial(_rows, tm)
    return pl.pallas_call(
        body, name="in_bwd", grid=(nt,),
        in_specs=[tok(D), tok(2 * AW), tok(AW), tok(RW), tok(RW), tok(RW), tok(RW), tok(RW), tok(RW),
                  tok(2 * RW), tok(D), tok(DH), tok(DH), _whole((6, D)), _whole((1, D)), _whole((INC, D)),
                  _whole((G, C, C)), _whole((G, C, 128)), _whole((1, AW))],
        out_specs=[tok(D), tok(INC), _acc((8, D)), _acc((G, C, C))],
        out_shape=(jax.ShapeDtypeStruct((t_len, D), F32), jax.ShapeDtypeStruct((t_len, INC), BF16),
                   jax.ShapeDtypeStruct((8, D), F32), jax.ShapeDtypeStruct((G, C, C), F32)),
        scratch_shapes=[pltpu.VMEM((G, C, 128), F32)],
        compiler_params=_cparams(1),
    )(x, zuv, dya, dqf, dkf, dvf, dqb, dkb, dvb, dgfb, dx1, cos, sin, mod6, norm1, win, sgw, sgbt, sggain)


def _tn_matmul(at, b, *, bm, bt, name, init=None, init_rows=None, after=(), cols=None, transposed=True):
    m, t_len = at.shape if transposed else at.shape[::-1]
    cj, n = (0, b.shape[1]) if cols is None else cols
    ni, ntt = m // bm, t_len // bt
    has_init = init is not None

    def body(*refs):
        o_ref, ob_ref = refs[-2:]
        a_ref, b_ref = refs[:2]
        init_ref = refs[2] if has_init else None
        i, t = pl.program_id(0), pl.program_id(1)

        @pl.when(t == 0)
        def _():
            o_ref[...] = jnp.zeros_like(o_ref)

        if has_init:
            for ib in range(ni):
                lo, hi = max(init_rows[0], ib * bm), min(init_rows[1], (ib + 1) * bm)
                if lo < hi:
                    @pl.when(jnp.logical_and(t == 0, i == ib))
                    def _(lo=lo, hi=hi, ib=ib):
                        o_ref[lo - ib * bm:hi - ib * bm, :] = init_ref[lo - init_rows[0]:hi - init_rows[0], :]

        o_ref[...] += (_dot if transposed else _dot_tn)(a_ref[...], b_ref[...])

        @pl.when(t == ntt - 1)
        def _():
            ob_ref[...] = o_ref[...].astype(BF16)

    a_spec = pl.BlockSpec((bm, bt), lambda i, t: (i, t)) if transposed else pl.BlockSpec((bt, bm), lambda i, t: (t, i))
    in_specs = [a_spec, pl.BlockSpec((bt, n), lambda i, t: (t, cj))]
    args = [at, b]
    if has_init:
        in_specs.append(pl.BlockSpec((init.shape[0], n), lambda i, t: (0, cj), pipeline_mode=pl.Buffered(1)))
        args.append(init)
    for arr in after:
        in_specs.append(pl.BlockSpec(memory_space=pl.ANY))
        args.append(arr)
    out_spec = pl.BlockSpec((bm, n), lambda i, t: (i, 0))
    return pl.pallas_call(
        body, name=name, grid=(ni, ntt),
        in_specs=in_specs,
        out_specs=[out_spec, out_spec],
        out_shape=(jax.ShapeDtypeStruct((m, n), F32), jax.ShapeDtypeStruct((m, n), BF16)),
        compiler_params=_cparams(2),
    )(*args)


def _ctx_bwd(ctx, hc, kvc, cmod6, norm1, win, rlf, rlb, dscf, dscb, dlg_in):
    lc = ctx.shape[0]

    def body(ctx_ref, hc_ref, kvc_ref, cmod_ref, n1_ref, win_ref, rlf_ref, rlb_ref, dscf_ref, dscb_ref, dlgin_ref,
             dwin_ref, small_ref, dlg_ref):
        k = kvc_ref[:, :RW]
        v = kvc_ref[:, RW:]
        t = lax.broadcasted_iota(jnp.int32, (lc, 1), 0).astype(F32)
        lgf = _log_sigmoid(rlf_ref[...])
        lgb = _log_sigmoid(rlb_ref[...])
        dks, dvs = [], []
        lane = lax.broadcasted_iota(jnp.int32, (1, 128), 1)
        row_f = jnp.zeros((1, 128), F32)
        row_b = jnp.zeros((1, 128), F32)
        for h in range(H):
            hs = slice(h * DH, (h + 1) * DH)
            wf = jnp.exp(lgf[h:h + 1, 0:1] * (lc - 1.0 - t))
            wb = jnp.exp(lgb[h:h + 1, 0:1] * t)
            kh, vhb = k[:, hs], v[:, hs].astype(BF16)
            dsf, dsb = dscf_ref[h].astype(BF16), dscb_ref[h].astype(BF16)
            dkf = wf * _dot_nt(vhb, dsf)
            dkb = wb * _dot_nt(vhb, dsb)
            dvs.append(_dot((kh * wf).astype(BF16), dsf) + _dot((kh * wb).astype(BF16), dsb))
            dks.append((dkf + dkb) * KSCALE)
            lf = jnp.sum((lc - 1.0 - t) * kh * dkf, axis=0, keepdims=True)
            lb = jnp.sum(t * kh * dkb, axis=0, keepdims=True)
            row_f = row_f + jnp.where(lane == h, jnp.sum(lf, axis=1, keepdims=True), 0.0)
            row_b = row_b + jnp.where(lane == h, jnp.sum(lb, axis=1, keepdims=True), 0.0)
        dlg_ref[...] = dlgin_ref[...]
        dlg_ref[0:1, 0:128] += row_f
        dlg_ref[1:2, 0:128] += row_b
        dkv = jnp.concatenate(dks + dvs, axis=1).astype(BF16)
        dhc = _dot(dkv, win_ref[KV_LO:KV_HI, :])
        dwin_ref[...] = _dot_tn(dkv, hc_ref[...])
        x = ctx_ref[...]
        r = lax.rsqrt(jnp.mean(x * x, axis=-1, keepdims=True) + EPS)
        xh = x * r
        small_ref[...] = jnp.zeros_like(small_ref)
        small_ref[0:1, :] = jnp.sum(dhc, axis=0, keepdims=True)
        dhxx = jnp.sum(dhc * xh, axis=0, keepdims=True)
        small_ref[1:2, :] = dhxx * n1_ref[...]
        small_ref[2:3, :] = dhxx * (1.0 + cmod_ref[1:2, :])

    return pl.pallas_call(
        body, name="ctx_bwd",
        out_shape=(jax.ShapeDtypeStruct((2 * RW, D), F32), jax.ShapeDtypeStruct((8, D), F32),
                   jax.ShapeDtypeStruct((8, D), F32)),
        compiler_params=_cparams(),
    )(ctx, hc, kvc, cmod6, norm1, win, rlf, rlb, dscf, dscb, dlg_in)


def _adam_math(w, g, m, v):
    m = ADAM_B1 * m + (1.0 - ADAM_B1) * g
    v = ADAM_B2 * v + (1.0 - ADAM_B2) * (g * g)
    m_hat = m / (1.0 - ADAM_B1 ** ADAM_STEP)
    v_hat = v / (1.0 - ADAM_B2 ** ADAM_STEP)
    delta = -ADAM_LR * (m_hat / (jnp.sqrt(v_hat) + ADAM_EPS) + ADAM_WD * w)
    return delta, m, v


def _place():
    x, y, c = lax.axis_index("x"), lax.axis_index("y"), lax.axis_index("c")
    return x, y, c, 4 * x + 2 * y + c


def _flip(x, y, c, k):
    px = 1 - x if (k >> 2) & 1 else x
    py = 1 - y if (k >> 1) & 1 else y
    pc = 1 - c if k & 1 else c
    return (px, py, pc), 4 * px + 2 * py + pc


def _gather_copy(buf_ref, send_sems, recv_sems, sem0, k, mine):
    x, y, c, me = _place()
    dev, idx = _flip(x, y, c, k)
    blk = me if mine else idx
    return pltpu.make_async_remote_copy(
        src_ref=buf_ref.at[blk], dst_ref=buf_ref.at[blk],
        send_sem=send_sems.at[sem0 + k - 1], recv_sem=recv_sems.at[sem0 + k - 1],
        device_id=dev, device_id_type=MESH)


def _entry_gather(c_row, cctx_row, wmod, bmod, shards):
    n = len(shards)
    ncol = wmod.shape[1]

    def body(*refs):
        c_ref, cctx_ref, wmod_ref, bmod_ref = refs[:4]
        in_refs = refs[4:4 + n]
        mod_ref, cs_ref = refs[4 + n:6 + n]
        out_refs = refs[6 + n:6 + 2 * n]
        cbuf, pbuf = refs[6 + 2 * n:8 + 2 * n]
        cast_refs = refs[8 + 2 * n:8 + 3 * n]
        small_send, small_recv, send_sems, recv_sems, local_sems = refs[8 + 3 * n:]
        x, y, c, me = _place()
        sib = (x, y, 1 - c)
        chips = [(1 - x, y), (x, 1 - y), (1 - x, 1 - y)]

        cbuf[me] = jnp.broadcast_to(c_ref[...], (8, D))
        small = [_gather_copy(cbuf, small_send, small_recv, 0, k, True) for k in range(1, NDEV)]
        for cp in small:
            cp.start()

        def blk(px, py, pc):
            return 4 * px + 2 * py + pc

        def copy(w, k, block, to, src=None):
            dst = out_refs[w].at[block]
            return pltpu.make_async_remote_copy(
                src_ref=dst if src is None else src, dst_ref=dst,
                send_sem=send_sems.at[w, k], recv_sem=recv_sems.at[w, k], device_id=to, device_id_type=MESH)

        first, passed, mine = [], [], []
        for w in range(n):
            cast_refs[w][...] = in_refs[w][...].astype(BF16)
            m = pltpu.make_async_copy(cast_refs[w], out_refs[w].at[me], local_sems.at[w])
            m.start()
            mine.append(m)
            cps = [copy(w, 0, me, sib, src=cast_refs[w])]
            cps += [copy(w, 1 + j, me, (*chip, c), src=cast_refs[w]) for j, chip in enumerate(chips)]
            for cp in cps:
                cp.start()
            first += cps

        for k in range(1, NDEV):
            _gather_copy(cbuf, small_send, small_recv, 0, k, False).wait_recv()
        row = lax.broadcasted_iota(jnp.int32, (16, D), 0)
        call = jnp.where(row == 8, jnp.broadcast_to(cctx_ref[...], (16, D)), 0.0)
        for d in range(NDEV):
            call = jnp.where(row == d, jnp.concatenate([cbuf[d], cbuf[d]], axis=0), call)
        cs = call * _sigmoid(call)
        cs_ref[...] = cs
        pbuf[me] = _dot(cs.astype(BF16), wmod_ref[...].astype(BF16))
        small2 = [_gather_copy(pbuf, small_send, small_recv, NDEV - 1, k, True) for k in range(1, NDEV)]
        for cp in small2:
            cp.start()

        for w in range(n):
            for j, chip in enumerate(chips):
                b = blk(*chip, c)
                copy(w, 1 + j, b, (x, y, c)).wait_recv()
                fw = copy(w, 4 + j, b, sib)
                fw.start()
                passed.append(fw)
        for w in range(n):
            copy(w, 0, blk(x, y, 1 - c), (x, y, c)).wait_recv()
            for j, chip in enumerate(chips):
                copy(w, 4 + j, blk(*chip, 1 - c), (x, y, c)).wait_recv()

        for k in range(1, NDEV):
            _gather_copy(pbuf, small_send, small_recv, NDEV - 1, k, False).wait_recv()
        for d in range(NDEV):
            mod_ref[:, d * ncol:(d + 1) * ncol] = pbuf[d] + bmod_ref[:, d * ncol:(d + 1) * ncol]
        for cp in small + small2 + first + passed:
            cp.wait_send()
        for m in mine:
            m.wait()

    vm = pl.BlockSpec(memory_space=pltpu.VMEM)
    hbm = pl.BlockSpec(memory_space=pltpu.HBM)
    return pl.pallas_call(
        body, name="entry_gather",
        in_specs=[vm] * (4 + n), out_specs=[vm, vm] + [hbm] * n,
        out_shape=(jax.ShapeDtypeStruct((16, 6 * D), F32), jax.ShapeDtypeStruct((16, D), F32))
        + tuple(jax.ShapeDtypeStruct((NDEV,) + s.shape, BF16) for s in shards),
        scratch_shapes=[pltpu.VMEM((NDEV, 8, D), F32), pltpu.VMEM((NDEV, 16, ncol), F32)]
        + [pltpu.VMEM(s.shape, BF16) for s in shards]
        + [pltpu.SemaphoreType.DMA((2 * (NDEV - 1),)), pltpu.SemaphoreType.DMA((2 * (NDEV - 1),)),
           pltpu.SemaphoreType.DMA((n, 7)), pltpu.SemaphoreType.DMA((n, 7)), pltpu.SemaphoreType.DMA((n,))],
        compiler_params=_cparams(),
    )(c_row, cctx_row, wmod, bmod, *shards)


_HBM = pl.BlockSpec(memory_space=pltpu.HBM)
_SEMS = pl.BlockSpec(memory_space=pltpu.SEMAPHORE)
_EFFECT = pltpu.SideEffectType.DATAFLOW_SIDE_EFFECTING


def _exchange_copy(src_refs, land_refs, send_sems, recv_sems, w, k, scatter, landing_slot_is_mine):
    x, y, c, me = _place()
    dev, pidx = _flip(x, y, c, k)
    src = src_refs[w].at[pidx] if scatter else src_refs[w]
    slot = me if landing_slot_is_mine else pidx
    return pltpu.make_async_remote_copy(
        src_ref=src, dst_ref=land_refs[w].at[slot],
        send_sem=send_sems.at[w * (NDEV - 1) + k - 1], recv_sem=recv_sems.at[w * (NDEV - 1) + k - 1],
        device_id=dev, device_id_type=MESH)


def _exchange_start(srcs, *, scatter, name):
    n = len(srcs)
    lands = [lax.empty((NDEV,) + tuple(s.shape[-2:]), s.dtype) for s in srcs]

    def body(*refs):
        src_refs, land_refs = refs[:n], refs[n:2 * n]
        send_sems, recv_sems = refs[2 * n], refs[2 * n + 1]
        token = refs[-1]
        for w in range(n):
            for k in range(1, NDEV):
                _exchange_copy(src_refs, land_refs, send_sems, recv_sems, w, k, scatter, True).start()
        token[...] = jnp.zeros_like(token)

    arrs = list(srcs) + lands
    out_shape = ([pltpu.SemaphoreType.DMA((n * (NDEV - 1),)), pltpu.SemaphoreType.DMA((n * (NDEV - 1),))]
                 + [pltpu.HBM(a.shape, a.dtype) for a in arrs] + [jax.ShapeDtypeStruct((8, 128), F32)])
    res = pl.pallas_call(
        body, name=name, out_shape=tuple(out_shape),
        in_specs=[_HBM] * (2 * n),
        out_specs=tuple([_SEMS, _SEMS] + [_HBM] * (2 * n) + [pl.BlockSpec(memory_space=pltpu.VMEM)]),
        input_output_aliases={i: 2 + i for i in range(2 * n)},
        compiler_params=pltpu.CompilerParams(has_side_effects=_EFFECT),
    )(*[pltpu.with_memory_space_constraint(a, pltpu.HBM) for a in arrs])
    return res[:-1], res[-1]


def _exchange_wait(handles, after, *, scatter, name):
    n = (len(handles) - 2) // 2
    na = len(after)

    def body(*refs):
        src_refs, land_refs = refs[:n], refs[n:2 * n]
        send_sems, recv_sems = refs[2 * n], refs[2 * n + 1]
        for w in range(n):
            for k in range(1, NDEV):
                cp = _exchange_copy(src_refs, land_refs, send_sems, recv_sems, w, k, scatter, False)
                cp.wait_send()
                cp.wait_recv()

    arrs = handles[2:]
    res = pl.pallas_call(
        body, name=name, out_shape=tuple(pltpu.HBM(a.shape, a.dtype) for a in arrs),
        in_specs=[_HBM] * (2 * n) + [_SEMS, _SEMS] + [_HBM] * na,
        out_specs=tuple([_HBM] * (2 * n)),
        input_output_aliases={i: i for i in range(2 * n)},
        compiler_params=pltpu.CompilerParams(has_side_effects=_EFFECT),
    )(*arrs, handles[0], handles[1], *[pltpu.with_memory_space_constraint(a, pltpu.HBM) for a in after])
    return res[:n], res[n:]


_SAME_CORE = (2, 4, 6)


def _gather2_copy(src_ref, land_ref, send_sems, recv_sems, sem, k, block):
    x, y, c, _ = _place()
    dev, _ = _flip(x, y, c, k)
    dst = land_ref.at[block]
    return pltpu.make_async_remote_copy(
        src_ref=dst if src_ref is None else src_ref, dst_ref=dst,
        send_sem=send_sems.at[sem], recv_sem=recv_sems.at[sem], device_id=dev, device_id_type=MESH)


def _split_call(body, name, arrs, n_sems, sems=None, after=(), token=False):
    na = len(arrs)
    out_shape, out_specs = [], []
    if n_sems is not None:
        out_shape += [pltpu.SemaphoreType.DMA((n_sems,)), pltpu.SemaphoreType.DMA((n_sems,))]
        out_specs += [_SEMS, _SEMS]
    first = len(out_shape)
    out_shape += [pltpu.HBM(a.shape, a.dtype) for a in arrs]
    out_specs += [_HBM] * na
    if token:
        out_shape.append(jax.ShapeDtypeStruct((8, 128), F32))
        out_specs.append(pl.BlockSpec(memory_space=pltpu.VMEM))
    in_specs = [_HBM] * na + ([_SEMS, _SEMS] if sems is not None else []) + [_HBM] * len(after)
    args = [pltpu.with_memory_space_constraint(a, pltpu.HBM) for a in arrs] + list(sems or ()) \
        + [pltpu.with_memory_space_constraint(a, pltpu.HBM) for a in after]
    return pl.pallas_call(
        body, name=name, out_shape=tuple(out_shape), in_specs=in_specs, out_specs=tuple(out_specs),
        input_output_aliases={i: first + i for i in range(na)},
        compiler_params=pltpu.CompilerParams(has_side_effects=_EFFECT))(*args)


def _gather2_start(srcs, *, name):
    n = len(srcs)
    lands = [lax.empty((NDEV,) + tuple(s.shape), s.dtype) for s in srcs]

    def body(*refs):
        src_refs, land_refs = refs[:n], refs[n:2 * n]
        send_sems, recv_sems = refs[2 * n], refs[2 * n + 1]
        _, _, _, me = _place()
        for w in range(n):
            for slot, k in enumerate((1,) + _SAME_CORE):
                _gather2_copy(src_refs[w], land_refs[w], send_sems, recv_sems, 4 * w + slot, k, me).start()
        refs[-1][...] = jnp.zeros_like(refs[-1])

    res = _split_call(body, name, list(srcs) + lands, 4 * n, token=True)
    return res[:2], res[2:-1], res[-1]


def _gather2_arrived(sems, arrs, after, *, name):
    n = len(arrs) // 2

    def body(*refs):
        src_refs, land_refs = refs[:n], refs[n:2 * n]
        send_sems, recv_sems = refs[2 * n], refs[2 * n + 1]
        x, y, c, me = _place()
        for w in range(n):
            for slot, k in enumerate((1,) + _SAME_CORE):
                _, pidx = _flip(x, y, c, k)
                _gather2_copy(src_refs[w], land_refs[w], send_sems, recv_sems, 4 * w + slot, k, me).wait_send()
                _gather2_copy(None, land_refs[w], send_sems, recv_sems, 4 * w + slot, k, pidx).wait_recv()

    return _split_call(body, name, arrs, None, sems=sems, after=after)


def _gather2_forward(lands, *, name):
    n = len(lands)

    def body(*refs):
        land_refs = refs[:n]
        send_sems, recv_sems = refs[n], refs[n + 1]
        x, y, c, _ = _place()
        for w in range(n):
            for j, k in enumerate(_SAME_CORE):
                _, pidx = _flip(x, y, c, k)
                _gather2_copy(None, land_refs[w], send_sems, recv_sems, 3 * w + j, 1, pidx).start()
        refs[-1][...] = jnp.zeros_like(refs[-1])

    res = _split_call(body, name, list(lands), 3 * n, token=True)
    return res[:2], res[2:-1], res[-1]


def _gather2_wait(sems, lands, after, *, name):
    n = len(lands)

    def body(*refs):
        land_refs = refs[:n]
        send_sems, recv_sems = refs[n], refs[n + 1]
        x, y, c, _ = _place()
        for w in range(n):
            for j, k in enumerate(_SAME_CORE):
                _, mine = _flip(x, y, c, k)
                _, theirs = _flip(x, y, c, k ^ 1)
                _gather2_copy(None, land_refs[w], send_sems, recv_sems, 3 * w + j, 1, mine).wait_send()
                _gather2_copy(None, land_refs[w], send_sems, recv_sems, 3 * w + j, 1, theirs).wait_recv()

    return _split_call(body, name, list(lands), None, sems=sems, after=after)


def _cast_bf16(arrs, *, name, after=()):
    n = len(arrs)

    def body(*refs):
        for i_ref, o_ref in zip(refs[:n], refs[n + len(after):]):
            o_ref[...] = i_ref[...].astype(BF16)

    return pl.pallas_call(
        body, name=name, out_shape=tuple(jax.ShapeDtypeStruct(a.shape, BF16) for a in arrs),
        in_specs=[pl.BlockSpec(memory_space=pltpu.VMEM)] * n + [pl.BlockSpec(memory_space=pl.ANY)] * len(after),
        compiler_params=_cparams())(*arrs, *after)


def _rs_adam(me_arr, fulls, lands, w, m, v, *, name):
    rows = lands[0].shape[1]
    k = len(fulls)

    def body(me_ref, *refs):
        own_refs, land_refs = refs[:k], refs[k:2 * k]
        w_ref, m_ref, v_ref, g_ref, d_ref, mo_ref, vo_ref = refs[2 * k:]
        me = me_ref[0]
        parts = []
        for own_ref, land_ref in zip(own_refs, land_refs):
            acc = jnp.zeros(own_ref.shape, F32)
            for p in range(NDEV):
                acc = acc + jnp.where(p == me, own_ref[...], land_ref[p].astype(F32))
            parts.append(acc)
        acc = parts[0] if k == 1 else jnp.concatenate(parts, axis=1)
        g_ref[...] = acc
        d_ref[...], mo_ref[...], vo_ref[...] = _adam_math(w_ref[...], acc, m_ref[...], v_ref[...])

    sh = jax.ShapeDtypeStruct((rows, D), F32)
    whole2 = pl.BlockSpec((rows, D), lambda i, me: (0, 0))
    grid_spec = pltpu.PrefetchScalarGridSpec(
        num_scalar_prefetch=1, grid=(1,),
        in_specs=[pl.BlockSpec((rows, f.shape[1]), lambda i, me: (me[0], 0)) for f in fulls]
        + [pl.BlockSpec((NDEV, rows, l.shape[2]), lambda i, me: (0, 0, 0)) for l in lands]
        + [whole2, whole2, whole2],
        out_specs=[whole2] * 4)
    return pl.pallas_call(
        body, name=name, out_shape=(sh, sh, sh, sh), grid_spec=grid_spec, compiler_params=_cparams(1),
    )(me_arr, *fulls, *lands, w, m, v)


ROW_F, ROW_I, ROW_C, ROW_G1, ROW_LG = 0, 8, 16, 24, 32
PACK_ROWS = 40


def _small_sum(me_arr, pack, pack_land, sgw, sgw_land, wmod):
    ncol = wmod.shape[1]

    def body(me_ref, pack_ref, pland_ref, sgw_ref, sland_ref, wmod_ref, sum_ref, all_ref, sgw_sum_ref, part_ref):
        me = me_ref[0]
        acc = jnp.zeros((PACK_ROWS, D), F32)
        acc_w = jnp.zeros(sgw_ref.shape, F32)
        for p in range(NDEV):
            rows = jnp.where(p == me, pack_ref[...], pland_ref[p])
            all_ref[p] = rows
            acc = acc + rows
            acc_w = acc_w + jnp.where(p == me, sgw_ref[...], sland_ref[p])
        sum_ref[...] = acc
        sgw_sum_ref[...] = acc_w
        zero = jnp.zeros((1, D), F32)
        dcmod = jnp.concatenate([acc[ROW_C:ROW_C + 1], acc[ROW_C + 1:ROW_C + 2], zero, zero, zero, zero], axis=1)
        mine = jnp.zeros((1, ncol), F32)
        for d in range(NDEV):
            mine = mine + jnp.where(d == me, dcmod[:, d * ncol:(d + 1) * ncol], 0.0)
        part_ref[...] = _dot_nt(jnp.broadcast_to(mine, (8, ncol)).astype(BF16), wmod_ref[...].astype(BF16))

    vm = pl.BlockSpec(memory_space=pltpu.VMEM)
    return pl.pallas_call(
        body, name="small_sum",
        out_shape=(jax.ShapeDtypeStruct((PACK_ROWS, D), F32), jax.ShapeDtypeStruct((NDEV, PACK_ROWS, D), F32),
                   jax.ShapeDtypeStruct(sgw.shape, F32), jax.ShapeDtypeStruct((8, D), F32)),
        in_specs=[pl.BlockSpec(memory_space=pltpu.SMEM)] + [vm] * 5,
        compiler_params=_cparams(),
    )(me_arr, pack, pack_land, sgw, sgw_land, wmod)


def _cctx_final(me_arr, part, part_land, cctx_row, m_row, v_row):
    def body(me_ref, part_ref, land_ref, c_ref, m_ref, v_ref, g_ref, d_ref, mo_ref, vo_ref):
        me = me_ref[0]
        tot = jnp.zeros((8, D), F32)
        for p in range(NDEV):
            tot = tot + jnp.where(p == me, part_ref[...], land_ref[p])
        cc = c_ref[...]
        _, dsilu = _silu_parts(cc)
        g = tot[0:1, :] * dsilu
        g_ref[...] = g
        d_ref[...], mo_ref[...], vo_ref[...] = _adam_math(cc, g, m_ref[...], v_ref[...])

    row = jax.ShapeDtypeStruct((1, D), F32)
    vm = pl.BlockSpec(memory_space=pltpu.VMEM)
    return pl.pallas_call(
        body, name="cctx_final", out_shape=(row, row, row, row),
        in_specs=[pl.BlockSpec(memory_space=pltpu.SMEM)] + [vm] * 5,
        compiler_params=_cparams(),
    )(me_arr, part, part_land, cctx_row, m_row, v_row)


def _adam_small(s, sgw_g, params):
    names = ["norm1", "norm2", "norm_f", "sg_gain", "sg_b", "ret_logit_f", "ret_logit_b", "b_mod", "sg_w"]
    flat = [a for n in names for a in params[n]]

    def body(*refs):
        s_ref, sgw_ref = refs[0], refs[1]
        p_refs = refs[2:2 + 3 * len(names)]
        o_refs = refs[2 + 3 * len(names):]
        loss_ref = o_refs[-1]

        def row(r):
            return s_ref[r:r + 1, :]

        grads = {
            "norm1": row(ROW_I + 2) + row(ROW_C + 2),
            "norm2": row(ROW_F + 4),
            "norm_f": row(ROW_F + 0),
            "sg_gain": s_ref[ROW_I + 3:ROW_I + 4, 0:AW],
            "sg_b": s_ref[ROW_I + 4:ROW_I + 8, 0:DH],
            "sg_w": sgw_ref[...],
        }
        for j, n in enumerate(names):
            w_ref, m_ref, v_ref = p_refs[3 * j:3 * j + 3]
            g_ref, d_ref, mo_ref, vo_ref = o_refs[4 * j:4 * j + 4]
            w = w_ref[...]
            if n in ("ret_logit_f", "ret_logit_b"):
                r = ROW_LG + (0 if n == "ret_logit_f" else 1)
                g = s_ref[r:r + 1, 0:H] * _sigmoid(-w)
            elif n == "b_mod":
                rows = [row(ROW_I + 0) + row(ROW_C + 0), row(ROW_I + 1) + row(ROW_C + 1), row(ROW_G1),
                        row(ROW_F + 2), row(ROW_F + 3), row(ROW_F + 1)]
                g = jnp.concatenate(rows, axis=1)
            else:
                g = grads[n]
            g_ref[...] = g
            d_ref[...], mo_ref[...], vo_ref[...] = _adam_math(w, g, m_ref[...], v_ref[...])
        loss_ref[...] = jnp.full((1, 1), 0.5 / D, F32) * jnp.sum(row(ROW_F + 5), axis=1, keepdims=True)

    out_shape = []
    for n in names:
        w = params[n][0]
        out_shape += [jax.ShapeDtypeStruct(w.shape, F32)] * 4
    out_shape.append(jax.ShapeDtypeStruct((1, 1), F32))
    outs = pl.pallas_call(body, name="adam_small", out_shape=tuple(out_shape), compiler_params=_cparams())(
        s, sgw_g, *flat)
    return {n: tuple(outs[4 * j:4 * j + 4]) for j, n in enumerate(names)}, outs[-1]


def _wmod_grad(cs_all, dmod_cols, wmod, m, v):
    ncol = wmod.shape[1]

    def body(cs_ref, dm_ref, w_ref, m_ref, v_ref, g_ref, d_ref, mo_ref, vo_ref):
        g = _dot_tn(cs_ref[...].astype(BF16), dm_ref[...].astype(BF16))
        g_ref[...] = g
        d_ref[...], mo_ref[...], vo_ref[...] = _adam_math(w_ref[...], g, m_ref[...], v_ref[...])

    sh = jax.ShapeDtypeStruct((D, ncol), F32)
    return pl.pallas_call(
        body, name="wmod_grad", out_shape=(sh, sh, sh, sh),
        compiler_params=_cparams(),
    )(cs_all, dmod_cols, wmod, m, v)


def _rope_tables(t_len):
    n_freq = DH // 4
    inv = (ROPE_BASE ** (-np.arange(n_freq, dtype=np.float32) / n_freq)).astype(np.float32)
    tok = np.arange(t_len)
    rows = (tok // GRID_W).astype(np.float32)
    cols = (tok % GRID_W).astype(np.float32)
    ang_r = rows[:, None] * inv[None, :]
    ang_c = cols[:, None] * inv[None, :]
    cos = np.concatenate([np.cos(ang_r)] * 2 + [np.cos(ang_c)] * 2, axis=1).astype(np.float32)
    sin = np.concatenate([-np.sin(ang_r), np.sin(ang_r), -np.sin(ang_c), np.sin(ang_c)], axis=1).astype(np.float32)
    return jnp.asarray(cos), jnp.asarray(sin)


def _local_step(x, tgt, ctx, mod6, cmod6, norm1, norm2, normf, win, wout, ffn_weights, sgw, sgb, sggain, rlf, rlb,
                *, tm_a, tm_b, tm_f, tm_m, tm_r, tm_i, bt_w, after_first_grads=None, small_path=None,
                after_in_half=None):
    t_len = x.shape[0]
    cos, sin = _rope_tables(t_len)
    sgbt = jnp.broadcast_to(sgb[:, :, None], (G, C, 128))
    rlf = jnp.broadcast_to(rlf.reshape(H, 1), (H, 128))
    rlb = jnp.broadcast_to(rlb.reshape(H, 1), (H, 128))
    hc, kvc, scf, scb = _ctx_fwd(ctx, cmod6, norm1, win, rlf, rlb)
    hx, zuv, q, k, v, gfb, of, ya, sst = _fwd_a(x, mod6, norm1, win, sgw, sgbt, sggain, cos, sin, rlf, scf, tm=tm_a)
    if callable(wout):
        wout, tok = wout(hx)
        rlb = rlb + tok
    ob, ycat, y, x1, rst = _fwd_b(q, k, v, of, gfb, ya, x, mod6, wout, rlb, scb, tm=tm_b)
    wg, wu, wd = ffn_weights(x1) if callable(ffn_weights) else ffn_weights
    dx1, h2, da, db, hm, df, small_f = _ffn(x1, tgt, mod6, norm2, normf, wg, wu, wd, tm=tm_f)
    bt2 = min(2 * bt_w, t_len)
    d_wd, d_wd_b = _tn_matmul(hm, df, bm=DFF // 2, bt=bt2, name="dw_down")
    d_wg, d_wg_b = _tn_matmul(da, h2, bm=DFF // 2, bt=bt2, name="dw_gate")
    d_wu, d_wu_b = _tn_matmul(db, h2, bm=DFF // 2, bt=bt2, name="dw_up")
    dy, dya, dgfb, dof, dob, dg1 = _mid_bwd(dx1, y, of, ob, gfb, mod6, wout, tm=tm_m)
    d_wo, d_wo_b = _tn_matmul(ycat, dy, bm=D, bt=bt2, name="dw_out", transposed=False)
    if after_first_grads is not None:
        rlf = rlf + after_first_grads((d_wd_b, d_wg_b, d_wu_b, d_wo_b))
    dqf, dkf, dvf, dqb, dkb, dvb, dscf, dscb, dlg = _ret_bwd(q, k, v, dof, dob, sst, rst, rlf, rlb, tm=tm_r)
    gx, dz, small_i, dsgw = _in_bwd(x, zuv, dya, dqf, dkf, dvf, dqb, dkb, dvb, dgfb, dx1, cos, sin,
                                    mod6, norm1, win, sgw, sgbt, sggain, tm=tm_i)
    dwin_c, small_c, dlg = _ctx_bwd(ctx, hc, kvc, cmod6, norm1, win, rlf, rlb, dscf, dscb, dlg)
    pack = jnp.concatenate([small_f, small_i, small_c, dg1, dlg], axis=0)
    after = small_path(pack, dsgw) if small_path is not None else ()
    halves = []
    for j in range(2):
        halves.append(_tn_matmul(dz, hx, bm=INC // 2, bt=bt2, name="dw_in_%d" % j, init=dwin_c,
                                 init_rows=(KV_LO, KV_HI), after=after, cols=(j, D // 2), transposed=False))
        if after_in_half is not None:
            after = after_in_half(j, halves[-1][1])
    grads = {"w_in": halves, "w_out": (d_wo, d_wo_b), "w_gate": (d_wg, d_wg_b), "w_up": (d_wu, d_wu_b),
             "w_down": (d_wd, d_wd_b)}
    return gx, grads, pack, dsgw


def kernel(x, c, ctx, c_ctx, w_mod, b_mod, norm1, w_in, sg_gain, sg_w, sg_b, ret_logit_f, ret_logit_b, w_out, norm2, w_gate, w_up, w_down, norm_f, loss_target, m_c_ctx, m_w_mod, m_b_mod, m_norm1, m_w_in, m_sg_gain, m_sg_w, m_sg_b, m_ret_logit_f, m_ret_logit_b, m_w_out, m_norm2, m_w_gate, m_w_up, m_w_down, m_norm_f, v_c_ctx, v_w_mod, v_b_mod, v_norm1, v_w_in, v_sg_gain, v_sg_w, v_sg_b, v_ret_logit_f, v_ret_logit_b, v_w_out, v_norm2, v_w_gate, v_w_up, v_w_down, v_norm_f):
    t_len = x.shape[1]
    tm = min(512, t_len)
    me = 4 * lax.axis_index("x") + 2 * lax.axis_index("y") + lax.axis_index("c")
    tr = lambda a: jnp.transpose(a[0])

    cctx_row = c_ctx.reshape(1, D)
    mod_all, cs_all, g_in = _entry_gather(c, cctx_row, w_mod[0], b_mod, [tr(w_in)])
    mod6 = lax.dynamic_slice(mod_all, (me, 0), (1, 6 * D)).reshape(6, D)
    cmod6 = mod_all[8].reshape(6, D)
    win_t = g_in.reshape(INC, D)

    (out_shard,) = _cast_bf16([w_out[0]], name="cast_out", after=[g_in])
    h_out, tok_out = _exchange_start([out_shard], scatter=False, name="wout_start")
    ffn_shards = _cast_bf16([tr(w_gate), tr(w_up), w_down[0]], name="cast_ffn", after=[h_out[2]])
    sems_ffn, arrs_ffn, tok = _gather2_start(ffn_shards, name="wffn_start")
    mod6 = mod6 + (tok_out[0, 0] + tok[0, 0])
    ffn = {}

    def place_own(own, lands, rows):
        return [lax.dynamic_update_slice(l, o[None], (me, 0, 0)).reshape(rows, D) for o, l in zip(own, lands)]

    def wout(after):
        own, lands = _exchange_wait(h_out, [after], scatter=False, name="wout_wait")
        arrs = _gather2_arrived(sems_ffn, arrs_ffn, [after], name="wffn_arrived")
        ffn["own"] = arrs[:3]
        ffn["sems"], ffn["lands"], tok_fwd = _gather2_forward(arrs[3:], name="wffn_forward")
        return place_own(own, lands, D)[0], tok_fwd[0, 0]

    def ffn_weights(after):
        lands = _gather2_wait(ffn["sems"], ffn["lands"], [after], name="wffn_wait")
        return place_own(ffn["own"], lands, DFF)

    rs, outs = {}, {}

    def after_first_grads(bf):
        rs["first"], tok_first = _exchange_start([b.reshape(NDEV, b.shape[0] // NDEV, D) for b in bf], scatter=True,
                                                 name="rs_first_start")
        return tok_first[0, 0]

    def small_path(pack, dsgw):
        rs["small"], _ = _exchange_start([pack, dsgw.reshape(G * C, C)], scatter=False, name="small_start")
        return [rs["small"][2], rs["small"][3]]

    def after_in_half(j, half_bf16):
        rs["in%d" % j], _ = _exchange_start([half_bf16.reshape(NDEV, INC // NDEV, D // 2)], scatter=True,
                                            name="rs_in%d_start" % j)
        return [rs["in%d" % j][2]]

    gx, grads, pack, dsgw = _local_step(
        x[0], loss_target[0], ctx[0], mod6, cmod6, norm1, norm2[0:1], norm_f.reshape(1, D), win_t, wout, ffn_weights,
        sg_w[0], sg_b[0], sg_gain, ret_logit_f, ret_logit_b,
        tm_a=tm, tm_b=tm, tm_f=min(256, t_len), tm_m=tm, tm_r=tm, tm_i=min(256, t_len), bt_w=min(1024, t_len),
        after_first_grads=after_first_grads, small_path=small_path, after_in_half=after_in_half)

    me_arr = me.reshape(1).astype(jnp.int32)
    (pack_own, sgw_own), (pack_land, sgw_land) = _exchange_wait(rs["small"], [rs["in1"][2]], scatter=False,
                                                               name="small_wait")
    psum_rows, pall, sgw_sum, part = _small_sum(me_arr, pack_own, pack_land, sgw_own, sgw_land, w_mod[0])
    h_cc, _ = _exchange_start([part], scatter=False, name="cctx_start")

    dmod_rows = (ROW_I + 0, ROW_I + 1, ROW_G1, ROW_F + 2, ROW_F + 3, ROW_F + 1)
    dmod_all = jnp.concatenate([pall[:, r, :] for r in dmod_rows], axis=1)
    dcmod = jnp.concatenate([psum_rows[ROW_C + 0:ROW_C + 1], psum_rows[ROW_C + 1:ROW_C + 2],
                             jnp.zeros((1, 4 * D), F32)], axis=1)
    dmod_mat = jnp.concatenate([dmod_all, jnp.pad(dcmod, ((0, 7), (0, 0)))], axis=0)
    ncol = 6 * D // NDEV
    dmod_cols = lax.dynamic_slice(dmod_mat, (0, me * ncol), (16, ncol))
    g_wm, d_wm, m_wm, v_wm = _wmod_grad(cs_all, dmod_cols, w_mod[0], m_w_mod[0], v_w_mod[0])
    outs["w_mod"] = (g_wm[None], d_wm[None], m_wm[None], v_wm[None])
    small_params = {
        "norm1": (norm1, m_norm1, v_norm1), "norm2": (norm2, m_norm2, v_norm2),
        "norm_f": tuple(a.reshape(1, D) for a in (norm_f, m_norm_f, v_norm_f)),
        "sg_gain": (sg_gain, m_sg_gain, v_sg_gain), "sg_b": (sg_b[0], m_sg_b[0], v_sg_b[0]),
        "ret_logit_f": (ret_logit_f, m_ret_logit_f, v_ret_logit_f),
        "ret_logit_b": (ret_logit_b, m_ret_logit_b, v_ret_logit_b),
        "b_mod": (b_mod, m_b_mod, v_b_mod), "sg_w": (sg_w[0], m_sg_w[0], v_sg_w[0])}
    small, loss = _adam_small(psum_rows, sgw_sum.reshape(G, C, C), small_params)
    back = {"norm_f": lambda a: a.reshape(D), "sg_b": lambda a: a[None], "sg_w": lambda a: a[None]}
    for name, vals in small.items():
        outs[name] = tuple(back.get(name, lambda a: a)(a) for a in vals)

    _, lands_first = _exchange_wait(rs["first"], [g_wm], scatter=True, name="rs_first_wait")

    def finish(name, fulls, lands, w, m, v, transposed):
        take = tr if transposed else (lambda a: a[0])
        res = _rs_adam(me_arr, fulls, lands, take(w), take(m), take(v), name="adam_" + name)
        put = (lambda a: jnp.transpose(a)[None]) if transposed else (lambda a: a[None])
        outs[name] = tuple(put(a) for a in res)
        return res[0]

    g_down = finish("w_down", [grads["w_down"][0]], [lands_first[0]], w_down, m_w_down, v_w_down, False)
    g_gate = finish("w_gate", [grads["w_gate"][0]], [lands_first[1]], w_gate, m_w_gate, v_w_gate, True)
    g_up = finish("w_up", [grads["w_up"][0]], [lands_first[2]], w_up, m_w_up, v_w_up, True)
    g_out = finish("w_out", [grads["w_out"][0]], [lands_first[3]], w_out, m_w_out, v_w_out, False)
    done = [g_down, g_gate, g_up, g_out]
    (part_own,), (part_land,) = _exchange_wait(h_cc, done, scatter=False, name="cctx_wait")
    res_cc = _cctx_final(me_arr, part_own, part_land, cctx_row, m_c_ctx.reshape(1, D), v_c_ctx.reshape(1, D))
    outs["c_ctx"] = tuple(a.reshape(D) for a in res_cc)
    lands_in = [_exchange_wait(rs["in%d" % j], done, scatter=True, name="rs_in%d_wait" % j)[1][0] for j in range(2)]
    finish("w_in", [h[0] for h in grads["w_in"]], lands_in, w_in, m_w_in, v_w_in, True)

    order = ["c_ctx", "w_mod", "b_mod", "norm1", "w_in", "sg_gain", "sg_w", "sg_b", "ret_logit_f", "ret_logit_b",
             "w_out", "norm2", "w_gate", "w_up", "w_down", "norm_f"]
    res = [loss.reshape(()), gx[None]]
    for j in range(4):
        res += [outs[name][j] for name in order]
    return tuple(res)
```

```python
import functools
import math

import numpy as np
import jax
import jax.numpy as jnp
from jax import lax
from jax.experimental import pallas as pl
from jax.experimental.pallas import tpu as pltpu

F32 = jnp.float32
BF16 = jnp.bfloat16

D = 1024
AW = 512
RW = 512
H = 4
DH = 128
G = 4
C = 128
CR = 256
DFF = 2816
INC = 3584
Q_LO = 2 * AW
KV_LO, VR_LO, G_LO = Q_LO + RW, Q_LO + 2 * RW, Q_LO + 3 * RW
KV_HI = G_LO
NDEV = 8
EPS = 1e-6
KSCALE = DH ** -0.5
ROPE_BASE = 10000.0
GRID_W = 64

ADAM_LR = 0.001
ADAM_B1 = 0.9
ADAM_B2 = 0.999
ADAM_EPS = 1e-08
ADAM_WD = 0.01
ADAM_STEP = 10

VMEM_LIMIT = 56 * 1024 * 1024
MESH = pl.DeviceIdType.MESH


def _cparams(n_grid=0, **kw):
    sem = ("arbitrary",) * n_grid if n_grid else None
    return pltpu.CompilerParams(dimension_semantics=sem, vmem_limit_bytes=VMEM_LIMIT, **kw)


def _dot(a, b):
    return jnp.dot(a, b, preferred_element_type=F32)


def _dot_nt(a, b):
    return lax.dot_general(a, b, (((1,), (1,)), ((), ())), preferred_element_type=F32)


def _dot_tn(a, b):
    return lax.dot_general(a, b, (((0,), (0,)), ((), ())), preferred_element_type=F32)


def _whole(shape):
    nd = len(shape)
    return pl.BlockSpec(shape, lambda *_: (0,) * nd, pipeline_mode=pl.Buffered(1))


def _acc(shape):
    nd = len(shape)
    return pl.BlockSpec(shape, lambda *_: (0,) * nd)


def _rows(tm, n):
    return pl.BlockSpec((tm, n), lambda i: (i, 0))


def _rows_rev(tm, n, nt):
    return pl.BlockSpec((tm, n), lambda i: (nt - 1 - i, 0))


def _cols(n, tm):
    return pl.BlockSpec((n, tm), lambda i: (0, i))


def _sigmoid(x):
    return 1.0 / (1.0 + jnp.exp(-x))


def _log_sigmoid(x):
    return jnp.minimum(x, 0.0) - jnp.log(1.0 + jnp.exp(-jnp.abs(x)))


_GK0 = math.sqrt(2.0 / math.pi)
_GK1 = 0.044715


def _gelu(x):
    x2 = x * x
    t = jnp.tanh(x * (_GK0 + (_GK0 * _GK1) * x2))
    h = 0.5 + 0.5 * t
    g = x * h
    dg = h + g * (1.0 - h) * ((2.0 * _GK0) + (6.0 * _GK0 * _GK1) * x2)
    return g, dg


def _silu_parts(a):
    s = _sigmoid(a)
    sa = a * s
    return sa, s + sa * (1.0 - s)


def _rope(t, cos, sins):
    n = t.shape[1]
    lane = lax.broadcasted_iota(jnp.int32, t.shape, 1)
    first = (lane & 63) < 32
    partner = jnp.where(first, pltpu.roll(t, n - 32, 1), pltpu.roll(t, 32, 1))
    return t * cos + partner * sins


def _headnorm(o):
    outs, rs = [], []
    for h in range(H):
        oh = o[:, h * DH:(h + 1) * DH]
        r = lax.rsqrt(jnp.mean(oh * oh, axis=-1, keepdims=True) + EPS)
        outs.append(oh * r)
        rs.append(r)
    return jnp.concatenate(outs, axis=1), rs


def _decay_consts(lg, forward):
    ii = lax.broadcasted_iota(jnp.int32, (CR, CR), 0).astype(F32)
    jj = lax.broadcasted_iota(jnp.int32, (CR, CR), 1).astype(F32)
    ic = lax.broadcasted_iota(jnp.int32, (CR, 1), 0).astype(F32)
    if forward:
        diff = ii - jj
        xi = jnp.exp(lg * (ic + 1.0))
        z = jnp.exp(lg * (CR - 1.0 - ic))
    else:
        diff = jj - ii
        xi = jnp.exp(lg * (CR - ic))
        z = jnp.exp(lg * ic)
    dm = jnp.where(diff >= 0.0, jnp.exp(lg * jnp.maximum(diff, 0.0)), 0.0)
    dec = jnp.exp(lg * float(CR))
    return dm, xi, z, dec, ic


def _ctx_fwd(ctx, cmod6, norm1, win, rlf, rlb):
    lc = ctx.shape[0]

    def body(ctx_ref, cmod_ref, n1_ref, win_ref, rlf_ref, rlb_ref, hc_ref, kvc_ref, scf_ref, scb_ref):
        x = ctx_ref[...]
        r = lax.rsqrt(jnp.mean(x * x, axis=-1, keepdims=True) + EPS)
        hc = (x * r) * (n1_ref[...] * (1.0 + cmod_ref[1:2, :])) + cmod_ref[0:1, :]
        hcb = hc.astype(BF16)
        hc_ref[...] = hcb
        kv = _dot_nt(hcb, win_ref[KV_LO:KV_HI, :])
        k = kv[:, :RW] * KSCALE
        v = kv[:, RW:]
        kvc_ref[:, :RW] = k
        kvc_ref[:, RW:] = v
        t = lax.broadcasted_iota(jnp.int32, (lc, 1), 0).astype(F32)
        lgf = _log_sigmoid(rlf_ref[...])
        lgb = _log_sigmoid(rlb_ref[...])
        for h in range(H):
            hs = slice(h * DH, (h + 1) * DH)
            wf = jnp.exp(lgf[h:h + 1, 0:1] * (lc - 1.0 - t))
            wb = jnp.exp(lgb[h:h + 1, 0:1] * t)
            vh = v[:, hs].astype(BF16)
            scf_ref[h] = _dot_tn((k[:, hs] * wf).astype(BF16), vh)
            scb_ref[h] = _dot_tn((k[:, hs] * wb).astype(BF16), vh)

    return pl.pallas_call(
        body, name="ctx_fwd",
        out_shape=(jax.ShapeDtypeStruct((lc, D), BF16), jax.ShapeDtypeStruct((lc, 2 * RW), F32),
                   jax.ShapeDtypeStruct((H, DH, DH), F32), jax.ShapeDtypeStruct((H, DH, DH), F32)),
        compiler_params=_cparams(),
    )(ctx, cmod6, norm1, win, rlf, rlb)


def _sg_forward(u, v, sgw_ref, sgbt_ref, sgg_ref, nc):
    ua, dgu = _gelu(u)
    va, dgv = _gelu(v)
    gain = sgg_ref[...]
    mixed, vn_b, vah_l, rv_l = [], [], [], []
    for g in range(G):
        gs = slice(g * DH, (g + 1) * DH)
        vag = va[:, gs]
        rv = lax.rsqrt(jnp.mean(vag * vag, axis=-1, keepdims=True) + EPS)
        vah = vag * rv
        vn = (vah * gain[:, gs]).astype(BF16)
        wg = sgw_ref[g].astype(BF16)
        bcol = sgbt_ref[g]
        rows = [_dot(wg, vn[c * C:(c + 1) * C, :]) + bcol for c in range(nc)]
        mixed.append(jnp.concatenate(rows, axis=0) if nc > 1 else rows[0])
        vn_b.append(vn)
        vah_l.append(vah)
        rv_l.append(rv)
    mixed = jnp.concatenate(mixed, axis=1)
    return ua * mixed, (ua, dgu, dgv, mixed, vn_b, vah_l, rv_l)


def _fwd_a(x, mod6, norm1, win, sgw, sgbt, sggain, cos, sin, rlf, scf, *, tm):
    t_len = x.shape[0]
    nt, nc, ncr = t_len // tm, tm // C, tm // CR

    def body(x_ref, mod_ref, n1_ref, win_ref, sgw_ref, sgbt_ref, sgg_ref, cos_ref, sin_ref, rlf_ref, scf_ref,
             hx_ref, zuv_ref, q_ref, k_ref, v_ref, gfb_ref, of_ref, ya_ref, sst_ref, s_scr):
        i = pl.program_id(0)

        @pl.when(i == 0)
        def _():
            s_scr[...] = scf_ref[...]

        x = x_ref[...]
        r = lax.rsqrt(jnp.mean(x * x, axis=-1, keepdims=True) + EPS)
        hx = (x * r) * (n1_ref[...] * (1.0 + mod_ref[1:2, :])) + mod_ref[0:1, :]
        hxb = hx.astype(BF16)
        hx_ref[...] = hxb
        z = _dot_nt(hxb, win_ref[...])
        zuv_ref[...] = z[:, :2 * AW]
        gfb_ref[...] = z[:, G_LO:INC].astype(BF16)
        cos = jnp.tile(cos_ref[...], (1, H))
        sins = jnp.tile(sin_ref[...], (1, H))
        q_ref[...] = _rope(z[:, Q_LO:KV_LO], cos, sins).astype(BF16)
        k_ref[...] = (_rope(z[:, KV_LO:VR_LO], cos, sins) * KSCALE).astype(BF16)
        v_ref[...] = z[:, VR_LO:G_LO].astype(BF16)
        ya, _ = _sg_forward(z[:, :AW], z[:, AW:2 * AW], sgw_ref, sgbt_ref, sgg_ref, nc)
        ya_ref[...] = ya.astype(BF16)

        lg = _log_sigmoid(rlf_ref[...])
        for h in range(H):
            dm, xi, zc, dec, _ = _decay_consts(lg[h:h + 1, 0:1], True)
            hs = slice(h * DH, (h + 1) * DH)
            for c in range(ncr):
                rs = slice(c * CR, (c + 1) * CR)
                qc, kc, vc = q_ref[rs, hs], k_ref[rs, hs], v_ref[rs, hs]
                s = s_scr[h]
                sst_ref[c, h] = s
                a = (_dot_nt(qc, kc) * dm).astype(BF16)
                of_ref[rs, hs] = (_dot(a, vc) + xi * _dot(qc, s.astype(BF16))).astype(BF16)
                kz = (kc.astype(F32) * zc).astype(BF16)
                s_scr[h] = dec * s + _dot_tn(kz, vc)

    n_chunks = t_len // CR
    return pl.pallas_call(
        body, name="fwd_a", grid=(nt,),
        in_specs=[_rows(tm, D), _whole((6, D)), _whole((1, D)), _whole((INC, D)), _whole((G, C, C)),
                  _whole((G, C, 128)), _whole((1, AW)), _rows(tm, DH), _rows(tm, DH), _whole((H, 128)),
                  _whole((H, DH, DH))],
        out_specs=[_rows(tm, D), _rows(tm, 2 * AW), _rows(tm, RW), _rows(tm, RW), _rows(tm, RW),
                   _rows(tm, 2 * RW), _rows(tm, RW), _rows(tm, AW),
                   pl.BlockSpec((ncr, H, DH, DH), lambda i: (i, 0, 0, 0))],
        out_shape=(jax.ShapeDtypeStruct((t_len, D), BF16), jax.ShapeDtypeStruct((t_len, 2 * AW), F32),
                   jax.ShapeDtypeStruct((t_len, RW), BF16), jax.ShapeDtypeStruct((t_len, RW), BF16),
                   jax.ShapeDtypeStruct((t_len, RW), BF16), jax.ShapeDtypeStruct((t_len, 2 * RW), BF16),
                   jax.ShapeDtypeStruct((t_len, RW), BF16), jax.ShapeDtypeStruct((t_len, AW), BF16),
                   jax.ShapeDtypeStruct((n_chunks, H, DH, DH), F32)),
        scratch_shapes=[pltpu.VMEM((H, DH, DH), F32)],
        compiler_params=_cparams(1),
    )(x, mod6, norm1, win, sgw, sgbt, sggain, cos, sin, rlf, scf)


def _fwd_b(q, k, v, of, gfb, ya, x, mod6, wout, rlb, scb, *, tm):
    t_len = x.shape[0]
    nt, nc = t_len // tm, tm // CR

    def body(q_ref, k_ref, v_ref, of_ref, gfb_ref, ya_ref, x_ref, mod_ref, wout_ref, rlb_ref, scb_ref,
             ob_ref, ycat_ref, y_ref, x1_ref, rst_ref, r_scr):
        i = pl.program_id(0)

        @pl.when(i == 0)
        def _():
            r_scr[...] = scb_ref[...]

        lg = _log_sigmoid(rlb_ref[...])
        for h in range(H):
            dm, xi, zc, dec, _ = _decay_consts(lg[h:h + 1, 0:1], False)
            hs = slice(h * DH, (h + 1) * DH)
            for c in reversed(range(nc)):
                rs = slice(c * CR, (c + 1) * CR)
                qc, kc, vc = q_ref[rs, hs], k_ref[rs, hs], v_ref[rs, hs]
                s = r_scr[h]
                rst_ref[c, h] = s
                a = (_dot_nt(qc, kc) * dm).astype(BF16)
                ob_ref[rs, hs] = (_dot(a, vc) + xi * _dot(qc, s.astype(BF16))).astype(BF16)
                kz = (kc.astype(F32) * zc).astype(BF16)
                r_scr[h] = dec * s + _dot_tn(kz, vc)

        gfb = gfb_ref[...].astype(F32)
        sf, _ = _silu_parts(gfb[:, :RW])
        sb, _ = _silu_parts(gfb[:, RW:])
        hnf, _ = _headnorm(of_ref[...].astype(F32))
        hnb, _ = _headnorm(ob_ref[...].astype(F32))
        yr = sf * hnf + sb * hnb
        ycat = jnp.concatenate([ya_ref[...], yr.astype(BF16)], axis=1)
        ycat_ref[...] = ycat
        y = _dot(ycat, wout_ref[...])
        y_ref[...] = y.astype(BF16)
        x1_ref[...] = x_ref[...] + mod_ref[2:3, :] * y

    n_chunks = t_len // CR
    rr = functools.partial(_rows_rev, nt=nt)
    return pl.pallas_call(
        body, name="fwd_b", grid=(nt,),
        in_specs=[rr(tm, RW), rr(tm, RW), rr(tm, RW), rr(tm, RW), rr(tm, 2 * RW), rr(tm, AW), rr(tm, D),
                  _whole((6, D)), _whole((D, D)), _whole((H, 128)), _whole((H, DH, DH))],
        out_specs=[rr(tm, RW), rr(tm, D), rr(tm, D), rr(tm, D),
                   pl.BlockSpec((nc, H, DH, DH), lambda i: (nt - 1 - i, 0, 0, 0))],
        out_shape=(jax.ShapeDtypeStruct((t_len, RW), BF16), jax.ShapeDtypeStruct((t_len, D), BF16),
                   jax.ShapeDtypeStruct((t_len, D), BF16), jax.ShapeDtypeStruct((t_len, D), F32),
                   jax.ShapeDtypeStruct((n_chunks, H, DH, DH), F32)),
        scratch_shapes=[pltpu.VMEM((H, DH, DH), F32)],
        compiler_params=_cparams(1),
    )(q, k, v, of, gfb, ya, x, mod6, wout, rlb, scb)


def _ffn(x1, tgt, mod6, norm2, normf, wg, wu, wd, *, tm):
    t_len = x1.shape[0]
    nt = t_len // tm

    def body(x1_ref, tgt_ref, mod_ref, n2_ref, nf_ref, wg_ref, wu_ref, wd_ref,
             dx1_ref, h2_ref, da_ref, db_ref, hm_ref, df_ref, small_ref):
        i = pl.program_id(0)

        @pl.when(i == 0)
        def _():
            small_ref[...] = jnp.zeros_like(small_ref)

        sh2, sc2, g2 = mod_ref[3:4, :], mod_ref[4:5, :], mod_ref[5:6, :]
        n2, nf = n2_ref[...], nf_ref[...]
        x1 = x1_ref[...]
        r2 = lax.rsqrt(jnp.mean(x1 * x1, axis=-1, keepdims=True) + EPS)
        x1h = x1 * r2
        w2 = n2 * (1.0 + sc2)
        h2b = (x1h * w2 + sh2).astype(BF16)
        h2_ref[...] = h2b
        a = _dot_nt(h2b, wg_ref[...])
        b = _dot_nt(h2b, wu_ref[...])
        sa, dsa = _silu_parts(a)
        hmb = (sa * b).astype(BF16)
        hm_ref[...] = hmb.T
        f = _dot(hmb, wd_ref[...])
        x2 = x1 + g2 * f
        r3 = lax.rsqrt(jnp.mean(x2 * x2, axis=-1, keepdims=True) + EPS)
        x2h = x2 * r3
        diff = x2h * nf - tgt_ref[...]
        small_ref[5:6, :] += jnp.sum(diff * diff, axis=0, keepdims=True)
        dout = diff * (1.0 / D)
        small_ref[0:1, :] += jnp.sum(dout * x2h, axis=0, keepdims=True)
        dyg = dout * nf
        dx2 = r3 * (dyg - x2h * jnp.mean(dyg * x2h, axis=-1, keepdims=True))
        small_ref[1:2, :] += jnp.sum(dx2 * f, axis=0, keepdims=True)
        dfb = (dx2 * g2).astype(BF16)
        df_ref[...] = dfb
        dhm = _dot_nt(dfb, wd_ref[...])
        dbb = (dhm * sa).astype(BF16)
        dab = (dhm * b * dsa).astype(BF16)
        da_ref[...] = dab.T
        db_ref[...] = dbb.T
        dh2 = _dot(dab, wg_ref[...]) + _dot(dbb, wu_ref[...])
        small_ref[2:3, :] += jnp.sum(dh2, axis=0, keepdims=True)
        dhx = dh2 * x1h
        small_ref[3:4, :] += jnp.sum(dhx, axis=0, keepdims=True) * n2
        small_ref[4:5, :] += jnp.sum(dhx, axis=0, keepdims=True) * (1.0 + sc2)
        dyg2 = dh2 * w2
        dx1_ref[...] = dx2 + r2 * (dyg2 - x1h * jnp.mean(dyg2 * x1h, axis=-1, keepdims=True))

    return pl.pallas_call(
        body, name="ffn", grid=(nt,),
        in_specs=[_rows(tm, D), _rows(tm, D), _whole((6, D)), _whole((1, D)), _whole((1, D)),
                  _whole((DFF, D)), _whole((DFF, D)), _whole((DFF, D))],
        out_specs=[_rows(tm, D), _rows(tm, D), _cols(DFF, tm), _cols(DFF, tm), _cols(DFF, tm), _rows(tm, D),
                   _acc((8, D))],
        out_shape=(jax.ShapeDtypeStruct((t_len, D), F32), jax.ShapeDtypeStruct((t_len, D), BF16),
                   jax.ShapeDtypeStruct((DFF, t_len), BF16), jax.ShapeDtypeStruct((DFF, t_len), BF16),
                   jax.ShapeDtypeStruct((DFF, t_len), BF16), jax.ShapeDtypeStruct((t_len, D), BF16),
                   jax.ShapeDtypeStruct((8, D), F32)),
        compiler_params=_cparams(1),
    )(x1, tgt, mod6, norm2, normf, wg, wu, wd)


def _mid_bwd(dx1, y, of, ob, gfb, mod6, wout, *, tm):
    t_len = dx1.shape[0]
    nt = t_len // tm

    def body(dx1_ref, y_ref, of_ref, ob_ref, gfb_ref, mod_ref, wout_ref,
             dy_ref, dya_ref, dgfb_ref, dof_ref, dob_ref, dg1_ref):
        i = pl.program_id(0)

        @pl.when(i == 0)
        def _():
            dg1_ref[...] = jnp.zeros_like(dg1_ref)

        dx1 = dx1_ref[...]
        dg1_ref[0:1, :] += jnp.sum(dx1 * y_ref[...].astype(F32), axis=0, keepdims=True)
        dyb = (dx1 * mod_ref[2:3, :]).astype(BF16)
        dy_ref[...] = dyb
        dycat = _dot_nt(dyb, wout_ref[...])
        dya_ref[...] = dycat[:, :AW].astype(BF16)
        dyr = dycat[:, AW:]
        gfb = gfb_ref[...].astype(F32)
        for o_ref, do_ref, lo in ((of_ref, dof_ref, 0), (ob_ref, dob_ref, RW)):
            sg, dsg = _silu_parts(gfb[:, lo:lo + RW])
            o = o_ref[...].astype(F32)
            hn, rs = _headnorm(o)
            dgfb_ref[:, lo:lo + RW] = (dyr * hn * dsg).astype(BF16)
            dhn = dyr * sg
            for h in range(H):
                hs = slice(h * DH, (h + 1) * DH)
                oh, dh = hn[:, hs], dhn[:, hs]
                do_ref[:, hs] = (rs[h] * (dh - oh * jnp.mean(dh * oh, axis=-1, keepdims=True))).astype(BF16)

    return pl.pallas_call(
        body, name="mid_bwd", grid=(nt,),
        in_specs=[_rows(tm, D), _rows(tm, D), _rows(tm, RW), _rows(tm, RW), _rows(tm, 2 * RW),
                  _whole((6, D)), _whole((D, D))],
        out_specs=[_rows(tm, D), _rows(tm, AW), _rows(tm, 2 * RW), _rows(tm, RW), _rows(tm, RW), _acc((8, D))],
        out_shape=(jax.ShapeDtypeStruct((t_len, D), BF16), jax.ShapeDtypeStruct((t_len, AW), BF16),
                   jax.ShapeDtypeStruct((t_len, 2 * RW), BF16), jax.ShapeDtypeStruct((t_len, RW), BF16),
                   jax.ShapeDtypeStruct((t_len, RW), BF16), jax.ShapeDtypeStruct((8, D), F32)),
        compiler_params=_cparams(1),
    )(dx1, y, of, ob, gfb, mod6, wout)


def _ret_bwd_dir(q_ref, k_ref, v_ref, do_ref, st_ref, dq_ref, dk_ref, dv_ref, ds_scr, dlg_scr, lg, forward, nc, row0):
    order = reversed(range(nc)) if forward else range(nc)
    order = list(order)
    for h in range(H):
        dm, xi, zc, dec, ic = _decay_consts(lg[h:h + 1, 0:1], forward)
        hs = slice(h * DH, (h + 1) * DH)
        if forward:
            w_qi, w_qc, w_ki, w_ks = ic, ic + 1.0, -ic, (CR - 1.0) - ic
        else:
            w_qi, w_qc, w_ki, w_ks = -ic, CR - ic, ic, ic
        acc = jnp.zeros((1, DH), F32)
        for c in order:
            rs = slice(c * CR, (c + 1) * CR)
            qc, kc, vc, doc = q_ref[rs, hs], k_ref[rs, hs], v_ref[rs, hs], do_ref[rs, hs]
            s = st_ref[c, h]
            dsn = ds_scr[h]
            sb, dsnb = s.astype(BF16), dsn.astype(BF16)
            qf, kf = qc.astype(F32), kc.astype(F32)
            a = (_dot_nt(qc, kc) * dm).astype(BF16)
            da = (_dot_nt(doc, vc) * dm).astype(BF16)
            xdo = (xi * doc.astype(F32)).astype(BF16)
            kz = (kf * zc).astype(BF16)
            vz = (vc.astype(F32) * zc).astype(BF16)
            dq_i = _dot(da, kc)
            dq_c = _dot_nt(xdo, sb)
            dk_i = _dot_tn(da, qc)
            dk_s = _dot_nt(vz, dsnb)
            dq_ref[rs, hs] = (dq_i + dq_c).astype(BF16)
            dk_ref[rs, hs] = (dk_i + dk_s).astype(BF16)
            dv_ref[rs, hs] = (_dot_tn(a, doc) + _dot(kz, dsnb)).astype(BF16)
            rowterm = qf * (w_qi * dq_i + w_qc * dq_c) + kf * (w_ki * dk_i + w_ks * dk_s)
            acc = acc + jnp.sum(rowterm, axis=0, keepdims=True)
            acc = acc + (float(CR) * dec) * jnp.sum(s * dsn, axis=0, keepdims=True)
            ds_scr[h] = _dot_tn((xi * qf).astype(BF16), doc) + dec * dsn
        dlg_scr[row0 + h:row0 + h + 1, :] += acc


def _ret_bwd(q, k, v, dof, dob, sst, rst, rlf, rlb, *, tm):
    t_len = q.shape[0]
    nt, nc = t_len // tm, tm // CR

    def body(qf_ref, kf_ref, vf_ref, dof_ref, sst_ref, qb_ref, kb_ref, vb_ref, dob_ref, rst_ref, rlf_ref, rlb_ref,
             dqf_ref, dkf_ref, dvf_ref, dqb_ref, dkb_ref, dvb_ref, dscf_ref, dscb_ref, dlg_ref,
             dsf_scr, dsb_scr, dlg_scr):
        i = pl.program_id(0)

        @pl.when(i == 0)
        def _():
            dsf_scr[...] = jnp.zeros_like(dsf_scr)
            dsb_scr[...] = jnp.zeros_like(dsb_scr)
            dlg_scr[...] = jnp.zeros_like(dlg_scr)

        lgf = _log_sigmoid(rlf_ref[...])
        lgb = _log_sigmoid(rlb_ref[...])
        _ret_bwd_dir(qf_ref, kf_ref, vf_ref, dof_ref, sst_ref, dqf_ref, dkf_ref, dvf_ref, dsf_scr, dlg_scr, lgf, True, nc, 0)
        _ret_bwd_dir(qb_ref, kb_ref, vb_ref, dob_ref, rst_ref, dqb_ref, dkb_ref, dvb_ref, dsb_scr, dlg_scr, lgb, False, nc, H)

        @pl.when(i == nt - 1)
        def _():
            dscf_ref[...] = dsf_scr[...]
            dscb_ref[...] = dsb_scr[...]
            tot = jnp.broadcast_to(jnp.sum(dlg_scr[...], axis=1, keepdims=True), (8, 128))
            ri = lax.broadcasted_iota(jnp.int32, (8, 128), 0)
            li = lax.broadcasted_iota(jnp.int32, (8, 128), 1)
            dlg_ref[...] = jnp.zeros_like(dlg_ref)
            dlg_ref[0:1, 0:128] = jnp.sum(jnp.where(ri == li, tot, 0.0), axis=0, keepdims=True)
            dlg_ref[1:2, 0:128] = jnp.sum(jnp.where(ri - H == li, tot, 0.0), axis=0, keepdims=True)

    rr = functools.partial(_rows_rev, nt=nt)
    st_f = pl.BlockSpec((nc, H, DH, DH), lambda i: (nt - 1 - i, 0, 0, 0))
    st_b = pl.BlockSpec((nc, H, DH, DH), lambda i: (i, 0, 0, 0))
    tok = jax.ShapeDtypeStruct((t_len, RW), BF16)
    st = jax.ShapeDtypeStruct((H, DH, DH), F32)
    return pl.pallas_call(
        body, name="ret_bwd", grid=(nt,),
        in_specs=[rr(tm, RW), rr(tm, RW), rr(tm, RW), rr(tm, RW), st_f,
                  _rows(tm, RW), _rows(tm, RW), _rows(tm, RW), _rows(tm, RW), st_b,
                  _whole((H, 128)), _whole((H, 128))],
        out_specs=[rr(tm, RW), rr(tm, RW), rr(tm, RW), _rows(tm, RW), _rows(tm, RW), _rows(tm, RW),
                   _acc((H, DH, DH)), _acc((H, DH, DH)), _acc((8, D))],
        out_shape=(tok, tok, tok, tok, tok, tok, st, st, jax.ShapeDtypeStruct((8, D), F32)),
        scratch_shapes=[pltpu.VMEM((H, DH, DH), F32), pltpu.VMEM((H, DH, DH), F32), pltpu.VMEM((8, 128), F32)],
        compiler_params=_cparams(1),
    )(q, k, v, dof, sst, q, k, v, dob, rst, rlf, rlb)


def _in_bwd(x, zuv, dya, dqf, dkf, dvf, dqb, dkb, dvb, dgfb, dx1, cos, sin, mod6, norm1, win, sgw, sgbt, sggain, *, tm):
    t_len = x.shape[0]
    nt, nc = t_len // tm, tm // C

    def body(x_ref, zuv_ref, dya_ref, dqf_ref, dkf_ref, dvf_ref, dqb_ref, dkb_ref, dvb_ref, dgfb_ref, dx1_ref,
             cos_ref, sin_ref, mod_ref, n1_ref, win_ref, sgw_ref, sgbt_ref, sgg_ref,
             gx_ref, dz_ref, small_ref, dsgw_ref, dsgbt_ref):
        i = pl.program_id(0)

        @pl.when(i == 0)
        def _():
            small_ref[...] = jnp.zeros_like(small_ref)
            dsgw_ref[...] = jnp.zeros_like(dsgw_ref)
            dsgbt_ref[...] = jnp.zeros_like(dsgbt_ref)

        zuv = zuv_ref[...]
        _, (ua, dgu, dgv, mixed, vn_b, vah_l, rv_l) = _sg_forward(zuv[:, :AW], zuv[:, AW:], sgw_ref, sgbt_ref, sgg_ref, nc)
        dya = dya_ref[...].astype(F32)
        dz_ref[:, 0:AW] = (dya * mixed * dgu).astype(BF16)
        dmixed = dya * ua
        gain = sgg_ref[...]
        for g in range(G):
            gs = slice(g * DH, (g + 1) * DH)
            dmg = dmixed[:, gs]
            dmb = dmg.astype(BF16)
            wgt = sgw_ref[g].astype(BF16)
            dsw = jnp.zeros((C, C), F32)
            dsb = jnp.zeros((C, DH), F32)
            rows = []
            for c in range(nc):
                rs = slice(c * C, (c + 1) * C)
                dsw = dsw + _dot_nt(dmb[rs, :], vn_b[g][rs, :])
                dsb = dsb + dmg[rs, :]
                rows.append(_dot_tn(wgt, dmb[rs, :]))
            dsgw_ref[g] += dsw
            dsgbt_ref[g] += dsb
            dvn = jnp.concatenate(rows, axis=0) if nc > 1 else rows[0]
            vah, rv = vah_l[g], rv_l[g]
            small_ref[3:4, gs] += jnp.sum(dvn * vah, axis=0, keepdims=True)
            dyg = dvn * gain[:, gs]
            dva = rv * (dyg - vah * jnp.mean(dyg * vah, axis=-1, keepdims=True))
            dz_ref[:, AW + g * DH:AW + (g + 1) * DH] = (dva * dgv[:, gs]).astype(BF16)

        cos = jnp.tile(cos_ref[...], (1, H))
        nsins = -jnp.tile(sin_ref[...], (1, H))
        def both(f_ref, b_ref):
            return f_ref[...].astype(F32) + b_ref[...].astype(F32)

        dz_ref[:, Q_LO:KV_LO] = _rope(both(dqf_ref, dqb_ref), cos, nsins).astype(BF16)
        dz_ref[:, KV_LO:VR_LO] = (_rope(both(dkf_ref, dkb_ref), cos, nsins) * KSCALE).astype(BF16)
        dz_ref[:, VR_LO:G_LO] = both(dvf_ref, dvb_ref).astype(BF16)
        dz_ref[:, G_LO:INC] = dgfb_ref[...]

        dhx = _dot(dz_ref[...], win_ref[...])
        x = x_ref[...]
        r = lax.rsqrt(jnp.mean(x * x, axis=-1, keepdims=True) + EPS)
        xh = x * r
        n1, sc1 = n1_ref[...], mod_ref[1:2, :]
        small_ref[0:1, :] += jnp.sum(dhx, axis=0, keepdims=True)
        dhxx = jnp.sum(dhx * xh, axis=0, keepdims=True)
        small_ref[1:2, :] += dhxx * n1
        small_ref[2:3, :] += dhxx * (1.0 + sc1)
        dyg = dhx * (n1 * (1.0 + sc1))
        gx_ref[...] = dx1_ref[...] + r * (dyg - xh * jnp.mean(dyg * xh, axis=-1, keepdims=True))

        @pl.when(i == nt - 1)
        def _():
            ri = lax.broadcasted_iota(jnp.int32, (C, DH), 0)
            li = lax.broadcasted_iota(jnp.int32, (C, DH), 1)
            for g in range(G):
                tot = jnp.broadcast_to(jnp.sum(dsgbt_ref[g], axis=1, keepdims=True), (C, DH))
                small_ref[4 + g:5 + g, 0:DH] = jnp.sum(jnp.where(ri == li, tot, 0.0), axis=0, keepdims=True)

    tok = functools.partial(_rows, tm)
    return pl.pallas_call(
        body, name="in_bwd", grid=(nt,),
        in_specs=[tok(D), tok(2 * AW), tok(AW), tok(RW), tok(RW), tok(RW), tok(RW), tok(RW), tok(RW),
                  tok(2 * RW), tok(D), tok(DH), tok(DH), _whole((6, D)), _whole((1, D)), _whole((INC, D)),
                  _whole((G, C, C)), _whole((G, C, 128)), _whole((1, AW))],
        out_specs=[tok(D), tok(INC), _acc((8, D)), _acc((G, C, C))],
        out_shape=(jax.ShapeDtypeStruct((t_len, D), F32), jax.ShapeDtypeStruct((t_len, INC), BF16),
                   jax.ShapeDtypeStruct((8, D), F32), jax.ShapeDtypeStruct((G, C, C), F32)),
        scratch_shapes=[pltpu.VMEM((G, C, 128), F32)],
        compiler_params=_cparams(1),
    )(x, zuv, dya, dqf, dkf, dvf, dqb, dkb, dvb, dgfb, dx1, cos, sin, mod6, norm1, win, sgw, sgbt, sggain)


def _tn_matmul(at, b, *, bm, bt, name, init=None, init_rows=None, after=(), cols=None, transposed=True):
    m, t_len = at.shape if transposed else at.shape[::-1]
    cj, n = (0, b.shape[1]) if cols is None else cols
    ni, ntt = m // bm, t_len // bt
    has_init = init is not None

    def body(*refs):
        o_ref, ob_ref = refs[-2:]
        a_ref, b_ref = refs[:2]
        init_ref = refs[2] if has_init else None
        i, t = pl.program_id(0), pl.program_id(1)

        @pl.when(t == 0)
        def _():
            o_ref[...] = jnp.zeros_like(o_ref)

        if has_init:
            for ib in range(ni):
                lo, hi = max(init_rows[0], ib * bm), min(init_rows[1], (ib + 1) * bm)
                if lo < hi:
                    @pl.when(jnp.logical_and(t == 0, i == ib))
                    def _(lo=lo, hi=hi, ib=ib):
                        o_ref[lo - ib * bm:hi - ib * bm, :] = init_ref[lo - init_rows[0]:hi - init_rows[0], :]

        o_ref[...] += (_dot if transposed else _dot_tn)(a_ref[...], b_ref[...])

        @pl.when(t == ntt - 1)
        def _():
            ob_ref[...] = o_ref[...].astype(BF16)

    a_spec = pl.BlockSpec((bm, bt), lambda i, t: (i, t)) if transposed else pl.BlockSpec((bt, bm), lambda i, t: (t, i))
    in_specs = [a_spec, pl.BlockSpec((bt, n), lambda i, t: (t, cj))]
    args = [at, b]
    if has_init:
        in_specs.append(pl.BlockSpec((init.shape[0], n), lambda i, t: (0, cj), pipeline_mode=pl.Buffered(1)))
        args.append(init)
    for arr in after:
        in_specs.append(pl.BlockSpec(memory_space=pl.ANY))
        args.append(arr)
    out_spec = pl.BlockSpec((bm, n), lambda i, t: (i, 0))
    return pl.pallas_call(
        body, name=name, grid=(ni, ntt),
        in_specs=in_specs,
        out_specs=[out_spec, out_spec],
        out_shape=(jax.ShapeDtypeStruct((m, n), F32), jax.ShapeDtypeStruct((m, n), BF16)),
        compiler_params=_cparams(2),
    )(*args)


def _ctx_bwd(ctx, hc, kvc, cmod6, norm1, win, rlf, rlb, dscf, dscb, dlg_in):
    lc = ctx.shape[0]

    def body(ctx_ref, hc_ref, kvc_ref, cmod_ref, n1_ref, win_ref, rlf_ref, rlb_ref, dscf_ref, dscb_ref, dlgin_ref,
             dwin_ref, small_ref, dlg_ref):
        k = kvc_ref[:, :RW]
        v = kvc_ref[:, RW:]
        t = lax.broadcasted_iota(jnp.int32, (lc, 1), 0).astype(F32)
        lgf = _log_sigmoid(rlf_ref[...])
        lgb = _log_sigmoid(rlb_ref[...])
        dks, dvs = [], []
        lane = lax.broadcasted_iota(jnp.int32, (1, 128), 1)
        row_f = jnp.zeros((1, 128), F32)
        row_b = jnp.zeros((1, 128), F32)
        for h in range(H):
            hs = slice(h * DH, (h + 1) * DH)
            wf = jnp.exp(lgf[h:h + 1, 0:1] * (lc - 1.0 - t))
            wb = jnp.exp(lgb[h:h + 1, 0:1] * t)
            kh, vhb = k[:, hs], v[:, hs].astype(BF16)
            dsf, dsb = dscf_ref[h].astype(BF16), dscb_ref[h].astype(BF16)
            dkf = wf * _dot_nt(vhb, dsf)
            dkb = wb * _dot_nt(vhb, dsb)
            dvs.append(_dot((kh * wf).astype(BF16), dsf) + _dot((kh * wb).astype(BF16), dsb))
            dks.append((dkf + dkb) * KSCALE)
            lf = jnp.sum((lc - 1.0 - t) * kh * dkf, axis=0, keepdims=True)
            lb = jnp.sum(t * kh * dkb, axis=0, keepdims=True)
            row_f = row_f + jnp.where(lane == h, jnp.sum(lf, axis=1, keepdims=True), 0.0)
            row_b = row_b + jnp.where(lane == h, jnp.sum(lb, axis=1, keepdims=True), 0.0)
        dlg_ref[...] = dlgin_ref[...]
        dlg_ref[0:1, 0:128] += row_f
        dlg_ref[1:2, 0:128] += row_b
        dkv = jnp.concatenate(dks + dvs, axis=1).astype(BF16)
        dhc = _dot(dkv, win_ref[KV_LO:KV_HI, :])
        dwin_ref[...] = _dot_tn(dkv, hc_ref[...])
        x = ctx_ref[...]
        r = lax.rsqrt(jnp.mean(x * x, axis=-1, keepdims=True) + EPS)
        xh = x * r
        small_ref[...] = jnp.zeros_like(small_ref)
        small_ref[0:1, :] = jnp.sum(dhc, axis=0, keepdims=True)
        dhxx = jnp.sum(dhc * xh, axis=0, keepdims=True)
        small_ref[1:2, :] = dhxx * n1_ref[...]
        small_ref[2:3, :] = dhxx * (1.0 + cmod_ref[1:2, :])

    return pl.pallas_call(
        body, name="ctx_bwd",
        out_shape=(jax.ShapeDtypeStruct((2 * RW, D), F32), jax.ShapeDtypeStruct((8, D), F32),
                   jax.ShapeDtypeStruct((8, D), F32)),
        compiler_params=_cparams(),
    )(ctx, hc, kvc, cmod6, norm1, win, rlf, rlb, dscf, dscb, dlg_in)


def _adam_math(w, g, m, v):
    m = ADAM_B1 * m + (1.0 - ADAM_B1) * g
    v = ADAM_B2 * v + (1.0 - ADAM_B2) * (g * g)
    m_hat = m / (1.0 - ADAM_B1 ** ADAM_STEP)
    v_hat = v / (1.0 - ADAM_B2 ** ADAM_STEP)
    delta = -ADAM_LR * (m_hat / (jnp.sqrt(v_hat) + ADAM_EPS) + ADAM_WD * w)
    return delta, m, v


def _place():
    x, y, c = lax.axis_index("x"), lax.axis_index("y"), lax.axis_index("c")
    return x, y, c, 4 * x + 2 * y + c


def _flip(x, y, c, k):
    px = 1 - x if (k >> 2) & 1 else x
    py = 1 - y if (k >> 1) & 1 else y
    pc = 1 - c if k & 1 else c
    return (px, py, pc), 4 * px + 2 * py + pc


def _gather_copy(buf_ref, send_sems, recv_sems, sem0, k, mine):
    x, y, c, me = _place()
    dev, idx = _flip(x, y, c, k)
    blk = me if mine else idx
    return pltpu.make_async_remote_copy(
        src_ref=buf_ref.at[blk], dst_ref=buf_ref.at[blk],
        send_sem=send_sems.at[sem0 + k - 1], recv_sem=recv_sems.at[sem0 + k - 1],
        device_id=dev, device_id_type=MESH)


def _entry_gather(c_row, cctx_row, wmod, bmod, shards):
    n = len(shards)
    ncol = wmod.shape[1]

    def body(*refs):
        c_ref, cctx_ref, wmod_ref, bmod_ref = refs[:4]
        in_refs = refs[4:4 + n]
        mod_ref, cs_ref = refs[4 + n:6 + n]
        out_refs = refs[6 + n:6 + 2 * n]
        cbuf, pbuf = refs[6 + 2 * n:8 + 2 * n]
        cast_refs = refs[8 + 2 * n:8 + 3 * n]
        small_send, small_recv, send_sems, recv_sems, local_sems = refs[8 + 3 * n:]
        x, y, c, me = _place()
        sib = (x, y, 1 - c)
        chips = [(1 - x, y), (x, 1 - y), (1 - x, 1 - y)]

        cbuf[me] = jnp.broadcast_to(c_ref[...], (8, D))
        small = [_gather_copy(cbuf, small_send, small_recv, 0, k, True) for k in range(1, NDEV)]
        for cp in small:
            cp.start()

        def blk(px, py, pc):
            return 4 * px + 2 * py + pc

        def copy(w, k, block, to, src=None):
            dst = out_refs[w].at[block]
            return pltpu.make_async_remote_copy(
                src_ref=dst if src is None else src, dst_ref=dst,
                send_sem=send_sems.at[w, k], recv_sem=recv_sems.at[w, k], device_id=to, device_id_type=MESH)

        first, passed, mine = [], [], []
        for w in range(n):
            cast_refs[w][...] = in_refs[w][...].astype(BF16)
            m = pltpu.make_async_copy(cast_refs[w], out_refs[w].at[me], local_sems.at[w])
            m.start()
            mine.append(m)
            cps = [copy(w, 0, me, sib, src=cast_refs[w])]
            cps += [copy(w, 1 + j, me, (*chip, c), src=cast_refs[w]) for j, chip in enumerate(chips)]
            for cp in cps:
                cp.start()
            first += cps

        for k in range(1, NDEV):
            _gather_copy(cbuf, small_send, small_recv, 0, k, False).wait_recv()
        row = lax.broadcasted_iota(jnp.int32, (16, D), 0)
        call = jnp.where(row == 8, jnp.broadcast_to(cctx_ref[...], (16, D)), 0.0)
        for d in range(NDEV):
            call = jnp.where(row == d, jnp.concatenate([cbuf[d], cbuf[d]], axis=0), call)
        cs = call * _sigmoid(call)
        cs_ref[...] = cs
        pbuf[me] = _dot(cs.astype(BF16), wmod_ref[...].astype(BF16))
        small2 = [_gather_copy(pbuf, small_send, small_recv, NDEV - 1, k, True) for k in range(1, NDEV)]
        for cp in small2:
            cp.start()

        for w in range(n):
            for j, chip in enumerate(chips):
                b = blk(*chip, c)
                copy(w, 1 + j, b, (x, y, c)).wait_recv()
                fw = copy(w, 4 + j, b, sib)
                fw.start()
                passed.append(fw)
        for w in range(n):
            copy(w, 0, blk(x, y, 1 - c), (x, y, c)).wait_recv()
            for j, chip in enumerate(chips):
                copy(w, 4 + j, blk(*chip, 1 - c), (x, y, c)).wait_recv()

        for k in range(1, NDEV):
            _gather_copy(pbuf, small_send, small_recv, NDEV - 1, k, False).wait_recv()
        for d in range(NDEV):
            mod_ref[:, d * ncol:(d + 1) * ncol] = pbuf[d] + bmod_ref[:, d * ncol:(d + 1) * ncol]
        for cp in small + small2 + first + passed:
            cp.wait_send()
        for m in mine:
            m.wait()

    vm = pl.BlockSpec(memory_space=pltpu.VMEM)
    hbm = pl.BlockSpec(memory_space=pltpu.HBM)
    return pl.pallas_call(
        body, name="entry_gather",
        in_specs=[vm] * (4 + n), out_specs=[vm, vm] + [hbm] * n,
        out_shape=(jax.ShapeDtypeStruct((16, 6 * D), F32), jax.ShapeDtypeStruct((16, D), F32))
        + tuple(jax.ShapeDtypeStruct((NDEV,) + s.shape, BF16) for s in shards),
        scratch_shapes=[pltpu.VMEM((NDEV, 8, D), F32), pltpu.VMEM((NDEV, 16, ncol), F32)]
        + [pltpu.VMEM(s.shape, BF16) for s in shards]
        + [pltpu.SemaphoreType.DMA((2 * (NDEV - 1),)), pltpu.SemaphoreType.DMA((2 * (NDEV - 1),)),
           pltpu.SemaphoreType.DMA((n, 7)), pltpu.SemaphoreType.DMA((n, 7)), pltpu.SemaphoreType.DMA((n,))],
        compiler_params=_cparams(),
    )(c_row, cctx_row, wmod, bmod, *shards)


_HBM = pl.BlockSpec(memory_space=pltpu.HBM)
_SEMS = pl.BlockSpec(memory_space=pltpu.SEMAPHORE)
_EFFECT = pltpu.SideEffectType.DATAFLOW_SIDE_EFFECTING


def _exchange_copy(src_refs, land_refs, send_sems, recv_sems, w, k, scatter, landing_slot_is_mine):
    x, y, c, me = _place()
    dev, pidx = _flip(x, y, c, k)
    src = src_refs[w].at[pidx] if scatter else src_refs[w]
    slot = me if landing_slot_is_mine else pidx
    return pltpu.make_async_remote_copy(
        src_ref=src, dst_ref=land_refs[w].at[slot],
        send_sem=send_sems.at[w * (NDEV - 1) + k - 1], recv_sem=recv_sems.at[w * (NDEV - 1) + k - 1],
        device_id=dev, device_id_type=MESH)


def _exchange_start(srcs, *, scatter, name):
    n = len(srcs)
    lands = [lax.empty((NDEV,) + tuple(s.shape[-2:]), s.dtype) for s in srcs]

    def body(*refs):
        src_refs, land_refs = refs[:n], refs[n:2 * n]
        send_sems, recv_sems = refs[2 * n], refs[2 * n + 1]
        token = refs[-1]
        for w in range(n):
            for k in range(1, NDEV):
                _exchange_copy(src_refs, land_refs, send_sems, recv_sems, w, k, scatter, True).start()
        token[...] = jnp.zeros_like(token)

    arrs = list(srcs) + lands
    out_shape = ([pltpu.SemaphoreType.DMA((n * (NDEV - 1),)), pltpu.SemaphoreType.DMA((n * (NDEV - 1),))]
                 + [pltpu.HBM(a.shape, a.dtype) for a in arrs] + [jax.ShapeDtypeStruct((8, 128), F32)])
    res = pl.pallas_call(
        body, name=name, out_shape=tuple(out_shape),
        in_specs=[_HBM] * (2 * n),
        out_specs=tuple([_SEMS, _SEMS] + [_HBM] * (2 * n) + [pl.BlockSpec(memory_space=pltpu.VMEM)]),
        input_output_aliases={i: 2 + i for i in range(2 * n)},
        compiler_params=pltpu.CompilerParams(has_side_effects=_EFFECT),
    )(*[pltpu.with_memory_space_constraint(a, pltpu.HBM) for a in arrs])
    return res[:-1], res[-1]


def _exchange_wait(handles, after, *, scatter, name):
    n = (len(handles) - 2) // 2
    na = len(after)

    def body(*refs):
        src_refs, land_refs = refs[:n], refs[n:2 * n]
        send_sems, recv_sems = refs[2 * n], refs[2 * n + 1]
        for w in range(n):
            for k in range(1, NDEV):
                cp = _exchange_copy(src_refs, land_refs, send_sems, recv_sems, w, k, scatter, False)
                cp.wait_send()
                cp.wait_recv()

    arrs = handles[2:]
    res = pl.pallas_call(
        body, name=name, out_shape=tuple(pltpu.HBM(a.shape, a.dtype) for a in arrs),
        in_specs=[_HBM] * (2 * n) + [_SEMS, _SEMS] + [_HBM] * na,
        out_specs=tuple([_HBM] * (2 * n)),
        input_output_aliases={i: i for i in range(2 * n)},
        compiler_params=pltpu.CompilerParams(has_side_effects=_EFFECT),
    )(*arrs, handles[0], handles[1], *[pltpu.with_memory_space_constraint(a, pltpu.HBM) for a in after])
    return res[:n], res[n:]


_SAME_CORE = (2, 4, 6)


def _gather2_copy(src_ref, land_ref, send_sems, recv_sems, sem, k, block):
    x, y, c, _ = _place()
    dev, _ = _flip(x, y, c, k)
    dst = land_ref.at[block]
    return pltpu.make_async_remote_copy(
        src_ref=dst if src_ref is None else src_ref, dst_ref=dst,
        send_sem=send_sems.at[sem], recv_sem=recv_sems.at[sem], device_id=dev, device_id_type=MESH)


def _split_call(body, name, arrs, n_sems, sems=None, after=(), token=False):
    na = len(arrs)
    out_shape, out_specs = [], []
    if n_sems is not None:
        out_shape += [pltpu.SemaphoreType.DMA((n_sems,)), pltpu.SemaphoreType.DMA((n_sems,))]
        out_specs += [_SEMS, _SEMS]
    first = len(out_shape)
    out_shape += [pltpu.HBM(a.shape, a.dtype) for a in arrs]
    out_specs += [_HBM] * na
    if token:
        out_shape.append(jax.ShapeDtypeStruct((8, 128), F32))
        out_specs.append(pl.BlockSpec(memory_space=pltpu.VMEM))
    in_specs = [_HBM] * na + ([_SEMS, _SEMS] if sems is not None else []) + [_HBM] * len(after)
    args = [pltpu.with_memory_space_constraint(a, pltpu.HBM) for a in arrs] + list(sems or ()) \
        + [pltpu.with_memory_space_constraint(a, pltpu.HBM) for a in after]
    return pl.pallas_call(
        body, name=name, out_shape=tuple(out_shape), in_specs=in_specs, out_specs=tuple(out_specs),
        input_output_aliases={i: first + i for i in range(na)},
        compiler_params=pltpu.CompilerParams(has_side_effects=_EFFECT))(*args)


def _gather2_start(srcs, *, name):
    n = len(srcs)
    lands = [lax.empty((NDEV,) + tuple(s.shape), s.dtype) for s in srcs]

    def body(*refs):
        src_refs, land_refs = refs[:n], refs[n:2 * n]
        send_sems, recv_sems = refs[2 * n], refs[2 * n + 1]
        _, _, _, me = _place()
        for w in range(n):
            for slot, k in enumerate((1,) + _SAME_CORE):
                _gather2_copy(src_refs[w], land_refs[w], send_sems, recv_sems, 4 * w + slot, k, me).start()
        refs[-1][...] = jnp.zeros_like(refs[-1])

    res = _split_call(body, name, list(srcs) + lands, 4 * n, token=True)
    return res[:2], res[2:-1], res[-1]


def _gather2_arrived(sems, arrs, after, *, name):
    n = len(arrs) // 2

    def body(*refs):
        src_refs, land_refs = refs[:n], refs[n:2 * n]
        send_sems, recv_sems = refs[2 * n], refs[2 * n + 1]
        x, y, c, me = _place()
        for w in range(n):
            for slot, k in enumerate((1,) + _SAME_CORE):
                _, pidx = _flip(x, y, c, k)
                _gather2_copy(src_refs[w], land_refs[w], send_sems, recv_sems, 4 * w + slot, k, me).wait_send()
                _gather2_copy(None, land_refs[w], send_sems, recv_sems, 4 * w + slot, k, pidx).wait_recv()

    return _split_call(body, name, arrs, None, sems=sems, after=after)


def _gather2_forward(lands, *, name):
    n = len(lands)

    def body(*refs):
        land_refs = refs[:n]
        send_sems, recv_sems = refs[n], refs[n + 1]
        x, y, c, _ = _place()
        for w in range(n):
            for j, k in enumerate(_SAME_CORE):
                _, pidx = _flip(x, y, c, k)
                _gather2_copy(None, land_refs[w], send_sems, recv_sems, 3 * w + j, 1, pidx).start()
        refs[-1][...] = jnp.zeros_like(refs[-1])

    res = _split_call(body, name, list(lands), 3 * n, token=True)
    return res[:2], res[2:-1], res[-1]


def _gather2_wait(sems, lands, after, *, name):
    n = len(lands)

    def body(*refs):
        land_refs = refs[:n]
        send_sems, recv_sems = refs[n], refs[n + 1]
        x, y, c, _ = _place()
        for w in range(n):
            for j, k in enumerate(_SAME_CORE):
                _, mine = _flip(x, y, c, k)
                _, theirs = _flip(x, y, c, k ^ 1)
                _gather2_copy(None, land_refs[w], send_sems, recv_sems, 3 * w + j, 1, mine).wait_send()
                _gather2_copy(None, land_refs[w], send_sems, recv_sems, 3 * w + j, 1, theirs).wait_recv()

    return _split_call(body, name, list(lands), None, sems=sems, after=after)


def _cast_bf16(arrs, *, name, after=()):
    n = len(arrs)

    def body(*refs):
        for i_ref, o_ref in zip(refs[:n], refs[n + len(after):]):
            o_ref[...] = i_ref[...].astype(BF16)

    return pl.pallas_call(
        body, name=name, out_shape=tuple(jax.ShapeDtypeStruct(a.shape, BF16) for a in arrs),
        in_specs=[pl.BlockSpec(memory_space=pltpu.VMEM)] * n + [pl.BlockSpec(memory_space=pl.ANY)] * len(after),
        compiler_params=_cparams())(*arrs, *after)


def _rs_adam(me_arr, fulls, lands, w, m, v, *, name):
    rows = lands[0].shape[1]
    k = len(fulls)

    def body(me_ref, *refs):
        own_refs, land_refs = refs[:k], refs[k:2 * k]
        w_ref, m_ref, v_ref, g_ref, d_ref, mo_ref, vo_ref = refs[2 * k:]
        me = me_ref[0]
        parts = []
        for own_ref, land_ref in zip(own_refs, land_refs):
            acc = jnp.zeros(own_ref.shape, F32)
            for p in range(NDEV):
                acc = acc + jnp.where(p == me, own_ref[...], land_ref[p].astype(F32))
            parts.append(acc)
        acc = parts[0] if k == 1 else jnp.concatenate(parts, axis=1)
        g_ref[...] = acc
        d_ref[...], mo_ref[...], vo_ref[...] = _adam_math(w_ref[...], acc, m_ref[...], v_ref[...])

    sh = jax.ShapeDtypeStruct((rows, D), F32)
    whole2 = pl.BlockSpec((rows, D), lambda i, me: (0, 0))
    grid_spec = pltpu.PrefetchScalarGridSpec(
        num_scalar_prefetch=1, grid=(1,),
        in_specs=[pl.BlockSpec((rows, f.shape[1]), lambda i, me: (me[0], 0)) for f in fulls]
        + [pl.BlockSpec((NDEV, rows, l.shape[2]), lambda i, me: (0, 0, 0)) for l in lands]
        + [whole2, whole2, whole2],
        out_specs=[whole2] * 4)
    return pl.pallas_call(
        body, name=name, out_shape=(sh, sh, sh, sh), grid_spec=grid_spec, compiler_params=_cparams(1),
    )(me_arr, *fulls, *lands, w, m, v)


ROW_F, ROW_I, ROW_C, ROW_G1, ROW_LG = 0, 8, 16, 24, 32
PACK_ROWS = 40


def _small_sum(me_arr, pack, pack_land, sgw, sgw_land, wmod):
    ncol = wmod.shape[1]

    def body(me_ref, pack_ref, pland_ref, sgw_ref, sland_ref, wmod_ref, sum_ref, all_ref, sgw_sum_ref, part_ref):
        me = me_ref[0]
        acc = jnp.zeros((PACK_ROWS, D), F32)
        acc_w = jnp.zeros(sgw_ref.shape, F32)
        for p in range(NDEV):
            rows = jnp.where(p == me, pack_ref[...], pland_ref[p])
            all_ref[p] = rows
            acc = acc + rows
            acc_w = acc_w + jnp.where(p == me, sgw_ref[...], sland_ref[p])
        sum_ref[...] = acc
        sgw_sum_ref[...] = acc_w
        zero = jnp.zeros((1, D), F32)
        dcmod = jnp.concatenate([acc[ROW_C:ROW_C + 1], acc[ROW_C + 1:ROW_C + 2], zero, zero, zero, zero], axis=1)
        mine = jnp.zeros((1, ncol), F32)
        for d in range(NDEV):
            mine = mine + jnp.where(d == me, dcmod[:, d * ncol:(d + 1) * ncol], 0.0)
        part_ref[...] = _dot_nt(jnp.broadcast_to(mine, (8, ncol)).astype(BF16), wmod_ref[...].astype(BF16))

    vm = pl.BlockSpec(memory_space=pltpu.VMEM)
    return pl.pallas_call(
        body, name="small_sum",
        out_shape=(jax.ShapeDtypeStruct((PACK_ROWS, D), F32), jax.ShapeDtypeStruct((NDEV, PACK_ROWS, D), F32),
                   jax.ShapeDtypeStruct(sgw.shape, F32), jax.ShapeDtypeStruct((8, D), F32)),
        in_specs=[pl.BlockSpec(memory_space=pltpu.SMEM)] + [vm] * 5,
        compiler_params=_cparams(),
    )(me_arr, pack, pack_land, sgw, sgw_land, wmod)


def _cctx_final(me_arr, part, part_land, cctx_row, m_row, v_row):
    def body(me_ref, part_ref, land_ref, c_ref, m_ref, v_ref, g_ref, d_ref, mo_ref, vo_ref):
        me = me_ref[0]
        tot = jnp.zeros((8, D), F32)
        for p in range(NDEV):
            tot = tot + jnp.where(p == me, part_ref[...], land_ref[p])
        cc = c_ref[...]
        _, dsilu = _silu_parts(cc)
        g = tot[0:1, :] * dsilu
        g_ref[...] = g
        d_ref[...], mo_ref[...], vo_ref[...] = _adam_math(cc, g, m_ref[...], v_ref[...])

    row = jax.ShapeDtypeStruct((1, D), F32)
    vm = pl.BlockSpec(memory_space=pltpu.VMEM)
    return pl.pallas_call(
        body, name="cctx_final", out_shape=(row, row, row, row),
        in_specs=[pl.BlockSpec(memory_space=pltpu.SMEM)] + [vm] * 5,
        compiler_params=_cparams(),
    )(me_arr, part, part_land, cctx_row, m_row, v_row)


def _adam_small(s, sgw_g, params):
    names = ["norm1", "norm2", "norm_f", "sg_gain", "sg_b", "ret_logit_f", "ret_logit_b", "b_mod", "sg_w"]
    flat = [a for n in names for a in params[n]]

    def body(*refs):
        s_ref, sgw_ref = refs[0], refs[1]
        p_refs = refs[2:2 + 3 * len(names)]
        o_refs = refs[2 + 3 * len(names):]
        loss_ref = o_refs[-1]

        def row(r):
            return s_ref[r:r + 1, :]

        grads = {
            "norm1": row(ROW_I + 2) + row(ROW_C + 2),
            "norm2": row(ROW_F + 4),
            "norm_f": row(ROW_F + 0),
            "sg_gain": s_ref[ROW_I + 3:ROW_I + 4, 0:AW],
            "sg_b": s_ref[ROW_I + 4:ROW_I + 8, 0:DH],
            "sg_w": sgw_ref[...],
        }
        for j, n in enumerate(names):
            w_ref, m_ref, v_ref = p_refs[3 * j:3 * j + 3]
            g_ref, d_ref, mo_ref, vo_ref = o_refs[4 * j:4 * j + 4]
            w = w_ref[...]
            if n in ("ret_logit_f", "ret_logit_b"):
                r = ROW_LG + (0 if n == "ret_logit_f" else 1)
                g = s_ref[r:r + 1, 0:H] * _sigmoid(-w)
            elif n == "b_mod":
                rows = [row(ROW_I + 0) + row(ROW_C + 0), row(ROW_I + 1) + row(ROW_C + 1), row(ROW_G1),
                        row(ROW_F + 2), row(ROW_F + 3), row(ROW_F + 1)]
                g = jnp.concatenate(rows, axis=1)
            else:
                g = grads[n]
            g_ref[...] = g
            d_ref[...], mo_ref[...], vo_ref[...] = _adam_math(w, g, m_ref[...], v_ref[...])
        loss_ref[...] = jnp.full((1, 1), 0.5 / D, F32) * jnp.sum(row(ROW_F + 5), axis=1, keepdims=True)

    out_shape = []
    for n in names:
        w = params[n][0]
        out_shape += [jax.ShapeDtypeStruct(w.shape, F32)] * 4
    out_shape.append(jax.ShapeDtypeStruct((1, 1), F32))
    outs = pl.pallas_call(body, name="adam_small", out_shape=tuple(out_shape), compiler_params=_cparams())(
        s, sgw_g, *flat)
    return {n: tuple(outs[4 * j:4 * j + 4]) for j, n in enumerate(names)}, outs[-1]


def _wmod_grad(cs_all, dmod_cols, wmod, m, v):
    ncol = wmod.shape[1]

    def body(cs_ref, dm_ref, w_ref, m_ref, v_ref, g_ref, d_ref, mo_ref, vo_ref):
        g = _dot_tn(cs_ref[...].astype(BF16), dm_ref[...].astype(BF16))
        g_ref[...] = g
        d_ref[...], mo_ref[...], vo_ref[...] = _adam_math(w_ref[...], g, m_ref[...], v_ref[...])

    sh = jax.ShapeDtypeStruct((D, ncol), F32)
    return pl.pallas_call(
        body, name="wmod_grad", out_shape=(sh, sh, sh, sh),
        compiler_params=_cparams(),
    )(cs_all, dmod_cols, wmod, m, v)


def _rope_tables(t_len):
    n_freq = DH // 4
    inv = (ROPE_BASE ** (-np.arange(n_freq, dtype=np.float32) / n_freq)).astype(np.float32)
    tok = np.arange(t_len)
    rows = (tok // GRID_W).astype(np.float32)
    cols = (tok % GRID_W).astype(np.float32)
    ang_r = rows[:, None] * inv[None, :]
    ang_c = cols[:, None] * inv[None, :]
    cos = np.concatenate([np.cos(ang_r)] * 2 + [np.cos(ang_c)] * 2, axis=1).astype(np.float32)
    sin = np.concatenate([-np.sin(ang_r), np.sin(ang_r), -np.sin(ang_c), np.sin(ang_c)], axis=1).astype(np.float32)
    return jnp.asarray(cos), jnp.asarray(sin)


def _local_step(x, tgt, ctx, mod6, cmod6, norm1, norm2, normf, win, wout, ffn_weights, sgw, sgb, sggain, rlf, rlb,
                *, tm_a, tm_b, tm_f, tm_m, tm_r, tm_i, bt_w, after_first_grads=None, small_path=None,
                after_in_half=None):
    t_len = x.shape[0]
    cos, sin = _rope_tables(t_len)
    sgbt = jnp.broadcast_to(sgb[:, :, None], (G, C, 128))
    rlf = jnp.broadcast_to(rlf.reshape(H, 1), (H, 128))
    rlb = jnp.broadcast_to(rlb.reshape(H, 1), (H, 128))
    hc, kvc, scf, scb = _ctx_fwd(ctx, cmod6, norm1, win, rlf, rlb)
    hx, zuv, q, k, v, gfb, of, ya, sst = _fwd_a(x, mod6, norm1, win, sgw, sgbt, sggain, cos, sin, rlf, scf, tm=tm_a)
    if callable(wout):
        wout, tok = wout(hx)
        rlb = rlb + tok
    ob, ycat, y, x1, rst = _fwd_b(q, k, v, of, gfb, ya, x, mod6, wout, rlb, scb, tm=tm_b)
    wg, wu, wd = ffn_weights(x1) if callable(ffn_weights) else ffn_weights
    dx1, h2, da, db, hm, df, small_f = _ffn(x1, tgt, mod6, norm2, normf, wg, wu, wd, tm=tm_f)
    bt2 = min(2 * bt_w, t_len)
    d_wd, d_wd_b = _tn_matmul(hm, df, bm=DFF // 2, bt=bt2, name="dw_down")
    d_wg, d_wg_b = _tn_matmul(da, h2, bm=DFF // 2, bt=bt2, name="dw_gate")
    d_wu, d_wu_b = _tn_matmul(db, h2, bm=DFF // 2, bt=bt2, name="dw_up")
    dy, dya, dgfb, dof, dob, dg1 = _mid_bwd(dx1, y, of, ob, gfb, mod6, wout, tm=tm_m)
    d_wo, d_wo_b = _tn_matmul(ycat, dy, bm=D, bt=bt2, name="dw_out", transposed=False)
    if after_first_grads is not None:
        rlf = rlf + after_first_grads((d_wd_b, d_wg_b, d_wu_b, d_wo_b))
    dqf, dkf, dvf, dqb, dkb, dvb, dscf, dscb, dlg = _ret_bwd(q, k, v, dof, dob, sst, rst, rlf, rlb, tm=tm_r)
    gx, dz, small_i, dsgw = _in_bwd(x, zuv, dya, dqf, dkf, dvf, dqb, dkb, dvb, dgfb, dx1, cos, sin,
                                    mod6, norm1, win, sgw, sgbt, sggain, tm=tm_i)
    dwin_c, small_c, dlg = _ctx_bwd(ctx, hc, kvc, cmod6, norm1, win, rlf, rlb, dscf, dscb, dlg)
    pack = jnp.concatenate([small_f, small_i, small_c, dg1, dlg], axis=0)
    after = small_path(pack, dsgw) if small_path is not None else ()
    halves = []
    for j in range(2):
        halves.append(_tn_matmul(dz, hx, bm=INC // 2, bt=bt2, name="dw_in_%d" % j, init=dwin_c,
                                 init_rows=(KV_LO, KV_HI), after=after, cols=(j, D // 2), transposed=False))
        if after_in_half is not None:
            after = after_in_half(j, halves[-1][1])
    grads = {"w_in": halves, "w_out": (d_wo, d_wo_b), "w_gate": (d_wg, d_wg_b), "w_up": (d_wu, d_wu_b),
             "w_down": (d_wd, d_wd_b)}
    return gx, grads, pack, dsgw


def kernel(x, c, ctx, c_ctx, w_mod, b_mod, norm1, w_in, sg_gain, sg_w, sg_b, ret_logit_f, ret_logit_b, w_out, norm2, w_gate, w_up, w_down, norm_f, loss_target, m_c_ctx, m_w_mod, m_b_mod, m_norm1, m_w_in, m_sg_gain, m_sg_w, m_sg_b, m_ret_logit_f, m_ret_logit_b, m_w_out, m_norm2, m_w_gate, m_w_up, m_w_down, m_norm_f, v_c_ctx, v_w_mod, v_b_mod, v_norm1, v_w_in, v_sg_gain, v_sg_w, v_sg_b, v_ret_logit_f, v_ret_logit_b, v_w_out, v_norm2, v_w_gate, v_w_up, v_w_down, v_norm_f):
    t_len = x.shape[1]
    tm = min(512, t_len)
    me = 4 * lax.axis_index("x") + 2 * lax.axis_index("y") + lax.axis_index("c")
    tr = lambda a: jnp.transpose(a[0])

    cctx_row = c_ctx.reshape(1, D)
    mod_all, cs_all, g_in = _entry_gather(c, cctx_row, w_mod[0], b_mod, [tr(w_in)])
    mod6 = lax.dynamic_slice(mod_all, (me, 0), (1, 6 * D)).reshape(6, D)
    cmod6 = mod_all[8].reshape(6, D)
    win_t = g_in.reshape(INC, D)

    (out_shard,) = _cast_bf16([w_out[0]], name="cast_out", after=[g_in])
    h_out, tok_out = _exchange_start([out_shard], scatter=False, name="wout_start")
    ffn_shards = _cast_bf16([tr(w_gate), tr(w_up), w_down[0]], name="cast_ffn", after=[h_out[2]])
    sems_ffn, arrs_ffn, tok = _gather2_start(ffn_shards, name="wffn_start")
    mod6 = mod6 + (tok_out[0, 0] + tok[0, 0])
    ffn = {}

    def place_own(own, lands, rows):
        return [lax.dynamic_update_slice(l, o[None], (me, 0, 0)).reshape(rows, D) for o, l in zip(own, lands)]

    def wout(after):
        own, lands = _exchange_wait(h_out, [after], scatter=False, name="wout_wait")
        arrs = _gather2_arrived(sems_ffn, arrs_ffn, [after], name="wffn_arrived")
        ffn["own"] = arrs[:3]
        ffn["sems"], ffn["lands"], tok_fwd = _gather2_forward(arrs[3:], name="wffn_forward")
        return place_own(own, lands, D)[0], tok_fwd[0, 0]

    def ffn_weights(after):
        lands = _gather2_wait(ffn["sems"], ffn["lands"], [after], name="wffn_wait")
        return place_own(ffn["own"], lands, DFF)

    rs, outs = {}, {}

    def after_first_grads(bf):
        rs["first"], tok_first = _exchange_start([b.reshape(NDEV, b.shape[0] // NDEV, D) for b in bf], scatter=True,
                                                 name="rs_first_start")
        return tok_first[0, 0]

    def small_path(pack, dsgw):
        rs["small"], _ = _exchange_start([pack, dsgw.reshape(G * C, C)], scatter=False, name="small_start")
        return [rs["small"][2], rs["small"][3]]

    def after_in_half(j, half_bf16):
        rs["in%d" % j], _ = _exchange_start([half_bf16.reshape(NDEV, INC // NDEV, D // 2)], scatter=True,
                                            name="rs_in%d_start" % j)
        return [rs["in%d" % j][2]]

    gx, grads, pack, dsgw = _local_step(
        x[0], loss_target[0], ctx[0], mod6, cmod6, norm1, norm2[0:1], norm_f.reshape(1, D), win_t, wout, ffn_weights,
        sg_w[0], sg_b[0], sg_gain, ret_logit_f, ret_logit_b,
        tm_a=tm, tm_b=tm, tm_f=min(256, t_len), tm_m=tm, tm_r=tm, tm_i=tm, bt_w=min(1024, t_len),
        after_first_grads=after_first_grads, small_path=small_path, after_in_half=after_in_half)

    me_arr = me.reshape(1).astype(jnp.int32)
    (pack_own, sgw_own), (pack_land, sgw_land) = _exchange_wait(rs["small"], [rs["in1"][2]], scatter=False,
                                                               name="small_wait")
    psum_rows, pall, sgw_sum, part = _small_sum(me_arr, pack_own, pack_land, sgw_own, sgw_land, w_mod[0])
    h_cc, _ = _exchange_start([part], scatter=False, name="cctx_start")

    dmod_rows = (ROW_I + 0, ROW_I + 1, ROW_G1, ROW_F + 2, ROW_F + 3, ROW_F + 1)
    dmod_all = jnp.concatenate([pall[:, r, :] for r in dmod_rows], axis=1)
    dcmod = jnp.concatenate([psum_rows[ROW_C + 0:ROW_C + 1], psum_rows[ROW_C + 1:ROW_C + 2],
                             jnp.zeros((1, 4 * D), F32)], axis=1)
    dmod_mat = jnp.concatenate([dmod_all, jnp.pad(dcmod, ((0, 7), (0, 0)))], axis=0)
    ncol = 6 * D // NDEV
    dmod_cols = lax.dynamic_slice(dmod_mat, (0, me * ncol), (16, ncol))
    g_wm, d_wm, m_wm, v_wm = _wmod_grad(cs_all, dmod_cols, w_mod[0], m_w_mod[0], v_w_mod[0])
    outs["w_mod"] = (g_wm[None], d_wm[None], m_wm[None], v_wm[None])
    small_params = {
        "norm1": (norm1, m_norm1, v_norm1), "norm2": (norm2, m_norm2, v_norm2),
        "norm_f": tuple(a.reshape(1, D) for a in (norm_f, m_norm_f, v_norm_f)),
        "sg_gain": (sg_gain, m_sg_gain, v_sg_gain), "sg_b": (sg_b[0], m_sg_b[0], v_sg_b[0]),
        "ret_logit_f": (ret_logit_f, m_ret_logit_f, v_ret_logit_f),
        "ret_logit_b": (ret_logit_b, m_ret_logit_b, v_ret_logit_b),
        "b_mod": (b_mod, m_b_mod, v_b_mod), "sg_w": (sg_w[0], m_sg_w[0], v_sg_w[0])}
    small, loss = _adam_small(psum_rows, sgw_sum.reshape(G, C, C), small_params)
    back = {"norm_f": lambda a: a.reshape(D), "sg_b": lambda a: a[None], "sg_w": lambda a: a[None]}
    for name, vals in small.items():
        outs[name] = tuple(back.get(name, lambda a: a)(a) for a in vals)

    _, lands_first = _exchange_wait(rs["first"], [g_wm], scatter=True, name="rs_first_wait")

    def finish(name, fulls, lands, w, m, v, transposed):
        take = tr if transposed else (lambda a: a[0])
        res = _rs_adam(me_arr, fulls, lands, take(w), take(m), take(v), name="adam_" + name)
        put = (lambda a: jnp.transpose(a)[None]) if transposed else (lambda a: a[None])
        outs[name] = tuple(put(a) for a in res)
        return res[0]

    g_down = finish("w_down", [grads["w_down"][0]], [lands_first[0]], w_down, m_w_down, v_w_down, False)
    g_gate = finish("w_gate", [grads["w_gate"][0]], [lands_first[1]], w_gate, m_w_gate, v_w_gate, True)
    g_up = finish("w_up", [grads["w_up"][0]], [lands_first[2]], w_up, m_w_up, v_w_up, True)
    g_out = finish("w_out", [grads["w_out"][0]], [lands_first[3]], w_out, m_w_out, v_w_out, False)
    done = [g_down, g_gate, g_up, g_out]
    (part_own,), (part_land,) = _exchange_wait(h_cc, done, scatter=False, name="cctx_wait")
    res_cc = _cctx_final(me_arr, part_own, part_land, cctx_row, m_c_ctx.reshape(1, D), v_c_ctx.reshape(1, D))
    outs["c_ctx"] = tuple(a.reshape(D) for a in res_cc)
    lands_in = [_exchange_wait(rs["in%d" % j], done, scatter=True, name="rs_in%d_wait" % j)[1][0] for j in range(2)]
    finish("w_in", [h[0] for h in grads["w_in"]], lands_in, w_in, m_w_in, v_w_in, True)

    order = ["c_ctx", "w_mod", "b_mod", "norm1", "w_in", "sg_gain", "sg_w", "sg_b", "ret_logit_f", "ret_logit_b",
             "w_out", "norm2", "w_gate", "w_up", "w_down", "norm_f"]
    res = [loss.reshape(()), gx[None]]
    for j in range(4):
        res += [outs[name][j] for name in order]
    return tuple(res)
```

```python
import functools
import math

import numpy as np
import jax
import jax.numpy as jnp
from jax import lax
from jax.experimental import pallas as pl
from jax.experimental.pallas import tpu as pltpu

F32 = jnp.float32
BF16 = jnp.bfloat16

D = 1024
AW = 512
RW = 512
H = 4
DH = 128
G = 4
C = 128
CR = 256
DFF = 2816
INC = 3584
Q_LO = 2 * AW
KV_LO, VR_LO, G_LO = Q_LO + RW, Q_LO + 2 * RW, Q_LO + 3 * RW
KV_HI = G_LO
NDEV = 8
EPS = 1e-6
KSCALE = DH ** -0.5
ROPE_BASE = 10000.0
GRID_W = 64

ADAM_LR = 0.001
ADAM_B1 = 0.9
ADAM_B2 = 0.999
ADAM_EPS = 1e-08
ADAM_WD = 0.01
ADAM_STEP = 10

VMEM_LIMIT = 56 * 1024 * 1024
MESH = pl.DeviceIdType.MESH


def _cparams(n_grid=0, **kw):
    sem = ("arbitrary",) * n_grid if n_grid else None
    return pltpu.CompilerParams(dimension_semantics=sem, vmem_limit_bytes=VMEM_LIMIT, **kw)


def _dot(a, b):
    return jnp.dot(a, b, preferred_element_type=F32)


def _dot_nt(a, b):
    return lax.dot_general(a, b, (((1,), (1,)), ((), ())), preferred_element_type=F32)


def _dot_tn(a, b):
    return lax.dot_general(a, b, (((0,), (0,)), ((), ())), preferred_element_type=F32)


def _whole(shape):
    nd = len(shape)
    return pl.BlockSpec(shape, lambda *_: (0,) * nd, pipeline_mode=pl.Buffered(1))


def _acc(shape):
    nd = len(shape)
    return pl.BlockSpec(shape, lambda *_: (0,) * nd)


def _rows(tm, n):
    return pl.BlockSpec((tm, n), lambda i: (i, 0))


def _rows_rev(tm, n, nt):
    return pl.BlockSpec((tm, n), lambda i: (nt - 1 - i, 0))


def _cols(n, tm):
    return pl.BlockSpec((n, tm), lambda i: (0, i))


def _sigmoid(x):
    return 1.0 / (1.0 + jnp.exp(-x))


def _log_sigmoid(x):
    return jnp.minimum(x, 0.0) - jnp.log(1.0 + jnp.exp(-jnp.abs(x)))


_GK0 = math.sqrt(2.0 / math.pi)
_GK1 = 0.044715


def _gelu(x):
    x2 = x * x
    t = jnp.tanh(x * (_GK0 + (_GK0 * _GK1) * x2))
    h = 0.5 + 0.5 * t
    g = x * h
    dg = h + g * (1.0 - h) * ((2.0 * _GK0) + (6.0 * _GK0 * _GK1) * x2)
    return g, dg


def _silu_parts(a):
    s = _sigmoid(a)
    sa = a * s
    return sa, s + sa * (1.0 - s)


def _rope(t, cos, sins):
    n = t.shape[1]
    lane = lax.broadcasted_iota(jnp.int32, t.shape, 1)
    first = (lane & 63) < 32
    partner = jnp.where(first, pltpu.roll(t, n - 32, 1), pltpu.roll(t, 32, 1))
    return t * cos + partner * sins


def _headnorm(o):
    outs, rs = [], []
    for h in range(H):
        oh = o[:, h * DH:(h + 1) * DH]
        r = lax.rsqrt(jnp.mean(oh * oh, axis=-1, keepdims=True) + EPS)
        outs.append(oh * r)
        rs.append(r)
    return jnp.concatenate(outs, axis=1), rs


def _decay_consts(lg, forward):
    ii = lax.broadcasted_iota(jnp.int32, (CR, CR), 0).astype(F32)
    jj = lax.broadcasted_iota(jnp.int32, (CR, CR), 1).astype(F32)
    ic = lax.broadcasted_iota(jnp.int32, (CR, 1), 0).astype(F32)
    if forward:
        diff = ii - jj
        xi = jnp.exp(lg * (ic + 1.0))
        z = jnp.exp(lg * (CR - 1.0 - ic))
    else:
        diff = jj - ii
        xi = jnp.exp(lg * (CR - ic))
        z = jnp.exp(lg * ic)
    dm = jnp.where(diff >= 0.0, jnp.exp(lg * jnp.maximum(diff, 0.0)), 0.0)
    dec = jnp.exp(lg * float(CR))
    return dm, xi, z, dec, ic


def _ctx_fwd(ctx, cmod6, norm1, win, rlf, rlb):
    lc = ctx.shape[0]

    def body(ctx_ref, cmod_ref, n1_ref, win_ref, rlf_ref, rlb_ref, hc_ref, kvc_ref, scf_ref, scb_ref):
        x = ctx_ref[...]
        r = lax.rsqrt(jnp.mean(x * x, axis=-1, keepdims=True) + EPS)
        hc = (x * r) * (n1_ref[...] * (1.0 + cmod_ref[1:2, :])) + cmod_ref[0:1, :]
        hcb = hc.astype(BF16)
        hc_ref[...] = hcb
        kv = _dot_nt(hcb, win_ref[KV_LO:KV_HI, :])
        k = kv[:, :RW] * KSCALE
        v = kv[:, RW:]
        kvc_ref[:, :RW] = k
        kvc_ref[:, RW:] = v
        t = lax.broadcasted_iota(jnp.int32, (lc, 1), 0).astype(F32)
        lgf = _log_sigmoid(rlf_ref[...])
        lgb = _log_sigmoid(rlb_ref[...])
        for h in range(H):
            hs = slice(h * DH, (h + 1) * DH)
            wf = jnp.exp(lgf[h:h + 1, 0:1] * (lc - 1.0 - t))
            wb = jnp.exp(lgb[h:h + 1, 0:1] * t)
            vh = v[:, hs].astype(BF16)
            scf_ref[h] = _dot_tn((k[:, hs] * wf).astype(BF16), vh)
            scb_ref[h] = _dot_tn((k[:, hs] * wb).astype(BF16), vh)

    return pl.pallas_call(
        body, name="ctx_fwd",
        out_shape=(jax.ShapeDtypeStruct((lc, D), BF16), jax.ShapeDtypeStruct((lc, 2 * RW), F32),
                   jax.ShapeDtypeStruct((H, DH, DH), F32), jax.ShapeDtypeStruct((H, DH, DH), F32)),
        compiler_params=_cparams(),
    )(ctx, cmod6, norm1, win, rlf, rlb)


def _sg_forward(u, v, sgw_ref, sgbt_ref, sgg_ref, nc):
    ua, dgu = _gelu(u)
    va, dgv = _gelu(v)
    gain = sgg_ref[...]
    mixed, vn_b, vah_l, rv_l = [], [], [], []
    for g in range(G):
        gs = slice(g * DH, (g + 1) * DH)
        vag = va[:, gs]
        rv = lax.rsqrt(jnp.mean(vag * vag, axis=-1, keepdims=True) + EPS)
        vah = vag * rv
        vn = (vah * gain[:, gs]).astype(BF16)
        wg = sgw_ref[g].astype(BF16)
        bcol = sgbt_ref[g]
        rows = [_dot(wg, vn[c * C:(c + 1) * C, :]) + bcol for c in range(nc)]
        mixed.append(jnp.concatenate(rows, axis=0) if nc > 1 else rows[0])
        vn_b.append(vn)
        vah_l.append(vah)
        rv_l.append(rv)
    mixed = jnp.concatenate(mixed, axis=1)
    return ua * mixed, (ua, dgu, dgv, mixed, vn_b, vah_l, rv_l)


def _fwd_a(x, mod6, norm1, win, sgw, sgbt, sggain, cos, sin, rlf, scf, *, tm):
    t_len = x.shape[0]
    nt, nc, ncr = t_len // tm, tm // C, tm // CR

    def body(x_ref, mod_ref, n1_ref, win_ref, sgw_ref, sgbt_ref, sgg_ref, cos_ref, sin_ref, rlf_ref, scf_ref,
             hx_ref, zuv_ref, q_ref, k_ref, v_ref, gfb_ref, of_ref, ya_ref, sst_ref, s_scr):
        i = pl.program_id(0)

        @pl.when(i == 0)
        def _():
            s_scr[...] = scf_ref[...]

        x = x_ref[...]
        r = lax.rsqrt(jnp.mean(x * x, axis=-1, keepdims=True) + EPS)
        hx = (x * r) * (n1_ref[...] * (1.0 + mod_ref[1:2, :])) + mod_ref[0:1, :]
        hxb = hx.astype(BF16)
        hx_ref[...] = hxb
        z = _dot_nt(hxb, win_ref[...])
        zuv_ref[...] = z[:, :2 * AW].astype(BF16)
        gfb_ref[...] = z[:, G_LO:INC].astype(BF16)
        cos = jnp.tile(cos_ref[...], (1, H))
        sins = jnp.tile(sin_ref[...], (1, H))
        q_ref[...] = _rope(z[:, Q_LO:KV_LO], cos, sins).astype(BF16)
        k_ref[...] = (_rope(z[:, KV_LO:VR_LO], cos, sins) * KSCALE).astype(BF16)
        v_ref[...] = z[:, VR_LO:G_LO].astype(BF16)
        ya, _ = _sg_forward(z[:, :AW], z[:, AW:2 * AW], sgw_ref, sgbt_ref, sgg_ref, nc)
        ya_ref[...] = ya.astype(BF16)

        lg = _log_sigmoid(rlf_ref[...])
        for h in range(H):
            dm, xi, zc, dec, _ = _decay_consts(lg[h:h + 1, 0:1], True)
            hs = slice(h * DH, (h + 1) * DH)
            for c in range(ncr):
                rs = slice(c * CR, (c + 1) * CR)
                qc, kc, vc = q_ref[rs, hs], k_ref[rs, hs], v_ref[rs, hs]
                s = s_scr[h]
                sst_ref[c, h] = s
                a = (_dot_nt(qc, kc) * dm).astype(BF16)
                of_ref[rs, hs] = (_dot(a, vc) + xi * _dot(qc, s.astype(BF16))).astype(BF16)
                kz = (kc.astype(F32) * zc).astype(BF16)
                s_scr[h] = dec * s + _dot_tn(kz, vc)

    n_chunks = t_len // CR
    return pl.pallas_call(
        body, name="fwd_a", grid=(nt,),
        in_specs=[_rows(tm, D), _whole((6, D)), _whole((1, D)), _whole((INC, D)), _whole((G, C, C)),
                  _whole((G, C, 128)), _whole((1, AW)), _rows(tm, DH), _rows(tm, DH), _whole((H, 128)),
                  _whole((H, DH, DH))],
        out_specs=[_rows(tm, D), _rows(tm, 2 * AW), _rows(tm, RW), _rows(tm, RW), _rows(tm, RW),
                   _rows(tm, 2 * RW), _rows(tm, RW), _rows(tm, AW),
                   pl.BlockSpec((ncr, H, DH, DH), lambda i: (i, 0, 0, 0))],
        out_shape=(jax.ShapeDtypeStruct((t_len, D), BF16), jax.ShapeDtypeStruct((t_len, 2 * AW), BF16),
                   jax.ShapeDtypeStruct((t_len, RW), BF16), jax.ShapeDtypeStruct((t_len, RW), BF16),
                   jax.ShapeDtypeStruct((t_len, RW), BF16), jax.ShapeDtypeStruct((t_len, 2 * RW), BF16),
                   jax.ShapeDtypeStruct((t_len, RW), BF16), jax.ShapeDtypeStruct((t_len, AW), BF16),
                   jax.ShapeDtypeStruct((n_chunks, H, DH, DH), F32)),
        scratch_shapes=[pltpu.VMEM((H, DH, DH), F32)],
        compiler_params=_cparams(1),
    )(x, mod6, norm1, win, sgw, sgbt, sggain, cos, sin, rlf, scf)


def _fwd_b(q, k, v, of, gfb, ya, x, mod6, wout, rlb, scb, *, tm):
    t_len = x.shape[0]
    nt, nc = t_len // tm, tm // CR

    def body(q_ref, k_ref, v_ref, of_ref, gfb_ref, ya_ref, x_ref, mod_ref, wout_ref, rlb_ref, scb_ref,
             ob_ref, ycat_ref, y_ref, x1_ref, rst_ref, r_scr):
        i = pl.program_id(0)

        @pl.when(i == 0)
        def _():
            r_scr[...] = scb_ref[...]

        lg = _log_sigmoid(rlb_ref[...])
        for h in range(H):
            dm, xi, zc, dec, _ = _decay_consts(lg[h:h + 1, 0:1], False)
            hs = slice(h * DH, (h + 1) * DH)
            for c in reversed(range(nc)):
                rs = slice(c * CR, (c + 1) * CR)
                qc, kc, vc = q_ref[rs, hs], k_ref[rs, hs], v_ref[rs, hs]
                s = r_scr[h]
                rst_ref[c, h] = s
                a = (_dot_nt(qc, kc) * dm).astype(BF16)
                ob_ref[rs, hs] = (_dot(a, vc) + xi * _dot(qc, s.astype(BF16))).astype(BF16)
                kz = (kc.astype(F32) * zc).astype(BF16)
                r_scr[h] = dec * s + _dot_tn(kz, vc)

        gfb = gfb_ref[...].astype(F32)
        sf, _ = _silu_parts(gfb[:, :RW])
        sb, _ = _silu_parts(gfb[:, RW:])
        hnf, _ = _headnorm(of_ref[...].astype(F32))
        hnb, _ = _headnorm(ob_ref[...].astype(F32))
        yr = sf * hnf + sb * hnb
        ycat = jnp.concatenate([ya_ref[...], yr.astype(BF16)], axis=1)
        ycat_ref[...] = ycat
        y = _dot(ycat, wout_ref[...])
        y_ref[...] = y.astype(BF16)
        x1_ref[...] = x_ref[...] + mod_ref[2:3, :] * y

    n_chunks = t_len // CR
    rr = functools.partial(_rows_rev, nt=nt)
    return pl.pallas_call(
        body, name="fwd_b", grid=(nt,),
        in_specs=[rr(tm, RW), rr(tm, RW), rr(tm, RW), rr(tm, RW), rr(tm, 2 * RW), rr(tm, AW), rr(tm, D),
                  _whole((6, D)), _whole((D, D)), _whole((H, 128)), _whole((H, DH, DH))],
        out_specs=[rr(tm, RW), rr(tm, D), rr(tm, D), rr(tm, D),
                   pl.BlockSpec((nc, H, DH, DH), lambda i: (nt - 1 - i, 0, 0, 0))],
        out_shape=(jax.ShapeDtypeStruct((t_len, RW), BF16), jax.ShapeDtypeStruct((t_len, D), BF16),
                   jax.ShapeDtypeStruct((t_len, D), BF16), jax.ShapeDtypeStruct((t_len, D), F32),
                   jax.ShapeDtypeStruct((n_chunks, H, DH, DH), F32)),
        scratch_shapes=[pltpu.VMEM((H, DH, DH), F32)],
        compiler_params=_cparams(1),
    )(q, k, v, of, gfb, ya, x, mod6, wout, rlb, scb)


def _ffn(x1, tgt, mod6, norm2, normf, wg, wu, wd, *, tm):
    t_len = x1.shape[0]
    nt = t_len // tm

    def body(x1_ref, tgt_ref, mod_ref, n2_ref, nf_ref, wg_ref, wu_ref, wd_ref,
             dx1_ref, h2_ref, da_ref, db_ref, hm_ref, df_ref, small_ref):
        i = pl.program_id(0)

        @pl.when(i == 0)
        def _():
            small_ref[...] = jnp.zeros_like(small_ref)

        sh2, sc2, g2 = mod_ref[3:4, :], mod_ref[4:5, :], mod_ref[5:6, :]
        n2, nf = n2_ref[...], nf_ref[...]
        x1 = x1_ref[...]
        r2 = lax.rsqrt(jnp.mean(x1 * x1, axis=-1, keepdims=True) + EPS)
        x1h = x1 * r2
        w2 = n2 * (1.0 + sc2)
        h2b = (x1h * w2 + sh2).astype(BF16)
        h2_ref[...] = h2b
        a = _dot_nt(h2b, wg_ref[...])
        b = _dot_nt(h2b, wu_ref[...])
        sa, dsa = _silu_parts(a)
        hmb = (sa * b).astype(BF16)
        hm_ref[...] = hmb.T
        f = _dot(hmb, wd_ref[...])
        x2 = x1 + g2 * f
        r3 = lax.rsqrt(jnp.mean(x2 * x2, axis=-1, keepdims=True) + EPS)
        x2h = x2 * r3
        diff = x2h * nf - tgt_ref[...]
        small_ref[5:6, :] += jnp.sum(diff * diff, axis=0, keepdims=True)
        dout = diff * (1.0 / D)
        small_ref[0:1, :] += jnp.sum(dout * x2h, axis=0, keepdims=True)
        dyg = dout * nf
        dx2 = r3 * (dyg - x2h * jnp.mean(dyg * x2h, axis=-1, keepdims=True))
        small_ref[1:2, :] += jnp.sum(dx2 * f, axis=0, keepdims=True)
        dfb = (dx2 * g2).astype(BF16)
        df_ref[...] = dfb
        dhm = _dot_nt(dfb, wd_ref[...])
        dbb = (dhm * sa).astype(BF16)
        dab = (dhm * b * dsa).astype(BF16)
        da_ref[...] = dab.T
        db_ref[...] = dbb.T
        dh2 = _dot(dab, wg_ref[...]) + _dot(dbb, wu_ref[...])
        small_ref[2:3, :] += jnp.sum(dh2, axis=0, keepdims=True)
        dhx = dh2 * x1h
        small_ref[3:4, :] += jnp.sum(dhx, axis=0, keepdims=True) * n2
        small_ref[4:5, :] += jnp.sum(dhx, axis=0, keepdims=True) * (1.0 + sc2)
        dyg2 = dh2 * w2
        dx1_ref[...] = dx2 + r2 * (dyg2 - x1h * jnp.mean(dyg2 * x1h, axis=-1, keepdims=True))

    return pl.pallas_call(
        body, name="ffn", grid=(nt,),
        in_specs=[_rows(tm, D), _rows(tm, D), _whole((6, D)), _whole((1, D)), _whole((1, D)),
                  _whole((DFF, D)), _whole((DFF, D)), _whole((DFF, D))],
        out_specs=[_rows(tm, D), _rows(tm, D), _cols(DFF, tm), _cols(DFF, tm), _cols(DFF, tm), _rows(tm, D),
                   _acc((8, D))],
        out_shape=(jax.ShapeDtypeStruct((t_len, D), F32), jax.ShapeDtypeStruct((t_len, D), BF16),
                   jax.ShapeDtypeStruct((DFF, t_len), BF16), jax.ShapeDtypeStruct((DFF, t_len), BF16),
                   jax.ShapeDtypeStruct((DFF, t_len), BF16), jax.ShapeDtypeStruct((t_len, D), BF16),
                   jax.ShapeDtypeStruct((8, D), F32)),
        compiler_params=_cparams(1),
    )(x1, tgt, mod6, norm2, normf, wg, wu, wd)


def _mid_bwd(dx1, y, of, ob, gfb, mod6, wout, *, tm):
    t_len = dx1.shape[0]
    nt = t_len // tm

    def body(dx1_ref, y_ref, of_ref, ob_ref, gfb_ref, mod_ref, wout_ref,
             dy_ref, dya_ref, dgfb_ref, dof_ref, dob_ref, dg1_ref):
        i = pl.program_id(0)

        @pl.when(i == 0)
        def _():
            dg1_ref[...] = jnp.zeros_like(dg1_ref)

        dx1 = dx1_ref[...]
        dg1_ref[0:1, :] += jnp.sum(dx1 * y_ref[...].astype(F32), axis=0, keepdims=True)
        dyb = (dx1 * mod_ref[2:3, :]).astype(BF16)
        dy_ref[...] = dyb
        dycat = _dot_nt(dyb, wout_ref[...])
        dya_ref[...] = dycat[:, :AW].astype(BF16)
        dyr = dycat[:, AW:]
        gfb = gfb_ref[...].astype(F32)
        for o_ref, do_ref, lo in ((of_ref, dof_ref, 0), (ob_ref, dob_ref, RW)):
            sg, dsg = _silu_parts(gfb[:, lo:lo + RW])
            o = o_ref[...].astype(F32)
            hn, rs = _headnorm(o)
            dgfb_ref[:, lo:lo + RW] = (dyr * hn * dsg).astype(BF16)
            dhn = dyr * sg
            for h in range(H):
                hs = slice(h * DH, (h + 1) * DH)
                oh, dh = hn[:, hs], dhn[:, hs]
                do_ref[:, hs] = (rs[h] * (dh - oh * jnp.mean(dh * oh, axis=-1, keepdims=True))).astype(BF16)

    return pl.pallas_call(
        body, name="mid_bwd", grid=(nt,),
        in_specs=[_rows(tm, D), _rows(tm, D), _rows(tm, RW), _rows(tm, RW), _rows(tm, 2 * RW),
                  _whole((6, D)), _whole((D, D))],
        out_specs=[_rows(tm, D), _rows(tm, AW), _rows(tm, 2 * RW), _rows(tm, RW), _rows(tm, RW), _acc((8, D))],
        out_shape=(jax.ShapeDtypeStruct((t_len, D), BF16), jax.ShapeDtypeStruct((t_len, AW), BF16),
                   jax.ShapeDtypeStruct((t_len, 2 * RW), BF16), jax.ShapeDtypeStruct((t_len, RW), BF16),
                   jax.ShapeDtypeStruct((t_len, RW), BF16), jax.ShapeDtypeStruct((8, D), F32)),
        compiler_params=_cparams(1),
    )(dx1, y, of, ob, gfb, mod6, wout)


def _ret_bwd_dir(q_ref, k_ref, v_ref, do_ref, st_ref, dq_ref, dk_ref, dv_ref, ds_scr, dlg_scr, lg, forward, nc, row0):
    order = reversed(range(nc)) if forward else range(nc)
    order = list(order)
    for h in range(H):
        dm, xi, zc, dec, ic = _decay_consts(lg[h:h + 1, 0:1], forward)
        hs = slice(h * DH, (h + 1) * DH)
        if forward:
            w_qi, w_qc, w_ki, w_ks = ic, ic + 1.0, -ic, (CR - 1.0) - ic
        else:
            w_qi, w_qc, w_ki, w_ks = -ic, CR - ic, ic, ic
        acc = jnp.zeros((1, DH), F32)
        for c in order:
            rs = slice(c * CR, (c + 1) * CR)
            qc, kc, vc, doc = q_ref[rs, hs], k_ref[rs, hs], v_ref[rs, hs], do_ref[rs, hs]
            s = st_ref[c, h]
            dsn = ds_scr[h]
            sb, dsnb = s.astype(BF16), dsn.astype(BF16)
            qf, kf = qc.astype(F32), kc.astype(F32)
            a = (_dot_nt(qc, kc) * dm).astype(BF16)
            da = (_dot_nt(doc, vc) * dm).astype(BF16)
            xdo = (xi * doc.astype(F32)).astype(BF16)
            kz = (kf * zc).astype(BF16)
            vz = (vc.astype(F32) * zc).astype(BF16)
            dq_i = _dot(da, kc)
            dq_c = _dot_nt(xdo, sb)
            dk_i = _dot_tn(da, qc)
            dk_s = _dot_nt(vz, dsnb)
            dq_ref[rs, hs] = (dq_i + dq_c).astype(BF16)
            dk_ref[rs, hs] = (dk_i + dk_s).astype(BF16)
            dv_ref[rs, hs] = (_dot_tn(a, doc) + _dot(kz, dsnb)).astype(BF16)
            rowterm = qf * (w_qi * dq_i + w_qc * dq_c) + kf * (w_ki * dk_i + w_ks * dk_s)
            acc = acc + jnp.sum(rowterm, axis=0, keepdims=True)
            acc = acc + (float(CR) * dec) * jnp.sum(s * dsn, axis=0, keepdims=True)
            ds_scr[h] = _dot_tn((xi * qf).astype(BF16), doc) + dec * dsn
        dlg_scr[row0 + h:row0 + h + 1, :] += acc


def _ret_bwd(q, k, v, dof, dob, sst, rst, rlf, rlb, *, tm):
    t_len = q.shape[0]
    nt, nc = t_len // tm, tm // CR

    def body(qf_ref, kf_ref, vf_ref, dof_ref, sst_ref, qb_ref, kb_ref, vb_ref, dob_ref, rst_ref, rlf_ref, rlb_ref,
             dqf_ref, dkf_ref, dvf_ref, dqb_ref, dkb_ref, dvb_ref, dscf_ref, dscb_ref, dlg_ref,
             dsf_scr, dsb_scr, dlg_scr):
        i = pl.program_id(0)

        @pl.when(i == 0)
        def _():
            dsf_scr[...] = jnp.zeros_like(dsf_scr)
            dsb_scr[...] = jnp.zeros_like(dsb_scr)
            dlg_scr[...] = jnp.zeros_like(dlg_scr)

        lgf = _log_sigmoid(rlf_ref[...])
        lgb = _log_sigmoid(rlb_ref[...])
        _ret_bwd_dir(qf_ref, kf_ref, vf_ref, dof_ref, sst_ref, dqf_ref, dkf_ref, dvf_ref, dsf_scr, dlg_scr, lgf, True, nc, 0)
        _ret_bwd_dir(qb_ref, kb_ref, vb_ref, dob_ref, rst_ref, dqb_ref, dkb_ref, dvb_ref, dsb_scr, dlg_scr, lgb, False, nc, H)

        @pl.when(i == nt - 1)
        def _():
            dscf_ref[...] = dsf_scr[...]
            dscb_ref[...] = dsb_scr[...]
            tot = jnp.broadcast_to(jnp.sum(dlg_scr[...], axis=1, keepdims=True), (8, 128))
            ri = lax.broadcasted_iota(jnp.int32, (8, 128), 0)
            li = lax.broadcasted_iota(jnp.int32, (8, 128), 1)
            dlg_ref[...] = jnp.zeros_like(dlg_ref)
            dlg_ref[0:1, 0:128] = jnp.sum(jnp.where(ri == li, tot, 0.0), axis=0, keepdims=True)
            dlg_ref[1:2, 0:128] = jnp.sum(jnp.where(ri - H == li, tot, 0.0), axis=0, keepdims=True)

    rr = functools.partial(_rows_rev, nt=nt)
    st_f = pl.BlockSpec((nc, H, DH, DH), lambda i: (nt - 1 - i, 0, 0, 0))
    st_b = pl.BlockSpec((nc, H, DH, DH), lambda i: (i, 0, 0, 0))
    tok = jax.ShapeDtypeStruct((t_len, RW), BF16)
    st = jax.ShapeDtypeStruct((H, DH, DH), F32)
    return pl.pallas_call(
        body, name="ret_bwd", grid=(nt,),
        in_specs=[rr(tm, RW), rr(tm, RW), rr(tm, RW), rr(tm, RW), st_f,
                  _rows(tm, RW), _rows(tm, RW), _rows(tm, RW), _rows(tm, RW), st_b,
                  _whole((H, 128)), _whole((H, 128))],
        out_specs=[rr(tm, RW), rr(tm, RW), rr(tm, RW), _rows(tm, RW), _rows(tm, RW), _rows(tm, RW),
                   _acc((H, DH, DH)), _acc((H, DH, DH)), _acc((8, D))],
        out_shape=(tok, tok, tok, tok, tok, tok, st, st, jax.ShapeDtypeStruct((8, D), F32)),
        scratch_shapes=[pltpu.VMEM((H, DH, DH), F32), pltpu.VMEM((H, DH, DH), F32), pltpu.VMEM((8, 128), F32)],
        compiler_params=_cparams(1),
    )(q, k, v, dof, sst, q, k, v, dob, rst, rlf, rlb)


def _in_bwd(x, zuv, dya, dqf, dkf, dvf, dqb, dkb, dvb, dgfb, dx1, cos, sin, mod6, norm1, win, sgw, sgbt, sggain, *, tm):
    t_len = x.shape[0]
    nt, nc = t_len // tm, tm // C

    def body(x_ref, zuv_ref, dya_ref, dqf_ref, dkf_ref, dvf_ref, dqb_ref, dkb_ref, dvb_ref, dgfb_ref, dx1_ref,
             cos_ref, sin_ref, mod_ref, n1_ref, win_ref, sgw_ref, sgbt_ref, sgg_ref,
             gx_ref, dz_ref, small_ref, dsgw_ref, dsgbt_ref):
        i = pl.program_id(0)

        @pl.when(i == 0)
        def _():
            small_ref[...] = jnp.zeros_like(small_ref)
            dsgw_ref[...] = jnp.zeros_like(dsgw_ref)
            dsgbt_ref[...] = jnp.zeros_like(dsgbt_ref)

        zuv = zuv_ref[...].astype(F32)
        _, (ua, dgu, dgv, mixed, vn_b, vah_l, rv_l) = _sg_forward(zuv[:, :AW], zuv[:, AW:], sgw_ref, sgbt_ref, sgg_ref, nc)
        dya = dya_ref[...].astype(F32)
        dz_ref[:, 0:AW] = (dya * mixed * dgu).astype(BF16)
        dmixed = dya * ua
        gain = sgg_ref[...]
        for g in range(G):
            gs = slice(g * DH, (g + 1) * DH)
            dmg = dmixed[:, gs]
            dmb = dmg.astype(BF16)
            wgt = sgw_ref[g].astype(BF16)
            dsw = jnp.zeros((C, C), F32)
            dsb = jnp.zeros((C, DH), F32)
            rows = []
            for c in range(nc):
                rs = slice(c * C, (c + 1) * C)
                dsw = dsw + _dot_nt(dmb[rs, :], vn_b[g][rs, :])
                dsb = dsb + dmg[rs, :]
                rows.append(_dot_tn(wgt, dmb[rs, :]))
            dsgw_ref[g] += dsw
            dsgbt_ref[g] += dsb
            dvn = jnp.concatenate(rows, axis=0) if nc > 1 else rows[0]
            vah, rv = vah_l[g], rv_l[g]
            small_ref[3:4, gs] += jnp.sum(dvn * vah, axis=0, keepdims=True)
            dyg = dvn * gain[:, gs]
            dva = rv * (dyg - vah * jnp.mean(dyg * vah, axis=-1, keepdims=True))
            dz_ref[:, AW + g * DH:AW + (g + 1) * DH] = (dva * dgv[:, gs]).astype(BF16)

        cos = jnp.tile(cos_ref[...], (1, H))
        nsins = -jnp.tile(sin_ref[...], (1, H))
        def both(f_ref, b_ref):
            return f_ref[...].astype(F32) + b_ref[...].astype(F32)

        dz_ref[:, Q_LO:KV_LO] = _rope(both(dqf_ref, dqb_ref), cos, nsins).astype(BF16)
        dz_ref[:, KV_LO:VR_LO] = (_rope(both(dkf_ref, dkb_ref), cos, nsins) * KSCALE).astype(BF16)
        dz_ref[:, VR_LO:G_LO] = both(dvf_ref, dvb_ref).astype(BF16)
        dz_ref[:, G_LO:INC] = dgfb_ref[...]

        dhx = _dot(dz_ref[...], win_ref[...])
        x = x_ref[...]
        r = lax.rsqrt(jnp.mean(x * x, axis=-1, keepdims=True) + EPS)
        xh = x * r
        n1, sc1 = n1_ref[...], mod_ref[1:2, :]
        small_ref[0:1, :] += jnp.sum(dhx, axis=0, keepdims=True)
        dhxx = jnp.sum(dhx * xh, axis=0, keepdims=True)
        small_ref[1:2, :] += dhxx * n1
        small_ref[2:3, :] += dhxx * (1.0 + sc1)
        dyg = dhx * (n1 * (1.0 + sc1))
        gx_ref[...] = dx1_ref[...] + r * (dyg - xh * jnp.mean(dyg * xh, axis=-1, keepdims=True))

        @pl.when(i == nt - 1)
        def _():
            ri = lax.broadcasted_iota(jnp.int32, (C, DH), 0)
            li = lax.broadcasted_iota(jnp.int32, (C, DH), 1)
            for g in range(G):
                tot = jnp.broadcast_to(jnp.sum(dsgbt_ref[g], axis=1, keepdims=True), (C, DH))
                small_ref[4 + g:5 + g, 0:DH] = jnp.sum(jnp.where(ri == li, tot, 0.0), axis=0, keepdims=True)

    tok = functools.partial(_rows, tm)
    return pl.pallas_call(
        body, name="in_bwd", grid=(nt,),
        in_specs=[tok(D), tok(2 * AW), tok(AW), tok(RW), tok(RW), tok(RW), tok(RW), tok(RW), tok(RW),
                  tok(2 * RW), tok(D), tok(DH), tok(DH), _whole((6, D)), _whole((1, D)), _whole((INC, D)),
                  _whole((G, C, C)), _whole((G, C, 128)), _whole((1, AW))],
        out_specs=[tok(D), tok(INC), _acc((8, D)), _acc((G, C, C))],
        out_shape=(jax.ShapeDtypeStruct((t_len, D), F32), jax.ShapeDtypeStruct((t_len, INC), BF16),
                   jax.ShapeDtypeStruct((8, D), F32), jax.ShapeDtypeStruct((G, C, C), F32)),
        scratch_shapes=[pltpu.VMEM((G, C, 128), F32)],
        compiler_params=_cparams(1),
    )(x, zuv, dya, dqf, dkf, dvf, dqb, dkb, dvb, dgfb, dx1, cos, sin, mod6, norm1, win, sgw, sgbt, sggain)


def _tn_matmul(at, b, *, bm, bt, name, init=None, init_rows=None, after=(), cols=None, transposed=True):
    m, t_len = at.shape if transposed else at.shape[::-1]
    cj, n = (0, b.shape[1]) if cols is None else cols
    ni, ntt = m // bm, t_len // bt
    has_init = init is not None

    def body(*refs):
        o_ref, ob_ref = refs[-2:]
        a_ref, b_ref = refs[:2]
        init_ref = refs[2] if has_init else None
        i, t = pl.program_id(0), pl.program_id(1)

        @pl.when(t == 0)
        def _():
            o_ref[...] = jnp.zeros_like(o_ref)

        if has_init:
            for ib in range(ni):
                lo, hi = max(init_rows[0], ib * bm), min(init_rows[1], (ib + 1) * bm)
                if lo < hi:
                    @pl.when(jnp.logical_and(t == 0, i == ib))
                    def _(lo=lo, hi=hi, ib=ib):
                        o_ref[lo - ib * bm:hi - ib * bm, :] = init_ref[lo - init_rows[0]:hi - init_rows[0], :]

        o_ref[...] += (_dot if transposed else _dot_tn)(a_ref[...], b_ref[...])

        @pl.when(t == ntt - 1)
        def _():
            ob_ref[...] = o_ref[...].astype(BF16)

    a_spec = pl.BlockSpec((bm, bt), lambda i, t: (i, t)) if transposed else pl.BlockSpec((bt, bm), lambda i, t: (t, i))
    in_specs = [a_spec, pl.BlockSpec((bt, n), lambda i, t: (t, cj))]
    args = [at, b]
    if has_init:
        in_specs.append(pl.BlockSpec((init.shape[0], n), lambda i, t: (0, cj), pipeline_mode=pl.Buffered(1)))
        args.append(init)
    for arr in after:
        in_specs.append(pl.BlockSpec(memory_space=pl.ANY))
        args.append(arr)
    out_spec = pl.BlockSpec((bm, n), lambda i, t: (i, 0))
    return pl.pallas_call(
        body, name=name, grid=(ni, ntt),
        in_specs=in_specs,
        out_specs=[out_spec, out_spec],
        out_shape=(jax.ShapeDtypeStruct((m, n), F32), jax.ShapeDtypeStruct((m, n), BF16)),
        compiler_params=_cparams(2),
    )(*args)


def _ctx_bwd(ctx, hc, kvc, cmod6, norm1, win, rlf, rlb, dscf, dscb, dlg_in):
    lc = ctx.shape[0]

    def body(ctx_ref, hc_ref, kvc_ref, cmod_ref, n1_ref, win_ref, rlf_ref, rlb_ref, dscf_ref, dscb_ref, dlgin_ref,
             dwin_ref, small_ref, dlg_ref):
        k = kvc_ref[:, :RW]
        v = kvc_ref[:, RW:]
        t = lax.broadcasted_iota(jnp.int32, (lc, 1), 0).astype(F32)
        lgf = _log_sigmoid(rlf_ref[...])
        lgb = _log_sigmoid(rlb_ref[...])
        dks, dvs = [], []
        lane = lax.broadcasted_iota(jnp.int32, (1, 128), 1)
        row_f = jnp.zeros((1, 128), F32)
        row_b = jnp.zeros((1, 128), F32)
        for h in range(H):
            hs = slice(h * DH, (h + 1) * DH)
            wf = jnp.exp(lgf[h:h + 1, 0:1] * (lc - 1.0 - t))
            wb = jnp.exp(lgb[h:h + 1, 0:1] * t)
            kh, vhb = k[:, hs], v[:, hs].astype(BF16)
            dsf, dsb = dscf_ref[h].astype(BF16), dscb_ref[h].astype(BF16)
            dkf = wf * _dot_nt(vhb, dsf)
            dkb = wb * _dot_nt(vhb, dsb)
            dvs.append(_dot((kh * wf).astype(BF16), dsf) + _dot((kh * wb).astype(BF16), dsb))
            dks.append((dkf + dkb) * KSCALE)
            lf = jnp.sum((lc - 1.0 - t) * kh * dkf, axis=0, keepdims=True)
            lb = jnp.sum(t * kh * dkb, axis=0, keepdims=True)
            row_f = row_f + jnp.where(lane == h, jnp.sum(lf, axis=1, keepdims=True), 0.0)
            row_b = row_b + jnp.where(lane == h, jnp.sum(lb, axis=1, keepdims=True), 0.0)
        dlg_ref[...] = dlgin_ref[...]
        dlg_ref[0:1, 0:128] += row_f
        dlg_ref[1:2, 0:128] += row_b
        dkv = jnp.concatenate(dks + dvs, axis=1).astype(BF16)
        dhc = _dot(dkv, win_ref[KV_LO:KV_HI, :])
        dwin_ref[...] = _dot_tn(dkv, hc_ref[...])
        x = ctx_ref[...]
        r = lax.rsqrt(jnp.mean(x * x, axis=-1, keepdims=True) + EPS)
        xh = x * r
        small_ref[...] = jnp.zeros_like(small_ref)
        small_ref[0:1, :] = jnp.sum(dhc, axis=0, keepdims=True)
        dhxx = jnp.sum(dhc * xh, axis=0, keepdims=True)
        small_ref[1:2, :] = dhxx * n1_ref[...]
        small_ref[2:3, :] = dhxx * (1.0 + cmod_ref[1:2, :])

    return pl.pallas_call(
        body, name="ctx_bwd",
        out_shape=(jax.ShapeDtypeStruct((2 * RW, D), F32), jax.ShapeDtypeStruct((8, D), F32),
                   jax.ShapeDtypeStruct((8, D), F32)),
        compiler_params=_cparams(),
    )(ctx, hc, kvc, cmod6, norm1, win, rlf, rlb, dscf, dscb, dlg_in)


def _adam_math(w, g, m, v):
    m = ADAM_B1 * m + (1.0 - ADAM_B1) * g
    v = ADAM_B2 * v + (1.0 - ADAM_B2) * (g * g)
    m_hat = m / (1.0 - ADAM_B1 ** ADAM_STEP)
    v_hat = v / (1.0 - ADAM_B2 ** ADAM_STEP)
    delta = -ADAM_LR * (m_hat / (jnp.sqrt(v_hat) + ADAM_EPS) + ADAM_WD * w)
    return delta, m, v


def _place():
    x, y, c = lax.axis_index("x"), lax.axis_index("y"), lax.axis_index("c")
    return x, y, c, 4 * x + 2 * y + c


def _flip(x, y, c, k):
    px = 1 - x if (k >> 2) & 1 else x
    py = 1 - y if (k >> 1) & 1 else y
    pc = 1 - c if k & 1 else c
    return (px, py, pc), 4 * px + 2 * py + pc


def _gather_copy(buf_ref, send_sems, recv_sems, sem0, k, mine):
    x, y, c, me = _place()
    dev, idx = _flip(x, y, c, k)
    blk = me if mine else idx
    return pltpu.make_async_remote_copy(
        src_ref=buf_ref.at[blk], dst_ref=buf_ref.at[blk],
        send_sem=send_sems.at[sem0 + k - 1], recv_sem=recv_sems.at[sem0 + k - 1],
        device_id=dev, device_id_type=MESH)


def _entry_gather(c_row, cctx_row, wmod, bmod, shards):
    n = len(shards)
    ncol = wmod.shape[1]

    def body(*refs):
        c_ref, cctx_ref, wmod_ref, bmod_ref = refs[:4]
        in_refs = refs[4:4 + n]
        mod_ref, cs_ref = refs[4 + n:6 + n]
        out_refs = refs[6 + n:6 + 2 * n]
        cbuf, pbuf = refs[6 + 2 * n:8 + 2 * n]
        cast_refs = refs[8 + 2 * n:8 + 3 * n]
        small_send, small_recv, send_sems, recv_sems, local_sems = refs[8 + 3 * n:]
        x, y, c, me = _place()
        sib = (x, y, 1 - c)
        chips = [(1 - x, y), (x, 1 - y), (1 - x, 1 - y)]

        cbuf[me] = jnp.broadcast_to(c_ref[...], (8, D))
        small = [_gather_copy(cbuf, small_send, small_recv, 0, k, True) for k in range(1, NDEV)]
        for cp in small:
            cp.start()

        def blk(px, py, pc):
            return 4 * px + 2 * py + pc

        def copy(w, k, block, to, src=None):
            dst = out_refs[w].at[block]
            return pltpu.make_async_remote_copy(
                src_ref=dst if src is None else src, dst_ref=dst,
                send_sem=send_sems.at[w, k], recv_sem=recv_sems.at[w, k], device_id=to, device_id_type=MESH)

        first, passed, mine = [], [], []
        for w in range(n):
            cast_refs[w][...] = in_refs[w][...].astype(BF16)
            m = pltpu.make_async_copy(cast_refs[w], out_refs[w].at[me], local_sems.at[w])
            m.start()
            mine.append(m)
            cps = [copy(w, 0, me, sib, src=cast_refs[w])]
            cps += [copy(w, 1 + j, me, (*chip, c), src=cast_refs[w]) for j, chip in enumerate(chips)]
            for cp in cps:
                cp.start()
            first += cps

        for k in range(1, NDEV):
            _gather_copy(cbuf, small_send, small_recv, 0, k, False).wait_recv()
        row = lax.broadcasted_iota(jnp.int32, (16, D), 0)
        call = jnp.where(row == 8, jnp.broadcast_to(cctx_ref[...], (16, D)), 0.0)
        for d in range(NDEV):
            call = jnp.where(row == d, jnp.concatenate([cbuf[d], cbuf[d]], axis=0), call)
        cs = call * _sigmoid(call)
        cs_ref[...] = cs
        pbuf[me] = _dot(cs.astype(BF16), wmod_ref[...].astype(BF16))
        small2 = [_gather_copy(pbuf, small_send, small_recv, NDEV - 1, k, True) for k in range(1, NDEV)]
        for cp in small2:
            cp.start()

        for w in range(n):
            for j, chip in enumerate(chips):
                b = blk(*chip, c)
                copy(w, 1 + j, b, (x, y, c)).wait_recv()
                fw = copy(w, 4 + j, b, sib)
                fw.start()
                passed.append(fw)
        for w in range(n):
            copy(w, 0, blk(x, y, 1 - c), (x, y, c)).wait_recv()
            for j, chip in enumerate(chips):
                copy(w, 4 + j, blk(*chip, 1 - c), (x, y, c)).wait_recv()

        for k in range(1, NDEV):
            _gather_copy(pbuf, small_send, small_recv, NDEV - 1, k, False).wait_recv()
        for d in range(NDEV):
            mod_ref[:, d * ncol:(d + 1) * ncol] = pbuf[d] + bmod_ref[:, d * ncol:(d + 1) * ncol]
        for cp in small + small2 + first + passed:
            cp.wait_send()
        for m in mine:
            m.wait()

    vm = pl.BlockSpec(memory_space=pltpu.VMEM)
    hbm = pl.BlockSpec(memory_space=pltpu.HBM)
    return pl.pallas_call(
        body, name="entry_gather",
        in_specs=[vm] * (4 + n), out_specs=[vm, vm] + [hbm] * n,
        out_shape=(jax.ShapeDtypeStruct((16, 6 * D), F32), jax.ShapeDtypeStruct((16, D), F32))
        + tuple(jax.ShapeDtypeStruct((NDEV,) + s.shape, BF16) for s in shards),
        scratch_shapes=[pltpu.VMEM((NDEV, 8, D), F32), pltpu.VMEM((NDEV, 16, ncol), F32)]
        + [pltpu.VMEM(s.shape, BF16) for s in shards]
        + [pltpu.SemaphoreType.DMA((2 * (NDEV - 1),)), pltpu.SemaphoreType.DMA((2 * (NDEV - 1),)),
           pltpu.SemaphoreType.DMA((n, 7)), pltpu.SemaphoreType.DMA((n, 7)), pltpu.SemaphoreType.DMA((n,))],
        compiler_params=_cparams(),
    )(c_row, cctx_row, wmod, bmod, *shards)


_HBM = pl.BlockSpec(memory_space=pltpu.HBM)
_SEMS = pl.BlockSpec(memory_space=pltpu.SEMAPHORE)
_EFFECT = pltpu.SideEffectType.DATAFLOW_SIDE_EFFECTING


def _exchange_copy(src_refs, land_refs, send_sems, recv_sems, w, k, scatter, landing_slot_is_mine):
    x, y, c, me = _place()
    dev, pidx = _flip(x, y, c, k)
    src = src_refs[w].at[pidx] if scatter else src_refs[w]
    slot = me if landing_slot_is_mine else pidx
    return pltpu.make_async_remote_copy(
        src_ref=src, dst_ref=land_refs[w].at[slot],
        send_sem=send_sems.at[w * (NDEV - 1) + k - 1], recv_sem=recv_sems.at[w * (NDEV - 1) + k - 1],
        device_id=dev, device_id_type=MESH)


def _exchange_start(srcs, *, scatter, name):
    n = len(srcs)
    lands = [lax.empty((NDEV,) + tuple(s.shape[-2:]), s.dtype) for s in srcs]

    def body(*refs):
        src_refs, land_refs = refs[:n], refs[n:2 * n]
        send_sems, recv_sems = refs[2 * n], refs[2 * n + 1]
        token = refs[-1]
        for w in range(n):
            for k in range(1, NDEV):
                _exchange_copy(src_refs, land_refs, send_sems, recv_sems, w, k, scatter, True).start()
        token[...] = jnp.zeros_like(token)

    arrs = list(srcs) + lands
    out_shape = ([pltpu.SemaphoreType.DMA((n * (NDEV - 1),)), pltpu.SemaphoreType.DMA((n * (NDEV - 1),))]
                 + [pltpu.HBM(a.shape, a.dtype) for a in arrs] + [jax.ShapeDtypeStruct((8, 128), F32)])
    res = pl.pallas_call(
        body, name=name, out_shape=tuple(out_shape),
        in_specs=[_HBM] * (2 * n),
        out_specs=tuple([_SEMS, _SEMS] + [_HBM] * (2 * n) + [pl.BlockSpec(memory_space=pltpu.VMEM)]),
        input_output_aliases={i: 2 + i for i in range(2 * n)},
        compiler_params=pltpu.CompilerParams(has_side_effects=_EFFECT),
    )(*[pltpu.with_memory_space_constraint(a, pltpu.HBM) for a in arrs])
    return res[:-1], res[-1]


def _exchange_wait(handles, after, *, scatter, name):
    n = (len(handles) - 2) // 2
    na = len(after)

    def body(*refs):
        src_refs, land_refs = refs[:n], refs[n:2 * n]
        send_sems, recv_sems = refs[2 * n], refs[2 * n + 1]
        for w in range(n):
            for k in range(1, NDEV):
                cp = _exchange_copy(src_refs, land_refs, send_sems, recv_sems, w, k, scatter, False)
                cp.wait_send()
                cp.wait_recv()

    arrs = handles[2:]
    res = pl.pallas_call(
        body, name=name, out_shape=tuple(pltpu.HBM(a.shape, a.dtype) for a in arrs),
        in_specs=[_HBM] * (2 * n) + [_SEMS, _SEMS] + [_HBM] * na,
        out_specs=tuple([_HBM] * (2 * n)),
        input_output_aliases={i: i for i in range(2 * n)},
        compiler_params=pltpu.CompilerParams(has_side_effects=_EFFECT),
    )(*arrs, handles[0], handles[1], *[pltpu.with_memory_space_constraint(a, pltpu.HBM) for a in after])
    return res[:n], res[n:]


_SAME_CORE = (2, 4, 6)


def _gather2_copy(src_ref, land_ref, send_sems, recv_sems, sem, k, block):
    x, y, c, _ = _place()
    dev, _ = _flip(x, y, c, k)
    dst = land_ref.at[block]
    return pltpu.make_async_remote_copy(
        src_ref=dst if src_ref is None else src_ref, dst_ref=dst,
        send_sem=send_sems.at[sem], recv_sem=recv_sems.at[sem], device_id=dev, device_id_type=MESH)


def _split_call(body, name, arrs, n_sems, sems=None, after=(), token=False):
    na = len(arrs)
    out_shape, out_specs = [], []
    if n_sems is not None:
        out_shape += [pltpu.SemaphoreType.DMA((n_sems,)), pltpu.SemaphoreType.DMA((n_sems,))]
        out_specs += [_SEMS, _SEMS]
    first = len(out_shape)
    out_shape += [pltpu.HBM(a.shape, a.dtype) for a in arrs]
    out_specs += [_HBM] * na
    if token:
        out_shape.append(jax.ShapeDtypeStruct((8, 128), F32))
        out_specs.append(pl.BlockSpec(memory_space=pltpu.VMEM))
    in_specs = [_HBM] * na + ([_SEMS, _SEMS] if sems is not None else []) + [_HBM] * len(after)
    args = [pltpu.with_memory_space_constraint(a, pltpu.HBM) for a in arrs] + list(sems or ()) \
        + [pltpu.with_memory_space_constraint(a, pltpu.HBM) for a in after]
    return pl.pallas_call(
        body, name=name, out_shape=tuple(out_shape), in_specs=in_specs, out_specs=tuple(out_specs),
        input_output_aliases={i: first + i for i in range(na)},
        compiler_params=pltpu.CompilerParams(has_side_effects=_EFFECT))(*args)


def _gather2_start(srcs, *, name):
    n = len(srcs)
    lands = [lax.empty((NDEV,) + tuple(s.shape), s.dtype) for s in srcs]

    def body(*refs):
        src_refs, land_refs = refs[:n], refs[n:2 * n]
        send_sems, recv_sems = refs[2 * n], refs[2 * n + 1]
        _, _, _, me = _place()
        for w in range(n):
            for slot, k in enumerate((1,) + _SAME_CORE):
                _gather2_copy(src_refs[w], land_refs[w], send_sems, recv_sems, 4 * w + slot, k, me).start()
        refs[-1][...] = jnp.zeros_like(refs[-1])

    res = _split_call(body, name, list(srcs) + lands, 4 * n, token=True)
    return res[:2], res[2:-1], res[-1]


def _gather2_arrived(sems, arrs, after, *, name):
    n = len(arrs) // 2

    def body(*refs):
        src_refs, land_refs = refs[:n], refs[n:2 * n]
        send_sems, recv_sems = refs[2 * n], refs[2 * n + 1]
        x, y, c, me = _place()
        for w in range(n):
            for slot, k in enumerate((1,) + _SAME_CORE):
                _, pidx = _flip(x, y, c, k)
                _gather2_copy(src_refs[w], land_refs[w], send_sems, recv_sems, 4 * w + slot, k, me).wait_send()
                _gather2_copy(None, land_refs[w], send_sems, recv_sems, 4 * w + slot, k, pidx).wait_recv()

    return _split_call(body, name, arrs, None, sems=sems, after=after)


def _gather2_forward(lands, *, name):
    n = len(lands)

    def body(*refs):
        land_refs = refs[:n]
        send_sems, recv_sems = refs[n], refs[n + 1]
        x, y, c, _ = _place()
        for w in range(n):
            for j, k in enumerate(_SAME_CORE):
                _, pidx = _flip(x, y, c, k)
                _gather2_copy(None, land_refs[w], send_sems, recv_sems, 3 * w + j, 1, pidx).start()
        refs[-1][...] = jnp.zeros_like(refs[-1])

    res = _split_call(body, name, list(lands), 3 * n, token=True)
    return res[:2], res[2:-1], res[-1]


def _gather2_wait(sems, lands, after, *, name):
    n = len(lands)

    def body(*refs):
        land_refs = refs[:n]
        send_sems, recv_sems = refs[n], refs[n + 1]
        x, y, c, _ = _place()
        for w in range(n):
            for j, k in enumerate(_SAME_CORE):
                _, mine = _flip(x, y, c, k)
                _, theirs = _flip(x, y, c, k ^ 1)
                _gather2_copy(None, land_refs[w], send_sems, recv_sems, 3 * w + j, 1, mine).wait_send()
                _gather2_copy(None, land_refs[w], send_sems, recv_sems, 3 * w + j, 1, theirs).wait_recv()

    return _split_call(body, name, list(lands), None, sems=sems, after=after)


def _cast_bf16(arrs, *, name, after=()):
    n = len(arrs)

    def body(*refs):
        for i_ref, o_ref in zip(refs[:n], refs[n + len(after):]):
            o_ref[...] = i_ref[...].astype(BF16)

    return pl.pallas_call(
        body, name=name, out_shape=tuple(jax.ShapeDtypeStruct(a.shape, BF16) for a in arrs),
        in_specs=[pl.BlockSpec(memory_space=pltpu.VMEM)] * n + [pl.BlockSpec(memory_space=pl.ANY)] * len(after),
        compiler_params=_cparams())(*arrs, *after)


def _rs_adam(me_arr, fulls, lands, w, m, v, *, name):
    rows = lands[0].shape[1]
    k = len(fulls)

    def body(me_ref, *refs):
        own_refs, land_refs = refs[:k], refs[k:2 * k]
        w_ref, m_ref, v_ref, g_ref, d_ref, mo_ref, vo_ref = refs[2 * k:]
        me = me_ref[0]
        parts = []
        for own_ref, land_ref in zip(own_refs, land_refs):
            acc = jnp.zeros(own_ref.shape, F32)
            for p in range(NDEV):
                acc = acc + jnp.where(p == me, own_ref[...], land_ref[p].astype(F32))
            parts.append(acc)
        acc = parts[0] if k == 1 else jnp.concatenate(parts, axis=1)
        g_ref[...] = acc
        d_ref[...], mo_ref[...], vo_ref[...] = _adam_math(w_ref[...], acc, m_ref[...], v_ref[...])

    sh = jax.ShapeDtypeStruct((rows, D), F32)
    whole2 = pl.BlockSpec((rows, D), lambda i, me: (0, 0))
    grid_spec = pltpu.PrefetchScalarGridSpec(
        num_scalar_prefetch=1, grid=(1,),
        in_specs=[pl.BlockSpec((rows, f.shape[1]), lambda i, me: (me[0], 0)) for f in fulls]
        + [pl.BlockSpec((NDEV, rows, l.shape[2]), lambda i, me: (0, 0, 0)) for l in lands]
        + [whole2, whole2, whole2],
        out_specs=[whole2] * 4)
    return pl.pallas_call(
        body, name=name, out_shape=(sh, sh, sh, sh), grid_spec=grid_spec, compiler_params=_cparams(1),
    )(me_arr, *fulls, *lands, w, m, v)


ROW_F, ROW_I, ROW_C, ROW_G1, ROW_LG = 0, 8, 16, 24, 32
PACK_ROWS = 40


def _small_sum(me_arr, pack, pack_land, sgw, sgw_land, wmod):
    ncol = wmod.shape[1]

    def body(me_ref, pack_ref, pland_ref, sgw_ref, sland_ref, wmod_ref, sum_ref, all_ref, sgw_sum_ref, part_ref):
        me = me_ref[0]
        acc = jnp.zeros((PACK_ROWS, D), F32)
        acc_w = jnp.zeros(sgw_ref.shape, F32)
        for p in range(NDEV):
            rows = jnp.where(p == me, pack_ref[...], pland_ref[p])
            all_ref[p] = rows
            acc = acc + rows
            acc_w = acc_w + jnp.where(p == me, sgw_ref[...], sland_ref[p])
        sum_ref[...] = acc
        sgw_sum_ref[...] = acc_w
        zero = jnp.zeros((1, D), F32)
        dcmod = jnp.concatenate([acc[ROW_C:ROW_C + 1], acc[ROW_C + 1:ROW_C + 2], zero, zero, zero, zero], axis=1)
        mine = jnp.zeros((1, ncol), F32)
        for d in range(NDEV):
            mine = mine + jnp.where(d == me, dcmod[:, d * ncol:(d + 1) * ncol], 0.0)
        part_ref[...] = _dot_nt(jnp.broadcast_to(mine, (8, ncol)).astype(BF16), wmod_ref[...].astype(BF16))

    vm = pl.BlockSpec(memory_space=pltpu.VMEM)
    return pl.pallas_call(
        body, name="small_sum",
        out_shape=(jax.ShapeDtypeStruct((PACK_ROWS, D), F32), jax.ShapeDtypeStruct((NDEV, PACK_ROWS, D), F32),
                   jax.ShapeDtypeStruct(sgw.shape, F32), jax.ShapeDtypeStruct((8, D), F32)),
        in_specs=[pl.BlockSpec(memory_space=pltpu.SMEM)] + [vm] * 5,
        compiler_params=_cparams(),
    )(me_arr, pack, pack_land, sgw, sgw_land, wmod)


def _cctx_final(me_arr, part, part_land, cctx_row, m_row, v_row):
    def body(me_ref, part_ref, land_ref, c_ref, m_ref, v_ref, g_ref, d_ref, mo_ref, vo_ref):
        me = me_ref[0]
        tot = jnp.zeros((8, D), F32)
        for p in range(NDEV):
            tot = tot + jnp.where(p == me, part_ref[...], land_ref[p])
        cc = c_ref[...]
        _, dsilu = _silu_parts(cc)
        g = tot[0:1, :] * dsilu
        g_ref[...] = g
        d_ref[...], mo_ref[...], vo_ref[...] = _adam_math(cc, g, m_ref[...], v_ref[...])

    row = jax.ShapeDtypeStruct((1, D), F32)
    vm = pl.BlockSpec(memory_space=pltpu.VMEM)
    return pl.pallas_call(
        body, name="cctx_final", out_shape=(row, row, row, row),
        in_specs=[pl.BlockSpec(memory_space=pltpu.SMEM)] + [vm] * 5,
        compiler_params=_cparams(),
    )(me_arr, part, part_land, cctx_row, m_row, v_row)


def _adam_small(s, sgw_g, params):
    names = ["norm1", "norm2", "norm_f", "sg_gain", "sg_b", "ret_logit_f", "ret_logit_b", "b_mod", "sg_w"]
    flat = [a for n in names for a in params[n]]

    def body(*refs):
        s_ref, sgw_ref = refs[0], refs[1]
        p_refs = refs[2:2 + 3 * len(names)]
        o_refs = refs[2 + 3 * len(names):]
        loss_ref = o_refs[-1]

        def row(r):
            return s_ref[r:r + 1, :]

        grads = {
            "norm1": row(ROW_I + 2) + row(ROW_C + 2),
            "norm2": row(ROW_F + 4),
            "norm_f": row(ROW_F + 0),
            "sg_gain": s_ref[ROW_I + 3:ROW_I + 4, 0:AW],
            "sg_b": s_ref[ROW_I + 4:ROW_I + 8, 0:DH],
            "sg_w": sgw_ref[...],
        }
        for j, n in enumerate(names):
            w_ref, m_ref, v_ref = p_refs[3 * j:3 * j + 3]
            g_ref, d_ref, mo_ref, vo_ref = o_refs[4 * j:4 * j + 4]
            w = w_ref[...]
            if n in ("ret_logit_f", "ret_logit_b"):
                r = ROW_LG + (0 if n == "ret_logit_f" else 1)
                g = s_ref[r:r + 1, 0:H] * _sigmoid(-w)
            elif n == "b_mod":
                rows = [row(ROW_I + 0) + row(ROW_C + 0), row(ROW_I + 1) + row(ROW_C + 1), row(ROW_G1),
                        row(ROW_F + 2), row(ROW_F + 3), row(ROW_F + 1)]
                g = jnp.concatenate(rows, axis=1)
            else:
                g = grads[n]
            g_ref[...] = g
            d_ref[...], mo_ref[...], vo_ref[...] = _adam_math(w, g, m_ref[...], v_ref[...])
        loss_ref[...] = jnp.full((1, 1), 0.5 / D, F32) * jnp.sum(row(ROW_F + 5), axis=1, keepdims=True)

    out_shape = []
    for n in names:
        w = params[n][0]
        out_shape += [jax.ShapeDtypeStruct(w.shape, F32)] * 4
    out_shape.append(jax.ShapeDtypeStruct((1, 1), F32))
    outs = pl.pallas_call(body, name="adam_small", out_shape=tuple(out_shape), compiler_params=_cparams())(
        s, sgw_g, *flat)
    return {n: tuple(outs[4 * j:4 * j + 4]) for j, n in enumerate(names)}, outs[-1]


def _wmod_grad(cs_all, dmod_cols, wmod, m, v):
    ncol = wmod.shape[1]

    def body(cs_ref, dm_ref, w_ref, m_ref, v_ref, g_ref, d_ref, mo_ref, vo_ref):
        g = _dot_tn(cs_ref[...].astype(BF16), dm_ref[...].astype(BF16))
        g_ref[...] = g
        d_ref[...], mo_ref[...], vo_ref[...] = _adam_math(w_ref[...], g, m_ref[...], v_ref[...])

    sh = jax.ShapeDtypeStruct((D, ncol), F32)
    return pl.pallas_call(
        body, name="wmod_grad", out_shape=(sh, sh, sh, sh),
        compiler_params=_cparams(),
    )(cs_all, dmod_cols, wmod, m, v)


def _rope_tables(t_len):
    n_freq = DH // 4
    inv = (ROPE_BASE ** (-np.arange(n_freq, dtype=np.float32) / n_freq)).astype(np.float32)
    tok = np.arange(t_len)
    rows = (tok // GRID_W).astype(np.float32)
    cols = (tok % GRID_W).astype(np.float32)
    ang_r = rows[:, None] * inv[None, :]
    ang_c = cols[:, None] * inv[None, :]
    cos = np.concatenate([np.cos(ang_r)] * 2 + [np.cos(ang_c)] * 2, axis=1).astype(np.float32)
    sin = np.concatenate([-np.sin(ang_r), np.sin(ang_r), -np.sin(ang_c), np.sin(ang_c)], axis=1).astype(np.float32)
    return jnp.asarray(cos), jnp.asarray(sin)


def _local_step(x, tgt, ctx, mod6, cmod6, norm1, norm2, normf, win, wout, ffn_weights, sgw, sgb, sggain, rlf, rlb,
                *, tm_a, tm_b, tm_f, tm_m, tm_r, tm_i, bt_w, after_first_grads=None, small_path=None,
                after_in_half=None):
    t_len = x.shape[0]
    cos, sin = _rope_tables(t_len)
    sgbt = jnp.broadcast_to(sgb[:, :, None], (G, C, 128))
    rlf = jnp.broadcast_to(rlf.reshape(H, 1), (H, 128))
    rlb = jnp.broadcast_to(rlb.reshape(H, 1), (H, 128))
    hc, kvc, scf, scb = _ctx_fwd(ctx, cmod6, norm1, win, rlf, rlb)
    hx, zuv, q, k, v, gfb, of, ya, sst = _fwd_a(x, mod6, norm1, win, sgw, sgbt, sggain, cos, sin, rlf, scf, tm=tm_a)
    if callable(wout):
        wout, tok = wout(hx)
        rlb = rlb + tok
    ob, ycat, y, x1, rst = _fwd_b(q, k, v, of, gfb, ya, x, mod6, wout, rlb, scb, tm=tm_b)
    wg, wu, wd = ffn_weights(x1) if callable(ffn_weights) else ffn_weights
    dx1, h2, da, db, hm, df, small_f = _ffn(x1, tgt, mod6, norm2, normf, wg, wu, wd, tm=tm_f)
    bt2 = min(2 * bt_w, t_len)
    d_wd, d_wd_b = _tn_matmul(hm, df, bm=DFF // 2, bt=bt2, name="dw_down")
    d_wg, d_wg_b = _tn_matmul(da, h2, bm=DFF // 2, bt=bt2, name="dw_gate")
    d_wu, d_wu_b = _tn_matmul(db, h2, bm=DFF // 2, bt=bt2, name="dw_up")
    dy, dya, dgfb, dof, dob, dg1 = _mid_bwd(dx1, y, of, ob, gfb, mod6, wout, tm=tm_m)
    d_wo, d_wo_b = _tn_matmul(ycat, dy, bm=D, bt=bt2, name="dw_out", transposed=False)
    if after_first_grads is not None:
        rlf = rlf + after_first_grads((d_wd_b, d_wg_b, d_wu_b, d_wo_b))
    dqf, dkf, dvf, dqb, dkb, dvb, dscf, dscb, dlg = _ret_bwd(q, k, v, dof, dob, sst, rst, rlf, rlb, tm=tm_r)
    gx, dz, small_i, dsgw = _in_bwd(x, zuv, dya, dqf, dkf, dvf, dqb, dkb, dvb, dgfb, dx1, cos, sin,
                                    mod6, norm1, win, sgw, sgbt, sggain, tm=tm_i)
    dwin_c, small_c, dlg = _ctx_bwd(ctx, hc, kvc, cmod6, norm1, win, rlf, rlb, dscf, dscb, dlg)
    pack = jnp.concatenate([small_f, small_i, small_c, dg1, dlg], axis=0)
    after = small_path(pack, dsgw) if small_path is not None else ()
    halves = []
    for j in range(2):
        halves.append(_tn_matmul(dz, hx, bm=INC // 2, bt=bt2, name="dw_in_%d" % j, init=dwin_c,
                                 init_rows=(KV_LO, KV_HI), after=after, cols=(j, D // 2), transposed=False))
        if after_in_half is not None:
            after = after_in_half(j, halves[-1][1])
    grads = {"w_in": halves, "w_out": (d_wo, d_wo_b), "w_gate": (d_wg, d_wg_b), "w_up": (d_wu, d_wu_b),
             "w_down": (d_wd, d_wd_b)}
    return gx, grads, pack, dsgw


def kernel(x, c, ctx, c_ctx, w_mod, b_mod, norm1, w_in, sg_gain, sg_w, sg_b, ret_logit_f, ret_logit_b, w_out, norm2, w_gate, w_up, w_down, norm_f, loss_target, m_c_ctx, m_w_mod, m_b_mod, m_norm1, m_w_in, m_sg_gain, m_sg_w, m_sg_b, m_ret_logit_f, m_ret_logit_b, m_w_out, m_norm2, m_w_gate, m_w_up, m_w_down, m_norm_f, v_c_ctx, v_w_mod, v_b_mod, v_norm1, v_w_in, v_sg_gain, v_sg_w, v_sg_b, v_ret_logit_f, v_ret_logit_b, v_w_out, v_norm2, v_w_gate, v_w_up, v_w_down, v_norm_f):
    t_len = x.shape[1]
    tm = min(512, t_len)
    me = 4 * lax.axis_index("x") + 2 * lax.axis_index("y") + lax.axis_index("c")
    tr = lambda a: jnp.transpose(a[0])

    cctx_row = c_ctx.reshape(1, D)
    mod_all, cs_all, g_in = _entry_gather(c, cctx_row, w_mod[0], b_mod, [tr(w_in)])
    mod6 = lax.dynamic_slice(mod_all, (me, 0), (1, 6 * D)).reshape(6, D)
    cmod6 = mod_all[8].reshape(6, D)
    win_t = g_in.reshape(INC, D)

    (out_shard,) = _cast_bf16([w_out[0]], name="cast_out", after=[g_in])
    h_out, tok_out = _exchange_start([out_shard], scatter=False, name="wout_start")
    ffn_shards = _cast_bf16([tr(w_gate), tr(w_up), w_down[0]], name="cast_ffn", after=[h_out[2]])
    sems_ffn, arrs_ffn, tok = _gather2_start(ffn_shards, name="wffn_start")
    mod6 = mod6 + (tok_out[0, 0] + tok[0, 0])
    ffn = {}

    def place_own(own, lands, rows):
        return [lax.dynamic_update_slice(l, o[None], (me, 0, 0)).reshape(rows, D) for o, l in zip(own, lands)]

    def wout(after):
        own, lands = _exchange_wait(h_out, [after], scatter=False, name="wout_wait")
        arrs = _gather2_arrived(sems_ffn, arrs_ffn, [after], name="wffn_arrived")
        ffn["own"] = arrs[:3]
        ffn["sems"], ffn["lands"], tok_fwd = _gather2_forward(arrs[3:], name="wffn_forward")
        return place_own(own, lands, D)[0], tok_fwd[0, 0]

    def ffn_weights(after):
        lands = _gather2_wait(ffn["sems"], ffn["lands"], [after], name="wffn_wait")
        return place_own(ffn["own"], lands, DFF)

    rs, outs = {}, {}

    def after_first_grads(bf):
        rs["first"], tok_first = _exchange_start([b.reshape(NDEV, b.shape[0] // NDEV, D) for b in bf], scatter=True,
                                                 name="rs_first_start")
        return tok_first[0, 0]

    def small_path(pack, dsgw):
        rs["small"], _ = _exchange_start([pack, dsgw.reshape(G * C, C)], scatter=False, name="small_start")
        return [rs["small"][2], rs["small"][3]]

    def after_in_half(j, half_bf16):
        rs["in%d" % j], _ = _exchange_start([half_bf16.reshape(NDEV, INC // NDEV, D // 2)], scatter=True,
                                            name="rs_in%d_start" % j)
        return [rs["in%d" % j][2]]

    gx, grads, pack, dsgw = _local_step(
        x[0], loss_target[0], ctx[0], mod6, cmod6, norm1, norm2[0:1], norm_f.reshape(1, D), win_t, wout, ffn_weights,
        sg_w[0], sg_b[0], sg_gain, ret_logit_f, ret_logit_b,
        tm_a=tm, tm_b=tm, tm_f=min(256, t_len), tm_m=min(1024, t_len), tm_r=tm, tm_i=tm, bt_w=min(1024, t_len),
        after_first_grads=after_first_grads, small_path=small_path, after_in_half=after_in_half)

    me_arr = me.reshape(1).astype(jnp.int32)
    (pack_own, sgw_own), (pack_land, sgw_land) = _exchange_wait(rs["small"], [rs["in1"][2]], scatter=False,
                                                               name="small_wait")
    psum_rows, pall, sgw_sum, part = _small_sum(me_arr, pack_own, pack_land, sgw_own, sgw_land, w_mod[0])
    h_cc, _ = _exchange_start([part], scatter=False, name="cctx_start")

    dmod_rows = (ROW_I + 0, ROW_I + 1, ROW_G1, ROW_F + 2, ROW_F + 3, ROW_F + 1)
    dmod_all = jnp.concatenate([pall[:, r, :] for r in dmod_rows], axis=1)
    dcmod = jnp.concatenate([psum_rows[ROW_C + 0:ROW_C + 1], psum_rows[ROW_C + 1:ROW_C + 2],
                             jnp.zeros((1, 4 * D), F32)], axis=1)
    dmod_mat = jnp.concatenate([dmod_all, jnp.pad(dcmod, ((0, 7), (0, 0)))], axis=0)
    ncol = 6 * D // NDEV
    dmod_cols = lax.dynamic_slice(dmod_mat, (0, me * ncol), (16, ncol))
    g_wm, d_wm, m_wm, v_wm = _wmod_grad(cs_all, dmod_cols, w_mod[0], m_w_mod[0], v_w_mod[0])
    outs["w_mod"] = (g_wm[None], d_wm[None], m_wm[None], v_wm[None])
    small_params = {
        "norm1": (norm1, m_norm1, v_norm1), "norm2": (norm2, m_norm2, v_norm2),
        "norm_f": tuple(a.reshape(1, D) for a in (norm_f, m_norm_f, v_norm_f)),
        "sg_gain": (sg_gain, m_sg_gain, v_sg_gain), "sg_b": (sg_b[0], m_sg_b[0], v_sg_b[0]),
        "ret_logit_f": (ret_logit_f, m_ret_logit_f, v_ret_logit_f),
        "ret_logit_b": (ret_logit_b, m_ret_logit_b, v_ret_logit_b),
        "b_mod": (b_mod, m_b_mod, v_b_mod), "sg_w": (sg_w[0], m_sg_w[0], v_sg_w[0])}
    small, loss = _adam_small(psum_rows, sgw_sum.reshape(G, C, C), small_params)
    back = {"norm_f": lambda a: a.reshape(D), "sg_b": lambda a: a[None], "sg_w": lambda a: a[None]}
    for name, vals in small.items():
        outs[name] = tuple(back.get(name, lambda a: a)(a) for a in vals)

    _, lands_first = _exchange_wait(rs["first"], [g_wm], scatter=True, name="rs_first_wait")

    def finish(name, fulls, lands, w, m, v, transposed):
        take = tr if transposed else (lambda a: a[0])
        res = _rs_adam(me_arr, fulls, lands, take(w), take(m), take(v), name="adam_" + name)
        put = (lambda a: jnp.transpose(a)[None]) if transposed else (lambda a: a[None])
        outs[name] = tuple(put(a) for a in res)
        return res[0]

    g_down = finish("w_down", [grads["w_down"][0]], [lands_first[0]], w_down, m_w_down, v_w_down, False)
    g_gate = finish("w_gate", [grads["w_gate"][0]], [lands_first[1]], w_gate, m_w_gate, v_w_gate, True)
    g_up = finish("w_up", [grads["w_up"][0]], [lands_first[2]], w_up, m_w_up, v_w_up, True)
    g_out = finish("w_out", [grads["w_out"][0]], [lands_first[3]], w_out, m_w_out, v_w_out, False)
    done = [g_down, g_gate, g_up, g_out]
    (part_own,), (part_land,) = _exchange_wait(h_cc, done, scatter=False, name="cctx_wait")
    res_cc = _cctx_final(me_arr, part_own, part_land, cctx_row, m_c_ctx.reshape(1, D), v_c_ctx.reshape(1, D))
    outs["c_ctx"] = tuple(a.reshape(D) for a in res_cc)
    lands_in = [_exchange_wait(rs["in%d" % j], done, scatter=True, name="rs_in%d_wait" % j)[1][0] for j in range(2)]
    finish("w_in", [h[0] for h in grads["w_in"]], lands_in, w_in, m_w_in, v_w_in, True)

    order = ["c_ctx", "w_mod", "b_mod", "norm1", "w_in", "sg_gain", "sg_w", "sg_b", "ret_logit_f", "ret_logit_b",
             "w_out", "norm2", "w_gate", "w_up", "w_down", "norm_f"]
    res = [loss.reshape(()), gx[None]]
    for j in range(4):
        res += [outs[name][j] for name in order]
    return tuple(res)
```

```python
import functools
import math

import numpy as np
import jax
import jax.numpy as jnp
from jax import lax
from jax.experimental import pallas as pl
from jax.experimental.pallas import tpu as pltpu

F32 = jnp.float32
BF16 = jnp.bfloat16

D = 1024
AW = 512
RW = 512
H = 4
DH = 128
G = 4
C = 128
CR = 256
DFF = 2816
INC = 3584
Q_LO = 2 * AW
KV_LO, VR_LO, G_LO = Q_LO + RW, Q_LO + 2 * RW, Q_LO + 3 * RW
KV_HI = G_LO
NDEV = 8
EPS = 1e-6
KSCALE = DH ** -0.5
ROPE_BASE = 10000.0
GRID_W = 64

ADAM_LR = 0.001
ADAM_B1 = 0.9
ADAM_B2 = 0.999
ADAM_EPS = 1e-08
ADAM_WD = 0.01
ADAM_STEP = 10

VMEM_LIMIT = 56 * 1024 * 1024
MESH = pl.DeviceIdType.MESH


def _cparams(n_grid=0, **kw):
    sem = ("arbitrary",) * n_grid if n_grid else None
    return pltpu.CompilerParams(dimension_semantics=sem, vmem_limit_bytes=VMEM_LIMIT, **kw)


def _dot(a, b):
    return jnp.dot(a, b, preferred_element_type=F32)


def _dot_nt(a, b):
    return lax.dot_general(a, b, (((1,), (1,)), ((), ())), preferred_element_type=F32)


def _dot_tn(a, b):
    return lax.dot_general(a, b, (((0,), (0,)), ((), ())), preferred_element_type=F32)


def _whole(shape):
    nd = len(shape)
    return pl.BlockSpec(shape, lambda *_: (0,) * nd, pipeline_mode=pl.Buffered(1))


def _acc(shape):
    nd = len(shape)
    return pl.BlockSpec(shape, lambda *_: (0,) * nd)


def _rows(tm, n):
    return pl.BlockSpec((tm, n), lambda i: (i, 0))


def _rows_rev(tm, n, nt):
    return pl.BlockSpec((tm, n), lambda i: (nt - 1 - i, 0))


def _cols(n, tm):
    return pl.BlockSpec((n, tm), lambda i: (0, i))


def _sigmoid(x):
    return 1.0 / (1.0 + jnp.exp(-x))


def _log_sigmoid(x):
    return jnp.minimum(x, 0.0) - jnp.log(1.0 + jnp.exp(-jnp.abs(x)))


_GK0 = math.sqrt(2.0 / math.pi)
_GK1 = 0.044715


def _gelu(x):
    x2 = x * x
    t = jnp.tanh(x * (_GK0 + (_GK0 * _GK1) * x2))
    h = 0.5 + 0.5 * t
    g = x * h
    dg = h + g * (1.0 - h) * ((2.0 * _GK0) + (6.0 * _GK0 * _GK1) * x2)
    return g, dg


def _silu_parts(a):
    s = _sigmoid(a)
    sa = a * s
    return sa, s + sa * (1.0 - s)


def _rope(t, cos, sins):
    n = t.shape[1]
    lane = lax.broadcasted_iota(jnp.int32, t.shape, 1)
    first = (lane & 63) < 32
    partner = jnp.where(first, pltpu.roll(t, n - 32, 1), pltpu.roll(t, 32, 1))
    return t * cos + partner * sins


def _headnorm(o):
    outs, rs = [], []
    for h in range(H):
        oh = o[:, h * DH:(h + 1) * DH]
        r = lax.rsqrt(jnp.mean(oh * oh, axis=-1, keepdims=True) + EPS)
        outs.append(oh * r)
        rs.append(r)
    return jnp.concatenate(outs, axis=1), rs


def _decay_consts(lg, forward):
    ii = lax.broadcasted_iota(jnp.int32, (CR, CR), 0).astype(F32)
    jj = lax.broadcasted_iota(jnp.int32, (CR, CR), 1).astype(F32)
    ic = lax.broadcasted_iota(jnp.int32, (CR, 1), 0).astype(F32)
    if forward:
        diff = ii - jj
        xi = jnp.exp(lg * (ic + 1.0))
        z = jnp.exp(lg * (CR - 1.0 - ic))
    else:
        diff = jj - ii
        xi = jnp.exp(lg * (CR - ic))
        z = jnp.exp(lg * ic)
    dm = jnp.where(diff >= 0.0, jnp.exp(lg * jnp.maximum(diff, 0.0)), 0.0)
    dec = jnp.exp(lg * float(CR))
    return dm, xi, z, dec, ic


def _ctx_fwd(ctx, cmod6, norm1, win, rlf, rlb):
    lc = ctx.shape[0]

    def body(ctx_ref, cmod_ref, n1_ref, win_ref, rlf_ref, rlb_ref, hc_ref, kvc_ref, scf_ref, scb_ref):
        x = ctx_ref[...]
        r = lax.rsqrt(jnp.mean(x * x, axis=-1, keepdims=True) + EPS)
        hc = (x * r) * (n1_ref[...] * (1.0 + cmod_ref[1:2, :])) + cmod_ref[0:1, :]
        hcb = hc.astype(BF16)
        hc_ref[...] = hcb
        kv = _dot_nt(hcb, win_ref[KV_LO:KV_HI, :])
        k = kv[:, :RW] * KSCALE
        v = kv[:, RW:]
        kvc_ref[:, :RW] = k
        kvc_ref[:, RW:] = v
        t = lax.broadcasted_iota(jnp.int32, (lc, 1), 0).astype(F32)
        lgf = _log_sigmoid(rlf_ref[...])
        lgb = _log_sigmoid(rlb_ref[...])
        for h in range(H):
            hs = slice(h * DH, (h + 1) * DH)
            wf = jnp.exp(lgf[h:h + 1, 0:1] * (lc - 1.0 - t))
            wb = jnp.exp(lgb[h:h + 1, 0:1] * t)
            vh = v[:, hs].astype(BF16)
            scf_ref[h] = _dot_tn((k[:, hs] * wf).astype(BF16), vh)
            scb_ref[h] = _dot_tn((k[:, hs] * wb).astype(BF16), vh)

    return pl.pallas_call(
        body, name="ctx_fwd",
        out_shape=(jax.ShapeDtypeStruct((lc, D), BF16), jax.ShapeDtypeStruct((lc, 2 * RW), F32),
                   jax.ShapeDtypeStruct((H, DH, DH), F32), jax.ShapeDtypeStruct((H, DH, DH), F32)),
        compiler_params=_cparams(),
    )(ctx, cmod6, norm1, win, rlf, rlb)


def _sg_forward(u, v, sgw_ref, sgbt_ref, sgg_ref, nc):
    ua, dgu = _gelu(u)
    va, dgv = _gelu(v)
    gain = sgg_ref[...]
    mixed, vn_b, vah_l, rv_l = [], [], [], []
    for g in range(G):
        gs = slice(g * DH, (g + 1) * DH)
        vag = va[:, gs]
        rv = lax.rsqrt(jnp.mean(vag * vag, axis=-1, keepdims=True) + EPS)
        vah = vag * rv
        vn = (vah * gain[:, gs]).astype(BF16)
        wg = sgw_ref[g].astype(BF16)
        bcol = sgbt_ref[g]
        rows = [_dot(wg, vn[c * C:(c + 1) * C, :]) + bcol for c in range(nc)]
        mixed.append(jnp.concatenate(rows, axis=0) if nc > 1 else rows[0])
        vn_b.append(vn)
        vah_l.append(vah)
        rv_l.append(rv)
    mixed = jnp.concatenate(mixed, axis=1)
    return ua * mixed, (ua, dgu, dgv, mixed, vn_b, vah_l, rv_l)


def _fwd_a(x, mod6, norm1, win, sgw, sgbt, sggain, cos, sin, rlf, scf, *, tm):
    t_len = x.shape[0]
    nt, nc, ncr = t_len // tm, tm // C, tm // CR

    def body(x_ref, mod_ref, n1_ref, win_ref, sgw_ref, sgbt_ref, sgg_ref, cos_ref, sin_ref, rlf_ref, scf_ref,
             hx_ref, zuv_ref, q_ref, k_ref, v_ref, gfb_ref, of_ref, ya_ref, sst_ref, s_scr):
        i = pl.program_id(0)

        @pl.when(i == 0)
        def _():
            s_scr[...] = scf_ref[...]

        x = x_ref[...]
        r = lax.rsqrt(jnp.mean(x * x, axis=-1, keepdims=True) + EPS)
        hx = (x * r) * (n1_ref[...] * (1.0 + mod_ref[1:2, :])) + mod_ref[0:1, :]
        hxb = hx.astype(BF16)
        hx_ref[...] = hxb
        z = _dot_nt(hxb, win_ref[...])
        zuv_ref[...] = z[:, :2 * AW].astype(BF16)
        gfb_ref[...] = z[:, G_LO:INC].astype(BF16)
        cos = jnp.tile(cos_ref[...], (1, H))
        sins = jnp.tile(sin_ref[...], (1, H))
        q_ref[...] = _rope(z[:, Q_LO:KV_LO], cos, sins).astype(BF16)
        k_ref[...] = (_rope(z[:, KV_LO:VR_LO], cos, sins) * KSCALE).astype(BF16)
        v_ref[...] = z[:, VR_LO:G_LO].astype(BF16)
        ya, _ = _sg_forward(z[:, :AW], z[:, AW:2 * AW], sgw_ref, sgbt_ref, sgg_ref, nc)
        ya_ref[...] = ya.astype(BF16)

        lg = _log_sigmoid(rlf_ref[...])
        for h in range(H):
            dm, xi, zc, dec, _ = _decay_consts(lg[h:h + 1, 0:1], True)
            hs = slice(h * DH, (h + 1) * DH)
            for c in range(ncr):
                rs = slice(c * CR, (c + 1) * CR)
                qc, kc, vc = q_ref[rs, hs], k_ref[rs, hs], v_ref[rs, hs]
                s = s_scr[h]
                sst_ref[c, h] = s
                a = (_dot_nt(qc, kc) * dm).astype(BF16)
                of_ref[rs, hs] = (_dot(a, vc) + xi * _dot(qc, s.astype(BF16))).astype(BF16)
                kz = (kc.astype(F32) * zc).astype(BF16)
                s_scr[h] = dec * s + _dot_tn(kz, vc)

    n_chunks = t_len // CR
    return pl.pallas_call(
        body, name="fwd_a", grid=(nt,),
        in_specs=[_rows(tm, D), _whole((6, D)), _whole((1, D)), _whole((INC, D)), _whole((G, C, C)),
                  _whole((G, C, 128)), _whole((1, AW)), _rows(tm, DH), _rows(tm, DH), _whole((H, 128)),
                  _whole((H, DH, DH))],
        out_specs=[_rows(tm, D), _rows(tm, 2 * AW), _rows(tm, RW), _rows(tm, RW), _rows(tm, RW),
                   _rows(tm, 2 * RW), _rows(tm, RW), _rows(tm, AW),
                   pl.BlockSpec((ncr, H, DH, DH), lambda i: (i, 0, 0, 0))],
        out_shape=(jax.ShapeDtypeStruct((t_len, D), BF16), jax.ShapeDtypeStruct((t_len, 2 * AW), BF16),
                   jax.ShapeDtypeStruct((t_len, RW), BF16), jax.ShapeDtypeStruct((t_len, RW), BF16),
                   jax.ShapeDtypeStruct((t_len, RW), BF16), jax.ShapeDtypeStruct((t_len, 2 * RW), BF16),
                   jax.ShapeDtypeStruct((t_len, RW), BF16), jax.ShapeDtypeStruct((t_len, AW), BF16),
                   jax.ShapeDtypeStruct((n_chunks, H, DH, DH), F32)),
        scratch_shapes=[pltpu.VMEM((H, DH, DH), F32)],
        compiler_params=_cparams(1),
    )(x, mod6, norm1, win, sgw, sgbt, sggain, cos, sin, rlf, scf)


def _fwd_b(q, k, v, of, gfb, ya, x, mod6, wout, rlb, scb, *, tm):
    t_len = x.shape[0]
    nt, nc = t_len // tm, tm // CR

    def body(q_ref, k_ref, v_ref, of_ref, gfb_ref, ya_ref, x_ref, mod_ref, wout_ref, rlb_ref, scb_ref,
             ob_ref, ycat_ref, y_ref, x1_ref, rst_ref, r_scr):
        i = pl.program_id(0)

        @pl.when(i == 0)
        def _():
            r_scr[...] = scb_ref[...]

        lg = _log_sigmoid(rlb_ref[...])
        for h in range(H):
            dm, xi, zc, dec, _ = _decay_consts(lg[h:h + 1, 0:1], False)
            hs = slice(h * DH, (h + 1) * DH)
            for c in reversed(range(nc)):
                rs = slice(c * CR, (c + 1) * CR)
                qc, kc, vc = q_ref[rs, hs], k_ref[rs, hs], v_ref[rs, hs]
                s = r_scr[h]
                rst_ref[c, h] = s
                a = (_dot_nt(qc, kc) * dm).astype(BF16)
                ob_ref[rs, hs] = (_dot(a, vc) + xi * _dot(qc, s.astype(BF16))).astype(BF16)
                kz = (kc.astype(F32) * zc).astype(BF16)
                r_scr[h] = dec * s + _dot_tn(kz, vc)

        gfb = gfb_ref[...].astype(F32)
        sf, _ = _silu_parts(gfb[:, :RW])
        sb, _ = _silu_parts(gfb[:, RW:])
        hnf, _ = _headnorm(of_ref[...].astype(F32))
        hnb, _ = _headnorm(ob_ref[...].astype(F32))
        yr = sf * hnf + sb * hnb
        ycat = jnp.concatenate([ya_ref[...], yr.astype(BF16)], axis=1)
        ycat_ref[...] = ycat
        y = _dot(ycat, wout_ref[...])
        y_ref[...] = y.astype(BF16)
        x1_ref[...] = x_ref[...] + mod_ref[2:3, :] * y

    n_chunks = t_len // CR
    rr = functools.partial(_rows_rev, nt=nt)
    return pl.pallas_call(
        body, name="fwd_b", grid=(nt,),
        in_specs=[rr(tm, RW), rr(tm, RW), rr(tm, RW), rr(tm, RW), rr(tm, 2 * RW), rr(tm, AW), rr(tm, D),
                  _whole((6, D)), _whole((D, D)), _whole((H, 128)), _whole((H, DH, DH))],
        out_specs=[rr(tm, RW), rr(tm, D), rr(tm, D), rr(tm, D),
                   pl.BlockSpec((nc, H, DH, DH), lambda i: (nt - 1 - i, 0, 0, 0))],
        out_shape=(jax.ShapeDtypeStruct((t_len, RW), BF16), jax.ShapeDtypeStruct((t_len, D), BF16),
                   jax.ShapeDtypeStruct((t_len, D), BF16), jax.ShapeDtypeStruct((t_len, D), F32),
                   jax.ShapeDtypeStruct((n_chunks, H, DH, DH), F32)),
        scratch_shapes=[pltpu.VMEM((H, DH, DH), F32)],
        compiler_params=_cparams(1),
    )(q, k, v, of, gfb, ya, x, mod6, wout, rlb, scb)


def _ffn(x1, tgt, mod6, norm2, normf, wg, wu, wd, *, tm):
    t_len = x1.shape[0]
    nt = t_len // tm

    def body(x1_ref, tgt_ref, mod_ref, n2_ref, nf_ref, wg_ref, wu_ref, wd_ref,
             dx1_ref, h2_ref, da_ref, db_ref, hm_ref, df_ref, small_ref):
        i = pl.program_id(0)

        @pl.when(i == 0)
        def _():
            small_ref[...] = jnp.zeros_like(small_ref)

        sh2, sc2, g2 = mod_ref[3:4, :], mod_ref[4:5, :], mod_ref[5:6, :]
        n2, nf = n2_ref[...], nf_ref[...]
        x1 = x1_ref[...]
        r2 = lax.rsqrt(jnp.mean(x1 * x1, axis=-1, keepdims=True) + EPS)
        x1h = x1 * r2
        w2 = n2 * (1.0 + sc2)
        h2b = (x1h * w2 + sh2).astype(BF16)
        h2_ref[...] = h2b
        a = _dot_nt(h2b, wg_ref[...])
        b = _dot_nt(h2b, wu_ref[...])
        sa, dsa = _silu_parts(a)
        hmb = (sa * b).astype(BF16)
        hm_ref[...] = hmb.T
        f = _dot(hmb, wd_ref[...])
        x2 = x1 + g2 * f
        r3 = lax.rsqrt(jnp.mean(x2 * x2, axis=-1, keepdims=True) + EPS)
        x2h = x2 * r3
        diff = x2h * nf - tgt_ref[...]
        small_ref[5:6, :] += jnp.sum(diff * diff, axis=0, keepdims=True)
        dout = diff * (1.0 / D)
        small_ref[0:1, :] += jnp.sum(dout * x2h, axis=0, keepdims=True)
        dyg = dout * nf
        dx2 = r3 * (dyg - x2h * jnp.mean(dyg * x2h, axis=-1, keepdims=True))
        small_ref[1:2, :] += jnp.sum(dx2 * f, axis=0, keepdims=True)
        dfb = (dx2 * g2).astype(BF16)
        df_ref[...] = dfb
        dhm = _dot_nt(dfb, wd_ref[...])
        dbb = (dhm * sa).astype(BF16)
        dab = (dhm * b * dsa).astype(BF16)
        da_ref[...] = dab.T
        db_ref[...] = dbb.T
        dh2 = _dot(dab, wg_ref[...]) + _dot(dbb, wu_ref[...])
        small_ref[2:3, :] += jnp.sum(dh2, axis=0, keepdims=True)
        dhx = dh2 * x1h
        small_ref[3:4, :] += jnp.sum(dhx, axis=0, keepdims=True) * n2
        small_ref[4:5, :] += jnp.sum(dhx, axis=0, keepdims=True) * (1.0 + sc2)
        dyg2 = dh2 * w2
        dx1_ref[...] = dx2 + r2 * (dyg2 - x1h * jnp.mean(dyg2 * x1h, axis=-1, keepdims=True))

    return pl.pallas_call(
        body, name="ffn", grid=(nt,),
        in_specs=[_rows(tm, D), _rows(tm, D), _whole((6, D)), _whole((1, D)), _whole((1, D)),
                  _whole((DFF, D)), _whole((DFF, D)), _whole((DFF, D))],
        out_specs=[_rows(tm, D), _rows(tm, D), _cols(DFF, tm), _cols(DFF, tm), _cols(DFF, tm), _rows(tm, D),
                   _acc((8, D))],
        out_shape=(jax.ShapeDtypeStruct((t_len, D), F32), jax.ShapeDtypeStruct((t_len, D), BF16),
                   jax.ShapeDtypeStruct((DFF, t_len), BF16), jax.ShapeDtypeStruct((DFF, t_len), BF16),
                   jax.ShapeDtypeStruct((DFF, t_len), BF16), jax.ShapeDtypeStruct((t_len, D), BF16),
                   jax.ShapeDtypeStruct((8, D), F32)),
        compiler_params=_cparams(1),
    )(x1, tgt, mod6, norm2, normf, wg, wu, wd)


def _mid_bwd(dx1, y, of, ob, gfb, mod6, wout, *, tm):
    t_len = dx1.shape[0]
    nt = t_len // tm

    def body(dx1_ref, y_ref, of_ref, ob_ref, gfb_ref, mod_ref, wout_ref,
             dy_ref, dya_ref, dgfb_ref, dof_ref, dob_ref, dg1_ref):
        i = pl.program_id(0)

        @pl.when(i == 0)
        def _():
            dg1_ref[...] = jnp.zeros_like(dg1_ref)

        dx1 = dx1_ref[...]
        dg1_ref[0:1, :] += jnp.sum(dx1 * y_ref[...].astype(F32), axis=0, keepdims=True)
        dyb = (dx1 * mod_ref[2:3, :]).astype(BF16)
        dy_ref[...] = dyb
        dycat = _dot_nt(dyb, wout_ref[...])
        dya_ref[...] = dycat[:, :AW].astype(BF16)
        dyr = dycat[:, AW:]
        gfb = gfb_ref[...].astype(F32)
        for o_ref, do_ref, lo in ((of_ref, dof_ref, 0), (ob_ref, dob_ref, RW)):
            sg, dsg = _silu_parts(gfb[:, lo:lo + RW])
            o = o_ref[...].astype(F32)
            hn, rs = _headnorm(o)
            dgfb_ref[:, lo:lo + RW] = (dyr * hn * dsg).astype(BF16)
            dhn = dyr * sg
            for h in range(H):
                hs = slice(h * DH, (h + 1) * DH)
                oh, dh = hn[:, hs], dhn[:, hs]
                do_ref[:, hs] = (rs[h] * (dh - oh * jnp.mean(dh * oh, axis=-1, keepdims=True))).astype(BF16)

    return pl.pallas_call(
        body, name="mid_bwd", grid=(nt,),
        in_specs=[_rows(tm, D), _rows(tm, D), _rows(tm, RW), _rows(tm, RW), _rows(tm, 2 * RW),
                  _whole((6, D)), _whole((D, D))],
        out_specs=[_rows(tm, D), _rows(tm, AW), _rows(tm, 2 * RW), _rows(tm, RW), _rows(tm, RW), _acc((8, D))],
        out_shape=(jax.ShapeDtypeStruct((t_len, D), BF16), jax.ShapeDtypeStruct((t_len, AW), BF16),
                   jax.ShapeDtypeStruct((t_len, 2 * RW), BF16), jax.ShapeDtypeStruct((t_len, RW), BF16),
                   jax.ShapeDtypeStruct((t_len, RW), BF16), jax.ShapeDtypeStruct((8, D), F32)),
        compiler_params=_cparams(1),
    )(dx1, y, of, ob, gfb, mod6, wout)


def _ret_bwd_dir(q_ref, k_ref, v_ref, do_ref, st_ref, dq_ref, dk_ref, dv_ref, ds_scr, dlg_scr, lg, forward, nc, row0):
    order = reversed(range(nc)) if forward else range(nc)
    order = list(order)
    for h in range(H):
        dm, xi, zc, dec, ic = _decay_consts(lg[h:h + 1, 0:1], forward)
        hs = slice(h * DH, (h + 1) * DH)
        if forward:
            w_qi, w_qc, w_ki, w_ks = ic, ic + 1.0, -ic, (CR - 1.0) - ic
        else:
            w_qi, w_qc, w_ki, w_ks = -ic, CR - ic, ic, ic
        acc = jnp.zeros((1, DH), F32)
        for c in order:
            rs = slice(c * CR, (c + 1) * CR)
            qc, kc, vc, doc = q_ref[rs, hs], k_ref[rs, hs], v_ref[rs, hs], do_ref[rs, hs]
            s = st_ref[c, h]
            dsn = ds_scr[h]
            sb, dsnb = s.astype(BF16), dsn.astype(BF16)
            qf, kf = qc.astype(F32), kc.astype(F32)
            a = (_dot_nt(qc, kc) * dm).astype(BF16)
            da = (_dot_nt(doc, vc) * dm).astype(BF16)
            xdo = (xi * doc.astype(F32)).astype(BF16)
            kz = (kf * zc).astype(BF16)
            vz = (vc.astype(F32) * zc).astype(BF16)
            dq_i = _dot(da, kc)
            dq_c = _dot_nt(xdo, sb)
            dk_i = _dot_tn(da, qc)
            dk_s = _dot_nt(vz, dsnb)
            dq_ref[rs, hs] = (dq_i + dq_c).astype(BF16)
            dk_ref[rs, hs] = (dk_i + dk_s).astype(BF16)
            dv_ref[rs, hs] = (_dot_tn(a, doc) + _dot(kz, dsnb)).astype(BF16)
            rowterm = qf * (w_qi * dq_i + w_qc * dq_c) + kf * (w_ki * dk_i + w_ks * dk_s)
            acc = acc + jnp.sum(rowterm, axis=0, keepdims=True)
            acc = acc + (float(CR) * dec) * jnp.sum(s * dsn, axis=0, keepdims=True)
            ds_scr[h] = _dot_tn((xi * qf).astype(BF16), doc) + dec * dsn
        dlg_scr[row0 + h:row0 + h + 1, :] += acc


def _ret_bwd(q, k, v, dof, dob, sst, rst, rlf, rlb, *, tm):
    t_len = q.shape[0]
    nt, nc = t_len // tm, tm // CR

    def body(qf_ref, kf_ref, vf_ref, dof_ref, sst_ref, qb_ref, kb_ref, vb_ref, dob_ref, rst_ref, rlf_ref, rlb_ref,
             dqf_ref, dkf_ref, dvf_ref, dqb_ref, dkb_ref, dvb_ref, dscf_ref, dscb_ref, dlg_ref,
             dsf_scr, dsb_scr, dlg_scr):
        i = pl.program_id(0)

        @pl.when(i == 0)
        def _():
            dsf_scr[...] = jnp.zeros_like(dsf_scr)
            dsb_scr[...] = jnp.zeros_like(dsb_scr)
            dlg_scr[...] = jnp.zeros_like(dlg_scr)

        lgf = _log_sigmoid(rlf_ref[...])
        lgb = _log_sigmoid(rlb_ref[...])
        _ret_bwd_dir(qf_ref, kf_ref, vf_ref, dof_ref, sst_ref, dqf_ref, dkf_ref, dvf_ref, dsf_scr, dlg_scr, lgf, True, nc, 0)
        _ret_bwd_dir(qb_ref, kb_ref, vb_ref, dob_ref, rst_ref, dqb_ref, dkb_ref, dvb_ref, dsb_scr, dlg_scr, lgb, False, nc, H)

        @pl.when(i == nt - 1)
        def _():
            dscf_ref[...] = dsf_scr[...]
            dscb_ref[...] = dsb_scr[...]
            tot = jnp.broadcast_to(jnp.sum(dlg_scr[...], axis=1, keepdims=True), (8, 128))
            ri = lax.broadcasted_iota(jnp.int32, (8, 128), 0)
            li = lax.broadcasted_iota(jnp.int32, (8, 128), 1)
            dlg_ref[...] = jnp.zeros_like(dlg_ref)
            dlg_ref[0:1, 0:128] = jnp.sum(jnp.where(ri == li, tot, 0.0), axis=0, keepdims=True)
            dlg_ref[1:2, 0:128] = jnp.sum(jnp.where(ri - H == li, tot, 0.0), axis=0, keepdims=True)

    rr = functools.partial(_rows_rev, nt=nt)
    st_f = pl.BlockSpec((nc, H, DH, DH), lambda i: (nt - 1 - i, 0, 0, 0))
    st_b = pl.BlockSpec((nc, H, DH, DH), lambda i: (i, 0, 0, 0))
    tok = jax.ShapeDtypeStruct((t_len, RW), BF16)
    st = jax.ShapeDtypeStruct((H, DH, DH), F32)
    return pl.pallas_call(
        body, name="ret_bwd", grid=(nt,),
        in_specs=[rr(tm, RW), rr(tm, RW), rr(tm, RW), rr(tm, RW), st_f,
                  _rows(tm, RW), _rows(tm, RW), _rows(tm, RW), _rows(tm, RW), st_b,
                  _whole((H, 128)), _whole((H, 128))],
        out_specs=[rr(tm, RW), rr(tm, RW), rr(tm, RW), _rows(tm, RW), _rows(tm, RW), _rows(tm, RW),
                   _acc((H, DH, DH)), _acc((H, DH, DH)), _acc((8, D))],
        out_shape=(tok, tok, tok, tok, tok, tok, st, st, jax.ShapeDtypeStruct((8, D), F32)),
        scratch_shapes=[pltpu.VMEM((H, DH, DH), F32), pltpu.VMEM((H, DH, DH), F32), pltpu.VMEM((8, 128), F32)],
        compiler_params=_cparams(1),
    )(q, k, v, dof, sst, q, k, v, dob, rst, rlf, rlb)


def _in_bwd(x, zuv, dya, dqf, dkf, dvf, dqb, dkb, dvb, dgfb, dx1, cos, sin, mod6, norm1, win, sgw, sgbt, sggain, *, tm):
    t_len = x.shape[0]
    nt, nc = t_len // tm, tm // C

    def body(x_ref, zuv_ref, dya_ref, dqf_ref, dkf_ref, dvf_ref, dqb_ref, dkb_ref, dvb_ref, dgfb_ref, dx1_ref,
             cos_ref, sin_ref, mod_ref, n1_ref, win_ref, sgw_ref, sgbt_ref, sgg_ref,
             gx_ref, dz_ref, small_ref, dsgw_ref, dsgbt_ref):
        i = pl.program_id(0)

        @pl.when(i == 0)
        def _():
            small_ref[...] = jnp.zeros_like(small_ref)
            dsgw_ref[...] = jnp.zeros_like(dsgw_ref)
            dsgbt_ref[...] = jnp.zeros_like(dsgbt_ref)

        zuv = zuv_ref[...].astype(F32)
        _, (ua, dgu, dgv, mixed, vn_b, vah_l, rv_l) = _sg_forward(zuv[:, :AW], zuv[:, AW:], sgw_ref, sgbt_ref, sgg_ref, nc)
        dya = dya_ref[...].astype(F32)
        dz_ref[:, 0:AW] = (dya * mixed * dgu).astype(BF16)
        dmixed = dya * ua
        gain = sgg_ref[...]
        for g in range(G):
            gs = slice(g * DH, (g + 1) * DH)
            dmg = dmixed[:, gs]
            dmb = dmg.astype(BF16)
            wgt = sgw_ref[g].astype(BF16)
            dsw = jnp.zeros((C, C), F32)
            dsb = jnp.zeros((C, DH), F32)
            rows = []
            for c in range(nc):
                rs = slice(c * C, (c + 1) * C)
                dsw = dsw + _dot_nt(dmb[rs, :], vn_b[g][rs, :])
                dsb = dsb + dmg[rs, :]
                rows.append(_dot_tn(wgt, dmb[rs, :]))
            dsgw_ref[g] += dsw
            dsgbt_ref[g] += dsb
            dvn = jnp.concatenate(rows, axis=0) if nc > 1 else rows[0]
            vah, rv = vah_l[g], rv_l[g]
            small_ref[3:4, gs] += jnp.sum(dvn * vah, axis=0, keepdims=True)
            dyg = dvn * gain[:, gs]
            dva = rv * (dyg - vah * jnp.mean(dyg * vah, axis=-1, keepdims=True))
            dz_ref[:, AW + g * DH:AW + (g + 1) * DH] = (dva * dgv[:, gs]).astype(BF16)

        cos = jnp.tile(cos_ref[...], (1, H))
        nsins = -jnp.tile(sin_ref[...], (1, H))
        def both(f_ref, b_ref):
            return f_ref[...].astype(F32) + b_ref[...].astype(F32)

        dz_ref[:, Q_LO:KV_LO] = _rope(both(dqf_ref, dqb_ref), cos, nsins).astype(BF16)
        dz_ref[:, KV_LO:VR_LO] = (_rope(both(dkf_ref, dkb_ref), cos, nsins) * KSCALE).astype(BF16)
        dz_ref[:, VR_LO:G_LO] = both(dvf_ref, dvb_ref).astype(BF16)
        dz_ref[:, G_LO:INC] = dgfb_ref[...]

        dhx = _dot(dz_ref[...], win_ref[...])
        x = x_ref[...]
        r = lax.rsqrt(jnp.mean(x * x, axis=-1, keepdims=True) + EPS)
        xh = x * r
        n1, sc1 = n1_ref[...], mod_ref[1:2, :]
        small_ref[0:1, :] += jnp.sum(dhx, axis=0, keepdims=True)
        dhxx = jnp.sum(dhx * xh, axis=0, keepdims=True)
        small_ref[1:2, :] += dhxx * n1
        small_ref[2:3, :] += dhxx * (1.0 + sc1)
        dyg = dhx * (n1 * (1.0 + sc1))
        gx_ref[...] = dx1_ref[...] + r * (dyg - xh * jnp.mean(dyg * xh, axis=-1, keepdims=True))

        @pl.when(i == nt - 1)
        def _():
            ri = lax.broadcasted_iota(jnp.int32, (C, DH), 0)
            li = lax.broadcasted_iota(jnp.int32, (C, DH), 1)
            for g in range(G):
                tot = jnp.broadcast_to(jnp.sum(dsgbt_ref[g], axis=1, keepdims=True), (C, DH))
                small_ref[4 + g:5 + g, 0:DH] = jnp.sum(jnp.where(ri == li, tot, 0.0), axis=0, keepdims=True)

    tok = functools.partial(_rows, tm)
    return pl.pallas_call(
        body, name="in_bwd", grid=(nt,),
        in_specs=[tok(D), tok(2 * AW), tok(AW), tok(RW), tok(RW), tok(RW), tok(RW), tok(RW), tok(RW),
                  tok(2 * RW), tok(D), tok(DH), tok(DH), _whole((6, D)), _whole((1, D)), _whole((INC, D)),
                  _whole((G, C, C)), _whole((G, C, 128)), _whole((1, AW))],
        out_specs=[tok(D), tok(INC), _acc((8, D)), _acc((G, C, C))],
        out_shape=(jax.ShapeDtypeStruct((t_len, D), F32), jax.ShapeDtypeStruct((t_len, INC), BF16),
                   jax.ShapeDtypeStruct((8, D), F32), jax.ShapeDtypeStruct((G, C, C), F32)),
        scratch_shapes=[pltpu.VMEM((G, C, 128), F32)],
        compiler_params=_cparams(1),
    )(x, zuv, dya, dqf, dkf, dvf, dqb, dkb, dvb, dgfb, dx1, cos, sin, mod6, norm1, win, sgw, sgbt, sggain)


def _tn_matmul(at, b, *, bm, bt, name, init=None, init_rows=None, after=(), cols=None, transposed=True):
    m, t_len = at.shape if transposed else at.shape[::-1]
    cj, n = (0, b.shape[1]) if cols is None else cols
    ni, ntt = m // bm, t_len // bt
    has_init = init is not None

    def body(*refs):
        o_ref, ob_ref = refs[-2:]
        a_ref, b_ref = refs[:2]
        init_ref = refs[2] if has_init else None
        i, t = pl.program_id(0), pl.program_id(1)

        @pl.when(t == 0)
        def _():
            o_ref[...] = jnp.zeros_like(o_ref)

        if has_init:
            for ib in range(ni):
                lo, hi = max(init_rows[0], ib * bm), min(init_rows[1], (ib + 1) * bm)
                if lo < hi:
                    @pl.when(jnp.logical_and(t == 0, i == ib))
                    def _(lo=lo, hi=hi, ib=ib):
                        o_ref[lo - ib * bm:hi - ib * bm, :] = init_ref[lo - init_rows[0]:hi - init_rows[0], :]

        o_ref[...] += (_dot if transposed else _dot_tn)(a_ref[...], b_ref[...])

        @pl.when(t == ntt - 1)
        def _():
            ob_ref[...] = o_ref[...].astype(BF16)

    a_spec = pl.BlockSpec((bm, bt), lambda i, t: (i, t)) if transposed else pl.BlockSpec((bt, bm), lambda i, t: (t, i))
    in_specs = [a_spec, pl.BlockSpec((bt, n), lambda i, t: (t, cj))]
    args = [at, b]
    if has_init:
        in_specs.append(pl.BlockSpec((init.shape[0], n), lambda i, t: (0, cj), pipeline_mode=pl.Buffered(1)))
        args.append(init)
    for arr in after:
        in_specs.append(pl.BlockSpec(memory_space=pl.ANY))
        args.append(arr)
    out_spec = pl.BlockSpec((bm, n), lambda i, t: (i, 0))
    return pl.pallas_call(
        body, name=name, grid=(ni, ntt),
        in_specs=in_specs,
        out_specs=[out_spec, out_spec],
        out_shape=(jax.ShapeDtypeStruct((m, n), F32), jax.ShapeDtypeStruct((m, n), BF16)),
        compiler_params=_cparams(2),
    )(*args)


def _ctx_bwd(ctx, hc, kvc, cmod6, norm1, win, rlf, rlb, dscf, dscb, dlg_in):
    lc = ctx.shape[0]

    def body(ctx_ref, hc_ref, kvc_ref, cmod_ref, n1_ref, win_ref, rlf_ref, rlb_ref, dscf_ref, dscb_ref, dlgin_ref,
             dwin_ref, small_ref, dlg_ref):
        k = kvc_ref[:, :RW]
        v = kvc_ref[:, RW:]
        t = lax.broadcasted_iota(jnp.int32, (lc, 1), 0).astype(F32)
        lgf = _log_sigmoid(rlf_ref[...])
        lgb = _log_sigmoid(rlb_ref[...])
        dks, dvs = [], []
        lane = lax.broadcasted_iota(jnp.int32, (1, 128), 1)
        row_f = jnp.zeros((1, 128), F32)
        row_b = jnp.zeros((1, 128), F32)
        for h in range(H):
            hs = slice(h * DH, (h + 1) * DH)
            wf = jnp.exp(lgf[h:h + 1, 0:1] * (lc - 1.0 - t))
            wb = jnp.exp(lgb[h:h + 1, 0:1] * t)
            kh, vhb = k[:, hs], v[:, hs].astype(BF16)
            dsf, dsb = dscf_ref[h].astype(BF16), dscb_ref[h].astype(BF16)
            dkf = wf * _dot_nt(vhb, dsf)
            dkb = wb * _dot_nt(vhb, dsb)
            dvs.append(_dot((kh * wf).astype(BF16), dsf) + _dot((kh * wb).astype(BF16), dsb))
            dks.append((dkf + dkb) * KSCALE)
            lf = jnp.sum((lc - 1.0 - t) * kh * dkf, axis=0, keepdims=True)
            lb = jnp.sum(t * kh * dkb, axis=0, keepdims=True)
            row_f = row_f + jnp.where(lane == h, jnp.sum(lf, axis=1, keepdims=True), 0.0)
            row_b = row_b + jnp.where(lane == h, jnp.sum(lb, axis=1, keepdims=True), 0.0)
        dlg_ref[...] = dlgin_ref[...]
        dlg_ref[0:1, 0:128] += row_f
        dlg_ref[1:2, 0:128] += row_b
        dkv = jnp.concatenate(dks + dvs, axis=1).astype(BF16)
        dhc = _dot(dkv, win_ref[KV_LO:KV_HI, :])
        dwin_ref[...] = _dot_tn(dkv, hc_ref[...])
        x = ctx_ref[...]
        r = lax.rsqrt(jnp.mean(x * x, axis=-1, keepdims=True) + EPS)
        xh = x * r
        small_ref[...] = jnp.zeros_like(small_ref)
        small_ref[0:1, :] = jnp.sum(dhc, axis=0, keepdims=True)
        dhxx = jnp.sum(dhc * xh, axis=0, keepdims=True)
        small_ref[1:2, :] = dhxx * n1_ref[...]
        small_ref[2:3, :] = dhxx * (1.0 + cmod_ref[1:2, :])

    return pl.pallas_call(
        body, name="ctx_bwd",
        out_shape=(jax.ShapeDtypeStruct((2 * RW, D), F32), jax.ShapeDtypeStruct((8, D), F32),
                   jax.ShapeDtypeStruct((8, D), F32)),
        compiler_params=_cparams(),
    )(ctx, hc, kvc, cmod6, norm1, win, rlf, rlb, dscf, dscb, dlg_in)


def _adam_math(w, g, m, v):
    m = ADAM_B1 * m + (1.0 - ADAM_B1) * g
    v = ADAM_B2 * v + (1.0 - ADAM_B2) * (g * g)
    m_hat = m / (1.0 - ADAM_B1 ** ADAM_STEP)
    v_hat = v / (1.0 - ADAM_B2 ** ADAM_STEP)
    delta = -ADAM_LR * (m_hat / (jnp.sqrt(v_hat) + ADAM_EPS) + ADAM_WD * w)
    return delta, m, v


def _place():
    x, y, c = lax.axis_index("x"), lax.axis_index("y"), lax.axis_index("c")
    return x, y, c, 4 * x + 2 * y + c


def _flip(x, y, c, k):
    px = 1 - x if (k >> 2) & 1 else x
    py = 1 - y if (k >> 1) & 1 else y
    pc = 1 - c if k & 1 else c
    return (px, py, pc), 4 * px + 2 * py + pc


def _gather_copy(buf_ref, send_sems, recv_sems, sem0, k, mine):
    x, y, c, me = _place()
    dev, idx = _flip(x, y, c, k)
    blk = me if mine else idx
    return pltpu.make_async_remote_copy(
        src_ref=buf_ref.at[blk], dst_ref=buf_ref.at[blk],
        send_sem=send_sems.at[sem0 + k - 1], recv_sem=recv_sems.at[sem0 + k - 1],
        device_id=dev, device_id_type=MESH)


def _entry_gather(c_row, cctx_row, wmod, bmod, shards):
    n = len(shards)
    ncol = wmod.shape[1]

    def body(*refs):
        c_ref, cctx_ref, wmod_ref, bmod_ref = refs[:4]
        in_refs = refs[4:4 + n]
        mod_ref, cs_ref = refs[4 + n:6 + n]
        out_refs = refs[6 + n:6 + 2 * n]
        cbuf, pbuf = refs[6 + 2 * n:8 + 2 * n]
        cast_refs = refs[8 + 2 * n:8 + 3 * n]
        small_send, small_recv, send_sems, recv_sems, local_sems = refs[8 + 3 * n:]
        x, y, c, me = _place()
        sib = (x, y, 1 - c)
        chips = [(1 - x, y), (x, 1 - y), (1 - x, 1 - y)]

        cbuf[me] = jnp.broadcast_to(c_ref[...], (8, D))
        small = [_gather_copy(cbuf, small_send, small_recv, 0, k, True) for k in range(1, NDEV)]
        for cp in small:
            cp.start()

        def blk(px, py, pc):
            return 4 * px + 2 * py + pc

        def copy(w, k, block, to, src=None):
            dst = out_refs[w].at[block]
            return pltpu.make_async_remote_copy(
                src_ref=dst if src is None else src, dst_ref=dst,
                send_sem=send_sems.at[w, k], recv_sem=recv_sems.at[w, k], device_id=to, device_id_type=MESH)

        first, passed, mine = [], [], []
        for w in range(n):
            cast_refs[w][...] = in_refs[w][...].astype(BF16)
            m = pltpu.make_async_copy(cast_refs[w], out_refs[w].at[me], local_sems.at[w])
            m.start()
            mine.append(m)
            cps = [copy(w, 0, me, sib, src=cast_refs[w])]
            cps += [copy(w, 1 + j, me, (*chip, c), src=cast_refs[w]) for j, chip in enumerate(chips)]
            for cp in cps:
                cp.start()
            first += cps

        for k in range(1, NDEV):
            _gather_copy(cbuf, small_send, small_recv, 0, k, False).wait_recv()
        row = lax.broadcasted_iota(jnp.int32, (16, D), 0)
        call = jnp.where(row == 8, jnp.broadcast_to(cctx_ref[...], (16, D)), 0.0)
        for d in range(NDEV):
            call = jnp.where(row == d, jnp.concatenate([cbuf[d], cbuf[d]], axis=0), call)
        cs = call * _sigmoid(call)
        cs_ref[...] = cs
        pbuf[me] = _dot(cs.astype(BF16), wmod_ref[...].astype(BF16))
        small2 = [_gather_copy(pbuf, small_send, small_recv, NDEV - 1, k, True) for k in range(1, NDEV)]
        for cp in small2:
            cp.start()

        for w in range(n):
            for j, chip in enumerate(chips):
                b = blk(*chip, c)
                copy(w, 1 + j, b, (x, y, c)).wait_recv()
                fw = copy(w, 4 + j, b, sib)
                fw.start()
                passed.append(fw)
        for w in range(n):
            copy(w, 0, blk(x, y, 1 - c), (x, y, c)).wait_recv()
            for j, chip in enumerate(chips):
                copy(w, 4 + j, blk(*chip, 1 - c), (x, y, c)).wait_recv()

        for k in range(1, NDEV):
            _gather_copy(pbuf, small_send, small_recv, NDEV - 1, k, False).wait_recv()
        for d in range(NDEV):
            mod_ref[:, d * ncol:(d + 1) * ncol] = pbuf[d] + bmod_ref[:, d * ncol:(d + 1) * ncol]
        for cp in small + small2 + first + passed:
            cp.wait_send()
        for m in mine:
            m.wait()

    vm = pl.BlockSpec(memory_space=pltpu.VMEM)
    hbm = pl.BlockSpec(memory_space=pltpu.HBM)
    return pl.pallas_call(
        body, name="entry_gather",
        in_specs=[vm] * (4 + n), out_specs=[vm, vm] + [hbm] * n,
        out_shape=(jax.ShapeDtypeStruct((16, 6 * D), F32), jax.ShapeDtypeStruct((16, D), F32))
        + tuple(jax.ShapeDtypeStruct((NDEV,) + s.shape, BF16) for s in shards),
        scratch_shapes=[pltpu.VMEM((NDEV, 8, D), F32), pltpu.VMEM((NDEV, 16, ncol), F32)]
        + [pltpu.VMEM(s.shape, BF16) for s in shards]
        + [pltpu.SemaphoreType.DMA((2 * (NDEV - 1),)), pltpu.SemaphoreType.DMA((2 * (NDEV - 1),)),
           pltpu.SemaphoreType.DMA((n, 7)), pltpu.SemaphoreType.DMA((n, 7)), pltpu.SemaphoreType.DMA((n,))],
        compiler_params=_cparams(),
    )(c_row, cctx_row, wmod, bmod, *shards)


_HBM = pl.BlockSpec(memory_space=pltpu.HBM)
_SEMS = pl.BlockSpec(memory_space=pltpu.SEMAPHORE)
_EFFECT = pltpu.SideEffectType.DATAFLOW_SIDE_EFFECTING


def _exchange_copy(src_refs, land_refs, send_sems, recv_sems, w, k, scatter, landing_slot_is_mine):
    x, y, c, me = _place()
    dev, pidx = _flip(x, y, c, k)
    src = src_refs[w].at[pidx] if scatter else src_refs[w]
    slot = me if landing_slot_is_mine else pidx
    return pltpu.make_async_remote_copy(
        src_ref=src, dst_ref=land_refs[w].at[slot],
        send_sem=send_sems.at[w * (NDEV - 1) + k - 1], recv_sem=recv_sems.at[w * (NDEV - 1) + k - 1],
        device_id=dev, device_id_type=MESH)


def _exchange_start(srcs, *, scatter, name):
    n = len(srcs)
    lands = [lax.empty((NDEV,) + tuple(s.shape[-2:]), s.dtype) for s in srcs]

    def body(*refs):
        src_refs, land_refs = refs[:n], refs[n:2 * n]
        send_sems, recv_sems = refs[2 * n], refs[2 * n + 1]
        token = refs[-1]
        for w in range(n):
            for k in range(1, NDEV):
                _exchange_copy(src_refs, land_refs, send_sems, recv_sems, w, k, scatter, True).start()
        token[...] = jnp.zeros_like(token)

    arrs = list(srcs) + lands
    out_shape = ([pltpu.SemaphoreType.DMA((n * (NDEV - 1),)), pltpu.SemaphoreType.DMA((n * (NDEV - 1),))]
                 + [pltpu.HBM(a.shape, a.dtype) for a in arrs] + [jax.ShapeDtypeStruct((8, 128), F32)])
    res = pl.pallas_call(
        body, name=name, out_shape=tuple(out_shape),
        in_specs=[_HBM] * (2 * n),
        out_specs=tuple([_SEMS, _SEMS] + [_HBM] * (2 * n) + [pl.BlockSpec(memory_space=pltpu.VMEM)]),
        input_output_aliases={i: 2 + i for i in range(2 * n)},
        compiler_params=pltpu.CompilerParams(has_side_effects=_EFFECT),
    )(*[pltpu.with_memory_space_constraint(a, pltpu.HBM) for a in arrs])
    return res[:-1], res[-1]


def _exchange_wait(handles, after, *, scatter, name):
    n = (len(handles) - 2) // 2
    na = len(after)

    def body(*refs):
        src_refs, land_refs = refs[:n], refs[n:2 * n]
        send_sems, recv_sems = refs[2 * n], refs[2 * n + 1]
        for w in range(n):
            for k in range(1, NDEV):
                cp = _exchange_copy(src_refs, land_refs, send_sems, recv_sems, w, k, scatter, False)
                cp.wait_send()
                cp.wait_recv()

    arrs = handles[2:]
    res = pl.pallas_call(
        body, name=name, out_shape=tuple(pltpu.HBM(a.shape, a.dtype) for a in arrs),
        in_specs=[_HBM] * (2 * n) + [_SEMS, _SEMS] + [_HBM] * na,
        out_specs=tuple([_HBM] * (2 * n)),
        input_output_aliases={i: i for i in range(2 * n)},
        compiler_params=pltpu.CompilerParams(has_side_effects=_EFFECT),
    )(*arrs, handles[0], handles[1], *[pltpu.with_memory_space_constraint(a, pltpu.HBM) for a in after])
    return res[:n], res[n:]


_SAME_CORE = (2, 4, 6)


def _gather2_copy(src_ref, land_ref, send_sems, recv_sems, sem, k, block):
    x, y, c, _ = _place()
    dev, _ = _flip(x, y, c, k)
    dst = land_ref.at[block]
    return pltpu.make_async_remote_copy(
        src_ref=dst if src_ref is None else src_ref, dst_ref=dst,
        send_sem=send_sems.at[sem], recv_sem=recv_sems.at[sem], device_id=dev, device_id_type=MESH)


def _split_call(body, name, arrs, n_sems, sems=None, after=(), token=False):
    na = len(arrs)
    out_shape, out_specs = [], []
    if n_sems is not None:
        out_shape += [pltpu.SemaphoreType.DMA((n_sems,)), pltpu.SemaphoreType.DMA((n_sems,))]
        out_specs += [_SEMS, _SEMS]
    first = len(out_shape)
    out_shape += [pltpu.HBM(a.shape, a.dtype) for a in arrs]
    out_specs += [_HBM] * na
    if token:
        out_shape.append(jax.ShapeDtypeStruct((8, 128), F32))
        out_specs.append(pl.BlockSpec(memory_space=pltpu.VMEM))
    in_specs = [_HBM] * na + ([_SEMS, _SEMS] if sems is not None else []) + [_HBM] * len(after)
    args = [pltpu.with_memory_space_constraint(a, pltpu.HBM) for a in arrs] + list(sems or ()) \
        + [pltpu.with_memory_space_constraint(a, pltpu.HBM) for a in after]
    return pl.pallas_call(
        body, name=name, out_shape=tuple(out_shape), in_specs=in_specs, out_specs=tuple(out_specs),
        input_output_aliases={i: first + i for i in range(na)},
        compiler_params=pltpu.CompilerParams(has_side_effects=_EFFECT))(*args)


def _gather2_start(srcs, *, name):
    n = len(srcs)
    lands = [lax.empty((NDEV,) + tuple(s.shape), s.dtype) for s in srcs]

    def body(*refs):
        src_refs, land_refs = refs[:n], refs[n:2 * n]
        send_sems, recv_sems = refs[2 * n], refs[2 * n + 1]
        _, _, _, me = _place()
        for w in range(n):
            for slot, k in enumerate((1,) + _SAME_CORE):
                _gather2_copy(src_refs[w], land_refs[w], send_sems, recv_sems, 4 * w + slot, k, me).start()
        refs[-1][...] = jnp.zeros_like(refs[-1])

    res = _split_call(body, name, list(srcs) + lands, 4 * n, token=True)
    return res[:2], res[2:-1], res[-1]


def _gather2_arrived(sems, arrs, after, *, name):
    n = len(arrs) // 2

    def body(*refs):
        src_refs, land_refs = refs[:n], refs[n:2 * n]
        send_sems, recv_sems = refs[2 * n], refs[2 * n + 1]
        x, y, c, me = _place()
        for w in range(n):
            for slot, k in enumerate((1,) + _SAME_CORE):
                _, pidx = _flip(x, y, c, k)
                _gather2_copy(src_refs[w], land_refs[w], send_sems, recv_sems, 4 * w + slot, k, me).wait_send()
                _gather2_copy(None, land_refs[w], send_sems, recv_sems, 4 * w + slot, k, pidx).wait_recv()

    return _split_call(body, name, arrs, None, sems=sems, after=after)


def _gather2_forward(lands, *, name):
    n = len(lands)

    def body(*refs):
        land_refs = refs[:n]
        send_sems, recv_sems = refs[n], refs[n + 1]
        x, y, c, _ = _place()
        for w in range(n):
            for j, k in enumerate(_SAME_CORE):
                _, pidx = _flip(x, y, c, k)
                _gather2_copy(None, land_refs[w], send_sems, recv_sems, 3 * w + j, 1, pidx).start()
        refs[-1][...] = jnp.zeros_like(refs[-1])

    res = _split_call(body, name, list(lands), 3 * n, token=True)
    return res[:2], res[2:-1], res[-1]


def _gather2_wait(sems, lands, after, *, name):
    n = len(lands)

    def body(*refs):
        land_refs = refs[:n]
        send_sems, recv_sems = refs[n], refs[n + 1]
        x, y, c, _ = _place()
        for w in range(n):
            for j, k in enumerate(_SAME_CORE):
                _, mine = _flip(x, y, c, k)
                _, theirs = _flip(x, y, c, k ^ 1)
                _gather2_copy(None, land_refs[w], send_sems, recv_sems, 3 * w + j, 1, mine).wait_send()
                _gather2_copy(None, land_refs[w], send_sems, recv_sems, 3 * w + j, 1, theirs).wait_recv()

    return _split_call(body, name, list(lands), None, sems=sems, after=after)


def _cast_bf16(arrs, *, name, after=()):
    n = len(arrs)

    def body(*refs):
        for i_ref, o_ref in zip(refs[:n], refs[n + len(after):]):
            o_ref[...] = i_ref[...].astype(BF16)

    return pl.pallas_call(
        body, name=name, out_shape=tuple(jax.ShapeDtypeStruct(a.shape, BF16) for a in arrs),
        in_specs=[pl.BlockSpec(memory_space=pltpu.VMEM)] * n + [pl.BlockSpec(memory_space=pl.ANY)] * len(after),
        compiler_params=_cparams())(*arrs, *after)


def _rs_adam(me_arr, fulls, lands, w, m, v, *, name):
    rows = lands[0].shape[1]
    k = len(fulls)

    def body(me_ref, *refs):
        own_refs, land_refs = refs[:k], refs[k:2 * k]
        w_ref, m_ref, v_ref, g_ref, d_ref, mo_ref, vo_ref = refs[2 * k:]
        me = me_ref[0]
        parts = []
        for own_ref, land_ref in zip(own_refs, land_refs):
            acc = jnp.zeros(own_ref.shape, F32)
            for p in range(NDEV):
                acc = acc + jnp.where(p == me, own_ref[...], land_ref[p].astype(F32))
            parts.append(acc)
        acc = parts[0] if k == 1 else jnp.concatenate(parts, axis=1)
        g_ref[...] = acc
        d_ref[...], mo_ref[...], vo_ref[...] = _adam_math(w_ref[...], acc, m_ref[...], v_ref[...])

    sh = jax.ShapeDtypeStruct((rows, D), F32)
    whole2 = pl.BlockSpec((rows, D), lambda i, me: (0, 0))
    grid_spec = pltpu.PrefetchScalarGridSpec(
        num_scalar_prefetch=1, grid=(1,),
        in_specs=[pl.BlockSpec((rows, f.shape[1]), lambda i, me: (me[0], 0)) for f in fulls]
        + [pl.BlockSpec((NDEV, rows, l.shape[2]), lambda i, me: (0, 0, 0)) for l in lands]
        + [whole2, whole2, whole2],
        out_specs=[whole2] * 4)
    return pl.pallas_call(
        body, name=name, out_shape=(sh, sh, sh, sh), grid_spec=grid_spec, compiler_params=_cparams(1),
    )(me_arr, *fulls, *lands, w, m, v)


ROW_F, ROW_I, ROW_C, ROW_G1, ROW_LG = 0, 8, 16, 24, 32
PACK_ROWS = 40


def _small_sum(me_arr, pack, pack_land, sgw, sgw_land, wmod):
    ncol = wmod.shape[1]

    def body(me_ref, pack_ref, pland_ref, sgw_ref, sland_ref, wmod_ref, sum_ref, all_ref, sgw_sum_ref, part_ref):
        me = me_ref[0]
        acc = jnp.zeros((PACK_ROWS, D), F32)
        acc_w = jnp.zeros(sgw_ref.shape, F32)
        for p in range(NDEV):
            rows = jnp.where(p == me, pack_ref[...], pland_ref[p])
            all_ref[p] = rows
            acc = acc + rows
            acc_w = acc_w + jnp.where(p == me, sgw_ref[...], sland_ref[p])
        sum_ref[...] = acc
        sgw_sum_ref[...] = acc_w
        zero = jnp.zeros((1, D), F32)
        dcmod = jnp.concatenate([acc[ROW_C:ROW_C + 1], acc[ROW_C + 1:ROW_C + 2], zero, zero, zero, zero], axis=1)
        mine = jnp.zeros((1, ncol), F32)
        for d in range(NDEV):
            mine = mine + jnp.where(d == me, dcmod[:, d * ncol:(d + 1) * ncol], 0.0)
        part_ref[...] = _dot_nt(jnp.broadcast_to(mine, (8, ncol)).astype(BF16), wmod_ref[...].astype(BF16))

    vm = pl.BlockSpec(memory_space=pltpu.VMEM)
    return pl.pallas_call(
        body, name="small_sum",
        out_shape=(jax.ShapeDtypeStruct((PACK_ROWS, D), F32), jax.ShapeDtypeStruct((NDEV, PACK_ROWS, D), F32),
                   jax.ShapeDtypeStruct(sgw.shape, F32), jax.ShapeDtypeStruct((8, D), F32)),
        in_specs=[pl.BlockSpec(memory_space=pltpu.SMEM)] + [vm] * 5,
        compiler_params=_cparams(),
    )(me_arr, pack, pack_land, sgw, sgw_land, wmod)


def _cctx_final(me_arr, part, part_land, cctx_row, m_row, v_row):
    def body(me_ref, part_ref, land_ref, c_ref, m_ref, v_ref, g_ref, d_ref, mo_ref, vo_ref):
        me = me_ref[0]
        tot = jnp.zeros((8, D), F32)
        for p in range(NDEV):
            tot = tot + jnp.where(p == me, part_ref[...], land_ref[p])
        cc = c_ref[...]
        _, dsilu = _silu_parts(cc)
        g = tot[0:1, :] * dsilu
        g_ref[...] = g
        d_ref[...], mo_ref[...], vo_ref[...] = _adam_math(cc, g, m_ref[...], v_ref[...])

    row = jax.ShapeDtypeStruct((1, D), F32)
    vm = pl.BlockSpec(memory_space=pltpu.VMEM)
    return pl.pallas_call(
        body, name="cctx_final", out_shape=(row, row, row, row),
        in_specs=[pl.BlockSpec(memory_space=pltpu.SMEM)] + [vm] * 5,
        compiler_params=_cparams(),
    )(me_arr, part, part_land, cctx_row, m_row, v_row)


def _adam_small(s, sgw_g, params):
    names = ["norm1", "norm2", "norm_f", "sg_gain", "sg_b", "ret_logit_f", "ret_logit_b", "b_mod", "sg_w"]
    flat = [a for n in names for a in params[n]]

    def body(*refs):
        s_ref, sgw_ref = refs[0], refs[1]
        p_refs = refs[2:2 + 3 * len(names)]
        o_refs = refs[2 + 3 * len(names):]
        loss_ref = o_refs[-1]

        def row(r):
            return s_ref[r:r + 1, :]

        grads = {
            "norm1": row(ROW_I + 2) + row(ROW_C + 2),
            "norm2": row(ROW_F + 4),
            "norm_f": row(ROW_F + 0),
            "sg_gain": s_ref[ROW_I + 3:ROW_I + 4, 0:AW],
            "sg_b": s_ref[ROW_I + 4:ROW_I + 8, 0:DH],
            "sg_w": sgw_ref[...],
        }
        for j, n in enumerate(names):
            w_ref, m_ref, v_ref = p_refs[3 * j:3 * j + 3]
            g_ref, d_ref, mo_ref, vo_ref = o_refs[4 * j:4 * j + 4]
            w = w_ref[...]
            if n in ("ret_logit_f", "ret_logit_b"):
                r = ROW_LG + (0 if n == "ret_logit_f" else 1)
                g = s_ref[r:r + 1, 0:H] * _sigmoid(-w)
            elif n == "b_mod":
                rows = [row(ROW_I + 0) + row(ROW_C + 0), row(ROW_I + 1) + row(ROW_C + 1), row(ROW_G1),
                        row(ROW_F + 2), row(ROW_F + 3), row(ROW_F + 1)]
                g = jnp.concatenate(rows, axis=1)
            else:
                g = grads[n]
            g_ref[...] = g
            d_ref[...], mo_ref[...], vo_ref[...] = _adam_math(w, g, m_ref[...], v_ref[...])
        loss_ref[...] = jnp.full((1, 1), 0.5 / D, F32) * jnp.sum(row(ROW_F + 5), axis=1, keepdims=True)

    out_shape = []
    for n in names:
        w = params[n][0]
        out_shape += [jax.ShapeDtypeStruct(w.shape, F32)] * 4
    out_shape.append(jax.ShapeDtypeStruct((1, 1), F32))
    outs = pl.pallas_call(body, name="adam_small", out_shape=tuple(out_shape), compiler_params=_cparams())(
        s, sgw_g, *flat)
    return {n: tuple(outs[4 * j:4 * j + 4]) for j, n in enumerate(names)}, outs[-1]


def _wmod_grad(cs_all, dmod_cols, wmod, m, v):
    ncol = wmod.shape[1]

    def body(cs_ref, dm_ref, w_ref, m_ref, v_ref, g_ref, d_ref, mo_ref, vo_ref):
        g = _dot_tn(cs_ref[...].astype(BF16), dm_ref[...].astype(BF16))
        g_ref[...] = g
        d_ref[...], mo_ref[...], vo_ref[...] = _adam_math(w_ref[...], g, m_ref[...], v_ref[...])

    sh = jax.ShapeDtypeStruct((D, ncol), F32)
    return pl.pallas_call(
        body, name="wmod_grad", out_shape=(sh, sh, sh, sh),
        compiler_params=_cparams(),
    )(cs_all, dmod_cols, wmod, m, v)


def _rope_tables(t_len):
    n_freq = DH // 4
    inv = (ROPE_BASE ** (-np.arange(n_freq, dtype=np.float32) / n_freq)).astype(np.float32)
    tok = np.arange(t_len)
    rows = (tok // GRID_W).astype(np.float32)
    cols = (tok % GRID_W).astype(np.float32)
    ang_r = rows[:, None] * inv[None, :]
    ang_c = cols[:, None] * inv[None, :]
    cos = np.concatenate([np.cos(ang_r)] * 2 + [np.cos(ang_c)] * 2, axis=1).astype(np.float32)
    sin = np.concatenate([-np.sin(ang_r), np.sin(ang_r), -np.sin(ang_c), np.sin(ang_c)], axis=1).astype(np.float32)
    return jnp.asarray(cos), jnp.asarray(sin)


def _local_step(x, tgt, ctx, mod6, cmod6, norm1, norm2, normf, win, wout, ffn_weights, sgw, sgb, sggain, rlf, rlb,
                *, tm_a, tm_b, tm_f, tm_m, tm_r, tm_i, bt_w, after_first_grads=None, small_path=None,
                after_in_half=None):
    t_len = x.shape[0]
    cos, sin = _rope_tables(t_len)
    sgbt = jnp.broadcast_to(sgb[:, :, None], (G, C, 128))
    rlf = jnp.broadcast_to(rlf.reshape(H, 1), (H, 128))
    rlb = jnp.broadcast_to(rlb.reshape(H, 1), (H, 128))
    hc, kvc, scf, scb = _ctx_fwd(ctx, cmod6, norm1, win, rlf, rlb)
    hx, zuv, q, k, v, gfb, of, ya, sst = _fwd_a(x, mod6, norm1, win, sgw, sgbt, sggain, cos, sin, rlf, scf, tm=tm_a)
    if callable(wout):
        wout, tok = wout(hx)
        rlb = rlb + tok
    ob, ycat, y, x1, rst = _fwd_b(q, k, v, of, gfb, ya, x, mod6, wout, rlb, scb, tm=tm_b)
    wg, wu, wd = ffn_weights(x1) if callable(ffn_weights) else ffn_weights
    dx1, h2, da, db, hm, df, small_f = _ffn(x1, tgt, mod6, norm2, normf, wg, wu, wd, tm=tm_f)
    bt2 = min(2 * bt_w, t_len)
    d_wd, d_wd_b = _tn_matmul(hm, df, bm=DFF // 2, bt=bt2, name="dw_down")
    d_wg, d_wg_b = _tn_matmul(da, h2, bm=DFF // 2, bt=bt2, name="dw_gate")
    d_wu, d_wu_b = _tn_matmul(db, h2, bm=DFF // 2, bt=bt2, name="dw_up")
    dy, dya, dgfb, dof, dob, dg1 = _mid_bwd(dx1, y, of, ob, gfb, mod6, wout, tm=tm_m)
    d_wo, d_wo_b = _tn_matmul(ycat, dy, bm=D, bt=bt2, name="dw_out", transposed=False)
    if after_first_grads is not None:
        rlf = rlf + after_first_grads((d_wd_b, d_wg_b, d_wu_b, d_wo_b))
    dqf, dkf, dvf, dqb, dkb, dvb, dscf, dscb, dlg = _ret_bwd(q, k, v, dof, dob, sst, rst, rlf, rlb, tm=tm_r)
    gx, dz, small_i, dsgw = _in_bwd(x, zuv, dya, dqf, dkf, dvf, dqb, dkb, dvb, dgfb, dx1, cos, sin,
                                    mod6, norm1, win, sgw, sgbt, sggain, tm=tm_i)
    dwin_c, small_c, dlg = _ctx_bwd(ctx, hc, kvc, cmod6, norm1, win, rlf, rlb, dscf, dscb, dlg)
    pack = jnp.concatenate([small_f, small_i, small_c, dg1, dlg], axis=0)
    after = small_path(pack, dsgw) if small_path is not None else ()
    halves = []
    for j in range(2):
        halves.append(_tn_matmul(dz, hx, bm=INC // 2, bt=bt2, name="dw_in_%d" % j, init=dwin_c,
                                 init_rows=(KV_LO, KV_HI), after=after, cols=(j, D // 2), transposed=False))
        if after_in_half is not None:
            after = after_in_half(j, halves[-1][1])
    grads = {"w_in": halves, "w_out": (d_wo, d_wo_b), "w_gate": (d_wg, d_wg_b), "w_up": (d_wu, d_wu_b),
             "w_down": (d_wd, d_wd_b)}
    return gx, grads, pack, dsgw


def kernel(x, c, ctx, c_ctx, w_mod, b_mod, norm1, w_in, sg_gain, sg_w, sg_b, ret_logit_f, ret_logit_b, w_out, norm2, w_gate, w_up, w_down, norm_f, loss_target, m_c_ctx, m_w_mod, m_b_mod, m_norm1, m_w_in, m_sg_gain, m_sg_w, m_sg_b, m_ret_logit_f, m_ret_logit_b, m_w_out, m_norm2, m_w_gate, m_w_up, m_w_down, m_norm_f, v_c_ctx, v_w_mod, v_b_mod, v_norm1, v_w_in, v_sg_gain, v_sg_w, v_sg_b, v_ret_logit_f, v_ret_logit_b, v_w_out, v_norm2, v_w_gate, v_w_up, v_w_down, v_norm_f):
    t_len = x.shape[1]
    tm = min(512, t_len)
    me = 4 * lax.axis_index("x") + 2 * lax.axis_index("y") + lax.axis_index("c")
    tr = lambda a: jnp.transpose(a[0])

    cctx_row = c_ctx.reshape(1, D)
    mod_all, cs_all, g_in = _entry_gather(c, cctx_row, w_mod[0], b_mod, [tr(w_in)])
    mod6 = lax.dynamic_slice(mod_all, (me, 0), (1, 6 * D)).reshape(6, D)
    cmod6 = mod_all[8].reshape(6, D)
    win_t = g_in.reshape(INC, D)

    (out_shard,) = _cast_bf16([w_out[0]], name="cast_out", after=[g_in])
    h_out, tok_out = _exchange_start([out_shard], scatter=False, name="wout_start")
    ffn_shards = _cast_bf16([tr(w_gate), tr(w_up), w_down[0]], name="cast_ffn", after=[h_out[2]])
    sems_ffn, arrs_ffn, tok = _gather2_start(ffn_shards, name="wffn_start")
    mod6 = mod6 + (tok_out[0, 0] + tok[0, 0])
    ffn = {}

    def place_own(own, lands, rows):
        return [lax.dynamic_update_slice(l, o[None], (me, 0, 0)).reshape(rows, D) for o, l in zip(own, lands)]

    def wout(after):
        own, lands = _exchange_wait(h_out, [after], scatter=False, name="wout_wait")
        arrs = _gather2_arrived(sems_ffn, arrs_ffn, [after], name="wffn_arrived")
        ffn["own"] = arrs[:3]
        ffn["sems"], ffn["lands"], tok_fwd = _gather2_forward(arrs[3:], name="wffn_forward")
        return place_own(own, lands, D)[0], tok_fwd[0, 0]

    def ffn_weights(after):
        lands = _gather2_wait(ffn["sems"], ffn["lands"], [after], name="wffn_wait")
        return place_own(ffn["own"], lands, DFF)

    rs, outs = {}, {}

    def after_first_grads(bf):
        rs["first"], tok_first = _exchange_start([b.reshape(NDEV, b.shape[0] // NDEV, D) for b in bf], scatter=True,
                                                 name="rs_first_start")
        return tok_first[0, 0]

    def small_path(pack, dsgw):
        rs["small"], _ = _exchange_start([pack, dsgw.reshape(G * C, C)], scatter=False, name="small_start")
        return [rs["small"][2], rs["small"][3]]

    def after_in_half(j, half_bf16):
        rs["in%d" % j], _ = _exchange_start([half_bf16.reshape(NDEV, INC // NDEV, D // 2)], scatter=True,
                                            name="rs_in%d_start" % j)
        return [rs["in%d" % j][2]]

    gx, grads, pack, dsgw = _local_step(
        x[0], loss_target[0], ctx[0], mod6, cmod6, norm1, norm2[0:1], norm_f.reshape(1, D), win_t, wout, ffn_weights,
        sg_w[0], sg_b[0], sg_gain, ret_logit_f, ret_logit_b,
        tm_a=tm, tm_b=tm, tm_f=min(256, t_len), tm_m=min(1024, t_len), tm_r=min(1024, t_len), tm_i=tm,
        bt_w=min(1024, t_len),
        after_first_grads=after_first_grads, small_path=small_path, after_in_half=after_in_half)

    me_arr = me.reshape(1).astype(jnp.int32)
    (pack_own, sgw_own), (pack_land, sgw_land) = _exchange_wait(rs["small"], [rs["in1"][2]], scatter=False,
                                                               name="small_wait")
    psum_rows, pall, sgw_sum, part = _small_sum(me_arr, pack_own, pack_land, sgw_own, sgw_land, w_mod[0])
    h_cc, _ = _exchange_start([part], scatter=False, name="cctx_start")

    dmod_rows = (ROW_I + 0, ROW_I + 1, ROW_G1, ROW_F + 2, ROW_F + 3, ROW_F + 1)
    dmod_all = jnp.concatenate([pall[:, r, :] for r in dmod_rows], axis=1)
    dcmod = jnp.concatenate([psum_rows[ROW_C + 0:ROW_C + 1], psum_rows[ROW_C + 1:ROW_C + 2],
                             jnp.zeros((1, 4 * D), F32)], axis=1)
    dmod_mat = jnp.concatenate([dmod_all, jnp.pad(dcmod, ((0, 7), (0, 0)))], axis=0)
    ncol = 6 * D // NDEV
    dmod_cols = lax.dynamic_slice(dmod_mat, (0, me * ncol), (16, ncol))
    g_wm, d_wm, m_wm, v_wm = _wmod_grad(cs_all, dmod_cols, w_mod[0], m_w_mod[0], v_w_mod[0])
    outs["w_mod"] = (g_wm[None], d_wm[None], m_wm[None], v_wm[None])
    small_params = {
        "norm1": (norm1, m_norm1, v_norm1), "norm2": (norm2, m_norm2, v_norm2),
        "norm_f": tuple(a.reshape(1, D) for a in (norm_f, m_norm_f, v_norm_f)),
        "sg_gain": (sg_gain, m_sg_gain, v_sg_gain), "sg_b": (sg_b[0], m_sg_b[0], v_sg_b[0]),
        "ret_logit_f": (ret_logit_f, m_ret_logit_f, v_ret_logit_f),
        "ret_logit_b": (ret_logit_b, m_ret_logit_b, v_ret_logit_b),
        "b_mod": (b_mod, m_b_mod, v_b_mod), "sg_w": (sg_w[0], m_sg_w[0], v_sg_w[0])}
    small, loss = _adam_small(psum_rows, sgw_sum.reshape(G, C, C), small_params)
    back = {"norm_f": lambda a: a.reshape(D), "sg_b": lambda a: a[None], "sg_w": lambda a: a[None]}
    for name, vals in small.items():
        outs[name] = tuple(back.get(name, lambda a: a)(a) for a in vals)

    _, lands_first = _exchange_wait(rs["first"], [g_wm], scatter=True, name="rs_first_wait")

    def finish(name, fulls, lands, w, m, v, transposed):
        take = tr if transposed else (lambda a: a[0])
        res = _rs_adam(me_arr, fulls, lands, take(w), take(m), take(v), name="adam_" + name)
        put = (lambda a: jnp.transpose(a)[None]) if transposed else (lambda a: a[None])
        outs[name] = tuple(put(a) for a in res)
        return res[0]

    g_down = finish("w_down", [grads["w_down"][0]], [lands_first[0]], w_down, m_w_down, v_w_down, False)
    g_gate = finish("w_gate", [grads["w_gate"][0]], [lands_first[1]], w_gate, m_w_gate, v_w_gate, True)
    g_up = finish("w_up", [grads["w_up"][0]], [lands_first[2]], w_up, m_w_up, v_w_up, True)
    g_out = finish("w_out", [grads["w_out"][0]], [lands_first[3]], w_out, m_w_out, v_w_out, False)
    done = [g_down, g_gate, g_up, g_out]
    (part_own,), (part_land,) = _exchange_wait(h_cc, done, scatter=False, name="cctx_wait")
    res_cc = _cctx_final(me_arr, part_own, part_land, cctx_row, m_c_ctx.reshape(1, D), v_c_ctx.reshape(1, D))
    outs["c_ctx"] = tuple(a.reshape(D) for a in res_cc)
    lands_in = [_exchange_wait(rs["in%d" % j], done, scatter=True, name="rs_in%d_wait" % j)[1][0] for j in range(2)]
    finish("w_in", [h[0] for h in grads["w_in"]], lands_in, w_in, m_w_in, v_w_in, True)

    order = ["c_ctx", "w_mod", "b_mod", "norm1", "w_in", "sg_gain", "sg_w", "sg_b", "ret_logit_f", "ret_logit_b",
             "w_out", "norm2", "w_gate", "w_up", "w_down", "norm_f"]
    res = [loss.reshape(()), gx[None]]
    for j in range(4):
        res += [outs[name][j] for name in order]
    return tuple(res)
```

```python
import functools
import math

import numpy as np
import jax
import jax.numpy as jnp
from jax import lax
from jax.experimental import pallas as pl
from jax.experimental.pallas import tpu as pltpu

F32 = jnp.float32
BF16 = jnp.bfloat16

D = 1024
AW = 512
RW = 512
H = 4
DH = 128
G = 4
C = 128
CR = 256
DFF = 2816
INC = 3584
Q_LO = 2 * AW
KV_LO, VR_LO, G_LO = Q_LO + RW, Q_LO + 2 * RW, Q_LO + 3 * RW
KV_HI = G_LO
NDEV = 8
EPS = 1e-6
KSCALE = DH ** -0.5
ROPE_BASE = 10000.0
GRID_W = 64

ADAM_LR = 0.001
ADAM_B1 = 0.9
ADAM_B2 = 0.999
ADAM_EPS = 1e-08
ADAM_WD = 0.01
ADAM_STEP = 10

VMEM_LIMIT = 56 * 1024 * 1024
MESH = pl.DeviceIdType.MESH


def _cparams(n_grid=0, **kw):
    sem = ("arbitrary",) * n_grid if n_grid else None
    return pltpu.CompilerParams(dimension_semantics=sem, vmem_limit_bytes=VMEM_LIMIT, **kw)


def _dot(a, b):
    return jnp.dot(a, b, preferred_element_type=F32)


def _dot_nt(a, b):
    return lax.dot_general(a, b, (((1,), (1,)), ((), ())), preferred_element_type=F32)


def _dot_tn(a, b):
    return lax.dot_general(a, b, (((0,), (0,)), ((), ())), preferred_element_type=F32)


def _whole(shape):
    nd = len(shape)
    return pl.BlockSpec(shape, lambda *_: (0,) * nd, pipeline_mode=pl.Buffered(1))


def _acc(shape):
    nd = len(shape)
    return pl.BlockSpec(shape, lambda *_: (0,) * nd)


def _rows(tm, n):
    return pl.BlockSpec((tm, n), lambda i: (i, 0))


def _rows_rev(tm, n, nt):
    return pl.BlockSpec((tm, n), lambda i: (nt - 1 - i, 0))


def _cols(n, tm):
    return pl.BlockSpec((n, tm), lambda i: (0, i))


def _sigmoid(x):
    return 1.0 / (1.0 + jnp.exp(-x))


def _log_sigmoid(x):
    return jnp.minimum(x, 0.0) - jnp.log(1.0 + jnp.exp(-jnp.abs(x)))


_GK0 = math.sqrt(2.0 / math.pi)
_GK1 = 0.044715


def _gelu(x):
    x2 = x * x
    t = jnp.tanh(x * (_GK0 + (_GK0 * _GK1) * x2))
    h = 0.5 + 0.5 * t
    g = x * h
    dg = h + g * (1.0 - h) * ((2.0 * _GK0) + (6.0 * _GK0 * _GK1) * x2)
    return g, dg


def _silu_parts(a):
    s = _sigmoid(a)
    sa = a * s
    return sa, s + sa * (1.0 - s)


def _rope(t, cos, sins):
    n = t.shape[1]
    lane = lax.broadcasted_iota(jnp.int32, t.shape, 1)
    first = (lane & 63) < 32
    partner = jnp.where(first, pltpu.roll(t, n - 32, 1), pltpu.roll(t, 32, 1))
    return t * cos + partner * sins


def _headnorm(o):
    outs, rs = [], []
    for h in range(H):
        oh = o[:, h * DH:(h + 1) * DH]
        r = lax.rsqrt(jnp.mean(oh * oh, axis=-1, keepdims=True) + EPS)
        outs.append(oh * r)
        rs.append(r)
    return jnp.concatenate(outs, axis=1), rs


def _decay_consts(lg, forward):
    ii = lax.broadcasted_iota(jnp.int32, (CR, CR), 0).astype(F32)
    jj = lax.broadcasted_iota(jnp.int32, (CR, CR), 1).astype(F32)
    ic = lax.broadcasted_iota(jnp.int32, (CR, 1), 0).astype(F32)
    if forward:
        diff = ii - jj
        xi = jnp.exp(lg * (ic + 1.0))
        z = jnp.exp(lg * (CR - 1.0 - ic))
    else:
        diff = jj - ii
        xi = jnp.exp(lg * (CR - ic))
        z = jnp.exp(lg * ic)
    dm = jnp.where(diff >= 0.0, jnp.exp(lg * jnp.maximum(diff, 0.0)), 0.0)
    dec = jnp.exp(lg * float(CR))
    return dm, xi, z, dec, ic


def _ctx_fwd(ctx, cmod6, norm1, win, rlf, rlb):
    lc = ctx.shape[0]

    def body(ctx_ref, cmod_ref, n1_ref, win_ref, rlf_ref, rlb_ref, hc_ref, kvc_ref, scf_ref, scb_ref):
        x = ctx_ref[...]
        r = lax.rsqrt(jnp.mean(x * x, axis=-1, keepdims=True) + EPS)
        hc = (x * r) * (n1_ref[...] * (1.0 + cmod_ref[1:2, :])) + cmod_ref[0:1, :]
        hcb = hc.astype(BF16)
        hc_ref[...] = hcb
        kv = _dot_nt(hcb, win_ref[KV_LO:KV_HI, :])
        k = kv[:, :RW] * KSCALE
        v = kv[:, RW:]
        kvc_ref[:, :RW] = k
        kvc_ref[:, RW:] = v
        t = lax.broadcasted_iota(jnp.int32, (lc, 1), 0).astype(F32)
        lgf = _log_sigmoid(rlf_ref[...])
        lgb = _log_sigmoid(rlb_ref[...])
        for h in range(H):
            hs = slice(h * DH, (h + 1) * DH)
            wf = jnp.exp(lgf[h:h + 1, 0:1] * (lc - 1.0 - t))
            wb = jnp.exp(lgb[h:h + 1, 0:1] * t)
            vh = v[:, hs].astype(BF16)
            scf_ref[h] = _dot_tn((k[:, hs] * wf).astype(BF16), vh)
            scb_ref[h] = _dot_tn((k[:, hs] * wb).astype(BF16), vh)

    return pl.pallas_call(
        body, name="ctx_fwd",
        out_shape=(jax.ShapeDtypeStruct((lc, D), BF16), jax.ShapeDtypeStruct((lc, 2 * RW), F32),
                   jax.ShapeDtypeStruct((H, DH, DH), F32), jax.ShapeDtypeStruct((H, DH, DH), F32)),
        compiler_params=_cparams(),
    )(ctx, cmod6, norm1, win, rlf, rlb)


def _sg_forward(u, v, sgw_ref, sgbt_ref, sgg_ref, nc):
    ua, dgu = _gelu(u)
    va, dgv = _gelu(v)
    gain = sgg_ref[...]
    mixed, vn_b, vah_l, rv_l = [], [], [], []
    for g in range(G):
        gs = slice(g * DH, (g + 1) * DH)
        vag = va[:, gs]
        rv = lax.rsqrt(jnp.mean(vag * vag, axis=-1, keepdims=True) + EPS)
        vah = vag * rv
        vn = (vah * gain[:, gs]).astype(BF16)
        wg = sgw_ref[g].astype(BF16)
        bcol = sgbt_ref[g]
        rows = [_dot(wg, vn[c * C:(c + 1) * C, :]) + bcol for c in range(nc)]
        mixed.append(jnp.concatenate(rows, axis=0) if nc > 1 else rows[0])
        vn_b.append(vn)
        vah_l.append(vah)
        rv_l.append(rv)
    mixed = jnp.concatenate(mixed, axis=1)
    return ua * mixed, (ua, dgu, dgv, mixed, vn_b, vah_l, rv_l)


def _fwd_a(x, mod6, norm1, win, sgw, sgbt, sggain, cos, sin, rlf, scf, *, tm):
    t_len = x.shape[0]
    nt, nc, ncr = t_len // tm, tm // C, tm // CR

    def body(x_ref, mod_ref, n1_ref, win_ref, sgw_ref, sgbt_ref, sgg_ref, cos_ref, sin_ref, rlf_ref, scf_ref,
             hx_ref, zuv_ref, q_ref, k_ref, v_ref, gfb_ref, of_ref, ya_ref, sst_ref, s_scr):
        i = pl.program_id(0)

        @pl.when(i == 0)
        def _():
            s_scr[...] = scf_ref[...]

        x = x_ref[...]
        r = lax.rsqrt(jnp.mean(x * x, axis=-1, keepdims=True) + EPS)
        hx = (x * r) * (n1_ref[...] * (1.0 + mod_ref[1:2, :])) + mod_ref[0:1, :]
        hxb = hx.astype(BF16)
        hx_ref[...] = hxb
        z = _dot_nt(hxb, win_ref[...])
        zuv_ref[...] = z[:, :2 * AW].astype(BF16)
        gfb_ref[...] = z[:, G_LO:INC].astype(BF16)
        cos = jnp.tile(cos_ref[...], (1, H))
        sins = jnp.tile(sin_ref[...], (1, H))
        q_ref[...] = _rope(z[:, Q_LO:KV_LO], cos, sins).astype(BF16)
        k_ref[...] = (_rope(z[:, KV_LO:VR_LO], cos, sins) * KSCALE).astype(BF16)
        v_ref[...] = z[:, VR_LO:G_LO].astype(BF16)
        ya, _ = _sg_forward(z[:, :AW], z[:, AW:2 * AW], sgw_ref, sgbt_ref, sgg_ref, nc)
        ya_ref[...] = ya.astype(BF16)

        lg = _log_sigmoid(rlf_ref[...])
        for h in range(H):
            dm, xi, zc, dec, _ = _decay_consts(lg[h:h + 1, 0:1], True)
            hs = slice(h * DH, (h + 1) * DH)
            for c in range(ncr):
                rs = slice(c * CR, (c + 1) * CR)
                qc, kc, vc = q_ref[rs, hs], k_ref[rs, hs], v_ref[rs, hs]
                s = s_scr[h]
                sst_ref[c, h] = s
                a = (_dot_nt(qc, kc) * dm).astype(BF16)
                of_ref[rs, hs] = (_dot(a, vc) + xi * _dot(qc, s.astype(BF16))).astype(BF16)
                kz = (kc.astype(F32) * zc).astype(BF16)
                s_scr[h] = dec * s + _dot_tn(kz, vc)

    n_chunks = t_len // CR
    return pl.pallas_call(
        body, name="fwd_a", grid=(nt,),
        in_specs=[_rows(tm, D), _whole((6, D)), _whole((1, D)), _whole((INC, D)), _whole((G, C, C)),
                  _whole((G, C, 128)), _whole((1, AW)), _rows(tm, DH), _rows(tm, DH), _whole((H, 128)),
                  _whole((H, DH, DH))],
        out_specs=[_rows(tm, D), _rows(tm, 2 * AW), _rows(tm, RW), _rows(tm, RW), _rows(tm, RW),
                   _rows(tm, 2 * RW), _rows(tm, RW), _rows(tm, AW),
                   pl.BlockSpec((ncr, H, DH, DH), lambda i: (i, 0, 0, 0))],
        out_shape=(jax.ShapeDtypeStruct((t_len, D), BF16), jax.ShapeDtypeStruct((t_len, 2 * AW), BF16),
                   jax.ShapeDtypeStruct((t_len, RW), BF16), jax.ShapeDtypeStruct((t_len, RW), BF16),
                   jax.ShapeDtypeStruct((t_len, RW), BF16), jax.ShapeDtypeStruct((t_len, 2 * RW), BF16),
                   jax.ShapeDtypeStruct((t_len, RW), BF16), jax.ShapeDtypeStruct((t_len, AW), BF16),
                   jax.ShapeDtypeStruct((n_chunks, H, DH, DH), F32)),
        scratch_shapes=[pltpu.VMEM((H, DH, DH), F32)],
        compiler_params=_cparams(1),
    )(x, mod6, norm1, win, sgw, sgbt, sggain, cos, sin, rlf, scf)


def _fwd_b(q, k, v, of, gfb, ya, x, mod6, wout, rlb, scb, *, tm):
    t_len = x.shape[0]
    nt, nc = t_len // tm, tm // CR

    def body(q_ref, k_ref, v_ref, of_ref, gfb_ref, ya_ref, x_ref, mod_ref, wout_ref, rlb_ref, scb_ref,
             ob_ref, ycat_ref, y_ref, x1_ref, rst_ref, r_scr):
        i = pl.program_id(0)

        @pl.when(i == 0)
        def _():
            r_scr[...] = scb_ref[...]

        lg = _log_sigmoid(rlb_ref[...])
        for h in range(H):
            dm, xi, zc, dec, _ = _decay_consts(lg[h:h + 1, 0:1], False)
            hs = slice(h * DH, (h + 1) * DH)
            for c in reversed(range(nc)):
                rs = slice(c * CR, (c + 1) * CR)
                qc, kc, vc = q_ref[rs, hs], k_ref[rs, hs], v_ref[rs, hs]
                s = r_scr[h]
                rst_ref[c, h] = s
                a = (_dot_nt(qc, kc) * dm).astype(BF16)
                ob_ref[rs, hs] = (_dot(a, vc) + xi * _dot(qc, s.astype(BF16))).astype(BF16)
                kz = (kc.astype(F32) * zc).astype(BF16)
                r_scr[h] = dec * s + _dot_tn(kz, vc)

        gfb = gfb_ref[...].astype(F32)
        sf, _ = _silu_parts(gfb[:, :RW])
        sb, _ = _silu_parts(gfb[:, RW:])
        hnf, _ = _headnorm(of_ref[...].astype(F32))
        hnb, _ = _headnorm(ob_ref[...].astype(F32))
        yr = sf * hnf + sb * hnb
        ycat = jnp.concatenate([ya_ref[...], yr.astype(BF16)], axis=1)
        ycat_ref[...] = ycat
        y = _dot(ycat, wout_ref[...])
        y_ref[...] = y.astype(BF16)
        x1_ref[...] = x_ref[...] + mod_ref[2:3, :] * y

    n_chunks = t_len // CR
    rr = functools.partial(_rows_rev, nt=nt)
    return pl.pallas_call(
        body, name="fwd_b", grid=(nt,),
        in_specs=[rr(tm, RW), rr(tm, RW), rr(tm, RW), rr(tm, RW), rr(tm, 2 * RW), rr(tm, AW), rr(tm, D),
                  _whole((6, D)), _whole((D, D)), _whole((H, 128)), _whole((H, DH, DH))],
        out_specs=[rr(tm, RW), rr(tm, D), rr(tm, D), rr(tm, D),
                   pl.BlockSpec((nc, H, DH, DH), lambda i: (nt - 1 - i, 0, 0, 0))],
        out_shape=(jax.ShapeDtypeStruct((t_len, RW), BF16), jax.ShapeDtypeStruct((t_len, D), BF16),
                   jax.ShapeDtypeStruct((t_len, D), BF16), jax.ShapeDtypeStruct((t_len, D), F32),
                   jax.ShapeDtypeStruct((n_chunks, H, DH, DH), F32)),
        scratch_shapes=[pltpu.VMEM((H, DH, DH), F32)],
        compiler_params=_cparams(1),
    )(q, k, v, of, gfb, ya, x, mod6, wout, rlb, scb)


def _ffn(x1, tgt, mod6, norm2, normf, wg, wu, wd, *, tm):
    t_len = x1.shape[0]
    nt = t_len // tm

    def body(x1_ref, tgt_ref, mod_ref, n2_ref, nf_ref, wg_ref, wu_ref, wd_ref,
             dx1_ref, h2_ref, da_ref, db_ref, hm_ref, df_ref, small_ref):
        i = pl.program_id(0)

        @pl.when(i == 0)
        def _():
            small_ref[...] = jnp.zeros_like(small_ref)

        sh2, sc2, g2 = mod_ref[3:4, :], mod_ref[4:5, :], mod_ref[5:6, :]
        n2, nf = n2_ref[...], nf_ref[...]
        x1 = x1_ref[...]
        r2 = lax.rsqrt(jnp.mean(x1 * x1, axis=-1, keepdims=True) + EPS)
        x1h = x1 * r2
        w2 = n2 * (1.0 + sc2)
        h2b = (x1h * w2 + sh2).astype(BF16)
        h2_ref[...] = h2b
        a = _dot_nt(h2b, wg_ref[...])
        b = _dot_nt(h2b, wu_ref[...])
        sa, dsa = _silu_parts(a)
        hmb = (sa * b).astype(BF16)
        hm_ref[...] = hmb.T
        f = _dot(hmb, wd_ref[...])
        x2 = x1 + g2 * f
        r3 = lax.rsqrt(jnp.mean(x2 * x2, axis=-1, keepdims=True) + EPS)
        x2h = x2 * r3
        diff = x2h * nf - tgt_ref[...]
        small_ref[5:6, :] += jnp.sum(diff * diff, axis=0, keepdims=True)
        dout = diff * (1.0 / D)
        small_ref[0:1, :] += jnp.sum(dout * x2h, axis=0, keepdims=True)
        dyg = dout * nf
        dx2 = r3 * (dyg - x2h * jnp.mean(dyg * x2h, axis=-1, keepdims=True))
        small_ref[1:2, :] += jnp.sum(dx2 * f, axis=0, keepdims=True)
        dfb = (dx2 * g2).astype(BF16)
        df_ref[...] = dfb
        dhm = _dot_nt(dfb, wd_ref[...])
        dbb = (dhm * sa).astype(BF16)
        dab = (dhm * b * dsa).astype(BF16)
        da_ref[...] = dab.T
        db_ref[...] = dbb.T
        dh2 = _dot(dab, wg_ref[...]) + _dot(dbb, wu_ref[...])
        small_ref[2:3, :] += jnp.sum(dh2, axis=0, keepdims=True)
        dhx = dh2 * x1h
        small_ref[3:4, :] += jnp.sum(dhx, axis=0, keepdims=True) * n2
        small_ref[4:5, :] += jnp.sum(dhx, axis=0, keepdims=True) * (1.0 + sc2)
        dyg2 = dh2 * w2
        dx1_ref[...] = dx2 + r2 * (dyg2 - x1h * jnp.mean(dyg2 * x1h, axis=-1, keepdims=True))

    return pl.pallas_call(
        body, name="ffn", grid=(nt,),
        in_specs=[_rows(tm, D), _rows(tm, D), _whole((6, D)), _whole((1, D)), _whole((1, D)),
                  _whole((DFF, D)), _whole((DFF, D)), _whole((DFF, D))],
        out_specs=[_rows(tm, D), _rows(tm, D), _cols(DFF, tm), _cols(DFF, tm), _cols(DFF, tm), _rows(tm, D),
                   _acc((8, D))],
        out_shape=(jax.ShapeDtypeStruct((t_len, D), F32), jax.ShapeDtypeStruct((t_len, D), BF16),
                   jax.ShapeDtypeStruct((DFF, t_len), BF16), jax.ShapeDtypeStruct((DFF, t_len), BF16),
                   jax.ShapeDtypeStruct((DFF, t_len), BF16), jax.ShapeDtypeStruct((t_len, D), BF16),
                   jax.ShapeDtypeStruct((8, D), F32)),
        compiler_params=_cparams(1),
    )(x1, tgt, mod6, norm2, normf, wg, wu, wd)


def _mid_bwd(dx1, y, of, ob, gfb, mod6, wout, *, tm):
    t_len = dx1.shape[0]
    nt = t_len // tm

    def body(dx1_ref, y_ref, of_ref, ob_ref, gfb_ref, mod_ref, wout_ref,
             dy_ref, dya_ref, dgfb_ref, dof_ref, dob_ref, dg1_ref):
        i = pl.program_id(0)

        @pl.when(i == 0)
        def _():
            dg1_ref[...] = jnp.zeros_like(dg1_ref)

        dx1 = dx1_ref[...]
        dg1_ref[0:1, :] += jnp.sum(dx1 * y_ref[...].astype(F32), axis=0, keepdims=True)
        dyb = (dx1 * mod_ref[2:3, :]).astype(BF16)
        dy_ref[...] = dyb
        dycat = _dot_nt(dyb, wout_ref[...])
        dya_ref[...] = dycat[:, :AW].astype(BF16)
        dyr = dycat[:, AW:]
        gfb = gfb_ref[...].astype(F32)
        for o_ref, do_ref, lo in ((of_ref, dof_ref, 0), (ob_ref, dob_ref, RW)):
            sg, dsg = _silu_parts(gfb[:, lo:lo + RW])
            o = o_ref[...].astype(F32)
            hn, rs = _headnorm(o)
            dgfb_ref[:, lo:lo + RW] = (dyr * hn * dsg).astype(BF16)
            dhn = dyr * sg
            for h in range(H):
                hs = slice(h * DH, (h + 1) * DH)
                oh, dh = hn[:, hs], dhn[:, hs]
                do_ref[:, hs] = (rs[h] * (dh - oh * jnp.mean(dh * oh, axis=-1, keepdims=True))).astype(BF16)

    return pl.pallas_call(
        body, name="mid_bwd", grid=(nt,),
        in_specs=[_rows(tm, D), _rows(tm, D), _rows(tm, RW), _rows(tm, RW), _rows(tm, 2 * RW),
                  _whole((6, D)), _whole((D, D))],
        out_specs=[_rows(tm, D), _rows(tm, AW), _rows(tm, 2 * RW), _rows(tm, RW), _rows(tm, RW), _acc((8, D))],
        out_shape=(jax.ShapeDtypeStruct((t_len, D), BF16), jax.ShapeDtypeStruct((t_len, AW), BF16),
                   jax.ShapeDtypeStruct((t_len, 2 * RW), BF16), jax.ShapeDtypeStruct((t_len, RW), BF16),
                   jax.ShapeDtypeStruct((t_len, RW), BF16), jax.ShapeDtypeStruct((8, D), F32)),
        compiler_params=_cparams(1),
    )(dx1, y, of, ob, gfb, mod6, wout)


def _ret_bwd_dir(q_ref, k_ref, v_ref, do_ref, st_ref, dq_ref, dk_ref, dv_ref, ds_scr, dlg_scr, lg, forward, nc, row0):
    order = reversed(range(nc)) if forward else range(nc)
    order = list(order)
    for h in range(H):
        dm, xi, zc, dec, ic = _decay_consts(lg[h:h + 1, 0:1], forward)
        hs = slice(h * DH, (h + 1) * DH)
        if forward:
            w_qi, w_qc, w_ki, w_ks = ic, ic + 1.0, -ic, (CR - 1.0) - ic
        else:
            w_qi, w_qc, w_ki, w_ks = -ic, CR - ic, ic, ic
        acc = jnp.zeros((1, DH), F32)
        for c in order:
            rs = slice(c * CR, (c + 1) * CR)
            qc, kc, vc, doc = q_ref[rs, hs], k_ref[rs, hs], v_ref[rs, hs], do_ref[rs, hs]
            s = st_ref[c, h]
            dsn = ds_scr[h]
            sb, dsnb = s.astype(BF16), dsn.astype(BF16)
            qf, kf = qc.astype(F32), kc.astype(F32)
            a = (_dot_nt(qc, kc) * dm).astype(BF16)
            da = (_dot_nt(doc, vc) * dm).astype(BF16)
            xdo = (xi * doc.astype(F32)).astype(BF16)
            kz = (kf * zc).astype(BF16)
            vz = (vc.astype(F32) * zc).astype(BF16)
            dq_i = _dot(da, kc)
            dq_c = _dot_nt(xdo, sb)
            dk_i = _dot_tn(da, qc)
            dk_s = _dot_nt(vz, dsnb)
            dq_ref[rs, hs] = (dq_i + dq_c).astype(BF16)
            dk_ref[rs, hs] = (dk_i + dk_s).astype(BF16)
            dv_ref[rs, hs] = (_dot_tn(a, doc) + _dot(kz, dsnb)).astype(BF16)
            rowterm = qf * (w_qi * dq_i + w_qc * dq_c) + kf * (w_ki * dk_i + w_ks * dk_s)
            acc = acc + jnp.sum(rowterm, axis=0, keepdims=True)
            acc = acc + (float(CR) * dec) * jnp.sum(s * dsn, axis=0, keepdims=True)
            ds_scr[h] = _dot_tn((xi * qf).astype(BF16), doc) + dec * dsn
        dlg_scr[row0 + h:row0 + h + 1, :] += acc


def _ret_bwd(q, k, v, dof, dob, sst, rst, rlf, rlb, *, tm):
    t_len = q.shape[0]
    nt, nc = t_len // tm, tm // CR

    def body(qf_ref, kf_ref, vf_ref, dof_ref, sst_ref, qb_ref, kb_ref, vb_ref, dob_ref, rst_ref, rlf_ref, rlb_ref,
             dqf_ref, dkf_ref, dvf_ref, dqb_ref, dkb_ref, dvb_ref, dscf_ref, dscb_ref, dlg_ref,
             dsf_scr, dsb_scr, dlg_scr):
        i = pl.program_id(0)

        @pl.when(i == 0)
        def _():
            dsf_scr[...] = jnp.zeros_like(dsf_scr)
            dsb_scr[...] = jnp.zeros_like(dsb_scr)
            dlg_scr[...] = jnp.zeros_like(dlg_scr)

        lgf = _log_sigmoid(rlf_ref[...])
        lgb = _log_sigmoid(rlb_ref[...])
        _ret_bwd_dir(qf_ref, kf_ref, vf_ref, dof_ref, sst_ref, dqf_ref, dkf_ref, dvf_ref, dsf_scr, dlg_scr, lgf, True, nc, 0)
        _ret_bwd_dir(qb_ref, kb_ref, vb_ref, dob_ref, rst_ref, dqb_ref, dkb_ref, dvb_ref, dsb_scr, dlg_scr, lgb, False, nc, H)

        @pl.when(i == nt - 1)
        def _():
            dscf_ref[...] = dsf_scr[...]
            dscb_ref[...] = dsb_scr[...]
            tot = jnp.broadcast_to(jnp.sum(dlg_scr[...], axis=1, keepdims=True), (8, 128))
            ri = lax.broadcasted_iota(jnp.int32, (8, 128), 0)
            li = lax.broadcasted_iota(jnp.int32, (8, 128), 1)
            dlg_ref[...] = jnp.zeros_like(dlg_ref)
            dlg_ref[0:1, 0:128] = jnp.sum(jnp.where(ri == li, tot, 0.0), axis=0, keepdims=True)
            dlg_ref[1:2, 0:128] = jnp.sum(jnp.where(ri - H == li, tot, 0.0), axis=0, keepdims=True)

    rr = functools.partial(_rows_rev, nt=nt)
    st_f = pl.BlockSpec((nc, H, DH, DH), lambda i: (nt - 1 - i, 0, 0, 0))
    st_b = pl.BlockSpec((nc, H, DH, DH), lambda i: (i, 0, 0, 0))
    tok = jax.ShapeDtypeStruct((t_len, RW), BF16)
    st = jax.ShapeDtypeStruct((H, DH, DH), F32)
    return pl.pallas_call(
        body, name="ret_bwd", grid=(nt,),
        in_specs=[rr(tm, RW), rr(tm, RW), rr(tm, RW), rr(tm, RW), st_f,
                  _rows(tm, RW), _rows(tm, RW), _rows(tm, RW), _rows(tm, RW), st_b,
                  _whole((H, 128)), _whole((H, 128))],
        out_specs=[rr(tm, RW), rr(tm, RW), rr(tm, RW), _rows(tm, RW), _rows(tm, RW), _rows(tm, RW),
                   _acc((H, DH, DH)), _acc((H, DH, DH)), _acc((8, D))],
        out_shape=(tok, tok, tok, tok, tok, tok, st, st, jax.ShapeDtypeStruct((8, D), F32)),
        scratch_shapes=[pltpu.VMEM((H, DH, DH), F32), pltpu.VMEM((H, DH, DH), F32), pltpu.VMEM((8, 128), F32)],
        compiler_params=_cparams(1),
    )(q, k, v, dof, sst, q, k, v, dob, rst, rlf, rlb)


def _in_bwd(x, zuv, dya, dqf, dkf, dvf, dqb, dkb, dvb, dgfb, dx1, cos, sin, mod6, norm1, win, sgw, sgbt, sggain, *, tm):
    t_len = x.shape[0]
    nt, nc = t_len // tm, tm // C

    def body(x_ref, zuv_ref, dya_ref, dqf_ref, dkf_ref, dvf_ref, dqb_ref, dkb_ref, dvb_ref, dgfb_ref, dx1_ref,
             cos_ref, sin_ref, mod_ref, n1_ref, win_ref, sgw_ref, sgbt_ref, sgg_ref,
             gx_ref, dz_ref, small_ref, dsgw_ref, dsgbt_ref):
        i = pl.program_id(0)

        @pl.when(i == 0)
        def _():
            small_ref[...] = jnp.zeros_like(small_ref)
            dsgw_ref[...] = jnp.zeros_like(dsgw_ref)
            dsgbt_ref[...] = jnp.zeros_like(dsgbt_ref)

        zuv = zuv_ref[...].astype(F32)
        _, (ua, dgu, dgv, mixed, vn_b, vah_l, rv_l) = _sg_forward(zuv[:, :AW], zuv[:, AW:], sgw_ref, sgbt_ref, sgg_ref, nc)
        dya = dya_ref[...].astype(F32)
        dz_ref[:, 0:AW] = (dya * mixed * dgu).astype(BF16)
        dmixed = dya * ua
        gain = sgg_ref[...]
        for g in range(G):
            gs = slice(g * DH, (g + 1) * DH)
            dmg = dmixed[:, gs]
            dmb = dmg.astype(BF16)
            wgt = sgw_ref[g].astype(BF16)
            dsw = jnp.zeros((C, C), F32)
            dsb = jnp.zeros((C, DH), F32)
            rows = []
            for c in range(nc):
                rs = slice(c * C, (c + 1) * C)
                dsw = dsw + _dot_nt(dmb[rs, :], vn_b[g][rs, :])
                dsb = dsb + dmg[rs, :]
                rows.append(_dot_tn(wgt, dmb[rs, :]))
            dsgw_ref[g] += dsw
            dsgbt_ref[g] += dsb
            dvn = jnp.concatenate(rows, axis=0) if nc > 1 else rows[0]
            vah, rv = vah_l[g], rv_l[g]
            small_ref[3:4, gs] += jnp.sum(dvn * vah, axis=0, keepdims=True)
            dyg = dvn * gain[:, gs]
            dva = rv * (dyg - vah * jnp.mean(dyg * vah, axis=-1, keepdims=True))
            dz_ref[:, AW + g * DH:AW + (g + 1) * DH] = (dva * dgv[:, gs]).astype(BF16)

        cos = jnp.tile(cos_ref[...], (1, H))
        nsins = -jnp.tile(sin_ref[...], (1, H))
        def both(f_ref, b_ref):
            return f_ref[...].astype(F32) + b_ref[...].astype(F32)

        dz_ref[:, Q_LO:KV_LO] = _rope(both(dqf_ref, dqb_ref), cos, nsins).astype(BF16)
        dz_ref[:, KV_LO:VR_LO] = (_rope(both(dkf_ref, dkb_ref), cos, nsins) * KSCALE).astype(BF16)
        dz_ref[:, VR_LO:G_LO] = both(dvf_ref, dvb_ref).astype(BF16)
        dz_ref[:, G_LO:INC] = dgfb_ref[...]

        dhx = _dot(dz_ref[...], win_ref[...])
        x = x_ref[...]
        r = lax.rsqrt(jnp.mean(x * x, axis=-1, keepdims=True) + EPS)
        xh = x * r
        n1, sc1 = n1_ref[...], mod_ref[1:2, :]
        small_ref[0:1, :] += jnp.sum(dhx, axis=0, keepdims=True)
        dhxx = jnp.sum(dhx * xh, axis=0, keepdims=True)
        small_ref[1:2, :] += dhxx * n1
        small_ref[2:3, :] += dhxx * (1.0 + sc1)
        dyg = dhx * (n1 * (1.0 + sc1))
        gx_ref[...] = dx1_ref[...] + r * (dyg - xh * jnp.mean(dyg * xh, axis=-1, keepdims=True))

        @pl.when(i == nt - 1)
        def _():
            ri = lax.broadcasted_iota(jnp.int32, (C, DH), 0)
            li = lax.broadcasted_iota(jnp.int32, (C, DH), 1)
            for g in range(G):
                tot = jnp.broadcast_to(jnp.sum(dsgbt_ref[g], axis=1, keepdims=True), (C, DH))
                small_ref[4 + g:5 + g, 0:DH] = jnp.sum(jnp.where(ri == li, tot, 0.0), axis=0, keepdims=True)

    tok = functools.partial(_rows, tm)
    return pl.pallas_call(
        body, name="in_bwd", grid=(nt,),
        in_specs=[tok(D), tok(2 * AW), tok(AW), tok(RW), tok(RW), tok(RW), tok(RW), tok(RW), tok(RW),
                  tok(2 * RW), tok(D), tok(DH), tok(DH), _whole((6, D)), _whole((1, D)), _whole((INC, D)),
                  _whole((G, C, C)), _whole((G, C, 128)), _whole((1, AW))],
        out_specs=[tok(D), tok(INC), _acc((8, D)), _acc((G, C, C))],
        out_shape=(jax.ShapeDtypeStruct((t_len, D), F32), jax.ShapeDtypeStruct((t_len, INC), BF16),
                   jax.ShapeDtypeStruct((8, D), F32), jax.ShapeDtypeStruct((G, C, C), F32)),
        scratch_shapes=[pltpu.VMEM((G, C, 128), F32)],
        compiler_params=_cparams(1),
    )(x, zuv, dya, dqf, dkf, dvf, dqb, dkb, dvb, dgfb, dx1, cos, sin, mod6, norm1, win, sgw, sgbt, sggain)


def _tn_matmul(at, b, *, bm, bt, name, init=None, init_rows=None, after=(), cols=None, transposed=True):
    m, t_len = at.shape if transposed else at.shape[::-1]
    cj, n = (0, b.shape[1]) if cols is None else cols
    ni, ntt = m // bm, t_len // bt
    has_init = init is not None

    def body(*refs):
        o_ref, ob_ref = refs[-2:]
        a_ref, b_ref = refs[:2]
        init_ref = refs[2] if has_init else None
        i, t = pl.program_id(0), pl.program_id(1)

        @pl.when(t == 0)
        def _():
            o_ref[...] = jnp.zeros_like(o_ref)

        if has_init:
            for ib in range(ni):
                lo, hi = max(init_rows[0], ib * bm), min(init_rows[1], (ib + 1) * bm)
                if lo < hi:
                    @pl.when(jnp.logical_and(t == 0, i == ib))
                    def _(lo=lo, hi=hi, ib=ib):
                        o_ref[lo - ib * bm:hi - ib * bm, :] = init_ref[lo - init_rows[0]:hi - init_rows[0], :]

        o_ref[...] += (_dot if transposed else _dot_tn)(a_ref[...], b_ref[...])

        @pl.when(t == ntt - 1)
        def _():
            ob_ref[...] = o_ref[...].astype(BF16)

    a_spec = pl.BlockSpec((bm, bt), lambda i, t: (i, t)) if transposed else pl.BlockSpec((bt, bm), lambda i, t: (t, i))
    in_specs = [a_spec, pl.BlockSpec((bt, n), lambda i, t: (t, cj))]
    args = [at, b]
    if has_init:
        in_specs.append(pl.BlockSpec((init.shape[0], n), lambda i, t: (0, cj), pipeline_mode=pl.Buffered(1)))
        args.append(init)
    for arr in after:
        in_specs.append(pl.BlockSpec(memory_space=pl.ANY))
        args.append(arr)
    out_spec = pl.BlockSpec((bm, n), lambda i, t: (i, 0))
    return pl.pallas_call(
        body, name=name, grid=(ni, ntt),
        in_specs=in_specs,
        out_specs=[out_spec, out_spec],
        out_shape=(jax.ShapeDtypeStruct((m, n), F32), jax.ShapeDtypeStruct((m, n), BF16)),
        compiler_params=_cparams(2),
    )(*args)


def _ctx_bwd(ctx, hc, kvc, cmod6, norm1, win, rlf, rlb, dscf, dscb, dlg_in):
    lc = ctx.shape[0]

    def body(ctx_ref, hc_ref, kvc_ref, cmod_ref, n1_ref, win_ref, rlf_ref, rlb_ref, dscf_ref, dscb_ref, dlgin_ref,
             dwin_ref, small_ref, dlg_ref):
        k = kvc_ref[:, :RW]
        v = kvc_ref[:, RW:]
        t = lax.broadcasted_iota(jnp.int32, (lc, 1), 0).astype(F32)
        lgf = _log_sigmoid(rlf_ref[...])
        lgb = _log_sigmoid(rlb_ref[...])
        dks, dvs = [], []
        lane = lax.broadcasted_iota(jnp.int32, (1, 128), 1)
        row_f = jnp.zeros((1, 128), F32)
        row_b = jnp.zeros((1, 128), F32)
        for h in range(H):
            hs = slice(h * DH, (h + 1) * DH)
            wf = jnp.exp(lgf[h:h + 1, 0:1] * (lc - 1.0 - t))
            wb = jnp.exp(lgb[h:h + 1, 0:1] * t)
            kh, vhb = k[:, hs], v[:, hs].astype(BF16)
            dsf, dsb = dscf_ref[h].astype(BF16), dscb_ref[h].astype(BF16)
            dkf = wf * _dot_nt(vhb, dsf)
            dkb = wb * _dot_nt(vhb, dsb)
            dvs.append(_dot((kh * wf).astype(BF16), dsf) + _dot((kh * wb).astype(BF16), dsb))
            dks.append((dkf + dkb) * KSCALE)
            lf = jnp.sum((lc - 1.0 - t) * kh * dkf, axis=0, keepdims=True)
            lb = jnp.sum(t * kh * dkb, axis=0, keepdims=True)
            row_f = row_f + jnp.where(lane == h, jnp.sum(lf, axis=1, keepdims=True), 0.0)
            row_b = row_b + jnp.where(lane == h, jnp.sum(lb, axis=1, keepdims=True), 0.0)
        dlg_ref[...] = dlgin_ref[...]
        dlg_ref[0:1, 0:128] += row_f
        dlg_ref[1:2, 0:128] += row_b
        dkv = jnp.concatenate(dks + dvs, axis=1).astype(BF16)
        dhc = _dot(dkv, win_ref[KV_LO:KV_HI, :])
        dwin_ref[...] = _dot_tn(dkv, hc_ref[...])
        x = ctx_ref[...]
        r = lax.rsqrt(jnp.mean(x * x, axis=-1, keepdims=True) + EPS)
        xh = x * r
        small_ref[...] = jnp.zeros_like(small_ref)
        small_ref[0:1, :] = jnp.sum(dhc, axis=0, keepdims=True)
        dhxx = jnp.sum(dhc * xh, axis=0, keepdims=True)
        small_ref[1:2, :] = dhxx * n1_ref[...]
        small_ref[2:3, :] = dhxx * (1.0 + cmod_ref[1:2, :])

    return pl.pallas_call(
        body, name="ctx_bwd",
        out_shape=(jax.ShapeDtypeStruct((2 * RW, D), F32), jax.ShapeDtypeStruct((8, D), F32),
                   jax.ShapeDtypeStruct((8, D), F32)),
        compiler_params=_cparams(),
    )(ctx, hc, kvc, cmod6, norm1, win, rlf, rlb, dscf, dscb, dlg_in)


def _adam_math(w, g, m, v):
    m = ADAM_B1 * m + (1.0 - ADAM_B1) * g
    v = ADAM_B2 * v + (1.0 - ADAM_B2) * (g * g)
    m_hat = m / (1.0 - ADAM_B1 ** ADAM_STEP)
    v_hat = v / (1.0 - ADAM_B2 ** ADAM_STEP)
    delta = -ADAM_LR * (m_hat / (jnp.sqrt(v_hat) + ADAM_EPS) + ADAM_WD * w)
    return delta, m, v


def _place():
    x, y, c = lax.axis_index("x"), lax.axis_index("y"), lax.axis_index("c")
    return x, y, c, 4 * x + 2 * y + c


def _flip(x, y, c, k):
    px = 1 - x if (k >> 2) & 1 else x
    py = 1 - y if (k >> 1) & 1 else y
    pc = 1 - c if k & 1 else c
    return (px, py, pc), 4 * px + 2 * py + pc


def _gather_copy(buf_ref, send_sems, recv_sems, sem0, k, mine):
    x, y, c, me = _place()
    dev, idx = _flip(x, y, c, k)
    blk = me if mine else idx
    return pltpu.make_async_remote_copy(
        src_ref=buf_ref.at[blk], dst_ref=buf_ref.at[blk],
        send_sem=send_sems.at[sem0 + k - 1], recv_sem=recv_sems.at[sem0 + k - 1],
        device_id=dev, device_id_type=MESH)


def _entry_gather(c_row, cctx_row, wmod, bmod, shards):
    n = len(shards)
    ncol = wmod.shape[1]

    def body(*refs):
        c_ref, cctx_ref, wmod_ref, bmod_ref = refs[:4]
        in_refs = refs[4:4 + n]
        mod_ref, cs_ref = refs[4 + n:6 + n]
        out_refs = refs[6 + n:6 + 2 * n]
        cbuf, pbuf = refs[6 + 2 * n:8 + 2 * n]
        cast_refs = refs[8 + 2 * n:8 + 3 * n]
        small_send, small_recv, send_sems, recv_sems, local_sems = refs[8 + 3 * n:]
        x, y, c, me = _place()
        sib = (x, y, 1 - c)
        chips = [(1 - x, y), (x, 1 - y), (1 - x, 1 - y)]

        cbuf[me] = jnp.broadcast_to(c_ref[...], (8, D))
        small = [_gather_copy(cbuf, small_send, small_recv, 0, k, True) for k in range(1, NDEV)]
        for cp in small:
            cp.start()

        def blk(px, py, pc):
            return 4 * px + 2 * py + pc

        def copy(w, k, block, to, src=None):
            dst = out_refs[w].at[block]
            return pltpu.make_async_remote_copy(
                src_ref=dst if src is None else src, dst_ref=dst,
                send_sem=send_sems.at[w, k], recv_sem=recv_sems.at[w, k], device_id=to, device_id_type=MESH)

        first, passed, mine = [], [], []
        for w in range(n):
            cast_refs[w][...] = in_refs[w][...].astype(BF16)
            m = pltpu.make_async_copy(cast_refs[w], out_refs[w].at[me], local_sems.at[w])
            m.start()
            mine.append(m)
            cps = [copy(w, 0, me, sib, src=cast_refs[w])]
            cps += [copy(w, 1 + j, me, (*chip, c), src=cast_refs[w]) for j, chip in enumerate(chips)]
            for cp in cps:
                cp.start()
            first += cps

        for k in range(1, NDEV):
            _gather_copy(cbuf, small_send, small_recv, 0, k, False).wait_recv()
        row = lax.broadcasted_iota(jnp.int32, (16, D), 0)
        call = jnp.where(row == 8, jnp.broadcast_to(cctx_ref[...], (16, D)), 0.0)
        for d in range(NDEV):
            call = jnp.where(row == d, jnp.concatenate([cbuf[d], cbuf[d]], axis=0), call)
        cs = call * _sigmoid(call)
        cs_ref[...] = cs
        pbuf[me] = _dot(cs.astype(BF16), wmod_ref[...].astype(BF16))
        small2 = [_gather_copy(pbuf, small_send, small_recv, NDEV - 1, k, True) for k in range(1, NDEV)]
        for cp in small2:
            cp.start()

        for w in range(n):
            for j, chip in enumerate(chips):
                b = blk(*chip, c)
                copy(w, 1 + j, b, (x, y, c)).wait_recv()
                fw = copy(w, 4 + j, b, sib)
                fw.start()
                passed.append(fw)
        for w in range(n):
            copy(w, 0, blk(x, y, 1 - c), (x, y, c)).wait_recv()
            for j, chip in enumerate(chips):
                copy(w, 4 + j, blk(*chip, 1 - c), (x, y, c)).wait_recv()

        for k in range(1, NDEV):
            _gather_copy(pbuf, small_send, small_recv, NDEV - 1, k, False).wait_recv()
        for d in range(NDEV):
            mod_ref[:, d * ncol:(d + 1) * ncol] = pbuf[d] + bmod_ref[:, d * ncol:(d + 1) * ncol]
        for cp in small + small2 + first + passed:
            cp.wait_send()
        for m in mine:
            m.wait()

    vm = pl.BlockSpec(memory_space=pltpu.VMEM)
    hbm = pl.BlockSpec(memory_space=pltpu.HBM)
    return pl.pallas_call(
        body, name="entry_gather",
        in_specs=[vm] * (4 + n), out_specs=[vm, vm] + [hbm] * n,
        out_shape=(jax.ShapeDtypeStruct((16, 6 * D), F32), jax.ShapeDtypeStruct((16, D), F32))
        + tuple(jax.ShapeDtypeStruct((NDEV,) + s.shape, BF16) for s in shards),
        scratch_shapes=[pltpu.VMEM((NDEV, 8, D), F32), pltpu.VMEM((NDEV, 16, ncol), F32)]
        + [pltpu.VMEM(s.shape, BF16) for s in shards]
        + [pltpu.SemaphoreType.DMA((2 * (NDEV - 1),)), pltpu.SemaphoreType.DMA((2 * (NDEV - 1),)),
           pltpu.SemaphoreType.DMA((n, 7)), pltpu.SemaphoreType.DMA((n, 7)), pltpu.SemaphoreType.DMA((n,))],
        compiler_params=_cparams(),
    )(c_row, cctx_row, wmod, bmod, *shards)


_HBM = pl.BlockSpec(memory_space=pltpu.HBM)
_SEMS = pl.BlockSpec(memory_space=pltpu.SEMAPHORE)
_EFFECT = pltpu.SideEffectType.DATAFLOW_SIDE_EFFECTING


def _exchange_copy(src_refs, land_refs, send_sems, recv_sems, w, k, scatter, landing_slot_is_mine):
    x, y, c, me = _place()
    dev, pidx = _flip(x, y, c, k)
    src = src_refs[w].at[pidx] if scatter else src_refs[w]
    slot = me if landing_slot_is_mine else pidx
    return pltpu.make_async_remote_copy(
        src_ref=src, dst_ref=land_refs[w].at[slot],
        send_sem=send_sems.at[w * (NDEV - 1) + k - 1], recv_sem=recv_sems.at[w * (NDEV - 1) + k - 1],
        device_id=dev, device_id_type=MESH)


def _exchange_start(srcs, *, scatter, name):
    n = len(srcs)
    lands = [lax.empty((NDEV,) + tuple(s.shape[-2:]), s.dtype) for s in srcs]

    def body(*refs):
        src_refs, land_refs = refs[:n], refs[n:2 * n]
        send_sems, recv_sems = refs[2 * n], refs[2 * n + 1]
        token = refs[-1]
        for w in range(n):
            for k in range(1, NDEV):
                _exchange_copy(src_refs, land_refs, send_sems, recv_sems, w, k, scatter, True).start()
        token[...] = jnp.zeros_like(token)

    arrs = list(srcs) + lands
    out_shape = ([pltpu.SemaphoreType.DMA((n * (NDEV - 1),)), pltpu.SemaphoreType.DMA((n * (NDEV - 1),))]
                 + [pltpu.HBM(a.shape, a.dtype) for a in arrs] + [jax.ShapeDtypeStruct((8, 128), F32)])
    res = pl.pallas_call(
        body, name=name, out_shape=tuple(out_shape),
        in_specs=[_HBM] * (2 * n),
        out_specs=tuple([_SEMS, _SEMS] + [_HBM] * (2 * n) + [pl.BlockSpec(memory_space=pltpu.VMEM)]),
        input_output_aliases={i: 2 + i for i in range(2 * n)},
        compiler_params=pltpu.CompilerParams(has_side_effects=_EFFECT),
    )(*[pltpu.with_memory_space_constraint(a, pltpu.HBM) for a in arrs])
    return res[:-1], res[-1]


def _exchange_wait(handles, after, *, scatter, name):
    n = (len(handles) - 2) // 2
    na = len(after)

    def body(*refs):
        src_refs, land_refs = refs[:n], refs[n:2 * n]
        send_sems, recv_sems = refs[2 * n], refs[2 * n + 1]
        for w in range(n):
            for k in range(1, NDEV):
                cp = _exchange_copy(src_refs, land_refs, send_sems, recv_sems, w, k, scatter, False)
                cp.wait_send()
                cp.wait_recv()

    arrs = handles[2:]
    res = pl.pallas_call(
        body, name=name, out_shape=tuple(pltpu.HBM(a.shape, a.dtype) for a in arrs),
        in_specs=[_HBM] * (2 * n) + [_SEMS, _SEMS] + [_HBM] * na,
        out_specs=tuple([_HBM] * (2 * n)),
        input_output_aliases={i: i for i in range(2 * n)},
        compiler_params=pltpu.CompilerParams(has_side_effects=_EFFECT),
    )(*arrs, handles[0], handles[1], *[pltpu.with_memory_space_constraint(a, pltpu.HBM) for a in after])
    return res[:n], res[n:]


_SAME_CORE = (2, 4, 6)


def _gather2_copy(src_ref, land_ref, send_sems, recv_sems, sem, k, block):
    x, y, c, _ = _place()
    dev, _ = _flip(x, y, c, k)
    dst = land_ref.at[block]
    return pltpu.make_async_remote_copy(
        src_ref=dst if src_ref is None else src_ref, dst_ref=dst,
        send_sem=send_sems.at[sem], recv_sem=recv_sems.at[sem], device_id=dev, device_id_type=MESH)


def _split_call(body, name, arrs, n_sems, sems=None, after=(), token=False):
    na = len(arrs)
    out_shape, out_specs = [], []
    if n_sems is not None:
        out_shape += [pltpu.SemaphoreType.DMA((n_sems,)), pltpu.SemaphoreType.DMA((n_sems,))]
        out_specs += [_SEMS, _SEMS]
    first = len(out_shape)
    out_shape += [pltpu.HBM(a.shape, a.dtype) for a in arrs]
    out_specs += [_HBM] * na
    if token:
        out_shape.append(jax.ShapeDtypeStruct((8, 128), F32))
        out_specs.append(pl.BlockSpec(memory_space=pltpu.VMEM))
    in_specs = [_HBM] * na + ([_SEMS, _SEMS] if sems is not None else []) + [_HBM] * len(after)
    args = [pltpu.with_memory_space_constraint(a, pltpu.HBM) for a in arrs] + list(sems or ()) \
        + [pltpu.with_memory_space_constraint(a, pltpu.HBM) for a in after]
    return pl.pallas_call(
        body, name=name, out_shape=tuple(out_shape), in_specs=in_specs, out_specs=tuple(out_specs),
        input_output_aliases={i: first + i for i in range(na)},
        compiler_params=pltpu.CompilerParams(has_side_effects=_EFFECT))(*args)


def _gather2_start(srcs, *, name):
    n = len(srcs)
    lands = [lax.empty((NDEV,) + tuple(s.shape), s.dtype) for s in srcs]

    def body(*refs):
        src_refs, land_refs = refs[:n], refs[n:2 * n]
        send_sems, recv_sems = refs[2 * n], refs[2 * n + 1]
        _, _, _, me = _place()
        for w in range(n):
            for slot, k in enumerate((1,) + _SAME_CORE):
                _gather2_copy(src_refs[w], land_refs[w], send_sems, recv_sems, 4 * w + slot, k, me).start()
        refs[-1][...] = jnp.zeros_like(refs[-1])

    res = _split_call(body, name, list(srcs) + lands, 4 * n, token=True)
    return res[:2], res[2:-1], res[-1]


def _gather2_arrived(sems, arrs, after, *, name):
    n = len(arrs) // 2

    def body(*refs):
        src_refs, land_refs = refs[:n], refs[n:2 * n]
        send_sems, recv_sems = refs[2 * n], refs[2 * n + 1]
        x, y, c, me = _place()
        for w in range(n):
            for slot, k in enumerate((1,) + _SAME_CORE):
                _, pidx = _flip(x, y, c, k)
                _gather2_copy(src_refs[w], land_refs[w], send_sems, recv_sems, 4 * w + slot, k, me).wait_send()
                _gather2_copy(None, land_refs[w], send_sems, recv_sems, 4 * w + slot, k, pidx).wait_recv()

    return _split_call(body, name, arrs, None, sems=sems, after=after)


def _gather2_forward(lands, *, name):
    n = len(lands)

    def body(*refs):
        land_refs = refs[:n]
        send_sems, recv_sems = refs[n], refs[n + 1]
        x, y, c, _ = _place()
        for w in range(n):
            for j, k in enumerate(_SAME_CORE):
                _, pidx = _flip(x, y, c, k)
                _gather2_copy(None, land_refs[w], send_sems, recv_sems, 3 * w + j, 1, pidx).start()
        refs[-1][...] = jnp.zeros_like(refs[-1])

    res = _split_call(body, name, list(lands), 3 * n, token=True)
    return res[:2], res[2:-1], res[-1]


def _gather2_wait(sems, lands, after, *, name):
    n = len(lands)

    def body(*refs):
        land_refs = refs[:n]
        send_sems, recv_sems = refs[n], refs[n + 1]
        x, y, c, _ = _place()
        for w in range(n):
            for j, k in enumerate(_SAME_CORE):
                _, mine = _flip(x, y, c, k)
                _, theirs = _flip(x, y, c, k ^ 1)
                _gather2_copy(None, land_refs[w], send_sems, recv_sems, 3 * w + j, 1, mine).wait_send()
                _gather2_copy(None, land_refs[w], send_sems, recv_sems, 3 * w + j, 1, theirs).wait_recv()

    return _split_call(body, name, list(lands), None, sems=sems, after=after)


def _cast_bf16(arrs, *, name, after=()):
    n = len(arrs)

    def body(*refs):
        for i_ref, o_ref in zip(refs[:n], refs[n + len(after):]):
            o_ref[...] = i_ref[...].astype(BF16)

    return pl.pallas_call(
        body, name=name, out_shape=tuple(jax.ShapeDtypeStruct(a.shape, BF16) for a in arrs),
        in_specs=[pl.BlockSpec(memory_space=pltpu.VMEM)] * n + [pl.BlockSpec(memory_space=pl.ANY)] * len(after),
        compiler_params=_cparams())(*arrs, *after)


def _rs_adam(me_arr, fulls, lands, w, m, v, *, name):
    rows = lands[0].shape[1]
    k = len(fulls)
    nb = next(n for n in (4, 2, 1) if rows % (16 * n) == 0)
    br = rows // nb

    def body(me_ref, *refs):
        own_refs, land_refs = refs[:k], refs[k:2 * k]
        w_ref, m_ref, v_ref, g_ref, d_ref, mo_ref, vo_ref = refs[2 * k:]
        me = me_ref[0]
        parts = []
        for own_ref, land_ref in zip(own_refs, land_refs):
            acc = jnp.zeros(own_ref.shape, F32)
            for p in range(NDEV):
                acc = acc + jnp.where(p == me, own_ref[...], land_ref[p].astype(F32))
            parts.append(acc)
        acc = parts[0] if k == 1 else jnp.concatenate(parts, axis=1)
        g_ref[...] = acc
        d_ref[...], mo_ref[...], vo_ref[...] = _adam_math(w_ref[...], acc, m_ref[...], v_ref[...])

    sh = jax.ShapeDtypeStruct((rows, D), F32)
    blk = pl.BlockSpec((br, D), lambda i, me: (i, 0))
    grid_spec = pltpu.PrefetchScalarGridSpec(
        num_scalar_prefetch=1, grid=(nb,),
        in_specs=[pl.BlockSpec((br, f.shape[1]), lambda i, me: (me[0] * nb + i, 0)) for f in fulls]
        + [pl.BlockSpec((NDEV, br, l.shape[2]), lambda i, me: (0, i, 0)) for l in lands]
        + [blk, blk, blk],
        out_specs=[blk] * 4)
    return pl.pallas_call(
        body, name=name, out_shape=(sh, sh, sh, sh), grid_spec=grid_spec, compiler_params=_cparams(1),
    )(me_arr, *fulls, *lands, w, m, v)


ROW_F, ROW_I, ROW_C, ROW_G1, ROW_LG = 0, 8, 16, 24, 32
PACK_ROWS = 40


def _small_sum(me_arr, pack, pack_land, sgw, sgw_land, wmod):
    ncol = wmod.shape[1]

    def body(me_ref, pack_ref, pland_ref, sgw_ref, sland_ref, wmod_ref, sum_ref, all_ref, sgw_sum_ref, part_ref):
        me = me_ref[0]
        acc = jnp.zeros((PACK_ROWS, D), F32)
        acc_w = jnp.zeros(sgw_ref.shape, F32)
        for p in range(NDEV):
            rows = jnp.where(p == me, pack_ref[...], pland_ref[p])
            all_ref[p] = rows
            acc = acc + rows
            acc_w = acc_w + jnp.where(p == me, sgw_ref[...], sland_ref[p])
        sum_ref[...] = acc
        sgw_sum_ref[...] = acc_w
        zero = jnp.zeros((1, D), F32)
        dcmod = jnp.concatenate([acc[ROW_C:ROW_C + 1], acc[ROW_C + 1:ROW_C + 2], zero, zero, zero, zero], axis=1)
        mine = jnp.zeros((1, ncol), F32)
        for d in range(NDEV):
            mine = mine + jnp.where(d == me, dcmod[:, d * ncol:(d + 1) * ncol], 0.0)
        part_ref[...] = _dot_nt(jnp.broadcast_to(mine, (8, ncol)).astype(BF16), wmod_ref[...].astype(BF16))

    vm = pl.BlockSpec(memory_space=pltpu.VMEM)
    return pl.pallas_call(
        body, name="small_sum",
        out_shape=(jax.ShapeDtypeStruct((PACK_ROWS, D), F32), jax.ShapeDtypeStruct((NDEV, PACK_ROWS, D), F32),
                   jax.ShapeDtypeStruct(sgw.shape, F32), jax.ShapeDtypeStruct((8, D), F32)),
        in_specs=[pl.BlockSpec(memory_space=pltpu.SMEM)] + [vm] * 5,
        compiler_params=_cparams(),
    )(me_arr, pack, pack_land, sgw, sgw_land, wmod)


def _cctx_final(me_arr, part, part_land, cctx_row, m_row, v_row):
    def body(me_ref, part_ref, land_ref, c_ref, m_ref, v_ref, g_ref, d_ref, mo_ref, vo_ref):
        me = me_ref[0]
        tot = jnp.zeros((8, D), F32)
        for p in range(NDEV):
            tot = tot + jnp.where(p == me, part_ref[...], land_ref[p])
        cc = c_ref[...]
        _, dsilu = _silu_parts(cc)
        g = tot[0:1, :] * dsilu
        g_ref[...] = g
        d_ref[...], mo_ref[...], vo_ref[...] = _adam_math(cc, g, m_ref[...], v_ref[...])

    row = jax.ShapeDtypeStruct((1, D), F32)
    vm = pl.BlockSpec(memory_space=pltpu.VMEM)
    return pl.pallas_call(
        body, name="cctx_final", out_shape=(row, row, row, row),
        in_specs=[pl.BlockSpec(memory_space=pltpu.SMEM)] + [vm] * 5,
        compiler_params=_cparams(),
    )(me_arr, part, part_land, cctx_row, m_row, v_row)


def _adam_small(s, sgw_g, params):
    names = ["norm1", "norm2", "norm_f", "sg_gain", "sg_b", "ret_logit_f", "ret_logit_b", "b_mod", "sg_w"]
    flat = [a for n in names for a in params[n]]

    def body(*refs):
        s_ref, sgw_ref = refs[0], refs[1]
        p_refs = refs[2:2 + 3 * len(names)]
        o_refs = refs[2 + 3 * len(names):]
        loss_ref = o_refs[-1]

        def row(r):
            return s_ref[r:r + 1, :]

        grads = {
            "norm1": row(ROW_I + 2) + row(ROW_C + 2),
            "norm2": row(ROW_F + 4),
            "norm_f": row(ROW_F + 0),
            "sg_gain": s_ref[ROW_I + 3:ROW_I + 4, 0:AW],
            "sg_b": s_ref[ROW_I + 4:ROW_I + 8, 0:DH],
            "sg_w": sgw_ref[...],
        }
        for j, n in enumerate(names):
            w_ref, m_ref, v_ref = p_refs[3 * j:3 * j + 3]
            g_ref, d_ref, mo_ref, vo_ref = o_refs[4 * j:4 * j + 4]
            w = w_ref[...]
            if n in ("ret_logit_f", "ret_logit_b"):
                r = ROW_LG + (0 if n == "ret_logit_f" else 1)
                g = s_ref[r:r + 1, 0:H] * _sigmoid(-w)
            elif n == "b_mod":
                rows = [row(ROW_I + 0) + row(ROW_C + 0), row(ROW_I + 1) + row(ROW_C + 1), row(ROW_G1),
                        row(ROW_F + 2), row(ROW_F + 3), row(ROW_F + 1)]
                g = jnp.concatenate(rows, axis=1)
            else:
                g = grads[n]
            g_ref[...] = g
            d_ref[...], mo_ref[...], vo_ref[...] = _adam_math(w, g, m_ref[...], v_ref[...])
        loss_ref[...] = jnp.full((1, 1), 0.5 / D, F32) * jnp.sum(row(ROW_F + 5), axis=1, keepdims=True)

    out_shape = []
    for n in names:
        w = params[n][0]
        out_shape += [jax.ShapeDtypeStruct(w.shape, F32)] * 4
    out_shape.append(jax.ShapeDtypeStruct((1, 1), F32))
    outs = pl.pallas_call(body, name="adam_small", out_shape=tuple(out_shape), compiler_params=_cparams())(
        s, sgw_g, *flat)
    return {n: tuple(outs[4 * j:4 * j + 4]) for j, n in enumerate(names)}, outs[-1]


def _wmod_grad(cs_all, dmod_cols, wmod, m, v):
    ncol = wmod.shape[1]

    def body(cs_ref, dm_ref, w_ref, m_ref, v_ref, g_ref, d_ref, mo_ref, vo_ref):
        g = _dot_tn(cs_ref[...].astype(BF16), dm_ref[...].astype(BF16))
        g_ref[...] = g
        d_ref[...], mo_ref[...], vo_ref[...] = _adam_math(w_ref[...], g, m_ref[...], v_ref[...])

    sh = jax.ShapeDtypeStruct((D, ncol), F32)
    br = D // 4
    blk = pl.BlockSpec((br, ncol), lambda i: (i, 0))
    return pl.pallas_call(
        body, name="wmod_grad", out_shape=(sh, sh, sh, sh), grid=(D // br,),
        in_specs=[pl.BlockSpec((cs_all.shape[0], br), lambda i: (0, i)), _whole(dmod_cols.shape), blk, blk, blk],
        out_specs=[blk] * 4,
        compiler_params=_cparams(1),
    )(cs_all, dmod_cols, wmod, m, v)


def _rope_tables(t_len):
    n_freq = DH // 4
    inv = (ROPE_BASE ** (-np.arange(n_freq, dtype=np.float32) / n_freq)).astype(np.float32)
    tok = np.arange(t_len)
    rows = (tok // GRID_W).astype(np.float32)
    cols = (tok % GRID_W).astype(np.float32)
    ang_r = rows[:, None] * inv[None, :]
    ang_c = cols[:, None] * inv[None, :]
    cos = np.concatenate([np.cos(ang_r)] * 2 + [np.cos(ang_c)] * 2, axis=1).astype(np.float32)
    sin = np.concatenate([-np.sin(ang_r), np.sin(ang_r), -np.sin(ang_c), np.sin(ang_c)], axis=1).astype(np.float32)
    return jnp.asarray(cos), jnp.asarray(sin)


def _local_step(x, tgt, ctx, mod6, cmod6, norm1, norm2, normf, win, wout, ffn_weights, sgw, sgb, sggain, rlf, rlb,
                *, tm_a, tm_b, tm_f, tm_m, tm_r, tm_i, bt_w, after_first_grads=None, small_path=None,
                after_in_half=None):
    t_len = x.shape[0]
    cos, sin = _rope_tables(t_len)
    sgbt = jnp.broadcast_to(sgb[:, :, None], (G, C, 128))
    rlf = jnp.broadcast_to(rlf.reshape(H, 1), (H, 128))
    rlb = jnp.broadcast_to(rlb.reshape(H, 1), (H, 128))
    hc, kvc, scf, scb = _ctx_fwd(ctx, cmod6, norm1, win, rlf, rlb)
    hx, zuv, q, k, v, gfb, of, ya, sst = _fwd_a(x, mod6, norm1, win, sgw, sgbt, sggain, cos, sin, rlf, scf, tm=tm_a)
    if callable(wout):
        wout, tok = wout(hx)
        rlb = rlb + tok
    ob, ycat, y, x1, rst = _fwd_b(q, k, v, of, gfb, ya, x, mod6, wout, rlb, scb, tm=tm_b)
    wg, wu, wd = ffn_weights(x1) if callable(ffn_weights) else ffn_weights
    dx1, h2, da, db, hm, df, small_f = _ffn(x1, tgt, mod6, norm2, normf, wg, wu, wd, tm=tm_f)
    bt2 = min(2 * bt_w, t_len)
    d_wd, d_wd_b = _tn_matmul(hm, df, bm=DFF // 2, bt=bt2, name="dw_down")
    d_wg, d_wg_b = _tn_matmul(da, h2, bm=DFF // 2, bt=bt2, name="dw_gate")
    d_wu, d_wu_b = _tn_matmul(db, h2, bm=DFF // 2, bt=bt2, name="dw_up")
    dy, dya, dgfb, dof, dob, dg1 = _mid_bwd(dx1, y, of, ob, gfb, mod6, wout, tm=tm_m)
    d_wo, d_wo_b = _tn_matmul(ycat, dy, bm=D, bt=bt2, name="dw_out", transposed=False)
    if after_first_grads is not None:
        rlf = rlf + after_first_grads((d_wd_b, d_wg_b, d_wu_b, d_wo_b))
    dqf, dkf, dvf, dqb, dkb, dvb, dscf, dscb, dlg = _ret_bwd(q, k, v, dof, dob, sst, rst, rlf, rlb, tm=tm_r)
    gx, dz, small_i, dsgw = _in_bwd(x, zuv, dya, dqf, dkf, dvf, dqb, dkb, dvb, dgfb, dx1, cos, sin,
                                    mod6, norm1, win, sgw, sgbt, sggain, tm=tm_i)
    dwin_c, small_c, dlg = _ctx_bwd(ctx, hc, kvc, cmod6, norm1, win, rlf, rlb, dscf, dscb, dlg)
    pack = jnp.concatenate([small_f, small_i, small_c, dg1, dlg], axis=0)
    after = small_path(pack, dsgw) if small_path is not None else ()
    halves = []
    for j in range(2):
        halves.append(_tn_matmul(dz, hx, bm=INC // 2, bt=bt2, name="dw_in_%d" % j, init=dwin_c,
                                 init_rows=(KV_LO, KV_HI), after=after, cols=(j, D // 2), transposed=False))
        if after_in_half is not None:
            after = after_in_half(j, halves[-1][1])
    grads = {"w_in": halves, "w_out": (d_wo, d_wo_b), "w_gate": (d_wg, d_wg_b), "w_up": (d_wu, d_wu_b),
             "w_down": (d_wd, d_wd_b)}
    return gx, grads, pack, dsgw


def kernel(x, c, ctx, c_ctx, w_mod, b_mod, norm1, w_in, sg_gain, sg_w, sg_b, ret_logit_f, ret_logit_b, w_out, norm2, w_gate, w_up, w_down, norm_f, loss_target, m_c_ctx, m_w_mod, m_b_mod, m_norm1, m_w_in, m_sg_gain, m_sg_w, m_sg_b, m_ret_logit_f, m_ret_logit_b, m_w_out, m_norm2, m_w_gate, m_w_up, m_w_down, m_norm_f, v_c_ctx, v_w_mod, v_b_mod, v_norm1, v_w_in, v_sg_gain, v_sg_w, v_sg_b, v_ret_logit_f, v_ret_logit_b, v_w_out, v_norm2, v_w_gate, v_w_up, v_w_down, v_norm_f):
    t_len = x.shape[1]
    tm = min(512, t_len)
    me = 4 * lax.axis_index("x") + 2 * lax.axis_index("y") + lax.axis_index("c")
    tr = lambda a: jnp.transpose(a[0])

    cctx_row = c_ctx.reshape(1, D)
    mod_all, cs_all, g_in = _entry_gather(c, cctx_row, w_mod[0], b_mod, [tr(w_in)])
    mod6 = lax.dynamic_slice(mod_all, (me, 0), (1, 6 * D)).reshape(6, D)
    cmod6 = mod_all[8].reshape(6, D)
    win_t = g_in.reshape(INC, D)

    (out_shard,) = _cast_bf16([w_out[0]], name="cast_out", after=[g_in])
    h_out, tok_out = _exchange_start([out_shard], scatter=False, name="wout_start")
    ffn_shards = _cast_bf16([tr(w_gate), tr(w_up), w_down[0]], name="cast_ffn", after=[h_out[2]])
    sems_ffn, arrs_ffn, tok = _gather2_start(ffn_shards, name="wffn_start")
    mod6 = mod6 + (tok_out[0, 0] + tok[0, 0])
    ffn = {}

    def place_own(own, lands, rows):
        return [lax.dynamic_update_slice(l, o[None], (me, 0, 0)).reshape(rows, D) for o, l in zip(own, lands)]

    def wout(after):
        own, lands = _exchange_wait(h_out, [after], scatter=False, name="wout_wait")
        arrs = _gather2_arrived(sems_ffn, arrs_ffn, [after], name="wffn_arrived")
        ffn["own"] = arrs[:3]
        ffn["sems"], ffn["lands"], tok_fwd = _gather2_forward(arrs[3:], name="wffn_forward")
        return place_own(own, lands, D)[0], tok_fwd[0, 0]

    def ffn_weights(after):
        lands = _gather2_wait(ffn["sems"], ffn["lands"], [after], name="wffn_wait")
        return place_own(ffn["own"], lands, DFF)

    rs, outs = {}, {}

    def after_first_grads(bf):
        rs["first"], tok_first = _exchange_start([b.reshape(NDEV, b.shape[0] // NDEV, D) for b in bf], scatter=True,
                                                 name="rs_first_start")
        return tok_first[0, 0]

    def small_path(pack, dsgw):
        rs["small"], _ = _exchange_start([pack, dsgw.reshape(G * C, C)], scatter=False, name="small_start")
        return [rs["small"][2], rs["small"][3]]

    def after_in_half(j, half_bf16):
        rs["in%d" % j], _ = _exchange_start([half_bf16.reshape(NDEV, INC // NDEV, D // 2)], scatter=True,
                                            name="rs_in%d_start" % j)
        return [rs["in%d" % j][2]]

    gx, grads, pack, dsgw = _local_step(
        x[0], loss_target[0], ctx[0], mod6, cmod6, norm1, norm2[0:1], norm_f.reshape(1, D), win_t, wout, ffn_weights,
        sg_w[0], sg_b[0], sg_gain, ret_logit_f, ret_logit_b,
        tm_a=tm, tm_b=tm, tm_f=min(256, t_len), tm_m=min(1024, t_len), tm_r=min(1024, t_len), tm_i=tm,
        bt_w=min(1024, t_len),
        after_first_grads=after_first_grads, small_path=small_path, after_in_half=after_in_half)

    me_arr = me.reshape(1).astype(jnp.int32)
    (pack_own, sgw_own), (pack_land, sgw_land) = _exchange_wait(rs["small"], [rs["in1"][2]], scatter=False,
                                                               name="small_wait")
    psum_rows, pall, sgw_sum, part = _small_sum(me_arr, pack_own, pack_land, sgw_own, sgw_land, w_mod[0])
    h_cc, _ = _exchange_start([part], scatter=False, name="cctx_start")

    dmod_rows = (ROW_I + 0, ROW_I + 1, ROW_G1, ROW_F + 2, ROW_F + 3, ROW_F + 1)
    dmod_all = jnp.concatenate([pall[:, r, :] for r in dmod_rows], axis=1)
    dcmod = jnp.concatenate([psum_rows[ROW_C + 0:ROW_C + 1], psum_rows[ROW_C + 1:ROW_C + 2],
                             jnp.zeros((1, 4 * D), F32)], axis=1)
    dmod_mat = jnp.concatenate([dmod_all, jnp.pad(dcmod, ((0, 7), (0, 0)))], axis=0)
    ncol = 6 * D // NDEV
    dmod_cols = lax.dynamic_slice(dmod_mat, (0, me * ncol), (16, ncol))
    g_wm, d_wm, m_wm, v_wm = _wmod_grad(cs_all, dmod_cols, w_mod[0], m_w_mod[0], v_w_mod[0])
    outs["w_mod"] = (g_wm[None], d_wm[None], m_wm[None], v_wm[None])
    small_params = {
        "norm1": (norm1, m_norm1, v_norm1), "norm2": (norm2, m_norm2, v_norm2),
        "norm_f": tuple(a.reshape(1, D) for a in (norm_f, m_norm_f, v_norm_f)),
        "sg_gain": (sg_gain, m_sg_gain, v_sg_gain), "sg_b": (sg_b[0], m_sg_b[0], v_sg_b[0]),
        "ret_logit_f": (ret_logit_f, m_ret_logit_f, v_ret_logit_f),
        "ret_logit_b": (ret_logit_b, m_ret_logit_b, v_ret_logit_b),
        "b_mod": (b_mod, m_b_mod, v_b_mod), "sg_w": (sg_w[0], m_sg_w[0], v_sg_w[0])}
    small, loss = _adam_small(psum_rows, sgw_sum.reshape(G, C, C), small_params)
    back = {"norm_f": lambda a: a.reshape(D), "sg_b": lambda a: a[None], "sg_w": lambda a: a[None]}
    for name, vals in small.items():
        outs[name] = tuple(back.get(name, lambda a: a)(a) for a in vals)

    _, lands_first = _exchange_wait(rs["first"], [g_wm], scatter=True, name="rs_first_wait")

    def finish(name, fulls, lands, w, m, v, transposed):
        take = tr if transposed else (lambda a: a[0])
        res = _rs_adam(me_arr, fulls, lands, take(w), take(m), take(v), name="adam_" + name)
        put = (lambda a: jnp.transpose(a)[None]) if transposed else (lambda a: a[None])
        outs[name] = tuple(put(a) for a in res)
        return res[0]

    g_down = finish("w_down", [grads["w_down"][0]], [lands_first[0]], w_down, m_w_down, v_w_down, False)
    g_gate = finish("w_gate", [grads["w_gate"][0]], [lands_first[1]], w_gate, m_w_gate, v_w_gate, True)
    g_up = finish("w_up", [grads["w_up"][0]], [lands_first[2]], w_up, m_w_up, v_w_up, True)
    g_out = finish("w_out", [grads["w_out"][0]], [lands_first[3]], w_out, m_w_out, v_w_out, False)
    done = [g_down, g_gate, g_up, g_out]
    (part_own,), (part_land,) = _exchange_wait(h_cc, done, scatter=False, name="cctx_wait")
    res_cc = _cctx_final(me_arr, part_own, part_land, cctx_row, m_c_ctx.reshape(1, D), v_c_ctx.reshape(1, D))
    outs["c_ctx"] = tuple(a.reshape(D) for a in res_cc)
    lands_in = [_exchange_wait(rs["in%d" % j], done, scatter=True, name="rs_in%d_wait" % j)[1][0] for j in range(2)]
    finish("w_in", [h[0] for h in grads["w_in"]], lands_in, w_in, m_w_in, v_w_in, True)

    order = ["c_ctx", "w_mod", "b_mod", "norm1", "w_in", "sg_gain", "sg_w", "sg_b", "ret_logit_f", "ret_logit_b",
             "w_out", "norm2", "w_gate", "w_up", "w_down", "norm_f"]
    res = [loss.reshape(()), gx[None]]
    for j in range(4):
        res += [outs[name][j] for name in order]
    return tuple(res)
```

```python
import functools
import math

import numpy as np
import jax
import jax.numpy as jnp
from jax import lax
from jax.experimental import pallas as pl
from jax.experimental.pallas import tpu as pltpu

F32 = jnp.float32
BF16 = jnp.bfloat16

D = 1024
AW = 512
RW = 512
H = 4
DH = 128
G = 4
C = 128
CR = 256
DFF = 2816
INC = 3584
Q_LO = 2 * AW
KV_LO, VR_LO, G_LO = Q_LO + RW, Q_LO + 2 * RW, Q_LO + 3 * RW
KV_HI = G_LO
NDEV = 8
EPS = 1e-6
KSCALE = DH ** -0.5
ROPE_BASE = 10000.0
GRID_W = 64

ADAM_LR = 0.001
ADAM_B1 = 0.9
ADAM_B2 = 0.999
ADAM_EPS = 1e-08
ADAM_WD = 0.01
ADAM_STEP = 10

VMEM_LIMIT = 56 * 1024 * 1024
MESH = pl.DeviceIdType.MESH


def _cparams(n_grid=0, **kw):
    sem = ("arbitrary",) * n_grid if n_grid else None
    return pltpu.CompilerParams(dimension_semantics=sem, vmem_limit_bytes=VMEM_LIMIT, **kw)


def _dot(a, b):
    return jnp.dot(a, b, preferred_element_type=F32)


def _dot_nt(a, b):
    return lax.dot_general(a, b, (((1,), (1,)), ((), ())), preferred_element_type=F32)


def _dot_tn(a, b):
    return lax.dot_general(a, b, (((0,), (0,)), ((), ())), preferred_element_type=F32)


def _whole(shape):
    nd = len(shape)
    return pl.BlockSpec(shape, lambda *_: (0,) * nd, pipeline_mode=pl.Buffered(1))


def _acc(shape):
    nd = len(shape)
    return pl.BlockSpec(shape, lambda *_: (0,) * nd)


def _rows(tm, n):
    return pl.BlockSpec((tm, n), lambda i: (i, 0))


def _rows_rev(tm, n, nt):
    return pl.BlockSpec((tm, n), lambda i: (nt - 1 - i, 0))


def _cols(n, tm):
    return pl.BlockSpec((n, tm), lambda i: (0, i))


def _sigmoid(x):
    return 1.0 / (1.0 + jnp.exp(-x))


def _log_sigmoid(x):
    return jnp.minimum(x, 0.0) - jnp.log(1.0 + jnp.exp(-jnp.abs(x)))


_GK0 = math.sqrt(2.0 / math.pi)
_GK1 = 0.044715


def _gelu(x):
    x2 = x * x
    t = jnp.tanh(x * (_GK0 + (_GK0 * _GK1) * x2))
    h = 0.5 + 0.5 * t
    g = x * h
    dg = h + g * (1.0 - h) * ((2.0 * _GK0) + (6.0 * _GK0 * _GK1) * x2)
    return g, dg


def _silu_parts(a):
    s = _sigmoid(a)
    sa = a * s
    return sa, s + sa * (1.0 - s)


def _rope(t, cos, sins):
    n = t.shape[1]
    lane = lax.broadcasted_iota(jnp.int32, t.shape, 1)
    first = (lane & 63) < 32
    partner = jnp.where(first, pltpu.roll(t, n - 32, 1), pltpu.roll(t, 32, 1))
    return t * cos + partner * sins


def _headnorm(o):
    outs, rs = [], []
    for h in range(H):
        oh = o[:, h * DH:(h + 1) * DH]
        r = lax.rsqrt(jnp.mean(oh * oh, axis=-1, keepdims=True) + EPS)
        outs.append(oh * r)
        rs.append(r)
    return jnp.concatenate(outs, axis=1), rs


def _decay_consts(lg, forward):
    ii = lax.broadcasted_iota(jnp.int32, (CR, CR), 0).astype(F32)
    jj = lax.broadcasted_iota(jnp.int32, (CR, CR), 1).astype(F32)
    ic = lax.broadcasted_iota(jnp.int32, (CR, 1), 0).astype(F32)
    if forward:
        diff = ii - jj
        xi = jnp.exp(lg * (ic + 1.0))
        z = jnp.exp(lg * (CR - 1.0 - ic))
    else:
        diff = jj - ii
        xi = jnp.exp(lg * (CR - ic))
        z = jnp.exp(lg * ic)
    dm = jnp.where(diff >= 0.0, jnp.exp(lg * jnp.maximum(diff, 0.0)), 0.0)
    dec = jnp.exp(lg * float(CR))
    return dm, xi, z, dec, ic


def _ctx_fwd(ctx, cmod6, norm1, win, rlf, rlb):
    lc = ctx.shape[0]

    def body(ctx_ref, cmod_ref, n1_ref, win_ref, rlf_ref, rlb_ref, hc_ref, kvc_ref, scf_ref, scb_ref):
        x = ctx_ref[...]
        r = lax.rsqrt(jnp.mean(x * x, axis=-1, keepdims=True) + EPS)
        hc = (x * r) * (n1_ref[...] * (1.0 + cmod_ref[1:2, :])) + cmod_ref[0:1, :]
        hcb = hc.astype(BF16)
        hc_ref[...] = hcb
        kv = _dot_nt(hcb, win_ref[KV_LO:KV_HI, :])
        k = kv[:, :RW] * KSCALE
        v = kv[:, RW:]
        kvc_ref[:, :RW] = k
        kvc_ref[:, RW:] = v
        t = lax.broadcasted_iota(jnp.int32, (lc, 1), 0).astype(F32)
        lgf = _log_sigmoid(rlf_ref[...])
        lgb = _log_sigmoid(rlb_ref[...])
        for h in range(H):
            hs = slice(h * DH, (h + 1) * DH)
            wf = jnp.exp(lgf[h:h + 1, 0:1] * (lc - 1.0 - t))
            wb = jnp.exp(lgb[h:h + 1, 0:1] * t)
            vh = v[:, hs].astype(BF16)
            scf_ref[h] = _dot_tn((k[:, hs] * wf).astype(BF16), vh)
            scb_ref[h] = _dot_tn((k[:, hs] * wb).astype(BF16), vh)

    return pl.pallas_call(
        body, name="ctx_fwd",
        out_shape=(jax.ShapeDtypeStruct((lc, D), BF16), jax.ShapeDtypeStruct((lc, 2 * RW), F32),
                   jax.ShapeDtypeStruct((H, DH, DH), F32), jax.ShapeDtypeStruct((H, DH, DH), F32)),
        compiler_params=_cparams(),
    )(ctx, cmod6, norm1, win, rlf, rlb)


def _sg_forward(u, v, sgw_ref, sgbt_ref, sgg_ref, nc):
    ua, dgu = _gelu(u)
    va, dgv = _gelu(v)
    gain = sgg_ref[...]
    mixed, vn_b, vah_l, rv_l = [], [], [], []
    for g in range(G):
        gs = slice(g * DH, (g + 1) * DH)
        vag = va[:, gs]
        rv = lax.rsqrt(jnp.mean(vag * vag, axis=-1, keepdims=True) + EPS)
        vah = vag * rv
        vn = (vah * gain[:, gs]).astype(BF16)
        wg = sgw_ref[g].astype(BF16)
        bcol = sgbt_ref[g]
        rows = [_dot(wg, vn[c * C:(c + 1) * C, :]) + bcol for c in range(nc)]
        mixed.append(jnp.concatenate(rows, axis=0) if nc > 1 else rows[0])
        vn_b.append(vn)
        vah_l.append(vah)
        rv_l.append(rv)
    mixed = jnp.concatenate(mixed, axis=1)
    return ua * mixed, (ua, dgu, dgv, mixed, vn_b, vah_l, rv_l)


def _fwd_a(x, mod6, norm1, win, sgw, sgbt, sggain, cos, sin, rlf, scf, *, tm):
    t_len = x.shape[0]
    nt, nc, ncr = t_len // tm, tm // C, tm // CR

    def body(x_ref, mod_ref, n1_ref, win_ref, sgw_ref, sgbt_ref, sgg_ref, cos_ref, sin_ref, rlf_ref, scf_ref,
             hx_ref, zuv_ref, q_ref, k_ref, v_ref, gfb_ref, of_ref, ya_ref, sst_ref, s_scr):
        i = pl.program_id(0)

        @pl.when(i == 0)
        def _():
            s_scr[...] = scf_ref[...]

        x = x_ref[...]
        r = lax.rsqrt(jnp.mean(x * x, axis=-1, keepdims=True) + EPS)
        hx = (x * r) * (n1_ref[...] * (1.0 + mod_ref[1:2, :])) + mod_ref[0:1, :]
        hxb = hx.astype(BF16)
        hx_ref[...] = hxb
        z = _dot_nt(hxb, win_ref[...])
        zuv_ref[...] = z[:, :2 * AW].astype(BF16)
        gfb_ref[...] = z[:, G_LO:INC].astype(BF16)
        cos = jnp.tile(cos_ref[...], (1, H))
        sins = jnp.tile(sin_ref[...], (1, H))
        q_ref[...] = _rope(z[:, Q_LO:KV_LO], cos, sins).astype(BF16)
        k_ref[...] = (_rope(z[:, KV_LO:VR_LO], cos, sins) * KSCALE).astype(BF16)
        v_ref[...] = z[:, VR_LO:G_LO].astype(BF16)
        ya, _ = _sg_forward(z[:, :AW], z[:, AW:2 * AW], sgw_ref, sgbt_ref, sgg_ref, nc)
        ya_ref[...] = ya.astype(BF16)

        lg = _log_sigmoid(rlf_ref[...])
        for h in range(H):
            dm, xi, zc, dec, _ = _decay_consts(lg[h:h + 1, 0:1], True)
            hs = slice(h * DH, (h + 1) * DH)
            for c in range(ncr):
                rs = slice(c * CR, (c + 1) * CR)
                qc, kc, vc = q_ref[rs, hs], k_ref[rs, hs], v_ref[rs, hs]
                s = s_scr[h]
                sst_ref[c, h] = s
                a = (_dot_nt(qc, kc) * dm).astype(BF16)
                of_ref[rs, hs] = (_dot(a, vc) + xi * _dot(qc, s.astype(BF16))).astype(BF16)
                kz = (kc.astype(F32) * zc).astype(BF16)
                s_scr[h] = dec * s + _dot_tn(kz, vc)

    n_chunks = t_len // CR
    return pl.pallas_call(
        body, name="fwd_a", grid=(nt,),
        in_specs=[_rows(tm, D), _whole((6, D)), _whole((1, D)), _whole((INC, D)), _whole((G, C, C)),
                  _whole((G, C, 128)), _whole((1, AW)), _rows(tm, DH), _rows(tm, DH), _whole((H, 128)),
                  _whole((H, DH, DH))],
        out_specs=[_rows(tm, D), _rows(tm, 2 * AW), _rows(tm, RW), _rows(tm, RW), _rows(tm, RW),
                   _rows(tm, 2 * RW), _rows(tm, RW), _rows(tm, AW),
                   pl.BlockSpec((ncr, H, DH, DH), lambda i: (i, 0, 0, 0))],
        out_shape=(jax.ShapeDtypeStruct((t_len, D), BF16), jax.ShapeDtypeStruct((t_len, 2 * AW), BF16),
                   jax.ShapeDtypeStruct((t_len, RW), BF16), jax.ShapeDtypeStruct((t_len, RW), BF16),
                   jax.ShapeDtypeStruct((t_len, RW), BF16), jax.ShapeDtypeStruct((t_len, 2 * RW), BF16),
                   jax.ShapeDtypeStruct((t_len, RW), BF16), jax.ShapeDtypeStruct((t_len, AW), BF16),
                   jax.ShapeDtypeStruct((n_chunks, H, DH, DH), F32)),
        scratch_shapes=[pltpu.VMEM((H, DH, DH), F32)],
        compiler_params=_cparams(1),
    )(x, mod6, norm1, win, sgw, sgbt, sggain, cos, sin, rlf, scf)


def _fwd_b(q, k, v, of, gfb, ya, x, mod6, wout, rlb, scb, *, tm):
    t_len = x.shape[0]
    nt, nc = t_len // tm, tm // CR

    def body(q_ref, k_ref, v_ref, of_ref, gfb_ref, ya_ref, x_ref, mod_ref, wout_ref, rlb_ref, scb_ref,
             ob_ref, ycat_ref, y_ref, x1_ref, rst_ref, r_scr):
        i = pl.program_id(0)

        @pl.when(i == 0)
        def _():
            r_scr[...] = scb_ref[...]

        lg = _log_sigmoid(rlb_ref[...])
        for h in range(H):
            dm, xi, zc, dec, _ = _decay_consts(lg[h:h + 1, 0:1], False)
            hs = slice(h * DH, (h + 1) * DH)
            for c in reversed(range(nc)):
                rs = slice(c * CR, (c + 1) * CR)
                qc, kc, vc = q_ref[rs, hs], k_ref[rs, hs], v_ref[rs, hs]
                s = r_scr[h]
                rst_ref[c, h] = s
                a = (_dot_nt(qc, kc) * dm).astype(BF16)
                ob_ref[rs, hs] = (_dot(a, vc) + xi * _dot(qc, s.astype(BF16))).astype(BF16)
                kz = (kc.astype(F32) * zc).astype(BF16)
                r_scr[h] = dec * s + _dot_tn(kz, vc)

        gfb = gfb_ref[...].astype(F32)
        sf, _ = _silu_parts(gfb[:, :RW])
        sb, _ = _silu_parts(gfb[:, RW:])
        hnf, _ = _headnorm(of_ref[...].astype(F32))
        hnb, _ = _headnorm(ob_ref[...].astype(F32))
        yr = sf * hnf + sb * hnb
        ycat = jnp.concatenate([ya_ref[...], yr.astype(BF16)], axis=1)
        ycat_ref[...] = ycat
        y = _dot(ycat, wout_ref[...])
        y_ref[...] = y.astype(BF16)
        x1_ref[...] = x_ref[...] + mod_ref[2:3, :] * y

    n_chunks = t_len // CR
    rr = functools.partial(_rows_rev, nt=nt)
    return pl.pallas_call(
        body, name="fwd_b", grid=(nt,),
        in_specs=[rr(tm, RW), rr(tm, RW), rr(tm, RW), rr(tm, RW), rr(tm, 2 * RW), rr(tm, AW), rr(tm, D),
                  _whole((6, D)), _whole((D, D)), _whole((H, 128)), _whole((H, DH, DH))],
        out_specs=[rr(tm, RW), rr(tm, D), rr(tm, D), rr(tm, D),
                   pl.BlockSpec((nc, H, DH, DH), lambda i: (nt - 1 - i, 0, 0, 0))],
        out_shape=(jax.ShapeDtypeStruct((t_len, RW), BF16), jax.ShapeDtypeStruct((t_len, D), BF16),
                   jax.ShapeDtypeStruct((t_len, D), BF16), jax.ShapeDtypeStruct((t_len, D), F32),
                   jax.ShapeDtypeStruct((n_chunks, H, DH, DH), F32)),
        scratch_shapes=[pltpu.VMEM((H, DH, DH), F32)],
        compiler_params=_cparams(1),
    )(q, k, v, of, gfb, ya, x, mod6, wout, rlb, scb)


def _ffn(x1, tgt, mod6, norm2, normf, wg, wu, wd, *, tm):
    t_len = x1.shape[0]
    nt = t_len // tm

    def body(x1_ref, tgt_ref, mod_ref, n2_ref, nf_ref, wg_ref, wu_ref, wd_ref,
             dx1_ref, h2_ref, da_ref, db_ref, hm_ref, df_ref, small_ref):
        i = pl.program_id(0)

        @pl.when(i == 0)
        def _():
            small_ref[...] = jnp.zeros_like(small_ref)

        sh2, sc2, g2 = mod_ref[3:4, :], mod_ref[4:5, :], mod_ref[5:6, :]
        n2, nf = n2_ref[...], nf_ref[...]
        x1 = x1_ref[...]
        r2 = lax.rsqrt(jnp.mean(x1 * x1, axis=-1, keepdims=True) + EPS)
        x1h = x1 * r2
        w2 = n2 * (1.0 + sc2)
        h2b = (x1h * w2 + sh2).astype(BF16)
        h2_ref[...] = h2b
        a = _dot_nt(h2b, wg_ref[...])
        b = _dot_nt(h2b, wu_ref[...])
        sa, dsa = _silu_parts(a)
        hmb = (sa * b).astype(BF16)
        hm_ref[...] = hmb.T
        f = _dot(hmb, wd_ref[...])
        x2 = x1 + g2 * f
        r3 = lax.rsqrt(jnp.mean(x2 * x2, axis=-1, keepdims=True) + EPS)
        x2h = x2 * r3
        diff = x2h * nf - tgt_ref[...]
        small_ref[5:6, :] += jnp.sum(diff * diff, axis=0, keepdims=True)
        dout = diff * (1.0 / D)
        small_ref[0:1, :] += jnp.sum(dout * x2h, axis=0, keepdims=True)
        dyg = dout * nf
        dx2 = r3 * (dyg - x2h * jnp.mean(dyg * x2h, axis=-1, keepdims=True))
        small_ref[1:2, :] += jnp.sum(dx2 * f, axis=0, keepdims=True)
        dfb = (dx2 * g2).astype(BF16)
        df_ref[...] = dfb
        dhm = _dot_nt(dfb, wd_ref[...])
        dbb = (dhm * sa).astype(BF16)
        dab = (dhm * b * dsa).astype(BF16)
        da_ref[...] = dab.T
        db_ref[...] = dbb.T
        dh2 = _dot(dab, wg_ref[...]) + _dot(dbb, wu_ref[...])
        small_ref[2:3, :] += jnp.sum(dh2, axis=0, keepdims=True)
        dhx = dh2 * x1h
        small_ref[3:4, :] += jnp.sum(dhx, axis=0, keepdims=True) * n2
        small_ref[4:5, :] += jnp.sum(dhx, axis=0, keepdims=True) * (1.0 + sc2)
        dyg2 = dh2 * w2
        dx1_ref[...] = dx2 + r2 * (dyg2 - x1h * jnp.mean(dyg2 * x1h, axis=-1, keepdims=True))

    return pl.pallas_call(
        body, name="ffn", grid=(nt,),
        in_specs=[_rows(tm, D), _rows(tm, D), _whole((6, D)), _whole((1, D)), _whole((1, D)),
                  _whole((DFF, D)), _whole((DFF, D)), _whole((DFF, D))],
        out_specs=[_rows(tm, D), _rows(tm, D), _cols(DFF, tm), _cols(DFF, tm), _cols(DFF, tm), _rows(tm, D),
                   _acc((8, D))],
        out_shape=(jax.ShapeDtypeStruct((t_len, D), F32), jax.ShapeDtypeStruct((t_len, D), BF16),
                   jax.ShapeDtypeStruct((DFF, t_len), BF16), jax.ShapeDtypeStruct((DFF, t_len), BF16),
                   jax.ShapeDtypeStruct((DFF, t_len), BF16), jax.ShapeDtypeStruct((t_len, D), BF16),
                   jax.ShapeDtypeStruct((8, D), F32)),
        compiler_params=_cparams(1),
    )(x1, tgt, mod6, norm2, normf, wg, wu, wd)


def _mid_bwd(dx1, y, of, ob, gfb, mod6, wout, *, tm):
    t_len = dx1.shape[0]
    nt = t_len // tm

    def body(dx1_ref, y_ref, of_ref, ob_ref, gfb_ref, mod_ref, wout_ref,
             dy_ref, dya_ref, dgfb_ref, dof_ref, dob_ref, dg1_ref):
        i = pl.program_id(0)

        @pl.when(i == 0)
        def _():
            dg1_ref[...] = jnp.zeros_like(dg1_ref)

        dx1 = dx1_ref[...]
        dg1_ref[0:1, :] += jnp.sum(dx1 * y_ref[...].astype(F32), axis=0, keepdims=True)
        dyb = (dx1 * mod_ref[2:3, :]).astype(BF16)
        dy_ref[...] = dyb
        dycat = _dot_nt(dyb, wout_ref[...])
        dya_ref[...] = dycat[:, :AW].astype(BF16)
        dyr = dycat[:, AW:]
        gfb = gfb_ref[...].astype(F32)
        for o_ref, do_ref, lo in ((of_ref, dof_ref, 0), (ob_ref, dob_ref, RW)):
            sg, dsg = _silu_parts(gfb[:, lo:lo + RW])
            o = o_ref[...].astype(F32)
            hn, rs = _headnorm(o)
            dgfb_ref[:, lo:lo + RW] = (dyr * hn * dsg).astype(BF16)
            dhn = dyr * sg
            for h in range(H):
                hs = slice(h * DH, (h + 1) * DH)
                oh, dh = hn[:, hs], dhn[:, hs]
                do_ref[:, hs] = (rs[h] * (dh - oh * jnp.mean(dh * oh, axis=-1, keepdims=True))).astype(BF16)

    return pl.pallas_call(
        body, name="mid_bwd", grid=(nt,),
        in_specs=[_rows(tm, D), _rows(tm, D), _rows(tm, RW), _rows(tm, RW), _rows(tm, 2 * RW),
                  _whole((6, D)), _whole((D, D))],
        out_specs=[_rows(tm, D), _rows(tm, AW), _rows(tm, 2 * RW), _rows(tm, RW), _rows(tm, RW), _acc((8, D))],
        out_shape=(jax.ShapeDtypeStruct((t_len, D), BF16), jax.ShapeDtypeStruct((t_len, AW), BF16),
                   jax.ShapeDtypeStruct((t_len, 2 * RW), BF16), jax.ShapeDtypeStruct((t_len, RW), BF16),
                   jax.ShapeDtypeStruct((t_len, RW), BF16), jax.ShapeDtypeStruct((8, D), F32)),
        compiler_params=_cparams(1),
    )(dx1, y, of, ob, gfb, mod6, wout)


def _ret_bwd_items(q_ref, k_ref, v_ref, do_ref, st_ref, dq_ref, dk_ref, dv_ref, ds_scr, dlg_scr, lg, forward, nc, row0):
    order = list(reversed(range(nc))) if forward else list(range(nc))
    consts = [_decay_consts(lg[h:h + 1, 0:1], forward) for h in range(H)]
    accs = [jnp.zeros((1, DH), F32) for _ in range(H)]

    def first(h, c):
        dm, xi, zc, dec, _ = consts[h]
        hs, rs = slice(h * DH, (h + 1) * DH), slice(c * CR, (c + 1) * CR)
        qc, kc, vc, doc = q_ref[rs, hs], k_ref[rs, hs], v_ref[rs, hs], do_ref[rs, hs]
        s, dsn = st_ref[c, h], ds_scr[h]
        sb, dsnb = s.astype(BF16), dsn.astype(BF16)
        qf, kf = qc.astype(F32), kc.astype(F32)
        a_raw = _dot_nt(qc, kc)
        da_raw = _dot_nt(doc, vc)
        xdo = (xi * doc.astype(F32)).astype(BF16)
        kz = (kf * zc).astype(BF16)
        vz = (vc.astype(F32) * zc).astype(BF16)
        dq_c = _dot_nt(xdo, sb)
        dk_s = _dot_nt(vz, dsnb)
        dv_s = _dot(kz, dsnb)
        ds_scr[h] = _dot_tn((xi * qf).astype(BF16), doc) + dec * dsn
        accs[h] = accs[h] + (float(CR) * dec) * jnp.sum(s * dsn, axis=0, keepdims=True)
        a = (a_raw * dm).astype(BF16)
        da = (da_raw * dm).astype(BF16)
        return a, da, dq_c, dk_s, dv_s

    def second(h, c, carried):
        a, da, dq_c, dk_s, dv_s = carried
        ic = consts[h][4]
        if forward:
            w_qi, w_qc, w_ki, w_ks = ic, ic + 1.0, -ic, (CR - 1.0) - ic
        else:
            w_qi, w_qc, w_ki, w_ks = -ic, CR - ic, ic, ic
        hs, rs = slice(h * DH, (h + 1) * DH), slice(c * CR, (c + 1) * CR)
        qc, kc, doc = q_ref[rs, hs], k_ref[rs, hs], do_ref[rs, hs]
        dq_i = _dot(da, kc)
        dk_i = _dot_tn(da, qc)
        dv_i = _dot_tn(a, doc)
        dq_ref[rs, hs] = (dq_i + dq_c).astype(BF16)
        dk_ref[rs, hs] = (dk_i + dk_s).astype(BF16)
        dv_ref[rs, hs] = (dv_i + dv_s).astype(BF16)
        rowterm = qc.astype(F32) * (w_qi * dq_i + w_qc * dq_c) + kc.astype(F32) * (w_ki * dk_i + w_ks * dk_s)
        accs[h] = accs[h] + jnp.sum(rowterm, axis=0, keepdims=True)

    def finish():
        for h in range(H):
            dlg_scr[row0 + h:row0 + h + 1, :] += accs[h]

    items = [[(functools.partial(first, h, c), functools.partial(second, h, c)) for h in range(H)] for c in order]
    return items, finish


def _ret_bwd_run(dirs):
    nc = len(dirs[0][0])
    flat = [it for j in range(nc) for items, _ in dirs for it in items[j]]
    pending = None
    for first, second in flat:
        carried = first()
        if pending is not None:
            pending[0](pending[1])
        pending = (second, carried)
    pending[0](pending[1])
    for _, finish in dirs:
        finish()


def _ret_bwd(q, k, v, dof, dob, sst, rst, rlf, rlb, *, tm):
    t_len = q.shape[0]
    nt, nc = t_len // tm, tm // CR

    def body(qf_ref, kf_ref, vf_ref, dof_ref, sst_ref, qb_ref, kb_ref, vb_ref, dob_ref, rst_ref, rlf_ref, rlb_ref,
             dqf_ref, dkf_ref, dvf_ref, dqb_ref, dkb_ref, dvb_ref, dscf_ref, dscb_ref, dlg_ref,
             dsf_scr, dsb_scr, dlg_scr):
        i = pl.program_id(0)

        @pl.when(i == 0)
        def _():
            dsf_scr[...] = jnp.zeros_like(dsf_scr)
            dsb_scr[...] = jnp.zeros_like(dsb_scr)
            dlg_scr[...] = jnp.zeros_like(dlg_scr)

        lgf = _log_sigmoid(rlf_ref[...])
        lgb = _log_sigmoid(rlb_ref[...])
        _ret_bwd_run([
            _ret_bwd_items(qf_ref, kf_ref, vf_ref, dof_ref, sst_ref, dqf_ref, dkf_ref, dvf_ref, dsf_scr, dlg_scr,
                           lgf, True, nc, 0),
            _ret_bwd_items(qb_ref, kb_ref, vb_ref, dob_ref, rst_ref, dqb_ref, dkb_ref, dvb_ref, dsb_scr, dlg_scr,
                           lgb, False, nc, H)])

        @pl.when(i == nt - 1)
        def _():
            dscf_ref[...] = dsf_scr[...]
            dscb_ref[...] = dsb_scr[...]
            tot = jnp.broadcast_to(jnp.sum(dlg_scr[...], axis=1, keepdims=True), (8, 128))
            ri = lax.broadcasted_iota(jnp.int32, (8, 128), 0)
            li = lax.broadcasted_iota(jnp.int32, (8, 128), 1)
            dlg_ref[...] = jnp.zeros_like(dlg_ref)
            dlg_ref[0:1, 0:128] = jnp.sum(jnp.where(ri == li, tot, 0.0), axis=0, keepdims=True)
            dlg_ref[1:2, 0:128] = jnp.sum(jnp.where(ri - H == li, tot, 0.0), axis=0, keepdims=True)

    rr = functools.partial(_rows_rev, nt=nt)
    st_f = pl.BlockSpec((nc, H, DH, DH), lambda i: (nt - 1 - i, 0, 0, 0))
    st_b = pl.BlockSpec((nc, H, DH, DH), lambda i: (i, 0, 0, 0))
    tok = jax.ShapeDtypeStruct((t_len, RW), BF16)
    st = jax.ShapeDtypeStruct((H, DH, DH), F32)
    return pl.pallas_call(
        body, name="ret_bwd", grid=(nt,),
        in_specs=[rr(tm, RW), rr(tm, RW), rr(tm, RW), rr(tm, RW), st_f,
                  _rows(tm, RW), _rows(tm, RW), _rows(tm, RW), _rows(tm, RW), st_b,
                  _whole((H, 128)), _whole((H, 128))],
        out_specs=[rr(tm, RW), rr(tm, RW), rr(tm, RW), _rows(tm, RW), _rows(tm, RW), _rows(tm, RW),
                   _acc((H, DH, DH)), _acc((H, DH, DH)), _acc((8, D))],
        out_shape=(tok, tok, tok, tok, tok, tok, st, st, jax.ShapeDtypeStruct((8, D), F32)),
        scratch_shapes=[pltpu.VMEM((H, DH, DH), F32), pltpu.VMEM((H, DH, DH), F32), pltpu.VMEM((8, 128), F32)],
        compiler_params=_cparams(1),
    )(q, k, v, dof, sst, q, k, v, dob, rst, rlf, rlb)


def _in_bwd(x, zuv, dya, dqf, dkf, dvf, dqb, dkb, dvb, dgfb, dx1, cos, sin, mod6, norm1, win, sgw, sgbt, sggain, *, tm):
    t_len = x.shape[0]
    nt, nc = t_len // tm, tm // C

    def body(x_ref, zuv_ref, dya_ref, dqf_ref, dkf_ref, dvf_ref, dqb_ref, dkb_ref, dvb_ref, dgfb_ref, dx1_ref,
             cos_ref, sin_ref, mod_ref, n1_ref, win_ref, sgw_ref, sgbt_ref, sgg_ref,
             gx_ref, dz_ref, small_ref, dsgw_ref, dsgbt_ref):
        i = pl.program_id(0)

        @pl.when(i == 0)
        def _():
            small_ref[...] = jnp.zeros_like(small_ref)
            dsgw_ref[...] = jnp.zeros_like(dsgw_ref)
            dsgbt_ref[...] = jnp.zeros_like(dsgbt_ref)

        zuv = zuv_ref[...].astype(F32)
        _, (ua, dgu, dgv, mixed, vn_b, vah_l, rv_l) = _sg_forward(zuv[:, :AW], zuv[:, AW:], sgw_ref, sgbt_ref, sgg_ref, nc)
        dya = dya_ref[...].astype(F32)
        dz_ref[:, 0:AW] = (dya * mixed * dgu).astype(BF16)
        dmixed = dya * ua
        gain = sgg_ref[...]
        for g in range(G):
            gs = slice(g * DH, (g + 1) * DH)
            dmg = dmixed[:, gs]
            dmb = dmg.astype(BF16)
            wgt = sgw_ref[g].astype(BF16)
            dsw = jnp.zeros((C, C), F32)
            dsb = jnp.zeros((C, DH), F32)
            rows = []
            for c in range(nc):
                rs = slice(c * C, (c + 1) * C)
                dsw = dsw + _dot_nt(dmb[rs, :], vn_b[g][rs, :])
                dsb = dsb + dmg[rs, :]
                rows.append(_dot_tn(wgt, dmb[rs, :]))
            dsgw_ref[g] += dsw
            dsgbt_ref[g] += dsb
            dvn = jnp.concatenate(rows, axis=0) if nc > 1 else rows[0]
            vah, rv = vah_l[g], rv_l[g]
            small_ref[3:4, gs] += jnp.sum(dvn * vah, axis=0, keepdims=True)
            dyg = dvn * gain[:, gs]
            dva = rv * (dyg - vah * jnp.mean(dyg * vah, axis=-1, keepdims=True))
            dz_ref[:, AW + g * DH:AW + (g + 1) * DH] = (dva * dgv[:, gs]).astype(BF16)

        cos = jnp.tile(cos_ref[...], (1, H))
        nsins = -jnp.tile(sin_ref[...], (1, H))
        def both(f_ref, b_ref):
            return f_ref[...].astype(F32) + b_ref[...].astype(F32)

        dz_ref[:, Q_LO:KV_LO] = _rope(both(dqf_ref, dqb_ref), cos, nsins).astype(BF16)
        dz_ref[:, KV_LO:VR_LO] = (_rope(both(dkf_ref, dkb_ref), cos, nsins) * KSCALE).astype(BF16)
        dz_ref[:, VR_LO:G_LO] = both(dvf_ref, dvb_ref).astype(BF16)
        dz_ref[:, G_LO:INC] = dgfb_ref[...]

        dhx = _dot(dz_ref[...], win_ref[...])
        x = x_ref[...]
        r = lax.rsqrt(jnp.mean(x * x, axis=-1, keepdims=True) + EPS)
        xh = x * r
        n1, sc1 = n1_ref[...], mod_ref[1:2, :]
        small_ref[0:1, :] += jnp.sum(dhx, axis=0, keepdims=True)
        dhxx = jnp.sum(dhx * xh, axis=0, keepdims=True)
        small_ref[1:2, :] += dhxx * n1
        small_ref[2:3, :] += dhxx * (1.0 + sc1)
        dyg = dhx * (n1 * (1.0 + sc1))
        gx_ref[...] = dx1_ref[...] + r * (dyg - xh * jnp.mean(dyg * xh, axis=-1, keepdims=True))

        @pl.when(i == nt - 1)
        def _():
            ri = lax.broadcasted_iota(jnp.int32, (C, DH), 0)
            li = lax.broadcasted_iota(jnp.int32, (C, DH), 1)
            for g in range(G):
                tot = jnp.broadcast_to(jnp.sum(dsgbt_ref[g], axis=1, keepdims=True), (C, DH))
                small_ref[4 + g:5 + g, 0:DH] = jnp.sum(jnp.where(ri == li, tot, 0.0), axis=0, keepdims=True)

    tok = functools.partial(_rows, tm)
    return pl.pallas_call(
        body, name="in_bwd", grid=(nt,),
        in_specs=[tok(D), tok(2 * AW), tok(AW), tok(RW), tok(RW), tok(RW), tok(RW), tok(RW), tok(RW),
                  tok(2 * RW), tok(D), tok(DH), tok(DH), _whole((6, D)), _whole((1, D)), _whole((INC, D)),
                  _whole((G, C, C)), _whole((G, C, 128)), _whole((1, AW))],
        out_specs=[tok(D), tok(INC), _acc((8, D)), _acc((G, C, C))],
        out_shape=(jax.ShapeDtypeStruct((t_len, D), F32), jax.ShapeDtypeStruct((t_len, INC), BF16),
                   jax.ShapeDtypeStruct((8, D), F32), jax.ShapeDtypeStruct((G, C, C), F32)),
        scratch_shapes=[pltpu.VMEM((G, C, 128), F32)],
        compiler_params=_cparams(1),
    )(x, zuv, dya, dqf, dkf, dvf, dqb, dkb, dvb, dgfb, dx1, cos, sin, mod6, norm1, win, sgw, sgbt, sggain)


def _tn_matmul(at, b, *, bm, bt, name, init=None, init_rows=None, after=(), cols=None, transposed=True):
    m, t_len = at.shape if transposed else at.shape[::-1]
    cj, n = (0, b.shape[1]) if cols is None else cols
    ni, ntt = m // bm, t_len // bt
    has_init = init is not None

    def body(*refs):
        o_ref, ob_ref = refs[-2:]
        a_ref, b_ref = refs[:2]
        init_ref = refs[2] if has_init else None
        i, t = pl.program_id(0), pl.program_id(1)

        @pl.when(t == 0)
        def _():
            o_ref[...] = jnp.zeros_like(o_ref)

        if has_init:
            for ib in range(ni):
                lo, hi = max(init_rows[0], ib * bm), min(init_rows[1], (ib + 1) * bm)
                if lo < hi:
                    @pl.when(jnp.logical_and(t == 0, i == ib))
                    def _(lo=lo, hi=hi, ib=ib):
                        o_ref[lo - ib * bm:hi - ib * bm, :] = init_ref[lo - init_rows[0]:hi - init_rows[0], :]

        o_ref[...] += (_dot if transposed else _dot_tn)(a_ref[...], b_ref[...])

        @pl.when(t == ntt - 1)
        def _():
            ob_ref[...] = o_ref[...].astype(BF16)

    a_spec = pl.BlockSpec((bm, bt), lambda i, t: (i, t)) if transposed else pl.BlockSpec((bt, bm), lambda i, t: (t, i))
    in_specs = [a_spec, pl.BlockSpec((bt, n), lambda i, t: (t, cj))]
    args = [at, b]
    if has_init:
        in_specs.append(pl.BlockSpec((init.shape[0], n), lambda i, t: (0, cj), pipeline_mode=pl.Buffered(1)))
        args.append(init)
    for arr in after:
        in_specs.append(pl.BlockSpec(memory_space=pl.ANY))
        args.append(arr)
    out_spec = pl.BlockSpec((bm, n), lambda i, t: (i, 0))
    return pl.pallas_call(
        body, name=name, grid=(ni, ntt),
        in_specs=in_specs,
        out_specs=[out_spec, out_spec],
        out_shape=(jax.ShapeDtypeStruct((m, n), F32), jax.ShapeDtypeStruct((m, n), BF16)),
        compiler_params=_cparams(2),
    )(*args)


def _ctx_bwd(ctx, hc, kvc, cmod6, norm1, win, rlf, rlb, dscf, dscb, dlg_in):
    lc = ctx.shape[0]

    def body(ctx_ref, hc_ref, kvc_ref, cmod_ref, n1_ref, win_ref, rlf_ref, rlb_ref, dscf_ref, dscb_ref, dlgin_ref,
             dwin_ref, small_ref, dlg_ref):
        k = kvc_ref[:, :RW]
        v = kvc_ref[:, RW:]
        t = lax.broadcasted_iota(jnp.int32, (lc, 1), 0).astype(F32)
        lgf = _log_sigmoid(rlf_ref[...])
        lgb = _log_sigmoid(rlb_ref[...])
        dks, dvs = [], []
        lane = lax.broadcasted_iota(jnp.int32, (1, 128), 1)
        row_f = jnp.zeros((1, 128), F32)
        row_b = jnp.zeros((1, 128), F32)
        for h in range(H):
            hs = slice(h * DH, (h + 1) * DH)
            wf = jnp.exp(lgf[h:h + 1, 0:1] * (lc - 1.0 - t))
            wb = jnp.exp(lgb[h:h + 1, 0:1] * t)
            kh, vhb = k[:, hs], v[:, hs].astype(BF16)
            dsf, dsb = dscf_ref[h].astype(BF16), dscb_ref[h].astype(BF16)
            dkf = wf * _dot_nt(vhb, dsf)
            dkb = wb * _dot_nt(vhb, dsb)
            dvs.append(_dot((kh * wf).astype(BF16), dsf) + _dot((kh * wb).astype(BF16), dsb))
            dks.append((dkf + dkb) * KSCALE)
            lf = jnp.sum((lc - 1.0 - t) * kh * dkf, axis=0, keepdims=True)
            lb = jnp.sum(t * kh * dkb, axis=0, keepdims=True)
            row_f = row_f + jnp.where(lane == h, jnp.sum(lf, axis=1, keepdims=True), 0.0)
            row_b = row_b + jnp.where(lane == h, jnp.sum(lb, axis=1, keepdims=True), 0.0)
        dlg_ref[...] = dlgin_ref[...]
        dlg_ref[0:1, 0:128] += row_f
        dlg_ref[1:2, 0:128] += row_b
        dkv = jnp.concatenate(dks + dvs, axis=1).astype(BF16)
        dhc = _dot(dkv, win_ref[KV_LO:KV_HI, :])
        dwin_ref[...] = _dot_tn(dkv, hc_ref[...])
        x = ctx_ref[...]
        r = lax.rsqrt(jnp.mean(x * x, axis=-1, keepdims=True) + EPS)
        xh = x * r
        small_ref[...] = jnp.zeros_like(small_ref)
        small_ref[0:1, :] = jnp.sum(dhc, axis=0, keepdims=True)
        dhxx = jnp.sum(dhc * xh, axis=0, keepdims=True)
        small_ref[1:2, :] = dhxx * n1_ref[...]
        small_ref[2:3, :] = dhxx * (1.0 + cmod_ref[1:2, :])

    return pl.pallas_call(
        body, name="ctx_bwd",
        out_shape=(jax.ShapeDtypeStruct((2 * RW, D), F32), jax.ShapeDtypeStruct((8, D), F32),
                   jax.ShapeDtypeStruct((8, D), F32)),
        compiler_params=_cparams(),
    )(ctx, hc, kvc, cmod6, norm1, win, rlf, rlb, dscf, dscb, dlg_in)


def _adam_math(w, g, m, v):
    m = ADAM_B1 * m + (1.0 - ADAM_B1) * g
    v = ADAM_B2 * v + (1.0 - ADAM_B2) * (g * g)
    m_hat = m / (1.0 - ADAM_B1 ** ADAM_STEP)
    v_hat = v / (1.0 - ADAM_B2 ** ADAM_STEP)
    delta = -ADAM_LR * (m_hat / (jnp.sqrt(v_hat) + ADAM_EPS) + ADAM_WD * w)
    return delta, m, v


def _place():
    x, y, c = lax.axis_index("x"), lax.axis_index("y"), lax.axis_index("c")
    return x, y, c, 4 * x + 2 * y + c


def _flip(x, y, c, k):
    px = 1 - x if (k >> 2) & 1 else x
    py = 1 - y if (k >> 1) & 1 else y
    pc = 1 - c if k & 1 else c
    return (px, py, pc), 4 * px + 2 * py + pc


def _gather_copy(buf_ref, send_sems, recv_sems, sem0, k, mine):
    x, y, c, me = _place()
    dev, idx = _flip(x, y, c, k)
    blk = me if mine else idx
    return pltpu.make_async_remote_copy(
        src_ref=buf_ref.at[blk], dst_ref=buf_ref.at[blk],
        send_sem=send_sems.at[sem0 + k - 1], recv_sem=recv_sems.at[sem0 + k - 1],
        device_id=dev, device_id_type=MESH)


def _entry_gather(c_row, cctx_row, wmod, bmod, shards):
    n = len(shards)
    ncol = wmod.shape[1]

    def body(*refs):
        c_ref, cctx_ref, wmod_ref, bmod_ref = refs[:4]
        in_refs = refs[4:4 + n]
        mod_ref, cs_ref = refs[4 + n:6 + n]
        out_refs = refs[6 + n:6 + 2 * n]
        cbuf, pbuf = refs[6 + 2 * n:8 + 2 * n]
        cast_refs = refs[8 + 2 * n:8 + 3 * n]
        small_send, small_recv, send_sems, recv_sems, local_sems = refs[8 + 3 * n:]
        x, y, c, me = _place()
        sib = (x, y, 1 - c)
        chips = [(1 - x, y), (x, 1 - y), (1 - x, 1 - y)]

        cbuf[me] = jnp.broadcast_to(c_ref[...], (8, D))
        small = [_gather_copy(cbuf, small_send, small_recv, 0, k, True) for k in range(1, NDEV)]
        for cp in small:
            cp.start()

        def blk(px, py, pc):
            return 4 * px + 2 * py + pc

        def copy(w, k, block, to, src=None):
            dst = out_refs[w].at[block]
            return pltpu.make_async_remote_copy(
                src_ref=dst if src is None else src, dst_ref=dst,
                send_sem=send_sems.at[w, k], recv_sem=recv_sems.at[w, k], device_id=to, device_id_type=MESH)

        first, passed, mine = [], [], []
        for w in range(n):
            cast_refs[w][...] = in_refs[w][...].astype(BF16)
            m = pltpu.make_async_copy(cast_refs[w], out_refs[w].at[me], local_sems.at[w])
            m.start()
            mine.append(m)
            cps = [copy(w, 0, me, sib, src=cast_refs[w])]
            cps += [copy(w, 1 + j, me, (*chip, c), src=cast_refs[w]) for j, chip in enumerate(chips)]
            for cp in cps:
                cp.start()
            first += cps

        for k in range(1, NDEV):
            _gather_copy(cbuf, small_send, small_recv, 0, k, False).wait_recv()
        row = lax.broadcasted_iota(jnp.int32, (16, D), 0)
        call = jnp.where(row == 8, jnp.broadcast_to(cctx_ref[...], (16, D)), 0.0)
        for d in range(NDEV):
            call = jnp.where(row == d, jnp.concatenate([cbuf[d], cbuf[d]], axis=0), call)
        cs = call * _sigmoid(call)
        cs_ref[...] = cs
        pbuf[me] = _dot(cs.astype(BF16), wmod_ref[...].astype(BF16))
        small2 = [_gather_copy(pbuf, small_send, small_recv, NDEV - 1, k, True) for k in range(1, NDEV)]
        for cp in small2:
            cp.start()

        for w in range(n):
            for j, chip in enumerate(chips):
                b = blk(*chip, c)
                copy(w, 1 + j, b, (x, y, c)).wait_recv()
                fw = copy(w, 4 + j, b, sib)
                fw.start()
                passed.append(fw)
        for w in range(n):
            copy(w, 0, blk(x, y, 1 - c), (x, y, c)).wait_recv()
            for j, chip in enumerate(chips):
                copy(w, 4 + j, blk(*chip, 1 - c), (x, y, c)).wait_recv()

        for k in range(1, NDEV):
            _gather_copy(pbuf, small_send, small_recv, NDEV - 1, k, False).wait_recv()
        for d in range(NDEV):
            mod_ref[:, d * ncol:(d + 1) * ncol] = pbuf[d] + bmod_ref[:, d * ncol:(d + 1) * ncol]
        for cp in small + small2 + first + passed:
            cp.wait_send()
        for m in mine:
            m.wait()

    vm = pl.BlockSpec(memory_space=pltpu.VMEM)
    hbm = pl.BlockSpec(memory_space=pltpu.HBM)
    return pl.pallas_call(
        body, name="entry_gather",
        in_specs=[vm] * (4 + n), out_specs=[vm, vm] + [hbm] * n,
        out_shape=(jax.ShapeDtypeStruct((16, 6 * D), F32), jax.ShapeDtypeStruct((16, D), F32))
        + tuple(jax.ShapeDtypeStruct((NDEV,) + s.shape, BF16) for s in shards),
        scratch_shapes=[pltpu.VMEM((NDEV, 8, D), F32), pltpu.VMEM((NDEV, 16, ncol), F32)]
        + [pltpu.VMEM(s.shape, BF16) for s in shards]
        + [pltpu.SemaphoreType.DMA((2 * (NDEV - 1),)), pltpu.SemaphoreType.DMA((2 * (NDEV - 1),)),
           pltpu.SemaphoreType.DMA((n, 7)), pltpu.SemaphoreType.DMA((n, 7)), pltpu.SemaphoreType.DMA((n,))],
        compiler_params=_cparams(),
    )(c_row, cctx_row, wmod, bmod, *shards)


_HBM = pl.BlockSpec(memory_space=pltpu.HBM)
_SEMS = pl.BlockSpec(memory_space=pltpu.SEMAPHORE)
_EFFECT = pltpu.SideEffectType.DATAFLOW_SIDE_EFFECTING


def _exchange_copy(src_refs, land_refs, send_sems, recv_sems, w, k, scatter, landing_slot_is_mine):
    x, y, c, me = _place()
    dev, pidx = _flip(x, y, c, k)
    src = src_refs[w].at[pidx] if scatter else src_refs[w]
    slot = me if landing_slot_is_mine else pidx
    return pltpu.make_async_remote_copy(
        src_ref=src, dst_ref=land_refs[w].at[slot],
        send_sem=send_sems.at[w * (NDEV - 1) + k - 1], recv_sem=recv_sems.at[w * (NDEV - 1) + k - 1],
        device_id=dev, device_id_type=MESH)


def _exchange_start(srcs, *, scatter, name):
    n = len(srcs)
    lands = [lax.empty((NDEV,) + tuple(s.shape[-2:]), s.dtype) for s in srcs]

    def body(*refs):
        src_refs, land_refs = refs[:n], refs[n:2 * n]
        send_sems, recv_sems = refs[2 * n], refs[2 * n + 1]
        token = refs[-1]
        for w in range(n):
            for k in range(1, NDEV):
                _exchange_copy(src_refs, land_refs, send_sems, recv_sems, w, k, scatter, True).start()
        token[...] = jnp.zeros_like(token)

    arrs = list(srcs) + lands
    out_shape = ([pltpu.SemaphoreType.DMA((n * (NDEV - 1),)), pltpu.SemaphoreType.DMA((n * (NDEV - 1),))]
                 + [pltpu.HBM(a.shape, a.dtype) for a in arrs] + [jax.ShapeDtypeStruct((8, 128), F32)])
    res = pl.pallas_call(
        body, name=name, out_shape=tuple(out_shape),
        in_specs=[_HBM] * (2 * n),
        out_specs=tuple([_SEMS, _SEMS] + [_HBM] * (2 * n) + [pl.BlockSpec(memory_space=pltpu.VMEM)]),
        input_output_aliases={i: 2 + i for i in range(2 * n)},
        compiler_params=pltpu.CompilerParams(has_side_effects=_EFFECT),
    )(*[pltpu.with_memory_space_constraint(a, pltpu.HBM) for a in arrs])
    return res[:-1], res[-1]


def _exchange_wait(handles, after, *, scatter, name):
    n = (len(handles) - 2) // 2
    na = len(after)

    def body(*refs):
        src_refs, land_refs = refs[:n], refs[n:2 * n]
        send_sems, recv_sems = refs[2 * n], refs[2 * n + 1]
        for w in range(n):
            for k in range(1, NDEV):
                cp = _exchange_copy(src_refs, land_refs, send_sems, recv_sems, w, k, scatter, False)
                cp.wait_send()
                cp.wait_recv()

    arrs = handles[2:]
    res = pl.pallas_call(
        body, name=name, out_shape=tuple(pltpu.HBM(a.shape, a.dtype) for a in arrs),
        in_specs=[_HBM] * (2 * n) + [_SEMS, _SEMS] + [_HBM] * na,
        out_specs=tuple([_HBM] * (2 * n)),
        input_output_aliases={i: i for i in range(2 * n)},
        compiler_params=pltpu.CompilerParams(has_side_effects=_EFFECT),
    )(*arrs, handles[0], handles[1], *[pltpu.with_memory_space_constraint(a, pltpu.HBM) for a in after])
    return res[:n], res[n:]


_SAME_CORE = (2, 4, 6)


def _gather2_copy(src_ref, land_ref, send_sems, recv_sems, sem, k, block):
    x, y, c, _ = _place()
    dev, _ = _flip(x, y, c, k)
    dst = land_ref.at[block]
    return pltpu.make_async_remote_copy(
        src_ref=dst if src_ref is None else src_ref, dst_ref=dst,
        send_sem=send_sems.at[sem], recv_sem=recv_sems.at[sem], device_id=dev, device_id_type=MESH)


def _split_call(body, name, arrs, n_sems, sems=None, after=(), token=False):
    na = len(arrs)
    out_shape, out_specs = [], []
    if n_sems is not None:
        out_shape += [pltpu.SemaphoreType.DMA((n_sems,)), pltpu.SemaphoreType.DMA((n_sems,))]
        out_specs += [_SEMS, _SEMS]
    first = len(out_shape)
    out_shape += [pltpu.HBM(a.shape, a.dtype) for a in arrs]
    out_specs += [_HBM] * na
    if token:
        out_shape.append(jax.ShapeDtypeStruct((8, 128), F32))
        out_specs.append(pl.BlockSpec(memory_space=pltpu.VMEM))
    in_specs = [_HBM] * na + ([_SEMS, _SEMS] if sems is not None else []) + [_HBM] * len(after)
    args = [pltpu.with_memory_space_constraint(a, pltpu.HBM) for a in arrs] + list(sems or ()) \
        + [pltpu.with_memory_space_constraint(a, pltpu.HBM) for a in after]
    return pl.pallas_call(
        body, name=name, out_shape=tuple(out_shape), in_specs=in_specs, out_specs=tuple(out_specs),
        input_output_aliases={i: first + i for i in range(na)},
        compiler_params=pltpu.CompilerParams(has_side_effects=_EFFECT))(*args)


def _gather2_start(srcs, *, name):
    n = len(srcs)
    lands = [lax.empty((NDEV,) + tuple(s.shape), s.dtype) for s in srcs]

    def body(*refs):
        src_refs, land_refs = refs[:n], refs[n:2 * n]
        send_sems, recv_sems = refs[2 * n], refs[2 * n + 1]
        _, _, _, me = _place()
        for w in range(n):
            for slot, k in enumerate((1,) + _SAME_CORE):
                _gather2_copy(src_refs[w], land_refs[w], send_sems, recv_sems, 4 * w + slot, k, me).start()
        refs[-1][...] = jnp.zeros_like(refs[-1])

    res = _split_call(body, name, list(srcs) + lands, 4 * n, token=True)
    return res[:2], res[2:-1], res[-1]


def _gather2_arrived(sems, arrs, after, *, name):
    n = len(arrs) // 2

    def body(*refs):
        src_refs, land_refs = refs[:n], refs[n:2 * n]
        send_sems, recv_sems = refs[2 * n], refs[2 * n + 1]
        x, y, c, me = _place()
        for w in range(n):
            for slot, k in enumerate((1,) + _SAME_CORE):
                _, pidx = _flip(x, y, c, k)
                _gather2_copy(src_refs[w], land_refs[w], send_sems, recv_sems, 4 * w + slot, k, me).wait_send()
                _gather2_copy(None, land_refs[w], send_sems, recv_sems, 4 * w + slot, k, pidx).wait_recv()

    return _split_call(body, name, arrs, None, sems=sems, after=after)


def _gather2_forward(lands, *, name):
    n = len(lands)

    def body(*refs):
        land_refs = refs[:n]
        send_sems, recv_sems = refs[n], refs[n + 1]
        x, y, c, _ = _place()
        for w in range(n):
            for j, k in enumerate(_SAME_CORE):
                _, pidx = _flip(x, y, c, k)
                _gather2_copy(None, land_refs[w], send_sems, recv_sems, 3 * w + j, 1, pidx).start()
        refs[-1][...] = jnp.zeros_like(refs[-1])

    res = _split_call(body, name, list(lands), 3 * n, token=True)
    return res[:2], res[2:-1], res[-1]


def _gather2_wait(sems, lands, after, *, name):
    n = len(lands)

    def body(*refs):
        land_refs = refs[:n]
        send_sems, recv_sems = refs[n], refs[n + 1]
        x, y, c, _ = _place()
        for w in range(n):
            for j, k in enumerate(_SAME_CORE):
                _, mine = _flip(x, y, c, k)
                _, theirs = _flip(x, y, c, k ^ 1)
                _gather2_copy(None, land_refs[w], send_sems, recv_sems, 3 * w + j, 1, mine).wait_send()
                _gather2_copy(None, land_refs[w], send_sems, recv_sems, 3 * w + j, 1, theirs).wait_recv()

    return _split_call(body, name, list(lands), None, sems=sems, after=after)


def _cast_bf16(arrs, *, name, after=()):
    n = len(arrs)

    def body(*refs):
        for i_ref, o_ref in zip(refs[:n], refs[n + len(after):]):
            o_ref[...] = i_ref[...].astype(BF16)

    return pl.pallas_call(
        body, name=name, out_shape=tuple(jax.ShapeDtypeStruct(a.shape, BF16) for a in arrs),
        in_specs=[pl.BlockSpec(memory_space=pltpu.VMEM)] * n + [pl.BlockSpec(memory_space=pl.ANY)] * len(after),
        compiler_params=_cparams())(*arrs, *after)


def _rs_adam(me_arr, fulls, lands, w, m, v, *, name):
    rows = lands[0].shape[1]
    k = len(fulls)
    nb = next(n for n in (4, 2, 1) if rows % (16 * n) == 0)
    br = rows // nb

    def body(me_ref, *refs):
        own_refs, land_refs = refs[:k], refs[k:2 * k]
        w_ref, m_ref, v_ref, g_ref, d_ref, mo_ref, vo_ref = refs[2 * k:]
        me = me_ref[0]
        parts = []
        for own_ref, land_ref in zip(own_refs, land_refs):
            acc = jnp.zeros(own_ref.shape, F32)
            for p in range(NDEV):
                acc = acc + jnp.where(p == me, own_ref[...], land_ref[p].astype(F32))
            parts.append(acc)
        acc = parts[0] if k == 1 else jnp.concatenate(parts, axis=1)
        g_ref[...] = acc
        d_ref[...], mo_ref[...], vo_ref[...] = _adam_math(w_ref[...], acc, m_ref[...], v_ref[...])

    sh = jax.ShapeDtypeStruct((rows, D), F32)
    blk = pl.BlockSpec((br, D), lambda i, me: (i, 0))
    grid_spec = pltpu.PrefetchScalarGridSpec(
        num_scalar_prefetch=1, grid=(nb,),
        in_specs=[pl.BlockSpec((br, f.shape[1]), lambda i, me: (me[0] * nb + i, 0)) for f in fulls]
        + [pl.BlockSpec((NDEV, br, l.shape[2]), lambda i, me: (0, i, 0)) for l in lands]
        + [blk, blk, blk],
        out_specs=[blk] * 4)
    return pl.pallas_call(
        body, name=name, out_shape=(sh, sh, sh, sh), grid_spec=grid_spec, compiler_params=_cparams(1),
    )(me_arr, *fulls, *lands, w, m, v)


ROW_F, ROW_I, ROW_C, ROW_G1, ROW_LG = 0, 8, 16, 24, 32
PACK_ROWS = 40


def _small_sum(me_arr, pack, pack_land, sgw, sgw_land, wmod):
    ncol = wmod.shape[1]

    def body(me_ref, pack_ref, pland_ref, sgw_ref, sland_ref, wmod_ref, sum_ref, all_ref, sgw_sum_ref, part_ref):
        me = me_ref[0]
        acc = jnp.zeros((PACK_ROWS, D), F32)
        acc_w = jnp.zeros(sgw_ref.shape, F32)
        for p in range(NDEV):
            rows = jnp.where(p == me, pack_ref[...], pland_ref[p])
            all_ref[p] = rows
            acc = acc + rows
            acc_w = acc_w + jnp.where(p == me, sgw_ref[...], sland_ref[p])
        sum_ref[...] = acc
        sgw_sum_ref[...] = acc_w
        zero = jnp.zeros((1, D), F32)
        dcmod = jnp.concatenate([acc[ROW_C:ROW_C + 1], acc[ROW_C + 1:ROW_C + 2], zero, zero, zero, zero], axis=1)
        mine = jnp.zeros((1, ncol), F32)
        for d in range(NDEV):
            mine = mine + jnp.where(d == me, dcmod[:, d * ncol:(d + 1) * ncol], 0.0)
        part_ref[...] = _dot_nt(jnp.broadcast_to(mine, (8, ncol)).astype(BF16), wmod_ref[...].astype(BF16))

    vm = pl.BlockSpec(memory_space=pltpu.VMEM)
    return pl.pallas_call(
        body, name="small_sum",
        out_shape=(jax.ShapeDtypeStruct((PACK_ROWS, D), F32), jax.ShapeDtypeStruct((NDEV, PACK_ROWS, D), F32),
                   jax.ShapeDtypeStruct(sgw.shape, F32), jax.ShapeDtypeStruct((8, D), F32)),
        in_specs=[pl.BlockSpec(memory_space=pltpu.SMEM)] + [vm] * 5,
        compiler_params=_cparams(),
    )(me_arr, pack, pack_land, sgw, sgw_land, wmod)


def _cctx_final(me_arr, part, part_land, cctx_row, m_row, v_row):
    def body(me_ref, part_ref, land_ref, c_ref, m_ref, v_ref, g_ref, d_ref, mo_ref, vo_ref):
        me = me_ref[0]
        tot = jnp.zeros((8, D), F32)
        for p in range(NDEV):
            tot = tot + jnp.where(p == me, part_ref[...], land_ref[p])
        cc = c_ref[...]
        _, dsilu = _silu_parts(cc)
        g = tot[0:1, :] * dsilu
        g_ref[...] = g
        d_ref[...], mo_ref[...], vo_ref[...] = _adam_math(cc, g, m_ref[...], v_ref[...])

    row = jax.ShapeDtypeStruct((1, D), F32)
    vm = pl.BlockSpec(memory_space=pltpu.VMEM)
    return pl.pallas_call(
        body, name="cctx_final", out_shape=(row, row, row, row),
        in_specs=[pl.BlockSpec(memory_space=pltpu.SMEM)] + [vm] * 5,
        compiler_params=_cparams(),
    )(me_arr, part, part_land, cctx_row, m_row, v_row)


def _adam_small(s, sgw_g, params):
    names = ["norm1", "norm2", "norm_f", "sg_gain", "sg_b", "ret_logit_f", "ret_logit_b", "b_mod", "sg_w"]
    flat = [a for n in names for a in params[n]]

    def body(*refs):
        s_ref, sgw_ref = refs[0], refs[1]
        p_refs = refs[2:2 + 3 * len(names)]
        o_refs = refs[2 + 3 * len(names):]
        loss_ref = o_refs[-1]

        def row(r):
            return s_ref[r:r + 1, :]

        grads = {
            "norm1": row(ROW_I + 2) + row(ROW_C + 2),
            "norm2": row(ROW_F + 4),
            "norm_f": row(ROW_F + 0),
            "sg_gain": s_ref[ROW_I + 3:ROW_I + 4, 0:AW],
            "sg_b": s_ref[ROW_I + 4:ROW_I + 8, 0:DH],
            "sg_w": sgw_ref[...],
        }
        for j, n in enumerate(names):
            w_ref, m_ref, v_ref = p_refs[3 * j:3 * j + 3]
            g_ref, d_ref, mo_ref, vo_ref = o_refs[4 * j:4 * j + 4]
            w = w_ref[...]
            if n in ("ret_logit_f", "ret_logit_b"):
                r = ROW_LG + (0 if n == "ret_logit_f" else 1)
                g = s_ref[r:r + 1, 0:H] * _sigmoid(-w)
            elif n == "b_mod":
                rows = [row(ROW_I + 0) + row(ROW_C + 0), row(ROW_I + 1) + row(ROW_C + 1), row(ROW_G1),
                        row(ROW_F + 2), row(ROW_F + 3), row(ROW_F + 1)]
                g = jnp.concatenate(rows, axis=1)
            else:
                g = grads[n]
            g_ref[...] = g
            d_ref[...], mo_ref[...], vo_ref[...] = _adam_math(w, g, m_ref[...], v_ref[...])
        loss_ref[...] = jnp.full((1, 1), 0.5 / D, F32) * jnp.sum(row(ROW_F + 5), axis=1, keepdims=True)

    out_shape = []
    for n in names:
        w = params[n][0]
        out_shape += [jax.ShapeDtypeStruct(w.shape, F32)] * 4
    out_shape.append(jax.ShapeDtypeStruct((1, 1), F32))
    outs = pl.pallas_call(body, name="adam_small", out_shape=tuple(out_shape), compiler_params=_cparams())(
        s, sgw_g, *flat)
    return {n: tuple(outs[4 * j:4 * j + 4]) for j, n in enumerate(names)}, outs[-1]


def _wmod_grad(cs_all, dmod_cols, wmod, m, v):
    ncol = wmod.shape[1]

    def body(cs_ref, dm_ref, w_ref, m_ref, v_ref, g_ref, d_ref, mo_ref, vo_ref):
        g = _dot_tn(cs_ref[...].astype(BF16), dm_ref[...].astype(BF16))
        g_ref[...] = g
        d_ref[...], mo_ref[...], vo_ref[...] = _adam_math(w_ref[...], g, m_ref[...], v_ref[...])

    sh = jax.ShapeDtypeStruct((D, ncol), F32)
    br = D // 4
    blk = pl.BlockSpec((br, ncol), lambda i: (i, 0))
    return pl.pallas_call(
        body, name="wmod_grad", out_shape=(sh, sh, sh, sh), grid=(D // br,),
        in_specs=[pl.BlockSpec((cs_all.shape[0], br), lambda i: (0, i)), _whole(dmod_cols.shape), blk, blk, blk],
        out_specs=[blk] * 4,
        compiler_params=_cparams(1),
    )(cs_all, dmod_cols, wmod, m, v)


def _rope_tables(t_len):
    n_freq = DH // 4
    inv = (ROPE_BASE ** (-np.arange(n_freq, dtype=np.float32) / n_freq)).astype(np.float32)
    tok = np.arange(t_len)
    rows = (tok // GRID_W).astype(np.float32)
    cols = (tok % GRID_W).astype(np.float32)
    ang_r = rows[:, None] * inv[None, :]
    ang_c = cols[:, None] * inv[None, :]
    cos = np.concatenate([np.cos(ang_r)] * 2 + [np.cos(ang_c)] * 2, axis=1).astype(np.float32)
    sin = np.concatenate([-np.sin(ang_r), np.sin(ang_r), -np.sin(ang_c), np.sin(ang_c)], axis=1).astype(np.float32)
    return jnp.asarray(cos), jnp.asarray(sin)


def _local_step(x, tgt, ctx, mod6, cmod6, norm1, norm2, normf, win, wout, ffn_weights, sgw, sgb, sggain, rlf, rlb,
                *, tm_a, tm_b, tm_f, tm_m, tm_r, tm_i, bt_w, after_first_grads=None, small_path=None,
                after_in_half=None):
    t_len = x.shape[0]
    cos, sin = _rope_tables(t_len)
    sgbt = jnp.broadcast_to(sgb[:, :, None], (G, C, 128))
    rlf = jnp.broadcast_to(rlf.reshape(H, 1), (H, 128))
    rlb = jnp.broadcast_to(rlb.reshape(H, 1), (H, 128))
    hc, kvc, scf, scb = _ctx_fwd(ctx, cmod6, norm1, win, rlf, rlb)
    hx, zuv, q, k, v, gfb, of, ya, sst = _fwd_a(x, mod6, norm1, win, sgw, sgbt, sggain, cos, sin, rlf, scf, tm=tm_a)
    if callable(wout):
        wout, tok = wout(hx)
        rlb = rlb + tok
    ob, ycat, y, x1, rst = _fwd_b(q, k, v, of, gfb, ya, x, mod6, wout, rlb, scb, tm=tm_b)
    wg, wu, wd = ffn_weights(x1) if callable(ffn_weights) else ffn_weights
    dx1, h2, da, db, hm, df, small_f = _ffn(x1, tgt, mod6, norm2, normf, wg, wu, wd, tm=tm_f)
    bt2 = min(2 * bt_w, t_len)
    d_wd, d_wd_b = _tn_matmul(hm, df, bm=DFF // 2, bt=bt2, name="dw_down")
    d_wg, d_wg_b = _tn_matmul(da, h2, bm=DFF // 2, bt=bt2, name="dw_gate")
    d_wu, d_wu_b = _tn_matmul(db, h2, bm=DFF // 2, bt=bt2, name="dw_up")
    dy, dya, dgfb, dof, dob, dg1 = _mid_bwd(dx1, y, of, ob, gfb, mod6, wout, tm=tm_m)
    d_wo, d_wo_b = _tn_matmul(ycat, dy, bm=D, bt=bt2, name="dw_out", transposed=False)
    if after_first_grads is not None:
        rlf = rlf + after_first_grads((d_wd_b, d_wg_b, d_wu_b, d_wo_b))
    dqf, dkf, dvf, dqb, dkb, dvb, dscf, dscb, dlg = _ret_bwd(q, k, v, dof, dob, sst, rst, rlf, rlb, tm=tm_r)
    gx, dz, small_i, dsgw = _in_bwd(x, zuv, dya, dqf, dkf, dvf, dqb, dkb, dvb, dgfb, dx1, cos, sin,
                                    mod6, norm1, win, sgw, sgbt, sggain, tm=tm_i)
    dwin_c, small_c, dlg = _ctx_bwd(ctx, hc, kvc, cmod6, norm1, win, rlf, rlb, dscf, dscb, dlg)
    pack = jnp.concatenate([small_f, small_i, small_c, dg1, dlg], axis=0)
    after = small_path(pack, dsgw) if small_path is not None else ()
    halves = []
    for j in range(2):
        halves.append(_tn_matmul(dz, hx, bm=INC // 2, bt=bt2, name="dw_in_%d" % j, init=dwin_c,
                                 init_rows=(KV_LO, KV_HI), after=after, cols=(j, D // 2), transposed=False))
        if after_in_half is not None:
            after = after_in_half(j, halves[-1][1])
    grads = {"w_in": halves, "w_out": (d_wo, d_wo_b), "w_gate": (d_wg, d_wg_b), "w_up": (d_wu, d_wu_b),
             "w_down": (d_wd, d_wd_b)}
    return gx, grads, pack, dsgw


def kernel(x, c, ctx, c_ctx, w_mod, b_mod, norm1, w_in, sg_gain, sg_w, sg_b, ret_logit_f, ret_logit_b, w_out, norm2, w_gate, w_up, w_down, norm_f, loss_target, m_c_ctx, m_w_mod, m_b_mod, m_norm1, m_w_in, m_sg_gain, m_sg_w, m_sg_b, m_ret_logit_f, m_ret_logit_b, m_w_out, m_norm2, m_w_gate, m_w_up, m_w_down, m_norm_f, v_c_ctx, v_w_mod, v_b_mod, v_norm1, v_w_in, v_sg_gain, v_sg_w, v_sg_b, v_ret_logit_f, v_ret_logit_b, v_w_out, v_norm2, v_w_gate, v_w_up, v_w_down, v_norm_f):
    t_len = x.shape[1]
    tm = min(512, t_len)
    me = 4 * lax.axis_index("x") + 2 * lax.axis_index("y") + lax.axis_index("c")
    tr = lambda a: jnp.transpose(a[0])

    cctx_row = c_ctx.reshape(1, D)
    mod_all, cs_all, g_in = _entry_gather(c, cctx_row, w_mod[0], b_mod, [tr(w_in)])
    mod6 = lax.dynamic_slice(mod_all, (me, 0), (1, 6 * D)).reshape(6, D)
    cmod6 = mod_all[8].reshape(6, D)
    win_t = g_in.reshape(INC, D)

    (out_shard,) = _cast_bf16([w_out[0]], name="cast_out", after=[g_in])
    h_out, tok_out = _exchange_start([out_shard], scatter=False, name="wout_start")
    ffn_shards = _cast_bf16([tr(w_gate), tr(w_up), w_down[0]], name="cast_ffn", after=[h_out[2]])
    sems_ffn, arrs_ffn, tok = _gather2_start(ffn_shards, name="wffn_start")
    mod6 = mod6 + (tok_out[0, 0] + tok[0, 0])
    ffn = {}

    def place_own(own, lands, rows):
        return [lax.dynamic_update_slice(l, o[None], (me, 0, 0)).reshape(rows, D) for o, l in zip(own, lands)]

    def wout(after):
        own, lands = _exchange_wait(h_out, [after], scatter=False, name="wout_wait")
        arrs = _gather2_arrived(sems_ffn, arrs_ffn, [after], name="wffn_arrived")
        ffn["own"] = arrs[:3]
        ffn["sems"], ffn["lands"], tok_fwd = _gather2_forward(arrs[3:], name="wffn_forward")
        return place_own(own, lands, D)[0], tok_fwd[0, 0]

    def ffn_weights(after):
        lands = _gather2_wait(ffn["sems"], ffn["lands"], [after], name="wffn_wait")
        return place_own(ffn["own"], lands, DFF)

    rs, outs = {}, {}

    def after_first_grads(bf):
        rs["first"], tok_first = _exchange_start([b.reshape(NDEV, b.shape[0] // NDEV, D) for b in bf], scatter=True,
                                                 name="rs_first_start")
        return tok_first[0, 0]

    def small_path(pack, dsgw):
        rs["small"], _ = _exchange_start([pack, dsgw.reshape(G * C, C)], scatter=False, name="small_start")
        return [rs["small"][2], rs["small"][3]]

    def after_in_half(j, half_bf16):
        rs["in%d" % j], _ = _exchange_start([half_bf16.reshape(NDEV, INC // NDEV, D // 2)], scatter=True,
                                            name="rs_in%d_start" % j)
        return [rs["in%d" % j][2]]

    gx, grads, pack, dsgw = _local_step(
        x[0], loss_target[0], ctx[0], mod6, cmod6, norm1, norm2[0:1], norm_f.reshape(1, D), win_t, wout, ffn_weights,
        sg_w[0], sg_b[0], sg_gain, ret_logit_f, ret_logit_b,
        tm_a=tm, tm_b=tm, tm_f=min(256, t_len), tm_m=min(1024, t_len), tm_r=min(1024, t_len), tm_i=tm,
        bt_w=min(1024, t_len),
        after_first_grads=after_first_grads, small_path=small_path, after_in_half=after_in_half)

    me_arr = me.reshape(1).astype(jnp.int32)
    (pack_own, sgw_own), (pack_land, sgw_land) = _exchange_wait(rs["small"], [rs["in1"][2]], scatter=False,
                                                               name="small_wait")
    psum_rows, pall, sgw_sum, part = _small_sum(me_arr, pack_own, pack_land, sgw_own, sgw_land, w_mod[0])
    h_cc, _ = _exchange_start([part], scatter=False, name="cctx_start")

    dmod_rows = (ROW_I + 0, ROW_I + 1, ROW_G1, ROW_F + 2, ROW_F + 3, ROW_F + 1)
    dmod_all = jnp.concatenate([pall[:, r, :] for r in dmod_rows], axis=1)
    dcmod = jnp.concatenate([psum_rows[ROW_C + 0:ROW_C + 1], psum_rows[ROW_C + 1:ROW_C + 2],
                             jnp.zeros((1, 4 * D), F32)], axis=1)
    dmod_mat = jnp.concatenate([dmod_all, jnp.pad(dcmod, ((0, 7), (0, 0)))], axis=0)
    ncol = 6 * D // NDEV
    dmod_cols = lax.dynamic_slice(dmod_mat, (0, me * ncol), (16, ncol))
    g_wm, d_wm, m_wm, v_wm = _wmod_grad(cs_all, dmod_cols, w_mod[0], m_w_mod[0], v_w_mod[0])
    outs["w_mod"] = (g_wm[None], d_wm[None], m_wm[None], v_wm[None])
    small_params = {
        "norm1": (norm1, m_norm1, v_norm1), "norm2": (norm2, m_norm2, v_norm2),
        "norm_f": tuple(a.reshape(1, D) for a in (norm_f, m_norm_f, v_norm_f)),
        "sg_gain": (sg_gain, m_sg_gain, v_sg_gain), "sg_b": (sg_b[0], m_sg_b[0], v_sg_b[0]),
        "ret_logit_f": (ret_logit_f, m_ret_logit_f, v_ret_logit_f),
        "ret_logit_b": (ret_logit_b, m_ret_logit_b, v_ret_logit_b),
        "b_mod": (b_mod, m_b_mod, v_b_mod), "sg_w": (sg_w[0], m_sg_w[0], v_sg_w[0])}
    small, loss = _adam_small(psum_rows, sgw_sum.reshape(G, C, C), small_params)
    back = {"norm_f": lambda a: a.reshape(D), "sg_b": lambda a: a[None], "sg_w": lambda a: a[None]}
    for name, vals in small.items():
        outs[name] = tuple(back.get(name, lambda a: a)(a) for a in vals)

    _, lands_first = _exchange_wait(rs["first"], [g_wm], scatter=True, name="rs_first_wait")

    def finish(name, fulls, lands, w, m, v, transposed):
        take = tr if transposed else (lambda a: a[0])
        res = _rs_adam(me_arr, fulls, lands, take(w), take(m), take(v), name="adam_" + name)
        put = (lambda a: jnp.transpose(a)[None]) if transposed else (lambda a: a[None])
        outs[name] = tuple(put(a) for a in res)
        return res[0]

    g_down = finish("w_down", [grads["w_down"][0]], [lands_first[0]], w_down, m_w_down, v_w_down, False)
    g_gate = finish("w_gate", [grads["w_gate"][0]], [lands_first[1]], w_gate, m_w_gate, v_w_gate, True)
    g_up = finish("w_up", [grads["w_up"][0]], [lands_first[2]], w_up, m_w_up, v_w_up, True)
    g_out = finish("w_out", [grads["w_out"][0]], [lands_first[3]], w_out, m_w_out, v_w_out, False)
    done = [g_down, g_gate, g_up, g_out]
    (part_own,), (part_land,) = _exchange_wait(h_cc, done, scatter=False, name="cctx_wait")
    res_cc = _cctx_final(me_arr, part_own, part_land, cctx_row, m_c_ctx.reshape(1, D), v_c_ctx.reshape(1, D))
    outs["c_ctx"] = tuple(a.reshape(D) for a in res_cc)
    lands_in = [_exchange_wait(rs["in%d" % j], done, scatter=True, name="rs_in%d_wait" % j)[1][0] for j in range(2)]
    finish("w_in", [h[0] for h in grads["w_in"]], lands_in, w_in, m_w_in, v_w_in, True)

    order = ["c_ctx", "w_mod", "b_mod", "norm1", "w_in", "sg_gain", "sg_w", "sg_b", "ret_logit_f", "ret_logit_b",
             "w_out", "norm2", "w_gate", "w_up", "w_down", "norm_f"]
    res = [loss.reshape(()), gx[None]]
    for j in range(4):
        res += [outs[name][j] for name in order]
    return tuple(res)
```

```python
import functools
import math

import numpy as np
import jax
import jax.numpy as jnp
from jax import lax
from jax.experimental import pallas as pl
from jax.experimental.pallas import tpu as pltpu

F32 = jnp.float32
BF16 = jnp.bfloat16

D = 1024
AW = 512
RW = 512
H = 4
DH = 128
G = 4
C = 128
CR = 256
DFF = 2816
INC = 3584
Q_LO = 2 * AW
KV_LO, VR_LO, G_LO = Q_LO + RW, Q_LO + 2 * RW, Q_LO + 3 * RW
KV_HI = G_LO
NDEV = 8
EPS = 1e-6
KSCALE = DH ** -0.5
ROPE_BASE = 10000.0
GRID_W = 64

ADAM_LR = 0.001
ADAM_B1 = 0.9
ADAM_B2 = 0.999
ADAM_EPS = 1e-08
ADAM_WD = 0.01
ADAM_STEP = 10

VMEM_LIMIT = 56 * 1024 * 1024
MESH = pl.DeviceIdType.MESH


def _cparams(n_grid=0, **kw):
    sem = ("arbitrary",) * n_grid if n_grid else None
    return pltpu.CompilerParams(dimension_semantics=sem, vmem_limit_bytes=VMEM_LIMIT, **kw)


def _dot(a, b):
    return jnp.dot(a, b, preferred_element_type=F32)


def _dot_nt(a, b):
    return lax.dot_general(a, b, (((1,), (1,)), ((), ())), preferred_element_type=F32)


def _dot_tn(a, b):
    return lax.dot_general(a, b, (((0,), (0,)), ((), ())), preferred_element_type=F32)


def _whole(shape):
    nd = len(shape)
    return pl.BlockSpec(shape, lambda *_: (0,) * nd, pipeline_mode=pl.Buffered(1))


def _acc(shape):
    nd = len(shape)
    return pl.BlockSpec(shape, lambda *_: (0,) * nd)


def _rows(tm, n):
    return pl.BlockSpec((tm, n), lambda i: (i, 0))


def _rows_rev(tm, n, nt):
    return pl.BlockSpec((tm, n), lambda i: (nt - 1 - i, 0))


def _sigmoid(x):
    return 1.0 / (1.0 + jnp.exp(-x))


def _log_sigmoid(x):
    return jnp.minimum(x, 0.0) - jnp.log(1.0 + jnp.exp(-jnp.abs(x)))


_GK0 = math.sqrt(2.0 / math.pi)
_GK1 = 0.044715


def _gelu(x):
    x2 = x * x
    t = jnp.tanh(x * (_GK0 + (_GK0 * _GK1) * x2))
    h = 0.5 + 0.5 * t
    g = x * h
    dg = h + g * (1.0 - h) * ((2.0 * _GK0) + (6.0 * _GK0 * _GK1) * x2)
    return g, dg


def _silu_parts(a):
    s = _sigmoid(a)
    sa = a * s
    return sa, s + sa * (1.0 - s)


def _rope(t, cos, sins):
    n = t.shape[1]
    lane = lax.broadcasted_iota(jnp.int32, t.shape, 1)
    first = (lane & 63) < 32
    partner = jnp.where(first, pltpu.roll(t, n - 32, 1), pltpu.roll(t, 32, 1))
    return t * cos + partner * sins


def _headnorm(o):
    outs, rs = [], []
    for h in range(H):
        oh = o[:, h * DH:(h + 1) * DH]
        r = lax.rsqrt(jnp.mean(oh * oh, axis=-1, keepdims=True) + EPS)
        outs.append(oh * r)
        rs.append(r)
    return jnp.concatenate(outs, axis=1), rs


def _decay_consts(lg, forward):
    ii = lax.broadcasted_iota(jnp.int32, (CR, CR), 0).astype(F32)
    jj = lax.broadcasted_iota(jnp.int32, (CR, CR), 1).astype(F32)
    ic = lax.broadcasted_iota(jnp.int32, (CR, 1), 0).astype(F32)
    if forward:
        diff = ii - jj
        xi = jnp.exp(lg * (ic + 1.0))
        z = jnp.exp(lg * (CR - 1.0 - ic))
    else:
        diff = jj - ii
        xi = jnp.exp(lg * (CR - ic))
        z = jnp.exp(lg * ic)
    dm = jnp.where(diff >= 0.0, jnp.exp(lg * jnp.maximum(diff, 0.0)), 0.0)
    dec = jnp.exp(lg * float(CR))
    return dm, xi, z, dec, ic


def _ctx_fwd(ctx, cmod6, norm1, win, rlf, rlb):
    lc = ctx.shape[0]

    def body(ctx_ref, cmod_ref, n1_ref, win_ref, rlf_ref, rlb_ref, hc_ref, kvc_ref, scf_ref, scb_ref):
        x = ctx_ref[...]
        r = lax.rsqrt(jnp.mean(x * x, axis=-1, keepdims=True) + EPS)
        hc = (x * r) * (n1_ref[...] * (1.0 + cmod_ref[1:2, :])) + cmod_ref[0:1, :]
        hcb = hc.astype(BF16)
        hc_ref[...] = hcb
        kv = _dot_nt(hcb, win_ref[KV_LO:KV_HI, :])
        k = kv[:, :RW] * KSCALE
        v = kv[:, RW:]
        kvc_ref[:, :RW] = k
        kvc_ref[:, RW:] = v
        t = lax.broadcasted_iota(jnp.int32, (lc, 1), 0).astype(F32)
        lgf = _log_sigmoid(rlf_ref[...])
        lgb = _log_sigmoid(rlb_ref[...])
        for h in range(H):
            hs = slice(h * DH, (h + 1) * DH)
            wf = jnp.exp(lgf[h:h + 1, 0:1] * (lc - 1.0 - t))
            wb = jnp.exp(lgb[h:h + 1, 0:1] * t)
            vh = v[:, hs].astype(BF16)
            scf_ref[h] = _dot_tn((k[:, hs] * wf).astype(BF16), vh)
            scb_ref[h] = _dot_tn((k[:, hs] * wb).astype(BF16), vh)

    return pl.pallas_call(
        body, name="ctx_fwd",
        out_shape=(jax.ShapeDtypeStruct((lc, D), BF16), jax.ShapeDtypeStruct((lc, 2 * RW), F32),
                   jax.ShapeDtypeStruct((H, DH, DH), F32), jax.ShapeDtypeStruct((H, DH, DH), F32)),
        compiler_params=_cparams(),
    )(ctx, cmod6, norm1, win, rlf, rlb)


def _sg_forward(u, v, sgw_ref, sgbt_ref, sgg_ref, nc):
    ua, dgu = _gelu(u)
    va, dgv = _gelu(v)
    gain = sgg_ref[...]
    mixed, vn_b, vah_l, rv_l = [], [], [], []
    for g in range(G):
        gs = slice(g * DH, (g + 1) * DH)
        vag = va[:, gs]
        rv = lax.rsqrt(jnp.mean(vag * vag, axis=-1, keepdims=True) + EPS)
        vah = vag * rv
        vn = (vah * gain[:, gs]).astype(BF16)
        wg = sgw_ref[g].astype(BF16)
        bcol = sgbt_ref[g]
        rows = [_dot(wg, vn[c * C:(c + 1) * C, :]) + bcol for c in range(nc)]
        mixed.append(jnp.concatenate(rows, axis=0) if nc > 1 else rows[0])
        vn_b.append(vn)
        vah_l.append(vah)
        rv_l.append(rv)
    mixed = jnp.concatenate(mixed, axis=1)
    return ua * mixed, (ua, dgu, dgv, mixed, vn_b, vah_l, rv_l)


def _fwd_a(x, mod6, norm1, win, sgw, sgbt, sggain, cos, sin, rlf, scf, *, tm):
    t_len = x.shape[0]
    nt, nc, ncr = t_len // tm, tm // C, tm // CR

    def body(x_ref, mod_ref, n1_ref, win_ref, sgw_ref, sgbt_ref, sgg_ref, cos_ref, sin_ref, rlf_ref, scf_ref,
             hx_ref, zuv_ref, q_ref, k_ref, v_ref, gfb_ref, of_ref, ya_ref, sst_ref, s_scr):
        i = pl.program_id(0)

        @pl.when(i == 0)
        def _():
            s_scr[...] = scf_ref[...]

        x = x_ref[...]
        r = lax.rsqrt(jnp.mean(x * x, axis=-1, keepdims=True) + EPS)
        hx = (x * r) * (n1_ref[...] * (1.0 + mod_ref[1:2, :])) + mod_ref[0:1, :]
        hxb = hx.astype(BF16)
        hx_ref[...] = hxb
        z = _dot_nt(hxb, win_ref[...])
        zuv_ref[...] = z[:, :2 * AW].astype(BF16)
        gfb_ref[...] = z[:, G_LO:INC].astype(BF16)
        cos = jnp.tile(cos_ref[...], (1, H))
        sins = jnp.tile(sin_ref[...], (1, H))
        q_ref[...] = _rope(z[:, Q_LO:KV_LO], cos, sins).astype(BF16)
        k_ref[...] = (_rope(z[:, KV_LO:VR_LO], cos, sins) * KSCALE).astype(BF16)
        v_ref[...] = z[:, VR_LO:G_LO].astype(BF16)
        ya, _ = _sg_forward(z[:, :AW], z[:, AW:2 * AW], sgw_ref, sgbt_ref, sgg_ref, nc)
        ya_ref[...] = ya.astype(BF16)

        lg = _log_sigmoid(rlf_ref[...])
        for h in range(H):
            dm, xi, zc, dec, _ = _decay_consts(lg[h:h + 1, 0:1], True)
            hs = slice(h * DH, (h + 1) * DH)
            for c in range(ncr):
                rs = slice(c * CR, (c + 1) * CR)
                qc, kc, vc = q_ref[rs, hs], k_ref[rs, hs], v_ref[rs, hs]
                s = s_scr[h]
                sst_ref[c, h] = s
                a = (_dot_nt(qc, kc) * dm).astype(BF16)
                of_ref[rs, hs] = (_dot(a, vc) + xi * _dot(qc, s.astype(BF16))).astype(BF16)
                kz = (kc.astype(F32) * zc).astype(BF16)
                s_scr[h] = dec * s + _dot_tn(kz, vc)

    n_chunks = t_len // CR
    return pl.pallas_call(
        body, name="fwd_a", grid=(nt,),
        in_specs=[_rows(tm, D), _whole((6, D)), _whole((1, D)), _whole((INC, D)), _whole((G, C, C)),
                  _whole((G, C, 128)), _whole((1, AW)), _rows(tm, DH), _rows(tm, DH), _whole((H, 128)),
                  _whole((H, DH, DH))],
        out_specs=[_rows(tm, D), _rows(tm, 2 * AW), _rows(tm, RW), _rows(tm, RW), _rows(tm, RW),
                   _rows(tm, 2 * RW), _rows(tm, RW), _rows(tm, AW),
                   pl.BlockSpec((ncr, H, DH, DH), lambda i: (i, 0, 0, 0))],
        out_shape=(jax.ShapeDtypeStruct((t_len, D), BF16), jax.ShapeDtypeStruct((t_len, 2 * AW), BF16),
                   jax.ShapeDtypeStruct((t_len, RW), BF16), jax.ShapeDtypeStruct((t_len, RW), BF16),
                   jax.ShapeDtypeStruct((t_len, RW), BF16), jax.ShapeDtypeStruct((t_len, 2 * RW), BF16),
                   jax.ShapeDtypeStruct((t_len, RW), BF16), jax.ShapeDtypeStruct((t_len, AW), BF16),
                   jax.ShapeDtypeStruct((n_chunks, H, DH, DH), F32)),
        scratch_shapes=[pltpu.VMEM((H, DH, DH), F32)],
        compiler_params=_cparams(1),
    )(x, mod6, norm1, win, sgw, sgbt, sggain, cos, sin, rlf, scf)


def _fwd_b(q, k, v, of, gfb, ya, x, mod6, wout, rlb, scb, *, tm):
    t_len = x.shape[0]
    nt, nc = t_len // tm, tm // CR

    def body(q_ref, k_ref, v_ref, of_ref, gfb_ref, ya_ref, x_ref, mod_ref, wout_ref, rlb_ref, scb_ref,
             ob_ref, ycat_ref, y_ref, x1_ref, rst_ref, r_scr):
        i = pl.program_id(0)

        @pl.when(i == 0)
        def _():
            r_scr[...] = scb_ref[...]

        lg = _log_sigmoid(rlb_ref[...])
        consts = [_decay_consts(lg[h:h + 1, 0:1], False) for h in range(H)]

        def first(c, h):
            dm, _, zc, dec, _ = consts[h]
            hs, rs = slice(h * DH, (h + 1) * DH), slice(c * CR, (c + 1) * CR)
            qc, kc, vc = q_ref[rs, hs], k_ref[rs, hs], v_ref[rs, hs]
            s = r_scr[h]
            rst_ref[c, h] = s
            scores = _dot_nt(qc, kc)
            cross = _dot(qc, s.astype(BF16))
            kz = (kc.astype(F32) * zc).astype(BF16)
            r_scr[h] = dec * s + _dot_tn(kz, vc)
            return (scores * dm).astype(BF16), cross

        def second(c, h, carried):
            a, cross = carried
            hs, rs = slice(h * DH, (h + 1) * DH), slice(c * CR, (c + 1) * CR)
            ob_ref[rs, hs] = (_dot(a, v_ref[rs, hs]) + consts[h][1] * cross).astype(BF16)

        def rows_out(c):
            rs = slice(c * CR, (c + 1) * CR)
            gfb = gfb_ref[rs, :].astype(F32)
            sf, _ = _silu_parts(gfb[:, :RW])
            sb, _ = _silu_parts(gfb[:, RW:])
            hnf, _ = _headnorm(of_ref[rs, :].astype(F32))
            hnb, _ = _headnorm(ob_ref[rs, :].astype(F32))
            yr = sf * hnf + sb * hnb
            ycat = jnp.concatenate([ya_ref[rs, :], yr.astype(BF16)], axis=1)
            ycat_ref[rs, :] = ycat
            y = _dot(ycat, wout_ref[...])
            y_ref[rs, :] = y.astype(BF16)
            x1_ref[rs, :] = x_ref[rs, :] + mod_ref[2:3, :] * y

        pending, waiting_rows = None, None
        for c in reversed(range(nc)):
            for h in range(H):
                carried = first(c, h)
                if pending is not None:
                    second(*pending)
                pending = (c, h, carried)
            if waiting_rows is not None:
                rows_out(waiting_rows)
            waiting_rows = c
        second(*pending)
        rows_out(waiting_rows)

    n_chunks = t_len // CR
    rr = functools.partial(_rows_rev, nt=nt)
    return pl.pallas_call(
        body, name="fwd_b", grid=(nt,),
        in_specs=[rr(tm, RW), rr(tm, RW), rr(tm, RW), rr(tm, RW), rr(tm, 2 * RW), rr(tm, AW), rr(tm, D),
                  _whole((6, D)), _whole((D, D)), _whole((H, 128)), _whole((H, DH, DH))],
        out_specs=[rr(tm, RW), rr(tm, D), rr(tm, D), rr(tm, D),
                   pl.BlockSpec((nc, H, DH, DH), lambda i: (nt - 1 - i, 0, 0, 0))],
        out_shape=(jax.ShapeDtypeStruct((t_len, RW), BF16), jax.ShapeDtypeStruct((t_len, D), BF16),
                   jax.ShapeDtypeStruct((t_len, D), BF16), jax.ShapeDtypeStruct((t_len, D), F32),
                   jax.ShapeDtypeStruct((n_chunks, H, DH, DH), F32)),
        scratch_shapes=[pltpu.VMEM((H, DH, DH), F32)],
        compiler_params=_cparams(1),
    )(q, k, v, of, gfb, ya, x, mod6, wout, rlb, scb)


def _ffn(x1, tgt, mod6, norm2, normf, wg, wu, wd, *, tm):
    t_len = x1.shape[0]
    nt = t_len // tm

    def body(x1_ref, tgt_ref, mod_ref, n2_ref, nf_ref, wg_ref, wu_ref, wd_ref,
             dx1_ref, h2_ref, da_ref, db_ref, hm_ref, df_ref, small_ref):
        i = pl.program_id(0)

        @pl.when(i == 0)
        def _():
            small_ref[...] = jnp.zeros_like(small_ref)

        sh2, sc2, g2 = mod_ref[3:4, :], mod_ref[4:5, :], mod_ref[5:6, :]
        n2, nf = n2_ref[...], nf_ref[...]
        x1 = x1_ref[...]
        r2 = lax.rsqrt(jnp.mean(x1 * x1, axis=-1, keepdims=True) + EPS)
        x1h = x1 * r2
        w2 = n2 * (1.0 + sc2)
        h2b = (x1h * w2 + sh2).astype(BF16)
        h2_ref[...] = h2b
        a = _dot_nt(h2b, wg_ref[...])
        b = _dot_nt(h2b, wu_ref[...])
        sa, dsa = _silu_parts(a)
        hmb = (sa * b).astype(BF16)
        hm_ref[...] = hmb
        f = _dot(hmb, wd_ref[...])
        x2 = x1 + g2 * f
        r3 = lax.rsqrt(jnp.mean(x2 * x2, axis=-1, keepdims=True) + EPS)
        x2h = x2 * r3
        diff = x2h * nf - tgt_ref[...]
        small_ref[5:6, :] += jnp.sum(diff * diff, axis=0, keepdims=True)
        dout = diff * (1.0 / D)
        small_ref[0:1, :] += jnp.sum(dout * x2h, axis=0, keepdims=True)
        dyg = dout * nf
        dx2 = r3 * (dyg - x2h * jnp.mean(dyg * x2h, axis=-1, keepdims=True))
        small_ref[1:2, :] += jnp.sum(dx2 * f, axis=0, keepdims=True)
        dfb = (dx2 * g2).astype(BF16)
        df_ref[...] = dfb
        dhm = _dot_nt(dfb, wd_ref[...])
        dbb = (dhm * sa).astype(BF16)
        dab = (dhm * b * dsa).astype(BF16)
        da_ref[...] = dab
        db_ref[...] = dbb
        dh2 = _dot(dab, wg_ref[...]) + _dot(dbb, wu_ref[...])
        small_ref[2:3, :] += jnp.sum(dh2, axis=0, keepdims=True)
        dhx = dh2 * x1h
        small_ref[3:4, :] += jnp.sum(dhx, axis=0, keepdims=True) * n2
        small_ref[4:5, :] += jnp.sum(dhx, axis=0, keepdims=True) * (1.0 + sc2)
        dyg2 = dh2 * w2
        dx1_ref[...] = dx2 + r2 * (dyg2 - x1h * jnp.mean(dyg2 * x1h, axis=-1, keepdims=True))

    return pl.pallas_call(
        body, name="ffn", grid=(nt,),
        in_specs=[_rows(tm, D), _rows(tm, D), _whole((6, D)), _whole((1, D)), _whole((1, D)),
                  _whole((DFF, D)), _whole((DFF, D)), _whole((DFF, D))],
        out_specs=[_rows(tm, D), _rows(tm, D), _rows(tm, DFF), _rows(tm, DFF), _rows(tm, DFF), _rows(tm, D),
                   _acc((8, D))],
        out_shape=(jax.ShapeDtypeStruct((t_len, D), F32), jax.ShapeDtypeStruct((t_len, D), BF16),
                   jax.ShapeDtypeStruct((t_len, DFF), BF16), jax.ShapeDtypeStruct((t_len, DFF), BF16),
                   jax.ShapeDtypeStruct((t_len, DFF), BF16), jax.ShapeDtypeStruct((t_len, D), BF16),
                   jax.ShapeDtypeStruct((8, D), F32)),
        compiler_params=_cparams(1),
    )(x1, tgt, mod6, norm2, normf, wg, wu, wd)


def _mid_bwd(dx1, y, of, ob, gfb, mod6, wout, *, tm):
    t_len = dx1.shape[0]
    nt = t_len // tm
    rc = min(256, tm)

    def body(dx1_ref, y_ref, of_ref, ob_ref, gfb_ref, mod_ref, wout_ref,
             dy_ref, dya_ref, dgfb_ref, dof_ref, dob_ref, dg1_ref):
        i = pl.program_id(0)

        @pl.when(i == 0)
        def _():
            dg1_ref[...] = jnp.zeros_like(dg1_ref)

        for c in range(tm // rc):
            rows = slice(c * rc, (c + 1) * rc)
            dx1 = dx1_ref[rows, :]
            dg1_ref[0:1, :] += jnp.sum(dx1 * y_ref[rows, :].astype(F32), axis=0, keepdims=True)
            dyb = (dx1 * mod_ref[2:3, :]).astype(BF16)
            dy_ref[rows, :] = dyb
            dycat = _dot_nt(dyb, wout_ref[...])
            dya_ref[rows, :] = dycat[:, :AW].astype(BF16)
            dyr = dycat[:, AW:]
            gfb = gfb_ref[rows, :].astype(F32)
            for o_ref, do_ref, lo in ((of_ref, dof_ref, 0), (ob_ref, dob_ref, RW)):
                sg, dsg = _silu_parts(gfb[:, lo:lo + RW])
                o = o_ref[rows, :].astype(F32)
                hn, rs = _headnorm(o)
                dgfb_ref[rows, lo:lo + RW] = (dyr * hn * dsg).astype(BF16)
                dhn = dyr * sg
                for h in range(H):
                    hs = slice(h * DH, (h + 1) * DH)
                    oh, dh = hn[:, hs], dhn[:, hs]
                    do_ref[rows, hs] = (rs[h] * (dh - oh * jnp.mean(dh * oh, axis=-1, keepdims=True))).astype(BF16)

    return pl.pallas_call(
        body, name="mid_bwd", grid=(nt,),
        in_specs=[_rows(tm, D), _rows(tm, D), _rows(tm, RW), _rows(tm, RW), _rows(tm, 2 * RW),
                  _whole((6, D)), _whole((D, D))],
        out_specs=[_rows(tm, D), _rows(tm, AW), _rows(tm, 2 * RW), _rows(tm, RW), _rows(tm, RW), _acc((8, D))],
        out_shape=(jax.ShapeDtypeStruct((t_len, D), BF16), jax.ShapeDtypeStruct((t_len, AW), BF16),
                   jax.ShapeDtypeStruct((t_len, 2 * RW), BF16), jax.ShapeDtypeStruct((t_len, RW), BF16),
                   jax.ShapeDtypeStruct((t_len, RW), BF16), jax.ShapeDtypeStruct((8, D), F32)),
        compiler_params=_cparams(1),
    )(dx1, y, of, ob, gfb, mod6, wout)


def _ret_bwd_items(q_ref, k_ref, v_ref, do_ref, st_ref, dq_ref, dk_ref, dv_ref, ds_scr, dlg_scr, lg, forward, nc, row0):
    order = list(reversed(range(nc))) if forward else list(range(nc))
    consts = [_decay_consts(lg[h:h + 1, 0:1], forward) for h in range(H)]
    accs = [jnp.zeros((1, DH), F32) for _ in range(H)]

    def first(h, c):
        dm, xi, zc, dec, _ = consts[h]
        hs, rs = slice(h * DH, (h + 1) * DH), slice(c * CR, (c + 1) * CR)
        qc, kc, vc, doc = q_ref[rs, hs], k_ref[rs, hs], v_ref[rs, hs], do_ref[rs, hs]
        s, dsn = st_ref[c, h], ds_scr[h]
        sb, dsnb = s.astype(BF16), dsn.astype(BF16)
        qf, kf = qc.astype(F32), kc.astype(F32)
        a_raw = _dot_nt(qc, kc)
        da_raw = _dot_nt(doc, vc)
        xdo = (xi * doc.astype(F32)).astype(BF16)
        kz = (kf * zc).astype(BF16)
        vz = (vc.astype(F32) * zc).astype(BF16)
        dq_c = _dot_nt(xdo, sb)
        dk_s = _dot_nt(vz, dsnb)
        dv_s = _dot(kz, dsnb)
        ds_scr[h] = _dot_tn((xi * qf).astype(BF16), doc) + dec * dsn
        accs[h] = accs[h] + (float(CR) * dec) * jnp.sum(s * dsn, axis=0, keepdims=True)
        a = (a_raw * dm).astype(BF16)
        da = (da_raw * dm).astype(BF16)
        return a, da, dq_c, dk_s, dv_s

    def second(h, c, carried):
        a, da, dq_c, dk_s, dv_s = carried
        ic = consts[h][4]
        if forward:
            w_qi, w_qc, w_ki, w_ks = ic, ic + 1.0, -ic, (CR - 1.0) - ic
        else:
            w_qi, w_qc, w_ki, w_ks = -ic, CR - ic, ic, ic
        hs, rs = slice(h * DH, (h + 1) * DH), slice(c * CR, (c + 1) * CR)
        qc, kc, doc = q_ref[rs, hs], k_ref[rs, hs], do_ref[rs, hs]
        dq_i = _dot(da, kc)
        dk_i = _dot_tn(da, qc)
        dv_i = _dot_tn(a, doc)
        dq_ref[rs, hs] = (dq_i + dq_c).astype(BF16)
        dk_ref[rs, hs] = (dk_i + dk_s).astype(BF16)
        dv_ref[rs, hs] = (dv_i + dv_s).astype(BF16)
        rowterm = qc.astype(F32) * (w_qi * dq_i + w_qc * dq_c) + kc.astype(F32) * (w_ki * dk_i + w_ks * dk_s)
        accs[h] = accs[h] + jnp.sum(rowterm, axis=0, keepdims=True)

    def finish():
        for h in range(H):
            dlg_scr[row0 + h:row0 + h + 1, :] += accs[h]

    items = [[(functools.partial(first, h, c), functools.partial(second, h, c)) for h in range(H)] for c in order]
    return items, finish


def _ret_bwd_run(dirs):
    nc = len(dirs[0][0])
    flat = [it for j in range(nc) for items, _ in dirs for it in items[j]]
    pending = None
    for first, second in flat:
        carried = first()
        if pending is not None:
            pending[0](pending[1])
        pending = (second, carried)
    pending[0](pending[1])
    for _, finish in dirs:
        finish()


def _ret_bwd(q, k, v, dof, dob, sst, rst, rlf, rlb, *, tm):
    t_len = q.shape[0]
    nt, nc = t_len // tm, tm // CR

    def body(qf_ref, kf_ref, vf_ref, dof_ref, sst_ref, qb_ref, kb_ref, vb_ref, dob_ref, rst_ref, rlf_ref, rlb_ref,
             dqf_ref, dkf_ref, dvf_ref, dqb_ref, dkb_ref, dvb_ref, dscf_ref, dscb_ref, dlg_ref,
             dsf_scr, dsb_scr, dlg_scr):
        i = pl.program_id(0)

        @pl.when(i == 0)
        def _():
            dsf_scr[...] = jnp.zeros_like(dsf_scr)
            dsb_scr[...] = jnp.zeros_like(dsb_scr)
            dlg_scr[...] = jnp.zeros_like(dlg_scr)

        lgf = _log_sigmoid(rlf_ref[...])
        lgb = _log_sigmoid(rlb_ref[...])
        _ret_bwd_run([
            _ret_bwd_items(qf_ref, kf_ref, vf_ref, dof_ref, sst_ref, dqf_ref, dkf_ref, dvf_ref, dsf_scr, dlg_scr,
                           lgf, True, nc, 0),
            _ret_bwd_items(qb_ref, kb_ref, vb_ref, dob_ref, rst_ref, dqb_ref, dkb_ref, dvb_ref, dsb_scr, dlg_scr,
                           lgb, False, nc, H)])

        @pl.when(i == nt - 1)
        def _():
            dscf_ref[...] = dsf_scr[...]
            dscb_ref[...] = dsb_scr[...]
            tot = jnp.broadcast_to(jnp.sum(dlg_scr[...], axis=1, keepdims=True), (8, 128))
            ri = lax.broadcasted_iota(jnp.int32, (8, 128), 0)
            li = lax.broadcasted_iota(jnp.int32, (8, 128), 1)
            dlg_ref[...] = jnp.zeros_like(dlg_ref)
            dlg_ref[0:1, 0:128] = jnp.sum(jnp.where(ri == li, tot, 0.0), axis=0, keepdims=True)
            dlg_ref[1:2, 0:128] = jnp.sum(jnp.where(ri - H == li, tot, 0.0), axis=0, keepdims=True)

    rr = functools.partial(_rows_rev, nt=nt)
    st_f = pl.BlockSpec((nc, H, DH, DH), lambda i: (nt - 1 - i, 0, 0, 0))
    st_b = pl.BlockSpec((nc, H, DH, DH), lambda i: (i, 0, 0, 0))
    tok = jax.ShapeDtypeStruct((t_len, RW), BF16)
    st = jax.ShapeDtypeStruct((H, DH, DH), F32)
    return pl.pallas_call(
        body, name="ret_bwd", grid=(nt,),
        in_specs=[rr(tm, RW), rr(tm, RW), rr(tm, RW), rr(tm, RW), st_f,
                  _rows(tm, RW), _rows(tm, RW), _rows(tm, RW), _rows(tm, RW), st_b,
                  _whole((H, 128)), _whole((H, 128))],
        out_specs=[rr(tm, RW), rr(tm, RW), rr(tm, RW), _rows(tm, RW), _rows(tm, RW), _rows(tm, RW),
                   _acc((H, DH, DH)), _acc((H, DH, DH)), _acc((8, D))],
        out_shape=(tok, tok, tok, tok, tok, tok, st, st, jax.ShapeDtypeStruct((8, D), F32)),
        scratch_shapes=[pltpu.VMEM((H, DH, DH), F32), pltpu.VMEM((H, DH, DH), F32), pltpu.VMEM((8, 128), F32)],
        compiler_params=_cparams(1),
    )(q, k, v, dof, sst, q, k, v, dob, rst, rlf, rlb)


def _in_bwd(x, zuv, dya, dqf, dkf, dvf, dqb, dkb, dvb, dgfb, dx1, cos, sin, mod6, norm1, win, sgw, sgbt, sggain, *, tm):
    t_len = x.shape[0]
    nt, nc = t_len // tm, tm // C

    def body(x_ref, zuv_ref, dya_ref, dqf_ref, dkf_ref, dvf_ref, dqb_ref, dkb_ref, dvb_ref, dgfb_ref, dx1_ref,
             cos_ref, sin_ref, mod_ref, n1_ref, win_ref, sgw_ref, sgbt_ref, sgg_ref,
             gx_ref, dz_ref, small_ref, dsgw_ref, dsgbt_ref):
        i = pl.program_id(0)

        @pl.when(i == 0)
        def _():
            small_ref[...] = jnp.zeros_like(small_ref)
            dsgw_ref[...] = jnp.zeros_like(dsgw_ref)
            dsgbt_ref[...] = jnp.zeros_like(dsgbt_ref)

        zuv = zuv_ref[...].astype(F32)
        _, (ua, dgu, dgv, mixed, vn_b, vah_l, rv_l) = _sg_forward(zuv[:, :AW], zuv[:, AW:], sgw_ref, sgbt_ref, sgg_ref, nc)
        dya = dya_ref[...].astype(F32)
        dz_ref[:, 0:AW] = (dya * mixed * dgu).astype(BF16)
        dmixed = dya * ua
        gain = sgg_ref[...]
        for g in range(G):
            gs = slice(g * DH, (g + 1) * DH)
            dmg = dmixed[:, gs]
            dmb = dmg.astype(BF16)
            wgt = sgw_ref[g].astype(BF16)
            dsw = jnp.zeros((C, C), F32)
            dsb = jnp.zeros((C, DH), F32)
            rows = []
            for c in range(nc):
                rs = slice(c * C, (c + 1) * C)
                dsw = dsw + _dot_nt(dmb[rs, :], vn_b[g][rs, :])
                dsb = dsb + dmg[rs, :]
                rows.append(_dot_tn(wgt, dmb[rs, :]))
            dsgw_ref[g] += dsw
            dsgbt_ref[g] += dsb
            dvn = jnp.concatenate(rows, axis=0) if nc > 1 else rows[0]
            vah, rv = vah_l[g], rv_l[g]
            small_ref[3:4, gs] += jnp.sum(dvn * vah, axis=0, keepdims=True)
            dyg = dvn * gain[:, gs]
            dva = rv * (dyg - vah * jnp.mean(dyg * vah, axis=-1, keepdims=True))
            dz_ref[:, AW + g * DH:AW + (g + 1) * DH] = (dva * dgv[:, gs]).astype(BF16)

        cos = jnp.tile(cos_ref[...], (1, H))
        nsins = -jnp.tile(sin_ref[...], (1, H))
        def both(f_ref, b_ref):
            return f_ref[...].astype(F32) + b_ref[...].astype(F32)

        dz_ref[:, Q_LO:KV_LO] = _rope(both(dqf_ref, dqb_ref), cos, nsins).astype(BF16)
        dz_ref[:, KV_LO:VR_LO] = (_rope(both(dkf_ref, dkb_ref), cos, nsins) * KSCALE).astype(BF16)
        dz_ref[:, VR_LO:G_LO] = both(dvf_ref, dvb_ref).astype(BF16)
        dz_ref[:, G_LO:INC] = dgfb_ref[...]

        dhx = _dot(dz_ref[...], win_ref[...])
        x = x_ref[...]
        r = lax.rsqrt(jnp.mean(x * x, axis=-1, keepdims=True) + EPS)
        xh = x * r
        n1, sc1 = n1_ref[...], mod_ref[1:2, :]
        small_ref[0:1, :] += jnp.sum(dhx, axis=0, keepdims=True)
        dhxx = jnp.sum(dhx * xh, axis=0, keepdims=True)
        small_ref[1:2, :] += dhxx * n1
        small_ref[2:3, :] += dhxx * (1.0 + sc1)
        dyg = dhx * (n1 * (1.0 + sc1))
        gx_ref[...] = dx1_ref[...] + r * (dyg - xh * jnp.mean(dyg * xh, axis=-1, keepdims=True))

        @pl.when(i == nt - 1)
        def _():
            ri = lax.broadcasted_iota(jnp.int32, (C, DH), 0)
            li = lax.broadcasted_iota(jnp.int32, (C, DH), 1)
            for g in range(G):
                tot = jnp.broadcast_to(jnp.sum(dsgbt_ref[g], axis=1, keepdims=True), (C, DH))
                small_ref[4 + g:5 + g, 0:DH] = jnp.sum(jnp.where(ri == li, tot, 0.0), axis=0, keepdims=True)

    tok = functools.partial(_rows, tm)
    return pl.pallas_call(
        body, name="in_bwd", grid=(nt,),
        in_specs=[tok(D), tok(2 * AW), tok(AW), tok(RW), tok(RW), tok(RW), tok(RW), tok(RW), tok(RW),
                  tok(2 * RW), tok(D), tok(DH), tok(DH), _whole((6, D)), _whole((1, D)), _whole((INC, D)),
                  _whole((G, C, C)), _whole((G, C, 128)), _whole((1, AW))],
        out_specs=[tok(D), tok(INC), _acc((8, D)), _acc((G, C, C))],
        out_shape=(jax.ShapeDtypeStruct((t_len, D), F32), jax.ShapeDtypeStruct((t_len, INC), BF16),
                   jax.ShapeDtypeStruct((8, D), F32), jax.ShapeDtypeStruct((G, C, C), F32)),
        scratch_shapes=[pltpu.VMEM((G, C, 128), F32)],
        compiler_params=_cparams(1),
    )(x, zuv, dya, dqf, dkf, dvf, dqb, dkb, dvb, dgfb, dx1, cos, sin, mod6, norm1, win, sgw, sgbt, sggain)


def _tn_matmul(a, b, *, bm, bt, name, init=None, init_rows=None, after=(), cols=None):
    t_len, m = a.shape
    cj, n = (0, b.shape[1]) if cols is None else cols
    ni, ntt = m // bm, t_len // bt
    has_init = init is not None

    def body(*refs):
        o_ref, ob_ref = refs[-2:]
        a_ref, b_ref = refs[:2]
        init_ref = refs[2] if has_init else None
        i, t = pl.program_id(0), pl.program_id(1)

        @pl.when(t == 0)
        def _():
            o_ref[...] = jnp.zeros_like(o_ref)

        if has_init:
            for ib in range(ni):
                lo, hi = max(init_rows[0], ib * bm), min(init_rows[1], (ib + 1) * bm)
                if lo < hi:
                    @pl.when(jnp.logical_and(t == 0, i == ib))
                    def _(lo=lo, hi=hi, ib=ib):
                        o_ref[lo - ib * bm:hi - ib * bm, :] = init_ref[lo - init_rows[0]:hi - init_rows[0], :]

        o_ref[...] += _dot_tn(a_ref[...], b_ref[...])

        @pl.when(t == ntt - 1)
        def _():
            ob_ref[...] = o_ref[...].astype(BF16)

    in_specs = [pl.BlockSpec((bt, bm), lambda i, t: (t, i)), pl.BlockSpec((bt, n), lambda i, t: (t, cj))]
    args = [a, b]
    if has_init:
        in_specs.append(pl.BlockSpec((init.shape[0], n), lambda i, t: (0, cj), pipeline_mode=pl.Buffered(1)))
        args.append(init)
    for arr in after:
        in_specs.append(pl.BlockSpec(memory_space=pl.ANY))
        args.append(arr)
    out_spec = pl.BlockSpec((bm, n), lambda i, t: (i, 0))
    return pl.pallas_call(
        body, name=name, grid=(ni, ntt),
        in_specs=in_specs,
        out_specs=[out_spec, out_spec],
        out_shape=(jax.ShapeDtypeStruct((m, n), F32), jax.ShapeDtypeStruct((m, n), BF16)),
        compiler_params=_cparams(2),
    )(*args)


def _ctx_bwd(ctx, hc, kvc, cmod6, norm1, win, rlf, rlb, dscf, dscb, dlg_in):
    lc = ctx.shape[0]

    def body(ctx_ref, hc_ref, kvc_ref, cmod_ref, n1_ref, win_ref, rlf_ref, rlb_ref, dscf_ref, dscb_ref, dlgin_ref,
             dwin_ref, small_ref, dlg_ref):
        k = kvc_ref[:, :RW]
        v = kvc_ref[:, RW:]
        t = lax.broadcasted_iota(jnp.int32, (lc, 1), 0).astype(F32)
        lgf = _log_sigmoid(rlf_ref[...])
        lgb = _log_sigmoid(rlb_ref[...])
        dks, dvs = [], []
        lane = lax.broadcasted_iota(jnp.int32, (1, 128), 1)
        row_f = jnp.zeros((1, 128), F32)
        row_b = jnp.zeros((1, 128), F32)
        for h in range(H):
            hs = slice(h * DH, (h + 1) * DH)
            wf = jnp.exp(lgf[h:h + 1, 0:1] * (lc - 1.0 - t))
            wb = jnp.exp(lgb[h:h + 1, 0:1] * t)
            kh, vhb = k[:, hs], v[:, hs].astype(BF16)
            dsf, dsb = dscf_ref[h].astype(BF16), dscb_ref[h].astype(BF16)
            dkf = wf * _dot_nt(vhb, dsf)
            dkb = wb * _dot_nt(vhb, dsb)
            dvs.append(_dot((kh * wf).astype(BF16), dsf) + _dot((kh * wb).astype(BF16), dsb))
            dks.append((dkf + dkb) * KSCALE)
            lf = jnp.sum((lc - 1.0 - t) * kh * dkf, axis=0, keepdims=True)
            lb = jnp.sum(t * kh * dkb, axis=0, keepdims=True)
            row_f = row_f + jnp.where(lane == h, jnp.sum(lf, axis=1, keepdims=True), 0.0)
            row_b = row_b + jnp.where(lane == h, jnp.sum(lb, axis=1, keepdims=True), 0.0)
        dlg_ref[...] = dlgin_ref[...]
        dlg_ref[0:1, 0:128] += row_f
        dlg_ref[1:2, 0:128] += row_b
        dkv = jnp.concatenate(dks + dvs, axis=1).astype(BF16)
        dhc = _dot(dkv, win_ref[KV_LO:KV_HI, :])
        dwin_ref[...] = _dot_tn(dkv, hc_ref[...])
        x = ctx_ref[...]
        r = lax.rsqrt(jnp.mean(x * x, axis=-1, keepdims=True) + EPS)
        xh = x * r
        small_ref[...] = jnp.zeros_like(small_ref)
        small_ref[0:1, :] = jnp.sum(dhc, axis=0, keepdims=True)
        dhxx = jnp.sum(dhc * xh, axis=0, keepdims=True)
        small_ref[1:2, :] = dhxx * n1_ref[...]
        small_ref[2:3, :] = dhxx * (1.0 + cmod_ref[1:2, :])

    return pl.pallas_call(
        body, name="ctx_bwd",
        out_shape=(jax.ShapeDtypeStruct((2 * RW, D), F32), jax.ShapeDtypeStruct((8, D), F32),
                   jax.ShapeDtypeStruct((8, D), F32)),
        compiler_params=_cparams(),
    )(ctx, hc, kvc, cmod6, norm1, win, rlf, rlb, dscf, dscb, dlg_in)


def _adam_math(w, g, m, v):
    m = ADAM_B1 * m + (1.0 - ADAM_B1) * g
    v = ADAM_B2 * v + (1.0 - ADAM_B2) * (g * g)
    m_hat = m / (1.0 - ADAM_B1 ** ADAM_STEP)
    v_hat = v / (1.0 - ADAM_B2 ** ADAM_STEP)
    delta = -ADAM_LR * (m_hat / (jnp.sqrt(v_hat) + ADAM_EPS) + ADAM_WD * w)
    return delta, m, v


def _place():
    x, y, c = lax.axis_index("x"), lax.axis_index("y"), lax.axis_index("c")
    return x, y, c, 4 * x + 2 * y + c


def _flip(x, y, c, k):
    px = 1 - x if (k >> 2) & 1 else x
    py = 1 - y if (k >> 1) & 1 else y
    pc = 1 - c if k & 1 else c
    return (px, py, pc), 4 * px + 2 * py + pc


def _gather_copy(buf_ref, send_sems, recv_sems, sem0, k, mine):
    x, y, c, me = _place()
    dev, idx = _flip(x, y, c, k)
    blk = me if mine else idx
    return pltpu.make_async_remote_copy(
        src_ref=buf_ref.at[blk], dst_ref=buf_ref.at[blk],
        send_sem=send_sems.at[sem0 + k - 1], recv_sem=recv_sems.at[sem0 + k - 1],
        device_id=dev, device_id_type=MESH)


def _entry_gather(c_row, cctx_row, wmod, bmod, shards):
    n = len(shards)
    ncol = wmod.shape[1]

    def body(*refs):
        c_ref, cctx_ref, wmod_ref, bmod_ref = refs[:4]
        in_refs = refs[4:4 + n]
        mod_ref, cs_ref = refs[4 + n:6 + n]
        out_refs = refs[6 + n:6 + 2 * n]
        cbuf, pbuf = refs[6 + 2 * n:8 + 2 * n]
        cast_refs = refs[8 + 2 * n:8 + 3 * n]
        small_send, small_recv, send_sems, recv_sems, local_sems = refs[8 + 3 * n:]
        x, y, c, me = _place()
        sib = (x, y, 1 - c)
        chips = [(1 - x, y), (x, 1 - y), (1 - x, 1 - y)]

        cbuf[me] = jnp.broadcast_to(c_ref[...], (8, D))
        small = [_gather_copy(cbuf, small_send, small_recv, 0, k, True) for k in range(1, NDEV)]
        for cp in small:
            cp.start()

        def blk(px, py, pc):
            return 4 * px + 2 * py + pc

        def copy(w, k, block, to, src=None):
            dst = out_refs[w].at[block]
            return pltpu.make_async_remote_copy(
                src_ref=dst if src is None else src, dst_ref=dst,
                send_sem=send_sems.at[w, k], recv_sem=recv_sems.at[w, k], device_id=to, device_id_type=MESH)

        first, passed, mine = [], [], []
        for w in range(n):
            cast_refs[w][...] = in_refs[w][...].astype(BF16)
            m = pltpu.make_async_copy(cast_refs[w], out_refs[w].at[me], local_sems.at[w])
            m.start()
            mine.append(m)
            cps = [copy(w, 0, me, sib, src=cast_refs[w])]
            cps += [copy(w, 1 + j, me, (*chip, c), src=cast_refs[w]) for j, chip in enumerate(chips)]
            for cp in cps:
                cp.start()
            first += cps

        for k in range(1, NDEV):
            _gather_copy(cbuf, small_send, small_recv, 0, k, False).wait_recv()
        row = lax.broadcasted_iota(jnp.int32, (16, D), 0)
        call = jnp.where(row == 8, jnp.broadcast_to(cctx_ref[...], (16, D)), 0.0)
        for d in range(NDEV):
            call = jnp.where(row == d, jnp.concatenate([cbuf[d], cbuf[d]], axis=0), call)
        cs = call * _sigmoid(call)
        cs_ref[...] = cs
        pbuf[me] = _dot(cs.astype(BF16), wmod_ref[...].astype(BF16))
        small2 = [_gather_copy(pbuf, small_send, small_recv, NDEV - 1, k, True) for k in range(1, NDEV)]
        for cp in small2:
            cp.start()

        for w in range(n):
            for j, chip in enumerate(chips):
                b = blk(*chip, c)
                copy(w, 1 + j, b, (x, y, c)).wait_recv()
                fw = copy(w, 4 + j, b, sib)
                fw.start()
                passed.append(fw)
        for w in range(n):
            copy(w, 0, blk(x, y, 1 - c), (x, y, c)).wait_recv()
            for j, chip in enumerate(chips):
                copy(w, 4 + j, blk(*chip, 1 - c), (x, y, c)).wait_recv()

        for k in range(1, NDEV):
            _gather_copy(pbuf, small_send, small_recv, NDEV - 1, k, False).wait_recv()
        for d in range(NDEV):
            mod_ref[:, d * ncol:(d + 1) * ncol] = pbuf[d] + bmod_ref[:, d * ncol:(d + 1) * ncol]
        for cp in small + small2 + first + passed:
            cp.wait_send()
        for m in mine:
            m.wait()

    vm = pl.BlockSpec(memory_space=pltpu.VMEM)
    hbm = pl.BlockSpec(memory_space=pltpu.HBM)
    return pl.pallas_call(
        body, name="entry_gather",
        in_specs=[vm] * (4 + n), out_specs=[vm, vm] + [hbm] * n,
        out_shape=(jax.ShapeDtypeStruct((16, 6 * D), F32), jax.ShapeDtypeStruct((16, D), F32))
        + tuple(jax.ShapeDtypeStruct((NDEV,) + s.shape, BF16) for s in shards),
        scratch_shapes=[pltpu.VMEM((NDEV, 8, D), F32), pltpu.VMEM((NDEV, 16, ncol), F32)]
        + [pltpu.VMEM(s.shape, BF16) for s in shards]
        + [pltpu.SemaphoreType.DMA((2 * (NDEV - 1),)), pltpu.SemaphoreType.DMA((2 * (NDEV - 1),)),
           pltpu.SemaphoreType.DMA((n, 7)), pltpu.SemaphoreType.DMA((n, 7)), pltpu.SemaphoreType.DMA((n,))],
        compiler_params=_cparams(),
    )(c_row, cctx_row, wmod, bmod, *shards)


_HBM = pl.BlockSpec(memory_space=pltpu.HBM)
_SEMS = pl.BlockSpec(memory_space=pltpu.SEMAPHORE)
_EFFECT = pltpu.SideEffectType.DATAFLOW_SIDE_EFFECTING


def _exchange_copy(src_refs, land_refs, send_sems, recv_sems, w, k, scatter, landing_slot_is_mine):
    x, y, c, me = _place()
    dev, pidx = _flip(x, y, c, k)
    src = src_refs[w].at[pidx] if scatter else src_refs[w]
    slot = me if landing_slot_is_mine else pidx
    return pltpu.make_async_remote_copy(
        src_ref=src, dst_ref=land_refs[w].at[slot],
        send_sem=send_sems.at[w * (NDEV - 1) + k - 1], recv_sem=recv_sems.at[w * (NDEV - 1) + k - 1],
        device_id=dev, device_id_type=MESH)


def _exchange_start(srcs, *, scatter, name):
    n = len(srcs)
    lands = [lax.empty((NDEV,) + tuple(s.shape[-2:]), s.dtype) for s in srcs]

    def body(*refs):
        src_refs, land_refs = refs[:n], refs[n:2 * n]
        send_sems, recv_sems = refs[2 * n], refs[2 * n + 1]
        token = refs[-1]
        for w in range(n):
            for k in range(1, NDEV):
                _exchange_copy(src_refs, land_refs, send_sems, recv_sems, w, k, scatter, True).start()
        token[...] = jnp.zeros_like(token)

    arrs = list(srcs) + lands
    out_shape = ([pltpu.SemaphoreType.DMA((n * (NDEV - 1),)), pltpu.SemaphoreType.DMA((n * (NDEV - 1),))]
                 + [pltpu.HBM(a.shape, a.dtype) for a in arrs] + [jax.ShapeDtypeStruct((8, 128), F32)])
    res = pl.pallas_call(
        body, name=name, out_shape=tuple(out_shape),
        in_specs=[_HBM] * (2 * n),
        out_specs=tuple([_SEMS, _SEMS] + [_HBM] * (2 * n) + [pl.BlockSpec(memory_space=pltpu.VMEM)]),
        input_output_aliases={i: 2 + i for i in range(2 * n)},
        compiler_params=pltpu.CompilerParams(has_side_effects=_EFFECT),
    )(*[pltpu.with_memory_space_constraint(a, pltpu.HBM) for a in arrs])
    return res[:-1], res[-1]


def _exchange_wait(handles, after, *, scatter, name):
    n = (len(handles) - 2) // 2
    na = len(after)

    def body(*refs):
        src_refs, land_refs = refs[:n], refs[n:2 * n]
        send_sems, recv_sems = refs[2 * n], refs[2 * n + 1]
        for w in range(n):
            for k in range(1, NDEV):
                cp = _exchange_copy(src_refs, land_refs, send_sems, recv_sems, w, k, scatter, False)
                cp.wait_send()
                cp.wait_recv()

    arrs = handles[2:]
    res = pl.pallas_call(
        body, name=name, out_shape=tuple(pltpu.HBM(a.shape, a.dtype) for a in arrs),
        in_specs=[_HBM] * (2 * n) + [_SEMS, _SEMS] + [_HBM] * na,
        out_specs=tuple([_HBM] * (2 * n)),
        input_output_aliases={i: i for i in range(2 * n)},
        compiler_params=pltpu.CompilerParams(has_side_effects=_EFFECT),
    )(*arrs, handles[0], handles[1], *[pltpu.with_memory_space_constraint(a, pltpu.HBM) for a in after])
    return res[:n], res[n:]


_SAME_CORE = (2, 4, 6)


def _gather2_copy(src_ref, land_ref, send_sems, recv_sems, sem, k, block):
    x, y, c, _ = _place()
    dev, _ = _flip(x, y, c, k)
    dst = land_ref.at[block]
    return pltpu.make_async_remote_copy(
        src_ref=dst if src_ref is None else src_ref, dst_ref=dst,
        send_sem=send_sems.at[sem], recv_sem=recv_sems.at[sem], device_id=dev, device_id_type=MESH)


def _split_call(body, name, arrs, n_sems, sems=None, after=(), token=False):
    na = len(arrs)
    out_shape, out_specs = [], []
    if n_sems is not None:
        out_shape += [pltpu.SemaphoreType.DMA((n_sems,)), pltpu.SemaphoreType.DMA((n_sems,))]
        out_specs += [_SEMS, _SEMS]
    first = len(out_shape)
    out_shape += [pltpu.HBM(a.shape, a.dtype) for a in arrs]
    out_specs += [_HBM] * na
    if token:
        out_shape.append(jax.ShapeDtypeStruct((8, 128), F32))
        out_specs.append(pl.BlockSpec(memory_space=pltpu.VMEM))
    in_specs = [_HBM] * na + ([_SEMS, _SEMS] if sems is not None else []) + [_HBM] * len(after)
    args = [pltpu.with_memory_space_constraint(a, pltpu.HBM) for a in arrs] + list(sems or ()) \
        + [pltpu.with_memory_space_constraint(a, pltpu.HBM) for a in after]
    return pl.pallas_call(
        body, name=name, out_shape=tuple(out_shape), in_specs=in_specs, out_specs=tuple(out_specs),
        input_output_aliases={i: first + i for i in range(na)},
        compiler_params=pltpu.CompilerParams(has_side_effects=_EFFECT))(*args)


def _gather2_start(srcs, *, name):
    n = len(srcs)
    lands = [lax.empty((NDEV,) + tuple(s.shape), s.dtype) for s in srcs]

    def body(*refs):
        src_refs, land_refs = refs[:n], refs[n:2 * n]
        send_sems, recv_sems = refs[2 * n], refs[2 * n + 1]
        _, _, _, me = _place()
        for w in range(n):
            for slot, k in enumerate((1,) + _SAME_CORE):
                _gather2_copy(src_refs[w], land_refs[w], send_sems, recv_sems, 4 * w + slot, k, me).start()
        refs[-1][...] = jnp.zeros_like(refs[-1])

    res = _split_call(body, name, list(srcs) + lands, 4 * n, token=True)
    return res[:2], res[2:-1], res[-1]


def _gather2_arrived(sems, arrs, after, *, name):
    n = len(arrs) // 2

    def body(*refs):
        src_refs, land_refs = refs[:n], refs[n:2 * n]
        send_sems, recv_sems = refs[2 * n], refs[2 * n + 1]
        x, y, c, me = _place()
        for w in range(n):
            for slot, k in enumerate((1,) + _SAME_CORE):
                _, pidx = _flip(x, y, c, k)
                _gather2_copy(src_refs[w], land_refs[w], send_sems, recv_sems, 4 * w + slot, k, me).wait_send()
                _gather2_copy(None, land_refs[w], send_sems, recv_sems, 4 * w + slot, k, pidx).wait_recv()

    return _split_call(body, name, arrs, None, sems=sems, after=after)


def _gather2_forward(lands, *, name):
    n = len(lands)

    def body(*refs):
        land_refs = refs[:n]
        send_sems, recv_sems = refs[n], refs[n + 1]
        x, y, c, _ = _place()
        for w in range(n):
            for j, k in enumerate(_SAME_CORE):
                _, pidx = _flip(x, y, c, k)
                _gather2_copy(None, land_refs[w], send_sems, recv_sems, 3 * w + j, 1, pidx).start()
        refs[-1][...] = jnp.zeros_like(refs[-1])

    res = _split_call(body, name, list(lands), 3 * n, token=True)
    return res[:2], res[2:-1], res[-1]


def _gather2_wait(sems, lands, after, *, name):
    n = len(lands)

    def body(*refs):
        land_refs = refs[:n]
        send_sems, recv_sems = refs[n], refs[n + 1]
        x, y, c, _ = _place()
        for w in range(n):
            for j, k in enumerate(_SAME_CORE):
                _, mine = _flip(x, y, c, k)
                _, theirs = _flip(x, y, c, k ^ 1)
                _gather2_copy(None, land_refs[w], send_sems, recv_sems, 3 * w + j, 1, mine).wait_send()
                _gather2_copy(None, land_refs[w], send_sems, recv_sems, 3 * w + j, 1, theirs).wait_recv()

    return _split_call(body, name, list(lands), None, sems=sems, after=after)


def _cast_bf16(arrs, *, name, after=()):
    n = len(arrs)

    def body(*refs):
        for i_ref, o_ref in zip(refs[:n], refs[n + len(after):]):
            o_ref[...] = i_ref[...].astype(BF16)

    return pl.pallas_call(
        body, name=name, out_shape=tuple(jax.ShapeDtypeStruct(a.shape, BF16) for a in arrs),
        in_specs=[pl.BlockSpec(memory_space=pltpu.VMEM)] * n + [pl.BlockSpec(memory_space=pl.ANY)] * len(after),
        compiler_params=_cparams())(*arrs, *after)


def _rs_adam(me_arr, fulls, lands, w, m, v, *, name):
    rows = lands[0].shape[1]
    k = len(fulls)
    nb = next(n for n in (4, 2, 1) if rows % (16 * n) == 0)
    br = rows // nb

    def body(me_ref, *refs):
        own_refs, land_refs = refs[:k], refs[k:2 * k]
        w_ref, m_ref, v_ref, g_ref, d_ref, mo_ref, vo_ref = refs[2 * k:]
        me = me_ref[0]
        parts = []
        for own_ref, land_ref in zip(own_refs, land_refs):
            acc = jnp.zeros(own_ref.shape, F32)
            for p in range(NDEV):
                acc = acc + jnp.where(p == me, own_ref[...], land_ref[p].astype(F32))
            parts.append(acc)
        acc = parts[0] if k == 1 else jnp.concatenate(parts, axis=1)
        g_ref[...] = acc
        d_ref[...], mo_ref[...], vo_ref[...] = _adam_math(w_ref[...], acc, m_ref[...], v_ref[...])

    sh = jax.ShapeDtypeStruct((rows, D), F32)
    blk = pl.BlockSpec((br, D), lambda i, me: (i, 0))
    grid_spec = pltpu.PrefetchScalarGridSpec(
        num_scalar_prefetch=1, grid=(nb,),
        in_specs=[pl.BlockSpec((br, f.shape[1]), lambda i, me: (me[0] * nb + i, 0)) for f in fulls]
        + [pl.BlockSpec((NDEV, br, l.shape[2]), lambda i, me: (0, i, 0)) for l in lands]
        + [blk, blk, blk],
        out_specs=[blk] * 4)
    return pl.pallas_call(
        body, name=name, out_shape=(sh, sh, sh, sh), grid_spec=grid_spec, compiler_params=_cparams(1),
    )(me_arr, *fulls, *lands, w, m, v)


ROW_F, ROW_I, ROW_C, ROW_G1, ROW_LG = 0, 8, 16, 24, 32
PACK_ROWS = 40


def _small_sum(me_arr, pack, pack_land, sgw, sgw_land, wmod):
    ncol = wmod.shape[1]

    def body(me_ref, pack_ref, pland_ref, sgw_ref, sland_ref, wmod_ref, sum_ref, all_ref, sgw_sum_ref, part_ref):
        me = me_ref[0]
        acc = jnp.zeros((PACK_ROWS, D), F32)
        acc_w = jnp.zeros(sgw_ref.shape, F32)
        for p in range(NDEV):
            rows = jnp.where(p == me, pack_ref[...], pland_ref[p])
            all_ref[p] = rows
            acc = acc + rows
            acc_w = acc_w + jnp.where(p == me, sgw_ref[...], sland_ref[p])
        sum_ref[...] = acc
        sgw_sum_ref[...] = acc_w
        zero = jnp.zeros((1, D), F32)
        dcmod = jnp.concatenate([acc[ROW_C:ROW_C + 1], acc[ROW_C + 1:ROW_C + 2], zero, zero, zero, zero], axis=1)
        mine = jnp.zeros((1, ncol), F32)
        for d in range(NDEV):
            mine = mine + jnp.where(d == me, dcmod[:, d * ncol:(d + 1) * ncol], 0.0)
        part_ref[...] = _dot_nt(jnp.broadcast_to(mine, (8, ncol)).astype(BF16), wmod_ref[...].astype(BF16))

    vm = pl.BlockSpec(memory_space=pltpu.VMEM)
    return pl.pallas_call(
        body, name="small_sum",
        out_shape=(jax.ShapeDtypeStruct((PACK_ROWS, D), F32), jax.ShapeDtypeStruct((NDEV, PACK_ROWS, D), F32),
                   jax.ShapeDtypeStruct(sgw.shape, F32), jax.ShapeDtypeStruct((8, D), F32)),
        in_specs=[pl.BlockSpec(memory_space=pltpu.SMEM)] + [vm] * 5,
        compiler_params=_cparams(),
    )(me_arr, pack, pack_land, sgw, sgw_land, wmod)


def _cctx_final(me_arr, part, part_land, cctx_row, m_row, v_row):
    def body(me_ref, part_ref, land_ref, c_ref, m_ref, v_ref, g_ref, d_ref, mo_ref, vo_ref):
        me = me_ref[0]
        tot = jnp.zeros((8, D), F32)
        for p in range(NDEV):
            tot = tot + jnp.where(p == me, part_ref[...], land_ref[p])
        cc = c_ref[...]
        _, dsilu = _silu_parts(cc)
        g = tot[0:1, :] * dsilu
        g_ref[...] = g
        d_ref[...], mo_ref[...], vo_ref[...] = _adam_math(cc, g, m_ref[...], v_ref[...])

    row = jax.ShapeDtypeStruct((1, D), F32)
    vm = pl.BlockSpec(memory_space=pltpu.VMEM)
    return pl.pallas_call(
        body, name="cctx_final", out_shape=(row, row, row, row),
        in_specs=[pl.BlockSpec(memory_space=pltpu.SMEM)] + [vm] * 5,
        compiler_params=_cparams(),
    )(me_arr, part, part_land, cctx_row, m_row, v_row)


def _adam_small(s, sgw_g, params):
    names = ["norm1", "norm2", "norm_f", "sg_gain", "sg_b", "ret_logit_f", "ret_logit_b", "b_mod", "sg_w"]
    flat = [a for n in names for a in params[n]]

    def body(*refs):
        s_ref, sgw_ref = refs[0], refs[1]
        p_refs = refs[2:2 + 3 * len(names)]
        o_refs = refs[2 + 3 * len(names):]
        loss_ref = o_refs[-1]

        def row(r):
            return s_ref[r:r + 1, :]

        grads = {
            "norm1": row(ROW_I + 2) + row(ROW_C + 2),
            "norm2": row(ROW_F + 4),
            "norm_f": row(ROW_F + 0),
            "sg_gain": s_ref[ROW_I + 3:ROW_I + 4, 0:AW],
            "sg_b": s_ref[ROW_I + 4:ROW_I + 8, 0:DH],
            "sg_w": sgw_ref[...],
        }
        for j, n in enumerate(names):
            w_ref, m_ref, v_ref = p_refs[3 * j:3 * j + 3]
            g_ref, d_ref, mo_ref, vo_ref = o_refs[4 * j:4 * j + 4]
            w = w_ref[...]
            if n in ("ret_logit_f", "ret_logit_b"):
                r = ROW_LG + (0 if n == "ret_logit_f" else 1)
                g = s_ref[r:r + 1, 0:H] * _sigmoid(-w)
            elif n == "b_mod":
                rows = [row(ROW_I + 0) + row(ROW_C + 0), row(ROW_I + 1) + row(ROW_C + 1), row(ROW_G1),
                        row(ROW_F + 2), row(ROW_F + 3), row(ROW_F + 1)]
                g = jnp.concatenate(rows, axis=1)
            else:
                g = grads[n]
            g_ref[...] = g
            d_ref[...], mo_ref[...], vo_ref[...] = _adam_math(w, g, m_ref[...], v_ref[...])
        loss_ref[...] = jnp.full((1, 1), 0.5 / D, F32) * jnp.sum(row(ROW_F + 5), axis=1, keepdims=True)

    out_shape = []
    for n in names:
        w = params[n][0]
        out_shape += [jax.ShapeDtypeStruct(w.shape, F32)] * 4
    out_shape.append(jax.ShapeDtypeStruct((1, 1), F32))
    outs = pl.pallas_call(body, name="adam_small", out_shape=tuple(out_shape), compiler_params=_cparams())(
        s, sgw_g, *flat)
    return {n: tuple(outs[4 * j:4 * j + 4]) for j, n in enumerate(names)}, outs[-1]


def _wmod_grad(cs_all, dmod_cols, wmod, m, v):
    ncol = wmod.shape[1]

    def body(cs_ref, dm_ref, w_ref, m_ref, v_ref, g_ref, d_ref, mo_ref, vo_ref):
        g = _dot_tn(cs_ref[...].astype(BF16), dm_ref[...].astype(BF16))
        g_ref[...] = g
        d_ref[...], mo_ref[...], vo_ref[...] = _adam_math(w_ref[...], g, m_ref[...], v_ref[...])

    sh = jax.ShapeDtypeStruct((D, ncol), F32)
    br = D // 4
    blk = pl.BlockSpec((br, ncol), lambda i: (i, 0))
    return pl.pallas_call(
        body, name="wmod_grad", out_shape=(sh, sh, sh, sh), grid=(D // br,),
        in_specs=[pl.BlockSpec((cs_all.shape[0], br), lambda i: (0, i)), _whole(dmod_cols.shape), blk, blk, blk],
        out_specs=[blk] * 4,
        compiler_params=_cparams(1),
    )(cs_all, dmod_cols, wmod, m, v)


def _rope_tables(t_len):
    n_freq = DH // 4
    inv = (ROPE_BASE ** (-np.arange(n_freq, dtype=np.float32) / n_freq)).astype(np.float32)
    tok = np.arange(t_len)
    rows = (tok // GRID_W).astype(np.float32)
    cols = (tok % GRID_W).astype(np.float32)
    ang_r = rows[:, None] * inv[None, :]
    ang_c = cols[:, None] * inv[None, :]
    cos = np.concatenate([np.cos(ang_r)] * 2 + [np.cos(ang_c)] * 2, axis=1).astype(np.float32)
    sin = np.concatenate([-np.sin(ang_r), np.sin(ang_r), -np.sin(ang_c), np.sin(ang_c)], axis=1).astype(np.float32)
    return jnp.asarray(cos), jnp.asarray(sin)


def _local_step(x, tgt, ctx, mod6, cmod6, norm1, norm2, normf, win, wout, ffn_weights, sgw, sgb, sggain, rlf, rlb,
                *, tm_a, tm_b, tm_f, tm_m, tm_r, tm_i, bt_w, after_first_grads=None, small_path=None,
                after_in_half=None):
    t_len = x.shape[0]
    cos, sin = _rope_tables(t_len)
    sgbt = jnp.broadcast_to(sgb[:, :, None], (G, C, 128))
    rlf = jnp.broadcast_to(rlf.reshape(H, 1), (H, 128))
    rlb = jnp.broadcast_to(rlb.reshape(H, 1), (H, 128))
    hc, kvc, scf, scb = _ctx_fwd(ctx, cmod6, norm1, win, rlf, rlb)
    hx, zuv, q, k, v, gfb, of, ya, sst = _fwd_a(x, mod6, norm1, win, sgw, sgbt, sggain, cos, sin, rlf, scf, tm=tm_a)
    if callable(wout):
        wout, tok = wout(hx)
        rlb = rlb + tok
    ob, ycat, y, x1, rst = _fwd_b(q, k, v, of, gfb, ya, x, mod6, wout, rlb, scb, tm=tm_b)
    wg, wu, wd = ffn_weights(x1) if callable(ffn_weights) else ffn_weights
    dx1, h2, da, db, hm, df, small_f = _ffn(x1, tgt, mod6, norm2, normf, wg, wu, wd, tm=tm_f)
    bt2 = min(2 * bt_w, t_len)
    d_wd, d_wd_b = _tn_matmul(hm, df, bm=DFF // 2, bt=bt2, name="dw_down")
    d_wg, d_wg_b = _tn_matmul(da, h2, bm=DFF // 2, bt=bt2, name="dw_gate")
    d_wu, d_wu_b = _tn_matmul(db, h2, bm=DFF // 2, bt=bt2, name="dw_up")
    dy, dya, dgfb, dof, dob, dg1 = _mid_bwd(dx1, y, of, ob, gfb, mod6, wout, tm=tm_m)
    d_wo, d_wo_b = _tn_matmul(ycat, dy, bm=D, bt=bt2, name="dw_out")
    if after_first_grads is not None:
        rlf = rlf + after_first_grads((d_wd_b, d_wg_b, d_wu_b, d_wo_b))
    dqf, dkf, dvf, dqb, dkb, dvb, dscf, dscb, dlg = _ret_bwd(q, k, v, dof, dob, sst, rst, rlf, rlb, tm=tm_r)
    gx, dz, small_i, dsgw = _in_bwd(x, zuv, dya, dqf, dkf, dvf, dqb, dkb, dvb, dgfb, dx1, cos, sin,
                                    mod6, norm1, win, sgw, sgbt, sggain, tm=tm_i)
    dwin_c, small_c, dlg = _ctx_bwd(ctx, hc, kvc, cmod6, norm1, win, rlf, rlb, dscf, dscb, dlg)
    pack = jnp.concatenate([small_f, small_i, small_c, dg1, dlg], axis=0)
    after = small_path(pack, dsgw) if small_path is not None else ()
    halves = []
    for j in range(2):
        halves.append(_tn_matmul(dz, hx, bm=INC // 2, bt=bt2, name="dw_in_%d" % j, init=dwin_c,
                                 init_rows=(KV_LO, KV_HI), after=after, cols=(j, D // 2)))
        if after_in_half is not None:
            after = after_in_half(j, halves[-1][1])
    grads = {"w_in": halves, "w_out": (d_wo, d_wo_b), "w_gate": (d_wg, d_wg_b), "w_up": (d_wu, d_wu_b),
             "w_down": (d_wd, d_wd_b)}
    return gx, grads, pack, dsgw


def kernel(x, c, ctx, c_ctx, w_mod, b_mod, norm1, w_in, sg_gain, sg_w, sg_b, ret_logit_f, ret_logit_b, w_out, norm2, w_gate, w_up, w_down, norm_f, loss_target, m_c_ctx, m_w_mod, m_b_mod, m_norm1, m_w_in, m_sg_gain, m_sg_w, m_sg_b, m_ret_logit_f, m_ret_logit_b, m_w_out, m_norm2, m_w_gate, m_w_up, m_w_down, m_norm_f, v_c_ctx, v_w_mod, v_b_mod, v_norm1, v_w_in, v_sg_gain, v_sg_w, v_sg_b, v_ret_logit_f, v_ret_logit_b, v_w_out, v_norm2, v_w_gate, v_w_up, v_w_down, v_norm_f):
    t_len = x.shape[1]
    tm = min(512, t_len)
    me = 4 * lax.axis_index("x") + 2 * lax.axis_index("y") + lax.axis_index("c")
    tr = lambda a: jnp.transpose(a[0])

    cctx_row = c_ctx.reshape(1, D)
    mod_all, cs_all, g_in = _entry_gather(c, cctx_row, w_mod[0], b_mod, [tr(w_in)])
    mod6 = lax.dynamic_slice(mod_all, (me, 0), (1, 6 * D)).reshape(6, D)
    cmod6 = mod_all[8].reshape(6, D)
    win_t = g_in.reshape(INC, D)

    (out_shard,) = _cast_bf16([w_out[0]], name="cast_out", after=[g_in])
    h_out, tok_out = _exchange_start([out_shard], scatter=False, name="wout_start")
    ffn_shards = _cast_bf16([tr(w_gate), tr(w_up), w_down[0]], name="cast_ffn", after=[h_out[2]])
    sems_ffn, arrs_ffn, tok = _gather2_start(ffn_shards, name="wffn_start")
    mod6 = mod6 + (tok_out[0, 0] + tok[0, 0])
    ffn = {}

    def place_own(own, lands, rows):
        return [lax.dynamic_update_slice(l, o[None], (me, 0, 0)).reshape(rows, D) for o, l in zip(own, lands)]

    def wout(after):
        own, lands = _exchange_wait(h_out, [after], scatter=False, name="wout_wait")
        arrs = _gather2_arrived(sems_ffn, arrs_ffn, [after], name="wffn_arrived")
        ffn["own"] = arrs[:3]
        ffn["sems"], ffn["lands"], tok_fwd = _gather2_forward(arrs[3:], name="wffn_forward")
        return place_own(own, lands, D)[0], tok_fwd[0, 0]

    def ffn_weights(after):
        lands = _gather2_wait(ffn["sems"], ffn["lands"], [after], name="wffn_wait")
        return place_own(ffn["own"], lands, DFF)

    rs, outs = {}, {}

    def after_first_grads(bf):
        rs["first"], tok_first = _exchange_start([b.reshape(NDEV, b.shape[0] // NDEV, D) for b in bf], scatter=True,
                                                 name="rs_first_start")
        return tok_first[0, 0]

    def small_path(pack, dsgw):
        rs["small"], _ = _exchange_start([pack, dsgw.reshape(G * C, C)], scatter=False, name="small_start")
        return [rs["small"][2], rs["small"][3]]

    def after_in_half(j, half_bf16):
        rs["in%d" % j], _ = _exchange_start([half_bf16.reshape(NDEV, INC // NDEV, D // 2)], scatter=True,
                                            name="rs_in%d_start" % j)
        return [rs["in%d" % j][2]]

    gx, grads, pack, dsgw = _local_step(
        x[0], loss_target[0], ctx[0], mod6, cmod6, norm1, norm2[0:1], norm_f.reshape(1, D), win_t, wout, ffn_weights,
        sg_w[0], sg_b[0], sg_gain, ret_logit_f, ret_logit_b,
        tm_a=tm, tm_b=tm, tm_f=min(256, t_len), tm_m=min(1024, t_len), tm_r=min(1024, t_len), tm_i=tm,
        bt_w=min(1024, t_len),
        after_first_grads=after_first_grads, small_path=small_path, after_in_half=after_in_half)

    me_arr = me.reshape(1).astype(jnp.int32)
    (pack_own, sgw_own), (pack_land, sgw_land) = _exchange_wait(rs["small"], [rs["in1"][2]], scatter=False,
                                                               name="small_wait")
    psum_rows, pall, sgw_sum, part = _small_sum(me_arr, pack_own, pack_land, sgw_own, sgw_land, w_mod[0])
    h_cc, _ = _exchange_start([part], scatter=False, name="cctx_start")

    dmod_rows = (ROW_I + 0, ROW_I + 1, ROW_G1, ROW_F + 2, ROW_F + 3, ROW_F + 1)
    dmod_all = jnp.concatenate([pall[:, r, :] for r in dmod_rows], axis=1)
    dcmod = jnp.concatenate([psum_rows[ROW_C + 0:ROW_C + 1], psum_rows[ROW_C + 1:ROW_C + 2],
                             jnp.zeros((1, 4 * D), F32)], axis=1)
    dmod_mat = jnp.concatenate([dmod_all, jnp.pad(dcmod, ((0, 7), (0, 0)))], axis=0)
    ncol = 6 * D // NDEV
    dmod_cols = lax.dynamic_slice(dmod_mat, (0, me * ncol), (16, ncol))
    g_wm, d_wm, m_wm, v_wm = _wmod_grad(cs_all, dmod_cols, w_mod[0], m_w_mod[0], v_w_mod[0])
    outs["w_mod"] = (g_wm[None], d_wm[None], m_wm[None], v_wm[None])
    small_params = {
        "norm1": (norm1, m_norm1, v_norm1), "norm2": (norm2, m_norm2, v_norm2),
        "norm_f": tuple(a.reshape(1, D) for a in (norm_f, m_norm_f, v_norm_f)),
        "sg_gain": (sg_gain, m_sg_gain, v_sg_gain), "sg_b": (sg_b[0], m_sg_b[0], v_sg_b[0]),
        "ret_logit_f": (ret_logit_f, m_ret_logit_f, v_ret_logit_f),
        "ret_logit_b": (ret_logit_b, m_ret_logit_b, v_ret_logit_b),
        "b_mod": (b_mod, m_b_mod, v_b_mod), "sg_w": (sg_w[0], m_sg_w[0], v_sg_w[0])}
    small, loss = _adam_small(psum_rows, sgw_sum.reshape(G, C, C), small_params)
    back = {"norm_f": lambda a: a.reshape(D), "sg_b": lambda a: a[None], "sg_w": lambda a: a[None]}
    for name, vals in small.items():
        outs[name] = tuple(back.get(name, lambda a: a)(a) for a in vals)

    _, lands_first = _exchange_wait(rs["first"], [g_wm], scatter=True, name="rs_first_wait")

    def finish(name, fulls, lands, w, m, v, transposed):
        take = tr if transposed else (lambda a: a[0])
        res = _rs_adam(me_arr, fulls, lands, take(w), take(m), take(v), name="adam_" + name)
        put = (lambda a: jnp.transpose(a)[None]) if transposed else (lambda a: a[None])
        outs[name] = tuple(put(a) for a in res)
        return res[0]

    g_down = finish("w_down", [grads["w_down"][0]], [lands_first[0]], w_down, m_w_down, v_w_down, False)
    g_gate = finish("w_gate", [grads["w_gate"][0]], [lands_first[1]], w_gate, m_w_gate, v_w_gate, True)
    g_up = finish("w_up", [grads["w_up"][0]], [lands_first[2]], w_up, m_w_up, v_w_up, True)
    g_out = finish("w_out", [grads["w_out"][0]], [lands_first[3]], w_out, m_w_out, v_w_out, False)
    done = [g_down, g_gate, g_up, g_out]
    (part_own,), (part_land,) = _exchange_wait(h_cc, done, scatter=False, name="cctx_wait")
    res_cc = _cctx_final(me_arr, part_own, part_land, cctx_row, m_c_ctx.reshape(1, D), v_c_ctx.reshape(1, D))
    outs["c_ctx"] = tuple(a.reshape(D) for a in res_cc)
    lands_in = [_exchange_wait(rs["in%d" % j], done, scatter=True, name="rs_in%d_wait" % j)[1][0] for j in range(2)]
    finish("w_in", [h[0] for h in grads["w_in"]], lands_in, w_in, m_w_in, v_w_in, True)

    order = ["c_ctx", "w_mod", "b_mod", "norm1", "w_in", "sg_gain", "sg_w", "sg_b", "ret_logit_f", "ret_logit_b",
             "w_out", "norm2", "w_gate", "w_up", "w_down", "norm_f"]
    res = [loss.reshape(()), gx[None]]
    for j in range(4):
        res += [outs[name][j] for name in order]
    return tuple(res)
```

```python
import functools
import math

import numpy as np
import jax
import jax.numpy as jnp
from jax import lax
from jax.experimental import pallas as pl
from jax.experimental.pallas import tpu as pltpu

F32 = jnp.float32
BF16 = jnp.bfloat16

D = 1024
AW = 512
RW = 512
H = 4
DH = 128
G = 4
C = 128
CR = 256
DFF = 2816
INC = 3584
Q_LO = 2 * AW
KV_LO, VR_LO, G_LO = Q_LO + RW, Q_LO + 2 * RW, Q_LO + 3 * RW
KV_HI = G_LO
NDEV = 8
EPS = 1e-6
KSCALE = DH ** -0.5
ROPE_BASE = 10000.0
GRID_W = 64

ADAM_LR = 0.001
ADAM_B1 = 0.9
ADAM_B2 = 0.999
ADAM_EPS = 1e-08
ADAM_WD = 0.01
ADAM_STEP = 10

VMEM_LIMIT = 56 * 1024 * 1024
MESH = pl.DeviceIdType.MESH


def _cparams(n_grid=0, **kw):
    sem = ("arbitrary",) * n_grid if n_grid else None
    return pltpu.CompilerParams(dimension_semantics=sem, vmem_limit_bytes=VMEM_LIMIT, **kw)


def _dot(a, b):
    return jnp.dot(a, b, preferred_element_type=F32)


def _dot_nt(a, b):
    return lax.dot_general(a, b, (((1,), (1,)), ((), ())), preferred_element_type=F32)


def _dot_tn(a, b):
    return lax.dot_general(a, b, (((0,), (0,)), ((), ())), preferred_element_type=F32)


def _whole(shape):
    nd = len(shape)
    return pl.BlockSpec(shape, lambda *_: (0,) * nd, pipeline_mode=pl.Buffered(1))


def _acc(shape):
    nd = len(shape)
    return pl.BlockSpec(shape, lambda *_: (0,) * nd)


def _rows(tm, n):
    return pl.BlockSpec((tm, n), lambda i: (i, 0))


def _rows_rev(tm, n, nt):
    return pl.BlockSpec((tm, n), lambda i: (nt - 1 - i, 0))


def _sigmoid(x):
    return 1.0 / (1.0 + jnp.exp(-x))


def _log_sigmoid(x):
    return jnp.minimum(x, 0.0) - jnp.log(1.0 + jnp.exp(-jnp.abs(x)))


_GK0 = math.sqrt(2.0 / math.pi)
_GK1 = 0.044715


def _gelu(x):
    x2 = x * x
    t = jnp.tanh(x * (_GK0 + (_GK0 * _GK1) * x2))
    h = 0.5 + 0.5 * t
    g = x * h
    dg = h + g * (1.0 - h) * ((2.0 * _GK0) + (6.0 * _GK0 * _GK1) * x2)
    return g, dg


def _silu_parts(a):
    s = _sigmoid(a)
    sa = a * s
    return sa, s + sa * (1.0 - s)


def _rope(t, cos, sins):
    n = t.shape[1]
    lane = lax.broadcasted_iota(jnp.int32, t.shape, 1)
    first = (lane & 63) < 32
    partner = jnp.where(first, pltpu.roll(t, n - 32, 1), pltpu.roll(t, 32, 1))
    return t * cos + partner * sins


def _headnorm(o):
    outs, rs = [], []
    for h in range(H):
        oh = o[:, h * DH:(h + 1) * DH]
        r = lax.rsqrt(jnp.mean(oh * oh, axis=-1, keepdims=True) + EPS)
        outs.append(oh * r)
        rs.append(r)
    return jnp.concatenate(outs, axis=1), rs


def _decay_mask(lg, forward):
    ii = lax.broadcasted_iota(jnp.int32, (CR, CR), 0).astype(F32)
    jj = lax.broadcasted_iota(jnp.int32, (CR, CR), 1).astype(F32)
    diff = ii - jj if forward else jj - ii
    return jnp.where(diff >= 0.0, jnp.exp(lg * jnp.maximum(diff, 0.0)), 0.0)


def _decay_consts(lg, forward):
    ic = lax.broadcasted_iota(jnp.int32, (CR, 1), 0).astype(F32)
    if forward:
        xi = jnp.exp(lg * (ic + 1.0))
        z = jnp.exp(lg * (CR - 1.0 - ic))
    else:
        xi = jnp.exp(lg * (CR - ic))
        z = jnp.exp(lg * ic)
    dec = jnp.exp(lg * float(CR))
    return xi, z, dec, ic


def _decay_masks(rlf, rlb):
    def body(rlf_ref, rlb_ref, mf_ref, mb_ref):
        lgf = _log_sigmoid(rlf_ref[...])
        lgb = _log_sigmoid(rlb_ref[...])
        for h in range(H):
            mf_ref[h] = _decay_mask(lgf[h:h + 1, 0:1], True)
            mb_ref[h] = _decay_mask(lgb[h:h + 1, 0:1], False)

    sh = jax.ShapeDtypeStruct((H, CR, CR), F32)
    return pl.pallas_call(body, name="decay_masks", out_shape=(sh, sh), compiler_params=_cparams())(rlf, rlb)


def _ctx_fwd(ctx, cmod6, norm1, win, rlf, rlb):
    lc = ctx.shape[0]

    def body(ctx_ref, cmod_ref, n1_ref, win_ref, rlf_ref, rlb_ref, hc_ref, kvc_ref, scf_ref, scb_ref):
        x = ctx_ref[...]
        r = lax.rsqrt(jnp.mean(x * x, axis=-1, keepdims=True) + EPS)
        hc = (x * r) * (n1_ref[...] * (1.0 + cmod_ref[1:2, :])) + cmod_ref[0:1, :]
        hcb = hc.astype(BF16)
        hc_ref[...] = hcb
        kv = _dot_nt(hcb, win_ref[KV_LO:KV_HI, :])
        k = kv[:, :RW] * KSCALE
        v = kv[:, RW:]
        kvc_ref[:, :RW] = k
        kvc_ref[:, RW:] = v
        t = lax.broadcasted_iota(jnp.int32, (lc, 1), 0).astype(F32)
        lgf = _log_sigmoid(rlf_ref[...])
        lgb = _log_sigmoid(rlb_ref[...])
        for h in range(H):
            hs = slice(h * DH, (h + 1) * DH)
            wf = jnp.exp(lgf[h:h + 1, 0:1] * (lc - 1.0 - t))
            wb = jnp.exp(lgb[h:h + 1, 0:1] * t)
            vh = v[:, hs].astype(BF16)
            scf_ref[h] = _dot_tn((k[:, hs] * wf).astype(BF16), vh)
            scb_ref[h] = _dot_tn((k[:, hs] * wb).astype(BF16), vh)

    return pl.pallas_call(
        body, name="ctx_fwd",
        out_shape=(jax.ShapeDtypeStruct((lc, D), BF16), jax.ShapeDtypeStruct((lc, 2 * RW), F32),
                   jax.ShapeDtypeStruct((H, DH, DH), F32), jax.ShapeDtypeStruct((H, DH, DH), F32)),
        compiler_params=_cparams(),
    )(ctx, cmod6, norm1, win, rlf, rlb)


def _sg_forward(u, v, sgw_ref, sgbt_ref, sgg_ref, nc):
    ua, dgu = _gelu(u)
    va, dgv = _gelu(v)
    gain = sgg_ref[...]
    mixed, vn_b, vah_l, rv_l = [], [], [], []
    for g in range(G):
        gs = slice(g * DH, (g + 1) * DH)
        vag = va[:, gs]
        rv = lax.rsqrt(jnp.mean(vag * vag, axis=-1, keepdims=True) + EPS)
        vah = vag * rv
        vn = (vah * gain[:, gs]).astype(BF16)
        wg = sgw_ref[g].astype(BF16)
        bcol = sgbt_ref[g]
        rows = [_dot(wg, vn[c * C:(c + 1) * C, :]) + bcol for c in range(nc)]
        mixed.append(jnp.concatenate(rows, axis=0) if nc > 1 else rows[0])
        vn_b.append(vn)
        vah_l.append(vah)
        rv_l.append(rv)
    mixed = jnp.concatenate(mixed, axis=1)
    return ua * mixed, (ua, dgu, dgv, mixed, vn_b, vah_l, rv_l)


def _fwd_a(x, mod6, norm1, win, sgw, sgbt, sggain, cos, sin, rlf, scf, dmf, *, tm):
    t_len = x.shape[0]
    nt, nc, ncr = t_len // tm, tm // C, tm // CR

    def body(x_ref, mod_ref, n1_ref, win_ref, sgw_ref, sgbt_ref, sgg_ref, cos_ref, sin_ref, rlf_ref, scf_ref, dm_ref,
             hx_ref, zuv_ref, q_ref, k_ref, v_ref, gfb_ref, of_ref, ya_ref, sst_ref, s_scr):
        i = pl.program_id(0)

        @pl.when(i == 0)
        def _():
            s_scr[...] = scf_ref[...]

        x = x_ref[...]
        r = lax.rsqrt(jnp.mean(x * x, axis=-1, keepdims=True) + EPS)
        hx = (x * r) * (n1_ref[...] * (1.0 + mod_ref[1:2, :])) + mod_ref[0:1, :]
        hxb = hx.astype(BF16)
        hx_ref[...] = hxb
        z = _dot_nt(hxb, win_ref[...])
        zuv_ref[...] = z[:, :2 * AW].astype(BF16)
        gfb_ref[...] = z[:, G_LO:INC].astype(BF16)
        cos = jnp.tile(cos_ref[...], (1, H))
        sins = jnp.tile(sin_ref[...], (1, H))
        q_ref[...] = _rope(z[:, Q_LO:KV_LO], cos, sins).astype(BF16)
        k_ref[...] = (_rope(z[:, KV_LO:VR_LO], cos, sins) * KSCALE).astype(BF16)
        v_ref[...] = z[:, VR_LO:G_LO].astype(BF16)
        ya, _ = _sg_forward(z[:, :AW], z[:, AW:2 * AW], sgw_ref, sgbt_ref, sgg_ref, nc)
        ya_ref[...] = ya.astype(BF16)

        lg = _log_sigmoid(rlf_ref[...])
        for h in range(H):
            xi, zc, dec, _ = _decay_consts(lg[h:h + 1, 0:1], True)
            dm = dm_ref[h]
            hs = slice(h * DH, (h + 1) * DH)
            for c in range(ncr):
                rs = slice(c * CR, (c + 1) * CR)
                qc, kc, vc = q_ref[rs, hs], k_ref[rs, hs], v_ref[rs, hs]
                s = s_scr[h]
                sst_ref[c, h] = s
                a = (_dot_nt(qc, kc) * dm).astype(BF16)
                of_ref[rs, hs] = (_dot(a, vc) + xi * _dot(qc, s.astype(BF16))).astype(BF16)
                kz = (kc.astype(F32) * zc).astype(BF16)
                s_scr[h] = dec * s + _dot_tn(kz, vc)

    n_chunks = t_len // CR
    return pl.pallas_call(
        body, name="fwd_a", grid=(nt,),
        in_specs=[_rows(tm, D), _whole((6, D)), _whole((1, D)), _whole((INC, D)), _whole((G, C, C)),
                  _whole((G, C, 128)), _whole((1, AW)), _rows(tm, DH), _rows(tm, DH), _whole((H, 128)),
                  _whole((H, DH, DH)), _whole((H, CR, CR))],
        out_specs=[_rows(tm, D), _rows(tm, 2 * AW), _rows(tm, RW), _rows(tm, RW), _rows(tm, RW),
                   _rows(tm, 2 * RW), _rows(tm, RW), _rows(tm, AW),
                   pl.BlockSpec((ncr, H, DH, DH), lambda i: (i, 0, 0, 0))],
        out_shape=(jax.ShapeDtypeStruct((t_len, D), BF16), jax.ShapeDtypeStruct((t_len, 2 * AW), BF16),
                   jax.ShapeDtypeStruct((t_len, RW), BF16), jax.ShapeDtypeStruct((t_len, RW), BF16),
                   jax.ShapeDtypeStruct((t_len, RW), BF16), jax.ShapeDtypeStruct((t_len, 2 * RW), BF16),
                   jax.ShapeDtypeStruct((t_len, RW), BF16), jax.ShapeDtypeStruct((t_len, AW), BF16),
                   jax.ShapeDtypeStruct((n_chunks, H, DH, DH), F32)),
        scratch_shapes=[pltpu.VMEM((H, DH, DH), F32)],
        compiler_params=_cparams(1),
    )(x, mod6, norm1, win, sgw, sgbt, sggain, cos, sin, rlf, scf, dmf)


def _fwd_b(q, k, v, of, gfb, ya, x, mod6, wout, rlb, scb, dmb, *, tm):
    t_len = x.shape[0]
    nt, nc = t_len // tm, tm // CR

    def body(q_ref, k_ref, v_ref, of_ref, gfb_ref, ya_ref, x_ref, mod_ref, wout_ref, rlb_ref, scb_ref, dm_ref,
             ob_ref, ycat_ref, y_ref, x1_ref, rst_ref, r_scr):
        i = pl.program_id(0)

        @pl.when(i == 0)
        def _():
            r_scr[...] = scb_ref[...]

        lg = _log_sigmoid(rlb_ref[...])
        consts = [_decay_consts(lg[h:h + 1, 0:1], False) for h in range(H)]

        def first(c, h):
            _, zc, dec, _ = consts[h]
            hs, rs = slice(h * DH, (h + 1) * DH), slice(c * CR, (c + 1) * CR)
            qc, kc, vc = q_ref[rs, hs], k_ref[rs, hs], v_ref[rs, hs]
            s = r_scr[h]
            rst_ref[c, h] = s
            scores = _dot_nt(qc, kc)
            cross = _dot(qc, s.astype(BF16))
            kz = (kc.astype(F32) * zc).astype(BF16)
            r_scr[h] = dec * s + _dot_tn(kz, vc)
            return (scores * dm_ref[h]).astype(BF16), cross

        def second(c, h, carried):
            a, cross = carried
            hs, rs = slice(h * DH, (h + 1) * DH), slice(c * CR, (c + 1) * CR)
            ob_ref[rs, hs] = (_dot(a, v_ref[rs, hs]) + consts[h][0] * cross).astype(BF16)

        def rows_out(c):
            rs = slice(c * CR, (c + 1) * CR)
            gfb = gfb_ref[rs, :].astype(F32)
            sf, _ = _silu_parts(gfb[:, :RW])
            sb, _ = _silu_parts(gfb[:, RW:])
            hnf, _ = _headnorm(of_ref[rs, :].astype(F32))
            hnb, _ = _headnorm(ob_ref[rs, :].astype(F32))
            yr = sf * hnf + sb * hnb
            ycat = jnp.concatenate([ya_ref[rs, :], yr.astype(BF16)], axis=1)
            ycat_ref[rs, :] = ycat
            y = _dot(ycat, wout_ref[...])
            y_ref[rs, :] = y.astype(BF16)
            x1_ref[rs, :] = x_ref[rs, :] + mod_ref[2:3, :] * y

        pending, waiting_rows = None, None
        for c in reversed(range(nc)):
            for h in range(H):
                carried = first(c, h)
                if pending is not None:
                    second(*pending)
                pending = (c, h, carried)
            if waiting_rows is not None:
                rows_out(waiting_rows)
            waiting_rows = c
        second(*pending)
        rows_out(waiting_rows)

    n_chunks = t_len // CR
    rr = functools.partial(_rows_rev, nt=nt)
    return pl.pallas_call(
        body, name="fwd_b", grid=(nt,),
        in_specs=[rr(tm, RW), rr(tm, RW), rr(tm, RW), rr(tm, RW), rr(tm, 2 * RW), rr(tm, AW), rr(tm, D),
                  _whole((6, D)), _whole((D, D)), _whole((H, 128)), _whole((H, DH, DH)), _whole((H, CR, CR))],
        out_specs=[rr(tm, RW), rr(tm, D), rr(tm, D), rr(tm, D),
                   pl.BlockSpec((nc, H, DH, DH), lambda i: (nt - 1 - i, 0, 0, 0))],
        out_shape=(jax.ShapeDtypeStruct((t_len, RW), BF16), jax.ShapeDtypeStruct((t_len, D), BF16),
                   jax.ShapeDtypeStruct((t_len, D), BF16), jax.ShapeDtypeStruct((t_len, D), F32),
                   jax.ShapeDtypeStruct((n_chunks, H, DH, DH), F32)),
        scratch_shapes=[pltpu.VMEM((H, DH, DH), F32)],
        compiler_params=_cparams(1),
    )(q, k, v, of, gfb, ya, x, mod6, wout, rlb, scb, dmb)


def _ffn(x1, tgt, mod6, norm2, normf, wg, wu, wd, *, tm):
    t_len = x1.shape[0]
    nt = t_len // tm

    def body(x1_ref, tgt_ref, mod_ref, n2_ref, nf_ref, wg_ref, wu_ref, wd_ref,
             dx1_ref, h2_ref, da_ref, db_ref, hm_ref, df_ref, small_ref):
        i = pl.program_id(0)

        @pl.when(i == 0)
        def _():
            small_ref[...] = jnp.zeros_like(small_ref)

        sh2, sc2, g2 = mod_ref[3:4, :], mod_ref[4:5, :], mod_ref[5:6, :]
        n2, nf = n2_ref[...], nf_ref[...]
        x1 = x1_ref[...]
        r2 = lax.rsqrt(jnp.mean(x1 * x1, axis=-1, keepdims=True) + EPS)
        x1h = x1 * r2
        w2 = n2 * (1.0 + sc2)
        h2b = (x1h * w2 + sh2).astype(BF16)
        h2_ref[...] = h2b
        a = _dot_nt(h2b, wg_ref[...])
        b = _dot_nt(h2b, wu_ref[...])
        sa, dsa = _silu_parts(a)
        hmb = (sa * b).astype(BF16)
        hm_ref[...] = hmb
        f = _dot(hmb, wd_ref[...])
        x2 = x1 + g2 * f
        r3 = lax.rsqrt(jnp.mean(x2 * x2, axis=-1, keepdims=True) + EPS)
        x2h = x2 * r3
        diff = x2h * nf - tgt_ref[...]
        small_ref[5:6, :] += jnp.sum(diff * diff, axis=0, keepdims=True)
        dout = diff * (1.0 / D)
        small_ref[0:1, :] += jnp.sum(dout * x2h, axis=0, keepdims=True)
        dyg = dout * nf
        dx2 = r3 * (dyg - x2h * jnp.mean(dyg * x2h, axis=-1, keepdims=True))
        small_ref[1:2, :] += jnp.sum(dx2 * f, axis=0, keepdims=True)
        dfb = (dx2 * g2).astype(BF16)
        df_ref[...] = dfb
        dhm = _dot_nt(dfb, wd_ref[...])
        dbb = (dhm * sa).astype(BF16)
        dab = (dhm * b * dsa).astype(BF16)
        da_ref[...] = dab
        db_ref[...] = dbb
        dh2 = _dot(dab, wg_ref[...]) + _dot(dbb, wu_ref[...])
        small_ref[2:3, :] += jnp.sum(dh2, axis=0, keepdims=True)
        dhx = dh2 * x1h
        small_ref[3:4, :] += jnp.sum(dhx, axis=0, keepdims=True) * n2
        small_ref[4:5, :] += jnp.sum(dhx, axis=0, keepdims=True) * (1.0 + sc2)
        dyg2 = dh2 * w2
        dx1_ref[...] = dx2 + r2 * (dyg2 - x1h * jnp.mean(dyg2 * x1h, axis=-1, keepdims=True))

    return pl.pallas_call(
        body, name="ffn", grid=(nt,),
        in_specs=[_rows(tm, D), _rows(tm, D), _whole((6, D)), _whole((1, D)), _whole((1, D)),
                  _whole((DFF, D)), _whole((DFF, D)), _whole((DFF, D))],
        out_specs=[_rows(tm, D), _rows(tm, D), _rows(tm, DFF), _rows(tm, DFF), _rows(tm, DFF), _rows(tm, D),
                   _acc((8, D))],
        out_shape=(jax.ShapeDtypeStruct((t_len, D), F32), jax.ShapeDtypeStruct((t_len, D), BF16),
                   jax.ShapeDtypeStruct((t_len, DFF), BF16), jax.ShapeDtypeStruct((t_len, DFF), BF16),
                   jax.ShapeDtypeStruct((t_len, DFF), BF16), jax.ShapeDtypeStruct((t_len, D), BF16),
                   jax.ShapeDtypeStruct((8, D), F32)),
        compiler_params=_cparams(1),
    )(x1, tgt, mod6, norm2, normf, wg, wu, wd)


def _mid_bwd(dx1, y, of, ob, gfb, mod6, wout, *, tm):
    t_len = dx1.shape[0]
    nt = t_len // tm
    rc = min(256, tm)

    def body(dx1_ref, y_ref, of_ref, ob_ref, gfb_ref, mod_ref, wout_ref,
             dy_ref, dya_ref, dgfb_ref, dof_ref, dob_ref, dg1_ref):
        i = pl.program_id(0)

        @pl.when(i == 0)
        def _():
            dg1_ref[...] = jnp.zeros_like(dg1_ref)

        for c in range(tm // rc):
            rows = slice(c * rc, (c + 1) * rc)
            dx1 = dx1_ref[rows, :]
            dg1_ref[0:1, :] += jnp.sum(dx1 * y_ref[rows, :].astype(F32), axis=0, keepdims=True)
            dyb = (dx1 * mod_ref[2:3, :]).astype(BF16)
            dy_ref[rows, :] = dyb
            dycat = _dot_nt(dyb, wout_ref[...])
            dya_ref[rows, :] = dycat[:, :AW].astype(BF16)
            dyr = dycat[:, AW:]
            gfb = gfb_ref[rows, :].astype(F32)
            for o_ref, do_ref, lo in ((of_ref, dof_ref, 0), (ob_ref, dob_ref, RW)):
                sg, dsg = _silu_parts(gfb[:, lo:lo + RW])
                o = o_ref[rows, :].astype(F32)
                hn, rs = _headnorm(o)
                dgfb_ref[rows, lo:lo + RW] = (dyr * hn * dsg).astype(BF16)
                dhn = dyr * sg
                for h in range(H):
                    hs = slice(h * DH, (h + 1) * DH)
                    oh, dh = hn[:, hs], dhn[:, hs]
                    do_ref[rows, hs] = (rs[h] * (dh - oh * jnp.mean(dh * oh, axis=-1, keepdims=True))).astype(BF16)

    return pl.pallas_call(
        body, name="mid_bwd", grid=(nt,),
        in_specs=[_rows(tm, D), _rows(tm, D), _rows(tm, RW), _rows(tm, RW), _rows(tm, 2 * RW),
                  _whole((6, D)), _whole((D, D))],
        out_specs=[_rows(tm, D), _rows(tm, AW), _rows(tm, 2 * RW), _rows(tm, RW), _rows(tm, RW), _acc((8, D))],
        out_shape=(jax.ShapeDtypeStruct((t_len, D), BF16), jax.ShapeDtypeStruct((t_len, AW), BF16),
                   jax.ShapeDtypeStruct((t_len, 2 * RW), BF16), jax.ShapeDtypeStruct((t_len, RW), BF16),
                   jax.ShapeDtypeStruct((t_len, RW), BF16), jax.ShapeDtypeStruct((8, D), F32)),
        compiler_params=_cparams(1),
    )(dx1, y, of, ob, gfb, mod6, wout)


def _ret_bwd_items(q_ref, k_ref, v_ref, do_ref, st_ref, dq_ref, dk_ref, dv_ref, ds_scr, dlg_scr, lg, dm_ref, forward,
                   nc, row0):
    order = list(reversed(range(nc))) if forward else list(range(nc))
    consts = [_decay_consts(lg[h:h + 1, 0:1], forward) for h in range(H)]
    accs = [jnp.zeros((1, DH), F32) for _ in range(H)]

    def first(h, c):
        xi, zc, dec, _ = consts[h]
        dm = dm_ref[h]
        hs, rs = slice(h * DH, (h + 1) * DH), slice(c * CR, (c + 1) * CR)
        qc, kc, vc, doc = q_ref[rs, hs], k_ref[rs, hs], v_ref[rs, hs], do_ref[rs, hs]
        s, dsn = st_ref[c, h], ds_scr[h]
        sb, dsnb = s.astype(BF16), dsn.astype(BF16)
        qf, kf = qc.astype(F32), kc.astype(F32)
        a_raw = _dot_nt(qc, kc)
        da_raw = _dot_nt(doc, vc)
        xdo = (xi * doc.astype(F32)).astype(BF16)
        kz = (kf * zc).astype(BF16)
        vz = (vc.astype(F32) * zc).astype(BF16)
        dq_c = _dot_nt(xdo, sb)
        dk_s = _dot_nt(vz, dsnb)
        dv_s = _dot(kz, dsnb)
        ds_scr[h] = _dot_tn((xi * qf).astype(BF16), doc) + dec * dsn
        accs[h] = accs[h] + (float(CR) * dec) * jnp.sum(s * dsn, axis=0, keepdims=True)
        a = (a_raw * dm).astype(BF16)
        da = (da_raw * dm).astype(BF16)
        return a, da, dq_c, dk_s, dv_s

    def second(h, c, carried):
        a, da, dq_c, dk_s, dv_s = carried
        ic = consts[h][3]
        if forward:
            w_qi, w_qc, w_ki, w_ks = ic, ic + 1.0, -ic, (CR - 1.0) - ic
        else:
            w_qi, w_qc, w_ki, w_ks = -ic, CR - ic, ic, ic
        hs, rs = slice(h * DH, (h + 1) * DH), slice(c * CR, (c + 1) * CR)
        qc, kc, doc = q_ref[rs, hs], k_ref[rs, hs], do_ref[rs, hs]
        dq_i = _dot(da, kc)
        dk_i = _dot_tn(da, qc)
        dv_i = _dot_tn(a, doc)
        dq_ref[rs, hs] = (dq_i + dq_c).astype(BF16)
        dk_ref[rs, hs] = (dk_i + dk_s).astype(BF16)
        dv_ref[rs, hs] = (dv_i + dv_s).astype(BF16)
        rowterm = qc.astype(F32) * (w_qi * dq_i + w_qc * dq_c) + kc.astype(F32) * (w_ki * dk_i + w_ks * dk_s)
        accs[h] = accs[h] + jnp.sum(rowterm, axis=0, keepdims=True)

    def finish():
        for h in range(H):
            dlg_scr[row0 + h:row0 + h + 1, :] += accs[h]

    items = [[(functools.partial(first, h, c), functools.partial(second, h, c)) for h in range(H)] for c in order]
    return items, finish


def _ret_bwd_run(dirs):
    nc = len(dirs[0][0])
    flat = [it for j in range(nc) for items, _ in dirs for it in items[j]]
    pending = None
    for first, second in flat:
        carried = first()
        if pending is not None:
            pending[0](pending[1])
        pending = (second, carried)
    pending[0](pending[1])
    for _, finish in dirs:
        finish()


def _ret_bwd(q, k, v, dof, dob, sst, rst, rlf, rlb, dmf, dmb, *, tm):
    t_len = q.shape[0]
    nt, nc = t_len // tm, tm // CR

    def body(qf_ref, kf_ref, vf_ref, dof_ref, sst_ref, qb_ref, kb_ref, vb_ref, dob_ref, rst_ref, rlf_ref, rlb_ref,
             dmf_ref, dmb_ref, dqf_ref, dkf_ref, dvf_ref, dqb_ref, dkb_ref, dvb_ref, dscf_ref, dscb_ref, dlg_ref,
             dsf_scr, dsb_scr, dlg_scr):
        i = pl.program_id(0)

        @pl.when(i == 0)
        def _():
            dsf_scr[...] = jnp.zeros_like(dsf_scr)
            dsb_scr[...] = jnp.zeros_like(dsb_scr)
            dlg_scr[...] = jnp.zeros_like(dlg_scr)

        lgf = _log_sigmoid(rlf_ref[...])
        lgb = _log_sigmoid(rlb_ref[...])
        _ret_bwd_run([
            _ret_bwd_items(qf_ref, kf_ref, vf_ref, dof_ref, sst_ref, dqf_ref, dkf_ref, dvf_ref, dsf_scr, dlg_scr,
                           lgf, dmf_ref, True, nc, 0),
            _ret_bwd_items(qb_ref, kb_ref, vb_ref, dob_ref, rst_ref, dqb_ref, dkb_ref, dvb_ref, dsb_scr, dlg_scr,
                           lgb, dmb_ref, False, nc, H)])

        @pl.when(i == nt - 1)
        def _():
            dscf_ref[...] = dsf_scr[...]
            dscb_ref[...] = dsb_scr[...]
            tot = jnp.broadcast_to(jnp.sum(dlg_scr[...], axis=1, keepdims=True), (8, 128))
            ri = lax.broadcasted_iota(jnp.int32, (8, 128), 0)
            li = lax.broadcasted_iota(jnp.int32, (8, 128), 1)
            dlg_ref[...] = jnp.zeros_like(dlg_ref)
            dlg_ref[0:1, 0:128] = jnp.sum(jnp.where(ri == li, tot, 0.0), axis=0, keepdims=True)
            dlg_ref[1:2, 0:128] = jnp.sum(jnp.where(ri - H == li, tot, 0.0), axis=0, keepdims=True)

    rr = functools.partial(_rows_rev, nt=nt)
    st_f = pl.BlockSpec((nc, H, DH, DH), lambda i: (nt - 1 - i, 0, 0, 0))
    st_b = pl.BlockSpec((nc, H, DH, DH), lambda i: (i, 0, 0, 0))
    tok = jax.ShapeDtypeStruct((t_len, RW), BF16)
    st = jax.ShapeDtypeStruct((H, DH, DH), F32)
    return pl.pallas_call(
        body, name="ret_bwd", grid=(nt,),
        in_specs=[rr(tm, RW), rr(tm, RW), rr(tm, RW), rr(tm, RW), st_f,
                  _rows(tm, RW), _rows(tm, RW), _rows(tm, RW), _rows(tm, RW), st_b,
                  _whole((H, 128)), _whole((H, 128)), _whole((H, CR, CR)), _whole((H, CR, CR))],
        out_specs=[rr(tm, RW), rr(tm, RW), rr(tm, RW), _rows(tm, RW), _rows(tm, RW), _rows(tm, RW),
                   _acc((H, DH, DH)), _acc((H, DH, DH)), _acc((8, D))],
        out_shape=(tok, tok, tok, tok, tok, tok, st, st, jax.ShapeDtypeStruct((8, D), F32)),
        scratch_shapes=[pltpu.VMEM((H, DH, DH), F32), pltpu.VMEM((H, DH, DH), F32), pltpu.VMEM((8, 128), F32)],
        compiler_params=_cparams(1),
    )(q, k, v, dof, sst, q, k, v, dob, rst, rlf, rlb, dmf, dmb)


def _in_bwd(x, zuv, dya, dqf, dkf, dvf, dqb, dkb, dvb, dgfb, dx1, cos, sin, mod6, norm1, win, sgw, sgbt, sggain, *, tm):
    t_len = x.shape[0]
    nt, nc = t_len // tm, tm // C

    def body(x_ref, zuv_ref, dya_ref, dqf_ref, dkf_ref, dvf_ref, dqb_ref, dkb_ref, dvb_ref, dgfb_ref, dx1_ref,
             cos_ref, sin_ref, mod_ref, n1_ref, win_ref, sgw_ref, sgbt_ref, sgg_ref,
             gx_ref, dz_ref, small_ref, dsgw_ref, dsgbt_ref):
        i = pl.program_id(0)

        @pl.when(i == 0)
        def _():
            small_ref[...] = jnp.zeros_like(small_ref)
            dsgw_ref[...] = jnp.zeros_like(dsgw_ref)
            dsgbt_ref[...] = jnp.zeros_like(dsgbt_ref)

        zuv = zuv_ref[...].astype(F32)
        _, (ua, dgu, dgv, mixed, vn_b, vah_l, rv_l) = _sg_forward(zuv[:, :AW], zuv[:, AW:], sgw_ref, sgbt_ref, sgg_ref, nc)
        dya = dya_ref[...].astype(F32)
        dz_ref[:, 0:AW] = (dya * mixed * dgu).astype(BF16)
        dmixed = dya * ua
        gain = sgg_ref[...]
        for g in range(G):
            gs = slice(g * DH, (g + 1) * DH)
            dmg = dmixed[:, gs]
            dmb = dmg.astype(BF16)
            wgt = sgw_ref[g].astype(BF16)
            dsw = jnp.zeros((C, C), F32)
            dsb = jnp.zeros((C, DH), F32)
            rows = []
            for c in range(nc):
                rs = slice(c * C, (c + 1) * C)
                dsw = dsw + _dot_nt(dmb[rs, :], vn_b[g][rs, :])
                dsb = dsb + dmg[rs, :]
                rows.append(_dot_tn(wgt, dmb[rs, :]))
            dsgw_ref[g] += dsw
            dsgbt_ref[g] += dsb
            dvn = jnp.concatenate(rows, axis=0) if nc > 1 else rows[0]
            vah, rv = vah_l[g], rv_l[g]
            small_ref[3:4, gs] += jnp.sum(dvn * vah, axis=0, keepdims=True)
            dyg = dvn * gain[:, gs]
            dva = rv * (dyg - vah * jnp.mean(dyg * vah, axis=-1, keepdims=True))
            dz_ref[:, AW + g * DH:AW + (g + 1) * DH] = (dva * dgv[:, gs]).astype(BF16)

        cos = jnp.tile(cos_ref[...], (1, H))
        nsins = -jnp.tile(sin_ref[...], (1, H))
        def both(f_ref, b_ref):
            return f_ref[...].astype(F32) + b_ref[...].astype(F32)

        dz_ref[:, Q_LO:KV_LO] = _rope(both(dqf_ref, dqb_ref), cos, nsins).astype(BF16)
        dz_ref[:, KV_LO:VR_LO] = (_rope(both(dkf_ref, dkb_ref), cos, nsins) * KSCALE).astype(BF16)
        dz_ref[:, VR_LO:G_LO] = both(dvf_ref, dvb_ref).astype(BF16)
        dz_ref[:, G_LO:INC] = dgfb_ref[...]

        dhx = _dot(dz_ref[...], win_ref[...])
        x = x_ref[...]
        r = lax.rsqrt(jnp.mean(x * x, axis=-1, keepdims=True) + EPS)
        xh = x * r
        n1, sc1 = n1_ref[...], mod_ref[1:2, :]
        small_ref[0:1, :] += jnp.sum(dhx, axis=0, keepdims=True)
        dhxx = jnp.sum(dhx * xh, axis=0, keepdims=True)
        small_ref[1:2, :] += dhxx * n1
        small_ref[2:3, :] += dhxx * (1.0 + sc1)
        dyg = dhx * (n1 * (1.0 + sc1))
        gx_ref[...] = dx1_ref[...] + r * (dyg - xh * jnp.mean(dyg * xh, axis=-1, keepdims=True))

        @pl.when(i == nt - 1)
        def _():
            ri = lax.broadcasted_iota(jnp.int32, (C, DH), 0)
            li = lax.broadcasted_iota(jnp.int32, (C, DH), 1)
            for g in range(G):
                tot = jnp.broadcast_to(jnp.sum(dsgbt_ref[g], axis=1, keepdims=True), (C, DH))
                small_ref[4 + g:5 + g, 0:DH] = jnp.sum(jnp.where(ri == li, tot, 0.0), axis=0, keepdims=True)

    tok = functools.partial(_rows, tm)
    return pl.pallas_call(
        body, name="in_bwd", grid=(nt,),
        in_specs=[tok(D), tok(2 * AW), tok(AW), tok(RW), tok(RW), tok(RW), tok(RW), tok(RW), tok(RW),
                  tok(2 * RW), tok(D), tok(DH), tok(DH), _whole((6, D)), _whole((1, D)), _whole((INC, D)),
                  _whole((G, C, C)), _whole((G, C, 128)), _whole((1, AW))],
        out_specs=[tok(D), tok(INC), _acc((8, D)), _acc((G, C, C))],
        out_shape=(jax.ShapeDtypeStruct((t_len, D), F32), jax.ShapeDtypeStruct((t_len, INC), BF16),
                   jax.ShapeDtypeStruct((8, D), F32), jax.ShapeDtypeStruct((G, C, C), F32)),
        scratch_shapes=[pltpu.VMEM((G, C, 128), F32)],
        compiler_params=_cparams(1),
    )(x, zuv, dya, dqf, dkf, dvf, dqb, dkb, dvb, dgfb, dx1, cos, sin, mod6, norm1, win, sgw, sgbt, sggain)


def _tn_matmul(a, b, *, bm, bt, name, init=None, init_rows=None, after=(), cols=None):
    t_len, m = a.shape
    cj, n = (0, b.shape[1]) if cols is None else cols
    ni, ntt = m // bm, t_len // bt
    has_init = init is not None

    def body(*refs):
        o_ref, ob_ref = refs[-2:]
        a_ref, b_ref = refs[:2]
        init_ref = refs[2] if has_init else None
        i, t = pl.program_id(0), pl.program_id(1)

        @pl.when(t == 0)
        def _():
            o_ref[...] = jnp.zeros_like(o_ref)

        if has_init:
            for ib in range(ni):
                lo, hi = max(init_rows[0], ib * bm), min(init_rows[1], (ib + 1) * bm)
                if lo < hi:
                    @pl.when(jnp.logical_and(t == 0, i == ib))
                    def _(lo=lo, hi=hi, ib=ib):
                        o_ref[lo - ib * bm:hi - ib * bm, :] = init_ref[lo - init_rows[0]:hi - init_rows[0], :]

        o_ref[...] += _dot_tn(a_ref[...], b_ref[...])

        @pl.when(t == ntt - 1)
        def _():
            ob_ref[...] = o_ref[...].astype(BF16)

    in_specs = [pl.BlockSpec((bt, bm), lambda i, t: (t, i)), pl.BlockSpec((bt, n), lambda i, t: (t, cj))]
    args = [a, b]
    if has_init:
        in_specs.append(pl.BlockSpec((init.shape[0], n), lambda i, t: (0, cj), pipeline_mode=pl.Buffered(1)))
        args.append(init)
    for arr in after:
        in_specs.append(pl.BlockSpec(memory_space=pl.ANY))
        args.append(arr)
    out_spec = pl.BlockSpec((bm, n), lambda i, t: (i, 0))
    return pl.pallas_call(
        body, name=name, grid=(ni, ntt),
        in_specs=in_specs,
        out_specs=[out_spec, out_spec],
        out_shape=(jax.ShapeDtypeStruct((m, n), F32), jax.ShapeDtypeStruct((m, n), BF16)),
        compiler_params=_cparams(2),
    )(*args)


def _ctx_bwd(ctx, hc, kvc, cmod6, norm1, win, rlf, rlb, dscf, dscb, dlg_in):
    lc = ctx.shape[0]

    def body(ctx_ref, hc_ref, kvc_ref, cmod_ref, n1_ref, win_ref, rlf_ref, rlb_ref, dscf_ref, dscb_ref, dlgin_ref,
             dwin_ref, small_ref, dlg_ref):
        k = kvc_ref[:, :RW]
        v = kvc_ref[:, RW:]
        t = lax.broadcasted_iota(jnp.int32, (lc, 1), 0).astype(F32)
        lgf = _log_sigmoid(rlf_ref[...])
        lgb = _log_sigmoid(rlb_ref[...])
        dks, dvs = [], []
        lane = lax.broadcasted_iota(jnp.int32, (1, 128), 1)
        row_f = jnp.zeros((1, 128), F32)
        row_b = jnp.zeros((1, 128), F32)
        for h in range(H):
            hs = slice(h * DH, (h + 1) * DH)
            wf = jnp.exp(lgf[h:h + 1, 0:1] * (lc - 1.0 - t))
            wb = jnp.exp(lgb[h:h + 1, 0:1] * t)
            kh, vhb = k[:, hs], v[:, hs].astype(BF16)
            dsf, dsb = dscf_ref[h].astype(BF16), dscb_ref[h].astype(BF16)
            dkf = wf * _dot_nt(vhb, dsf)
            dkb = wb * _dot_nt(vhb, dsb)
            dvs.append(_dot((kh * wf).astype(BF16), dsf) + _dot((kh * wb).astype(BF16), dsb))
            dks.append((dkf + dkb) * KSCALE)
            lf = jnp.sum((lc - 1.0 - t) * kh * dkf, axis=0, keepdims=True)
            lb = jnp.sum(t * kh * dkb, axis=0, keepdims=True)
            row_f = row_f + jnp.where(lane == h, jnp.sum(lf, axis=1, keepdims=True), 0.0)
            row_b = row_b + jnp.where(lane == h, jnp.sum(lb, axis=1, keepdims=True), 0.0)
        dlg_ref[...] = dlgin_ref[...]
        dlg_ref[0:1, 0:128] += row_f
        dlg_ref[1:2, 0:128] += row_b
        dkv = jnp.concatenate(dks + dvs, axis=1).astype(BF16)
        dhc = _dot(dkv, win_ref[KV_LO:KV_HI, :])
        dwin_ref[...] = _dot_tn(dkv, hc_ref[...])
        x = ctx_ref[...]
        r = lax.rsqrt(jnp.mean(x * x, axis=-1, keepdims=True) + EPS)
        xh = x * r
        small_ref[...] = jnp.zeros_like(small_ref)
        small_ref[0:1, :] = jnp.sum(dhc, axis=0, keepdims=True)
        dhxx = jnp.sum(dhc * xh, axis=0, keepdims=True)
        small_ref[1:2, :] = dhxx * n1_ref[...]
        small_ref[2:3, :] = dhxx * (1.0 + cmod_ref[1:2, :])

    return pl.pallas_call(
        body, name="ctx_bwd",
        out_shape=(jax.ShapeDtypeStruct((2 * RW, D), F32), jax.ShapeDtypeStruct((8, D), F32),
                   jax.ShapeDtypeStruct((8, D), F32)),
        compiler_params=_cparams(),
    )(ctx, hc, kvc, cmod6, norm1, win, rlf, rlb, dscf, dscb, dlg_in)


def _adam_math(w, g, m, v):
    m = ADAM_B1 * m + (1.0 - ADAM_B1) * g
    v = ADAM_B2 * v + (1.0 - ADAM_B2) * (g * g)
    m_hat = m / (1.0 - ADAM_B1 ** ADAM_STEP)
    v_hat = v / (1.0 - ADAM_B2 ** ADAM_STEP)
    delta = -ADAM_LR * (m_hat / (jnp.sqrt(v_hat) + ADAM_EPS) + ADAM_WD * w)
    return delta, m, v


def _place():
    x, y, c = lax.axis_index("x"), lax.axis_index("y"), lax.axis_index("c")
    return x, y, c, 4 * x + 2 * y + c


def _flip(x, y, c, k):
    px = 1 - x if (k >> 2) & 1 else x
    py = 1 - y if (k >> 1) & 1 else y
    pc = 1 - c if k & 1 else c
    return (px, py, pc), 4 * px + 2 * py + pc


def _gather_copy(buf_ref, send_sems, recv_sems, sem0, k, mine):
    x, y, c, me = _place()
    dev, idx = _flip(x, y, c, k)
    blk = me if mine else idx
    return pltpu.make_async_remote_copy(
        src_ref=buf_ref.at[blk], dst_ref=buf_ref.at[blk],
        send_sem=send_sems.at[sem0 + k - 1], recv_sem=recv_sems.at[sem0 + k - 1],
        device_id=dev, device_id_type=MESH)


def _entry_gather(c_row, cctx_row, wmod, bmod, shards):
    n = len(shards)
    ncol = wmod.shape[1]

    def body(*refs):
        c_ref, cctx_ref, wmod_ref, bmod_ref = refs[:4]
        in_refs = refs[4:4 + n]
        mod_ref, cs_ref = refs[4 + n:6 + n]
        out_refs = refs[6 + n:6 + 2 * n]
        cbuf, pbuf = refs[6 + 2 * n:8 + 2 * n]
        cast_refs = refs[8 + 2 * n:8 + 3 * n]
        small_send, small_recv, send_sems, recv_sems, local_sems = refs[8 + 3 * n:]
        x, y, c, me = _place()
        sib = (x, y, 1 - c)
        chips = [(1 - x, y), (x, 1 - y), (1 - x, 1 - y)]

        cbuf[me] = jnp.broadcast_to(c_ref[...], (8, D))
        small = [_gather_copy(cbuf, small_send, small_recv, 0, k, True) for k in range(1, NDEV)]
        for cp in small:
            cp.start()

        def blk(px, py, pc):
            return 4 * px + 2 * py + pc

        def copy(w, k, block, to, src=None):
            dst = out_refs[w].at[block]
            return pltpu.make_async_remote_copy(
                src_ref=dst if src is None else src, dst_ref=dst,
                send_sem=send_sems.at[w, k], recv_sem=recv_sems.at[w, k], device_id=to, device_id_type=MESH)

        first, passed, mine = [], [], []
        for w in range(n):
            cast_refs[w][...] = in_refs[w][...].astype(BF16)
            m = pltpu.make_async_copy(cast_refs[w], out_refs[w].at[me], local_sems.at[w])
            m.start()
            mine.append(m)
            cps = [copy(w, 0, me, sib, src=cast_refs[w])]
            cps += [copy(w, 1 + j, me, (*chip, c), src=cast_refs[w]) for j, chip in enumerate(chips)]
            for cp in cps:
                cp.start()
            first += cps

        for k in range(1, NDEV):
            _gather_copy(cbuf, small_send, small_recv, 0, k, False).wait_recv()
        row = lax.broadcasted_iota(jnp.int32, (16, D), 0)
        call = jnp.where(row == 8, jnp.broadcast_to(cctx_ref[...], (16, D)), 0.0)
        for d in range(NDEV):
            call = jnp.where(row == d, jnp.concatenate([cbuf[d], cbuf[d]], axis=0), call)
        cs = call * _sigmoid(call)
        cs_ref[...] = cs
        pbuf[me] = _dot(cs.astype(BF16), wmod_ref[...].astype(BF16))
        small2 = [_gather_copy(pbuf, small_send, small_recv, NDEV - 1, k, True) for k in range(1, NDEV)]
        for cp in small2:
            cp.start()

        for w in range(n):
            for j, chip in enumerate(chips):
                b = blk(*chip, c)
                copy(w, 1 + j, b, (x, y, c)).wait_recv()
                fw = copy(w, 4 + j, b, sib)
                fw.start()
                passed.append(fw)
        for w in range(n):
            copy(w, 0, blk(x, y, 1 - c), (x, y, c)).wait_recv()
            for j, chip in enumerate(chips):
                copy(w, 4 + j, blk(*chip, 1 - c), (x, y, c)).wait_recv()

        for k in range(1, NDEV):
            _gather_copy(pbuf, small_send, small_recv, NDEV - 1, k, False).wait_recv()
        for d in range(NDEV):
            mod_ref[:, d * ncol:(d + 1) * ncol] = pbuf[d] + bmod_ref[:, d * ncol:(d + 1) * ncol]
        for cp in small + small2 + first + passed:
            cp.wait_send()
        for m in mine:
            m.wait()

    vm = pl.BlockSpec(memory_space=pltpu.VMEM)
    hbm = pl.BlockSpec(memory_space=pltpu.HBM)
    return pl.pallas_call(
        body, name="entry_gather",
        in_specs=[vm] * (4 + n), out_specs=[vm, vm] + [hbm] * n,
        out_shape=(jax.ShapeDtypeStruct((16, 6 * D), F32), jax.ShapeDtypeStruct((16, D), F32))
        + tuple(jax.ShapeDtypeStruct((NDEV,) + s.shape, BF16) for s in shards),
        scratch_shapes=[pltpu.VMEM((NDEV, 8, D), F32), pltpu.VMEM((NDEV, 16, ncol), F32)]
        + [pltpu.VMEM(s.shape, BF16) for s in shards]
        + [pltpu.SemaphoreType.DMA((2 * (NDEV - 1),)), pltpu.SemaphoreType.DMA((2 * (NDEV - 1),)),
           pltpu.SemaphoreType.DMA((n, 7)), pltpu.SemaphoreType.DMA((n, 7)), pltpu.SemaphoreType.DMA((n,))],
        compiler_params=_cparams(),
    )(c_row, cctx_row, wmod, bmod, *shards)


_HBM = pl.BlockSpec(memory_space=pltpu.HBM)
_SEMS = pl.BlockSpec(memory_space=pltpu.SEMAPHORE)
_EFFECT = pltpu.SideEffectType.DATAFLOW_SIDE_EFFECTING


def _exchange_copy(src_refs, land_refs, send_sems, recv_sems, w, k, scatter, landing_slot_is_mine):
    x, y, c, me = _place()
    dev, pidx = _flip(x, y, c, k)
    src = src_refs[w].at[pidx] if scatter else src_refs[w]
    slot = me if landing_slot_is_mine else pidx
    return pltpu.make_async_remote_copy(
        src_ref=src, dst_ref=land_refs[w].at[slot],
        send_sem=send_sems.at[w * (NDEV - 1) + k - 1], recv_sem=recv_sems.at[w * (NDEV - 1) + k - 1],
        device_id=dev, device_id_type=MESH)


def _exchange_start(srcs, *, scatter, name):
    n = len(srcs)
    lands = [lax.empty((NDEV,) + tuple(s.shape[-2:]), s.dtype) for s in srcs]

    def body(*refs):
        src_refs, land_refs = refs[:n], refs[n:2 * n]
        send_sems, recv_sems = refs[2 * n], refs[2 * n + 1]
        token = refs[-1]
        for w in range(n):
            for k in range(1, NDEV):
                _exchange_copy(src_refs, land_refs, send_sems, recv_sems, w, k, scatter, True).start()
        token[...] = jnp.zeros_like(token)

    arrs = list(srcs) + lands
    out_shape = ([pltpu.SemaphoreType.DMA((n * (NDEV - 1),)), pltpu.SemaphoreType.DMA((n * (NDEV - 1),))]
                 + [pltpu.HBM(a.shape, a.dtype) for a in arrs] + [jax.ShapeDtypeStruct((8, 128), F32)])
    res = pl.pallas_call(
        body, name=name, out_shape=tuple(out_shape),
        in_specs=[_HBM] * (2 * n),
        out_specs=tuple([_SEMS, _SEMS] + [_HBM] * (2 * n) + [pl.BlockSpec(memory_space=pltpu.VMEM)]),
        input_output_aliases={i: 2 + i for i in range(2 * n)},
        compiler_params=pltpu.CompilerParams(has_side_effects=_EFFECT),
    )(*[pltpu.with_memory_space_constraint(a, pltpu.HBM) for a in arrs])
    return res[:-1], res[-1]


def _exchange_wait(handles, after, *, scatter, name):
    n = (len(handles) - 2) // 2
    na = len(after)

    def body(*refs):
        src_refs, land_refs = refs[:n], refs[n:2 * n]
        send_sems, recv_sems = refs[2 * n], refs[2 * n + 1]
        for w in range(n):
            for k in range(1, NDEV):
                cp = _exchange_copy(src_refs, land_refs, send_sems, recv_sems, w, k, scatter, False)
                cp.wait_send()
                cp.wait_recv()

    arrs = handles[2:]
    res = pl.pallas_call(
        body, name=name, out_shape=tuple(pltpu.HBM(a.shape, a.dtype) for a in arrs),
        in_specs=[_HBM] * (2 * n) + [_SEMS, _SEMS] + [_HBM] * na,
        out_specs=tuple([_HBM] * (2 * n)),
        input_output_aliases={i: i for i in range(2 * n)},
        compiler_params=pltpu.CompilerParams(has_side_effects=_EFFECT),
    )(*arrs, handles[0], handles[1], *[pltpu.with_memory_space_constraint(a, pltpu.HBM) for a in after])
    return res[:n], res[n:]


_SAME_CORE = (2, 4, 6)


def _gather2_copy(src_ref, land_ref, send_sems, recv_sems, sem, k, block):
    x, y, c, _ = _place()
    dev, _ = _flip(x, y, c, k)
    dst = land_ref.at[block]
    return pltpu.make_async_remote_copy(
        src_ref=dst if src_ref is None else src_ref, dst_ref=dst,
        send_sem=send_sems.at[sem], recv_sem=recv_sems.at[sem], device_id=dev, device_id_type=MESH)


def _split_call(body, name, arrs, n_sems, sems=None, after=(), token=False):
    na = len(arrs)
    out_shape, out_specs = [], []
    if n_sems is not None:
        out_shape += [pltpu.SemaphoreType.DMA((n_sems,)), pltpu.SemaphoreType.DMA((n_sems,))]
        out_specs += [_SEMS, _SEMS]
    first = len(out_shape)
    out_shape += [pltpu.HBM(a.shape, a.dtype) for a in arrs]
    out_specs += [_HBM] * na
    if token:
        out_shape.append(jax.ShapeDtypeStruct((8, 128), F32))
        out_specs.append(pl.BlockSpec(memory_space=pltpu.VMEM))
    in_specs = [_HBM] * na + ([_SEMS, _SEMS] if sems is not None else []) + [_HBM] * len(after)
    args = [pltpu.with_memory_space_constraint(a, pltpu.HBM) for a in arrs] + list(sems or ()) \
        + [pltpu.with_memory_space_constraint(a, pltpu.HBM) for a in after]
    return pl.pallas_call(
        body, name=name, out_shape=tuple(out_shape), in_specs=in_specs, out_specs=tuple(out_specs),
        input_output_aliases={i: first + i for i in range(na)},
        compiler_params=pltpu.CompilerParams(has_side_effects=_EFFECT))(*args)


def _gather2_start(srcs, *, name):
    n = len(srcs)
    lands = [lax.empty((NDEV,) + tuple(s.shape), s.dtype) for s in srcs]

    def body(*refs):
        src_refs, land_refs = refs[:n], refs[n:2 * n]
        send_sems, recv_sems = refs[2 * n], refs[2 * n + 1]
        _, _, _, me = _place()
        for w in range(n):
            for slot, k in enumerate((1,) + _SAME_CORE):
                _gather2_copy(src_refs[w], land_refs[w], send_sems, recv_sems, 4 * w + slot, k, me).start()
        refs[-1][...] = jnp.zeros_like(refs[-1])

    res = _split_call(body, name, list(srcs) + lands, 4 * n, token=True)
    return res[:2], res[2:-1], res[-1]


def _gather2_arrived(sems, arrs, after, *, name):
    n = len(arrs) // 2

    def body(*refs):
        src_refs, land_refs = refs[:n], refs[n:2 * n]
        send_sems, recv_sems = refs[2 * n], refs[2 * n + 1]
        x, y, c, me = _place()
        for w in range(n):
            for slot, k in enumerate((1,) + _SAME_CORE):
                _, pidx = _flip(x, y, c, k)
                _gather2_copy(src_refs[w], land_refs[w], send_sems, recv_sems, 4 * w + slot, k, me).wait_send()
                _gather2_copy(None, land_refs[w], send_sems, recv_sems, 4 * w + slot, k, pidx).wait_recv()

    return _split_call(body, name, arrs, None, sems=sems, after=after)


def _gather2_forward(lands, *, name):
    n = len(lands)

    def body(*refs):
        land_refs = refs[:n]
        send_sems, recv_sems = refs[n], refs[n + 1]
        x, y, c, _ = _place()
        for w in range(n):
            for j, k in enumerate(_SAME_CORE):
                _, pidx = _flip(x, y, c, k)
                _gather2_copy(None, land_refs[w], send_sems, recv_sems, 3 * w + j, 1, pidx).start()
        refs[-1][...] = jnp.zeros_like(refs[-1])

    res = _split_call(body, name, list(lands), 3 * n, token=True)
    return res[:2], res[2:-1], res[-1]


def _gather2_wait(sems, lands, after, *, name):
    n = len(lands)

    def body(*refs):
        land_refs = refs[:n]
        send_sems, recv_sems = refs[n], refs[n + 1]
        x, y, c, _ = _place()
        for w in range(n):
            for j, k in enumerate(_SAME_CORE):
                _, mine = _flip(x, y, c, k)
                _, theirs = _flip(x, y, c, k ^ 1)
                _gather2_copy(None, land_refs[w], send_sems, recv_sems, 3 * w + j, 1, mine).wait_send()
                _gather2_copy(None, land_refs[w], send_sems, recv_sems, 3 * w + j, 1, theirs).wait_recv()

    return _split_call(body, name, list(lands), None, sems=sems, after=after)


def _cast_bf16(arrs, *, name, after=()):
    n = len(arrs)

    def body(*refs):
        for i_ref, o_ref in zip(refs[:n], refs[n + len(after):]):
            o_ref[...] = i_ref[...].astype(BF16)

    return pl.pallas_call(
        body, name=name, out_shape=tuple(jax.ShapeDtypeStruct(a.shape, BF16) for a in arrs),
        in_specs=[pl.BlockSpec(memory_space=pltpu.VMEM)] * n + [pl.BlockSpec(memory_space=pl.ANY)] * len(after),
        compiler_params=_cparams())(*arrs, *after)


def _rs_adam(me_arr, fulls, lands, w, m, v, *, name):
    rows = lands[0].shape[1]
    k = len(fulls)
    nb = next(n for n in (4, 2, 1) if rows % (16 * n) == 0)
    br = rows // nb

    def body(me_ref, *refs):
        own_refs, land_refs = refs[:k], refs[k:2 * k]
        w_ref, m_ref, v_ref, g_ref, d_ref, mo_ref, vo_ref = refs[2 * k:]
        me = me_ref[0]
        parts = []
        for own_ref, land_ref in zip(own_refs, land_refs):
            acc = jnp.zeros(own_ref.shape, F32)
            for p in range(NDEV):
                acc = acc + jnp.where(p == me, own_ref[...], land_ref[p].astype(F32))
            parts.append(acc)
        acc = parts[0] if k == 1 else jnp.concatenate(parts, axis=1)
        g_ref[...] = acc
        d_ref[...], mo_ref[...], vo_ref[...] = _adam_math(w_ref[...], acc, m_ref[...], v_ref[...])

    sh = jax.ShapeDtypeStruct((rows, D), F32)
    blk = pl.BlockSpec((br, D), lambda i, me: (i, 0))
    grid_spec = pltpu.PrefetchScalarGridSpec(
        num_scalar_prefetch=1, grid=(nb,),
        in_specs=[pl.BlockSpec((br, f.shape[1]), lambda i, me: (me[0] * nb + i, 0)) for f in fulls]
        + [pl.BlockSpec((NDEV, br, l.shape[2]), lambda i, me: (0, i, 0)) for l in lands]
        + [blk, blk, blk],
        out_specs=[blk] * 4)
    return pl.pallas_call(
        body, name=name, out_shape=(sh, sh, sh, sh), grid_spec=grid_spec, compiler_params=_cparams(1),
    )(me_arr, *fulls, *lands, w, m, v)


ROW_F, ROW_I, ROW_C, ROW_G1, ROW_LG = 0, 8, 16, 24, 32
PACK_ROWS = 40


def _small_sum(me_arr, pack, pack_land, sgw, sgw_land, wmod):
    ncol = wmod.shape[1]

    def body(me_ref, pack_ref, pland_ref, sgw_ref, sland_ref, wmod_ref, sum_ref, all_ref, sgw_sum_ref, part_ref):
        me = me_ref[0]
        acc = jnp.zeros((PACK_ROWS, D), F32)
        acc_w = jnp.zeros(sgw_ref.shape, F32)
        for p in range(NDEV):
            rows = jnp.where(p == me, pack_ref[...], pland_ref[p])
            all_ref[p] = rows
            acc = acc + rows
            acc_w = acc_w + jnp.where(p == me, sgw_ref[...], sland_ref[p])
        sum_ref[...] = acc
        sgw_sum_ref[...] = acc_w
        zero = jnp.zeros((1, D), F32)
        dcmod = jnp.concatenate([acc[ROW_C:ROW_C + 1], acc[ROW_C + 1:ROW_C + 2], zero, zero, zero, zero], axis=1)
        mine = jnp.zeros((1, ncol), F32)
        for d in range(NDEV):
            mine = mine + jnp.where(d == me, dcmod[:, d * ncol:(d + 1) * ncol], 0.0)
        part_ref[...] = _dot_nt(jnp.broadcast_to(mine, (8, ncol)).astype(BF16), wmod_ref[...].astype(BF16))

    vm = pl.BlockSpec(memory_space=pltpu.VMEM)
    return pl.pallas_call(
        body, name="small_sum",
        out_shape=(jax.ShapeDtypeStruct((PACK_ROWS, D), F32), jax.ShapeDtypeStruct((NDEV, PACK_ROWS, D), F32),
                   jax.ShapeDtypeStruct(sgw.shape, F32), jax.ShapeDtypeStruct((8, D), F32)),
        in_specs=[pl.BlockSpec(memory_space=pltpu.SMEM)] + [vm] * 5,
        compiler_params=_cparams(),
    )(me_arr, pack, pack_land, sgw, sgw_land, wmod)


def _cctx_final(me_arr, part, part_land, cctx_row, m_row, v_row):
    def body(me_ref, part_ref, land_ref, c_ref, m_ref, v_ref, g_ref, d_ref, mo_ref, vo_ref):
        me = me_ref[0]
        tot = jnp.zeros((8, D), F32)
        for p in range(NDEV):
            tot = tot + jnp.where(p == me, part_ref[...], land_ref[p])
        cc = c_ref[...]
        _, dsilu = _silu_parts(cc)
        g = tot[0:1, :] * dsilu
        g_ref[...] = g
        d_ref[...], mo_ref[...], vo_ref[...] = _adam_math(cc, g, m_ref[...], v_ref[...])

    row = jax.ShapeDtypeStruct((1, D), F32)
    vm = pl.BlockSpec(memory_space=pltpu.VMEM)
    return pl.pallas_call(
        body, name="cctx_final", out_shape=(row, row, row, row),
        in_specs=[pl.BlockSpec(memory_space=pltpu.SMEM)] + [vm] * 5,
        compiler_params=_cparams(),
    )(me_arr, part, part_land, cctx_row, m_row, v_row)


def _adam_small(s, sgw_g, params):
    names = ["norm1", "norm2", "norm_f", "sg_gain", "sg_b", "ret_logit_f", "ret_logit_b", "b_mod", "sg_w"]
    flat = [a for n in names for a in params[n]]

    def body(*refs):
        s_ref, sgw_ref = refs[0], refs[1]
        p_refs = refs[2:2 + 3 * len(names)]
        o_refs = refs[2 + 3 * len(names):]
        loss_ref = o_refs[-1]

        def row(r):
            return s_ref[r:r + 1, :]

        grads = {
            "norm1": row(ROW_I + 2) + row(ROW_C + 2),
            "norm2": row(ROW_F + 4),
            "norm_f": row(ROW_F + 0),
            "sg_gain": s_ref[ROW_I + 3:ROW_I + 4, 0:AW],
            "sg_b": s_ref[ROW_I + 4:ROW_I + 8, 0:DH],
            "sg_w": sgw_ref[...],
        }
        for j, n in enumerate(names):
            w_ref, m_ref, v_ref = p_refs[3 * j:3 * j + 3]
            g_ref, d_ref, mo_ref, vo_ref = o_refs[4 * j:4 * j + 4]
            w = w_ref[...]
            if n in ("ret_logit_f", "ret_logit_b"):
                r = ROW_LG + (0 if n == "ret_logit_f" else 1)
                g = s_ref[r:r + 1, 0:H] * _sigmoid(-w)
            elif n == "b_mod":
                rows = [row(ROW_I + 0) + row(ROW_C + 0), row(ROW_I + 1) + row(ROW_C + 1), row(ROW_G1),
                        row(ROW_F + 2), row(ROW_F + 3), row(ROW_F + 1)]
                g = jnp.concatenate(rows, axis=1)
            else:
                g = grads[n]
            g_ref[...] = g
            d_ref[...], mo_ref[...], vo_ref[...] = _adam_math(w, g, m_ref[...], v_ref[...])
        loss_ref[...] = jnp.full((1, 1), 0.5 / D, F32) * jnp.sum(row(ROW_F + 5), axis=1, keepdims=True)

    out_shape = []
    for n in names:
        w = params[n][0]
        out_shape += [jax.ShapeDtypeStruct(w.shape, F32)] * 4
    out_shape.append(jax.ShapeDtypeStruct((1, 1), F32))
    outs = pl.pallas_call(body, name="adam_small", out_shape=tuple(out_shape), compiler_params=_cparams())(
        s, sgw_g, *flat)
    return {n: tuple(outs[4 * j:4 * j + 4]) for j, n in enumerate(names)}, outs[-1]


def _wmod_grad(cs_all, dmod_cols, wmod, m, v):
    ncol = wmod.shape[1]

    def body(cs_ref, dm_ref, w_ref, m_ref, v_ref, g_ref, d_ref, mo_ref, vo_ref):
        g = _dot_tn(cs_ref[...].astype(BF16), dm_ref[...].astype(BF16))
        g_ref[...] = g
        d_ref[...], mo_ref[...], vo_ref[...] = _adam_math(w_ref[...], g, m_ref[...], v_ref[...])

    sh = jax.ShapeDtypeStruct((D, ncol), F32)
    br = D // 4
    blk = pl.BlockSpec((br, ncol), lambda i: (i, 0))
    return pl.pallas_call(
        body, name="wmod_grad", out_shape=(sh, sh, sh, sh), grid=(D // br,),
        in_specs=[pl.BlockSpec((cs_all.shape[0], br), lambda i: (0, i)), _whole(dmod_cols.shape), blk, blk, blk],
        out_specs=[blk] * 4,
        compiler_params=_cparams(1),
    )(cs_all, dmod_cols, wmod, m, v)


def _rope_tables(t_len):
    n_freq = DH // 4
    inv = (ROPE_BASE ** (-np.arange(n_freq, dtype=np.float32) / n_freq)).astype(np.float32)
    tok = np.arange(t_len)
    rows = (tok // GRID_W).astype(np.float32)
    cols = (tok % GRID_W).astype(np.float32)
    ang_r = rows[:, None] * inv[None, :]
    ang_c = cols[:, None] * inv[None, :]
    cos = np.concatenate([np.cos(ang_r)] * 2 + [np.cos(ang_c)] * 2, axis=1).astype(np.float32)
    sin = np.concatenate([-np.sin(ang_r), np.sin(ang_r), -np.sin(ang_c), np.sin(ang_c)], axis=1).astype(np.float32)
    return jnp.asarray(cos), jnp.asarray(sin)


def _local_step(x, tgt, ctx, mod6, cmod6, norm1, norm2, normf, win, wout, ffn_weights, sgw, sgb, sggain, rlf, rlb,
                *, tm_a, tm_b, tm_f, tm_m, tm_r, tm_i, bt_w, after_first_grads=None, small_path=None,
                after_in_half=None):
    t_len = x.shape[0]
    cos, sin = _rope_tables(t_len)
    sgbt = jnp.broadcast_to(sgb[:, :, None], (G, C, 128))
    rlf = jnp.broadcast_to(rlf.reshape(H, 1), (H, 128))
    rlb = jnp.broadcast_to(rlb.reshape(H, 1), (H, 128))
    dmf, dmb = _decay_masks(rlf, rlb)
    hc, kvc, scf, scb = _ctx_fwd(ctx, cmod6, norm1, win, rlf, rlb)
    hx, zuv, q, k, v, gfb, of, ya, sst = _fwd_a(x, mod6, norm1, win, sgw, sgbt, sggain, cos, sin, rlf, scf, dmf,
                                                tm=tm_a)
    if callable(wout):
        wout, tok = wout(hx)
        rlb = rlb + tok
    ob, ycat, y, x1, rst = _fwd_b(q, k, v, of, gfb, ya, x, mod6, wout, rlb, scb, dmb, tm=tm_b)
    wg, wu, wd = ffn_weights(x1) if callable(ffn_weights) else ffn_weights
    dx1, h2, da, db, hm, df, small_f = _ffn(x1, tgt, mod6, norm2, normf, wg, wu, wd, tm=tm_f)
    bt2 = min(2 * bt_w, t_len)
    d_wd, d_wd_b = _tn_matmul(hm, df, bm=DFF // 2, bt=bt2, name="dw_down")
    d_wg, d_wg_b = _tn_matmul(da, h2, bm=DFF // 2, bt=bt2, name="dw_gate")
    d_wu, d_wu_b = _tn_matmul(db, h2, bm=DFF // 2, bt=bt2, name="dw_up")
    dy, dya, dgfb, dof, dob, dg1 = _mid_bwd(dx1, y, of, ob, gfb, mod6, wout, tm=tm_m)
    d_wo, d_wo_b = _tn_matmul(ycat, dy, bm=D, bt=bt2, name="dw_out")
    if after_first_grads is not None:
        rlf = rlf + after_first_grads((d_wd_b, d_wg_b, d_wu_b, d_wo_b))
    dqf, dkf, dvf, dqb, dkb, dvb, dscf, dscb, dlg = _ret_bwd(q, k, v, dof, dob, sst, rst, rlf, rlb, dmf, dmb,
                                                             tm=tm_r)
    gx, dz, small_i, dsgw = _in_bwd(x, zuv, dya, dqf, dkf, dvf, dqb, dkb, dvb, dgfb, dx1, cos, sin,
                                    mod6, norm1, win, sgw, sgbt, sggain, tm=tm_i)
    dwin_c, small_c, dlg = _ctx_bwd(ctx, hc, kvc, cmod6, norm1, win, rlf, rlb, dscf, dscb, dlg)
    pack = jnp.concatenate([small_f, small_i, small_c, dg1, dlg], axis=0)
    after = small_path(pack, dsgw) if small_path is not None else ()
    halves = []
    for j in range(2):
        halves.append(_tn_matmul(dz, hx, bm=INC // 2, bt=bt2, name="dw_in_%d" % j, init=dwin_c,
                                 init_rows=(KV_LO, KV_HI), after=after, cols=(j, D // 2)))
        if after_in_half is not None:
            after = after_in_half(j, halves[-1][1])
    grads = {"w_in": halves, "w_out": (d_wo, d_wo_b), "w_gate": (d_wg, d_wg_b), "w_up": (d_wu, d_wu_b),
             "w_down": (d_wd, d_wd_b)}
    return gx, grads, pack, dsgw


def kernel(x, c, ctx, c_ctx, w_mod, b_mod, norm1, w_in, sg_gain, sg_w, sg_b, ret_logit_f, ret_logit_b, w_out, norm2, w_gate, w_up, w_down, norm_f, loss_target, m_c_ctx, m_w_mod, m_b_mod, m_norm1, m_w_in, m_sg_gain, m_sg_w, m_sg_b, m_ret_logit_f, m_ret_logit_b, m_w_out, m_norm2, m_w_gate, m_w_up, m_w_down, m_norm_f, v_c_ctx, v_w_mod, v_b_mod, v_norm1, v_w_in, v_sg_gain, v_sg_w, v_sg_b, v_ret_logit_f, v_ret_logit_b, v_w_out, v_norm2, v_w_gate, v_w_up, v_w_down, v_norm_f):
    t_len = x.shape[1]
    tm = min(512, t_len)
    me = 4 * lax.axis_index("x") + 2 * lax.axis_index("y") + lax.axis_index("c")
    tr = lambda a: jnp.transpose(a[0])

    cctx_row = c_ctx.reshape(1, D)
    mod_all, cs_all, g_in = _entry_gather(c, cctx_row, w_mod[0], b_mod, [tr(w_in)])
    mod6 = lax.dynamic_slice(mod_all, (me, 0), (1, 6 * D)).reshape(6, D)
    cmod6 = mod_all[8].reshape(6, D)
    win_t = g_in.reshape(INC, D)

    (out_shard,) = _cast_bf16([w_out[0]], name="cast_out", after=[g_in])
    h_out, tok_out = _exchange_start([out_shard], scatter=False, name="wout_start")
    ffn_shards = _cast_bf16([tr(w_gate), tr(w_up), w_down[0]], name="cast_ffn", after=[h_out[2]])
    sems_ffn, arrs_ffn, tok = _gather2_start(ffn_shards, name="wffn_start")
    mod6 = mod6 + (tok_out[0, 0] + tok[0, 0])
    ffn = {}

    def place_own(own, lands, rows):
        return [lax.dynamic_update_slice(l, o[None], (me, 0, 0)).reshape(rows, D) for o, l in zip(own, lands)]

    def wout(after):
        own, lands = _exchange_wait(h_out, [after], scatter=False, name="wout_wait")
        arrs = _gather2_arrived(sems_ffn, arrs_ffn, [after], name="wffn_arrived")
        ffn["own"] = arrs[:3]
        ffn["sems"], ffn["lands"], tok_fwd = _gather2_forward(arrs[3:], name="wffn_forward")
        return place_own(own, lands, D)[0], tok_fwd[0, 0]

    def ffn_weights(after):
        lands = _gather2_wait(ffn["sems"], ffn["lands"], [after], name="wffn_wait")
        return place_own(ffn["own"], lands, DFF)

    rs, outs = {}, {}

    def after_first_grads(bf):
        rs["first"], tok_first = _exchange_start([b.reshape(NDEV, b.shape[0] // NDEV, D) for b in bf], scatter=True,
                                                 name="rs_first_start")
        return tok_first[0, 0]

    def small_path(pack, dsgw):
        rs["small"], _ = _exchange_start([pack, dsgw.reshape(G * C, C)], scatter=False, name="small_start")
        return [rs["small"][2], rs["small"][3]]

    def after_in_half(j, half_bf16):
        rs["in%d" % j], _ = _exchange_start([half_bf16.reshape(NDEV, INC // NDEV, D // 2)], scatter=True,
                                            name="rs_in%d_start" % j)
        return [rs["in%d" % j][2]]

    gx, grads, pack, dsgw = _local_step(
        x[0], loss_target[0], ctx[0], mod6, cmod6, norm1, norm2[0:1], norm_f.reshape(1, D), win_t, wout, ffn_weights,
        sg_w[0], sg_b[0], sg_gain, ret_logit_f, ret_logit_b,
        tm_a=tm, tm_b=tm, tm_f=min(256, t_len), tm_m=min(1024, t_len), tm_r=min(1024, t_len), tm_i=tm,
        bt_w=min(1024, t_len),
        after_first_grads=after_first_grads, small_path=small_path, after_in_half=after_in_half)

    me_arr = me.reshape(1).astype(jnp.int32)
    (pack_own, sgw_own), (pack_land, sgw_land) = _exchange_wait(rs["small"], [rs["in1"][2]], scatter=False,
                                                               name="small_wait")
    psum_rows, pall, sgw_sum, part = _small_sum(me_arr, pack_own, pack_land, sgw_own, sgw_land, w_mod[0])
    h_cc, _ = _exchange_start([part], scatter=False, name="cctx_start")

    dmod_rows = (ROW_I + 0, ROW_I + 1, ROW_G1, ROW_F + 2, ROW_F + 3, ROW_F + 1)
    dmod_all = jnp.concatenate([pall[:, r, :] for r in dmod_rows], axis=1)
    dcmod = jnp.concatenate([psum_rows[ROW_C + 0:ROW_C + 1], psum_rows[ROW_C + 1:ROW_C + 2],
                             jnp.zeros((1, 4 * D), F32)], axis=1)
    dmod_mat = jnp.concatenate([dmod_all, jnp.pad(dcmod, ((0, 7), (0, 0)))], axis=0)
    ncol = 6 * D // NDEV
    dmod_cols = lax.dynamic_slice(dmod_mat, (0, me * ncol), (16, ncol))
    g_wm, d_wm, m_wm, v_wm = _wmod_grad(cs_all, dmod_cols, w_mod[0], m_w_mod[0], v_w_mod[0])
    outs["w_mod"] = (g_wm[None], d_wm[None], m_wm[None], v_wm[None])
    small_params = {
        "norm1": (norm1, m_norm1, v_norm1), "norm2": (norm2, m_norm2, v_norm2),
        "norm_f": tuple(a.reshape(1, D) for a in (norm_f, m_norm_f, v_norm_f)),
        "sg_gain": (sg_gain, m_sg_gain, v_sg_gain), "sg_b": (sg_b[0], m_sg_b[0], v_sg_b[0]),
        "ret_logit_f": (ret_logit_f, m_ret_logit_f, v_ret_logit_f),
        "ret_logit_b": (ret_logit_b, m_ret_logit_b, v_ret_logit_b),
        "b_mod": (b_mod, m_b_mod, v_b_mod), "sg_w": (sg_w[0], m_sg_w[0], v_sg_w[0])}
    small, loss = _adam_small(psum_rows, sgw_sum.reshape(G, C, C), small_params)
    back = {"norm_f": lambda a: a.reshape(D), "sg_b": lambda a: a[None], "sg_w": lambda a: a[None]}
    for name, vals in small.items():
        outs[name] = tuple(back.get(name, lambda a: a)(a) for a in vals)

    _, lands_first = _exchange_wait(rs["first"], [g_wm], scatter=True, name="rs_first_wait")

    def finish(name, fulls, lands, w, m, v, transposed):
        take = tr if transposed else (lambda a: a[0])
        res = _rs_adam(me_arr, fulls, lands, take(w), take(m), take(v), name="adam_" + name)
        put = (lambda a: jnp.transpose(a)[None]) if transposed else (lambda a: a[None])
        outs[name] = tuple(put(a) for a in res)
        return res[0]

    g_down = finish("w_down", [grads["w_down"][0]], [lands_first[0]], w_down, m_w_down, v_w_down, False)
    g_gate = finish("w_gate", [grads["w_gate"][0]], [lands_first[1]], w_gate, m_w_gate, v_w_gate, True)
    g_up = finish("w_up", [grads["w_up"][0]], [lands_first[2]], w_up, m_w_up, v_w_up, True)
    g_out = finish("w_out", [grads["w_out"][0]], [lands_first[3]], w_out, m_w_out, v_w_out, False)
    done = [g_down, g_gate, g_up, g_out]
    (part_own,), (part_land,) = _exchange_wait(h_cc, done, scatter=False, name="cctx_wait")
    res_cc = _cctx_final(me_arr, part_own, part_land, cctx_row, m_c_ctx.reshape(1, D), v_c_ctx.reshape(1, D))
    outs["c_ctx"] = tuple(a.reshape(D) for a in res_cc)
    lands_in = [_exchange_wait(rs["in%d" % j], done, scatter=True, name="rs_in%d_wait" % j)[1][0] for j in range(2)]
    finish("w_in", [h[0] for h in grads["w_in"]], lands_in, w_in, m_w_in, v_w_in, True)

    order = ["c_ctx", "w_mod", "b_mod", "norm1", "w_in", "sg_gain", "sg_w", "sg_b", "ret_logit_f", "ret_logit_b",
             "w_out", "norm2", "w_gate", "w_up", "w_down", "norm_f"]
    res = [loss.reshape(()), gx[None]]
    for j in range(4):
        res += [outs[name][j] for name in order]
    return tuple(res)
```

```python
import functools
import math

import numpy as np
import jax
import jax.numpy as jnp
from jax import lax
from jax.experimental import pallas as pl
from jax.experimental.pallas import tpu as pltpu

F32 = jnp.float32
BF16 = jnp.bfloat16

D = 1024
AW = 512
RW = 512
H = 4
DH = 128
G = 4
C = 128
CR = 256
DFF = 2816
INC = 3584
Q_LO = 2 * AW
KV_LO, VR_LO, G_LO = Q_LO + RW, Q_LO + 2 * RW, Q_LO + 3 * RW
KV_HI = G_LO
NDEV = 8
EPS = 1e-6
KSCALE = DH ** -0.5
ROPE_BASE = 10000.0
GRID_W = 64

ADAM_LR = 0.001
ADAM_B1 = 0.9
ADAM_B2 = 0.999
ADAM_EPS = 1e-08
ADAM_WD = 0.01
ADAM_STEP = 10

VMEM_LIMIT = 56 * 1024 * 1024
MESH = pl.DeviceIdType.MESH


def _cparams(n_grid=0, **kw):
    sem = ("arbitrary",) * n_grid if n_grid else None
    return pltpu.CompilerParams(dimension_semantics=sem, vmem_limit_bytes=VMEM_LIMIT, **kw)


def _dot(a, b):
    return jnp.dot(a, b, preferred_element_type=F32)


def _dot_nt(a, b):
    return lax.dot_general(a, b, (((1,), (1,)), ((), ())), preferred_element_type=F32)


def _dot_tn(a, b):
    return lax.dot_general(a, b, (((0,), (0,)), ((), ())), preferred_element_type=F32)


def _whole(shape):
    nd = len(shape)
    return pl.BlockSpec(shape, lambda *_: (0,) * nd, pipeline_mode=pl.Buffered(1))


def _acc(shape):
    nd = len(shape)
    return pl.BlockSpec(shape, lambda *_: (0,) * nd)


def _rows(tm, n):
    return pl.BlockSpec((tm, n), lambda i: (i, 0))


def _rows_rev(tm, n, nt):
    return pl.BlockSpec((tm, n), lambda i: (nt - 1 - i, 0))


def _sigmoid(x):
    return 1.0 / (1.0 + jnp.exp(-x))


def _log_sigmoid(x):
    return jnp.minimum(x, 0.0) - jnp.log(1.0 + jnp.exp(-jnp.abs(x)))


_GK0 = math.sqrt(2.0 / math.pi)
_GK1 = 0.044715


def _gelu(x):
    x2 = x * x
    t = jnp.tanh(x * (_GK0 + (_GK0 * _GK1) * x2))
    h = 0.5 + 0.5 * t
    g = x * h
    dg = h + g * (1.0 - h) * ((2.0 * _GK0) + (6.0 * _GK0 * _GK1) * x2)
    return g, dg


def _silu_parts(a):
    s = _sigmoid(a)
    sa = a * s
    return sa, s + sa * (1.0 - s)


def _rope(t, cos, sins):
    n = t.shape[1]
    lane = lax.broadcasted_iota(jnp.int32, t.shape, 1)
    first = (lane & 63) < 32
    partner = jnp.where(first, pltpu.roll(t, n - 32, 1), pltpu.roll(t, 32, 1))
    return t * cos + partner * sins


def _headnorm(o):
    outs, rs = [], []
    for h in range(H):
        oh = o[:, h * DH:(h + 1) * DH]
        r = lax.rsqrt(jnp.mean(oh * oh, axis=-1, keepdims=True) + EPS)
        outs.append(oh * r)
        rs.append(r)
    return jnp.concatenate(outs, axis=1), rs


def _decay_consts(lg, forward):
    ii = lax.broadcasted_iota(jnp.int32, (CR, CR), 0).astype(F32)
    jj = lax.broadcasted_iota(jnp.int32, (CR, CR), 1).astype(F32)
    ic = lax.broadcasted_iota(jnp.int32, (CR, 1), 0).astype(F32)
    if forward:
        diff = ii - jj
        xi = jnp.exp(lg * (ic + 1.0))
        z = jnp.exp(lg * (CR - 1.0 - ic))
    else:
        diff = jj - ii
        xi = jnp.exp(lg * (CR - ic))
        z = jnp.exp(lg * ic)
    dm = jnp.where(diff >= 0.0, jnp.exp(lg * jnp.maximum(diff, 0.0)), 0.0)
    dec = jnp.exp(lg * float(CR))
    return dm, xi, z, dec, ic


def _ctx_fwd(ctx, cmod6, norm1, win, rlf, rlb):
    lc = ctx.shape[0]

    def body(ctx_ref, cmod_ref, n1_ref, win_ref, rlf_ref, rlb_ref, hc_ref, kvc_ref, scf_ref, scb_ref):
        x = ctx_ref[...]
        r = lax.rsqrt(jnp.mean(x * x, axis=-1, keepdims=True) + EPS)
        hc = (x * r) * (n1_ref[...] * (1.0 + cmod_ref[1:2, :])) + cmod_ref[0:1, :]
        hcb = hc.astype(BF16)
        hc_ref[...] = hcb
        kv = _dot_nt(hcb, win_ref[KV_LO:KV_HI, :])
        k = kv[:, :RW] * KSCALE
        v = kv[:, RW:]
        kvc_ref[:, :RW] = k
        kvc_ref[:, RW:] = v
        t = lax.broadcasted_iota(jnp.int32, (lc, 1), 0).astype(F32)
        lgf = _log_sigmoid(rlf_ref[...])
        lgb = _log_sigmoid(rlb_ref[...])
        for h in range(H):
            hs = slice(h * DH, (h + 1) * DH)
            wf = jnp.exp(lgf[h:h + 1, 0:1] * (lc - 1.0 - t))
            wb = jnp.exp(lgb[h:h + 1, 0:1] * t)
            vh = v[:, hs].astype(BF16)
            scf_ref[h] = _dot_tn((k[:, hs] * wf).astype(BF16), vh)
            scb_ref[h] = _dot_tn((k[:, hs] * wb).astype(BF16), vh)

    return pl.pallas_call(
        body, name="ctx_fwd",
        out_shape=(jax.ShapeDtypeStruct((lc, D), BF16), jax.ShapeDtypeStruct((lc, 2 * RW), F32),
                   jax.ShapeDtypeStruct((H, DH, DH), F32), jax.ShapeDtypeStruct((H, DH, DH), F32)),
        compiler_params=_cparams(),
    )(ctx, cmod6, norm1, win, rlf, rlb)


def _sg_forward(u, v, sgw_ref, sgbt_ref, sgg_ref, nc):
    ua, dgu = _gelu(u)
    va, dgv = _gelu(v)
    gain = sgg_ref[...]
    mixed, vn_b, vah_l, rv_l = [], [], [], []
    for g in range(G):
        gs = slice(g * DH, (g + 1) * DH)
        vag = va[:, gs]
        rv = lax.rsqrt(jnp.mean(vag * vag, axis=-1, keepdims=True) + EPS)
        vah = vag * rv
        vn = (vah * gain[:, gs]).astype(BF16)
        wg = sgw_ref[g].astype(BF16)
        bcol = sgbt_ref[g]
        rows = [_dot(wg, vn[c * C:(c + 1) * C, :]) + bcol for c in range(nc)]
        mixed.append(jnp.concatenate(rows, axis=0) if nc > 1 else rows[0])
        vn_b.append(vn)
        vah_l.append(vah)
        rv_l.append(rv)
    mixed = jnp.concatenate(mixed, axis=1)
    return ua * mixed, (ua, dgu, dgv, mixed, vn_b, vah_l, rv_l)


def _fwd_a(x, mod6, norm1, win, sgw, sgbt, sggain, cos, sin, rlf, scf, *, tm):
    t_len = x.shape[0]
    nt, nc, ncr = t_len // tm, tm // C, tm // CR

    def body(x_ref, mod_ref, n1_ref, win_ref, sgw_ref, sgbt_ref, sgg_ref, cos_ref, sin_ref, rlf_ref, scf_ref,
             hx_ref, zuv_ref, q_ref, k_ref, v_ref, gfb_ref, of_ref, ya_ref, sst_ref, s_scr):
        i = pl.program_id(0)

        @pl.when(i == 0)
        def _():
            s_scr[...] = scf_ref[...]

        x = x_ref[...]
        r = lax.rsqrt(jnp.mean(x * x, axis=-1, keepdims=True) + EPS)
        hx = (x * r) * (n1_ref[...] * (1.0 + mod_ref[1:2, :])) + mod_ref[0:1, :]
        hxb = hx.astype(BF16)
        hx_ref[...] = hxb
        z = _dot_nt(hxb, win_ref[...])
        zuv_ref[...] = z[:, :2 * AW].astype(BF16)
        gfb_ref[...] = z[:, G_LO:INC].astype(BF16)
        cos = jnp.tile(cos_ref[...], (1, H))
        sins = jnp.tile(sin_ref[...], (1, H))
        q_ref[...] = _rope(z[:, Q_LO:KV_LO], cos, sins).astype(BF16)
        k_ref[...] = (_rope(z[:, KV_LO:VR_LO], cos, sins) * KSCALE).astype(BF16)
        v_ref[...] = z[:, VR_LO:G_LO].astype(BF16)
        ya, _ = _sg_forward(z[:, :AW], z[:, AW:2 * AW], sgw_ref, sgbt_ref, sgg_ref, nc)
        ya_ref[...] = ya.astype(BF16)

        lg = _log_sigmoid(rlf_ref[...])
        for h in range(H):
            dm, xi, zc, dec, _ = _decay_consts(lg[h:h + 1, 0:1], True)
            hs = slice(h * DH, (h + 1) * DH)
            for c in range(ncr):
                rs = slice(c * CR, (c + 1) * CR)
                qc, kc, vc = q_ref[rs, hs], k_ref[rs, hs], v_ref[rs, hs]
                s = s_scr[h]
                sst_ref[c, h] = s
                a = (_dot_nt(qc, kc) * dm).astype(BF16)
                of_ref[rs, hs] = (_dot(a, vc) + xi * _dot(qc, s.astype(BF16))).astype(BF16)
                kz = (kc.astype(F32) * zc).astype(BF16)
                s_scr[h] = dec * s + _dot_tn(kz, vc)

    n_chunks = t_len // CR
    return pl.pallas_call(
        body, name="fwd_a", grid=(nt,),
        in_specs=[_rows(tm, D), _whole((6, D)), _whole((1, D)), _whole((INC, D)), _whole((G, C, C)),
                  _whole((G, C, 128)), _whole((1, AW)), _rows(tm, DH), _rows(tm, DH), _whole((H, 128)),
                  _whole((H, DH, DH))],
        out_specs=[_rows(tm, D), _rows(tm, 2 * AW), _rows(tm, RW), _rows(tm, RW), _rows(tm, RW),
                   _rows(tm, 2 * RW), _rows(tm, RW), _rows(tm, AW),
                   pl.BlockSpec((ncr, H, DH, DH), lambda i: (i, 0, 0, 0))],
        out_shape=(jax.ShapeDtypeStruct((t_len, D), BF16), jax.ShapeDtypeStruct((t_len, 2 * AW), BF16),
                   jax.ShapeDtypeStruct((t_len, RW), BF16), jax.ShapeDtypeStruct((t_len, RW), BF16),
                   jax.ShapeDtypeStruct((t_len, RW), BF16), jax.ShapeDtypeStruct((t_len, 2 * RW), BF16),
                   jax.ShapeDtypeStruct((t_len, RW), BF16), jax.ShapeDtypeStruct((t_len, AW), BF16),
                   jax.ShapeDtypeStruct((n_chunks, H, DH, DH), F32)),
        scratch_shapes=[pltpu.VMEM((H, DH, DH), F32)],
        compiler_params=_cparams(1),
    )(x, mod6, norm1, win, sgw, sgbt, sggain, cos, sin, rlf, scf)


def _fwd_b(q, k, v, of, gfb, ya, x, mod6, wout, rlb, scb, *, tm):
    t_len = x.shape[0]
    nt, nc = t_len // tm, tm // CR

    def body(q_ref, k_ref, v_ref, of_ref, gfb_ref, ya_ref, x_ref, mod_ref, wout_ref, rlb_ref, scb_ref,
             ob_ref, ycat_ref, x1_ref, rst_ref, r_scr):
        i = pl.program_id(0)

        @pl.when(i == 0)
        def _():
            r_scr[...] = scb_ref[...]

        lg = _log_sigmoid(rlb_ref[...])
        consts = [_decay_consts(lg[h:h + 1, 0:1], False) for h in range(H)]

        def first(c, h):
            dm, _, zc, dec, _ = consts[h]
            hs, rs = slice(h * DH, (h + 1) * DH), slice(c * CR, (c + 1) * CR)
            qc, kc, vc = q_ref[rs, hs], k_ref[rs, hs], v_ref[rs, hs]
            s = r_scr[h]
            rst_ref[c, h] = s
            scores = _dot_nt(qc, kc)
            cross = _dot(qc, s.astype(BF16))
            kz = (kc.astype(F32) * zc).astype(BF16)
            r_scr[h] = dec * s + _dot_tn(kz, vc)
            return (scores * dm).astype(BF16), cross

        def second(c, h, carried):
            a, cross = carried
            hs, rs = slice(h * DH, (h + 1) * DH), slice(c * CR, (c + 1) * CR)
            ob_ref[rs, hs] = (_dot(a, v_ref[rs, hs]) + consts[h][1] * cross).astype(BF16)

        def rows_out(c):
            rs = slice(c * CR, (c + 1) * CR)
            gfb = gfb_ref[rs, :].astype(F32)
            sf, _ = _silu_parts(gfb[:, :RW])
            sb, _ = _silu_parts(gfb[:, RW:])
            hnf, _ = _headnorm(of_ref[rs, :].astype(F32))
            hnb, _ = _headnorm(ob_ref[rs, :].astype(F32))
            yr = sf * hnf + sb * hnb
            ycat = jnp.concatenate([ya_ref[rs, :], yr.astype(BF16)], axis=1)
            ycat_ref[rs, :] = ycat
            x1_ref[rs, :] = x_ref[rs, :] + mod_ref[2:3, :] * _dot(ycat, wout_ref[...])

        pending, waiting_rows = None, None
        for c in reversed(range(nc)):
            for h in range(H):
                carried = first(c, h)
                if pending is not None:
                    second(*pending)
                pending = (c, h, carried)
            if waiting_rows is not None:
                rows_out(waiting_rows)
            waiting_rows = c
        second(*pending)
        rows_out(waiting_rows)

    n_chunks = t_len // CR
    rr = functools.partial(_rows_rev, nt=nt)
    return pl.pallas_call(
        body, name="fwd_b", grid=(nt,),
        in_specs=[rr(tm, RW), rr(tm, RW), rr(tm, RW), rr(tm, RW), rr(tm, 2 * RW), rr(tm, AW), rr(tm, D),
                  _whole((6, D)), _whole((D, D)), _whole((H, 128)), _whole((H, DH, DH))],
        out_specs=[rr(tm, RW), rr(tm, D), rr(tm, D),
                   pl.BlockSpec((nc, H, DH, DH), lambda i: (nt - 1 - i, 0, 0, 0))],
        out_shape=(jax.ShapeDtypeStruct((t_len, RW), BF16), jax.ShapeDtypeStruct((t_len, D), BF16),
                   jax.ShapeDtypeStruct((t_len, D), F32),
                   jax.ShapeDtypeStruct((n_chunks, H, DH, DH), F32)),
        scratch_shapes=[pltpu.VMEM((H, DH, DH), F32)],
        compiler_params=_cparams(1),
    )(q, k, v, of, gfb, ya, x, mod6, wout, rlb, scb)


def _ffn(x1, tgt, mod6, norm2, normf, wg, wu, wd, *, tm):
    t_len = x1.shape[0]
    nt = t_len // tm

    def body(x1_ref, tgt_ref, mod_ref, n2_ref, nf_ref, wg_ref, wu_ref, wd_ref,
             dx1_ref, h2_ref, da_ref, db_ref, hm_ref, df_ref, small_ref):
        i = pl.program_id(0)

        @pl.when(i == 0)
        def _():
            small_ref[...] = jnp.zeros_like(small_ref)

        sh2, sc2, g2 = mod_ref[3:4, :], mod_ref[4:5, :], mod_ref[5:6, :]
        n2, nf = n2_ref[...], nf_ref[...]
        x1 = x1_ref[...]
        r2 = lax.rsqrt(jnp.mean(x1 * x1, axis=-1, keepdims=True) + EPS)
        x1h = x1 * r2
        w2 = n2 * (1.0 + sc2)
        h2b = (x1h * w2 + sh2).astype(BF16)
        h2_ref[...] = h2b
        a = _dot_nt(h2b, wg_ref[...])
        b = _dot_nt(h2b, wu_ref[...])
        sa, dsa = _silu_parts(a)
        hmb = (sa * b).astype(BF16)
        hm_ref[...] = hmb
        f = _dot(hmb, wd_ref[...])
        x2 = x1 + g2 * f
        r3 = lax.rsqrt(jnp.mean(x2 * x2, axis=-1, keepdims=True) + EPS)
        x2h = x2 * r3
        diff = x2h * nf - tgt_ref[...]
        small_ref[5:6, :] += jnp.sum(diff * diff, axis=0, keepdims=True)
        dout = diff * (1.0 / D)
        small_ref[0:1, :] += jnp.sum(dout * x2h, axis=0, keepdims=True)
        dyg = dout * nf
        dx2 = r3 * (dyg - x2h * jnp.mean(dyg * x2h, axis=-1, keepdims=True))
        small_ref[1:2, :] += jnp.sum(dx2 * f, axis=0, keepdims=True)
        dfb = (dx2 * g2).astype(BF16)
        df_ref[...] = dfb
        dhm = _dot_nt(dfb, wd_ref[...])
        dbb = (dhm * sa).astype(BF16)
        dab = (dhm * b * dsa).astype(BF16)
        da_ref[...] = dab
        db_ref[...] = dbb
        dh2 = _dot(dab, wg_ref[...]) + _dot(dbb, wu_ref[...])
        small_ref[2:3, :] += jnp.sum(dh2, axis=0, keepdims=True)
        dhx = dh2 * x1h
        small_ref[3:4, :] += jnp.sum(dhx, axis=0, keepdims=True) * n2
        small_ref[4:5, :] += jnp.sum(dhx, axis=0, keepdims=True) * (1.0 + sc2)
        dyg2 = dh2 * w2
        dx1_ref[...] = dx2 + r2 * (dyg2 - x1h * jnp.mean(dyg2 * x1h, axis=-1, keepdims=True))

    return pl.pallas_call(
        body, name="ffn", grid=(nt,),
        in_specs=[_rows(tm, D), _rows(tm, D), _whole((6, D)), _whole((1, D)), _whole((1, D)),
                  _whole((DFF, D)), _whole((DFF, D)), _whole((DFF, D))],
        out_specs=[_rows(tm, D), _rows(tm, D), _rows(tm, DFF), _rows(tm, DFF), _rows(tm, DFF), _rows(tm, D),
                   _acc((8, D))],
        out_shape=(jax.ShapeDtypeStruct((t_len, D), F32), jax.ShapeDtypeStruct((t_len, D), BF16),
                   jax.ShapeDtypeStruct((t_len, DFF), BF16), jax.ShapeDtypeStruct((t_len, DFF), BF16),
                   jax.ShapeDtypeStruct((t_len, DFF), BF16), jax.ShapeDtypeStruct((t_len, D), BF16),
                   jax.ShapeDtypeStruct((8, D), F32)),
        compiler_params=_cparams(1),
    )(x1, tgt, mod6, norm2, normf, wg, wu, wd)


def _mid_bwd(dx1, of, ob, gfb, mod6, wout, *, tm):
    t_len = dx1.shape[0]
    nt = t_len // tm
    rc = min(256, tm)

    def body(dx1_ref, of_ref, ob_ref, gfb_ref, mod_ref, wout_ref,
             dxb_ref, dya_ref, dgfb_ref, dof_ref, dob_ref):
        for c in range(tm // rc):
            rows = slice(c * rc, (c + 1) * rc)
            dx1 = dx1_ref[rows, :]
            dxb_ref[rows, :] = dx1.astype(BF16)
            dyb = (dx1 * mod_ref[2:3, :]).astype(BF16)
            dycat = _dot_nt(dyb, wout_ref[...])
            dya_ref[rows, :] = dycat[:, :AW].astype(BF16)
            dyr = dycat[:, AW:]
            gfb = gfb_ref[rows, :].astype(F32)
            for o_ref, do_ref, lo in ((of_ref, dof_ref, 0), (ob_ref, dob_ref, RW)):
                sg, dsg = _silu_parts(gfb[:, lo:lo + RW])
                o = o_ref[rows, :].astype(F32)
                hn, rs = _headnorm(o)
                dgfb_ref[rows, lo:lo + RW] = (dyr * hn * dsg).astype(BF16)
                dhn = dyr * sg
                for h in range(H):
                    hs = slice(h * DH, (h + 1) * DH)
                    oh, dh = hn[:, hs], dhn[:, hs]
                    do_ref[rows, hs] = (rs[h] * (dh - oh * jnp.mean(dh * oh, axis=-1, keepdims=True))).astype(BF16)

    return pl.pallas_call(
        body, name="mid_bwd", grid=(nt,),
        in_specs=[_rows(tm, D), _rows(tm, RW), _rows(tm, RW), _rows(tm, 2 * RW), _whole((6, D)), _whole((D, D))],
        out_specs=[_rows(tm, D), _rows(tm, AW), _rows(tm, 2 * RW), _rows(tm, RW), _rows(tm, RW)],
        out_shape=(jax.ShapeDtypeStruct((t_len, D), BF16), jax.ShapeDtypeStruct((t_len, AW), BF16),
                   jax.ShapeDtypeStruct((t_len, 2 * RW), BF16), jax.ShapeDtypeStruct((t_len, RW), BF16),
                   jax.ShapeDtypeStruct((t_len, RW), BF16)),
        compiler_params=_cparams(1),
    )(dx1, of, ob, gfb, mod6, wout)


def _ret_bwd_items(q_ref, k_ref, v_ref, do_ref, st_ref, dq_ref, dk_ref, dv_ref, ds_scr, dlg_scr, lg, forward, nc, row0):
    order = list(reversed(range(nc))) if forward else list(range(nc))
    consts = [_decay_consts(lg[h:h + 1, 0:1], forward) for h in range(H)]
    accs = [jnp.zeros((1, DH), F32) for _ in range(H)]

    def first(h, c):
        dm, xi, zc, dec, _ = consts[h]
        hs, rs = slice(h * DH, (h + 1) * DH), slice(c * CR, (c + 1) * CR)
        qc, kc, vc, doc = q_ref[rs, hs], k_ref[rs, hs], v_ref[rs, hs], do_ref[rs, hs]
        s, dsn = st_ref[c, h], ds_scr[h]
        sb, dsnb = s.astype(BF16), dsn.astype(BF16)
        qf, kf = qc.astype(F32), kc.astype(F32)
        a_raw = _dot_nt(qc, kc)
        da_raw = _dot_nt(doc, vc)
        xdo = (xi * doc.astype(F32)).astype(BF16)
        kz = (kf * zc).astype(BF16)
        vz = (vc.astype(F32) * zc).astype(BF16)
        dq_c = _dot_nt(xdo, sb)
        dk_s = _dot_nt(vz, dsnb)
        dv_s = _dot(kz, dsnb)
        ds_scr[h] = _dot_tn((xi * qf).astype(BF16), doc) + dec * dsn
        accs[h] = accs[h] + (float(CR) * dec) * jnp.sum(s * dsn, axis=0, keepdims=True)
        a = (a_raw * dm).astype(BF16)
        da = (da_raw * dm).astype(BF16)
        return a, da, dq_c, dk_s, dv_s

    def second(h, c, carried):
        a, da, dq_c, dk_s, dv_s = carried
        ic = consts[h][4]
        if forward:
            w_qi, w_qc, w_ki, w_ks = ic, ic + 1.0, -ic, (CR - 1.0) - ic
        else:
            w_qi, w_qc, w_ki, w_ks = -ic, CR - ic, ic, ic
        hs, rs = slice(h * DH, (h + 1) * DH), slice(c * CR, (c + 1) * CR)
        qc, kc, doc = q_ref[rs, hs], k_ref[rs, hs], do_ref[rs, hs]
        dq_i = _dot(da, kc)
        dk_i = _dot_tn(da, qc)
        dv_i = _dot_tn(a, doc)
        dq_ref[rs, hs] = (dq_i + dq_c).astype(BF16)
        dk_ref[rs, hs] = (dk_i + dk_s).astype(BF16)
        dv_ref[rs, hs] = (dv_i + dv_s).astype(BF16)
        rowterm = qc.astype(F32) * (w_qi * dq_i + w_qc * dq_c) + kc.astype(F32) * (w_ki * dk_i + w_ks * dk_s)
        accs[h] = accs[h] + jnp.sum(rowterm, axis=0, keepdims=True)

    def finish():
        for h in range(H):
            dlg_scr[row0 + h:row0 + h + 1, :] += accs[h]

    items = [[(functools.partial(first, h, c), functools.partial(second, h, c)) for h in range(H)] for c in order]
    return items, finish


def _ret_bwd_run(dirs):
    nc = len(dirs[0][0])
    flat = [it for j in range(nc) for items, _ in dirs for it in items[j]]
    pending = None
    for first, second in flat:
        carried = first()
        if pending is not None:
            pending[0](pending[1])
        pending = (second, carried)
    pending[0](pending[1])
    for _, finish in dirs:
        finish()


def _ret_bwd(q, k, v, dof, dob, sst, rst, rlf, rlb, *, tm):
    t_len = q.shape[0]
    nt, nc = t_len // tm, tm // CR

    def body(qf_ref, kf_ref, vf_ref, dof_ref, sst_ref, qb_ref, kb_ref, vb_ref, dob_ref, rst_ref, rlf_ref, rlb_ref,
             dqf_ref, dkf_ref, dvf_ref, dqb_ref, dkb_ref, dvb_ref, dscf_ref, dscb_ref, dlg_ref,
             dsf_scr, dsb_scr, dlg_scr):
        i = pl.program_id(0)

        @pl.when(i == 0)
        def _():
            dsf_scr[...] = jnp.zeros_like(dsf_scr)
            dsb_scr[...] = jnp.zeros_like(dsb_scr)
            dlg_scr[...] = jnp.zeros_like(dlg_scr)

        lgf = _log_sigmoid(rlf_ref[...])
        lgb = _log_sigmoid(rlb_ref[...])
        _ret_bwd_run([
            _ret_bwd_items(qf_ref, kf_ref, vf_ref, dof_ref, sst_ref, dqf_ref, dkf_ref, dvf_ref, dsf_scr, dlg_scr,
                           lgf, True, nc, 0),
            _ret_bwd_items(qb_ref, kb_ref, vb_ref, dob_ref, rst_ref, dqb_ref, dkb_ref, dvb_ref, dsb_scr, dlg_scr,
                           lgb, False, nc, H)])

        @pl.when(i == nt - 1)
        def _():
            dscf_ref[...] = dsf_scr[...]
            dscb_ref[...] = dsb_scr[...]
            tot = jnp.broadcast_to(jnp.sum(dlg_scr[...], axis=1, keepdims=True), (8, 128))
            ri = lax.broadcasted_iota(jnp.int32, (8, 128), 0)
            li = lax.broadcasted_iota(jnp.int32, (8, 128), 1)
            dlg_ref[...] = jnp.zeros_like(dlg_ref)
            dlg_ref[0:1, 0:128] = jnp.sum(jnp.where(ri == li, tot, 0.0), axis=0, keepdims=True)
            dlg_ref[1:2, 0:128] = jnp.sum(jnp.where(ri - H == li, tot, 0.0), axis=0, keepdims=True)

    rr = functools.partial(_rows_rev, nt=nt)
    st_f = pl.BlockSpec((nc, H, DH, DH), lambda i: (nt - 1 - i, 0, 0, 0))
    st_b = pl.BlockSpec((nc, H, DH, DH), lambda i: (i, 0, 0, 0))
    tok = jax.ShapeDtypeStruct((t_len, RW), BF16)
    st = jax.ShapeDtypeStruct((H, DH, DH), F32)
    return pl.pallas_call(
        body, name="ret_bwd", grid=(nt,),
        in_specs=[rr(tm, RW), rr(tm, RW), rr(tm, RW), rr(tm, RW), st_f,
                  _rows(tm, RW), _rows(tm, RW), _rows(tm, RW), _rows(tm, RW), st_b,
                  _whole((H, 128)), _whole((H, 128))],
        out_specs=[rr(tm, RW), rr(tm, RW), rr(tm, RW), _rows(tm, RW), _rows(tm, RW), _rows(tm, RW),
                   _acc((H, DH, DH)), _acc((H, DH, DH)), _acc((8, D))],
        out_shape=(tok, tok, tok, tok, tok, tok, st, st, jax.ShapeDtypeStruct((8, D), F32)),
        scratch_shapes=[pltpu.VMEM((H, DH, DH), F32), pltpu.VMEM((H, DH, DH), F32), pltpu.VMEM((8, 128), F32)],
        compiler_params=_cparams(1),
    )(q, k, v, dof, sst, q, k, v, dob, rst, rlf, rlb)


def _in_bwd(x, zuv, dya, dqf, dkf, dvf, dqb, dkb, dvb, dgfb, dx1, cos, sin, mod6, norm1, win, sgw, sgbt, sggain, *, tm):
    t_len = x.shape[0]
    nt, nc = t_len // tm, tm // C

    def body(x_ref, zuv_ref, dya_ref, dqf_ref, dkf_ref, dvf_ref, dqb_ref, dkb_ref, dvb_ref, dgfb_ref, dx1_ref,
             cos_ref, sin_ref, mod_ref, n1_ref, win_ref, sgw_ref, sgbt_ref, sgg_ref,
             gx_ref, dz_ref, small_ref, dsgw_ref, dsgbt_ref):
        i = pl.program_id(0)

        @pl.when(i == 0)
        def _():
            small_ref[...] = jnp.zeros_like(small_ref)
            dsgw_ref[...] = jnp.zeros_like(dsgw_ref)
            dsgbt_ref[...] = jnp.zeros_like(dsgbt_ref)

        zuv = zuv_ref[...].astype(F32)
        _, (ua, dgu, dgv, mixed, vn_b, vah_l, rv_l) = _sg_forward(zuv[:, :AW], zuv[:, AW:], sgw_ref, sgbt_ref, sgg_ref, nc)
        dya = dya_ref[...].astype(F32)
        dz_ref[:, 0:AW] = (dya * mixed * dgu).astype(BF16)
        dmixed = dya * ua
        gain = sgg_ref[...]
        for g in range(G):
            gs = slice(g * DH, (g + 1) * DH)
            dmg = dmixed[:, gs]
            dmb = dmg.astype(BF16)
            wgt = sgw_ref[g].astype(BF16)
            dsw = jnp.zeros((C, C), F32)
            dsb = jnp.zeros((C, DH), F32)
            rows = []
            for c in range(nc):
                rs = slice(c * C, (c + 1) * C)
                dsw = dsw + _dot_nt(dmb[rs, :], vn_b[g][rs, :])
                dsb = dsb + dmg[rs, :]
                rows.append(_dot_tn(wgt, dmb[rs, :]))
            dsgw_ref[g] += dsw
            dsgbt_ref[g] += dsb
            dvn = jnp.concatenate(rows, axis=0) if nc > 1 else rows[0]
            vah, rv = vah_l[g], rv_l[g]
            small_ref[3:4, gs] += jnp.sum(dvn * vah, axis=0, keepdims=True)
            dyg = dvn * gain[:, gs]
            dva = rv * (dyg - vah * jnp.mean(dyg * vah, axis=-1, keepdims=True))
            dz_ref[:, AW + g * DH:AW + (g + 1) * DH] = (dva * dgv[:, gs]).astype(BF16)

        cos = jnp.tile(cos_ref[...], (1, H))
        nsins = -jnp.tile(sin_ref[...], (1, H))
        def both(f_ref, b_ref):
            return f_ref[...].astype(F32) + b_ref[...].astype(F32)

        dz_ref[:, Q_LO:KV_LO] = _rope(both(dqf_ref, dqb_ref), cos, nsins).astype(BF16)
        dz_ref[:, KV_LO:VR_LO] = (_rope(both(dkf_ref, dkb_ref), cos, nsins) * KSCALE).astype(BF16)
        dz_ref[:, VR_LO:G_LO] = both(dvf_ref, dvb_ref).astype(BF16)
        dz_ref[:, G_LO:INC] = dgfb_ref[...]

        dhx = _dot(dz_ref[...], win_ref[...])
        x = x_ref[...]
        r = lax.rsqrt(jnp.mean(x * x, axis=-1, keepdims=True) + EPS)
        xh = x * r
        n1, sc1 = n1_ref[...], mod_ref[1:2, :]
        small_ref[0:1, :] += jnp.sum(dhx, axis=0, keepdims=True)
        dhxx = jnp.sum(dhx * xh, axis=0, keepdims=True)
        small_ref[1:2, :] += dhxx * n1
        small_ref[2:3, :] += dhxx * (1.0 + sc1)
        dyg = dhx * (n1 * (1.0 + sc1))
        gx_ref[...] = dx1_ref[...] + r * (dyg - xh * jnp.mean(dyg * xh, axis=-1, keepdims=True))

        @pl.when(i == nt - 1)
        def _():
            ri = lax.broadcasted_iota(jnp.int32, (C, DH), 0)
            li = lax.broadcasted_iota(jnp.int32, (C, DH), 1)
            for g in range(G):
                tot = jnp.broadcast_to(jnp.sum(dsgbt_ref[g], axis=1, keepdims=True), (C, DH))
                small_ref[4 + g:5 + g, 0:DH] = jnp.sum(jnp.where(ri == li, tot, 0.0), axis=0, keepdims=True)

    tok = functools.partial(_rows, tm)
    return pl.pallas_call(
        body, name="in_bwd", grid=(nt,),
        in_specs=[tok(D), tok(2 * AW), tok(AW), tok(RW), tok(RW), tok(RW), tok(RW), tok(RW), tok(RW),
                  tok(2 * RW), tok(D), tok(DH), tok(DH), _whole((6, D)), _whole((1, D)), _whole((INC, D)),
                  _whole((G, C, C)), _whole((G, C, 128)), _whole((1, AW))],
        out_specs=[tok(D), tok(INC), _acc((8, D)), _acc((G, C, C))],
        out_shape=(jax.ShapeDtypeStruct((t_len, D), F32), jax.ShapeDtypeStruct((t_len, INC), BF16),
                   jax.ShapeDtypeStruct((8, D), F32), jax.ShapeDtypeStruct((G, C, C), F32)),
        scratch_shapes=[pltpu.VMEM((G, C, 128), F32)],
        compiler_params=_cparams(1),
    )(x, zuv, dya, dqf, dkf, dvf, dqb, dkb, dvb, dgfb, dx1, cos, sin, mod6, norm1, win, sgw, sgbt, sggain)


def _tn_matmul(a, b, *, bm, bt, name, init=None, init_rows=None, after=(), cols=None):
    t_len, m = a.shape
    cj, n = (0, b.shape[1]) if cols is None else cols
    ni, ntt = m // bm, t_len // bt
    has_init = init is not None

    def body(*refs):
        o_ref, ob_ref = refs[-2:]
        a_ref, b_ref = refs[:2]
        init_ref = refs[2] if has_init else None
        i, t = pl.program_id(0), pl.program_id(1)

        @pl.when(t == 0)
        def _():
            o_ref[...] = jnp.zeros_like(o_ref)

        if has_init:
            for ib in range(ni):
                lo, hi = max(init_rows[0], ib * bm), min(init_rows[1], (ib + 1) * bm)
                if lo < hi:
                    @pl.when(jnp.logical_and(t == 0, i == ib))
                    def _(lo=lo, hi=hi, ib=ib):
                        o_ref[lo - ib * bm:hi - ib * bm, :] = init_ref[lo - init_rows[0]:hi - init_rows[0], :]

        o_ref[...] += _dot_tn(a_ref[...], b_ref[...])

        @pl.when(t == ntt - 1)
        def _():
            ob_ref[...] = o_ref[...].astype(BF16)

    in_specs = [pl.BlockSpec((bt, bm), lambda i, t: (t, i)), pl.BlockSpec((bt, n), lambda i, t: (t, cj))]
    args = [a, b]
    if has_init:
        in_specs.append(pl.BlockSpec((init.shape[0], n), lambda i, t: (0, cj), pipeline_mode=pl.Buffered(1)))
        args.append(init)
    for arr in after:
        in_specs.append(pl.BlockSpec(memory_space=pl.ANY))
        args.append(arr)
    out_spec = pl.BlockSpec((bm, n), lambda i, t: (i, 0))
    return pl.pallas_call(
        body, name=name, grid=(ni, ntt),
        in_specs=in_specs,
        out_specs=[out_spec, out_spec],
        out_shape=(jax.ShapeDtypeStruct((m, n), F32), jax.ShapeDtypeStruct((m, n), BF16)),
        compiler_params=_cparams(2),
    )(*args)


def _dw_out(ycat, dxb, mod6, wout, *, bt):
    t_len = ycat.shape[0]
    ntt = t_len // bt

    def body(a_ref, b_ref, mod_ref, wout_ref, o_ref, ob_ref, dg1_ref):
        t = pl.program_id(0)

        @pl.when(t == 0)
        def _():
            o_ref[...] = jnp.zeros_like(o_ref)

        o_ref[...] += _dot_tn(a_ref[...], b_ref[...])

        @pl.when(t == ntt - 1)
        def _():
            m = o_ref[...]
            dg1_ref[...] = jnp.zeros_like(dg1_ref)
            dg1_ref[0:1, :] = jnp.sum(m * wout_ref[...].astype(F32), axis=0, keepdims=True)
            g = m * mod_ref[2:3, :]
            o_ref[...] = g
            ob_ref[...] = g.astype(BF16)

    return pl.pallas_call(
        body, name="dw_out", grid=(ntt,),
        in_specs=[pl.BlockSpec((bt, D), lambda t: (t, 0)), pl.BlockSpec((bt, D), lambda t: (t, 0)),
                  _whole((6, D)), _whole((D, D))],
        out_specs=[_acc((D, D)), _acc((D, D)), _acc((8, D))],
        out_shape=(jax.ShapeDtypeStruct((D, D), F32), jax.ShapeDtypeStruct((D, D), BF16),
                   jax.ShapeDtypeStruct((8, D), F32)),
        compiler_params=_cparams(1),
    )(ycat, dxb, mod6, wout)


def _ctx_bwd(ctx, hc, kvc, cmod6, norm1, win, rlf, rlb, dscf, dscb, dlg_in):
    lc = ctx.shape[0]

    def body(ctx_ref, hc_ref, kvc_ref, cmod_ref, n1_ref, win_ref, rlf_ref, rlb_ref, dscf_ref, dscb_ref, dlgin_ref,
             dwin_ref, small_ref, dlg_ref):
        k = kvc_ref[:, :RW]
        v = kvc_ref[:, RW:]
        t = lax.broadcasted_iota(jnp.int32, (lc, 1), 0).astype(F32)
        lgf = _log_sigmoid(rlf_ref[...])
        lgb = _log_sigmoid(rlb_ref[...])
        dks, dvs = [], []
        lane = lax.broadcasted_iota(jnp.int32, (1, 128), 1)
        row_f = jnp.zeros((1, 128), F32)
        row_b = jnp.zeros((1, 128), F32)
        for h in range(H):
            hs = slice(h * DH, (h + 1) * DH)
            wf = jnp.exp(lgf[h:h + 1, 0:1] * (lc - 1.0 - t))
            wb = jnp.exp(lgb[h:h + 1, 0:1] * t)
            kh, vhb = k[:, hs], v[:, hs].astype(BF16)
            dsf, dsb = dscf_ref[h].astype(BF16), dscb_ref[h].astype(BF16)
            dkf = wf * _dot_nt(vhb, dsf)
            dkb = wb * _dot_nt(vhb, dsb)
            dvs.append(_dot((kh * wf).astype(BF16), dsf) + _dot((kh * wb).astype(BF16), dsb))
            dks.append((dkf + dkb) * KSCALE)
            lf = jnp.sum((lc - 1.0 - t) * kh * dkf, axis=0, keepdims=True)
            lb = jnp.sum(t * kh * dkb, axis=0, keepdims=True)
            row_f = row_f + jnp.where(lane == h, jnp.sum(lf, axis=1, keepdims=True), 0.0)
            row_b = row_b + jnp.where(lane == h, jnp.sum(lb, axis=1, keepdims=True), 0.0)
        dlg_ref[...] = dlgin_ref[...]
        dlg_ref[0:1, 0:128] += row_f
        dlg_ref[1:2, 0:128] += row_b
        dkv = jnp.concatenate(dks + dvs, axis=1).astype(BF16)
        dhc = _dot(dkv, win_ref[KV_LO:KV_HI, :])
        dwin_ref[...] = _dot_tn(dkv, hc_ref[...])
        x = ctx_ref[...]
        r = lax.rsqrt(jnp.mean(x * x, axis=-1, keepdims=True) + EPS)
        xh = x * r
        small_ref[...] = jnp.zeros_like(small_ref)
        small_ref[0:1, :] = jnp.sum(dhc, axis=0, keepdims=True)
        dhxx = jnp.sum(dhc * xh, axis=0, keepdims=True)
        small_ref[1:2, :] = dhxx * n1_ref[...]
        small_ref[2:3, :] = dhxx * (1.0 + cmod_ref[1:2, :])

    return pl.pallas_call(
        body, name="ctx_bwd",
        out_shape=(jax.ShapeDtypeStruct((2 * RW, D), F32), jax.ShapeDtypeStruct((8, D), F32),
                   jax.ShapeDtypeStruct((8, D), F32)),
        compiler_params=_cparams(),
    )(ctx, hc, kvc, cmod6, norm1, win, rlf, rlb, dscf, dscb, dlg_in)


def _adam_math(w, g, m, v):
    m = ADAM_B1 * m + (1.0 - ADAM_B1) * g
    v = ADAM_B2 * v + (1.0 - ADAM_B2) * (g * g)
    m_hat = m / (1.0 - ADAM_B1 ** ADAM_STEP)
    v_hat = v / (1.0 - ADAM_B2 ** ADAM_STEP)
    delta = -ADAM_LR * (m_hat / (jnp.sqrt(v_hat) + ADAM_EPS) + ADAM_WD * w)
    return delta, m, v


def _place():
    x, y, c = lax.axis_index("x"), lax.axis_index("y"), lax.axis_index("c")
    return x, y, c, 4 * x + 2 * y + c


def _flip(x, y, c, k):
    px = 1 - x if (k >> 2) & 1 else x
    py = 1 - y if (k >> 1) & 1 else y
    pc = 1 - c if k & 1 else c
    return (px, py, pc), 4 * px + 2 * py + pc


def _gather_copy(buf_ref, send_sems, recv_sems, sem0, k, mine):
    x, y, c, me = _place()
    dev, idx = _flip(x, y, c, k)
    blk = me if mine else idx
    return pltpu.make_async_remote_copy(
        src_ref=buf_ref.at[blk], dst_ref=buf_ref.at[blk],
        send_sem=send_sems.at[sem0 + k - 1], recv_sem=recv_sems.at[sem0 + k - 1],
        device_id=dev, device_id_type=MESH)


def _entry_gather(c_row, cctx_row, wmod, bmod, shards):
    n = len(shards)
    ncol = wmod.shape[1]

    def body(*refs):
        c_ref, cctx_ref, wmod_ref, bmod_ref = refs[:4]
        in_refs = refs[4:4 + n]
        mod_ref, cs_ref = refs[4 + n:6 + n]
        out_refs = refs[6 + n:6 + 2 * n]
        cbuf, pbuf = refs[6 + 2 * n:8 + 2 * n]
        cast_refs = refs[8 + 2 * n:8 + 3 * n]
        small_send, small_recv, send_sems, recv_sems, local_sems = refs[8 + 3 * n:]
        x, y, c, me = _place()
        sib = (x, y, 1 - c)
        chips = [(1 - x, y), (x, 1 - y), (1 - x, 1 - y)]

        cbuf[me] = jnp.broadcast_to(c_ref[...], (8, D))
        small = [_gather_copy(cbuf, small_send, small_recv, 0, k, True) for k in range(1, NDEV)]
        for cp in small:
            cp.start()

        def blk(px, py, pc):
            return 4 * px + 2 * py + pc

        def copy(w, k, block, to, src=None):
            dst = out_refs[w].at[block]
            return pltpu.make_async_remote_copy(
                src_ref=dst if src is None else src, dst_ref=dst,
                send_sem=send_sems.at[w, k], recv_sem=recv_sems.at[w, k], device_id=to, device_id_type=MESH)

        first, passed, mine = [], [], []
        for w in range(n):
            cast_refs[w][...] = in_refs[w][...].astype(BF16)
            m = pltpu.make_async_copy(cast_refs[w], out_refs[w].at[me], local_sems.at[w])
            m.start()
            mine.append(m)
            cps = [copy(w, 0, me, sib, src=cast_refs[w])]
            cps += [copy(w, 1 + j, me, (*chip, c), src=cast_refs[w]) for j, chip in enumerate(chips)]
            for cp in cps:
                cp.start()
            first += cps

        for k in range(1, NDEV):
            _gather_copy(cbuf, small_send, small_recv, 0, k, False).wait_recv()
        row = lax.broadcasted_iota(jnp.int32, (16, D), 0)
        call = jnp.where(row == 8, jnp.broadcast_to(cctx_ref[...], (16, D)), 0.0)
        for d in range(NDEV):
            call = jnp.where(row == d, jnp.concatenate([cbuf[d], cbuf[d]], axis=0), call)
        cs = call * _sigmoid(call)
        cs_ref[...] = cs
        pbuf[me] = _dot(cs.astype(BF16), wmod_ref[...].astype(BF16))
        small2 = [_gather_copy(pbuf, small_send, small_recv, NDEV - 1, k, True) for k in range(1, NDEV)]
        for cp in small2:
            cp.start()

        for w in range(n):
            for j, chip in enumerate(chips):
                b = blk(*chip, c)
                copy(w, 1 + j, b, (x, y, c)).wait_recv()
                fw = copy(w, 4 + j, b, sib)
                fw.start()
                passed.append(fw)
        for w in range(n):
            copy(w, 0, blk(x, y, 1 - c), (x, y, c)).wait_recv()
            for j, chip in enumerate(chips):
                copy(w, 4 + j, blk(*chip, 1 - c), (x, y, c)).wait_recv()

        for k in range(1, NDEV):
            _gather_copy(pbuf, small_send, small_recv, NDEV - 1, k, False).wait_recv()
        for d in range(NDEV):
            mod_ref[:, d * ncol:(d + 1) * ncol] = pbuf[d] + bmod_ref[:, d * ncol:(d + 1) * ncol]
        for cp in small + small2 + first + passed:
            cp.wait_send()
        for m in mine:
            m.wait()

    vm = pl.BlockSpec(memory_space=pltpu.VMEM)
    hbm = pl.BlockSpec(memory_space=pltpu.HBM)
    return pl.pallas_call(
        body, name="entry_gather",
        in_specs=[vm] * (4 + n), out_specs=[vm, vm] + [hbm] * n,
        out_shape=(jax.ShapeDtypeStruct((16, 6 * D), F32), jax.ShapeDtypeStruct((16, D), F32))
        + tuple(jax.ShapeDtypeStruct((NDEV,) + s.shape, BF16) for s in shards),
        scratch_shapes=[pltpu.VMEM((NDEV, 8, D), F32), pltpu.VMEM((NDEV, 16, ncol), F32)]
        + [pltpu.VMEM(s.shape, BF16) for s in shards]
        + [pltpu.SemaphoreType.DMA((2 * (NDEV - 1),)), pltpu.SemaphoreType.DMA((2 * (NDEV - 1),)),
           pltpu.SemaphoreType.DMA((n, 7)), pltpu.SemaphoreType.DMA((n, 7)), pltpu.SemaphoreType.DMA((n,))],
        compiler_params=_cparams(),
    )(c_row, cctx_row, wmod, bmod, *shards)


_HBM = pl.BlockSpec(memory_space=pltpu.HBM)
_SEMS = pl.BlockSpec(memory_space=pltpu.SEMAPHORE)
_EFFECT = pltpu.SideEffectType.DATAFLOW_SIDE_EFFECTING


def _exchange_copy(src_refs, land_refs, send_sems, recv_sems, w, k, scatter, landing_slot_is_mine):
    x, y, c, me = _place()
    dev, pidx = _flip(x, y, c, k)
    src = src_refs[w].at[pidx] if scatter else src_refs[w]
    slot = me if landing_slot_is_mine else pidx
    return pltpu.make_async_remote_copy(
        src_ref=src, dst_ref=land_refs[w].at[slot],
        send_sem=send_sems.at[w * (NDEV - 1) + k - 1], recv_sem=recv_sems.at[w * (NDEV - 1) + k - 1],
        device_id=dev, device_id_type=MESH)


def _exchange_start(srcs, *, scatter, name):
    n = len(srcs)
    lands = [lax.empty((NDEV,) + tuple(s.shape[-2:]), s.dtype) for s in srcs]

    def body(*refs):
        src_refs, land_refs = refs[:n], refs[n:2 * n]
        send_sems, recv_sems = refs[2 * n], refs[2 * n + 1]
        token = refs[-1]
        for w in range(n):
            for k in range(1, NDEV):
                _exchange_copy(src_refs, land_refs, send_sems, recv_sems, w, k, scatter, True).start()
        token[...] = jnp.zeros_like(token)

    arrs = list(srcs) + lands
    out_shape = ([pltpu.SemaphoreType.DMA((n * (NDEV - 1),)), pltpu.SemaphoreType.DMA((n * (NDEV - 1),))]
                 + [pltpu.HBM(a.shape, a.dtype) for a in arrs] + [jax.ShapeDtypeStruct((8, 128), F32)])
    res = pl.pallas_call(
        body, name=name, out_shape=tuple(out_shape),
        in_specs=[_HBM] * (2 * n),
        out_specs=tuple([_SEMS, _SEMS] + [_HBM] * (2 * n) + [pl.BlockSpec(memory_space=pltpu.VMEM)]),
        input_output_aliases={i: 2 + i for i in range(2 * n)},
        compiler_params=pltpu.CompilerParams(has_side_effects=_EFFECT),
    )(*[pltpu.with_memory_space_constraint(a, pltpu.HBM) for a in arrs])
    return res[:-1], res[-1]


def _exchange_wait(handles, after, *, scatter, name):
    n = (len(handles) - 2) // 2
    na = len(after)

    def body(*refs):
        src_refs, land_refs = refs[:n], refs[n:2 * n]
        send_sems, recv_sems = refs[2 * n], refs[2 * n + 1]
        for w in range(n):
            for k in range(1, NDEV):
                cp = _exchange_copy(src_refs, land_refs, send_sems, recv_sems, w, k, scatter, False)
                cp.wait_send()
                cp.wait_recv()

    arrs = handles[2:]
    res = pl.pallas_call(
        body, name=name, out_shape=tuple(pltpu.HBM(a.shape, a.dtype) for a in arrs),
        in_specs=[_HBM] * (2 * n) + [_SEMS, _SEMS] + [_HBM] * na,
        out_specs=tuple([_HBM] * (2 * n)),
        input_output_aliases={i: i for i in range(2 * n)},
        compiler_params=pltpu.CompilerParams(has_side_effects=_EFFECT),
    )(*arrs, handles[0], handles[1], *[pltpu.with_memory_space_constraint(a, pltpu.HBM) for a in after])
    return res[:n], res[n:]


_SAME_CORE = (2, 4, 6)


def _gather2_copy(src_ref, land_ref, send_sems, recv_sems, sem, k, block):
    x, y, c, _ = _place()
    dev, _ = _flip(x, y, c, k)
    dst = land_ref.at[block]
    return pltpu.make_async_remote_copy(
        src_ref=dst if src_ref is None else src_ref, dst_ref=dst,
        send_sem=send_sems.at[sem], recv_sem=recv_sems.at[sem], device_id=dev, device_id_type=MESH)


def _split_call(body, name, arrs, n_sems, sems=None, after=(), token=False):
    na = len(arrs)
    out_shape, out_specs = [], []
    if n_sems is not None:
        out_shape += [pltpu.SemaphoreType.DMA((n_sems,)), pltpu.SemaphoreType.DMA((n_sems,))]
        out_specs += [_SEMS, _SEMS]
    first = len(out_shape)
    out_shape += [pltpu.HBM(a.shape, a.dtype) for a in arrs]
    out_specs += [_HBM] * na
    if token:
        out_shape.append(jax.ShapeDtypeStruct((8, 128), F32))
        out_specs.append(pl.BlockSpec(memory_space=pltpu.VMEM))
    in_specs = [_HBM] * na + ([_SEMS, _SEMS] if sems is not None else []) + [_HBM] * len(after)
    args = [pltpu.with_memory_space_constraint(a, pltpu.HBM) for a in arrs] + list(sems or ()) \
        + [pltpu.with_memory_space_constraint(a, pltpu.HBM) for a in after]
    return pl.pallas_call(
        body, name=name, out_shape=tuple(out_shape), in_specs=in_specs, out_specs=tuple(out_specs),
        input_output_aliases={i: first + i for i in range(na)},
        compiler_params=pltpu.CompilerParams(has_side_effects=_EFFECT))(*args)


def _gather2_start(srcs, *, name):
    n = len(srcs)
    lands = [lax.empty((NDEV,) + tuple(s.shape), s.dtype) for s in srcs]

    def body(*refs):
        src_refs, land_refs = refs[:n], refs[n:2 * n]
        send_sems, recv_sems = refs[2 * n], refs[2 * n + 1]
        _, _, _, me = _place()
        for w in range(n):
            for slot, k in enumerate((1,) + _SAME_CORE):
                _gather2_copy(src_refs[w], land_refs[w], send_sems, recv_sems, 4 * w + slot, k, me).start()
        refs[-1][...] = jnp.zeros_like(refs[-1])

    res = _split_call(body, name, list(srcs) + lands, 4 * n, token=True)
    return res[:2], res[2:-1], res[-1]


def _gather2_arrived(sems, arrs, after, *, name):
    n = len(arrs) // 2

    def body(*refs):
        src_refs, land_refs = refs[:n], refs[n:2 * n]
        send_sems, recv_sems = refs[2 * n], refs[2 * n + 1]
        x, y, c, me = _place()
        for w in range(n):
            for slot, k in enumerate((1,) + _SAME_CORE):
                _, pidx = _flip(x, y, c, k)
                _gather2_copy(src_refs[w], land_refs[w], send_sems, recv_sems, 4 * w + slot, k, me).wait_send()
                _gather2_copy(None, land_refs[w], send_sems, recv_sems, 4 * w + slot, k, pidx).wait_recv()

    return _split_call(body, name, arrs, None, sems=sems, after=after)


def _gather2_forward(lands, *, name):
    n = len(lands)

    def body(*refs):
        land_refs = refs[:n]
        send_sems, recv_sems = refs[n], refs[n + 1]
        x, y, c, _ = _place()
        for w in range(n):
            for j, k in enumerate(_SAME_CORE):
                _, pidx = _flip(x, y, c, k)
                _gather2_copy(None, land_refs[w], send_sems, recv_sems, 3 * w + j, 1, pidx).start()
        refs[-1][...] = jnp.zeros_like(refs[-1])

    res = _split_call(body, name, list(lands), 3 * n, token=True)
    return res[:2], res[2:-1], res[-1]


def _gather2_wait(sems, lands, after, *, name):
    n = len(lands)

    def body(*refs):
        land_refs = refs[:n]
        send_sems, recv_sems = refs[n], refs[n + 1]
        x, y, c, _ = _place()
        for w in range(n):
            for j, k in enumerate(_SAME_CORE):
                _, mine = _flip(x, y, c, k)
                _, theirs = _flip(x, y, c, k ^ 1)
                _gather2_copy(None, land_refs[w], send_sems, recv_sems, 3 * w + j, 1, mine).wait_send()
                _gather2_copy(None, land_refs[w], send_sems, recv_sems, 3 * w + j, 1, theirs).wait_recv()

    return _split_call(body, name, list(lands), None, sems=sems, after=after)


def _cast_bf16(arrs, *, name, after=()):
    n = len(arrs)

    def body(*refs):
        for i_ref, o_ref in zip(refs[:n], refs[n + len(after):]):
            o_ref[...] = i_ref[...].astype(BF16)

    return pl.pallas_call(
        body, name=name, out_shape=tuple(jax.ShapeDtypeStruct(a.shape, BF16) for a in arrs),
        in_specs=[pl.BlockSpec(memory_space=pltpu.VMEM)] * n + [pl.BlockSpec(memory_space=pl.ANY)] * len(after),
        compiler_params=_cparams())(*arrs, *after)


def _rs_adam(me_arr, fulls, lands, w, m, v, *, name):
    rows = lands[0].shape[1]
    k = len(fulls)
    nb = next(n for n in (4, 2, 1) if rows % (16 * n) == 0)
    br = rows // nb

    def body(me_ref, *refs):
        own_refs, land_refs = refs[:k], refs[k:2 * k]
        w_ref, m_ref, v_ref, g_ref, d_ref, mo_ref, vo_ref = refs[2 * k:]
        me = me_ref[0]
        parts = []
        for own_ref, land_ref in zip(own_refs, land_refs):
            acc = jnp.zeros(own_ref.shape, F32)
            for p in range(NDEV):
                acc = acc + jnp.where(p == me, own_ref[...], land_ref[p].astype(F32))
            parts.append(acc)
        acc = parts[0] if k == 1 else jnp.concatenate(parts, axis=1)
        g_ref[...] = acc
        d_ref[...], mo_ref[...], vo_ref[...] = _adam_math(w_ref[...], acc, m_ref[...], v_ref[...])

    sh = jax.ShapeDtypeStruct((rows, D), F32)
    blk = pl.BlockSpec((br, D), lambda i, me: (i, 0))
    grid_spec = pltpu.PrefetchScalarGridSpec(
        num_scalar_prefetch=1, grid=(nb,),
        in_specs=[pl.BlockSpec((br, f.shape[1]), lambda i, me: (me[0] * nb + i, 0)) for f in fulls]
        + [pl.BlockSpec((NDEV, br, l.shape[2]), lambda i, me: (0, i, 0)) for l in lands]
        + [blk, blk, blk],
        out_specs=[blk] * 4)
    return pl.pallas_call(
        body, name=name, out_shape=(sh, sh, sh, sh), grid_spec=grid_spec, compiler_params=_cparams(1),
    )(me_arr, *fulls, *lands, w, m, v)


ROW_F, ROW_I, ROW_C, ROW_G1, ROW_LG = 0, 8, 16, 24, 32
PACK_ROWS = 40


def _small_sum(me_arr, pack, pack_land, sgw, sgw_land, wmod):
    ncol = wmod.shape[1]

    def body(me_ref, pack_ref, pland_ref, sgw_ref, sland_ref, wmod_ref, sum_ref, all_ref, sgw_sum_ref, part_ref):
        me = me_ref[0]
        acc = jnp.zeros((PACK_ROWS, D), F32)
        acc_w = jnp.zeros(sgw_ref.shape, F32)
        for p in range(NDEV):
            rows = jnp.where(p == me, pack_ref[...], pland_ref[p])
            all_ref[p] = rows
            acc = acc + rows
            acc_w = acc_w + jnp.where(p == me, sgw_ref[...], sland_ref[p])
        sum_ref[...] = acc
        sgw_sum_ref[...] = acc_w
        zero = jnp.zeros((1, D), F32)
        dcmod = jnp.concatenate([acc[ROW_C:ROW_C + 1], acc[ROW_C + 1:ROW_C + 2], zero, zero, zero, zero], axis=1)
        mine = jnp.zeros((1, ncol), F32)
        for d in range(NDEV):
            mine = mine + jnp.where(d == me, dcmod[:, d * ncol:(d + 1) * ncol], 0.0)
        part_ref[...] = _dot_nt(jnp.broadcast_to(mine, (8, ncol)).astype(BF16), wmod_ref[...].astype(BF16))

    vm = pl.BlockSpec(memory_space=pltpu.VMEM)
    return pl.pallas_call(
        body, name="small_sum",
        out_shape=(jax.ShapeDtypeStruct((PACK_ROWS, D), F32), jax.ShapeDtypeStruct((NDEV, PACK_ROWS, D), F32),
                   jax.ShapeDtypeStruct(sgw.shape, F32), jax.ShapeDtypeStruct((8, D), F32)),
        in_specs=[pl.BlockSpec(memory_space=pltpu.SMEM)] + [vm] * 5,
        compiler_params=_cparams(),
    )(me_arr, pack, pack_land, sgw, sgw_land, wmod)


def _cctx_final(me_arr, part, part_land, cctx_row, m_row, v_row):
    def body(me_ref, part_ref, land_ref, c_ref, m_ref, v_ref, g_ref, d_ref, mo_ref, vo_ref):
        me = me_ref[0]
        tot = jnp.zeros((8, D), F32)
        for p in range(NDEV):
            tot = tot + jnp.where(p == me, part_ref[...], land_ref[p])
        cc = c_ref[...]
        _, dsilu = _silu_parts(cc)
        g = tot[0:1, :] * dsilu
        g_ref[...] = g
        d_ref[...], mo_ref[...], vo_ref[...] = _adam_math(cc, g, m_ref[...], v_ref[...])

    row = jax.ShapeDtypeStruct((1, D), F32)
    vm = pl.BlockSpec(memory_space=pltpu.VMEM)
    return pl.pallas_call(
        body, name="cctx_final", out_shape=(row, row, row, row),
        in_specs=[pl.BlockSpec(memory_space=pltpu.SMEM)] + [vm] * 5,
        compiler_params=_cparams(),
    )(me_arr, part, part_land, cctx_row, m_row, v_row)


def _adam_small(s, sgw_g, params):
    names = ["norm1", "norm2", "norm_f", "sg_gain", "sg_b", "ret_logit_f", "ret_logit_b", "b_mod", "sg_w"]
    flat = [a for n in names for a in params[n]]

    def body(*refs):
        s_ref, sgw_ref = refs[0], refs[1]
        p_refs = refs[2:2 + 3 * len(names)]
        o_refs = refs[2 + 3 * len(names):]
        loss_ref = o_refs[-1]

        def row(r):
            return s_ref[r:r + 1, :]

        grads = {
            "norm1": row(ROW_I + 2) + row(ROW_C + 2),
            "norm2": row(ROW_F + 4),
            "norm_f": row(ROW_F + 0),
            "sg_gain": s_ref[ROW_I + 3:ROW_I + 4, 0:AW],
            "sg_b": s_ref[ROW_I + 4:ROW_I + 8, 0:DH],
            "sg_w": sgw_ref[...],
        }
        for j, n in enumerate(names):
            w_ref, m_ref, v_ref = p_refs[3 * j:3 * j + 3]
            g_ref, d_ref, mo_ref, vo_ref = o_refs[4 * j:4 * j + 4]
            w = w_ref[...]
            if n in ("ret_logit_f", "ret_logit_b"):
                r = ROW_LG + (0 if n == "ret_logit_f" else 1)
                g = s_ref[r:r + 1, 0:H] * _sigmoid(-w)
            elif n == "b_mod":
                rows = [row(ROW_I + 0) + row(ROW_C + 0), row(ROW_I + 1) + row(ROW_C + 1), row(ROW_G1),
                        row(ROW_F + 2), row(ROW_F + 3), row(ROW_F + 1)]
                g = jnp.concatenate(rows, axis=1)
            else:
                g = grads[n]
            g_ref[...] = g
            d_ref[...], mo_ref[...], vo_ref[...] = _adam_math(w, g, m_ref[...], v_ref[...])
        loss_ref[...] = jnp.full((1, 1), 0.5 / D, F32) * jnp.sum(row(ROW_F + 5), axis=1, keepdims=True)

    out_shape = []
    for n in names:
        w = params[n][0]
        out_shape += [jax.ShapeDtypeStruct(w.shape, F32)] * 4
    out_shape.append(jax.ShapeDtypeStruct((1, 1), F32))
    outs = pl.pallas_call(body, name="adam_small", out_shape=tuple(out_shape), compiler_params=_cparams())(
        s, sgw_g, *flat)
    return {n: tuple(outs[4 * j:4 * j + 4]) for j, n in enumerate(names)}, outs[-1]


def _wmod_grad(cs_all, dmod_cols, wmod, m, v):
    ncol = wmod.shape[1]

    def body(cs_ref, dm_ref, w_ref, m_ref, v_ref, g_ref, d_ref, mo_ref, vo_ref):
        g = _dot_tn(cs_ref[...].astype(BF16), dm_ref[...].astype(BF16))
        g_ref[...] = g
        d_ref[...], mo_ref[...], vo_ref[...] = _adam_math(w_ref[...], g, m_ref[...], v_ref[...])

    sh = jax.ShapeDtypeStruct((D, ncol), F32)
    br = D // 4
    blk = pl.BlockSpec((br, ncol), lambda i: (i, 0))
    return pl.pallas_call(
        body, name="wmod_grad", out_shape=(sh, sh, sh, sh), grid=(D // br,),
        in_specs=[pl.BlockSpec((cs_all.shape[0], br), lambda i: (0, i)), _whole(dmod_cols.shape), blk, blk, blk],
        out_specs=[blk] * 4,
        compiler_params=_cparams(1),
    )(cs_all, dmod_cols, wmod, m, v)


def _rope_tables(t_len):
    n_freq = DH // 4
    inv = (ROPE_BASE ** (-np.arange(n_freq, dtype=np.float32) / n_freq)).astype(np.float32)
    tok = np.arange(t_len)
    rows = (tok // GRID_W).astype(np.float32)
    cols = (tok % GRID_W).astype(np.float32)
    ang_r = rows[:, None] * inv[None, :]
    ang_c = cols[:, None] * inv[None, :]
    cos = np.concatenate([np.cos(ang_r)] * 2 + [np.cos(ang_c)] * 2, axis=1).astype(np.float32)
    sin = np.concatenate([-np.sin(ang_r), np.sin(ang_r), -np.sin(ang_c), np.sin(ang_c)], axis=1).astype(np.float32)
    return jnp.asarray(cos), jnp.asarray(sin)


def _local_step(x, tgt, ctx, mod6, cmod6, norm1, norm2, normf, win, wout, ffn_weights, sgw, sgb, sggain, rlf, rlb,
                *, tm_a, tm_b, tm_f, tm_m, tm_r, tm_i, bt_w, after_first_grads=None, small_path=None,
                after_in_half=None):
    t_len = x.shape[0]
    cos, sin = _rope_tables(t_len)
    sgbt = jnp.broadcast_to(sgb[:, :, None], (G, C, 128))
    rlf = jnp.broadcast_to(rlf.reshape(H, 1), (H, 128))
    rlb = jnp.broadcast_to(rlb.reshape(H, 1), (H, 128))
    hc, kvc, scf, scb = _ctx_fwd(ctx, cmod6, norm1, win, rlf, rlb)
    hx, zuv, q, k, v, gfb, of, ya, sst = _fwd_a(x, mod6, norm1, win, sgw, sgbt, sggain, cos, sin, rlf, scf, tm=tm_a)
    if callable(wout):
        wout, tok = wout(hx)
        rlb = rlb + tok
    ob, ycat, x1, rst = _fwd_b(q, k, v, of, gfb, ya, x, mod6, wout, rlb, scb, tm=tm_b)
    wg, wu, wd = ffn_weights(x1) if callable(ffn_weights) else ffn_weights
    dx1, h2, da, db, hm, df, small_f = _ffn(x1, tgt, mod6, norm2, normf, wg, wu, wd, tm=tm_f)
    bt2 = min(2 * bt_w, t_len)
    d_wd, d_wd_b = _tn_matmul(hm, df, bm=DFF // 2, bt=bt2, name="dw_down")
    d_wg, d_wg_b = _tn_matmul(da, h2, bm=DFF // 2, bt=bt2, name="dw_gate")
    d_wu, d_wu_b = _tn_matmul(db, h2, bm=DFF // 2, bt=bt2, name="dw_up")
    dxb, dya, dgfb, dof, dob = _mid_bwd(dx1, of, ob, gfb, mod6, wout, tm=tm_m)
    d_wo, d_wo_b, dg1 = _dw_out(ycat, dxb, mod6, wout, bt=bt2)
    if after_first_grads is not None:
        rlf = rlf + after_first_grads((d_wd_b, d_wg_b, d_wu_b, d_wo_b))
    dqf, dkf, dvf, dqb, dkb, dvb, dscf, dscb, dlg = _ret_bwd(q, k, v, dof, dob, sst, rst, rlf, rlb, tm=tm_r)
    gx, dz, small_i, dsgw = _in_bwd(x, zuv, dya, dqf, dkf, dvf, dqb, dkb, dvb, dgfb, dx1, cos, sin,
                                    mod6, norm1, win, sgw, sgbt, sggain, tm=tm_i)
    dwin_c, small_c, dlg = _ctx_bwd(ctx, hc, kvc, cmod6, norm1, win, rlf, rlb, dscf, dscb, dlg)
    pack = jnp.concatenate([small_f, small_i, small_c, dg1, dlg], axis=0)
    after = small_path(pack, dsgw) if small_path is not None else ()
    halves = []
    for j in range(2):
        halves.append(_tn_matmul(dz, hx, bm=INC // 2, bt=bt2, name="dw_in_%d" % j, init=dwin_c,
                                 init_rows=(KV_LO, KV_HI), after=after, cols=(j, D // 2)))
        if after_in_half is not None:
            after = after_in_half(j, halves[-1][1])
    grads = {"w_in": halves, "w_out": (d_wo, d_wo_b), "w_gate": (d_wg, d_wg_b), "w_up": (d_wu, d_wu_b),
             "w_down": (d_wd, d_wd_b)}
    return gx, grads, pack, dsgw


def kernel(x, c, ctx, c_ctx, w_mod, b_mod, norm1, w_in, sg_gain, sg_w, sg_b, ret_logit_f, ret_logit_b, w_out, norm2, w_gate, w_up, w_down, norm_f, loss_target, m_c_ctx, m_w_mod, m_b_mod, m_norm1, m_w_in, m_sg_gain, m_sg_w, m_sg_b, m_ret_logit_f, m_ret_logit_b, m_w_out, m_norm2, m_w_gate, m_w_up, m_w_down, m_norm_f, v_c_ctx, v_w_mod, v_b_mod, v_norm1, v_w_in, v_sg_gain, v_sg_w, v_sg_b, v_ret_logit_f, v_ret_logit_b, v_w_out, v_norm2, v_w_gate, v_w_up, v_w_down, v_norm_f):
    t_len = x.shape[1]
    tm = min(512, t_len)
    me = 4 * lax.axis_index("x") + 2 * lax.axis_index("y") + lax.axis_index("c")
    tr = lambda a: jnp.transpose(a[0])

    cctx_row = c_ctx.reshape(1, D)
    mod_all, cs_all, g_in = _entry_gather(c, cctx_row, w_mod[0], b_mod, [tr(w_in)])
    mod6 = lax.dynamic_slice(mod_all, (me, 0), (1, 6 * D)).reshape(6, D)
    cmod6 = mod_all[8].reshape(6, D)
    win_t = g_in.reshape(INC, D)

    (out_shard,) = _cast_bf16([w_out[0]], name="cast_out", after=[g_in])
    h_out, tok_out = _exchange_start([out_shard], scatter=False, name="wout_start")
    ffn_shards = _cast_bf16([tr(w_gate), tr(w_up), w_down[0]], name="cast_ffn", after=[h_out[2]])
    sems_ffn, arrs_ffn, tok = _gather2_start(ffn_shards, name="wffn_start")
    mod6 = mod6 + (tok_out[0, 0] + tok[0, 0])
    ffn = {}

    def place_own(own, lands, rows):
        return [lax.dynamic_update_slice(l, o[None], (me, 0, 0)).reshape(rows, D) for o, l in zip(own, lands)]

    def wout(after):
        own, lands = _exchange_wait(h_out, [after], scatter=False, name="wout_wait")
        arrs = _gather2_arrived(sems_ffn, arrs_ffn, [after], name="wffn_arrived")
        ffn["own"] = arrs[:3]
        ffn["sems"], ffn["lands"], tok_fwd = _gather2_forward(arrs[3:], name="wffn_forward")
        return place_own(own, lands, D)[0], tok_fwd[0, 0]

    def ffn_weights(after):
        lands = _gather2_wait(ffn["sems"], ffn["lands"], [after], name="wffn_wait")
        return place_own(ffn["own"], lands, DFF)

    rs, outs = {}, {}

    def after_first_grads(bf):
        rs["first"], tok_first = _exchange_start([b.reshape(NDEV, b.shape[0] // NDEV, D) for b in bf], scatter=True,
                                                 name="rs_first_start")
        return tok_first[0, 0]

    def small_path(pack, dsgw):
        rs["small"], _ = _exchange_start([pack, dsgw.reshape(G * C, C)], scatter=False, name="small_start")
        return [rs["small"][2], rs["small"][3]]

    def after_in_half(j, half_bf16):
        rs["in%d" % j], _ = _exchange_start([half_bf16.reshape(NDEV, INC // NDEV, D // 2)], scatter=True,
                                            name="rs_in%d_start" % j)
        return [rs["in%d" % j][2]]

    gx, grads, pack, dsgw = _local_step(
        x[0], loss_target[0], ctx[0], mod6, cmod6, norm1, norm2[0:1], norm_f.reshape(1, D), win_t, wout, ffn_weights,
        sg_w[0], sg_b[0], sg_gain, ret_logit_f, ret_logit_b,
        tm_a=tm, tm_b=tm, tm_f=min(256, t_len), tm_m=min(1024, t_len), tm_r=min(1024, t_len), tm_i=tm,
        bt_w=min(1024, t_len),
        after_first_grads=after_first_grads, small_path=small_path, after_in_half=after_in_half)

    me_arr = me.reshape(1).astype(jnp.int32)
    (pack_own, sgw_own), (pack_land, sgw_land) = _exchange_wait(rs["small"], [rs["in1"][2]], scatter=False,
                                                               name="small_wait")
    psum_rows, pall, sgw_sum, part = _small_sum(me_arr, pack_own, pack_land, sgw_own, sgw_land, w_mod[0])
    h_cc, _ = _exchange_start([part], scatter=False, name="cctx_start")

    dmod_rows = (ROW_I + 0, ROW_I + 1, ROW_G1, ROW_F + 2, ROW_F + 3, ROW_F + 1)
    dmod_all = jnp.concatenate([pall[:, r, :] for r in dmod_rows], axis=1)
    dcmod = jnp.concatenate([psum_rows[ROW_C + 0:ROW_C + 1], psum_rows[ROW_C + 1:ROW_C + 2],
                             jnp.zeros((1, 4 * D), F32)], axis=1)
    dmod_mat = jnp.concatenate([dmod_all, jnp.pad(dcmod, ((0, 7), (0, 0)))], axis=0)
    ncol = 6 * D // NDEV
    dmod_cols = lax.dynamic_slice(dmod_mat, (0, me * ncol), (16, ncol))
    g_wm, d_wm, m_wm, v_wm = _wmod_grad(cs_all, dmod_cols, w_mod[0], m_w_mod[0], v_w_mod[0])
    outs["w_mod"] = (g_wm[None], d_wm[None], m_wm[None], v_wm[None])
    small_params = {
        "norm1": (norm1, m_norm1, v_norm1), "norm2": (norm2, m_norm2, v_norm2),
        "norm_f": tuple(a.reshape(1, D) for a in (norm_f, m_norm_f, v_norm_f)),
        "sg_gain": (sg_gain, m_sg_gain, v_sg_gain), "sg_b": (sg_b[0], m_sg_b[0], v_sg_b[0]),
        "ret_logit_f": (ret_logit_f, m_ret_logit_f, v_ret_logit_f),
        "ret_logit_b": (ret_logit_b, m_ret_logit_b, v_ret_logit_b),
        "b_mod": (b_mod, m_b_mod, v_b_mod), "sg_w": (sg_w[0], m_sg_w[0], v_sg_w[0])}
    small, loss = _adam_small(psum_rows, sgw_sum.reshape(G, C, C), small_params)
    back = {"norm_f": lambda a: a.reshape(D), "sg_b": lambda a: a[None], "sg_w": lambda a: a[None]}
    for name, vals in small.items():
        outs[name] = tuple(back.get(name, lambda a: a)(a) for a in vals)

    _, lands_first = _exchange_wait(rs["first"], [g_wm], scatter=True, name="rs_first_wait")

    def finish(name, fulls, lands, w, m, v, transposed):
        take = tr if transposed else (lambda a: a[0])
        res = _rs_adam(me_arr, fulls, lands, take(w), take(m), take(v), name="adam_" + name)
        put = (lambda a: jnp.transpose(a)[None]) if transposed else (lambda a: a[None])
        outs[name] = tuple(put(a) for a in res)
        return res[0]

    g_down = finish("w_down", [grads["w_down"][0]], [lands_first[0]], w_down, m_w_down, v_w_down, False)
    g_gate = finish("w_gate", [grads["w_gate"][0]], [lands_first[1]], w_gate, m_w_gate, v_w_gate, True)
    g_up = finish("w_up", [grads["w_up"][0]], [lands_first[2]], w_up, m_w_up, v_w_up, True)
    g_out = finish("w_out", [grads["w_out"][0]], [lands_first[3]], w_out, m_w_out, v_w_out, False)
    done = [g_down, g_gate, g_up, g_out]
    (part_own,), (part_land,) = _exchange_wait(h_cc, done, scatter=False, name="cctx_wait")
    res_cc = _cctx_final(me_arr, part_own, part_land, cctx_row, m_c_ctx.reshape(1, D), v_c_ctx.reshape(1, D))
    outs["c_ctx"] = tuple(a.reshape(D) for a in res_cc)
    lands_in = [_exchange_wait(rs["in%d" % j], done, scatter=True, name="rs_in%d_wait" % j)[1][0] for j in range(2)]
    finish("w_in", [h[0] for h in grads["w_in"]], lands_in, w_in, m_w_in, v_w_in, True)

    order = ["c_ctx", "w_mod", "b_mod", "norm1", "w_in", "sg_gain", "sg_w", "sg_b", "ret_logit_f", "ret_logit_b",
             "w_out", "norm2", "w_gate", "w_up", "w_down", "norm_f"]
    res = [loss.reshape(()), gx[None]]
    for j in range(4):
        res += [outs[name][j] for name in order]
    return tuple(res)
```

```python
import functools
import math

import numpy as np
import jax
import jax.numpy as jnp
from jax import lax
from jax.experimental import pallas as pl
from jax.experimental.pallas import tpu as pltpu

F32 = jnp.float32
BF16 = jnp.bfloat16

D = 1024
AW = 512
RW = 512
H = 4
DH = 128
G = 4
C = 128
CR = 256
DFF = 2816
INC = 3584
Q_LO = 2 * AW
KV_LO, VR_LO, G_LO = Q_LO + RW, Q_LO + 2 * RW, Q_LO + 3 * RW
KV_HI = G_LO
NDEV = 8
EPS = 1e-6
KSCALE = DH ** -0.5
ROPE_BASE = 10000.0
GRID_W = 64

ADAM_LR = 0.001
ADAM_B1 = 0.9
ADAM_B2 = 0.999
ADAM_EPS = 1e-08
ADAM_WD = 0.01
ADAM_STEP = 10

VMEM_LIMIT = 56 * 1024 * 1024
MESH = pl.DeviceIdType.MESH


def _cparams(n_grid=0, **kw):
    sem = ("arbitrary",) * n_grid if n_grid else None
    return pltpu.CompilerParams(dimension_semantics=sem, vmem_limit_bytes=VMEM_LIMIT, **kw)


def _dot(a, b):
    return jnp.dot(a, b, preferred_element_type=F32)


def _dot_nt(a, b):
    return lax.dot_general(a, b, (((1,), (1,)), ((), ())), preferred_element_type=F32)


def _dot_tn(a, b):
    return lax.dot_general(a, b, (((0,), (0,)), ((), ())), preferred_element_type=F32)


def _whole(shape):
    nd = len(shape)
    return pl.BlockSpec(shape, lambda *_: (0,) * nd, pipeline_mode=pl.Buffered(1))


def _acc(shape):
    nd = len(shape)
    return pl.BlockSpec(shape, lambda *_: (0,) * nd)


def _rows(tm, n):
    return pl.BlockSpec((tm, n), lambda i: (i, 0))


def _rows_rev(tm, n, nt):
    return pl.BlockSpec((tm, n), lambda i: (nt - 1 - i, 0))


def _sigmoid(x):
    return 1.0 / (1.0 + jnp.exp(-x))


def _log_sigmoid(x):
    return jnp.minimum(x, 0.0) - jnp.log(1.0 + jnp.exp(-jnp.abs(x)))


_GK0 = math.sqrt(2.0 / math.pi)
_GK1 = 0.044715


def _gelu(x):
    x2 = x * x
    t = jnp.tanh(x * (_GK0 + (_GK0 * _GK1) * x2))
    h = 0.5 + 0.5 * t
    g = x * h
    dg = h + g * (1.0 - h) * ((2.0 * _GK0) + (6.0 * _GK0 * _GK1) * x2)
    return g, dg


def _silu_parts(a):
    s = _sigmoid(a)
    sa = a * s
    return sa, s + sa * (1.0 - s)


def _rope(t, cos, sins):
    n = t.shape[1]
    lane = lax.broadcasted_iota(jnp.int32, t.shape, 1)
    first = (lane & 63) < 32
    partner = jnp.where(first, pltpu.roll(t, n - 32, 1), pltpu.roll(t, 32, 1))
    return t * cos + partner * sins


def _headnorm(o):
    outs, rs = [], []
    for h in range(H):
        oh = o[:, h * DH:(h + 1) * DH]
        r = lax.rsqrt(jnp.mean(oh * oh, axis=-1, keepdims=True) + EPS)
        outs.append(oh * r)
        rs.append(r)
    return jnp.concatenate(outs, axis=1), rs


def _decay_consts(lg, forward):
    ii = lax.broadcasted_iota(jnp.int32, (CR, CR), 0).astype(F32)
    jj = lax.broadcasted_iota(jnp.int32, (CR, CR), 1).astype(F32)
    ic = lax.broadcasted_iota(jnp.int32, (CR, 1), 0).astype(F32)
    if forward:
        diff = ii - jj
        xi = jnp.exp(lg * (ic + 1.0))
        z = jnp.exp(lg * (CR - 1.0 - ic))
    else:
        diff = jj - ii
        xi = jnp.exp(lg * (CR - ic))
        z = jnp.exp(lg * ic)
    dm = jnp.where(diff >= 0.0, jnp.exp(lg * jnp.maximum(diff, 0.0)), 0.0)
    dec = jnp.exp(lg * float(CR))
    return dm, xi, z, dec, ic


def _ctx_fwd(ctx, cmod6, norm1, win, rlf, rlb):
    lc = ctx.shape[0]

    def body(ctx_ref, cmod_ref, n1_ref, win_ref, rlf_ref, rlb_ref, hc_ref, kvc_ref, scf_ref, scb_ref):
        x = ctx_ref[...]
        r = lax.rsqrt(jnp.mean(x * x, axis=-1, keepdims=True) + EPS)
        hc = (x * r) * (n1_ref[...] * (1.0 + cmod_ref[1:2, :])) + cmod_ref[0:1, :]
        hcb = hc.astype(BF16)
        hc_ref[...] = hcb
        kv = _dot_nt(hcb, win_ref[KV_LO:KV_HI, :])
        k = kv[:, :RW] * KSCALE
        v = kv[:, RW:]
        kvc_ref[:, :RW] = k
        kvc_ref[:, RW:] = v
        t = lax.broadcasted_iota(jnp.int32, (lc, 1), 0).astype(F32)
        lgf = _log_sigmoid(rlf_ref[...])
        lgb = _log_sigmoid(rlb_ref[...])
        for h in range(H):
            hs = slice(h * DH, (h + 1) * DH)
            wf = jnp.exp(lgf[h:h + 1, 0:1] * (lc - 1.0 - t))
            wb = jnp.exp(lgb[h:h + 1, 0:1] * t)
            vh = v[:, hs].astype(BF16)
            scf_ref[h] = _dot_tn((k[:, hs] * wf).astype(BF16), vh)
            scb_ref[h] = _dot_tn((k[:, hs] * wb).astype(BF16), vh)

    return pl.pallas_call(
        body, name="ctx_fwd",
        out_shape=(jax.ShapeDtypeStruct((lc, D), BF16), jax.ShapeDtypeStruct((lc, 2 * RW), F32),
                   jax.ShapeDtypeStruct((H, DH, DH), F32), jax.ShapeDtypeStruct((H, DH, DH), F32)),
        compiler_params=_cparams(),
    )(ctx, cmod6, norm1, win, rlf, rlb)


def _sg_forward(u, v, sgw_ref, sgbt_ref, sgg_ref, nc):
    ua, dgu = _gelu(u)
    va, dgv = _gelu(v)
    gain = sgg_ref[...]
    mixed, vn_b, vah_l, rv_l = [], [], [], []
    for g in range(G):
        gs = slice(g * DH, (g + 1) * DH)
        vag = va[:, gs]
        rv = lax.rsqrt(jnp.mean(vag * vag, axis=-1, keepdims=True) + EPS)
        vah = vag * rv
        vn = (vah * gain[:, gs]).astype(BF16)
        wg = sgw_ref[g].astype(BF16)
        bcol = sgbt_ref[g]
        rows = [_dot(wg, vn[c * C:(c + 1) * C, :]) + bcol for c in range(nc)]
        mixed.append(jnp.concatenate(rows, axis=0) if nc > 1 else rows[0])
        vn_b.append(vn)
        vah_l.append(vah)
        rv_l.append(rv)
    mixed = jnp.concatenate(mixed, axis=1)
    return ua * mixed, (ua, dgu, dgv, mixed, vn_b, vah_l, rv_l)


def _fwd_a(x, mod6, norm1, win, sgw, sgbt, sggain, cos, sin, rlf, scf, *, tm):
    t_len = x.shape[0]
    nt, nc, ncr = t_len // tm, tm // C, tm // CR

    def body(x_ref, mod_ref, n1_ref, win_ref, sgw_ref, sgbt_ref, sgg_ref, cos_ref, sin_ref, rlf_ref, scf_ref,
             hx_ref, zuv_ref, q_ref, k_ref, v_ref, gfb_ref, of_ref, ya_ref, sst_ref, s_scr):
        i = pl.program_id(0)

        @pl.when(i == 0)
        def _():
            s_scr[...] = scf_ref[...]

        x = x_ref[...]
        r = lax.rsqrt(jnp.mean(x * x, axis=-1, keepdims=True) + EPS)
        hx = (x * r) * (n1_ref[...] * (1.0 + mod_ref[1:2, :])) + mod_ref[0:1, :]
        hxb = hx.astype(BF16)
        hx_ref[...] = hxb
        z = _dot_nt(hxb, win_ref[...])
        zuv_ref[...] = z[:, :2 * AW].astype(BF16)
        gfb_ref[...] = z[:, G_LO:INC].astype(BF16)
        cos = jnp.tile(cos_ref[...], (1, H))
        sins = jnp.tile(sin_ref[...], (1, H))
        q_ref[...] = _rope(z[:, Q_LO:KV_LO], cos, sins).astype(BF16)
        k_ref[...] = (_rope(z[:, KV_LO:VR_LO], cos, sins) * KSCALE).astype(BF16)
        v_ref[...] = z[:, VR_LO:G_LO].astype(BF16)
        ya, _ = _sg_forward(z[:, :AW], z[:, AW:2 * AW], sgw_ref, sgbt_ref, sgg_ref, nc)
        ya_ref[...] = ya.astype(BF16)

        lg = _log_sigmoid(rlf_ref[...])
        for h in range(H):
            dm, xi, zc, dec, _ = _decay_consts(lg[h:h + 1, 0:1], True)
            hs = slice(h * DH, (h + 1) * DH)
            for c in range(ncr):
                rs = slice(c * CR, (c + 1) * CR)
                qc, kc, vc = q_ref[rs, hs], k_ref[rs, hs], v_ref[rs, hs]
                s = s_scr[h]
                sst_ref[c, h] = s
                a = (_dot_nt(qc, kc) * dm).astype(BF16)
                of_ref[rs, hs] = (_dot(a, vc) + xi * _dot(qc, s.astype(BF16))).astype(BF16)
                kz = (kc.astype(F32) * zc).astype(BF16)
                s_scr[h] = dec * s + _dot_tn(kz, vc)

    n_chunks = t_len // CR
    return pl.pallas_call(
        body, name="fwd_a", grid=(nt,),
        in_specs=[_rows(tm, D), _whole((6, D)), _whole((1, D)), _whole((INC, D)), _whole((G, C, C)),
                  _whole((G, C, 128)), _whole((1, AW)), _rows(tm, DH), _rows(tm, DH), _whole((H, 128)),
                  _whole((H, DH, DH))],
        out_specs=[_rows(tm, D), _rows(tm, 2 * AW), _rows(tm, RW), _rows(tm, RW), _rows(tm, RW),
                   _rows(tm, 2 * RW), _rows(tm, RW), _rows(tm, AW),
                   pl.BlockSpec((ncr, H, DH, DH), lambda i: (i, 0, 0, 0))],
        out_shape=(jax.ShapeDtypeStruct((t_len, D), BF16), jax.ShapeDtypeStruct((t_len, 2 * AW), BF16),
                   jax.ShapeDtypeStruct((t_len, RW), BF16), jax.ShapeDtypeStruct((t_len, RW), BF16),
                   jax.ShapeDtypeStruct((t_len, RW), BF16), jax.ShapeDtypeStruct((t_len, 2 * RW), BF16),
                   jax.ShapeDtypeStruct((t_len, RW), BF16), jax.ShapeDtypeStruct((t_len, AW), BF16),
                   jax.ShapeDtypeStruct((n_chunks, H, DH, DH), F32)),
        scratch_shapes=[pltpu.VMEM((H, DH, DH), F32)],
        compiler_params=_cparams(1),
    )(x, mod6, norm1, win, sgw, sgbt, sggain, cos, sin, rlf, scf)


def _fwd_b(q, k, v, of, gfb, ya, x, mod6, wout, rlb, scb, *, tm):
    t_len = x.shape[0]
    nt, nc = t_len // tm, tm // CR

    def body(q_ref, k_ref, v_ref, of_ref, gfb_ref, ya_ref, x_ref, mod_ref, wout_ref, rlb_ref, scb_ref,
             ob_ref, yr_ref, x1_ref, rst_ref, r_scr):
        i = pl.program_id(0)

        @pl.when(i == 0)
        def _():
            r_scr[...] = scb_ref[...]

        lg = _log_sigmoid(rlb_ref[...])
        consts = [_decay_consts(lg[h:h + 1, 0:1], False) for h in range(H)]

        def first(c, h):
            dm, _, zc, dec, _ = consts[h]
            hs, rs = slice(h * DH, (h + 1) * DH), slice(c * CR, (c + 1) * CR)
            qc, kc, vc = q_ref[rs, hs], k_ref[rs, hs], v_ref[rs, hs]
            s = r_scr[h]
            rst_ref[c, h] = s
            scores = _dot_nt(qc, kc)
            cross = _dot(qc, s.astype(BF16))
            kz = (kc.astype(F32) * zc).astype(BF16)
            r_scr[h] = dec * s + _dot_tn(kz, vc)
            return (scores * dm).astype(BF16), cross

        def second(c, h, carried):
            a, cross = carried
            hs, rs = slice(h * DH, (h + 1) * DH), slice(c * CR, (c + 1) * CR)
            ob_ref[rs, hs] = (_dot(a, v_ref[rs, hs]) + consts[h][1] * cross).astype(BF16)

        def rows_out(c):
            rs = slice(c * CR, (c + 1) * CR)
            gfb = gfb_ref[rs, :].astype(F32)
            sf, _ = _silu_parts(gfb[:, :RW])
            sb, _ = _silu_parts(gfb[:, RW:])
            hnf, _ = _headnorm(of_ref[rs, :].astype(F32))
            hnb, _ = _headnorm(ob_ref[rs, :].astype(F32))
            yr = (sf * hnf + sb * hnb).astype(BF16)
            yr_ref[rs, :] = yr
            ycat = jnp.concatenate([ya_ref[rs, :], yr], axis=1)
            x1_ref[rs, :] = x_ref[rs, :] + mod_ref[2:3, :] * _dot(ycat, wout_ref[...])

        pending, waiting_rows = None, None
        for c in reversed(range(nc)):
            for h in range(H):
                carried = first(c, h)
                if pending is not None:
                    second(*pending)
                pending = (c, h, carried)
            if waiting_rows is not None:
                rows_out(waiting_rows)
            waiting_rows = c
        second(*pending)
        rows_out(waiting_rows)

    n_chunks = t_len // CR
    rr = functools.partial(_rows_rev, nt=nt)
    return pl.pallas_call(
        body, name="fwd_b", grid=(nt,),
        in_specs=[rr(tm, RW), rr(tm, RW), rr(tm, RW), rr(tm, RW), rr(tm, 2 * RW), rr(tm, AW), rr(tm, D),
                  _whole((6, D)), _whole((D, D)), _whole((H, 128)), _whole((H, DH, DH))],
        out_specs=[rr(tm, RW), rr(tm, RW), rr(tm, D),
                   pl.BlockSpec((nc, H, DH, DH), lambda i: (nt - 1 - i, 0, 0, 0))],
        out_shape=(jax.ShapeDtypeStruct((t_len, RW), BF16), jax.ShapeDtypeStruct((t_len, RW), BF16),
                   jax.ShapeDtypeStruct((t_len, D), F32),
                   jax.ShapeDtypeStruct((n_chunks, H, DH, DH), F32)),
        scratch_shapes=[pltpu.VMEM((H, DH, DH), F32)],
        compiler_params=_cparams(1),
    )(q, k, v, of, gfb, ya, x, mod6, wout, rlb, scb)


def _ffn(x1, tgt, mod6, norm2, normf, wg, wu, wd, *, tm):
    t_len = x1.shape[0]
    nt = t_len // tm

    def body(x1_ref, tgt_ref, mod_ref, n2_ref, nf_ref, wg_ref, wu_ref, wd_ref,
             dx1_ref, h2_ref, da_ref, db_ref, hm_ref, df_ref, small_ref):
        i = pl.program_id(0)

        @pl.when(i == 0)
        def _():
            small_ref[...] = jnp.zeros_like(small_ref)

        sh2, sc2, g2 = mod_ref[3:4, :], mod_ref[4:5, :], mod_ref[5:6, :]
        n2, nf = n2_ref[...], nf_ref[...]
        x1 = x1_ref[...]
        r2 = lax.rsqrt(jnp.mean(x1 * x1, axis=-1, keepdims=True) + EPS)
        x1h = x1 * r2
        w2 = n2 * (1.0 + sc2)
        h2b = (x1h * w2 + sh2).astype(BF16)
        h2_ref[...] = h2b
        a = _dot_nt(h2b, wg_ref[...])
        b = _dot_nt(h2b, wu_ref[...])
        sa, dsa = _silu_parts(a)
        hmb = (sa * b).astype(BF16)
        hm_ref[...] = hmb
        f = _dot(hmb, wd_ref[...])
        x2 = x1 + g2 * f
        r3 = lax.rsqrt(jnp.mean(x2 * x2, axis=-1, keepdims=True) + EPS)
        x2h = x2 * r3
        diff = x2h * nf - tgt_ref[...]
        small_ref[5:6, :] += jnp.sum(diff * diff, axis=0, keepdims=True)
        dout = diff * (1.0 / D)
        small_ref[0:1, :] += jnp.sum(dout * x2h, axis=0, keepdims=True)
        dyg = dout * nf
        dx2 = r3 * (dyg - x2h * jnp.mean(dyg * x2h, axis=-1, keepdims=True))
        small_ref[1:2, :] += jnp.sum(dx2 * f, axis=0, keepdims=True)
        dfb = (dx2 * g2).astype(BF16)
        df_ref[...] = dfb
        dhm = _dot_nt(dfb, wd_ref[...])
        dbb = (dhm * sa).astype(BF16)
        dab = (dhm * b * dsa).astype(BF16)
        da_ref[...] = dab
        db_ref[...] = dbb
        dh2 = _dot(dab, wg_ref[...]) + _dot(dbb, wu_ref[...])
        small_ref[2:3, :] += jnp.sum(dh2, axis=0, keepdims=True)
        dhx = dh2 * x1h
        small_ref[3:4, :] += jnp.sum(dhx, axis=0, keepdims=True) * n2
        small_ref[4:5, :] += jnp.sum(dhx, axis=0, keepdims=True) * (1.0 + sc2)
        dyg2 = dh2 * w2
        dx1_ref[...] = dx2 + r2 * (dyg2 - x1h * jnp.mean(dyg2 * x1h, axis=-1, keepdims=True))

    return pl.pallas_call(
        body, name="ffn", grid=(nt,),
        in_specs=[_rows(tm, D), _rows(tm, D), _whole((6, D)), _whole((1, D)), _whole((1, D)),
                  _whole((DFF, D)), _whole((DFF, D)), _whole((DFF, D))],
        out_specs=[_rows(tm, D), _rows(tm, D), _rows(tm, DFF), _rows(tm, DFF), _rows(tm, DFF), _rows(tm, D),
                   _acc((8, D))],
        out_shape=(jax.ShapeDtypeStruct((t_len, D), F32), jax.ShapeDtypeStruct((t_len, D), BF16),
                   jax.ShapeDtypeStruct((t_len, DFF), BF16), jax.ShapeDtypeStruct((t_len, DFF), BF16),
                   jax.ShapeDtypeStruct((t_len, DFF), BF16), jax.ShapeDtypeStruct((t_len, D), BF16),
                   jax.ShapeDtypeStruct((8, D), F32)),
        compiler_params=_cparams(1),
    )(x1, tgt, mod6, norm2, normf, wg, wu, wd)


def _mid_bwd(dx1, of, ob, gfb, mod6, wout, *, tm):
    t_len = dx1.shape[0]
    nt = t_len // tm
    rc = min(256, tm)

    def body(dx1_ref, of_ref, ob_ref, gfb_ref, mod_ref, wout_ref,
             dya_ref, dgfb_ref, dof_ref, dob_ref):
        for c in range(tm // rc):
            rows = slice(c * rc, (c + 1) * rc)
            dyb = (dx1_ref[rows, :] * mod_ref[2:3, :]).astype(BF16)
            dycat = _dot_nt(dyb, wout_ref[...])
            dya_ref[rows, :] = dycat[:, :AW].astype(BF16)
            dyr = dycat[:, AW:]
            gfb = gfb_ref[rows, :].astype(F32)
            for o_ref, do_ref, lo in ((of_ref, dof_ref, 0), (ob_ref, dob_ref, RW)):
                sg, dsg = _silu_parts(gfb[:, lo:lo + RW])
                o = o_ref[rows, :].astype(F32)
                hn, rs = _headnorm(o)
                dgfb_ref[rows, lo:lo + RW] = (dyr * hn * dsg).astype(BF16)
                dhn = dyr * sg
                for h in range(H):
                    hs = slice(h * DH, (h + 1) * DH)
                    oh, dh = hn[:, hs], dhn[:, hs]
                    do_ref[rows, hs] = (rs[h] * (dh - oh * jnp.mean(dh * oh, axis=-1, keepdims=True))).astype(BF16)

    return pl.pallas_call(
        body, name="mid_bwd", grid=(nt,),
        in_specs=[_rows(tm, D), _rows(tm, RW), _rows(tm, RW), _rows(tm, 2 * RW), _whole((6, D)), _whole((D, D))],
        out_specs=[_rows(tm, AW), _rows(tm, 2 * RW), _rows(tm, RW), _rows(tm, RW)],
        out_shape=(jax.ShapeDtypeStruct((t_len, AW), BF16), jax.ShapeDtypeStruct((t_len, 2 * RW), BF16),
                   jax.ShapeDtypeStruct((t_len, RW), BF16), jax.ShapeDtypeStruct((t_len, RW), BF16)),
        compiler_params=_cparams(1),
    )(dx1, of, ob, gfb, mod6, wout)


def _ret_bwd_items(q_ref, k_ref, v_ref, do_ref, st_ref, dq_ref, dk_ref, dv_ref, ds_scr, dlg_scr, lg, forward, nc, row0):
    order = list(reversed(range(nc))) if forward else list(range(nc))
    consts = [_decay_consts(lg[h:h + 1, 0:1], forward) for h in range(H)]
    accs = [jnp.zeros((1, DH), F32) for _ in range(H)]

    def first(h, c):
        dm, xi, zc, dec, _ = consts[h]
        hs, rs = slice(h * DH, (h + 1) * DH), slice(c * CR, (c + 1) * CR)
        qc, kc, vc, doc = q_ref[rs, hs], k_ref[rs, hs], v_ref[rs, hs], do_ref[rs, hs]
        s, dsn = st_ref[c, h], ds_scr[h]
        sb, dsnb = s.astype(BF16), dsn.astype(BF16)
        qf, kf = qc.astype(F32), kc.astype(F32)
        a_raw = _dot_nt(qc, kc)
        da_raw = _dot_nt(doc, vc)
        xdo = (xi * doc.astype(F32)).astype(BF16)
        kz = (kf * zc).astype(BF16)
        vz = (vc.astype(F32) * zc).astype(BF16)
        dq_c = _dot_nt(xdo, sb)
        dk_s = _dot_nt(vz, dsnb)
        dv_s = _dot(kz, dsnb)
        ds_scr[h] = _dot_tn((xi * qf).astype(BF16), doc) + dec * dsn
        accs[h] = accs[h] + (float(CR) * dec) * jnp.sum(s * dsn, axis=0, keepdims=True)
        a = (a_raw * dm).astype(BF16)
        da = (da_raw * dm).astype(BF16)
        return a, da, dq_c, dk_s, dv_s

    def second(h, c, carried):
        a, da, dq_c, dk_s, dv_s = carried
        ic = consts[h][4]
        if forward:
            w_qi, w_qc, w_ki, w_ks = ic, ic + 1.0, -ic, (CR - 1.0) - ic
        else:
            w_qi, w_qc, w_ki, w_ks = -ic, CR - ic, ic, ic
        hs, rs = slice(h * DH, (h + 1) * DH), slice(c * CR, (c + 1) * CR)
        qc, kc, doc = q_ref[rs, hs], k_ref[rs, hs], do_ref[rs, hs]
        dq_i = _dot(da, kc)
        dk_i = _dot_tn(da, qc)
        dv_i = _dot_tn(a, doc)
        dq_ref[rs, hs] = (dq_i + dq_c).astype(BF16)
        dk_ref[rs, hs] = (dk_i + dk_s).astype(BF16)
        dv_ref[rs, hs] = (dv_i + dv_s).astype(BF16)
        rowterm = qc.astype(F32) * (w_qi * dq_i + w_qc * dq_c) + kc.astype(F32) * (w_ki * dk_i + w_ks * dk_s)
        accs[h] = accs[h] + jnp.sum(rowterm, axis=0, keepdims=True)

    def finish():
        for h in range(H):
            dlg_scr[row0 + h:row0 + h + 1, :] += accs[h]

    items = [[(functools.partial(first, h, c), functools.partial(second, h, c)) for h in range(H)] for c in order]
    return items, finish


def _ret_bwd_run(dirs):
    nc = len(dirs[0][0])
    flat = [it for j in range(nc) for items, _ in dirs for it in items[j]]
    pending = None
    for first, second in flat:
        carried = first()
        if pending is not None:
            pending[0](pending[1])
        pending = (second, carried)
    pending[0](pending[1])
    for _, finish in dirs:
        finish()


def _ret_bwd(q, k, v, dof, dob, sst, rst, rlf, rlb, *, tm):
    t_len = q.shape[0]
    nt, nc = t_len // tm, tm // CR

    def body(qf_ref, kf_ref, vf_ref, dof_ref, sst_ref, qb_ref, kb_ref, vb_ref, dob_ref, rst_ref, rlf_ref, rlb_ref,
             dqf_ref, dkf_ref, dvf_ref, dqb_ref, dkb_ref, dvb_ref, dscf_ref, dscb_ref, dlg_ref,
             dsf_scr, dsb_scr, dlg_scr):
        i = pl.program_id(0)

        @pl.when(i == 0)
        def _():
            dsf_scr[...] = jnp.zeros_like(dsf_scr)
            dsb_scr[...] = jnp.zeros_like(dsb_scr)
            dlg_scr[...] = jnp.zeros_like(dlg_scr)

        lgf = _log_sigmoid(rlf_ref[...])
        lgb = _log_sigmoid(rlb_ref[...])
        _ret_bwd_run([
            _ret_bwd_items(qf_ref, kf_ref, vf_ref, dof_ref, sst_ref, dqf_ref, dkf_ref, dvf_ref, dsf_scr, dlg_scr,
                           lgf, True, nc, 0),
            _ret_bwd_items(qb_ref, kb_ref, vb_ref, dob_ref, rst_ref, dqb_ref, dkb_ref, dvb_ref, dsb_scr, dlg_scr,
                           lgb, False, nc, H)])

        @pl.when(i == nt - 1)
        def _():
            dscf_ref[...] = dsf_scr[...]
            dscb_ref[...] = dsb_scr[...]
            tot = jnp.broadcast_to(jnp.sum(dlg_scr[...], axis=1, keepdims=True), (8, 128))
            ri = lax.broadcasted_iota(jnp.int32, (8, 128), 0)
            li = lax.broadcasted_iota(jnp.int32, (8, 128), 1)
            dlg_ref[...] = jnp.zeros_like(dlg_ref)
            dlg_ref[0:1, 0:128] = jnp.sum(jnp.where(ri == li, tot, 0.0), axis=0, keepdims=True)
            dlg_ref[1:2, 0:128] = jnp.sum(jnp.where(ri - H == li, tot, 0.0), axis=0, keepdims=True)

    rr = functools.partial(_rows_rev, nt=nt)
    st_f = pl.BlockSpec((nc, H, DH, DH), lambda i: (nt - 1 - i, 0, 0, 0))
    st_b = pl.BlockSpec((nc, H, DH, DH), lambda i: (i, 0, 0, 0))
    tok = jax.ShapeDtypeStruct((t_len, RW), BF16)
    st = jax.ShapeDtypeStruct((H, DH, DH), F32)
    return pl.pallas_call(
        body, name="ret_bwd", grid=(nt,),
        in_specs=[rr(tm, RW), rr(tm, RW), rr(tm, RW), rr(tm, RW), st_f,
                  _rows(tm, RW), _rows(tm, RW), _rows(tm, RW), _rows(tm, RW), st_b,
                  _whole((H, 128)), _whole((H, 128))],
        out_specs=[rr(tm, RW), rr(tm, RW), rr(tm, RW), _rows(tm, RW), _rows(tm, RW), _rows(tm, RW),
                   _acc((H, DH, DH)), _acc((H, DH, DH)), _acc((8, D))],
        out_shape=(tok, tok, tok, tok, tok, tok, st, st, jax.ShapeDtypeStruct((8, D), F32)),
        scratch_shapes=[pltpu.VMEM((H, DH, DH), F32), pltpu.VMEM((H, DH, DH), F32), pltpu.VMEM((8, 128), F32)],
        compiler_params=_cparams(1),
    )(q, k, v, dof, sst, q, k, v, dob, rst, rlf, rlb)


def _in_bwd(x, zuv, dya, dqf, dkf, dvf, dqb, dkb, dvb, dgfb, dx1, cos, sin, mod6, norm1, win, sgw, sgbt, sggain, *, tm):
    t_len = x.shape[0]
    nt, nc = t_len // tm, tm // C

    def body(x_ref, zuv_ref, dya_ref, dqf_ref, dkf_ref, dvf_ref, dqb_ref, dkb_ref, dvb_ref, dgfb_ref, dx1_ref,
             cos_ref, sin_ref, mod_ref, n1_ref, win_ref, sgw_ref, sgbt_ref, sgg_ref,
             gx_ref, dz_ref, small_ref, dsgw_ref, dsgbt_ref):
        i = pl.program_id(0)

        @pl.when(i == 0)
        def _():
            small_ref[...] = jnp.zeros_like(small_ref)
            dsgw_ref[...] = jnp.zeros_like(dsgw_ref)
            dsgbt_ref[...] = jnp.zeros_like(dsgbt_ref)

        zuv = zuv_ref[...].astype(F32)
        _, (ua, dgu, dgv, mixed, vn_b, vah_l, rv_l) = _sg_forward(zuv[:, :AW], zuv[:, AW:], sgw_ref, sgbt_ref, sgg_ref, nc)
        dya = dya_ref[...].astype(F32)
        dz_ref[:, 0:AW] = (dya * mixed * dgu).astype(BF16)
        dmixed = dya * ua
        gain = sgg_ref[...]
        for g in range(G):
            gs = slice(g * DH, (g + 1) * DH)
            dmg = dmixed[:, gs]
            dmb = dmg.astype(BF16)
            wgt = sgw_ref[g].astype(BF16)
            dsw = jnp.zeros((C, C), F32)
            dsb = jnp.zeros((C, DH), F32)
            rows = []
            for c in range(nc):
                rs = slice(c * C, (c + 1) * C)
                dsw = dsw + _dot_nt(dmb[rs, :], vn_b[g][rs, :])
                dsb = dsb + dmg[rs, :]
                rows.append(_dot_tn(wgt, dmb[rs, :]))
            dsgw_ref[g] += dsw
            dsgbt_ref[g] += dsb
            dvn = jnp.concatenate(rows, axis=0) if nc > 1 else rows[0]
            vah, rv = vah_l[g], rv_l[g]
            small_ref[3:4, gs] += jnp.sum(dvn * vah, axis=0, keepdims=True)
            dyg = dvn * gain[:, gs]
            dva = rv * (dyg - vah * jnp.mean(dyg * vah, axis=-1, keepdims=True))
            dz_ref[:, AW + g * DH:AW + (g + 1) * DH] = (dva * dgv[:, gs]).astype(BF16)

        cos = jnp.tile(cos_ref[...], (1, H))
        nsins = -jnp.tile(sin_ref[...], (1, H))
        def both(f_ref, b_ref):
            return f_ref[...].astype(F32) + b_ref[...].astype(F32)

        dz_ref[:, Q_LO:KV_LO] = _rope(both(dqf_ref, dqb_ref), cos, nsins).astype(BF16)
        dz_ref[:, KV_LO:VR_LO] = (_rope(both(dkf_ref, dkb_ref), cos, nsins) * KSCALE).astype(BF16)
        dz_ref[:, VR_LO:G_LO] = both(dvf_ref, dvb_ref).astype(BF16)
        dz_ref[:, G_LO:INC] = dgfb_ref[...]

        dhx = _dot(dz_ref[...], win_ref[...])
        x = x_ref[...]
        r = lax.rsqrt(jnp.mean(x * x, axis=-1, keepdims=True) + EPS)
        xh = x * r
        n1, sc1 = n1_ref[...], mod_ref[1:2, :]
        small_ref[0:1, :] += jnp.sum(dhx, axis=0, keepdims=True)
        dhxx = jnp.sum(dhx * xh, axis=0, keepdims=True)
        small_ref[1:2, :] += dhxx * n1
        small_ref[2:3, :] += dhxx * (1.0 + sc1)
        dyg = dhx * (n1 * (1.0 + sc1))
        gx_ref[...] = dx1_ref[...] + r * (dyg - xh * jnp.mean(dyg * xh, axis=-1, keepdims=True))

        @pl.when(i == nt - 1)
        def _():
            ri = lax.broadcasted_iota(jnp.int32, (C, DH), 0)
            li = lax.broadcasted_iota(jnp.int32, (C, DH), 1)
            for g in range(G):
                tot = jnp.broadcast_to(jnp.sum(dsgbt_ref[g], axis=1, keepdims=True), (C, DH))
                small_ref[4 + g:5 + g, 0:DH] = jnp.sum(jnp.where(ri == li, tot, 0.0), axis=0, keepdims=True)

    tok = functools.partial(_rows, tm)
    return pl.pallas_call(
        body, name="in_bwd", grid=(nt,),
        in_specs=[tok(D), tok(2 * AW), tok(AW), tok(RW), tok(RW), tok(RW), tok(RW), tok(RW), tok(RW),
                  tok(2 * RW), tok(D), tok(DH), tok(DH), _whole((6, D)), _whole((1, D)), _whole((INC, D)),
                  _whole((G, C, C)), _whole((G, C, 128)), _whole((1, AW))],
        out_specs=[tok(D), tok(INC), _acc((8, D)), _acc((G, C, C))],
        out_shape=(jax.ShapeDtypeStruct((t_len, D), F32), jax.ShapeDtypeStruct((t_len, INC), BF16),
                   jax.ShapeDtypeStruct((8, D), F32), jax.ShapeDtypeStruct((G, C, C), F32)),
        scratch_shapes=[pltpu.VMEM((G, C, 128), F32)],
        compiler_params=_cparams(1),
    )(x, zuv, dya, dqf, dkf, dvf, dqb, dkb, dvb, dgfb, dx1, cos, sin, mod6, norm1, win, sgw, sgbt, sggain)


def _tn_matmul(a, b, *, bm, bt, name, init=None, init_rows=None, after=(), cols=None):
    t_len, m = a.shape
    cj, n = (0, b.shape[1]) if cols is None else cols
    ni, ntt = m // bm, t_len // bt
    has_init = init is not None

    def body(*refs):
        o_ref, ob_ref = refs[-2:]
        a_ref, b_ref = refs[:2]
        init_ref = refs[2] if has_init else None
        i, t = pl.program_id(0), pl.program_id(1)

        @pl.when(t == 0)
        def _():
            o_ref[...] = jnp.zeros_like(o_ref)

        if has_init:
            for ib in range(ni):
                lo, hi = max(init_rows[0], ib * bm), min(init_rows[1], (ib + 1) * bm)
                if lo < hi:
                    @pl.when(jnp.logical_and(t == 0, i == ib))
                    def _(lo=lo, hi=hi, ib=ib):
                        o_ref[lo - ib * bm:hi - ib * bm, :] = init_ref[lo - init_rows[0]:hi - init_rows[0], :]

        o_ref[...] += _dot_tn(a_ref[...], b_ref[...])

        @pl.when(t == ntt - 1)
        def _():
            ob_ref[...] = o_ref[...].astype(BF16)

    in_specs = [pl.BlockSpec((bt, bm), lambda i, t: (t, i)), pl.BlockSpec((bt, n), lambda i, t: (t, cj))]
    args = [a, b]
    if has_init:
        in_specs.append(pl.BlockSpec((init.shape[0], n), lambda i, t: (0, cj), pipeline_mode=pl.Buffered(1)))
        args.append(init)
    for arr in after:
        in_specs.append(pl.BlockSpec(memory_space=pl.ANY))
        args.append(arr)
    out_spec = pl.BlockSpec((bm, n), lambda i, t: (i, 0))
    return pl.pallas_call(
        body, name=name, grid=(ni, ntt),
        in_specs=in_specs,
        out_specs=[out_spec, out_spec],
        out_shape=(jax.ShapeDtypeStruct((m, n), F32), jax.ShapeDtypeStruct((m, n), BF16)),
        compiler_params=_cparams(2),
    )(*args)


def _dw_out(ya, yr, dx1, mod6, wout, *, bt):
    t_len = dx1.shape[0]
    ntt = t_len // bt

    def body(ya_ref, yr_ref, dx_ref, mod_ref, wout_ref, o_ref, ob_ref, dg1_ref):
        t = pl.program_id(0)

        @pl.when(t == 0)
        def _():
            o_ref[...] = jnp.zeros_like(o_ref)

        dxb = dx_ref[...].astype(BF16)
        o_ref[0:AW, :] += _dot_tn(ya_ref[...], dxb)
        o_ref[AW:D, :] += _dot_tn(yr_ref[...], dxb)

        @pl.when(t == ntt - 1)
        def _():
            m = o_ref[...]
            dg1_ref[...] = jnp.zeros_like(dg1_ref)
            dg1_ref[0:1, :] = jnp.sum(m * wout_ref[...].astype(F32), axis=0, keepdims=True)
            g = m * mod_ref[2:3, :]
            o_ref[...] = g
            ob_ref[...] = g.astype(BF16)

    return pl.pallas_call(
        body, name="dw_out", grid=(ntt,),
        in_specs=[pl.BlockSpec((bt, AW), lambda t: (t, 0)), pl.BlockSpec((bt, RW), lambda t: (t, 0)),
                  pl.BlockSpec((bt, D), lambda t: (t, 0)), _whole((6, D)), _whole((D, D))],
        out_specs=[_acc((D, D)), _acc((D, D)), _acc((8, D))],
        out_shape=(jax.ShapeDtypeStruct((D, D), F32), jax.ShapeDtypeStruct((D, D), BF16),
                   jax.ShapeDtypeStruct((8, D), F32)),
        compiler_params=_cparams(1),
    )(ya, yr, dx1, mod6, wout)


def _ctx_bwd(ctx, hc, kvc, cmod6, norm1, win, rlf, rlb, dscf, dscb, dlg_in):
    lc = ctx.shape[0]

    def body(ctx_ref, hc_ref, kvc_ref, cmod_ref, n1_ref, win_ref, rlf_ref, rlb_ref, dscf_ref, dscb_ref, dlgin_ref,
             dwin_ref, small_ref, dlg_ref):
        k = kvc_ref[:, :RW]
        v = kvc_ref[:, RW:]
        t = lax.broadcasted_iota(jnp.int32, (lc, 1), 0).astype(F32)
        lgf = _log_sigmoid(rlf_ref[...])
        lgb = _log_sigmoid(rlb_ref[...])
        dks, dvs = [], []
        lane = lax.broadcasted_iota(jnp.int32, (1, 128), 1)
        row_f = jnp.zeros((1, 128), F32)
        row_b = jnp.zeros((1, 128), F32)
        for h in range(H):
            hs = slice(h * DH, (h + 1) * DH)
            wf = jnp.exp(lgf[h:h + 1, 0:1] * (lc - 1.0 - t))
            wb = jnp.exp(lgb[h:h + 1, 0:1] * t)
            kh, vhb = k[:, hs], v[:, hs].astype(BF16)
            dsf, dsb = dscf_ref[h].astype(BF16), dscb_ref[h].astype(BF16)
            dkf = wf * _dot_nt(vhb, dsf)
            dkb = wb * _dot_nt(vhb, dsb)
            dvs.append(_dot((kh * wf).astype(BF16), dsf) + _dot((kh * wb).astype(BF16), dsb))
            dks.append((dkf + dkb) * KSCALE)
            lf = jnp.sum((lc - 1.0 - t) * kh * dkf, axis=0, keepdims=True)
            lb = jnp.sum(t * kh * dkb, axis=0, keepdims=True)
            row_f = row_f + jnp.where(lane == h, jnp.sum(lf, axis=1, keepdims=True), 0.0)
            row_b = row_b + jnp.where(lane == h, jnp.sum(lb, axis=1, keepdims=True), 0.0)
        dlg_ref[...] = dlgin_ref[...]
        dlg_ref[0:1, 0:128] += row_f
        dlg_ref[1:2, 0:128] += row_b
        dkv = jnp.concatenate(dks + dvs, axis=1).astype(BF16)
        dhc = _dot(dkv, win_ref[KV_LO:KV_HI, :])
        dwin_ref[...] = _dot_tn(dkv, hc_ref[...])
        x = ctx_ref[...]
        r = lax.rsqrt(jnp.mean(x * x, axis=-1, keepdims=True) + EPS)
        xh = x * r
        small_ref[...] = jnp.zeros_like(small_ref)
        small_ref[0:1, :] = jnp.sum(dhc, axis=0, keepdims=True)
        dhxx = jnp.sum(dhc * xh, axis=0, keepdims=True)
        small_ref[1:2, :] = dhxx * n1_ref[...]
        small_ref[2:3, :] = dhxx * (1.0 + cmod_ref[1:2, :])

    return pl.pallas_call(
        body, name="ctx_bwd",
        out_shape=(jax.ShapeDtypeStruct((2 * RW, D), F32), jax.ShapeDtypeStruct((8, D), F32),
                   jax.ShapeDtypeStruct((8, D), F32)),
        compiler_params=_cparams(),
    )(ctx, hc, kvc, cmod6, norm1, win, rlf, rlb, dscf, dscb, dlg_in)


def _adam_math(w, g, m, v):
    m = ADAM_B1 * m + (1.0 - ADAM_B1) * g
    v = ADAM_B2 * v + (1.0 - ADAM_B2) * (g * g)
    m_hat = m / (1.0 - ADAM_B1 ** ADAM_STEP)
    v_hat = v / (1.0 - ADAM_B2 ** ADAM_STEP)
    delta = -ADAM_LR * (m_hat / (jnp.sqrt(v_hat) + ADAM_EPS) + ADAM_WD * w)
    return delta, m, v


def _place():
    x, y, c = lax.axis_index("x"), lax.axis_index("y"), lax.axis_index("c")
    return x, y, c, 4 * x + 2 * y + c


def _flip(x, y, c, k):
    px = 1 - x if (k >> 2) & 1 else x
    py = 1 - y if (k >> 1) & 1 else y
    pc = 1 - c if k & 1 else c
    return (px, py, pc), 4 * px + 2 * py + pc


def _gather_copy(buf_ref, send_sems, recv_sems, sem0, k, mine):
    x, y, c, me = _place()
    dev, idx = _flip(x, y, c, k)
    blk = me if mine else idx
    return pltpu.make_async_remote_copy(
        src_ref=buf_ref.at[blk], dst_ref=buf_ref.at[blk],
        send_sem=send_sems.at[sem0 + k - 1], recv_sem=recv_sems.at[sem0 + k - 1],
        device_id=dev, device_id_type=MESH)


def _entry_gather(c_row, cctx_row, wmod, bmod, shards):
    n = len(shards)
    ncol = wmod.shape[1]

    def body(*refs):
        c_ref, cctx_ref, wmod_ref, bmod_ref = refs[:4]
        in_refs = refs[4:4 + n]
        mod_ref, cs_ref = refs[4 + n:6 + n]
        out_refs = refs[6 + n:6 + 2 * n]
        cbuf, pbuf = refs[6 + 2 * n:8 + 2 * n]
        cast_refs = refs[8 + 2 * n:8 + 3 * n]
        small_send, small_recv, send_sems, recv_sems, local_sems = refs[8 + 3 * n:]
        x, y, c, me = _place()
        sib = (x, y, 1 - c)
        chips = [(1 - x, y), (x, 1 - y), (1 - x, 1 - y)]

        cbuf[me] = jnp.broadcast_to(c_ref[...], (8, D))
        small = [_gather_copy(cbuf, small_send, small_recv, 0, k, True) for k in range(1, NDEV)]
        for cp in small:
            cp.start()

        def blk(px, py, pc):
            return 4 * px + 2 * py + pc

        def copy(w, k, block, to, src=None):
            dst = out_refs[w].at[block]
            return pltpu.make_async_remote_copy(
                src_ref=dst if src is None else src, dst_ref=dst,
                send_sem=send_sems.at[w, k], recv_sem=recv_sems.at[w, k], device_id=to, device_id_type=MESH)

        first, passed, mine = [], [], []
        for w in range(n):
            cast_refs[w][...] = in_refs[w][...].astype(BF16)
            m = pltpu.make_async_copy(cast_refs[w], out_refs[w].at[me], local_sems.at[w])
            m.start()
            mine.append(m)
            cps = [copy(w, 0, me, sib, src=cast_refs[w])]
            cps += [copy(w, 1 + j, me, (*chip, c), src=cast_refs[w]) for j, chip in enumerate(chips)]
            for cp in cps:
                cp.start()
            first += cps

        for k in range(1, NDEV):
            _gather_copy(cbuf, small_send, small_recv, 0, k, False).wait_recv()
        row = lax.broadcasted_iota(jnp.int32, (16, D), 0)
        call = jnp.where(row == 8, jnp.broadcast_to(cctx_ref[...], (16, D)), 0.0)
        for d in range(NDEV):
            call = jnp.where(row == d, jnp.concatenate([cbuf[d], cbuf[d]], axis=0), call)
        cs = call * _sigmoid(call)
        cs_ref[...] = cs
        pbuf[me] = _dot(cs.astype(BF16), wmod_ref[...].astype(BF16))
        small2 = [_gather_copy(pbuf, small_send, small_recv, NDEV - 1, k, True) for k in range(1, NDEV)]
        for cp in small2:
            cp.start()

        for w in range(n):
            for j, chip in enumerate(chips):
                b = blk(*chip, c)
                copy(w, 1 + j, b, (x, y, c)).wait_recv()
                fw = copy(w, 4 + j, b, sib)
                fw.start()
                passed.append(fw)
        for w in range(n):
            copy(w, 0, blk(x, y, 1 - c), (x, y, c)).wait_recv()
            for j, chip in enumerate(chips):
                copy(w, 4 + j, blk(*chip, 1 - c), (x, y, c)).wait_recv()

        for k in range(1, NDEV):
            _gather_copy(pbuf, small_send, small_recv, NDEV - 1, k, False).wait_recv()
        for d in range(NDEV):
            mod_ref[:, d * ncol:(d + 1) * ncol] = pbuf[d] + bmod_ref[:, d * ncol:(d + 1) * ncol]
        for cp in small + small2 + first + passed:
            cp.wait_send()
        for m in mine:
            m.wait()

    vm = pl.BlockSpec(memory_space=pltpu.VMEM)
    hbm = pl.BlockSpec(memory_space=pltpu.HBM)
    return pl.pallas_call(
        body, name="entry_gather",
        in_specs=[vm] * (4 + n), out_specs=[vm, vm] + [hbm] * n,
        out_shape=(jax.ShapeDtypeStruct((16, 6 * D), F32), jax.ShapeDtypeStruct((16, D), F32))
        + tuple(jax.ShapeDtypeStruct((NDEV,) + s.shape, BF16) for s in shards),
        scratch_shapes=[pltpu.VMEM((NDEV, 8, D), F32), pltpu.VMEM((NDEV, 16, ncol), F32)]
        + [pltpu.VMEM(s.shape, BF16) for s in shards]
        + [pltpu.SemaphoreType.DMA((2 * (NDEV - 1),)), pltpu.SemaphoreType.DMA((2 * (NDEV - 1),)),
           pltpu.SemaphoreType.DMA((n, 7)), pltpu.SemaphoreType.DMA((n, 7)), pltpu.SemaphoreType.DMA((n,))],
        compiler_params=_cparams(),
    )(c_row, cctx_row, wmod, bmod, *shards)


_HBM = pl.BlockSpec(memory_space=pltpu.HBM)
_SEMS = pl.BlockSpec(memory_space=pltpu.SEMAPHORE)
_EFFECT = pltpu.SideEffectType.DATAFLOW_SIDE_EFFECTING


def _exchange_copy(src_refs, land_refs, send_sems, recv_sems, w, k, scatter, landing_slot_is_mine):
    x, y, c, me = _place()
    dev, pidx = _flip(x, y, c, k)
    src = src_refs[w].at[pidx] if scatter else src_refs[w]
    slot = me if landing_slot_is_mine else pidx
    return pltpu.make_async_remote_copy(
        src_ref=src, dst_ref=land_refs[w].at[slot],
        send_sem=send_sems.at[w * (NDEV - 1) + k - 1], recv_sem=recv_sems.at[w * (NDEV - 1) + k - 1],
        device_id=dev, device_id_type=MESH)


def _exchange_start(srcs, *, scatter, name):
    n = len(srcs)
    lands = [lax.empty((NDEV,) + tuple(s.shape[-2:]), s.dtype) for s in srcs]

    def body(*refs):
        src_refs, land_refs = refs[:n], refs[n:2 * n]
        send_sems, recv_sems = refs[2 * n], refs[2 * n + 1]
        token = refs[-1]
        for w in range(n):
            for k in range(1, NDEV):
                _exchange_copy(src_refs, land_refs, send_sems, recv_sems, w, k, scatter, True).start()
        token[...] = jnp.zeros_like(token)

    arrs = list(srcs) + lands
    out_shape = ([pltpu.SemaphoreType.DMA((n * (NDEV - 1),)), pltpu.SemaphoreType.DMA((n * (NDEV - 1),))]
                 + [pltpu.HBM(a.shape, a.dtype) for a in arrs] + [jax.ShapeDtypeStruct((8, 128), F32)])
    res = pl.pallas_call(
        body, name=name, out_shape=tuple(out_shape),
        in_specs=[_HBM] * (2 * n),
        out_specs=tuple([_SEMS, _SEMS] + [_HBM] * (2 * n) + [pl.BlockSpec(memory_space=pltpu.VMEM)]),
        input_output_aliases={i: 2 + i for i in range(2 * n)},
        compiler_params=pltpu.CompilerParams(has_side_effects=_EFFECT),
    )(*[pltpu.with_memory_space_constraint(a, pltpu.HBM) for a in arrs])
    return res[:-1], res[-1]


def _exchange_wait(handles, after, *, scatter, name):
    n = (len(handles) - 2) // 2
    na = len(after)

    def body(*refs):
        src_refs, land_refs = refs[:n], refs[n:2 * n]
        send_sems, recv_sems = refs[2 * n], refs[2 * n + 1]
        for w in range(n):
            for k in range(1, NDEV):
                cp = _exchange_copy(src_refs, land_refs, send_sems, recv_sems, w, k, scatter, False)
                cp.wait_send()
                cp.wait_recv()

    arrs = handles[2:]
    res = pl.pallas_call(
        body, name=name, out_shape=tuple(pltpu.HBM(a.shape, a.dtype) for a in arrs),
        in_specs=[_HBM] * (2 * n) + [_SEMS, _SEMS] + [_HBM] * na,
        out_specs=tuple([_HBM] * (2 * n)),
        input_output_aliases={i: i for i in range(2 * n)},
        compiler_params=pltpu.CompilerParams(has_side_effects=_EFFECT),
    )(*arrs, handles[0], handles[1], *[pltpu.with_memory_space_constraint(a, pltpu.HBM) for a in after])
    return res[:n], res[n:]


_SAME_CORE = (2, 4, 6)


def _gather2_copy(src_ref, land_ref, send_sems, recv_sems, sem, k, block):
    x, y, c, _ = _place()
    dev, _ = _flip(x, y, c, k)
    dst = land_ref.at[block]
    return pltpu.make_async_remote_copy(
        src_ref=dst if src_ref is None else src_ref, dst_ref=dst,
        send_sem=send_sems.at[sem], recv_sem=recv_sems.at[sem], device_id=dev, device_id_type=MESH)


def _split_call(body, name, arrs, n_sems, sems=None, after=(), token=False):
    na = len(arrs)
    out_shape, out_specs = [], []
    if n_sems is not None:
        out_shape += [pltpu.SemaphoreType.DMA((n_sems,)), pltpu.SemaphoreType.DMA((n_sems,))]
        out_specs += [_SEMS, _SEMS]
    first = len(out_shape)
    out_shape += [pltpu.HBM(a.shape, a.dtype) for a in arrs]
    out_specs += [_HBM] * na
    if token:
        out_shape.append(jax.ShapeDtypeStruct((8, 128), F32))
        out_specs.append(pl.BlockSpec(memory_space=pltpu.VMEM))
    in_specs = [_HBM] * na + ([_SEMS, _SEMS] if sems is not None else []) + [_HBM] * len(after)
    args = [pltpu.with_memory_space_constraint(a, pltpu.HBM) for a in arrs] + list(sems or ()) \
        + [pltpu.with_memory_space_constraint(a, pltpu.HBM) for a in after]
    return pl.pallas_call(
        body, name=name, out_shape=tuple(out_shape), in_specs=in_specs, out_specs=tuple(out_specs),
        input_output_aliases={i: first + i for i in range(na)},
        compiler_params=pltpu.CompilerParams(has_side_effects=_EFFECT))(*args)


def _gather2_start(srcs, *, name):
    n = len(srcs)
    lands = [lax.empty((NDEV,) + tuple(s.shape), s.dtype) for s in srcs]

    def body(*refs):
        src_refs, land_refs = refs[:n], refs[n:2 * n]
        send_sems, recv_sems = refs[2 * n], refs[2 * n + 1]
        _, _, _, me = _place()
        for w in range(n):
            for slot, k in enumerate((1,) + _SAME_CORE):
                _gather2_copy(src_refs[w], land_refs[w], send_sems, recv_sems, 4 * w + slot, k, me).start()
        refs[-1][...] = jnp.zeros_like(refs[-1])

    res = _split_call(body, name, list(srcs) + lands, 4 * n, token=True)
    return res[:2], res[2:-1], res[-1]


def _gather2_arrived(sems, arrs, after, *, name):
    n = len(arrs) // 2

    def body(*refs):
        src_refs, land_refs = refs[:n], refs[n:2 * n]
        send_sems, recv_sems = refs[2 * n], refs[2 * n + 1]
        x, y, c, me = _place()
        for w in range(n):
            for slot, k in enumerate((1,) + _SAME_CORE):
                _, pidx = _flip(x, y, c, k)
                _gather2_copy(src_refs[w], land_refs[w], send_sems, recv_sems, 4 * w + slot, k, me).wait_send()
                _gather2_copy(None, land_refs[w], send_sems, recv_sems, 4 * w + slot, k, pidx).wait_recv()

    return _split_call(body, name, arrs, None, sems=sems, after=after)


def _gather2_forward(lands, *, name):
    n = len(lands)

    def body(*refs):
        land_refs = refs[:n]
        send_sems, recv_sems = refs[n], refs[n + 1]
        x, y, c, _ = _place()
        for w in range(n):
            for j, k in enumerate(_SAME_CORE):
                _, pidx = _flip(x, y, c, k)
                _gather2_copy(None, land_refs[w], send_sems, recv_sems, 3 * w + j, 1, pidx).start()
        refs[-1][...] = jnp.zeros_like(refs[-1])

    res = _split_call(body, name, list(lands), 3 * n, token=True)
    return res[:2], res[2:-1], res[-1]


def _gather2_wait(sems, lands, after, *, name):
    n = len(lands)

    def body(*refs):
        land_refs = refs[:n]
        send_sems, recv_sems = refs[n], refs[n + 1]
        x, y, c, _ = _place()
        for w in range(n):
            for j, k in enumerate(_SAME_CORE):
                _, mine = _flip(x, y, c, k)
                _, theirs = _flip(x, y, c, k ^ 1)
                _gather2_copy(None, land_refs[w], send_sems, recv_sems, 3 * w + j, 1, mine).wait_send()
                _gather2_copy(None, land_refs[w], send_sems, recv_sems, 3 * w + j, 1, theirs).wait_recv()

    return _split_call(body, name, list(lands), None, sems=sems, after=after)


def _cast_bf16(arrs, *, name, after=()):
    n = len(arrs)

    def body(*refs):
        for i_ref, o_ref in zip(refs[:n], refs[n + len(after):]):
            o_ref[...] = i_ref[...].astype(BF16)

    return pl.pallas_call(
        body, name=name, out_shape=tuple(jax.ShapeDtypeStruct(a.shape, BF16) for a in arrs),
        in_specs=[pl.BlockSpec(memory_space=pltpu.VMEM)] * n + [pl.BlockSpec(memory_space=pl.ANY)] * len(after),
        compiler_params=_cparams())(*arrs, *after)


def _rs_adam(me_arr, fulls, lands, w, m, v, *, name):
    rows = lands[0].shape[1]
    k = len(fulls)
    nb = next(n for n in (4, 2, 1) if rows % (16 * n) == 0)
    br = rows // nb

    def body(me_ref, *refs):
        own_refs, land_refs = refs[:k], refs[k:2 * k]
        w_ref, m_ref, v_ref, g_ref, d_ref, mo_ref, vo_ref = refs[2 * k:]
        me = me_ref[0]
        parts = []
        for own_ref, land_ref in zip(own_refs, land_refs):
            acc = jnp.zeros(own_ref.shape, F32)
            for p in range(NDEV):
                acc = acc + jnp.where(p == me, own_ref[...], land_ref[p].astype(F32))
            parts.append(acc)
        acc = parts[0] if k == 1 else jnp.concatenate(parts, axis=1)
        g_ref[...] = acc
        d_ref[...], mo_ref[...], vo_ref[...] = _adam_math(w_ref[...], acc, m_ref[...], v_ref[...])

    sh = jax.ShapeDtypeStruct((rows, D), F32)
    blk = pl.BlockSpec((br, D), lambda i, me: (i, 0))
    grid_spec = pltpu.PrefetchScalarGridSpec(
        num_scalar_prefetch=1, grid=(nb,),
        in_specs=[pl.BlockSpec((br, f.shape[1]), lambda i, me: (me[0] * nb + i, 0)) for f in fulls]
        + [pl.BlockSpec((NDEV, br, l.shape[2]), lambda i, me: (0, i, 0)) for l in lands]
        + [blk, blk, blk],
        out_specs=[blk] * 4)
    return pl.pallas_call(
        body, name=name, out_shape=(sh, sh, sh, sh), grid_spec=grid_spec, compiler_params=_cparams(1),
    )(me_arr, *fulls, *lands, w, m, v)


ROW_F, ROW_I, ROW_C, ROW_G1, ROW_LG = 0, 8, 16, 24, 32
PACK_ROWS = 40


def _small_sum(me_arr, pack, pack_land, sgw, sgw_land, wmod):
    ncol = wmod.shape[1]

    def body(me_ref, pack_ref, pland_ref, sgw_ref, sland_ref, wmod_ref, sum_ref, all_ref, sgw_sum_ref, part_ref):
        me = me_ref[0]
        acc = jnp.zeros((PACK_ROWS, D), F32)
        acc_w = jnp.zeros(sgw_ref.shape, F32)
        for p in range(NDEV):
            rows = jnp.where(p == me, pack_ref[...], pland_ref[p])
            all_ref[p] = rows
            acc = acc + rows
            acc_w = acc_w + jnp.where(p == me, sgw_ref[...], sland_ref[p])
        sum_ref[...] = acc
        sgw_sum_ref[...] = acc_w
        zero = jnp.zeros((1, D), F32)
        dcmod = jnp.concatenate([acc[ROW_C:ROW_C + 1], acc[ROW_C + 1:ROW_C + 2], zero, zero, zero, zero], axis=1)
        mine = jnp.zeros((1, ncol), F32)
        for d in range(NDEV):
            mine = mine + jnp.where(d == me, dcmod[:, d * ncol:(d + 1) * ncol], 0.0)
        part_ref[...] = _dot_nt(jnp.broadcast_to(mine, (8, ncol)).astype(BF16), wmod_ref[...].astype(BF16))

    vm = pl.BlockSpec(memory_space=pltpu.VMEM)
    return pl.pallas_call(
        body, name="small_sum",
        out_shape=(jax.ShapeDtypeStruct((PACK_ROWS, D), F32), jax.ShapeDtypeStruct((NDEV, PACK_ROWS, D), F32),
                   jax.ShapeDtypeStruct(sgw.shape, F32), jax.ShapeDtypeStruct((8, D), F32)),
        in_specs=[pl.BlockSpec(memory_space=pltpu.SMEM)] + [vm] * 5,
        compiler_params=_cparams(),
    )(me_arr, pack, pack_land, sgw, sgw_land, wmod)


def _cctx_final(me_arr, part, part_land, cctx_row, m_row, v_row):
    def body(me_ref, part_ref, land_ref, c_ref, m_ref, v_ref, g_ref, d_ref, mo_ref, vo_ref):
        me = me_ref[0]
        tot = jnp.zeros((8, D), F32)
        for p in range(NDEV):
            tot = tot + jnp.where(p == me, part_ref[...], land_ref[p])
        cc = c_ref[...]
        _, dsilu = _silu_parts(cc)
        g = tot[0:1, :] * dsilu
        g_ref[...] = g
        d_ref[...], mo_ref[...], vo_ref[...] = _adam_math(cc, g, m_ref[...], v_ref[...])

    row = jax.ShapeDtypeStruct((1, D), F32)
    vm = pl.BlockSpec(memory_space=pltpu.VMEM)
    return pl.pallas_call(
        body, name="cctx_final", out_shape=(row, row, row, row),
        in_specs=[pl.BlockSpec(memory_space=pltpu.SMEM)] + [vm] * 5,
        compiler_params=_cparams(),
    )(me_arr, part, part_land, cctx_row, m_row, v_row)


def _adam_small(s, sgw_g, params):
    names = ["norm1", "norm2", "norm_f", "sg_gain", "sg_b", "ret_logit_f", "ret_logit_b", "b_mod", "sg_w"]
    flat = [a for n in names for a in params[n]]

    def body(*refs):
        s_ref, sgw_ref = refs[0], refs[1]
        p_refs = refs[2:2 + 3 * len(names)]
        o_refs = refs[2 + 3 * len(names):]
        loss_ref = o_refs[-1]

        def row(r):
            return s_ref[r:r + 1, :]

        grads = {
            "norm1": row(ROW_I + 2) + row(ROW_C + 2),
            "norm2": row(ROW_F + 4),
            "norm_f": row(ROW_F + 0),
            "sg_gain": s_ref[ROW_I + 3:ROW_I + 4, 0:AW],
            "sg_b": s_ref[ROW_I + 4:ROW_I + 8, 0:DH],
            "sg_w": sgw_ref[...],
        }
        for j, n in enumerate(names):
            w_ref, m_ref, v_ref = p_refs[3 * j:3 * j + 3]
            g_ref, d_ref, mo_ref, vo_ref = o_refs[4 * j:4 * j + 4]
            w = w_ref[...]
            if n in ("ret_logit_f", "ret_logit_b"):
                r = ROW_LG + (0 if n == "ret_logit_f" else 1)
                g = s_ref[r:r + 1, 0:H] * _sigmoid(-w)
            elif n == "b_mod":
                rows = [row(ROW_I + 0) + row(ROW_C + 0), row(ROW_I + 1) + row(ROW_C + 1), row(ROW_G1),
                        row(ROW_F + 2), row(ROW_F + 3), row(ROW_F + 1)]
                g = jnp.concatenate(rows, axis=1)
            else:
                g = grads[n]
            g_ref[...] = g
            d_ref[...], mo_ref[...], vo_ref[...] = _adam_math(w, g, m_ref[...], v_ref[...])
        loss_ref[...] = jnp.full((1, 1), 0.5 / D, F32) * jnp.sum(row(ROW_F + 5), axis=1, keepdims=True)

    out_shape = []
    for n in names:
        w = params[n][0]
        out_shape += [jax.ShapeDtypeStruct(w.shape, F32)] * 4
    out_shape.append(jax.ShapeDtypeStruct((1, 1), F32))
    outs = pl.pallas_call(body, name="adam_small", out_shape=tuple(out_shape), compiler_params=_cparams())(
        s, sgw_g, *flat)
    return {n: tuple(outs[4 * j:4 * j + 4]) for j, n in enumerate(names)}, outs[-1]


def _wmod_grad(cs_all, dmod_cols, wmod, m, v):
    ncol = wmod.shape[1]

    def body(cs_ref, dm_ref, w_ref, m_ref, v_ref, g_ref, d_ref, mo_ref, vo_ref):
        g = _dot_tn(cs_ref[...].astype(BF16), dm_ref[...].astype(BF16))
        g_ref[...] = g
        d_ref[...], mo_ref[...], vo_ref[...] = _adam_math(w_ref[...], g, m_ref[...], v_ref[...])

    sh = jax.ShapeDtypeStruct((D, ncol), F32)
    br = D // 4
    blk = pl.BlockSpec((br, ncol), lambda i: (i, 0))
    return pl.pallas_call(
        body, name="wmod_grad", out_shape=(sh, sh, sh, sh), grid=(D // br,),
        in_specs=[pl.BlockSpec((cs_all.shape[0], br), lambda i: (0, i)), _whole(dmod_cols.shape), blk, blk, blk],
        out_specs=[blk] * 4,
        compiler_params=_cparams(1),
    )(cs_all, dmod_cols, wmod, m, v)


def _rope_tables(t_len):
    n_freq = DH // 4
    inv = (ROPE_BASE ** (-np.arange(n_freq, dtype=np.float32) / n_freq)).astype(np.float32)
    tok = np.arange(t_len)
    rows = (tok // GRID_W).astype(np.float32)
    cols = (tok % GRID_W).astype(np.float32)
    ang_r = rows[:, None] * inv[None, :]
    ang_c = cols[:, None] * inv[None, :]
    cos = np.concatenate([np.cos(ang_r)] * 2 + [np.cos(ang_c)] * 2, axis=1).astype(np.float32)
    sin = np.concatenate([-np.sin(ang_r), np.sin(ang_r), -np.sin(ang_c), np.sin(ang_c)], axis=1).astype(np.float32)
    return jnp.asarray(cos), jnp.asarray(sin)


def _local_step(x, tgt, ctx, mod6, cmod6, norm1, norm2, normf, win, wout, ffn_weights, sgw, sgb, sggain, rlf, rlb,
                *, tm_a, tm_b, tm_f, tm_m, tm_r, tm_i, bt_w, after_first_grads=None, small_path=None,
                after_in_half=None):
    t_len = x.shape[0]
    cos, sin = _rope_tables(t_len)
    sgbt = jnp.broadcast_to(sgb[:, :, None], (G, C, 128))
    rlf = jnp.broadcast_to(rlf.reshape(H, 1), (H, 128))
    rlb = jnp.broadcast_to(rlb.reshape(H, 1), (H, 128))
    hc, kvc, scf, scb = _ctx_fwd(ctx, cmod6, norm1, win, rlf, rlb)
    hx, zuv, q, k, v, gfb, of, ya, sst = _fwd_a(x, mod6, norm1, win, sgw, sgbt, sggain, cos, sin, rlf, scf, tm=tm_a)
    if callable(wout):
        wout, tok = wout(hx)
        rlb = rlb + tok
    ob, yr, x1, rst = _fwd_b(q, k, v, of, gfb, ya, x, mod6, wout, rlb, scb, tm=tm_b)
    wg, wu, wd = ffn_weights(x1) if callable(ffn_weights) else ffn_weights
    dx1, h2, da, db, hm, df, small_f = _ffn(x1, tgt, mod6, norm2, normf, wg, wu, wd, tm=tm_f)
    bt2 = min(2 * bt_w, t_len)
    d_wd, d_wd_b = _tn_matmul(hm, df, bm=DFF // 2, bt=bt2, name="dw_down")
    d_wg, d_wg_b = _tn_matmul(da, h2, bm=DFF // 2, bt=bt2, name="dw_gate")
    d_wu, d_wu_b = _tn_matmul(db, h2, bm=DFF // 2, bt=bt2, name="dw_up")
    dya, dgfb, dof, dob = _mid_bwd(dx1, of, ob, gfb, mod6, wout, tm=tm_m)
    d_wo, d_wo_b, dg1 = _dw_out(ya, yr, dx1, mod6, wout, bt=bt2)
    if after_first_grads is not None:
        rlf = rlf + after_first_grads((d_wd_b, d_wg_b, d_wu_b, d_wo_b))
    dqf, dkf, dvf, dqb, dkb, dvb, dscf, dscb, dlg = _ret_bwd(q, k, v, dof, dob, sst, rst, rlf, rlb, tm=tm_r)
    gx, dz, small_i, dsgw = _in_bwd(x, zuv, dya, dqf, dkf, dvf, dqb, dkb, dvb, dgfb, dx1, cos, sin,
                                    mod6, norm1, win, sgw, sgbt, sggain, tm=tm_i)
    dwin_c, small_c, dlg = _ctx_bwd(ctx, hc, kvc, cmod6, norm1, win, rlf, rlb, dscf, dscb, dlg)
    pack = jnp.concatenate([small_f, small_i, small_c, dg1, dlg], axis=0)
    after = small_path(pack, dsgw) if small_path is not None else ()
    halves = []
    for j in range(2):
        halves.append(_tn_matmul(dz, hx, bm=INC // 2, bt=bt2, name="dw_in_%d" % j, init=dwin_c,
                                 init_rows=(KV_LO, KV_HI), after=after, cols=(j, D // 2)))
        if after_in_half is not None:
            after = after_in_half(j, halves[-1][1])
    grads = {"w_in": halves, "w_out": (d_wo, d_wo_b), "w_gate": (d_wg, d_wg_b), "w_up": (d_wu, d_wu_b),
             "w_down": (d_wd, d_wd_b)}
    return gx, grads, pack, dsgw


def kernel(x, c, ctx, c_ctx, w_mod, b_mod, norm1, w_in, sg_gain, sg_w, sg_b, ret_logit_f, ret_logit_b, w_out, norm2, w_gate, w_up, w_down, norm_f, loss_target, m_c_ctx, m_w_mod, m_b_mod, m_norm1, m_w_in, m_sg_gain, m_sg_w, m_sg_b, m_ret_logit_f, m_ret_logit_b, m_w_out, m_norm2, m_w_gate, m_w_up, m_w_down, m_norm_f, v_c_ctx, v_w_mod, v_b_mod, v_norm1, v_w_in, v_sg_gain, v_sg_w, v_sg_b, v_ret_logit_f, v_ret_logit_b, v_w_out, v_norm2, v_w_gate, v_w_up, v_w_down, v_norm_f):
    t_len = x.shape[1]
    tm = min(512, t_len)
    me = 4 * lax.axis_index("x") + 2 * lax.axis_index("y") + lax.axis_index("c")
    tr = lambda a: jnp.transpose(a[0])

    cctx_row = c_ctx.reshape(1, D)
    mod_all, cs_all, g_in = _entry_gather(c, cctx_row, w_mod[0], b_mod, [tr(w_in)])
    mod6 = lax.dynamic_slice(mod_all, (me, 0), (1, 6 * D)).reshape(6, D)
    cmod6 = mod_all[8].reshape(6, D)
    win_t = g_in.reshape(INC, D)

    (out_shard,) = _cast_bf16([w_out[0]], name="cast_out", after=[g_in])
    h_out, tok_out = _exchange_start([out_shard], scatter=False, name="wout_start")
    ffn_shards = _cast_bf16([tr(w_gate), tr(w_up), w_down[0]], name="cast_ffn", after=[h_out[2]])
    sems_ffn, arrs_ffn, tok = _gather2_start(ffn_shards, name="wffn_start")
    mod6 = mod6 + (tok_out[0, 0] + tok[0, 0])
    ffn = {}

    def place_own(own, lands, rows):
        return [lax.dynamic_update_slice(l, o[None], (me, 0, 0)).reshape(rows, D) for o, l in zip(own, lands)]

    def wout(after):
        own, lands = _exchange_wait(h_out, [after], scatter=False, name="wout_wait")
        arrs = _gather2_arrived(sems_ffn, arrs_ffn, [after], name="wffn_arrived")
        ffn["own"] = arrs[:3]
        ffn["sems"], ffn["lands"], tok_fwd = _gather2_forward(arrs[3:], name="wffn_forward")
        return place_own(own, lands, D)[0], tok_fwd[0, 0]

    def ffn_weights(after):
        lands = _gather2_wait(ffn["sems"], ffn["lands"], [after], name="wffn_wait")
        return place_own(ffn["own"], lands, DFF)

    rs, outs = {}, {}

    def after_first_grads(bf):
        rs["first"], tok_first = _exchange_start([b.reshape(NDEV, b.shape[0] // NDEV, D) for b in bf], scatter=True,
                                                 name="rs_first_start")
        return tok_first[0, 0]

    def small_path(pack, dsgw):
        rs["small"], _ = _exchange_start([pack, dsgw.reshape(G * C, C)], scatter=False, name="small_start")
        return [rs["small"][2], rs["small"][3]]

    def after_in_half(j, half_bf16):
        rs["in%d" % j], _ = _exchange_start([half_bf16.reshape(NDEV, INC // NDEV, D // 2)], scatter=True,
                                            name="rs_in%d_start" % j)
        return [rs["in%d" % j][2]]

    gx, grads, pack, dsgw = _local_step(
        x[0], loss_target[0], ctx[0], mod6, cmod6, norm1, norm2[0:1], norm_f.reshape(1, D), win_t, wout, ffn_weights,
        sg_w[0], sg_b[0], sg_gain, ret_logit_f, ret_logit_b,
        tm_a=tm, tm_b=tm, tm_f=min(256, t_len), tm_m=min(1024, t_len), tm_r=min(1024, t_len), tm_i=tm,
        bt_w=min(1024, t_len),
        after_first_grads=after_first_grads, small_path=small_path, after_in_half=after_in_half)

    me_arr = me.reshape(1).astype(jnp.int32)
    (pack_own, sgw_own), (pack_land, sgw_land) = _exchange_wait(rs["small"], [rs["in1"][2]], scatter=False,
                                                               name="small_wait")
    psum_rows, pall, sgw_sum, part = _small_sum(me_arr, pack_own, pack_land, sgw_own, sgw_land, w_mod[0])
    h_cc, _ = _exchange_start([part], scatter=False, name="cctx_start")

    dmod_rows = (ROW_I + 0, ROW_I + 1, ROW_G1, ROW_F + 2, ROW_F + 3, ROW_F + 1)
    dmod_all = jnp.concatenate([pall[:, r, :] for r in dmod_rows], axis=1)
    dcmod = jnp.concatenate([psum_rows[ROW_C + 0:ROW_C + 1], psum_rows[ROW_C + 1:ROW_C + 2],
                             jnp.zeros((1, 4 * D), F32)], axis=1)
    dmod_mat = jnp.concatenate([dmod_all, jnp.pad(dcmod, ((0, 7), (0, 0)))], axis=0)
    ncol = 6 * D // NDEV
    dmod_cols = lax.dynamic_slice(dmod_mat, (0, me * ncol), (16, ncol))
    g_wm, d_wm, m_wm, v_wm = _wmod_grad(cs_all, dmod_cols, w_mod[0], m_w_mod[0], v_w_mod[0])
    outs["w_mod"] = (g_wm[None], d_wm[None], m_wm[None], v_wm[None])
    small_params = {
        "norm1": (norm1, m_norm1, v_norm1), "norm2": (norm2, m_norm2, v_norm2),
        "norm_f": tuple(a.reshape(1, D) for a in (norm_f, m_norm_f, v_norm_f)),
        "sg_gain": (sg_gain, m_sg_gain, v_sg_gain), "sg_b": (sg_b[0], m_sg_b[0], v_sg_b[0]),
        "ret_logit_f": (ret_logit_f, m_ret_logit_f, v_ret_logit_f),
        "ret_logit_b": (ret_logit_b, m_ret_logit_b, v_ret_logit_b),
        "b_mod": (b_mod, m_b_mod, v_b_mod), "sg_w": (sg_w[0], m_sg_w[0], v_sg_w[0])}
    small, loss = _adam_small(psum_rows, sgw_sum.reshape(G, C, C), small_params)
    back = {"norm_f": lambda a: a.reshape(D), "sg_b": lambda a: a[None], "sg_w": lambda a: a[None]}
    for name, vals in small.items():
        outs[name] = tuple(back.get(name, lambda a: a)(a) for a in vals)

    _, lands_first = _exchange_wait(rs["first"], [g_wm], scatter=True, name="rs_first_wait")

    def finish(name, fulls, lands, w, m, v, transposed):
        take = tr if transposed else (lambda a: a[0])
        res = _rs_adam(me_arr, fulls, lands, take(w), take(m), take(v), name="adam_" + name)
        put = (lambda a: jnp.transpose(a)[None]) if transposed else (lambda a: a[None])
        outs[name] = tuple(put(a) for a in res)
        return res[0]

    g_down = finish("w_down", [grads["w_down"][0]], [lands_first[0]], w_down, m_w_down, v_w_down, False)
    g_gate = finish("w_gate", [grads["w_gate"][0]], [lands_first[1]], w_gate, m_w_gate, v_w_gate, True)
    g_up = finish("w_up", [grads["w_up"][0]], [lands_first[2]], w_up, m_w_up, v_w_up, True)
    g_out = finish("w_out", [grads["w_out"][0]], [lands_first[3]], w_out, m_w_out, v_w_out, False)
    done = [g_down, g_gate, g_up, g_out]
    (part_own,), (part_land,) = _exchange_wait(h_cc, done, scatter=False, name="cctx_wait")
    res_cc = _cctx_final(me_arr, part_own, part_land, cctx_row, m_c_ctx.reshape(1, D), v_c_ctx.reshape(1, D))
    outs["c_ctx"] = tuple(a.reshape(D) for a in res_cc)
    lands_in = [_exchange_wait(rs["in%d" % j], done, scatter=True, name="rs_in%d_wait" % j)[1][0] for j in range(2)]
    finish("w_in", [h[0] for h in grads["w_in"]], lands_in, w_in, m_w_in, v_w_in, True)

    order = ["c_ctx", "w_mod", "b_mod", "norm1", "w_in", "sg_gain", "sg_w", "sg_b", "ret_logit_f", "ret_logit_b",
             "w_out", "norm2", "w_gate", "w_up", "w_down", "norm_f"]
    res = [loss.reshape(()), gx[None]]
    for j in range(4):
        res += [outs[name][j] for name in order]
    return tuple(res)
```

```python
import functools
import math

import numpy as np
import jax
import jax.numpy as jnp
from jax import lax
from jax.experimental import pallas as pl
from jax.experimental.pallas import tpu as pltpu

F32 = jnp.float32
BF16 = jnp.bfloat16

D = 1024
AW = 512
RW = 512
H = 4
DH = 128
G = 4
C = 128
CR = 256
DFF = 2816
INC = 3584
Q_LO = 2 * AW
KV_LO, VR_LO, G_LO = Q_LO + RW, Q_LO + 2 * RW, Q_LO + 3 * RW
KV_HI = G_LO
NDEV = 8
EPS = 1e-6
KSCALE = DH ** -0.5
ROPE_BASE = 10000.0
GRID_W = 64

ADAM_LR = 0.001
ADAM_B1 = 0.9
ADAM_B2 = 0.999
ADAM_EPS = 1e-08
ADAM_WD = 0.01
ADAM_STEP = 10

VMEM_LIMIT = 56 * 1024 * 1024
MESH = pl.DeviceIdType.MESH


def _cparams(n_grid=0, **kw):
    sem = ("arbitrary",) * n_grid if n_grid else None
    return pltpu.CompilerParams(dimension_semantics=sem, vmem_limit_bytes=VMEM_LIMIT, **kw)


def _dot(a, b):
    return jnp.dot(a, b, preferred_element_type=F32)


def _dot_nt(a, b):
    return lax.dot_general(a, b, (((1,), (1,)), ((), ())), preferred_element_type=F32)


def _dot_tn(a, b):
    return lax.dot_general(a, b, (((0,), (0,)), ((), ())), preferred_element_type=F32)


def _whole(shape):
    nd = len(shape)
    return pl.BlockSpec(shape, lambda *_: (0,) * nd, pipeline_mode=pl.Buffered(1))


def _acc(shape):
    nd = len(shape)
    return pl.BlockSpec(shape, lambda *_: (0,) * nd)


def _rows(tm, n):
    return pl.BlockSpec((tm, n), lambda i: (i, 0))


def _rows_rev(tm, n, nt):
    return pl.BlockSpec((tm, n), lambda i: (nt - 1 - i, 0))


def _sigmoid(x):
    return 1.0 / (1.0 + jnp.exp(-x))


def _log_sigmoid(x):
    return jnp.minimum(x, 0.0) - jnp.log(1.0 + jnp.exp(-jnp.abs(x)))


_GK0 = math.sqrt(2.0 / math.pi)
_GK1 = 0.044715


def _gelu(x):
    x2 = x * x
    t = jnp.tanh(x * (_GK0 + (_GK0 * _GK1) * x2))
    h = 0.5 + 0.5 * t
    g = x * h
    dg = h + g * (1.0 - h) * ((2.0 * _GK0) + (6.0 * _GK0 * _GK1) * x2)
    return g, dg


def _silu_parts(a):
    s = _sigmoid(a)
    sa = a * s
    return sa, s + sa * (1.0 - s)


def _rope(t, cos, sins):
    n = t.shape[1]
    lane = lax.broadcasted_iota(jnp.int32, t.shape, 1)
    first = (lane & 63) < 32
    partner = jnp.where(first, pltpu.roll(t, n - 32, 1), pltpu.roll(t, 32, 1))
    return t * cos + partner * sins


def _headnorm(o):
    outs, rs = [], []
    for h in range(H):
        oh = o[:, h * DH:(h + 1) * DH]
        r = lax.rsqrt(jnp.mean(oh * oh, axis=-1, keepdims=True) + EPS)
        outs.append(oh * r)
        rs.append(r)
    return jnp.concatenate(outs, axis=1), rs


def _decay_consts(lg, forward):
    ii = lax.broadcasted_iota(jnp.int32, (CR, CR), 0).astype(F32)
    jj = lax.broadcasted_iota(jnp.int32, (CR, CR), 1).astype(F32)
    ic = lax.broadcasted_iota(jnp.int32, (CR, 1), 0).astype(F32)
    if forward:
        diff = ii - jj
        xi = jnp.exp(lg * (ic + 1.0))
        z = jnp.exp(lg * (CR - 1.0 - ic))
    else:
        diff = jj - ii
        xi = jnp.exp(lg * (CR - ic))
        z = jnp.exp(lg * ic)
    dm = jnp.where(diff >= 0.0, jnp.exp(lg * jnp.maximum(diff, 0.0)), 0.0)
    dec = jnp.exp(lg * float(CR))
    return dm, xi, z, dec, ic


def _ctx_fwd(ctx, cmod6, norm1, win, rlf, rlb):
    lc = ctx.shape[0]

    def body(ctx_ref, cmod_ref, n1_ref, win_ref, rlf_ref, rlb_ref, hc_ref, kvc_ref, scf_ref, scb_ref):
        x = ctx_ref[...]
        r = lax.rsqrt(jnp.mean(x * x, axis=-1, keepdims=True) + EPS)
        hc = (x * r) * (n1_ref[...] * (1.0 + cmod_ref[1:2, :])) + cmod_ref[0:1, :]
        hcb = hc.astype(BF16)
        hc_ref[...] = hcb
        kv = _dot_nt(hcb, win_ref[KV_LO:KV_HI, :])
        k = kv[:, :RW] * KSCALE
        v = kv[:, RW:]
        kvc_ref[:, :RW] = k
        kvc_ref[:, RW:] = v
        t = lax.broadcasted_iota(jnp.int32, (lc, 1), 0).astype(F32)
        lgf = _log_sigmoid(rlf_ref[...])
        lgb = _log_sigmoid(rlb_ref[...])
        for h in range(H):
            hs = slice(h * DH, (h + 1) * DH)
            wf = jnp.exp(lgf[h:h + 1, 0:1] * (lc - 1.0 - t))
            wb = jnp.exp(lgb[h:h + 1, 0:1] * t)
            vh = v[:, hs].astype(BF16)
            scf_ref[h] = _dot_tn((k[:, hs] * wf).astype(BF16), vh)
            scb_ref[h] = _dot_tn((k[:, hs] * wb).astype(BF16), vh)

    return pl.pallas_call(
        body, name="ctx_fwd",
        out_shape=(jax.ShapeDtypeStruct((lc, D), BF16), jax.ShapeDtypeStruct((lc, 2 * RW), F32),
                   jax.ShapeDtypeStruct((H, DH, DH), F32), jax.ShapeDtypeStruct((H, DH, DH), F32)),
        compiler_params=_cparams(),
    )(ctx, cmod6, norm1, win, rlf, rlb)


def _sg_forward(u, v, sgw_ref, sgbt_ref, sgg_ref, nc):
    ua, dgu = _gelu(u)
    va, dgv = _gelu(v)
    gain = sgg_ref[...]
    mixed, vn_b, vah_l, rv_l = [], [], [], []
    for g in range(G):
        gs = slice(g * DH, (g + 1) * DH)
        vag = va[:, gs]
        rv = lax.rsqrt(jnp.mean(vag * vag, axis=-1, keepdims=True) + EPS)
        vah = vag * rv
        vn = (vah * gain[:, gs]).astype(BF16)
        wg = sgw_ref[g].astype(BF16)
        bcol = sgbt_ref[g]
        rows = [_dot(wg, vn[c * C:(c + 1) * C, :]) + bcol for c in range(nc)]
        mixed.append(jnp.concatenate(rows, axis=0) if nc > 1 else rows[0])
        vn_b.append(vn)
        vah_l.append(vah)
        rv_l.append(rv)
    mixed = jnp.concatenate(mixed, axis=1)
    return ua * mixed, (ua, dgu, dgv, mixed, vn_b, vah_l, rv_l)


def _fwd_a(x, mod6, norm1, win, sgw, sgbt, sggain, cos, sin, rlf, scf, *, tm):
    t_len = x.shape[0]
    nt, nc, ncr = t_len // tm, tm // C, tm // CR

    def body(x_ref, mod_ref, n1_ref, win_ref, sgw_ref, sgbt_ref, sgg_ref, cos_ref, sin_ref, rlf_ref, scf_ref,
             hx_ref, zuv_ref, q_ref, k_ref, v_ref, gfb_ref, of_ref, ya_ref, sst_ref, s_scr):
        i = pl.program_id(0)

        @pl.when(i == 0)
        def _():
            s_scr[...] = scf_ref[...]

        x = x_ref[...]
        r = lax.rsqrt(jnp.mean(x * x, axis=-1, keepdims=True) + EPS)
        hx = (x * r) * (n1_ref[...] * (1.0 + mod_ref[1:2, :])) + mod_ref[0:1, :]
        hxb = hx.astype(BF16)
        hx_ref[...] = hxb
        z = _dot_nt(hxb, win_ref[...])
        zuv_ref[...] = z[:, :2 * AW].astype(BF16)
        gfb_ref[...] = z[:, G_LO:INC].astype(BF16)
        cos = jnp.tile(cos_ref[...], (1, H))
        sins = jnp.tile(sin_ref[...], (1, H))
        q_ref[...] = _rope(z[:, Q_LO:KV_LO], cos, sins).astype(BF16)
        k_ref[...] = (_rope(z[:, KV_LO:VR_LO], cos, sins) * KSCALE).astype(BF16)
        v_ref[...] = z[:, VR_LO:G_LO].astype(BF16)
        ya, _ = _sg_forward(z[:, :AW], z[:, AW:2 * AW], sgw_ref, sgbt_ref, sgg_ref, nc)
        ya_ref[...] = ya.astype(BF16)

        lg = _log_sigmoid(rlf_ref[...])
        for h in range(H):
            dm, xi, zc, dec, _ = _decay_consts(lg[h:h + 1, 0:1], True)
            hs = slice(h * DH, (h + 1) * DH)
            for c in range(ncr):
                rs = slice(c * CR, (c + 1) * CR)
                qc, kc, vc = q_ref[rs, hs], k_ref[rs, hs], v_ref[rs, hs]
                s = s_scr[h]
                sst_ref[c, h] = s
                a = (_dot_nt(qc, kc) * dm).astype(BF16)
                of_ref[rs, hs] = (_dot(a, vc) + xi * _dot(qc, s.astype(BF16))).astype(BF16)
                kz = (kc.astype(F32) * zc).astype(BF16)
                s_scr[h] = dec * s + _dot_tn(kz, vc)

    n_chunks = t_len // CR
    return pl.pallas_call(
        body, name="fwd_a", grid=(nt,),
        in_specs=[_rows(tm, D), _whole((6, D)), _whole((1, D)), _whole((INC, D)), _whole((G, C, C)),
                  _whole((G, C, 128)), _whole((1, AW)), _rows(tm, DH), _rows(tm, DH), _whole((H, 128)),
                  _whole((H, DH, DH))],
        out_specs=[_rows(tm, D), _rows(tm, 2 * AW), _rows(tm, RW), _rows(tm, RW), _rows(tm, RW),
                   _rows(tm, 2 * RW), _rows(tm, RW), _rows(tm, AW),
                   pl.BlockSpec((ncr, H, DH, DH), lambda i: (i, 0, 0, 0))],
        out_shape=(jax.ShapeDtypeStruct((t_len, D), BF16), jax.ShapeDtypeStruct((t_len, 2 * AW), BF16),
                   jax.ShapeDtypeStruct((t_len, RW), BF16), jax.ShapeDtypeStruct((t_len, RW), BF16),
                   jax.ShapeDtypeStruct((t_len, RW), BF16), jax.ShapeDtypeStruct((t_len, 2 * RW), BF16),
                   jax.ShapeDtypeStruct((t_len, RW), BF16), jax.ShapeDtypeStruct((t_len, AW), BF16),
                   jax.ShapeDtypeStruct((n_chunks, H, DH, DH), F32)),
        scratch_shapes=[pltpu.VMEM((H, DH, DH), F32)],
        compiler_params=_cparams(1),
    )(x, mod6, norm1, win, sgw, sgbt, sggain, cos, sin, rlf, scf)


def _fwd_b(q, k, v, of, gfb, ya, x, mod6, wout, rlb, scb, *, tm):
    t_len = x.shape[0]
    nt, nc = t_len // tm, tm // CR

    def body(q_ref, k_ref, v_ref, of_ref, gfb_ref, ya_ref, x_ref, mod_ref, wout_ref, rlb_ref, scb_ref,
             ob_ref, ycat_ref, x1_ref, rst_ref, r_scr):
        i = pl.program_id(0)

        @pl.when(i == 0)
        def _():
            r_scr[...] = scb_ref[...]

        lg = _log_sigmoid(rlb_ref[...])
        consts = [_decay_consts(lg[h:h + 1, 0:1], False) for h in range(H)]

        def first(c, h):
            dm, _, zc, dec, _ = consts[h]
            hs, rs = slice(h * DH, (h + 1) * DH), slice(c * CR, (c + 1) * CR)
            qc, kc, vc = q_ref[rs, hs], k_ref[rs, hs], v_ref[rs, hs]
            s = r_scr[h]
            rst_ref[c, h] = s
            scores = _dot_nt(qc, kc)
            cross = _dot(qc, s.astype(BF16))
            kz = (kc.astype(F32) * zc).astype(BF16)
            r_scr[h] = dec * s + _dot_tn(kz, vc)
            return (scores * dm).astype(BF16), cross

        def second(c, h, carried):
            a, cross = carried
            hs, rs = slice(h * DH, (h + 1) * DH), slice(c * CR, (c + 1) * CR)
            ob_ref[rs, hs] = (_dot(a, v_ref[rs, hs]) + consts[h][1] * cross).astype(BF16)

        def rows_out(c):
            rs = slice(c * CR, (c + 1) * CR)
            gfb = gfb_ref[rs, :].astype(F32)
            sf, _ = _silu_parts(gfb[:, :RW])
            sb, _ = _silu_parts(gfb[:, RW:])
            hnf, _ = _headnorm(of_ref[rs, :].astype(F32))
            hnb, _ = _headnorm(ob_ref[rs, :].astype(F32))
            yr = sf * hnf + sb * hnb
            ycat = jnp.concatenate([ya_ref[rs, :], yr.astype(BF16)], axis=1)
            ycat_ref[rs, :] = ycat
            x1_ref[rs, :] = x_ref[rs, :] + mod_ref[2:3, :] * _dot(ycat, wout_ref[...])

        pending, waiting_rows = None, None
        for c in reversed(range(nc)):
            for h in range(H):
                carried = first(c, h)
                if pending is not None:
                    second(*pending)
                pending = (c, h, carried)
            if waiting_rows is not None:
                rows_out(waiting_rows)
            waiting_rows = c
        second(*pending)
        rows_out(waiting_rows)

    n_chunks = t_len // CR
    rr = functools.partial(_rows_rev, nt=nt)
    return pl.pallas_call(
        body, name="fwd_b", grid=(nt,),
        in_specs=[rr(tm, RW), rr(tm, RW), rr(tm, RW), rr(tm, RW), rr(tm, 2 * RW), rr(tm, AW), rr(tm, D),
                  _whole((6, D)), _whole((D, D)), _whole((H, 128)), _whole((H, DH, DH))],
        out_specs=[rr(tm, RW), rr(tm, D), rr(tm, D),
                   pl.BlockSpec((nc, H, DH, DH), lambda i: (nt - 1 - i, 0, 0, 0))],
        out_shape=(jax.ShapeDtypeStruct((t_len, RW), BF16), jax.ShapeDtypeStruct((t_len, D), BF16),
                   jax.ShapeDtypeStruct((t_len, D), F32),
                   jax.ShapeDtypeStruct((n_chunks, H, DH, DH), F32)),
        scratch_shapes=[pltpu.VMEM((H, DH, DH), F32)],
        compiler_params=_cparams(1),
    )(q, k, v, of, gfb, ya, x, mod6, wout, rlb, scb)


def _ffn(x1, tgt, mod6, norm2, normf, wg, wu, wd, *, tm):
    t_len = x1.shape[0]
    nt = t_len // tm

    def body(x1_ref, tgt_ref, mod_ref, n2_ref, nf_ref, wg_ref, wu_ref, wd_ref,
             dx1_ref, h2_ref, da_ref, db_ref, hm_ref, df_ref, small_ref, dxb_ref):
        i = pl.program_id(0)

        @pl.when(i == 0)
        def _():
            small_ref[...] = jnp.zeros_like(small_ref)

        sh2, sc2, g2 = mod_ref[3:4, :], mod_ref[4:5, :], mod_ref[5:6, :]
        n2, nf = n2_ref[...], nf_ref[...]
        x1 = x1_ref[...]
        r2 = lax.rsqrt(jnp.mean(x1 * x1, axis=-1, keepdims=True) + EPS)
        x1h = x1 * r2
        w2 = n2 * (1.0 + sc2)
        h2b = (x1h * w2 + sh2).astype(BF16)
        h2_ref[...] = h2b
        a = _dot_nt(h2b, wg_ref[...])
        b = _dot_nt(h2b, wu_ref[...])
        sa, dsa = _silu_parts(a)
        hmb = (sa * b).astype(BF16)
        hm_ref[...] = hmb
        f = _dot(hmb, wd_ref[...])
        x2 = x1 + g2 * f
        r3 = lax.rsqrt(jnp.mean(x2 * x2, axis=-1, keepdims=True) + EPS)
        x2h = x2 * r3
        diff = x2h * nf - tgt_ref[...]
        small_ref[5:6, :] += jnp.sum(diff * diff, axis=0, keepdims=True)
        dout = diff * (1.0 / D)
        small_ref[0:1, :] += jnp.sum(dout * x2h, axis=0, keepdims=True)
        dyg = dout * nf
        dx2 = r3 * (dyg - x2h * jnp.mean(dyg * x2h, axis=-1, keepdims=True))
        small_ref[1:2, :] += jnp.sum(dx2 * f, axis=0, keepdims=True)
        dfb = (dx2 * g2).astype(BF16)
        df_ref[...] = dfb
        dhm = _dot_nt(dfb, wd_ref[...])
        dbb = (dhm * sa).astype(BF16)
        dab = (dhm * b * dsa).astype(BF16)
        da_ref[...] = dab
        db_ref[...] = dbb
        dh2 = _dot(dab, wg_ref[...]) + _dot(dbb, wu_ref[...])
        small_ref[2:3, :] += jnp.sum(dh2, axis=0, keepdims=True)
        dhx = dh2 * x1h
        small_ref[3:4, :] += jnp.sum(dhx, axis=0, keepdims=True) * n2
        small_ref[4:5, :] += jnp.sum(dhx, axis=0, keepdims=True) * (1.0 + sc2)
        dyg2 = dh2 * w2
        dx1 = dx2 + r2 * (dyg2 - x1h * jnp.mean(dyg2 * x1h, axis=-1, keepdims=True))
        dx1_ref[...] = dx1
        dxb_ref[...] = dx1.astype(BF16)

    return pl.pallas_call(
        body, name="ffn", grid=(nt,),
        in_specs=[_rows(tm, D), _rows(tm, D), _whole((6, D)), _whole((1, D)), _whole((1, D)),
                  _whole((DFF, D)), _whole((DFF, D)), _whole((DFF, D))],
        out_specs=[_rows(tm, D), _rows(tm, D), _rows(tm, DFF), _rows(tm, DFF), _rows(tm, DFF), _rows(tm, D),
                   _acc((8, D)), _rows(tm, D)],
        out_shape=(jax.ShapeDtypeStruct((t_len, D), F32), jax.ShapeDtypeStruct((t_len, D), BF16),
                   jax.ShapeDtypeStruct((t_len, DFF), BF16), jax.ShapeDtypeStruct((t_len, DFF), BF16),
                   jax.ShapeDtypeStruct((t_len, DFF), BF16), jax.ShapeDtypeStruct((t_len, D), BF16),
                   jax.ShapeDtypeStruct((8, D), F32), jax.ShapeDtypeStruct((t_len, D), BF16)),
        compiler_params=_cparams(1),
    )(x1, tgt, mod6, norm2, normf, wg, wu, wd)


def _mid_bwd(dxb, of, ob, gfb, mod6, wout, *, tm):
    t_len = dxb.shape[0]
    nt = t_len // tm
    rc = min(256, tm)

    def body(dxb_ref, of_ref, ob_ref, gfb_ref, mod_ref, wout_ref,
             dya_ref, dgfb_ref, dof_ref, dob_ref):
        for c in range(tm // rc):
            rows = slice(c * rc, (c + 1) * rc)
            dyb = (dxb_ref[rows, :].astype(F32) * mod_ref[2:3, :]).astype(BF16)
            dycat = _dot_nt(dyb, wout_ref[...])
            dya_ref[rows, :] = dycat[:, :AW].astype(BF16)
            dyr = dycat[:, AW:]
            gfb = gfb_ref[rows, :].astype(F32)
            for o_ref, do_ref, lo in ((of_ref, dof_ref, 0), (ob_ref, dob_ref, RW)):
                sg, dsg = _silu_parts(gfb[:, lo:lo + RW])
                o = o_ref[rows, :].astype(F32)
                hn, rs = _headnorm(o)
                dgfb_ref[rows, lo:lo + RW] = (dyr * hn * dsg).astype(BF16)
                dhn = dyr * sg
                for h in range(H):
                    hs = slice(h * DH, (h + 1) * DH)
                    oh, dh = hn[:, hs], dhn[:, hs]
                    do_ref[rows, hs] = (rs[h] * (dh - oh * jnp.mean(dh * oh, axis=-1, keepdims=True))).astype(BF16)

    return pl.pallas_call(
        body, name="mid_bwd", grid=(nt,),
        in_specs=[_rows(tm, D), _rows(tm, RW), _rows(tm, RW), _rows(tm, 2 * RW), _whole((6, D)), _whole((D, D))],
        out_specs=[_rows(tm, AW), _rows(tm, 2 * RW), _rows(tm, RW), _rows(tm, RW)],
        out_shape=(jax.ShapeDtypeStruct((t_len, AW), BF16), jax.ShapeDtypeStruct((t_len, 2 * RW), BF16),
                   jax.ShapeDtypeStruct((t_len, RW), BF16), jax.ShapeDtypeStruct((t_len, RW), BF16)),
        compiler_params=_cparams(1),
    )(dxb, of, ob, gfb, mod6, wout)


def _ret_bwd_items(q_ref, k_ref, v_ref, do_ref, st_ref, dq_ref, dk_ref, dv_ref, ds_scr, dlg_scr, lg, forward, nc, row0):
    order = list(reversed(range(nc))) if forward else list(range(nc))
    consts = [_decay_consts(lg[h:h + 1, 0:1], forward) for h in range(H)]
    accs = [jnp.zeros((1, DH), F32) for _ in range(H)]

    def first(h, c):
        dm, xi, zc, dec, _ = consts[h]
        hs, rs = slice(h * DH, (h + 1) * DH), slice(c * CR, (c + 1) * CR)
        qc, kc, vc, doc = q_ref[rs, hs], k_ref[rs, hs], v_ref[rs, hs], do_ref[rs, hs]
        s, dsn = st_ref[c, h], ds_scr[h]
        sb, dsnb = s.astype(BF16), dsn.astype(BF16)
        qf, kf = qc.astype(F32), kc.astype(F32)
        a_raw = _dot_nt(qc, kc)
        da_raw = _dot_nt(doc, vc)
        xdo = (xi * doc.astype(F32)).astype(BF16)
        kz = (kf * zc).astype(BF16)
        vz = (vc.astype(F32) * zc).astype(BF16)
        dq_c = _dot_nt(xdo, sb)
        dk_s = _dot_nt(vz, dsnb)
        dv_s = _dot(kz, dsnb)
        ds_scr[h] = _dot_tn((xi * qf).astype(BF16), doc) + dec * dsn
        accs[h] = accs[h] + (float(CR) * dec) * jnp.sum(s * dsn, axis=0, keepdims=True)
        a = (a_raw * dm).astype(BF16)
        da = (da_raw * dm).astype(BF16)
        return a, da, dq_c, dk_s, dv_s

    def second(h, c, carried):
        a, da, dq_c, dk_s, dv_s = carried
        ic = consts[h][4]
        if forward:
            w_qi, w_qc, w_ki, w_ks = ic, ic + 1.0, -ic, (CR - 1.0) - ic
        else:
            w_qi, w_qc, w_ki, w_ks = -ic, CR - ic, ic, ic
        hs, rs = slice(h * DH, (h + 1) * DH), slice(c * CR, (c + 1) * CR)
        qc, kc, doc = q_ref[rs, hs], k_ref[rs, hs], do_ref[rs, hs]
        dq_i = _dot(da, kc)
        dk_i = _dot_tn(da, qc)
        dv_i = _dot_tn(a, doc)
        dq_ref[rs, hs] = (dq_i + dq_c).astype(BF16)
        dk_ref[rs, hs] = (dk_i + dk_s).astype(BF16)
        dv_ref[rs, hs] = (dv_i + dv_s).astype(BF16)
        rowterm = qc.astype(F32) * (w_qi * dq_i + w_qc * dq_c) + kc.astype(F32) * (w_ki * dk_i + w_ks * dk_s)
        accs[h] = accs[h] + jnp.sum(rowterm, axis=0, keepdims=True)

    def finish():
        for h in range(H):
            dlg_scr[row0 + h:row0 + h + 1, :] += accs[h]

    items = [[(functools.partial(first, h, c), functools.partial(second, h, c)) for h in range(H)] for c in order]
    return items, finish


def _ret_bwd_run(dirs):
    nc = len(dirs[0][0])
    flat = [it for j in range(nc) for items, _ in dirs for it in items[j]]
    pending = None
    for first, second in flat:
        carried = first()
        if pending is not None:
            pending[0](pending[1])
        pending = (second, carried)
    pending[0](pending[1])
    for _, finish in dirs:
        finish()


def _ret_bwd(q, k, v, dof, dob, sst, rst, rlf, rlb, *, tm):
    t_len = q.shape[0]
    nt, nc = t_len // tm, tm // CR

    def body(qf_ref, kf_ref, vf_ref, dof_ref, sst_ref, qb_ref, kb_ref, vb_ref, dob_ref, rst_ref, rlf_ref, rlb_ref,
             dqf_ref, dkf_ref, dvf_ref, dqb_ref, dkb_ref, dvb_ref, dscf_ref, dscb_ref, dlg_ref,
             dsf_scr, dsb_scr, dlg_scr):
        i = pl.program_id(0)

        @pl.when(i == 0)
        def _():
            dsf_scr[...] = jnp.zeros_like(dsf_scr)
            dsb_scr[...] = jnp.zeros_like(dsb_scr)
            dlg_scr[...] = jnp.zeros_like(dlg_scr)

        lgf = _log_sigmoid(rlf_ref[...])
        lgb = _log_sigmoid(rlb_ref[...])
        _ret_bwd_run([
            _ret_bwd_items(qf_ref, kf_ref, vf_ref, dof_ref, sst_ref, dqf_ref, dkf_ref, dvf_ref, dsf_scr, dlg_scr,
                           lgf, True, nc, 0),
            _ret_bwd_items(qb_ref, kb_ref, vb_ref, dob_ref, rst_ref, dqb_ref, dkb_ref, dvb_ref, dsb_scr, dlg_scr,
                           lgb, False, nc, H)])

        @pl.when(i == nt - 1)
        def _():
            dscf_ref[...] = dsf_scr[...]
            dscb_ref[...] = dsb_scr[...]
            tot = jnp.broadcast_to(jnp.sum(dlg_scr[...], axis=1, keepdims=True), (8, 128))
            ri = lax.broadcasted_iota(jnp.int32, (8, 128), 0)
            li = lax.broadcasted_iota(jnp.int32, (8, 128), 1)
            dlg_ref[...] = jnp.zeros_like(dlg_ref)
            dlg_ref[0:1, 0:128] = jnp.sum(jnp.where(ri == li, tot, 0.0), axis=0, keepdims=True)
            dlg_ref[1:2, 0:128] = jnp.sum(jnp.where(ri - H == li, tot, 0.0), axis=0, keepdims=True)

    rr = functools.partial(_rows_rev, nt=nt)
    st_f = pl.BlockSpec((nc, H, DH, DH), lambda i: (nt - 1 - i, 0, 0, 0))
    st_b = pl.BlockSpec((nc, H, DH, DH), lambda i: (i, 0, 0, 0))
    tok = jax.ShapeDtypeStruct((t_len, RW), BF16)
    st = jax.ShapeDtypeStruct((H, DH, DH), F32)
    return pl.pallas_call(
        body, name="ret_bwd", grid=(nt,),
        in_specs=[rr(tm, RW), rr(tm, RW), rr(tm, RW), rr(tm, RW), st_f,
                  _rows(tm, RW), _rows(tm, RW), _rows(tm, RW), _rows(tm, RW), st_b,
                  _whole((H, 128)), _whole((H, 128))],
        out_specs=[rr(tm, RW), rr(tm, RW), rr(tm, RW), _rows(tm, RW), _rows(tm, RW), _rows(tm, RW),
                   _acc((H, DH, DH)), _acc((H, DH, DH)), _acc((8, D))],
        out_shape=(tok, tok, tok, tok, tok, tok, st, st, jax.ShapeDtypeStruct((8, D), F32)),
        scratch_shapes=[pltpu.VMEM((H, DH, DH), F32), pltpu.VMEM((H, DH, DH), F32), pltpu.VMEM((8, 128), F32)],
        compiler_params=_cparams(1),
    )(q, k, v, dof, sst, q, k, v, dob, rst, rlf, rlb)


def _in_bwd(x, zuv, dya, dqf, dkf, dvf, dqb, dkb, dvb, dgfb, dx1, cos, sin, mod6, norm1, win, sgw, sgbt, sggain, *, tm):
    t_len = x.shape[0]
    nt, nc = t_len // tm, tm // C

    def body(x_ref, zuv_ref, dya_ref, dqf_ref, dkf_ref, dvf_ref, dqb_ref, dkb_ref, dvb_ref, dgfb_ref, dx1_ref,
             cos_ref, sin_ref, mod_ref, n1_ref, win_ref, sgw_ref, sgbt_ref, sgg_ref,
             gx_ref, dz_ref, small_ref, dsgw_ref, dsgbt_ref):
        i = pl.program_id(0)

        @pl.when(i == 0)
        def _():
            small_ref[...] = jnp.zeros_like(small_ref)
            dsgw_ref[...] = jnp.zeros_like(dsgw_ref)
            dsgbt_ref[...] = jnp.zeros_like(dsgbt_ref)

        zuv = zuv_ref[...].astype(F32)
        _, (ua, dgu, dgv, mixed, vn_b, vah_l, rv_l) = _sg_forward(zuv[:, :AW], zuv[:, AW:], sgw_ref, sgbt_ref, sgg_ref, nc)
        dya = dya_ref[...].astype(F32)
        dz_ref[:, 0:AW] = (dya * mixed * dgu).astype(BF16)
        dmixed = dya * ua
        gain = sgg_ref[...]
        for g in range(G):
            gs = slice(g * DH, (g + 1) * DH)
            dmg = dmixed[:, gs]
            dmb = dmg.astype(BF16)
            wgt = sgw_ref[g].astype(BF16)
            dsw = jnp.zeros((C, C), F32)
            dsb = jnp.zeros((C, DH), F32)
            rows = []
            for c in range(nc):
                rs = slice(c * C, (c + 1) * C)
                dsw = dsw + _dot_nt(dmb[rs, :], vn_b[g][rs, :])
                dsb = dsb + dmg[rs, :]
                rows.append(_dot_tn(wgt, dmb[rs, :]))
            dsgw_ref[g] += dsw
            dsgbt_ref[g] += dsb
            dvn = jnp.concatenate(rows, axis=0) if nc > 1 else rows[0]
            vah, rv = vah_l[g], rv_l[g]
            small_ref[3:4, gs] += jnp.sum(dvn * vah, axis=0, keepdims=True)
            dyg = dvn * gain[:, gs]
            dva = rv * (dyg - vah * jnp.mean(dyg * vah, axis=-1, keepdims=True))
            dz_ref[:, AW + g * DH:AW + (g + 1) * DH] = (dva * dgv[:, gs]).astype(BF16)

        cos = jnp.tile(cos_ref[...], (1, H))
        nsins = -jnp.tile(sin_ref[...], (1, H))
        def both(f_ref, b_ref):
            return f_ref[...].astype(F32) + b_ref[...].astype(F32)

        dz_ref[:, Q_LO:KV_LO] = _rope(both(dqf_ref, dqb_ref), cos, nsins).astype(BF16)
        dz_ref[:, KV_LO:VR_LO] = (_rope(both(dkf_ref, dkb_ref), cos, nsins) * KSCALE).astype(BF16)
        dz_ref[:, VR_LO:G_LO] = both(dvf_ref, dvb_ref).astype(BF16)
        dz_ref[:, G_LO:INC] = dgfb_ref[...]

        dhx = _dot(dz_ref[...], win_ref[...])
        x = x_ref[...]
        r = lax.rsqrt(jnp.mean(x * x, axis=-1, keepdims=True) + EPS)
        xh = x * r
        n1, sc1 = n1_ref[...], mod_ref[1:2, :]
        small_ref[0:1, :] += jnp.sum(dhx, axis=0, keepdims=True)
        dhxx = jnp.sum(dhx * xh, axis=0, keepdims=True)
        small_ref[1:2, :] += dhxx * n1
        small_ref[2:3, :] += dhxx * (1.0 + sc1)
        dyg = dhx * (n1 * (1.0 + sc1))
        gx_ref[...] = dx1_ref[...] + r * (dyg - xh * jnp.mean(dyg * xh, axis=-1, keepdims=True))

        @pl.when(i == nt - 1)
        def _():
            ri = lax.broadcasted_iota(jnp.int32, (C, DH), 0)
            li = lax.broadcasted_iota(jnp.int32, (C, DH), 1)
            for g in range(G):
                tot = jnp.broadcast_to(jnp.sum(dsgbt_ref[g], axis=1, keepdims=True), (C, DH))
                small_ref[4 + g:5 + g, 0:DH] = jnp.sum(jnp.where(ri == li, tot, 0.0), axis=0, keepdims=True)

    tok = functools.partial(_rows, tm)
    return pl.pallas_call(
        body, name="in_bwd", grid=(nt,),
        in_specs=[tok(D), tok(2 * AW), tok(AW), tok(RW), tok(RW), tok(RW), tok(RW), tok(RW), tok(RW),
                  tok(2 * RW), tok(D), tok(DH), tok(DH), _whole((6, D)), _whole((1, D)), _whole((INC, D)),
                  _whole((G, C, C)), _whole((G, C, 128)), _whole((1, AW))],
        out_specs=[tok(D), tok(INC), _acc((8, D)), _acc((G, C, C))],
        out_shape=(jax.ShapeDtypeStruct((t_len, D), F32), jax.ShapeDtypeStruct((t_len, INC), BF16),
                   jax.ShapeDtypeStruct((8, D), F32), jax.ShapeDtypeStruct((G, C, C), F32)),
        scratch_shapes=[pltpu.VMEM((G, C, 128), F32)],
        compiler_params=_cparams(1),
    )(x, zuv, dya, dqf, dkf, dvf, dqb, dkb, dvb, dgfb, dx1, cos, sin, mod6, norm1, win, sgw, sgbt, sggain)


def _tn_matmul(a, b, *, bm, bt, name, init=None, init_rows=None, after=(), cols=None):
    t_len, m = a.shape
    cj, n = (0, b.shape[1]) if cols is None else cols
    ni, ntt = m // bm, t_len // bt
    has_init = init is not None

    def body(*refs):
        o_ref, ob_ref = refs[-2:]
        a_ref, b_ref = refs[:2]
        init_ref = refs[2] if has_init else None
        i, t = pl.program_id(0), pl.program_id(1)

        @pl.when(t == 0)
        def _():
            o_ref[...] = jnp.zeros_like(o_ref)

        if has_init:
            for ib in range(ni):
                lo, hi = max(init_rows[0], ib * bm), min(init_rows[1], (ib + 1) * bm)
                if lo < hi:
                    @pl.when(jnp.logical_and(t == 0, i == ib))
                    def _(lo=lo, hi=hi, ib=ib):
                        o_ref[lo - ib * bm:hi - ib * bm, :] = init_ref[lo - init_rows[0]:hi - init_rows[0], :]

        o_ref[...] += _dot_tn(a_ref[...], b_ref[...])

        @pl.when(t == ntt - 1)
        def _():
            ob_ref[...] = o_ref[...].astype(BF16)

    in_specs = [pl.BlockSpec((bt, bm), lambda i, t: (t, i)), pl.BlockSpec((bt, n), lambda i, t: (t, cj))]
    args = [a, b]
    if has_init:
        in_specs.append(pl.BlockSpec((init.shape[0], n), lambda i, t: (0, cj), pipeline_mode=pl.Buffered(1)))
        args.append(init)
    for arr in after:
        in_specs.append(pl.BlockSpec(memory_space=pl.ANY))
        args.append(arr)
    out_spec = pl.BlockSpec((bm, n), lambda i, t: (i, 0))
    return pl.pallas_call(
        body, name=name, grid=(ni, ntt),
        in_specs=in_specs,
        out_specs=[out_spec, out_spec],
        out_shape=(jax.ShapeDtypeStruct((m, n), F32), jax.ShapeDtypeStruct((m, n), BF16)),
        compiler_params=_cparams(2),
    )(*args)


def _dw_out(ycat, dxb, mod6, wout, *, bt):
    t_len = ycat.shape[0]
    ntt = t_len // bt

    def body(a_ref, b_ref, mod_ref, wout_ref, o_ref, ob_ref, dg1_ref):
        t = pl.program_id(0)

        @pl.when(t == 0)
        def _():
            o_ref[...] = jnp.zeros_like(o_ref)

        o_ref[...] += _dot_tn(a_ref[...], b_ref[...])

        @pl.when(t == ntt - 1)
        def _():
            m = o_ref[...]
            dg1_ref[...] = jnp.zeros_like(dg1_ref)
            dg1_ref[0:1, :] = jnp.sum(m * wout_ref[...].astype(F32), axis=0, keepdims=True)
            g = m * mod_ref[2:3, :]
            o_ref[...] = g
            ob_ref[...] = g.astype(BF16)

    return pl.pallas_call(
        body, name="dw_out", grid=(ntt,),
        in_specs=[pl.BlockSpec((bt, D), lambda t: (t, 0)), pl.BlockSpec((bt, D), lambda t: (t, 0)),
                  _whole((6, D)), _whole((D, D))],
        out_specs=[_acc((D, D)), _acc((D, D)), _acc((8, D))],
        out_shape=(jax.ShapeDtypeStruct((D, D), F32), jax.ShapeDtypeStruct((D, D), BF16),
                   jax.ShapeDtypeStruct((8, D), F32)),
        compiler_params=_cparams(1),
    )(ycat, dxb, mod6, wout)


def _ctx_bwd(ctx, hc, kvc, cmod6, norm1, win, rlf, rlb, dscf, dscb, dlg_in):
    lc = ctx.shape[0]

    def body(ctx_ref, hc_ref, kvc_ref, cmod_ref, n1_ref, win_ref, rlf_ref, rlb_ref, dscf_ref, dscb_ref, dlgin_ref,
             dwin_ref, small_ref, dlg_ref):
        k = kvc_ref[:, :RW]
        v = kvc_ref[:, RW:]
        t = lax.broadcasted_iota(jnp.int32, (lc, 1), 0).astype(F32)
        lgf = _log_sigmoid(rlf_ref[...])
        lgb = _log_sigmoid(rlb_ref[...])
        dks, dvs = [], []
        lane = lax.broadcasted_iota(jnp.int32, (1, 128), 1)
        row_f = jnp.zeros((1, 128), F32)
        row_b = jnp.zeros((1, 128), F32)
        for h in range(H):
            hs = slice(h * DH, (h + 1) * DH)
            wf = jnp.exp(lgf[h:h + 1, 0:1] * (lc - 1.0 - t))
            wb = jnp.exp(lgb[h:h + 1, 0:1] * t)
            kh, vhb = k[:, hs], v[:, hs].astype(BF16)
            dsf, dsb = dscf_ref[h].astype(BF16), dscb_ref[h].astype(BF16)
            dkf = wf * _dot_nt(vhb, dsf)
            dkb = wb * _dot_nt(vhb, dsb)
            dvs.append(_dot((kh * wf).astype(BF16), dsf) + _dot((kh * wb).astype(BF16), dsb))
            dks.append((dkf + dkb) * KSCALE)
            lf = jnp.sum((lc - 1.0 - t) * kh * dkf, axis=0, keepdims=True)
            lb = jnp.sum(t * kh * dkb, axis=0, keepdims=True)
            row_f = row_f + jnp.where(lane == h, jnp.sum(lf, axis=1, keepdims=True), 0.0)
            row_b = row_b + jnp.where(lane == h, jnp.sum(lb, axis=1, keepdims=True), 0.0)
        dlg_ref[...] = dlgin_ref[...]
        dlg_ref[0:1, 0:128] += row_f
        dlg_ref[1:2, 0:128] += row_b
        dkv = jnp.concatenate(dks + dvs, axis=1).astype(BF16)
        dhc = _dot(dkv, win_ref[KV_LO:KV_HI, :])
        dwin_ref[...] = _dot_tn(dkv, hc_ref[...])
        x = ctx_ref[...]
        r = lax.rsqrt(jnp.mean(x * x, axis=-1, keepdims=True) + EPS)
        xh = x * r
        small_ref[...] = jnp.zeros_like(small_ref)
        small_ref[0:1, :] = jnp.sum(dhc, axis=0, keepdims=True)
        dhxx = jnp.sum(dhc * xh, axis=0, keepdims=True)
        small_ref[1:2, :] = dhxx * n1_ref[...]
        small_ref[2:3, :] = dhxx * (1.0 + cmod_ref[1:2, :])

    return pl.pallas_call(
        body, name="ctx_bwd",
        out_shape=(jax.ShapeDtypeStruct((2 * RW, D), F32), jax.ShapeDtypeStruct((8, D), F32),
                   jax.ShapeDtypeStruct((8, D), F32)),
        compiler_params=_cparams(),
    )(ctx, hc, kvc, cmod6, norm1, win, rlf, rlb, dscf, dscb, dlg_in)


def _adam_math(w, g, m, v):
    m = ADAM_B1 * m + (1.0 - ADAM_B1) * g
    v = ADAM_B2 * v + (1.0 - ADAM_B2) * (g * g)
    m_hat = m / (1.0 - ADAM_B1 ** ADAM_STEP)
    v_hat = v / (1.0 - ADAM_B2 ** ADAM_STEP)
    delta = -ADAM_LR * (m_hat / (jnp.sqrt(v_hat) + ADAM_EPS) + ADAM_WD * w)
    return delta, m, v


def _place():
    x, y, c = lax.axis_index("x"), lax.axis_index("y"), lax.axis_index("c")
    return x, y, c, 4 * x + 2 * y + c


def _flip(x, y, c, k):
    px = 1 - x if (k >> 2) & 1 else x
    py = 1 - y if (k >> 1) & 1 else y
    pc = 1 - c if k & 1 else c
    return (px, py, pc), 4 * px + 2 * py + pc


def _gather_copy(buf_ref, send_sems, recv_sems, sem0, k, mine):
    x, y, c, me = _place()
    dev, idx = _flip(x, y, c, k)
    blk = me if mine else idx
    return pltpu.make_async_remote_copy(
        src_ref=buf_ref.at[blk], dst_ref=buf_ref.at[blk],
        send_sem=send_sems.at[sem0 + k - 1], recv_sem=recv_sems.at[sem0 + k - 1],
        device_id=dev, device_id_type=MESH)


def _entry_gather(c_row, cctx_row, wmod, bmod, shards):
    n = len(shards)
    ncol = wmod.shape[1]

    def body(*refs):
        c_ref, cctx_ref, wmod_ref, bmod_ref = refs[:4]
        in_refs = refs[4:4 + n]
        mod_ref, cs_ref = refs[4 + n:6 + n]
        out_refs = refs[6 + n:6 + 2 * n]
        cbuf, pbuf = refs[6 + 2 * n:8 + 2 * n]
        cast_refs = refs[8 + 2 * n:8 + 3 * n]
        small_send, small_recv, send_sems, recv_sems, local_sems = refs[8 + 3 * n:]
        x, y, c, me = _place()
        sib = (x, y, 1 - c)
        chips = [(1 - x, y), (x, 1 - y), (1 - x, 1 - y)]

        cbuf[me] = jnp.broadcast_to(c_ref[...], (8, D))
        small = [_gather_copy(cbuf, small_send, small_recv, 0, k, True) for k in range(1, NDEV)]
        for cp in small:
            cp.start()

        def blk(px, py, pc):
            return 4 * px + 2 * py + pc

        def copy(w, k, block, to, src=None):
            dst = out_refs[w].at[block]
            return pltpu.make_async_remote_copy(
                src_ref=dst if src is None else src, dst_ref=dst,
                send_sem=send_sems.at[w, k], recv_sem=recv_sems.at[w, k], device_id=to, device_id_type=MESH)

        first, passed, mine = [], [], []
        for w in range(n):
            cast_refs[w][...] = in_refs[w][...].astype(BF16)
            m = pltpu.make_async_copy(cast_refs[w], out_refs[w].at[me], local_sems.at[w])
            m.start()
            mine.append(m)
            cps = [copy(w, 0, me, sib, src=cast_refs[w])]
            cps += [copy(w, 1 + j, me, (*chip, c), src=cast_refs[w]) for j, chip in enumerate(chips)]
            for cp in cps:
                cp.start()
            first += cps

        for k in range(1, NDEV):
            _gather_copy(cbuf, small_send, small_recv, 0, k, False).wait_recv()
        row = lax.broadcasted_iota(jnp.int32, (16, D), 0)
        call = jnp.where(row == 8, jnp.broadcast_to(cctx_ref[...], (16, D)), 0.0)
        for d in range(NDEV):
            call = jnp.where(row == d, jnp.concatenate([cbuf[d], cbuf[d]], axis=0), call)
        cs = call * _sigmoid(call)
        cs_ref[...] = cs
        pbuf[me] = _dot(cs.astype(BF16), wmod_ref[...].astype(BF16))
        small2 = [_gather_copy(pbuf, small_send, small_recv, NDEV - 1, k, True) for k in range(1, NDEV)]
        for cp in small2:
            cp.start()

        for w in range(n):
            for j, chip in enumerate(chips):
                b = blk(*chip, c)
                copy(w, 1 + j, b, (x, y, c)).wait_recv()
                fw = copy(w, 4 + j, b, sib)
                fw.start()
                passed.append(fw)
        for w in range(n):
            copy(w, 0, blk(x, y, 1 - c), (x, y, c)).wait_recv()
            for j, chip in enumerate(chips):
                copy(w, 4 + j, blk(*chip, 1 - c), (x, y, c)).wait_recv()

        for k in range(1, NDEV):
            _gather_copy(pbuf, small_send, small_recv, NDEV - 1, k, False).wait_recv()
        for d in range(NDEV):
            mod_ref[:, d * ncol:(d + 1) * ncol] = pbuf[d] + bmod_ref[:, d * ncol:(d + 1) * ncol]
        for cp in small + small2 + first + passed:
            cp.wait_send()
        for m in mine:
            m.wait()

    vm = pl.BlockSpec(memory_space=pltpu.VMEM)
    hbm = pl.BlockSpec(memory_space=pltpu.HBM)
    return pl.pallas_call(
        body, name="entry_gather",
        in_specs=[vm] * (4 + n), out_specs=[vm, vm] + [hbm] * n,
        out_shape=(jax.ShapeDtypeStruct((16, 6 * D), F32), jax.ShapeDtypeStruct((16, D), F32))
        + tuple(jax.ShapeDtypeStruct((NDEV,) + s.shape, BF16) for s in shards),
        scratch_shapes=[pltpu.VMEM((NDEV, 8, D), F32), pltpu.VMEM((NDEV, 16, ncol), F32)]
        + [pltpu.VMEM(s.shape, BF16) for s in shards]
        + [pltpu.SemaphoreType.DMA((2 * (NDEV - 1),)), pltpu.SemaphoreType.DMA((2 * (NDEV - 1),)),
           pltpu.SemaphoreType.DMA((n, 7)), pltpu.SemaphoreType.DMA((n, 7)), pltpu.SemaphoreType.DMA((n,))],
        compiler_params=_cparams(),
    )(c_row, cctx_row, wmod, bmod, *shards)


_HBM = pl.BlockSpec(memory_space=pltpu.HBM)
_SEMS = pl.BlockSpec(memory_space=pltpu.SEMAPHORE)
_EFFECT = pltpu.SideEffectType.DATAFLOW_SIDE_EFFECTING


def _exchange_copy(src_refs, land_refs, send_sems, recv_sems, w, k, scatter, landing_slot_is_mine):
    x, y, c, me = _place()
    dev, pidx = _flip(x, y, c, k)
    src = src_refs[w].at[pidx] if scatter else src_refs[w]
    slot = me if landing_slot_is_mine else pidx
    return pltpu.make_async_remote_copy(
        src_ref=src, dst_ref=land_refs[w].at[slot],
        send_sem=send_sems.at[w * (NDEV - 1) + k - 1], recv_sem=recv_sems.at[w * (NDEV - 1) + k - 1],
        device_id=dev, device_id_type=MESH)


def _exchange_start(srcs, *, scatter, name):
    n = len(srcs)
    lands = [lax.empty((NDEV,) + tuple(s.shape[-2:]), s.dtype) for s in srcs]

    def body(*refs):
        src_refs, land_refs = refs[:n], refs[n:2 * n]
        send_sems, recv_sems = refs[2 * n], refs[2 * n + 1]
        token = refs[-1]
        for w in range(n):
            for k in range(1, NDEV):
                _exchange_copy(src_refs, land_refs, send_sems, recv_sems, w, k, scatter, True).start()
        token[...] = jnp.zeros_like(token)

    arrs = list(srcs) + lands
    out_shape = ([pltpu.SemaphoreType.DMA((n * (NDEV - 1),)), pltpu.SemaphoreType.DMA((n * (NDEV - 1),))]
                 + [pltpu.HBM(a.shape, a.dtype) for a in arrs] + [jax.ShapeDtypeStruct((8, 128), F32)])
    res = pl.pallas_call(
        body, name=name, out_shape=tuple(out_shape),
        in_specs=[_HBM] * (2 * n),
        out_specs=tuple([_SEMS, _SEMS] + [_HBM] * (2 * n) + [pl.BlockSpec(memory_space=pltpu.VMEM)]),
        input_output_aliases={i: 2 + i for i in range(2 * n)},
        compiler_params=pltpu.CompilerParams(has_side_effects=_EFFECT),
    )(*[pltpu.with_memory_space_constraint(a, pltpu.HBM) for a in arrs])
    return res[:-1], res[-1]


def _exchange_wait(handles, after, *, scatter, name):
    n = (len(handles) - 2) // 2
    na = len(after)

    def body(*refs):
        src_refs, land_refs = refs[:n], refs[n:2 * n]
        send_sems, recv_sems = refs[2 * n], refs[2 * n + 1]
        for w in range(n):
            for k in range(1, NDEV):
                cp = _exchange_copy(src_refs, land_refs, send_sems, recv_sems, w, k, scatter, False)
                cp.wait_send()
                cp.wait_recv()

    arrs = handles[2:]
    res = pl.pallas_call(
        body, name=name, out_shape=tuple(pltpu.HBM(a.shape, a.dtype) for a in arrs),
        in_specs=[_HBM] * (2 * n) + [_SEMS, _SEMS] + [_HBM] * na,
        out_specs=tuple([_HBM] * (2 * n)),
        input_output_aliases={i: i for i in range(2 * n)},
        compiler_params=pltpu.CompilerParams(has_side_effects=_EFFECT),
    )(*arrs, handles[0], handles[1], *[pltpu.with_memory_space_constraint(a, pltpu.HBM) for a in after])
    return res[:n], res[n:]


_SAME_CORE = (2, 4, 6)


def _gather2_copy(src_ref, land_ref, send_sems, recv_sems, sem, k, block):
    x, y, c, _ = _place()
    dev, _ = _flip(x, y, c, k)
    dst = land_ref.at[block]
    return pltpu.make_async_remote_copy(
        src_ref=dst if src_ref is None else src_ref, dst_ref=dst,
        send_sem=send_sems.at[sem], recv_sem=recv_sems.at[sem], device_id=dev, device_id_type=MESH)


def _split_call(body, name, arrs, n_sems, sems=None, after=(), token=False):
    na = len(arrs)
    out_shape, out_specs = [], []
    if n_sems is not None:
        out_shape += [pltpu.SemaphoreType.DMA((n_sems,)), pltpu.SemaphoreType.DMA((n_sems,))]
        out_specs += [_SEMS, _SEMS]
    first = len(out_shape)
    out_shape += [pltpu.HBM(a.shape, a.dtype) for a in arrs]
    out_specs += [_HBM] * na
    if token:
        out_shape.append(jax.ShapeDtypeStruct((8, 128), F32))
        out_specs.append(pl.BlockSpec(memory_space=pltpu.VMEM))
    in_specs = [_HBM] * na + ([_SEMS, _SEMS] if sems is not None else []) + [_HBM] * len(after)
    args = [pltpu.with_memory_space_constraint(a, pltpu.HBM) for a in arrs] + list(sems or ()) \
        + [pltpu.with_memory_space_constraint(a, pltpu.HBM) for a in after]
    return pl.pallas_call(
        body, name=name, out_shape=tuple(out_shape), in_specs=in_specs, out_specs=tuple(out_specs),
        input_output_aliases={i: first + i for i in range(na)},
        compiler_params=pltpu.CompilerParams(has_side_effects=_EFFECT))(*args)


def _gather2_start(srcs, *, name):
    n = len(srcs)
    lands = [lax.empty((NDEV,) + tuple(s.shape), s.dtype) for s in srcs]

    def body(*refs):
        src_refs, land_refs = refs[:n], refs[n:2 * n]
        send_sems, recv_sems = refs[2 * n], refs[2 * n + 1]
        _, _, _, me = _place()
        for w in range(n):
            for slot, k in enumerate((1,) + _SAME_CORE):
                _gather2_copy(src_refs[w], land_refs[w], send_sems, recv_sems, 4 * w + slot, k, me).start()
        refs[-1][...] = jnp.zeros_like(refs[-1])

    res = _split_call(body, name, list(srcs) + lands, 4 * n, token=True)
    return res[:2], res[2:-1], res[-1]


def _gather2_arrived(sems, arrs, after, *, name):
    n = len(arrs) // 2

    def body(*refs):
        src_refs, land_refs = refs[:n], refs[n:2 * n]
        send_sems, recv_sems = refs[2 * n], refs[2 * n + 1]
        x, y, c, me = _place()
        for w in range(n):
            for slot, k in enumerate((1,) + _SAME_CORE):
                _, pidx = _flip(x, y, c, k)
                _gather2_copy(src_refs[w], land_refs[w], send_sems, recv_sems, 4 * w + slot, k, me).wait_send()
                _gather2_copy(None, land_refs[w], send_sems, recv_sems, 4 * w + slot, k, pidx).wait_recv()

    return _split_call(body, name, arrs, None, sems=sems, after=after)


def _gather2_forward(lands, *, name):
    n = len(lands)

    def body(*refs):
        land_refs = refs[:n]
        send_sems, recv_sems = refs[n], refs[n + 1]
        x, y, c, _ = _place()
        for w in range(n):
            for j, k in enumerate(_SAME_CORE):
                _, pidx = _flip(x, y, c, k)
                _gather2_copy(None, land_refs[w], send_sems, recv_sems, 3 * w + j, 1, pidx).start()
        refs[-1][...] = jnp.zeros_like(refs[-1])

    res = _split_call(body, name, list(lands), 3 * n, token=True)
    return res[:2], res[2:-1], res[-1]


def _gather2_wait(sems, lands, after, *, name):
    n = len(lands)

    def body(*refs):
        land_refs = refs[:n]
        send_sems, recv_sems = refs[n], refs[n + 1]
        x, y, c, _ = _place()
        for w in range(n):
            for j, k in enumerate(_SAME_CORE):
                _, mine = _flip(x, y, c, k)
                _, theirs = _flip(x, y, c, k ^ 1)
                _gather2_copy(None, land_refs[w], send_sems, recv_sems, 3 * w + j, 1, mine).wait_send()
                _gather2_copy(None, land_refs[w], send_sems, recv_sems, 3 * w + j, 1, theirs).wait_recv()

    return _split_call(body, name, list(lands), None, sems=sems, after=after)


def _cast_bf16(arrs, *, name, after=()):
    n = len(arrs)

    def body(*refs):
        for i_ref, o_ref in zip(refs[:n], refs[n + len(after):]):
            o_ref[...] = i_ref[...].astype(BF16)

    return pl.pallas_call(
        body, name=name, out_shape=tuple(jax.ShapeDtypeStruct(a.shape, BF16) for a in arrs),
        in_specs=[pl.BlockSpec(memory_space=pltpu.VMEM)] * n + [pl.BlockSpec(memory_space=pl.ANY)] * len(after),
        compiler_params=_cparams())(*arrs, *after)


def _rs_adam(me_arr, fulls, lands, w, m, v, *, name):
    rows = lands[0].shape[1]
    k = len(fulls)
    nb = next(n for n in (4, 2, 1) if rows % (16 * n) == 0)
    br = rows // nb

    def body(me_ref, *refs):
        own_refs, land_refs = refs[:k], refs[k:2 * k]
        w_ref, m_ref, v_ref, g_ref, d_ref, mo_ref, vo_ref = refs[2 * k:]
        me = me_ref[0]
        parts = []
        for own_ref, land_ref in zip(own_refs, land_refs):
            acc = jnp.zeros(own_ref.shape, F32)
            for p in range(NDEV):
                acc = acc + jnp.where(p == me, own_ref[...], land_ref[p].astype(F32))
            parts.append(acc)
        acc = parts[0] if k == 1 else jnp.concatenate(parts, axis=1)
        g_ref[...] = acc
        d_ref[...], mo_ref[...], vo_ref[...] = _adam_math(w_ref[...], acc, m_ref[...], v_ref[...])

    sh = jax.ShapeDtypeStruct((rows, D), F32)
    blk = pl.BlockSpec((br, D), lambda i, me: (i, 0))
    grid_spec = pltpu.PrefetchScalarGridSpec(
        num_scalar_prefetch=1, grid=(nb,),
        in_specs=[pl.BlockSpec((br, f.shape[1]), lambda i, me: (me[0] * nb + i, 0)) for f in fulls]
        + [pl.BlockSpec((NDEV, br, l.shape[2]), lambda i, me: (0, i, 0)) for l in lands]
        + [blk, blk, blk],
        out_specs=[blk] * 4)
    return pl.pallas_call(
        body, name=name, out_shape=(sh, sh, sh, sh), grid_spec=grid_spec, compiler_params=_cparams(1),
    )(me_arr, *fulls, *lands, w, m, v)


ROW_F, ROW_I, ROW_C, ROW_G1, ROW_LG = 0, 8, 16, 24, 32
PACK_ROWS = 40


def _small_sum(me_arr, pack, pack_land, sgw, sgw_land, wmod):
    ncol = wmod.shape[1]

    def body(me_ref, pack_ref, pland_ref, sgw_ref, sland_ref, wmod_ref, sum_ref, all_ref, sgw_sum_ref, part_ref):
        me = me_ref[0]
        acc = jnp.zeros((PACK_ROWS, D), F32)
        acc_w = jnp.zeros(sgw_ref.shape, F32)
        for p in range(NDEV):
            rows = jnp.where(p == me, pack_ref[...], pland_ref[p])
            all_ref[p] = rows
            acc = acc + rows
            acc_w = acc_w + jnp.where(p == me, sgw_ref[...], sland_ref[p])
        sum_ref[...] = acc
        sgw_sum_ref[...] = acc_w
        zero = jnp.zeros((1, D), F32)
        dcmod = jnp.concatenate([acc[ROW_C:ROW_C + 1], acc[ROW_C + 1:ROW_C + 2], zero, zero, zero, zero], axis=1)
        mine = jnp.zeros((1, ncol), F32)
        for d in range(NDEV):
            mine = mine + jnp.where(d == me, dcmod[:, d * ncol:(d + 1) * ncol], 0.0)
        part_ref[...] = _dot_nt(jnp.broadcast_to(mine, (8, ncol)).astype(BF16), wmod_ref[...].astype(BF16))

    vm = pl.BlockSpec(memory_space=pltpu.VMEM)
    return pl.pallas_call(
        body, name="small_sum",
        out_shape=(jax.ShapeDtypeStruct((PACK_ROWS, D), F32), jax.ShapeDtypeStruct((NDEV, PACK_ROWS, D), F32),
                   jax.ShapeDtypeStruct(sgw.shape, F32), jax.ShapeDtypeStruct((8, D), F32)),
        in_specs=[pl.BlockSpec(memory_space=pltpu.SMEM)] + [vm] * 5,
        compiler_params=_cparams(),
    )(me_arr, pack, pack_land, sgw, sgw_land, wmod)


def _cctx_final(me_arr, part, part_land, cctx_row, m_row, v_row):
    def body(me_ref, part_ref, land_ref, c_ref, m_ref, v_ref, g_ref, d_ref, mo_ref, vo_ref):
        me = me_ref[0]
        tot = jnp.zeros((8, D), F32)
        for p in range(NDEV):
            tot = tot + jnp.where(p == me, part_ref[...], land_ref[p])
        cc = c_ref[...]
        _, dsilu = _silu_parts(cc)
        g = tot[0:1, :] * dsilu
        g_ref[...] = g
        d_ref[...], mo_ref[...], vo_ref[...] = _adam_math(cc, g, m_ref[...], v_ref[...])

    row = jax.ShapeDtypeStruct((1, D), F32)
    vm = pl.BlockSpec(memory_space=pltpu.VMEM)
    return pl.pallas_call(
        body, name="cctx_final", out_shape=(row, row, row, row),
        in_specs=[pl.BlockSpec(memory_space=pltpu.SMEM)] + [vm] * 5,
        compiler_params=_cparams(),
    )(me_arr, part, part_land, cctx_row, m_row, v_row)


def _adam_small(s, sgw_g, params):
    names = ["norm1", "norm2", "norm_f", "sg_gain", "sg_b", "ret_logit_f", "ret_logit_b", "b_mod", "sg_w"]
    flat = [a for n in names for a in params[n]]

    def body(*refs):
        s_ref, sgw_ref = refs[0], refs[1]
        p_refs = refs[2:2 + 3 * len(names)]
        o_refs = refs[2 + 3 * len(names):]
        loss_ref = o_refs[-1]

        def row(r):
            return s_ref[r:r + 1, :]

        grads = {
            "norm1": row(ROW_I + 2) + row(ROW_C + 2),
            "norm2": row(ROW_F + 4),
            "norm_f": row(ROW_F + 0),
            "sg_gain": s_ref[ROW_I + 3:ROW_I + 4, 0:AW],
            "sg_b": s_ref[ROW_I + 4:ROW_I + 8, 0:DH],
            "sg_w": sgw_ref[...],
        }
        for j, n in enumerate(names):
            w_ref, m_ref, v_ref = p_refs[3 * j:3 * j + 3]
            g_ref, d_ref, mo_ref, vo_ref = o_refs[4 * j:4 * j + 4]
            w = w_ref[...]
            if n in ("ret_logit_f", "ret_logit_b"):
                r = ROW_LG + (0 if n == "ret_logit_f" else 1)
                g = s_ref[r:r + 1, 0:H] * _sigmoid(-w)
            elif n == "b_mod":
                rows = [row(ROW_I + 0) + row(ROW_C + 0), row(ROW_I + 1) + row(ROW_C + 1), row(ROW_G1),
                        row(ROW_F + 2), row(ROW_F + 3), row(ROW_F + 1)]
                g = jnp.concatenate(rows, axis=1)
            else:
                g = grads[n]
            g_ref[...] = g
            d_ref[...], mo_ref[...], vo_ref[...] = _adam_math(w, g, m_ref[...], v_ref[...])
        loss_ref[...] = jnp.full((1, 1), 0.5 / D, F32) * jnp.sum(row(ROW_F + 5), axis=1, keepdims=True)

    out_shape = []
    for n in names:
        w = params[n][0]
        out_shape += [jax.ShapeDtypeStruct(w.shape, F32)] * 4
    out_shape.append(jax.ShapeDtypeStruct((1, 1), F32))
    outs = pl.pallas_call(body, name="adam_small", out_shape=tuple(out_shape), compiler_params=_cparams())(
        s, sgw_g, *flat)
    return {n: tuple(outs[4 * j:4 * j + 4]) for j, n in enumerate(names)}, outs[-1]


def _wmod_grad(cs_all, dmod_cols, wmod, m, v):
    ncol = wmod.shape[1]

    def body(cs_ref, dm_ref, w_ref, m_ref, v_ref, g_ref, d_ref, mo_ref, vo_ref):
        g = _dot_tn(cs_ref[...].astype(BF16), dm_ref[...].astype(BF16))
        g_ref[...] = g
        d_ref[...], mo_ref[...], vo_ref[...] = _adam_math(w_ref[...], g, m_ref[...], v_ref[...])

    sh = jax.ShapeDtypeStruct((D, ncol), F32)
    br = D // 4
    blk = pl.BlockSpec((br, ncol), lambda i: (i, 0))
    return pl.pallas_call(
        body, name="wmod_grad", out_shape=(sh, sh, sh, sh), grid=(D // br,),
        in_specs=[pl.BlockSpec((cs_all.shape[0], br), lambda i: (0, i)), _whole(dmod_cols.shape), blk, blk, blk],
        out_specs=[blk] * 4,
        compiler_params=_cparams(1),
    )(cs_all, dmod_cols, wmod, m, v)


def _rope_tables(t_len):
    n_freq = DH // 4
    inv = (ROPE_BASE ** (-np.arange(n_freq, dtype=np.float32) / n_freq)).astype(np.float32)
    tok = np.arange(t_len)
    rows = (tok // GRID_W).astype(np.float32)
    cols = (tok % GRID_W).astype(np.float32)
    ang_r = rows[:, None] * inv[None, :]
    ang_c = cols[:, None] * inv[None, :]
    cos = np.concatenate([np.cos(ang_r)] * 2 + [np.cos(ang_c)] * 2, axis=1).astype(np.float32)
    sin = np.concatenate([-np.sin(ang_r), np.sin(ang_r), -np.sin(ang_c), np.sin(ang_c)], axis=1).astype(np.float32)
    return jnp.asarray(cos), jnp.asarray(sin)


def _local_step(x, tgt, ctx, mod6, cmod6, norm1, norm2, normf, win, wout, ffn_weights, sgw, sgb, sggain, rlf, rlb,
                *, tm_a, tm_b, tm_f, tm_m, tm_r, tm_i, bt_w, after_first_grads=None, small_path=None,
                after_in_half=None):
    t_len = x.shape[0]
    cos, sin = _rope_tables(t_len)
    sgbt = jnp.broadcast_to(sgb[:, :, None], (G, C, 128))
    rlf = jnp.broadcast_to(rlf.reshape(H, 1), (H, 128))
    rlb = jnp.broadcast_to(rlb.reshape(H, 1), (H, 128))
    hc, kvc, scf, scb = _ctx_fwd(ctx, cmod6, norm1, win, rlf, rlb)
    hx, zuv, q, k, v, gfb, of, ya, sst = _fwd_a(x, mod6, norm1, win, sgw, sgbt, sggain, cos, sin, rlf, scf, tm=tm_a)
    if callable(wout):
        wout, tok = wout(hx)
        rlb = rlb + tok
    ob, ycat, x1, rst = _fwd_b(q, k, v, of, gfb, ya, x, mod6, wout, rlb, scb, tm=tm_b)
    wg, wu, wd = ffn_weights(x1) if callable(ffn_weights) else ffn_weights
    dx1, h2, da, db, hm, df, small_f, dxb = _ffn(x1, tgt, mod6, norm2, normf, wg, wu, wd, tm=tm_f)
    bt2 = min(2 * bt_w, t_len)
    d_wd, d_wd_b = _tn_matmul(hm, df, bm=DFF // 2, bt=bt2, name="dw_down")
    d_wg, d_wg_b = _tn_matmul(da, h2, bm=DFF // 2, bt=bt2, name="dw_gate")
    d_wu, d_wu_b = _tn_matmul(db, h2, bm=DFF // 2, bt=bt2, name="dw_up")
    dya, dgfb, dof, dob = _mid_bwd(dxb, of, ob, gfb, mod6, wout, tm=tm_m)
    d_wo, d_wo_b, dg1 = _dw_out(ycat, dxb, mod6, wout, bt=bt2)
    if after_first_grads is not None:
        rlf = rlf + after_first_grads((d_wd_b, d_wg_b, d_wu_b, d_wo_b))
    dqf, dkf, dvf, dqb, dkb, dvb, dscf, dscb, dlg = _ret_bwd(q, k, v, dof, dob, sst, rst, rlf, rlb, tm=tm_r)
    gx, dz, small_i, dsgw = _in_bwd(x, zuv, dya, dqf, dkf, dvf, dqb, dkb, dvb, dgfb, dx1, cos, sin,
                                    mod6, norm1, win, sgw, sgbt, sggain, tm=tm_i)
    dwin_c, small_c, dlg = _ctx_bwd(ctx, hc, kvc, cmod6, norm1, win, rlf, rlb, dscf, dscb, dlg)
    pack = jnp.concatenate([small_f, small_i, small_c, dg1, dlg], axis=0)
    after = small_path(pack, dsgw) if small_path is not None else ()
    halves = []
    for j in range(2):
        halves.append(_tn_matmul(dz, hx, bm=INC // 2, bt=bt2, name="dw_in_%d" % j, init=dwin_c,
                                 init_rows=(KV_LO, KV_HI), after=after, cols=(j, D // 2)))
        if after_in_half is not None:
            after = after_in_half(j, halves[-1][1])
    grads = {"w_in": halves, "w_out": (d_wo, d_wo_b), "w_gate": (d_wg, d_wg_b), "w_up": (d_wu, d_wu_b),
             "w_down": (d_wd, d_wd_b)}
    return gx, grads, pack, dsgw


def kernel(x, c, ctx, c_ctx, w_mod, b_mod, norm1, w_in, sg_gain, sg_w, sg_b, ret_logit_f, ret_logit_b, w_out, norm2, w_gate, w_up, w_down, norm_f, loss_target, m_c_ctx, m_w_mod, m_b_mod, m_norm1, m_w_in, m_sg_gain, m_sg_w, m_sg_b, m_ret_logit_f, m_ret_logit_b, m_w_out, m_norm2, m_w_gate, m_w_up, m_w_down, m_norm_f, v_c_ctx, v_w_mod, v_b_mod, v_norm1, v_w_in, v_sg_gain, v_sg_w, v_sg_b, v_ret_logit_f, v_ret_logit_b, v_w_out, v_norm2, v_w_gate, v_w_up, v_w_down, v_norm_f):
    t_len = x.shape[1]
    tm = min(512, t_len)
    me = 4 * lax.axis_index("x") + 2 * lax.axis_index("y") + lax.axis_index("c")
    tr = lambda a: jnp.transpose(a[0])

    cctx_row = c_ctx.reshape(1, D)
    mod_all, cs_all, g_in = _entry_gather(c, cctx_row, w_mod[0], b_mod, [tr(w_in)])
    mod6 = lax.dynamic_slice(mod_all, (me, 0), (1, 6 * D)).reshape(6, D)
    cmod6 = mod_all[8].reshape(6, D)
    win_t = g_in.reshape(INC, D)

    (out_shard,) = _cast_bf16([w_out[0]], name="cast_out", after=[g_in])
    h_out, tok_out = _exchange_start([out_shard], scatter=False, name="wout_start")
    ffn_shards = _cast_bf16([tr(w_gate), tr(w_up), w_down[0]], name="cast_ffn", after=[h_out[2]])
    sems_ffn, arrs_ffn, tok = _gather2_start(ffn_shards, name="wffn_start")
    mod6 = mod6 + (tok_out[0, 0] + tok[0, 0])
    ffn = {}

    def place_own(own, lands, rows):
        return [lax.dynamic_update_slice(l, o[None], (me, 0, 0)).reshape(rows, D) for o, l in zip(own, lands)]

    def wout(after):
        own, lands = _exchange_wait(h_out, [after], scatter=False, name="wout_wait")
        arrs = _gather2_arrived(sems_ffn, arrs_ffn, [after], name="wffn_arrived")
        ffn["own"] = arrs[:3]
        ffn["sems"], ffn["lands"], tok_fwd = _gather2_forward(arrs[3:], name="wffn_forward")
        return place_own(own, lands, D)[0], tok_fwd[0, 0]

    def ffn_weights(after):
        lands = _gather2_wait(ffn["sems"], ffn["lands"], [after], name="wffn_wait")
        return place_own(ffn["own"], lands, DFF)

    rs, outs = {}, {}

    def after_first_grads(bf):
        rs["first"], tok_first = _exchange_start([b.reshape(NDEV, b.shape[0] // NDEV, D) for b in bf], scatter=True,
                                                 name="rs_first_start")
        return tok_first[0, 0]

    def small_path(pack, dsgw):
        rs["small"], _ = _exchange_start([pack, dsgw.reshape(G * C, C)], scatter=False, name="small_start")
        return [rs["small"][2], rs["small"][3]]

    def after_in_half(j, half_bf16):
        rs["in%d" % j], _ = _exchange_start([half_bf16.reshape(NDEV, INC // NDEV, D // 2)], scatter=True,
                                            name="rs_in%d_start" % j)
        return [rs["in%d" % j][2]]

    gx, grads, pack, dsgw = _local_step(
        x[0], loss_target[0], ctx[0], mod6, cmod6, norm1, norm2[0:1], norm_f.reshape(1, D), win_t, wout, ffn_weights,
        sg_w[0], sg_b[0], sg_gain, ret_logit_f, ret_logit_b,
        tm_a=tm, tm_b=tm, tm_f=min(256, t_len), tm_m=min(1024, t_len), tm_r=min(1024, t_len), tm_i=tm,
        bt_w=min(1024, t_len),
        after_first_grads=after_first_grads, small_path=small_path, after_in_half=after_in_half)

    me_arr = me.reshape(1).astype(jnp.int32)
    (pack_own, sgw_own), (pack_land, sgw_land) = _exchange_wait(rs["small"], [rs["in1"][2]], scatter=False,
                                                               name="small_wait")
    psum_rows, pall, sgw_sum, part = _small_sum(me_arr, pack_own, pack_land, sgw_own, sgw_land, w_mod[0])
    h_cc, _ = _exchange_start([part], scatter=False, name="cctx_start")

    dmod_rows = (ROW_I + 0, ROW_I + 1, ROW_G1, ROW_F + 2, ROW_F + 3, ROW_F + 1)
    dmod_all = jnp.concatenate([pall[:, r, :] for r in dmod_rows], axis=1)
    dcmod = jnp.concatenate([psum_rows[ROW_C + 0:ROW_C + 1], psum_rows[ROW_C + 1:ROW_C + 2],
                             jnp.zeros((1, 4 * D), F32)], axis=1)
    dmod_mat = jnp.concatenate([dmod_all, jnp.pad(dcmod, ((0, 7), (0, 0)))], axis=0)
    ncol = 6 * D // NDEV
    dmod_cols = lax.dynamic_slice(dmod_mat, (0, me * ncol), (16, ncol))
    g_wm, d_wm, m_wm, v_wm = _wmod_grad(cs_all, dmod_cols, w_mod[0], m_w_mod[0], v_w_mod[0])
    outs["w_mod"] = (g_wm[None], d_wm[None], m_wm[None], v_wm[None])
    small_params = {
        "norm1": (norm1, m_norm1, v_norm1), "norm2": (norm2, m_norm2, v_norm2),
        "norm_f": tuple(a.reshape(1, D) for a in (norm_f, m_norm_f, v_norm_f)),
        "sg_gain": (sg_gain, m_sg_gain, v_sg_gain), "sg_b": (sg_b[0], m_sg_b[0], v_sg_b[0]),
        "ret_logit_f": (ret_logit_f, m_ret_logit_f, v_ret_logit_f),
        "ret_logit_b": (ret_logit_b, m_ret_logit_b, v_ret_logit_b),
        "b_mod": (b_mod, m_b_mod, v_b_mod), "sg_w": (sg_w[0], m_sg_w[0], v_sg_w[0])}
    small, loss = _adam_small(psum_rows, sgw_sum.reshape(G, C, C), small_params)
    back = {"norm_f": lambda a: a.reshape(D), "sg_b": lambda a: a[None], "sg_w": lambda a: a[None]}
    for name, vals in small.items():
        outs[name] = tuple(back.get(name, lambda a: a)(a) for a in vals)

    _, lands_first = _exchange_wait(rs["first"], [g_wm], scatter=True, name="rs_first_wait")

    def finish(name, fulls, lands, w, m, v, transposed):
        take = tr if transposed else (lambda a: a[0])
        res = _rs_adam(me_arr, fulls, lands, take(w), take(m), take(v), name="adam_" + name)
        put = (lambda a: jnp.transpose(a)[None]) if transposed else (lambda a: a[None])
        outs[name] = tuple(put(a) for a in res)
        return res[0]

    g_down = finish("w_down", [grads["w_down"][0]], [lands_first[0]], w_down, m_w_down, v_w_down, False)
    g_gate = finish("w_gate", [grads["w_gate"][0]], [lands_first[1]], w_gate, m_w_gate, v_w_gate, True)
    g_up = finish("w_up", [grads["w_up"][0]], [lands_first[2]], w_up, m_w_up, v_w_up, True)
    g_out = finish("w_out", [grads["w_out"][0]], [lands_first[3]], w_out, m_w_out, v_w_out, False)
    done = [g_down, g_gate, g_up, g_out]
    (part_own,), (part_land,) = _exchange_wait(h_cc, done, scatter=False, name="cctx_wait")
    res_cc = _cctx_final(me_arr, part_own, part_land, cctx_row, m_c_ctx.reshape(1, D), v_c_ctx.reshape(1, D))
    outs["c_ctx"] = tuple(a.reshape(D) for a in res_cc)
    lands_in = [_exchange_wait(rs["in%d" % j], done, scatter=True, name="rs_in%d_wait" % j)[1][0] for j in range(2)]
    finish("w_in", [h[0] for h in grads["w_in"]], lands_in, w_in, m_w_in, v_w_in, True)

    order = ["c_ctx", "w_mod", "b_mod", "norm1", "w_in", "sg_gain", "sg_w", "sg_b", "ret_logit_f", "ret_logit_b",
             "w_out", "norm2", "w_gate", "w_up", "w_down", "norm_f"]
    res = [loss.reshape(()), gx[None]]
    for j in range(4):
        res += [outs[name][j] for name in order]
    return tuple(res)
```

```python
import functools
import math

import numpy as np
import jax
import jax.numpy as jnp
from jax import lax
from jax.experimental import pallas as pl
from jax.experimental.pallas import tpu as pltpu

F32 = jnp.float32
BF16 = jnp.bfloat16

D = 1024
AW = 512
RW = 512
H = 4
DH = 128
G = 4
C = 128
CR = 256
DFF = 2816
INC = 3584
Q_LO = 2 * AW
KV_LO, VR_LO, G_LO = Q_LO + RW, Q_LO + 2 * RW, Q_LO + 3 * RW
KV_HI = G_LO
NDEV = 8
EPS = 1e-6
KSCALE = DH ** -0.5
ROPE_BASE = 10000.0
GRID_W = 64

ADAM_LR = 0.001
ADAM_B1 = 0.9
ADAM_B2 = 0.999
ADAM_EPS = 1e-08
ADAM_WD = 0.01
ADAM_STEP = 10

VMEM_LIMIT = 56 * 1024 * 1024
MESH = pl.DeviceIdType.MESH


def _cparams(n_grid=0, **kw):
    sem = ("arbitrary",) * n_grid if n_grid else None
    return pltpu.CompilerParams(dimension_semantics=sem, vmem_limit_bytes=VMEM_LIMIT, **kw)


def _dot(a, b):
    return jnp.dot(a, b, preferred_element_type=F32)


def _dot_nt(a, b):
    return lax.dot_general(a, b, (((1,), (1,)), ((), ())), preferred_element_type=F32)


def _dot_tn(a, b):
    return lax.dot_general(a, b, (((0,), (0,)), ((), ())), preferred_element_type=F32)


def _whole(shape):
    nd = len(shape)
    return pl.BlockSpec(shape, lambda *_: (0,) * nd, pipeline_mode=pl.Buffered(1))


def _acc(shape):
    nd = len(shape)
    return pl.BlockSpec(shape, lambda *_: (0,) * nd)


def _rows(tm, n):
    return pl.BlockSpec((tm, n), lambda i: (i, 0))


def _rows_rev(tm, n, nt):
    return pl.BlockSpec((tm, n), lambda i: (nt - 1 - i, 0))


def _sigmoid(x):
    return 1.0 / (1.0 + jnp.exp(-x))


def _log_sigmoid(x):
    return jnp.minimum(x, 0.0) - jnp.log(1.0 + jnp.exp(-jnp.abs(x)))


_GK0 = math.sqrt(2.0 / math.pi)
_GK1 = 0.044715


def _gelu(x):
    x2 = x * x
    t = jnp.tanh(x * (_GK0 + (_GK0 * _GK1) * x2))
    h = 0.5 + 0.5 * t
    g = x * h
    dg = h + g * (1.0 - h) * ((2.0 * _GK0) + (6.0 * _GK0 * _GK1) * x2)
    return g, dg


def _silu_parts(a):
    s = _sigmoid(a)
    sa = a * s
    return sa, s + sa * (1.0 - s)


def _rope(t, cos, sins):
    n = t.shape[1]
    lane = lax.broadcasted_iota(jnp.int32, t.shape, 1)
    first = (lane & 63) < 32
    partner = jnp.where(first, pltpu.roll(t, n - 32, 1), pltpu.roll(t, 32, 1))
    return t * cos + partner * sins


def _headnorm(o):
    outs, rs = [], []
    for h in range(H):
        oh = o[:, h * DH:(h + 1) * DH]
        r = lax.rsqrt(jnp.mean(oh * oh, axis=-1, keepdims=True) + EPS)
        outs.append(oh * r)
        rs.append(r)
    return jnp.concatenate(outs, axis=1), rs


def _decay_consts(lg, forward):
    ii = lax.broadcasted_iota(jnp.int32, (CR, CR), 0).astype(F32)
    jj = lax.broadcasted_iota(jnp.int32, (CR, CR), 1).astype(F32)
    ic = lax.broadcasted_iota(jnp.int32, (CR, 1), 0).astype(F32)
    if forward:
        diff = ii - jj
        xi = jnp.exp(lg * (ic + 1.0))
        z = jnp.exp(lg * (CR - 1.0 - ic))
    else:
        diff = jj - ii
        xi = jnp.exp(lg * (CR - ic))
        z = jnp.exp(lg * ic)
    dm = jnp.where(diff >= 0.0, jnp.exp(lg * jnp.maximum(diff, 0.0)), 0.0)
    dec = jnp.exp(lg * float(CR))
    return dm, xi, z, dec, ic


def _ctx_fwd(ctx, cmod6, norm1, win, rlf, rlb):
    lc = ctx.shape[0]

    def body(ctx_ref, cmod_ref, n1_ref, win_ref, rlf_ref, rlb_ref, hc_ref, kvc_ref, scf_ref, scb_ref):
        x = ctx_ref[...]
        r = lax.rsqrt(jnp.mean(x * x, axis=-1, keepdims=True) + EPS)
        hc = (x * r) * (n1_ref[...] * (1.0 + cmod_ref[1:2, :])) + cmod_ref[0:1, :]
        hcb = hc.astype(BF16)
        hc_ref[...] = hcb
        kv = _dot_nt(hcb, win_ref[KV_LO:KV_HI, :])
        k = kv[:, :RW] * KSCALE
        v = kv[:, RW:]
        kvc_ref[:, :RW] = k
        kvc_ref[:, RW:] = v
        t = lax.broadcasted_iota(jnp.int32, (lc, 1), 0).astype(F32)
        lgf = _log_sigmoid(rlf_ref[...])
        lgb = _log_sigmoid(rlb_ref[...])
        for h in range(H):
            hs = slice(h * DH, (h + 1) * DH)
            wf = jnp.exp(lgf[h:h + 1, 0:1] * (lc - 1.0 - t))
            wb = jnp.exp(lgb[h:h + 1, 0:1] * t)
            vh = v[:, hs].astype(BF16)
            scf_ref[h] = _dot_tn((k[:, hs] * wf).astype(BF16), vh)
            scb_ref[h] = _dot_tn((k[:, hs] * wb).astype(BF16), vh)

    return pl.pallas_call(
        body, name="ctx_fwd",
        out_shape=(jax.ShapeDtypeStruct((lc, D), BF16), jax.ShapeDtypeStruct((lc, 2 * RW), F32),
                   jax.ShapeDtypeStruct((H, DH, DH), F32), jax.ShapeDtypeStruct((H, DH, DH), F32)),
        compiler_params=_cparams(),
    )(ctx, cmod6, norm1, win, rlf, rlb)


def _sg_forward(u, v, sgw_ref, sgbt_ref, sgg_ref, nc):
    ua, dgu = _gelu(u)
    va, dgv = _gelu(v)
    gain = sgg_ref[...]
    mixed, vn_b, vah_l, rv_l = [], [], [], []
    for g in range(G):
        gs = slice(g * DH, (g + 1) * DH)
        vag = va[:, gs]
        rv = lax.rsqrt(jnp.mean(vag * vag, axis=-1, keepdims=True) + EPS)
        vah = vag * rv
        vn = (vah * gain[:, gs]).astype(BF16)
        wg = sgw_ref[g].astype(BF16)
        bcol = sgbt_ref[g]
        rows = [_dot(wg, vn[c * C:(c + 1) * C, :]) + bcol for c in range(nc)]
        mixed.append(jnp.concatenate(rows, axis=0) if nc > 1 else rows[0])
        vn_b.append(vn)
        vah_l.append(vah)
        rv_l.append(rv)
    mixed = jnp.concatenate(mixed, axis=1)
    return ua * mixed, (ua, dgu, dgv, mixed, vn_b, vah_l, rv_l)


def _fwd_a(x, mod6, norm1, win, sgw, sgbt, sggain, cos, sin, rlf, scf, *, tm):
    t_len = x.shape[0]
    nt, nc, ncr = t_len // tm, tm // C, tm // CR

    def body(x_ref, mod_ref, n1_ref, win_ref, sgw_ref, sgbt_ref, sgg_ref, cos_ref, sin_ref, rlf_ref, scf_ref,
             hx_ref, zuv_ref, q_ref, k_ref, v_ref, gfb_ref, of_ref, ya_ref, sst_ref, s_scr):
        i = pl.program_id(0)

        @pl.when(i == 0)
        def _():
            s_scr[...] = scf_ref[...]

        x = x_ref[...]
        r = lax.rsqrt(jnp.mean(x * x, axis=-1, keepdims=True) + EPS)
        hx = (x * r) * (n1_ref[...] * (1.0 + mod_ref[1:2, :])) + mod_ref[0:1, :]
        hxb = hx.astype(BF16)
        hx_ref[...] = hxb
        z = _dot_nt(hxb, win_ref[...])
        zuv_ref[...] = z[:, :2 * AW].astype(BF16)
        gfb_ref[...] = z[:, G_LO:INC].astype(BF16)
        cos = jnp.tile(cos_ref[...], (1, H))
        sins = jnp.tile(sin_ref[...], (1, H))
        q_ref[...] = _rope(z[:, Q_LO:KV_LO], cos, sins).astype(BF16)
        k_ref[...] = (_rope(z[:, KV_LO:VR_LO], cos, sins) * KSCALE).astype(BF16)
        v_ref[...] = z[:, VR_LO:G_LO].astype(BF16)
        ya, _ = _sg_forward(z[:, :AW], z[:, AW:2 * AW], sgw_ref, sgbt_ref, sgg_ref, nc)
        ya_ref[...] = ya.astype(BF16)

        lg = _log_sigmoid(rlf_ref[...])
        for h in range(H):
            dm, xi, zc, dec, _ = _decay_consts(lg[h:h + 1, 0:1], True)
            hs = slice(h * DH, (h + 1) * DH)
            for c in range(ncr):
                rs = slice(c * CR, (c + 1) * CR)
                qc, kc, vc = q_ref[rs, hs], k_ref[rs, hs], v_ref[rs, hs]
                s = s_scr[h]
                sst_ref[c, h] = s
                a = (_dot_nt(qc, kc) * dm).astype(BF16)
                of_ref[rs, hs] = (_dot(a, vc) + xi * _dot(qc, s.astype(BF16))).astype(BF16)
                kz = (kc.astype(F32) * zc).astype(BF16)
                s_scr[h] = dec * s + _dot_tn(kz, vc)

    n_chunks = t_len // CR
    return pl.pallas_call(
        body, name="fwd_a", grid=(nt,),
        in_specs=[_rows(tm, D), _whole((6, D)), _whole((1, D)), _whole((INC, D)), _whole((G, C, C)),
                  _whole((G, C, 128)), _whole((1, AW)), _rows(tm, DH), _rows(tm, DH), _whole((H, 128)),
                  _whole((H, DH, DH))],
        out_specs=[_rows(tm, D), _rows(tm, 2 * AW), _rows(tm, RW), _rows(tm, RW), _rows(tm, RW),
                   _rows(tm, 2 * RW), _rows(tm, RW), _rows(tm, AW),
                   pl.BlockSpec((ncr, H, DH, DH), lambda i: (i, 0, 0, 0))],
        out_shape=(jax.ShapeDtypeStruct((t_len, D), BF16), jax.ShapeDtypeStruct((t_len, 2 * AW), BF16),
                   jax.ShapeDtypeStruct((t_len, RW), BF16), jax.ShapeDtypeStruct((t_len, RW), BF16),
                   jax.ShapeDtypeStruct((t_len, RW), BF16), jax.ShapeDtypeStruct((t_len, 2 * RW), BF16),
                   jax.ShapeDtypeStruct((t_len, RW), BF16), jax.ShapeDtypeStruct((t_len, AW), BF16),
                   jax.ShapeDtypeStruct((n_chunks, H, DH, DH), F32)),
        scratch_shapes=[pltpu.VMEM((H, DH, DH), F32)],
        compiler_params=_cparams(1),
    )(x, mod6, norm1, win, sgw, sgbt, sggain, cos, sin, rlf, scf)


def _fwd_b(q, k, v, of, gfb, ya, x, mod6, wout, rlb, scb, *, tm):
    t_len = x.shape[0]
    nt, nc = t_len // tm, tm // CR

    def body(q_ref, k_ref, v_ref, of_ref, gfb_ref, ya_ref, x_ref, mod_ref, wout_ref, rlb_ref, scb_ref,
             ob_ref, ycat_ref, x1_ref, rst_ref, r_scr):
        i = pl.program_id(0)

        @pl.when(i == 0)
        def _():
            r_scr[...] = scb_ref[...]

        lg = _log_sigmoid(rlb_ref[...])
        consts = [_decay_consts(lg[h:h + 1, 0:1], False) for h in range(H)]

        def first(c, h):
            dm, _, zc, dec, _ = consts[h]
            hs, rs = slice(h * DH, (h + 1) * DH), slice(c * CR, (c + 1) * CR)
            qc, kc, vc = q_ref[rs, hs], k_ref[rs, hs], v_ref[rs, hs]
            s = r_scr[h]
            rst_ref[c, h] = s
            scores = _dot_nt(qc, kc)
            cross = _dot(qc, s.astype(BF16))
            kz = (kc.astype(F32) * zc).astype(BF16)
            r_scr[h] = dec * s + _dot_tn(kz, vc)
            return (scores * dm).astype(BF16), cross

        def second(c, h, carried):
            a, cross = carried
            hs, rs = slice(h * DH, (h + 1) * DH), slice(c * CR, (c + 1) * CR)
            ob_ref[rs, hs] = (_dot(a, v_ref[rs, hs]) + consts[h][1] * cross).astype(BF16)

        def rows_out(c):
            rs = slice(c * CR, (c + 1) * CR)
            gfb = gfb_ref[rs, :].astype(F32)
            sf, _ = _silu_parts(gfb[:, :RW])
            sb, _ = _silu_parts(gfb[:, RW:])
            hnf, _ = _headnorm(of_ref[rs, :].astype(F32))
            hnb, _ = _headnorm(ob_ref[rs, :].astype(F32))
            yr = sf * hnf + sb * hnb
            ycat = jnp.concatenate([ya_ref[rs, :], yr.astype(BF16)], axis=1)
            ycat_ref[rs, :] = ycat
            x1_ref[rs, :] = x_ref[rs, :] + mod_ref[2:3, :] * _dot(ycat, wout_ref[...])

        pending, waiting_rows = None, None
        for c in reversed(range(nc)):
            for h in range(H):
                carried = first(c, h)
                if pending is not None:
                    second(*pending)
                pending = (c, h, carried)
            if waiting_rows is not None:
                rows_out(waiting_rows)
            waiting_rows = c
        second(*pending)
        rows_out(waiting_rows)

    n_chunks = t_len // CR
    rr = functools.partial(_rows_rev, nt=nt)
    return pl.pallas_call(
        body, name="fwd_b", grid=(nt,),
        in_specs=[rr(tm, RW), rr(tm, RW), rr(tm, RW), rr(tm, RW), rr(tm, 2 * RW), rr(tm, AW), rr(tm, D),
                  _whole((6, D)), _whole((D, D)), _whole((H, 128)), _whole((H, DH, DH))],
        out_specs=[rr(tm, RW), rr(tm, D), rr(tm, D),
                   pl.BlockSpec((nc, H, DH, DH), lambda i: (nt - 1 - i, 0, 0, 0))],
        out_shape=(jax.ShapeDtypeStruct((t_len, RW), BF16), jax.ShapeDtypeStruct((t_len, D), BF16),
                   jax.ShapeDtypeStruct((t_len, D), F32),
                   jax.ShapeDtypeStruct((n_chunks, H, DH, DH), F32)),
        scratch_shapes=[pltpu.VMEM((H, DH, DH), F32)],
        compiler_params=_cparams(1),
    )(q, k, v, of, gfb, ya, x, mod6, wout, rlb, scb)


def _ffn(x1, tgt, mod6, norm2, normf, wg, wu, wd, *, tm):
    t_len = x1.shape[0]
    nt = t_len // tm

    def body(x1_ref, tgt_ref, mod_ref, n2_ref, nf_ref, wg_ref, wu_ref, wd_ref,
             dx1_ref, h2_ref, da_ref, db_ref, hm_ref, df_ref, small_ref):
        i = pl.program_id(0)

        @pl.when(i == 0)
        def _():
            small_ref[...] = jnp.zeros_like(small_ref)

        sh2, sc2, g2 = mod_ref[3:4, :], mod_ref[4:5, :], mod_ref[5:6, :]
        n2, nf = n2_ref[...], nf_ref[...]
        x1 = x1_ref[...]
        r2 = lax.rsqrt(jnp.mean(x1 * x1, axis=-1, keepdims=True) + EPS)
        x1h = x1 * r2
        w2 = n2 * (1.0 + sc2)
        h2b = (x1h * w2 + sh2).astype(BF16)
        h2_ref[...] = h2b
        a = _dot_nt(h2b, wg_ref[...])
        b = _dot_nt(h2b, wu_ref[...])
        sa, dsa = _silu_parts(a)
        hmb = (sa * b).astype(BF16)
        hm_ref[...] = hmb
        f = _dot(hmb, wd_ref[...])
        x2 = x1 + g2 * f
        r3 = lax.rsqrt(jnp.mean(x2 * x2, axis=-1, keepdims=True) + EPS)
        x2h = x2 * r3
        diff = x2h * nf - tgt_ref[...]
        small_ref[5:6, :] += jnp.sum(diff * diff, axis=0, keepdims=True)
        dout = diff * (1.0 / D)
        small_ref[0:1, :] += jnp.sum(dout * x2h, axis=0, keepdims=True)
        dyg = dout * nf
        dx2 = r3 * (dyg - x2h * jnp.mean(dyg * x2h, axis=-1, keepdims=True))
        small_ref[1:2, :] += jnp.sum(dx2 * f, axis=0, keepdims=True)
        dfb = (dx2 * g2).astype(BF16)
        df_ref[...] = dfb
        dhm = _dot_nt(dfb, wd_ref[...])
        dbb = (dhm * sa).astype(BF16)
        dab = (dhm * b * dsa).astype(BF16)
        da_ref[...] = dab
        db_ref[...] = dbb
        dh2 = _dot(dab, wg_ref[...]) + _dot(dbb, wu_ref[...])
        small_ref[2:3, :] += jnp.sum(dh2, axis=0, keepdims=True)
        dhx = dh2 * x1h
        small_ref[3:4, :] += jnp.sum(dhx, axis=0, keepdims=True) * n2
        small_ref[4:5, :] += jnp.sum(dhx, axis=0, keepdims=True) * (1.0 + sc2)
        dyg2 = dh2 * w2
        dx1_ref[...] = dx2 + r2 * (dyg2 - x1h * jnp.mean(dyg2 * x1h, axis=-1, keepdims=True))

    return pl.pallas_call(
        body, name="ffn", grid=(nt,),
        in_specs=[_rows(tm, D), _rows(tm, D), _whole((6, D)), _whole((1, D)), _whole((1, D)),
                  _whole((DFF, D)), _whole((DFF, D)), _whole((DFF, D))],
        out_specs=[_rows(tm, D), _rows(tm, D), _rows(tm, DFF), _rows(tm, DFF), _rows(tm, DFF), _rows(tm, D),
                   _acc((8, D))],
        out_shape=(jax.ShapeDtypeStruct((t_len, D), F32), jax.ShapeDtypeStruct((t_len, D), BF16),
                   jax.ShapeDtypeStruct((t_len, DFF), BF16), jax.ShapeDtypeStruct((t_len, DFF), BF16),
                   jax.ShapeDtypeStruct((t_len, DFF), BF16), jax.ShapeDtypeStruct((t_len, D), BF16),
                   jax.ShapeDtypeStruct((8, D), F32)),
        compiler_params=_cparams(1),
    )(x1, tgt, mod6, norm2, normf, wg, wu, wd)


def _mid_bwd(dx1, of, ob, gfb, mod6, wout, *, tm):
    t_len = dx1.shape[0]
    nt = t_len // tm
    rc = min(256, tm)

    def body(dx1_ref, of_ref, ob_ref, gfb_ref, mod_ref, wout_ref,
             dxb_ref, dya_ref, dgfb_ref, dof_ref, dob_ref):
        for c in range(tm // rc):
            rows = slice(c * rc, (c + 1) * rc)
            dx1 = dx1_ref[rows, :]
            dxb_ref[rows, :] = dx1.astype(BF16)
            dyb = (dx1 * mod_ref[2:3, :]).astype(BF16)
            dycat = _dot_nt(dyb, wout_ref[...])
            dya_ref[rows, :] = dycat[:, :AW].astype(BF16)
            dyr = dycat[:, AW:]
            gfb = gfb_ref[rows, :].astype(F32)
            for o_ref, do_ref, lo in ((of_ref, dof_ref, 0), (ob_ref, dob_ref, RW)):
                sg, dsg = _silu_parts(gfb[:, lo:lo + RW])
                o = o_ref[rows, :].astype(F32)
                hn, rs = _headnorm(o)
                dgfb_ref[rows, lo:lo + RW] = (dyr * hn * dsg).astype(BF16)
                dhn = dyr * sg
                for h in range(H):
                    hs = slice(h * DH, (h + 1) * DH)
                    oh, dh = hn[:, hs], dhn[:, hs]
                    do_ref[rows, hs] = (rs[h] * (dh - oh * jnp.mean(dh * oh, axis=-1, keepdims=True))).astype(BF16)

    return pl.pallas_call(
        body, name="mid_bwd", grid=(nt,),
        in_specs=[_rows(tm, D), _rows(tm, RW), _rows(tm, RW), _rows(tm, 2 * RW), _whole((6, D)), _whole((D, D))],
        out_specs=[_rows(tm, D), _rows(tm, AW), _rows(tm, 2 * RW), _rows(tm, RW), _rows(tm, RW)],
        out_shape=(jax.ShapeDtypeStruct((t_len, D), BF16), jax.ShapeDtypeStruct((t_len, AW), BF16),
                   jax.ShapeDtypeStruct((t_len, 2 * RW), BF16), jax.ShapeDtypeStruct((t_len, RW), BF16),
                   jax.ShapeDtypeStruct((t_len, RW), BF16)),
        compiler_params=_cparams(1),
    )(dx1, of, ob, gfb, mod6, wout)


def _ret_bwd_items(q_ref, k_ref, v_ref, do_ref, st_ref, dq_ref, dk_ref, dv_ref, ds_scr, dlg_scr, lg, forward, nc, row0):
    order = list(reversed(range(nc))) if forward else list(range(nc))
    consts = [_decay_consts(lg[h:h + 1, 0:1], forward) for h in range(H)]
    accs = [jnp.zeros((1, DH), F32) for _ in range(H)]

    def first(h, c):
        dm, xi, zc, dec, _ = consts[h]
        hs, rs = slice(h * DH, (h + 1) * DH), slice(c * CR, (c + 1) * CR)
        qc, kc, vc, doc = q_ref[rs, hs], k_ref[rs, hs], v_ref[rs, hs], do_ref[rs, hs]
        s, dsn = st_ref[c, h], ds_scr[h]
        sb, dsnb = s.astype(BF16), dsn.astype(BF16)
        qf, kf = qc.astype(F32), kc.astype(F32)
        a_raw = _dot_nt(qc, kc)
        da_raw = _dot_nt(doc, vc)
        xdo = (xi * doc.astype(F32)).astype(BF16)
        kz = (kf * zc).astype(BF16)
        vz = (vc.astype(F32) * zc).astype(BF16)
        dq_c = _dot_nt(xdo, sb)
        dk_s = _dot_nt(vz, dsnb)
        dv_s = _dot(kz, dsnb)
        ds_scr[h] = _dot_tn((xi * qf).astype(BF16), doc) + dec * dsn
        accs[h] = accs[h] + (float(CR) * dec) * jnp.sum(s * dsn, axis=0, keepdims=True)
        a = (a_raw * dm).astype(BF16)
        da = (da_raw * dm).astype(BF16)
        return a, da, dq_c, dk_s, dv_s

    def second(h, c, carried):
        a, da, dq_c, dk_s, dv_s = carried
        ic = consts[h][4]
        if forward:
            w_qi, w_qc, w_ki, w_ks = ic, ic + 1.0, -ic, (CR - 1.0) - ic
        else:
            w_qi, w_qc, w_ki, w_ks = -ic, CR - ic, ic, ic
        hs, rs = slice(h * DH, (h + 1) * DH), slice(c * CR, (c + 1) * CR)
        qc, kc, doc = q_ref[rs, hs], k_ref[rs, hs], do_ref[rs, hs]
        dq_i = _dot(da, kc)
        dk_i = _dot_tn(da, qc)
        dv_i = _dot_tn(a, doc)
        dq_ref[rs, hs] = (dq_i + dq_c).astype(BF16)
        dk_ref[rs, hs] = (dk_i + dk_s).astype(BF16)
        dv_ref[rs, hs] = (dv_i + dv_s).astype(BF16)
        rowterm = qc.astype(F32) * (w_qi * dq_i + w_qc * dq_c) + kc.astype(F32) * (w_ki * dk_i + w_ks * dk_s)
        accs[h] = accs[h] + jnp.sum(rowterm, axis=0, keepdims=True)

    def finish():
        for h in range(H):
            dlg_scr[row0 + h:row0 + h + 1, :] += accs[h]

    items = [[(functools.partial(first, h, c), functools.partial(second, h, c)) for h in range(H)] for c in order]
    return items, finish


def _ret_bwd_run(dirs):
    nc = len(dirs[0][0])
    flat = [it for j in range(nc) for items, _ in dirs for it in items[j]]
    pending = None
    for first, second in flat:
        carried = first()
        if pending is not None:
            pending[0](pending[1])
        pending = (second, carried)
    pending[0](pending[1])
    for _, finish in dirs:
        finish()


def _ret_bwd(q, k, v, dof, dob, sst, rst, rlf, rlb, *, tm):
    t_len = q.shape[0]
    nt, nc = t_len // tm, tm // CR

    def body(qf_ref, kf_ref, vf_ref, dof_ref, sst_ref, qb_ref, kb_ref, vb_ref, dob_ref, rst_ref, rlf_ref, rlb_ref,
             dqf_ref, dkf_ref, dvf_ref, dqb_ref, dkb_ref, dvb_ref, dscf_ref, dscb_ref, dlg_ref,
             dsf_scr, dsb_scr, dlg_scr):
        i = pl.program_id(0)

        @pl.when(i == 0)
        def _():
            dsf_scr[...] = jnp.zeros_like(dsf_scr)
            dsb_scr[...] = jnp.zeros_like(dsb_scr)
            dlg_scr[...] = jnp.zeros_like(dlg_scr)

        lgf = _log_sigmoid(rlf_ref[...])
        lgb = _log_sigmoid(rlb_ref[...])
        _ret_bwd_run([
            _ret_bwd_items(qf_ref, kf_ref, vf_ref, dof_ref, sst_ref, dqf_ref, dkf_ref, dvf_ref, dsf_scr, dlg_scr,
                           lgf, True, nc, 0),
            _ret_bwd_items(qb_ref, kb_ref, vb_ref, dob_ref, rst_ref, dqb_ref, dkb_ref, dvb_ref, dsb_scr, dlg_scr,
                           lgb, False, nc, H)])

        @pl.when(i == nt - 1)
        def _():
            dscf_ref[...] = dsf_scr[...]
            dscb_ref[...] = dsb_scr[...]
            tot = jnp.broadcast_to(jnp.sum(dlg_scr[...], axis=1, keepdims=True), (8, 128))
            ri = lax.broadcasted_iota(jnp.int32, (8, 128), 0)
            li = lax.broadcasted_iota(jnp.int32, (8, 128), 1)
            dlg_ref[...] = jnp.zeros_like(dlg_ref)
            dlg_ref[0:1, 0:128] = jnp.sum(jnp.where(ri == li, tot, 0.0), axis=0, keepdims=True)
            dlg_ref[1:2, 0:128] = jnp.sum(jnp.where(ri - H == li, tot, 0.0), axis=0, keepdims=True)

    rr = functools.partial(_rows_rev, nt=nt)
    st_f = pl.BlockSpec((nc, H, DH, DH), lambda i: (nt - 1 - i, 0, 0, 0))
    st_b = pl.BlockSpec((nc, H, DH, DH), lambda i: (i, 0, 0, 0))
    tok = jax.ShapeDtypeStruct((t_len, RW), BF16)
    st = jax.ShapeDtypeStruct((H, DH, DH), F32)
    return pl.pallas_call(
        body, name="ret_bwd", grid=(nt,),
        in_specs=[rr(tm, RW), rr(tm, RW), rr(tm, RW), rr(tm, RW), st_f,
                  _rows(tm, RW), _rows(tm, RW), _rows(tm, RW), _rows(tm, RW), st_b,
                  _whole((H, 128)), _whole((H, 128))],
        out_specs=[rr(tm, RW), rr(tm, RW), rr(tm, RW), _rows(tm, RW), _rows(tm, RW), _rows(tm, RW),
                   _acc((H, DH, DH)), _acc((H, DH, DH)), _acc((8, D))],
        out_shape=(tok, tok, tok, tok, tok, tok, st, st, jax.ShapeDtypeStruct((8, D), F32)),
        scratch_shapes=[pltpu.VMEM((H, DH, DH), F32), pltpu.VMEM((H, DH, DH), F32), pltpu.VMEM((8, 128), F32)],
        compiler_params=_cparams(1),
    )(q, k, v, dof, sst, q, k, v, dob, rst, rlf, rlb)


def _in_bwd(x, zuv, dya, dqf, dkf, dvf, dqb, dkb, dvb, dgfb, dx1, cos, sin, mod6, norm1, win, sgw, sgbt, sggain, *, tm):
    t_len = x.shape[0]
    nt, nc = t_len // tm, tm // C

    def body(x_ref, zuv_ref, dya_ref, dqf_ref, dkf_ref, dvf_ref, dqb_ref, dkb_ref, dvb_ref, dgfb_ref, dx1_ref,
             cos_ref, sin_ref, mod_ref, n1_ref, win_ref, sgw_ref, sgbt_ref, sgg_ref,
             gx_ref, dz_ref, small_ref, dsgw_ref, dsgbt_ref):
        i = pl.program_id(0)

        @pl.when(i == 0)
        def _():
            small_ref[...] = jnp.zeros_like(small_ref)
            dsgw_ref[...] = jnp.zeros_like(dsgw_ref)
            dsgbt_ref[...] = jnp.zeros_like(dsgbt_ref)

        zuv = zuv_ref[...].astype(F32)
        _, (ua, dgu, dgv, mixed, vn_b, vah_l, rv_l) = _sg_forward(zuv[:, :AW], zuv[:, AW:], sgw_ref, sgbt_ref, sgg_ref, nc)
        dya = dya_ref[...].astype(F32)
        dz_ref[:, 0:AW] = (dya * mixed * dgu).astype(BF16)
        dmixed = dya * ua
        gain = sgg_ref[...]
        for g in range(G):
            gs = slice(g * DH, (g + 1) * DH)
            dmg = dmixed[:, gs]
            dmb = dmg.astype(BF16)
            wgt = sgw_ref[g].astype(BF16)
            dsw = jnp.zeros((C, C), F32)
            dsb = jnp.zeros((C, DH), F32)
            rows = []
            for c in range(nc):
                rs = slice(c * C, (c + 1) * C)
                dsw = dsw + _dot_nt(dmb[rs, :], vn_b[g][rs, :])
                dsb = dsb + dmg[rs, :]
                rows.append(_dot_tn(wgt, dmb[rs, :]))
            dsgw_ref[g] += dsw
            dsgbt_ref[g] += dsb
            dvn = jnp.concatenate(rows, axis=0) if nc > 1 else rows[0]
            vah, rv = vah_l[g], rv_l[g]
            small_ref[3:4, gs] += jnp.sum(dvn * vah, axis=0, keepdims=True)
            dyg = dvn * gain[:, gs]
            dva = rv * (dyg - vah * jnp.mean(dyg * vah, axis=-1, keepdims=True))
            dz_ref[:, AW + g * DH:AW + (g + 1) * DH] = (dva * dgv[:, gs]).astype(BF16)

        cos = jnp.tile(cos_ref[...], (1, H))
        nsins = -jnp.tile(sin_ref[...], (1, H))
        def both(f_ref, b_ref):
            return f_ref[...].astype(F32) + b_ref[...].astype(F32)

        dz_ref[:, Q_LO:KV_LO] = _rope(both(dqf_ref, dqb_ref), cos, nsins).astype(BF16)
        dz_ref[:, KV_LO:VR_LO] = (_rope(both(dkf_ref, dkb_ref), cos, nsins) * KSCALE).astype(BF16)
        dz_ref[:, VR_LO:G_LO] = both(dvf_ref, dvb_ref).astype(BF16)
        dz_ref[:, G_LO:INC] = dgfb_ref[...]

        dhx = _dot(dz_ref[...], win_ref[...])
        x = x_ref[...]
        r = lax.rsqrt(jnp.mean(x * x, axis=-1, keepdims=True) + EPS)
        xh = x * r
        n1, sc1 = n1_ref[...], mod_ref[1:2, :]
        small_ref[0:1, :] += jnp.sum(dhx, axis=0, keepdims=True)
        dhxx = jnp.sum(dhx * xh, axis=0, keepdims=True)
        small_ref[1:2, :] += dhxx * n1
        small_ref[2:3, :] += dhxx * (1.0 + sc1)
        dyg = dhx * (n1 * (1.0 + sc1))
        gx_ref[...] = dx1_ref[...] + r * (dyg - xh * jnp.mean(dyg * xh, axis=-1, keepdims=True))

        @pl.when(i == nt - 1)
        def _():
            ri = lax.broadcasted_iota(jnp.int32, (C, DH), 0)
            li = lax.broadcasted_iota(jnp.int32, (C, DH), 1)
            for g in range(G):
                tot = jnp.broadcast_to(jnp.sum(dsgbt_ref[g], axis=1, keepdims=True), (C, DH))
                small_ref[4 + g:5 + g, 0:DH] = jnp.sum(jnp.where(ri == li, tot, 0.0), axis=0, keepdims=True)

    tok = functools.partial(_rows, tm)
    return pl.pallas_call(
        body, name="in_bwd", grid=(nt,),
        in_specs=[tok(D), tok(2 * AW), tok(AW), tok(RW), tok(RW), tok(RW), tok(RW), tok(RW), tok(RW),
                  tok(2 * RW), tok(D), tok(DH), tok(DH), _whole((6, D)), _whole((1, D)), _whole((INC, D)),
                  _whole((G, C, C)), _whole((G, C, 128)), _whole((1, AW))],
        out_specs=[tok(D), tok(INC), _acc((8, D)), _acc((G, C, C))],
        out_shape=(jax.ShapeDtypeStruct((t_len, D), F32), jax.ShapeDtypeStruct((t_len, INC), BF16),
                   jax.ShapeDtypeStruct((8, D), F32), jax.ShapeDtypeStruct((G, C, C), F32)),
        scratch_shapes=[pltpu.VMEM((G, C, 128), F32)],
        compiler_params=_cparams(1),
    )(x, zuv, dya, dqf, dkf, dvf, dqb, dkb, dvb, dgfb, dx1, cos, sin, mod6, norm1, win, sgw, sgbt, sggain)


def _tn_matmul(a, b, *, bm, bt, name, init=None, init_rows=None, after=(), cols=None):
    t_len, m = a.shape
    cj, n = (0, b.shape[1]) if cols is None else cols
    ni, ntt = m // bm, t_len // bt
    has_init = init is not None

    def body(*refs):
        o_ref, ob_ref = refs[-2:]
        a_ref, b_ref = refs[:2]
        init_ref = refs[2] if has_init else None
        i, t = pl.program_id(0), pl.program_id(1)

        @pl.when(t == 0)
        def _():
            o_ref[...] = jnp.zeros_like(o_ref)

        if has_init:
            for ib in range(ni):
                lo, hi = max(init_rows[0], ib * bm), min(init_rows[1], (ib + 1) * bm)
                if lo < hi:
                    @pl.when(jnp.logical_and(t == 0, i == ib))
                    def _(lo=lo, hi=hi, ib=ib):
                        o_ref[lo - ib * bm:hi - ib * bm, :] = init_ref[lo - init_rows[0]:hi - init_rows[0], :]

        o_ref[...] += _dot_tn(a_ref[...], b_ref[...])

        @pl.when(t == ntt - 1)
        def _():
            ob_ref[...] = o_ref[...].astype(BF16)

    in_specs = [pl.BlockSpec((bt, bm), lambda i, t: (t, i)), pl.BlockSpec((bt, n), lambda i, t: (t, cj))]
    args = [a, b]
    if has_init:
        in_specs.append(pl.BlockSpec((init.shape[0], n), lambda i, t: (0, cj), pipeline_mode=pl.Buffered(1)))
        args.append(init)
    for arr in after:
        in_specs.append(pl.BlockSpec(memory_space=pl.ANY))
        args.append(arr)
    out_spec = pl.BlockSpec((bm, n), lambda i, t: (i, 0))
    return pl.pallas_call(
        body, name=name, grid=(ni, ntt),
        in_specs=in_specs,
        out_specs=[out_spec, out_spec],
        out_shape=(jax.ShapeDtypeStruct((m, n), F32), jax.ShapeDtypeStruct((m, n), BF16)),
        compiler_params=_cparams(2),
    )(*args)


def _dw_out(ycat, dxb, mod6, wout, *, bt):
    t_len = ycat.shape[0]
    ntt = t_len // bt

    def body(a_ref, b_ref, mod_ref, wout_ref, o_ref, ob_ref, dg1_ref):
        t = pl.program_id(0)

        @pl.when(t == 0)
        def _():
            o_ref[...] = jnp.zeros_like(o_ref)

        o_ref[...] += _dot_tn(a_ref[...], b_ref[...])

        @pl.when(t == ntt - 1)
        def _():
            m = o_ref[...]
            dg1_ref[...] = jnp.zeros_like(dg1_ref)
            dg1_ref[0:1, :] = jnp.sum(m * wout_ref[...].astype(F32), axis=0, keepdims=True)
            g = m * mod_ref[2:3, :]
            o_ref[...] = g
            ob_ref[...] = g.astype(BF16)

    return pl.pallas_call(
        body, name="dw_out", grid=(ntt,),
        in_specs=[pl.BlockSpec((bt, D), lambda t: (t, 0)), pl.BlockSpec((bt, D), lambda t: (t, 0)),
                  _whole((6, D)), _whole((D, D))],
        out_specs=[_acc((D, D)), _acc((D, D)), _acc((8, D))],
        out_shape=(jax.ShapeDtypeStruct((D, D), F32), jax.ShapeDtypeStruct((D, D), BF16),
                   jax.ShapeDtypeStruct((8, D), F32)),
        compiler_params=_cparams(1),
    )(ycat, dxb, mod6, wout)


def _ctx_bwd(ctx, hc, kvc, cmod6, norm1, win, rlf, rlb, dscf, dscb, dlg_in):
    lc = ctx.shape[0]

    def body(ctx_ref, hc_ref, kvc_ref, cmod_ref, n1_ref, win_ref, rlf_ref, rlb_ref, dscf_ref, dscb_ref, dlgin_ref,
             dwin_ref, small_ref, dlg_ref):
        k = kvc_ref[:, :RW]
        v = kvc_ref[:, RW:]
        t = lax.broadcasted_iota(jnp.int32, (lc, 1), 0).astype(F32)
        lgf = _log_sigmoid(rlf_ref[...])
        lgb = _log_sigmoid(rlb_ref[...])
        dks, dvs = [], []
        lane = lax.broadcasted_iota(jnp.int32, (1, 128), 1)
        row_f = jnp.zeros((1, 128), F32)
        row_b = jnp.zeros((1, 128), F32)
        for h in range(H):
            hs = slice(h * DH, (h + 1) * DH)
            wf = jnp.exp(lgf[h:h + 1, 0:1] * (lc - 1.0 - t))
            wb = jnp.exp(lgb[h:h + 1, 0:1] * t)
            kh, vhb = k[:, hs], v[:, hs].astype(BF16)
            dsf, dsb = dscf_ref[h].astype(BF16), dscb_ref[h].astype(BF16)
            dkf = wf * _dot_nt(vhb, dsf)
            dkb = wb * _dot_nt(vhb, dsb)
            dvs.append(_dot((kh * wf).astype(BF16), dsf) + _dot((kh * wb).astype(BF16), dsb))
            dks.append((dkf + dkb) * KSCALE)
            lf = jnp.sum((lc - 1.0 - t) * kh * dkf, axis=0, keepdims=True)
            lb = jnp.sum(t * kh * dkb, axis=0, keepdims=True)
            row_f = row_f + jnp.where(lane == h, jnp.sum(lf, axis=1, keepdims=True), 0.0)
            row_b = row_b + jnp.where(lane == h, jnp.sum(lb, axis=1, keepdims=True), 0.0)
        dlg_ref[...] = dlgin_ref[...]
        dlg_ref[0:1, 0:128] += row_f
        dlg_ref[1:2, 0:128] += row_b
        dkv = jnp.concatenate(dks + dvs, axis=1).astype(BF16)
        dhc = _dot(dkv, win_ref[KV_LO:KV_HI, :])
        dwin_ref[...] = _dot_tn(dkv, hc_ref[...])
        x = ctx_ref[...]
        r = lax.rsqrt(jnp.mean(x * x, axis=-1, keepdims=True) + EPS)
        xh = x * r
        small_ref[...] = jnp.zeros_like(small_ref)
        small_ref[0:1, :] = jnp.sum(dhc, axis=0, keepdims=True)
        dhxx = jnp.sum(dhc * xh, axis=0, keepdims=True)
        small_ref[1:2, :] = dhxx * n1_ref[...]
        small_ref[2:3, :] = dhxx * (1.0 + cmod_ref[1:2, :])

    return pl.pallas_call(
        body, name="ctx_bwd",
        out_shape=(jax.ShapeDtypeStruct((2 * RW, D), F32), jax.ShapeDtypeStruct((8, D), F32),
                   jax.ShapeDtypeStruct((8, D), F32)),
        compiler_params=_cparams(),
    )(ctx, hc, kvc, cmod6, norm1, win, rlf, rlb, dscf, dscb, dlg_in)


def _adam_math(w, g, m, v):
    m = ADAM_B1 * m + (1.0 - ADAM_B1) * g
    v = ADAM_B2 * v + (1.0 - ADAM_B2) * (g * g)
    m_hat = m / (1.0 - ADAM_B1 ** ADAM_STEP)
    v_hat = v / (1.0 - ADAM_B2 ** ADAM_STEP)
    delta = -ADAM_LR * (m_hat / (jnp.sqrt(v_hat) + ADAM_EPS) + ADAM_WD * w)
    return delta, m, v


def _place():
    x, y, c = lax.axis_index("x"), lax.axis_index("y"), lax.axis_index("c")
    return x, y, c, 4 * x + 2 * y + c


def _flip(x, y, c, k):
    px = 1 - x if (k >> 2) & 1 else x
    py = 1 - y if (k >> 1) & 1 else y
    pc = 1 - c if k & 1 else c
    return (px, py, pc), 4 * px + 2 * py + pc


def _gather_copy(buf_ref, send_sems, recv_sems, sem0, k, mine):
    x, y, c, me = _place()
    dev, idx = _flip(x, y, c, k)
    blk = me if mine else idx
    return pltpu.make_async_remote_copy(
        src_ref=buf_ref.at[blk], dst_ref=buf_ref.at[blk],
        send_sem=send_sems.at[sem0 + k - 1], recv_sem=recv_sems.at[sem0 + k - 1],
        device_id=dev, device_id_type=MESH)


def _entry_gather(c_row, cctx_row, wmod, bmod, shards):
    n = len(shards)
    ncol = wmod.shape[1]

    def body(*refs):
        c_ref, cctx_ref, wmod_ref, bmod_ref = refs[:4]
        in_refs = refs[4:4 + n]
        mod_ref, cs_ref = refs[4 + n:6 + n]
        out_refs = refs[6 + n:6 + 2 * n]
        cbuf, pbuf = refs[6 + 2 * n:8 + 2 * n]
        cast_refs = refs[8 + 2 * n:8 + 3 * n]
        small_send, small_recv, send_sems, recv_sems, local_sems = refs[8 + 3 * n:]
        x, y, c, me = _place()
        sib = (x, y, 1 - c)
        chips = [(1 - x, y), (x, 1 - y), (1 - x, 1 - y)]

        cbuf[me] = jnp.broadcast_to(c_ref[...], (8, D))
        small = [_gather_copy(cbuf, small_send, small_recv, 0, k, True) for k in range(1, NDEV)]
        for cp in small:
            cp.start()

        def blk(px, py, pc):
            return 4 * px + 2 * py + pc

        def copy(w, k, block, to, src=None):
            dst = out_refs[w].at[block]
            return pltpu.make_async_remote_copy(
                src_ref=dst if src is None else src, dst_ref=dst,
                send_sem=send_sems.at[w, k], recv_sem=recv_sems.at[w, k], device_id=to, device_id_type=MESH)

        first, passed, mine = [], [], []
        for w in range(n):
            cast_refs[w][...] = in_refs[w][...].astype(BF16)
            m = pltpu.make_async_copy(cast_refs[w], out_refs[w].at[me], local_sems.at[w])
            m.start()
            mine.append(m)
            cps = [copy(w, 0, me, sib, src=cast_refs[w])]
            cps += [copy(w, 1 + j, me, (*chip, c), src=cast_refs[w]) for j, chip in enumerate(chips)]
            for cp in cps:
                cp.start()
            first += cps

        for k in range(1, NDEV):
            _gather_copy(cbuf, small_send, small_recv, 0, k, False).wait_recv()
        row = lax.broadcasted_iota(jnp.int32, (16, D), 0)
        call = jnp.where(row == 8, jnp.broadcast_to(cctx_ref[...], (16, D)), 0.0)
        for d in range(NDEV):
            call = jnp.where(row == d, jnp.concatenate([cbuf[d], cbuf[d]], axis=0), call)
        cs = call * _sigmoid(call)
        cs_ref[...] = cs
        pbuf[me] = _dot(cs.astype(BF16), wmod_ref[...].astype(BF16))
        small2 = [_gather_copy(pbuf, small_send, small_recv, NDEV - 1, k, True) for k in range(1, NDEV)]
        for cp in small2:
            cp.start()

        for w in range(n):
            for j, chip in enumerate(chips):
                b = blk(*chip, c)
                copy(w, 1 + j, b, (x, y, c)).wait_recv()
                fw = copy(w, 4 + j, b, sib)
                fw.start()
                passed.append(fw)
        for w in range(n):
            copy(w, 0, blk(x, y, 1 - c), (x, y, c)).wait_recv()
            for j, chip in enumerate(chips):
                copy(w, 4 + j, blk(*chip, 1 - c), (x, y, c)).wait_recv()

        for k in range(1, NDEV):
            _gather_copy(pbuf, small_send, small_recv, NDEV - 1, k, False).wait_recv()
        for d in range(NDEV):
            mod_ref[:, d * ncol:(d + 1) * ncol] = pbuf[d] + bmod_ref[:, d * ncol:(d + 1) * ncol]
        for cp in small + small2 + first + passed:
            cp.wait_send()
        for m in mine:
            m.wait()

    vm = pl.BlockSpec(memory_space=pltpu.VMEM)
    hbm = pl.BlockSpec(memory_space=pltpu.HBM)
    return pl.pallas_call(
        body, name="entry_gather",
        in_specs=[vm] * (4 + n), out_specs=[vm, vm] + [hbm] * n,
        out_shape=(jax.ShapeDtypeStruct((16, 6 * D), F32), jax.ShapeDtypeStruct((16, D), F32))
        + tuple(jax.ShapeDtypeStruct((NDEV,) + s.shape, BF16) for s in shards),
        scratch_shapes=[pltpu.VMEM((NDEV, 8, D), F32), pltpu.VMEM((NDEV, 16, ncol), F32)]
        + [pltpu.VMEM(s.shape, BF16) for s in shards]
        + [pltpu.SemaphoreType.DMA((2 * (NDEV - 1),)), pltpu.SemaphoreType.DMA((2 * (NDEV - 1),)),
           pltpu.SemaphoreType.DMA((n, 7)), pltpu.SemaphoreType.DMA((n, 7)), pltpu.SemaphoreType.DMA((n,))],
        compiler_params=_cparams(),
    )(c_row, cctx_row, wmod, bmod, *shards)


_HBM = pl.BlockSpec(memory_space=pltpu.HBM)
_SEMS = pl.BlockSpec(memory_space=pltpu.SEMAPHORE)
_EFFECT = pltpu.SideEffectType.DATAFLOW_SIDE_EFFECTING


def _exchange_copy(src_refs, land_refs, send_sems, recv_sems, w, k, scatter, landing_slot_is_mine):
    x, y, c, me = _place()
    dev, pidx = _flip(x, y, c, k)
    src = src_refs[w].at[pidx] if scatter else src_refs[w]
    slot = me if landing_slot_is_mine else pidx
    return pltpu.make_async_remote_copy(
        src_ref=src, dst_ref=land_refs[w].at[slot],
        send_sem=send_sems.at[w * (NDEV - 1) + k - 1], recv_sem=recv_sems.at[w * (NDEV - 1) + k - 1],
        device_id=dev, device_id_type=MESH)


def _exchange_start(srcs, *, scatter, name):
    n = len(srcs)
    lands = [lax.empty((NDEV,) + tuple(s.shape[-2:]), s.dtype) for s in srcs]

    def body(*refs):
        src_refs, land_refs = refs[:n], refs[n:2 * n]
        send_sems, recv_sems = refs[2 * n], refs[2 * n + 1]
        token = refs[-1]
        for w in range(n):
            for k in range(1, NDEV):
                _exchange_copy(src_refs, land_refs, send_sems, recv_sems, w, k, scatter, True).start()
        token[...] = jnp.zeros_like(token)

    arrs = list(srcs) + lands
    out_shape = ([pltpu.SemaphoreType.DMA((n * (NDEV - 1),)), pltpu.SemaphoreType.DMA((n * (NDEV - 1),))]
                 + [pltpu.HBM(a.shape, a.dtype) for a in arrs] + [jax.ShapeDtypeStruct((8, 128), F32)])
    res = pl.pallas_call(
        body, name=name, out_shape=tuple(out_shape),
        in_specs=[_HBM] * (2 * n),
        out_specs=tuple([_SEMS, _SEMS] + [_HBM] * (2 * n) + [pl.BlockSpec(memory_space=pltpu.VMEM)]),
        input_output_aliases={i: 2 + i for i in range(2 * n)},
        compiler_params=pltpu.CompilerParams(has_side_effects=_EFFECT),
    )(*[pltpu.with_memory_space_constraint(a, pltpu.HBM) for a in arrs])
    return res[:-1], res[-1]


def _exchange_wait(handles, after, *, scatter, name):
    n = (len(handles) - 2) // 2
    na = len(after)

    def body(*refs):
        src_refs, land_refs = refs[:n], refs[n:2 * n]
        send_sems, recv_sems = refs[2 * n], refs[2 * n + 1]
        for w in range(n):
            for k in range(1, NDEV):
                cp = _exchange_copy(src_refs, land_refs, send_sems, recv_sems, w, k, scatter, False)
                cp.wait_send()
                cp.wait_recv()

    arrs = handles[2:]
    res = pl.pallas_call(
        body, name=name, out_shape=tuple(pltpu.HBM(a.shape, a.dtype) for a in arrs),
        in_specs=[_HBM] * (2 * n) + [_SEMS, _SEMS] + [_HBM] * na,
        out_specs=tuple([_HBM] * (2 * n)),
        input_output_aliases={i: i for i in range(2 * n)},
        compiler_params=pltpu.CompilerParams(has_side_effects=_EFFECT),
    )(*arrs, handles[0], handles[1], *[pltpu.with_memory_space_constraint(a, pltpu.HBM) for a in after])
    return res[:n], res[n:]


_SAME_CORE = (2, 4, 6)


def _gather2_copy(src_ref, land_ref, send_sems, recv_sems, sem, k, block):
    x, y, c, _ = _place()
    dev, _ = _flip(x, y, c, k)
    dst = land_ref.at[block]
    return pltpu.make_async_remote_copy(
        src_ref=dst if src_ref is None else src_ref, dst_ref=dst,
        send_sem=send_sems.at[sem], recv_sem=recv_sems.at[sem], device_id=dev, device_id_type=MESH)


def _split_call(body, name, arrs, n_sems, sems=None, after=(), token=False):
    na = len(arrs)
    out_shape, out_specs = [], []
    if n_sems is not None:
        out_shape += [pltpu.SemaphoreType.DMA((n_sems,)), pltpu.SemaphoreType.DMA((n_sems,))]
        out_specs += [_SEMS, _SEMS]
    first = len(out_shape)
    out_shape += [pltpu.HBM(a.shape, a.dtype) for a in arrs]
    out_specs += [_HBM] * na
    if token:
        out_shape.append(jax.ShapeDtypeStruct((8, 128), F32))
        out_specs.append(pl.BlockSpec(memory_space=pltpu.VMEM))
    in_specs = [_HBM] * na + ([_SEMS, _SEMS] if sems is not None else []) + [_HBM] * len(after)
    args = [pltpu.with_memory_space_constraint(a, pltpu.HBM) for a in arrs] + list(sems or ()) \
        + [pltpu.with_memory_space_constraint(a, pltpu.HBM) for a in after]
    return pl.pallas_call(
        body, name=name, out_shape=tuple(out_shape), in_specs=in_specs, out_specs=tuple(out_specs),
        input_output_aliases={i: first + i for i in range(na)},
        compiler_params=pltpu.CompilerParams(has_side_effects=_EFFECT))(*args)


def _gather2_start(srcs, *, name):
    n = len(srcs)
    lands = [lax.empty((NDEV,) + tuple(s.shape), s.dtype) for s in srcs]

    def body(*refs):
        src_refs, land_refs = refs[:n], refs[n:2 * n]
        send_sems, recv_sems = refs[2 * n], refs[2 * n + 1]
        _, _, _, me = _place()
        for w in range(n):
            for slot, k in enumerate((1,) + _SAME_CORE):
                _gather2_copy(src_refs[w], land_refs[w], send_sems, recv_sems, 4 * w + slot, k, me).start()
        refs[-1][...] = jnp.zeros_like(refs[-1])

    res = _split_call(body, name, list(srcs) + lands, 4 * n, token=True)
    return res[:2], res[2:-1], res[-1]


def _gather2_arrived(sems, arrs, after, *, name):
    n = len(arrs) // 2

    def body(*refs):
        src_refs, land_refs = refs[:n], refs[n:2 * n]
        send_sems, recv_sems = refs[2 * n], refs[2 * n + 1]
        x, y, c, me = _place()
        for w in range(n):
            for slot, k in enumerate((1,) + _SAME_CORE):
                _, pidx = _flip(x, y, c, k)
                _gather2_copy(src_refs[w], land_refs[w], send_sems, recv_sems, 4 * w + slot, k, me).wait_send()
                _gather2_copy(None, land_refs[w], send_sems, recv_sems, 4 * w + slot, k, pidx).wait_recv()

    return _split_call(body, name, arrs, None, sems=sems, after=after)


def _gather2_forward(lands, *, name):
    n = len(lands)

    def body(*refs):
        land_refs = refs[:n]
        send_sems, recv_sems = refs[n], refs[n + 1]
        x, y, c, _ = _place()
        for w in range(n):
            for j, k in enumerate(_SAME_CORE):
                _, pidx = _flip(x, y, c, k)
                _gather2_copy(None, land_refs[w], send_sems, recv_sems, 3 * w + j, 1, pidx).start()
        refs[-1][...] = jnp.zeros_like(refs[-1])

    res = _split_call(body, name, list(lands), 3 * n, token=True)
    return res[:2], res[2:-1], res[-1]


def _gather2_wait(sems, lands, after, *, name):
    n = len(lands)

    def body(*refs):
        land_refs = refs[:n]
        send_sems, recv_sems = refs[n], refs[n + 1]
        x, y, c, _ = _place()
        for w in range(n):
            for j, k in enumerate(_SAME_CORE):
                _, mine = _flip(x, y, c, k)
                _, theirs = _flip(x, y, c, k ^ 1)
                _gather2_copy(None, land_refs[w], send_sems, recv_sems, 3 * w + j, 1, mine).wait_send()
                _gather2_copy(None, land_refs[w], send_sems, recv_sems, 3 * w + j, 1, theirs).wait_recv()

    return _split_call(body, name, list(lands), None, sems=sems, after=after)


def _cast_bf16(arrs, *, name, after=()):
    n = len(arrs)

    def body(*refs):
        for i_ref, o_ref in zip(refs[:n], refs[n + len(after):]):
            o_ref[...] = i_ref[...].astype(BF16)

    return pl.pallas_call(
        body, name=name, out_shape=tuple(jax.ShapeDtypeStruct(a.shape, BF16) for a in arrs),
        in_specs=[pl.BlockSpec(memory_space=pltpu.VMEM)] * n + [pl.BlockSpec(memory_space=pl.ANY)] * len(after),
        compiler_params=_cparams())(*arrs, *after)


def _rs_adam(me_arr, fulls, lands, w, m, v, *, name):
    rows = lands[0].shape[1]
    k = len(fulls)
    nb = next(n for n in (4, 2, 1) if rows % (16 * n) == 0)
    br = rows // nb

    def body(me_ref, *refs):
        own_refs, land_refs = refs[:k], refs[k:2 * k]
        w_ref, m_ref, v_ref, g_ref, d_ref, mo_ref, vo_ref = refs[2 * k:]
        me = me_ref[0]
        parts = []
        for own_ref, land_ref in zip(own_refs, land_refs):
            acc = jnp.zeros(own_ref.shape, F32)
            for p in range(NDEV):
                acc = acc + jnp.where(p == me, own_ref[...], land_ref[p].astype(F32))
            parts.append(acc)
        acc = parts[0] if k == 1 else jnp.concatenate(parts, axis=1)
        g_ref[...] = acc
        d_ref[...], mo_ref[...], vo_ref[...] = _adam_math(w_ref[...], acc, m_ref[...], v_ref[...])

    sh = jax.ShapeDtypeStruct((rows, D), F32)
    blk = pl.BlockSpec((br, D), lambda i, me: (i, 0))
    grid_spec = pltpu.PrefetchScalarGridSpec(
        num_scalar_prefetch=1, grid=(nb,),
        in_specs=[pl.BlockSpec((br, f.shape[1]), lambda i, me: (me[0] * nb + i, 0)) for f in fulls]
        + [pl.BlockSpec((NDEV, br, l.shape[2]), lambda i, me: (0, i, 0)) for l in lands]
        + [blk, blk, blk],
        out_specs=[blk] * 4)
    return pl.pallas_call(
        body, name=name, out_shape=(sh, sh, sh, sh), grid_spec=grid_spec, compiler_params=_cparams(1),
    )(me_arr, *fulls, *lands, w, m, v)


ROW_F, ROW_I, ROW_C, ROW_G1, ROW_LG = 0, 8, 16, 24, 32
PACK_ROWS = 40


def _small_sum(me_arr, pack, pack_land, sgw, sgw_land, wmod):
    ncol = wmod.shape[1]

    def body(me_ref, pack_ref, pland_ref, sgw_ref, sland_ref, wmod_ref, sum_ref, all_ref, sgw_sum_ref, part_ref):
        me = me_ref[0]
        acc = jnp.zeros((PACK_ROWS, D), F32)
        acc_w = jnp.zeros(sgw_ref.shape, F32)
        for p in range(NDEV):
            rows = jnp.where(p == me, pack_ref[...], pland_ref[p])
            all_ref[p] = rows
            acc = acc + rows
            acc_w = acc_w + jnp.where(p == me, sgw_ref[...], sland_ref[p])
        sum_ref[...] = acc
        sgw_sum_ref[...] = acc_w
        zero = jnp.zeros((1, D), F32)
        dcmod = jnp.concatenate([acc[ROW_C:ROW_C + 1], acc[ROW_C + 1:ROW_C + 2], zero, zero, zero, zero], axis=1)
        mine = jnp.zeros((1, ncol), F32)
        for d in range(NDEV):
            mine = mine + jnp.where(d == me, dcmod[:, d * ncol:(d + 1) * ncol], 0.0)
        part_ref[...] = _dot_nt(jnp.broadcast_to(mine, (8, ncol)).astype(BF16), wmod_ref[...].astype(BF16))

    vm = pl.BlockSpec(memory_space=pltpu.VMEM)
    return pl.pallas_call(
        body, name="small_sum",
        out_shape=(jax.ShapeDtypeStruct((PACK_ROWS, D), F32), jax.ShapeDtypeStruct((NDEV, PACK_ROWS, D), F32),
                   jax.ShapeDtypeStruct(sgw.shape, F32), jax.ShapeDtypeStruct((8, D), F32)),
        in_specs=[pl.BlockSpec(memory_space=pltpu.SMEM)] + [vm] * 5,
        compiler_params=_cparams(),
    )(me_arr, pack, pack_land, sgw, sgw_land, wmod)


def _cctx_final(me_arr, part, part_land, cctx_row, m_row, v_row):
    def body(me_ref, part_ref, land_ref, c_ref, m_ref, v_ref, g_ref, d_ref, mo_ref, vo_ref):
        me = me_ref[0]
        tot = jnp.zeros((8, D), F32)
        for p in range(NDEV):
            tot = tot + jnp.where(p == me, part_ref[...], land_ref[p])
        cc = c_ref[...]
        _, dsilu = _silu_parts(cc)
        g = tot[0:1, :] * dsilu
        g_ref[...] = g
        d_ref[...], mo_ref[...], vo_ref[...] = _adam_math(cc, g, m_ref[...], v_ref[...])

    row = jax.ShapeDtypeStruct((1, D), F32)
    vm = pl.BlockSpec(memory_space=pltpu.VMEM)
    return pl.pallas_call(
        body, name="cctx_final", out_shape=(row, row, row, row),
        in_specs=[pl.BlockSpec(memory_space=pltpu.SMEM)] + [vm] * 5,
        compiler_params=_cparams(),
    )(me_arr, part, part_land, cctx_row, m_row, v_row)


def _adam_small(s, sgw_g, params):
    names = ["norm1", "norm2", "norm_f", "sg_gain", "sg_b", "ret_logit_f", "ret_logit_b", "b_mod", "sg_w"]
    flat = [a for n in names for a in params[n]]

    def body(*refs):
        s_ref, sgw_ref = refs[0], refs[1]
        p_refs = refs[2:2 + 3 * len(names)]
        o_refs = refs[2 + 3 * len(names):]
        loss_ref = o_refs[-1]

        def row(r):
            return s_ref[r:r + 1, :]

        grads = {
            "norm1": row(ROW_I + 2) + row(ROW_C + 2),
            "norm2": row(ROW_F + 4),
            "norm_f": row(ROW_F + 0),
            "sg_gain": s_ref[ROW_I + 3:ROW_I + 4, 0:AW],
            "sg_b": s_ref[ROW_I + 4:ROW_I + 8, 0:DH],
            "sg_w": sgw_ref[...],
        }
        for j, n in enumerate(names):
            w_ref, m_ref, v_ref = p_refs[3 * j:3 * j + 3]
            g_ref, d_ref, mo_ref, vo_ref = o_refs[4 * j:4 * j + 4]
            w = w_ref[...]
            if n in ("ret_logit_f", "ret_logit_b"):
                r = ROW_LG + (0 if n == "ret_logit_f" else 1)
                g = s_ref[r:r + 1, 0:H] * _sigmoid(-w)
            elif n == "b_mod":
                rows = [row(ROW_I + 0) + row(ROW_C + 0), row(ROW_I + 1) + row(ROW_C + 1), row(ROW_G1),
                        row(ROW_F + 2), row(ROW_F + 3), row(ROW_F + 1)]
                g = jnp.concatenate(rows, axis=1)
            else:
                g = grads[n]
            g_ref[...] = g
            d_ref[...], mo_ref[...], vo_ref[...] = _adam_math(w, g, m_ref[...], v_ref[...])
        loss_ref[...] = jnp.full((1, 1), 0.5 / D, F32) * jnp.sum(row(ROW_F + 5), axis=1, keepdims=True)

    out_shape = []
    for n in names:
        w = params[n][0]
        out_shape += [jax.ShapeDtypeStruct(w.shape, F32)] * 4
    out_shape.append(jax.ShapeDtypeStruct((1, 1), F32))
    outs = pl.pallas_call(body, name="adam_small", out_shape=tuple(out_shape), compiler_params=_cparams())(
        s, sgw_g, *flat)
    return {n: tuple(outs[4 * j:4 * j + 4]) for j, n in enumerate(names)}, outs[-1]


def _wmod_grad(cs_all, dmod_cols, wmod, m, v):
    ncol = wmod.shape[1]

    def body(cs_ref, dm_ref, w_ref, m_ref, v_ref, g_ref, d_ref, mo_ref, vo_ref):
        g = _dot_tn(cs_ref[...].astype(BF16), dm_ref[...].astype(BF16))
        g_ref[...] = g
        d_ref[...], mo_ref[...], vo_ref[...] = _adam_math(w_ref[...], g, m_ref[...], v_ref[...])

    sh = jax.ShapeDtypeStruct((D, ncol), F32)
    br = D // 4
    blk = pl.BlockSpec((br, ncol), lambda i: (i, 0))
    return pl.pallas_call(
        body, name="wmod_grad", out_shape=(sh, sh, sh, sh), grid=(D // br,),
        in_specs=[pl.BlockSpec((cs_all.shape[0], br), lambda i: (0, i)), _whole(dmod_cols.shape), blk, blk, blk],
        out_specs=[blk] * 4,
        compiler_params=_cparams(1),
    )(cs_all, dmod_cols, wmod, m, v)


def _rope_tables(t_len):
    n_freq = DH // 4
    inv = (ROPE_BASE ** (-np.arange(n_freq, dtype=np.float32) / n_freq)).astype(np.float32)
    tok = np.arange(t_len)
    rows = (tok // GRID_W).astype(np.float32)
    cols = (tok % GRID_W).astype(np.float32)
    ang_r = rows[:, None] * inv[None, :]
    ang_c = cols[:, None] * inv[None, :]
    cos = np.concatenate([np.cos(ang_r)] * 2 + [np.cos(ang_c)] * 2, axis=1).astype(np.float32)
    sin = np.concatenate([-np.sin(ang_r), np.sin(ang_r), -np.sin(ang_c), np.sin(ang_c)], axis=1).astype(np.float32)
    return jnp.asarray(cos), jnp.asarray(sin)


def _local_step(x, tgt, ctx, mod6, cmod6, norm1, norm2, normf, win, wout, ffn_weights, sgw, sgb, sggain, rlf, rlb,
                *, tm_a, tm_b, tm_f, tm_m, tm_r, tm_i, bt_w, after_first_grads=None, small_path=None,
                after_in_half=None):
    t_len = x.shape[0]
    cos, sin = _rope_tables(t_len)
    sgbt = jnp.broadcast_to(sgb[:, :, None], (G, C, 128))
    rlf = jnp.broadcast_to(rlf.reshape(H, 1), (H, 128))
    rlb = jnp.broadcast_to(rlb.reshape(H, 1), (H, 128))
    hc, kvc, scf, scb = _ctx_fwd(ctx, cmod6, norm1, win, rlf, rlb)
    hx, zuv, q, k, v, gfb, of, ya, sst = _fwd_a(x, mod6, norm1, win, sgw, sgbt, sggain, cos, sin, rlf, scf, tm=tm_a)
    if callable(wout):
        wout, tok = wout(hx)
        rlb = rlb + tok
    ob, ycat, x1, rst = _fwd_b(q, k, v, of, gfb, ya, x, mod6, wout, rlb, scb, tm=tm_b)
    wg, wu, wd = ffn_weights(x1) if callable(ffn_weights) else ffn_weights
    dx1, h2, da, db, hm, df, small_f = _ffn(x1, tgt, mod6, norm2, normf, wg, wu, wd, tm=tm_f)
    bt2 = min(2 * bt_w, t_len)
    d_wd, d_wd_b = _tn_matmul(hm, df, bm=DFF // 2, bt=bt2, name="dw_down")
    d_wg, d_wg_b = _tn_matmul(da, h2, bm=DFF // 2, bt=bt2, name="dw_gate")
    d_wu, d_wu_b = _tn_matmul(db, h2, bm=DFF // 2, bt=bt2, name="dw_up")
    dxb, dya, dgfb, dof, dob = _mid_bwd(dx1, of, ob, gfb, mod6, wout, tm=tm_m)
    d_wo, d_wo_b, dg1 = _dw_out(ycat, dxb, mod6, wout, bt=bt2)
    if after_first_grads is not None:
        rlf = rlf + after_first_grads((d_wd_b, d_wg_b, d_wu_b, d_wo_b))
    dqf, dkf, dvf, dqb, dkb, dvb, dscf, dscb, dlg = _ret_bwd(q, k, v, dof, dob, sst, rst, rlf, rlb, tm=tm_r)
    gx, dz, small_i, dsgw = _in_bwd(x, zuv, dya, dqf, dkf, dvf, dqb, dkb, dvb, dgfb, dx1, cos, sin,
                                    mod6, norm1, win, sgw, sgbt, sggain, tm=tm_i)
    dwin_c, small_c, dlg = _ctx_bwd(ctx, hc, kvc, cmod6, norm1, win, rlf, rlb, dscf, dscb, dlg)
    pack = jnp.concatenate([small_f, small_i, small_c, dg1, dlg], axis=0)
    after = small_path(pack, dsgw) if small_path is not None else ()
    halves = []
    for j in range(2):
        halves.append(_tn_matmul(dz, hx, bm=INC // 2, bt=bt2, name="dw_in_%d" % j, init=dwin_c,
                                 init_rows=(KV_LO, KV_HI), after=after, cols=(j, D // 2)))
        if after_in_half is not None:
            after = after_in_half(j, halves[-1][1])
    grads = {"w_in": halves, "w_out": (d_wo, d_wo_b), "w_gate": (d_wg, d_wg_b), "w_up": (d_wu, d_wu_b),
             "w_down": (d_wd, d_wd_b)}
    return gx, grads, pack, dsgw


def kernel(x, c, ctx, c_ctx, w_mod, b_mod, norm1, w_in, sg_gain, sg_w, sg_b, ret_logit_f, ret_logit_b, w_out, norm2, w_gate, w_up, w_down, norm_f, loss_target, m_c_ctx, m_w_mod, m_b_mod, m_norm1, m_w_in, m_sg_gain, m_sg_w, m_sg_b, m_ret_logit_f, m_ret_logit_b, m_w_out, m_norm2, m_w_gate, m_w_up, m_w_down, m_norm_f, v_c_ctx, v_w_mod, v_b_mod, v_norm1, v_w_in, v_sg_gain, v_sg_w, v_sg_b, v_ret_logit_f, v_ret_logit_b, v_w_out, v_norm2, v_w_gate, v_w_up, v_w_down, v_norm_f):
    t_len = x.shape[1]
    tm = min(512, t_len)
    me = 4 * lax.axis_index("x") + 2 * lax.axis_index("y") + lax.axis_index("c")
    tr = lambda a: jnp.transpose(a[0])

    cctx_row = c_ctx.reshape(1, D)
    mod_all, cs_all, g_in = _entry_gather(c, cctx_row, w_mod[0], b_mod, [tr(w_in)])
    mod6 = lax.dynamic_slice(mod_all, (me, 0), (1, 6 * D)).reshape(6, D)
    cmod6 = mod_all[8].reshape(6, D)
    win_t = g_in.reshape(INC, D)

    (out_shard,) = _cast_bf16([w_out[0]], name="cast_out", after=[g_in])
    h_out, tok_out = _exchange_start([out_shard], scatter=False, name="wout_start")
    ffn_shards = _cast_bf16([tr(w_gate), tr(w_up), w_down[0]], name="cast_ffn", after=[h_out[2]])
    sems_ffn, arrs_ffn, tok = _gather2_start(ffn_shards, name="wffn_start")
    mod6 = mod6 + (tok_out[0, 0] + tok[0, 0])
    ffn = {}

    def place_own(own, lands, rows):
        return [lax.dynamic_update_slice(l, o[None], (me, 0, 0)).reshape(rows, D) for o, l in zip(own, lands)]

    def wout(after):
        own, lands = _exchange_wait(h_out, [after], scatter=False, name="wout_wait")
        arrs = _gather2_arrived(sems_ffn, arrs_ffn, [after], name="wffn_arrived")
        ffn["own"] = arrs[:3]
        ffn["sems"], ffn["lands"], tok_fwd = _gather2_forward(arrs[3:], name="wffn_forward")
        return place_own(own, lands, D)[0], tok_fwd[0, 0]

    def ffn_weights(after):
        lands = _gather2_wait(ffn["sems"], ffn["lands"], [after], name="wffn_wait")
        return place_own(ffn["own"], lands, DFF)

    rs, outs = {}, {}

    def after_first_grads(bf):
        rs["first"], tok_first = _exchange_start([b.reshape(NDEV, b.shape[0] // NDEV, D) for b in bf], scatter=True,
                                                 name="rs_first_start")
        return tok_first[0, 0]

    def small_path(pack, dsgw):
        rs["small"], _ = _exchange_start([pack, dsgw.reshape(G * C, C)], scatter=False, name="small_start")
        return [rs["small"][2], rs["small"][3]]

    def after_in_half(j, half_bf16):
        rs["in%d" % j], _ = _exchange_start([half_bf16.reshape(NDEV, INC // NDEV, D // 2)], scatter=True,
                                            name="rs_in%d_start" % j)
        return [rs["in%d" % j][2]]

    gx, grads, pack, dsgw = _local_step(
        x[0], loss_target[0], ctx[0], mod6, cmod6, norm1, norm2[0:1], norm_f.reshape(1, D), win_t, wout, ffn_weights,
        sg_w[0], sg_b[0], sg_gain, ret_logit_f, ret_logit_b,
        tm_a=tm, tm_b=min(1024, t_len), tm_f=min(256, t_len), tm_m=min(1024, t_len), tm_r=min(1024, t_len), tm_i=tm,
        bt_w=min(1024, t_len),
        after_first_grads=after_first_grads, small_path=small_path, after_in_half=after_in_half)

    me_arr = me.reshape(1).astype(jnp.int32)
    (pack_own, sgw_own), (pack_land, sgw_land) = _exchange_wait(rs["small"], [rs["in1"][2]], scatter=False,
                                                               name="small_wait")
    psum_rows, pall, sgw_sum, part = _small_sum(me_arr, pack_own, pack_land, sgw_own, sgw_land, w_mod[0])
    h_cc, _ = _exchange_start([part], scatter=False, name="cctx_start")

    dmod_rows = (ROW_I + 0, ROW_I + 1, ROW_G1, ROW_F + 2, ROW_F + 3, ROW_F + 1)
    dmod_all = jnp.concatenate([pall[:, r, :] for r in dmod_rows], axis=1)
    dcmod = jnp.concatenate([psum_rows[ROW_C + 0:ROW_C + 1], psum_rows[ROW_C + 1:ROW_C + 2],
                             jnp.zeros((1, 4 * D), F32)], axis=1)
    dmod_mat = jnp.concatenate([dmod_all, jnp.pad(dcmod, ((0, 7), (0, 0)))], axis=0)
    ncol = 6 * D // NDEV
    dmod_cols = lax.dynamic_slice(dmod_mat, (0, me * ncol), (16, ncol))
    g_wm, d_wm, m_wm, v_wm = _wmod_grad(cs_all, dmod_cols, w_mod[0], m_w_mod[0], v_w_mod[0])
    outs["w_mod"] = (g_wm[None], d_wm[None], m_wm[None], v_wm[None])
    small_params = {
        "norm1": (norm1, m_norm1, v_norm1), "norm2": (norm2, m_norm2, v_norm2),
        "norm_f": tuple(a.reshape(1, D) for a in (norm_f, m_norm_f, v_norm_f)),
        "sg_gain": (sg_gain, m_sg_gain, v_sg_gain), "sg_b": (sg_b[0], m_sg_b[0], v_sg_b[0]),
        "ret_logit_f": (ret_logit_f, m_ret_logit_f, v_ret_logit_f),
        "ret_logit_b": (ret_logit_b, m_ret_logit_b, v_ret_logit_b),
        "b_mod": (b_mod, m_b_mod, v_b_mod), "sg_w": (sg_w[0], m_sg_w[0], v_sg_w[0])}
    small, loss = _adam_small(psum_rows, sgw_sum.reshape(G, C, C), small_params)
    back = {"norm_f": lambda a: a.reshape(D), "sg_b": lambda a: a[None], "sg_w": lambda a: a[None]}
    for name, vals in small.items():
        outs[name] = tuple(back.get(name, lambda a: a)(a) for a in vals)

    _, lands_first = _exchange_wait(rs["first"], [g_wm], scatter=True, name="rs_first_wait")

    def finish(name, fulls, lands, w, m, v, transposed):
        take = tr if transposed else (lambda a: a[0])
        res = _rs_adam(me_arr, fulls, lands, take(w), take(m), take(v), name="adam_" + name)
        put = (lambda a: jnp.transpose(a)[None]) if transposed else (lambda a: a[None])
        outs[name] = tuple(put(a) for a in res)
        return res[0]

    g_down = finish("w_down", [grads["w_down"][0]], [lands_first[0]], w_down, m_w_down, v_w_down, False)
    g_gate = finish("w_gate", [grads["w_gate"][0]], [lands_first[1]], w_gate, m_w_gate, v_w_gate, True)
    g_up = finish("w_up", [grads["w_up"][0]], [lands_first[2]], w_up, m_w_up, v_w_up, True)
    g_out = finish("w_out", [grads["w_out"][0]], [lands_first[3]], w_out, m_w_out, v_w_out, False)
    done = [g_down, g_gate, g_up, g_out]
    (part_own,), (part_land,) = _exchange_wait(h_cc, done, scatter=False, name="cctx_wait")
    res_cc = _cctx_final(me_arr, part_own, part_land, cctx_row, m_c_ctx.reshape(1, D), v_c_ctx.reshape(1, D))
    outs["c_ctx"] = tuple(a.reshape(D) for a in res_cc)
    lands_in = [_exchange_wait(rs["in%d" % j], done, scatter=True, name="rs_in%d_wait" % j)[1][0] for j in range(2)]
    finish("w_in", [h[0] for h in grads["w_in"]], lands_in, w_in, m_w_in, v_w_in, True)

    order = ["c_ctx", "w_mod", "b_mod", "norm1", "w_in", "sg_gain", "sg_w", "sg_b", "ret_logit_f", "ret_logit_b",
             "w_out", "norm2", "w_gate", "w_up", "w_down", "norm_f"]
    res = [loss.reshape(()), gx[None]]
    for j in range(4):
        res += [outs[name][j] for name in order]
    return tuple(res)
```

```python
import functools
import math

import numpy as np
import jax
import jax.numpy as jnp
from jax import lax
from jax.experimental import pallas as pl
from jax.experimental.pallas import tpu as pltpu

F32 = jnp.float32
BF16 = jnp.bfloat16

D = 1024
AW = 512
RW = 512
H = 4
DH = 128
G = 4
C = 128
CR = 256
DFF = 2816
INC = 3584
Q_LO = 2 * AW
KV_LO, VR_LO, G_LO = Q_LO + RW, Q_LO + 2 * RW, Q_LO + 3 * RW
KV_HI = G_LO
NDEV = 8
EPS = 1e-6
KSCALE = DH ** -0.5
ROPE_BASE = 10000.0
GRID_W = 64

ADAM_LR = 0.001
ADAM_B1 = 0.9
ADAM_B2 = 0.999
ADAM_EPS = 1e-08
ADAM_WD = 0.01
ADAM_STEP = 10

VMEM_LIMIT = 56 * 1024 * 1024
MESH = pl.DeviceIdType.MESH


def _cparams(n_grid=0, **kw):
    sem = ("arbitrary",) * n_grid if n_grid else None
    return pltpu.CompilerParams(dimension_semantics=sem, vmem_limit_bytes=VMEM_LIMIT, **kw)


def _dot(a, b):
    return jnp.dot(a, b, preferred_element_type=F32)


def _dot_nt(a, b):
    return lax.dot_general(a, b, (((1,), (1,)), ((), ())), preferred_element_type=F32)


def _dot_tn(a, b):
    return lax.dot_general(a, b, (((0,), (0,)), ((), ())), preferred_element_type=F32)


def _whole(shape):
    nd = len(shape)
    return pl.BlockSpec(shape, lambda *_: (0,) * nd, pipeline_mode=pl.Buffered(1))


def _acc(shape):
    nd = len(shape)
    return pl.BlockSpec(shape, lambda *_: (0,) * nd)


def _rows(tm, n):
    return pl.BlockSpec((tm, n), lambda i: (i, 0))


def _rows_rev(tm, n, nt):
    return pl.BlockSpec((tm, n), lambda i: (nt - 1 - i, 0))


def _sigmoid(x):
    return 1.0 / (1.0 + jnp.exp(-x))


def _log_sigmoid(x):
    return jnp.minimum(x, 0.0) - jnp.log(1.0 + jnp.exp(-jnp.abs(x)))


_GK0 = math.sqrt(2.0 / math.pi)
_GK1 = 0.044715


def _gelu(x):
    x2 = x * x
    t = jnp.tanh(x * (_GK0 + (_GK0 * _GK1) * x2))
    h = 0.5 + 0.5 * t
    g = x * h
    dg = h + g * (1.0 - h) * ((2.0 * _GK0) + (6.0 * _GK0 * _GK1) * x2)
    return g, dg


def _silu_parts(a):
    s = _sigmoid(a)
    sa = a * s
    return sa, s + sa * (1.0 - s)


def _rope(t, cos, sins):
    n = t.shape[1]
    lane = lax.broadcasted_iota(jnp.int32, t.shape, 1)
    first = (lane & 63) < 32
    partner = jnp.where(first, pltpu.roll(t, n - 32, 1), pltpu.roll(t, 32, 1))
    return t * cos + partner * sins


def _headnorm(o):
    outs, rs = [], []
    for h in range(H):
        oh = o[:, h * DH:(h + 1) * DH]
        r = lax.rsqrt(jnp.mean(oh * oh, axis=-1, keepdims=True) + EPS)
        outs.append(oh * r)
        rs.append(r)
    return jnp.concatenate(outs, axis=1), rs


def _decay_consts(lg, forward):
    ii = lax.broadcasted_iota(jnp.int32, (CR, CR), 0).astype(F32)
    jj = lax.broadcasted_iota(jnp.int32, (CR, CR), 1).astype(F32)
    ic = lax.broadcasted_iota(jnp.int32, (CR, 1), 0).astype(F32)
    if forward:
        diff = ii - jj
        xi = jnp.exp(lg * (ic + 1.0))
        z = jnp.exp(lg * (CR - 1.0 - ic))
    else:
        diff = jj - ii
        xi = jnp.exp(lg * (CR - ic))
        z = jnp.exp(lg * ic)
    dm = jnp.where(diff >= 0.0, jnp.exp(lg * jnp.maximum(diff, 0.0)), 0.0)
    dec = jnp.exp(lg * float(CR))
    return dm, xi, z, dec, ic


def _ctx_fwd(ctx, cmod6, norm1, win, rlf, rlb):
    lc = ctx.shape[0]

    def body(ctx_ref, cmod_ref, n1_ref, win_ref, rlf_ref, rlb_ref, hc_ref, kvc_ref, scf_ref, scb_ref):
        x = ctx_ref[...]
        r = lax.rsqrt(jnp.mean(x * x, axis=-1, keepdims=True) + EPS)
        hc = (x * r) * (n1_ref[...] * (1.0 + cmod_ref[1:2, :])) + cmod_ref[0:1, :]
        hcb = hc.astype(BF16)
        hc_ref[...] = hcb
        kv = _dot_nt(hcb, win_ref[KV_LO:KV_HI, :])
        k = kv[:, :RW] * KSCALE
        v = kv[:, RW:]
        kvc_ref[:, :RW] = k
        kvc_ref[:, RW:] = v
        t = lax.broadcasted_iota(jnp.int32, (lc, 1), 0).astype(F32)
        lgf = _log_sigmoid(rlf_ref[...])
        lgb = _log_sigmoid(rlb_ref[...])
        for h in range(H):
            hs = slice(h * DH, (h + 1) * DH)
            wf = jnp.exp(lgf[h:h + 1, 0:1] * (lc - 1.0 - t))
            wb = jnp.exp(lgb[h:h + 1, 0:1] * t)
            vh = v[:, hs].astype(BF16)
            scf_ref[h] = _dot_tn((k[:, hs] * wf).astype(BF16), vh)
            scb_ref[h] = _dot_tn((k[:, hs] * wb).astype(BF16), vh)

    return pl.pallas_call(
        body, name="ctx_fwd",
        out_shape=(jax.ShapeDtypeStruct((lc, D), BF16), jax.ShapeDtypeStruct((lc, 2 * RW), F32),
                   jax.ShapeDtypeStruct((H, DH, DH), F32), jax.ShapeDtypeStruct((H, DH, DH), F32)),
        compiler_params=_cparams(),
    )(ctx, cmod6, norm1, win, rlf, rlb)


def _sg_forward(u, v, sgw_ref, sgbt_ref, sgg_ref, nc):
    ua, dgu = _gelu(u)
    va, dgv = _gelu(v)
    gain = sgg_ref[...]
    mixed, vn_b, vah_l, rv_l = [], [], [], []
    for g in range(G):
        gs = slice(g * DH, (g + 1) * DH)
        vag = va[:, gs]
        rv = lax.rsqrt(jnp.mean(vag * vag, axis=-1, keepdims=True) + EPS)
        vah = vag * rv
        vn = (vah * gain[:, gs]).astype(BF16)
        wg = sgw_ref[g].astype(BF16)
        bcol = sgbt_ref[g]
        rows = [_dot(wg, vn[c * C:(c + 1) * C, :]) + bcol for c in range(nc)]
        mixed.append(jnp.concatenate(rows, axis=0) if nc > 1 else rows[0])
        vn_b.append(vn)
        vah_l.append(vah)
        rv_l.append(rv)
    mixed = jnp.concatenate(mixed, axis=1)
    return ua * mixed, (ua, dgu, dgv, mixed, vn_b, vah_l, rv_l)


def _fwd_a(x, mod6, norm1, win, sgw, sgbt, sggain, cos, sin, rlf, scf, *, tm):
    t_len = x.shape[0]
    nt, nc, ncr = t_len // tm, tm // C, tm // CR

    def body(x_ref, mod_ref, n1_ref, win_ref, sgw_ref, sgbt_ref, sgg_ref, cos_ref, sin_ref, rlf_ref, scf_ref,
             hx_ref, zuv_ref, q_ref, k_ref, v_ref, gfb_ref, of_ref, ya_ref, sst_ref, s_scr):
        i = pl.program_id(0)

        @pl.when(i == 0)
        def _():
            s_scr[...] = scf_ref[...]

        x = x_ref[...]
        r = lax.rsqrt(jnp.mean(x * x, axis=-1, keepdims=True) + EPS)
        hx = (x * r) * (n1_ref[...] * (1.0 + mod_ref[1:2, :])) + mod_ref[0:1, :]
        hxb = hx.astype(BF16)
        hx_ref[...] = hxb
        z = _dot_nt(hxb, win_ref[...])
        zuv_ref[...] = z[:, :2 * AW].astype(BF16)
        gfb_ref[...] = z[:, G_LO:INC].astype(BF16)
        cos = jnp.tile(cos_ref[...], (1, H))
        sins = jnp.tile(sin_ref[...], (1, H))
        q_ref[...] = _rope(z[:, Q_LO:KV_LO], cos, sins).astype(BF16)
        k_ref[...] = (_rope(z[:, KV_LO:VR_LO], cos, sins) * KSCALE).astype(BF16)
        v_ref[...] = z[:, VR_LO:G_LO].astype(BF16)
        ya, _ = _sg_forward(z[:, :AW], z[:, AW:2 * AW], sgw_ref, sgbt_ref, sgg_ref, nc)
        ya_ref[...] = ya.astype(BF16)

        lg = _log_sigmoid(rlf_ref[...])
        for h in range(H):
            dm, xi, zc, dec, _ = _decay_consts(lg[h:h + 1, 0:1], True)
            hs = slice(h * DH, (h + 1) * DH)
            for c in range(ncr):
                rs = slice(c * CR, (c + 1) * CR)
                qc, kc, vc = q_ref[rs, hs], k_ref[rs, hs], v_ref[rs, hs]
                s = s_scr[h]
                sst_ref[c, h] = s
                a = (_dot_nt(qc, kc) * dm).astype(BF16)
                of_ref[rs, hs] = (_dot(a, vc) + xi * _dot(qc, s.astype(BF16))).astype(BF16)
                kz = (kc.astype(F32) * zc).astype(BF16)
                s_scr[h] = dec * s + _dot_tn(kz, vc)

    n_chunks = t_len // CR
    return pl.pallas_call(
        body, name="fwd_a", grid=(nt,),
        in_specs=[_rows(tm, D), _whole((6, D)), _whole((1, D)), _whole((INC, D)), _whole((G, C, C)),
                  _whole((G, C, 128)), _whole((1, AW)), _rows(tm, DH), _rows(tm, DH), _whole((H, 128)),
                  _whole((H, DH, DH))],
        out_specs=[_rows(tm, D), _rows(tm, 2 * AW), _rows(tm, RW), _rows(tm, RW), _rows(tm, RW),
                   _rows(tm, 2 * RW), _rows(tm, RW), _rows(tm, AW),
                   pl.BlockSpec((ncr, H, DH, DH), lambda i: (i, 0, 0, 0))],
        out_shape=(jax.ShapeDtypeStruct((t_len, D), BF16), jax.ShapeDtypeStruct((t_len, 2 * AW), BF16),
                   jax.ShapeDtypeStruct((t_len, RW), BF16), jax.ShapeDtypeStruct((t_len, RW), BF16),
                   jax.ShapeDtypeStruct((t_len, RW), BF16), jax.ShapeDtypeStruct((t_len, 2 * RW), BF16),
                   jax.ShapeDtypeStruct((t_len, RW), BF16), jax.ShapeDtypeStruct((t_len, AW), BF16),
                   jax.ShapeDtypeStruct((n_chunks, H, DH, DH), F32)),
        scratch_shapes=[pltpu.VMEM((H, DH, DH), F32)],
        compiler_params=_cparams(1),
    )(x, mod6, norm1, win, sgw, sgbt, sggain, cos, sin, rlf, scf)


def _fwd_b(q, k, v, of, gfb, ya, x, mod6, wout, rlb, scb, *, tm):
    t_len = x.shape[0]
    nt, nc = t_len // tm, tm // CR

    def body(q_ref, k_ref, v_ref, of_ref, gfb_ref, ya_ref, x_ref, mod_ref, wout_ref, rlb_ref, scb_ref,
             ob_ref, ycat_ref, x1_ref, rst_ref, r_scr):
        i = pl.program_id(0)

        @pl.when(i == 0)
        def _():
            r_scr[...] = scb_ref[...]

        lg = _log_sigmoid(rlb_ref[...])
        consts = [_decay_consts(lg[h:h + 1, 0:1], False) for h in range(H)]

        def first(c, h):
            dm, _, zc, dec, _ = consts[h]
            hs, rs = slice(h * DH, (h + 1) * DH), slice(c * CR, (c + 1) * CR)
            qc, kc, vc = q_ref[rs, hs], k_ref[rs, hs], v_ref[rs, hs]
            s = r_scr[h]
            rst_ref[c, h] = s
            scores = _dot_nt(qc, kc)
            cross = _dot(qc, s.astype(BF16))
            kz = (kc.astype(F32) * zc).astype(BF16)
            r_scr[h] = dec * s + _dot_tn(kz, vc)
            return (scores * dm).astype(BF16), cross

        def second(c, h, carried):
            a, cross = carried
            hs, rs = slice(h * DH, (h + 1) * DH), slice(c * CR, (c + 1) * CR)
            ob_ref[rs, hs] = (_dot(a, v_ref[rs, hs]) + consts[h][1] * cross).astype(BF16)

        def rows_out(c):
            rs = slice(c * CR, (c + 1) * CR)
            gfb = gfb_ref[rs, :].astype(F32)
            sf, _ = _silu_parts(gfb[:, :RW])
            sb, _ = _silu_parts(gfb[:, RW:])
            hnf, _ = _headnorm(of_ref[rs, :].astype(F32))
            hnb, _ = _headnorm(ob_ref[rs, :].astype(F32))
            yr = sf * hnf + sb * hnb
            ycat = jnp.concatenate([ya_ref[rs, :], yr.astype(BF16)], axis=1)
            ycat_ref[rs, :] = ycat
            x1_ref[rs, :] = x_ref[rs, :] + mod_ref[2:3, :] * _dot(ycat, wout_ref[...])

        pending, waiting_rows = None, None
        for c in reversed(range(nc)):
            for h in range(H):
                carried = first(c, h)
                if pending is not None:
                    second(*pending)
                pending = (c, h, carried)
            if waiting_rows is not None:
                rows_out(waiting_rows)
            waiting_rows = c
        second(*pending)
        rows_out(waiting_rows)

    n_chunks = t_len // CR
    rr = functools.partial(_rows_rev, nt=nt)
    return pl.pallas_call(
        body, name="fwd_b", grid=(nt,),
        in_specs=[rr(tm, RW), rr(tm, RW), rr(tm, RW), rr(tm, RW), rr(tm, 2 * RW), rr(tm, AW), rr(tm, D),
                  _whole((6, D)), _whole((D, D)), _whole((H, 128)), _whole((H, DH, DH))],
        out_specs=[rr(tm, RW), rr(tm, D), rr(tm, D),
                   pl.BlockSpec((nc, H, DH, DH), lambda i: (nt - 1 - i, 0, 0, 0))],
        out_shape=(jax.ShapeDtypeStruct((t_len, RW), BF16), jax.ShapeDtypeStruct((t_len, D), BF16),
                   jax.ShapeDtypeStruct((t_len, D), F32),
                   jax.ShapeDtypeStruct((n_chunks, H, DH, DH), F32)),
        scratch_shapes=[pltpu.VMEM((H, DH, DH), F32)],
        compiler_params=_cparams(1),
    )(q, k, v, of, gfb, ya, x, mod6, wout, rlb, scb)


def _ffn(x1, tgt, mod6, norm2, normf, wg, wu, wd, *, tm):
    t_len = x1.shape[0]
    nt = t_len // tm

    def body(x1_ref, tgt_ref, mod_ref, n2_ref, nf_ref, wg_ref, wu_ref, wd_ref,
             dx1_ref, h2_ref, da_ref, db_ref, hm_ref, df_ref, small_ref):
        i = pl.program_id(0)

        @pl.when(i == 0)
        def _():
            small_ref[...] = jnp.zeros_like(small_ref)

        sh2, sc2, g2 = mod_ref[3:4, :], mod_ref[4:5, :], mod_ref[5:6, :]
        n2, nf = n2_ref[...], nf_ref[...]
        x1 = x1_ref[...]
        r2 = lax.rsqrt(jnp.mean(x1 * x1, axis=-1, keepdims=True) + EPS)
        x1h = x1 * r2
        w2 = n2 * (1.0 + sc2)
        h2b = (x1h * w2 + sh2).astype(BF16)
        h2_ref[...] = h2b
        a = _dot_nt(h2b, wg_ref[...])
        b = _dot_nt(h2b, wu_ref[...])
        sa, dsa = _silu_parts(a)
        hmb = (sa * b).astype(BF16)
        hm_ref[...] = hmb
        f = _dot(hmb, wd_ref[...])
        x2 = x1 + g2 * f
        r3 = lax.rsqrt(jnp.mean(x2 * x2, axis=-1, keepdims=True) + EPS)
        x2h = x2 * r3
        diff = x2h * nf - tgt_ref[...]
        small_ref[5:6, :] += jnp.sum(diff * diff, axis=0, keepdims=True)
        dout = diff * (1.0 / D)
        small_ref[0:1, :] += jnp.sum(dout * x2h, axis=0, keepdims=True)
        dyg = dout * nf
        dx2 = r3 * (dyg - x2h * jnp.mean(dyg * x2h, axis=-1, keepdims=True))
        small_ref[1:2, :] += jnp.sum(dx2 * f, axis=0, keepdims=True)
        dfb = (dx2 * g2).astype(BF16)
        df_ref[...] = dfb
        dhm = _dot_nt(dfb, wd_ref[...])
        dbb = (dhm * sa).astype(BF16)
        dab = (dhm * b * dsa).astype(BF16)
        da_ref[...] = dab
        db_ref[...] = dbb
        dh2 = _dot(dab, wg_ref[...]) + _dot(dbb, wu_ref[...])
        small_ref[2:3, :] += jnp.sum(dh2, axis=0, keepdims=True)
        dhx = dh2 * x1h
        small_ref[3:4, :] += jnp.sum(dhx, axis=0, keepdims=True) * n2
        small_ref[4:5, :] += jnp.sum(dhx, axis=0, keepdims=True) * (1.0 + sc2)
        dyg2 = dh2 * w2
        dx1_ref[...] = dx2 + r2 * (dyg2 - x1h * jnp.mean(dyg2 * x1h, axis=-1, keepdims=True))

    return pl.pallas_call(
        body, name="ffn", grid=(nt,),
        in_specs=[_rows(tm, D), _rows(tm, D), _whole((6, D)), _whole((1, D)), _whole((1, D)),
                  _whole((DFF, D)), _whole((DFF, D)), _whole((DFF, D))],
        out_specs=[_rows(tm, D), _rows(tm, D), _rows(tm, DFF), _rows(tm, DFF), _rows(tm, DFF), _rows(tm, D),
                   _acc((8, D))],
        out_shape=(jax.ShapeDtypeStruct((t_len, D), F32), jax.ShapeDtypeStruct((t_len, D), BF16),
                   jax.ShapeDtypeStruct((t_len, DFF), BF16), jax.ShapeDtypeStruct((t_len, DFF), BF16),
                   jax.ShapeDtypeStruct((t_len, DFF), BF16), jax.ShapeDtypeStruct((t_len, D), BF16),
                   jax.ShapeDtypeStruct((8, D), F32)),
        compiler_params=_cparams(1),
    )(x1, tgt, mod6, norm2, normf, wg, wu, wd)


def _mid_bwd(dx1, of, ob, gfb, mod6, wout, *, tm):
    t_len = dx1.shape[0]
    nt = t_len // tm
    rc = min(256, tm)

    def body(dx1_ref, of_ref, ob_ref, gfb_ref, mod_ref, wout_ref,
             dxb_ref, dya_ref, dgfb_ref, dof_ref, dob_ref):
        for c in range(tm // rc):
            rows = slice(c * rc, (c + 1) * rc)
            dx1 = dx1_ref[rows, :]
            dxb_ref[rows, :] = dx1.astype(BF16)
            dyb = (dx1 * mod_ref[2:3, :]).astype(BF16)
            dycat = _dot_nt(dyb, wout_ref[...])
            dya_ref[rows, :] = dycat[:, :AW].astype(BF16)
            dyr = dycat[:, AW:]
            gfb = gfb_ref[rows, :].astype(F32)
            for o_ref, do_ref, lo in ((of_ref, dof_ref, 0), (ob_ref, dob_ref, RW)):
                sg, dsg = _silu_parts(gfb[:, lo:lo + RW])
                o = o_ref[rows, :].astype(F32)
                hn, rs = _headnorm(o)
                dgfb_ref[rows, lo:lo + RW] = (dyr * hn * dsg).astype(BF16)
                dhn = dyr * sg
                for h in range(H):
                    hs = slice(h * DH, (h + 1) * DH)
                    oh, dh = hn[:, hs], dhn[:, hs]
                    do_ref[rows, hs] = (rs[h] * (dh - oh * jnp.mean(dh * oh, axis=-1, keepdims=True))).astype(BF16)

    return pl.pallas_call(
        body, name="mid_bwd", grid=(nt,),
        in_specs=[_rows(tm, D), _rows(tm, RW), _rows(tm, RW), _rows(tm, 2 * RW), _whole((6, D)), _whole((D, D))],
        out_specs=[_rows(tm, D), _rows(tm, AW), _rows(tm, 2 * RW), _rows(tm, RW), _rows(tm, RW)],
        out_shape=(jax.ShapeDtypeStruct((t_len, D), BF16), jax.ShapeDtypeStruct((t_len, AW), BF16),
                   jax.ShapeDtypeStruct((t_len, 2 * RW), BF16), jax.ShapeDtypeStruct((t_len, RW), BF16),
                   jax.ShapeDtypeStruct((t_len, RW), BF16)),
        compiler_params=_cparams(1),
    )(dx1, of, ob, gfb, mod6, wout)


def _ret_bwd_items(q_ref, k_ref, v_ref, do_ref, st_ref, dq_ref, dk_ref, dv_ref, ds_scr, dlg_scr, lg, forward, nc, row0):
    order = list(reversed(range(nc))) if forward else list(range(nc))
    consts = [_decay_consts(lg[h:h + 1, 0:1], forward) for h in range(H)]
    accs = [jnp.zeros((1, DH), F32) for _ in range(H)]

    def first(h, c):
        dm, xi, zc, dec, _ = consts[h]
        hs, rs = slice(h * DH, (h + 1) * DH), slice(c * CR, (c + 1) * CR)
        qc, kc, vc, doc = q_ref[rs, hs], k_ref[rs, hs], v_ref[rs, hs], do_ref[rs, hs]
        s, dsn = st_ref[c, h], ds_scr[h]
        sb, dsnb = s.astype(BF16), dsn.astype(BF16)
        qf, kf = qc.astype(F32), kc.astype(F32)
        a_raw = _dot_nt(qc, kc)
        da_raw = _dot_nt(doc, vc)
        xdo = (xi * doc.astype(F32)).astype(BF16)
        kz = (kf * zc).astype(BF16)
        vz = (vc.astype(F32) * zc).astype(BF16)
        dq_c = _dot_nt(xdo, sb)
        dk_s = _dot_nt(vz, dsnb)
        dv_s = _dot(kz, dsnb)
        ds_scr[h] = _dot_tn((xi * qf).astype(BF16), doc) + dec * dsn
        accs[h] = accs[h] + (float(CR) * dec) * jnp.sum(s * dsn, axis=0, keepdims=True)
        a = (a_raw * dm).astype(BF16)
        da = (da_raw * dm).astype(BF16)
        return a, da, dq_c, dk_s, dv_s

    def second(h, c, carried):
        a, da, dq_c, dk_s, dv_s = carried
        ic = consts[h][4]
        hs, rs = slice(h * DH, (h + 1) * DH), slice(c * CR, (c + 1) * CR)
        qc, kc, doc = q_ref[rs, hs], k_ref[rs, hs], do_ref[rs, hs]
        dq = _dot(da, kc) + dq_c
        dk = _dot_tn(da, qc) + dk_s
        dq_ref[rs, hs] = dq.astype(BF16)
        dk_ref[rs, hs] = dk.astype(BF16)
        dv_ref[rs, hs] = (_dot_tn(a, doc) + dv_s).astype(BF16)
        qf, kf = qc.astype(F32), kc.astype(F32)
        both = jnp.sum((ic if forward else -ic) * (qf * dq - kf * dk), axis=0, keepdims=True)
        cross = jnp.sum(qf * dq_c, axis=0, keepdims=True)
        if forward:
            accs[h] = accs[h] + both + cross + (CR - 1.0) * jnp.sum(kf * dk_s, axis=0, keepdims=True)
        else:
            accs[h] = accs[h] + both + float(CR) * cross

    def finish():
        for h in range(H):
            dlg_scr[row0 + h:row0 + h + 1, :] += accs[h]

    items = [[(functools.partial(first, h, c), functools.partial(second, h, c)) for h in range(H)] for c in order]
    return items, finish


def _ret_bwd_run(dirs):
    nc = len(dirs[0][0])
    flat = [it for j in range(nc) for items, _ in dirs for it in items[j]]
    pending = None
    for first, second in flat:
        carried = first()
        if pending is not None:
            pending[0](pending[1])
        pending = (second, carried)
    pending[0](pending[1])
    for _, finish in dirs:
        finish()


def _ret_bwd(q, k, v, dof, dob, sst, rst, rlf, rlb, *, tm):
    t_len = q.shape[0]
    nt, nc = t_len // tm, tm // CR

    def body(qf_ref, kf_ref, vf_ref, dof_ref, sst_ref, qb_ref, kb_ref, vb_ref, dob_ref, rst_ref, rlf_ref, rlb_ref,
             dqf_ref, dkf_ref, dvf_ref, dqb_ref, dkb_ref, dvb_ref, dscf_ref, dscb_ref, dlg_ref,
             dsf_scr, dsb_scr, dlg_scr):
        i = pl.program_id(0)

        @pl.when(i == 0)
        def _():
            dsf_scr[...] = jnp.zeros_like(dsf_scr)
            dsb_scr[...] = jnp.zeros_like(dsb_scr)
            dlg_scr[...] = jnp.zeros_like(dlg_scr)

        lgf = _log_sigmoid(rlf_ref[...])
        lgb = _log_sigmoid(rlb_ref[...])
        _ret_bwd_run([
            _ret_bwd_items(qf_ref, kf_ref, vf_ref, dof_ref, sst_ref, dqf_ref, dkf_ref, dvf_ref, dsf_scr, dlg_scr,
                           lgf, True, nc, 0),
            _ret_bwd_items(qb_ref, kb_ref, vb_ref, dob_ref, rst_ref, dqb_ref, dkb_ref, dvb_ref, dsb_scr, dlg_scr,
                           lgb, False, nc, H)])

        @pl.when(i == nt - 1)
        def _():
            dscf_ref[...] = dsf_scr[...]
            dscb_ref[...] = dsb_scr[...]
            tot = jnp.broadcast_to(jnp.sum(dlg_scr[...], axis=1, keepdims=True), (8, 128))
            ri = lax.broadcasted_iota(jnp.int32, (8, 128), 0)
            li = lax.broadcasted_iota(jnp.int32, (8, 128), 1)
            dlg_ref[...] = jnp.zeros_like(dlg_ref)
            dlg_ref[0:1, 0:128] = jnp.sum(jnp.where(ri == li, tot, 0.0), axis=0, keepdims=True)
            dlg_ref[1:2, 0:128] = jnp.sum(jnp.where(ri - H == li, tot, 0.0), axis=0, keepdims=True)

    rr = functools.partial(_rows_rev, nt=nt)
    st_f = pl.BlockSpec((nc, H, DH, DH), lambda i: (nt - 1 - i, 0, 0, 0))
    st_b = pl.BlockSpec((nc, H, DH, DH), lambda i: (i, 0, 0, 0))
    tok = jax.ShapeDtypeStruct((t_len, RW), BF16)
    st = jax.ShapeDtypeStruct((H, DH, DH), F32)
    return pl.pallas_call(
        body, name="ret_bwd", grid=(nt,),
        in_specs=[rr(tm, RW), rr(tm, RW), rr(tm, RW), rr(tm, RW), st_f,
                  _rows(tm, RW), _rows(tm, RW), _rows(tm, RW), _rows(tm, RW), st_b,
                  _whole((H, 128)), _whole((H, 128))],
        out_specs=[rr(tm, RW), rr(tm, RW), rr(tm, RW), _rows(tm, RW), _rows(tm, RW), _rows(tm, RW),
                   _acc((H, DH, DH)), _acc((H, DH, DH)), _acc((8, D))],
        out_shape=(tok, tok, tok, tok, tok, tok, st, st, jax.ShapeDtypeStruct((8, D), F32)),
        scratch_shapes=[pltpu.VMEM((H, DH, DH), F32), pltpu.VMEM((H, DH, DH), F32), pltpu.VMEM((8, 128), F32)],
        compiler_params=_cparams(1),
    )(q, k, v, dof, sst, q, k, v, dob, rst, rlf, rlb)


def _in_bwd(x, zuv, dya, dqf, dkf, dvf, dqb, dkb, dvb, dgfb, dx1, cos, sin, mod6, norm1, win, sgw, sgbt, sggain, *, tm):
    t_len = x.shape[0]
    nt, nc = t_len // tm, tm // C

    def body(x_ref, zuv_ref, dya_ref, dqf_ref, dkf_ref, dvf_ref, dqb_ref, dkb_ref, dvb_ref, dgfb_ref, dx1_ref,
             cos_ref, sin_ref, mod_ref, n1_ref, win_ref, sgw_ref, sgbt_ref, sgg_ref,
             gx_ref, dz_ref, small_ref, dsgw_ref, dsgbt_ref):
        i = pl.program_id(0)

        @pl.when(i == 0)
        def _():
            small_ref[...] = jnp.zeros_like(small_ref)
            dsgw_ref[...] = jnp.zeros_like(dsgw_ref)
            dsgbt_ref[...] = jnp.zeros_like(dsgbt_ref)

        zuv = zuv_ref[...].astype(F32)
        _, (ua, dgu, dgv, mixed, vn_b, vah_l, rv_l) = _sg_forward(zuv[:, :AW], zuv[:, AW:], sgw_ref, sgbt_ref, sgg_ref, nc)
        dya = dya_ref[...].astype(F32)
        dz_ref[:, 0:AW] = (dya * mixed * dgu).astype(BF16)
        dmixed = dya * ua
        gain = sgg_ref[...]
        for g in range(G):
            gs = slice(g * DH, (g + 1) * DH)
            dmg = dmixed[:, gs]
            dmb = dmg.astype(BF16)
            wgt = sgw_ref[g].astype(BF16)
            dsw = jnp.zeros((C, C), F32)
            dsb = jnp.zeros((C, DH), F32)
            rows = []
            for c in range(nc):
                rs = slice(c * C, (c + 1) * C)
                dsw = dsw + _dot_nt(dmb[rs, :], vn_b[g][rs, :])
                dsb = dsb + dmg[rs, :]
                rows.append(_dot_tn(wgt, dmb[rs, :]))
            dsgw_ref[g] += dsw
            dsgbt_ref[g] += dsb
            dvn = jnp.concatenate(rows, axis=0) if nc > 1 else rows[0]
            vah, rv = vah_l[g], rv_l[g]
            small_ref[3:4, gs] += jnp.sum(dvn * vah, axis=0, keepdims=True)
            dyg = dvn * gain[:, gs]
            dva = rv * (dyg - vah * jnp.mean(dyg * vah, axis=-1, keepdims=True))
            dz_ref[:, AW + g * DH:AW + (g + 1) * DH] = (dva * dgv[:, gs]).astype(BF16)

        cos = jnp.tile(cos_ref[...], (1, H))
        nsins = -jnp.tile(sin_ref[...], (1, H))
        def both(f_ref, b_ref):
            return f_ref[...].astype(F32) + b_ref[...].astype(F32)

        dz_ref[:, Q_LO:KV_LO] = _rope(both(dqf_ref, dqb_ref), cos, nsins).astype(BF16)
        dz_ref[:, KV_LO:VR_LO] = (_rope(both(dkf_ref, dkb_ref), cos, nsins) * KSCALE).astype(BF16)
        dz_ref[:, VR_LO:G_LO] = both(dvf_ref, dvb_ref).astype(BF16)
        dz_ref[:, G_LO:INC] = dgfb_ref[...]

        dhx = _dot(dz_ref[...], win_ref[...])
        x = x_ref[...]
        r = lax.rsqrt(jnp.mean(x * x, axis=-1, keepdims=True) + EPS)
        xh = x * r
        n1, sc1 = n1_ref[...], mod_ref[1:2, :]
        small_ref[0:1, :] += jnp.sum(dhx, axis=0, keepdims=True)
        dhxx = jnp.sum(dhx * xh, axis=0, keepdims=True)
        small_ref[1:2, :] += dhxx * n1
        small_ref[2:3, :] += dhxx * (1.0 + sc1)
        dyg = dhx * (n1 * (1.0 + sc1))
        gx_ref[...] = dx1_ref[...] + r * (dyg - xh * jnp.mean(dyg * xh, axis=-1, keepdims=True))

        @pl.when(i == nt - 1)
        def _():
            ri = lax.broadcasted_iota(jnp.int32, (C, DH), 0)
            li = lax.broadcasted_iota(jnp.int32, (C, DH), 1)
            for g in range(G):
                tot = jnp.broadcast_to(jnp.sum(dsgbt_ref[g], axis=1, keepdims=True), (C, DH))
                small_ref[4 + g:5 + g, 0:DH] = jnp.sum(jnp.where(ri == li, tot, 0.0), axis=0, keepdims=True)

    tok = functools.partial(_rows, tm)
    return pl.pallas_call(
        body, name="in_bwd", grid=(nt,),
        in_specs=[tok(D), tok(2 * AW), tok(AW), tok(RW), tok(RW), tok(RW), tok(RW), tok(RW), tok(RW),
                  tok(2 * RW), tok(D), tok(DH), tok(DH), _whole((6, D)), _whole((1, D)), _whole((INC, D)),
                  _whole((G, C, C)), _whole((G, C, 128)), _whole((1, AW))],
        out_specs=[tok(D), tok(INC), _acc((8, D)), _acc((G, C, C))],
        out_shape=(jax.ShapeDtypeStruct((t_len, D), F32), jax.ShapeDtypeStruct((t_len, INC), BF16),
                   jax.ShapeDtypeStruct((8, D), F32), jax.ShapeDtypeStruct((G, C, C), F32)),
        scratch_shapes=[pltpu.VMEM((G, C, 128), F32)],
        compiler_params=_cparams(1),
    )(x, zuv, dya, dqf, dkf, dvf, dqb, dkb, dvb, dgfb, dx1, cos, sin, mod6, norm1, win, sgw, sgbt, sggain)


def _tn_matmul(a, b, *, bm, bt, name, init=None, init_rows=None, after=(), cols=None):
    t_len, m = a.shape
    cj, n = (0, b.shape[1]) if cols is None else cols
    ni, ntt = m // bm, t_len // bt
    has_init = init is not None

    def body(*refs):
        o_ref, ob_ref = refs[-2:]
        a_ref, b_ref = refs[:2]
        init_ref = refs[2] if has_init else None
        i, t = pl.program_id(0), pl.program_id(1)

        @pl.when(t == 0)
        def _():
            o_ref[...] = jnp.zeros_like(o_ref)

        if has_init:
            for ib in range(ni):
                lo, hi = max(init_rows[0], ib * bm), min(init_rows[1], (ib + 1) * bm)
                if lo < hi:
                    @pl.when(jnp.logical_and(t == 0, i == ib))
                    def _(lo=lo, hi=hi, ib=ib):
                        o_ref[lo - ib * bm:hi - ib * bm, :] = init_ref[lo - init_rows[0]:hi - init_rows[0], :]

        o_ref[...] += _dot_tn(a_ref[...], b_ref[...])

        @pl.when(t == ntt - 1)
        def _():
            ob_ref[...] = o_ref[...].astype(BF16)

    in_specs = [pl.BlockSpec((bt, bm), lambda i, t: (t, i)), pl.BlockSpec((bt, n), lambda i, t: (t, cj))]
    args = [a, b]
    if has_init:
        in_specs.append(pl.BlockSpec((init.shape[0], n), lambda i, t: (0, cj), pipeline_mode=pl.Buffered(1)))
        args.append(init)
    for arr in after:
        in_specs.append(pl.BlockSpec(memory_space=pl.ANY))
        args.append(arr)
    out_spec = pl.BlockSpec((bm, n), lambda i, t: (i, 0))
    return pl.pallas_call(
        body, name=name, grid=(ni, ntt),
        in_specs=in_specs,
        out_specs=[out_spec, out_spec],
        out_shape=(jax.ShapeDtypeStruct((m, n), F32), jax.ShapeDtypeStruct((m, n), BF16)),
        compiler_params=_cparams(2),
    )(*args)


def _dw_out(ycat, dxb, mod6, wout, *, bt):
    t_len = ycat.shape[0]
    ntt = t_len // bt

    def body(a_ref, b_ref, mod_ref, wout_ref, o_ref, ob_ref, dg1_ref):
        t = pl.program_id(0)

        @pl.when(t == 0)
        def _():
            o_ref[...] = jnp.zeros_like(o_ref)

        o_ref[...] += _dot_tn(a_ref[...], b_ref[...])

        @pl.when(t == ntt - 1)
        def _():
            m = o_ref[...]
            dg1_ref[...] = jnp.zeros_like(dg1_ref)
            dg1_ref[0:1, :] = jnp.sum(m * wout_ref[...].astype(F32), axis=0, keepdims=True)
            g = m * mod_ref[2:3, :]
            o_ref[...] = g
            ob_ref[...] = g.astype(BF16)

    return pl.pallas_call(
        body, name="dw_out", grid=(ntt,),
        in_specs=[pl.BlockSpec((bt, D), lambda t: (t, 0)), pl.BlockSpec((bt, D), lambda t: (t, 0)),
                  _whole((6, D)), _whole((D, D))],
        out_specs=[_acc((D, D)), _acc((D, D)), _acc((8, D))],
        out_shape=(jax.ShapeDtypeStruct((D, D), F32), jax.ShapeDtypeStruct((D, D), BF16),
                   jax.ShapeDtypeStruct((8, D), F32)),
        compiler_params=_cparams(1),
    )(ycat, dxb, mod6, wout)


def _ctx_bwd(ctx, hc, kvc, cmod6, norm1, win, rlf, rlb, dscf, dscb, dlg_in):
    lc = ctx.shape[0]

    def body(ctx_ref, hc_ref, kvc_ref, cmod_ref, n1_ref, win_ref, rlf_ref, rlb_ref, dscf_ref, dscb_ref, dlgin_ref,
             dwin_ref, small_ref, dlg_ref):
        k = kvc_ref[:, :RW]
        v = kvc_ref[:, RW:]
        t = lax.broadcasted_iota(jnp.int32, (lc, 1), 0).astype(F32)
        lgf = _log_sigmoid(rlf_ref[...])
        lgb = _log_sigmoid(rlb_ref[...])
        dks, dvs = [], []
        lane = lax.broadcasted_iota(jnp.int32, (1, 128), 1)
        row_f = jnp.zeros((1, 128), F32)
        row_b = jnp.zeros((1, 128), F32)
        for h in range(H):
            hs = slice(h * DH, (h + 1) * DH)
            wf = jnp.exp(lgf[h:h + 1, 0:1] * (lc - 1.0 - t))
            wb = jnp.exp(lgb[h:h + 1, 0:1] * t)
            kh, vhb = k[:, hs], v[:, hs].astype(BF16)
            dsf, dsb = dscf_ref[h].astype(BF16), dscb_ref[h].astype(BF16)
            dkf = wf * _dot_nt(vhb, dsf)
            dkb = wb * _dot_nt(vhb, dsb)
            dvs.append(_dot((kh * wf).astype(BF16), dsf) + _dot((kh * wb).astype(BF16), dsb))
            dks.append((dkf + dkb) * KSCALE)
            lf = jnp.sum((lc - 1.0 - t) * kh * dkf, axis=0, keepdims=True)
            lb = jnp.sum(t * kh * dkb, axis=0, keepdims=True)
            row_f = row_f + jnp.where(lane == h, jnp.sum(lf, axis=1, keepdims=True), 0.0)
            row_b = row_b + jnp.where(lane == h, jnp.sum(lb, axis=1, keepdims=True), 0.0)
        dlg_ref[...] = dlgin_ref[...]
        dlg_ref[0:1, 0:128] += row_f
        dlg_ref[1:2, 0:128] += row_b
        dkv = jnp.concatenate(dks + dvs, axis=1).astype(BF16)
        dhc = _dot(dkv, win_ref[KV_LO:KV_HI, :])
        dwin_ref[...] = _dot_tn(dkv, hc_ref[...])
        x = ctx_ref[...]
        r = lax.rsqrt(jnp.mean(x * x, axis=-1, keepdims=True) + EPS)
        xh = x * r
        small_ref[...] = jnp.zeros_like(small_ref)
        small_ref[0:1, :] = jnp.sum(dhc, axis=0, keepdims=True)
        dhxx = jnp.sum(dhc * xh, axis=0, keepdims=True)
        small_ref[1:2, :] = dhxx * n1_ref[...]
        small_ref[2:3, :] = dhxx * (1.0 + cmod_ref[1:2, :])

    return pl.pallas_call(
        body, name="ctx_bwd",
        out_shape=(jax.ShapeDtypeStruct((2 * RW, D), F32), jax.ShapeDtypeStruct((8, D), F32),
                   jax.ShapeDtypeStruct((8, D), F32)),
        compiler_params=_cparams(),
    )(ctx, hc, kvc, cmod6, norm1, win, rlf, rlb, dscf, dscb, dlg_in)


def _adam_math(w, g, m, v):
    m = ADAM_B1 * m + (1.0 - ADAM_B1) * g
    v = ADAM_B2 * v + (1.0 - ADAM_B2) * (g * g)
    m_hat = m / (1.0 - ADAM_B1 ** ADAM_STEP)
    v_hat = v / (1.0 - ADAM_B2 ** ADAM_STEP)
    delta = -ADAM_LR * (m_hat / (jnp.sqrt(v_hat) + ADAM_EPS) + ADAM_WD * w)
    return delta, m, v


def _place():
    x, y, c = lax.axis_index("x"), lax.axis_index("y"), lax.axis_index("c")
    return x, y, c, 4 * x + 2 * y + c


def _flip(x, y, c, k):
    px = 1 - x if (k >> 2) & 1 else x
    py = 1 - y if (k >> 1) & 1 else y
    pc = 1 - c if k & 1 else c
    return (px, py, pc), 4 * px + 2 * py + pc


def _gather_copy(buf_ref, send_sems, recv_sems, sem0, k, mine):
    x, y, c, me = _place()
    dev, idx = _flip(x, y, c, k)
    blk = me if mine else idx
    return pltpu.make_async_remote_copy(
        src_ref=buf_ref.at[blk], dst_ref=buf_ref.at[blk],
        send_sem=send_sems.at[sem0 + k - 1], recv_sem=recv_sems.at[sem0 + k - 1],
        device_id=dev, device_id_type=MESH)


def _entry_gather(c_row, cctx_row, wmod, bmod, shards):
    n = len(shards)
    ncol = wmod.shape[1]

    def body(*refs):
        c_ref, cctx_ref, wmod_ref, bmod_ref = refs[:4]
        in_refs = refs[4:4 + n]
        mod_ref, cs_ref = refs[4 + n:6 + n]
        out_refs = refs[6 + n:6 + 2 * n]
        cbuf, pbuf = refs[6 + 2 * n:8 + 2 * n]
        cast_refs = refs[8 + 2 * n:8 + 3 * n]
        small_send, small_recv, send_sems, recv_sems, local_sems = refs[8 + 3 * n:]
        x, y, c, me = _place()
        sib = (x, y, 1 - c)
        chips = [(1 - x, y), (x, 1 - y), (1 - x, 1 - y)]

        cbuf[me] = jnp.broadcast_to(c_ref[...], (8, D))
        small = [_gather_copy(cbuf, small_send, small_recv, 0, k, True) for k in range(1, NDEV)]
        for cp in small:
            cp.start()

        def blk(px, py, pc):
            return 4 * px + 2 * py + pc

        def copy(w, k, block, to, src=None):
            dst = out_refs[w].at[block]
            return pltpu.make_async_remote_copy(
                src_ref=dst if src is None else src, dst_ref=dst,
                send_sem=send_sems.at[w, k], recv_sem=recv_sems.at[w, k], device_id=to, device_id_type=MESH)

        first, passed, mine = [], [], []
        for w in range(n):
            cast_refs[w][...] = in_refs[w][...].astype(BF16)
            m = pltpu.make_async_copy(cast_refs[w], out_refs[w].at[me], local_sems.at[w])
            m.start()
            mine.append(m)
            cps = [copy(w, 0, me, sib, src=cast_refs[w])]
            cps += [copy(w, 1 + j, me, (*chip, c), src=cast_refs[w]) for j, chip in enumerate(chips)]
            for cp in cps:
                cp.start()
            first += cps

        for k in range(1, NDEV):
            _gather_copy(cbuf, small_send, small_recv, 0, k, False).wait_recv()
        row = lax.broadcasted_iota(jnp.int32, (16, D), 0)
        call = jnp.where(row == 8, jnp.broadcast_to(cctx_ref[...], (16, D)), 0.0)
        for d in range(NDEV):
            call = jnp.where(row == d, jnp.concatenate([cbuf[d], cbuf[d]], axis=0), call)
        cs = call * _sigmoid(call)
        cs_ref[...] = cs
        pbuf[me] = _dot(cs.astype(BF16), wmod_ref[...].astype(BF16))
        small2 = [_gather_copy(pbuf, small_send, small_recv, NDEV - 1, k, True) for k in range(1, NDEV)]
        for cp in small2:
            cp.start()

        for w in range(n):
            for j, chip in enumerate(chips):
                b = blk(*chip, c)
                copy(w, 1 + j, b, (x, y, c)).wait_recv()
                fw = copy(w, 4 + j, b, sib)
                fw.start()
                passed.append(fw)
        for w in range(n):
            copy(w, 0, blk(x, y, 1 - c), (x, y, c)).wait_recv()
            for j, chip in enumerate(chips):
                copy(w, 4 + j, blk(*chip, 1 - c), (x, y, c)).wait_recv()

        for k in range(1, NDEV):
            _gather_copy(pbuf, small_send, small_recv, NDEV - 1, k, False).wait_recv()
        for d in range(NDEV):
            mod_ref[:, d * ncol:(d + 1) * ncol] = pbuf[d] + bmod_ref[:, d * ncol:(d + 1) * ncol]
        for cp in small + small2 + first + passed:
            cp.wait_send()
        for m in mine:
            m.wait()

    vm = pl.BlockSpec(memory_space=pltpu.VMEM)
    hbm = pl.BlockSpec(memory_space=pltpu.HBM)
    return pl.pallas_call(
        body, name="entry_gather",
        in_specs=[vm] * (4 + n), out_specs=[vm, vm] + [hbm] * n,
        out_shape=(jax.ShapeDtypeStruct((16, 6 * D), F32), jax.ShapeDtypeStruct((16, D), F32))
        + tuple(jax.ShapeDtypeStruct((NDEV,) + s.shape, BF16) for s in shards),
        scratch_shapes=[pltpu.VMEM((NDEV, 8, D), F32), pltpu.VMEM((NDEV, 16, ncol), F32)]
        + [pltpu.VMEM(s.shape, BF16) for s in shards]
        + [pltpu.SemaphoreType.DMA((2 * (NDEV - 1),)), pltpu.SemaphoreType.DMA((2 * (NDEV - 1),)),
           pltpu.SemaphoreType.DMA((n, 7)), pltpu.SemaphoreType.DMA((n, 7)), pltpu.SemaphoreType.DMA((n,))],
        compiler_params=_cparams(),
    )(c_row, cctx_row, wmod, bmod, *shards)


_HBM = pl.BlockSpec(memory_space=pltpu.HBM)
_SEMS = pl.BlockSpec(memory_space=pltpu.SEMAPHORE)
_EFFECT = pltpu.SideEffectType.DATAFLOW_SIDE_EFFECTING


def _exchange_copy(src_refs, land_refs, send_sems, recv_sems, w, k, scatter, landing_slot_is_mine):
    x, y, c, me = _place()
    dev, pidx = _flip(x, y, c, k)
    src = src_refs[w].at[pidx] if scatter else src_refs[w]
    slot = me if landing_slot_is_mine else pidx
    return pltpu.make_async_remote_copy(
        src_ref=src, dst_ref=land_refs[w].at[slot],
        send_sem=send_sems.at[w * (NDEV - 1) + k - 1], recv_sem=recv_sems.at[w * (NDEV - 1) + k - 1],
        device_id=dev, device_id_type=MESH)


def _exchange_start(srcs, *, scatter, name):
    n = len(srcs)
    lands = [lax.empty((NDEV,) + tuple(s.shape[-2:]), s.dtype) for s in srcs]

    def body(*refs):
        src_refs, land_refs = refs[:n], refs[n:2 * n]
        send_sems, recv_sems = refs[2 * n], refs[2 * n + 1]
        token = refs[-1]
        for w in range(n):
            for k in range(1, NDEV):
                _exchange_copy(src_refs, land_refs, send_sems, recv_sems, w, k, scatter, True).start()
        token[...] = jnp.zeros_like(token)

    arrs = list(srcs) + lands
    out_shape = ([pltpu.SemaphoreType.DMA((n * (NDEV - 1),)), pltpu.SemaphoreType.DMA((n * (NDEV - 1),))]
                 + [pltpu.HBM(a.shape, a.dtype) for a in arrs] + [jax.ShapeDtypeStruct((8, 128), F32)])
    res = pl.pallas_call(
        body, name=name, out_shape=tuple(out_shape),
        in_specs=[_HBM] * (2 * n),
        out_specs=tuple([_SEMS, _SEMS] + [_HBM] * (2 * n) + [pl.BlockSpec(memory_space=pltpu.VMEM)]),
        input_output_aliases={i: 2 + i for i in range(2 * n)},
        compiler_params=pltpu.CompilerParams(has_side_effects=_EFFECT),
    )(*[pltpu.with_memory_space_constraint(a, pltpu.HBM) for a in arrs])
    return res[:-1], res[-1]


def _exchange_wait(handles, after, *, scatter, name):
    n = (len(handles) - 2) // 2
    na = len(after)

    def body(*refs):
        src_refs, land_refs = refs[:n], refs[n:2 * n]
        send_sems, recv_sems = refs[2 * n], refs[2 * n + 1]
        for w in range(n):
            for k in range(1, NDEV):
                cp = _exchange_copy(src_refs, land_refs, send_sems, recv_sems, w, k, scatter, False)
                cp.wait_send()
                cp.wait_recv()

    arrs = handles[2:]
    res = pl.pallas_call(
        body, name=name, out_shape=tuple(pltpu.HBM(a.shape, a.dtype) for a in arrs),
        in_specs=[_HBM] * (2 * n) + [_SEMS, _SEMS] + [_HBM] * na,
        out_specs=tuple([_HBM] * (2 * n)),
        input_output_aliases={i: i for i in range(2 * n)},
        compiler_params=pltpu.CompilerParams(has_side_effects=_EFFECT),
    )(*arrs, handles[0], handles[1], *[pltpu.with_memory_space_constraint(a, pltpu.HBM) for a in after])
    return res[:n], res[n:]


_SAME_CORE = (2, 4, 6)


def _gather2_copy(src_ref, land_ref, send_sems, recv_sems, sem, k, block):
    x, y, c, _ = _place()
    dev, _ = _flip(x, y, c, k)
    dst = land_ref.at[block]
    return pltpu.make_async_remote_copy(
        src_ref=dst if src_ref is None else src_ref, dst_ref=dst,
        send_sem=send_sems.at[sem], recv_sem=recv_sems.at[sem], device_id=dev, device_id_type=MESH)


def _split_call(body, name, arrs, n_sems, sems=None, after=(), token=False):
    na = len(arrs)
    out_shape, out_specs = [], []
    if n_sems is not None:
        out_shape += [pltpu.SemaphoreType.DMA((n_sems,)), pltpu.SemaphoreType.DMA((n_sems,))]
        out_specs += [_SEMS, _SEMS]
    first = len(out_shape)
    out_shape += [pltpu.HBM(a.shape, a.dtype) for a in arrs]
    out_specs += [_HBM] * na
    if token:
        out_shape.append(jax.ShapeDtypeStruct((8, 128), F32))
        out_specs.append(pl.BlockSpec(memory_space=pltpu.VMEM))
    in_specs = [_HBM] * na + ([_SEMS, _SEMS] if sems is not None else []) + [_HBM] * len(after)
    args = [pltpu.with_memory_space_constraint(a, pltpu.HBM) for a in arrs] + list(sems or ()) \
        + [pltpu.with_memory_space_constraint(a, pltpu.HBM) for a in after]
    return pl.pallas_call(
        body, name=name, out_shape=tuple(out_shape), in_specs=in_specs, out_specs=tuple(out_specs),
        input_output_aliases={i: first + i for i in range(na)},
        compiler_params=pltpu.CompilerParams(has_side_effects=_EFFECT))(*args)


def _gather2_start(srcs, *, name):
    n = len(srcs)
    lands = [lax.empty((NDEV,) + tuple(s.shape), s.dtype) for s in srcs]

    def body(*refs):
        src_refs, land_refs = refs[:n], refs[n:2 * n]
        send_sems, recv_sems = refs[2 * n], refs[2 * n + 1]
        _, _, _, me = _place()
        for w in range(n):
            for slot, k in enumerate((1,) + _SAME_CORE):
                _gather2_copy(src_refs[w], land_refs[w], send_sems, recv_sems, 4 * w + slot, k, me).start()
        refs[-1][...] = jnp.zeros_like(refs[-1])

    res = _split_call(body, name, list(srcs) + lands, 4 * n, token=True)
    return res[:2], res[2:-1], res[-1]


def _gather2_arrived(sems, arrs, after, *, name):
    n = len(arrs) // 2

    def body(*refs):
        src_refs, land_refs = refs[:n], refs[n:2 * n]
        send_sems, recv_sems = refs[2 * n], refs[2 * n + 1]
        x, y, c, me = _place()
        for w in range(n):
            for slot, k in enumerate((1,) + _SAME_CORE):
                _, pidx = _flip(x, y, c, k)
                _gather2_copy(src_refs[w], land_refs[w], send_sems, recv_sems, 4 * w + slot, k, me).wait_send()
                _gather2_copy(None, land_refs[w], send_sems, recv_sems, 4 * w + slot, k, pidx).wait_recv()

    return _split_call(body, name, arrs, None, sems=sems, after=after)


def _gather2_forward(lands, *, name):
    n = len(lands)

    def body(*refs):
        land_refs = refs[:n]
        send_sems, recv_sems = refs[n], refs[n + 1]
        x, y, c, _ = _place()
        for w in range(n):
            for j, k in enumerate(_SAME_CORE):
                _, pidx = _flip(x, y, c, k)
                _gather2_copy(None, land_refs[w], send_sems, recv_sems, 3 * w + j, 1, pidx).start()
        refs[-1][...] = jnp.zeros_like(refs[-1])

    res = _split_call(body, name, list(lands), 3 * n, token=True)
    return res[:2], res[2:-1], res[-1]


def _gather2_wait(sems, lands, after, *, name):
    n = len(lands)

    def body(*refs):
        land_refs = refs[:n]
        send_sems, recv_sems = refs[n], refs[n + 1]
        x, y, c, _ = _place()
        for w in range(n):
            for j, k in enumerate(_SAME_CORE):
                _, mine = _flip(x, y, c, k)
                _, theirs = _flip(x, y, c, k ^ 1)
                _gather2_copy(None, land_refs[w], send_sems, recv_sems, 3 * w + j, 1, mine).wait_send()
                _gather2_copy(None, land_refs[w], send_sems, recv_sems, 3 * w + j, 1, theirs).wait_recv()

    return _split_call(body, name, list(lands), None, sems=sems, after=after)


def _cast_bf16(arrs, *, name, after=()):
    n = len(arrs)

    def body(*refs):
        for i_ref, o_ref in zip(refs[:n], refs[n + len(after):]):
            o_ref[...] = i_ref[...].astype(BF16)

    return pl.pallas_call(
        body, name=name, out_shape=tuple(jax.ShapeDtypeStruct(a.shape, BF16) for a in arrs),
        in_specs=[pl.BlockSpec(memory_space=pltpu.VMEM)] * n + [pl.BlockSpec(memory_space=pl.ANY)] * len(after),
        compiler_params=_cparams())(*arrs, *after)


def _rs_adam(me_arr, fulls, lands, w, m, v, *, name):
    rows = lands[0].shape[1]
    k = len(fulls)
    nb = next(n for n in (4, 2, 1) if rows % (16 * n) == 0)
    br = rows // nb

    def body(me_ref, *refs):
        own_refs, land_refs = refs[:k], refs[k:2 * k]
        w_ref, m_ref, v_ref, g_ref, d_ref, mo_ref, vo_ref = refs[2 * k:]
        me = me_ref[0]
        parts = []
        for own_ref, land_ref in zip(own_refs, land_refs):
            acc = jnp.zeros(own_ref.shape, F32)
            for p in range(NDEV):
                acc = acc + jnp.where(p == me, own_ref[...], land_ref[p].astype(F32))
            parts.append(acc)
        acc = parts[0] if k == 1 else jnp.concatenate(parts, axis=1)
        g_ref[...] = acc
        d_ref[...], mo_ref[...], vo_ref[...] = _adam_math(w_ref[...], acc, m_ref[...], v_ref[...])

    sh = jax.ShapeDtypeStruct((rows, D), F32)
    blk = pl.BlockSpec((br, D), lambda i, me: (i, 0))
    grid_spec = pltpu.PrefetchScalarGridSpec(
        num_scalar_prefetch=1, grid=(nb,),
        in_specs=[pl.BlockSpec((br, f.shape[1]), lambda i, me: (me[0] * nb + i, 0)) for f in fulls]
        + [pl.BlockSpec((NDEV, br, l.shape[2]), lambda i, me: (0, i, 0)) for l in lands]
        + [blk, blk, blk],
        out_specs=[blk] * 4)
    return pl.pallas_call(
        body, name=name, out_shape=(sh, sh, sh, sh), grid_spec=grid_spec, compiler_params=_cparams(1),
    )(me_arr, *fulls, *lands, w, m, v)


ROW_F, ROW_I, ROW_C, ROW_G1, ROW_LG = 0, 8, 16, 24, 32
PACK_ROWS = 40


def _small_sum(me_arr, pack, pack_land, sgw, sgw_land, wmod):
    ncol = wmod.shape[1]

    def body(me_ref, pack_ref, pland_ref, sgw_ref, sland_ref, wmod_ref, sum_ref, all_ref, sgw_sum_ref, part_ref):
        me = me_ref[0]
        acc = jnp.zeros((PACK_ROWS, D), F32)
        acc_w = jnp.zeros(sgw_ref.shape, F32)
        for p in range(NDEV):
            rows = jnp.where(p == me, pack_ref[...], pland_ref[p])
            all_ref[p] = rows
            acc = acc + rows
            acc_w = acc_w + jnp.where(p == me, sgw_ref[...], sland_ref[p])
        sum_ref[...] = acc
        sgw_sum_ref[...] = acc_w
        zero = jnp.zeros((1, D), F32)
        dcmod = jnp.concatenate([acc[ROW_C:ROW_C + 1], acc[ROW_C + 1:ROW_C + 2], zero, zero, zero, zero], axis=1)
        mine = jnp.zeros((1, ncol), F32)
        for d in range(NDEV):
            mine = mine + jnp.where(d == me, dcmod[:, d * ncol:(d + 1) * ncol], 0.0)
        part_ref[...] = _dot_nt(jnp.broadcast_to(mine, (8, ncol)).astype(BF16), wmod_ref[...].astype(BF16))

    vm = pl.BlockSpec(memory_space=pltpu.VMEM)
    return pl.pallas_call(
        body, name="small_sum",
        out_shape=(jax.ShapeDtypeStruct((PACK_ROWS, D), F32), jax.ShapeDtypeStruct((NDEV, PACK_ROWS, D), F32),
                   jax.ShapeDtypeStruct(sgw.shape, F32), jax.ShapeDtypeStruct((8, D), F32)),
        in_specs=[pl.BlockSpec(memory_space=pltpu.SMEM)] + [vm] * 5,
        compiler_params=_cparams(),
    )(me_arr, pack, pack_land, sgw, sgw_land, wmod)


def _cctx_final(me_arr, part, part_land, cctx_row, m_row, v_row):
    def body(me_ref, part_ref, land_ref, c_ref, m_ref, v_ref, g_ref, d_ref, mo_ref, vo_ref):
        me = me_ref[0]
        tot = jnp.zeros((8, D), F32)
        for p in range(NDEV):
            tot = tot + jnp.where(p == me, part_ref[...], land_ref[p])
        cc = c_ref[...]
        _, dsilu = _silu_parts(cc)
        g = tot[0:1, :] * dsilu
        g_ref[...] = g
        d_ref[...], mo_ref[...], vo_ref[...] = _adam_math(cc, g, m_ref[...], v_ref[...])

    row = jax.ShapeDtypeStruct((1, D), F32)
    vm = pl.BlockSpec(memory_space=pltpu.VMEM)
    return pl.pallas_call(
        body, name="cctx_final", out_shape=(row, row, row, row),
        in_specs=[pl.BlockSpec(memory_space=pltpu.SMEM)] + [vm] * 5,
        compiler_params=_cparams(),
    )(me_arr, part, part_land, cctx_row, m_row, v_row)


def _adam_small(s, sgw_g, params):
    names = ["norm1", "norm2", "norm_f", "sg_gain", "sg_b", "ret_logit_f", "ret_logit_b", "b_mod", "sg_w"]
    flat = [a for n in names for a in params[n]]

    def body(*refs):
        s_ref, sgw_ref = refs[0], refs[1]
        p_refs = refs[2:2 + 3 * len(names)]
        o_refs = refs[2 + 3 * len(names):]
        loss_ref = o_refs[-1]

        def row(r):
            return s_ref[r:r + 1, :]

        grads = {
            "norm1": row(ROW_I + 2) + row(ROW_C + 2),
            "norm2": row(ROW_F + 4),
            "norm_f": row(ROW_F + 0),
            "sg_gain": s_ref[ROW_I + 3:ROW_I + 4, 0:AW],
            "sg_b": s_ref[ROW_I + 4:ROW_I + 8, 0:DH],
            "sg_w": sgw_ref[...],
        }
        for j, n in enumerate(names):
            w_ref, m_ref, v_ref = p_refs[3 * j:3 * j + 3]
            g_ref, d_ref, mo_ref, vo_ref = o_refs[4 * j:4 * j + 4]
            w = w_ref[...]
            if n in ("ret_logit_f", "ret_logit_b"):
                r = ROW_LG + (0 if n == "ret_logit_f" else 1)
                g = s_ref[r:r + 1, 0:H] * _sigmoid(-w)
            elif n == "b_mod":
                rows = [row(ROW_I + 0) + row(ROW_C + 0), row(ROW_I + 1) + row(ROW_C + 1), row(ROW_G1),
                        row(ROW_F + 2), row(ROW_F + 3), row(ROW_F + 1)]
                g = jnp.concatenate(rows, axis=1)
            else:
                g = grads[n]
            g_ref[...] = g
            d_ref[...], mo_ref[...], vo_ref[...] = _adam_math(w, g, m_ref[...], v_ref[...])
        loss_ref[...] = jnp.full((1, 1), 0.5 / D, F32) * jnp.sum(row(ROW_F + 5), axis=1, keepdims=True)

    out_shape = []
    for n in names:
        w = params[n][0]
        out_shape += [jax.ShapeDtypeStruct(w.shape, F32)] * 4
    out_shape.append(jax.ShapeDtypeStruct((1, 1), F32))
    outs = pl.pallas_call(body, name="adam_small", out_shape=tuple(out_shape), compiler_params=_cparams())(
        s, sgw_g, *flat)
    return {n: tuple(outs[4 * j:4 * j + 4]) for j, n in enumerate(names)}, outs[-1]


def _wmod_grad(cs_all, dmod_cols, wmod, m, v):
    ncol = wmod.shape[1]

    def body(cs_ref, dm_ref, w_ref, m_ref, v_ref, g_ref, d_ref, mo_ref, vo_ref):
        g = _dot_tn(cs_ref[...].astype(BF16), dm_ref[...].astype(BF16))
        g_ref[...] = g
        d_ref[...], mo_ref[...], vo_ref[...] = _adam_math(w_ref[...], g, m_ref[...], v_ref[...])

    sh = jax.ShapeDtypeStruct((D, ncol), F32)
    br = D // 4
    blk = pl.BlockSpec((br, ncol), lambda i: (i, 0))
    return pl.pallas_call(
        body, name="wmod_grad", out_shape=(sh, sh, sh, sh), grid=(D // br,),
        in_specs=[pl.BlockSpec((cs_all.shape[0], br), lambda i: (0, i)), _whole(dmod_cols.shape), blk, blk, blk],
        out_specs=[blk] * 4,
        compiler_params=_cparams(1),
    )(cs_all, dmod_cols, wmod, m, v)


def _rope_tables(t_len):
    n_freq = DH // 4
    inv = (ROPE_BASE ** (-np.arange(n_freq, dtype=np.float32) / n_freq)).astype(np.float32)
    tok = np.arange(t_len)
    rows = (tok // GRID_W).astype(np.float32)
    cols = (tok % GRID_W).astype(np.float32)
    ang_r = rows[:, None] * inv[None, :]
    ang_c = cols[:, None] * inv[None, :]
    cos = np.concatenate([np.cos(ang_r)] * 2 + [np.cos(ang_c)] * 2, axis=1).astype(np.float32)
    sin = np.concatenate([-np.sin(ang_r), np.sin(ang_r), -np.sin(ang_c), np.sin(ang_c)], axis=1).astype(np.float32)
    return jnp.asarray(cos), jnp.asarray(sin)


def _local_step(x, tgt, ctx, mod6, cmod6, norm1, norm2, normf, win, wout, ffn_weights, sgw, sgb, sggain, rlf, rlb,
                *, tm_a, tm_b, tm_f, tm_m, tm_r, tm_i, bt_w, after_first_grads=None, small_path=None,
                after_in_half=None):
    t_len = x.shape[0]
    cos, sin = _rope_tables(t_len)
    sgbt = jnp.broadcast_to(sgb[:, :, None], (G, C, 128))
    rlf = jnp.broadcast_to(rlf.reshape(H, 1), (H, 128))
    rlb = jnp.broadcast_to(rlb.reshape(H, 1), (H, 128))
    hc, kvc, scf, scb = _ctx_fwd(ctx, cmod6, norm1, win, rlf, rlb)
    hx, zuv, q, k, v, gfb, of, ya, sst = _fwd_a(x, mod6, norm1, win, sgw, sgbt, sggain, cos, sin, rlf, scf, tm=tm_a)
    if callable(wout):
        wout, tok = wout(hx)
        rlb = rlb + tok
    ob, ycat, x1, rst = _fwd_b(q, k, v, of, gfb, ya, x, mod6, wout, rlb, scb, tm=tm_b)
    wg, wu, wd = ffn_weights(x1) if callable(ffn_weights) else ffn_weights
    dx1, h2, da, db, hm, df, small_f = _ffn(x1, tgt, mod6, norm2, normf, wg, wu, wd, tm=tm_f)
    bt2 = min(2 * bt_w, t_len)
    d_wd, d_wd_b = _tn_matmul(hm, df, bm=DFF // 2, bt=bt2, name="dw_down")
    d_wg, d_wg_b = _tn_matmul(da, h2, bm=DFF // 2, bt=bt2, name="dw_gate")
    d_wu, d_wu_b = _tn_matmul(db, h2, bm=DFF // 2, bt=bt2, name="dw_up")
    dxb, dya, dgfb, dof, dob = _mid_bwd(dx1, of, ob, gfb, mod6, wout, tm=tm_m)
    d_wo, d_wo_b, dg1 = _dw_out(ycat, dxb, mod6, wout, bt=bt2)
    if after_first_grads is not None:
        rlf = rlf + after_first_grads((d_wd_b, d_wg_b, d_wu_b, d_wo_b))
    dqf, dkf, dvf, dqb, dkb, dvb, dscf, dscb, dlg = _ret_bwd(q, k, v, dof, dob, sst, rst, rlf, rlb, tm=tm_r)
    gx, dz, small_i, dsgw = _in_bwd(x, zuv, dya, dqf, dkf, dvf, dqb, dkb, dvb, dgfb, dx1, cos, sin,
                                    mod6, norm1, win, sgw, sgbt, sggain, tm=tm_i)
    dwin_c, small_c, dlg = _ctx_bwd(ctx, hc, kvc, cmod6, norm1, win, rlf, rlb, dscf, dscb, dlg)
    pack = jnp.concatenate([small_f, small_i, small_c, dg1, dlg], axis=0)
    after = small_path(pack, dsgw) if small_path is not None else ()
    halves = []
    for j in range(2):
        halves.append(_tn_matmul(dz, hx, bm=INC // 2, bt=bt2, name="dw_in_%d" % j, init=dwin_c,
                                 init_rows=(KV_LO, KV_HI), after=after, cols=(j, D // 2)))
        if after_in_half is not None:
            after = after_in_half(j, halves[-1][1])
    grads = {"w_in": halves, "w_out": (d_wo, d_wo_b), "w_gate": (d_wg, d_wg_b), "w_up": (d_wu, d_wu_b),
             "w_down": (d_wd, d_wd_b)}
    return gx, grads, pack, dsgw


def kernel(x, c, ctx, c_ctx, w_mod, b_mod, norm1, w_in, sg_gain, sg_w, sg_b, ret_logit_f, ret_logit_b, w_out, norm2, w_gate, w_up, w_down, norm_f, loss_target, m_c_ctx, m_w_mod, m_b_mod, m_norm1, m_w_in, m_sg_gain, m_sg_w, m_sg_b, m_ret_logit_f, m_ret_logit_b, m_w_out, m_norm2, m_w_gate, m_w_up, m_w_down, m_norm_f, v_c_ctx, v_w_mod, v_b_mod, v_norm1, v_w_in, v_sg_gain, v_sg_w, v_sg_b, v_ret_logit_f, v_ret_logit_b, v_w_out, v_norm2, v_w_gate, v_w_up, v_w_down, v_norm_f):
    t_len = x.shape[1]
    tm = min(512, t_len)
    me = 4 * lax.axis_index("x") + 2 * lax.axis_index("y") + lax.axis_index("c")
    tr = lambda a: jnp.transpose(a[0])

    cctx_row = c_ctx.reshape(1, D)
    mod_all, cs_all, g_in = _entry_gather(c, cctx_row, w_mod[0], b_mod, [tr(w_in)])
    mod6 = lax.dynamic_slice(mod_all, (me, 0), (1, 6 * D)).reshape(6, D)
    cmod6 = mod_all[8].reshape(6, D)
    win_t = g_in.reshape(INC, D)

    (out_shard,) = _cast_bf16([w_out[0]], name="cast_out", after=[g_in])
    h_out, tok_out = _exchange_start([out_shard], scatter=False, name="wout_start")
    ffn_shards = _cast_bf16([tr(w_gate), tr(w_up), w_down[0]], name="cast_ffn", after=[h_out[2]])
    sems_ffn, arrs_ffn, tok = _gather2_start(ffn_shards, name="wffn_start")
    mod6 = mod6 + (tok_out[0, 0] + tok[0, 0])
    ffn = {}

    def place_own(own, lands, rows):
        return [lax.dynamic_update_slice(l, o[None], (me, 0, 0)).reshape(rows, D) for o, l in zip(own, lands)]

    def wout(after):
        own, lands = _exchange_wait(h_out, [after], scatter=False, name="wout_wait")
        arrs = _gather2_arrived(sems_ffn, arrs_ffn, [after], name="wffn_arrived")
        ffn["own"] = arrs[:3]
        ffn["sems"], ffn["lands"], tok_fwd = _gather2_forward(arrs[3:], name="wffn_forward")
        return place_own(own, lands, D)[0], tok_fwd[0, 0]

    def ffn_weights(after):
        lands = _gather2_wait(ffn["sems"], ffn["lands"], [after], name="wffn_wait")
        return place_own(ffn["own"], lands, DFF)

    rs, outs = {}, {}

    def after_first_grads(bf):
        rs["first"], tok_first = _exchange_start([b.reshape(NDEV, b.shape[0] // NDEV, D) for b in bf], scatter=True,
                                                 name="rs_first_start")
        return tok_first[0, 0]

    def small_path(pack, dsgw):
        rs["small"], _ = _exchange_start([pack, dsgw.reshape(G * C, C)], scatter=False, name="small_start")
        return [rs["small"][2], rs["small"][3]]

    def after_in_half(j, half_bf16):
        rs["in%d" % j], _ = _exchange_start([half_bf16.reshape(NDEV, INC // NDEV, D // 2)], scatter=True,
                                            name="rs_in%d_start" % j)
        return [rs["in%d" % j][2]]

    gx, grads, pack, dsgw = _local_step(
        x[0], loss_target[0], ctx[0], mod6, cmod6, norm1, norm2[0:1], norm_f.reshape(1, D), win_t, wout, ffn_weights,
        sg_w[0], sg_b[0], sg_gain, ret_logit_f, ret_logit_b,
        tm_a=tm, tm_b=min(1024, t_len), tm_f=min(256, t_len), tm_m=min(1024, t_len), tm_r=min(1024, t_len), tm_i=tm,
        bt_w=min(1024, t_len),
        after_first_grads=after_first_grads, small_path=small_path, after_in_half=after_in_half)

    me_arr = me.reshape(1).astype(jnp.int32)
    (pack_own, sgw_own), (pack_land, sgw_land) = _exchange_wait(rs["small"], [rs["in1"][2]], scatter=False,
                                                               name="small_wait")
    psum_rows, pall, sgw_sum, part = _small_sum(me_arr, pack_own, pack_land, sgw_own, sgw_land, w_mod[0])
    h_cc, _ = _exchange_start([part], scatter=False, name="cctx_start")

    dmod_rows = (ROW_I + 0, ROW_I + 1, ROW_G1, ROW_F + 2, ROW_F + 3, ROW_F + 1)
    dmod_all = jnp.concatenate([pall[:, r, :] for r in dmod_rows], axis=1)
    dcmod = jnp.concatenate([psum_rows[ROW_C + 0:ROW_C + 1], psum_rows[ROW_C + 1:ROW_C + 2],
                             jnp.zeros((1, 4 * D), F32)], axis=1)
    dmod_mat = jnp.concatenate([dmod_all, jnp.pad(dcmod, ((0, 7), (0, 0)))], axis=0)
    ncol = 6 * D // NDEV
    dmod_cols = lax.dynamic_slice(dmod_mat, (0, me * ncol), (16, ncol))
    g_wm, d_wm, m_wm, v_wm = _wmod_grad(cs_all, dmod_cols, w_mod[0], m_w_mod[0], v_w_mod[0])
    outs["w_mod"] = (g_wm[None], d_wm[None], m_wm[None], v_wm[None])
    small_params = {
        "norm1": (norm1, m_norm1, v_norm1), "norm2": (norm2, m_norm2, v_norm2),
        "norm_f": tuple(a.reshape(1, D) for a in (norm_f, m_norm_f, v_norm_f)),
        "sg_gain": (sg_gain, m_sg_gain, v_sg_gain), "sg_b": (sg_b[0], m_sg_b[0], v_sg_b[0]),
        "ret_logit_f": (ret_logit_f, m_ret_logit_f, v_ret_logit_f),
        "ret_logit_b": (ret_logit_b, m_ret_logit_b, v_ret_logit_b),
        "b_mod": (b_mod, m_b_mod, v_b_mod), "sg_w": (sg_w[0], m_sg_w[0], v_sg_w[0])}
    small, loss = _adam_small(psum_rows, sgw_sum.reshape(G, C, C), small_params)
    back = {"norm_f": lambda a: a.reshape(D), "sg_b": lambda a: a[None], "sg_w": lambda a: a[None]}
    for name, vals in small.items():
        outs[name] = tuple(back.get(name, lambda a: a)(a) for a in vals)

    _, lands_first = _exchange_wait(rs["first"], [g_wm], scatter=True, name="rs_first_wait")

    def finish(name, fulls, lands, w, m, v, transposed):
        take = tr if transposed else (lambda a: a[0])
        res = _rs_adam(me_arr, fulls, lands, take(w), take(m), take(v), name="adam_" + name)
        put = (lambda a: jnp.transpose(a)[None]) if transposed else (lambda a: a[None])
        outs[name] = tuple(put(a) for a in res)
        return res[0]

    g_down = finish("w_down", [grads["w_down"][0]], [lands_first[0]], w_down, m_w_down, v_w_down, False)
    g_gate = finish("w_gate", [grads["w_gate"][0]], [lands_first[1]], w_gate, m_w_gate, v_w_gate, True)
    g_up = finish("w_up", [grads["w_up"][0]], [lands_first[2]], w_up, m_w_up, v_w_up, True)
    g_out = finish("w_out", [grads["w_out"][0]], [lands_first[3]], w_out, m_w_out, v_w_out, False)
    done = [g_down, g_gate, g_up, g_out]
    (part_own,), (part_land,) = _exchange_wait(h_cc, done, scatter=False, name="cctx_wait")
    res_cc = _cctx_final(me_arr, part_own, part_land, cctx_row, m_c_ctx.reshape(1, D), v_c_ctx.reshape(1, D))
    outs["c_ctx"] = tuple(a.reshape(D) for a in res_cc)
    lands_in = [_exchange_wait(rs["in%d" % j], done, scatter=True, name="rs_in%d_wait" % j)[1][0] for j in range(2)]
    finish("w_in", [h[0] for h in grads["w_in"]], lands_in, w_in, m_w_in, v_w_in, True)

    order = ["c_ctx", "w_mod", "b_mod", "norm1", "w_in", "sg_gain", "sg_w", "sg_b", "ret_logit_f", "ret_logit_b",
             "w_out", "norm2", "w_gate", "w_up", "w_down", "norm_f"]
    res = [loss.reshape(()), gx[None]]
    for j in range(4):
        res += [outs[name][j] for name in order]
    return tuple(res)
```

```python
import functools
import math

import numpy as np
import jax
import jax.numpy as jnp
from jax import lax
from jax.experimental import pallas as pl
from jax.experimental.pallas import tpu as pltpu

F32 = jnp.float32
BF16 = jnp.bfloat16

D = 1024
AW = 512
RW = 512
H = 4
DH = 128
G = 4
C = 128
CR = 256
DFF = 2816
INC = 3584
Q_LO = 2 * AW
KV_LO, VR_LO, G_LO = Q_LO + RW, Q_LO + 2 * RW, Q_LO + 3 * RW
KV_HI = G_LO
NDEV = 8
EPS = 1e-6
KSCALE = DH ** -0.5
ROPE_BASE = 10000.0
GRID_W = 64

ADAM_LR = 0.001
ADAM_B1 = 0.9
ADAM_B2 = 0.999
ADAM_EPS = 1e-08
ADAM_WD = 0.01
ADAM_STEP = 10

VMEM_LIMIT = 56 * 1024 * 1024
MESH = pl.DeviceIdType.MESH


def _cparams(n_grid=0, **kw):
    sem = ("arbitrary",) * n_grid if n_grid else None
    return pltpu.CompilerParams(dimension_semantics=sem, vmem_limit_bytes=VMEM_LIMIT, **kw)


def _dot(a, b):
    return jnp.dot(a, b, preferred_element_type=F32)


def _dot_nt(a, b):
    return lax.dot_general(a, b, (((1,), (1,)), ((), ())), preferred_element_type=F32)


def _dot_tn(a, b):
    return lax.dot_general(a, b, (((0,), (0,)), ((), ())), preferred_element_type=F32)


def _whole(shape):
    nd = len(shape)
    return pl.BlockSpec(shape, lambda *_: (0,) * nd, pipeline_mode=pl.Buffered(1))


def _acc(shape):
    nd = len(shape)
    return pl.BlockSpec(shape, lambda *_: (0,) * nd)


def _rows(tm, n):
    return pl.BlockSpec((tm, n), lambda i: (i, 0))


def _rows_rev(tm, n, nt):
    return pl.BlockSpec((tm, n), lambda i: (nt - 1 - i, 0))


def _sigmoid(x):
    return 1.0 / (1.0 + jnp.exp(-x))


def _log_sigmoid(x):
    return jnp.minimum(x, 0.0) - jnp.log(1.0 + jnp.exp(-jnp.abs(x)))


_GK0 = math.sqrt(2.0 / math.pi)
_GK1 = 0.044715


def _gelu(x):
    x2 = x * x
    t = jnp.tanh(x * (_GK0 + (_GK0 * _GK1) * x2))
    h = 0.5 + 0.5 * t
    g = x * h
    dg = h + g * (1.0 - h) * ((2.0 * _GK0) + (6.0 * _GK0 * _GK1) * x2)
    return g, dg


def _silu_parts(a):
    s = _sigmoid(a)
    sa = a * s
    return sa, s + sa * (1.0 - s)


def _rope(t, cos, sins):
    n = t.shape[1]
    lane = lax.broadcasted_iota(jnp.int32, t.shape, 1)
    first = (lane & 63) < 32
    partner = jnp.where(first, pltpu.roll(t, n - 32, 1), pltpu.roll(t, 32, 1))
    return t * cos + partner * sins


def _headnorm(o):
    outs, rs = [], []
    for h in range(H):
        oh = o[:, h * DH:(h + 1) * DH]
        r = lax.rsqrt(jnp.mean(oh * oh, axis=-1, keepdims=True) + EPS)
        outs.append(oh * r)
        rs.append(r)
    return jnp.concatenate(outs, axis=1), rs


def _decay_consts(lg, forward):
    ii = lax.broadcasted_iota(jnp.int32, (CR, CR), 0).astype(F32)
    jj = lax.broadcasted_iota(jnp.int32, (CR, CR), 1).astype(F32)
    ic = lax.broadcasted_iota(jnp.int32, (CR, 1), 0).astype(F32)
    if forward:
        diff = ii - jj
        xi = jnp.exp(lg * (ic + 1.0))
        z = jnp.exp(lg * (CR - 1.0 - ic))
    else:
        diff = jj - ii
        xi = jnp.exp(lg * (CR - ic))
        z = jnp.exp(lg * ic)
    dm = jnp.where(diff >= 0.0, jnp.exp(lg * jnp.maximum(diff, 0.0)), 0.0)
    dec = jnp.exp(lg * float(CR))
    return dm, xi, z, dec, ic


def _ctx_fwd(ctx, cmod6, norm1, win, rlf, rlb):
    lc = ctx.shape[0]

    def body(ctx_ref, cmod_ref, n1_ref, win_ref, rlf_ref, rlb_ref, hc_ref, kvc_ref, scf_ref, scb_ref):
        x = ctx_ref[...]
        r = lax.rsqrt(jnp.mean(x * x, axis=-1, keepdims=True) + EPS)
        hc = (x * r) * (n1_ref[...] * (1.0 + cmod_ref[1:2, :])) + cmod_ref[0:1, :]
        hcb = hc.astype(BF16)
        hc_ref[...] = hcb
        kv = _dot_nt(hcb, win_ref[KV_LO:KV_HI, :])
        k = kv[:, :RW] * KSCALE
        v = kv[:, RW:]
        kvc_ref[:, :RW] = k
        kvc_ref[:, RW:] = v
        t = lax.broadcasted_iota(jnp.int32, (lc, 1), 0).astype(F32)
        lgf = _log_sigmoid(rlf_ref[...])
        lgb = _log_sigmoid(rlb_ref[...])
        for h in range(H):
            hs = slice(h * DH, (h + 1) * DH)
            wf = jnp.exp(lgf[h:h + 1, 0:1] * (lc - 1.0 - t))
            wb = jnp.exp(lgb[h:h + 1, 0:1] * t)
            vh = v[:, hs].astype(BF16)
            scf_ref[h] = _dot_tn((k[:, hs] * wf).astype(BF16), vh)
            scb_ref[h] = _dot_tn((k[:, hs] * wb).astype(BF16), vh)

    return pl.pallas_call(
        body, name="ctx_fwd",
        out_shape=(jax.ShapeDtypeStruct((lc, D), BF16), jax.ShapeDtypeStruct((lc, 2 * RW), F32),
                   jax.ShapeDtypeStruct((H, DH, DH), F32), jax.ShapeDtypeStruct((H, DH, DH), F32)),
        compiler_params=_cparams(),
    )(ctx, cmod6, norm1, win, rlf, rlb)


def _sg_forward(u, v, sgw_ref, sgbt_ref, sgg_ref, nc):
    ua, dgu = _gelu(u)
    va, dgv = _gelu(v)
    gain = sgg_ref[...]
    mixed, vn_b, vah_l, rv_l = [], [], [], []
    for g in range(G):
        gs = slice(g * DH, (g + 1) * DH)
        vag = va[:, gs]
        rv = lax.rsqrt(jnp.mean(vag * vag, axis=-1, keepdims=True) + EPS)
        vah = vag * rv
        vn = (vah * gain[:, gs]).astype(BF16)
        wg = sgw_ref[g].astype(BF16)
        bcol = sgbt_ref[g]
        rows = [_dot(wg, vn[c * C:(c + 1) * C, :]) + bcol for c in range(nc)]
        mixed.append(jnp.concatenate(rows, axis=0) if nc > 1 else rows[0])
        vn_b.append(vn)
        vah_l.append(vah)
        rv_l.append(rv)
    mixed = jnp.concatenate(mixed, axis=1)
    return ua * mixed, (ua, dgu, dgv, mixed, vn_b, vah_l, rv_l)


def _fwd_a(x, mod6, norm1, win, sgw, sgbt, sggain, cos, sin, rlf, scf, *, tm):
    t_len = x.shape[0]
    nt, nc, ncr = t_len // tm, tm // C, tm // CR

    def body(x_ref, mod_ref, n1_ref, win_ref, sgw_ref, sgbt_ref, sgg_ref, cos_ref, sin_ref, rlf_ref, scf_ref,
             hx_ref, zuv_ref, q_ref, k_ref, v_ref, gfb_ref, of_ref, ya_ref, sst_ref, s_scr):
        i = pl.program_id(0)

        @pl.when(i == 0)
        def _():
            s_scr[...] = scf_ref[...]

        x = x_ref[...]
        r = lax.rsqrt(jnp.mean(x * x, axis=-1, keepdims=True) + EPS)
        hx = (x * r) * (n1_ref[...] * (1.0 + mod_ref[1:2, :])) + mod_ref[0:1, :]
        hxb = hx.astype(BF16)
        hx_ref[...] = hxb
        z = _dot_nt(hxb, win_ref[...])
        zuv_ref[...] = z[:, :2 * AW].astype(BF16)
        gfb_ref[...] = z[:, G_LO:INC].astype(BF16)
        cos = jnp.tile(cos_ref[...], (1, H))
        sins = jnp.tile(sin_ref[...], (1, H))
        q_ref[...] = _rope(z[:, Q_LO:KV_LO], cos, sins).astype(BF16)
        k_ref[...] = (_rope(z[:, KV_LO:VR_LO], cos, sins) * KSCALE).astype(BF16)
        v_ref[...] = z[:, VR_LO:G_LO].astype(BF16)
        ya, _ = _sg_forward(z[:, :AW], z[:, AW:2 * AW], sgw_ref, sgbt_ref, sgg_ref, nc)
        ya_ref[...] = ya.astype(BF16)

        lg = _log_sigmoid(rlf_ref[...])
        for h in range(H):
            dm, xi, zc, dec, _ = _decay_consts(lg[h:h + 1, 0:1], True)
            hs = slice(h * DH, (h + 1) * DH)
            for c in range(ncr):
                rs = slice(c * CR, (c + 1) * CR)
                qc, kc, vc = q_ref[rs, hs], k_ref[rs, hs], v_ref[rs, hs]
                s = s_scr[h]
                sst_ref[c, h] = s
                a = (_dot_nt(qc, kc) * dm).astype(BF16)
                of_ref[rs, hs] = (_dot(a, vc) + xi * _dot(qc, s.astype(BF16))).astype(BF16)
                kz = (kc.astype(F32) * zc).astype(BF16)
                s_scr[h] = dec * s + _dot_tn(kz, vc)

    n_chunks = t_len // CR
    return pl.pallas_call(
        body, name="fwd_a", grid=(nt,),
        in_specs=[_rows(tm, D), _whole((6, D)), _whole((1, D)), _whole((INC, D)), _whole((G, C, C)),
                  _whole((G, C, 128)), _whole((1, AW)), _rows(tm, DH), _rows(tm, DH), _whole((H, 128)),
                  _whole((H, DH, DH))],
        out_specs=[_rows(tm, D), _rows(tm, 2 * AW), _rows(tm, RW), _rows(tm, RW), _rows(tm, RW),
                   _rows(tm, 2 * RW), _rows(tm, RW), _rows(tm, AW),
                   pl.BlockSpec((ncr, H, DH, DH), lambda i: (i, 0, 0, 0))],
        out_shape=(jax.ShapeDtypeStruct((t_len, D), BF16), jax.ShapeDtypeStruct((t_len, 2 * AW), BF16),
                   jax.ShapeDtypeStruct((t_len, RW), BF16), jax.ShapeDtypeStruct((t_len, RW), BF16),
                   jax.ShapeDtypeStruct((t_len, RW), BF16), jax.ShapeDtypeStruct((t_len, 2 * RW), BF16),
                   jax.ShapeDtypeStruct((t_len, RW), BF16), jax.ShapeDtypeStruct((t_len, AW), BF16),
                   jax.ShapeDtypeStruct((n_chunks, H, DH, DH), F32)),
        scratch_shapes=[pltpu.VMEM((H, DH, DH), F32)],
        compiler_params=_cparams(1),
    )(x, mod6, norm1, win, sgw, sgbt, sggain, cos, sin, rlf, scf)


def _fwd_b(q, k, v, of, gfb, ya, x, mod6, wout, rlb, scb, *, tm):
    t_len = x.shape[0]
    nt, nc = t_len // tm, tm // CR

    def body(q_ref, k_ref, v_ref, of_ref, gfb_ref, ya_ref, x_ref, mod_ref, wout_ref, rlb_ref, scb_ref,
             ob_ref, ycat_ref, x1_ref, rst_ref, r_scr):
        i = pl.program_id(0)

        @pl.when(i == 0)
        def _():
            r_scr[...] = scb_ref[...]

        lg = _log_sigmoid(rlb_ref[...])
        consts = [_decay_consts(lg[h:h + 1, 0:1], False) for h in range(H)]

        def first(c, h):
            dm, _, zc, dec, _ = consts[h]
            hs, rs = slice(h * DH, (h + 1) * DH), slice(c * CR, (c + 1) * CR)
            qc, kc, vc = q_ref[rs, hs], k_ref[rs, hs], v_ref[rs, hs]
            s = r_scr[h]
            rst_ref[c, h] = s
            scores = _dot_nt(qc, kc)
            cross = _dot(qc, s.astype(BF16))
            kz = (kc.astype(F32) * zc).astype(BF16)
            r_scr[h] = dec * s + _dot_tn(kz, vc)
            return (scores * dm).astype(BF16), cross

        def second(c, h, carried):
            a, cross = carried
            hs, rs = slice(h * DH, (h + 1) * DH), slice(c * CR, (c + 1) * CR)
            ob_ref[rs, hs] = (_dot(a, v_ref[rs, hs]) + consts[h][1] * cross).astype(BF16)

        def rows_out(c):
            rs = slice(c * CR, (c + 1) * CR)
            gfb = gfb_ref[rs, :].astype(F32)
            sf, _ = _silu_parts(gfb[:, :RW])
            sb, _ = _silu_parts(gfb[:, RW:])
            hnf, _ = _headnorm(of_ref[rs, :].astype(F32))
            hnb, _ = _headnorm(ob_ref[rs, :].astype(F32))
            yr = sf * hnf + sb * hnb
            ycat = jnp.concatenate([ya_ref[rs, :], yr.astype(BF16)], axis=1)
            ycat_ref[rs, :] = ycat
            x1_ref[rs, :] = x_ref[rs, :] + mod_ref[2:3, :] * _dot(ycat, wout_ref[...])

        pending, waiting_rows = None, None
        for c in reversed(range(nc)):
            for h in range(H):
                carried = first(c, h)
                if pending is not None:
                    second(*pending)
                pending = (c, h, carried)
            if waiting_rows is not None:
                rows_out(waiting_rows)
            waiting_rows = c
        second(*pending)
        rows_out(waiting_rows)

    n_chunks = t_len // CR
    rr = functools.partial(_rows_rev, nt=nt)
    return pl.pallas_call(
        body, name="fwd_b", grid=(nt,),
        in_specs=[rr(tm, RW), rr(tm, RW), rr(tm, RW), rr(tm, RW), rr(tm, 2 * RW), rr(tm, AW), rr(tm, D),
                  _whole((6, D)), _whole((D, D)), _whole((H, 128)), _whole((H, DH, DH))],
        out_specs=[rr(tm, RW), rr(tm, D), rr(tm, D),
                   pl.BlockSpec((nc, H, DH, DH), lambda i: (nt - 1 - i, 0, 0, 0))],
        out_shape=(jax.ShapeDtypeStruct((t_len, RW), BF16), jax.ShapeDtypeStruct((t_len, D), BF16),
                   jax.ShapeDtypeStruct((t_len, D), F32),
                   jax.ShapeDtypeStruct((n_chunks, H, DH, DH), F32)),
        scratch_shapes=[pltpu.VMEM((H, DH, DH), F32)],
        compiler_params=_cparams(1),
    )(q, k, v, of, gfb, ya, x, mod6, wout, rlb, scb)


def _ffn(x1, tgt, mod6, norm2, normf, wg, wu, wd, *, tm):
    t_len = x1.shape[0]
    nt = t_len // tm

    def body(x1_ref, tgt_ref, mod_ref, n2_ref, nf_ref, wg_ref, wu_ref, wd_ref,
             dx1_ref, h2_ref, da_ref, db_ref, hm_ref, df_ref, small_ref):
        i = pl.program_id(0)

        @pl.when(i == 0)
        def _():
            small_ref[...] = jnp.zeros_like(small_ref)

        sh2, sc2, g2 = mod_ref[3:4, :], mod_ref[4:5, :], mod_ref[5:6, :]
        n2, nf = n2_ref[...], nf_ref[...]
        x1 = x1_ref[...]
        r2 = lax.rsqrt(jnp.mean(x1 * x1, axis=-1, keepdims=True) + EPS)
        x1h = x1 * r2
        w2 = n2 * (1.0 + sc2)
        h2b = (x1h * w2 + sh2).astype(BF16)
        h2_ref[...] = h2b
        a = _dot_nt(h2b, wg_ref[...])
        b = _dot_nt(h2b, wu_ref[...])
        sa, dsa = _silu_parts(a)
        hmb = (sa * b).astype(BF16)
        hm_ref[...] = hmb
        f = _dot(hmb, wd_ref[...])
        x2 = x1 + g2 * f
        r3 = lax.rsqrt(jnp.mean(x2 * x2, axis=-1, keepdims=True) + EPS)
        x2h = x2 * r3
        diff = x2h * nf - tgt_ref[...]
        small_ref[5:6, :] += jnp.sum(diff * diff, axis=0, keepdims=True)
        dout = diff * (1.0 / D)
        small_ref[0:1, :] += jnp.sum(dout * x2h, axis=0, keepdims=True)
        dyg = dout * nf
        dx2 = r3 * (dyg - x2h * jnp.mean(dyg * x2h, axis=-1, keepdims=True))
        small_ref[1:2, :] += jnp.sum(dx2 * f, axis=0, keepdims=True)
        dfb = (dx2 * g2).astype(BF16)
        df_ref[...] = dfb
        dhm = _dot_nt(dfb, wd_ref[...])
        dbb = (dhm * sa).astype(BF16)
        dab = (dhm * b * dsa).astype(BF16)
        da_ref[...] = dab
        db_ref[...] = dbb
        dh2 = _dot(dab, wg_ref[...]) + _dot(dbb, wu_ref[...])
        small_ref[2:3, :] += jnp.sum(dh2, axis=0, keepdims=True)
        dhx = dh2 * x1h
        small_ref[3:4, :] += jnp.sum(dhx, axis=0, keepdims=True) * n2
        small_ref[4:5, :] += jnp.sum(dhx, axis=0, keepdims=True) * (1.0 + sc2)
        dyg2 = dh2 * w2
        dx1_ref[...] = dx2 + r2 * (dyg2 - x1h * jnp.mean(dyg2 * x1h, axis=-1, keepdims=True))

    return pl.pallas_call(
        body, name="ffn", grid=(nt,),
        in_specs=[_rows(tm, D), _rows(tm, D), _whole((6, D)), _whole((1, D)), _whole((1, D)),
                  _whole((DFF, D)), _whole((DFF, D)), _whole((DFF, D))],
        out_specs=[_rows(tm, D), _rows(tm, D), _rows(tm, DFF), _rows(tm, DFF), _rows(tm, DFF), _rows(tm, D),
                   _acc((8, D))],
        out_shape=(jax.ShapeDtypeStruct((t_len, D), F32), jax.ShapeDtypeStruct((t_len, D), BF16),
                   jax.ShapeDtypeStruct((t_len, DFF), BF16), jax.ShapeDtypeStruct((t_len, DFF), BF16),
                   jax.ShapeDtypeStruct((t_len, DFF), BF16), jax.ShapeDtypeStruct((t_len, D), BF16),
                   jax.ShapeDtypeStruct((8, D), F32)),
        compiler_params=_cparams(1),
    )(x1, tgt, mod6, norm2, normf, wg, wu, wd)


RING = 3


def _ring_tile(hbm_ref, ring, sems, tm, nt):
    i = pl.program_id(0)

    def copy(step):
        slot = step % RING
        return pltpu.make_async_copy(hbm_ref.at[pl.ds(step * tm, tm), :], ring.at[slot], sems.at[slot])

    @pl.when(i == 0)
    def _():
        for s in range(min(RING - 1, nt)):
            copy(s).start()

    @pl.when(i + RING - 1 < nt)
    def _():
        copy(i + RING - 1).start()

    copy(i).wait()
    return ring.at[i % RING]


def _mid_bwd(dx1, of, ob, gfb, mod6, wout, *, tm):
    t_len = dx1.shape[0]
    nt = t_len // tm
    rc = min(256, tm)

    def body(dx1_hbm, of_ref, ob_ref, gfb_hbm, mod_ref, wout_ref,
             dxb_ref, dya_ref, dgfb_ref, dof_ref, dob_ref, dx_ring, gfb_ring, dx_sems, gfb_sems):
        dx1_ref = _ring_tile(dx1_hbm, dx_ring, dx_sems, tm, nt)
        gfb_ref = _ring_tile(gfb_hbm, gfb_ring, gfb_sems, tm, nt)
        for c in range(tm // rc):
            rows = slice(c * rc, (c + 1) * rc)
            dx1 = dx1_ref[rows, :]
            dxb_ref[rows, :] = dx1.astype(BF16)
            dyb = (dx1 * mod_ref[2:3, :]).astype(BF16)
            dycat = _dot_nt(dyb, wout_ref[...])
            dya_ref[rows, :] = dycat[:, :AW].astype(BF16)
            dyr = dycat[:, AW:]
            gfb = gfb_ref[rows, :].astype(F32)
            for o_ref, do_ref, lo in ((of_ref, dof_ref, 0), (ob_ref, dob_ref, RW)):
                sg, dsg = _silu_parts(gfb[:, lo:lo + RW])
                o = o_ref[rows, :].astype(F32)
                hn, rs = _headnorm(o)
                dgfb_ref[rows, lo:lo + RW] = (dyr * hn * dsg).astype(BF16)
                dhn = dyr * sg
                for h in range(H):
                    hs = slice(h * DH, (h + 1) * DH)
                    oh, dh = hn[:, hs], dhn[:, hs]
                    do_ref[rows, hs] = (rs[h] * (dh - oh * jnp.mean(dh * oh, axis=-1, keepdims=True))).astype(BF16)

    return pl.pallas_call(
        body, name="mid_bwd", grid=(nt,),
        in_specs=[pl.BlockSpec(memory_space=pl.ANY), _rows(tm, RW), _rows(tm, RW), pl.BlockSpec(memory_space=pl.ANY),
                  _whole((6, D)), _whole((D, D))],
        out_specs=[_rows(tm, D), _rows(tm, AW), _rows(tm, 2 * RW), _rows(tm, RW), _rows(tm, RW)],
        out_shape=(jax.ShapeDtypeStruct((t_len, D), BF16), jax.ShapeDtypeStruct((t_len, AW), BF16),
                   jax.ShapeDtypeStruct((t_len, 2 * RW), BF16), jax.ShapeDtypeStruct((t_len, RW), BF16),
                   jax.ShapeDtypeStruct((t_len, RW), BF16)),
        scratch_shapes=[pltpu.VMEM((RING, tm, D), F32), pltpu.VMEM((RING, tm, 2 * RW), BF16),
                        pltpu.SemaphoreType.DMA((RING,)), pltpu.SemaphoreType.DMA((RING,))],
        compiler_params=_cparams(1),
    )(dx1, of, ob, gfb, mod6, wout)


def _ret_bwd_items(q_ref, k_ref, v_ref, do_ref, st_ref, dq_ref, dk_ref, dv_ref, ds_scr, dlg_scr, lg, forward, nc, row0):
    order = list(reversed(range(nc))) if forward else list(range(nc))
    consts = [_decay_consts(lg[h:h + 1, 0:1], forward) for h in range(H)]
    accs = [jnp.zeros((1, DH), F32) for _ in range(H)]

    def first(h, c):
        dm, xi, zc, dec, _ = consts[h]
        hs, rs = slice(h * DH, (h + 1) * DH), slice(c * CR, (c + 1) * CR)
        qc, kc, vc, doc = q_ref[rs, hs], k_ref[rs, hs], v_ref[rs, hs], do_ref[rs, hs]
        s, dsn = st_ref[c, h], ds_scr[h]
        sb, dsnb = s.astype(BF16), dsn.astype(BF16)
        qf, kf = qc.astype(F32), kc.astype(F32)
        a_raw = _dot_nt(qc, kc)
        da_raw = _dot_nt(doc, vc)
        xdo = (xi * doc.astype(F32)).astype(BF16)
        kz = (kf * zc).astype(BF16)
        vz = (vc.astype(F32) * zc).astype(BF16)
        dq_c = _dot_nt(xdo, sb)
        dk_s = _dot_nt(vz, dsnb)
        dv_s = _dot(kz, dsnb)
        ds_scr[h] = _dot_tn((xi * qf).astype(BF16), doc) + dec * dsn
        accs[h] = accs[h] + (float(CR) * dec) * jnp.sum(s * dsn, axis=0, keepdims=True)
        a = (a_raw * dm).astype(BF16)
        da = (da_raw * dm).astype(BF16)
        return a, da, dq_c, dk_s, dv_s

    def second(h, c, carried):
        a, da, dq_c, dk_s, dv_s = carried
        ic = consts[h][4]
        hs, rs = slice(h * DH, (h + 1) * DH), slice(c * CR, (c + 1) * CR)
        qc, kc, doc = q_ref[rs, hs], k_ref[rs, hs], do_ref[rs, hs]
        dq = _dot(da, kc) + dq_c
        dk = _dot_tn(da, qc) + dk_s
        dq_ref[rs, hs] = dq.astype(BF16)
        dk_ref[rs, hs] = dk.astype(BF16)
        dv_ref[rs, hs] = (_dot_tn(a, doc) + dv_s).astype(BF16)
        qf, kf = qc.astype(F32), kc.astype(F32)
        both = jnp.sum((ic if forward else -ic) * (qf * dq - kf * dk), axis=0, keepdims=True)
        cross = jnp.sum(qf * dq_c, axis=0, keepdims=True)
        if forward:
            accs[h] = accs[h] + both + cross + (CR - 1.0) * jnp.sum(kf * dk_s, axis=0, keepdims=True)
        else:
            accs[h] = accs[h] + both + float(CR) * cross

    def finish():
        for h in range(H):
            dlg_scr[row0 + h:row0 + h + 1, :] += accs[h]

    items = [[(functools.partial(first, h, c), functools.partial(second, h, c)) for h in range(H)] for c in order]
    return items, finish


def _ret_bwd_run(dirs):
    nc = len(dirs[0][0])
    flat = [it for j in range(nc) for items, _ in dirs for it in items[j]]
    pending = None
    for first, second in flat:
        carried = first()
        if pending is not None:
            pending[0](pending[1])
        pending = (second, carried)
    pending[0](pending[1])
    for _, finish in dirs:
        finish()


def _ret_bwd(q, k, v, dof, dob, sst, rst, rlf, rlb, *, tm):
    t_len = q.shape[0]
    nt, nc = t_len // tm, tm // CR

    def body(qf_ref, kf_ref, vf_ref, dof_ref, sst_ref, qb_ref, kb_ref, vb_ref, dob_ref, rst_ref, rlf_ref, rlb_ref,
             dqf_ref, dkf_ref, dvf_ref, dqb_ref, dkb_ref, dvb_ref, dscf_ref, dscb_ref, dlg_ref,
             dsf_scr, dsb_scr, dlg_scr):
        i = pl.program_id(0)

        @pl.when(i == 0)
        def _():
            dsf_scr[...] = jnp.zeros_like(dsf_scr)
            dsb_scr[...] = jnp.zeros_like(dsb_scr)
            dlg_scr[...] = jnp.zeros_like(dlg_scr)

        lgf = _log_sigmoid(rlf_ref[...])
        lgb = _log_sigmoid(rlb_ref[...])
        _ret_bwd_run([
            _ret_bwd_items(qf_ref, kf_ref, vf_ref, dof_ref, sst_ref, dqf_ref, dkf_ref, dvf_ref, dsf_scr, dlg_scr,
                           lgf, True, nc, 0),
            _ret_bwd_items(qb_ref, kb_ref, vb_ref, dob_ref, rst_ref, dqb_ref, dkb_ref, dvb_ref, dsb_scr, dlg_scr,
                           lgb, False, nc, H)])

        @pl.when(i == nt - 1)
        def _():
            dscf_ref[...] = dsf_scr[...]
            dscb_ref[...] = dsb_scr[...]
            tot = jnp.broadcast_to(jnp.sum(dlg_scr[...], axis=1, keepdims=True), (8, 128))
            ri = lax.broadcasted_iota(jnp.int32, (8, 128), 0)
            li = lax.broadcasted_iota(jnp.int32, (8, 128), 1)
            dlg_ref[...] = jnp.zeros_like(dlg_ref)
            dlg_ref[0:1, 0:128] = jnp.sum(jnp.where(ri == li, tot, 0.0), axis=0, keepdims=True)
            dlg_ref[1:2, 0:128] = jnp.sum(jnp.where(ri - H == li, tot, 0.0), axis=0, keepdims=True)

    rr = functools.partial(_rows_rev, nt=nt)
    st_f = pl.BlockSpec((nc, H, DH, DH), lambda i: (nt - 1 - i, 0, 0, 0))
    st_b = pl.BlockSpec((nc, H, DH, DH), lambda i: (i, 0, 0, 0))
    tok = jax.ShapeDtypeStruct((t_len, RW), BF16)
    st = jax.ShapeDtypeStruct((H, DH, DH), F32)
    return pl.pallas_call(
        body, name="ret_bwd", grid=(nt,),
        in_specs=[rr(tm, RW), rr(tm, RW), rr(tm, RW), rr(tm, RW), st_f,
                  _rows(tm, RW), _rows(tm, RW), _rows(tm, RW), _rows(tm, RW), st_b,
                  _whole((H, 128)), _whole((H, 128))],
        out_specs=[rr(tm, RW), rr(tm, RW), rr(tm, RW), _rows(tm, RW), _rows(tm, RW), _rows(tm, RW),
                   _acc((H, DH, DH)), _acc((H, DH, DH)), _acc((8, D))],
        out_shape=(tok, tok, tok, tok, tok, tok, st, st, jax.ShapeDtypeStruct((8, D), F32)),
        scratch_shapes=[pltpu.VMEM((H, DH, DH), F32), pltpu.VMEM((H, DH, DH), F32), pltpu.VMEM((8, 128), F32)],
        compiler_params=_cparams(1),
    )(q, k, v, dof, sst, q, k, v, dob, rst, rlf, rlb)


def _in_bwd(x, zuv, dya, dqf, dkf, dvf, dqb, dkb, dvb, dgfb, dx1, cos, sin, mod6, norm1, win, sgw, sgbt, sggain, *, tm):
    t_len = x.shape[0]
    nt, nc = t_len // tm, tm // C

    def body(x_ref, zuv_ref, dya_ref, dqf_ref, dkf_ref, dvf_ref, dqb_ref, dkb_ref, dvb_ref, dgfb_ref, dx1_ref,
             cos_ref, sin_ref, mod_ref, n1_ref, win_ref, sgw_ref, sgbt_ref, sgg_ref,
             gx_ref, dz_ref, small_ref, dsgw_ref, dsgbt_ref):
        i = pl.program_id(0)

        @pl.when(i == 0)
        def _():
            small_ref[...] = jnp.zeros_like(small_ref)
            dsgw_ref[...] = jnp.zeros_like(dsgw_ref)
            dsgbt_ref[...] = jnp.zeros_like(dsgbt_ref)

        zuv = zuv_ref[...].astype(F32)
        _, (ua, dgu, dgv, mixed, vn_b, vah_l, rv_l) = _sg_forward(zuv[:, :AW], zuv[:, AW:], sgw_ref, sgbt_ref, sgg_ref, nc)
        dya = dya_ref[...].astype(F32)
        dz_ref[:, 0:AW] = (dya * mixed * dgu).astype(BF16)
        dmixed = dya * ua
        gain = sgg_ref[...]
        for g in range(G):
            gs = slice(g * DH, (g + 1) * DH)
            dmg = dmixed[:, gs]
            dmb = dmg.astype(BF16)
            wgt = sgw_ref[g].astype(BF16)
            dsw = jnp.zeros((C, C), F32)
            dsb = jnp.zeros((C, DH), F32)
            rows = []
            for c in range(nc):
                rs = slice(c * C, (c + 1) * C)
                dsw = dsw + _dot_nt(dmb[rs, :], vn_b[g][rs, :])
                dsb = dsb + dmg[rs, :]
                rows.append(_dot_tn(wgt, dmb[rs, :]))
            dsgw_ref[g] += dsw
            dsgbt_ref[g] += dsb
            dvn = jnp.concatenate(rows, axis=0) if nc > 1 else rows[0]
            vah, rv = vah_l[g], rv_l[g]
            small_ref[3:4, gs] += jnp.sum(dvn * vah, axis=0, keepdims=True)
            dyg = dvn * gain[:, gs]
            dva = rv * (dyg - vah * jnp.mean(dyg * vah, axis=-1, keepdims=True))
            dz_ref[:, AW + g * DH:AW + (g + 1) * DH] = (dva * dgv[:, gs]).astype(BF16)

        cos = jnp.tile(cos_ref[...], (1, H))
        nsins = -jnp.tile(sin_ref[...], (1, H))
        def both(f_ref, b_ref):
            return f_ref[...].astype(F32) + b_ref[...].astype(F32)

        dz_ref[:, Q_LO:KV_LO] = _rope(both(dqf_ref, dqb_ref), cos, nsins).astype(BF16)
        dz_ref[:, KV_LO:VR_LO] = (_rope(both(dkf_ref, dkb_ref), cos, nsins) * KSCALE).astype(BF16)
        dz_ref[:, VR_LO:G_LO] = both(dvf_ref, dvb_ref).astype(BF16)
        dz_ref[:, G_LO:INC] = dgfb_ref[...]

        dhx = _dot(dz_ref[...], win_ref[...])
        x = x_ref[...]
        r = lax.rsqrt(jnp.mean(x * x, axis=-1, keepdims=True) + EPS)
        xh = x * r
        n1, sc1 = n1_ref[...], mod_ref[1:2, :]
        small_ref[0:1, :] += jnp.sum(dhx, axis=0, keepdims=True)
        dhxx = jnp.sum(dhx * xh, axis=0, keepdims=True)
        small_ref[1:2, :] += dhxx * n1
        small_ref[2:3, :] += dhxx * (1.0 + sc1)
        dyg = dhx * (n1 * (1.0 + sc1))
        gx_ref[...] = dx1_ref[...] + r * (dyg - xh * jnp.mean(dyg * xh, axis=-1, keepdims=True))

        @pl.when(i == nt - 1)
        def _():
            ri = lax.broadcasted_iota(jnp.int32, (C, DH), 0)
            li = lax.broadcasted_iota(jnp.int32, (C, DH), 1)
            for g in range(G):
                tot = jnp.broadcast_to(jnp.sum(dsgbt_ref[g], axis=1, keepdims=True), (C, DH))
                small_ref[4 + g:5 + g, 0:DH] = jnp.sum(jnp.where(ri == li, tot, 0.0), axis=0, keepdims=True)

    tok = functools.partial(_rows, tm)
    return pl.pallas_call(
        body, name="in_bwd", grid=(nt,),
        in_specs=[tok(D), tok(2 * AW), tok(AW), tok(RW), tok(RW), tok(RW), tok(RW), tok(RW), tok(RW),
                  tok(2 * RW), tok(D), tok(DH), tok(DH), _whole((6, D)), _whole((1, D)), _whole((INC, D)),
                  _whole((G, C, C)), _whole((G, C, 128)), _whole((1, AW))],
        out_specs=[tok(D), tok(INC), _acc((8, D)), _acc((G, C, C))],
        out_shape=(jax.ShapeDtypeStruct((t_len, D), F32), jax.ShapeDtypeStruct((t_len, INC), BF16),
                   jax.ShapeDtypeStruct((8, D), F32), jax.ShapeDtypeStruct((G, C, C), F32)),
        scratch_shapes=[pltpu.VMEM((G, C, 128), F32)],
        compiler_params=_cparams(1),
    )(x, zuv, dya, dqf, dkf, dvf, dqb, dkb, dvb, dgfb, dx1, cos, sin, mod6, norm1, win, sgw, sgbt, sggain)


def _tn_matmul(a, b, *, bm, bt, name, init=None, init_rows=None, after=(), cols=None):
    t_len, m = a.shape
    cj, n = (0, b.shape[1]) if cols is None else cols
    ni, ntt = m // bm, t_len // bt
    has_init = init is not None

    def body(*refs):
        o_ref, ob_ref = refs[-2:]
        a_ref, b_ref = refs[:2]
        init_ref = refs[2] if has_init else None
        i, t = pl.program_id(0), pl.program_id(1)

        @pl.when(t == 0)
        def _():
            o_ref[...] = jnp.zeros_like(o_ref)

        if has_init:
            for ib in range(ni):
                lo, hi = max(init_rows[0], ib * bm), min(init_rows[1], (ib + 1) * bm)
                if lo < hi:
                    @pl.when(jnp.logical_and(t == 0, i == ib))
                    def _(lo=lo, hi=hi, ib=ib):
                        o_ref[lo - ib * bm:hi - ib * bm, :] = init_ref[lo - init_rows[0]:hi - init_rows[0], :]

        o_ref[...] += _dot_tn(a_ref[...], b_ref[...])

        @pl.when(t == ntt - 1)
        def _():
            ob_ref[...] = o_ref[...].astype(BF16)

    in_specs = [pl.BlockSpec((bt, bm), lambda i, t: (t, i)), pl.BlockSpec((bt, n), lambda i, t: (t, cj))]
    args = [a, b]
    if has_init:
        in_specs.append(pl.BlockSpec((init.shape[0], n), lambda i, t: (0, cj), pipeline_mode=pl.Buffered(1)))
        args.append(init)
    for arr in after:
        in_specs.append(pl.BlockSpec(memory_space=pl.ANY))
        args.append(arr)
    out_spec = pl.BlockSpec((bm, n), lambda i, t: (i, 0))
    return pl.pallas_call(
        body, name=name, grid=(ni, ntt),
        in_specs=in_specs,
        out_specs=[out_spec, out_spec],
        out_shape=(jax.ShapeDtypeStruct((m, n), F32), jax.ShapeDtypeStruct((m, n), BF16)),
        compiler_params=_cparams(2),
    )(*args)


def _dw_out(ycat, dxb, mod6, wout, *, bt):
    t_len = ycat.shape[0]
    ntt = t_len // bt

    def body(a_ref, b_ref, mod_ref, wout_ref, o_ref, ob_ref, dg1_ref):
        t = pl.program_id(0)

        @pl.when(t == 0)
        def _():
            o_ref[...] = jnp.zeros_like(o_ref)

        o_ref[...] += _dot_tn(a_ref[...], b_ref[...])

        @pl.when(t == ntt - 1)
        def _():
            m = o_ref[...]
            dg1_ref[...] = jnp.zeros_like(dg1_ref)
            dg1_ref[0:1, :] = jnp.sum(m * wout_ref[...].astype(F32), axis=0, keepdims=True)
            g = m * mod_ref[2:3, :]
            o_ref[...] = g
            ob_ref[...] = g.astype(BF16)

    return pl.pallas_call(
        body, name="dw_out", grid=(ntt,),
        in_specs=[pl.BlockSpec((bt, D), lambda t: (t, 0)), pl.BlockSpec((bt, D), lambda t: (t, 0)),
                  _whole((6, D)), _whole((D, D))],
        out_specs=[_acc((D, D)), _acc((D, D)), _acc((8, D))],
        out_shape=(jax.ShapeDtypeStruct((D, D), F32), jax.ShapeDtypeStruct((D, D), BF16),
                   jax.ShapeDtypeStruct((8, D), F32)),
        compiler_params=_cparams(1),
    )(ycat, dxb, mod6, wout)


def _ctx_bwd(ctx, hc, kvc, cmod6, norm1, win, rlf, rlb, dscf, dscb, dlg_in):
    lc = ctx.shape[0]

    def body(ctx_ref, hc_ref, kvc_ref, cmod_ref, n1_ref, win_ref, rlf_ref, rlb_ref, dscf_ref, dscb_ref, dlgin_ref,
             dwin_ref, small_ref, dlg_ref):
        k = kvc_ref[:, :RW]
        v = kvc_ref[:, RW:]
        t = lax.broadcasted_iota(jnp.int32, (lc, 1), 0).astype(F32)
        lgf = _log_sigmoid(rlf_ref[...])
        lgb = _log_sigmoid(rlb_ref[...])
        dks, dvs = [], []
        lane = lax.broadcasted_iota(jnp.int32, (1, 128), 1)
        row_f = jnp.zeros((1, 128), F32)
        row_b = jnp.zeros((1, 128), F32)
        for h in range(H):
            hs = slice(h * DH, (h + 1) * DH)
            wf = jnp.exp(lgf[h:h + 1, 0:1] * (lc - 1.0 - t))
            wb = jnp.exp(lgb[h:h + 1, 0:1] * t)
            kh, vhb = k[:, hs], v[:, hs].astype(BF16)
            dsf, dsb = dscf_ref[h].astype(BF16), dscb_ref[h].astype(BF16)
            dkf = wf * _dot_nt(vhb, dsf)
            dkb = wb * _dot_nt(vhb, dsb)
            dvs.append(_dot((kh * wf).astype(BF16), dsf) + _dot((kh * wb).astype(BF16), dsb))
            dks.append((dkf + dkb) * KSCALE)
            lf = jnp.sum((lc - 1.0 - t) * kh * dkf, axis=0, keepdims=True)
            lb = jnp.sum(t * kh * dkb, axis=0, keepdims=True)
            row_f = row_f + jnp.where(lane == h, jnp.sum(lf, axis=1, keepdims=True), 0.0)
            row_b = row_b + jnp.where(lane == h, jnp.sum(lb, axis=1, keepdims=True), 0.0)
        dlg_ref[...] = dlgin_ref[...]
        dlg_ref[0:1, 0:128] += row_f
        dlg_ref[1:2, 0:128] += row_b
        dkv = jnp.concatenate(dks + dvs, axis=1).astype(BF16)
        dhc = _dot(dkv, win_ref[KV_LO:KV_HI, :])
        dwin_ref[...] = _dot_tn(dkv, hc_ref[...])
        x = ctx_ref[...]
        r = lax.rsqrt(jnp.mean(x * x, axis=-1, keepdims=True) + EPS)
        xh = x * r
        small_ref[...] = jnp.zeros_like(small_ref)
        small_ref[0:1, :] = jnp.sum(dhc, axis=0, keepdims=True)
        dhxx = jnp.sum(dhc * xh, axis=0, keepdims=True)
        small_ref[1:2, :] = dhxx * n1_ref[...]
        small_ref[2:3, :] = dhxx * (1.0 + cmod_ref[1:2, :])

    return pl.pallas_call(
        body, name="ctx_bwd",
        out_shape=(jax.ShapeDtypeStruct((2 * RW, D), F32), jax.ShapeDtypeStruct((8, D), F32),
                   jax.ShapeDtypeStruct((8, D), F32)),
        compiler_params=_cparams(),
    )(ctx, hc, kvc, cmod6, norm1, win, rlf, rlb, dscf, dscb, dlg_in)


def _adam_math(w, g, m, v):
    m = ADAM_B1 * m + (1.0 - ADAM_B1) * g
    v = ADAM_B2 * v + (1.0 - ADAM_B2) * (g * g)
    m_hat = m / (1.0 - ADAM_B1 ** ADAM_STEP)
    v_hat = v / (1.0 - ADAM_B2 ** ADAM_STEP)
    delta = -ADAM_LR * (m_hat / (jnp.sqrt(v_hat) + ADAM_EPS) + ADAM_WD * w)
    return delta, m, v


def _place():
    x, y, c = lax.axis_index("x"), lax.axis_index("y"), lax.axis_index("c")
    return x, y, c, 4 * x + 2 * y + c


def _flip(x, y, c, k):
    px = 1 - x if (k >> 2) & 1 else x
    py = 1 - y if (k >> 1) & 1 else y
    pc = 1 - c if k & 1 else c
    return (px, py, pc), 4 * px + 2 * py + pc


def _gather_copy(buf_ref, send_sems, recv_sems, sem0, k, mine):
    x, y, c, me = _place()
    dev, idx = _flip(x, y, c, k)
    blk = me if mine else idx
    return pltpu.make_async_remote_copy(
        src_ref=buf_ref.at[blk], dst_ref=buf_ref.at[blk],
        send_sem=send_sems.at[sem0 + k - 1], recv_sem=recv_sems.at[sem0 + k - 1],
        device_id=dev, device_id_type=MESH)


def _entry_gather(c_row, cctx_row, wmod, bmod, shards):
    n = len(shards)
    ncol = wmod.shape[1]

    def body(*refs):
        c_ref, cctx_ref, wmod_ref, bmod_ref = refs[:4]
        in_refs = refs[4:4 + n]
        mod_ref, cs_ref = refs[4 + n:6 + n]
        out_refs = refs[6 + n:6 + 2 * n]
        cbuf, pbuf = refs[6 + 2 * n:8 + 2 * n]
        cast_refs = refs[8 + 2 * n:8 + 3 * n]
        small_send, small_recv, send_sems, recv_sems, local_sems = refs[8 + 3 * n:]
        x, y, c, me = _place()
        sib = (x, y, 1 - c)
        chips = [(1 - x, y), (x, 1 - y), (1 - x, 1 - y)]

        cbuf[me] = jnp.broadcast_to(c_ref[...], (8, D))
        small = [_gather_copy(cbuf, small_send, small_recv, 0, k, True) for k in range(1, NDEV)]
        for cp in small:
            cp.start()

        def blk(px, py, pc):
            return 4 * px + 2 * py + pc

        def copy(w, k, block, to, src=None):
            dst = out_refs[w].at[block]
            return pltpu.make_async_remote_copy(
                src_ref=dst if src is None else src, dst_ref=dst,
                send_sem=send_sems.at[w, k], recv_sem=recv_sems.at[w, k], device_id=to, device_id_type=MESH)

        first, passed, mine = [], [], []
        for w in range(n):
            cast_refs[w][...] = in_refs[w][...].astype(BF16)
            m = pltpu.make_async_copy(cast_refs[w], out_refs[w].at[me], local_sems.at[w])
            m.start()
            mine.append(m)
            cps = [copy(w, 0, me, sib, src=cast_refs[w])]
            cps += [copy(w, 1 + j, me, (*chip, c), src=cast_refs[w]) for j, chip in enumerate(chips)]
            for cp in cps:
                cp.start()
            first += cps

        for k in range(1, NDEV):
            _gather_copy(cbuf, small_send, small_recv, 0, k, False).wait_recv()
        row = lax.broadcasted_iota(jnp.int32, (16, D), 0)
        call = jnp.where(row == 8, jnp.broadcast_to(cctx_ref[...], (16, D)), 0.0)
        for d in range(NDEV):
            call = jnp.where(row == d, jnp.concatenate([cbuf[d], cbuf[d]], axis=0), call)
        cs = call * _sigmoid(call)
        cs_ref[...] = cs
        pbuf[me] = _dot(cs.astype(BF16), wmod_ref[...].astype(BF16))
        small2 = [_gather_copy(pbuf, small_send, small_recv, NDEV - 1, k, True) for k in range(1, NDEV)]
        for cp in small2:
            cp.start()

        for w in range(n):
            for j, chip in enumerate(chips):
                b = blk(*chip, c)
                copy(w, 1 + j, b, (x, y, c)).wait_recv()
                fw = copy(w, 4 + j, b, sib)
                fw.start()
                passed.append(fw)
        for w in range(n):
            copy(w, 0, blk(x, y, 1 - c), (x, y, c)).wait_recv()
            for j, chip in enumerate(chips):
                copy(w, 4 + j, blk(*chip, 1 - c), (x, y, c)).wait_recv()

        for k in range(1, NDEV):
            _gather_copy(pbuf, small_send, small_recv, NDEV - 1, k, False).wait_recv()
        for d in range(NDEV):
            mod_ref[:, d * ncol:(d + 1) * ncol] = pbuf[d] + bmod_ref[:, d * ncol:(d + 1) * ncol]
        for cp in small + small2 + first + passed:
            cp.wait_send()
        for m in mine:
            m.wait()

    vm = pl.BlockSpec(memory_space=pltpu.VMEM)
    hbm = pl.BlockSpec(memory_space=pltpu.HBM)
    return pl.pallas_call(
        body, name="entry_gather",
        in_specs=[vm] * (4 + n), out_specs=[vm, vm] + [hbm] * n,
        out_shape=(jax.ShapeDtypeStruct((16, 6 * D), F32), jax.ShapeDtypeStruct((16, D), F32))
        + tuple(jax.ShapeDtypeStruct((NDEV,) + s.shape, BF16) for s in shards),
        scratch_shapes=[pltpu.VMEM((NDEV, 8, D), F32), pltpu.VMEM((NDEV, 16, ncol), F32)]
        + [pltpu.VMEM(s.shape, BF16) for s in shards]
        + [pltpu.SemaphoreType.DMA((2 * (NDEV - 1),)), pltpu.SemaphoreType.DMA((2 * (NDEV - 1),)),
           pltpu.SemaphoreType.DMA((n, 7)), pltpu.SemaphoreType.DMA((n, 7)), pltpu.SemaphoreType.DMA((n,))],
        compiler_params=_cparams(),
    )(c_row, cctx_row, wmod, bmod, *shards)


_HBM = pl.BlockSpec(memory_space=pltpu.HBM)
_SEMS = pl.BlockSpec(memory_space=pltpu.SEMAPHORE)
_EFFECT = pltpu.SideEffectType.DATAFLOW_SIDE_EFFECTING


def _exchange_copy(src_refs, land_refs, send_sems, recv_sems, w, k, scatter, landing_slot_is_mine):
    x, y, c, me = _place()
    dev, pidx = _flip(x, y, c, k)
    src = src_refs[w].at[pidx] if scatter else src_refs[w]
    slot = me if landing_slot_is_mine else pidx
    return pltpu.make_async_remote_copy(
        src_ref=src, dst_ref=land_refs[w].at[slot],
        send_sem=send_sems.at[w * (NDEV - 1) + k - 1], recv_sem=recv_sems.at[w * (NDEV - 1) + k - 1],
        device_id=dev, device_id_type=MESH)


def _exchange_start(srcs, *, scatter, name):
    n = len(srcs)
    lands = [lax.empty((NDEV,) + tuple(s.shape[-2:]), s.dtype) for s in srcs]

    def body(*refs):
        src_refs, land_refs = refs[:n], refs[n:2 * n]
        send_sems, recv_sems = refs[2 * n], refs[2 * n + 1]
        token = refs[-1]
        for w in range(n):
            for k in range(1, NDEV):
                _exchange_copy(src_refs, land_refs, send_sems, recv_sems, w, k, scatter, True).start()
        token[...] = jnp.zeros_like(token)

    arrs = list(srcs) + lands
    out_shape = ([pltpu.SemaphoreType.DMA((n * (NDEV - 1),)), pltpu.SemaphoreType.DMA((n * (NDEV - 1),))]
                 + [pltpu.HBM(a.shape, a.dtype) for a in arrs] + [jax.ShapeDtypeStruct((8, 128), F32)])
    res = pl.pallas_call(
        body, name=name, out_shape=tuple(out_shape),
        in_specs=[_HBM] * (2 * n),
        out_specs=tuple([_SEMS, _SEMS] + [_HBM] * (2 * n) + [pl.BlockSpec(memory_space=pltpu.VMEM)]),
        input_output_aliases={i: 2 + i for i in range(2 * n)},
        compiler_params=pltpu.CompilerParams(has_side_effects=_EFFECT),
    )(*[pltpu.with_memory_space_constraint(a, pltpu.HBM) for a in arrs])
    return res[:-1], res[-1]


def _exchange_wait(handles, after, *, scatter, name):
    n = (len(handles) - 2) // 2
    na = len(after)

    def body(*refs):
        src_refs, land_refs = refs[:n], refs[n:2 * n]
        send_sems, recv_sems = refs[2 * n], refs[2 * n + 1]
        for w in range(n):
            for k in range(1, NDEV):
                cp = _exchange_copy(src_refs, land_refs, send_sems, recv_sems, w, k, scatter, False)
                cp.wait_send()
                cp.wait_recv()

    arrs = handles[2:]
    res = pl.pallas_call(
        body, name=name, out_shape=tuple(pltpu.HBM(a.shape, a.dtype) for a in arrs),
        in_specs=[_HBM] * (2 * n) + [_SEMS, _SEMS] + [_HBM] * na,
        out_specs=tuple([_HBM] * (2 * n)),
        input_output_aliases={i: i for i in range(2 * n)},
        compiler_params=pltpu.CompilerParams(has_side_effects=_EFFECT),
    )(*arrs, handles[0], handles[1], *[pltpu.with_memory_space_constraint(a, pltpu.HBM) for a in after])
    return res[:n], res[n:]


_SAME_CORE = (2, 4, 6)


def _gather2_copy(src_ref, land_ref, send_sems, recv_sems, sem, k, block):
    x, y, c, _ = _place()
    dev, _ = _flip(x, y, c, k)
    dst = land_ref.at[block]
    return pltpu.make_async_remote_copy(
        src_ref=dst if src_ref is None else src_ref, dst_ref=dst,
        send_sem=send_sems.at[sem], recv_sem=recv_sems.at[sem], device_id=dev, device_id_type=MESH)


def _split_call(body, name, arrs, n_sems, sems=None, after=(), token=False):
    na = len(arrs)
    out_shape, out_specs = [], []
    if n_sems is not None:
        out_shape += [pltpu.SemaphoreType.DMA((n_sems,)), pltpu.SemaphoreType.DMA((n_sems,))]
        out_specs += [_SEMS, _SEMS]
    first = len(out_shape)
    out_shape += [pltpu.HBM(a.shape, a.dtype) for a in arrs]
    out_specs += [_HBM] * na
    if token:
        out_shape.append(jax.ShapeDtypeStruct((8, 128), F32))
        out_specs.append(pl.BlockSpec(memory_space=pltpu.VMEM))
    in_specs = [_HBM] * na + ([_SEMS, _SEMS] if sems is not None else []) + [_HBM] * len(after)
    args = [pltpu.with_memory_space_constraint(a, pltpu.HBM) for a in arrs] + list(sems or ()) \
        + [pltpu.with_memory_space_constraint(a, pltpu.HBM) for a in after]
    return pl.pallas_call(
        body, name=name, out_shape=tuple(out_shape), in_specs=in_specs, out_specs=tuple(out_specs),
        input_output_aliases={i: first + i for i in range(na)},
        compiler_params=pltpu.CompilerParams(has_side_effects=_EFFECT))(*args)


def _gather2_start(srcs, *, name):
    n = len(srcs)
    lands = [lax.empty((NDEV,) + tuple(s.shape), s.dtype) for s in srcs]

    def body(*refs):
        src_refs, land_refs = refs[:n], refs[n:2 * n]
        send_sems, recv_sems = refs[2 * n], refs[2 * n + 1]
        _, _, _, me = _place()
        for w in range(n):
            for slot, k in enumerate((1,) + _SAME_CORE):
                _gather2_copy(src_refs[w], land_refs[w], send_sems, recv_sems, 4 * w + slot, k, me).start()
        refs[-1][...] = jnp.zeros_like(refs[-1])

    res = _split_call(body, name, list(srcs) + lands, 4 * n, token=True)
    return res[:2], res[2:-1], res[-1]


def _gather2_arrived(sems, arrs, after, *, name):
    n = len(arrs) // 2

    def body(*refs):
        src_refs, land_refs = refs[:n], refs[n:2 * n]
        send_sems, recv_sems = refs[2 * n], refs[2 * n + 1]
        x, y, c, me = _place()
        for w in range(n):
            for slot, k in enumerate((1,) + _SAME_CORE):
                _, pidx = _flip(x, y, c, k)
                _gather2_copy(src_refs[w], land_refs[w], send_sems, recv_sems, 4 * w + slot, k, me).wait_send()
                _gather2_copy(None, land_refs[w], send_sems, recv_sems, 4 * w + slot, k, pidx).wait_recv()

    return _split_call(body, name, arrs, None, sems=sems, after=after)


def _gather2_forward(lands, *, name):
    n = len(lands)

    def body(*refs):
        land_refs = refs[:n]
        send_sems, recv_sems = refs[n], refs[n + 1]
        x, y, c, _ = _place()
        for w in range(n):
            for j, k in enumerate(_SAME_CORE):
                _, pidx = _flip(x, y, c, k)
                _gather2_copy(None, land_refs[w], send_sems, recv_sems, 3 * w + j, 1, pidx).start()
        refs[-1][...] = jnp.zeros_like(refs[-1])

    res = _split_call(body, name, list(lands), 3 * n, token=True)
    return res[:2], res[2:-1], res[-1]


def _gather2_wait(sems, lands, after, *, name):
    n = len(lands)

    def body(*refs):
        land_refs = refs[:n]
        send_sems, recv_sems = refs[n], refs[n + 1]
        x, y, c, _ = _place()
        for w in range(n):
            for j, k in enumerate(_SAME_CORE):
                _, mine = _flip(x, y, c, k)
                _, theirs = _flip(x, y, c, k ^ 1)
                _gather2_copy(None, land_refs[w], send_sems, recv_sems, 3 * w + j, 1, mine).wait_send()
                _gather2_copy(None, land_refs[w], send_sems, recv_sems, 3 * w + j, 1, theirs).wait_recv()

    return _split_call(body, name, list(lands), None, sems=sems, after=after)


def _cast_bf16(arrs, *, name, after=()):
    n = len(arrs)

    def body(*refs):
        for i_ref, o_ref in zip(refs[:n], refs[n + len(after):]):
            o_ref[...] = i_ref[...].astype(BF16)

    return pl.pallas_call(
        body, name=name, out_shape=tuple(jax.ShapeDtypeStruct(a.shape, BF16) for a in arrs),
        in_specs=[pl.BlockSpec(memory_space=pltpu.VMEM)] * n + [pl.BlockSpec(memory_space=pl.ANY)] * len(after),
        compiler_params=_cparams())(*arrs, *after)


def _rs_adam(me_arr, fulls, lands, w, m, v, *, name):
    rows = lands[0].shape[1]
    k = len(fulls)
    nb = next(n for n in (4, 2, 1) if rows % (16 * n) == 0)
    br = rows // nb

    def body(me_ref, *refs):
        own_refs, land_refs = refs[:k], refs[k:2 * k]
        w_ref, m_ref, v_ref, g_ref, d_ref, mo_ref, vo_ref = refs[2 * k:]
        me = me_ref[0]
        parts = []
        for own_ref, land_ref in zip(own_refs, land_refs):
            acc = jnp.zeros(own_ref.shape, F32)
            for p in range(NDEV):
                acc = acc + jnp.where(p == me, own_ref[...], land_ref[p].astype(F32))
            parts.append(acc)
        acc = parts[0] if k == 1 else jnp.concatenate(parts, axis=1)
        g_ref[...] = acc
        d_ref[...], mo_ref[...], vo_ref[...] = _adam_math(w_ref[...], acc, m_ref[...], v_ref[...])

    sh = jax.ShapeDtypeStruct((rows, D), F32)
    blk = pl.BlockSpec((br, D), lambda i, me: (i, 0))
    grid_spec = pltpu.PrefetchScalarGridSpec(
        num_scalar_prefetch=1, grid=(nb,),
        in_specs=[pl.BlockSpec((br, f.shape[1]), lambda i, me: (me[0] * nb + i, 0)) for f in fulls]
        + [pl.BlockSpec((NDEV, br, l.shape[2]), lambda i, me: (0, i, 0)) for l in lands]
        + [blk, blk, blk],
        out_specs=[blk] * 4)
    return pl.pallas_call(
        body, name=name, out_shape=(sh, sh, sh, sh), grid_spec=grid_spec, compiler_params=_cparams(1),
    )(me_arr, *fulls, *lands, w, m, v)


ROW_F, ROW_I, ROW_C, ROW_G1, ROW_LG = 0, 8, 16, 24, 32
PACK_ROWS = 40


def _small_sum(me_arr, pack, pack_land, sgw, sgw_land, wmod):
    ncol = wmod.shape[1]

    def body(me_ref, pack_ref, pland_ref, sgw_ref, sland_ref, wmod_ref, sum_ref, all_ref, sgw_sum_ref, part_ref):
        me = me_ref[0]
        acc = jnp.zeros((PACK_ROWS, D), F32)
        acc_w = jnp.zeros(sgw_ref.shape, F32)
        for p in range(NDEV):
            rows = jnp.where(p == me, pack_ref[...], pland_ref[p])
            all_ref[p] = rows
            acc = acc + rows
            acc_w = acc_w + jnp.where(p == me, sgw_ref[...], sland_ref[p])
        sum_ref[...] = acc
        sgw_sum_ref[...] = acc_w
        zero = jnp.zeros((1, D), F32)
        dcmod = jnp.concatenate([acc[ROW_C:ROW_C + 1], acc[ROW_C + 1:ROW_C + 2], zero, zero, zero, zero], axis=1)
        mine = jnp.zeros((1, ncol), F32)
        for d in range(NDEV):
            mine = mine + jnp.where(d == me, dcmod[:, d * ncol:(d + 1) * ncol], 0.0)
        part_ref[...] = _dot_nt(jnp.broadcast_to(mine, (8, ncol)).astype(BF16), wmod_ref[...].astype(BF16))

    vm = pl.BlockSpec(memory_space=pltpu.VMEM)
    return pl.pallas_call(
        body, name="small_sum",
        out_shape=(jax.ShapeDtypeStruct((PACK_ROWS, D), F32), jax.ShapeDtypeStruct((NDEV, PACK_ROWS, D), F32),
                   jax.ShapeDtypeStruct(sgw.shape, F32), jax.ShapeDtypeStruct((8, D), F32)),
        in_specs=[pl.BlockSpec(memory_space=pltpu.SMEM)] + [vm] * 5,
        compiler_params=_cparams(),
    )(me_arr, pack, pack_land, sgw, sgw_land, wmod)


def _cctx_final(me_arr, part, part_land, cctx_row, m_row, v_row):
    def body(me_ref, part_ref, land_ref, c_ref, m_ref, v_ref, g_ref, d_ref, mo_ref, vo_ref):
        me = me_ref[0]
        tot = jnp.zeros((8, D), F32)
        for p in range(NDEV):
            tot = tot + jnp.where(p == me, part_ref[...], land_ref[p])
        cc = c_ref[...]
        _, dsilu = _silu_parts(cc)
        g = tot[0:1, :] * dsilu
        g_ref[...] = g
        d_ref[...], mo_ref[...], vo_ref[...] = _adam_math(cc, g, m_ref[...], v_ref[...])

    row = jax.ShapeDtypeStruct((1, D), F32)
    vm = pl.BlockSpec(memory_space=pltpu.VMEM)
    return pl.pallas_call(
        body, name="cctx_final", out_shape=(row, row, row, row),
        in_specs=[pl.BlockSpec(memory_space=pltpu.SMEM)] + [vm] * 5,
        compiler_params=_cparams(),
    )(me_arr, part, part_land, cctx_row, m_row, v_row)


def _adam_small(s, sgw_g, params):
    names = ["norm1", "norm2", "norm_f", "sg_gain", "sg_b", "ret_logit_f", "ret_logit_b", "b_mod", "sg_w"]
    flat = [a for n in names for a in params[n]]

    def body(*refs):
        s_ref, sgw_ref = refs[0], refs[1]
        p_refs = refs[2:2 + 3 * len(names)]
        o_refs = refs[2 + 3 * len(names):]
        loss_ref = o_refs[-1]

        def row(r):
            return s_ref[r:r + 1, :]

        grads = {
            "norm1": row(ROW_I + 2) + row(ROW_C + 2),
            "norm2": row(ROW_F + 4),
            "norm_f": row(ROW_F + 0),
            "sg_gain": s_ref[ROW_I + 3:ROW_I + 4, 0:AW],
            "sg_b": s_ref[ROW_I + 4:ROW_I + 8, 0:DH],
            "sg_w": sgw_ref[...],
        }
        for j, n in enumerate(names):
            w_ref, m_ref, v_ref = p_refs[3 * j:3 * j + 3]
            g_ref, d_ref, mo_ref, vo_ref = o_refs[4 * j:4 * j + 4]
            w = w_ref[...]
            if n in ("ret_logit_f", "ret_logit_b"):
                r = ROW_LG + (0 if n == "ret_logit_f" else 1)
                g = s_ref[r:r + 1, 0:H] * _sigmoid(-w)
            elif n == "b_mod":
                rows = [row(ROW_I + 0) + row(ROW_C + 0), row(ROW_I + 1) + row(ROW_C + 1), row(ROW_G1),
                        row(ROW_F + 2), row(ROW_F + 3), row(ROW_F + 1)]
                g = jnp.concatenate(rows, axis=1)
            else:
                g = grads[n]
            g_ref[...] = g
            d_ref[...], mo_ref[...], vo_ref[...] = _adam_math(w, g, m_ref[...], v_ref[...])
        loss_ref[...] = jnp.full((1, 1), 0.5 / D, F32) * jnp.sum(row(ROW_F + 5), axis=1, keepdims=True)

    out_shape = []
    for n in names:
        w = params[n][0]
        out_shape += [jax.ShapeDtypeStruct(w.shape, F32)] * 4
    out_shape.append(jax.ShapeDtypeStruct((1, 1), F32))
    outs = pl.pallas_call(body, name="adam_small", out_shape=tuple(out_shape), compiler_params=_cparams())(
        s, sgw_g, *flat)
    return {n: tuple(outs[4 * j:4 * j + 4]) for j, n in enumerate(names)}, outs[-1]


def _wmod_grad(cs_all, dmod_cols, wmod, m, v):
    ncol = wmod.shape[1]

    def body(cs_ref, dm_ref, w_ref, m_ref, v_ref, g_ref, d_ref, mo_ref, vo_ref):
        g = _dot_tn(cs_ref[...].astype(BF16), dm_ref[...].astype(BF16))
        g_ref[...] = g
        d_ref[...], mo_ref[...], vo_ref[...] = _adam_math(w_ref[...], g, m_ref[...], v_ref[...])

    sh = jax.ShapeDtypeStruct((D, ncol), F32)
    br = D // 4
    blk = pl.BlockSpec((br, ncol), lambda i: (i, 0))
    return pl.pallas_call(
        body, name="wmod_grad", out_shape=(sh, sh, sh, sh), grid=(D // br,),
        in_specs=[pl.BlockSpec((cs_all.shape[0], br), lambda i: (0, i)), _whole(dmod_cols.shape), blk, blk, blk],
        out_specs=[blk] * 4,
        compiler_params=_cparams(1),
    )(cs_all, dmod_cols, wmod, m, v)


def _rope_tables(t_len):
    n_freq = DH // 4
    inv = (ROPE_BASE ** (-np.arange(n_freq, dtype=np.float32) / n_freq)).astype(np.float32)
    tok = np.arange(t_len)
    rows = (tok // GRID_W).astype(np.float32)
    cols = (tok % GRID_W).astype(np.float32)
    ang_r = rows[:, None] * inv[None, :]
    ang_c = cols[:, None] * inv[None, :]
    cos = np.concatenate([np.cos(ang_r)] * 2 + [np.cos(ang_c)] * 2, axis=1).astype(np.float32)
    sin = np.concatenate([-np.sin(ang_r), np.sin(ang_r), -np.sin(ang_c), np.sin(ang_c)], axis=1).astype(np.float32)
    return jnp.asarray(cos), jnp.asarray(sin)


def _local_step(x, tgt, ctx, mod6, cmod6, norm1, norm2, normf, win, wout, ffn_weights, sgw, sgb, sggain, rlf, rlb,
                *, tm_a, tm_b, tm_f, tm_m, tm_r, tm_i, bt_w, after_first_grads=None, small_path=None,
                after_in_half=None):
    t_len = x.shape[0]
    cos, sin = _rope_tables(t_len)
    sgbt = jnp.broadcast_to(sgb[:, :, None], (G, C, 128))
    rlf = jnp.broadcast_to(rlf.reshape(H, 1), (H, 128))
    rlb = jnp.broadcast_to(rlb.reshape(H, 1), (H, 128))
    hc, kvc, scf, scb = _ctx_fwd(ctx, cmod6, norm1, win, rlf, rlb)
    hx, zuv, q, k, v, gfb, of, ya, sst = _fwd_a(x, mod6, norm1, win, sgw, sgbt, sggain, cos, sin, rlf, scf, tm=tm_a)
    if callable(wout):
        wout, tok = wout(hx)
        rlb = rlb + tok
    ob, ycat, x1, rst = _fwd_b(q, k, v, of, gfb, ya, x, mod6, wout, rlb, scb, tm=tm_b)
    wg, wu, wd = ffn_weights(x1) if callable(ffn_weights) else ffn_weights
    dx1, h2, da, db, hm, df, small_f = _ffn(x1, tgt, mod6, norm2, normf, wg, wu, wd, tm=tm_f)
    bt2 = min(2 * bt_w, t_len)
    d_wd, d_wd_b = _tn_matmul(hm, df, bm=DFF // 2, bt=bt2, name="dw_down")
    d_wg, d_wg_b = _tn_matmul(da, h2, bm=DFF // 2, bt=bt2, name="dw_gate")
    d_wu, d_wu_b = _tn_matmul(db, h2, bm=DFF // 2, bt=bt2, name="dw_up")
    dxb, dya, dgfb, dof, dob = _mid_bwd(dx1, of, ob, gfb, mod6, wout, tm=tm_m)
    d_wo, d_wo_b, dg1 = _dw_out(ycat, dxb, mod6, wout, bt=bt2)
    if after_first_grads is not None:
        rlf = rlf + after_first_grads((d_wd_b, d_wg_b, d_wu_b, d_wo_b))
    dqf, dkf, dvf, dqb, dkb, dvb, dscf, dscb, dlg = _ret_bwd(q, k, v, dof, dob, sst, rst, rlf, rlb, tm=tm_r)
    gx, dz, small_i, dsgw = _in_bwd(x, zuv, dya, dqf, dkf, dvf, dqb, dkb, dvb, dgfb, dx1, cos, sin,
                                    mod6, norm1, win, sgw, sgbt, sggain, tm=tm_i)
    dwin_c, small_c, dlg = _ctx_bwd(ctx, hc, kvc, cmod6, norm1, win, rlf, rlb, dscf, dscb, dlg)
    pack = jnp.concatenate([small_f, small_i, small_c, dg1, dlg], axis=0)
    after = small_path(pack, dsgw) if small_path is not None else ()
    halves = []
    for j in range(2):
        halves.append(_tn_matmul(dz, hx, bm=INC // 2, bt=bt2, name="dw_in_%d" % j, init=dwin_c,
                                 init_rows=(KV_LO, KV_HI), after=after, cols=(j, D // 2)))
        if after_in_half is not None:
            after = after_in_half(j, halves[-1][1])
    grads = {"w_in": halves, "w_out": (d_wo, d_wo_b), "w_gate": (d_wg, d_wg_b), "w_up": (d_wu, d_wu_b),
             "w_down": (d_wd, d_wd_b)}
    return gx, grads, pack, dsgw


def kernel(x, c, ctx, c_ctx, w_mod, b_mod, norm1, w_in, sg_gain, sg_w, sg_b, ret_logit_f, ret_logit_b, w_out, norm2, w_gate, w_up, w_down, norm_f, loss_target, m_c_ctx, m_w_mod, m_b_mod, m_norm1, m_w_in, m_sg_gain, m_sg_w, m_sg_b, m_ret_logit_f, m_ret_logit_b, m_w_out, m_norm2, m_w_gate, m_w_up, m_w_down, m_norm_f, v_c_ctx, v_w_mod, v_b_mod, v_norm1, v_w_in, v_sg_gain, v_sg_w, v_sg_b, v_ret_logit_f, v_ret_logit_b, v_w_out, v_norm2, v_w_gate, v_w_up, v_w_down, v_norm_f):
    t_len = x.shape[1]
    tm = min(512, t_len)
    me = 4 * lax.axis_index("x") + 2 * lax.axis_index("y") + lax.axis_index("c")
    tr = lambda a: jnp.transpose(a[0])

    cctx_row = c_ctx.reshape(1, D)
    mod_all, cs_all, g_in = _entry_gather(c, cctx_row, w_mod[0], b_mod, [tr(w_in)])
    mod6 = lax.dynamic_slice(mod_all, (me, 0), (1, 6 * D)).reshape(6, D)
    cmod6 = mod_all[8].reshape(6, D)
    win_t = g_in.reshape(INC, D)

    (out_shard,) = _cast_bf16([w_out[0]], name="cast_out", after=[g_in])
    h_out, tok_out = _exchange_start([out_shard], scatter=False, name="wout_start")
    ffn_shards = _cast_bf16([tr(w_gate), tr(w_up), w_down[0]], name="cast_ffn", after=[h_out[2]])
    sems_ffn, arrs_ffn, tok = _gather2_start(ffn_shards, name="wffn_start")
    mod6 = mod6 + (tok_out[0, 0] + tok[0, 0])
    ffn = {}

    def place_own(own, lands, rows):
        return [lax.dynamic_update_slice(l, o[None], (me, 0, 0)).reshape(rows, D) for o, l in zip(own, lands)]

    def wout(after):
        own, lands = _exchange_wait(h_out, [after], scatter=False, name="wout_wait")
        arrs = _gather2_arrived(sems_ffn, arrs_ffn, [after], name="wffn_arrived")
        ffn["own"] = arrs[:3]
        ffn["sems"], ffn["lands"], tok_fwd = _gather2_forward(arrs[3:], name="wffn_forward")
        return place_own(own, lands, D)[0], tok_fwd[0, 0]

    def ffn_weights(after):
        lands = _gather2_wait(ffn["sems"], ffn["lands"], [after], name="wffn_wait")
        return place_own(ffn["own"], lands, DFF)

    rs, outs = {}, {}

    def after_first_grads(bf):
        rs["first"], tok_first = _exchange_start([b.reshape(NDEV, b.shape[0] // NDEV, D) for b in bf], scatter=True,
                                                 name="rs_first_start")
        return tok_first[0, 0]

    def small_path(pack, dsgw):
        rs["small"], _ = _exchange_start([pack, dsgw.reshape(G * C, C)], scatter=False, name="small_start")
        return [rs["small"][2], rs["small"][3]]

    def after_in_half(j, half_bf16):
        rs["in%d" % j], _ = _exchange_start([half_bf16.reshape(NDEV, INC // NDEV, D // 2)], scatter=True,
                                            name="rs_in%d_start" % j)
        return [rs["in%d" % j][2]]

    gx, grads, pack, dsgw = _local_step(
        x[0], loss_target[0], ctx[0], mod6, cmod6, norm1, norm2[0:1], norm_f.reshape(1, D), win_t, wout, ffn_weights,
        sg_w[0], sg_b[0], sg_gain, ret_logit_f, ret_logit_b,
        tm_a=tm, tm_b=min(1024, t_len), tm_f=min(256, t_len), tm_m=min(1024, t_len), tm_r=min(1024, t_len), tm_i=tm,
        bt_w=min(1024, t_len),
        after_first_grads=after_first_grads, small_path=small_path, after_in_half=after_in_half)

    me_arr = me.reshape(1).astype(jnp.int32)
    (pack_own, sgw_own), (pack_land, sgw_land) = _exchange_wait(rs["small"], [rs["in1"][2]], scatter=False,
                                                               name="small_wait")
    psum_rows, pall, sgw_sum, part = _small_sum(me_arr, pack_own, pack_land, sgw_own, sgw_land, w_mod[0])
    h_cc, _ = _exchange_start([part], scatter=False, name="cctx_start")

    dmod_rows = (ROW_I + 0, ROW_I + 1, ROW_G1, ROW_F + 2, ROW_F + 3, ROW_F + 1)
    dmod_all = jnp.concatenate([pall[:, r, :] for r in dmod_rows], axis=1)
    dcmod = jnp.concatenate([psum_rows[ROW_C + 0:ROW_C + 1], psum_rows[ROW_C + 1:ROW_C + 2],
                             jnp.zeros((1, 4 * D), F32)], axis=1)
    dmod_mat = jnp.concatenate([dmod_all, jnp.pad(dcmod, ((0, 7), (0, 0)))], axis=0)
    ncol = 6 * D // NDEV
    dmod_cols = lax.dynamic_slice(dmod_mat, (0, me * ncol), (16, ncol))
    g_wm, d_wm, m_wm, v_wm = _wmod_grad(cs_all, dmod_cols, w_mod[0], m_w_mod[0], v_w_mod[0])
    outs["w_mod"] = (g_wm[None], d_wm[None], m_wm[None], v_wm[None])
    small_params = {
        "norm1": (norm1, m_norm1, v_norm1), "norm2": (norm2, m_norm2, v_norm2),
        "norm_f": tuple(a.reshape(1, D) for a in (norm_f, m_norm_f, v_norm_f)),
        "sg_gain": (sg_gain, m_sg_gain, v_sg_gain), "sg_b": (sg_b[0], m_sg_b[0], v_sg_b[0]),
        "ret_logit_f": (ret_logit_f, m_ret_logit_f, v_ret_logit_f),
        "ret_logit_b": (ret_logit_b, m_ret_logit_b, v_ret_logit_b),
        "b_mod": (b_mod, m_b_mod, v_b_mod), "sg_w": (sg_w[0], m_sg_w[0], v_sg_w[0])}
    small, loss = _adam_small(psum_rows, sgw_sum.reshape(G, C, C), small_params)
    back = {"norm_f": lambda a: a.reshape(D), "sg_b": lambda a: a[None], "sg_w": lambda a: a[None]}
    for name, vals in small.items():
        outs[name] = tuple(back.get(name, lambda a: a)(a) for a in vals)

    _, lands_first = _exchange_wait(rs["first"], [g_wm], scatter=True, name="rs_first_wait")

    def finish(name, fulls, lands, w, m, v, transposed):
        take = tr if transposed else (lambda a: a[0])
        res = _rs_adam(me_arr, fulls, lands, take(w), take(m), take(v), name="adam_" + name)
        put = (lambda a: jnp.transpose(a)[None]) if transposed else (lambda a: a[None])
        outs[name] = tuple(put(a) for a in res)
        return res[0]

    g_down = finish("w_down", [grads["w_down"][0]], [lands_first[0]], w_down, m_w_down, v_w_down, False)
    g_gate = finish("w_gate", [grads["w_gate"][0]], [lands_first[1]], w_gate, m_w_gate, v_w_gate, True)
    g_up = finish("w_up", [grads["w_up"][0]], [lands_first[2]], w_up, m_w_up, v_w_up, True)
    g_out = finish("w_out", [grads["w_out"][0]], [lands_first[3]], w_out, m_w_out, v_w_out, False)
    done = [g_down, g_gate, g_up, g_out]
    (part_own,), (part_land,) = _exchange_wait(h_cc, done, scatter=False, name="cctx_wait")
    res_cc = _cctx_final(me_arr, part_own, part_land, cctx_row, m_c_ctx.reshape(1, D), v_c_ctx.reshape(1, D))
    outs["c_ctx"] = tuple(a.reshape(D) for a in res_cc)
    lands_in = [_exchange_wait(rs["in%d" % j], done, scatter=True, name="rs_in%d_wait" % j)[1][0] for j in range(2)]
    finish("w_in", [h[0] for h in grads["w_in"]], lands_in, w_in, m_w_in, v_w_in, True)

    order = ["c_ctx", "w_mod", "b_mod", "norm1", "w_in", "sg_gain", "sg_w", "sg_b", "ret_logit_f", "ret_logit_b",
             "w_out", "norm2", "w_gate", "w_up", "w_down", "norm_f"]
    res = [loss.reshape(()), gx[None]]
    for j in range(4):
        res += [outs[name][j] for name in order]
    return tuple(res)
```

```python
import functools
import math

import numpy as np
import jax
import jax.numpy as jnp
from jax import lax
from jax.experimental import pallas as pl
from jax.experimental.pallas import tpu as pltpu

F32 = jnp.float32
BF16 = jnp.bfloat16

D = 1024
AW = 512
RW = 512
H = 4
DH = 128
G = 4
C = 128
CR = 256
DFF = 2816
INC = 3584
Q_LO = 2 * AW
KV_LO, VR_LO, G_LO = Q_LO + RW, Q_LO + 2 * RW, Q_LO + 3 * RW
KV_HI = G_LO
NDEV = 8
EPS = 1e-6
KSCALE = DH ** -0.5
ROPE_BASE = 10000.0
GRID_W = 64

ADAM_LR = 0.001
ADAM_B1 = 0.9
ADAM_B2 = 0.999
ADAM_EPS = 1e-08
ADAM_WD = 0.01
ADAM_STEP = 10

VMEM_LIMIT = 56 * 1024 * 1024
MESH = pl.DeviceIdType.MESH


def _cparams(n_grid=0, **kw):
    sem = ("arbitrary",) * n_grid if n_grid else None
    return pltpu.CompilerParams(dimension_semantics=sem, vmem_limit_bytes=VMEM_LIMIT, **kw)


def _dot(a, b):
    return jnp.dot(a, b, preferred_element_type=F32)


def _dot_nt(a, b):
    return lax.dot_general(a, b, (((1,), (1,)), ((), ())), preferred_element_type=F32)


def _dot_tn(a, b):
    return lax.dot_general(a, b, (((0,), (0,)), ((), ())), preferred_element_type=F32)


def _whole(shape):
    nd = len(shape)
    return pl.BlockSpec(shape, lambda *_: (0,) * nd, pipeline_mode=pl.Buffered(1))


def _acc(shape):
    nd = len(shape)
    return pl.BlockSpec(shape, lambda *_: (0,) * nd)


def _rows(tm, n):
    return pl.BlockSpec((tm, n), lambda i: (i, 0))


def _rows_rev(tm, n, nt):
    return pl.BlockSpec((tm, n), lambda i: (nt - 1 - i, 0))


def _sigmoid(x):
    return 1.0 / (1.0 + jnp.exp(-x))


def _log_sigmoid(x):
    return jnp.minimum(x, 0.0) - jnp.log(1.0 + jnp.exp(-jnp.abs(x)))


_GK0 = math.sqrt(2.0 / math.pi)
_GK1 = 0.044715


def _gelu(x):
    x2 = x * x
    t = jnp.tanh(x * (_GK0 + (_GK0 * _GK1) * x2))
    h = 0.5 + 0.5 * t
    g = x * h
    dg = h + g * (1.0 - h) * ((2.0 * _GK0) + (6.0 * _GK0 * _GK1) * x2)
    return g, dg


def _silu_parts(a):
    s = _sigmoid(a)
    sa = a * s
    return sa, s + sa * (1.0 - s)


def _rope(t, cos, sins):
    n = t.shape[1]
    lane = lax.broadcasted_iota(jnp.int32, t.shape, 1)
    first = (lane & 63) < 32
    partner = jnp.where(first, pltpu.roll(t, n - 32, 1), pltpu.roll(t, 32, 1))
    return t * cos + partner * sins


def _headnorm(o):
    outs, rs = [], []
    for h in range(H):
        oh = o[:, h * DH:(h + 1) * DH]
        r = lax.rsqrt(jnp.mean(oh * oh, axis=-1, keepdims=True) + EPS)
        outs.append(oh * r)
        rs.append(r)
    return jnp.concatenate(outs, axis=1), rs


def _decay_consts(lg, forward):
    ii = lax.broadcasted_iota(jnp.int32, (CR, CR), 0).astype(F32)
    jj = lax.broadcasted_iota(jnp.int32, (CR, CR), 1).astype(F32)
    ic = lax.broadcasted_iota(jnp.int32, (CR, 1), 0).astype(F32)
    if forward:
        diff = ii - jj
        xi = jnp.exp(lg * (ic + 1.0))
        z = jnp.exp(lg * (CR - 1.0 - ic))
    else:
        diff = jj - ii
        xi = jnp.exp(lg * (CR - ic))
        z = jnp.exp(lg * ic)
    dm = jnp.where(diff >= 0.0, jnp.exp(lg * jnp.maximum(diff, 0.0)), 0.0)
    dec = jnp.exp(lg * float(CR))
    return dm, xi, z, dec, ic


def _ctx_fwd(ctx, cmod6, norm1, win, rlf, rlb):
    lc = ctx.shape[0]

    def body(ctx_ref, cmod_ref, n1_ref, win_ref, rlf_ref, rlb_ref, hc_ref, kvc_ref, scf_ref, scb_ref):
        x = ctx_ref[...]
        r = lax.rsqrt(jnp.mean(x * x, axis=-1, keepdims=True) + EPS)
        hc = (x * r) * (n1_ref[...] * (1.0 + cmod_ref[1:2, :])) + cmod_ref[0:1, :]
        hcb = hc.astype(BF16)
        hc_ref[...] = hcb
        kv = _dot_nt(hcb, win_ref[KV_LO:KV_HI, :])
        k = kv[:, :RW] * KSCALE
        v = kv[:, RW:]
        kvc_ref[:, :RW] = k
        kvc_ref[:, RW:] = v
        t = lax.broadcasted_iota(jnp.int32, (lc, 1), 0).astype(F32)
        lgf = _log_sigmoid(rlf_ref[...])
        lgb = _log_sigmoid(rlb_ref[...])
        for h in range(H):
            hs = slice(h * DH, (h + 1) * DH)
            wf = jnp.exp(lgf[h:h + 1, 0:1] * (lc - 1.0 - t))
            wb = jnp.exp(lgb[h:h + 1, 0:1] * t)
            vh = v[:, hs].astype(BF16)
            scf_ref[h] = _dot_tn((k[:, hs] * wf).astype(BF16), vh)
            scb_ref[h] = _dot_tn((k[:, hs] * wb).astype(BF16), vh)

    return pl.pallas_call(
        body, name="ctx_fwd",
        out_shape=(jax.ShapeDtypeStruct((lc, D), BF16), jax.ShapeDtypeStruct((lc, 2 * RW), F32),
                   jax.ShapeDtypeStruct((H, DH, DH), F32), jax.ShapeDtypeStruct((H, DH, DH), F32)),
        compiler_params=_cparams(),
    )(ctx, cmod6, norm1, win, rlf, rlb)


def _sg_forward(u, v, sgw_ref, sgbt_ref, sgg_ref, nc):
    ua, dgu = _gelu(u)
    va, dgv = _gelu(v)
    gain = sgg_ref[...]
    mixed, vn_b, vah_l, rv_l = [], [], [], []
    for g in range(G):
        gs = slice(g * DH, (g + 1) * DH)
        vag = va[:, gs]
        rv = lax.rsqrt(jnp.mean(vag * vag, axis=-1, keepdims=True) + EPS)
        vah = vag * rv
        vn = (vah * gain[:, gs]).astype(BF16)
        wg = sgw_ref[g].astype(BF16)
        bcol = sgbt_ref[g]
        rows = [_dot(wg, vn[c * C:(c + 1) * C, :]) + bcol for c in range(nc)]
        mixed.append(jnp.concatenate(rows, axis=0) if nc > 1 else rows[0])
        vn_b.append(vn)
        vah_l.append(vah)
        rv_l.append(rv)
    mixed = jnp.concatenate(mixed, axis=1)
    return ua * mixed, (ua, dgu, dgv, mixed, vn_b, vah_l, rv_l)


def _fwd_a(x, mod6, norm1, win, sgw, sgbt, sggain, cos, sin, rlf, scf, *, tm):
    t_len = x.shape[0]
    nt, nc, ncr = t_len // tm, tm // C, tm // CR

    def body(x_ref, mod_ref, n1_ref, win_ref, sgw_ref, sgbt_ref, sgg_ref, cos_ref, sin_ref, rlf_ref, scf_ref,
             hx_ref, zuv_ref, q_ref, k_ref, v_ref, gfb_ref, of_ref, ya_ref, sst_ref, s_scr):
        i = pl.program_id(0)

        @pl.when(i == 0)
        def _():
            s_scr[...] = scf_ref[...]

        x = x_ref[...]
        r = lax.rsqrt(jnp.mean(x * x, axis=-1, keepdims=True) + EPS)
        hx = (x * r) * (n1_ref[...] * (1.0 + mod_ref[1:2, :])) + mod_ref[0:1, :]
        hxb = hx.astype(BF16)
        hx_ref[...] = hxb
        z = _dot_nt(hxb, win_ref[...])
        zuv_ref[...] = z[:, :2 * AW].astype(BF16)
        gfb_ref[...] = z[:, G_LO:INC].astype(BF16)
        cos = jnp.tile(cos_ref[...], (1, H))
        sins = jnp.tile(sin_ref[...], (1, H))
        q_ref[...] = _rope(z[:, Q_LO:KV_LO], cos, sins).astype(BF16)
        k_ref[...] = (_rope(z[:, KV_LO:VR_LO], cos, sins) * KSCALE).astype(BF16)
        v_ref[...] = z[:, VR_LO:G_LO].astype(BF16)
        ya, _ = _sg_forward(z[:, :AW], z[:, AW:2 * AW], sgw_ref, sgbt_ref, sgg_ref, nc)
        ya_ref[...] = ya.astype(BF16)

        lg = _log_sigmoid(rlf_ref[...])
        consts = [_decay_consts(lg[h:h + 1, 0:1], True) for h in range(H)]

        def first(c, h):
            dm, _, zc, dec, _ = consts[h]
            hs, rs = slice(h * DH, (h + 1) * DH), slice(c * CR, (c + 1) * CR)
            qc, kc, vc = q_ref[rs, hs], k_ref[rs, hs], v_ref[rs, hs]
            s = s_scr[h]
            sst_ref[c, h] = s
            scores = _dot_nt(qc, kc)
            cross = _dot(qc, s.astype(BF16))
            kz = (kc.astype(F32) * zc).astype(BF16)
            s_scr[h] = dec * s + _dot_tn(kz, vc)
            return (scores * dm).astype(BF16), cross

        def second(c, h, carried):
            a, cross = carried
            hs, rs = slice(h * DH, (h + 1) * DH), slice(c * CR, (c + 1) * CR)
            of_ref[rs, hs] = (_dot(a, v_ref[rs, hs]) + consts[h][1] * cross).astype(BF16)

        pending = None
        for c in range(ncr):
            for h in range(H):
                carried = first(c, h)
                if pending is not None:
                    second(*pending)
                pending = (c, h, carried)
        second(*pending)

    n_chunks = t_len // CR
    return pl.pallas_call(
        body, name="fwd_a", grid=(nt,),
        in_specs=[_rows(tm, D), _whole((6, D)), _whole((1, D)), _whole((INC, D)), _whole((G, C, C)),
                  _whole((G, C, 128)), _whole((1, AW)), _rows(tm, DH), _rows(tm, DH), _whole((H, 128)),
                  _whole((H, DH, DH))],
        out_specs=[_rows(tm, D), _rows(tm, 2 * AW), _rows(tm, RW), _rows(tm, RW), _rows(tm, RW),
                   _rows(tm, 2 * RW), _rows(tm, RW), _rows(tm, AW),
                   pl.BlockSpec((ncr, H, DH, DH), lambda i: (i, 0, 0, 0))],
        out_shape=(jax.ShapeDtypeStruct((t_len, D), BF16), jax.ShapeDtypeStruct((t_len, 2 * AW), BF16),
                   jax.ShapeDtypeStruct((t_len, RW), BF16), jax.ShapeDtypeStruct((t_len, RW), BF16),
                   jax.ShapeDtypeStruct((t_len, RW), BF16), jax.ShapeDtypeStruct((t_len, 2 * RW), BF16),
                   jax.ShapeDtypeStruct((t_len, RW), BF16), jax.ShapeDtypeStruct((t_len, AW), BF16),
                   jax.ShapeDtypeStruct((n_chunks, H, DH, DH), F32)),
        scratch_shapes=[pltpu.VMEM((H, DH, DH), F32)],
        compiler_params=_cparams(1),
    )(x, mod6, norm1, win, sgw, sgbt, sggain, cos, sin, rlf, scf)


def _fwd_b(q, k, v, of, gfb, ya, x, mod6, wout, rlb, scb, *, tm):
    t_len = x.shape[0]
    nt, nc = t_len // tm, tm // CR

    def body(q_ref, k_ref, v_ref, of_ref, gfb_ref, ya_ref, x_ref, mod_ref, wout_ref, rlb_ref, scb_ref,
             ob_ref, ycat_ref, x1_ref, rst_ref, r_scr):
        i = pl.program_id(0)

        @pl.when(i == 0)
        def _():
            r_scr[...] = scb_ref[...]

        lg = _log_sigmoid(rlb_ref[...])
        consts = [_decay_consts(lg[h:h + 1, 0:1], False) for h in range(H)]

        def first(c, h):
            dm, _, zc, dec, _ = consts[h]
            hs, rs = slice(h * DH, (h + 1) * DH), slice(c * CR, (c + 1) * CR)
            qc, kc, vc = q_ref[rs, hs], k_ref[rs, hs], v_ref[rs, hs]
            s = r_scr[h]
            rst_ref[c, h] = s
            scores = _dot_nt(qc, kc)
            cross = _dot(qc, s.astype(BF16))
            kz = (kc.astype(F32) * zc).astype(BF16)
            r_scr[h] = dec * s + _dot_tn(kz, vc)
            return (scores * dm).astype(BF16), cross

        def second(c, h, carried):
            a, cross = carried
            hs, rs = slice(h * DH, (h + 1) * DH), slice(c * CR, (c + 1) * CR)
            ob_ref[rs, hs] = (_dot(a, v_ref[rs, hs]) + consts[h][1] * cross).astype(BF16)

        def rows_out(c):
            rs = slice(c * CR, (c + 1) * CR)
            gfb = gfb_ref[rs, :].astype(F32)
            sf, _ = _silu_parts(gfb[:, :RW])
            sb, _ = _silu_parts(gfb[:, RW:])
            hnf, _ = _headnorm(of_ref[rs, :].astype(F32))
            hnb, _ = _headnorm(ob_ref[rs, :].astype(F32))
            yr = sf * hnf + sb * hnb
            ycat = jnp.concatenate([ya_ref[rs, :], yr.astype(BF16)], axis=1)
            ycat_ref[rs, :] = ycat
            x1_ref[rs, :] = x_ref[rs, :] + mod_ref[2:3, :] * _dot(ycat, wout_ref[...])

        pending, waiting_rows = None, None
        for c in reversed(range(nc)):
            for h in range(H):
                carried = first(c, h)
                if pending is not None:
                    second(*pending)
                pending = (c, h, carried)
            if waiting_rows is not None:
                rows_out(waiting_rows)
            waiting_rows = c
        second(*pending)
        rows_out(waiting_rows)

    n_chunks = t_len // CR
    rr = functools.partial(_rows_rev, nt=nt)
    return pl.pallas_call(
        body, name="fwd_b", grid=(nt,),
        in_specs=[rr(tm, RW), rr(tm, RW), rr(tm, RW), rr(tm, RW), rr(tm, 2 * RW), rr(tm, AW), rr(tm, D),
                  _whole((6, D)), _whole((D, D)), _whole((H, 128)), _whole((H, DH, DH))],
        out_specs=[rr(tm, RW), rr(tm, D), rr(tm, D),
                   pl.BlockSpec((nc, H, DH, DH), lambda i: (nt - 1 - i, 0, 0, 0))],
        out_shape=(jax.ShapeDtypeStruct((t_len, RW), BF16), jax.ShapeDtypeStruct((t_len, D), BF16),
                   jax.ShapeDtypeStruct((t_len, D), F32),
                   jax.ShapeDtypeStruct((n_chunks, H, DH, DH), F32)),
        scratch_shapes=[pltpu.VMEM((H, DH, DH), F32)],
        compiler_params=_cparams(1),
    )(q, k, v, of, gfb, ya, x, mod6, wout, rlb, scb)


def _ffn(x1, tgt, mod6, norm2, normf, wg, wu, wd, *, tm):
    t_len = x1.shape[0]
    nt = t_len // tm

    def body(x1_ref, tgt_ref, mod_ref, n2_ref, nf_ref, wg_ref, wu_ref, wd_ref,
             dx1_ref, h2_ref, da_ref, db_ref, hm_ref, df_ref, small_ref):
        i = pl.program_id(0)

        @pl.when(i == 0)
        def _():
            small_ref[...] = jnp.zeros_like(small_ref)

        sh2, sc2, g2 = mod_ref[3:4, :], mod_ref[4:5, :], mod_ref[5:6, :]
        n2, nf = n2_ref[...], nf_ref[...]
        x1 = x1_ref[...]
        r2 = lax.rsqrt(jnp.mean(x1 * x1, axis=-1, keepdims=True) + EPS)
        x1h = x1 * r2
        w2 = n2 * (1.0 + sc2)
        h2b = (x1h * w2 + sh2).astype(BF16)
        h2_ref[...] = h2b
        a = _dot_nt(h2b, wg_ref[...])
        b = _dot_nt(h2b, wu_ref[...])
        sa, dsa = _silu_parts(a)
        hmb = (sa * b).astype(BF16)
        hm_ref[...] = hmb
        f = _dot(hmb, wd_ref[...])
        x2 = x1 + g2 * f
        r3 = lax.rsqrt(jnp.mean(x2 * x2, axis=-1, keepdims=True) + EPS)
        x2h = x2 * r3
        diff = x2h * nf - tgt_ref[...]
        small_ref[5:6, :] += jnp.sum(diff * diff, axis=0, keepdims=True)
        dout = diff * (1.0 / D)
        small_ref[0:1, :] += jnp.sum(dout * x2h, axis=0, keepdims=True)
        dyg = dout * nf
        dx2 = r3 * (dyg - x2h * jnp.mean(dyg * x2h, axis=-1, keepdims=True))
        small_ref[1:2, :] += jnp.sum(dx2 * f, axis=0, keepdims=True)
        dfb = (dx2 * g2).astype(BF16)
        df_ref[...] = dfb
        dhm = _dot_nt(dfb, wd_ref[...])
        dbb = (dhm * sa).astype(BF16)
        dab = (dhm * b * dsa).astype(BF16)
        da_ref[...] = dab
        db_ref[...] = dbb
        dh2 = _dot(dab, wg_ref[...]) + _dot(dbb, wu_ref[...])
        small_ref[2:3, :] += jnp.sum(dh2, axis=0, keepdims=True)
        dhx = dh2 * x1h
        small_ref[3:4, :] += jnp.sum(dhx, axis=0, keepdims=True) * n2
        small_ref[4:5, :] += jnp.sum(dhx, axis=0, keepdims=True) * (1.0 + sc2)
        dyg2 = dh2 * w2
        dx1_ref[...] = dx2 + r2 * (dyg2 - x1h * jnp.mean(dyg2 * x1h, axis=-1, keepdims=True))

    return pl.pallas_call(
        body, name="ffn", grid=(nt,),
        in_specs=[_rows(tm, D), _rows(tm, D), _whole((6, D)), _whole((1, D)), _whole((1, D)),
                  _whole((DFF, D)), _whole((DFF, D)), _whole((DFF, D))],
        out_specs=[_rows(tm, D), _rows(tm, D), _rows(tm, DFF), _rows(tm, DFF), _rows(tm, DFF), _rows(tm, D),
                   _acc((8, D))],
        out_shape=(jax.ShapeDtypeStruct((t_len, D), F32), jax.ShapeDtypeStruct((t_len, D), BF16),
                   jax.ShapeDtypeStruct((t_len, DFF), BF16), jax.ShapeDtypeStruct((t_len, DFF), BF16),
                   jax.ShapeDtypeStruct((t_len, DFF), BF16), jax.ShapeDtypeStruct((t_len, D), BF16),
                   jax.ShapeDtypeStruct((8, D), F32)),
        compiler_params=_cparams(1),
    )(x1, tgt, mod6, norm2, normf, wg, wu, wd)


def _mid_bwd(dx1, of, ob, gfb, mod6, wout, *, tm):
    t_len = dx1.shape[0]
    nt = t_len // tm
    rc = min(256, tm)

    def body(dx1_ref, of_ref, ob_ref, gfb_ref, mod_ref, wout_ref,
             dxb_ref, dya_ref, dgfb_ref, dof_ref, dob_ref):
        for c in range(tm // rc):
            rows = slice(c * rc, (c + 1) * rc)
            dx1 = dx1_ref[rows, :]
            dxb_ref[rows, :] = dx1.astype(BF16)
            dyb = (dx1 * mod_ref[2:3, :]).astype(BF16)
            dycat = _dot_nt(dyb, wout_ref[...])
            dya_ref[rows, :] = dycat[:, :AW].astype(BF16)
            dyr = dycat[:, AW:]
            gfb = gfb_ref[rows, :].astype(F32)
            for o_ref, do_ref, lo in ((of_ref, dof_ref, 0), (ob_ref, dob_ref, RW)):
                sg, dsg = _silu_parts(gfb[:, lo:lo + RW])
                o = o_ref[rows, :].astype(F32)
                hn, rs = _headnorm(o)
                dgfb_ref[rows, lo:lo + RW] = (dyr * hn * dsg).astype(BF16)
                dhn = dyr * sg
                for h in range(H):
                    hs = slice(h * DH, (h + 1) * DH)
                    oh, dh = hn[:, hs], dhn[:, hs]
                    do_ref[rows, hs] = (rs[h] * (dh - oh * jnp.mean(dh * oh, axis=-1, keepdims=True))).astype(BF16)

    return pl.pallas_call(
        body, name="mid_bwd", grid=(nt,),
        in_specs=[_rows(tm, D), _rows(tm, RW), _rows(tm, RW), _rows(tm, 2 * RW), _whole((6, D)), _whole((D, D))],
        out_specs=[_rows(tm, D), _rows(tm, AW), _rows(tm, 2 * RW), _rows(tm, RW), _rows(tm, RW)],
        out_shape=(jax.ShapeDtypeStruct((t_len, D), BF16), jax.ShapeDtypeStruct((t_len, AW), BF16),
                   jax.ShapeDtypeStruct((t_len, 2 * RW), BF16), jax.ShapeDtypeStruct((t_len, RW), BF16),
                   jax.ShapeDtypeStruct((t_len, RW), BF16)),
        compiler_params=_cparams(1),
    )(dx1, of, ob, gfb, mod6, wout)


def _ret_bwd_items(q_ref, k_ref, v_ref, do_ref, st_ref, dq_ref, dk_ref, dv_ref, ds_scr, dlg_scr, lg, forward, nc, row0):
    order = list(reversed(range(nc))) if forward else list(range(nc))
    consts = [_decay_consts(lg[h:h + 1, 0:1], forward) for h in range(H)]
    accs = [jnp.zeros((1, DH), F32) for _ in range(H)]

    def first(h, c):
        dm, xi, zc, dec, _ = consts[h]
        hs, rs = slice(h * DH, (h + 1) * DH), slice(c * CR, (c + 1) * CR)
        qc, kc, vc, doc = q_ref[rs, hs], k_ref[rs, hs], v_ref[rs, hs], do_ref[rs, hs]
        s, dsn = st_ref[c, h], ds_scr[h]
        sb, dsnb = s.astype(BF16), dsn.astype(BF16)
        qf, kf = qc.astype(F32), kc.astype(F32)
        a_raw = _dot_nt(qc, kc)
        da_raw = _dot_nt(doc, vc)
        xdo = (xi * doc.astype(F32)).astype(BF16)
        kz = (kf * zc).astype(BF16)
        vz = (vc.astype(F32) * zc).astype(BF16)
        dq_c = _dot_nt(xdo, sb)
        dk_s = _dot_nt(vz, dsnb)
        dv_s = _dot(kz, dsnb)
        ds_scr[h] = _dot_tn((xi * qf).astype(BF16), doc) + dec * dsn
        accs[h] = accs[h] + (float(CR) * dec) * jnp.sum(s * dsn, axis=0, keepdims=True)
        a = (a_raw * dm).astype(BF16)
        da = (da_raw * dm).astype(BF16)
        return a, da, dq_c, dk_s, dv_s

    def second(h, c, carried):
        a, da, dq_c, dk_s, dv_s = carried
        ic = consts[h][4]
        hs, rs = slice(h * DH, (h + 1) * DH), slice(c * CR, (c + 1) * CR)
        qc, kc, doc = q_ref[rs, hs], k_ref[rs, hs], do_ref[rs, hs]
        dq = _dot(da, kc) + dq_c
        dk = _dot_tn(da, qc) + dk_s
        dq_ref[rs, hs] = dq.astype(BF16)
        dk_ref[rs, hs] = dk.astype(BF16)
        dv_ref[rs, hs] = (_dot_tn(a, doc) + dv_s).astype(BF16)
        qf, kf = qc.astype(F32), kc.astype(F32)
        both = jnp.sum((ic if forward else -ic) * (qf * dq - kf * dk), axis=0, keepdims=True)
        cross = jnp.sum(qf * dq_c, axis=0, keepdims=True)
        if forward:
            accs[h] = accs[h] + both + cross + (CR - 1.0) * jnp.sum(kf * dk_s, axis=0, keepdims=True)
        else:
            accs[h] = accs[h] + both + float(CR) * cross

    def finish():
        for h in range(H):
            dlg_scr[row0 + h:row0 + h + 1, :] += accs[h]

    items = [[(functools.partial(first, h, c), functools.partial(second, h, c)) for h in range(H)] for c in order]
    return items, finish


def _ret_bwd_run(dirs):
    nc = len(dirs[0][0])
    flat = [it for j in range(nc) for items, _ in dirs for it in items[j]]
    pending = None
    for first, second in flat:
        carried = first()
        if pending is not None:
            pending[0](pending[1])
        pending = (second, carried)
    pending[0](pending[1])
    for _, finish in dirs:
        finish()


def _ret_bwd(q, k, v, dof, dob, sst, rst, rlf, rlb, *, tm):
    t_len = q.shape[0]
    nt, nc = t_len // tm, tm // CR

    def body(qf_ref, kf_ref, vf_ref, dof_ref, sst_ref, qb_ref, kb_ref, vb_ref, dob_ref, rst_ref, rlf_ref, rlb_ref,
             dqf_ref, dkf_ref, dvf_ref, dqb_ref, dkb_ref, dvb_ref, dscf_ref, dscb_ref, dlg_ref,
             dsf_scr, dsb_scr, dlg_scr):
        i = pl.program_id(0)

        @pl.when(i == 0)
        def _():
            dsf_scr[...] = jnp.zeros_like(dsf_scr)
            dsb_scr[...] = jnp.zeros_like(dsb_scr)
            dlg_scr[...] = jnp.zeros_like(dlg_scr)

        lgf = _log_sigmoid(rlf_ref[...])
        lgb = _log_sigmoid(rlb_ref[...])
        _ret_bwd_run([
            _ret_bwd_items(qf_ref, kf_ref, vf_ref, dof_ref, sst_ref, dqf_ref, dkf_ref, dvf_ref, dsf_scr, dlg_scr,
                           lgf, True, nc, 0),
            _ret_bwd_items(qb_ref, kb_ref, vb_ref, dob_ref, rst_ref, dqb_ref, dkb_ref, dvb_ref, dsb_scr, dlg_scr,
                           lgb, False, nc, H)])

        @pl.when(i == nt - 1)
        def _():
            dscf_ref[...] = dsf_scr[...]
            dscb_ref[...] = dsb_scr[...]
            tot = jnp.broadcast_to(jnp.sum(dlg_scr[...], axis=1, keepdims=True), (8, 128))
            ri = lax.broadcasted_iota(jnp.int32, (8, 128), 0)
            li = lax.broadcasted_iota(jnp.int32, (8, 128), 1)
            dlg_ref[...] = jnp.zeros_like(dlg_ref)
            dlg_ref[0:1, 0:128] = jnp.sum(jnp.where(ri == li, tot, 0.0), axis=0, keepdims=True)
            dlg_ref[1:2, 0:128] = jnp.sum(jnp.where(ri - H == li, tot, 0.0), axis=0, keepdims=True)

    rr = functools.partial(_rows_rev, nt=nt)
    st_f = pl.BlockSpec((nc, H, DH, DH), lambda i: (nt - 1 - i, 0, 0, 0))
    st_b = pl.BlockSpec((nc, H, DH, DH), lambda i: (i, 0, 0, 0))
    tok = jax.ShapeDtypeStruct((t_len, RW), BF16)
    st = jax.ShapeDtypeStruct((H, DH, DH), F32)
    return pl.pallas_call(
        body, name="ret_bwd", grid=(nt,),
        in_specs=[rr(tm, RW), rr(tm, RW), rr(tm, RW), rr(tm, RW), st_f,
                  _rows(tm, RW), _rows(tm, RW), _rows(tm, RW), _rows(tm, RW), st_b,
                  _whole((H, 128)), _whole((H, 128))],
        out_specs=[rr(tm, RW), rr(tm, RW), rr(tm, RW), _rows(tm, RW), _rows(tm, RW), _rows(tm, RW),
                   _acc((H, DH, DH)), _acc((H, DH, DH)), _acc((8, D))],
        out_shape=(tok, tok, tok, tok, tok, tok, st, st, jax.ShapeDtypeStruct((8, D), F32)),
        scratch_shapes=[pltpu.VMEM((H, DH, DH), F32), pltpu.VMEM((H, DH, DH), F32), pltpu.VMEM((8, 128), F32)],
        compiler_params=_cparams(1),
    )(q, k, v, dof, sst, q, k, v, dob, rst, rlf, rlb)


def _in_bwd(x, zuv, dya, dqf, dkf, dvf, dqb, dkb, dvb, dgfb, dx1, cos, sin, mod6, norm1, win, sgw, sgbt, sggain, *, tm):
    t_len = x.shape[0]
    nt, nc = t_len // tm, tm // C

    def body(x_ref, zuv_ref, dya_ref, dqf_ref, dkf_ref, dvf_ref, dqb_ref, dkb_ref, dvb_ref, dgfb_ref, dx1_ref,
             cos_ref, sin_ref, mod_ref, n1_ref, win_ref, sgw_ref, sgbt_ref, sgg_ref,
             gx_ref, dz_ref, small_ref, dsgw_ref, dsgbt_ref):
        i = pl.program_id(0)

        @pl.when(i == 0)
        def _():
            small_ref[...] = jnp.zeros_like(small_ref)
            dsgw_ref[...] = jnp.zeros_like(dsgw_ref)
            dsgbt_ref[...] = jnp.zeros_like(dsgbt_ref)

        zuv = zuv_ref[...].astype(F32)
        _, (ua, dgu, dgv, mixed, vn_b, vah_l, rv_l) = _sg_forward(zuv[:, :AW], zuv[:, AW:], sgw_ref, sgbt_ref, sgg_ref, nc)
        dya = dya_ref[...].astype(F32)
        dz_ref[:, 0:AW] = (dya * mixed * dgu).astype(BF16)
        dmixed = dya * ua
        gain = sgg_ref[...]
        for g in range(G):
            gs = slice(g * DH, (g + 1) * DH)
            dmg = dmixed[:, gs]
            dmb = dmg.astype(BF16)
            wgt = sgw_ref[g].astype(BF16)
            dsw = jnp.zeros((C, C), F32)
            dsb = jnp.zeros((C, DH), F32)
            rows = []
            for c in range(nc):
                rs = slice(c * C, (c + 1) * C)
                dsw = dsw + _dot_nt(dmb[rs, :], vn_b[g][rs, :])
                dsb = dsb + dmg[rs, :]
                rows.append(_dot_tn(wgt, dmb[rs, :]))
            dsgw_ref[g] += dsw
            dsgbt_ref[g] += dsb
            dvn = jnp.concatenate(rows, axis=0) if nc > 1 else rows[0]
            vah, rv = vah_l[g], rv_l[g]
            small_ref[3:4, gs] += jnp.sum(dvn * vah, axis=0, keepdims=True)
            dyg = dvn * gain[:, gs]
            dva = rv * (dyg - vah * jnp.mean(dyg * vah, axis=-1, keepdims=True))
            dz_ref[:, AW + g * DH:AW + (g + 1) * DH] = (dva * dgv[:, gs]).astype(BF16)

        cos = jnp.tile(cos_ref[...], (1, H))
        nsins = -jnp.tile(sin_ref[...], (1, H))
        def both(f_ref, b_ref):
            return f_ref[...].astype(F32) + b_ref[...].astype(F32)

        dz_ref[:, Q_LO:KV_LO] = _rope(both(dqf_ref, dqb_ref), cos, nsins).astype(BF16)
        dz_ref[:, KV_LO:VR_LO] = (_rope(both(dkf_ref, dkb_ref), cos, nsins) * KSCALE).astype(BF16)
        dz_ref[:, VR_LO:G_LO] = both(dvf_ref, dvb_ref).astype(BF16)
        dz_ref[:, G_LO:INC] = dgfb_ref[...]

        dhx = _dot(dz_ref[...], win_ref[...])
        x = x_ref[...]
        r = lax.rsqrt(jnp.mean(x * x, axis=-1, keepdims=True) + EPS)
        xh = x * r
        n1, sc1 = n1_ref[...], mod_ref[1:2, :]
        small_ref[0:1, :] += jnp.sum(dhx, axis=0, keepdims=True)
        dhxx = jnp.sum(dhx * xh, axis=0, keepdims=True)
        small_ref[1:2, :] += dhxx * n1
        small_ref[2:3, :] += dhxx * (1.0 + sc1)
        dyg = dhx * (n1 * (1.0 + sc1))
        gx_ref[...] = dx1_ref[...] + r * (dyg - xh * jnp.mean(dyg * xh, axis=-1, keepdims=True))

        @pl.when(i == nt - 1)
        def _():
            ri = lax.broadcasted_iota(jnp.int32, (C, DH), 0)
            li = lax.broadcasted_iota(jnp.int32, (C, DH), 1)
            for g in range(G):
                tot = jnp.broadcast_to(jnp.sum(dsgbt_ref[g], axis=1, keepdims=True), (C, DH))
                small_ref[4 + g:5 + g, 0:DH] = jnp.sum(jnp.where(ri == li, tot, 0.0), axis=0, keepdims=True)

    tok = functools.partial(_rows, tm)
    return pl.pallas_call(
        body, name="in_bwd", grid=(nt,),
        in_specs=[tok(D), tok(2 * AW), tok(AW), tok(RW), tok(RW), tok(RW), tok(RW), tok(RW), tok(RW),
                  tok(2 * RW), tok(D), tok(DH), tok(DH), _whole((6, D)), _whole((1, D)), _whole((INC, D)),
                  _whole((G, C, C)), _whole((G, C, 128)), _whole((1, AW))],
        out_specs=[tok(D), tok(INC), _acc((8, D)), _acc((G, C, C))],
        out_shape=(jax.ShapeDtypeStruct((t_len, D), F32), jax.ShapeDtypeStruct((t_len, INC), BF16),
                   jax.ShapeDtypeStruct((8, D), F32), jax.ShapeDtypeStruct((G, C, C), F32)),
        scratch_shapes=[pltpu.VMEM((G, C, 128), F32)],
        compiler_params=_cparams(1),
    )(x, zuv, dya, dqf, dkf, dvf, dqb, dkb, dvb, dgfb, dx1, cos, sin, mod6, norm1, win, sgw, sgbt, sggain)


def _tn_matmul(a, b, *, bm, bt, name, init=None, init_rows=None, after=(), cols=None):
    t_len, m = a.shape
    cj, n = (0, b.shape[1]) if cols is None else cols
    ni, ntt = m // bm, t_len // bt
    has_init = init is not None

    def body(*refs):
        o_ref, ob_ref = refs[-2:]
        a_ref, b_ref = refs[:2]
        init_ref = refs[2] if has_init else None
        i, t = pl.program_id(0), pl.program_id(1)

        @pl.when(t == 0)
        def _():
            o_ref[...] = jnp.zeros_like(o_ref)

        if has_init:
            for ib in range(ni):
                lo, hi = max(init_rows[0], ib * bm), min(init_rows[1], (ib + 1) * bm)
                if lo < hi:
                    @pl.when(jnp.logical_and(t == 0, i == ib))
                    def _(lo=lo, hi=hi, ib=ib):
                        o_ref[lo - ib * bm:hi - ib * bm, :] = init_ref[lo - init_rows[0]:hi - init_rows[0], :]

        o_ref[...] += _dot_tn(a_ref[...], b_ref[...])

        @pl.when(t == ntt - 1)
        def _():
            ob_ref[...] = o_ref[...].astype(BF16)

    in_specs = [pl.BlockSpec((bt, bm), lambda i, t: (t, i)), pl.BlockSpec((bt, n), lambda i, t: (t, cj))]
    args = [a, b]
    if has_init:
        in_specs.append(pl.BlockSpec((init.shape[0], n), lambda i, t: (0, cj), pipeline_mode=pl.Buffered(1)))
        args.append(init)
    for arr in after:
        in_specs.append(pl.BlockSpec(memory_space=pl.ANY))
        args.append(arr)
    out_spec = pl.BlockSpec((bm, n), lambda i, t: (i, 0))
    return pl.pallas_call(
        body, name=name, grid=(ni, ntt),
        in_specs=in_specs,
        out_specs=[out_spec, out_spec],
        out_shape=(jax.ShapeDtypeStruct((m, n), F32), jax.ShapeDtypeStruct((m, n), BF16)),
        compiler_params=_cparams(2),
    )(*args)


def _dw_out(ycat, dxb, mod6, wout, *, bt):
    t_len = ycat.shape[0]
    ntt = t_len // bt

    def body(a_ref, b_ref, mod_ref, wout_ref, o_ref, ob_ref, dg1_ref):
        t = pl.program_id(0)

        @pl.when(t == 0)
        def _():
            o_ref[...] = jnp.zeros_like(o_ref)

        o_ref[...] += _dot_tn(a_ref[...], b_ref[...])

        @pl.when(t == ntt - 1)
        def _():
            m = o_ref[...]
            dg1_ref[...] = jnp.zeros_like(dg1_ref)
            dg1_ref[0:1, :] = jnp.sum(m * wout_ref[...].astype(F32), axis=0, keepdims=True)
            g = m * mod_ref[2:3, :]
            o_ref[...] = g
            ob_ref[...] = g.astype(BF16)

    return pl.pallas_call(
        body, name="dw_out", grid=(ntt,),
        in_specs=[pl.BlockSpec((bt, D), lambda t: (t, 0)), pl.BlockSpec((bt, D), lambda t: (t, 0)),
                  _whole((6, D)), _whole((D, D))],
        out_specs=[_acc((D, D)), _acc((D, D)), _acc((8, D))],
        out_shape=(jax.ShapeDtypeStruct((D, D), F32), jax.ShapeDtypeStruct((D, D), BF16),
                   jax.ShapeDtypeStruct((8, D), F32)),
        compiler_params=_cparams(1),
    )(ycat, dxb, mod6, wout)


def _ctx_bwd(ctx, hc, kvc, cmod6, norm1, win, rlf, rlb, dscf, dscb, dlg_in):
    lc = ctx.shape[0]

    def body(ctx_ref, hc_ref, kvc_ref, cmod_ref, n1_ref, win_ref, rlf_ref, rlb_ref, dscf_ref, dscb_ref, dlgin_ref,
             dwin_ref, small_ref, dlg_ref):
        k = kvc_ref[:, :RW]
        v = kvc_ref[:, RW:]
        t = lax.broadcasted_iota(jnp.int32, (lc, 1), 0).astype(F32)
        lgf = _log_sigmoid(rlf_ref[...])
        lgb = _log_sigmoid(rlb_ref[...])
        dks, dvs = [], []
        lane = lax.broadcasted_iota(jnp.int32, (1, 128), 1)
        row_f = jnp.zeros((1, 128), F32)
        row_b = jnp.zeros((1, 128), F32)
        for h in range(H):
            hs = slice(h * DH, (h + 1) * DH)
            wf = jnp.exp(lgf[h:h + 1, 0:1] * (lc - 1.0 - t))
            wb = jnp.exp(lgb[h:h + 1, 0:1] * t)
            kh, vhb = k[:, hs], v[:, hs].astype(BF16)
            dsf, dsb = dscf_ref[h].astype(BF16), dscb_ref[h].astype(BF16)
            dkf = wf * _dot_nt(vhb, dsf)
            dkb = wb * _dot_nt(vhb, dsb)
            dvs.append(_dot((kh * wf).astype(BF16), dsf) + _dot((kh * wb).astype(BF16), dsb))
            dks.append((dkf + dkb) * KSCALE)
            lf = jnp.sum((lc - 1.0 - t) * kh * dkf, axis=0, keepdims=True)
            lb = jnp.sum(t * kh * dkb, axis=0, keepdims=True)
            row_f = row_f + jnp.where(lane == h, jnp.sum(lf, axis=1, keepdims=True), 0.0)
            row_b = row_b + jnp.where(lane == h, jnp.sum(lb, axis=1, keepdims=True), 0.0)
        dlg_ref[...] = dlgin_ref[...]
        dlg_ref[0:1, 0:128] += row_f
        dlg_ref[1:2, 0:128] += row_b
        dkv = jnp.concatenate(dks + dvs, axis=1).astype(BF16)
        dhc = _dot(dkv, win_ref[KV_LO:KV_HI, :])
        dwin_ref[...] = _dot_tn(dkv, hc_ref[...])
        x = ctx_ref[...]
        r = lax.rsqrt(jnp.mean(x * x, axis=-1, keepdims=True) + EPS)
        xh = x * r
        small_ref[...] = jnp.zeros_like(small_ref)
        small_ref[0:1, :] = jnp.sum(dhc, axis=0, keepdims=True)
        dhxx = jnp.sum(dhc * xh, axis=0, keepdims=True)
        small_ref[1:2, :] = dhxx * n1_ref[...]
        small_ref[2:3, :] = dhxx * (1.0 + cmod_ref[1:2, :])

    return pl.pallas_call(
        body, name="ctx_bwd",
        out_shape=(jax.ShapeDtypeStruct((2 * RW, D), F32), jax.ShapeDtypeStruct((8, D), F32),
                   jax.ShapeDtypeStruct((8, D), F32)),
        compiler_params=_cparams(),
    )(ctx, hc, kvc, cmod6, norm1, win, rlf, rlb, dscf, dscb, dlg_in)


def _adam_math(w, g, m, v):
    m = ADAM_B1 * m + (1.0 - ADAM_B1) * g
    v = ADAM_B2 * v + (1.0 - ADAM_B2) * (g * g)
    m_hat = m / (1.0 - ADAM_B1 ** ADAM_STEP)
    v_hat = v / (1.0 - ADAM_B2 ** ADAM_STEP)
    delta = -ADAM_LR * (m_hat / (jnp.sqrt(v_hat) + ADAM_EPS) + ADAM_WD * w)
    return delta, m, v


def _place():
    x, y, c = lax.axis_index("x"), lax.axis_index("y"), lax.axis_index("c")
    return x, y, c, 4 * x + 2 * y + c


def _flip(x, y, c, k):
    px = 1 - x if (k >> 2) & 1 else x
    py = 1 - y if (k >> 1) & 1 else y
    pc = 1 - c if k & 1 else c
    return (px, py, pc), 4 * px + 2 * py + pc


def _gather_copy(buf_ref, send_sems, recv_sems, sem0, k, mine):
    x, y, c, me = _place()
    dev, idx = _flip(x, y, c, k)
    blk = me if mine else idx
    return pltpu.make_async_remote_copy(
        src_ref=buf_ref.at[blk], dst_ref=buf_ref.at[blk],
        send_sem=send_sems.at[sem0 + k - 1], recv_sem=recv_sems.at[sem0 + k - 1],
        device_id=dev, device_id_type=MESH)


def _entry_gather(c_row, cctx_row, wmod, bmod, shards):
    n = len(shards)
    ncol = wmod.shape[1]

    def body(*refs):
        c_ref, cctx_ref, wmod_ref, bmod_ref = refs[:4]
        in_refs = refs[4:4 + n]
        mod_ref, cs_ref = refs[4 + n:6 + n]
        out_refs = refs[6 + n:6 + 2 * n]
        cbuf, pbuf = refs[6 + 2 * n:8 + 2 * n]
        cast_refs = refs[8 + 2 * n:8 + 3 * n]
        small_send, small_recv, send_sems, recv_sems, local_sems = refs[8 + 3 * n:]
        x, y, c, me = _place()
        sib = (x, y, 1 - c)
        chips = [(1 - x, y), (x, 1 - y), (1 - x, 1 - y)]

        cbuf[me] = jnp.broadcast_to(c_ref[...], (8, D))
        small = [_gather_copy(cbuf, small_send, small_recv, 0, k, True) for k in range(1, NDEV)]
        for cp in small:
            cp.start()

        def blk(px, py, pc):
            return 4 * px + 2 * py + pc

        def copy(w, k, block, to, src=None):
            dst = out_refs[w].at[block]
            return pltpu.make_async_remote_copy(
                src_ref=dst if src is None else src, dst_ref=dst,
                send_sem=send_sems.at[w, k], recv_sem=recv_sems.at[w, k], device_id=to, device_id_type=MESH)

        first, passed, mine = [], [], []
        for w in range(n):
            cast_refs[w][...] = in_refs[w][...].astype(BF16)
            m = pltpu.make_async_copy(cast_refs[w], out_refs[w].at[me], local_sems.at[w])
            m.start()
            mine.append(m)
            cps = [copy(w, 0, me, sib, src=cast_refs[w])]
            cps += [copy(w, 1 + j, me, (*chip, c), src=cast_refs[w]) for j, chip in enumerate(chips)]
            for cp in cps:
                cp.start()
            first += cps

        for k in range(1, NDEV):
            _gather_copy(cbuf, small_send, small_recv, 0, k, False).wait_recv()
        row = lax.broadcasted_iota(jnp.int32, (16, D), 0)
        call = jnp.where(row == 8, jnp.broadcast_to(cctx_ref[...], (16, D)), 0.0)
        for d in range(NDEV):
            call = jnp.where(row == d, jnp.concatenate([cbuf[d], cbuf[d]], axis=0), call)
        cs = call * _sigmoid(call)
        cs_ref[...] = cs
        pbuf[me] = _dot(cs.astype(BF16), wmod_ref[...].astype(BF16))
        small2 = [_gather_copy(pbuf, small_send, small_recv, NDEV - 1, k, True) for k in range(1, NDEV)]
        for cp in small2:
            cp.start()

        for w in range(n):
            for j, chip in enumerate(chips):
                b = blk(*chip, c)
                copy(w, 1 + j, b, (x, y, c)).wait_recv()
                fw = copy(w, 4 + j, b, sib)
                fw.start()
                passed.append(fw)
        for w in range(n):
            copy(w, 0, blk(x, y, 1 - c), (x, y, c)).wait_recv()
            for j, chip in enumerate(chips):
                copy(w, 4 + j, blk(*chip, 1 - c), (x, y, c)).wait_recv()

        for k in range(1, NDEV):
            _gather_copy(pbuf, small_send, small_recv, NDEV - 1, k, False).wait_recv()
        for d in range(NDEV):
            mod_ref[:, d * ncol:(d + 1) * ncol] = pbuf[d] + bmod_ref[:, d * ncol:(d + 1) * ncol]
        for cp in small + small2 + first + passed:
            cp.wait_send()
        for m in mine:
            m.wait()

    vm = pl.BlockSpec(memory_space=pltpu.VMEM)
    hbm = pl.BlockSpec(memory_space=pltpu.HBM)
    return pl.pallas_call(
        body, name="entry_gather",
        in_specs=[vm] * (4 + n), out_specs=[vm, vm] + [hbm] * n,
        out_shape=(jax.ShapeDtypeStruct((16, 6 * D), F32), jax.ShapeDtypeStruct((16, D), F32))
        + tuple(jax.ShapeDtypeStruct((NDEV,) + s.shape, BF16) for s in shards),
        scratch_shapes=[pltpu.VMEM((NDEV, 8, D), F32), pltpu.VMEM((NDEV, 16, ncol), F32)]
        + [pltpu.VMEM(s.shape, BF16) for s in shards]
        + [pltpu.SemaphoreType.DMA((2 * (NDEV - 1),)), pltpu.SemaphoreType.DMA((2 * (NDEV - 1),)),
           pltpu.SemaphoreType.DMA((n, 7)), pltpu.SemaphoreType.DMA((n, 7)), pltpu.SemaphoreType.DMA((n,))],
        compiler_params=_cparams(),
    )(c_row, cctx_row, wmod, bmod, *shards)


_HBM = pl.BlockSpec(memory_space=pltpu.HBM)
_SEMS = pl.BlockSpec(memory_space=pltpu.SEMAPHORE)
_EFFECT = pltpu.SideEffectType.DATAFLOW_SIDE_EFFECTING


def _exchange_copy(src_refs, land_refs, send_sems, recv_sems, w, k, scatter, landing_slot_is_mine):
    x, y, c, me = _place()
    dev, pidx = _flip(x, y, c, k)
    src = src_refs[w].at[pidx] if scatter else src_refs[w]
    slot = me if landing_slot_is_mine else pidx
    return pltpu.make_async_remote_copy(
        src_ref=src, dst_ref=land_refs[w].at[slot],
        send_sem=send_sems.at[w * (NDEV - 1) + k - 1], recv_sem=recv_sems.at[w * (NDEV - 1) + k - 1],
        device_id=dev, device_id_type=MESH)


def _exchange_start(srcs, *, scatter, name):
    n = len(srcs)
    lands = [lax.empty((NDEV,) + tuple(s.shape[-2:]), s.dtype) for s in srcs]

    def body(*refs):
        src_refs, land_refs = refs[:n], refs[n:2 * n]
        send_sems, recv_sems = refs[2 * n], refs[2 * n + 1]
        token = refs[-1]
        for w in range(n):
            for k in range(1, NDEV):
                _exchange_copy(src_refs, land_refs, send_sems, recv_sems, w, k, scatter, True).start()
        token[...] = jnp.zeros_like(token)

    arrs = list(srcs) + lands
    out_shape = ([pltpu.SemaphoreType.DMA((n * (NDEV - 1),)), pltpu.SemaphoreType.DMA((n * (NDEV - 1),))]
                 + [pltpu.HBM(a.shape, a.dtype) for a in arrs] + [jax.ShapeDtypeStruct((8, 128), F32)])
    res = pl.pallas_call(
        body, name=name, out_shape=tuple(out_shape),
        in_specs=[_HBM] * (2 * n),
        out_specs=tuple([_SEMS, _SEMS] + [_HBM] * (2 * n) + [pl.BlockSpec(memory_space=pltpu.VMEM)]),
        input_output_aliases={i: 2 + i for i in range(2 * n)},
        compiler_params=pltpu.CompilerParams(has_side_effects=_EFFECT),
    )(*[pltpu.with_memory_space_constraint(a, pltpu.HBM) for a in arrs])
    return res[:-1], res[-1]


def _exchange_wait(handles, after, *, scatter, name):
    n = (len(handles) - 2) // 2
    na = len(after)

    def body(*refs):
        src_refs, land_refs = refs[:n], refs[n:2 * n]
        send_sems, recv_sems = refs[2 * n], refs[2 * n + 1]
        for w in range(n):
            for k in range(1, NDEV):
                cp = _exchange_copy(src_refs, land_refs, send_sems, recv_sems, w, k, scatter, False)
                cp.wait_send()
                cp.wait_recv()

    arrs = handles[2:]
    res = pl.pallas_call(
        body, name=name, out_shape=tuple(pltpu.HBM(a.shape, a.dtype) for a in arrs),
        in_specs=[_HBM] * (2 * n) + [_SEMS, _SEMS] + [_HBM] * na,
        out_specs=tuple([_HBM] * (2 * n)),
        input_output_aliases={i: i for i in range(2 * n)},
        compiler_params=pltpu.CompilerParams(has_side_effects=_EFFECT),
    )(*arrs, handles[0], handles[1], *[pltpu.with_memory_space_constraint(a, pltpu.HBM) for a in after])
    return res[:n], res[n:]


_SAME_CORE = (2, 4, 6)


def _gather2_copy(src_ref, land_ref, send_sems, recv_sems, sem, k, block):
    x, y, c, _ = _place()
    dev, _ = _flip(x, y, c, k)
    dst = land_ref.at[block]
    return pltpu.make_async_remote_copy(
        src_ref=dst if src_ref is None else src_ref, dst_ref=dst,
        send_sem=send_sems.at[sem], recv_sem=recv_sems.at[sem], device_id=dev, device_id_type=MESH)


def _split_call(body, name, arrs, n_sems, sems=None, after=(), token=False):
    na = len(arrs)
    out_shape, out_specs = [], []
    if n_sems is not None:
        out_shape += [pltpu.SemaphoreType.DMA((n_sems,)), pltpu.SemaphoreType.DMA((n_sems,))]
        out_specs += [_SEMS, _SEMS]
    first = len(out_shape)
    out_shape += [pltpu.HBM(a.shape, a.dtype) for a in arrs]
    out_specs += [_HBM] * na
    if token:
        out_shape.append(jax.ShapeDtypeStruct((8, 128), F32))
        out_specs.append(pl.BlockSpec(memory_space=pltpu.VMEM))
    in_specs = [_HBM] * na + ([_SEMS, _SEMS] if sems is not None else []) + [_HBM] * len(after)
    args = [pltpu.with_memory_space_constraint(a, pltpu.HBM) for a in arrs] + list(sems or ()) \
        + [pltpu.with_memory_space_constraint(a, pltpu.HBM) for a in after]
    return pl.pallas_call(
        body, name=name, out_shape=tuple(out_shape), in_specs=in_specs, out_specs=tuple(out_specs),
        input_output_aliases={i: first + i for i in range(na)},
        compiler_params=pltpu.CompilerParams(has_side_effects=_EFFECT))(*args)


def _gather2_start(srcs, *, name):
    n = len(srcs)
    lands = [lax.empty((NDEV,) + tuple(s.shape), s.dtype) for s in srcs]

    def body(*refs):
        src_refs, land_refs = refs[:n], refs[n:2 * n]
        send_sems, recv_sems = refs[2 * n], refs[2 * n + 1]
        _, _, _, me = _place()
        for w in range(n):
            for slot, k in enumerate((1,) + _SAME_CORE):
                _gather2_copy(src_refs[w], land_refs[w], send_sems, recv_sems, 4 * w + slot, k, me).start()
        refs[-1][...] = jnp.zeros_like(refs[-1])

    res = _split_call(body, name, list(srcs) + lands, 4 * n, token=True)
    return res[:2], res[2:-1], res[-1]


def _gather2_arrived(sems, arrs, after, *, name):
    n = len(arrs) // 2

    def body(*refs):
        src_refs, land_refs = refs[:n], refs[n:2 * n]
        send_sems, recv_sems = refs[2 * n], refs[2 * n + 1]
        x, y, c, me = _place()
        for w in range(n):
            for slot, k in enumerate((1,) + _SAME_CORE):
                _, pidx = _flip(x, y, c, k)
                _gather2_copy(src_refs[w], land_refs[w], send_sems, recv_sems, 4 * w + slot, k, me).wait_send()
                _gather2_copy(None, land_refs[w], send_sems, recv_sems, 4 * w + slot, k, pidx).wait_recv()

    return _split_call(body, name, arrs, None, sems=sems, after=after)


def _gather2_forward(lands, *, name):
    n = len(lands)

    def body(*refs):
        land_refs = refs[:n]
        send_sems, recv_sems = refs[n], refs[n + 1]
        x, y, c, _ = _place()
        for w in range(n):
            for j, k in enumerate(_SAME_CORE):
                _, pidx = _flip(x, y, c, k)
                _gather2_copy(None, land_refs[w], send_sems, recv_sems, 3 * w + j, 1, pidx).start()
        refs[-1][...] = jnp.zeros_like(refs[-1])

    res = _split_call(body, name, list(lands), 3 * n, token=True)
    return res[:2], res[2:-1], res[-1]


def _gather2_wait(sems, lands, after, *, name):
    n = len(lands)

    def body(*refs):
        land_refs = refs[:n]
        send_sems, recv_sems = refs[n], refs[n + 1]
        x, y, c, _ = _place()
        for w in range(n):
            for j, k in enumerate(_SAME_CORE):
                _, mine = _flip(x, y, c, k)
                _, theirs = _flip(x, y, c, k ^ 1)
                _gather2_copy(None, land_refs[w], send_sems, recv_sems, 3 * w + j, 1, mine).wait_send()
                _gather2_copy(None, land_refs[w], send_sems, recv_sems, 3 * w + j, 1, theirs).wait_recv()

    return _split_call(body, name, list(lands), None, sems=sems, after=after)


def _cast_bf16(arrs, *, name, after=()):
    n = len(arrs)

    def body(*refs):
        for i_ref, o_ref in zip(refs[:n], refs[n + len(after):]):
            o_ref[...] = i_ref[...].astype(BF16)

    return pl.pallas_call(
        body, name=name, out_shape=tuple(jax.ShapeDtypeStruct(a.shape, BF16) for a in arrs),
        in_specs=[pl.BlockSpec(memory_space=pltpu.VMEM)] * n + [pl.BlockSpec(memory_space=pl.ANY)] * len(after),
        compiler_params=_cparams())(*arrs, *after)


def _rs_adam(me_arr, fulls, lands, w, m, v, *, name):
    rows = lands[0].shape[1]
    k = len(fulls)
    nb = next(n for n in (4, 2, 1) if rows % (16 * n) == 0)
    br = rows // nb

    def body(me_ref, *refs):
        own_refs, land_refs = refs[:k], refs[k:2 * k]
        w_ref, m_ref, v_ref, g_ref, d_ref, mo_ref, vo_ref = refs[2 * k:]
        me = me_ref[0]
        parts = []
        for own_ref, land_ref in zip(own_refs, land_refs):
            acc = jnp.zeros(own_ref.shape, F32)
            for p in range(NDEV):
                acc = acc + jnp.where(p == me, own_ref[...], land_ref[p].astype(F32))
            parts.append(acc)
        acc = parts[0] if k == 1 else jnp.concatenate(parts, axis=1)
        g_ref[...] = acc
        d_ref[...], mo_ref[...], vo_ref[...] = _adam_math(w_ref[...], acc, m_ref[...], v_ref[...])

    sh = jax.ShapeDtypeStruct((rows, D), F32)
    blk = pl.BlockSpec((br, D), lambda i, me: (i, 0))
    grid_spec = pltpu.PrefetchScalarGridSpec(
        num_scalar_prefetch=1, grid=(nb,),
        in_specs=[pl.BlockSpec((br, f.shape[1]), lambda i, me: (me[0] * nb + i, 0)) for f in fulls]
        + [pl.BlockSpec((NDEV, br, l.shape[2]), lambda i, me: (0, i, 0)) for l in lands]
        + [blk, blk, blk],
        out_specs=[blk] * 4)
    return pl.pallas_call(
        body, name=name, out_shape=(sh, sh, sh, sh), grid_spec=grid_spec, compiler_params=_cparams(1),
    )(me_arr, *fulls, *lands, w, m, v)


ROW_F, ROW_I, ROW_C, ROW_G1, ROW_LG = 0, 8, 16, 24, 32
PACK_ROWS = 40


def _small_sum(me_arr, pack, pack_land, sgw, sgw_land, wmod):
    ncol = wmod.shape[1]

    def body(me_ref, pack_ref, pland_ref, sgw_ref, sland_ref, wmod_ref, sum_ref, all_ref, sgw_sum_ref, part_ref):
        me = me_ref[0]
        acc = jnp.zeros((PACK_ROWS, D), F32)
        acc_w = jnp.zeros(sgw_ref.shape, F32)
        for p in range(NDEV):
            rows = jnp.where(p == me, pack_ref[...], pland_ref[p])
            all_ref[p] = rows
            acc = acc + rows
            acc_w = acc_w + jnp.where(p == me, sgw_ref[...], sland_ref[p])
        sum_ref[...] = acc
        sgw_sum_ref[...] = acc_w
        zero = jnp.zeros((1, D), F32)
        dcmod = jnp.concatenate([acc[ROW_C:ROW_C + 1], acc[ROW_C + 1:ROW_C + 2], zero, zero, zero, zero], axis=1)
        mine = jnp.zeros((1, ncol), F32)
        for d in range(NDEV):
            mine = mine + jnp.where(d == me, dcmod[:, d * ncol:(d + 1) * ncol], 0.0)
        part_ref[...] = _dot_nt(jnp.broadcast_to(mine, (8, ncol)).astype(BF16), wmod_ref[...].astype(BF16))

    vm = pl.BlockSpec(memory_space=pltpu.VMEM)
    return pl.pallas_call(
        body, name="small_sum",
        out_shape=(jax.ShapeDtypeStruct((PACK_ROWS, D), F32), jax.ShapeDtypeStruct((NDEV, PACK_ROWS, D), F32),
                   jax.ShapeDtypeStruct(sgw.shape, F32), jax.ShapeDtypeStruct((8, D), F32)),
        in_specs=[pl.BlockSpec(memory_space=pltpu.SMEM)] + [vm] * 5,
        compiler_params=_cparams(),
    )(me_arr, pack, pack_land, sgw, sgw_land, wmod)


def _cctx_final(me_arr, part, part_land, cctx_row, m_row, v_row):
    def body(me_ref, part_ref, land_ref, c_ref, m_ref, v_ref, g_ref, d_ref, mo_ref, vo_ref):
        me = me_ref[0]
        tot = jnp.zeros((8, D), F32)
        for p in range(NDEV):
            tot = tot + jnp.where(p == me, part_ref[...], land_ref[p])
        cc = c_ref[...]
        _, dsilu = _silu_parts(cc)
        g = tot[0:1, :] * dsilu
        g_ref[...] = g
        d_ref[...], mo_ref[...], vo_ref[...] = _adam_math(cc, g, m_ref[...], v_ref[...])

    row = jax.ShapeDtypeStruct((1, D), F32)
    vm = pl.BlockSpec(memory_space=pltpu.VMEM)
    return pl.pallas_call(
        body, name="cctx_final", out_shape=(row, row, row, row),
        in_specs=[pl.BlockSpec(memory_space=pltpu.SMEM)] + [vm] * 5,
        compiler_params=_cparams(),
    )(me_arr, part, part_land, cctx_row, m_row, v_row)


def _adam_small(s, sgw_g, params):
    names = ["norm1", "norm2", "norm_f", "sg_gain", "sg_b", "ret_logit_f", "ret_logit_b", "b_mod", "sg_w"]
    flat = [a for n in names for a in params[n]]

    def body(*refs):
        s_ref, sgw_ref = refs[0], refs[1]
        p_refs = refs[2:2 + 3 * len(names)]
        o_refs = refs[2 + 3 * len(names):]
        loss_ref = o_refs[-1]

        def row(r):
            return s_ref[r:r + 1, :]

        grads = {
            "norm1": row(ROW_I + 2) + row(ROW_C + 2),
            "norm2": row(ROW_F + 4),
            "norm_f": row(ROW_F + 0),
            "sg_gain": s_ref[ROW_I + 3:ROW_I + 4, 0:AW],
            "sg_b": s_ref[ROW_I + 4:ROW_I + 8, 0:DH],
            "sg_w": sgw_ref[...],
        }
        for j, n in enumerate(names):
            w_ref, m_ref, v_ref = p_refs[3 * j:3 * j + 3]
            g_ref, d_ref, mo_ref, vo_ref = o_refs[4 * j:4 * j + 4]
            w = w_ref[...]
            if n in ("ret_logit_f", "ret_logit_b"):
                r = ROW_LG + (0 if n == "ret_logit_f" else 1)
                g = s_ref[r:r + 1, 0:H] * _sigmoid(-w)
            elif n == "b_mod":
                rows = [row(ROW_I + 0) + row(ROW_C + 0), row(ROW_I + 1) + row(ROW_C + 1), row(ROW_G1),
                        row(ROW_F + 2), row(ROW_F + 3), row(ROW_F + 1)]
                g = jnp.concatenate(rows, axis=1)
            else:
                g = grads[n]
            g_ref[...] = g
            d_ref[...], mo_ref[...], vo_ref[...] = _adam_math(w, g, m_ref[...], v_ref[...])
        loss_ref[...] = jnp.full((1, 1), 0.5 / D, F32) * jnp.sum(row(ROW_F + 5), axis=1, keepdims=True)

    out_shape = []
    for n in names:
        w = params[n][0]
        out_shape += [jax.ShapeDtypeStruct(w.shape, F32)] * 4
    out_shape.append(jax.ShapeDtypeStruct((1, 1), F32))
    outs = pl.pallas_call(body, name="adam_small", out_shape=tuple(out_shape), compiler_params=_cparams())(
        s, sgw_g, *flat)
    return {n: tuple(outs[4 * j:4 * j + 4]) for j, n in enumerate(names)}, outs[-1]


def _wmod_grad(cs_all, dmod_cols, wmod, m, v):
    ncol = wmod.shape[1]

    def body(cs_ref, dm_ref, w_ref, m_ref, v_ref, g_ref, d_ref, mo_ref, vo_ref):
        g = _dot_tn(cs_ref[...].astype(BF16), dm_ref[...].astype(BF16))
        g_ref[...] = g
        d_ref[...], mo_ref[...], vo_ref[...] = _adam_math(w_ref[...], g, m_ref[...], v_ref[...])

    sh = jax.ShapeDtypeStruct((D, ncol), F32)
    br = D // 4
    blk = pl.BlockSpec((br, ncol), lambda i: (i, 0))
    return pl.pallas_call(
        body, name="wmod_grad", out_shape=(sh, sh, sh, sh), grid=(D // br,),
        in_specs=[pl.BlockSpec((cs_all.shape[0], br), lambda i: (0, i)), _whole(dmod_cols.shape), blk, blk, blk],
        out_specs=[blk] * 4,
        compiler_params=_cparams(1),
    )(cs_all, dmod_cols, wmod, m, v)


def _rope_tables(t_len):
    n_freq = DH // 4
    inv = (ROPE_BASE ** (-np.arange(n_freq, dtype=np.float32) / n_freq)).astype(np.float32)
    tok = np.arange(t_len)
    rows = (tok // GRID_W).astype(np.float32)
    cols = (tok % GRID_W).astype(np.float32)
    ang_r = rows[:, None] * inv[None, :]
    ang_c = cols[:, None] * inv[None, :]
    cos = np.concatenate([np.cos(ang_r)] * 2 + [np.cos(ang_c)] * 2, axis=1).astype(np.float32)
    sin = np.concatenate([-np.sin(ang_r), np.sin(ang_r), -np.sin(ang_c), np.sin(ang_c)], axis=1).astype(np.float32)
    return jnp.asarray(cos), jnp.asarray(sin)


def _local_step(x, tgt, ctx, mod6, cmod6, norm1, norm2, normf, win, wout, ffn_weights, sgw, sgb, sggain, rlf, rlb,
                *, tm_a, tm_b, tm_f, tm_m, tm_r, tm_i, bt_w, after_first_grads=None, small_path=None,
                after_in_half=None):
    t_len = x.shape[0]
    cos, sin = _rope_tables(t_len)
    sgbt = jnp.broadcast_to(sgb[:, :, None], (G, C, 128))
    rlf = jnp.broadcast_to(rlf.reshape(H, 1), (H, 128))
    rlb = jnp.broadcast_to(rlb.reshape(H, 1), (H, 128))
    hc, kvc, scf, scb = _ctx_fwd(ctx, cmod6, norm1, win, rlf, rlb)
    hx, zuv, q, k, v, gfb, of, ya, sst = _fwd_a(x, mod6, norm1, win, sgw, sgbt, sggain, cos, sin, rlf, scf, tm=tm_a)
    if callable(wout):
        wout, tok = wout(hx)
        rlb = rlb + tok
    ob, ycat, x1, rst = _fwd_b(q, k, v, of, gfb, ya, x, mod6, wout, rlb, scb, tm=tm_b)
    wg, wu, wd = ffn_weights(x1) if callable(ffn_weights) else ffn_weights
    dx1, h2, da, db, hm, df, small_f = _ffn(x1, tgt, mod6, norm2, normf, wg, wu, wd, tm=tm_f)
    bt2 = min(2 * bt_w, t_len)
    d_wd, d_wd_b = _tn_matmul(hm, df, bm=DFF // 2, bt=bt2, name="dw_down")
    d_wg, d_wg_b = _tn_matmul(da, h2, bm=DFF // 2, bt=bt2, name="dw_gate")
    d_wu, d_wu_b = _tn_matmul(db, h2, bm=DFF // 2, bt=bt2, name="dw_up")
    dxb, dya, dgfb, dof, dob = _mid_bwd(dx1, of, ob, gfb, mod6, wout, tm=tm_m)
    d_wo, d_wo_b, dg1 = _dw_out(ycat, dxb, mod6, wout, bt=bt2)
    if after_first_grads is not None:
        rlf = rlf + after_first_grads((d_wd_b, d_wg_b, d_wu_b, d_wo_b))
    dqf, dkf, dvf, dqb, dkb, dvb, dscf, dscb, dlg = _ret_bwd(q, k, v, dof, dob, sst, rst, rlf, rlb, tm=tm_r)
    gx, dz, small_i, dsgw = _in_bwd(x, zuv, dya, dqf, dkf, dvf, dqb, dkb, dvb, dgfb, dx1, cos, sin,
                                    mod6, norm1, win, sgw, sgbt, sggain, tm=tm_i)
    dwin_c, small_c, dlg = _ctx_bwd(ctx, hc, kvc, cmod6, norm1, win, rlf, rlb, dscf, dscb, dlg)
    pack = jnp.concatenate([small_f, small_i, small_c, dg1, dlg], axis=0)
    after = small_path(pack, dsgw) if small_path is not None else ()
    halves = []
    for j in range(2):
        halves.append(_tn_matmul(dz, hx, bm=INC // 2, bt=bt2, name="dw_in_%d" % j, init=dwin_c,
                                 init_rows=(KV_LO, KV_HI), after=after, cols=(j, D // 2)))
        if after_in_half is not None:
            after = after_in_half(j, halves[-1][1])
    grads = {"w_in": halves, "w_out": (d_wo, d_wo_b), "w_gate": (d_wg, d_wg_b), "w_up": (d_wu, d_wu_b),
             "w_down": (d_wd, d_wd_b)}
    return gx, grads, pack, dsgw


def kernel(x, c, ctx, c_ctx, w_mod, b_mod, norm1, w_in, sg_gain, sg_w, sg_b, ret_logit_f, ret_logit_b, w_out, norm2, w_gate, w_up, w_down, norm_f, loss_target, m_c_ctx, m_w_mod, m_b_mod, m_norm1, m_w_in, m_sg_gain, m_sg_w, m_sg_b, m_ret_logit_f, m_ret_logit_b, m_w_out, m_norm2, m_w_gate, m_w_up, m_w_down, m_norm_f, v_c_ctx, v_w_mod, v_b_mod, v_norm1, v_w_in, v_sg_gain, v_sg_w, v_sg_b, v_ret_logit_f, v_ret_logit_b, v_w_out, v_norm2, v_w_gate, v_w_up, v_w_down, v_norm_f):
    t_len = x.shape[1]
    tm = min(512, t_len)
    me = 4 * lax.axis_index("x") + 2 * lax.axis_index("y") + lax.axis_index("c")
    tr = lambda a: jnp.transpose(a[0])

    cctx_row = c_ctx.reshape(1, D)
    mod_all, cs_all, g_in = _entry_gather(c, cctx_row, w_mod[0], b_mod, [tr(w_in)])
    mod6 = lax.dynamic_slice(mod_all, (me, 0), (1, 6 * D)).reshape(6, D)
    cmod6 = mod_all[8].reshape(6, D)
    win_t = g_in.reshape(INC, D)

    (out_shard,) = _cast_bf16([w_out[0]], name="cast_out", after=[g_in])
    h_out, tok_out = _exchange_start([out_shard], scatter=False, name="wout_start")
    ffn_shards = _cast_bf16([tr(w_gate), tr(w_up), w_down[0]], name="cast_ffn", after=[h_out[2]])
    sems_ffn, arrs_ffn, tok = _gather2_start(ffn_shards, name="wffn_start")
    mod6 = mod6 + (tok_out[0, 0] + tok[0, 0])
    ffn = {}

    def place_own(own, lands, rows):
        return [lax.dynamic_update_slice(l, o[None], (me, 0, 0)).reshape(rows, D) for o, l in zip(own, lands)]

    def wout(after):
        own, lands = _exchange_wait(h_out, [after], scatter=False, name="wout_wait")
        arrs = _gather2_arrived(sems_ffn, arrs_ffn, [after], name="wffn_arrived")
        ffn["own"] = arrs[:3]
        ffn["sems"], ffn["lands"], tok_fwd = _gather2_forward(arrs[3:], name="wffn_forward")
        return place_own(own, lands, D)[0], tok_fwd[0, 0]

    def ffn_weights(after):
        lands = _gather2_wait(ffn["sems"], ffn["lands"], [after], name="wffn_wait")
        return place_own(ffn["own"], lands, DFF)

    rs, outs = {}, {}

    def after_first_grads(bf):
        rs["first"], tok_first = _exchange_start([b.reshape(NDEV, b.shape[0] // NDEV, D) for b in bf], scatter=True,
                                                 name="rs_first_start")
        return tok_first[0, 0]

    def small_path(pack, dsgw):
        rs["small"], _ = _exchange_start([pack, dsgw.reshape(G * C, C)], scatter=False, name="small_start")
        return [rs["small"][2], rs["small"][3]]

    def after_in_half(j, half_bf16):
        rs["in%d" % j], _ = _exchange_start([half_bf16.reshape(NDEV, INC // NDEV, D // 2)], scatter=True,
                                            name="rs_in%d_start" % j)
        return [rs["in%d" % j][2]]

    gx, grads, pack, dsgw = _local_step(
        x[0], loss_target[0], ctx[0], mod6, cmod6, norm1, norm2[0:1], norm_f.reshape(1, D), win_t, wout, ffn_weights,
        sg_w[0], sg_b[0], sg_gain, ret_logit_f, ret_logit_b,
        tm_a=tm, tm_b=min(1024, t_len), tm_f=min(256, t_len), tm_m=min(1024, t_len), tm_r=min(1024, t_len), tm_i=tm,
        bt_w=min(1024, t_len),
        after_first_grads=after_first_grads, small_path=small_path, after_in_half=after_in_half)

    me_arr = me.reshape(1).astype(jnp.int32)
    (pack_own, sgw_own), (pack_land, sgw_land) = _exchange_wait(rs["small"], [rs["in1"][2]], scatter=False,
                                                               name="small_wait")
    psum_rows, pall, sgw_sum, part = _small_sum(me_arr, pack_own, pack_land, sgw_own, sgw_land, w_mod[0])
    h_cc, _ = _exchange_start([part], scatter=False, name="cctx_start")

    dmod_rows = (ROW_I + 0, ROW_I + 1, ROW_G1, ROW_F + 2, ROW_F + 3, ROW_F + 1)
    dmod_all = jnp.concatenate([pall[:, r, :] for r in dmod_rows], axis=1)
    dcmod = jnp.concatenate([psum_rows[ROW_C + 0:ROW_C + 1], psum_rows[ROW_C + 1:ROW_C + 2],
                             jnp.zeros((1, 4 * D), F32)], axis=1)
    dmod_mat = jnp.concatenate([dmod_all, jnp.pad(dcmod, ((0, 7), (0, 0)))], axis=0)
    ncol = 6 * D // NDEV
    dmod_cols = lax.dynamic_slice(dmod_mat, (0, me * ncol), (16, ncol))
    g_wm, d_wm, m_wm, v_wm = _wmod_grad(cs_all, dmod_cols, w_mod[0], m_w_mod[0], v_w_mod[0])
    outs["w_mod"] = (g_wm[None], d_wm[None], m_wm[None], v_wm[None])
    small_params = {
        "norm1": (norm1, m_norm1, v_norm1), "norm2": (norm2, m_norm2, v_norm2),
        "norm_f": tuple(a.reshape(1, D) for a in (norm_f, m_norm_f, v_norm_f)),
        "sg_gain": (sg_gain, m_sg_gain, v_sg_gain), "sg_b": (sg_b[0], m_sg_b[0], v_sg_b[0]),
        "ret_logit_f": (ret_logit_f, m_ret_logit_f, v_ret_logit_f),
        "ret_logit_b": (ret_logit_b, m_ret_logit_b, v_ret_logit_b),
        "b_mod": (b_mod, m_b_mod, v_b_mod), "sg_w": (sg_w[0], m_sg_w[0], v_sg_w[0])}
    small, loss = _adam_small(psum_rows, sgw_sum.reshape(G, C, C), small_params)
    back = {"norm_f": lambda a: a.reshape(D), "sg_b": lambda a: a[None], "sg_w": lambda a: a[None]}
    for name, vals in small.items():
        outs[name] = tuple(back.get(name, lambda a: a)(a) for a in vals)

    _, lands_first = _exchange_wait(rs["first"], [g_wm], scatter=True, name="rs_first_wait")

    def finish(name, fulls, lands, w, m, v, transposed):
        take = tr if transposed else (lambda a: a[0])
        res = _rs_adam(me_arr, fulls, lands, take(w), take(m), take(v), name="adam_" + name)
        put = (lambda a: jnp.transpose(a)[None]) if transposed else (lambda a: a[None])
        outs[name] = tuple(put(a) for a in res)
        return res[0]

    g_down = finish("w_down", [grads["w_down"][0]], [lands_first[0]], w_down, m_w_down, v_w_down, False)
    g_gate = finish("w_gate", [grads["w_gate"][0]], [lands_first[1]], w_gate, m_w_gate, v_w_gate, True)
    g_up = finish("w_up", [grads["w_up"][0]], [lands_first[2]], w_up, m_w_up, v_w_up, True)
    g_out = finish("w_out", [grads["w_out"][0]], [lands_first[3]], w_out, m_w_out, v_w_out, False)
    done = [g_down, g_gate, g_up, g_out]
    (part_own,), (part_land,) = _exchange_wait(h_cc, done, scatter=False, name="cctx_wait")
    res_cc = _cctx_final(me_arr, part_own, part_land, cctx_row, m_c_ctx.reshape(1, D), v_c_ctx.reshape(1, D))
    outs["c_ctx"] = tuple(a.reshape(D) for a in res_cc)
    lands_in = [_exchange_wait(rs["in%d" % j], done, scatter=True, name="rs_in%d_wait" % j)[1][0] for j in range(2)]
    finish("w_in", [h[0] for h in grads["w_in"]], lands_in, w_in, m_w_in, v_w_in, True)

    order = ["c_ctx", "w_mod", "b_mod", "norm1", "w_in", "sg_gain", "sg_w", "sg_b", "ret_logit_f", "ret_logit_b",
             "w_out", "norm2", "w_gate", "w_up", "w_down", "norm_f"]
    res = [loss.reshape(()), gx[None]]
    for j in range(4):
        res += [outs[name][j] for name in order]
    return tuple(res)
```
